```python
import math
import jax, jax.numpy as jnp
from jax import lax
import numpy as np

D_MODEL = 1024
BATCH = 8
SEQ = 4096
DEPTH = 1

D_MIX = D_MODEL
D_ATTN = D_MIX // 2
D_SSM = D_MIX - D_ATTN
HEAD_DIM = 64
N_Q_HEADS = D_ATTN // HEAD_DIM
N_KV_HEADS = 2
GQA_GROUP = N_Q_HEADS // N_KV_HEADS
WINDOW = 128
BLOCK = 128
HALO_BLOCKS = -(-WINDOW // BLOCK)
ROPE_THETA = 10000.0
SSM_CH = 16
N_SSM_GROUPS = D_SSM // SSM_CH
SSM_STATE = 64
N_DIR = 2
DT_MIN = 1e-3
DT_MAX = 1e-1
NORM_EPS = 1e-5
NEG_INF = -1e30
DEEPNORM_ALPHA = (2.0 * DEPTH) ** 0.25
DEEPNORM_BETA = (8.0 * DEPTH) ** -0.25
Q_COLS = N_Q_HEADS * HEAD_DIM
KV_COLS = N_KV_HEADS * HEAD_DIM
D_IN_PROJ = Q_COLS + 2 * KV_COLS + D_ATTN + D_SSM + D_SSM
SPLITS = [Q_COLS, Q_COLS + KV_COLS, Q_COLS + 2 * KV_COLS,
          Q_COLS + 2 * KV_COLS + D_ATTN, Q_COLS + 2 * KV_COLS + D_ATTN + D_SSM]

kernel_name = "hymba_deepnorm_swa_s5_encoder"


def layer_norm(x, g, b):
    xf = x.astype(jnp.float32)
    mu = jnp.mean(xf, axis=-1, keepdims=True)
    var = jnp.mean(jnp.square(xf - mu), axis=-1, keepdims=True)
    y = (xf - mu) * lax.rsqrt(var + NORM_EPS)
    return (y * g.astype(jnp.float32) + b.astype(jnp.float32)).astype(x.dtype)


def rms_norm(x, g):
    xf = x.astype(jnp.float32)
    y = xf * lax.rsqrt(jnp.mean(jnp.square(xf), axis=-1, keepdims=True) + NORM_EPS)
    return (y * g.astype(jnp.float32)).astype(x.dtype)


def rope(x, pos):
    half = HEAD_DIM // 2
    inv_freq = ROPE_THETA ** (-jnp.arange(half, dtype=jnp.float32) / half)
    ang = pos.astype(jnp.float32)[:, None] * inv_freq[None, :]
    cos = jnp.cos(ang)[None, :, None, :].astype(x.dtype)
    sin = jnp.sin(ang)[None, :, None, :].astype(x.dtype)
    x1, x2 = x[..., :half], x[..., half:]
    return jnp.concatenate([x1 * cos - x2 * sin, x2 * cos + x1 * sin], axis=-1)


def windowed_gqa_with_sink(q, k, v, sink):
    b, s = q.shape[0], q.shape[1]
    nb = s // BLOCK
    pad = HALO_BLOCKS * BLOCK
    span = 2 * HALO_BLOCKS + 1
    n_keys = span * BLOCK
    kp = jnp.pad(k, ((0, 0), (pad, pad), (0, 0), (0, 0))).reshape(b, nb + 2 * HALO_BLOCKS, BLOCK, N_KV_HEADS, HEAD_DIM)
    vp = jnp.pad(v, ((0, 0), (pad, pad), (0, 0), (0, 0))).reshape(b, nb + 2 * HALO_BLOCKS, BLOCK, N_KV_HEADS, HEAD_DIM)
    kb = jnp.concatenate([kp[:, j:j + nb] for j in range(span)], axis=2)
    vb = jnp.concatenate([vp[:, j:j + nb] for j in range(span)], axis=2)
    qb = q.reshape(b, nb, BLOCK, N_KV_HEADS, GQA_GROUP, HEAD_DIM)
    scale = HEAD_DIM ** -0.5
    scores = jnp.einsum('bnqgrd,bnkgd->bngrqk', qb, kb).astype(jnp.float32) * scale
    blk = jnp.arange(nb, dtype=jnp.int32)[:, None] * BLOCK
    qpos = blk + jnp.arange(BLOCK, dtype=jnp.int32)[None, :]
    kpos = blk - pad + jnp.arange(n_keys, dtype=jnp.int32)[None, :]
    valid = ((jnp.abs(qpos[:, :, None] - kpos[:, None, :]) <= WINDOW)
             & (kpos[:, None, :] >= 0) & (kpos[:, None, :] < s))
    scores = jnp.where(valid[None, :, None, None], scores, NEG_INF)
    sink_l = sink.astype(jnp.float32).reshape(N_KV_HEADS, GQA_GROUP)[None, None, :, :, None, None]
    m = jnp.maximum(jnp.max(scores, axis=-1, keepdims=True), sink_l)
    p = jnp.exp(scores - m)
    denom = jnp.sum(p, axis=-1, keepdims=True) + jnp.exp(sink_l - m)
    probs = (p / denom).astype(v.dtype)
    out = jnp.einsum('bngrqk,bnkgd->bnqgrd', probs, vb)
    return out.reshape(b, s, N_Q_HEADS * HEAD_DIM)


def _complex_linear_combine(e1, e2):
    a1r, a1i, b1r, b1i = e1
    a2r, a2i, b2r, b2i = e2
    return (a2r * a1r - a2i * a1i,
            a2r * a1i + a2i * a1r,
            a2r * b1r - a2i * b1i + b2r,
            a2r * b1i + a2i * b1r + b2i)


def s5_bidirectional(u, a_re, a_im, log_dt, b_re, b_im, c_re, c_im, d, w_glu, b_glu):
    bsz, s = u.shape[0], u.shape[1]
    uf = u.astype(jnp.float32)
    ug = uf.reshape(bsz, s, N_SSM_GROUPS, SSM_CH)
    y = d.astype(jnp.float32) * uf
    for direction in range(N_DIR):
        dt = jnp.exp(log_dt[direction].astype(jnp.float32))[:, None]
        ar = a_re[direction].astype(jnp.float32)
        ai = a_im[direction].astype(jnp.float32)
        mag = jnp.exp(dt * ar)
        lr = mag * jnp.cos(dt * ai)
        li = mag * jnp.sin(dt * ai)
        den = ar * ar + ai * ai
        fr = ((lr - 1.0) * ar + li * ai) / den
        fi = (li * ar - (lr - 1.0) * ai) / den
        br = b_re[direction].astype(jnp.float32)
        bi = b_im[direction].astype(jnp.float32)
        bbar_r = fr[:, :, None] * br - fi[:, :, None] * bi
        bbar_i = fr[:, :, None] * bi + fi[:, :, None] * br
        bu_r = jnp.einsum('bsgc,gpc->bsgp', ug, bbar_r)
        bu_i = jnp.einsum('bsgc,gpc->bsgp', ug, bbar_i)
        a_r = jnp.broadcast_to(lr[None, None], bu_r.shape)
        a_i = jnp.broadcast_to(li[None, None], bu_r.shape)
        _, _, xr, xi = lax.associative_scan(_complex_linear_combine, (a_r, a_i, bu_r, bu_i),
                                            reverse=(direction == 1), axis=1)
        cr = c_re[direction].astype(jnp.float32)
        ci = c_im[direction].astype(jnp.float32)
        yc = jnp.einsum('bsgp,gcp->bsgc', xr, cr) - jnp.einsum('bsgp,gcp->bsgc', xi, ci)
        y = y + yc.reshape(bsz, s, D_SSM)
    g = jax.nn.gelu(y)
    y = g * jax.nn.sigmoid(g @ w_glu.astype(jnp.float32) + b_glu.astype(jnp.float32))
    return y.astype(u.dtype)


def _fwd_setup_inputs(seed: int = 0) -> dict:
    key = jax.random.key(seed)
    ks = jax.random.split(key, 20)
    n = jnp.arange(SSM_STATE, dtype=jnp.float32)
    shp_a = (DEPTH, N_DIR, N_SSM_GROUPS, SSM_STATE)
    shp_b = (DEPTH, N_DIR, N_SSM_GROUPS, SSM_STATE, SSM_CH)
    shp_c = (DEPTH, N_DIR, N_SSM_GROUPS, SSM_CH, SSM_STATE)
    return {
        "x": jax.random.normal(ks[0], (BATCH, SEQ, D_MODEL), jnp.float32),
        "w_in": jax.random.normal(ks[1], (DEPTH, D_MODEL, D_IN_PROJ), jnp.float32) * D_MODEL ** -0.5,
        "attn_sink": jax.random.normal(ks[2], (DEPTH, N_Q_HEADS), jnp.float32) * 0.5,
        "ssm_a_re": -0.5 + 0.01 * jax.random.normal(ks[3], shp_a, jnp.float32),
        "ssm_a_im": math.pi * n + 0.01 * jax.random.normal(ks[4], shp_a, jnp.float32),
        "ssm_log_dt": jax.random.uniform(ks[5], (DEPTH, N_DIR, N_SSM_GROUPS), jnp.float32,
                                         math.log(DT_MIN), math.log(DT_MAX)),
        "ssm_b_re": jax.random.normal(ks[6], shp_b, jnp.float32) * (2.0 * SSM_CH) ** -0.5,
        "ssm_b_im": jax.random.normal(ks[7], shp_b, jnp.float32) * (2.0 * SSM_CH) ** -0.5,
        "ssm_c_re": jax.random.normal(ks[8], shp_c, jnp.float32) * (2.0 * SSM_STATE) ** -0.5,
        "ssm_c_im": jax.random.normal(ks[9], shp_c, jnp.float32) * (2.0 * SSM_STATE) ** -0.5,
        "ssm_d": jax.random.normal(ks[10], (DEPTH, D_SSM), jnp.float32),
        "w_glu": jax.random.normal(ks[11], (DEPTH, D_SSM, D_SSM), jnp.float32) * D_SSM ** -0.5,
        "b_glu": 0.01 * jax.random.normal(ks[12], (DEPTH, D_SSM), jnp.float32),
        "norm_attn_g": 1.0 + 0.01 * jax.random.normal(ks[13], (DEPTH, D_ATTN), jnp.float32),
        "norm_ssm_g": 1.0 + 0.01 * jax.random.normal(ks[14], (DEPTH, D_SSM), jnp.float32),
        "w_out": jax.random.normal(ks[15], (DEPTH, D_MIX, D_MODEL), jnp.float32) * (D_MIX ** -0.5) * DEEPNORM_BETA,
        "ln_g": 1.0 + 0.01 * jax.random.normal(ks[16], (DEPTH, D_MODEL), jnp.float32),
        "ln_b": 0.01 * jax.random.normal(ks[17], (DEPTH, D_MODEL), jnp.float32),
    }


def _fwd_reference(x, w_in, attn_sink, ssm_a_re, ssm_a_im, ssm_log_dt, ssm_b_re, ssm_b_im, ssm_c_re, ssm_c_im,
              ssm_d, w_glu, b_glu, norm_attn_g, norm_ssm_g, w_out, ln_g, ln_b):
    bsz, s = x.shape[0], x.shape[1]
    pos = jnp.arange(s, dtype=jnp.int32)
    h = x
    for l in range(DEPTH):
        proj = h @ w_in[l]
        q, k, v, z_attn, u, z_ssm = jnp.split(proj, SPLITS, axis=-1)
        q = rope(q.reshape(bsz, s, N_Q_HEADS, HEAD_DIM), pos)
        k = rope(k.reshape(bsz, s, N_KV_HEADS, HEAD_DIM), pos)
        v = v.reshape(bsz, s, N_KV_HEADS, HEAD_DIM)
        y_attn = windowed_gqa_with_sink(q, k, v, attn_sink[l]) * jax.nn.silu(z_attn)
        y_attn = rms_norm(y_attn, norm_attn_g[l])
        y_ssm = s5_bidirectional(u, ssm_a_re[l], ssm_a_im[l], ssm_log_dt[l], ssm_b_re[l], ssm_b_im[l],
                                 ssm_c_re[l], ssm_c_im[l], ssm_d[l], w_glu[l], b_glu[l]) * jax.nn.silu(z_ssm)
        y_ssm = rms_norm(y_ssm, norm_ssm_g[l])
        mixed = jnp.concatenate([y_attn, y_ssm], axis=-1)
        h = layer_norm(DEEPNORM_ALPHA * h + mixed @ w_out[l], ln_g[l], ln_b[l])
    return h


import jax as _jax
import jax.numpy as _jnp

TWIN_FORMAT = 'train_step'
FWD_PARAMS = ['x', 'w_in', 'attn_sink', 'ssm_a_re', 'ssm_a_im', 'ssm_log_dt', 'ssm_b_re', 'ssm_b_im', 'ssm_c_re', 'ssm_c_im', 'ssm_d', 'w_glu', 'b_glu', 'norm_attn_g', 'norm_ssm_g', 'w_out', 'ln_g', 'ln_b']
TWIN_WEIGHTS = ['w_in', 'attn_sink', 'ssm_a_re', 'ssm_a_im', 'ssm_log_dt', 'ssm_b_re', 'ssm_b_im', 'ssm_c_re', 'ssm_c_im', 'ssm_d', 'w_glu', 'b_glu', 'norm_attn_g', 'norm_ssm_g', 'w_out', 'ln_g', 'ln_b']
TWIN_DIFF_INPUT = 'x'
TWIN_INPUTS = ['x', 'w_in', 'attn_sink', 'ssm_a_re', 'ssm_a_im', 'ssm_log_dt', 'ssm_b_re', 'ssm_b_im', 'ssm_c_re', 'ssm_c_im', 'ssm_d', 'w_glu', 'b_glu', 'norm_attn_g', 'norm_ssm_g', 'w_out', 'ln_g', 'ln_b', 'loss_target', 'm_w_in', 'm_attn_sink', 'm_ssm_a_re', 'm_ssm_a_im', 'm_ssm_log_dt', 'm_ssm_b_re', 'm_ssm_b_im', 'm_ssm_c_re', 'm_ssm_c_im', 'm_ssm_d', 'm_w_glu', 'm_b_glu', 'm_norm_attn_g', 'm_norm_ssm_g', 'm_w_out', 'm_ln_g', 'm_ln_b', 'v_w_in', 'v_attn_sink', 'v_ssm_a_re', 'v_ssm_a_im', 'v_ssm_log_dt', 'v_ssm_b_re', 'v_ssm_b_im', 'v_ssm_c_re', 'v_ssm_c_im', 'v_ssm_d', 'v_w_glu', 'v_b_glu', 'v_norm_attn_g', 'v_norm_ssm_g', 'v_w_out', 'v_ln_g', 'v_ln_b']
TWIN_OUTPUTS = ['loss', 'grad_x', 'grad_w_in', 'grad_attn_sink', 'grad_ssm_a_re', 'grad_ssm_a_im', 'grad_ssm_log_dt', 'grad_ssm_b_re', 'grad_ssm_b_im', 'grad_ssm_c_re', 'grad_ssm_c_im', 'grad_ssm_d', 'grad_w_glu', 'grad_b_glu', 'grad_norm_attn_g', 'grad_norm_ssm_g', 'grad_w_out', 'grad_ln_g', 'grad_ln_b', 'delta_w_in', 'delta_attn_sink', 'delta_ssm_a_re', 'delta_ssm_a_im', 'delta_ssm_log_dt', 'delta_ssm_b_re', 'delta_ssm_b_im', 'delta_ssm_c_re', 'delta_ssm_c_im', 'delta_ssm_d', 'delta_w_glu', 'delta_b_glu', 'delta_norm_attn_g', 'delta_norm_ssm_g', 'delta_w_out', 'delta_ln_g', 'delta_ln_b', 'new_m_w_in', 'new_m_attn_sink', 'new_m_ssm_a_re', 'new_m_ssm_a_im', 'new_m_ssm_log_dt', 'new_m_ssm_b_re', 'new_m_ssm_b_im', 'new_m_ssm_c_re', 'new_m_ssm_c_im', 'new_m_ssm_d', 'new_m_w_glu', 'new_m_b_glu', 'new_m_norm_attn_g', 'new_m_norm_ssm_g', 'new_m_w_out', 'new_m_ln_g', 'new_m_ln_b', 'new_v_w_in', 'new_v_attn_sink', 'new_v_ssm_a_re', 'new_v_ssm_a_im', 'new_v_ssm_log_dt', 'new_v_ssm_b_re', 'new_v_ssm_b_im', 'new_v_ssm_c_re', 'new_v_ssm_c_im', 'new_v_ssm_d', 'new_v_w_glu', 'new_v_b_glu', 'new_v_norm_attn_g', 'new_v_norm_ssm_g', 'new_v_w_out', 'new_v_ln_g', 'new_v_ln_b']
TWIN_LEAF_KINDS = {'loss': 'loss', 'grad_x': 'grad_x', 'grad_w_in': 'grad_w', 'grad_attn_sink': 'grad_w', 'grad_ssm_a_re': 'grad_w', 'grad_ssm_a_im': 'grad_w', 'grad_ssm_log_dt': 'grad_w', 'grad_ssm_b_re': 'grad_w', 'grad_ssm_b_im': 'grad_w', 'grad_ssm_c_re': 'grad_w', 'grad_ssm_c_im': 'grad_w', 'grad_ssm_d': 'grad_w', 'grad_w_glu': 'grad_w', 'grad_b_glu': 'grad_w', 'grad_norm_attn_g': 'grad_w', 'grad_norm_ssm_g': 'grad_w', 'grad_w_out': 'grad_w', 'grad_ln_g': 'grad_w', 'grad_ln_b': 'grad_w', 'delta_w_in': 'delta_w', 'delta_attn_sink': 'delta_w', 'delta_ssm_a_re': 'delta_w', 'delta_ssm_a_im': 'delta_w', 'delta_ssm_log_dt': 'delta_w', 'delta_ssm_b_re': 'delta_w', 'delta_ssm_b_im': 'delta_w', 'delta_ssm_c_re': 'delta_w', 'delta_ssm_c_im': 'delta_w', 'delta_ssm_d': 'delta_w', 'delta_w_glu': 'delta_w', 'delta_b_glu': 'delta_w', 'delta_norm_attn_g': 'delta_w', 'delta_norm_ssm_g': 'delta_w', 'delta_w_out': 'delta_w', 'delta_ln_g': 'delta_w', 'delta_ln_b': 'delta_w', 'new_m_w_in': 'new_m', 'new_m_attn_sink': 'new_m', 'new_m_ssm_a_re': 'new_m', 'new_m_ssm_a_im': 'new_m', 'new_m_ssm_log_dt': 'new_m', 'new_m_ssm_b_re': 'new_m', 'new_m_ssm_b_im': 'new_m', 'new_m_ssm_c_re': 'new_m', 'new_m_ssm_c_im': 'new_m', 'new_m_ssm_d': 'new_m', 'new_m_w_glu': 'new_m', 'new_m_b_glu': 'new_m', 'new_m_norm_attn_g': 'new_m', 'new_m_norm_ssm_g': 'new_m', 'new_m_w_out': 'new_m', 'new_m_ln_g': 'new_m', 'new_m_ln_b': 'new_m', 'new_v_w_in': 'new_v', 'new_v_attn_sink': 'new_v', 'new_v_ssm_a_re': 'new_v', 'new_v_ssm_a_im': 'new_v', 'new_v_ssm_log_dt': 'new_v', 'new_v_ssm_b_re': 'new_v', 'new_v_ssm_b_im': 'new_v', 'new_v_ssm_c_re': 'new_v', 'new_v_ssm_c_im': 'new_v', 'new_v_ssm_d': 'new_v', 'new_v_w_glu': 'new_v', 'new_v_b_glu': 'new_v', 'new_v_norm_attn_g': 'new_v', 'new_v_norm_ssm_g': 'new_v', 'new_v_w_out': 'new_v', 'new_v_ln_g': 'new_v', 'new_v_ln_b': 'new_v'}


def _forward(args):
    return _fwd_reference(*[args[k] for k in FWD_PARAMS])


def _output_shape():
    def fwd():
        inp = _fwd_setup_inputs(0)
        return _fwd_reference(*[inp[k] for k in FWD_PARAMS])
    out = _jax.eval_shape(fwd)
    return out.shape, out.dtype

N_MICROBATCH = 1
ADAM_LR = 0.001
ADAM_B1 = 0.9
ADAM_B2 = 0.999
ADAM_EPS = 1e-08
ADAM_WD = 0.01
ADAM_STEP = 10
PER_EXAMPLE_BATCH_AXIS = {'x': 0, 'loss_target': 0}
SHARED_INPUTS = []
_WEIGHT_DTYPES = {'w_in': _jnp.float32, 'attn_sink': _jnp.float32, 'ssm_a_re': _jnp.float32, 'ssm_a_im': _jnp.float32, 'ssm_log_dt': _jnp.float32, 'ssm_b_re': _jnp.float32, 'ssm_b_im': _jnp.float32, 'ssm_c_re': _jnp.float32, 'ssm_c_im': _jnp.float32, 'ssm_d': _jnp.float32, 'w_glu': _jnp.float32, 'b_glu': _jnp.float32, 'norm_attn_g': _jnp.float32, 'norm_ssm_g': _jnp.float32, 'w_out': _jnp.float32, 'ln_g': _jnp.float32, 'ln_b': _jnp.float32}
MOMENT_SCALE = {'w_in': 9.502081e-02, 'attn_sink': 1.237632e-03, 'ssm_a_re': 4.295475e-03, 'ssm_a_im': 4.638212e-03, 'ssm_log_dt': 1.835109e+00, 'ssm_b_re': 2.935925e-03, 'ssm_b_im': 2.968976e-03, 'ssm_c_re': 5.910702e-03, 'ssm_c_im': 5.917217e-03, 'ssm_d': 8.815019e-02, 'w_glu': 2.295729e-02, 'b_glu': 3.456200e-02, 'norm_attn_g': 7.970027e-02, 'norm_ssm_g': 8.086678e-02, 'w_out': 1.348470e-01, 'ln_g': 3.200917e+01, 'ln_b': 1.562104e+00}


def _to_microbatches(a, axis):
    t = _jnp.moveaxis(a, axis, 0)
    t = t.reshape((N_MICROBATCH, t.shape[0] // N_MICROBATCH) + t.shape[1:])
    return _jnp.moveaxis(t, 1, axis + 1)


def setup_inputs(seed: int = 0) -> dict:
    inp = _fwd_setup_inputs(seed)
    key = _jax.random.fold_in(_jax.random.key(seed), 7919)
    shape, _ = _output_shape()
    out = dict(inp)
    out["loss_target"] = _jax.random.normal(_jax.random.fold_in(key, 0), shape, _jnp.float32)
    for i, name in enumerate(TWIN_WEIGHTS):
        w = inp[name].astype(_jnp.float32)
        if MOMENT_SCALE is None:
            s = _jnp.sqrt(_jnp.mean(_jnp.square(w)) + 1e-30)
        else:
            s = MOMENT_SCALE[name]
        km, kv = _jax.random.split(_jax.random.fold_in(key, i + 1))
        out[name] = w
        out["m_" + name] = s * _jax.random.normal(km, w.shape, _jnp.float32)
        out["v_" + name] = (s * s) * _jax.random.uniform(kv, w.shape, _jnp.float32, 0.5, 1.5)
    if N_MICROBATCH > 1:
        for name, axis in PER_EXAMPLE_BATCH_AXIS.items():
            out[name] = _to_microbatches(out[name], axis)
    return {'x': out['x'], 'w_in': out['w_in'], 'attn_sink': out['attn_sink'], 'ssm_a_re': out['ssm_a_re'], 'ssm_a_im': out['ssm_a_im'], 'ssm_log_dt': out['ssm_log_dt'], 'ssm_b_re': out['ssm_b_re'], 'ssm_b_im': out['ssm_b_im'], 'ssm_c_re': out['ssm_c_re'], 'ssm_c_im': out['ssm_c_im'], 'ssm_d': out['ssm_d'], 'w_glu': out['w_glu'], 'b_glu': out['b_glu'], 'norm_attn_g': out['norm_attn_g'], 'norm_ssm_g': out['norm_ssm_g'], 'w_out': out['w_out'], 'ln_g': out['ln_g'], 'ln_b': out['ln_b'], 'loss_target': out['loss_target'], 'm_w_in': out['m_w_in'], 'm_attn_sink': out['m_attn_sink'], 'm_ssm_a_re': out['m_ssm_a_re'], 'm_ssm_a_im': out['m_ssm_a_im'], 'm_ssm_log_dt': out['m_ssm_log_dt'], 'm_ssm_b_re': out['m_ssm_b_re'], 'm_ssm_b_im': out['m_ssm_b_im'], 'm_ssm_c_re': out['m_ssm_c_re'], 'm_ssm_c_im': out['m_ssm_c_im'], 'm_ssm_d': out['m_ssm_d'], 'm_w_glu': out['m_w_glu'], 'm_b_glu': out['m_b_glu'], 'm_norm_attn_g': out['m_norm_attn_g'], 'm_norm_ssm_g': out['m_norm_ssm_g'], 'm_w_out': out['m_w_out'], 'm_ln_g': out['m_ln_g'], 'm_ln_b': out['m_ln_b'], 'v_w_in': out['v_w_in'], 'v_attn_sink': out['v_attn_sink'], 'v_ssm_a_re': out['v_ssm_a_re'], 'v_ssm_a_im': out['v_ssm_a_im'], 'v_ssm_log_dt': out['v_ssm_log_dt'], 'v_ssm_b_re': out['v_ssm_b_re'], 'v_ssm_b_im': out['v_ssm_b_im'], 'v_ssm_c_re': out['v_ssm_c_re'], 'v_ssm_c_im': out['v_ssm_c_im'], 'v_ssm_d': out['v_ssm_d'], 'v_w_glu': out['v_w_glu'], 'v_b_glu': out['v_b_glu'], 'v_norm_attn_g': out['v_norm_attn_g'], 'v_norm_ssm_g': out['v_norm_ssm_g'], 'v_w_out': out['v_w_out'], 'v_ln_g': out['v_ln_g'], 'v_ln_b': out['v_ln_b']}


def _loss(weights, diff, rest, loss_target):
    with _jax.named_scope("forward"):
        args = {**rest, TWIN_DIFF_INPUT: diff, **{k: w.astype(_WEIGHT_DTYPES[k]) for k, w in weights.items()}}
        y = _forward(args)
    with _jax.named_scope("loss_head"):
        err = _jnp.square(y.astype(_jnp.float32) - loss_target)
        return 0.5 * _jnp.sum(_jnp.mean(err, axis=-1)) if err.ndim else 0.5 * err


def _adamw(w, g, m, v):
    m = ADAM_B1 * m + (1.0 - ADAM_B1) * g
    v = ADAM_B2 * v + (1.0 - ADAM_B2) * _jnp.square(g)
    m_hat = m / (1.0 - ADAM_B1 ** ADAM_STEP)
    v_hat = v / (1.0 - ADAM_B2 ** ADAM_STEP)
    delta = -ADAM_LR * (m_hat / (_jnp.sqrt(v_hat) + ADAM_EPS) + ADAM_WD * w)
    return delta, m, v


def reference(x, w_in, attn_sink, ssm_a_re, ssm_a_im, ssm_log_dt, ssm_b_re, ssm_b_im, ssm_c_re, ssm_c_im, ssm_d, w_glu, b_glu, norm_attn_g, norm_ssm_g, w_out, ln_g, ln_b, loss_target, m_w_in, m_attn_sink, m_ssm_a_re, m_ssm_a_im, m_ssm_log_dt, m_ssm_b_re, m_ssm_b_im, m_ssm_c_re, m_ssm_c_im, m_ssm_d, m_w_glu, m_b_glu, m_norm_attn_g, m_norm_ssm_g, m_w_out, m_ln_g, m_ln_b, v_w_in, v_attn_sink, v_ssm_a_re, v_ssm_a_im, v_ssm_log_dt, v_ssm_b_re, v_ssm_b_im, v_ssm_c_re, v_ssm_c_im, v_ssm_d, v_w_glu, v_b_glu, v_norm_attn_g, v_norm_ssm_g, v_w_out, v_ln_g, v_ln_b):
    given = dict(x=x, w_in=w_in, attn_sink=attn_sink, ssm_a_re=ssm_a_re, ssm_a_im=ssm_a_im, ssm_log_dt=ssm_log_dt, ssm_b_re=ssm_b_re, ssm_b_im=ssm_b_im, ssm_c_re=ssm_c_re, ssm_c_im=ssm_c_im, ssm_d=ssm_d, w_glu=w_glu, b_glu=b_glu, norm_attn_g=norm_attn_g, norm_ssm_g=norm_ssm_g, w_out=w_out, ln_g=ln_g, ln_b=ln_b, loss_target=loss_target, m_w_in=m_w_in, m_attn_sink=m_attn_sink, m_ssm_a_re=m_ssm_a_re, m_ssm_a_im=m_ssm_a_im, m_ssm_log_dt=m_ssm_log_dt, m_ssm_b_re=m_ssm_b_re, m_ssm_b_im=m_ssm_b_im, m_ssm_c_re=m_ssm_c_re, m_ssm_c_im=m_ssm_c_im, m_ssm_d=m_ssm_d, m_w_glu=m_w_glu, m_b_glu=m_b_glu, m_norm_attn_g=m_norm_attn_g, m_norm_ssm_g=m_norm_ssm_g, m_w_out=m_w_out, m_ln_g=m_ln_g, m_ln_b=m_ln_b, v_w_in=v_w_in, v_attn_sink=v_attn_sink, v_ssm_a_re=v_ssm_a_re, v_ssm_a_im=v_ssm_a_im, v_ssm_log_dt=v_ssm_log_dt, v_ssm_b_re=v_ssm_b_re, v_ssm_b_im=v_ssm_b_im, v_ssm_c_re=v_ssm_c_re, v_ssm_c_im=v_ssm_c_im, v_ssm_d=v_ssm_d, v_w_glu=v_w_glu, v_b_glu=v_b_glu, v_norm_attn_g=v_norm_attn_g, v_norm_ssm_g=v_norm_ssm_g, v_w_out=v_w_out, v_ln_g=v_ln_g, v_ln_b=v_ln_b)
    weights = {n: given[n] for n in TWIN_WEIGHTS}
    shared = {n: given[n] for n in SHARED_INPUTS}
    per_example = {n: given[n] for n in ['x']}
    grad_fn = _jax.value_and_grad(_loss, argnums=(0, 1))

    def one_microbatch(ex, loss_target):
        ex = dict(ex)
        diff = ex.pop(TWIN_DIFF_INPUT)
        return grad_fn(weights, diff, {**shared, **ex}, loss_target)

    if N_MICROBATCH == 1:
        loss, (grad_w, grad_x) = one_microbatch(per_example, given["loss_target"])
    else:
        def body(carry, xs):
            loss_sum, grad_sum = carry
            l_k, (gw_k, gx_k) = one_microbatch(xs[0], xs[1])
            with _jax.named_scope("update"):
                return (loss_sum + l_k, _jax.tree.map(_jnp.add, grad_sum, gw_k)), gx_k

        init = (_jnp.zeros((), _jnp.float32), _jax.tree.map(_jnp.zeros_like, weights))
        (loss, grad_w), grad_x = _jax.lax.scan(body, init, (per_example, given["loss_target"]))
    with _jax.named_scope("update"):
        delta_w, new_m, new_v = {}, {}, {}
        for n in TWIN_WEIGHTS:
            delta_w[n], new_m[n], new_v[n] = _adamw(weights[n], grad_w[n], given["m_" + n], given["v_" + n])
    return (loss, grad_x, *[grad_w[n] for n in TWIN_WEIGHTS], *[delta_w[n] for n in TWIN_WEIGHTS],
            *[new_m[n] for n in TWIN_WEIGHTS], *[new_v[n] for n in TWIN_WEIGHTS])
```

```python
import functools
import math

import numpy as np
import jax
import jax.numpy as jnp
from jax import lax
from jax.experimental import pallas as pl
from jax.experimental.pallas import tpu as pltpu

F32 = jnp.float32
BF16 = jnp.bfloat16
MESH = pl.DeviceIdType.MESH

D_MODEL = 1024
D_ATTN = 512
D_SSM = 512
HEAD_DIM = 64
N_Q_HEADS = 8
WINDOW = 128
ROPE_THETA = 10000.0
SSM_CH = 16
N_GROUPS = 32
SSM_STATE = 64
N_DIR = 2
STATE_W = N_GROUPS * SSM_STATE
N_SLAB = 4
SLAB_IN = 128
SLAB_ST = 512
NORM_EPS = 1e-5
NEG_INF = -1e30
ALPHA = 2.0 ** 0.25
D_IN_PROJ = 2304
N_CHIPS = 4

ADAM_LR = 0.001
ADAM_B1 = 0.9
ADAM_B2 = 0.999
ADAM_EPS = 1e-08
ADAM_WD = 0.01
ADAM_STEP = 10

SUBSEG = 8
SCAN_LANES = 512
VMEM_LIMIT = 48 * 1024 * 1024
ADAMW_BLOCK_BYTES = 3 * 512 * 1024

_PAIR_PERM = np.array([(64 * j + l) if l < 64 else (64 * (j + 4) + l - 64) for j in range(4) for l in range(128)])
_COL_PERM = np.concatenate([_PAIR_PERM, np.arange(512, 768), 768 + _PAIR_PERM, np.arange(1280, 2304)])
_COL_INV = np.argsort(_COL_PERM)
_PAIR_INV = np.argsort(_PAIR_PERM)


def _cparams(sem=None):
    return pltpu.CompilerParams(dimension_semantics=sem, vmem_limit_bytes=VMEM_LIMIT)


def _dot(a, b):
    return jnp.dot(a, b, preferred_element_type=F32)


def _dot_nt(a, b):
    return lax.dot_general(a, b, (((1,), (1,)), ((), ())), preferred_element_type=F32)


def _dot_tn(a, b):
    return lax.dot_general(a, b, (((0,), (0,)), ((), ())), preferred_element_type=F32)


def _sigmoid(z):
    return 1.0 / (1.0 + jnp.exp(-z))


def _all_gather_chips(shards, out_dtype, name):
    n = len(shards)

    def body(*refs):
        in_refs, out_refs = refs[:n], refs[n:2 * n]
        send_sems, recv_sems = refs[2 * n:]
        x, y, c = lax.axis_index("x"), lax.axis_index("y"), lax.axis_index("c")
        sibling = (x, y, 1 - c)
        chips = [(1 - x, y), (x, 1 - y), (1 - x, 1 - y)]

        for a in range(n):
            out_refs[a][2 * x + y] = in_refs[a][...].astype(out_dtype)

        def half_of(a, px, py, half):
            rows = in_refs[a].shape[0] // 2
            return out_refs[a].at[2 * px + py, pl.ds(half * rows, rows), :]

        def copy(a, k, px, py, half, to):
            blk = half_of(a, px, py, half)
            return pltpu.make_async_remote_copy(src_ref=blk, dst_ref=blk, send_sem=send_sems.at[6 * a + k],
                                                recv_sem=recv_sems.at[6 * a + k], device_id=to, device_id_type=MESH)

        first = [copy(a, j, x, y, c, (*chips[j], c)) for a in range(n) for j in range(3)]
        for cp in first:
            cp.start()
        passed = []
        for a in range(n):
            for j in range(3):
                copy(a, j, *chips[j], c, (x, y, c)).wait_recv()
                fwd = copy(a, 3 + j, *chips[j], c, sibling)
                fwd.start()
                passed.append(fwd)
        for a in range(n):
            for j in range(3):
                copy(a, 3 + j, *chips[j], 1 - c, (x, y, c)).wait_recv()
        for cp in first + passed:
            cp.wait_send()

    vmem = pl.BlockSpec(memory_space=pltpu.VMEM)
    return pl.pallas_call(
        body, name=name,
        out_shape=[jax.ShapeDtypeStruct((N_CHIPS,) + s.shape, out_dtype) for s in shards],
        in_specs=[vmem] * n, out_specs=[vmem] * n,
        scratch_shapes=[pltpu.SemaphoreType.DMA((6 * n,)), pltpu.SemaphoreType.DMA((6 * n,))],
        compiler_params=pltpu.CompilerParams(vmem_limit_bytes=VMEM_LIMIT),
    )(*shards)


def _reduce_scatter_chips(pieces, name):
    _, rows, cols = pieces.shape
    h = rows // 2

    def body(p_ref, out_ref, a_ref, b_ref, send_sems, recv_sems):
        x, y, c = lax.axis_index("x"), lax.axis_index("y"), lax.axis_index("c")
        me = 2 * x + y
        sibling = (x, y, 1 - c)
        chips = [(1 - x, y), (x, 1 - y), (1 - x, 1 - y)]
        mine = pl.multiple_of(c * h, 8)
        other = pl.multiple_of((1 - c) * h, 8)

        swap = pltpu.make_async_remote_copy(src_ref=p_ref.at[:, pl.ds(other, h), :], dst_ref=a_ref,
                                            send_sem=send_sems.at[0], recv_sem=recv_sems.at[0],
                                            device_id=sibling, device_id_type=MESH)
        swap.start()
        swap.wait_recv()
        for k in range(N_CHIPS):
            a_ref[k] = a_ref[k] + p_ref[k, pl.ds(mine, h), :]

        def chip_copy(j, dst_slot):
            px, py = chips[j]
            return pltpu.make_async_remote_copy(src_ref=a_ref.at[2 * px + py], dst_ref=b_ref.at[dst_slot],
                                                send_sem=send_sems.at[1 + j], recv_sem=recv_sems.at[1 + j],
                                                device_id=(px, py, c), device_id_type=MESH)

        sends = [chip_copy(j, me) for j in range(3)]
        for cp in sends:
            cp.start()
        b_ref[me] = a_ref[me]
        for j in range(3):
            chip_copy(j, 2 * chips[j][0] + chips[j][1]).wait_recv()
        out_ref[pl.ds(mine, h), :] = (b_ref[0] + b_ref[1]) + (b_ref[2] + b_ref[3])

        back = pltpu.make_async_remote_copy(src_ref=out_ref.at[pl.ds(mine, h), :], dst_ref=out_ref.at[pl.ds(mine, h), :],
                                            send_sem=send_sems.at[4], recv_sem=recv_sems.at[4],
                                            device_id=sibling, device_id_type=MESH)
        back.start()
        pltpu.make_async_remote_copy(src_ref=out_ref.at[pl.ds(other, h), :], dst_ref=out_ref.at[pl.ds(other, h), :],
                                     send_sem=send_sems.at[4], recv_sem=recv_sems.at[4],
                                     device_id=sibling, device_id_type=MESH).wait_recv()
        swap.wait_send()
        for cp in sends:
            cp.wait_send()
        back.wait_send()

    vmem = pl.BlockSpec(memory_space=pltpu.VMEM)
    return pl.pallas_call(
        body, name=name,
        out_shape=jax.ShapeDtypeStruct((rows, cols), F32),
        in_specs=[vmem], out_specs=vmem,
        scratch_shapes=[pltpu.VMEM((N_CHIPS, h, cols), F32), pltpu.VMEM((N_CHIPS, h, cols), F32),
                        pltpu.SemaphoreType.DMA((5,)), pltpu.SemaphoreType.DMA((5,))],
        compiler_params=pltpu.CompilerParams(vmem_limit_bytes=VMEM_LIMIT),
    )(pieces)


def _ssm_param_values(ar, ai, logdt):
    dt = jnp.exp(logdt)
    mag = jnp.exp(dt * ar)
    cs, sn = jnp.cos(dt * ai), jnp.sin(dt * ai)
    lr, li = mag * cs, mag * sn
    den = ar * ar + ai * ai
    nr = (lr - 1.0) * ar + li * ai
    ni = li * ar - (lr - 1.0) * ai
    return dt, mag, lr, li, den, nr, ni


def _ssm_params_fwd(ar, ai, logdt, br, bi, n_square):
    def body(ar_ref, ai_ref, dt_ref, br_ref, bi_ref, lr_ref, li_ref, pr_ref, pi_ref, bbr_ref, bbi_ref):
        _, _, lr, li, den, nr, ni = _ssm_param_values(ar_ref[...], ai_ref[...], dt_ref[...])
        lr_ref[...] = lr
        li_ref[...] = li
        pr, pi = lr, li
        for _ in range(n_square):
            pr, pi = pr * pr - pi * pi, 2.0 * pr * pi
        pr_ref[...] = pr
        pi_ref[...] = pi
        fr, fi = nr / den, ni / den
        b_r, b_i = br_ref[...], bi_ref[...]
        bbr_ref[...] = fr * b_r - fi * b_i
        bbi_ref[...] = fr * b_i + fi * b_r

    small = jax.ShapeDtypeStruct(ar.shape, F32)
    big = jax.ShapeDtypeStruct(br.shape, F32)
    return pl.pallas_call(body, name="ssm_params_fwd", out_shape=[small] * 4 + [big] * 2)(ar, ai, logdt, br, bi)


def _ssm_params_bwd(ar, ai, logdt, br, bi, dlam_r, dlam_i, dbb_r, dbb_i):
    def body(ar_ref, ai_ref, dt_ref, br_ref, bi_ref, dlr_ref, dli_ref, dbr_ref, dbi_ref,
             gar_ref, gai_ref, gdt_ref, gbr_ref, gbi_ref):
        a_r, a_i = ar_ref[...], ai_ref[...]
        dt, mag, lr, li, den, nr, ni = _ssm_param_values(a_r, a_i, dt_ref[...])
        fr, fi = nr / den, ni / den
        b_r, b_i = br_ref[...], bi_ref[...]
        g_r, g_i = dbr_ref[...], dbi_ref[...]
        gbr_ref[...] = fr * g_r + fi * g_i
        gbi_ref[...] = fr * g_i - fi * g_r
        d_fr = jnp.sum(b_r * g_r + b_i * g_i, axis=1, keepdims=True)
        d_fi = jnp.sum(b_r * g_i - b_i * g_r, axis=1, keepdims=True)
        d_nr, d_ni = d_fr / den, d_fi / den
        d_den = -(d_fr * nr + d_fi * ni) / (den * den)
        d_lr = jnp.sum(dlr_ref[...], axis=1, keepdims=True) + d_nr * a_r - d_ni * a_i
        d_li = jnp.sum(dli_ref[...], axis=1, keepdims=True) + d_nr * a_i + d_ni * a_r
        d_ar = d_nr * (lr - 1.0) + d_ni * li + d_den * 2.0 * a_r
        d_ai = d_nr * li - d_ni * (lr - 1.0) + d_den * 2.0 * a_i
        d_mag = (d_lr * lr + d_li * li) / mag
        d_theta = d_li * lr - d_lr * li
        gar_ref[...] = d_ar + d_mag * mag * dt
        gai_ref[...] = d_ai + d_theta * dt
        d_dt = d_mag * mag * a_r + d_theta * a_i
        gdt_ref[...] = jnp.sum(d_dt, axis=2, keepdims=True) * dt

    small = jax.ShapeDtypeStruct(ar.shape, F32)
    return pl.pallas_call(
        body, name="ssm_params_bwd",
        out_shape=[small, small, jax.ShapeDtypeStruct(logdt.shape, F32),
                   jax.ShapeDtypeStruct(br.shape, F32), jax.ShapeDtypeStruct(br.shape, F32)],
    )(ar, ai, logdt, br, bi, dlam_r, dlam_i, dbb_r, dbb_i)


def _rope_tables(seq):
    half = HEAD_DIM // 2
    inv_freq = ROPE_THETA ** (-jnp.arange(half, dtype=F32) / half)
    ang = jnp.arange(seq, dtype=jnp.int32).astype(F32)[:, None] * inv_freq[None, :]
    cos, sin = jnp.cos(ang), jnp.sin(ang)
    cos128 = jnp.concatenate([cos, cos, cos, cos], axis=1)
    sin128 = jnp.concatenate([-sin, sin, -sin, sin], axis=1)
    return cos128, sin128


def _rotate_half_unsigned(t):
    lane = lax.broadcasted_iota(jnp.int32, t.shape, 1)
    return jnp.where((lane % HEAD_DIM) < HEAD_DIM // 2, pltpu.roll(t, 96, 1), pltpu.roll(t, 32, 1))


def _rope(t, cos, sin_signed):
    return t * cos + _rotate_half_unsigned(t) * sin_signed


def _proj(x, w_bf, cos128, sin128, tb):
    seq = x.shape[0]

    def body(x_ref, w_ref, cos_ref, sin_ref, q_ref, k_ref, v_ref, za_ref, u_ref, zs_ref):
        xb = x_ref[...].astype(BF16)
        cos, sin = cos_ref[...], sin_ref[...]
        lo = lax.broadcasted_iota(jnp.int32, (tb, 128), 1) < HEAD_DIM
        q = _dot(xb, w_ref[:, 0:512])
        for j in range(4):
            qj = _rope(q[:, 128 * j:128 * (j + 1)], cos, sin)
            q_ref[j] = jnp.where(lo, qj, 0.0).astype(BF16)
            q_ref[4 + j] = jnp.where(lo, 0.0, qj).astype(BF16)
        kv = _dot(xb, w_ref[:, 512:768])
        k_ref[...] = _rope(kv[:, 0:128], cos, sin).astype(BF16)
        v_ref[...] = kv[:, 128:256].astype(BF16)
        za_ref[...] = _dot(xb, w_ref[:, 768:1280])
        u_val = _dot(xb, w_ref[:, 1280:1792])
        for k in range(N_SLAB):
            u_ref[k] = u_val[:, k * SLAB_IN:(k + 1) * SLAB_IN]
        zs_ref[...] = _dot(xb, w_ref[:, 1792:2304])

    row = lambda w: pl.BlockSpec((tb, w), lambda i: (i, 0))
    return pl.pallas_call(
        body, name="proj", grid=(seq // tb,),
        in_specs=[row(D_MODEL), pl.BlockSpec((D_MODEL, D_IN_PROJ), lambda i: (0, 0)), row(128), row(128)],
        out_specs=[pl.BlockSpec((8, tb, 128), lambda i: (0, i, 0)), row(128), row(128), row(512),
                   pl.BlockSpec((N_SLAB, tb, SLAB_IN), lambda i: (0, i, 0)), row(512)],
        out_shape=[jax.ShapeDtypeStruct((8, seq, 128), BF16), jax.ShapeDtypeStruct((seq, 128), BF16),
                   jax.ShapeDtypeStruct((seq, 128), BF16), jax.ShapeDtypeStruct((seq, 512), F32),
                   jax.ShapeDtypeStruct((N_SLAB, seq, SLAB_IN), F32), jax.ShapeDtypeStruct((seq, 512), F32)],
        compiler_params=_cparams(("arbitrary",)),
    )(x, w_bf, cos128, sin128)


ATT_TQ = 128
ATT_KEYS = 3 * ATT_TQ


def _attn_window(i, seq):
    start = jnp.clip(i * ATT_TQ - WINDOW, 0, seq - ATT_KEYS)
    return pl.multiple_of(start, ATT_TQ)


def _attn_probs(q_ref, k_ref, sink_ref, i, seq):
    start = _attn_window(i, seq)
    qall = q_ref[...].reshape(8 * ATT_TQ, 128)
    kw = k_ref[pl.ds(start, ATT_KEYS), :]
    s = _dot_nt(qall, kw) * (HEAD_DIM ** -0.5)
    qpos = i * ATT_TQ + lax.broadcasted_iota(jnp.int32, (8 * ATT_TQ, ATT_KEYS), 0) % ATT_TQ
    kpos = start + lax.broadcasted_iota(jnp.int32, (8 * ATT_TQ, ATT_KEYS), 1)
    s = jnp.where(jnp.abs(qpos - kpos) <= WINDOW, s, NEG_INF)
    sink = sink_ref[...]
    m = jnp.maximum(jnp.max(s, axis=1, keepdims=True), sink)
    p = jnp.exp(s - m)
    p_sink = jnp.exp(sink - m)
    denom = jnp.sum(p, axis=1, keepdims=True) + p_sink
    return start, qall, kw, p, denom, p_sink


def _attn_fwd(q_stack, k, v, sink_rows):
    seq = k.shape[0]

    def body(q_ref, k_ref, v_ref, sink_ref, o_ref):
        i = pl.program_id(0)
        start, _, _, p, denom, _ = _attn_probs(q_ref, k_ref, sink_ref, i, seq)
        vw = v_ref[pl.ds(start, ATT_KEYS), :]
        o_all = _dot(p.astype(BF16), vw) / denom
        lo = lax.broadcasted_iota(jnp.int32, (ATT_TQ, 128), 1) < HEAD_DIM
        for j in range(4):
            o_ref[:, 128 * j:128 * (j + 1)] = jnp.where(lo, o_all[j * ATT_TQ:(j + 1) * ATT_TQ],
                                                        o_all[(4 + j) * ATT_TQ:(5 + j) * ATT_TQ])

    full = lambda w: pl.BlockSpec((seq, w), lambda i: (0, 0))
    return pl.pallas_call(
        body, name="attn_fwd", grid=(seq // ATT_TQ,),
        in_specs=[pl.BlockSpec((8, ATT_TQ, 128), lambda i: (0, i, 0)), full(128), full(128),
                  pl.BlockSpec((8 * ATT_TQ, 1), lambda i: (0, 0))],
        out_specs=pl.BlockSpec((ATT_TQ, 512), lambda i: (i, 0)),
        out_shape=jax.ShapeDtypeStruct((seq, 512), F32),
        compiler_params=_cparams(("arbitrary",)),
    )(q_stack, k, v, sink_rows)


def _attn_bwd(q_stack, k, v, sink_rows, d_o):
    seq = k.shape[0]

    def body(q_ref, k_ref, v_ref, sink_ref, do_ref, dq_ref, dk_ref, dv_ref, dsink_ref):
        i = pl.program_id(0)

        @pl.when(i == 0)
        def _():
            dk_ref[...] = jnp.zeros_like(dk_ref)
            dv_ref[...] = jnp.zeros_like(dv_ref)
            dsink_ref[...] = jnp.zeros_like(dsink_ref)

        start, qall, kw, p, denom, p_sink = _attn_probs(q_ref, k_ref, sink_ref, i, seq)
        vw = v_ref[pl.ds(start, ATT_KEYS), :]
        lo = lax.broadcasted_iota(jnp.int32, (ATT_TQ, 128), 1) < HEAD_DIM
        d_o_blk = do_ref[...]
        parts = [jnp.where(lo, d_o_blk[:, 128 * j:128 * (j + 1)], 0.0) for j in range(4)]
        parts += [jnp.where(lo, 0.0, d_o_blk[:, 128 * j:128 * (j + 1)]) for j in range(4)]
        do_all = jnp.concatenate(parts, axis=0).astype(BF16)
        inv = 1.0 / denom
        probs = p * inv
        dp = _dot_nt(do_all, vw)
        delta = jnp.sum(probs * dp, axis=1, keepdims=True)
        ds = (probs * (dp - delta)).astype(BF16)
        dsink_ref[...] += -(p_sink * inv) * delta
        scale = HEAD_DIM ** -0.5
        dq_all = _dot(ds, kw) * scale
        for j in range(4):
            dq_ref[:, 128 * j:128 * (j + 1)] = jnp.where(lo, dq_all[j * ATT_TQ:(j + 1) * ATT_TQ],
                                                         dq_all[(4 + j) * ATT_TQ:(5 + j) * ATT_TQ])
        dk_ref[pl.ds(start, ATT_KEYS), :] += _dot_tn(ds, qall) * scale
        dv_ref[pl.ds(start, ATT_KEYS), :] += _dot_tn(probs.astype(BF16), do_all)

    full = lambda w: pl.BlockSpec((seq, w), lambda i: (0, 0))
    rows = pl.BlockSpec((8 * ATT_TQ, 1), lambda i: (0, 0))
    return pl.pallas_call(
        body, name="attn_bwd", grid=(seq // ATT_TQ,),
        in_specs=[pl.BlockSpec((8, ATT_TQ, 128), lambda i: (0, i, 0)), full(128), full(128), rows,
                  pl.BlockSpec((ATT_TQ, 512), lambda i: (i, 0))],
        out_specs=[pl.BlockSpec((ATT_TQ, 512), lambda i: (i, 0)), full(128), full(128), rows],
        out_shape=[jax.ShapeDtypeStruct((seq, 512), F32), jax.ShapeDtypeStruct((seq, 128), F32),
                   jax.ShapeDtypeStruct((seq, 128), F32), jax.ShapeDtypeStruct((8 * ATT_TQ, 1), F32)],
        compiler_params=_cparams(("arbitrary",)),
    )(q_stack, k, v, sink_rows, d_o)


def _permute_rows(dst_ref, src_ref, sub_len):
    for k in range(N_SLAB):
        for j in range(sub_len):
            dst_ref[k, 8 * j:8 * (j + 1), :] = src_ref.at[k][pl.ds(j, SUBSEG, stride=sub_len), :]


def _unpermute_rows(dst_ref, src_ref, sub_len):
    for k in range(N_SLAB):
        for s in range(SUBSEG):
            dst_ref[k, s * sub_len:(s + 1) * sub_len, :] = src_ref.at[k][pl.ds(s, sub_len, stride=SUBSEG), :]


def _scan_pass(br_ref, bi_ref, lr_row, li_row, start, end_refs, *, sub_len, reverse, store):
    width = br_ref.shape[1]
    for c0 in range(0, width, SCAN_LANES):
        cols = slice(c0, c0 + SCAN_LANES)
        lr = jnp.broadcast_to(lr_row[:, cols], (SUBSEG, SCAN_LANES))
        li = jnp.broadcast_to(li_row[:, cols], (SUBSEG, SCAN_LANES))
        if start is None:
            init = (jnp.zeros((SUBSEG, SCAN_LANES), F32), jnp.zeros((SUBSEG, SCAN_LANES), F32))
        else:
            init = (start[0][:, cols], start[1][:, cols])

        def step(jj, state, cols=cols, lr=lr, li=li):
            sr, si = state
            j = (sub_len - 1 - jj) if reverse else jj
            r0 = pl.multiple_of(j * SUBSEG, SUBSEG)
            nr = lr * sr - li * si + br_ref[pl.ds(r0, SUBSEG), cols]
            ni = lr * si + li * sr + bi_ref[pl.ds(r0, SUBSEG), cols]
            if store:
                br_ref[pl.ds(r0, SUBSEG), cols] = nr
                bi_ref[pl.ds(r0, SUBSEG), cols] = ni
            return nr, ni

        sr, si = lax.fori_loop(0, sub_len, step, init, unroll=4)
        if end_refs is not None:
            end_refs[0][:, cols] = sr
            end_refs[1][:, cols] = si


def _resolve_starts(z_refs, carry_refs, start_refs, pr_row, pi_row, *, reverse):
    cr, ci = carry_refs[0][0:1, :], carry_refs[1][0:1, :]
    for s in (range(SUBSEG - 1, -1, -1) if reverse else range(SUBSEG)):
        start_refs[0][s:s + 1, :] = cr
        start_refs[1][s:s + 1, :] = ci
        zr, zi = z_refs[0][s:s + 1, :], z_refs[1][s:s + 1, :]
        cr, ci = pr_row * cr - pi_row * ci + zr, pr_row * ci + pi_row * cr + zi
    carry_refs[0][0:1, :] = cr
    carry_refs[1][0:1, :] = ci


def _ssm_fwd(u, lam, lam_pow, bb, cb, *, reverse, tb, name):
    seq = u.shape[1]
    nblk = seq // tb
    sub_len = tb // SUBSEG

    def body(u_ref, lr_ref, li_ref, pr_ref, pi_ref, bbr_ref, bbi_ref, cbr_ref, cbi_ref,
             y_ref, sr_ref, si_ref, xr, xi, up, yp, zr, zi, car, cai):
        @pl.when(pl.program_id(0) == 0)
        def _():
            car[...] = jnp.zeros_like(car)
            cai[...] = jnp.zeros_like(cai)

        _permute_rows(up, u_ref, sub_len)
        for k in range(N_SLAB):
            ub = up[k].astype(BF16)
            xr[:, k * SLAB_ST:(k + 1) * SLAB_ST] = _dot(ub, bbr_ref[k])
            xi[:, k * SLAB_ST:(k + 1) * SLAB_ST] = _dot(ub, bbi_ref[k])
        lr, li = lr_ref[...], li_ref[...]
        _scan_pass(xr, xi, lr, li, None, (zr, zi), sub_len=sub_len, reverse=reverse, store=False)
        _resolve_starts((zr, zi), (car, cai), (sr_ref, si_ref), pr_ref[...], pi_ref[...], reverse=reverse)
        _scan_pass(xr, xi, lr, li, (sr_ref, si_ref), None, sub_len=sub_len, reverse=reverse, store=True)
        for k in range(N_SLAB):
            st = slice(k * SLAB_ST, (k + 1) * SLAB_ST)
            yp[k] = _dot(xr[:, st].astype(BF16), cbr_ref[k]) - _dot(xi[:, st].astype(BF16), cbi_ref[k])
        _unpermute_rows(y_ref, yp, sub_len)

    blk = (lambda i: nblk - 1 - i) if reverse else (lambda i: i)
    row = pl.BlockSpec((1, STATE_W), lambda i: (0, 0))
    slab_b = pl.BlockSpec((N_SLAB, SLAB_IN, SLAB_ST), lambda i: (0, 0, 0))
    slab_c = pl.BlockSpec((N_SLAB, SLAB_ST, SLAB_IN), lambda i: (0, 0, 0))
    tok = pl.BlockSpec((N_SLAB, tb, SLAB_IN), lambda i: (0, blk(i), 0))
    start_spec = pl.BlockSpec((None, SUBSEG, STATE_W), lambda i: (blk(i), 0, 0))
    return pl.pallas_call(
        body, name=name, grid=(nblk,),
        in_specs=[tok, row, row, row, row, slab_b, slab_b, slab_c, slab_c],
        out_specs=[tok, start_spec, start_spec],
        out_shape=[jax.ShapeDtypeStruct((N_SLAB, seq, SLAB_IN), F32), jax.ShapeDtypeStruct((nblk, SUBSEG, STATE_W), F32),
                   jax.ShapeDtypeStruct((nblk, SUBSEG, STATE_W), F32)],
        scratch_shapes=[pltpu.VMEM((tb, STATE_W), F32), pltpu.VMEM((tb, STATE_W), F32),
                        pltpu.VMEM((N_SLAB, tb, SLAB_IN), F32), pltpu.VMEM((N_SLAB, tb, SLAB_IN), F32),
                        pltpu.VMEM((SUBSEG, STATE_W), F32), pltpu.VMEM((SUBSEG, STATE_W), F32),
                        pltpu.VMEM((SUBSEG, STATE_W), F32), pltpu.VMEM((SUBSEG, STATE_W), F32)],
        compiler_params=_cparams(("arbitrary",)),
    )(u, *lam, *lam_pow, *bb, *cb)


def _ssm_bwd(u, dy, starts, lam, lam_pow, bb, bbt, cb_t, *, reverse, tb, name):
    seq = u.shape[1]
    nblk = seq // tb
    sub_len = tb // SUBSEG

    def body(u_ref, dy_ref, sr_ref, si_ref, lr_ref, li_ref, pr_ref, pi_ref, bbr_ref, bbi_ref, btr_ref, bti_ref,
             ctr_ref, cti_ref, du_ref, gbr_ref, gbi_ref, gcr_ref, gci_ref, dlr_ref, dli_ref,
             xr, xi, gr, gi, up, dyp, dup, zr, zi, gsr, gsi, car, cai):
        @pl.when(pl.program_id(0) == 0)
        def _():
            for ref in (car, cai, gbr_ref, gbi_ref, gcr_ref, gci_ref, dlr_ref, dli_ref):
                ref[...] = jnp.zeros_like(ref)

        _permute_rows(up, u_ref, sub_len)
        _permute_rows(dyp, dy_ref, sub_len)
        lr, li = lr_ref[...], li_ref[...]
        for k in range(N_SLAB):
            st = slice(k * SLAB_ST, (k + 1) * SLAB_ST)
            ub = up[k].astype(BF16)
            xr[:, st] = _dot(ub, bbr_ref[k])
            xi[:, st] = _dot(ub, bbi_ref[k])
            dyb = dyp[k].astype(BF16)
            gr[:, st] = _dot(dyb, ctr_ref[k])
            gi[:, st] = -_dot(dyb, cti_ref[k])
        _scan_pass(xr, xi, lr, li, (sr_ref, si_ref), None, sub_len=sub_len, reverse=reverse, store=True)
        for k in range(N_SLAB):
            st = slice(k * SLAB_ST, (k + 1) * SLAB_ST)
            dyb = dyp[k].astype(BF16)
            gcr_ref[k] += _dot_tn(xr[:, st].astype(BF16), dyb)
            gci_ref[k] -= _dot_tn(xi[:, st].astype(BF16), dyb)
        nli = -li
        _scan_pass(gr, gi, lr, nli, None, (zr, zi), sub_len=sub_len, reverse=not reverse, store=False)
        _resolve_starts((zr, zi), (car, cai), (gsr, gsi), pr_ref[...], -pi_ref[...], reverse=not reverse)
        _scan_pass(gr, gi, lr, nli, (gsr, gsi), None, sub_len=sub_len, reverse=not reverse, store=True)
        for k in range(N_SLAB):
            st = slice(k * SLAB_ST, (k + 1) * SLAB_ST)
            ub = up[k].astype(BF16)
            grb, gib = gr[:, st].astype(BF16), gi[:, st].astype(BF16)
            gbr_ref[k] += _dot_tn(ub, grb)
            gbi_ref[k] += _dot_tn(ub, gib)
            dup[k] = _dot(grb, btr_ref[k]) + _dot(gib, bti_ref[k])
        _unpermute_rows(du_ref, dup, sub_len)

        for c0 in range(0, STATE_W, SCAN_LANES):
            cols = slice(c0, c0 + SCAN_LANES)
            edge = (sub_len - 1) * SUBSEG if reverse else 0
            g_r, g_i = gr[edge:edge + SUBSEG, cols], gi[edge:edge + SUBSEG, cols]
            x_r, x_i = sr_ref[:, cols], si_ref[:, cols]
            acc = (dlr_ref[:, cols] + (g_r * x_r + g_i * x_i), dli_ref[:, cols] + (g_i * x_r - g_r * x_i))

            def step(jj, acc, cols=cols):
                r_g = pl.multiple_of((jj if reverse else jj + 1) * SUBSEG, SUBSEG)
                r_x = pl.multiple_of((jj + 1 if reverse else jj) * SUBSEG, SUBSEG)
                g_r, g_i = gr[pl.ds(r_g, SUBSEG), cols], gi[pl.ds(r_g, SUBSEG), cols]
                x_r, x_i = xr[pl.ds(r_x, SUBSEG), cols], xi[pl.ds(r_x, SUBSEG), cols]
                return acc[0] + (g_r * x_r + g_i * x_i), acc[1] + (g_i * x_r - g_r * x_i)

            acc = lax.fori_loop(0, sub_len - 1, step, acc, unroll=4)
            dlr_ref[:, cols] = acc[0]
            dli_ref[:, cols] = acc[1]

    blk = (lambda i: i) if reverse else (lambda i: nblk - 1 - i)
    row = pl.BlockSpec((1, STATE_W), lambda i: (0, 0))
    slab_b = pl.BlockSpec((N_SLAB, SLAB_IN, SLAB_ST), lambda i: (0, 0, 0))
    slab_c = pl.BlockSpec((N_SLAB, SLAB_ST, SLAB_IN), lambda i: (0, 0, 0))
    tok = pl.BlockSpec((N_SLAB, tb, SLAB_IN), lambda i: (0, blk(i), 0))
    start_spec = pl.BlockSpec((None, SUBSEG, STATE_W), lambda i: (blk(i), 0, 0))
    acc8 = pl.BlockSpec((SUBSEG, STATE_W), lambda i: (0, 0))
    big = lambda: pltpu.VMEM((tb, STATE_W), F32)
    slabs = lambda: pltpu.VMEM((N_SLAB, tb, SLAB_IN), F32)
    tile = lambda: pltpu.VMEM((SUBSEG, STATE_W), F32)
    return pl.pallas_call(
        body, name=name, grid=(nblk,),
        in_specs=[tok, tok, start_spec, start_spec, row, row, row, row, slab_b, slab_b, slab_c, slab_c, slab_b, slab_b],
        out_specs=[tok, slab_b, slab_b, slab_c, slab_c, acc8, acc8],
        out_shape=[jax.ShapeDtypeStruct((N_SLAB, seq, SLAB_IN), F32),
                   jax.ShapeDtypeStruct((N_SLAB, SLAB_IN, SLAB_ST), F32), jax.ShapeDtypeStruct((N_SLAB, SLAB_IN, SLAB_ST), F32),
                   jax.ShapeDtypeStruct((N_SLAB, SLAB_ST, SLAB_IN), F32), jax.ShapeDtypeStruct((N_SLAB, SLAB_ST, SLAB_IN), F32),
                   jax.ShapeDtypeStruct((SUBSEG, STATE_W), F32), jax.ShapeDtypeStruct((SUBSEG, STATE_W), F32)],
        scratch_shapes=[big(), big(), big(), big(), slabs(), slabs(), slabs(),
                        tile(), tile(), tile(), tile(), tile(), tile()],
        compiler_params=_cparams(("arbitrary",)),
    )(u, dy, *starts, *lam, *lam_pow, *bb, *bbt, *cb_t)


GELU_C = math.sqrt(2.0 / math.pi)
GELU_K = 0.044715


def _mid(o, za, u, y_f, y_b, zs, x, target, ssm_d, w_glu, w_glu_t, b_glu, g_attn, g_ssm, w_out, w_out_t, ln_g, ln_b, tb):
    seq = x.shape[0]

    def body(o_ref, za_ref, u_ref, yf_ref, yb_ref, zs_ref, x_ref, t_ref, d_ref, wg_ref, wgt_ref, bg_ref, ga_ref, gs_ref,
             wo_ref, wot_ref, lg_ref, lb_ref,
             loss_ref, do_ref, dza_ref, dyl_ref, dzs_ref, dpre_ref, gwo_ref, gwg_ref, vec_ref):
        @pl.when(pl.program_id(0) == 0)
        def _():
            for ref in (loss_ref, gwo_ref, gwg_ref, vec_ref):
                ref[...] = jnp.zeros_like(ref)

        o, za = o_ref[...], za_ref[...]
        sig_a = _sigmoid(za)
        silu_a = za * sig_a
        ya = o * silu_a
        r_a = lax.rsqrt(jnp.mean(ya * ya, axis=1, keepdims=True) + NORM_EPS)
        n_a = ya * r_a
        g_a = ga_ref[...]
        unslab = lambda ref: jnp.concatenate([ref[k] for k in range(N_SLAB)], axis=1)
        u_blk, zs = unslab(u_ref), zs_ref[...]
        d_row = d_ref[...]
        ylin = d_row * u_blk + unslab(yf_ref) + unslab(yb_ref)
        inner = GELU_C * (ylin + GELU_K * ylin * ylin * ylin)
        th = jnp.tanh(inner)
        gl = 0.5 * ylin * (1.0 + th)
        glb = gl.astype(BF16)
        sg = _sigmoid(_dot(glb, wg_ref[...]) + bg_ref[...])
        y2 = gl * sg
        sig_s = _sigmoid(zs)
        silu_s = zs * sig_s
        ys = y2 * silu_s
        r_s = lax.rsqrt(jnp.mean(ys * ys, axis=1, keepdims=True) + NORM_EPS)
        n_s = ys * r_s
        g_s = gs_ref[...]
        mixed = jnp.concatenate([n_a * g_a, n_s * g_s], axis=1).astype(BF16)
        pre = ALPHA * x_ref[...] + _dot(mixed, wo_ref[...])
        mu = jnp.mean(pre, axis=1, keepdims=True)
        cen = pre - mu
        rstd = lax.rsqrt(jnp.mean(cen * cen, axis=1, keepdims=True) + NORM_EPS)
        hhat = cen * rstd
        ln_g = lg_ref[...]
        err = hhat * ln_g + lb_ref[...] - t_ref[...]
        loss_ref[...] += 0.5 * jnp.sum(jnp.mean(err * err, axis=1, keepdims=True))

        dh = err * (1.0 / D_MODEL)
        vec_ref[0:1, :] += jnp.sum(dh * hhat, axis=0, keepdims=True)
        vec_ref[1:2, :] += jnp.sum(dh, axis=0, keepdims=True)
        dhh = dh * ln_g
        dpre = rstd * (dhh - jnp.mean(dhh, axis=1, keepdims=True) - hhat * jnp.mean(dhh * hhat, axis=1, keepdims=True))
        dpre_ref[...] = dpre
        dpb = dpre.astype(BF16)
        gwo_ref[...] += _dot_tn(mixed, dpb)
        dmix = _dot(dpb, wot_ref[...])
        dna = dmix[:, :D_ATTN]
        vec_ref[2:3, 0:D_ATTN] += jnp.sum(dna * n_a, axis=0, keepdims=True)
        dna = dna * g_a
        dya = r_a * (dna - n_a * jnp.mean(dna * n_a, axis=1, keepdims=True))
        do_ref[...] = dya * silu_a
        dza_ref[...] = dya * o * (sig_a * (1.0 + za * (1.0 - sig_a)))
        dns = dmix[:, D_ATTN:]
        vec_ref[2:3, D_ATTN:] += jnp.sum(dns * n_s, axis=0, keepdims=True)
        dns = dns * g_s
        dys = r_s * (dns - n_s * jnp.mean(dns * n_s, axis=1, keepdims=True))
        dzs_ref[...] = dys * y2 * (sig_s * (1.0 + zs * (1.0 - sig_s)))
        dy2 = dys * silu_s
        da = dy2 * gl * sg * (1.0 - sg)
        vec_ref[3:4, D_SSM:] += jnp.sum(da, axis=0, keepdims=True)
        dab = da.astype(BF16)
        gwg_ref[...] += _dot_tn(glb, dab)
        dgl = dy2 * sg + _dot(dab, wgt_ref[...])
        dylin = dgl * (0.5 * (1.0 + th) + 0.5 * ylin * (1.0 - th * th) * GELU_C * (1.0 + 3.0 * GELU_K * ylin * ylin))
        for k in range(N_SLAB):
            dyl_ref[k] = dylin[:, k * SLAB_IN:(k + 1) * SLAB_IN]
        vec_ref[3:4, 0:D_SSM] += jnp.sum(dylin * u_blk, axis=0, keepdims=True)

    tok = lambda w: pl.BlockSpec((tb, w), lambda i: (i, 0))
    slab = pl.BlockSpec((N_SLAB, tb, SLAB_IN), lambda i: (0, i, 0))
    const = lambda r, c: pl.BlockSpec((r, c), lambda i: (0, 0))
    tok_shape = jax.ShapeDtypeStruct((seq, 512), F32)
    return pl.pallas_call(
        body, name="mid", grid=(seq // tb,),
        in_specs=[tok(512), tok(512), slab, slab, slab, tok(512), tok(1024), tok(1024),
                  const(1, 512), const(512, 512), const(512, 512), const(1, 512), const(1, 512), const(1, 512),
                  const(1024, 1024), const(1024, 1024), const(1, 1024), const(1, 1024)],
        out_specs=[const(8, 128), tok(512), tok(512), slab, tok(512), tok(1024),
                   const(1024, 1024), const(512, 512), const(8, 1024)],
        out_shape=[jax.ShapeDtypeStruct((8, 128), F32), tok_shape, tok_shape,
                   jax.ShapeDtypeStruct((N_SLAB, seq, SLAB_IN), F32), tok_shape,
                   jax.ShapeDtypeStruct((seq, 1024), F32), jax.ShapeDtypeStruct((1024, 1024), F32),
                   jax.ShapeDtypeStruct((512, 512), F32), jax.ShapeDtypeStruct((8, 1024), F32)],
        compiler_params=_cparams(("arbitrary",)),
    )(o, za, u, y_f, y_b, zs, x, target, ssm_d, w_glu, w_glu_t, b_glu, g_attn, g_ssm, w_out, w_out_t, ln_g, ln_b)


def _proj_bwd(x, dq, dk, dv, dza, du_f, du_b, dylin, dzs, dpre, ssm_d, cos128, sin128, w_t, tb):
    seq = x.shape[0]

    def body(x_ref, dq_ref, dk_ref, dv_ref, dza_ref, duf_ref, dub_ref, dyl_ref, dzs_ref, dpre_ref, d_ref,
             cos_ref, sin_ref, wt_ref, gx_ref, gw_ref):
        @pl.when(pl.program_id(0) == 0)
        def _():
            gw_ref[...] = jnp.zeros_like(gw_ref)

        cos, sin = cos_ref[...], sin_ref[...]

        def unrope(t):
            return t * cos + _rotate_half_unsigned(t * sin)

        dq_rot = dq_ref[...]
        pieces = [unrope(dq_rot[:, 128 * j:128 * (j + 1)]) for j in range(4)]
        d_row = d_ref[...]
        pieces += [unrope(dk_ref[...]), dv_ref[...], dza_ref[...]]
        pieces += [duf_ref[k] + dub_ref[k] + d_row[:, k * SLAB_IN:(k + 1) * SLAB_IN] * dyl_ref[k] for k in range(N_SLAB)]
        pieces += [dzs_ref[...]]
        dproj = jnp.concatenate(pieces, axis=1).astype(BF16)
        gx_ref[...] = ALPHA * dpre_ref[...] + _dot(dproj, wt_ref[...])
        gw_ref[...] += _dot_tn(x_ref[...].astype(BF16), dproj)

    tok = lambda w: pl.BlockSpec((tb, w), lambda i: (i, 0))
    slab = pl.BlockSpec((N_SLAB, tb, SLAB_IN), lambda i: (0, i, 0))
    const = lambda r, c: pl.BlockSpec((r, c), lambda i: (0, 0))
    return pl.pallas_call(
        body, name="proj_bwd", grid=(seq // tb,),
        in_specs=[tok(1024), tok(512), tok(128), tok(128), tok(512), slab, slab, slab, tok(512), tok(1024),
                  const(1, 512), tok(128), tok(128), const(D_IN_PROJ, D_MODEL)],
        out_specs=[tok(1024), const(D_MODEL, D_IN_PROJ)],
        out_shape=[jax.ShapeDtypeStruct((seq, D_MODEL), F32), jax.ShapeDtypeStruct((D_MODEL, D_IN_PROJ), F32)],
        compiler_params=_cparams(("arbitrary",)),
    )(x, dq, dk, dv, dza, du_f, du_b, dylin, dzs, dpre, ssm_d, cos128, sin128, w_t)


def _adamw(w, g, m, v, name):
    rows, cols = w.shape
    tb = rows
    while tb * cols * 4 > ADAMW_BLOCK_BYTES and tb % 16 == 0:
        tb //= 2

    def body(w_ref, g_ref, m_ref, v_ref, d_ref, nm_ref, nv_ref):
        g_blk = g_ref[...]
        m_new = ADAM_B1 * m_ref[...] + (1.0 - ADAM_B1) * g_blk
        v_new = ADAM_B2 * v_ref[...] + (1.0 - ADAM_B2) * (g_blk * g_blk)
        m_hat = m_new / (1.0 - ADAM_B1 ** ADAM_STEP)
        v_hat = v_new / (1.0 - ADAM_B2 ** ADAM_STEP)
        d_ref[...] = -ADAM_LR * (m_hat / (jnp.sqrt(v_hat) + ADAM_EPS) + ADAM_WD * w_ref[...])
        nm_ref[...] = m_new
        nv_ref[...] = v_new

    spec = pl.BlockSpec((tb, cols), lambda i: (i, 0))
    return pl.pallas_call(
        body, name=name, grid=(rows // tb,), in_specs=[spec] * 4, out_specs=[spec] * 3,
        out_shape=[jax.ShapeDtypeStruct((rows, cols), F32)] * 3,
        compiler_params=_cparams(("arbitrary",)),
    )(w, g, m, v)


_SMALL = ["attn_sink", "ssm_a_re", "ssm_a_im", "ssm_log_dt", "ssm_b_re", "ssm_b_im", "ssm_c_re", "ssm_c_im", "ssm_d",
          "b_glu", "norm_attn_g", "norm_ssm_g", "ln_g", "ln_b"]
_WEIGHTS = ["w_in", "attn_sink", "ssm_a_re", "ssm_a_im", "ssm_log_dt", "ssm_b_re", "ssm_b_im", "ssm_c_re", "ssm_c_im",
            "ssm_d", "w_glu", "b_glu", "norm_attn_g", "norm_ssm_g", "w_out", "ln_g", "ln_b"]


def _pack_small(vals, rows):
    flat = jnp.concatenate([vals[n].reshape(-1) for n in _SMALL])
    return jnp.pad(flat, (0, rows * 128 - flat.shape[0])).reshape(rows, 128)


def _unpack_small(packed, shapes):
    flat = packed.reshape(-1)
    out, off = {}, 0
    for n in _SMALL:
        size = math.prod(shapes[n])
        out[n] = flat[off:off + size].reshape(shapes[n])
        off += size
    return out


def _block_diag_in(t):
    eye = jnp.eye(8, dtype=t.dtype)
    return jnp.einsum("dkgcp,gh->dkgchp", t.reshape(N_DIR, N_SLAB, 8, SSM_CH, SSM_STATE), eye).reshape(
        N_DIR, N_SLAB, SLAB_IN, SLAB_ST)


def _block_diag_in_extract(t):
    eye = jnp.eye(8, dtype=t.dtype)
    return jnp.einsum("dkgchp,gh->dkgcp", t.reshape(N_DIR, N_SLAB, 8, SSM_CH, 8, SSM_STATE), eye).reshape(
        N_DIR, N_GROUPS, SSM_CH, SSM_STATE)


def _block_diag_out_extract(t):
    eye = jnp.eye(8, dtype=t.dtype)
    return jnp.einsum("dkhpgc,gh->dkgcp", t.reshape(N_DIR, N_SLAB, 8, SSM_STATE, 8, SSM_CH), eye).reshape(
        N_DIR, N_GROUPS, SSM_CH, SSM_STATE)


def kernel(x, w_in, attn_sink, ssm_a_re, ssm_a_im, ssm_log_dt, ssm_b_re, ssm_b_im, ssm_c_re, ssm_c_im, ssm_d, w_glu, b_glu, norm_attn_g, norm_ssm_g, w_out, ln_g, ln_b, loss_target, m_w_in, m_attn_sink, m_ssm_a_re, m_ssm_a_im, m_ssm_log_dt, m_ssm_b_re, m_ssm_b_im, m_ssm_c_re, m_ssm_c_im, m_ssm_d, m_w_glu, m_b_glu, m_norm_attn_g, m_norm_ssm_g, m_w_out, m_ln_g, m_ln_b, v_w_in, v_attn_sink, v_ssm_a_re, v_ssm_a_im, v_ssm_log_dt, v_ssm_b_re, v_ssm_b_im, v_ssm_c_re, v_ssm_c_im, v_ssm_d, v_w_glu, v_b_glu, v_norm_attn_g, v_norm_ssm_g, v_w_out, v_ln_g, v_ln_b):
    args = dict(locals())
    weights = {n: args[n] for n in _WEIGHTS}
    mom_m = {n: args["m_" + n] for n in _WEIGHTS}
    mom_v = {n: args["v_" + n] for n in _WEIGHTS}
    xs = x[0]
    target = loss_target[0]

    w_in_g, w_glu_g, w_out_g = _all_gather_chips([w_in[0], w_glu[0], w_out[0]], BF16, "gather_weights")
    w_in_full = jnp.transpose(w_in_g, (1, 0, 2)).reshape(D_MODEL, D_IN_PROJ)
    w_glu_full = w_glu_g.reshape(D_SSM, D_SSM)
    w_out_full = w_out_g.reshape(D_MODEL, D_MODEL)

    loss_local, g_x, g_w_in, g_w_out, g_w_glu, local = _local_step(
        xs, target, w_in_full, w_glu_full, w_out_full, attn_sink, ssm_a_re, ssm_a_im, ssm_log_dt, ssm_b_re, ssm_b_im,
        ssm_c_re, ssm_c_im, ssm_d, b_glu, norm_attn_g, norm_ssm_g, ln_g, ln_b)
    loss = lax.psum(loss_local, ("x", "y", "c"))

    shapes = {n: weights[n].shape for n in _SMALL}
    n_small = sum(math.prod(s) for s in shapes.values())
    small_rows = -(-n_small // (128 * 64)) * 16
    g_small = _pack_small(local, N_CHIPS * small_rows).reshape(N_CHIPS, small_rows, 128)
    r_w_in = _reduce_scatter_chips(jnp.transpose(g_w_in.reshape(D_MODEL, N_CHIPS, -1), (1, 0, 2)), "reduce_w_in")
    r_w_out = _reduce_scatter_chips(g_w_out.reshape(N_CHIPS, -1, D_MODEL), "reduce_w_out")
    r_w_glu = _reduce_scatter_chips(g_w_glu.reshape(N_CHIPS, -1, D_SSM), "reduce_w_glu")
    r_small = _reduce_scatter_chips(g_small, "reduce_small")
    (g_small_all,) = _all_gather_chips([r_small], F32, "gather_small")
    g_small_all = g_small_all.reshape(N_CHIPS * small_rows, 128)

    grads, deltas, new_m, new_v = {}, {}, {}, {}
    for n, g in (("w_in", r_w_in), ("w_out", r_w_out), ("w_glu", r_w_glu)):
        d_w, m_w, v_w = _adamw(weights[n][0], g, mom_m[n][0], mom_v[n][0], "adamw_" + n)
        grads[n], deltas[n], new_m[n], new_v[n] = g[None], d_w[None], m_w[None], v_w[None]
    d_s, m_s, v_s = _adamw(_pack_small(weights, N_CHIPS * small_rows), g_small_all,
                           _pack_small(mom_m, N_CHIPS * small_rows), _pack_small(mom_v, N_CHIPS * small_rows),
                           "adamw_small")
    for dst, packed in ((grads, g_small_all), (deltas, d_s), (new_m, m_s), (new_v, v_s)):
        dst.update(_unpack_small(packed, shapes))

    return (loss, g_x[None], *[grads[n] for n in _WEIGHTS], *[deltas[n] for n in _WEIGHTS],
            *[new_m[n] for n in _WEIGHTS], *[new_v[n] for n in _WEIGHTS])


def _local_step(xs, target, w_in_full, w_glu_full, w_out_full, attn_sink, ssm_a_re, ssm_a_im, ssm_log_dt, ssm_b_re,
                ssm_b_im, ssm_c_re, ssm_c_im, ssm_d, b_glu, norm_attn_g, norm_ssm_g, ln_g, ln_b):
    seq = xs.shape[0]
    w_in_full = w_in_full[:, _COL_PERM]
    w_out_full = jnp.concatenate([w_out_full[:D_ATTN][_PAIR_PERM], w_out_full[D_ATTN:]], axis=0)

    n_dg = N_DIR * N_GROUPS
    a_r = ssm_a_re.reshape(n_dg, 1, SSM_STATE)
    a_i = ssm_a_im.reshape(n_dg, 1, SSM_STATE)
    log_dt = ssm_log_dt.reshape(n_dg, 1, 1)
    b_r = jnp.swapaxes(ssm_b_re[0], 2, 3).reshape(n_dg, SSM_CH, SSM_STATE)
    b_i = jnp.swapaxes(ssm_b_im[0], 2, 3).reshape(n_dg, SSM_CH, SSM_STATE)
    ssm_tb = min(256, seq)
    sub_len = ssm_tb // SUBSEG
    lam_r, lam_i, pow_r, pow_i, bbar_r, bbar_i = _ssm_params_fwd(a_r, a_i, log_dt, b_r, b_i, int(math.log2(sub_len)))
    per_dir = lambda t: t.reshape(N_DIR, 1, STATE_W)
    lam_r, lam_i, pow_r, pow_i = per_dir(lam_r), per_dir(lam_i), per_dir(pow_r), per_dir(pow_i)
    bb_r = _block_diag_in(bbar_r.reshape(N_DIR, N_GROUPS, SSM_CH, SSM_STATE)).astype(BF16)
    bb_i = _block_diag_in(bbar_i.reshape(N_DIR, N_GROUPS, SSM_CH, SSM_STATE)).astype(BF16)
    cb_t_r = _block_diag_in(ssm_c_re[0]).astype(BF16)
    cb_t_i = _block_diag_in(ssm_c_im[0]).astype(BF16)
    swap = lambda t: jnp.swapaxes(t, 2, 3)

    cos128, sin128 = _rope_tables(seq)
    tok_tb = min(512, seq)
    q_stack, k_rot, v_bf, z_attn, u, z_ssm = _proj(xs, w_in_full, cos128, sin128, tok_tb)
    sink_rows = jnp.repeat(attn_sink[0], ATT_TQ)[:, None]
    o = _attn_fwd(q_stack, k_rot, v_bf, sink_rows)
    ys, starts = [], []
    for d in range(N_DIR):
        y_d, s_r, s_i = _ssm_fwd(u, (lam_r[d], lam_i[d]), (pow_r[d], pow_i[d]), (bb_r[d], bb_i[d]),
                                 (swap(cb_t_r)[d], swap(cb_t_i)[d]), reverse=(d == 1), tb=ssm_tb, name=f"ssm_fwd_{d}")
        ys.append(y_d)
        starts.append((s_r, s_i))

    row = lambda t: t.reshape(1, -1)
    g_attn_p = row(norm_attn_g)[:, _PAIR_PERM]
    loss_blk, d_o, d_za, d_ylin, d_zs, d_pre, g_w_out, g_w_glu, g_vec = _mid(
        o, z_attn, u, ys[0], ys[1], z_ssm, xs, target, row(ssm_d), w_glu_full, w_glu_full.T, row(b_glu),
        g_attn_p, row(norm_ssm_g), w_out_full, w_out_full.T, row(ln_g), row(ln_b), min(256, seq))

    dq, dk, dv, d_sink_rows = _attn_bwd(q_stack, k_rot, v_bf, sink_rows, d_o)
    dus, g_bb, g_cb, g_lam = [], [], [], []
    for d in range(N_DIR):
        du_d, gbr, gbi, gcr, gci, dlr, dli = _ssm_bwd(
            u, d_ylin, starts[d], (lam_r[d], lam_i[d]), (pow_r[d], pow_i[d]), (bb_r[d], bb_i[d]),
            (swap(bb_r)[d], swap(bb_i)[d]), (cb_t_r[d], cb_t_i[d]), reverse=(d == 1), tb=ssm_tb, name=f"ssm_bwd_{d}")
        dus.append(du_d)
        g_bb.append((gbr, gbi))
        g_cb.append((gcr, gci))
        g_lam.append((dlr, dli))
    stack = lambda pairs, i: jnp.stack([p[i] for p in pairs])
    g_bbar_r = _block_diag_in_extract(stack(g_bb, 0)).reshape(n_dg, SSM_CH, SSM_STATE)
    g_bbar_i = _block_diag_in_extract(stack(g_bb, 1)).reshape(n_dg, SSM_CH, SSM_STATE)
    lam_part = lambda t: jnp.swapaxes(t.reshape(N_DIR, SUBSEG, N_GROUPS, SSM_STATE), 1, 2).reshape(n_dg, SUBSEG, SSM_STATE)
    g_ar, g_ai, g_dt, g_br, g_bi = _ssm_params_bwd(a_r, a_i, log_dt, b_r, b_i, lam_part(stack(g_lam, 0)),
                                                   lam_part(stack(g_lam, 1)), g_bbar_r, g_bbar_i)

    g_x, g_w_in = _proj_bwd(xs, dq, dk, dv, d_za, dus[0], dus[1], d_ylin, d_zs, d_pre, row(ssm_d), cos128, sin128,
                            w_in_full.T, tok_tb)

    local = {
        "attn_sink": d_sink_rows.reshape(N_Q_HEADS, ATT_TQ).sum(axis=1).reshape(attn_sink.shape),
        "ssm_a_re": g_ar.reshape(ssm_a_re.shape), "ssm_a_im": g_ai.reshape(ssm_a_im.shape),
        "ssm_log_dt": g_dt.reshape(ssm_log_dt.shape),
        "ssm_b_re": jnp.swapaxes(g_br.reshape(N_DIR, N_GROUPS, SSM_CH, SSM_STATE), 2, 3).reshape(ssm_b_re.shape),
        "ssm_b_im": jnp.swapaxes(g_bi.reshape(N_DIR, N_GROUPS, SSM_CH, SSM_STATE), 2, 3).reshape(ssm_b_im.shape),
        "ssm_c_re": _block_diag_out_extract(stack(g_cb, 0)).reshape(ssm_c_re.shape),
        "ssm_c_im": _block_diag_out_extract(stack(g_cb, 1)).reshape(ssm_c_im.shape),
        "ssm_d": g_vec[3:4, :D_SSM], "b_glu": g_vec[3:4, D_SSM:],
        "norm_attn_g": g_vec[2:3, :D_ATTN][:, _PAIR_INV], "norm_ssm_g": g_vec[2:3, D_ATTN:],
        "ln_g": g_vec[0:1], "ln_b": g_vec[1:2],
    }
    g_w_in = g_w_in[:, _COL_INV]
    g_w_out = jnp.concatenate([g_w_out[:D_ATTN][_PAIR_INV], g_w_out[D_ATTN:]], axis=0)
    return loss_blk[0, 0], g_x, g_w_in, g_w_out, g_w_glu, local
```

```python
import functools
import math

import numpy as np
import jax
import jax.numpy as jnp
from jax import lax
from jax.experimental import pallas as pl
from jax.experimental.pallas import tpu as pltpu

F32 = jnp.float32
BF16 = jnp.bfloat16
MESH = pl.DeviceIdType.MESH

D_MODEL = 1024
D_ATTN = 512
D_SSM = 512
HEAD_DIM = 64
N_Q_HEADS = 8
WINDOW = 128
ROPE_THETA = 10000.0
SSM_CH = 16
N_GROUPS = 32
SSM_STATE = 64
N_DIR = 2
STATE_W = N_GROUPS * SSM_STATE
N_SLAB = 4
SLAB_IN = 128
SLAB_ST = 512
NORM_EPS = 1e-5
NEG_INF = -1e30
ALPHA = 2.0 ** 0.25
D_IN_PROJ = 2304
N_CHIPS = 4

ADAM_LR = 0.001
ADAM_B1 = 0.9
ADAM_B2 = 0.999
ADAM_EPS = 1e-08
ADAM_WD = 0.01
ADAM_STEP = 10

SUBSEG = 8
SCAN_LANES = 512
VMEM_LIMIT = 48 * 1024 * 1024
ADAMW_BLOCK_BYTES = 3 * 512 * 1024

_PAIR_PERM = np.array([(64 * j + l) if l < 64 else (64 * (j + 4) + l - 64) for j in range(4) for l in range(128)])
_PAIR_INV = np.argsort(_PAIR_PERM)


def _cparams(sem=None):
    return pltpu.CompilerParams(dimension_semantics=sem, vmem_limit_bytes=VMEM_LIMIT)


def _dot(a, b):
    return jnp.dot(a, b, preferred_element_type=F32)


def _dot_nt(a, b):
    return lax.dot_general(a, b, (((1,), (1,)), ((), ())), preferred_element_type=F32)


def _dot_tn(a, b):
    return lax.dot_general(a, b, (((0,), (0,)), ((), ())), preferred_element_type=F32)


def _sigmoid(z):
    return 1.0 / (1.0 + jnp.exp(-z))


def _all_gather_chips(shards, out_dtype, name):
    n = len(shards)

    def body(*refs):
        in_refs, out_refs = refs[:n], refs[n:2 * n]
        send_sems, recv_sems = refs[2 * n:]
        x, y, c = lax.axis_index("x"), lax.axis_index("y"), lax.axis_index("c")
        sibling = (x, y, 1 - c)
        chips = [(1 - x, y), (x, 1 - y), (1 - x, 1 - y)]

        for a in range(n):
            out_refs[a][2 * x + y] = in_refs[a][...].astype(out_dtype)

        def half_of(a, px, py, half):
            rows = in_refs[a].shape[0] // 2
            return out_refs[a].at[2 * px + py, pl.ds(half * rows, rows), :]

        def copy(a, k, px, py, half, to):
            blk = half_of(a, px, py, half)
            return pltpu.make_async_remote_copy(src_ref=blk, dst_ref=blk, send_sem=send_sems.at[6 * a + k],
                                                recv_sem=recv_sems.at[6 * a + k], device_id=to, device_id_type=MESH)

        first = [copy(a, j, x, y, c, (*chips[j], c)) for a in range(n) for j in range(3)]
        for cp in first:
            cp.start()
        passed = []
        for a in range(n):
            for j in range(3):
                copy(a, j, *chips[j], c, (x, y, c)).wait_recv()
                fwd = copy(a, 3 + j, *chips[j], c, sibling)
                fwd.start()
                passed.append(fwd)
        for a in range(n):
            for j in range(3):
                copy(a, 3 + j, *chips[j], 1 - c, (x, y, c)).wait_recv()
        for cp in first + passed:
            cp.wait_send()

    vmem = pl.BlockSpec(memory_space=pltpu.VMEM)
    return pl.pallas_call(
        body, name=name,
        out_shape=[jax.ShapeDtypeStruct((N_CHIPS,) + s.shape, out_dtype) for s in shards],
        in_specs=[vmem] * n, out_specs=[vmem] * n,
        scratch_shapes=[pltpu.SemaphoreType.DMA((6 * n,)), pltpu.SemaphoreType.DMA((6 * n,))],
        compiler_params=pltpu.CompilerParams(vmem_limit_bytes=VMEM_LIMIT),
    )(*shards)


def _reduce_scatter_chips(pieces, name):
    _, rows, cols = pieces.shape
    h = rows // 2

    def body(p_ref, out_ref, a_ref, b_ref, send_sems, recv_sems):
        x, y, c = lax.axis_index("x"), lax.axis_index("y"), lax.axis_index("c")
        me = 2 * x + y
        sibling = (x, y, 1 - c)
        chips = [(1 - x, y), (x, 1 - y), (1 - x, 1 - y)]
        mine = pl.multiple_of(c * h, 8)
        other = pl.multiple_of((1 - c) * h, 8)

        swap = pltpu.make_async_remote_copy(src_ref=p_ref.at[:, pl.ds(other, h), :], dst_ref=a_ref,
                                            send_sem=send_sems.at[0], recv_sem=recv_sems.at[0],
                                            device_id=sibling, device_id_type=MESH)
        swap.start()
        swap.wait_recv()
        for k in range(N_CHIPS):
            a_ref[k] = a_ref[k] + p_ref[k, pl.ds(mine, h), :]

        def chip_copy(j, dst_slot):
            px, py = chips[j]
            return pltpu.make_async_remote_copy(src_ref=a_ref.at[2 * px + py], dst_ref=b_ref.at[dst_slot],
                                                send_sem=send_sems.at[1 + j], recv_sem=recv_sems.at[1 + j],
                                                device_id=(px, py, c), device_id_type=MESH)

        sends = [chip_copy(j, me) for j in range(3)]
        for cp in sends:
            cp.start()
        b_ref[me] = a_ref[me]
        for j in range(3):
            chip_copy(j, 2 * chips[j][0] + chips[j][1]).wait_recv()
        out_ref[pl.ds(mine, h), :] = (b_ref[0] + b_ref[1]) + (b_ref[2] + b_ref[3])

        back = pltpu.make_async_remote_copy(src_ref=out_ref.at[pl.ds(mine, h), :], dst_ref=out_ref.at[pl.ds(mine, h), :],
                                            send_sem=send_sems.at[4], recv_sem=recv_sems.at[4],
                                            device_id=sibling, device_id_type=MESH)
        back.start()
        pltpu.make_async_remote_copy(src_ref=out_ref.at[pl.ds(other, h), :], dst_ref=out_ref.at[pl.ds(other, h), :],
                                     send_sem=send_sems.at[4], recv_sem=recv_sems.at[4],
                                     device_id=sibling, device_id_type=MESH).wait_recv()
        swap.wait_send()
        for cp in sends:
            cp.wait_send()
        back.wait_send()

    vmem = pl.BlockSpec(memory_space=pltpu.VMEM)
    return pl.pallas_call(
        body, name=name,
        out_shape=jax.ShapeDtypeStruct((rows, cols), F32),
        in_specs=[vmem], out_specs=vmem,
        scratch_shapes=[pltpu.VMEM((N_CHIPS, h, cols), F32), pltpu.VMEM((N_CHIPS, h, cols), F32),
                        pltpu.SemaphoreType.DMA((5,)), pltpu.SemaphoreType.DMA((5,))],
        compiler_params=pltpu.CompilerParams(vmem_limit_bytes=VMEM_LIMIT),
    )(pieces)


def _ssm_param_values(ar, ai, logdt):
    dt = jnp.exp(logdt)
    mag = jnp.exp(dt * ar)
    cs, sn = jnp.cos(dt * ai), jnp.sin(dt * ai)
    lr, li = mag * cs, mag * sn
    den = ar * ar + ai * ai
    nr = (lr - 1.0) * ar + li * ai
    ni = li * ar - (lr - 1.0) * ai
    return dt, mag, lr, li, den, nr, ni


def _ssm_params_fwd(ar, ai, logdt, br, bi, n_square):
    def body(ar_ref, ai_ref, dt_ref, br_ref, bi_ref, lam_ref, bb_ref):
        _, _, lr, li, den, nr, ni = _ssm_param_values(ar_ref[...], ai_ref[...], dt_ref[...])
        lam_ref[0] = lr
        lam_ref[1] = li
        pr, pi = lr, li
        for _ in range(n_square):
            pr, pi = pr * pr - pi * pi, 2.0 * pr * pi
        lam_ref[2] = pr
        lam_ref[3] = pi
        fr, fi = nr / den, ni / den
        b_r, b_i = br_ref[...], bi_ref[...]
        bb_ref[0] = fr * b_r - fi * b_i
        bb_ref[1] = fr * b_i + fi * b_r

    return pl.pallas_call(body, name="ssm_params_fwd",
                          out_shape=[jax.ShapeDtypeStruct((4,) + ar.shape, F32),
                                     jax.ShapeDtypeStruct((2,) + br.shape, F32)])(ar, ai, logdt, br, bi)


def _ssm_params_bwd(ar, ai, logdt, br, bi, dlam_r, dlam_i, dbb_r, dbb_i):
    def body(ar_ref, ai_ref, dt_ref, br_ref, bi_ref, dlr_ref, dli_ref, dbr_ref, dbi_ref,
             gar_ref, gai_ref, gdt_ref, gbr_ref, gbi_ref):
        a_r, a_i = ar_ref[...], ai_ref[...]
        dt, mag, lr, li, den, nr, ni = _ssm_param_values(a_r, a_i, dt_ref[...])
        fr, fi = nr / den, ni / den
        b_r, b_i = br_ref[...], bi_ref[...]
        g_r, g_i = dbr_ref[...], dbi_ref[...]
        gbr_ref[...] = fr * g_r + fi * g_i
        gbi_ref[...] = fr * g_i - fi * g_r
        d_fr = jnp.sum(b_r * g_r + b_i * g_i, axis=1, keepdims=True)
        d_fi = jnp.sum(b_r * g_i - b_i * g_r, axis=1, keepdims=True)
        d_nr, d_ni = d_fr / den, d_fi / den
        d_den = -(d_fr * nr + d_fi * ni) / (den * den)
        d_lr = jnp.sum(dlr_ref[...], axis=1, keepdims=True) + d_nr * a_r - d_ni * a_i
        d_li = jnp.sum(dli_ref[...], axis=1, keepdims=True) + d_nr * a_i + d_ni * a_r
        d_ar = d_nr * (lr - 1.0) + d_ni * li + d_den * 2.0 * a_r
        d_ai = d_nr * li - d_ni * (lr - 1.0) + d_den * 2.0 * a_i
        d_mag = (d_lr * lr + d_li * li) / mag
        d_theta = d_li * lr - d_lr * li
        gar_ref[...] = d_ar + d_mag * mag * dt
        gai_ref[...] = d_ai + d_theta * dt
        d_dt = d_mag * mag * a_r + d_theta * a_i
        gdt_ref[...] = jnp.sum(d_dt, axis=2, keepdims=True) * dt

    small = jax.ShapeDtypeStruct(ar.shape, F32)
    return pl.pallas_call(
        body, name="ssm_params_bwd",
        out_shape=[small, small, jax.ShapeDtypeStruct(logdt.shape, F32),
                   jax.ShapeDtypeStruct(br.shape, F32), jax.ShapeDtypeStruct(br.shape, F32)],
    )(ar, ai, logdt, br, bi, dlam_r, dlam_i, dbb_r, dbb_i)


def _rope_tables(seq):
    half = HEAD_DIM // 2
    inv_freq = ROPE_THETA ** (-jnp.arange(half, dtype=F32) / half)
    ang = jnp.arange(seq, dtype=jnp.int32).astype(F32)[:, None] * inv_freq[None, :]
    cos, sin = jnp.cos(ang), jnp.sin(ang)
    cos128 = jnp.concatenate([cos, cos, cos, cos], axis=1)
    sin128 = jnp.concatenate([-sin, sin, -sin, sin], axis=1)
    return cos128, sin128


def _rotate_half_unsigned(t):
    lane = lax.broadcasted_iota(jnp.int32, t.shape, 1)
    return jnp.where((lane % HEAD_DIM) < HEAD_DIM // 2, pltpu.roll(t, 96, 1), pltpu.roll(t, 32, 1))


def _rope(t, cos, sin_signed):
    return t * cos + _rotate_half_unsigned(t) * sin_signed


def _pair_blocks(base):
    out = []
    for j in range(4):
        for g in range(2):
            nat = base + HEAD_DIM * (4 * g + j)
            par = base + 128 * j + HEAD_DIM * g
            out.append((slice(nat, nat + HEAD_DIM), slice(par, par + HEAD_DIM)))
    return out


W_Q, W_KV, W_ZA, W_U, W_ZS = 0, 512, 768, 1280, 1792


def _proj(x, wt, cos128, sin128, tb):
    seq = x.shape[0]

    def body(x_ref, wt_ref, cos_ref, sin_ref, q_ref, k_ref, v_ref, za_ref, u_ref, zs_ref, wp):
        @pl.when(pl.program_id(0) == 0)
        def _():
            for dst_base, src_base in ((0, W_Q), (512, W_ZA)):
                for nat, par in _pair_blocks(0):
                    wp[dst_base + par.start:dst_base + par.stop, :] = wt_ref[src_base + nat.start:src_base + nat.stop, :]

        xb = x_ref[...].astype(BF16)
        cos, sin = cos_ref[...], sin_ref[...]
        lo = lax.broadcasted_iota(jnp.int32, (tb, 128), 1) < HEAD_DIM
        q = _dot_nt(xb, wp[0:512, :])
        for j in range(4):
            qj = _rope(q[:, 128 * j:128 * (j + 1)], cos, sin)
            q_ref[j] = jnp.where(lo, qj, 0.0).astype(BF16)
            q_ref[4 + j] = jnp.where(lo, 0.0, qj).astype(BF16)
        kv = _dot_nt(xb, wt_ref[W_KV:W_ZA, :])
        k_ref[...] = _rope(kv[:, 0:128], cos, sin).astype(BF16)
        v_ref[...] = kv[:, 128:256].astype(BF16)
        za_ref[...] = _dot_nt(xb, wp[512:1024, :])
        u_val = _dot_nt(xb, wt_ref[W_U:W_ZS, :])
        for k in range(N_SLAB):
            u_ref[k] = u_val[:, k * SLAB_IN:(k + 1) * SLAB_IN]
        zs_ref[...] = _dot_nt(xb, wt_ref[W_ZS:D_IN_PROJ, :])

    row = lambda w: pl.BlockSpec((tb, w), lambda i: (i, 0))
    return pl.pallas_call(
        body, name="proj", grid=(seq // tb,),
        in_specs=[row(D_MODEL), pl.BlockSpec((D_IN_PROJ, D_MODEL), lambda i: (0, 0)), row(128), row(128)],
        out_specs=[pl.BlockSpec((8, tb, 128), lambda i: (0, i, 0)), row(128), row(128), row(512),
                   pl.BlockSpec((N_SLAB, tb, SLAB_IN), lambda i: (0, i, 0)), row(512)],
        out_shape=[jax.ShapeDtypeStruct((8, seq, 128), BF16), jax.ShapeDtypeStruct((seq, 128), BF16),
                   jax.ShapeDtypeStruct((seq, 128), BF16), jax.ShapeDtypeStruct((seq, 512), F32),
                   jax.ShapeDtypeStruct((N_SLAB, seq, SLAB_IN), F32), jax.ShapeDtypeStruct((seq, 512), F32)],
        scratch_shapes=[pltpu.VMEM((1024, D_MODEL), BF16)],
        compiler_params=_cparams(("arbitrary",)),
    )(x, wt, cos128, sin128)


ATT_TQ = 128
ATT_KEYS = 3 * ATT_TQ


def _attn_window(i, seq):
    start = jnp.clip(i * ATT_TQ - WINDOW, 0, seq - ATT_KEYS)
    return pl.multiple_of(start, ATT_TQ)


def _attn_probs(q_ref, k_ref, sink_ref, i, seq):
    start = _attn_window(i, seq)
    qall = q_ref[...].reshape(8 * ATT_TQ, 128)
    kw = k_ref[pl.ds(start, ATT_KEYS), :]
    s = _dot_nt(qall, kw) * (HEAD_DIM ** -0.5)
    qpos = i * ATT_TQ + lax.broadcasted_iota(jnp.int32, (8 * ATT_TQ, ATT_KEYS), 0) % ATT_TQ
    kpos = start + lax.broadcasted_iota(jnp.int32, (8 * ATT_TQ, ATT_KEYS), 1)
    s = jnp.where(jnp.abs(qpos - kpos) <= WINDOW, s, NEG_INF)
    sink = sink_ref[...]
    m = jnp.maximum(jnp.max(s, axis=1, keepdims=True), sink)
    p = jnp.exp(s - m)
    p_sink = jnp.exp(sink - m)
    denom = jnp.sum(p, axis=1, keepdims=True) + p_sink
    return start, qall, kw, p, denom, p_sink


def _attn_fwd(q_stack, k, v, sink_rows):
    seq = k.shape[0]

    def body(q_ref, k_ref, v_ref, sink_ref, o_ref):
        i = pl.program_id(0)
        start, _, _, p, denom, _ = _attn_probs(q_ref, k_ref, sink_ref, i, seq)
        vw = v_ref[pl.ds(start, ATT_KEYS), :]
        o_all = _dot(p.astype(BF16), vw) / denom
        lo = lax.broadcasted_iota(jnp.int32, (ATT_TQ, 128), 1) < HEAD_DIM
        for j in range(4):
            o_ref[:, 128 * j:128 * (j + 1)] = jnp.where(lo, o_all[j * ATT_TQ:(j + 1) * ATT_TQ],
                                                        o_all[(4 + j) * ATT_TQ:(5 + j) * ATT_TQ])

    full = lambda w: pl.BlockSpec((seq, w), lambda i: (0, 0))
    return pl.pallas_call(
        body, name="attn_fwd", grid=(seq // ATT_TQ,),
        in_specs=[pl.BlockSpec((8, ATT_TQ, 128), lambda i: (0, i, 0)), full(128), full(128),
                  pl.BlockSpec((8 * ATT_TQ, 1), lambda i: (0, 0))],
        out_specs=pl.BlockSpec((ATT_TQ, 512), lambda i: (i, 0)),
        out_shape=jax.ShapeDtypeStruct((seq, 512), F32),
        compiler_params=_cparams(("arbitrary",)),
    )(q_stack, k, v, sink_rows)


def _attn_bwd(q_stack, k, v, sink_rows, d_o):
    seq = k.shape[0]

    def body(q_ref, k_ref, v_ref, sink_ref, do_ref, dq_ref, dk_ref, dv_ref, dsink_ref, sink_acc):
        i = pl.program_id(0)

        @pl.when(i == 0)
        def _():
            dk_ref[...] = jnp.zeros_like(dk_ref)
            dv_ref[...] = jnp.zeros_like(dv_ref)
            sink_acc[...] = jnp.zeros_like(sink_acc)

        start, qall, kw, p, denom, p_sink = _attn_probs(q_ref, k_ref, sink_ref, i, seq)
        vw = v_ref[pl.ds(start, ATT_KEYS), :]
        lo = lax.broadcasted_iota(jnp.int32, (ATT_TQ, 128), 1) < HEAD_DIM
        d_o_blk = do_ref[...]
        parts = [jnp.where(lo, d_o_blk[:, 128 * j:128 * (j + 1)], 0.0) for j in range(4)]
        parts += [jnp.where(lo, 0.0, d_o_blk[:, 128 * j:128 * (j + 1)]) for j in range(4)]
        do_all = jnp.concatenate(parts, axis=0).astype(BF16)
        inv = 1.0 / denom
        probs = p * inv
        dp = _dot_nt(do_all, vw)
        delta = jnp.sum(probs * dp, axis=1, keepdims=True)
        ds = (probs * (dp - delta)).astype(BF16)
        sink_acc[...] += -(p_sink * inv) * delta
        scale = HEAD_DIM ** -0.5
        dq_all = _dot(ds, kw) * scale
        for j in range(4):
            dq_ref[:, 128 * j:128 * (j + 1)] = jnp.where(lo, dq_all[j * ATT_TQ:(j + 1) * ATT_TQ],
                                                         dq_all[(4 + j) * ATT_TQ:(5 + j) * ATT_TQ])
        dk_ref[pl.ds(start, ATT_KEYS), :] += _dot_tn(ds, qall) * scale
        dv_ref[pl.ds(start, ATT_KEYS), :] += _dot_tn(probs.astype(BF16), do_all)

        @pl.when(i == pl.num_programs(0) - 1)
        def _():
            for hd in range(N_Q_HEADS):
                total = jnp.sum(sink_acc[hd * ATT_TQ:(hd + 1) * ATT_TQ, :], axis=0, keepdims=True)
                dsink_ref[hd:hd + 1, :] = jnp.broadcast_to(total, (1, 128))

    full = lambda w: pl.BlockSpec((seq, w), lambda i: (0, 0))
    rows = pl.BlockSpec((8 * ATT_TQ, 1), lambda i: (0, 0))
    return pl.pallas_call(
        body, name="attn_bwd", grid=(seq // ATT_TQ,),
        in_specs=[pl.BlockSpec((8, ATT_TQ, 128), lambda i: (0, i, 0)), full(128), full(128), rows,
                  pl.BlockSpec((ATT_TQ, 512), lambda i: (i, 0))],
        out_specs=[pl.BlockSpec((ATT_TQ, 512), lambda i: (i, 0)), full(128), full(128),
                   pl.BlockSpec((N_Q_HEADS, 128), lambda i: (0, 0))],
        out_shape=[jax.ShapeDtypeStruct((seq, 512), F32), jax.ShapeDtypeStruct((seq, 128), F32),
                   jax.ShapeDtypeStruct((seq, 128), F32), jax.ShapeDtypeStruct((N_Q_HEADS, 128), F32)],
        scratch_shapes=[pltpu.VMEM((8 * ATT_TQ, 1), F32)],
        compiler_params=_cparams(("arbitrary",)),
    )(q_stack, k, v, sink_rows, d_o)


def _permute_rows(dst_ref, src_ref, sub_len):
    for k in range(N_SLAB):
        for j in range(sub_len):
            dst_ref[k, 8 * j:8 * (j + 1), :] = src_ref.at[k][pl.ds(j, SUBSEG, stride=sub_len), :]


def _unpermute_rows(dst_ref, src_ref, sub_len):
    for k in range(N_SLAB):
        for s in range(SUBSEG):
            dst_ref[k, s * sub_len:(s + 1) * sub_len, :] = src_ref.at[k][pl.ds(s, sub_len, stride=SUBSEG), :]


def _scan_pass(br_ref, bi_ref, lr_row, li_row, start, end_refs, *, sub_len, reverse, store):
    width = br_ref.shape[1]
    for c0 in range(0, width, SCAN_LANES):
        cols = slice(c0, c0 + SCAN_LANES)
        lr = jnp.broadcast_to(lr_row[:, cols], (SUBSEG, SCAN_LANES))
        li = jnp.broadcast_to(li_row[:, cols], (SUBSEG, SCAN_LANES))
        if start is None:
            init = (jnp.zeros((SUBSEG, SCAN_LANES), F32), jnp.zeros((SUBSEG, SCAN_LANES), F32))
        else:
            init = (start[0][:, cols], start[1][:, cols])

        def step(jj, state, cols=cols, lr=lr, li=li):
            sr, si = state
            j = (sub_len - 1 - jj) if reverse else jj
            r0 = pl.multiple_of(j * SUBSEG, SUBSEG)
            nr = lr * sr - li * si + br_ref[pl.ds(r0, SUBSEG), cols]
            ni = lr * si + li * sr + bi_ref[pl.ds(r0, SUBSEG), cols]
            if store:
                br_ref[pl.ds(r0, SUBSEG), cols] = nr
                bi_ref[pl.ds(r0, SUBSEG), cols] = ni
            return nr, ni

        sr, si = lax.fori_loop(0, sub_len, step, init, unroll=4)
        if end_refs is not None:
            end_refs[0][:, cols] = sr
            end_refs[1][:, cols] = si


def _resolve_starts(z_refs, carry_refs, start_refs, pr_row, pi_row, *, reverse):
    cr, ci = carry_refs[0][0:1, :], carry_refs[1][0:1, :]
    for s in (range(SUBSEG - 1, -1, -1) if reverse else range(SUBSEG)):
        start_refs[0][s:s + 1, :] = cr
        start_refs[1][s:s + 1, :] = ci
        zr, zi = z_refs[0][s:s + 1, :], z_refs[1][s:s + 1, :]
        cr, ci = pr_row * cr - pi_row * ci + zr, pr_row * ci + pi_row * cr + zi
    carry_refs[0][0:1, :] = cr
    carry_refs[1][0:1, :] = ci


def _param_specs(direction):
    row = lambda q: pl.BlockSpec((None, None, 1, STATE_W), lambda i: (q, direction, 0, 0))
    wide = lambda q: pl.BlockSpec((None, None, N_SLAB, SLAB_IN, SLAB_ST), lambda i: (q, direction, 0, 0, 0))
    tall = lambda q: pl.BlockSpec((None, None, N_SLAB, SLAB_ST, SLAB_IN), lambda i: (q, direction, 0, 0, 0))
    return [row(q) for q in range(4)], [wide(0), wide(1)], [tall(0), tall(1)]


def _ssm_fwd(u, lam, bb, cb, *, direction, tb, name):
    reverse = direction == 1
    seq = u.shape[1]
    nblk = seq // tb
    sub_len = tb // SUBSEG

    def body(u_ref, lr_ref, li_ref, pr_ref, pi_ref, bbr_ref, bbi_ref, cbr_ref, cbi_ref,
             y_ref, sr_ref, si_ref, xr, xi, up, yp, zr, zi, car, cai):
        @pl.when(pl.program_id(0) == 0)
        def _():
            car[...] = jnp.zeros_like(car)
            cai[...] = jnp.zeros_like(cai)

        _permute_rows(up, u_ref, sub_len)
        for k in range(N_SLAB):
            ub = up[k].astype(BF16)
            xr[:, k * SLAB_ST:(k + 1) * SLAB_ST] = _dot(ub, bbr_ref[k])
            xi[:, k * SLAB_ST:(k + 1) * SLAB_ST] = _dot(ub, bbi_ref[k])
        lr, li = lr_ref[...], li_ref[...]
        _scan_pass(xr, xi, lr, li, None, (zr, zi), sub_len=sub_len, reverse=reverse, store=False)
        _resolve_starts((zr, zi), (car, cai), (sr_ref, si_ref), pr_ref[...], pi_ref[...], reverse=reverse)
        _scan_pass(xr, xi, lr, li, (sr_ref, si_ref), None, sub_len=sub_len, reverse=reverse, store=True)
        for k in range(N_SLAB):
            st = slice(k * SLAB_ST, (k + 1) * SLAB_ST)
            yp[k] = _dot(xr[:, st].astype(BF16), cbr_ref[k]) - _dot(xi[:, st].astype(BF16), cbi_ref[k])
        _unpermute_rows(y_ref, yp, sub_len)

    blk = (lambda i: nblk - 1 - i) if reverse else (lambda i: i)
    rows, wide, tall = _param_specs(direction)
    tok = pl.BlockSpec((N_SLAB, tb, SLAB_IN), lambda i: (0, blk(i), 0))
    start_spec = pl.BlockSpec((None, SUBSEG, STATE_W), lambda i: (blk(i), 0, 0))
    return pl.pallas_call(
        body, name=name, grid=(nblk,),
        in_specs=[tok] + rows + wide + tall,
        out_specs=[tok, start_spec, start_spec],
        out_shape=[jax.ShapeDtypeStruct((N_SLAB, seq, SLAB_IN), F32), jax.ShapeDtypeStruct((nblk, SUBSEG, STATE_W), F32),
                   jax.ShapeDtypeStruct((nblk, SUBSEG, STATE_W), F32)],
        scratch_shapes=[pltpu.VMEM((tb, STATE_W), F32), pltpu.VMEM((tb, STATE_W), F32),
                        pltpu.VMEM((N_SLAB, tb, SLAB_IN), F32), pltpu.VMEM((N_SLAB, tb, SLAB_IN), F32),
                        pltpu.VMEM((SUBSEG, STATE_W), F32), pltpu.VMEM((SUBSEG, STATE_W), F32),
                        pltpu.VMEM((SUBSEG, STATE_W), F32), pltpu.VMEM((SUBSEG, STATE_W), F32)],
        compiler_params=_cparams(("arbitrary",)),
    )(u, lam, lam, lam, lam, bb, bb, cb, cb)


def _ssm_bwd(u, dy, starts, lam, bb, bbt, cb_t, *, direction, tb, name):
    reverse = direction == 1
    seq = u.shape[1]
    nblk = seq // tb
    sub_len = tb // SUBSEG

    def body(u_ref, dy_ref, sr_ref, si_ref, lr_ref, li_ref, pr_ref, pi_ref, bbr_ref, bbi_ref, btr_ref, bti_ref,
             ctr_ref, cti_ref, du_ref, gb_ref, gc_ref, dl_ref,
             xr, xi, gr, gi, up, dyp, dup, zr, zi, gsr, gsi, car, cai):
        gbr_ref, gbi_ref = gb_ref.at[0], gb_ref.at[1]
        gcr_ref, gci_ref = gc_ref.at[0], gc_ref.at[1]
        dlr_ref, dli_ref = dl_ref.at[0], dl_ref.at[1]

        @pl.when(pl.program_id(0) == 0)
        def _():
            for ref in (car, cai, gbr_ref, gbi_ref, gcr_ref, gci_ref, dlr_ref, dli_ref):
                ref[...] = jnp.zeros_like(ref)

        _permute_rows(up, u_ref, sub_len)
        _permute_rows(dyp, dy_ref, sub_len)
        lr, li = lr_ref[...], li_ref[...]
        for k in range(N_SLAB):
            st = slice(k * SLAB_ST, (k + 1) * SLAB_ST)
            ub = up[k].astype(BF16)
            xr[:, st] = _dot(ub, bbr_ref[k])
            xi[:, st] = _dot(ub, bbi_ref[k])
            dyb = dyp[k].astype(BF16)
            gr[:, st] = _dot(dyb, ctr_ref[k])
            gi[:, st] = -_dot(dyb, cti_ref[k])
        _scan_pass(xr, xi, lr, li, (sr_ref, si_ref), None, sub_len=sub_len, reverse=reverse, store=True)
        for k in range(N_SLAB):
            st = slice(k * SLAB_ST, (k + 1) * SLAB_ST)
            dyb = dyp[k].astype(BF16)
            gcr_ref[k] += _dot_tn(xr[:, st].astype(BF16), dyb)
            gci_ref[k] -= _dot_tn(xi[:, st].astype(BF16), dyb)
        nli = -li
        _scan_pass(gr, gi, lr, nli, None, (zr, zi), sub_len=sub_len, reverse=not reverse, store=False)
        _resolve_starts((zr, zi), (car, cai), (gsr, gsi), pr_ref[...], -pi_ref[...], reverse=not reverse)
        _scan_pass(gr, gi, lr, nli, (gsr, gsi), None, sub_len=sub_len, reverse=not reverse, store=True)
        for k in range(N_SLAB):
            st = slice(k * SLAB_ST, (k + 1) * SLAB_ST)
            ub = up[k].astype(BF16)
            grb, gib = gr[:, st].astype(BF16), gi[:, st].astype(BF16)
            gbr_ref[k] += _dot_tn(ub, grb)
            gbi_ref[k] += _dot_tn(ub, gib)
            dup[k] = _dot(grb, btr_ref[k]) + _dot(gib, bti_ref[k])
        _unpermute_rows(du_ref, dup, sub_len)

        for c0 in range(0, STATE_W, SCAN_LANES):
            cols = slice(c0, c0 + SCAN_LANES)
            edge = (sub_len - 1) * SUBSEG if reverse else 0
            g_r, g_i = gr[edge:edge + SUBSEG, cols], gi[edge:edge + SUBSEG, cols]
            x_r, x_i = sr_ref[:, cols], si_ref[:, cols]
            acc = (dlr_ref[:, cols] + (g_r * x_r + g_i * x_i), dli_ref[:, cols] + (g_i * x_r - g_r * x_i))

            def step(jj, acc, cols=cols):
                r_g = pl.multiple_of((jj if reverse else jj + 1) * SUBSEG, SUBSEG)
                r_x = pl.multiple_of((jj + 1 if reverse else jj) * SUBSEG, SUBSEG)
                g_r, g_i = gr[pl.ds(r_g, SUBSEG), cols], gi[pl.ds(r_g, SUBSEG), cols]
                x_r, x_i = xr[pl.ds(r_x, SUBSEG), cols], xi[pl.ds(r_x, SUBSEG), cols]
                return acc[0] + (g_r * x_r + g_i * x_i), acc[1] + (g_i * x_r - g_r * x_i)

            acc = lax.fori_loop(0, sub_len - 1, step, acc, unroll=4)
            dlr_ref[:, cols] = acc[0]
            dli_ref[:, cols] = acc[1]

    blk = (lambda i: i) if reverse else (lambda i: nblk - 1 - i)
    rows, wide, tall = _param_specs(direction)
    tok = pl.BlockSpec((N_SLAB, tb, SLAB_IN), lambda i: (0, blk(i), 0))
    start_spec = pl.BlockSpec((None, SUBSEG, STATE_W), lambda i: (blk(i), 0, 0))
    gb_shape, gc_shape, dl_shape = (2, N_SLAB, SLAB_IN, SLAB_ST), (2, N_SLAB, SLAB_ST, SLAB_IN), (2, SUBSEG, STATE_W)
    whole = lambda shape: pl.BlockSpec(shape, lambda i: (0,) * len(shape))
    big = lambda: pltpu.VMEM((tb, STATE_W), F32)
    slabs = lambda: pltpu.VMEM((N_SLAB, tb, SLAB_IN), F32)
    tile = lambda: pltpu.VMEM((SUBSEG, STATE_W), F32)
    return pl.pallas_call(
        body, name=name, grid=(nblk,),
        in_specs=[tok, tok, start_spec, start_spec] + rows + wide + tall + wide,
        out_specs=[tok, whole(gb_shape), whole(gc_shape), whole(dl_shape)],
        out_shape=[jax.ShapeDtypeStruct((N_SLAB, seq, SLAB_IN), F32), jax.ShapeDtypeStruct(gb_shape, F32),
                   jax.ShapeDtypeStruct(gc_shape, F32), jax.ShapeDtypeStruct(dl_shape, F32)],
        scratch_shapes=[big(), big(), big(), big(), slabs(), slabs(), slabs(),
                        tile(), tile(), tile(), tile(), tile(), tile()],
        compiler_params=_cparams(("arbitrary",)),
    )(u, dy, *starts, lam, lam, lam, lam, bb, bb, bbt, bbt, cb_t, cb_t)


GELU_C = math.sqrt(2.0 / math.pi)
GELU_K = 0.044715


def _mid(o, za, u, y_f, y_b, zs, x, target, ssm_d, w_glu, b_glu, g_attn, g_ssm, w_out, ln_g, ln_b, tb):
    seq = x.shape[0]

    def body(o_ref, za_ref, u_ref, yf_ref, yb_ref, zs_ref, x_ref, t_ref, d_ref, wg_ref, bg_ref, ga_ref, gs_ref,
             wo_ref, lg_ref, lb_ref,
             loss_ref, do_ref, dza_ref, dyl_ref, dzs_ref, dpre_ref, gwo_ref, gwg_ref, vec_ref, wop):
        @pl.when(pl.program_id(0) == 0)
        def _():
            for ref in (loss_ref, gwo_ref, gwg_ref, vec_ref):
                ref[...] = jnp.zeros_like(ref)
            for nat, par in _pair_blocks(0):
                wop[par, :] = wo_ref[nat, :]
            wop[D_ATTN:, :] = wo_ref[D_ATTN:, :]

        o, za = o_ref[...], za_ref[...]
        sig_a = _sigmoid(za)
        silu_a = za * sig_a
        ya = o * silu_a
        r_a = lax.rsqrt(jnp.mean(ya * ya, axis=1, keepdims=True) + NORM_EPS)
        n_a = ya * r_a
        g_a = ga_ref[...]
        unslab = lambda ref: jnp.concatenate([ref[k] for k in range(N_SLAB)], axis=1)
        u_blk, zs = unslab(u_ref), zs_ref[...]
        d_row = d_ref[...]
        ylin = d_row * u_blk + unslab(yf_ref) + unslab(yb_ref)
        inner = GELU_C * (ylin + GELU_K * ylin * ylin * ylin)
        th = jnp.tanh(inner)
        gl = 0.5 * ylin * (1.0 + th)
        glb = gl.astype(BF16)
        sg = _sigmoid(_dot(glb, wg_ref[...]) + bg_ref[...])
        y2 = gl * sg
        sig_s = _sigmoid(zs)
        silu_s = zs * sig_s
        ys = y2 * silu_s
        r_s = lax.rsqrt(jnp.mean(ys * ys, axis=1, keepdims=True) + NORM_EPS)
        n_s = ys * r_s
        g_s = gs_ref[...]
        mixed = jnp.concatenate([n_a * g_a, n_s * g_s], axis=1).astype(BF16)
        pre = ALPHA * x_ref[...] + _dot(mixed, wop[...])
        mu = jnp.mean(pre, axis=1, keepdims=True)
        cen = pre - mu
        rstd = lax.rsqrt(jnp.mean(cen * cen, axis=1, keepdims=True) + NORM_EPS)
        hhat = cen * rstd
        ln_g = lg_ref[...]
        err = hhat * ln_g + lb_ref[...] - t_ref[...]
        loss_ref[...] += 0.5 * jnp.sum(jnp.mean(err * err, axis=1, keepdims=True))

        dh = err * (1.0 / D_MODEL)
        vec_ref[0:1, :] += jnp.sum(dh * hhat, axis=0, keepdims=True)
        vec_ref[1:2, :] += jnp.sum(dh, axis=0, keepdims=True)
        dhh = dh * ln_g
        dpre = rstd * (dhh - jnp.mean(dhh, axis=1, keepdims=True) - hhat * jnp.mean(dhh * hhat, axis=1, keepdims=True))
        dpre_ref[...] = dpre
        dpb = dpre.astype(BF16)
        for j in range(4):
            g_pair = _dot_tn(mixed[:, 128 * j:128 * (j + 1)], dpb)
            for g in range(2):
                nat = HEAD_DIM * (4 * g + j)
                gwo_ref[nat:nat + HEAD_DIM, :] += g_pair[HEAD_DIM * g:HEAD_DIM * (g + 1), :]
        gwo_ref[D_ATTN:, :] += _dot_tn(mixed[:, D_ATTN:], dpb)
        dmix = _dot_nt(dpb, wop[...])
        dna = dmix[:, :D_ATTN]
        vec_ref[2:3, 0:D_ATTN] += jnp.sum(dna * n_a, axis=0, keepdims=True)
        dna = dna * g_a
        dya = r_a * (dna - n_a * jnp.mean(dna * n_a, axis=1, keepdims=True))
        do_ref[...] = dya * silu_a
        dza_ref[...] = dya * o * (sig_a * (1.0 + za * (1.0 - sig_a)))
        dns = dmix[:, D_ATTN:]
        vec_ref[2:3, D_ATTN:] += jnp.sum(dns * n_s, axis=0, keepdims=True)
        dns = dns * g_s
        dys = r_s * (dns - n_s * jnp.mean(dns * n_s, axis=1, keepdims=True))
        dzs_ref[...] = dys * y2 * (sig_s * (1.0 + zs * (1.0 - sig_s)))
        dy2 = dys * silu_s
        da = dy2 * gl * sg * (1.0 - sg)
        vec_ref[3:4, D_SSM:] += jnp.sum(da, axis=0, keepdims=True)
        dab = da.astype(BF16)
        gwg_ref[...] += _dot_tn(glb, dab)
        dgl = dy2 * sg + _dot_nt(dab, wg_ref[...])
        dylin = dgl * (0.5 * (1.0 + th) + 0.5 * ylin * (1.0 - th * th) * GELU_C * (1.0 + 3.0 * GELU_K * ylin * ylin))
        for k in range(N_SLAB):
            dyl_ref[k] = dylin[:, k * SLAB_IN:(k + 1) * SLAB_IN]
        vec_ref[3:4, 0:D_SSM] += jnp.sum(dylin * u_blk, axis=0, keepdims=True)

    tok = lambda w: pl.BlockSpec((tb, w), lambda i: (i, 0))
    slab = pl.BlockSpec((N_SLAB, tb, SLAB_IN), lambda i: (0, i, 0))
    const = lambda r, c: pl.BlockSpec((r, c), lambda i: (0, 0))
    tok_shape = jax.ShapeDtypeStruct((seq, 512), F32)
    return pl.pallas_call(
        body, name="mid", grid=(seq // tb,),
        in_specs=[tok(512), tok(512), slab, slab, slab, tok(512), tok(1024), tok(1024),
                  const(1, 512), const(512, 512), const(1, 512), const(1, 512), const(1, 512),
                  const(1024, 1024), const(1, 1024), const(1, 1024)],
        out_specs=[const(8, 128), tok(512), tok(512), slab, tok(512), tok(1024),
                   const(1024, 1024), const(512, 512), const(8, 1024)],
        out_shape=[jax.ShapeDtypeStruct((8, 128), F32), tok_shape, tok_shape,
                   jax.ShapeDtypeStruct((N_SLAB, seq, SLAB_IN), F32), tok_shape,
                   jax.ShapeDtypeStruct((seq, 1024), F32), jax.ShapeDtypeStruct((1024, 1024), F32),
                   jax.ShapeDtypeStruct((512, 512), F32), jax.ShapeDtypeStruct((8, 1024), F32)],
        scratch_shapes=[pltpu.VMEM((D_MODEL, D_MODEL), BF16)],
        compiler_params=_cparams(("arbitrary",)),
    )(o, za, u, y_f, y_b, zs, x, target, ssm_d, w_glu, b_glu, g_attn, g_ssm, w_out, ln_g, ln_b)


def _proj_bwd(x, dq, dk, dv, dza, du_f, du_b, dylin, dzs, dpre, ssm_d, cos128, sin128, wt, tb):
    seq = x.shape[0]

    def body(x_ref, dq_ref, dk_ref, dv_ref, dza_ref, duf_ref, dub_ref, dyl_ref, dzs_ref, dpre_ref, d_ref,
             cos_ref, sin_ref, wt_ref, gx_ref, gw_ref, wp):
        @pl.when(pl.program_id(0) == 0)
        def _():
            gw_ref[...] = jnp.zeros_like(gw_ref)
            for base in (W_Q, W_ZA):
                for nat, par in _pair_blocks(base):
                    wp[par, :] = wt_ref[nat, :]
            wp[W_KV:W_ZA, :] = wt_ref[W_KV:W_ZA, :]
            wp[W_U:, :] = wt_ref[W_U:, :]

        cos, sin = cos_ref[...], sin_ref[...]

        def unrope(t):
            return t * cos + _rotate_half_unsigned(t * sin)

        dq_rot = dq_ref[...]
        pieces = [unrope(dq_rot[:, 128 * j:128 * (j + 1)]) for j in range(4)]
        d_row = d_ref[...]
        pieces += [unrope(dk_ref[...]), dv_ref[...], dza_ref[...]]
        pieces += [duf_ref[k] + dub_ref[k] + d_row[:, k * SLAB_IN:(k + 1) * SLAB_IN] * dyl_ref[k] for k in range(N_SLAB)]
        pieces += [dzs_ref[...]]
        dproj = jnp.concatenate(pieces, axis=1).astype(BF16)
        gx_ref[...] = ALPHA * dpre_ref[...] + _dot(dproj, wp[...])
        xb = x_ref[...].astype(BF16)
        for base in (W_Q, W_ZA):
            for j in range(4):
                g_pair = _dot_tn(dproj[:, base + 128 * j:base + 128 * (j + 1)], xb)
                for g in range(2):
                    nat = base + HEAD_DIM * (4 * g + j)
                    gw_ref[nat:nat + HEAD_DIM, :] += g_pair[HEAD_DIM * g:HEAD_DIM * (g + 1), :]
        gw_ref[W_KV:W_ZA, :] += _dot_tn(dproj[:, W_KV:W_ZA], xb)
        gw_ref[W_U:, :] += _dot_tn(dproj[:, W_U:], xb)

    tok = lambda w: pl.BlockSpec((tb, w), lambda i: (i, 0))
    slab = pl.BlockSpec((N_SLAB, tb, SLAB_IN), lambda i: (0, i, 0))
    const = lambda r, c: pl.BlockSpec((r, c), lambda i: (0, 0))
    return pl.pallas_call(
        body, name="proj_bwd", grid=(seq // tb,),
        in_specs=[tok(1024), tok(512), tok(128), tok(128), tok(512), slab, slab, slab, tok(512), tok(1024),
                  const(1, 512), tok(128), tok(128), const(D_IN_PROJ, D_MODEL)],
        out_specs=[tok(1024), const(D_IN_PROJ, D_MODEL)],
        out_shape=[jax.ShapeDtypeStruct((seq, D_MODEL), F32), jax.ShapeDtypeStruct((D_IN_PROJ, D_MODEL), F32)],
        scratch_shapes=[pltpu.VMEM((D_IN_PROJ, D_MODEL), BF16)],
        compiler_params=_cparams(("arbitrary",)),
    )(x, dq, dk, dv, dza, du_f, du_b, dylin, dzs, dpre, ssm_d, cos128, sin128, wt)


def _adamw(w, g, m, v, name):
    rows, cols = w.shape
    tb = rows
    while tb * cols * 4 > ADAMW_BLOCK_BYTES and tb % 16 == 0:
        tb //= 2

    def body(w_ref, g_ref, m_ref, v_ref, d_ref, nm_ref, nv_ref):
        _adamw_update(w_ref, g_ref, m_ref, v_ref, d_ref, nm_ref, nv_ref)

    spec = pl.BlockSpec((tb, cols), lambda i: (i, 0))
    return pl.pallas_call(
        body, name=name, grid=(rows // tb,), in_specs=[spec] * 4, out_specs=[spec] * 3,
        out_shape=[jax.ShapeDtypeStruct((rows, cols), F32)] * 3,
        compiler_params=_cparams(("arbitrary",)),
    )(w, g, m, v)


def _adamw_update(w_ref, g_ref, m_ref, v_ref, d_ref, nm_ref, nv_ref):
    g_blk = g_ref[...]
    m_new = ADAM_B1 * m_ref[...] + (1.0 - ADAM_B1) * g_blk
    v_new = ADAM_B2 * v_ref[...] + (1.0 - ADAM_B2) * (g_blk * g_blk)
    m_hat = m_new / (1.0 - ADAM_B1 ** ADAM_STEP)
    v_hat = v_new / (1.0 - ADAM_B2 ** ADAM_STEP)
    d_ref[...] = -ADAM_LR * (m_hat / (jnp.sqrt(v_hat) + ADAM_EPS) + ADAM_WD * w_ref[...])
    nm_ref[...] = m_new
    nv_ref[...] = v_new


def _adamw_many(groups, name):
    n = len(groups)

    def body(*refs):
        for p in range(n):
            _adamw_update(*refs[4 * p:4 * p + 4], *refs[4 * n + 3 * p:4 * n + 3 * p + 3])

    return pl.pallas_call(
        body, name=name,
        out_shape=[jax.ShapeDtypeStruct(grp[0].shape, F32) for grp in groups for _ in range(3)],
    )(*[a for grp in groups for a in grp])


_WEIGHTS = ["w_in", "attn_sink", "ssm_a_re", "ssm_a_im", "ssm_log_dt", "ssm_b_re", "ssm_b_im", "ssm_c_re", "ssm_c_im",
            "ssm_d", "w_glu", "b_glu", "norm_attn_g", "norm_ssm_g", "w_out", "ln_g", "ln_b"]
N_DG = N_DIR * N_GROUPS
BIG_ROWS = N_DG * SSM_CH * SSM_STATE // 128
TINY_ROWS = 48


def _slabs_wide(t):
    eye = jnp.eye(8, dtype=t.dtype)
    return jnp.einsum("rdkgcp,gh->rdkgchp", t.reshape(2, N_DIR, N_SLAB, 8, SSM_CH, SSM_STATE), eye).reshape(
        2, N_DIR, N_SLAB, SLAB_IN, SLAB_ST)


def _slabs_tall(t):
    eye = jnp.eye(8, dtype=t.dtype)
    return jnp.einsum("rdkgcp,gh->rdkhpgc", t.reshape(2, N_DIR, N_SLAB, 8, SSM_CH, SSM_STATE), eye).reshape(
        2, N_DIR, N_SLAB, SLAB_ST, SLAB_IN)


def _wide_diagonal(t):
    eye = jnp.eye(8, dtype=t.dtype)
    return jnp.einsum("rdkgchp,gh->rdkgcp", t.reshape(2, N_DIR, N_SLAB, 8, SSM_CH, 8, SSM_STATE), eye).reshape(
        2, N_DIR, N_GROUPS, SSM_CH, SSM_STATE)


def _tall_diagonal(t):
    eye = jnp.eye(8, dtype=t.dtype)
    return jnp.einsum("rdkhpgc,gh->rdkgcp", t.reshape(2, N_DIR, N_SLAB, 8, SSM_STATE, 8, SSM_CH), eye).reshape(
        2, N_DIR, N_GROUPS, SSM_CH, SSM_STATE)


def _pack_small_grads(g_b, g_c, g_vec, g_ar, g_ai, g_dt, g_sink):
    big = jnp.concatenate([g_b.reshape(2, BIG_ROWS, 128), g_c.reshape(2, BIG_ROWS, 128)], axis=0)
    row = lambda t: jnp.pad(t.reshape(1, -1), ((0, 0), (0, 128 - t.size)))
    tiny = jnp.concatenate([g_vec.reshape(64, 128), g_ar.reshape(32, 128), g_ai.reshape(32, 128), row(g_dt), row(g_sink),
                            jnp.zeros((N_CHIPS * TINY_ROWS - 130, 128), F32)], axis=0)
    return jnp.concatenate([big, tiny.reshape(N_CHIPS, TINY_ROWS, 128)], axis=1)


def _unpack_small_grads(packed):
    big = packed[:, :BIG_ROWS].reshape(N_CHIPS, 2 * BIG_ROWS, SSM_STATE)
    tiny = packed[:, BIG_ROWS:].reshape(N_CHIPS * TINY_ROWS, 128)
    g_vec = tiny[0:64].reshape(8, 1024)
    return {
        "ssm_b_re": big[0], "ssm_b_im": big[1], "ssm_c_re": big[2], "ssm_c_im": big[3],
        "ln_g": g_vec[0:1], "ln_b": g_vec[1:2],
        "norm_attn_g": g_vec[2:3, :D_ATTN][:, _PAIR_INV], "norm_ssm_g": g_vec[2:3, D_ATTN:],
        "ssm_d": g_vec[3:4, :D_SSM], "b_glu": g_vec[3:4, D_SSM:],
        "ssm_a_re": tiny[64:96].reshape(N_DG, SSM_STATE), "ssm_a_im": tiny[96:128].reshape(N_DG, SSM_STATE),
        "ssm_log_dt": tiny[128:129, :N_DG].reshape(N_DIR, N_GROUPS), "attn_sink": tiny[129:130, :N_Q_HEADS],
    }


def _small_view(name, t):
    if name in ("ssm_b_re", "ssm_b_im"):
        return jnp.swapaxes(t[0], 2, 3).reshape(N_DG * SSM_CH, SSM_STATE)
    if name in ("ssm_c_re", "ssm_c_im"):
        return t.reshape(N_DG * SSM_CH, SSM_STATE)
    if name in ("ssm_a_re", "ssm_a_im"):
        return t.reshape(N_DG, SSM_STATE)
    if name == "ssm_log_dt":
        return t.reshape(N_DIR, N_GROUPS)
    return t.reshape(1, -1)


def _small_unview(name, t, shape):
    if name in ("ssm_b_re", "ssm_b_im"):
        return jnp.swapaxes(t.reshape(N_DIR, N_GROUPS, SSM_CH, SSM_STATE), 2, 3).reshape(shape)
    return t.reshape(shape)


def kernel(x, w_in, attn_sink, ssm_a_re, ssm_a_im, ssm_log_dt, ssm_b_re, ssm_b_im, ssm_c_re, ssm_c_im, ssm_d, w_glu, b_glu, norm_attn_g, norm_ssm_g, w_out, ln_g, ln_b, loss_target, m_w_in, m_attn_sink, m_ssm_a_re, m_ssm_a_im, m_ssm_log_dt, m_ssm_b_re, m_ssm_b_im, m_ssm_c_re, m_ssm_c_im, m_ssm_d, m_w_glu, m_b_glu, m_norm_attn_g, m_norm_ssm_g, m_w_out, m_ln_g, m_ln_b, v_w_in, v_attn_sink, v_ssm_a_re, v_ssm_a_im, v_ssm_log_dt, v_ssm_b_re, v_ssm_b_im, v_ssm_c_re, v_ssm_c_im, v_ssm_d, v_w_glu, v_b_glu, v_norm_attn_g, v_norm_ssm_g, v_w_out, v_ln_g, v_ln_b):
    args = dict(locals())
    weights = {n: args[n] for n in _WEIGHTS}
    mom_m = {n: args["m_" + n] for n in _WEIGHTS}
    mom_v = {n: args["v_" + n] for n in _WEIGHTS}
    xs = x[0]
    target = loss_target[0]

    wt_g, w_glu_g, w_out_g = _all_gather_chips([w_in[0].T, w_glu[0], w_out[0]], BF16, "gather_weights")
    wt_full = wt_g.reshape(D_IN_PROJ, D_MODEL)
    w_glu_full = w_glu_g.reshape(D_SSM, D_SSM)
    w_out_full = w_out_g.reshape(D_MODEL, D_MODEL)

    loss_local, g_x, g_wt, g_w_out, g_w_glu, g_small = _local_step(
        xs, target, wt_full, w_glu_full, w_out_full, attn_sink, ssm_a_re, ssm_a_im, ssm_log_dt, ssm_b_re, ssm_b_im,
        ssm_c_re, ssm_c_im, ssm_d, b_glu, norm_attn_g, norm_ssm_g, ln_g, ln_b)
    loss = lax.psum(loss_local, ("x", "y", "c"))

    r_wt = _reduce_scatter_chips(g_wt.reshape(N_CHIPS, -1, D_MODEL), "reduce_w_in")
    r_w_out = _reduce_scatter_chips(g_w_out.reshape(N_CHIPS, -1, D_MODEL), "reduce_w_out")
    r_w_glu = _reduce_scatter_chips(g_w_glu.reshape(N_CHIPS, -1, D_SSM), "reduce_w_glu")
    r_small = _reduce_scatter_chips(g_small, "reduce_small")
    (g_small_all,) = _all_gather_chips([r_small], F32, "gather_small")
    small_grads = _unpack_small_grads(g_small_all)

    grads, deltas, new_m, new_v = {}, {}, {}, {}
    d_w, m_w, v_w = _adamw(w_in[0].T, r_wt, m_w_in[0].T, v_w_in[0].T, "adamw_w_in")
    grads["w_in"], deltas["w_in"], new_m["w_in"], new_v["w_in"] = r_wt.T[None], d_w.T[None], m_w.T[None], v_w.T[None]
    for n, g in (("w_out", r_w_out), ("w_glu", r_w_glu)):
        d_w, m_w, v_w = _adamw(weights[n][0], g, mom_m[n][0], mom_v[n][0], "adamw_" + n)
        grads[n], deltas[n], new_m[n], new_v[n] = g[None], d_w[None], m_w[None], v_w[None]
    names = sorted(small_grads)
    updates = _adamw_many([(_small_view(n, weights[n]), small_grads[n], _small_view(n, mom_m[n]), _small_view(n, mom_v[n]))
                           for n in names], "adamw_small")
    for i, n in enumerate(names):
        shape = weights[n].shape
        grads[n] = _small_unview(n, small_grads[n], shape)
        deltas[n], new_m[n], new_v[n] = (_small_unview(n, t, shape) for t in updates[3 * i:3 * i + 3])

    return (loss, g_x[None], *[grads[n] for n in _WEIGHTS], *[deltas[n] for n in _WEIGHTS],
            *[new_m[n] for n in _WEIGHTS], *[new_v[n] for n in _WEIGHTS])


def _local_step(xs, target, wt_full, w_glu_full, w_out_full, attn_sink, ssm_a_re, ssm_a_im, ssm_log_dt, ssm_b_re,
                ssm_b_im, ssm_c_re, ssm_c_im, ssm_d, b_glu, norm_attn_g, norm_ssm_g, ln_g, ln_b):
    seq = xs.shape[0]

    a_r = ssm_a_re.reshape(N_DG, 1, SSM_STATE)
    a_i = ssm_a_im.reshape(N_DG, 1, SSM_STATE)
    log_dt = ssm_log_dt.reshape(N_DG, 1, 1)
    b_r = jnp.swapaxes(ssm_b_re[0], 2, 3).reshape(N_DG, SSM_CH, SSM_STATE)
    b_i = jnp.swapaxes(ssm_b_im[0], 2, 3).reshape(N_DG, SSM_CH, SSM_STATE)
    ssm_tb = min(256, seq)
    sub_len = ssm_tb // SUBSEG
    lam, bbar = _ssm_params_fwd(a_r, a_i, log_dt, b_r, b_i, int(math.log2(sub_len)))
    lam = lam.reshape(4, N_DIR, 1, STATE_W)
    bbar = bbar.astype(BF16).reshape(2, N_DIR, N_GROUPS, SSM_CH, SSM_STATE)
    c_both = jnp.stack([ssm_c_re[0], ssm_c_im[0]]).astype(BF16)
    bb, bbt = _slabs_wide(bbar), _slabs_tall(bbar)
    cb, cb_t = _slabs_tall(c_both), _slabs_wide(c_both)

    cos128, sin128 = _rope_tables(seq)
    q_stack, k_rot, v_bf, z_attn, u, z_ssm = _proj(xs, wt_full, cos128, sin128, min(512, seq))
    sink_rows = jnp.repeat(attn_sink[0], ATT_TQ)[:, None]
    o = _attn_fwd(q_stack, k_rot, v_bf, sink_rows)
    ys, starts = [], []
    for d in range(N_DIR):
        y_d, s_r, s_i = _ssm_fwd(u, lam, bb, cb, direction=d, tb=ssm_tb, name=f"ssm_fwd_{d}")
        ys.append(y_d)
        starts.append((s_r, s_i))

    row = lambda t: t.reshape(1, -1)
    g_attn_p = row(norm_attn_g)[:, _PAIR_PERM]
    loss_blk, d_o, d_za, d_ylin, d_zs, d_pre, g_w_out, g_w_glu, g_vec = _mid(
        o, z_attn, u, ys[0], ys[1], z_ssm, xs, target, row(ssm_d), w_glu_full, row(b_glu),
        g_attn_p, row(norm_ssm_g), w_out_full, row(ln_g), row(ln_b), min(256, seq))

    dq, dk, dv, g_sink = _attn_bwd(q_stack, k_rot, v_bf, sink_rows, d_o)
    dus, g_bb, g_cb, g_lam = [], [], [], []
    for d in range(N_DIR):
        du_d, gb_d, gc_d, dl_d = _ssm_bwd(u, d_ylin, starts[d], lam, bb, bbt, cb_t, direction=d, tb=ssm_tb,
                                          name=f"ssm_bwd_{d}")
        dus.append(du_d)
        g_bb.append(gb_d)
        g_cb.append(gc_d)
        g_lam.append(dl_d)
    g_bbar = _wide_diagonal(jnp.stack(g_bb, axis=1)).reshape(2, N_DG, SSM_CH, SSM_STATE)
    g_c = _tall_diagonal(jnp.stack(g_cb, axis=1))
    g_lam = jnp.swapaxes(jnp.stack(g_lam, axis=1).reshape(2, N_DIR, SUBSEG, N_GROUPS, SSM_STATE), 2, 3).reshape(
        2, N_DG, SUBSEG, SSM_STATE)
    g_ar, g_ai, g_dt, g_br, g_bi = _ssm_params_bwd(a_r, a_i, log_dt, b_r, b_i, g_lam[0], g_lam[1], g_bbar[0], g_bbar[1])

    g_x, g_wt = _proj_bwd(xs, dq, dk, dv, d_za, dus[0], dus[1], d_ylin, d_zs, d_pre, row(ssm_d), cos128, sin128,
                          wt_full, min(256, seq))

    g_b = jnp.stack([g_br, g_bi]).reshape(2, N_DIR, N_GROUPS, SSM_CH, SSM_STATE)
    g_small = _pack_small_grads(g_b, g_c, g_vec, g_ar, g_ai, g_dt, g_sink[:, 0])
    return loss_blk[0, 0], g_x, g_wt, g_w_out, g_w_glu, g_small
```

```python
import functools
import math

import numpy as np
import jax
import jax.numpy as jnp
from jax import lax
from jax.experimental import pallas as pl
from jax.experimental.pallas import tpu as pltpu

F32 = jnp.float32
BF16 = jnp.bfloat16
MESH = pl.DeviceIdType.MESH

D_MODEL = 1024
D_ATTN = 512
D_SSM = 512
HEAD_DIM = 64
N_Q_HEADS = 8
WINDOW = 128
ROPE_THETA = 10000.0
SSM_CH = 16
N_GROUPS = 32
SSM_STATE = 64
N_DIR = 2
STATE_W = N_GROUPS * SSM_STATE
N_SLAB = 4
SLAB_IN = 128
SLAB_ST = 512
NORM_EPS = 1e-5
NEG_INF = -1e30
ALPHA = 2.0 ** 0.25
D_IN_PROJ = 2304
N_CHIPS = 4

ADAM_LR = 0.001
ADAM_B1 = 0.9
ADAM_B2 = 0.999
ADAM_EPS = 1e-08
ADAM_WD = 0.01
ADAM_STEP = 10

SUBSEG = 8
SCAN_LANES = 512
SCAN_UNROLL = 4
VMEM_LIMIT = 48 * 1024 * 1024
ADAMW_BLOCK_BYTES = 3 * 512 * 1024

_PAIR_PERM = np.array([(64 * j + l) if l < 64 else (64 * (j + 4) + l - 64) for j in range(4) for l in range(128)])
_PAIR_INV = np.argsort(_PAIR_PERM)


def _cparams(sem=None):
    return pltpu.CompilerParams(dimension_semantics=sem, vmem_limit_bytes=VMEM_LIMIT)


def _dot(a, b):
    return jnp.dot(a, b, preferred_element_type=F32)


def _dot_nt(a, b):
    return lax.dot_general(a, b, (((1,), (1,)), ((), ())), preferred_element_type=F32)


def _dot_tn(a, b):
    return lax.dot_general(a, b, (((0,), (0,)), ((), ())), preferred_element_type=F32)


def _sigmoid(z):
    return 1.0 / (1.0 + jnp.exp(-z))


def _all_gather_chips(shards, out_dtype, name):
    n = len(shards)

    def body(*refs):
        in_refs, out_refs = refs[:n], refs[n:2 * n]
        send_sems, recv_sems = refs[2 * n:]
        x, y, c = lax.axis_index("x"), lax.axis_index("y"), lax.axis_index("c")
        sibling = (x, y, 1 - c)
        chips = [(1 - x, y), (x, 1 - y), (1 - x, 1 - y)]

        for a in range(n):
            out_refs[a][2 * x + y] = in_refs[a][...].astype(out_dtype)

        def half_of(a, px, py, half):
            rows = in_refs[a].shape[0] // 2
            return out_refs[a].at[2 * px + py, pl.ds(half * rows, rows), :]

        def copy(a, k, px, py, half, to):
            blk = half_of(a, px, py, half)
            return pltpu.make_async_remote_copy(src_ref=blk, dst_ref=blk, send_sem=send_sems.at[6 * a + k],
                                                recv_sem=recv_sems.at[6 * a + k], device_id=to, device_id_type=MESH)

        first = [copy(a, j, x, y, c, (*chips[j], c)) for a in range(n) for j in range(3)]
        for cp in first:
            cp.start()
        passed = []
        for a in range(n):
            for j in range(3):
                copy(a, j, *chips[j], c, (x, y, c)).wait_recv()
                fwd = copy(a, 3 + j, *chips[j], c, sibling)
                fwd.start()
                passed.append(fwd)
        for a in range(n):
            for j in range(3):
                copy(a, 3 + j, *chips[j], 1 - c, (x, y, c)).wait_recv()
        for cp in first + passed:
            cp.wait_send()

    vmem = pl.BlockSpec(memory_space=pltpu.VMEM)
    return pl.pallas_call(
        body, name=name,
        out_shape=[jax.ShapeDtypeStruct((N_CHIPS,) + s.shape, out_dtype) for s in shards],
        in_specs=[vmem] * n, out_specs=[vmem] * n,
        scratch_shapes=[pltpu.SemaphoreType.DMA((6 * n,)), pltpu.SemaphoreType.DMA((6 * n,))],
        compiler_params=pltpu.CompilerParams(vmem_limit_bytes=VMEM_LIMIT),
    )(*shards)


def _reduce_scatter_chips(pieces, name):
    _, rows, cols = pieces.shape
    h = rows // 2

    def body(p_ref, out_ref, a_ref, b_ref, send_sems, recv_sems):
        x, y, c = lax.axis_index("x"), lax.axis_index("y"), lax.axis_index("c")
        me = 2 * x + y
        sibling = (x, y, 1 - c)
        chips = [(1 - x, y), (x, 1 - y), (1 - x, 1 - y)]
        mine = pl.multiple_of(c * h, 8)
        other = pl.multiple_of((1 - c) * h, 8)

        swap = pltpu.make_async_remote_copy(src_ref=p_ref.at[:, pl.ds(other, h), :], dst_ref=a_ref,
                                            send_sem=send_sems.at[0], recv_sem=recv_sems.at[0],
                                            device_id=sibling, device_id_type=MESH)
        swap.start()
        swap.wait_recv()
        for k in range(N_CHIPS):
            a_ref[k] = a_ref[k] + p_ref[k, pl.ds(mine, h), :]

        def chip_copy(j, dst_slot):
            px, py = chips[j]
            return pltpu.make_async_remote_copy(src_ref=a_ref.at[2 * px + py], dst_ref=b_ref.at[dst_slot],
                                                send_sem=send_sems.at[1 + j], recv_sem=recv_sems.at[1 + j],
                                                device_id=(px, py, c), device_id_type=MESH)

        sends = [chip_copy(j, me) for j in range(3)]
        for cp in sends:
            cp.start()
        b_ref[me] = a_ref[me]
        for j in range(3):
            chip_copy(j, 2 * chips[j][0] + chips[j][1]).wait_recv()
        out_ref[pl.ds(mine, h), :] = (b_ref[0] + b_ref[1]) + (b_ref[2] + b_ref[3])

        back = pltpu.make_async_remote_copy(src_ref=out_ref.at[pl.ds(mine, h), :], dst_ref=out_ref.at[pl.ds(mine, h), :],
                                            send_sem=send_sems.at[4], recv_sem=recv_sems.at[4],
                                            device_id=sibling, device_id_type=MESH)
        back.start()
        pltpu.make_async_remote_copy(src_ref=out_ref.at[pl.ds(other, h), :], dst_ref=out_ref.at[pl.ds(other, h), :],
                                     send_sem=send_sems.at[4], recv_sem=recv_sems.at[4],
                                     device_id=sibling, device_id_type=MESH).wait_recv()
        swap.wait_send()
        for cp in sends:
            cp.wait_send()
        back.wait_send()

    vmem = pl.BlockSpec(memory_space=pltpu.VMEM)
    return pl.pallas_call(
        body, name=name,
        out_shape=jax.ShapeDtypeStruct((rows, cols), F32),
        in_specs=[vmem], out_specs=vmem,
        scratch_shapes=[pltpu.VMEM((N_CHIPS, h, cols), F32), pltpu.VMEM((N_CHIPS, h, cols), F32),
                        pltpu.SemaphoreType.DMA((5,)), pltpu.SemaphoreType.DMA((5,))],
        compiler_params=pltpu.CompilerParams(vmem_limit_bytes=VMEM_LIMIT),
    )(pieces)


def _ssm_param_values(ar, ai, logdt):
    dt = jnp.exp(logdt)
    mag = jnp.exp(dt * ar)
    cs, sn = jnp.cos(dt * ai), jnp.sin(dt * ai)
    lr, li = mag * cs, mag * sn
    den = ar * ar + ai * ai
    nr = (lr - 1.0) * ar + li * ai
    ni = li * ar - (lr - 1.0) * ai
    return dt, mag, lr, li, den, nr, ni


def _ssm_params_fwd(ar, ai, logdt, br, bi, n_square):
    def body(ar_ref, ai_ref, dt_ref, br_ref, bi_ref, lam_ref, bb_ref):
        _, _, lr, li, den, nr, ni = _ssm_param_values(ar_ref[...], ai_ref[...], dt_ref[...])
        lam_ref[0] = lr
        lam_ref[1] = li
        pr, pi = lr, li
        for _ in range(n_square):
            pr, pi = pr * pr - pi * pi, 2.0 * pr * pi
        lam_ref[2] = pr
        lam_ref[3] = pi
        fr, fi = nr / den, ni / den
        b_r, b_i = br_ref[...], bi_ref[...]
        bb_ref[0] = fr * b_r - fi * b_i
        bb_ref[1] = fr * b_i + fi * b_r

    return pl.pallas_call(body, name="ssm_params_fwd",
                          out_shape=[jax.ShapeDtypeStruct((4,) + ar.shape, F32),
                                     jax.ShapeDtypeStruct((2,) + br.shape, F32)])(ar, ai, logdt, br, bi)


def _ssm_params_bwd(ar, ai, logdt, br, bi, dlam_r, dlam_i, dbb_r, dbb_i):
    def body(ar_ref, ai_ref, dt_ref, br_ref, bi_ref, dlr_ref, dli_ref, dbr_ref, dbi_ref,
             gar_ref, gai_ref, gdt_ref, gbr_ref, gbi_ref):
        a_r, a_i = ar_ref[...], ai_ref[...]
        dt, mag, lr, li, den, nr, ni = _ssm_param_values(a_r, a_i, dt_ref[...])
        fr, fi = nr / den, ni / den
        b_r, b_i = br_ref[...], bi_ref[...]
        g_r, g_i = dbr_ref[...], dbi_ref[...]
        gbr_ref[...] = fr * g_r + fi * g_i
        gbi_ref[...] = fr * g_i - fi * g_r
        d_fr = jnp.sum(b_r * g_r + b_i * g_i, axis=1, keepdims=True)
        d_fi = jnp.sum(b_r * g_i - b_i * g_r, axis=1, keepdims=True)
        d_nr, d_ni = d_fr / den, d_fi / den
        d_den = -(d_fr * nr + d_fi * ni) / (den * den)
        d_lr = jnp.sum(dlr_ref[...], axis=1, keepdims=True) + d_nr * a_r - d_ni * a_i
        d_li = jnp.sum(dli_ref[...], axis=1, keepdims=True) + d_nr * a_i + d_ni * a_r
        d_ar = d_nr * (lr - 1.0) + d_ni * li + d_den * 2.0 * a_r
        d_ai = d_nr * li - d_ni * (lr - 1.0) + d_den * 2.0 * a_i
        d_mag = (d_lr * lr + d_li * li) / mag
        d_theta = d_li * lr - d_lr * li
        gar_ref[...] = d_ar + d_mag * mag * dt
        gai_ref[...] = d_ai + d_theta * dt
        d_dt = d_mag * mag * a_r + d_theta * a_i
        gdt_ref[...] = jnp.sum(d_dt, axis=2, keepdims=True) * dt

    small = jax.ShapeDtypeStruct(ar.shape, F32)
    return pl.pallas_call(
        body, name="ssm_params_bwd",
        out_shape=[small, small, jax.ShapeDtypeStruct(logdt.shape, F32),
                   jax.ShapeDtypeStruct(br.shape, F32), jax.ShapeDtypeStruct(br.shape, F32)],
    )(ar, ai, logdt, br, bi, dlam_r, dlam_i, dbb_r, dbb_i)


def _rope_tables(seq):
    half = HEAD_DIM // 2
    inv_freq = ROPE_THETA ** (-jnp.arange(half, dtype=F32) / half)
    ang = jnp.arange(seq, dtype=jnp.int32).astype(F32)[:, None] * inv_freq[None, :]
    cos, sin = jnp.cos(ang), jnp.sin(ang)
    cos128 = jnp.concatenate([cos, cos, cos, cos], axis=1)
    sin128 = jnp.concatenate([-sin, sin, -sin, sin], axis=1)
    return cos128, sin128


def _rotate_half_unsigned(t):
    lane = lax.broadcasted_iota(jnp.int32, t.shape, 1)
    return jnp.where((lane % HEAD_DIM) < HEAD_DIM // 2, pltpu.roll(t, 96, 1), pltpu.roll(t, 32, 1))


def _rope(t, cos, sin_signed):
    return t * cos + _rotate_half_unsigned(t) * sin_signed


def _pair_blocks(base):
    out = []
    for j in range(4):
        for g in range(2):
            nat = base + HEAD_DIM * (4 * g + j)
            par = base + 128 * j + HEAD_DIM * g
            out.append((slice(nat, nat + HEAD_DIM), slice(par, par + HEAD_DIM)))
    return out


W_Q, W_KV, W_ZA, W_U, W_ZS = 0, 512, 768, 1280, 1792


def _proj(x, wt, cos128, sin128, tb):
    seq = x.shape[0]

    def body(x_ref, wt_ref, cos_ref, sin_ref, q_ref, k_ref, v_ref, za_ref, u_ref, zs_ref, wp):
        @pl.when(pl.program_id(0) == 0)
        def _():
            for dst_base, src_base in ((0, W_Q), (512, W_ZA)):
                for nat, par in _pair_blocks(0):
                    wp[dst_base + par.start:dst_base + par.stop, :] = wt_ref[src_base + nat.start:src_base + nat.stop, :]

        xb = x_ref[...].astype(BF16)
        cos, sin = cos_ref[...], sin_ref[...]
        lo = lax.broadcasted_iota(jnp.int32, (tb, 128), 1) < HEAD_DIM
        q = _dot_nt(xb, wp[0:512, :])
        for j in range(4):
            qj = _rope(q[:, 128 * j:128 * (j + 1)], cos, sin)
            q_ref[j] = jnp.where(lo, qj, 0.0).astype(BF16)
            q_ref[4 + j] = jnp.where(lo, 0.0, qj).astype(BF16)
        kv = _dot_nt(xb, wt_ref[W_KV:W_ZA, :])
        k_ref[...] = _rope(kv[:, 0:128], cos, sin).astype(BF16)
        v_ref[...] = kv[:, 128:256].astype(BF16)
        za_ref[...] = _dot_nt(xb, wp[512:1024, :])
        u_val = _dot_nt(xb, wt_ref[W_U:W_ZS, :])
        for k in range(N_SLAB):
            u_ref[k] = u_val[:, k * SLAB_IN:(k + 1) * SLAB_IN]
        zs_ref[...] = _dot_nt(xb, wt_ref[W_ZS:D_IN_PROJ, :])

    row = lambda w: pl.BlockSpec((tb, w), lambda i: (i, 0))
    return pl.pallas_call(
        body, name="proj", grid=(seq // tb,),
        in_specs=[row(D_MODEL), pl.BlockSpec((D_IN_PROJ, D_MODEL), lambda i: (0, 0)), row(128), row(128)],
        out_specs=[pl.BlockSpec((8, tb, 128), lambda i: (0, i, 0)), row(128), row(128), row(512),
                   pl.BlockSpec((N_SLAB, tb, SLAB_IN), lambda i: (0, i, 0)), row(512)],
        out_shape=[jax.ShapeDtypeStruct((8, seq, 128), BF16), jax.ShapeDtypeStruct((seq, 128), BF16),
                   jax.ShapeDtypeStruct((seq, 128), BF16), jax.ShapeDtypeStruct((seq, 512), F32),
                   jax.ShapeDtypeStruct((N_SLAB, seq, SLAB_IN), F32), jax.ShapeDtypeStruct((seq, 512), F32)],
        scratch_shapes=[pltpu.VMEM((1024, D_MODEL), BF16)],
        compiler_params=_cparams(("arbitrary",)),
    )(x, wt, cos128, sin128)


ATT_TQ = 128
ATT_KEYS = 3 * ATT_TQ


def _attn_window(i, seq):
    start = jnp.clip(i * ATT_TQ - WINDOW, 0, seq - ATT_KEYS)
    return pl.multiple_of(start, ATT_TQ)


def _attn_bias():
    r = jnp.arange(ATT_TQ, dtype=jnp.int32)[None, :, None]
    c = jnp.arange(ATT_KEYS, dtype=jnp.int32)[None, None, :]
    off = (jnp.arange(3, dtype=jnp.int32) * ATT_TQ)[:, None, None]
    return jnp.where(jnp.abs(r + off - c) <= WINDOW, 0.0, NEG_INF).astype(F32)


def _attn_bias_spec(nblk):
    pick = lambda i: jnp.where(i == 0, 0, jnp.where(i == nblk - 1, 2, 1))
    return pl.BlockSpec((None, ATT_TQ, ATT_KEYS), lambda i: (pick(i), 0, 0))


def _attn_head(q_ref, sink_ref, bias, kw, hd):
    q = q_ref[hd]
    s = _dot_nt(q, kw) * (HEAD_DIM ** -0.5) + bias
    sink = sink_ref[hd * ATT_TQ:(hd + 1) * ATT_TQ, :]
    m = jnp.maximum(jnp.max(s, axis=1, keepdims=True), sink)
    p = jnp.exp(s - m)
    p_sink = jnp.exp(sink - m)
    denom = jnp.sum(p, axis=1, keepdims=True) + p_sink
    return q, p, denom, p_sink


def _attn_fwd(q_stack, k, v, sink_rows, bias):
    seq = k.shape[0]

    def body(q_ref, k_ref, v_ref, sink_ref, bias_ref, o_ref):
        start = _attn_window(pl.program_id(0), seq)
        kw = k_ref[pl.ds(start, ATT_KEYS), :]
        vw = v_ref[pl.ds(start, ATT_KEYS), :]
        bias_blk = bias_ref[...]
        outs = []
        for hd in range(N_Q_HEADS):
            _, p, denom, _ = _attn_head(q_ref, sink_ref, bias_blk, kw, hd)
            outs.append(_dot(p.astype(BF16), vw) / denom)
        lo = lax.broadcasted_iota(jnp.int32, (ATT_TQ, 128), 1) < HEAD_DIM
        for j in range(4):
            o_ref[:, 128 * j:128 * (j + 1)] = jnp.where(lo, outs[j], outs[4 + j])

    full = lambda w: pl.BlockSpec((seq, w), lambda i: (0, 0))
    return pl.pallas_call(
        body, name="attn_fwd", grid=(seq // ATT_TQ,),
        in_specs=[pl.BlockSpec((8, ATT_TQ, 128), lambda i: (0, i, 0)), full(128), full(128),
                  pl.BlockSpec((8 * ATT_TQ, 1), lambda i: (0, 0)), _attn_bias_spec(seq // ATT_TQ)],
        out_specs=pl.BlockSpec((ATT_TQ, 512), lambda i: (i, 0)),
        out_shape=jax.ShapeDtypeStruct((seq, 512), F32),
        compiler_params=_cparams(("arbitrary",)),
    )(q_stack, k, v, sink_rows, bias)


def _attn_bwd(q_stack, k, v, sink_rows, bias, d_o):
    seq = k.shape[0]

    def body(q_ref, k_ref, v_ref, sink_ref, bias_ref, do_ref, dq_ref, dk_ref, dv_ref, dsink_ref, sink_acc):
        i = pl.program_id(0)

        @pl.when(i == 0)
        def _():
            dk_ref[...] = jnp.zeros_like(dk_ref)
            dv_ref[...] = jnp.zeros_like(dv_ref)
            sink_acc[...] = jnp.zeros_like(sink_acc)

        start = _attn_window(i, seq)
        kw = k_ref[pl.ds(start, ATT_KEYS), :]
        vw = v_ref[pl.ds(start, ATT_KEYS), :]
        bias_blk = bias_ref[...]
        lo = lax.broadcasted_iota(jnp.int32, (ATT_TQ, 128), 1) < HEAD_DIM
        d_o_blk = do_ref[...]
        scale = HEAD_DIM ** -0.5
        q_all, do_all, ds_all, probs_all, dq_heads = [], [], [], [], []
        for hd in range(N_Q_HEADS):
            q, p, denom, p_sink = _attn_head(q_ref, sink_ref, bias_blk, kw, hd)
            d_pair = d_o_blk[:, 128 * (hd % 4):128 * (hd % 4 + 1)]
            do_h = (jnp.where(lo, d_pair, 0.0) if hd < 4 else jnp.where(lo, 0.0, d_pair)).astype(BF16)
            inv = 1.0 / denom
            probs = p * inv
            dp = _dot_nt(do_h, vw)
            delta = jnp.sum(probs * dp, axis=1, keepdims=True)
            ds = (probs * (dp - delta)).astype(BF16)
            sink_acc[hd * ATT_TQ:(hd + 1) * ATT_TQ, :] += -(p_sink * inv) * delta
            dq_heads.append(_dot(ds, kw) * scale)
            q_all.append(q)
            do_all.append(do_h)
            ds_all.append(ds)
            probs_all.append(probs.astype(BF16))
        for j in range(4):
            dq_ref[:, 128 * j:128 * (j + 1)] = jnp.where(lo, dq_heads[j], dq_heads[4 + j])
        stack = lambda parts: jnp.concatenate(parts, axis=0)
        dk_ref[pl.ds(start, ATT_KEYS), :] += _dot_tn(stack(ds_all), stack(q_all)) * scale
        dv_ref[pl.ds(start, ATT_KEYS), :] += _dot_tn(stack(probs_all), stack(do_all))

        @pl.when(i == pl.num_programs(0) - 1)
        def _():
            for hd in range(N_Q_HEADS):
                total = jnp.sum(sink_acc[hd * ATT_TQ:(hd + 1) * ATT_TQ, :], axis=0, keepdims=True)
                dsink_ref[hd:hd + 1, :] = jnp.broadcast_to(total, (1, 128))

    full = lambda w: pl.BlockSpec((seq, w), lambda i: (0, 0))
    rows = pl.BlockSpec((8 * ATT_TQ, 1), lambda i: (0, 0))
    return pl.pallas_call(
        body, name="attn_bwd", grid=(seq // ATT_TQ,),
        in_specs=[pl.BlockSpec((8, ATT_TQ, 128), lambda i: (0, i, 0)), full(128), full(128), rows,
                  _attn_bias_spec(seq // ATT_TQ), pl.BlockSpec((ATT_TQ, 512), lambda i: (i, 0))],
        out_specs=[pl.BlockSpec((ATT_TQ, 512), lambda i: (i, 0)), full(128), full(128),
                   pl.BlockSpec((N_Q_HEADS, 128), lambda i: (0, 0))],
        out_shape=[jax.ShapeDtypeStruct((seq, 512), F32), jax.ShapeDtypeStruct((seq, 128), F32),
                   jax.ShapeDtypeStruct((seq, 128), F32), jax.ShapeDtypeStruct((N_Q_HEADS, 128), F32)],
        scratch_shapes=[pltpu.VMEM((8 * ATT_TQ, 1), F32)],
        compiler_params=_cparams(("arbitrary",)),
    )(q_stack, k, v, sink_rows, bias, d_o)


def _permute_rows(dst_ref, src_ref, sub_len):
    for k in range(N_SLAB):
        for j in range(sub_len):
            dst_ref[k, 8 * j:8 * (j + 1), :] = src_ref.at[k][pl.ds(j, SUBSEG, stride=sub_len), :]


def _unpermute_rows(dst_ref, src_ref, sub_len):
    for k in range(N_SLAB):
        for s in range(SUBSEG):
            dst_ref[k, s * sub_len:(s + 1) * sub_len, :] = src_ref.at[k][pl.ds(s, sub_len, stride=SUBSEG), :]


def _scan_pass(br_ref, bi_ref, lr_row, li_row, start, end_refs, *, sub_len, reverse, store):
    width = br_ref.shape[1]
    for c0 in range(0, width, SCAN_LANES):
        cols = slice(c0, c0 + SCAN_LANES)
        lr = jnp.broadcast_to(lr_row[:, cols], (SUBSEG, SCAN_LANES))
        li = jnp.broadcast_to(li_row[:, cols], (SUBSEG, SCAN_LANES))
        if start is None:
            init = (jnp.zeros((SUBSEG, SCAN_LANES), F32), jnp.zeros((SUBSEG, SCAN_LANES), F32))
        else:
            init = (start[0][:, cols], start[1][:, cols])

        def steps(jo, state, cols=cols, lr=lr, li=li):
            sr, si = state
            for ju in range(SCAN_UNROLL):
                jj = jo * SCAN_UNROLL + ju
                j = (sub_len - 1 - jj) if reverse else jj
                r0 = pl.multiple_of(j * SUBSEG, SUBSEG)
                nr = lr * sr - li * si + br_ref[pl.ds(r0, SUBSEG), cols]
                ni = lr * si + li * sr + bi_ref[pl.ds(r0, SUBSEG), cols]
                if store:
                    br_ref[pl.ds(r0, SUBSEG), cols] = nr
                    bi_ref[pl.ds(r0, SUBSEG), cols] = ni
                sr, si = nr, ni
            return sr, si

        sr, si = lax.fori_loop(0, sub_len // SCAN_UNROLL, steps, init)
        if end_refs is not None:
            end_refs[0][:, cols] = sr
            end_refs[1][:, cols] = si


def _resolve_starts(z_refs, carry_refs, start_refs, pr_row, pi_row, *, reverse):
    cr, ci = carry_refs[0][0:1, :], carry_refs[1][0:1, :]
    for s in (range(SUBSEG - 1, -1, -1) if reverse else range(SUBSEG)):
        start_refs[0][s:s + 1, :] = cr
        start_refs[1][s:s + 1, :] = ci
        zr, zi = z_refs[0][s:s + 1, :], z_refs[1][s:s + 1, :]
        cr, ci = pr_row * cr - pi_row * ci + zr, pr_row * ci + pi_row * cr + zi
    carry_refs[0][0:1, :] = cr
    carry_refs[1][0:1, :] = ci


def _param_specs(direction):
    row = lambda q: pl.BlockSpec((None, None, 1, STATE_W), lambda i: (q, direction, 0, 0))
    wide = lambda q: pl.BlockSpec((None, None, N_SLAB, SLAB_IN, SLAB_ST), lambda i: (q, direction, 0, 0, 0))
    tall = lambda q: pl.BlockSpec((None, None, N_SLAB, SLAB_ST, SLAB_IN), lambda i: (q, direction, 0, 0, 0))
    return [row(q) for q in range(4)], [wide(0), wide(1)], [tall(0), tall(1)]


def _ssm_fwd(u, lam, bb, cb, *, direction, tb, name):
    reverse = direction == 1
    seq = u.shape[1]
    nblk = seq // tb
    sub_len = tb // SUBSEG

    def body(u_ref, lr_ref, li_ref, pr_ref, pi_ref, bbr_ref, bbi_ref, cbr_ref, cbi_ref,
             y_ref, sr_ref, si_ref, xr, xi, up, yp, zr, zi, car, cai):
        @pl.when(pl.program_id(0) == 0)
        def _():
            car[...] = jnp.zeros_like(car)
            cai[...] = jnp.zeros_like(cai)

        _permute_rows(up, u_ref, sub_len)
        for k in range(N_SLAB):
            ub = up[k].astype(BF16)
            xr[:, k * SLAB_ST:(k + 1) * SLAB_ST] = _dot(ub, bbr_ref[k])
            xi[:, k * SLAB_ST:(k + 1) * SLAB_ST] = _dot(ub, bbi_ref[k])
        lr, li = lr_ref[...], li_ref[...]
        _scan_pass(xr, xi, lr, li, None, (zr, zi), sub_len=sub_len, reverse=reverse, store=False)
        _resolve_starts((zr, zi), (car, cai), (sr_ref, si_ref), pr_ref[...], pi_ref[...], reverse=reverse)
        _scan_pass(xr, xi, lr, li, (sr_ref, si_ref), None, sub_len=sub_len, reverse=reverse, store=True)
        for k in range(N_SLAB):
            st = slice(k * SLAB_ST, (k + 1) * SLAB_ST)
            yp[k] = _dot(xr[:, st].astype(BF16), cbr_ref[k]) - _dot(xi[:, st].astype(BF16), cbi_ref[k])
        _unpermute_rows(y_ref, yp, sub_len)

    blk = (lambda i: nblk - 1 - i) if reverse else (lambda i: i)
    rows, wide, tall = _param_specs(direction)
    tok = pl.BlockSpec((N_SLAB, tb, SLAB_IN), lambda i: (0, blk(i), 0))
    start_spec = pl.BlockSpec((None, SUBSEG, STATE_W), lambda i: (blk(i), 0, 0))
    return pl.pallas_call(
        body, name=name, grid=(nblk,),
        in_specs=[tok] + rows + wide + tall,
        out_specs=[tok, start_spec, start_spec],
        out_shape=[jax.ShapeDtypeStruct((N_SLAB, seq, SLAB_IN), F32), jax.ShapeDtypeStruct((nblk, SUBSEG, STATE_W), F32),
                   jax.ShapeDtypeStruct((nblk, SUBSEG, STATE_W), F32)],
        scratch_shapes=[pltpu.VMEM((tb, STATE_W), F32), pltpu.VMEM((tb, STATE_W), F32),
                        pltpu.VMEM((N_SLAB, tb, SLAB_IN), F32), pltpu.VMEM((N_SLAB, tb, SLAB_IN), F32),
                        pltpu.VMEM((SUBSEG, STATE_W), F32), pltpu.VMEM((SUBSEG, STATE_W), F32),
                        pltpu.VMEM((SUBSEG, STATE_W), F32), pltpu.VMEM((SUBSEG, STATE_W), F32)],
        compiler_params=_cparams(("arbitrary",)),
    )(u, lam, lam, lam, lam, bb, bb, cb, cb)


def _ssm_bwd(u, dy, starts, lam, bb, bbt, cb_t, *, direction, tb, name):
    reverse = direction == 1
    seq = u.shape[1]
    nblk = seq // tb
    sub_len = tb // SUBSEG

    def body(u_ref, dy_ref, sr_ref, si_ref, lr_ref, li_ref, pr_ref, pi_ref, bbr_ref, bbi_ref, btr_ref, bti_ref,
             ctr_ref, cti_ref, du_ref, gb_ref, gc_ref, dl_ref,
             xr, xi, gr, gi, up, dyp, dup, zr, zi, gsr, gsi, car, cai):
        gbr_ref, gbi_ref = gb_ref.at[0], gb_ref.at[1]
        gcr_ref, gci_ref = gc_ref.at[0], gc_ref.at[1]
        dlr_ref, dli_ref = dl_ref.at[0], dl_ref.at[1]

        @pl.when(pl.program_id(0) == 0)
        def _():
            for ref in (car, cai, gbr_ref, gbi_ref, gcr_ref, gci_ref, dlr_ref, dli_ref):
                ref[...] = jnp.zeros_like(ref)

        _permute_rows(up, u_ref, sub_len)
        _permute_rows(dyp, dy_ref, sub_len)
        lr, li = lr_ref[...], li_ref[...]
        for k in range(N_SLAB):
            st = slice(k * SLAB_ST, (k + 1) * SLAB_ST)
            ub = up[k].astype(BF16)
            xr[:, st] = _dot(ub, bbr_ref[k])
            xi[:, st] = _dot(ub, bbi_ref[k])
            dyb = dyp[k].astype(BF16)
            gr[:, st] = _dot(dyb, ctr_ref[k])
            gi[:, st] = -_dot(dyb, cti_ref[k])
        _scan_pass(xr, xi, lr, li, (sr_ref, si_ref), None, sub_len=sub_len, reverse=reverse, store=True)
        for k in range(N_SLAB):
            st = slice(k * SLAB_ST, (k + 1) * SLAB_ST)
            dyb = dyp[k].astype(BF16)
            gcr_ref[k] += _dot_tn(xr[:, st].astype(BF16), dyb)
            gci_ref[k] -= _dot_tn(xi[:, st].astype(BF16), dyb)
        nli = -li
        _scan_pass(gr, gi, lr, nli, None, (zr, zi), sub_len=sub_len, reverse=not reverse, store=False)
        _resolve_starts((zr, zi), (car, cai), (gsr, gsi), pr_ref[...], -pi_ref[...], reverse=not reverse)
        _scan_pass(gr, gi, lr, nli, (gsr, gsi), None, sub_len=sub_len, reverse=not reverse, store=True)
        for k in range(N_SLAB):
            st = slice(k * SLAB_ST, (k + 1) * SLAB_ST)
            ub = up[k].astype(BF16)
            grb, gib = gr[:, st].astype(BF16), gi[:, st].astype(BF16)
            gbr_ref[k] += _dot_tn(ub, grb)
            gbi_ref[k] += _dot_tn(ub, gib)
            dup[k] = _dot(grb, btr_ref[k]) + _dot(gib, bti_ref[k])
        _unpermute_rows(du_ref, dup, sub_len)

        for c0 in range(0, STATE_W, SCAN_LANES):
            cols = slice(c0, c0 + SCAN_LANES)
            edge = (sub_len - 1) * SUBSEG if reverse else 0
            g_r, g_i = gr[edge:edge + SUBSEG, cols], gi[edge:edge + SUBSEG, cols]
            x_r, x_i = sr_ref[:, cols], si_ref[:, cols]
            acc = (dlr_ref[:, cols] + (g_r * x_r + g_i * x_i), dli_ref[:, cols] + (g_i * x_r - g_r * x_i))

            def step(jj, acc, cols=cols):
                tile = lambda t: t * SUBSEG if isinstance(t, int) else pl.multiple_of(t * SUBSEG, SUBSEG)
                r_g, r_x = tile(jj if reverse else jj + 1), tile(jj + 1 if reverse else jj)
                g_r, g_i = gr[pl.ds(r_g, SUBSEG), cols], gi[pl.ds(r_g, SUBSEG), cols]
                x_r, x_i = xr[pl.ds(r_x, SUBSEG), cols], xi[pl.ds(r_x, SUBSEG), cols]
                return acc[0] + (g_r * x_r + g_i * x_i), acc[1] + (g_i * x_r - g_r * x_i)

            acc = step(0, acc)
            acc = lax.fori_loop(0, (sub_len - 2) // 2, lambda jo, a: step(2 * jo + 2, step(2 * jo + 1, a)), acc)
            dlr_ref[:, cols] = acc[0]
            dli_ref[:, cols] = acc[1]

    blk = (lambda i: i) if reverse else (lambda i: nblk - 1 - i)
    rows, wide, tall = _param_specs(direction)
    tok = pl.BlockSpec((N_SLAB, tb, SLAB_IN), lambda i: (0, blk(i), 0))
    start_spec = pl.BlockSpec((None, SUBSEG, STATE_W), lambda i: (blk(i), 0, 0))
    gb_shape, gc_shape, dl_shape = (2, N_SLAB, SLAB_IN, SLAB_ST), (2, N_SLAB, SLAB_ST, SLAB_IN), (2, SUBSEG, STATE_W)
    whole = lambda shape: pl.BlockSpec(shape, lambda i: (0,) * len(shape))
    big = lambda: pltpu.VMEM((tb, STATE_W), F32)
    slabs = lambda: pltpu.VMEM((N_SLAB, tb, SLAB_IN), F32)
    tile = lambda: pltpu.VMEM((SUBSEG, STATE_W), F32)
    return pl.pallas_call(
        body, name=name, grid=(nblk,),
        in_specs=[tok, tok, start_spec, start_spec] + rows + wide + tall + wide,
        out_specs=[tok, whole(gb_shape), whole(gc_shape), whole(dl_shape)],
        out_shape=[jax.ShapeDtypeStruct((N_SLAB, seq, SLAB_IN), F32), jax.ShapeDtypeStruct(gb_shape, F32),
                   jax.ShapeDtypeStruct(gc_shape, F32), jax.ShapeDtypeStruct(dl_shape, F32)],
        scratch_shapes=[big(), big(), big(), big(), slabs(), slabs(), slabs(),
                        tile(), tile(), tile(), tile(), tile(), tile()],
        compiler_params=_cparams(("arbitrary",)),
    )(u, dy, *starts, lam, lam, lam, lam, bb, bb, bbt, bbt, cb_t, cb_t)


GELU_C = math.sqrt(2.0 / math.pi)
GELU_K = 0.044715


def _mid(o, za, u, y_f, y_b, zs, x, target, ssm_d, w_glu, b_glu, g_attn, g_ssm, w_out, ln_g, ln_b, tb):
    seq = x.shape[0]

    def body(o_ref, za_ref, u_ref, yf_ref, yb_ref, zs_ref, x_ref, t_ref, d_ref, wg_ref, bg_ref, ga_ref, gs_ref,
             wo_ref, lg_ref, lb_ref,
             loss_ref, do_ref, dza_ref, dyl_ref, dzs_ref, dpre_ref, gwo_ref, gwg_ref, vec_ref, wop):
        @pl.when(pl.program_id(0) == 0)
        def _():
            for ref in (loss_ref, gwo_ref, gwg_ref, vec_ref):
                ref[...] = jnp.zeros_like(ref)
            for nat, par in _pair_blocks(0):
                wop[par, :] = wo_ref[nat, :]
            wop[D_ATTN:, :] = wo_ref[D_ATTN:, :]

        o, za = o_ref[...], za_ref[...]
        sig_a = _sigmoid(za)
        silu_a = za * sig_a
        ya = o * silu_a
        r_a = lax.rsqrt(jnp.mean(ya * ya, axis=1, keepdims=True) + NORM_EPS)
        n_a = ya * r_a
        g_a = ga_ref[...]
        unslab = lambda ref: jnp.concatenate([ref[k] for k in range(N_SLAB)], axis=1)
        u_blk, zs = unslab(u_ref), zs_ref[...]
        d_row = d_ref[...]
        ylin = d_row * u_blk + unslab(yf_ref) + unslab(yb_ref)
        inner = GELU_C * (ylin + GELU_K * ylin * ylin * ylin)
        th = jnp.tanh(inner)
        gl = 0.5 * ylin * (1.0 + th)
        glb = gl.astype(BF16)
        sg = _sigmoid(_dot(glb, wg_ref[...]) + bg_ref[...])
        y2 = gl * sg
        sig_s = _sigmoid(zs)
        silu_s = zs * sig_s
        ys = y2 * silu_s
        r_s = lax.rsqrt(jnp.mean(ys * ys, axis=1, keepdims=True) + NORM_EPS)
        n_s = ys * r_s
        g_s = gs_ref[...]
        mixed = jnp.concatenate([n_a * g_a, n_s * g_s], axis=1).astype(BF16)
        pre = ALPHA * x_ref[...] + _dot(mixed, wop[...])
        mu = jnp.mean(pre, axis=1, keepdims=True)
        cen = pre - mu
        rstd = lax.rsqrt(jnp.mean(cen * cen, axis=1, keepdims=True) + NORM_EPS)
        hhat = cen * rstd
        ln_g = lg_ref[...]
        err = hhat * ln_g + lb_ref[...] - t_ref[...]
        loss_ref[...] += 0.5 * jnp.sum(jnp.mean(err * err, axis=1, keepdims=True))

        dh = err * (1.0 / D_MODEL)
        vec_ref[0:1, :] += jnp.sum(dh * hhat, axis=0, keepdims=True)
        vec_ref[1:2, :] += jnp.sum(dh, axis=0, keepdims=True)
        dhh = dh * ln_g
        dpre = rstd * (dhh - jnp.mean(dhh, axis=1, keepdims=True) - hhat * jnp.mean(dhh * hhat, axis=1, keepdims=True))
        dpre_ref[...] = dpre
        dpb = dpre.astype(BF16)
        for j in range(4):
            g_pair = _dot_tn(mixed[:, 128 * j:128 * (j + 1)], dpb)
            for g in range(2):
                nat = HEAD_DIM * (4 * g + j)
                gwo_ref[nat:nat + HEAD_DIM, :] += g_pair[HEAD_DIM * g:HEAD_DIM * (g + 1), :]
        gwo_ref[D_ATTN:, :] += _dot_tn(mixed[:, D_ATTN:], dpb)
        dmix = _dot_nt(dpb, wop[...])
        dna = dmix[:, :D_ATTN]
        vec_ref[2:3, 0:D_ATTN] += jnp.sum(dna * n_a, axis=0, keepdims=True)
        dna = dna * g_a
        dya = r_a * (dna - n_a * jnp.mean(dna * n_a, axis=1, keepdims=True))
        do_ref[...] = dya * silu_a
        dza_ref[...] = dya * o * (sig_a * (1.0 + za * (1.0 - sig_a)))
        dns = dmix[:, D_ATTN:]
        vec_ref[2:3, D_ATTN:] += jnp.sum(dns * n_s, axis=0, keepdims=True)
        dns = dns * g_s
        dys = r_s * (dns - n_s * jnp.mean(dns * n_s, axis=1, keepdims=True))
        dzs_ref[...] = dys * y2 * (sig_s * (1.0 + zs * (1.0 - sig_s)))
        dy2 = dys * silu_s
        da = dy2 * gl * sg * (1.0 - sg)
        vec_ref[3:4, D_SSM:] += jnp.sum(da, axis=0, keepdims=True)
        dab = da.astype(BF16)
        gwg_ref[...] += _dot_tn(glb, dab)
        dgl = dy2 * sg + _dot_nt(dab, wg_ref[...])
        dylin = dgl * (0.5 * (1.0 + th) + 0.5 * ylin * (1.0 - th * th) * GELU_C * (1.0 + 3.0 * GELU_K * ylin * ylin))
        for k in range(N_SLAB):
            dyl_ref[k] = dylin[:, k * SLAB_IN:(k + 1) * SLAB_IN]
        vec_ref[3:4, 0:D_SSM] += jnp.sum(dylin * u_blk, axis=0, keepdims=True)

    tok = lambda w: pl.BlockSpec((tb, w), lambda i: (i, 0))
    slab = pl.BlockSpec((N_SLAB, tb, SLAB_IN), lambda i: (0, i, 0))
    const = lambda r, c: pl.BlockSpec((r, c), lambda i: (0, 0))
    tok_shape = jax.ShapeDtypeStruct((seq, 512), F32)
    return pl.pallas_call(
        body, name="mid", grid=(seq // tb,),
        in_specs=[tok(512), tok(512), slab, slab, slab, tok(512), tok(1024), tok(1024),
                  const(1, 512), const(512, 512), const(1, 512), const(1, 512), const(1, 512),
                  const(1024, 1024), const(1, 1024), const(1, 1024)],
        out_specs=[const(8, 128), tok(512), tok(512), slab, tok(512), tok(1024),
                   const(1024, 1024), const(512, 512), const(8, 1024)],
        out_shape=[jax.ShapeDtypeStruct((8, 128), F32), tok_shape, tok_shape,
                   jax.ShapeDtypeStruct((N_SLAB, seq, SLAB_IN), F32), tok_shape,
                   jax.ShapeDtypeStruct((seq, 1024), F32), jax.ShapeDtypeStruct((1024, 1024), F32),
                   jax.ShapeDtypeStruct((512, 512), F32), jax.ShapeDtypeStruct((8, 1024), F32)],
        scratch_shapes=[pltpu.VMEM((D_MODEL, D_MODEL), BF16)],
        compiler_params=_cparams(("arbitrary",)),
    )(o, za, u, y_f, y_b, zs, x, target, ssm_d, w_glu, b_glu, g_attn, g_ssm, w_out, ln_g, ln_b)


def _proj_bwd(x, dq, dk, dv, dza, du_f, du_b, dylin, dzs, dpre, ssm_d, cos128, sin128, wt, tb):
    seq = x.shape[0]

    def body(x_ref, dq_ref, dk_ref, dv_ref, dza_ref, duf_ref, dub_ref, dyl_ref, dzs_ref, dpre_ref, d_ref,
             cos_ref, sin_ref, wt_ref, gx_ref, gw_ref, wp):
        @pl.when(pl.program_id(0) == 0)
        def _():
            gw_ref[...] = jnp.zeros_like(gw_ref)
            for base in (W_Q, W_ZA):
                for nat, par in _pair_blocks(base):
                    wp[par, :] = wt_ref[nat, :]
            wp[W_KV:W_ZA, :] = wt_ref[W_KV:W_ZA, :]
            wp[W_U:, :] = wt_ref[W_U:, :]

        cos, sin = cos_ref[...], sin_ref[...]

        def unrope(t):
            return t * cos + _rotate_half_unsigned(t * sin)

        dq_rot = dq_ref[...]
        pieces = [unrope(dq_rot[:, 128 * j:128 * (j + 1)]) for j in range(4)]
        d_row = d_ref[...]
        pieces += [unrope(dk_ref[...]), dv_ref[...], dza_ref[...]]
        pieces += [duf_ref[k] + dub_ref[k] + d_row[:, k * SLAB_IN:(k + 1) * SLAB_IN] * dyl_ref[k] for k in range(N_SLAB)]
        pieces += [dzs_ref[...]]
        dproj = jnp.concatenate(pieces, axis=1).astype(BF16)
        gx_ref[...] = ALPHA * dpre_ref[...] + _dot(dproj, wp[...])
        xb = x_ref[...].astype(BF16)
        for base in (W_Q, W_ZA):
            for j in range(4):
                g_pair = _dot_tn(dproj[:, base + 128 * j:base + 128 * (j + 1)], xb)
                for g in range(2):
                    nat = base + HEAD_DIM * (4 * g + j)
                    gw_ref[nat:nat + HEAD_DIM, :] += g_pair[HEAD_DIM * g:HEAD_DIM * (g + 1), :]
        gw_ref[W_KV:W_ZA, :] += _dot_tn(dproj[:, W_KV:W_ZA], xb)
        gw_ref[W_U:, :] += _dot_tn(dproj[:, W_U:], xb)

    tok = lambda w: pl.BlockSpec((tb, w), lambda i: (i, 0))
    slab = pl.BlockSpec((N_SLAB, tb, SLAB_IN), lambda i: (0, i, 0))
    const = lambda r, c: pl.BlockSpec((r, c), lambda i: (0, 0))
    return pl.pallas_call(
        body, name="proj_bwd", grid=(seq // tb,),
        in_specs=[tok(1024), tok(512), tok(128), tok(128), tok(512), slab, slab, slab, tok(512), tok(1024),
                  const(1, 512), tok(128), tok(128), const(D_IN_PROJ, D_MODEL)],
        out_specs=[tok(1024), const(D_IN_PROJ, D_MODEL)],
        out_shape=[jax.ShapeDtypeStruct((seq, D_MODEL), F32), jax.ShapeDtypeStruct((D_IN_PROJ, D_MODEL), F32)],
        scratch_shapes=[pltpu.VMEM((D_IN_PROJ, D_MODEL), BF16)],
        compiler_params=_cparams(("arbitrary",)),
    )(x, dq, dk, dv, dza, du_f, du_b, dylin, dzs, dpre, ssm_d, cos128, sin128, wt)


def _adamw(w, g, m, v, name):
    rows, cols = w.shape
    tb = rows
    while tb * cols * 4 > ADAMW_BLOCK_BYTES and tb % 16 == 0:
        tb //= 2

    def body(w_ref, g_ref, m_ref, v_ref, d_ref, nm_ref, nv_ref):
        _adamw_update(w_ref, g_ref, m_ref, v_ref, d_ref, nm_ref, nv_ref)

    spec = pl.BlockSpec((tb, cols), lambda i: (i, 0))
    return pl.pallas_call(
        body, name=name, grid=(rows // tb,), in_specs=[spec] * 4, out_specs=[spec] * 3,
        out_shape=[jax.ShapeDtypeStruct((rows, cols), F32)] * 3,
        compiler_params=_cparams(("arbitrary",)),
    )(w, g, m, v)


def _adamw_update(w_ref, g_ref, m_ref, v_ref, d_ref, nm_ref, nv_ref):
    g_blk = g_ref[...]
    m_new = ADAM_B1 * m_ref[...] + (1.0 - ADAM_B1) * g_blk
    v_new = ADAM_B2 * v_ref[...] + (1.0 - ADAM_B2) * (g_blk * g_blk)
    m_hat = m_new / (1.0 - ADAM_B1 ** ADAM_STEP)
    v_hat = v_new / (1.0 - ADAM_B2 ** ADAM_STEP)
    d_ref[...] = -ADAM_LR * (m_hat / (jnp.sqrt(v_hat) + ADAM_EPS) + ADAM_WD * w_ref[...])
    nm_ref[...] = m_new
    nv_ref[...] = v_new


def _adamw_many(groups, name):
    n = len(groups)

    def body(*refs):
        for p in range(n):
            _adamw_update(*refs[4 * p:4 * p + 4], *refs[4 * n + 3 * p:4 * n + 3 * p + 3])

    return pl.pallas_call(
        body, name=name,
        out_shape=[jax.ShapeDtypeStruct(grp[0].shape, F32) for grp in groups for _ in range(3)],
    )(*[a for grp in groups for a in grp])


_WEIGHTS = ["w_in", "attn_sink", "ssm_a_re", "ssm_a_im", "ssm_log_dt", "ssm_b_re", "ssm_b_im", "ssm_c_re", "ssm_c_im",
            "ssm_d", "w_glu", "b_glu", "norm_attn_g", "norm_ssm_g", "w_out", "ln_g", "ln_b"]
N_DG = N_DIR * N_GROUPS
BIG_ROWS = N_DG * SSM_CH * SSM_STATE // 128
TINY_ROWS = 48


def _slabs_wide(t):
    eye = jnp.eye(8, dtype=t.dtype)
    return jnp.einsum("rdkgcp,gh->rdkgchp", t.reshape(2, N_DIR, N_SLAB, 8, SSM_CH, SSM_STATE), eye).reshape(
        2, N_DIR, N_SLAB, SLAB_IN, SLAB_ST)


def _slabs_tall(t):
    eye = jnp.eye(8, dtype=t.dtype)
    return jnp.einsum("rdkgcp,gh->rdkhpgc", t.reshape(2, N_DIR, N_SLAB, 8, SSM_CH, SSM_STATE), eye).reshape(
        2, N_DIR, N_SLAB, SLAB_ST, SLAB_IN)


def _wide_diagonal(t):
    eye = jnp.eye(8, dtype=t.dtype)
    return jnp.einsum("rdkgchp,gh->rdkgcp", t.reshape(2, N_DIR, N_SLAB, 8, SSM_CH, 8, SSM_STATE), eye).reshape(
        2, N_DIR, N_GROUPS, SSM_CH, SSM_STATE)


def _tall_diagonal(t):
    eye = jnp.eye(8, dtype=t.dtype)
    return jnp.einsum("rdkhpgc,gh->rdkgcp", t.reshape(2, N_DIR, N_SLAB, 8, SSM_STATE, 8, SSM_CH), eye).reshape(
        2, N_DIR, N_GROUPS, SSM_CH, SSM_STATE)


def _pack_small_grads(g_b, g_c, g_vec, g_ar, g_ai, g_dt, g_sink):
    big = jnp.concatenate([g_b.reshape(2, BIG_ROWS, 128), g_c.reshape(2, BIG_ROWS, 128)], axis=0)
    row = lambda t: jnp.pad(t.reshape(1, -1), ((0, 0), (0, 128 - t.size)))
    tiny = jnp.concatenate([g_vec.reshape(64, 128), g_ar.reshape(32, 128), g_ai.reshape(32, 128), row(g_dt), row(g_sink),
                            jnp.zeros((N_CHIPS * TINY_ROWS - 130, 128), F32)], axis=0)
    return jnp.concatenate([big, tiny.reshape(N_CHIPS, TINY_ROWS, 128)], axis=1)


def _unpack_small_grads(packed):
    big = packed[:, :BIG_ROWS].reshape(N_CHIPS, 2 * BIG_ROWS, SSM_STATE)
    tiny = packed[:, BIG_ROWS:].reshape(N_CHIPS * TINY_ROWS, 128)
    g_vec = tiny[0:64].reshape(8, 1024)
    return {
        "ssm_b_re": big[0], "ssm_b_im": big[1], "ssm_c_re": big[2], "ssm_c_im": big[3],
        "ln_g": g_vec[0:1], "ln_b": g_vec[1:2],
        "norm_attn_g": g_vec[2:3, :D_ATTN][:, _PAIR_INV], "norm_ssm_g": g_vec[2:3, D_ATTN:],
        "ssm_d": g_vec[3:4, :D_SSM], "b_glu": g_vec[3:4, D_SSM:],
        "ssm_a_re": tiny[64:96].reshape(N_DG, SSM_STATE), "ssm_a_im": tiny[96:128].reshape(N_DG, SSM_STATE),
        "ssm_log_dt": tiny[128:129, :N_DG].reshape(N_DIR, N_GROUPS), "attn_sink": tiny[129:130, :N_Q_HEADS],
    }


def _small_view(name, t):
    if name in ("ssm_b_re", "ssm_b_im"):
        return jnp.swapaxes(t[0], 2, 3).reshape(N_DG * SSM_CH, SSM_STATE)
    if name in ("ssm_c_re", "ssm_c_im"):
        return t.reshape(N_DG * SSM_CH, SSM_STATE)
    if name in ("ssm_a_re", "ssm_a_im"):
        return t.reshape(N_DG, SSM_STATE)
    if name == "ssm_log_dt":
        return t.reshape(N_DIR, N_GROUPS)
    return t.reshape(1, -1)


def _small_unview(name, t, shape):
    if name in ("ssm_b_re", "ssm_b_im"):
        return jnp.swapaxes(t.reshape(N_DIR, N_GROUPS, SSM_CH, SSM_STATE), 2, 3).reshape(shape)
    return t.reshape(shape)


def kernel(x, w_in, attn_sink, ssm_a_re, ssm_a_im, ssm_log_dt, ssm_b_re, ssm_b_im, ssm_c_re, ssm_c_im, ssm_d, w_glu, b_glu, norm_attn_g, norm_ssm_g, w_out, ln_g, ln_b, loss_target, m_w_in, m_attn_sink, m_ssm_a_re, m_ssm_a_im, m_ssm_log_dt, m_ssm_b_re, m_ssm_b_im, m_ssm_c_re, m_ssm_c_im, m_ssm_d, m_w_glu, m_b_glu, m_norm_attn_g, m_norm_ssm_g, m_w_out, m_ln_g, m_ln_b, v_w_in, v_attn_sink, v_ssm_a_re, v_ssm_a_im, v_ssm_log_dt, v_ssm_b_re, v_ssm_b_im, v_ssm_c_re, v_ssm_c_im, v_ssm_d, v_w_glu, v_b_glu, v_norm_attn_g, v_norm_ssm_g, v_w_out, v_ln_g, v_ln_b):
    args = dict(locals())
    weights = {n: args[n] for n in _WEIGHTS}
    mom_m = {n: args["m_" + n] for n in _WEIGHTS}
    mom_v = {n: args["v_" + n] for n in _WEIGHTS}
    xs = x[0]
    target = loss_target[0]

    wt_g, w_glu_g, w_out_g = _all_gather_chips([w_in[0].T, w_glu[0], w_out[0]], BF16, "gather_weights")
    wt_full = wt_g.reshape(D_IN_PROJ, D_MODEL)
    w_glu_full = w_glu_g.reshape(D_SSM, D_SSM)
    w_out_full = w_out_g.reshape(D_MODEL, D_MODEL)

    loss_local, g_x, g_wt, g_w_out, g_w_glu, g_small = _local_step(
        xs, target, wt_full, w_glu_full, w_out_full, attn_sink, ssm_a_re, ssm_a_im, ssm_log_dt, ssm_b_re, ssm_b_im,
        ssm_c_re, ssm_c_im, ssm_d, b_glu, norm_attn_g, norm_ssm_g, ln_g, ln_b)
    loss = lax.psum(loss_local, ("x", "y", "c"))

    r_wt = _reduce_scatter_chips(g_wt.reshape(N_CHIPS, -1, D_MODEL), "reduce_w_in")
    r_w_out = _reduce_scatter_chips(g_w_out.reshape(N_CHIPS, -1, D_MODEL), "reduce_w_out")
    r_w_glu = _reduce_scatter_chips(g_w_glu.reshape(N_CHIPS, -1, D_SSM), "reduce_w_glu")
    r_small = _reduce_scatter_chips(g_small, "reduce_small")
    (g_small_all,) = _all_gather_chips([r_small], F32, "gather_small")
    small_grads = _unpack_small_grads(g_small_all)

    grads, deltas, new_m, new_v = {}, {}, {}, {}
    d_w, m_w, v_w = _adamw(w_in[0].T, r_wt, m_w_in[0].T, v_w_in[0].T, "adamw_w_in")
    grads["w_in"], deltas["w_in"], new_m["w_in"], new_v["w_in"] = r_wt.T[None], d_w.T[None], m_w.T[None], v_w.T[None]
    for n, g in (("w_out", r_w_out), ("w_glu", r_w_glu)):
        d_w, m_w, v_w = _adamw(weights[n][0], g, mom_m[n][0], mom_v[n][0], "adamw_" + n)
        grads[n], deltas[n], new_m[n], new_v[n] = g[None], d_w[None], m_w[None], v_w[None]
    names = sorted(small_grads)
    updates = _adamw_many([(_small_view(n, weights[n]), small_grads[n], _small_view(n, mom_m[n]), _small_view(n, mom_v[n]))
                           for n in names], "adamw_small")
    for i, n in enumerate(names):
        shape = weights[n].shape
        grads[n] = _small_unview(n, small_grads[n], shape)
        deltas[n], new_m[n], new_v[n] = (_small_unview(n, t, shape) for t in updates[3 * i:3 * i + 3])

    return (loss, g_x[None], *[grads[n] for n in _WEIGHTS], *[deltas[n] for n in _WEIGHTS],
            *[new_m[n] for n in _WEIGHTS], *[new_v[n] for n in _WEIGHTS])


def _local_step(xs, target, wt_full, w_glu_full, w_out_full, attn_sink, ssm_a_re, ssm_a_im, ssm_log_dt, ssm_b_re,
                ssm_b_im, ssm_c_re, ssm_c_im, ssm_d, b_glu, norm_attn_g, norm_ssm_g, ln_g, ln_b):
    seq = xs.shape[0]

    a_r = ssm_a_re.reshape(N_DG, 1, SSM_STATE)
    a_i = ssm_a_im.reshape(N_DG, 1, SSM_STATE)
    log_dt = ssm_log_dt.reshape(N_DG, 1, 1)
    b_r = jnp.swapaxes(ssm_b_re[0], 2, 3).reshape(N_DG, SSM_CH, SSM_STATE)
    b_i = jnp.swapaxes(ssm_b_im[0], 2, 3).reshape(N_DG, SSM_CH, SSM_STATE)
    ssm_tb = min(256, seq)
    sub_len = ssm_tb // SUBSEG
    lam, bbar = _ssm_params_fwd(a_r, a_i, log_dt, b_r, b_i, int(math.log2(sub_len)))
    lam = lam.reshape(4, N_DIR, 1, STATE_W)
    bbar = bbar.astype(BF16).reshape(2, N_DIR, N_GROUPS, SSM_CH, SSM_STATE)
    c_both = jnp.stack([ssm_c_re[0], ssm_c_im[0]]).astype(BF16)
    bb, bbt = _slabs_wide(bbar), _slabs_tall(bbar)
    cb, cb_t = _slabs_tall(c_both), _slabs_wide(c_both)

    cos128, sin128 = _rope_tables(seq)
    q_stack, k_rot, v_bf, z_attn, u, z_ssm = _proj(xs, wt_full, cos128, sin128, min(512, seq))
    sink_rows = jnp.repeat(attn_sink[0], ATT_TQ)[:, None]
    attn_bias = _attn_bias()
    o = _attn_fwd(q_stack, k_rot, v_bf, sink_rows, attn_bias)
    ys, starts = [], []
    for d in range(N_DIR):
        y_d, s_r, s_i = _ssm_fwd(u, lam, bb, cb, direction=d, tb=ssm_tb, name=f"ssm_fwd_{d}")
        ys.append(y_d)
        starts.append((s_r, s_i))

    row = lambda t: t.reshape(1, -1)
    g_attn_p = row(norm_attn_g)[:, _PAIR_PERM]
    loss_blk, d_o, d_za, d_ylin, d_zs, d_pre, g_w_out, g_w_glu, g_vec = _mid(
        o, z_attn, u, ys[0], ys[1], z_ssm, xs, target, row(ssm_d), w_glu_full, row(b_glu),
        g_attn_p, row(norm_ssm_g), w_out_full, row(ln_g), row(ln_b), min(256, seq))

    dq, dk, dv, g_sink = _attn_bwd(q_stack, k_rot, v_bf, sink_rows, attn_bias, d_o)
    dus, g_bb, g_cb, g_lam = [], [], [], []
    for d in range(N_DIR):
        du_d, gb_d, gc_d, dl_d = _ssm_bwd(u, d_ylin, starts[d], lam, bb, bbt, cb_t, direction=d, tb=ssm_tb,
                                          name=f"ssm_bwd_{d}")
        dus.append(du_d)
        g_bb.append(gb_d)
        g_cb.append(gc_d)
        g_lam.append(dl_d)
    g_bbar = _wide_diagonal(jnp.stack(g_bb, axis=1)).reshape(2, N_DG, SSM_CH, SSM_STATE)
    g_c = _tall_diagonal(jnp.stack(g_cb, axis=1))
    g_lam = jnp.swapaxes(jnp.stack(g_lam, axis=1).reshape(2, N_DIR, SUBSEG, N_GROUPS, SSM_STATE), 2, 3).reshape(
        2, N_DG, SUBSEG, SSM_STATE)
    g_ar, g_ai, g_dt, g_br, g_bi = _ssm_params_bwd(a_r, a_i, log_dt, b_r, b_i, g_lam[0], g_lam[1], g_bbar[0], g_bbar[1])

    g_x, g_wt = _proj_bwd(xs, dq, dk, dv, d_za, dus[0], dus[1], d_ylin, d_zs, d_pre, row(ssm_d), cos128, sin128,
                          wt_full, min(256, seq))

    g_b = jnp.stack([g_br, g_bi]).reshape(2, N_DIR, N_GROUPS, SSM_CH, SSM_STATE)
    g_small = _pack_small_grads(g_b, g_c, g_vec, g_ar, g_ai, g_dt, g_sink[:, 0])
    return loss_blk[0, 0], g_x, g_wt, g_w_out, g_w_glu, g_small
```

```python
import functools
import math

import numpy as np
import jax
import jax.numpy as jnp
from jax import lax
from jax.experimental import pallas as pl
from jax.experimental.pallas import tpu as pltpu

F32 = jnp.float32
BF16 = jnp.bfloat16
MESH = pl.DeviceIdType.MESH

D_MODEL = 1024
D_ATTN = 512
D_SSM = 512
HEAD_DIM = 64
N_Q_HEADS = 8
WINDOW = 128
ROPE_THETA = 10000.0
SSM_CH = 16
N_GROUPS = 32
SSM_STATE = 64
N_DIR = 2
STATE_W = N_GROUPS * SSM_STATE
N_SLAB = 4
SLAB_IN = 128
SLAB_ST = 512
NORM_EPS = 1e-5
NEG_INF = -1e30
ALPHA = 2.0 ** 0.25
D_IN_PROJ = 2304
N_CHIPS = 4

ADAM_LR = 0.001
ADAM_B1 = 0.9
ADAM_B2 = 0.999
ADAM_EPS = 1e-08
ADAM_WD = 0.01
ADAM_STEP = 10

SUBSEG = 8
SCAN_LANES = 512
SCAN_UNROLL = 4
VMEM_LIMIT = 48 * 1024 * 1024
ADAMW_BLOCK_BYTES = 3 * 512 * 1024

_PAIR_PERM = np.array([(64 * j + l) if l < 64 else (64 * (j + 4) + l - 64) for j in range(4) for l in range(128)])
_PAIR_INV = np.argsort(_PAIR_PERM)


def _cparams(sem=None):
    return pltpu.CompilerParams(dimension_semantics=sem, vmem_limit_bytes=VMEM_LIMIT)


def _dot(a, b):
    return jnp.dot(a, b, preferred_element_type=F32)


def _dot_nt(a, b):
    return lax.dot_general(a, b, (((1,), (1,)), ((), ())), preferred_element_type=F32)


def _dot_tn(a, b):
    return lax.dot_general(a, b, (((0,), (0,)), ((), ())), preferred_element_type=F32)


def _sigmoid(z):
    return 1.0 / (1.0 + jnp.exp(-z))


def _all_gather_chips(shards, out_dtype, name):
    n = len(shards)

    def body(*refs):
        in_refs, out_refs = refs[:n], refs[n:2 * n]
        send_sems, recv_sems = refs[2 * n:]
        x, y, c = lax.axis_index("x"), lax.axis_index("y"), lax.axis_index("c")
        sibling = (x, y, 1 - c)
        chips = [(1 - x, y), (x, 1 - y), (1 - x, 1 - y)]

        for a in range(n):
            out_refs[a][2 * x + y] = in_refs[a][...].astype(out_dtype)

        def half_of(a, px, py, half):
            rows = in_refs[a].shape[0] // 2
            return out_refs[a].at[2 * px + py, pl.ds(half * rows, rows), :]

        def copy(a, k, px, py, half, to):
            blk = half_of(a, px, py, half)
            return pltpu.make_async_remote_copy(src_ref=blk, dst_ref=blk, send_sem=send_sems.at[6 * a + k],
                                                recv_sem=recv_sems.at[6 * a + k], device_id=to, device_id_type=MESH)

        first = [copy(a, j, x, y, c, (*chips[j], c)) for a in range(n) for j in range(3)]
        for cp in first:
            cp.start()
        passed = []
        for a in range(n):
            for j in range(3):
                copy(a, j, *chips[j], c, (x, y, c)).wait_recv()
                fwd = copy(a, 3 + j, *chips[j], c, sibling)
                fwd.start()
                passed.append(fwd)
        for a in range(n):
            for j in range(3):
                copy(a, 3 + j, *chips[j], 1 - c, (x, y, c)).wait_recv()
        for cp in first + passed:
            cp.wait_send()

    vmem = pl.BlockSpec(memory_space=pltpu.VMEM)
    return pl.pallas_call(
        body, name=name,
        out_shape=[jax.ShapeDtypeStruct((N_CHIPS,) + s.shape, out_dtype) for s in shards],
        in_specs=[vmem] * n, out_specs=[vmem] * n,
        scratch_shapes=[pltpu.SemaphoreType.DMA((6 * n,)), pltpu.SemaphoreType.DMA((6 * n,))],
        compiler_params=pltpu.CompilerParams(vmem_limit_bytes=VMEM_LIMIT),
    )(*shards)


def _reduce_scatter_chips(pieces, name):
    _, rows, cols = pieces.shape
    h = rows // 2

    def body(p_ref, out_ref, a_ref, b_ref, send_sems, recv_sems):
        x, y, c = lax.axis_index("x"), lax.axis_index("y"), lax.axis_index("c")
        me = 2 * x + y
        sibling = (x, y, 1 - c)
        chips = [(1 - x, y), (x, 1 - y), (1 - x, 1 - y)]
        mine = pl.multiple_of(c * h, 8)
        other = pl.multiple_of((1 - c) * h, 8)

        swap = pltpu.make_async_remote_copy(src_ref=p_ref.at[:, pl.ds(other, h), :], dst_ref=a_ref,
                                            send_sem=send_sems.at[0], recv_sem=recv_sems.at[0],
                                            device_id=sibling, device_id_type=MESH)
        swap.start()
        swap.wait_recv()
        for k in range(N_CHIPS):
            a_ref[k] = a_ref[k] + p_ref[k, pl.ds(mine, h), :]

        def chip_copy(j, dst_slot):
            px, py = chips[j]
            return pltpu.make_async_remote_copy(src_ref=a_ref.at[2 * px + py], dst_ref=b_ref.at[dst_slot],
                                                send_sem=send_sems.at[1 + j], recv_sem=recv_sems.at[1 + j],
                                                device_id=(px, py, c), device_id_type=MESH)

        sends = [chip_copy(j, me) for j in range(3)]
        for cp in sends:
            cp.start()
        b_ref[me] = a_ref[me]
        for j in range(3):
            chip_copy(j, 2 * chips[j][0] + chips[j][1]).wait_recv()
        out_ref[pl.ds(mine, h), :] = (b_ref[0] + b_ref[1]) + (b_ref[2] + b_ref[3])

        back = pltpu.make_async_remote_copy(src_ref=out_ref.at[pl.ds(mine, h), :], dst_ref=out_ref.at[pl.ds(mine, h), :],
                                            send_sem=send_sems.at[4], recv_sem=recv_sems.at[4],
                                            device_id=sibling, device_id_type=MESH)
        back.start()
        pltpu.make_async_remote_copy(src_ref=out_ref.at[pl.ds(other, h), :], dst_ref=out_ref.at[pl.ds(other, h), :],
                                     send_sem=send_sems.at[4], recv_sem=recv_sems.at[4],
                                     device_id=sibling, device_id_type=MESH).wait_recv()
        swap.wait_send()
        for cp in sends:
            cp.wait_send()
        back.wait_send()

    vmem = pl.BlockSpec(memory_space=pltpu.VMEM)
    return pl.pallas_call(
        body, name=name,
        out_shape=jax.ShapeDtypeStruct((rows, cols), F32),
        in_specs=[vmem], out_specs=vmem,
        scratch_shapes=[pltpu.VMEM((N_CHIPS, h, cols), F32), pltpu.VMEM((N_CHIPS, h, cols), F32),
                        pltpu.SemaphoreType.DMA((5,)), pltpu.SemaphoreType.DMA((5,))],
        compiler_params=pltpu.CompilerParams(vmem_limit_bytes=VMEM_LIMIT),
    )(pieces)


def _ssm_param_values(ar, ai, logdt):
    dt = jnp.exp(logdt)
    mag = jnp.exp(dt * ar)
    cs, sn = jnp.cos(dt * ai), jnp.sin(dt * ai)
    lr, li = mag * cs, mag * sn
    den = ar * ar + ai * ai
    nr = (lr - 1.0) * ar + li * ai
    ni = li * ar - (lr - 1.0) * ai
    return dt, mag, lr, li, den, nr, ni


def _ssm_params_fwd(ar, ai, logdt, br, bi, n_square):
    def body(ar_ref, ai_ref, dt_ref, br_ref, bi_ref, lam_ref, bb_ref):
        _, _, lr, li, den, nr, ni = _ssm_param_values(ar_ref[...], ai_ref[...], dt_ref[...])
        lam_ref[0] = lr
        lam_ref[1] = li
        pr, pi = lr, li
        for _ in range(n_square):
            pr, pi = pr * pr - pi * pi, 2.0 * pr * pi
        lam_ref[2] = pr
        lam_ref[3] = pi
        fr, fi = nr / den, ni / den
        b_r, b_i = br_ref[...], bi_ref[...]
        bb_ref[0] = fr * b_r - fi * b_i
        bb_ref[1] = fr * b_i + fi * b_r

    return pl.pallas_call(body, name="ssm_params_fwd",
                          out_shape=[jax.ShapeDtypeStruct((4,) + ar.shape, F32),
                                     jax.ShapeDtypeStruct((2,) + br.shape, F32)])(ar, ai, logdt, br, bi)


def _ssm_params_bwd(ar, ai, logdt, br, bi, dlam_r, dlam_i, dbb_r, dbb_i):
    def body(ar_ref, ai_ref, dt_ref, br_ref, bi_ref, dlr_ref, dli_ref, dbr_ref, dbi_ref,
             gar_ref, gai_ref, gdt_ref, gbr_ref, gbi_ref):
        a_r, a_i = ar_ref[...], ai_ref[...]
        dt, mag, lr, li, den, nr, ni = _ssm_param_values(a_r, a_i, dt_ref[...])
        fr, fi = nr / den, ni / den
        b_r, b_i = br_ref[...], bi_ref[...]
        g_r, g_i = dbr_ref[...], dbi_ref[...]
        gbr_ref[...] = fr * g_r + fi * g_i
        gbi_ref[...] = fr * g_i - fi * g_r
        d_fr = jnp.sum(b_r * g_r + b_i * g_i, axis=1, keepdims=True)
        d_fi = jnp.sum(b_r * g_i - b_i * g_r, axis=1, keepdims=True)
        d_nr, d_ni = d_fr / den, d_fi / den
        d_den = -(d_fr * nr + d_fi * ni) / (den * den)
        d_lr = jnp.sum(dlr_ref[...], axis=1, keepdims=True) + d_nr * a_r - d_ni * a_i
        d_li = jnp.sum(dli_ref[...], axis=1, keepdims=True) + d_nr * a_i + d_ni * a_r
        d_ar = d_nr * (lr - 1.0) + d_ni * li + d_den * 2.0 * a_r
        d_ai = d_nr * li - d_ni * (lr - 1.0) + d_den * 2.0 * a_i
        d_mag = (d_lr * lr + d_li * li) / mag
        d_theta = d_li * lr - d_lr * li
        gar_ref[...] = d_ar + d_mag * mag * dt
        gai_ref[...] = d_ai + d_theta * dt
        d_dt = d_mag * mag * a_r + d_theta * a_i
        gdt_ref[...] = jnp.sum(d_dt, axis=2, keepdims=True) * dt

    small = jax.ShapeDtypeStruct(ar.shape, F32)
    return pl.pallas_call(
        body, name="ssm_params_bwd",
        out_shape=[small, small, jax.ShapeDtypeStruct(logdt.shape, F32),
                   jax.ShapeDtypeStruct(br.shape, F32), jax.ShapeDtypeStruct(br.shape, F32)],
    )(ar, ai, logdt, br, bi, dlam_r, dlam_i, dbb_r, dbb_i)


def _rope_tables(seq):
    half = HEAD_DIM // 2
    inv_freq = ROPE_THETA ** (-jnp.arange(half, dtype=F32) / half)
    ang = jnp.arange(seq, dtype=jnp.int32).astype(F32)[:, None] * inv_freq[None, :]
    cos, sin = jnp.cos(ang), jnp.sin(ang)
    cos128 = jnp.concatenate([cos, cos, cos, cos], axis=1)
    sin128 = jnp.concatenate([-sin, sin, -sin, sin], axis=1)
    return cos128, sin128


def _rotate_half_unsigned(t):
    lane = lax.broadcasted_iota(jnp.int32, t.shape, 1)
    return jnp.where((lane % HEAD_DIM) < HEAD_DIM // 2, pltpu.roll(t, 96, 1), pltpu.roll(t, 32, 1))


def _rope(t, cos, sin_signed):
    return t * cos + _rotate_half_unsigned(t) * sin_signed


def _pair_blocks(base):
    out = []
    for j in range(4):
        for g in range(2):
            nat = base + HEAD_DIM * (4 * g + j)
            par = base + 128 * j + HEAD_DIM * g
            out.append((slice(nat, nat + HEAD_DIM), slice(par, par + HEAD_DIM)))
    return out


W_Q, W_KV, W_ZA, W_U, W_ZS = 0, 512, 768, 1280, 1792


def _proj(x, wt, cos128, sin128, tb):
    seq = x.shape[0]

    def body(x_ref, wt_ref, cos_ref, sin_ref, q_ref, k_ref, v_ref, za_ref, u_ref, zs_ref, wp):
        @pl.when(pl.program_id(0) == 0)
        def _():
            for dst_base, src_base in ((0, W_Q), (512, W_ZA)):
                for nat, par in _pair_blocks(0):
                    wp[dst_base + par.start:dst_base + par.stop, :] = wt_ref[src_base + nat.start:src_base + nat.stop, :]

        xb = x_ref[...].astype(BF16)
        cos, sin = cos_ref[...], sin_ref[...]
        lo = lax.broadcasted_iota(jnp.int32, (tb, 128), 1) < HEAD_DIM
        q = _dot_nt(xb, wp[0:512, :])
        for j in range(4):
            qj = _rope(q[:, 128 * j:128 * (j + 1)], cos, sin)
            q_ref[j] = jnp.where(lo, qj, 0.0).astype(BF16)
            q_ref[4 + j] = jnp.where(lo, 0.0, qj).astype(BF16)
        kv = _dot_nt(xb, wt_ref[W_KV:W_ZA, :])
        k_ref[...] = _rope(kv[:, 0:128], cos, sin).astype(BF16)
        v_ref[...] = kv[:, 128:256].astype(BF16)
        za_ref[...] = _dot_nt(xb, wp[512:1024, :])
        u_val = _dot_nt(xb, wt_ref[W_U:W_ZS, :])
        for k in range(N_SLAB):
            u_ref[k] = u_val[:, k * SLAB_IN:(k + 1) * SLAB_IN]
        zs_ref[...] = _dot_nt(xb, wt_ref[W_ZS:D_IN_PROJ, :])

    row = lambda w: pl.BlockSpec((tb, w), lambda i: (i, 0))
    return pl.pallas_call(
        body, name="proj", grid=(seq // tb,),
        in_specs=[row(D_MODEL), pl.BlockSpec((D_IN_PROJ, D_MODEL), lambda i: (0, 0)), row(128), row(128)],
        out_specs=[pl.BlockSpec((8, tb, 128), lambda i: (0, i, 0)), row(128), row(128), row(512),
                   pl.BlockSpec((N_SLAB, tb, SLAB_IN), lambda i: (0, i, 0)), row(512)],
        out_shape=[jax.ShapeDtypeStruct((8, seq, 128), BF16), jax.ShapeDtypeStruct((seq, 128), BF16),
                   jax.ShapeDtypeStruct((seq, 128), BF16), jax.ShapeDtypeStruct((seq, 512), F32),
                   jax.ShapeDtypeStruct((N_SLAB, seq, SLAB_IN), F32), jax.ShapeDtypeStruct((seq, 512), F32)],
        scratch_shapes=[pltpu.VMEM((1024, D_MODEL), BF16)],
        compiler_params=_cparams(("arbitrary",)),
    )(x, wt, cos128, sin128)


ATT_TQ = 128
ATT_KEYS = 3 * ATT_TQ


def _attn_window(i, seq):
    start = jnp.clip(i * ATT_TQ - WINDOW, 0, seq - ATT_KEYS)
    return pl.multiple_of(start, ATT_TQ)


def _attn_bias():
    r = jnp.arange(ATT_TQ, dtype=jnp.int32)[None, :, None]
    c = jnp.arange(ATT_KEYS, dtype=jnp.int32)[None, None, :]
    off = (jnp.arange(3, dtype=jnp.int32) * ATT_TQ)[:, None, None]
    return jnp.where(jnp.abs(r + off - c) <= WINDOW, 0.0, NEG_INF).astype(F32)


def _attn_bias_spec(nblk):
    pick = lambda i: jnp.where(i == 0, 0, jnp.where(i == nblk - 1, 2, 1))
    return pl.BlockSpec((None, ATT_TQ, ATT_KEYS), lambda i: (pick(i), 0, 0))


def _attn_softmax(q_ref, k_ref, v_ref, sink_ref, bias_ref, start):
    kw = k_ref[pl.ds(start, ATT_KEYS), :]
    vw = v_ref[pl.ds(start, ATT_KEYS), :]
    qall = q_ref[...].reshape(N_Q_HEADS * ATT_TQ, 128)
    s = (_dot_nt(qall, kw) * (HEAD_DIM ** -0.5)).reshape(N_Q_HEADS, ATT_TQ, ATT_KEYS) + bias_ref[...][None]
    tiles = [s[:, :, 128 * t:128 * (t + 1)] for t in range(ATT_KEYS // 128)]
    m = jnp.max(jnp.maximum(jnp.maximum(tiles[0], tiles[1]), tiles[2]), axis=2, keepdims=True)
    sink = sink_ref[...]
    m_b = jnp.maximum(jnp.broadcast_to(m, (N_Q_HEADS, ATT_TQ, 128)), sink)
    p = jnp.concatenate([jnp.exp(t - m_b) for t in tiles], axis=2)
    p_sink = jnp.exp(sink - m_b)
    lo_k = lax.broadcasted_iota(jnp.int32, (ATT_KEYS, 128), 1) < HEAD_DIM
    v_f = vw.astype(F32)
    v_lo, v_hi = jnp.where(lo_k, v_f, 1.0).astype(BF16), jnp.where(lo_k, 1.0, v_f).astype(BF16)
    pb = p.astype(BF16).reshape(N_Q_HEADS * ATT_TQ, ATT_KEYS)
    half = 4 * ATT_TQ
    r = jnp.concatenate([_dot(pb[:half], v_lo), _dot(pb[half:], v_hi)], axis=0).reshape(N_Q_HEADS, ATT_TQ, 128)
    return kw, vw, qall, p, p_sink, r


def _attn_fwd(q_stack, k, v, sink128, bias):
    seq = k.shape[0]

    def body(q_ref, k_ref, v_ref, sink_ref, bias_ref, o_ref):
        start = _attn_window(pl.program_id(0), seq)
        _, _, _, _, p_sink, r = _attn_softmax(q_ref, k_ref, v_ref, sink_ref, bias_ref, start)
        out = r / (pltpu.roll(r, HEAD_DIM, 2) + p_sink)
        lo = lax.broadcasted_iota(jnp.int32, (ATT_TQ, 128), 1) < HEAD_DIM
        for j in range(4):
            o_ref[:, 128 * j:128 * (j + 1)] = jnp.where(lo, out[j], out[4 + j])

    full = lambda w: pl.BlockSpec((seq, w), lambda i: (0, 0))
    return pl.pallas_call(
        body, name="attn_fwd", grid=(seq // ATT_TQ,),
        in_specs=[pl.BlockSpec((8, ATT_TQ, 128), lambda i: (0, i, 0)), full(128), full(128),
                  pl.BlockSpec((N_Q_HEADS, 1, 128), lambda i: (0, 0, 0)), _attn_bias_spec(seq // ATT_TQ)],
        out_specs=pl.BlockSpec((ATT_TQ, 512), lambda i: (i, 0)),
        out_shape=jax.ShapeDtypeStruct((seq, 512), F32),
        compiler_params=_cparams(("arbitrary",)),
    )(q_stack, k, v, sink128, bias)


def _attn_bwd(q_stack, k, v, sink128, bias, d_o):
    seq = k.shape[0]

    def body(q_ref, k_ref, v_ref, sink_ref, bias_ref, do_ref, dq_ref, dk_ref, dv_ref, dsink_ref, sink_acc):
        i = pl.program_id(0)

        @pl.when(i == 0)
        def _():
            dk_ref[...] = jnp.zeros_like(dk_ref)
            dv_ref[...] = jnp.zeros_like(dv_ref)
            sink_acc[...] = jnp.zeros_like(sink_acc)

        start = _attn_window(i, seq)
        kw, vw, qall, p, p_sink, r = _attn_softmax(q_ref, k_ref, v_ref, sink_ref, bias_ref, start)
        lo = lax.broadcasted_iota(jnp.int32, (ATT_TQ, 128), 1) < HEAD_DIM
        lo3 = lo[None]
        grp0 = lax.broadcasted_iota(jnp.int32, (N_Q_HEADS, ATT_TQ, 128), 0) < 4
        val = grp0 == lo3
        swapped = pltpu.roll(r, HEAD_DIM, 2)
        inv = 1.0 / (jnp.where(val, swapped, r) + p_sink)
        d_o_blk = do_ref[...]
        do3 = jnp.where(val, jnp.concatenate([d_o_blk[None, :, 128 * j:128 * (j + 1)] for j in range(4)] * 2, axis=0), 0.0)
        t = (do3 * r).reshape(N_Q_HEADS * ATT_TQ, 128)
        t_hi = t.astype(BF16)
        t_lo = (t - t_hi.astype(F32)).astype(BF16)
        ones = jnp.ones((128, 128), BF16)
        delta = (_dot(t_hi, ones) + _dot(t_lo, ones)).reshape(N_Q_HEADS, ATT_TQ, 128) * inv
        sink_acc[...] += -(p_sink * inv) * delta
        do_all = do3.astype(BF16).reshape(N_Q_HEADS * ATT_TQ, 128)
        dp = _dot_nt(do_all, vw).reshape(N_Q_HEADS, ATT_TQ, ATT_KEYS)
        probs, ds = [], []
        for tl in range(ATT_KEYS // 128):
            cols = slice(128 * tl, 128 * (tl + 1))
            probs_t = p[:, :, cols] * inv
            probs.append(probs_t.astype(BF16))
            ds.append((probs_t * (dp[:, :, cols] - delta)).astype(BF16))
        probs_all = jnp.concatenate(probs, axis=2).reshape(N_Q_HEADS * ATT_TQ, ATT_KEYS)
        ds_all = jnp.concatenate(ds, axis=2).reshape(N_Q_HEADS * ATT_TQ, ATT_KEYS)
        scale = HEAD_DIM ** -0.5
        dq_all = (_dot(ds_all, kw) * scale).reshape(N_Q_HEADS, ATT_TQ, 128)
        for j in range(4):
            dq_ref[:, 128 * j:128 * (j + 1)] = jnp.where(lo, dq_all[j], dq_all[4 + j])
        dk_ref[pl.ds(start, ATT_KEYS), :] += _dot_tn(ds_all, qall) * scale
        dv_ref[pl.ds(start, ATT_KEYS), :] += _dot_tn(probs_all, do_all)

        @pl.when(i == pl.num_programs(0) - 1)
        def _():
            dsink_ref[...] = jnp.sum(sink_acc[...], axis=1)

    full = lambda w: pl.BlockSpec((seq, w), lambda i: (0, 0))
    return pl.pallas_call(
        body, name="attn_bwd", grid=(seq // ATT_TQ,),
        in_specs=[pl.BlockSpec((8, ATT_TQ, 128), lambda i: (0, i, 0)), full(128), full(128),
                  pl.BlockSpec((N_Q_HEADS, 1, 128), lambda i: (0, 0, 0)),
                  _attn_bias_spec(seq // ATT_TQ), pl.BlockSpec((ATT_TQ, 512), lambda i: (i, 0))],
        out_specs=[pl.BlockSpec((ATT_TQ, 512), lambda i: (i, 0)), full(128), full(128),
                   pl.BlockSpec((N_Q_HEADS, 128), lambda i: (0, 0))],
        out_shape=[jax.ShapeDtypeStruct((seq, 512), F32), jax.ShapeDtypeStruct((seq, 128), F32),
                   jax.ShapeDtypeStruct((seq, 128), F32), jax.ShapeDtypeStruct((N_Q_HEADS, 128), F32)],
        scratch_shapes=[pltpu.VMEM((N_Q_HEADS, ATT_TQ, 128), F32)],
        compiler_params=_cparams(("arbitrary",)),
    )(q_stack, k, v, sink128, bias, d_o)


def _permute_rows(dst_ref, src_ref, sub_len):
    for k in range(N_SLAB):
        for j in range(sub_len):
            dst_ref[k, 8 * j:8 * (j + 1), :] = src_ref.at[k][pl.ds(j, SUBSEG, stride=sub_len), :]


def _unpermute_rows(dst_ref, src_ref, sub_len):
    for k in range(N_SLAB):
        for s in range(SUBSEG):
            dst_ref[k, s * sub_len:(s + 1) * sub_len, :] = src_ref.at[k][pl.ds(s, sub_len, stride=SUBSEG), :]


def _scan_pass(br_ref, bi_ref, lr_row, li_row, start, end_refs, *, sub_len, reverse, store):
    width = br_ref.shape[1]
    for c0 in range(0, width, SCAN_LANES):
        cols = slice(c0, c0 + SCAN_LANES)
        lr = jnp.broadcast_to(lr_row[:, cols], (SUBSEG, SCAN_LANES))
        li = jnp.broadcast_to(li_row[:, cols], (SUBSEG, SCAN_LANES))
        if start is None:
            init = (jnp.zeros((SUBSEG, SCAN_LANES), F32), jnp.zeros((SUBSEG, SCAN_LANES), F32))
        else:
            init = (start[0][:, cols], start[1][:, cols])

        def steps(jo, state, cols=cols, lr=lr, li=li):
            sr, si = state
            for ju in range(SCAN_UNROLL):
                jj = jo * SCAN_UNROLL + ju
                j = (sub_len - 1 - jj) if reverse else jj
                r0 = pl.multiple_of(j * SUBSEG, SUBSEG)
                nr = lr * sr - li * si + br_ref[pl.ds(r0, SUBSEG), cols]
                ni = lr * si + li * sr + bi_ref[pl.ds(r0, SUBSEG), cols]
                if store:
                    br_ref[pl.ds(r0, SUBSEG), cols] = nr
                    bi_ref[pl.ds(r0, SUBSEG), cols] = ni
                sr, si = nr, ni
            return sr, si

        sr, si = lax.fori_loop(0, sub_len // SCAN_UNROLL, steps, init)
        if end_refs is not None:
            end_refs[0][:, cols] = sr
            end_refs[1][:, cols] = si


def _resolve_starts(z_refs, carry_refs, start_refs, pr_row, pi_row, *, reverse):
    cr, ci = carry_refs[0][0:1, :], carry_refs[1][0:1, :]
    for s in (range(SUBSEG - 1, -1, -1) if reverse else range(SUBSEG)):
        start_refs[0][s:s + 1, :] = cr
        start_refs[1][s:s + 1, :] = ci
        zr, zi = z_refs[0][s:s + 1, :], z_refs[1][s:s + 1, :]
        cr, ci = pr_row * cr - pi_row * ci + zr, pr_row * ci + pi_row * cr + zi
    carry_refs[0][0:1, :] = cr
    carry_refs[1][0:1, :] = ci


def _param_specs(direction):
    row = lambda q: pl.BlockSpec((None, None, 1, STATE_W), lambda i: (q, direction, 0, 0))
    wide = lambda q: pl.BlockSpec((None, None, N_SLAB, SLAB_IN, SLAB_ST), lambda i: (q, direction, 0, 0, 0))
    tall = lambda q: pl.BlockSpec((None, None, N_SLAB, SLAB_ST, SLAB_IN), lambda i: (q, direction, 0, 0, 0))
    return [row(q) for q in range(4)], [wide(0), wide(1)], [tall(0), tall(1)]


def _ssm_fwd(u, lam, bb, cb, *, direction, tb, name):
    reverse = direction == 1
    seq = u.shape[1]
    nblk = seq // tb
    sub_len = tb // SUBSEG

    def body(u_ref, lr_ref, li_ref, pr_ref, pi_ref, bbr_ref, bbi_ref, cbr_ref, cbi_ref,
             y_ref, sr_ref, si_ref, xr, xi, up, yp, zr, zi, car, cai):
        @pl.when(pl.program_id(0) == 0)
        def _():
            car[...] = jnp.zeros_like(car)
            cai[...] = jnp.zeros_like(cai)

        _permute_rows(up, u_ref, sub_len)
        for k in range(N_SLAB):
            ub = up[k].astype(BF16)
            xr[:, k * SLAB_ST:(k + 1) * SLAB_ST] = _dot(ub, bbr_ref[k])
            xi[:, k * SLAB_ST:(k + 1) * SLAB_ST] = _dot(ub, bbi_ref[k])
        lr, li = lr_ref[...], li_ref[...]
        _scan_pass(xr, xi, lr, li, None, (zr, zi), sub_len=sub_len, reverse=reverse, store=False)
        _resolve_starts((zr, zi), (car, cai), (sr_ref, si_ref), pr_ref[...], pi_ref[...], reverse=reverse)
        _scan_pass(xr, xi, lr, li, (sr_ref, si_ref), None, sub_len=sub_len, reverse=reverse, store=True)
        for k in range(N_SLAB):
            st = slice(k * SLAB_ST, (k + 1) * SLAB_ST)
            yp[k] = _dot(xr[:, st].astype(BF16), cbr_ref[k]) - _dot(xi[:, st].astype(BF16), cbi_ref[k])
        _unpermute_rows(y_ref, yp, sub_len)

    blk = (lambda i: nblk - 1 - i) if reverse else (lambda i: i)
    rows, wide, tall = _param_specs(direction)
    tok = pl.BlockSpec((N_SLAB, tb, SLAB_IN), lambda i: (0, blk(i), 0))
    start_spec = pl.BlockSpec((None, SUBSEG, STATE_W), lambda i: (blk(i), 0, 0))
    return pl.pallas_call(
        body, name=name, grid=(nblk,),
        in_specs=[tok] + rows + wide + tall,
        out_specs=[tok, start_spec, start_spec],
        out_shape=[jax.ShapeDtypeStruct((N_SLAB, seq, SLAB_IN), F32), jax.ShapeDtypeStruct((nblk, SUBSEG, STATE_W), F32),
                   jax.ShapeDtypeStruct((nblk, SUBSEG, STATE_W), F32)],
        scratch_shapes=[pltpu.VMEM((tb, STATE_W), F32), pltpu.VMEM((tb, STATE_W), F32),
                        pltpu.VMEM((N_SLAB, tb, SLAB_IN), F32), pltpu.VMEM((N_SLAB, tb, SLAB_IN), F32),
                        pltpu.VMEM((SUBSEG, STATE_W), F32), pltpu.VMEM((SUBSEG, STATE_W), F32),
                        pltpu.VMEM((SUBSEG, STATE_W), F32), pltpu.VMEM((SUBSEG, STATE_W), F32)],
        compiler_params=_cparams(("arbitrary",)),
    )(u, lam, lam, lam, lam, bb, bb, cb, cb)


def _ssm_bwd(u, dy, starts, lam, bb, bbt, cb_t, *, direction, tb, name):
    reverse = direction == 1
    seq = u.shape[1]
    nblk = seq // tb
    sub_len = tb // SUBSEG

    def body(u_ref, dy_ref, sr_ref, si_ref, lr_ref, li_ref, pr_ref, pi_ref, bbr_ref, bbi_ref, btr_ref, bti_ref,
             ctr_ref, cti_ref, du_ref, gb_ref, gc_ref, dl_ref,
             xr, xi, gr, gi, up, dyp, dup, zr, zi, gsr, gsi, car, cai):
        gbr_ref, gbi_ref = gb_ref.at[0], gb_ref.at[1]
        gcr_ref, gci_ref = gc_ref.at[0], gc_ref.at[1]
        dlr_ref, dli_ref = dl_ref.at[0], dl_ref.at[1]

        @pl.when(pl.program_id(0) == 0)
        def _():
            for ref in (car, cai, gbr_ref, gbi_ref, gcr_ref, gci_ref, dlr_ref, dli_ref):
                ref[...] = jnp.zeros_like(ref)

        _permute_rows(up, u_ref, sub_len)
        _permute_rows(dyp, dy_ref, sub_len)
        lr, li = lr_ref[...], li_ref[...]
        for k in range(N_SLAB):
            st = slice(k * SLAB_ST, (k + 1) * SLAB_ST)
            ub = up[k].astype(BF16)
            xr[:, st] = _dot(ub, bbr_ref[k])
            xi[:, st] = _dot(ub, bbi_ref[k])
            dyb = dyp[k].astype(BF16)
            gr[:, st] = _dot(dyb, ctr_ref[k])
            gi[:, st] = -_dot(dyb, cti_ref[k])
        _scan_pass(xr, xi, lr, li, (sr_ref, si_ref), None, sub_len=sub_len, reverse=reverse, store=True)
        for k in range(N_SLAB):
            st = slice(k * SLAB_ST, (k + 1) * SLAB_ST)
            dyb = dyp[k].astype(BF16)
            gcr_ref[k] += _dot_tn(xr[:, st].astype(BF16), dyb)
            gci_ref[k] -= _dot_tn(xi[:, st].astype(BF16), dyb)
        nli = -li
        _scan_pass(gr, gi, lr, nli, None, (zr, zi), sub_len=sub_len, reverse=not reverse, store=False)
        _resolve_starts((zr, zi), (car, cai), (gsr, gsi), pr_ref[...], -pi_ref[...], reverse=not reverse)
        _scan_pass(gr, gi, lr, nli, (gsr, gsi), None, sub_len=sub_len, reverse=not reverse, store=True)
        for k in range(N_SLAB):
            st = slice(k * SLAB_ST, (k + 1) * SLAB_ST)
            ub = up[k].astype(BF16)
            grb, gib = gr[:, st].astype(BF16), gi[:, st].astype(BF16)
            gbr_ref[k] += _dot_tn(ub, grb)
            gbi_ref[k] += _dot_tn(ub, gib)
            dup[k] = _dot(grb, btr_ref[k]) + _dot(gib, bti_ref[k])
        _unpermute_rows(du_ref, dup, sub_len)

        for c0 in range(0, STATE_W, SCAN_LANES):
            cols = slice(c0, c0 + SCAN_LANES)
            edge = (sub_len - 1) * SUBSEG if reverse else 0
            g_r, g_i = gr[edge:edge + SUBSEG, cols], gi[edge:edge + SUBSEG, cols]
            x_r, x_i = sr_ref[:, cols], si_ref[:, cols]
            acc = (dlr_ref[:, cols] + (g_r * x_r + g_i * x_i), dli_ref[:, cols] + (g_i * x_r - g_r * x_i))

            def step(jj, acc, cols=cols):
                tile = lambda t: t * SUBSEG if isinstance(t, int) else pl.multiple_of(t * SUBSEG, SUBSEG)
                r_g, r_x = tile(jj if reverse else jj + 1), tile(jj + 1 if reverse else jj)
                g_r, g_i = gr[pl.ds(r_g, SUBSEG), cols], gi[pl.ds(r_g, SUBSEG), cols]
                x_r, x_i = xr[pl.ds(r_x, SUBSEG), cols], xi[pl.ds(r_x, SUBSEG), cols]
                return acc[0] + (g_r * x_r + g_i * x_i), acc[1] + (g_i * x_r - g_r * x_i)

            acc = step(0, acc)
            acc = lax.fori_loop(0, (sub_len - 2) // 2, lambda jo, a: step(2 * jo + 2, step(2 * jo + 1, a)), acc)
            dlr_ref[:, cols] = acc[0]
            dli_ref[:, cols] = acc[1]

    blk = (lambda i: i) if reverse else (lambda i: nblk - 1 - i)
    rows, wide, tall = _param_specs(direction)
    tok = pl.BlockSpec((N_SLAB, tb, SLAB_IN), lambda i: (0, blk(i), 0))
    start_spec = pl.BlockSpec((None, SUBSEG, STATE_W), lambda i: (blk(i), 0, 0))
    gb_shape, gc_shape, dl_shape = (2, N_SLAB, SLAB_IN, SLAB_ST), (2, N_SLAB, SLAB_ST, SLAB_IN), (2, SUBSEG, STATE_W)
    whole = lambda shape: pl.BlockSpec(shape, lambda i: (0,) * len(shape))
    big = lambda: pltpu.VMEM((tb, STATE_W), F32)
    slabs = lambda: pltpu.VMEM((N_SLAB, tb, SLAB_IN), F32)
    tile = lambda: pltpu.VMEM((SUBSEG, STATE_W), F32)
    return pl.pallas_call(
        body, name=name, grid=(nblk,),
        in_specs=[tok, tok, start_spec, start_spec] + rows + wide + tall + wide,
        out_specs=[tok, whole(gb_shape), whole(gc_shape), whole(dl_shape)],
        out_shape=[jax.ShapeDtypeStruct((N_SLAB, seq, SLAB_IN), F32), jax.ShapeDtypeStruct(gb_shape, F32),
                   jax.ShapeDtypeStruct(gc_shape, F32), jax.ShapeDtypeStruct(dl_shape, F32)],
        scratch_shapes=[big(), big(), big(), big(), slabs(), slabs(), slabs(),
                        tile(), tile(), tile(), tile(), tile(), tile()],
        compiler_params=_cparams(("arbitrary",)),
    )(u, dy, *starts, lam, lam, lam, lam, bb, bb, bbt, bbt, cb_t, cb_t)


GELU_C = math.sqrt(2.0 / math.pi)
GELU_K = 0.044715


def _mid(o, za, u, y_f, y_b, zs, x, target, ssm_d, w_glu, b_glu, g_attn, g_ssm, w_out, ln_g, ln_b, tb):
    seq = x.shape[0]

    def body(o_ref, za_ref, u_ref, yf_ref, yb_ref, zs_ref, x_ref, t_ref, d_ref, wg_ref, bg_ref, ga_ref, gs_ref,
             wo_ref, lg_ref, lb_ref,
             loss_ref, do_ref, dza_ref, dyl_ref, dzs_ref, dpre_ref, gwo_ref, gwg_ref, vec_ref, wop):
        @pl.when(pl.program_id(0) == 0)
        def _():
            for ref in (loss_ref, gwo_ref, gwg_ref, vec_ref):
                ref[...] = jnp.zeros_like(ref)
            for nat, par in _pair_blocks(0):
                wop[par, :] = wo_ref[nat, :]
            wop[D_ATTN:, :] = wo_ref[D_ATTN:, :]

        o, za = o_ref[...], za_ref[...]
        sig_a = _sigmoid(za)
        silu_a = za * sig_a
        ya = o * silu_a
        r_a = lax.rsqrt(jnp.mean(ya * ya, axis=1, keepdims=True) + NORM_EPS)
        n_a = ya * r_a
        g_a = ga_ref[...]
        unslab = lambda ref: jnp.concatenate([ref[k] for k in range(N_SLAB)], axis=1)
        u_blk, zs = unslab(u_ref), zs_ref[...]
        d_row = d_ref[...]
        ylin = d_row * u_blk + unslab(yf_ref) + unslab(yb_ref)
        inner = GELU_C * (ylin + GELU_K * ylin * ylin * ylin)
        th = jnp.tanh(inner)
        gl = 0.5 * ylin * (1.0 + th)
        glb = gl.astype(BF16)
        sg = _sigmoid(_dot(glb, wg_ref[...]) + bg_ref[...])
        y2 = gl * sg
        sig_s = _sigmoid(zs)
        silu_s = zs * sig_s
        ys = y2 * silu_s
        r_s = lax.rsqrt(jnp.mean(ys * ys, axis=1, keepdims=True) + NORM_EPS)
        n_s = ys * r_s
        g_s = gs_ref[...]
        mixed = jnp.concatenate([n_a * g_a, n_s * g_s], axis=1).astype(BF16)
        pre = ALPHA * x_ref[...] + _dot(mixed, wop[...])
        mu = jnp.mean(pre, axis=1, keepdims=True)
        cen = pre - mu
        rstd = lax.rsqrt(jnp.mean(cen * cen, axis=1, keepdims=True) + NORM_EPS)
        hhat = cen * rstd
        ln_g = lg_ref[...]
        err = hhat * ln_g + lb_ref[...] - t_ref[...]
        loss_ref[...] += 0.5 * jnp.sum(jnp.mean(err * err, axis=1, keepdims=True))

        dh = err * (1.0 / D_MODEL)
        vec_ref[0:1, :] += jnp.sum(dh * hhat, axis=0, keepdims=True)
        vec_ref[1:2, :] += jnp.sum(dh, axis=0, keepdims=True)
        dhh = dh * ln_g
        dpre = rstd * (dhh - jnp.mean(dhh, axis=1, keepdims=True) - hhat * jnp.mean(dhh * hhat, axis=1, keepdims=True))
        dpre_ref[...] = dpre
        dpb = dpre.astype(BF16)
        for j in range(4):
            g_pair = _dot_tn(mixed[:, 128 * j:128 * (j + 1)], dpb)
            for g in range(2):
                nat = HEAD_DIM * (4 * g + j)
                gwo_ref[nat:nat + HEAD_DIM, :] += g_pair[HEAD_DIM * g:HEAD_DIM * (g + 1), :]
        gwo_ref[D_ATTN:, :] += _dot_tn(mixed[:, D_ATTN:], dpb)
        dmix = _dot_nt(dpb, wop[...])
        dna = dmix[:, :D_ATTN]
        vec_ref[2:3, 0:D_ATTN] += jnp.sum(dna * n_a, axis=0, keepdims=True)
        dna = dna * g_a
        dya = r_a * (dna - n_a * jnp.mean(dna * n_a, axis=1, keepdims=True))
        do_ref[...] = dya * silu_a
        dza_ref[...] = dya * o * (sig_a * (1.0 + za * (1.0 - sig_a)))
        dns = dmix[:, D_ATTN:]
        vec_ref[2:3, D_ATTN:] += jnp.sum(dns * n_s, axis=0, keepdims=True)
        dns = dns * g_s
        dys = r_s * (dns - n_s * jnp.mean(dns * n_s, axis=1, keepdims=True))
        dzs_ref[...] = dys * y2 * (sig_s * (1.0 + zs * (1.0 - sig_s)))
        dy2 = dys * silu_s
        da = dy2 * gl * sg * (1.0 - sg)
        vec_ref[3:4, D_SSM:] += jnp.sum(da, axis=0, keepdims=True)
        dab = da.astype(BF16)
        gwg_ref[...] += _dot_tn(glb, dab)
        dgl = dy2 * sg + _dot_nt(dab, wg_ref[...])
        dylin = dgl * (0.5 * (1.0 + th) + 0.5 * ylin * (1.0 - th * th) * GELU_C * (1.0 + 3.0 * GELU_K * ylin * ylin))
        for k in range(N_SLAB):
            dyl_ref[k] = dylin[:, k * SLAB_IN:(k + 1) * SLAB_IN]
        vec_ref[3:4, 0:D_SSM] += jnp.sum(dylin * u_blk, axis=0, keepdims=True)

    tok = lambda w: pl.BlockSpec((tb, w), lambda i: (i, 0))
    slab = pl.BlockSpec((N_SLAB, tb, SLAB_IN), lambda i: (0, i, 0))
    const = lambda r, c: pl.BlockSpec((r, c), lambda i: (0, 0))
    tok_shape = jax.ShapeDtypeStruct((seq, 512), F32)
    return pl.pallas_call(
        body, name="mid", grid=(seq // tb,),
        in_specs=[tok(512), tok(512), slab, slab, slab, tok(512), tok(1024), tok(1024),
                  const(1, 512), const(512, 512), const(1, 512), const(1, 512), const(1, 512),
                  const(1024, 1024), const(1, 1024), const(1, 1024)],
        out_specs=[const(8, 128), tok(512), tok(512), slab, tok(512), tok(1024),
                   const(1024, 1024), const(512, 512), const(8, 1024)],
        out_shape=[jax.ShapeDtypeStruct((8, 128), F32), tok_shape, tok_shape,
                   jax.ShapeDtypeStruct((N_SLAB, seq, SLAB_IN), F32), tok_shape,
                   jax.ShapeDtypeStruct((seq, 1024), F32), jax.ShapeDtypeStruct((1024, 1024), F32),
                   jax.ShapeDtypeStruct((512, 512), F32), jax.ShapeDtypeStruct((8, 1024), F32)],
        scratch_shapes=[pltpu.VMEM((D_MODEL, D_MODEL), BF16)],
        compiler_params=_cparams(("arbitrary",)),
    )(o, za, u, y_f, y_b, zs, x, target, ssm_d, w_glu, b_glu, g_attn, g_ssm, w_out, ln_g, ln_b)


def _proj_bwd(x, dq, dk, dv, dza, du_f, du_b, dylin, dzs, dpre, ssm_d, cos128, sin128, wt, tb):
    seq = x.shape[0]

    def body(x_ref, dq_ref, dk_ref, dv_ref, dza_ref, duf_ref, dub_ref, dyl_ref, dzs_ref, dpre_ref, d_ref,
             cos_ref, sin_ref, wt_ref, gx_ref, gw_ref, wp):
        @pl.when(pl.program_id(0) == 0)
        def _():
            gw_ref[...] = jnp.zeros_like(gw_ref)
            for base in (W_Q, W_ZA):
                for nat, par in _pair_blocks(base):
                    wp[par, :] = wt_ref[nat, :]
            wp[W_KV:W_ZA, :] = wt_ref[W_KV:W_ZA, :]
            wp[W_U:, :] = wt_ref[W_U:, :]

        cos, sin = cos_ref[...], sin_ref[...]

        def unrope(t):
            return t * cos + _rotate_half_unsigned(t * sin)

        dq_rot = dq_ref[...]
        pieces = [unrope(dq_rot[:, 128 * j:128 * (j + 1)]) for j in range(4)]
        d_row = d_ref[...]
        pieces += [unrope(dk_ref[...]), dv_ref[...], dza_ref[...]]
        pieces += [duf_ref[k] + dub_ref[k] + d_row[:, k * SLAB_IN:(k + 1) * SLAB_IN] * dyl_ref[k] for k in range(N_SLAB)]
        pieces += [dzs_ref[...]]
        dproj = jnp.concatenate(pieces, axis=1).astype(BF16)
        gx_ref[...] = ALPHA * dpre_ref[...] + _dot(dproj, wp[...])
        xb = x_ref[...].astype(BF16)
        for base in (W_Q, W_ZA):
            for j in range(4):
                g_pair = _dot_tn(dproj[:, base + 128 * j:base + 128 * (j + 1)], xb)
                for g in range(2):
                    nat = base + HEAD_DIM * (4 * g + j)
                    gw_ref[nat:nat + HEAD_DIM, :] += g_pair[HEAD_DIM * g:HEAD_DIM * (g + 1), :]
        gw_ref[W_KV:W_ZA, :] += _dot_tn(dproj[:, W_KV:W_ZA], xb)
        gw_ref[W_U:, :] += _dot_tn(dproj[:, W_U:], xb)

    tok = lambda w: pl.BlockSpec((tb, w), lambda i: (i, 0))
    slab = pl.BlockSpec((N_SLAB, tb, SLAB_IN), lambda i: (0, i, 0))
    const = lambda r, c: pl.BlockSpec((r, c), lambda i: (0, 0))
    return pl.pallas_call(
        body, name="proj_bwd", grid=(seq // tb,),
        in_specs=[tok(1024), tok(512), tok(128), tok(128), tok(512), slab, slab, slab, tok(512), tok(1024),
                  const(1, 512), tok(128), tok(128), const(D_IN_PROJ, D_MODEL)],
        out_specs=[tok(1024), const(D_IN_PROJ, D_MODEL)],
        out_shape=[jax.ShapeDtypeStruct((seq, D_MODEL), F32), jax.ShapeDtypeStruct((D_IN_PROJ, D_MODEL), F32)],
        scratch_shapes=[pltpu.VMEM((D_IN_PROJ, D_MODEL), BF16)],
        compiler_params=_cparams(("arbitrary",)),
    )(x, dq, dk, dv, dza, du_f, du_b, dylin, dzs, dpre, ssm_d, cos128, sin128, wt)


def _adamw(w, g, m, v, name):
    rows, cols = w.shape
    tb = rows
    while tb * cols * 4 > ADAMW_BLOCK_BYTES and tb % 16 == 0:
        tb //= 2

    def body(w_ref, g_ref, m_ref, v_ref, d_ref, nm_ref, nv_ref):
        _adamw_update(w_ref, g_ref, m_ref, v_ref, d_ref, nm_ref, nv_ref)

    spec = pl.BlockSpec((tb, cols), lambda i: (i, 0))
    return pl.pallas_call(
        body, name=name, grid=(rows // tb,), in_specs=[spec] * 4, out_specs=[spec] * 3,
        out_shape=[jax.ShapeDtypeStruct((rows, cols), F32)] * 3,
        compiler_params=_cparams(("arbitrary",)),
    )(w, g, m, v)


def _adamw_update(w_ref, g_ref, m_ref, v_ref, d_ref, nm_ref, nv_ref):
    g_blk = g_ref[...]
    m_new = ADAM_B1 * m_ref[...] + (1.0 - ADAM_B1) * g_blk
    v_new = ADAM_B2 * v_ref[...] + (1.0 - ADAM_B2) * (g_blk * g_blk)
    m_hat = m_new / (1.0 - ADAM_B1 ** ADAM_STEP)
    v_hat = v_new / (1.0 - ADAM_B2 ** ADAM_STEP)
    d_ref[...] = -ADAM_LR * (m_hat / (jnp.sqrt(v_hat) + ADAM_EPS) + ADAM_WD * w_ref[...])
    nm_ref[...] = m_new
    nv_ref[...] = v_new


def _adamw_many(groups, name):
    n = len(groups)

    def body(*refs):
        for p in range(n):
            _adamw_update(*refs[4 * p:4 * p + 4], *refs[4 * n + 3 * p:4 * n + 3 * p + 3])

    return pl.pallas_call(
        body, name=name,
        out_shape=[jax.ShapeDtypeStruct(grp[0].shape, F32) for grp in groups for _ in range(3)],
    )(*[a for grp in groups for a in grp])


_WEIGHTS = ["w_in", "attn_sink", "ssm_a_re", "ssm_a_im", "ssm_log_dt", "ssm_b_re", "ssm_b_im", "ssm_c_re", "ssm_c_im",
            "ssm_d", "w_glu", "b_glu", "norm_attn_g", "norm_ssm_g", "w_out", "ln_g", "ln_b"]
N_DG = N_DIR * N_GROUPS
BIG_ROWS = N_DG * SSM_CH * SSM_STATE // 128
TINY_ROWS = 48


def _slabs_wide(t):
    eye = jnp.eye(8, dtype=t.dtype)
    return jnp.einsum("rdkgcp,gh->rdkgchp", t.reshape(2, N_DIR, N_SLAB, 8, SSM_CH, SSM_STATE), eye).reshape(
        2, N_DIR, N_SLAB, SLAB_IN, SLAB_ST)


def _slabs_tall(t):
    eye = jnp.eye(8, dtype=t.dtype)
    return jnp.einsum("rdkgcp,gh->rdkhpgc", t.reshape(2, N_DIR, N_SLAB, 8, SSM_CH, SSM_STATE), eye).reshape(
        2, N_DIR, N_SLAB, SLAB_ST, SLAB_IN)


def _wide_diagonal(t):
    eye = jnp.eye(8, dtype=t.dtype)
    return jnp.einsum("rdkgchp,gh->rdkgcp", t.reshape(2, N_DIR, N_SLAB, 8, SSM_CH, 8, SSM_STATE), eye).reshape(
        2, N_DIR, N_GROUPS, SSM_CH, SSM_STATE)


def _tall_diagonal(t):
    eye = jnp.eye(8, dtype=t.dtype)
    return jnp.einsum("rdkhpgc,gh->rdkgcp", t.reshape(2, N_DIR, N_SLAB, 8, SSM_STATE, 8, SSM_CH), eye).reshape(
        2, N_DIR, N_GROUPS, SSM_CH, SSM_STATE)


def _pack_small_grads(g_b, g_c, g_vec, g_ar, g_ai, g_dt, g_sink):
    big = jnp.concatenate([g_b.reshape(2, BIG_ROWS, 128), g_c.reshape(2, BIG_ROWS, 128)], axis=0)
    row = lambda t: jnp.pad(t.reshape(1, -1), ((0, 0), (0, 128 - t.size)))
    tiny = jnp.concatenate([g_vec.reshape(64, 128), g_ar.reshape(32, 128), g_ai.reshape(32, 128), row(g_dt), row(g_sink),
                            jnp.zeros((N_CHIPS * TINY_ROWS - 130, 128), F32)], axis=0)
    return jnp.concatenate([big, tiny.reshape(N_CHIPS, TINY_ROWS, 128)], axis=1)


def _unpack_small_grads(packed):
    big = packed[:, :BIG_ROWS].reshape(N_CHIPS, 2 * BIG_ROWS, SSM_STATE)
    tiny = packed[:, BIG_ROWS:].reshape(N_CHIPS * TINY_ROWS, 128)
    g_vec = tiny[0:64].reshape(8, 1024)
    return {
        "ssm_b_re": big[0], "ssm_b_im": big[1], "ssm_c_re": big[2], "ssm_c_im": big[3],
        "ln_g": g_vec[0:1], "ln_b": g_vec[1:2],
        "norm_attn_g": g_vec[2:3, :D_ATTN][:, _PAIR_INV], "norm_ssm_g": g_vec[2:3, D_ATTN:],
        "ssm_d": g_vec[3:4, :D_SSM], "b_glu": g_vec[3:4, D_SSM:],
        "ssm_a_re": tiny[64:96].reshape(N_DG, SSM_STATE), "ssm_a_im": tiny[96:128].reshape(N_DG, SSM_STATE),
        "ssm_log_dt": tiny[128:129, :N_DG].reshape(N_DIR, N_GROUPS), "attn_sink": tiny[129:130, :N_Q_HEADS],
    }


def _small_view(name, t):
    if name in ("ssm_b_re", "ssm_b_im"):
        return jnp.swapaxes(t[0], 2, 3).reshape(N_DG * SSM_CH, SSM_STATE)
    if name in ("ssm_c_re", "ssm_c_im"):
        return t.reshape(N_DG * SSM_CH, SSM_STATE)
    if name in ("ssm_a_re", "ssm_a_im"):
        return t.reshape(N_DG, SSM_STATE)
    if name == "ssm_log_dt":
        return t.reshape(N_DIR, N_GROUPS)
    return t.reshape(1, -1)


def _small_unview(name, t, shape):
    if name in ("ssm_b_re", "ssm_b_im"):
        return jnp.swapaxes(t.reshape(N_DIR, N_GROUPS, SSM_CH, SSM_STATE), 2, 3).reshape(shape)
    return t.reshape(shape)


def kernel(x, w_in, attn_sink, ssm_a_re, ssm_a_im, ssm_log_dt, ssm_b_re, ssm_b_im, ssm_c_re, ssm_c_im, ssm_d, w_glu, b_glu, norm_attn_g, norm_ssm_g, w_out, ln_g, ln_b, loss_target, m_w_in, m_attn_sink, m_ssm_a_re, m_ssm_a_im, m_ssm_log_dt, m_ssm_b_re, m_ssm_b_im, m_ssm_c_re, m_ssm_c_im, m_ssm_d, m_w_glu, m_b_glu, m_norm_attn_g, m_norm_ssm_g, m_w_out, m_ln_g, m_ln_b, v_w_in, v_attn_sink, v_ssm_a_re, v_ssm_a_im, v_ssm_log_dt, v_ssm_b_re, v_ssm_b_im, v_ssm_c_re, v_ssm_c_im, v_ssm_d, v_w_glu, v_b_glu, v_norm_attn_g, v_norm_ssm_g, v_w_out, v_ln_g, v_ln_b):
    args = dict(locals())
    weights = {n: args[n] for n in _WEIGHTS}
    mom_m = {n: args["m_" + n] for n in _WEIGHTS}
    mom_v = {n: args["v_" + n] for n in _WEIGHTS}
    xs = x[0]
    target = loss_target[0]

    wt_g, w_glu_g, w_out_g = _all_gather_chips([w_in[0].T, w_glu[0], w_out[0]], BF16, "gather_weights")
    wt_full = wt_g.reshape(D_IN_PROJ, D_MODEL)
    w_glu_full = w_glu_g.reshape(D_SSM, D_SSM)
    w_out_full = w_out_g.reshape(D_MODEL, D_MODEL)

    loss_local, g_x, g_wt, g_w_out, g_w_glu, g_small = _local_step(
        xs, target, wt_full, w_glu_full, w_out_full, attn_sink, ssm_a_re, ssm_a_im, ssm_log_dt, ssm_b_re, ssm_b_im,
        ssm_c_re, ssm_c_im, ssm_d, b_glu, norm_attn_g, norm_ssm_g, ln_g, ln_b)
    loss = lax.psum(loss_local, ("x", "y", "c"))

    r_wt = _reduce_scatter_chips(g_wt.reshape(N_CHIPS, -1, D_MODEL), "reduce_w_in")
    r_w_out = _reduce_scatter_chips(g_w_out.reshape(N_CHIPS, -1, D_MODEL), "reduce_w_out")
    r_w_glu = _reduce_scatter_chips(g_w_glu.reshape(N_CHIPS, -1, D_SSM), "reduce_w_glu")
    r_small = _reduce_scatter_chips(g_small, "reduce_small")
    (g_small_all,) = _all_gather_chips([r_small], F32, "gather_small")
    small_grads = _unpack_small_grads(g_small_all)

    grads, deltas, new_m, new_v = {}, {}, {}, {}
    d_w, m_w, v_w = _adamw(w_in[0].T, r_wt, m_w_in[0].T, v_w_in[0].T, "adamw_w_in")
    grads["w_in"], deltas["w_in"], new_m["w_in"], new_v["w_in"] = r_wt.T[None], d_w.T[None], m_w.T[None], v_w.T[None]
    for n, g in (("w_out", r_w_out), ("w_glu", r_w_glu)):
        d_w, m_w, v_w = _adamw(weights[n][0], g, mom_m[n][0], mom_v[n][0], "adamw_" + n)
        grads[n], deltas[n], new_m[n], new_v[n] = g[None], d_w[None], m_w[None], v_w[None]
    names = sorted(small_grads)
    updates = _adamw_many([(_small_view(n, weights[n]), small_grads[n], _small_view(n, mom_m[n]), _small_view(n, mom_v[n]))
                           for n in names], "adamw_small")
    for i, n in enumerate(names):
        shape = weights[n].shape
        grads[n] = _small_unview(n, small_grads[n], shape)
        deltas[n], new_m[n], new_v[n] = (_small_unview(n, t, shape) for t in updates[3 * i:3 * i + 3])

    return (loss, g_x[None], *[grads[n] for n in _WEIGHTS], *[deltas[n] for n in _WEIGHTS],
            *[new_m[n] for n in _WEIGHTS], *[new_v[n] for n in _WEIGHTS])


def _local_step(xs, target, wt_full, w_glu_full, w_out_full, attn_sink, ssm_a_re, ssm_a_im, ssm_log_dt, ssm_b_re,
                ssm_b_im, ssm_c_re, ssm_c_im, ssm_d, b_glu, norm_attn_g, norm_ssm_g, ln_g, ln_b):
    seq = xs.shape[0]

    a_r = ssm_a_re.reshape(N_DG, 1, SSM_STATE)
    a_i = ssm_a_im.reshape(N_DG, 1, SSM_STATE)
    log_dt = ssm_log_dt.reshape(N_DG, 1, 1)
    b_r = jnp.swapaxes(ssm_b_re[0], 2, 3).reshape(N_DG, SSM_CH, SSM_STATE)
    b_i = jnp.swapaxes(ssm_b_im[0], 2, 3).reshape(N_DG, SSM_CH, SSM_STATE)
    ssm_tb = min(256, seq)
    sub_len = ssm_tb // SUBSEG
    lam, bbar = _ssm_params_fwd(a_r, a_i, log_dt, b_r, b_i, int(math.log2(sub_len)))
    lam = lam.reshape(4, N_DIR, 1, STATE_W)
    bbar = bbar.astype(BF16).reshape(2, N_DIR, N_GROUPS, SSM_CH, SSM_STATE)
    c_both = jnp.stack([ssm_c_re[0], ssm_c_im[0]]).astype(BF16)
    bb, bbt = _slabs_wide(bbar), _slabs_tall(bbar)
    cb, cb_t = _slabs_tall(c_both), _slabs_wide(c_both)

    cos128, sin128 = _rope_tables(seq)
    q_stack, k_rot, v_bf, z_attn, u, z_ssm = _proj(xs, wt_full, cos128, sin128, min(512, seq))
    sink128 = jnp.broadcast_to(attn_sink[0][:, None, None], (N_Q_HEADS, 1, 128))
    attn_bias = _attn_bias()
    o = _attn_fwd(q_stack, k_rot, v_bf, sink128, attn_bias)
    ys, starts = [], []
    for d in range(N_DIR):
        y_d, s_r, s_i = _ssm_fwd(u, lam, bb, cb, direction=d, tb=ssm_tb, name=f"ssm_fwd_{d}")
        ys.append(y_d)
        starts.append((s_r, s_i))

    row = lambda t: t.reshape(1, -1)
    g_attn_p = row(norm_attn_g)[:, _PAIR_PERM]
    loss_blk, d_o, d_za, d_ylin, d_zs, d_pre, g_w_out, g_w_glu, g_vec = _mid(
        o, z_attn, u, ys[0], ys[1], z_ssm, xs, target, row(ssm_d), w_glu_full, row(b_glu),
        g_attn_p, row(norm_ssm_g), w_out_full, row(ln_g), row(ln_b), min(256, seq))

    dq, dk, dv, g_sink = _attn_bwd(q_stack, k_rot, v_bf, sink128, attn_bias, d_o)
    dus, g_bb, g_cb, g_lam = [], [], [], []
    for d in range(N_DIR):
        du_d, gb_d, gc_d, dl_d = _ssm_bwd(u, d_ylin, starts[d], lam, bb, bbt, cb_t, direction=d, tb=ssm_tb,
                                          name=f"ssm_bwd_{d}")
        dus.append(du_d)
        g_bb.append(gb_d)
        g_cb.append(gc_d)
        g_lam.append(dl_d)
    g_bbar = _wide_diagonal(jnp.stack(g_bb, axis=1)).reshape(2, N_DG, SSM_CH, SSM_STATE)
    g_c = _tall_diagonal(jnp.stack(g_cb, axis=1))
    g_lam = jnp.swapaxes(jnp.stack(g_lam, axis=1).reshape(2, N_DIR, SUBSEG, N_GROUPS, SSM_STATE), 2, 3).reshape(
        2, N_DG, SUBSEG, SSM_STATE)
    g_ar, g_ai, g_dt, g_br, g_bi = _ssm_params_bwd(a_r, a_i, log_dt, b_r, b_i, g_lam[0], g_lam[1], g_bbar[0], g_bbar[1])

    g_x, g_wt = _proj_bwd(xs, dq, dk, dv, d_za, dus[0], dus[1], d_ylin, d_zs, d_pre, row(ssm_d), cos128, sin128,
                          wt_full, min(256, seq))

    g_b = jnp.stack([g_br, g_bi]).reshape(2, N_DIR, N_GROUPS, SSM_CH, SSM_STATE)
    g_small = _pack_small_grads(g_b, g_c, g_vec, g_ar, g_ai, g_dt, g_sink[:, 0])
    return loss_blk[0, 0], g_x, g_wt, g_w_out, g_w_glu, g_small
```

```python
import functools
import math

import numpy as np
import jax
import jax.numpy as jnp
from jax import lax
from jax.experimental import pallas as pl
from jax.experimental.pallas import tpu as pltpu

F32 = jnp.float32
BF16 = jnp.bfloat16
MESH = pl.DeviceIdType.MESH

D_MODEL = 1024
D_ATTN = 512
D_SSM = 512
HEAD_DIM = 64
N_Q_HEADS = 8
WINDOW = 128
ROPE_THETA = 10000.0
SSM_CH = 16
N_GROUPS = 32
SSM_STATE = 64
N_DIR = 2
STATE_W = N_GROUPS * SSM_STATE
N_SLAB = 4
SLAB_IN = 128
SLAB_ST = 512
NORM_EPS = 1e-5
NEG_INF = -1e30
ALPHA = 2.0 ** 0.25
D_IN_PROJ = 2304
N_CHIPS = 4

ADAM_LR = 0.001
ADAM_B1 = 0.9
ADAM_B2 = 0.999
ADAM_EPS = 1e-08
ADAM_WD = 0.01
ADAM_STEP = 10

SUBSEG = 8
SCAN_LANES = 512
SCAN_UNROLL = 4
VMEM_LIMIT = 48 * 1024 * 1024
ADAMW_BLOCK_BYTES = 3 * 512 * 1024

_PAIR_PERM = np.array([(64 * j + l) if l < 64 else (64 * (j + 4) + l - 64) for j in range(4) for l in range(128)])
_PAIR_INV = np.argsort(_PAIR_PERM)


def _cparams(sem=None):
    return pltpu.CompilerParams(dimension_semantics=sem, vmem_limit_bytes=VMEM_LIMIT)


def _dot(a, b):
    return jnp.dot(a, b, preferred_element_type=F32)


def _dot_nt(a, b):
    return lax.dot_general(a, b, (((1,), (1,)), ((), ())), preferred_element_type=F32)


def _dot_tn(a, b):
    return lax.dot_general(a, b, (((0,), (0,)), ((), ())), preferred_element_type=F32)


def _sigmoid(z):
    return 1.0 / (1.0 + jnp.exp(-z))


def _all_gather_chips(shards, out_dtype, name):
    n = len(shards)

    def body(*refs):
        in_refs, out_refs = refs[:n], refs[n:2 * n]
        send_sems, recv_sems = refs[2 * n:]
        x, y, c = lax.axis_index("x"), lax.axis_index("y"), lax.axis_index("c")
        sibling = (x, y, 1 - c)
        chips = [(1 - x, y), (x, 1 - y), (1 - x, 1 - y)]

        for a in range(n):
            out_refs[a][2 * x + y] = in_refs[a][...].astype(out_dtype)

        def half_of(a, px, py, half):
            rows = in_refs[a].shape[0] // 2
            return out_refs[a].at[2 * px + py, pl.ds(half * rows, rows), :]

        def copy(a, k, px, py, half, to):
            blk = half_of(a, px, py, half)
            return pltpu.make_async_remote_copy(src_ref=blk, dst_ref=blk, send_sem=send_sems.at[6 * a + k],
                                                recv_sem=recv_sems.at[6 * a + k], device_id=to, device_id_type=MESH)

        first = [copy(a, j, x, y, c, (*chips[j], c)) for a in range(n) for j in range(3)]
        for cp in first:
            cp.start()
        passed = []
        for a in range(n):
            for j in range(3):
                copy(a, j, *chips[j], c, (x, y, c)).wait_recv()
                fwd = copy(a, 3 + j, *chips[j], c, sibling)
                fwd.start()
                passed.append(fwd)
        for a in range(n):
            for j in range(3):
                copy(a, 3 + j, *chips[j], 1 - c, (x, y, c)).wait_recv()
        for cp in first + passed:
            cp.wait_send()

    vmem = pl.BlockSpec(memory_space=pltpu.VMEM)
    return pl.pallas_call(
        body, name=name,
        out_shape=[jax.ShapeDtypeStruct((N_CHIPS,) + s.shape, out_dtype) for s in shards],
        in_specs=[vmem] * n, out_specs=[vmem] * n,
        scratch_shapes=[pltpu.SemaphoreType.DMA((6 * n,)), pltpu.SemaphoreType.DMA((6 * n,))],
        compiler_params=pltpu.CompilerParams(vmem_limit_bytes=VMEM_LIMIT),
    )(*shards)


SEMS_PER_ARRAY = 11


def _reduce_all(pieces, narrow, name):
    n = len(pieces)
    halves = [p.shape[1] // 2 for p in pieces]
    wire = [BF16 if nar else F32 for nar in narrow]

    def body(*refs):
        p_refs, out_refs = refs[:n], refs[n:2 * n]
        a_refs, s_refs, b_refs = refs[2 * n:3 * n], refs[3 * n:4 * n], refs[4 * n:5 * n]
        send_sems, recv_sems = refs[5 * n:]
        x, y, c = lax.axis_index("x"), lax.axis_index("y"), lax.axis_index("c")
        me = 2 * x + y
        sibling = (x, y, 1 - c)
        chips = [(1 - x, y), (x, 1 - y), (1 - x, 1 - y)]
        slot = [2 * px + py for px, py in chips]

        def copy(a, k, src, dst, to):
            return pltpu.make_async_remote_copy(src_ref=src, dst_ref=dst, send_sem=send_sems.at[SEMS_PER_ARRAY * a + k],
                                                recv_sem=recv_sems.at[SEMS_PER_ARRAY * a + k],
                                                device_id=to, device_id_type=MESH)

        def rows(a, half):
            return pl.ds(pl.multiple_of(half * halves[a], 16), halves[a])

        started = []
        for a in range(n):
            cp = copy(a, 0, p_refs[a].at[:, rows(a, 1 - c), :], a_refs[a], sibling)
            cp.start()
            started.append(cp)
        for a in range(n):
            started[a].wait_recv()
            for k in range(N_CHIPS):
                acc = a_refs[a][k] + p_refs[a][k, rows(a, c), :]
                a_refs[a][k] = acc
                s_refs[a][k] = acc.astype(wire[a])
            b_refs[a][me] = s_refs[a][me]
            for j in range(3):
                cp = copy(a, 1 + j, s_refs[a].at[slot[j]], b_refs[a].at[me], (*chips[j], c))
                cp.start()
                started.append(cp)
        for a in range(n):
            for j in range(3):
                copy(a, 1 + j, s_refs[a].at[slot[j]], b_refs[a].at[slot[j]], (x, y, c)).wait_recv()
            terms = [jnp.where(me == k, a_refs[a][k], b_refs[a][k].astype(F32)) for k in range(N_CHIPS)]
            total = (terms[0] + terms[1]) + (terms[2] + terms[3])
            if a < n - 1:
                done = out_refs[a].at[rows(a, c), :]
                out_refs[a][rows(a, c), :] = total
            else:
                done = out_refs[a].at[me, rows(a, c), :]
                out_refs[a][me, rows(a, c), :] = total
            cp = copy(a, 4, done, done, sibling)
            cp.start()
            started.append(cp)
        last = n - 1
        piece = lambda k, half: out_refs[last].at[k, rows(last, half), :]
        for j in range(3):
            cp = copy(last, 5 + j, piece(me, c), piece(me, c), (*chips[j], c))
            cp.start()
            started.append(cp)
        for j in range(3):
            copy(last, 5 + j, piece(slot[j], c), piece(slot[j], c), (x, y, c)).wait_recv()
            cp = copy(last, 8 + j, piece(slot[j], c), piece(slot[j], c), sibling)
            cp.start()
            started.append(cp)
        for a in range(n - 1):
            copy(a, 4, out_refs[a].at[rows(a, 1 - c), :], out_refs[a].at[rows(a, 1 - c), :], (x, y, c)).wait_recv()
        copy(last, 4, piece(me, 1 - c), piece(me, 1 - c), (x, y, c)).wait_recv()
        for j in range(3):
            copy(last, 8 + j, piece(slot[j], 1 - c), piece(slot[j], 1 - c), (x, y, c)).wait_recv()
        for cp in started:
            cp.wait_send()

    vmem = pl.BlockSpec(memory_space=pltpu.VMEM)
    half_shape = lambda a: (N_CHIPS, halves[a], pieces[a].shape[2])
    return pl.pallas_call(
        body, name=name,
        out_shape=[jax.ShapeDtypeStruct(p.shape[1:] if a < n - 1 else p.shape, F32) for a, p in enumerate(pieces)],
        in_specs=[vmem] * n, out_specs=[vmem] * n,
        scratch_shapes=[pltpu.VMEM(half_shape(a), F32) for a in range(n)]
        + [pltpu.VMEM(half_shape(a), wire[a]) for a in range(n)]
        + [pltpu.VMEM(half_shape(a), wire[a]) for a in range(n)]
        + [pltpu.SemaphoreType.DMA((SEMS_PER_ARRAY * n,)), pltpu.SemaphoreType.DMA((SEMS_PER_ARRAY * n,))],
        compiler_params=pltpu.CompilerParams(vmem_limit_bytes=VMEM_LIMIT),
    )(*pieces)


def _ssm_param_values(ar, ai, logdt):
    dt = jnp.exp(logdt)
    mag = jnp.exp(dt * ar)
    cs, sn = jnp.cos(dt * ai), jnp.sin(dt * ai)
    lr, li = mag * cs, mag * sn
    den = ar * ar + ai * ai
    nr = (lr - 1.0) * ar + li * ai
    ni = li * ar - (lr - 1.0) * ai
    return dt, mag, lr, li, den, nr, ni


def _ssm_params_fwd(ar, ai, logdt, br, bi, n_square):
    def body(ar_ref, ai_ref, dt_ref, br_ref, bi_ref, lam_ref, bb_ref):
        _, _, lr, li, den, nr, ni = _ssm_param_values(ar_ref[...], ai_ref[...], dt_ref[...])
        lam_ref[0] = lr
        lam_ref[1] = li
        pr, pi = lr, li
        for _ in range(n_square):
            pr, pi = pr * pr - pi * pi, 2.0 * pr * pi
        lam_ref[2] = pr
        lam_ref[3] = pi
        fr, fi = nr / den, ni / den
        b_r, b_i = br_ref[...], bi_ref[...]
        bb_ref[0] = fr * b_r - fi * b_i
        bb_ref[1] = fr * b_i + fi * b_r

    return pl.pallas_call(body, name="ssm_params_fwd",
                          out_shape=[jax.ShapeDtypeStruct((4,) + ar.shape, F32),
                                     jax.ShapeDtypeStruct((2,) + br.shape, F32)])(ar, ai, logdt, br, bi)


def _ssm_params_bwd(ar, ai, logdt, br, bi, dlam_r, dlam_i, dbb_r, dbb_i):
    def body(ar_ref, ai_ref, dt_ref, br_ref, bi_ref, dlr_ref, dli_ref, dbr_ref, dbi_ref,
             gar_ref, gai_ref, gdt_ref, gbr_ref, gbi_ref):
        a_r, a_i = ar_ref[...], ai_ref[...]
        dt, mag, lr, li, den, nr, ni = _ssm_param_values(a_r, a_i, dt_ref[...])
        fr, fi = nr / den, ni / den
        b_r, b_i = br_ref[...], bi_ref[...]
        g_r, g_i = dbr_ref[...], dbi_ref[...]
        gbr_ref[...] = fr * g_r + fi * g_i
        gbi_ref[...] = fr * g_i - fi * g_r
        d_fr = jnp.sum(b_r * g_r + b_i * g_i, axis=1, keepdims=True)
        d_fi = jnp.sum(b_r * g_i - b_i * g_r, axis=1, keepdims=True)
        d_nr, d_ni = d_fr / den, d_fi / den
        d_den = -(d_fr * nr + d_fi * ni) / (den * den)
        d_lr = jnp.sum(dlr_ref[...], axis=1, keepdims=True) + d_nr * a_r - d_ni * a_i
        d_li = jnp.sum(dli_ref[...], axis=1, keepdims=True) + d_nr * a_i + d_ni * a_r
        d_ar = d_nr * (lr - 1.0) + d_ni * li + d_den * 2.0 * a_r
        d_ai = d_nr * li - d_ni * (lr - 1.0) + d_den * 2.0 * a_i
        d_mag = (d_lr * lr + d_li * li) / mag
        d_theta = d_li * lr - d_lr * li
        gar_ref[...] = d_ar + d_mag * mag * dt
        gai_ref[...] = d_ai + d_theta * dt
        d_dt = d_mag * mag * a_r + d_theta * a_i
        gdt_ref[...] = jnp.sum(d_dt, axis=2, keepdims=True) * dt

    small = jax.ShapeDtypeStruct(ar.shape, F32)
    return pl.pallas_call(
        body, name="ssm_params_bwd",
        out_shape=[small, small, jax.ShapeDtypeStruct(logdt.shape, F32),
                   jax.ShapeDtypeStruct(br.shape, F32), jax.ShapeDtypeStruct(br.shape, F32)],
    )(ar, ai, logdt, br, bi, dlam_r, dlam_i, dbb_r, dbb_i)


def _rope_tables(seq):
    half = HEAD_DIM // 2
    inv_freq = ROPE_THETA ** (-jnp.arange(half, dtype=F32) / half)
    ang = jnp.arange(seq, dtype=jnp.int32).astype(F32)[:, None] * inv_freq[None, :]
    cos, sin = jnp.cos(ang), jnp.sin(ang)
    cos128 = jnp.concatenate([cos, cos, cos, cos], axis=1)
    sin128 = jnp.concatenate([-sin, sin, -sin, sin], axis=1)
    return cos128, sin128


def _rotate_half_unsigned(t):
    lane = lax.broadcasted_iota(jnp.int32, t.shape, 1)
    return jnp.where((lane % HEAD_DIM) < HEAD_DIM // 2, pltpu.roll(t, 96, 1), pltpu.roll(t, 32, 1))


def _rope(t, cos, sin_signed):
    return t * cos + _rotate_half_unsigned(t) * sin_signed


def _pair_blocks(base):
    out = []
    for j in range(4):
        for g in range(2):
            nat = base + HEAD_DIM * (4 * g + j)
            par = base + 128 * j + HEAD_DIM * g
            out.append((slice(nat, nat + HEAD_DIM), slice(par, par + HEAD_DIM)))
    return out


W_Q, W_KV, W_ZA, W_U, W_ZS = 0, 512, 768, 1280, 1792


def _proj(x, wt, cos128, sin128, tb):
    seq = x.shape[0]

    def body(x_ref, wt_ref, cos_ref, sin_ref, q_ref, k_ref, v_ref, za_ref, u_ref, zs_ref, wp):
        @pl.when(pl.program_id(0) == 0)
        def _():
            for dst_base, src_base in ((0, W_Q), (512, W_ZA)):
                for nat, par in _pair_blocks(0):
                    wp[dst_base + par.start:dst_base + par.stop, :] = wt_ref[src_base + nat.start:src_base + nat.stop, :]

        xb = x_ref[...].astype(BF16)
        cos, sin = cos_ref[...], sin_ref[...]
        lo = lax.broadcasted_iota(jnp.int32, (tb, 128), 1) < HEAD_DIM
        q = _dot_nt(xb, wp[0:512, :])
        for j in range(4):
            qj = _rope(q[:, 128 * j:128 * (j + 1)], cos, sin)
            q_ref[j] = jnp.where(lo, qj, 0.0).astype(BF16)
            q_ref[4 + j] = jnp.where(lo, 0.0, qj).astype(BF16)
        kv = _dot_nt(xb, wt_ref[W_KV:W_ZA, :])
        k_ref[...] = _rope(kv[:, 0:128], cos, sin).astype(BF16)
        v_ref[...] = kv[:, 128:256].astype(BF16)
        za_ref[...] = _dot_nt(xb, wp[512:1024, :])
        u_val = _dot_nt(xb, wt_ref[W_U:W_ZS, :])
        for k in range(N_SLAB):
            u_ref[k] = u_val[:, k * SLAB_IN:(k + 1) * SLAB_IN]
        zs_ref[...] = _dot_nt(xb, wt_ref[W_ZS:D_IN_PROJ, :])

    row = lambda w: pl.BlockSpec((tb, w), lambda i: (i, 0))
    return pl.pallas_call(
        body, name="proj", grid=(seq // tb,),
        in_specs=[row(D_MODEL), pl.BlockSpec((D_IN_PROJ, D_MODEL), lambda i: (0, 0)), row(128), row(128)],
        out_specs=[pl.BlockSpec((8, tb, 128), lambda i: (0, i, 0)), row(128), row(128), row(512),
                   pl.BlockSpec((N_SLAB, tb, SLAB_IN), lambda i: (0, i, 0)), row(512)],
        out_shape=[jax.ShapeDtypeStruct((8, seq, 128), BF16), jax.ShapeDtypeStruct((seq, 128), BF16),
                   jax.ShapeDtypeStruct((seq, 128), BF16), jax.ShapeDtypeStruct((seq, 512), F32),
                   jax.ShapeDtypeStruct((N_SLAB, seq, SLAB_IN), F32), jax.ShapeDtypeStruct((seq, 512), F32)],
        scratch_shapes=[pltpu.VMEM((1024, D_MODEL), BF16)],
        compiler_params=_cparams(("arbitrary",)),
    )(x, wt, cos128, sin128)


ATT_TQ = 128
ATT_KEYS = 3 * ATT_TQ


def _attn_window(i, seq):
    start = jnp.clip(i * ATT_TQ - WINDOW, 0, seq - ATT_KEYS)
    return pl.multiple_of(start, ATT_TQ)


def _attn_bias():
    r = jnp.arange(ATT_TQ, dtype=jnp.int32)[None, :, None]
    c = jnp.arange(ATT_KEYS, dtype=jnp.int32)[None, None, :]
    off = (jnp.arange(3, dtype=jnp.int32) * ATT_TQ)[:, None, None]
    return jnp.where(jnp.abs(r + off - c) <= WINDOW, 0.0, NEG_INF).astype(F32)


def _attn_bias_spec(nblk):
    pick = lambda i: jnp.where(i == 0, 0, jnp.where(i == nblk - 1, 2, 1))
    return pl.BlockSpec((None, ATT_TQ, ATT_KEYS), lambda i: (pick(i), 0, 0))


def _attn_softmax(q_ref, k_ref, v_ref, sink_ref, bias_ref, start):
    kw = k_ref[pl.ds(start, ATT_KEYS), :]
    vw = v_ref[pl.ds(start, ATT_KEYS), :]
    qall = q_ref[...].reshape(N_Q_HEADS * ATT_TQ, 128)
    s = (_dot_nt(qall, kw) * (HEAD_DIM ** -0.5)).reshape(N_Q_HEADS, ATT_TQ, ATT_KEYS) + bias_ref[...][None]
    tiles = [s[:, :, 128 * t:128 * (t + 1)] for t in range(ATT_KEYS // 128)]
    m = jnp.max(jnp.maximum(jnp.maximum(tiles[0], tiles[1]), tiles[2]), axis=2, keepdims=True)
    sink = sink_ref[...]
    m_b = jnp.maximum(jnp.broadcast_to(m, (N_Q_HEADS, ATT_TQ, 128)), sink)
    p = jnp.concatenate([jnp.exp(t - m_b) for t in tiles], axis=2)
    p_sink = jnp.exp(sink - m_b)
    lo_k = lax.broadcasted_iota(jnp.int32, (ATT_KEYS, 128), 1) < HEAD_DIM
    v_f = vw.astype(F32)
    v_lo, v_hi = jnp.where(lo_k, v_f, 1.0).astype(BF16), jnp.where(lo_k, 1.0, v_f).astype(BF16)
    pb = p.astype(BF16).reshape(N_Q_HEADS * ATT_TQ, ATT_KEYS)
    half = 4 * ATT_TQ
    r = jnp.concatenate([_dot(pb[:half], v_lo), _dot(pb[half:], v_hi)], axis=0).reshape(N_Q_HEADS, ATT_TQ, 128)
    return kw, vw, qall, p, p_sink, r


def _attn_fwd(q_stack, k, v, sink128, bias):
    seq = k.shape[0]

    def body(q_ref, k_ref, v_ref, sink_ref, bias_ref, o_ref):
        start = _attn_window(pl.program_id(0), seq)
        _, _, _, _, p_sink, r = _attn_softmax(q_ref, k_ref, v_ref, sink_ref, bias_ref, start)
        out = r / (pltpu.roll(r, HEAD_DIM, 2) + p_sink)
        lo = lax.broadcasted_iota(jnp.int32, (ATT_TQ, 128), 1) < HEAD_DIM
        for j in range(4):
            o_ref[:, 128 * j:128 * (j + 1)] = jnp.where(lo, out[j], out[4 + j])

    full = lambda w: pl.BlockSpec((seq, w), lambda i: (0, 0))
    return pl.pallas_call(
        body, name="attn_fwd", grid=(seq // ATT_TQ,),
        in_specs=[pl.BlockSpec((8, ATT_TQ, 128), lambda i: (0, i, 0)), full(128), full(128),
                  pl.BlockSpec((N_Q_HEADS, 1, 128), lambda i: (0, 0, 0)), _attn_bias_spec(seq // ATT_TQ)],
        out_specs=pl.BlockSpec((ATT_TQ, 512), lambda i: (i, 0)),
        out_shape=jax.ShapeDtypeStruct((seq, 512), F32),
        compiler_params=_cparams(("arbitrary",)),
    )(q_stack, k, v, sink128, bias)


def _attn_bwd(q_stack, k, v, sink128, bias, d_o):
    seq = k.shape[0]

    def body(q_ref, k_ref, v_ref, sink_ref, bias_ref, do_ref, dq_ref, dk_ref, dv_ref, dsink_ref, sink_acc):
        i = pl.program_id(0)

        @pl.when(i == 0)
        def _():
            dk_ref[...] = jnp.zeros_like(dk_ref)
            dv_ref[...] = jnp.zeros_like(dv_ref)
            sink_acc[...] = jnp.zeros_like(sink_acc)

        start = _attn_window(i, seq)
        kw, vw, qall, p, p_sink, r = _attn_softmax(q_ref, k_ref, v_ref, sink_ref, bias_ref, start)
        lo = lax.broadcasted_iota(jnp.int32, (ATT_TQ, 128), 1) < HEAD_DIM
        lo3 = lo[None]
        grp0 = lax.broadcasted_iota(jnp.int32, (N_Q_HEADS, ATT_TQ, 128), 0) < 4
        val = grp0 == lo3
        swapped = pltpu.roll(r, HEAD_DIM, 2)
        inv = 1.0 / (jnp.where(val, swapped, r) + p_sink)
        d_o_blk = do_ref[...]
        do3 = jnp.where(val, jnp.concatenate([d_o_blk[None, :, 128 * j:128 * (j + 1)] for j in range(4)] * 2, axis=0), 0.0)
        t = (do3 * r).reshape(N_Q_HEADS * ATT_TQ, 128)
        t_hi = t.astype(BF16)
        t_lo = (t - t_hi.astype(F32)).astype(BF16)
        ones = jnp.ones((128, 128), BF16)
        delta = (_dot(t_hi, ones) + _dot(t_lo, ones)).reshape(N_Q_HEADS, ATT_TQ, 128) * inv
        sink_acc[...] += -(p_sink * inv) * delta
        do_all = do3.astype(BF16).reshape(N_Q_HEADS * ATT_TQ, 128)
        dp = _dot_nt(do_all, vw).reshape(N_Q_HEADS, ATT_TQ, ATT_KEYS)
        probs, ds = [], []
        for tl in range(ATT_KEYS // 128):
            cols = slice(128 * tl, 128 * (tl + 1))
            probs_t = p[:, :, cols] * inv
            probs.append(probs_t.astype(BF16))
            ds.append((probs_t * (dp[:, :, cols] - delta)).astype(BF16))
        probs_all = jnp.concatenate(probs, axis=2).reshape(N_Q_HEADS * ATT_TQ, ATT_KEYS)
        ds_all = jnp.concatenate(ds, axis=2).reshape(N_Q_HEADS * ATT_TQ, ATT_KEYS)
        scale = HEAD_DIM ** -0.5
        dq_all = (_dot(ds_all, kw) * scale).reshape(N_Q_HEADS, ATT_TQ, 128)
        for j in range(4):
            dq_ref[:, 128 * j:128 * (j + 1)] = jnp.where(lo, dq_all[j], dq_all[4 + j])
        dk_ref[pl.ds(start, ATT_KEYS), :] += _dot_tn(ds_all, qall) * scale
        dv_ref[pl.ds(start, ATT_KEYS), :] += _dot_tn(probs_all, do_all)

        @pl.when(i == pl.num_programs(0) - 1)
        def _():
            dsink_ref[...] = jnp.sum(sink_acc[...], axis=1)

    full = lambda w: pl.BlockSpec((seq, w), lambda i: (0, 0))
    return pl.pallas_call(
        body, name="attn_bwd", grid=(seq // ATT_TQ,),
        in_specs=[pl.BlockSpec((8, ATT_TQ, 128), lambda i: (0, i, 0)), full(128), full(128),
                  pl.BlockSpec((N_Q_HEADS, 1, 128), lambda i: (0, 0, 0)),
                  _attn_bias_spec(seq // ATT_TQ), pl.BlockSpec((ATT_TQ, 512), lambda i: (i, 0))],
        out_specs=[pl.BlockSpec((ATT_TQ, 512), lambda i: (i, 0)), full(128), full(128),
                   pl.BlockSpec((N_Q_HEADS, 128), lambda i: (0, 0))],
        out_shape=[jax.ShapeDtypeStruct((seq, 512), F32), jax.ShapeDtypeStruct((seq, 128), F32),
                   jax.ShapeDtypeStruct((seq, 128), F32), jax.ShapeDtypeStruct((N_Q_HEADS, 128), F32)],
        scratch_shapes=[pltpu.VMEM((N_Q_HEADS, ATT_TQ, 128), F32)],
        compiler_params=_cparams(("arbitrary",)),
    )(q_stack, k, v, sink128, bias, d_o)


def _permute_rows(dst_ref, src_ref, sub_len):
    for k in range(N_SLAB):
        for j in range(sub_len):
            dst_ref[k, 8 * j:8 * (j + 1), :] = src_ref.at[k][pl.ds(j, SUBSEG, stride=sub_len), :]


def _unpermute_rows(dst_ref, src_ref, sub_len):
    for k in range(N_SLAB):
        for s in range(SUBSEG):
            dst_ref[k, s * sub_len:(s + 1) * sub_len, :] = src_ref.at[k][pl.ds(s, sub_len, stride=SUBSEG), :]


def _scan_pass(br_ref, bi_ref, lr_row, li_row, start, end_refs, *, sub_len, reverse, store):
    width = br_ref.shape[1]
    for c0 in range(0, width, SCAN_LANES):
        cols = slice(c0, c0 + SCAN_LANES)
        lr = jnp.broadcast_to(lr_row[:, cols], (SUBSEG, SCAN_LANES))
        li = jnp.broadcast_to(li_row[:, cols], (SUBSEG, SCAN_LANES))
        if start is None:
            init = (jnp.zeros((SUBSEG, SCAN_LANES), F32), jnp.zeros((SUBSEG, SCAN_LANES), F32))
        else:
            init = (start[0][:, cols], start[1][:, cols])

        def steps(jo, state, cols=cols, lr=lr, li=li):
            sr, si = state
            for ju in range(SCAN_UNROLL):
                jj = jo * SCAN_UNROLL + ju
                j = (sub_len - 1 - jj) if reverse else jj
                r0 = pl.multiple_of(j * SUBSEG, SUBSEG)
                nr = lr * sr - li * si + br_ref[pl.ds(r0, SUBSEG), cols]
                ni = lr * si + li * sr + bi_ref[pl.ds(r0, SUBSEG), cols]
                if store:
                    br_ref[pl.ds(r0, SUBSEG), cols] = nr
                    bi_ref[pl.ds(r0, SUBSEG), cols] = ni
                sr, si = nr, ni
            return sr, si

        sr, si = lax.fori_loop(0, sub_len // SCAN_UNROLL, steps, init)
        if end_refs is not None:
            end_refs[0][:, cols] = sr
            end_refs[1][:, cols] = si


def _resolve_starts(z_refs, carry_refs, start_refs, pr_row, pi_row, *, reverse):
    cr, ci = carry_refs[0][0:1, :], carry_refs[1][0:1, :]
    for s in (range(SUBSEG - 1, -1, -1) if reverse else range(SUBSEG)):
        start_refs[0][s:s + 1, :] = cr
        start_refs[1][s:s + 1, :] = ci
        zr, zi = z_refs[0][s:s + 1, :], z_refs[1][s:s + 1, :]
        cr, ci = pr_row * cr - pi_row * ci + zr, pr_row * ci + pi_row * cr + zi
    carry_refs[0][0:1, :] = cr
    carry_refs[1][0:1, :] = ci


def _param_specs(direction):
    row = lambda q: pl.BlockSpec((None, None, 1, STATE_W), lambda i: (q, direction, 0, 0))
    wide = lambda q: pl.BlockSpec((None, None, N_SLAB, SLAB_IN, SLAB_ST), lambda i: (q, direction, 0, 0, 0))
    tall = lambda q: pl.BlockSpec((None, None, N_SLAB, SLAB_ST, SLAB_IN), lambda i: (q, direction, 0, 0, 0))
    return [row(q) for q in range(4)], [wide(0), wide(1)], [tall(0), tall(1)]


def _ssm_fwd(u, lam, bb, cb, *, direction, tb, name):
    reverse = direction == 1
    seq = u.shape[1]
    nblk = seq // tb
    sub_len = tb // SUBSEG

    def body(u_ref, lr_ref, li_ref, pr_ref, pi_ref, bbr_ref, bbi_ref, cbr_ref, cbi_ref,
             y_ref, sr_ref, si_ref, xr, xi, up, yp, zr, zi, car, cai):
        @pl.when(pl.program_id(0) == 0)
        def _():
            car[...] = jnp.zeros_like(car)
            cai[...] = jnp.zeros_like(cai)

        _permute_rows(up, u_ref, sub_len)
        for k in range(N_SLAB):
            ub = up[k].astype(BF16)
            xr[:, k * SLAB_ST:(k + 1) * SLAB_ST] = _dot(ub, bbr_ref[k])
            xi[:, k * SLAB_ST:(k + 1) * SLAB_ST] = _dot(ub, bbi_ref[k])
        lr, li = lr_ref[...], li_ref[...]
        _scan_pass(xr, xi, lr, li, None, (zr, zi), sub_len=sub_len, reverse=reverse, store=False)
        _resolve_starts((zr, zi), (car, cai), (sr_ref, si_ref), pr_ref[...], pi_ref[...], reverse=reverse)
        _scan_pass(xr, xi, lr, li, (sr_ref, si_ref), None, sub_len=sub_len, reverse=reverse, store=True)
        for k in range(N_SLAB):
            st = slice(k * SLAB_ST, (k + 1) * SLAB_ST)
            yp[k] = _dot(xr[:, st].astype(BF16), cbr_ref[k]) - _dot(xi[:, st].astype(BF16), cbi_ref[k])
        _unpermute_rows(y_ref, yp, sub_len)

    blk = (lambda i: nblk - 1 - i) if reverse else (lambda i: i)
    rows, wide, tall = _param_specs(direction)
    tok = pl.BlockSpec((N_SLAB, tb, SLAB_IN), lambda i: (0, blk(i), 0))
    start_spec = pl.BlockSpec((None, SUBSEG, STATE_W), lambda i: (blk(i), 0, 0))
    return pl.pallas_call(
        body, name=name, grid=(nblk,),
        in_specs=[tok] + rows + wide + tall,
        out_specs=[tok, start_spec, start_spec],
        out_shape=[jax.ShapeDtypeStruct((N_SLAB, seq, SLAB_IN), F32), jax.ShapeDtypeStruct((nblk, SUBSEG, STATE_W), F32),
                   jax.ShapeDtypeStruct((nblk, SUBSEG, STATE_W), F32)],
        scratch_shapes=[pltpu.VMEM((tb, STATE_W), F32), pltpu.VMEM((tb, STATE_W), F32),
                        pltpu.VMEM((N_SLAB, tb, SLAB_IN), F32), pltpu.VMEM((N_SLAB, tb, SLAB_IN), F32),
                        pltpu.VMEM((SUBSEG, STATE_W), F32), pltpu.VMEM((SUBSEG, STATE_W), F32),
                        pltpu.VMEM((SUBSEG, STATE_W), F32), pltpu.VMEM((SUBSEG, STATE_W), F32)],
        compiler_params=_cparams(("arbitrary",)),
    )(u, lam, lam, lam, lam, bb, bb, cb, cb)


def _ssm_bwd(u, dy, starts, lam, bb, bbt, cb_t, *, direction, tb, name):
    reverse = direction == 1
    seq = u.shape[1]
    nblk = seq // tb
    sub_len = tb // SUBSEG

    def body(u_ref, dy_ref, sr_ref, si_ref, lr_ref, li_ref, pr_ref, pi_ref, bbr_ref, bbi_ref, btr_ref, bti_ref,
             ctr_ref, cti_ref, du_ref, gb_ref, gc_ref, dl_ref,
             xr, xi, gr, gi, up, dyp, dup, zr, zi, gsr, gsi, car, cai):
        gbr_ref, gbi_ref = gb_ref.at[0], gb_ref.at[1]
        gcr_ref, gci_ref = gc_ref.at[0], gc_ref.at[1]
        dlr_ref, dli_ref = dl_ref.at[0], dl_ref.at[1]

        @pl.when(pl.program_id(0) == 0)
        def _():
            for ref in (car, cai, gbr_ref, gbi_ref, gcr_ref, gci_ref, dlr_ref, dli_ref):
                ref[...] = jnp.zeros_like(ref)

        _permute_rows(up, u_ref, sub_len)
        _permute_rows(dyp, dy_ref, sub_len)
        lr, li = lr_ref[...], li_ref[...]
        for k in range(N_SLAB):
            st = slice(k * SLAB_ST, (k + 1) * SLAB_ST)
            ub = up[k].astype(BF16)
            xr[:, st] = _dot(ub, bbr_ref[k])
            xi[:, st] = _dot(ub, bbi_ref[k])
            dyb = dyp[k].astype(BF16)
            gr[:, st] = _dot(dyb, ctr_ref[k])
            gi[:, st] = -_dot(dyb, cti_ref[k])
        _scan_pass(xr, xi, lr, li, (sr_ref, si_ref), None, sub_len=sub_len, reverse=reverse, store=True)
        for k in range(N_SLAB):
            st = slice(k * SLAB_ST, (k + 1) * SLAB_ST)
            dyb = dyp[k].astype(BF16)
            gcr_ref[k] += _dot_tn(xr[:, st].astype(BF16), dyb)
            gci_ref[k] -= _dot_tn(xi[:, st].astype(BF16), dyb)
        nli = -li
        _scan_pass(gr, gi, lr, nli, None, (zr, zi), sub_len=sub_len, reverse=not reverse, store=False)
        _resolve_starts((zr, zi), (car, cai), (gsr, gsi), pr_ref[...], -pi_ref[...], reverse=not reverse)
        _scan_pass(gr, gi, lr, nli, (gsr, gsi), None, sub_len=sub_len, reverse=not reverse, store=True)
        for k in range(N_SLAB):
            st = slice(k * SLAB_ST, (k + 1) * SLAB_ST)
            ub = up[k].astype(BF16)
            grb, gib = gr[:, st].astype(BF16), gi[:, st].astype(BF16)
            gbr_ref[k] += _dot_tn(ub, grb)
            gbi_ref[k] += _dot_tn(ub, gib)
            dup[k] = _dot(grb, btr_ref[k]) + _dot(gib, bti_ref[k])
        _unpermute_rows(du_ref, dup, sub_len)

        for c0 in range(0, STATE_W, SCAN_LANES):
            cols = slice(c0, c0 + SCAN_LANES)
            edge = (sub_len - 1) * SUBSEG if reverse else 0
            g_r, g_i = gr[edge:edge + SUBSEG, cols], gi[edge:edge + SUBSEG, cols]
            x_r, x_i = sr_ref[:, cols], si_ref[:, cols]
            acc = (dlr_ref[:, cols] + (g_r * x_r + g_i * x_i), dli_ref[:, cols] + (g_i * x_r - g_r * x_i))

            def step(jj, acc, cols=cols):
                tile = lambda t: t * SUBSEG if isinstance(t, int) else pl.multiple_of(t * SUBSEG, SUBSEG)
                r_g, r_x = tile(jj if reverse else jj + 1), tile(jj + 1 if reverse else jj)
                g_r, g_i = gr[pl.ds(r_g, SUBSEG), cols], gi[pl.ds(r_g, SUBSEG), cols]
                x_r, x_i = xr[pl.ds(r_x, SUBSEG), cols], xi[pl.ds(r_x, SUBSEG), cols]
                return acc[0] + (g_r * x_r + g_i * x_i), acc[1] + (g_i * x_r - g_r * x_i)

            acc = step(0, acc)
            acc = lax.fori_loop(0, (sub_len - 2) // 2, lambda jo, a: step(2 * jo + 2, step(2 * jo + 1, a)), acc)
            dlr_ref[:, cols] = acc[0]
            dli_ref[:, cols] = acc[1]

    blk = (lambda i: i) if reverse else (lambda i: nblk - 1 - i)
    rows, wide, tall = _param_specs(direction)
    tok = pl.BlockSpec((N_SLAB, tb, SLAB_IN), lambda i: (0, blk(i), 0))
    start_spec = pl.BlockSpec((None, SUBSEG, STATE_W), lambda i: (blk(i), 0, 0))
    gb_shape, gc_shape, dl_shape = (2, N_SLAB, SLAB_IN, SLAB_ST), (2, N_SLAB, SLAB_ST, SLAB_IN), (2, SUBSEG, STATE_W)
    whole = lambda shape: pl.BlockSpec(shape, lambda i: (0,) * len(shape))
    big = lambda: pltpu.VMEM((tb, STATE_W), F32)
    slabs = lambda: pltpu.VMEM((N_SLAB, tb, SLAB_IN), F32)
    tile = lambda: pltpu.VMEM((SUBSEG, STATE_W), F32)
    return pl.pallas_call(
        body, name=name, grid=(nblk,),
        in_specs=[tok, tok, start_spec, start_spec] + rows + wide + tall + wide,
        out_specs=[tok, whole(gb_shape), whole(gc_shape), whole(dl_shape)],
        out_shape=[jax.ShapeDtypeStruct((N_SLAB, seq, SLAB_IN), F32), jax.ShapeDtypeStruct(gb_shape, F32),
                   jax.ShapeDtypeStruct(gc_shape, F32), jax.ShapeDtypeStruct(dl_shape, F32)],
        scratch_shapes=[big(), big(), big(), big(), slabs(), slabs(), slabs(),
                        tile(), tile(), tile(), tile(), tile(), tile()],
        compiler_params=_cparams(("arbitrary",)),
    )(u, dy, *starts, lam, lam, lam, lam, bb, bb, bbt, bbt, cb_t, cb_t)


GELU_C = math.sqrt(2.0 / math.pi)
GELU_K = 0.044715


def _mid(o, za, u, y_f, y_b, zs, x, target, ssm_d, w_glu, b_glu, g_attn, g_ssm, w_out, ln_g, ln_b, tb):
    seq = x.shape[0]

    def body(o_ref, za_ref, u_ref, yf_ref, yb_ref, zs_ref, x_ref, t_ref, d_ref, wg_ref, bg_ref, ga_ref, gs_ref,
             wo_ref, lg_ref, lb_ref,
             loss_ref, do_ref, dza_ref, dyl_ref, dzs_ref, dpre_ref, gwo_ref, gwg_ref, vec_ref, wop):
        @pl.when(pl.program_id(0) == 0)
        def _():
            for ref in (loss_ref, gwo_ref, gwg_ref, vec_ref):
                ref[...] = jnp.zeros_like(ref)
            for nat, par in _pair_blocks(0):
                wop[par, :] = wo_ref[nat, :]
            wop[D_ATTN:, :] = wo_ref[D_ATTN:, :]

        o, za = o_ref[...], za_ref[...]
        sig_a = _sigmoid(za)
        silu_a = za * sig_a
        ya = o * silu_a
        r_a = lax.rsqrt(jnp.mean(ya * ya, axis=1, keepdims=True) + NORM_EPS)
        n_a = ya * r_a
        g_a = ga_ref[...]
        unslab = lambda ref: jnp.concatenate([ref[k] for k in range(N_SLAB)], axis=1)
        u_blk, zs = unslab(u_ref), zs_ref[...]
        d_row = d_ref[...]
        ylin = d_row * u_blk + unslab(yf_ref) + unslab(yb_ref)
        inner = GELU_C * (ylin + GELU_K * ylin * ylin * ylin)
        th = jnp.tanh(inner)
        gl = 0.5 * ylin * (1.0 + th)
        glb = gl.astype(BF16)
        sg = _sigmoid(_dot(glb, wg_ref[...]) + bg_ref[...])
        y2 = gl * sg
        sig_s = _sigmoid(zs)
        silu_s = zs * sig_s
        ys = y2 * silu_s
        r_s = lax.rsqrt(jnp.mean(ys * ys, axis=1, keepdims=True) + NORM_EPS)
        n_s = ys * r_s
        g_s = gs_ref[...]
        mixed = jnp.concatenate([n_a * g_a, n_s * g_s], axis=1).astype(BF16)
        pre = ALPHA * x_ref[...] + _dot(mixed, wop[...])
        mu = jnp.mean(pre, axis=1, keepdims=True)
        cen = pre - mu
        rstd = lax.rsqrt(jnp.mean(cen * cen, axis=1, keepdims=True) + NORM_EPS)
        hhat = cen * rstd
        ln_g = lg_ref[...]
        err = hhat * ln_g + lb_ref[...] - t_ref[...]
        loss_ref[...] += 0.5 * jnp.sum(jnp.mean(err * err, axis=1, keepdims=True))

        dh = err * (1.0 / D_MODEL)
        vec_ref[0:1, :] += jnp.sum(dh * hhat, axis=0, keepdims=True)
        vec_ref[1:2, :] += jnp.sum(dh, axis=0, keepdims=True)
        dhh = dh * ln_g
        dpre = rstd * (dhh - jnp.mean(dhh, axis=1, keepdims=True) - hhat * jnp.mean(dhh * hhat, axis=1, keepdims=True))
        dpre_ref[...] = dpre
        dpb = dpre.astype(BF16)
        for j in range(4):
            g_pair = _dot_tn(mixed[:, 128 * j:128 * (j + 1)], dpb)
            for g in range(2):
                nat = HEAD_DIM * (4 * g + j)
                gwo_ref[nat:nat + HEAD_DIM, :] += g_pair[HEAD_DIM * g:HEAD_DIM * (g + 1), :]
        gwo_ref[D_ATTN:, :] += _dot_tn(mixed[:, D_ATTN:], dpb)
        dmix = _dot_nt(dpb, wop[...])
        dna = dmix[:, :D_ATTN]
        vec_ref[2:3, 0:D_ATTN] += jnp.sum(dna * n_a, axis=0, keepdims=True)
        dna = dna * g_a
        dya = r_a * (dna - n_a * jnp.mean(dna * n_a, axis=1, keepdims=True))
        do_ref[...] = dya * silu_a
        dza_ref[...] = dya * o * (sig_a * (1.0 + za * (1.0 - sig_a)))
        dns = dmix[:, D_ATTN:]
        vec_ref[2:3, D_ATTN:] += jnp.sum(dns * n_s, axis=0, keepdims=True)
        dns = dns * g_s
        dys = r_s * (dns - n_s * jnp.mean(dns * n_s, axis=1, keepdims=True))
        dzs_ref[...] = dys * y2 * (sig_s * (1.0 + zs * (1.0 - sig_s)))
        dy2 = dys * silu_s
        da = dy2 * gl * sg * (1.0 - sg)
        vec_ref[3:4, D_SSM:] += jnp.sum(da, axis=0, keepdims=True)
        dab = da.astype(BF16)
        gwg_ref[...] += _dot_tn(glb, dab)
        dgl = dy2 * sg + _dot_nt(dab, wg_ref[...])
        dylin = dgl * (0.5 * (1.0 + th) + 0.5 * ylin * (1.0 - th * th) * GELU_C * (1.0 + 3.0 * GELU_K * ylin * ylin))
        for k in range(N_SLAB):
            dyl_ref[k] = dylin[:, k * SLAB_IN:(k + 1) * SLAB_IN]
        vec_ref[3:4, 0:D_SSM] += jnp.sum(dylin * u_blk, axis=0, keepdims=True)

    tok = lambda w: pl.BlockSpec((tb, w), lambda i: (i, 0))
    slab = pl.BlockSpec((N_SLAB, tb, SLAB_IN), lambda i: (0, i, 0))
    const = lambda r, c: pl.BlockSpec((r, c), lambda i: (0, 0))
    tok_shape = jax.ShapeDtypeStruct((seq, 512), F32)
    return pl.pallas_call(
        body, name="mid", grid=(seq // tb,),
        in_specs=[tok(512), tok(512), slab, slab, slab, tok(512), tok(1024), tok(1024),
                  const(1, 512), const(512, 512), const(1, 512), const(1, 512), const(1, 512),
                  const(1024, 1024), const(1, 1024), const(1, 1024)],
        out_specs=[const(8, 128), tok(512), tok(512), slab, tok(512), tok(1024),
                   const(1024, 1024), const(512, 512), const(8, 1024)],
        out_shape=[jax.ShapeDtypeStruct((8, 128), F32), tok_shape, tok_shape,
                   jax.ShapeDtypeStruct((N_SLAB, seq, SLAB_IN), F32), tok_shape,
                   jax.ShapeDtypeStruct((seq, 1024), F32), jax.ShapeDtypeStruct((1024, 1024), F32),
                   jax.ShapeDtypeStruct((512, 512), F32), jax.ShapeDtypeStruct((8, 1024), F32)],
        scratch_shapes=[pltpu.VMEM((D_MODEL, D_MODEL), BF16)],
        compiler_params=_cparams(("arbitrary",)),
    )(o, za, u, y_f, y_b, zs, x, target, ssm_d, w_glu, b_glu, g_attn, g_ssm, w_out, ln_g, ln_b)


def _proj_bwd(x, dq, dk, dv, dza, du_f, du_b, dylin, dzs, dpre, ssm_d, cos128, sin128, wt, tb):
    seq = x.shape[0]

    def body(x_ref, dq_ref, dk_ref, dv_ref, dza_ref, duf_ref, dub_ref, dyl_ref, dzs_ref, dpre_ref, d_ref,
             cos_ref, sin_ref, wt_ref, gx_ref, gw_ref, wp):
        @pl.when(pl.program_id(0) == 0)
        def _():
            gw_ref[...] = jnp.zeros_like(gw_ref)
            for base in (W_Q, W_ZA):
                for nat, par in _pair_blocks(base):
                    wp[par, :] = wt_ref[nat, :]
            wp[W_KV:W_ZA, :] = wt_ref[W_KV:W_ZA, :]
            wp[W_U:, :] = wt_ref[W_U:, :]

        cos, sin = cos_ref[...], sin_ref[...]

        def unrope(t):
            return t * cos + _rotate_half_unsigned(t * sin)

        dq_rot = dq_ref[...]
        pieces = [unrope(dq_rot[:, 128 * j:128 * (j + 1)]) for j in range(4)]
        d_row = d_ref[...]
        pieces += [unrope(dk_ref[...]), dv_ref[...], dza_ref[...]]
        pieces += [duf_ref[k] + dub_ref[k] + d_row[:, k * SLAB_IN:(k + 1) * SLAB_IN] * dyl_ref[k] for k in range(N_SLAB)]
        pieces += [dzs_ref[...]]
        dproj = jnp.concatenate(pieces, axis=1).astype(BF16)
        gx_ref[...] = ALPHA * dpre_ref[...] + _dot(dproj, wp[...])
        xb = x_ref[...].astype(BF16)
        for base in (W_Q, W_ZA):
            for j in range(4):
                g_pair = _dot_tn(dproj[:, base + 128 * j:base + 128 * (j + 1)], xb)
                for g in range(2):
                    nat = base + HEAD_DIM * (4 * g + j)
                    gw_ref[nat:nat + HEAD_DIM, :] += g_pair[HEAD_DIM * g:HEAD_DIM * (g + 1), :]
        gw_ref[W_KV:W_ZA, :] += _dot_tn(dproj[:, W_KV:W_ZA], xb)
        gw_ref[W_U:, :] += _dot_tn(dproj[:, W_U:], xb)

    tok = lambda w: pl.BlockSpec((tb, w), lambda i: (i, 0))
    slab = pl.BlockSpec((N_SLAB, tb, SLAB_IN), lambda i: (0, i, 0))
    const = lambda r, c: pl.BlockSpec((r, c), lambda i: (0, 0))
    return pl.pallas_call(
        body, name="proj_bwd", grid=(seq // tb,),
        in_specs=[tok(1024), tok(512), tok(128), tok(128), tok(512), slab, slab, slab, tok(512), tok(1024),
                  const(1, 512), tok(128), tok(128), const(D_IN_PROJ, D_MODEL)],
        out_specs=[tok(1024), const(D_IN_PROJ, D_MODEL)],
        out_shape=[jax.ShapeDtypeStruct((seq, D_MODEL), F32), jax.ShapeDtypeStruct((D_IN_PROJ, D_MODEL), F32)],
        scratch_shapes=[pltpu.VMEM((D_IN_PROJ, D_MODEL), BF16)],
        compiler_params=_cparams(("arbitrary",)),
    )(x, dq, dk, dv, dza, du_f, du_b, dylin, dzs, dpre, ssm_d, cos128, sin128, wt)


def _adamw(w, g, m, v, name):
    rows, cols = w.shape
    tb = rows
    while tb * cols * 4 > ADAMW_BLOCK_BYTES and tb % 16 == 0:
        tb //= 2

    def body(w_ref, g_ref, m_ref, v_ref, d_ref, nm_ref, nv_ref):
        _adamw_update(w_ref, g_ref, m_ref, v_ref, d_ref, nm_ref, nv_ref)

    spec = pl.BlockSpec((tb, cols), lambda i: (i, 0))
    return pl.pallas_call(
        body, name=name, grid=(rows // tb,), in_specs=[spec] * 4, out_specs=[spec] * 3,
        out_shape=[jax.ShapeDtypeStruct((rows, cols), F32)] * 3,
        compiler_params=_cparams(("arbitrary",)),
    )(w, g, m, v)


def _adamw_update(w_ref, g_ref, m_ref, v_ref, d_ref, nm_ref, nv_ref):
    g_blk = g_ref[...]
    m_new = ADAM_B1 * m_ref[...] + (1.0 - ADAM_B1) * g_blk
    v_new = ADAM_B2 * v_ref[...] + (1.0 - ADAM_B2) * (g_blk * g_blk)
    m_hat = m_new / (1.0 - ADAM_B1 ** ADAM_STEP)
    v_hat = v_new / (1.0 - ADAM_B2 ** ADAM_STEP)
    d_ref[...] = -ADAM_LR * (m_hat / (jnp.sqrt(v_hat) + ADAM_EPS) + ADAM_WD * w_ref[...])
    nm_ref[...] = m_new
    nv_ref[...] = v_new


def _adamw_many(groups, name):
    n = len(groups)

    def body(*refs):
        for p in range(n):
            _adamw_update(*refs[4 * p:4 * p + 4], *refs[4 * n + 3 * p:4 * n + 3 * p + 3])

    return pl.pallas_call(
        body, name=name,
        out_shape=[jax.ShapeDtypeStruct(grp[0].shape, F32) for grp in groups for _ in range(3)],
    )(*[a for grp in groups for a in grp])


_WEIGHTS = ["w_in", "attn_sink", "ssm_a_re", "ssm_a_im", "ssm_log_dt", "ssm_b_re", "ssm_b_im", "ssm_c_re", "ssm_c_im",
            "ssm_d", "w_glu", "b_glu", "norm_attn_g", "norm_ssm_g", "w_out", "ln_g", "ln_b"]
N_DG = N_DIR * N_GROUPS
BIG_ROWS = N_DG * SSM_CH * SSM_STATE // 128
TINY_ROWS = 64


def _slabs_wide(t):
    eye = jnp.eye(8, dtype=t.dtype)
    return jnp.einsum("rdkgcp,gh->rdkgchp", t.reshape(2, N_DIR, N_SLAB, 8, SSM_CH, SSM_STATE), eye).reshape(
        2, N_DIR, N_SLAB, SLAB_IN, SLAB_ST)


def _slabs_tall(t):
    eye = jnp.eye(8, dtype=t.dtype)
    return jnp.einsum("rdkgcp,gh->rdkhpgc", t.reshape(2, N_DIR, N_SLAB, 8, SSM_CH, SSM_STATE), eye).reshape(
        2, N_DIR, N_SLAB, SLAB_ST, SLAB_IN)


def _wide_diagonal(t):
    eye = jnp.eye(8, dtype=t.dtype)
    return jnp.einsum("rdkgchp,gh->rdkgcp", t.reshape(2, N_DIR, N_SLAB, 8, SSM_CH, 8, SSM_STATE), eye).reshape(
        2, N_DIR, N_GROUPS, SSM_CH, SSM_STATE)


def _tall_diagonal(t):
    eye = jnp.eye(8, dtype=t.dtype)
    return jnp.einsum("rdkhpgc,gh->rdkgcp", t.reshape(2, N_DIR, N_SLAB, 8, SSM_STATE, 8, SSM_CH), eye).reshape(
        2, N_DIR, N_GROUPS, SSM_CH, SSM_STATE)


def _pack_small_grads(g_b, g_c, g_vec, g_ar, g_ai, g_dt, g_sink, loss):
    big = jnp.concatenate([g_b.reshape(2, BIG_ROWS, 128), g_c.reshape(2, BIG_ROWS, 128)], axis=0)
    row = lambda t: jnp.pad(t.reshape(1, -1), ((0, 0), (0, 128 - t.size)))
    tiny = jnp.concatenate([g_vec.reshape(64, 128), g_ar.reshape(32, 128), g_ai.reshape(32, 128), row(g_dt), row(g_sink),
                            row(loss), jnp.zeros((N_CHIPS * TINY_ROWS - 131, 128), F32)], axis=0)
    return jnp.concatenate([big, tiny.reshape(N_CHIPS, TINY_ROWS, 128)], axis=1)


def _unpack_small_grads(packed):
    big = packed[:, :BIG_ROWS].reshape(N_CHIPS, 2 * BIG_ROWS, SSM_STATE)
    tiny = packed[:, BIG_ROWS:].reshape(N_CHIPS * TINY_ROWS, 128)
    g_vec = tiny[0:64].reshape(8, 1024)
    return tiny[130, 0], {
        "ssm_b_re": big[0], "ssm_b_im": big[1], "ssm_c_re": big[2], "ssm_c_im": big[3],
        "ln_g": g_vec[0:1], "ln_b": g_vec[1:2],
        "norm_attn_g": g_vec[2:3, :D_ATTN][:, _PAIR_INV], "norm_ssm_g": g_vec[2:3, D_ATTN:],
        "ssm_d": g_vec[3:4, :D_SSM], "b_glu": g_vec[3:4, D_SSM:],
        "ssm_a_re": tiny[64:96].reshape(N_DG, SSM_STATE), "ssm_a_im": tiny[96:128].reshape(N_DG, SSM_STATE),
        "ssm_log_dt": tiny[128:129, :N_DG].reshape(N_DIR, N_GROUPS), "attn_sink": tiny[129:130, :N_Q_HEADS],
    }


def _small_view(name, t):
    if name in ("ssm_b_re", "ssm_b_im"):
        return jnp.swapaxes(t[0], 2, 3).reshape(N_DG * SSM_CH, SSM_STATE)
    if name in ("ssm_c_re", "ssm_c_im"):
        return t.reshape(N_DG * SSM_CH, SSM_STATE)
    if name in ("ssm_a_re", "ssm_a_im"):
        return t.reshape(N_DG, SSM_STATE)
    if name == "ssm_log_dt":
        return t.reshape(N_DIR, N_GROUPS)
    return t.reshape(1, -1)


def _small_unview(name, t, shape):
    if name in ("ssm_b_re", "ssm_b_im"):
        return jnp.swapaxes(t.reshape(N_DIR, N_GROUPS, SSM_CH, SSM_STATE), 2, 3).reshape(shape)
    return t.reshape(shape)


def kernel(x, w_in, attn_sink, ssm_a_re, ssm_a_im, ssm_log_dt, ssm_b_re, ssm_b_im, ssm_c_re, ssm_c_im, ssm_d, w_glu, b_glu, norm_attn_g, norm_ssm_g, w_out, ln_g, ln_b, loss_target, m_w_in, m_attn_sink, m_ssm_a_re, m_ssm_a_im, m_ssm_log_dt, m_ssm_b_re, m_ssm_b_im, m_ssm_c_re, m_ssm_c_im, m_ssm_d, m_w_glu, m_b_glu, m_norm_attn_g, m_norm_ssm_g, m_w_out, m_ln_g, m_ln_b, v_w_in, v_attn_sink, v_ssm_a_re, v_ssm_a_im, v_ssm_log_dt, v_ssm_b_re, v_ssm_b_im, v_ssm_c_re, v_ssm_c_im, v_ssm_d, v_w_glu, v_b_glu, v_norm_attn_g, v_norm_ssm_g, v_w_out, v_ln_g, v_ln_b):
    args = dict(locals())
    weights = {n: args[n] for n in _WEIGHTS}
    mom_m = {n: args["m_" + n] for n in _WEIGHTS}
    mom_v = {n: args["v_" + n] for n in _WEIGHTS}
    xs = x[0]
    target = loss_target[0]

    wt_g, w_glu_g, w_out_g = _all_gather_chips([w_in[0].T, w_glu[0], w_out[0]], BF16, "gather_weights")
    wt_full = wt_g.reshape(D_IN_PROJ, D_MODEL)
    w_glu_full = w_glu_g.reshape(D_SSM, D_SSM)
    w_out_full = w_out_g.reshape(D_MODEL, D_MODEL)

    g_x, g_wt, g_w_out, g_w_glu, g_small = _local_step(
        xs, target, wt_full, w_glu_full, w_out_full, attn_sink, ssm_a_re, ssm_a_im, ssm_log_dt, ssm_b_re, ssm_b_im,
        ssm_c_re, ssm_c_im, ssm_d, b_glu, norm_attn_g, norm_ssm_g, ln_g, ln_b)

    r_wt, r_w_out, r_w_glu, g_small_all = _reduce_all(
        [g_wt.reshape(N_CHIPS, -1, D_MODEL), g_w_out.reshape(N_CHIPS, -1, D_MODEL), g_w_glu.reshape(N_CHIPS, -1, D_SSM),
         g_small], [True, True, True, False], "reduce_grads")
    loss, small_grads = _unpack_small_grads(g_small_all)

    grads, deltas, new_m, new_v = {}, {}, {}, {}
    d_w, m_w, v_w = _adamw(w_in[0].T, r_wt, m_w_in[0].T, v_w_in[0].T, "adamw_w_in")
    grads["w_in"], deltas["w_in"], new_m["w_in"], new_v["w_in"] = r_wt.T[None], d_w.T[None], m_w.T[None], v_w.T[None]
    for n, g in (("w_out", r_w_out), ("w_glu", r_w_glu)):
        d_w, m_w, v_w = _adamw(weights[n][0], g, mom_m[n][0], mom_v[n][0], "adamw_" + n)
        grads[n], deltas[n], new_m[n], new_v[n] = g[None], d_w[None], m_w[None], v_w[None]
    names = sorted(small_grads)
    updates = _adamw_many([(_small_view(n, weights[n]), small_grads[n], _small_view(n, mom_m[n]), _small_view(n, mom_v[n]))
                           for n in names], "adamw_small")
    for i, n in enumerate(names):
        shape = weights[n].shape
        grads[n] = _small_unview(n, small_grads[n], shape)
        deltas[n], new_m[n], new_v[n] = (_small_unview(n, t, shape) for t in updates[3 * i:3 * i + 3])

    return (loss, g_x[None], *[grads[n] for n in _WEIGHTS], *[deltas[n] for n in _WEIGHTS],
            *[new_m[n] for n in _WEIGHTS], *[new_v[n] for n in _WEIGHTS])


def _local_step(xs, target, wt_full, w_glu_full, w_out_full, attn_sink, ssm_a_re, ssm_a_im, ssm_log_dt, ssm_b_re,
                ssm_b_im, ssm_c_re, ssm_c_im, ssm_d, b_glu, norm_attn_g, norm_ssm_g, ln_g, ln_b):
    seq = xs.shape[0]

    a_r = ssm_a_re.reshape(N_DG, 1, SSM_STATE)
    a_i = ssm_a_im.reshape(N_DG, 1, SSM_STATE)
    log_dt = ssm_log_dt.reshape(N_DG, 1, 1)
    b_r = jnp.swapaxes(ssm_b_re[0], 2, 3).reshape(N_DG, SSM_CH, SSM_STATE)
    b_i = jnp.swapaxes(ssm_b_im[0], 2, 3).reshape(N_DG, SSM_CH, SSM_STATE)
    ssm_tb = min(256, seq)
    sub_len = ssm_tb // SUBSEG
    lam, bbar = _ssm_params_fwd(a_r, a_i, log_dt, b_r, b_i, int(math.log2(sub_len)))
    lam = lam.reshape(4, N_DIR, 1, STATE_W)
    bbar = bbar.astype(BF16).reshape(2, N_DIR, N_GROUPS, SSM_CH, SSM_STATE)
    c_both = jnp.stack([ssm_c_re[0], ssm_c_im[0]]).astype(BF16)
    bb, bbt = _slabs_wide(bbar), _slabs_tall(bbar)
    cb, cb_t = _slabs_tall(c_both), _slabs_wide(c_both)

    cos128, sin128 = _rope_tables(seq)
    q_stack, k_rot, v_bf, z_attn, u, z_ssm = _proj(xs, wt_full, cos128, sin128, min(512, seq))
    sink128 = jnp.broadcast_to(attn_sink[0][:, None, None], (N_Q_HEADS, 1, 128))
    attn_bias = _attn_bias()
    o = _attn_fwd(q_stack, k_rot, v_bf, sink128, attn_bias)
    ys, starts = [], []
    for d in range(N_DIR):
        y_d, s_r, s_i = _ssm_fwd(u, lam, bb, cb, direction=d, tb=ssm_tb, name=f"ssm_fwd_{d}")
        ys.append(y_d)
        starts.append((s_r, s_i))

    row = lambda t: t.reshape(1, -1)
    g_attn_p = row(norm_attn_g)[:, _PAIR_PERM]
    loss_blk, d_o, d_za, d_ylin, d_zs, d_pre, g_w_out, g_w_glu, g_vec = _mid(
        o, z_attn, u, ys[0], ys[1], z_ssm, xs, target, row(ssm_d), w_glu_full, row(b_glu),
        g_attn_p, row(norm_ssm_g), w_out_full, row(ln_g), row(ln_b), min(256, seq))

    dq, dk, dv, g_sink = _attn_bwd(q_stack, k_rot, v_bf, sink128, attn_bias, d_o)
    dus, g_bb, g_cb, g_lam = [], [], [], []
    for d in range(N_DIR):
        du_d, gb_d, gc_d, dl_d = _ssm_bwd(u, d_ylin, starts[d], lam, bb, bbt, cb_t, direction=d, tb=ssm_tb,
                                          name=f"ssm_bwd_{d}")
        dus.append(du_d)
        g_bb.append(gb_d)
        g_cb.append(gc_d)
        g_lam.append(dl_d)
    g_bbar = _wide_diagonal(jnp.stack(g_bb, axis=1)).reshape(2, N_DG, SSM_CH, SSM_STATE)
    g_c = _tall_diagonal(jnp.stack(g_cb, axis=1))
    g_lam = jnp.swapaxes(jnp.stack(g_lam, axis=1).reshape(2, N_DIR, SUBSEG, N_GROUPS, SSM_STATE), 2, 3).reshape(
        2, N_DG, SUBSEG, SSM_STATE)
    g_ar, g_ai, g_dt, g_br, g_bi = _ssm_params_bwd(a_r, a_i, log_dt, b_r, b_i, g_lam[0], g_lam[1], g_bbar[0], g_bbar[1])

    g_x, g_wt = _proj_bwd(xs, dq, dk, dv, d_za, dus[0], dus[1], d_ylin, d_zs, d_pre, row(ssm_d), cos128, sin128,
                          wt_full, min(256, seq))

    g_b = jnp.stack([g_br, g_bi]).reshape(2, N_DIR, N_GROUPS, SSM_CH, SSM_STATE)
    g_small = _pack_small_grads(g_b, g_c, g_vec, g_ar, g_ai, g_dt, g_sink[:, 0], loss_blk[0, 0])
    return g_x, g_wt, g_w_out, g_w_glu, g_small
```

```python
import functools
import math

import numpy as np
import jax
import jax.numpy as jnp
from jax import lax
from jax.experimental import pallas as pl
from jax.experimental.pallas import tpu as pltpu

F32 = jnp.float32
BF16 = jnp.bfloat16
MESH = pl.DeviceIdType.MESH

D_MODEL = 1024
D_ATTN = 512
D_SSM = 512
HEAD_DIM = 64
N_Q_HEADS = 8
WINDOW = 128
ROPE_THETA = 10000.0
SSM_CH = 16
N_GROUPS = 32
SSM_STATE = 64
N_DIR = 2
STATE_W = N_GROUPS * SSM_STATE
N_SLAB = 4
SLAB_IN = 128
SLAB_ST = 512
NORM_EPS = 1e-5
NEG_INF = -1e30
ALPHA = 2.0 ** 0.25
D_IN_PROJ = 2304
N_CHIPS = 4

ADAM_LR = 0.001
ADAM_B1 = 0.9
ADAM_B2 = 0.999
ADAM_EPS = 1e-08
ADAM_WD = 0.01
ADAM_STEP = 10

SUBSEG = 8
SCAN_LANES = 512
SCAN_UNROLL = 4
VMEM_LIMIT = 48 * 1024 * 1024
ADAMW_BLOCK_BYTES = 3 * 512 * 1024

_PAIR_PERM = np.array([(64 * j + l) if l < 64 else (64 * (j + 4) + l - 64) for j in range(4) for l in range(128)])
_PAIR_INV = np.argsort(_PAIR_PERM)


def _cparams(sem=None):
    return pltpu.CompilerParams(dimension_semantics=sem, vmem_limit_bytes=VMEM_LIMIT)


def _dot(a, b):
    return jnp.dot(a, b, preferred_element_type=F32)


def _dot_nt(a, b):
    return lax.dot_general(a, b, (((1,), (1,)), ((), ())), preferred_element_type=F32)


def _dot_tn(a, b):
    return lax.dot_general(a, b, (((0,), (0,)), ((), ())), preferred_element_type=F32)


def _sigmoid(z):
    return 1.0 / (1.0 + jnp.exp(-z))


def _all_gather_chips(shards, out_dtype, name):
    n = len(shards)

    def body(*refs):
        in_refs, out_refs = refs[:n], refs[n:2 * n]
        send_sems, recv_sems = refs[2 * n:]
        x, y, c = lax.axis_index("x"), lax.axis_index("y"), lax.axis_index("c")
        sibling = (x, y, 1 - c)
        chips = [(1 - x, y), (x, 1 - y), (1 - x, 1 - y)]

        for a in range(n):
            out_refs[a][2 * x + y] = in_refs[a][...].astype(out_dtype)

        def half_of(a, px, py, half):
            rows = in_refs[a].shape[0] // 2
            return out_refs[a].at[2 * px + py, pl.ds(half * rows, rows), :]

        def copy(a, k, px, py, half, to):
            blk = half_of(a, px, py, half)
            return pltpu.make_async_remote_copy(src_ref=blk, dst_ref=blk, send_sem=send_sems.at[6 * a + k],
                                                recv_sem=recv_sems.at[6 * a + k], device_id=to, device_id_type=MESH)

        first = [copy(a, j, x, y, c, (*chips[j], c)) for a in range(n) for j in range(3)]
        for cp in first:
            cp.start()
        passed = []
        for a in range(n):
            for j in range(3):
                copy(a, j, *chips[j], c, (x, y, c)).wait_recv()
                fwd = copy(a, 3 + j, *chips[j], c, sibling)
                fwd.start()
                passed.append(fwd)
        for a in range(n):
            for j in range(3):
                copy(a, 3 + j, *chips[j], 1 - c, (x, y, c)).wait_recv()
        for cp in first + passed:
            cp.wait_send()

    vmem = pl.BlockSpec(memory_space=pltpu.VMEM)
    return pl.pallas_call(
        body, name=name,
        out_shape=[jax.ShapeDtypeStruct((N_CHIPS,) + s.shape, out_dtype) for s in shards],
        in_specs=[vmem] * n, out_specs=[vmem] * n,
        scratch_shapes=[pltpu.SemaphoreType.DMA((6 * n,)), pltpu.SemaphoreType.DMA((6 * n,))],
        compiler_params=pltpu.CompilerParams(vmem_limit_bytes=VMEM_LIMIT),
    )(*shards)


SEMS_PER_ARRAY = 11


def _reduce_all(pieces, narrow, name):
    n = len(pieces)
    halves = [p.shape[1] // 2 for p in pieces]
    wire = [BF16 if nar else F32 for nar in narrow]

    def body(*refs):
        p_refs, out_refs = refs[:n], refs[n:2 * n]
        a_refs, s_refs, b_refs = refs[2 * n:3 * n], refs[3 * n:4 * n], refs[4 * n:5 * n]
        send_sems, recv_sems = refs[5 * n:]
        x, y, c = lax.axis_index("x"), lax.axis_index("y"), lax.axis_index("c")
        me = 2 * x + y
        sibling = (x, y, 1 - c)
        chips = [(1 - x, y), (x, 1 - y), (1 - x, 1 - y)]
        slot = [2 * px + py for px, py in chips]

        def copy(a, k, src, dst, to):
            return pltpu.make_async_remote_copy(src_ref=src, dst_ref=dst, send_sem=send_sems.at[SEMS_PER_ARRAY * a + k],
                                                recv_sem=recv_sems.at[SEMS_PER_ARRAY * a + k],
                                                device_id=to, device_id_type=MESH)

        def rows(a, half):
            return pl.ds(pl.multiple_of(half * halves[a], 16), halves[a])

        started = []
        for a in range(n):
            cp = copy(a, 0, p_refs[a].at[:, rows(a, 1 - c), :], a_refs[a], sibling)
            cp.start()
            started.append(cp)
        for a in range(n):
            started[a].wait_recv()
            for k in range(N_CHIPS):
                acc = a_refs[a][k] + p_refs[a][k, rows(a, c), :]
                a_refs[a][k] = acc
                s_refs[a][k] = acc.astype(wire[a])
            b_refs[a][me] = s_refs[a][me]
            for j in range(3):
                cp = copy(a, 1 + j, s_refs[a].at[slot[j]], b_refs[a].at[me], (*chips[j], c))
                cp.start()
                started.append(cp)
        for a in range(n):
            for j in range(3):
                copy(a, 1 + j, s_refs[a].at[slot[j]], b_refs[a].at[slot[j]], (x, y, c)).wait_recv()
            terms = [jnp.where(me == k, a_refs[a][k], b_refs[a][k].astype(F32)) for k in range(N_CHIPS)]
            total = (terms[0] + terms[1]) + (terms[2] + terms[3])
            if a < n - 1:
                done = out_refs[a].at[rows(a, c), :]
                out_refs[a][rows(a, c), :] = total
            else:
                done = out_refs[a].at[me, rows(a, c), :]
                out_refs[a][me, rows(a, c), :] = total
            cp = copy(a, 4, done, done, sibling)
            cp.start()
            started.append(cp)
        last = n - 1
        piece = lambda k, half: out_refs[last].at[k, rows(last, half), :]
        for j in range(3):
            cp = copy(last, 5 + j, piece(me, c), piece(me, c), (*chips[j], c))
            cp.start()
            started.append(cp)
        for j in range(3):
            copy(last, 5 + j, piece(slot[j], c), piece(slot[j], c), (x, y, c)).wait_recv()
            cp = copy(last, 8 + j, piece(slot[j], c), piece(slot[j], c), sibling)
            cp.start()
            started.append(cp)
        for a in range(n - 1):
            copy(a, 4, out_refs[a].at[rows(a, 1 - c), :], out_refs[a].at[rows(a, 1 - c), :], (x, y, c)).wait_recv()
        copy(last, 4, piece(me, 1 - c), piece(me, 1 - c), (x, y, c)).wait_recv()
        for j in range(3):
            copy(last, 8 + j, piece(slot[j], 1 - c), piece(slot[j], 1 - c), (x, y, c)).wait_recv()
        for cp in started:
            cp.wait_send()

    vmem = pl.BlockSpec(memory_space=pltpu.VMEM)
    half_shape = lambda a: (N_CHIPS, halves[a], pieces[a].shape[2])
    return pl.pallas_call(
        body, name=name,
        out_shape=[jax.ShapeDtypeStruct(p.shape[1:] if a < n - 1 else p.shape, F32) for a, p in enumerate(pieces)],
        in_specs=[vmem] * n, out_specs=[vmem] * n,
        scratch_shapes=[pltpu.VMEM(half_shape(a), F32) for a in range(n)]
        + [pltpu.VMEM(half_shape(a), wire[a]) for a in range(n)]
        + [pltpu.VMEM(half_shape(a), wire[a]) for a in range(n)]
        + [pltpu.SemaphoreType.DMA((SEMS_PER_ARRAY * n,)), pltpu.SemaphoreType.DMA((SEMS_PER_ARRAY * n,))],
        compiler_params=pltpu.CompilerParams(vmem_limit_bytes=VMEM_LIMIT),
    )(*pieces)


def _ssm_param_values(ar, ai, logdt):
    dt = jnp.exp(logdt)
    mag = jnp.exp(dt * ar)
    cs, sn = jnp.cos(dt * ai), jnp.sin(dt * ai)
    lr, li = mag * cs, mag * sn
    den = ar * ar + ai * ai
    nr = (lr - 1.0) * ar + li * ai
    ni = li * ar - (lr - 1.0) * ai
    return dt, mag, lr, li, den, nr, ni


GROUPS_PER_SLAB = N_GROUPS // N_SLAB


def _slab_masks():
    def eq(shape, f_row, f_col):
        return (f_row(lax.broadcasted_iota(jnp.int32, shape, 0)) == f_col(lax.broadcasted_iota(jnp.int32, shape, 1))).astype(F32)
    spread = eq((SSM_STATE, SLAB_ST), lambda r: r, lambda c: c % SSM_STATE)
    spread_t = eq((SLAB_ST, SSM_STATE), lambda r: r % SSM_STATE, lambda c: c)
    keep = eq((SLAB_IN, SLAB_ST), lambda r: r // SSM_CH, lambda c: c // SSM_STATE)
    keep_t = eq((SLAB_ST, SLAB_IN), lambda r: r // SSM_STATE, lambda c: c // SSM_CH)
    repeat = eq((N_DG * SSM_CH, N_DG), lambda r: r // SSM_CH, lambda c: c)
    return spread, spread_t, keep, keep_t, repeat


def _split3(t):
    hi = t.astype(BF16)
    rest = t - hi.astype(F32)
    mid = rest.astype(BF16)
    return hi, mid, (rest - mid.astype(F32)).astype(BF16)


def _select(dot, ones01, t, ones_first):
    o = ones01.astype(BF16)
    parts = [dot(o, p) if ones_first else dot(p, o) for p in _split3(t)]
    return (parts[0] + parts[1]) + parts[2]


def _ssm_params_fwd(ar, ai, logdt, br, bi, cr, ci, n_square):
    def body(ar_ref, ai_ref, dt_ref, br_ref, bi_ref, cr_ref, ci_ref, lam_ref, bb_ref, bbt_ref, cb_ref, cbt_ref):
        _, _, lr, li, den, nr, ni = _ssm_param_values(ar_ref[...], ai_ref[...], dt_ref[...])
        lam_ref[0] = lr
        lam_ref[1] = li
        pr, pi = lr, li
        for _ in range(n_square):
            pr, pi = pr * pr - pi * pi, 2.0 * pr * pi
        lam_ref[2] = pr
        lam_ref[3] = pi
        spread, spread_t, keep, keep_t, repeat = _slab_masks()
        fr = _select(_dot, repeat, nr / den, True)
        fi = _select(_dot, repeat, ni / den, True)
        b_r, b_i = br_ref[...], bi_ref[...]
        bbar = (fr * b_r - fi * b_i, fr * b_i + fi * b_r)
        c_par = (cr_ref[...], ci_ref[...])
        spread, spread_t = spread.astype(BF16), spread_t.astype(BF16)
        for src, wide_ref, tall_ref in ((bbar, bb_ref, bbt_ref), (c_par, cbt_ref, cb_ref)):
            for q in range(2):
                for d in range(N_DIR):
                    for k in range(N_SLAB):
                        r0 = (d * N_GROUPS + k * GROUPS_PER_SLAB) * SSM_CH
                        blk = src[q][r0:r0 + SLAB_IN].astype(BF16)
                        wide_ref[q, d, k] = (_dot(blk, spread) * keep).astype(BF16)
                        tall_ref[q, d, k] = (_dot_nt(spread_t, blk) * keep_t).astype(BF16)

    wide = jax.ShapeDtypeStruct((2, N_DIR, N_SLAB, SLAB_IN, SLAB_ST), BF16)
    tall = jax.ShapeDtypeStruct((2, N_DIR, N_SLAB, SLAB_ST, SLAB_IN), BF16)
    return pl.pallas_call(body, name="ssm_params_fwd",
                          out_shape=[jax.ShapeDtypeStruct((4,) + ar.shape, F32), wide, tall, tall, wide],
                          compiler_params=pltpu.CompilerParams(vmem_limit_bytes=VMEM_LIMIT),
                          )(ar, ai, logdt, br, bi, cr, ci)


def _ssm_params_bwd(ar, ai, logdt, br, bi, g_slabs_b, g_slabs_c, g_lam):
    def body(ar_ref, ai_ref, dt_ref, br_ref, bi_ref, gb0_ref, gb1_ref, gc0_ref, gc1_ref, gl0_ref, gl1_ref,
             gar_ref, gai_ref, gdt_ref, gbr_ref, gbi_ref, gcr_ref, gci_ref, dbb, dlam):
        spread, spread_t, keep, keep_t, repeat = _slab_masks()
        for d, (gb_ref, gc_ref) in enumerate(((gb0_ref, gc0_ref), (gb1_ref, gc1_ref))):
            for q in range(2):
                for k in range(N_SLAB):
                    r0 = (d * N_GROUPS + k * GROUPS_PER_SLAB) * SSM_CH
                    dbb[q, r0:r0 + SLAB_IN, :] = _select(_dot, spread_t, gb_ref[q, k] * keep, False)
                    out_ref = gcr_ref if q == 0 else gci_ref
                    out_ref[r0:r0 + SLAB_IN, :] = _select(_dot_tn, spread_t, gc_ref[q, k] * keep_t, False)
        grp = (lax.broadcasted_iota(jnp.int32, (N_GROUPS, STATE_W), 0)
               == lax.broadcasted_iota(jnp.int32, (N_GROUPS, STATE_W), 1) // SSM_STATE).astype(F32)
        pick = (lax.broadcasted_iota(jnp.int32, (STATE_W, SSM_STATE), 0) % SSM_STATE
                == lax.broadcasted_iota(jnp.int32, (STATE_W, SSM_STATE), 1)).astype(F32)
        for d, gl_ref in enumerate((gl0_ref, gl1_ref)):
            for q in range(2):
                row = jnp.sum(gl_ref[q], axis=0, keepdims=True)
                dlam[q, d * N_GROUPS:(d + 1) * N_GROUPS, :] = _select(_dot, pick, grp * row, False)

        a_r, a_i = ar_ref[...], ai_ref[...]
        dt, mag, lr, li, den, nr, ni = _ssm_param_values(a_r, a_i, dt_ref[...])
        fr = _select(_dot, repeat, nr / den, True)
        fi = _select(_dot, repeat, ni / den, True)
        b_r, b_i = br_ref[...], bi_ref[...]
        g_r, g_i = dbb[0], dbb[1]
        gbr_ref[...] = fr * g_r + fi * g_i
        gbi_ref[...] = fr * g_i - fi * g_r
        d_fr = _select(_dot_tn, repeat, b_r * g_r + b_i * g_i, True)
        d_fi = _select(_dot_tn, repeat, b_r * g_i - b_i * g_r, True)
        d_nr, d_ni = d_fr / den, d_fi / den
        d_den = -(d_fr * nr + d_fi * ni) / (den * den)
        d_lr = dlam[0] + d_nr * a_r - d_ni * a_i
        d_li = dlam[1] + d_nr * a_i + d_ni * a_r
        d_ar = d_nr * (lr - 1.0) + d_ni * li + d_den * 2.0 * a_r
        d_ai = d_nr * li - d_ni * (lr - 1.0) + d_den * 2.0 * a_i
        d_mag = (d_lr * lr + d_li * li) / mag
        d_theta = d_li * lr - d_lr * li
        gar_ref[...] = d_ar + d_mag * mag * dt
        gai_ref[...] = d_ai + d_theta * dt
        d_dt = d_mag * mag * a_r + d_theta * a_i
        gdt_ref[...] = jnp.sum(d_dt, axis=1, keepdims=True) * dt

    small = jax.ShapeDtypeStruct(ar.shape, F32)
    big = jax.ShapeDtypeStruct(br.shape, F32)
    return pl.pallas_call(
        body, name="ssm_params_bwd",
        out_shape=[small, small, jax.ShapeDtypeStruct(logdt.shape, F32), big, big, big, big],
        scratch_shapes=[pltpu.VMEM((2,) + br.shape, F32), pltpu.VMEM((2,) + ar.shape, F32)],
        compiler_params=pltpu.CompilerParams(vmem_limit_bytes=VMEM_LIMIT),
    )(ar, ai, logdt, br, bi, *g_slabs_b, *g_slabs_c, *g_lam)


def _rope_tables(seq):
    half = HEAD_DIM // 2
    inv_freq = ROPE_THETA ** (-jnp.arange(half, dtype=F32) / half)
    ang = jnp.arange(seq, dtype=jnp.int32).astype(F32)[:, None] * inv_freq[None, :]
    cos, sin = jnp.cos(ang), jnp.sin(ang)
    cos128 = jnp.concatenate([cos, cos, cos, cos], axis=1)
    sin128 = jnp.concatenate([-sin, sin, -sin, sin], axis=1)
    return cos128, sin128


def _rotate_half_unsigned(t):
    lane = lax.broadcasted_iota(jnp.int32, t.shape, 1)
    return jnp.where((lane % HEAD_DIM) < HEAD_DIM // 2, pltpu.roll(t, 96, 1), pltpu.roll(t, 32, 1))


def _rope(t, cos, sin_signed):
    return t * cos + _rotate_half_unsigned(t) * sin_signed


def _pair_blocks(base):
    out = []
    for j in range(4):
        for g in range(2):
            nat = base + HEAD_DIM * (4 * g + j)
            par = base + 128 * j + HEAD_DIM * g
            out.append((slice(nat, nat + HEAD_DIM), slice(par, par + HEAD_DIM)))
    return out


W_Q, W_KV, W_ZA, W_U, W_ZS = 0, 512, 768, 1280, 1792


def _proj(x, wt, cos128, sin128, tb):
    seq = x.shape[0]

    def body(x_ref, wt_ref, cos_ref, sin_ref, q_ref, k_ref, v_ref, za_ref, u_ref, zs_ref, wp):
        @pl.when(pl.program_id(0) == 0)
        def _():
            for dst_base, src_base in ((0, W_Q), (512, W_ZA)):
                for nat, par in _pair_blocks(0):
                    wp[dst_base + par.start:dst_base + par.stop, :] = wt_ref[src_base + nat.start:src_base + nat.stop, :]

        xb = x_ref[...].astype(BF16)
        cos, sin = cos_ref[...], sin_ref[...]
        lo = lax.broadcasted_iota(jnp.int32, (tb, 128), 1) < HEAD_DIM
        q = _dot_nt(xb, wp[0:512, :])
        for j in range(4):
            qj = _rope(q[:, 128 * j:128 * (j + 1)], cos, sin)
            q_ref[j] = jnp.where(lo, qj, 0.0).astype(BF16)
            q_ref[4 + j] = jnp.where(lo, 0.0, qj).astype(BF16)
        kv = _dot_nt(xb, wt_ref[W_KV:W_ZA, :])
        k_ref[...] = _rope(kv[:, 0:128], cos, sin).astype(BF16)
        v_ref[...] = kv[:, 128:256].astype(BF16)
        za_ref[...] = _dot_nt(xb, wp[512:1024, :])
        u_val = _dot_nt(xb, wt_ref[W_U:W_ZS, :])
        for k in range(N_SLAB):
            u_ref[k] = u_val[:, k * SLAB_IN:(k + 1) * SLAB_IN]
        zs_ref[...] = _dot_nt(xb, wt_ref[W_ZS:D_IN_PROJ, :])

    row = lambda w: pl.BlockSpec((tb, w), lambda i: (i, 0))
    return pl.pallas_call(
        body, name="proj", grid=(seq // tb,),
        in_specs=[row(D_MODEL), pl.BlockSpec((D_IN_PROJ, D_MODEL), lambda i: (0, 0)), row(128), row(128)],
        out_specs=[pl.BlockSpec((8, tb, 128), lambda i: (0, i, 0)), row(128), row(128), row(512),
                   pl.BlockSpec((N_SLAB, tb, SLAB_IN), lambda i: (0, i, 0)), row(512)],
        out_shape=[jax.ShapeDtypeStruct((8, seq, 128), BF16), jax.ShapeDtypeStruct((seq, 128), BF16),
                   jax.ShapeDtypeStruct((seq, 128), BF16), jax.ShapeDtypeStruct((seq, 512), F32),
                   jax.ShapeDtypeStruct((N_SLAB, seq, SLAB_IN), F32), jax.ShapeDtypeStruct((seq, 512), F32)],
        scratch_shapes=[pltpu.VMEM((1024, D_MODEL), BF16)],
        compiler_params=_cparams(("arbitrary",)),
    )(x, wt, cos128, sin128)


ATT_TQ = 128
ATT_KEYS = 3 * ATT_TQ


def _attn_window(i, seq):
    start = jnp.clip(i * ATT_TQ - WINDOW, 0, seq - ATT_KEYS)
    return pl.multiple_of(start, ATT_TQ)


def _attn_bias():
    r = jnp.arange(ATT_TQ, dtype=jnp.int32)[None, :, None]
    c = jnp.arange(ATT_KEYS, dtype=jnp.int32)[None, None, :]
    off = (jnp.arange(3, dtype=jnp.int32) * ATT_TQ)[:, None, None]
    return jnp.where(jnp.abs(r + off - c) <= WINDOW, 0.0, NEG_INF).astype(F32)


def _attn_bias_spec(nblk):
    pick = lambda i: jnp.where(i == 0, 0, jnp.where(i == nblk - 1, 2, 1))
    return pl.BlockSpec((None, ATT_TQ, ATT_KEYS), lambda i: (pick(i), 0, 0))


def _attn_softmax(q_ref, k_ref, v_ref, sink_ref, bias_ref, start):
    kw = k_ref[pl.ds(start, ATT_KEYS), :]
    vw = v_ref[pl.ds(start, ATT_KEYS), :]
    qall = q_ref[...].reshape(N_Q_HEADS * ATT_TQ, 128)
    s = (_dot_nt(qall, kw) * (HEAD_DIM ** -0.5)).reshape(N_Q_HEADS, ATT_TQ, ATT_KEYS) + bias_ref[...][None]
    tiles = [s[:, :, 128 * t:128 * (t + 1)] for t in range(ATT_KEYS // 128)]
    m = jnp.max(jnp.maximum(jnp.maximum(tiles[0], tiles[1]), tiles[2]), axis=2, keepdims=True)
    sink = sink_ref[...]
    m_b = jnp.maximum(jnp.broadcast_to(m, (N_Q_HEADS, ATT_TQ, 128)), sink)
    p = jnp.concatenate([jnp.exp(t - m_b) for t in tiles], axis=2)
    p_sink = jnp.exp(sink - m_b)
    lo_k = lax.broadcasted_iota(jnp.int32, (ATT_KEYS, 128), 1) < HEAD_DIM
    v_f = vw.astype(F32)
    v_lo, v_hi = jnp.where(lo_k, v_f, 1.0).astype(BF16), jnp.where(lo_k, 1.0, v_f).astype(BF16)
    pb = p.astype(BF16).reshape(N_Q_HEADS * ATT_TQ, ATT_KEYS)
    half = 4 * ATT_TQ
    r = jnp.concatenate([_dot(pb[:half], v_lo), _dot(pb[half:], v_hi)], axis=0).reshape(N_Q_HEADS, ATT_TQ, 128)
    return kw, vw, qall, p, p_sink, r


def _attn_fwd(q_stack, k, v, sink128, bias):
    seq = k.shape[0]

    def body(q_ref, k_ref, v_ref, sink_ref, bias_ref, o_ref):
        start = _attn_window(pl.program_id(0), seq)
        _, _, _, _, p_sink, r = _attn_softmax(q_ref, k_ref, v_ref, sink_ref, bias_ref, start)
        out = r / (pltpu.roll(r, HEAD_DIM, 2) + p_sink)
        lo = lax.broadcasted_iota(jnp.int32, (ATT_TQ, 128), 1) < HEAD_DIM
        for j in range(4):
            o_ref[:, 128 * j:128 * (j + 1)] = jnp.where(lo, out[j], out[4 + j])

    full = lambda w: pl.BlockSpec((seq, w), lambda i: (0, 0))
    return pl.pallas_call(
        body, name="attn_fwd", grid=(seq // ATT_TQ,),
        in_specs=[pl.BlockSpec((8, ATT_TQ, 128), lambda i: (0, i, 0)), full(128), full(128),
                  pl.BlockSpec((N_Q_HEADS, 1, 128), lambda i: (0, 0, 0)), _attn_bias_spec(seq // ATT_TQ)],
        out_specs=pl.BlockSpec((ATT_TQ, 512), lambda i: (i, 0)),
        out_shape=jax.ShapeDtypeStruct((seq, 512), F32),
        compiler_params=_cparams(("arbitrary",)),
    )(q_stack, k, v, sink128, bias)


def _attn_bwd(q_stack, k, v, sink128, bias, d_o):
    seq = k.shape[0]

    def body(q_ref, k_ref, v_ref, sink_ref, bias_ref, do_ref, dq_ref, dk_ref, dv_ref, dsink_ref, sink_acc):
        i = pl.program_id(0)

        @pl.when(i == 0)
        def _():
            dk_ref[...] = jnp.zeros_like(dk_ref)
            dv_ref[...] = jnp.zeros_like(dv_ref)
            sink_acc[...] = jnp.zeros_like(sink_acc)

        start = _attn_window(i, seq)
        kw, vw, qall, p, p_sink, r = _attn_softmax(q_ref, k_ref, v_ref, sink_ref, bias_ref, start)
        lo = lax.broadcasted_iota(jnp.int32, (ATT_TQ, 128), 1) < HEAD_DIM
        lo3 = lo[None]
        grp0 = lax.broadcasted_iota(jnp.int32, (N_Q_HEADS, ATT_TQ, 128), 0) < 4
        val = grp0 == lo3
        swapped = pltpu.roll(r, HEAD_DIM, 2)
        inv = 1.0 / (jnp.where(val, swapped, r) + p_sink)
        d_o_blk = do_ref[...]
        do3 = jnp.where(val, jnp.concatenate([d_o_blk[None, :, 128 * j:128 * (j + 1)] for j in range(4)] * 2, axis=0), 0.0)
        t = (do3 * r).reshape(N_Q_HEADS * ATT_TQ, 128)
        t_hi = t.astype(BF16)
        t_lo = (t - t_hi.astype(F32)).astype(BF16)
        ones = jnp.ones((128, 128), BF16)
        delta = (_dot(t_hi, ones) + _dot(t_lo, ones)).reshape(N_Q_HEADS, ATT_TQ, 128) * inv
        sink_acc[...] += -(p_sink * inv) * delta
        do_all = do3.astype(BF16).reshape(N_Q_HEADS * ATT_TQ, 128)
        dp = _dot_nt(do_all, vw).reshape(N_Q_HEADS, ATT_TQ, ATT_KEYS)
        probs, ds = [], []
        for tl in range(ATT_KEYS // 128):
            cols = slice(128 * tl, 128 * (tl + 1))
            probs_t = p[:, :, cols] * inv
            probs.append(probs_t.astype(BF16))
            ds.append((probs_t * (dp[:, :, cols] - delta)).astype(BF16))
        probs_all = jnp.concatenate(probs, axis=2).reshape(N_Q_HEADS * ATT_TQ, ATT_KEYS)
        ds_all = jnp.concatenate(ds, axis=2).reshape(N_Q_HEADS * ATT_TQ, ATT_KEYS)
        scale = HEAD_DIM ** -0.5
        dq_all = (_dot(ds_all, kw) * scale).reshape(N_Q_HEADS, ATT_TQ, 128)
        for j in range(4):
            dq_ref[:, 128 * j:128 * (j + 1)] = jnp.where(lo, dq_all[j], dq_all[4 + j])
        dk_ref[pl.ds(start, ATT_KEYS), :] += _dot_tn(ds_all, qall) * scale
        dv_ref[pl.ds(start, ATT_KEYS), :] += _dot_tn(probs_all, do_all)

        @pl.when(i == pl.num_programs(0) - 1)
        def _():
            dsink_ref[...] = jnp.sum(sink_acc[...], axis=1)

    full = lambda w: pl.BlockSpec((seq, w), lambda i: (0, 0))
    return pl.pallas_call(
        body, name="attn_bwd", grid=(seq // ATT_TQ,),
        in_specs=[pl.BlockSpec((8, ATT_TQ, 128), lambda i: (0, i, 0)), full(128), full(128),
                  pl.BlockSpec((N_Q_HEADS, 1, 128), lambda i: (0, 0, 0)),
                  _attn_bias_spec(seq // ATT_TQ), pl.BlockSpec((ATT_TQ, 512), lambda i: (i, 0))],
        out_specs=[pl.BlockSpec((ATT_TQ, 512), lambda i: (i, 0)), full(128), full(128),
                   pl.BlockSpec((N_Q_HEADS, 128), lambda i: (0, 0))],
        out_shape=[jax.ShapeDtypeStruct((seq, 512), F32), jax.ShapeDtypeStruct((seq, 128), F32),
                   jax.ShapeDtypeStruct((seq, 128), F32), jax.ShapeDtypeStruct((N_Q_HEADS, 128), F32)],
        scratch_shapes=[pltpu.VMEM((N_Q_HEADS, ATT_TQ, 128), F32)],
        compiler_params=_cparams(("arbitrary",)),
    )(q_stack, k, v, sink128, bias, d_o)


def _permute_rows(dst_ref, src_ref, sub_len):
    for k in range(N_SLAB):
        for j in range(sub_len):
            dst_ref[k, 8 * j:8 * (j + 1), :] = src_ref.at[k][pl.ds(j, SUBSEG, stride=sub_len), :]


def _unpermute_rows(dst_ref, src_ref, sub_len):
    for k in range(N_SLAB):
        for s in range(SUBSEG):
            dst_ref[k, s * sub_len:(s + 1) * sub_len, :] = src_ref.at[k][pl.ds(s, sub_len, stride=SUBSEG), :]


def _scan_pass(br_ref, bi_ref, lr_row, li_row, start, end_refs, *, sub_len, reverse, store):
    width = br_ref.shape[1]
    for c0 in range(0, width, SCAN_LANES):
        cols = slice(c0, c0 + SCAN_LANES)
        lr = jnp.broadcast_to(lr_row[:, cols], (SUBSEG, SCAN_LANES))
        li = jnp.broadcast_to(li_row[:, cols], (SUBSEG, SCAN_LANES))
        if start is None:
            init = (jnp.zeros((SUBSEG, SCAN_LANES), F32), jnp.zeros((SUBSEG, SCAN_LANES), F32))
        else:
            init = (start[0][:, cols], start[1][:, cols])

        def steps(jo, state, cols=cols, lr=lr, li=li):
            sr, si = state
            for ju in range(SCAN_UNROLL):
                jj = jo * SCAN_UNROLL + ju
                j = (sub_len - 1 - jj) if reverse else jj
                r0 = pl.multiple_of(j * SUBSEG, SUBSEG)
                nr = lr * sr - li * si + br_ref[pl.ds(r0, SUBSEG), cols]
                ni = lr * si + li * sr + bi_ref[pl.ds(r0, SUBSEG), cols]
                if store:
                    br_ref[pl.ds(r0, SUBSEG), cols] = nr
                    bi_ref[pl.ds(r0, SUBSEG), cols] = ni
                sr, si = nr, ni
            return sr, si

        sr, si = lax.fori_loop(0, sub_len // SCAN_UNROLL, steps, init)
        if end_refs is not None:
            end_refs[0][:, cols] = sr
            end_refs[1][:, cols] = si


def _resolve_starts(z_refs, carry_refs, start_refs, pr_row, pi_row, *, reverse):
    cr, ci = carry_refs[0][0:1, :], carry_refs[1][0:1, :]
    for s in (range(SUBSEG - 1, -1, -1) if reverse else range(SUBSEG)):
        start_refs[0][s:s + 1, :] = cr
        start_refs[1][s:s + 1, :] = ci
        zr, zi = z_refs[0][s:s + 1, :], z_refs[1][s:s + 1, :]
        cr, ci = pr_row * cr - pi_row * ci + zr, pr_row * ci + pi_row * cr + zi
    carry_refs[0][0:1, :] = cr
    carry_refs[1][0:1, :] = ci


def _param_specs(direction):
    row = lambda q: pl.BlockSpec((None, None, 1, STATE_W), lambda i: (q, direction, 0, 0))
    wide = lambda q: pl.BlockSpec((None, None, N_SLAB, SLAB_IN, SLAB_ST), lambda i: (q, direction, 0, 0, 0))
    tall = lambda q: pl.BlockSpec((None, None, N_SLAB, SLAB_ST, SLAB_IN), lambda i: (q, direction, 0, 0, 0))
    return [row(q) for q in range(4)], [wide(0), wide(1)], [tall(0), tall(1)]


def _ssm_fwd(u, lam, bb, cb, *, direction, tb, name):
    reverse = direction == 1
    seq = u.shape[1]
    nblk = seq // tb
    sub_len = tb // SUBSEG

    def body(u_ref, lr_ref, li_ref, pr_ref, pi_ref, bbr_ref, bbi_ref, cbr_ref, cbi_ref,
             y_ref, sr_ref, si_ref, xr, xi, up, yp, zr, zi, car, cai):
        @pl.when(pl.program_id(0) == 0)
        def _():
            car[...] = jnp.zeros_like(car)
            cai[...] = jnp.zeros_like(cai)

        _permute_rows(up, u_ref, sub_len)
        for k in range(N_SLAB):
            ub = up[k].astype(BF16)
            xr[:, k * SLAB_ST:(k + 1) * SLAB_ST] = _dot(ub, bbr_ref[k])
            xi[:, k * SLAB_ST:(k + 1) * SLAB_ST] = _dot(ub, bbi_ref[k])
        lr, li = lr_ref[...], li_ref[...]
        _scan_pass(xr, xi, lr, li, None, (zr, zi), sub_len=sub_len, reverse=reverse, store=False)
        _resolve_starts((zr, zi), (car, cai), (sr_ref, si_ref), pr_ref[...], pi_ref[...], reverse=reverse)
        _scan_pass(xr, xi, lr, li, (sr_ref, si_ref), None, sub_len=sub_len, reverse=reverse, store=True)
        for k in range(N_SLAB):
            st = slice(k * SLAB_ST, (k + 1) * SLAB_ST)
            yp[k] = _dot(xr[:, st].astype(BF16), cbr_ref[k]) - _dot(xi[:, st].astype(BF16), cbi_ref[k])
        _unpermute_rows(y_ref, yp, sub_len)

    blk = (lambda i: nblk - 1 - i) if reverse else (lambda i: i)
    rows, wide, tall = _param_specs(direction)
    tok = pl.BlockSpec((N_SLAB, tb, SLAB_IN), lambda i: (0, blk(i), 0))
    start_spec = pl.BlockSpec((None, SUBSEG, STATE_W), lambda i: (blk(i), 0, 0))
    return pl.pallas_call(
        body, name=name, grid=(nblk,),
        in_specs=[tok] + rows + wide + tall,
        out_specs=[tok, start_spec, start_spec],
        out_shape=[jax.ShapeDtypeStruct((N_SLAB, seq, SLAB_IN), F32), jax.ShapeDtypeStruct((nblk, SUBSEG, STATE_W), F32),
                   jax.ShapeDtypeStruct((nblk, SUBSEG, STATE_W), F32)],
        scratch_shapes=[pltpu.VMEM((tb, STATE_W), F32), pltpu.VMEM((tb, STATE_W), F32),
                        pltpu.VMEM((N_SLAB, tb, SLAB_IN), F32), pltpu.VMEM((N_SLAB, tb, SLAB_IN), F32),
                        pltpu.VMEM((SUBSEG, STATE_W), F32), pltpu.VMEM((SUBSEG, STATE_W), F32),
                        pltpu.VMEM((SUBSEG, STATE_W), F32), pltpu.VMEM((SUBSEG, STATE_W), F32)],
        compiler_params=_cparams(("arbitrary",)),
    )(u, lam, lam, lam, lam, bb, bb, cb, cb)


def _ssm_bwd(u, dy, starts, lam, bb, bbt, cb_t, *, direction, tb, name):
    reverse = direction == 1
    seq = u.shape[1]
    nblk = seq // tb
    sub_len = tb // SUBSEG

    def body(u_ref, dy_ref, sr_ref, si_ref, lr_ref, li_ref, pr_ref, pi_ref, bbr_ref, bbi_ref, btr_ref, bti_ref,
             ctr_ref, cti_ref, du_ref, gb_ref, gc_ref, dl_ref,
             xr, xi, gr, gi, up, dyp, dup, zr, zi, gsr, gsi, car, cai):
        gbr_ref, gbi_ref = gb_ref.at[0], gb_ref.at[1]
        gcr_ref, gci_ref = gc_ref.at[0], gc_ref.at[1]
        dlr_ref, dli_ref = dl_ref.at[0], dl_ref.at[1]

        @pl.when(pl.program_id(0) == 0)
        def _():
            for ref in (car, cai, gbr_ref, gbi_ref, gcr_ref, gci_ref, dlr_ref, dli_ref):
                ref[...] = jnp.zeros_like(ref)

        _permute_rows(up, u_ref, sub_len)
        _permute_rows(dyp, dy_ref, sub_len)
        lr, li = lr_ref[...], li_ref[...]
        for k in range(N_SLAB):
            st = slice(k * SLAB_ST, (k + 1) * SLAB_ST)
            ub = up[k].astype(BF16)
            xr[:, st] = _dot(ub, bbr_ref[k])
            xi[:, st] = _dot(ub, bbi_ref[k])
            dyb = dyp[k].astype(BF16)
            gr[:, st] = _dot(dyb, ctr_ref[k])
            gi[:, st] = -_dot(dyb, cti_ref[k])
        _scan_pass(xr, xi, lr, li, (sr_ref, si_ref), None, sub_len=sub_len, reverse=reverse, store=True)
        for k in range(N_SLAB):
            st = slice(k * SLAB_ST, (k + 1) * SLAB_ST)
            dyb = dyp[k].astype(BF16)
            gcr_ref[k] += _dot_tn(xr[:, st].astype(BF16), dyb)
            gci_ref[k] -= _dot_tn(xi[:, st].astype(BF16), dyb)
        nli = -li
        _scan_pass(gr, gi, lr, nli, None, (zr, zi), sub_len=sub_len, reverse=not reverse, store=False)
        _resolve_starts((zr, zi), (car, cai), (gsr, gsi), pr_ref[...], -pi_ref[...], reverse=not reverse)
        _scan_pass(gr, gi, lr, nli, (gsr, gsi), None, sub_len=sub_len, reverse=not reverse, store=True)
        for k in range(N_SLAB):
            st = slice(k * SLAB_ST, (k + 1) * SLAB_ST)
            ub = up[k].astype(BF16)
            grb, gib = gr[:, st].astype(BF16), gi[:, st].astype(BF16)
            gbr_ref[k] += _dot_tn(ub, grb)
            gbi_ref[k] += _dot_tn(ub, gib)
            dup[k] = _dot(grb, btr_ref[k]) + _dot(gib, bti_ref[k])
        _unpermute_rows(du_ref, dup, sub_len)

        for c0 in range(0, STATE_W, SCAN_LANES):
            cols = slice(c0, c0 + SCAN_LANES)
            edge = (sub_len - 1) * SUBSEG if reverse else 0
            g_r, g_i = gr[edge:edge + SUBSEG, cols], gi[edge:edge + SUBSEG, cols]
            x_r, x_i = sr_ref[:, cols], si_ref[:, cols]
            acc = (dlr_ref[:, cols] + (g_r * x_r + g_i * x_i), dli_ref[:, cols] + (g_i * x_r - g_r * x_i))

            def step(jj, acc, cols=cols):
                tile = lambda t: t * SUBSEG if isinstance(t, int) else pl.multiple_of(t * SUBSEG, SUBSEG)
                r_g, r_x = tile(jj if reverse else jj + 1), tile(jj + 1 if reverse else jj)
                g_r, g_i = gr[pl.ds(r_g, SUBSEG), cols], gi[pl.ds(r_g, SUBSEG), cols]
                x_r, x_i = xr[pl.ds(r_x, SUBSEG), cols], xi[pl.ds(r_x, SUBSEG), cols]
                return acc[0] + (g_r * x_r + g_i * x_i), acc[1] + (g_i * x_r - g_r * x_i)

            acc = step(0, acc)
            acc = lax.fori_loop(0, (sub_len - 2) // 2, lambda jo, a: step(2 * jo + 2, step(2 * jo + 1, a)), acc)
            dlr_ref[:, cols] = acc[0]
            dli_ref[:, cols] = acc[1]

    blk = (lambda i: i) if reverse else (lambda i: nblk - 1 - i)
    rows, wide, tall = _param_specs(direction)
    tok = pl.BlockSpec((N_SLAB, tb, SLAB_IN), lambda i: (0, blk(i), 0))
    start_spec = pl.BlockSpec((None, SUBSEG, STATE_W), lambda i: (blk(i), 0, 0))
    gb_shape, gc_shape, dl_shape = (2, N_SLAB, SLAB_IN, SLAB_ST), (2, N_SLAB, SLAB_ST, SLAB_IN), (2, SUBSEG, STATE_W)
    whole = lambda shape: pl.BlockSpec(shape, lambda i: (0,) * len(shape))
    big = lambda: pltpu.VMEM((tb, STATE_W), F32)
    slabs = lambda: pltpu.VMEM((N_SLAB, tb, SLAB_IN), F32)
    tile = lambda: pltpu.VMEM((SUBSEG, STATE_W), F32)
    return pl.pallas_call(
        body, name=name, grid=(nblk,),
        in_specs=[tok, tok, start_spec, start_spec] + rows + wide + tall + wide,
        out_specs=[tok, whole(gb_shape), whole(gc_shape), whole(dl_shape)],
        out_shape=[jax.ShapeDtypeStruct((N_SLAB, seq, SLAB_IN), F32), jax.ShapeDtypeStruct(gb_shape, F32),
                   jax.ShapeDtypeStruct(gc_shape, F32), jax.ShapeDtypeStruct(dl_shape, F32)],
        scratch_shapes=[big(), big(), big(), big(), slabs(), slabs(), slabs(),
                        tile(), tile(), tile(), tile(), tile(), tile()],
        compiler_params=_cparams(("arbitrary",)),
    )(u, dy, *starts, lam, lam, lam, lam, bb, bb, bbt, bbt, cb_t, cb_t)


GELU_C = math.sqrt(2.0 / math.pi)
GELU_K = 0.044715


def _mid(o, za, u, y_f, y_b, zs, x, target, ssm_d, w_glu, b_glu, g_attn, g_ssm, w_out, ln_g, ln_b, tb):
    seq = x.shape[0]

    def body(o_ref, za_ref, u_ref, yf_ref, yb_ref, zs_ref, x_ref, t_ref, d_ref, wg_ref, bg_ref, ga_ref, gs_ref,
             wo_ref, lg_ref, lb_ref,
             loss_ref, do_ref, dza_ref, dyl_ref, dzs_ref, dpre_ref, gwo_ref, gwg_ref, vec_ref, wop):
        @pl.when(pl.program_id(0) == 0)
        def _():
            for ref in (loss_ref, gwo_ref, gwg_ref, vec_ref):
                ref[...] = jnp.zeros_like(ref)
            for nat, par in _pair_blocks(0):
                wop[par, :] = wo_ref[nat, :]
            wop[D_ATTN:, :] = wo_ref[D_ATTN:, :]

        o, za = o_ref[...], za_ref[...]
        sig_a = _sigmoid(za)
        silu_a = za * sig_a
        ya = o * silu_a
        r_a = lax.rsqrt(jnp.mean(ya * ya, axis=1, keepdims=True) + NORM_EPS)
        n_a = ya * r_a
        g_a = ga_ref[...]
        unslab = lambda ref: jnp.concatenate([ref[k] for k in range(N_SLAB)], axis=1)
        u_blk, zs = unslab(u_ref), zs_ref[...]
        d_row = d_ref[...]
        ylin = d_row * u_blk + unslab(yf_ref) + unslab(yb_ref)
        inner = GELU_C * (ylin + GELU_K * ylin * ylin * ylin)
        th = jnp.tanh(inner)
        gl = 0.5 * ylin * (1.0 + th)
        glb = gl.astype(BF16)
        sg = _sigmoid(_dot(glb, wg_ref[...]) + bg_ref[...])
        y2 = gl * sg
        sig_s = _sigmoid(zs)
        silu_s = zs * sig_s
        ys = y2 * silu_s
        r_s = lax.rsqrt(jnp.mean(ys * ys, axis=1, keepdims=True) + NORM_EPS)
        n_s = ys * r_s
        g_s = gs_ref[...]
        mixed = jnp.concatenate([n_a * g_a, n_s * g_s], axis=1).astype(BF16)
        pre = ALPHA * x_ref[...] + _dot(mixed, wop[...])
        mu = jnp.mean(pre, axis=1, keepdims=True)
        cen = pre - mu
        rstd = lax.rsqrt(jnp.mean(cen * cen, axis=1, keepdims=True) + NORM_EPS)
        hhat = cen * rstd
        ln_g = lg_ref[...]
        err = hhat * ln_g + lb_ref[...] - t_ref[...]
        loss_ref[...] += 0.5 * jnp.sum(jnp.mean(err * err, axis=1, keepdims=True))

        dh = err * (1.0 / D_MODEL)
        vec_ref[0:1, :] += jnp.sum(dh * hhat, axis=0, keepdims=True)
        vec_ref[1:2, :] += jnp.sum(dh, axis=0, keepdims=True)
        dhh = dh * ln_g
        dpre = rstd * (dhh - jnp.mean(dhh, axis=1, keepdims=True) - hhat * jnp.mean(dhh * hhat, axis=1, keepdims=True))
        dpre_ref[...] = dpre
        dpb = dpre.astype(BF16)
        for j in range(4):
            g_pair = _dot_tn(mixed[:, 128 * j:128 * (j + 1)], dpb)
            for g in range(2):
                nat = HEAD_DIM * (4 * g + j)
                gwo_ref[nat:nat + HEAD_DIM, :] += g_pair[HEAD_DIM * g:HEAD_DIM * (g + 1), :]
        gwo_ref[D_ATTN:, :] += _dot_tn(mixed[:, D_ATTN:], dpb)
        dmix = _dot_nt(dpb, wop[...])
        dna = dmix[:, :D_ATTN]
        vec_ref[2:3, 0:D_ATTN] += jnp.sum(dna * n_a, axis=0, keepdims=True)
        dna = dna * g_a
        dya = r_a * (dna - n_a * jnp.mean(dna * n_a, axis=1, keepdims=True))
        do_ref[...] = dya * silu_a
        dza_ref[...] = dya * o * (sig_a * (1.0 + za * (1.0 - sig_a)))
        dns = dmix[:, D_ATTN:]
        vec_ref[2:3, D_ATTN:] += jnp.sum(dns * n_s, axis=0, keepdims=True)
        dns = dns * g_s
        dys = r_s * (dns - n_s * jnp.mean(dns * n_s, axis=1, keepdims=True))
        dzs_ref[...] = dys * y2 * (sig_s * (1.0 + zs * (1.0 - sig_s)))
        dy2 = dys * silu_s
        da = dy2 * gl * sg * (1.0 - sg)
        vec_ref[3:4, D_SSM:] += jnp.sum(da, axis=0, keepdims=True)
        dab = da.astype(BF16)
        gwg_ref[...] += _dot_tn(glb, dab)
        dgl = dy2 * sg + _dot_nt(dab, wg_ref[...])
        dylin = dgl * (0.5 * (1.0 + th) + 0.5 * ylin * (1.0 - th * th) * GELU_C * (1.0 + 3.0 * GELU_K * ylin * ylin))
        for k in range(N_SLAB):
            dyl_ref[k] = dylin[:, k * SLAB_IN:(k + 1) * SLAB_IN]
        vec_ref[3:4, 0:D_SSM] += jnp.sum(dylin * u_blk, axis=0, keepdims=True)

    tok = lambda w: pl.BlockSpec((tb, w), lambda i: (i, 0))
    slab = pl.BlockSpec((N_SLAB, tb, SLAB_IN), lambda i: (0, i, 0))
    const = lambda r, c: pl.BlockSpec((r, c), lambda i: (0, 0))
    tok_shape = jax.ShapeDtypeStruct((seq, 512), F32)
    return pl.pallas_call(
        body, name="mid", grid=(seq // tb,),
        in_specs=[tok(512), tok(512), slab, slab, slab, tok(512), tok(1024), tok(1024),
                  const(1, 512), const(512, 512), const(1, 512), const(1, 512), const(1, 512),
                  const(1024, 1024), const(1, 1024), const(1, 1024)],
        out_specs=[const(8, 128), tok(512), tok(512), slab, tok(512), tok(1024),
                   const(1024, 1024), const(512, 512), const(8, 1024)],
        out_shape=[jax.ShapeDtypeStruct((8, 128), F32), tok_shape, tok_shape,
                   jax.ShapeDtypeStruct((N_SLAB, seq, SLAB_IN), F32), tok_shape,
                   jax.ShapeDtypeStruct((seq, 1024), F32), jax.ShapeDtypeStruct((1024, 1024), F32),
                   jax.ShapeDtypeStruct((512, 512), F32), jax.ShapeDtypeStruct((8, 1024), F32)],
        scratch_shapes=[pltpu.VMEM((D_MODEL, D_MODEL), BF16)],
        compiler_params=_cparams(("arbitrary",)),
    )(o, za, u, y_f, y_b, zs, x, target, ssm_d, w_glu, b_glu, g_attn, g_ssm, w_out, ln_g, ln_b)


def _proj_bwd(x, dq, dk, dv, dza, du_f, du_b, dylin, dzs, dpre, ssm_d, cos128, sin128, wt, tb):
    seq = x.shape[0]

    def body(x_ref, dq_ref, dk_ref, dv_ref, dza_ref, duf_ref, dub_ref, dyl_ref, dzs_ref, dpre_ref, d_ref,
             cos_ref, sin_ref, wt_ref, gx_ref, gw_ref, wp):
        @pl.when(pl.program_id(0) == 0)
        def _():
            gw_ref[...] = jnp.zeros_like(gw_ref)
            for base in (W_Q, W_ZA):
                for nat, par in _pair_blocks(base):
                    wp[par, :] = wt_ref[nat, :]
            wp[W_KV:W_ZA, :] = wt_ref[W_KV:W_ZA, :]
            wp[W_U:, :] = wt_ref[W_U:, :]

        cos, sin = cos_ref[...], sin_ref[...]

        def unrope(t):
            return t * cos + _rotate_half_unsigned(t * sin)

        dq_rot = dq_ref[...]
        pieces = [unrope(dq_rot[:, 128 * j:128 * (j + 1)]) for j in range(4)]
        d_row = d_ref[...]
        pieces += [unrope(dk_ref[...]), dv_ref[...], dza_ref[...]]
        pieces += [duf_ref[k] + dub_ref[k] + d_row[:, k * SLAB_IN:(k + 1) * SLAB_IN] * dyl_ref[k] for k in range(N_SLAB)]
        pieces += [dzs_ref[...]]
        dproj = jnp.concatenate(pieces, axis=1).astype(BF16)
        gx_ref[...] = ALPHA * dpre_ref[...] + _dot(dproj, wp[...])
        xb = x_ref[...].astype(BF16)
        for base in (W_Q, W_ZA):
            for j in range(4):
                g_pair = _dot_tn(dproj[:, base + 128 * j:base + 128 * (j + 1)], xb)
                for g in range(2):
                    nat = base + HEAD_DIM * (4 * g + j)
                    gw_ref[nat:nat + HEAD_DIM, :] += g_pair[HEAD_DIM * g:HEAD_DIM * (g + 1), :]
        gw_ref[W_KV:W_ZA, :] += _dot_tn(dproj[:, W_KV:W_ZA], xb)
        gw_ref[W_U:, :] += _dot_tn(dproj[:, W_U:], xb)

    tok = lambda w: pl.BlockSpec((tb, w), lambda i: (i, 0))
    slab = pl.BlockSpec((N_SLAB, tb, SLAB_IN), lambda i: (0, i, 0))
    const = lambda r, c: pl.BlockSpec((r, c), lambda i: (0, 0))
    return pl.pallas_call(
        body, name="proj_bwd", grid=(seq // tb,),
        in_specs=[tok(1024), tok(512), tok(128), tok(128), tok(512), slab, slab, slab, tok(512), tok(1024),
                  const(1, 512), tok(128), tok(128), const(D_IN_PROJ, D_MODEL)],
        out_specs=[tok(1024), const(D_IN_PROJ, D_MODEL)],
        out_shape=[jax.ShapeDtypeStruct((seq, D_MODEL), F32), jax.ShapeDtypeStruct((D_IN_PROJ, D_MODEL), F32)],
        scratch_shapes=[pltpu.VMEM((D_IN_PROJ, D_MODEL), BF16)],
        compiler_params=_cparams(("arbitrary",)),
    )(x, dq, dk, dv, dza, du_f, du_b, dylin, dzs, dpre, ssm_d, cos128, sin128, wt)


def _adamw(w, g, m, v, name):
    rows, cols = w.shape
    tb = rows
    while tb * cols * 4 > ADAMW_BLOCK_BYTES and tb % 16 == 0:
        tb //= 2

    def body(w_ref, g_ref, m_ref, v_ref, d_ref, nm_ref, nv_ref):
        _adamw_update(w_ref, g_ref, m_ref, v_ref, d_ref, nm_ref, nv_ref)

    spec = pl.BlockSpec((tb, cols), lambda i: (i, 0))
    return pl.pallas_call(
        body, name=name, grid=(rows // tb,), in_specs=[spec] * 4, out_specs=[spec] * 3,
        out_shape=[jax.ShapeDtypeStruct((rows, cols), F32)] * 3,
        compiler_params=_cparams(("arbitrary",)),
    )(w, g, m, v)


def _adamw_update(w_ref, g_ref, m_ref, v_ref, d_ref, nm_ref, nv_ref):
    g_blk = g_ref[...]
    m_new = ADAM_B1 * m_ref[...] + (1.0 - ADAM_B1) * g_blk
    v_new = ADAM_B2 * v_ref[...] + (1.0 - ADAM_B2) * (g_blk * g_blk)
    m_hat = m_new / (1.0 - ADAM_B1 ** ADAM_STEP)
    v_hat = v_new / (1.0 - ADAM_B2 ** ADAM_STEP)
    d_ref[...] = -ADAM_LR * (m_hat / (jnp.sqrt(v_hat) + ADAM_EPS) + ADAM_WD * w_ref[...])
    nm_ref[...] = m_new
    nv_ref[...] = v_new


def _adamw_many(groups, name):
    n = len(groups)

    def body(*refs):
        for p in range(n):
            _adamw_update(*refs[4 * p:4 * p + 4], *refs[4 * n + 3 * p:4 * n + 3 * p + 3])

    return pl.pallas_call(
        body, name=name,
        out_shape=[jax.ShapeDtypeStruct(grp[0].shape, F32) for grp in groups for _ in range(3)],
    )(*[a for grp in groups for a in grp])


_WEIGHTS = ["w_in", "attn_sink", "ssm_a_re", "ssm_a_im", "ssm_log_dt", "ssm_b_re", "ssm_b_im", "ssm_c_re", "ssm_c_im",
            "ssm_d", "w_glu", "b_glu", "norm_attn_g", "norm_ssm_g", "w_out", "ln_g", "ln_b"]
N_DG = N_DIR * N_GROUPS
BIG_ROWS = N_DG * SSM_CH * SSM_STATE // 128
TINY_ROWS = 64


def _pack_small_grads(g_bc, g_vec, g_ar, g_ai, g_dt, g_sink, loss):
    big = jnp.stack([t.reshape(BIG_ROWS, 128) for t in g_bc])
    row = lambda t: jnp.pad(t.reshape(1, -1), ((0, 0), (0, 128 - t.size)))
    tiny = jnp.concatenate([g_vec.reshape(64, 128), g_ar.reshape(32, 128), g_ai.reshape(32, 128), row(g_dt), row(g_sink),
                            row(loss), jnp.zeros((N_CHIPS * TINY_ROWS - 131, 128), F32)], axis=0)
    return jnp.concatenate([big, tiny.reshape(N_CHIPS, TINY_ROWS, 128)], axis=1)


def _unpack_small_grads(packed):
    big = packed[:, :BIG_ROWS].reshape(N_CHIPS, 2 * BIG_ROWS, SSM_STATE)
    tiny = packed[:, BIG_ROWS:].reshape(N_CHIPS * TINY_ROWS, 128)
    g_vec = tiny[0:64].reshape(8, 1024)
    return tiny[130, 0], {
        "ssm_b_re": big[0], "ssm_b_im": big[1], "ssm_c_re": big[2], "ssm_c_im": big[3],
        "ln_g": g_vec[0:1], "ln_b": g_vec[1:2],
        "norm_attn_g": g_vec[2:3, :D_ATTN][:, _PAIR_INV], "norm_ssm_g": g_vec[2:3, D_ATTN:],
        "ssm_d": g_vec[3:4, :D_SSM], "b_glu": g_vec[3:4, D_SSM:],
        "ssm_a_re": tiny[64:96].reshape(N_DG, SSM_STATE), "ssm_a_im": tiny[96:128].reshape(N_DG, SSM_STATE),
        "ssm_log_dt": tiny[128:129, :N_DG].reshape(N_DIR, N_GROUPS), "attn_sink": tiny[129:130, :N_Q_HEADS],
    }


def _small_view(name, t):
    if name in ("ssm_b_re", "ssm_b_im"):
        return jnp.swapaxes(t[0], 2, 3).reshape(N_DG * SSM_CH, SSM_STATE)
    if name in ("ssm_c_re", "ssm_c_im"):
        return t.reshape(N_DG * SSM_CH, SSM_STATE)
    if name in ("ssm_a_re", "ssm_a_im"):
        return t.reshape(N_DG, SSM_STATE)
    if name == "ssm_log_dt":
        return t.reshape(N_DIR, N_GROUPS)
    return t.reshape(1, -1)


def _small_unview(name, t, shape):
    if name in ("ssm_b_re", "ssm_b_im"):
        return jnp.swapaxes(t.reshape(N_DIR, N_GROUPS, SSM_CH, SSM_STATE), 2, 3).reshape(shape)
    return t.reshape(shape)


def kernel(x, w_in, attn_sink, ssm_a_re, ssm_a_im, ssm_log_dt, ssm_b_re, ssm_b_im, ssm_c_re, ssm_c_im, ssm_d, w_glu, b_glu, norm_attn_g, norm_ssm_g, w_out, ln_g, ln_b, loss_target, m_w_in, m_attn_sink, m_ssm_a_re, m_ssm_a_im, m_ssm_log_dt, m_ssm_b_re, m_ssm_b_im, m_ssm_c_re, m_ssm_c_im, m_ssm_d, m_w_glu, m_b_glu, m_norm_attn_g, m_norm_ssm_g, m_w_out, m_ln_g, m_ln_b, v_w_in, v_attn_sink, v_ssm_a_re, v_ssm_a_im, v_ssm_log_dt, v_ssm_b_re, v_ssm_b_im, v_ssm_c_re, v_ssm_c_im, v_ssm_d, v_w_glu, v_b_glu, v_norm_attn_g, v_norm_ssm_g, v_w_out, v_ln_g, v_ln_b):
    args = dict(locals())
    weights = {n: args[n] for n in _WEIGHTS}
    mom_m = {n: args["m_" + n] for n in _WEIGHTS}
    mom_v = {n: args["v_" + n] for n in _WEIGHTS}
    xs = x[0]
    target = loss_target[0]

    wt_g, w_glu_g, w_out_g = _all_gather_chips([w_in[0].T, w_glu[0], w_out[0]], BF16, "gather_weights")
    wt_full = wt_g.reshape(D_IN_PROJ, D_MODEL)
    w_glu_full = w_glu_g.reshape(D_SSM, D_SSM)
    w_out_full = w_out_g.reshape(D_MODEL, D_MODEL)

    g_x, g_wt, g_w_out, g_w_glu, g_small = _local_step(
        xs, target, wt_full, w_glu_full, w_out_full, attn_sink, ssm_a_re, ssm_a_im, ssm_log_dt, ssm_b_re, ssm_b_im,
        ssm_c_re, ssm_c_im, ssm_d, b_glu, norm_attn_g, norm_ssm_g, ln_g, ln_b)

    r_wt, r_w_out, r_w_glu, g_small_all = _reduce_all(
        [g_wt.reshape(N_CHIPS, -1, D_MODEL), g_w_out.reshape(N_CHIPS, -1, D_MODEL), g_w_glu.reshape(N_CHIPS, -1, D_SSM),
         g_small], [True, True, True, False], "reduce_grads")
    loss, small_grads = _unpack_small_grads(g_small_all)

    grads, deltas, new_m, new_v = {}, {}, {}, {}
    d_w, m_w, v_w = _adamw(w_in[0].T, r_wt, m_w_in[0].T, v_w_in[0].T, "adamw_w_in")
    grads["w_in"], deltas["w_in"], new_m["w_in"], new_v["w_in"] = r_wt.T[None], d_w.T[None], m_w.T[None], v_w.T[None]
    for n, g in (("w_out", r_w_out), ("w_glu", r_w_glu)):
        d_w, m_w, v_w = _adamw(weights[n][0], g, mom_m[n][0], mom_v[n][0], "adamw_" + n)
        grads[n], deltas[n], new_m[n], new_v[n] = g[None], d_w[None], m_w[None], v_w[None]
    names = sorted(small_grads)
    updates = _adamw_many([(_small_view(n, weights[n]), small_grads[n], _small_view(n, mom_m[n]), _small_view(n, mom_v[n]))
                           for n in names], "adamw_small")
    for i, n in enumerate(names):
        shape = weights[n].shape
        grads[n] = _small_unview(n, small_grads[n], shape)
        deltas[n], new_m[n], new_v[n] = (_small_unview(n, t, shape) for t in updates[3 * i:3 * i + 3])

    return (loss, g_x[None], *[grads[n] for n in _WEIGHTS], *[deltas[n] for n in _WEIGHTS],
            *[new_m[n] for n in _WEIGHTS], *[new_v[n] for n in _WEIGHTS])


def _local_step(xs, target, wt_full, w_glu_full, w_out_full, attn_sink, ssm_a_re, ssm_a_im, ssm_log_dt, ssm_b_re,
                ssm_b_im, ssm_c_re, ssm_c_im, ssm_d, b_glu, norm_attn_g, norm_ssm_g, ln_g, ln_b):
    seq = xs.shape[0]

    a_r, a_i = _small_view("ssm_a_re", ssm_a_re), _small_view("ssm_a_im", ssm_a_im)
    log_dt = ssm_log_dt.reshape(N_DG, 1)
    b_r, b_i = _small_view("ssm_b_re", ssm_b_re), _small_view("ssm_b_im", ssm_b_im)
    c_r, c_i = _small_view("ssm_c_re", ssm_c_re), _small_view("ssm_c_im", ssm_c_im)
    ssm_tb = min(256, seq)
    sub_len = ssm_tb // SUBSEG
    lam, bb, bbt, cb, cb_t = _ssm_params_fwd(a_r, a_i, log_dt, b_r, b_i, c_r, c_i, int(math.log2(sub_len)))
    lam = lam.reshape(4, N_DIR, 1, STATE_W)

    cos128, sin128 = _rope_tables(seq)
    q_stack, k_rot, v_bf, z_attn, u, z_ssm = _proj(xs, wt_full, cos128, sin128, min(512, seq))
    sink128 = jnp.broadcast_to(attn_sink[0][:, None, None], (N_Q_HEADS, 1, 128))
    attn_bias = _attn_bias()
    o = _attn_fwd(q_stack, k_rot, v_bf, sink128, attn_bias)
    ys, starts = [], []
    for d in range(N_DIR):
        y_d, s_r, s_i = _ssm_fwd(u, lam, bb, cb, direction=d, tb=ssm_tb, name=f"ssm_fwd_{d}")
        ys.append(y_d)
        starts.append((s_r, s_i))

    row = lambda t: t.reshape(1, -1)
    g_attn_p = row(norm_attn_g)[:, _PAIR_PERM]
    loss_blk, d_o, d_za, d_ylin, d_zs, d_pre, g_w_out, g_w_glu, g_vec = _mid(
        o, z_attn, u, ys[0], ys[1], z_ssm, xs, target, row(ssm_d), w_glu_full, row(b_glu),
        g_attn_p, row(norm_ssm_g), w_out_full, row(ln_g), row(ln_b), min(256, seq))

    dq, dk, dv, g_sink = _attn_bwd(q_stack, k_rot, v_bf, sink128, attn_bias, d_o)
    dus, g_bb, g_cb, g_lam = [], [], [], []
    for d in range(N_DIR):
        du_d, gb_d, gc_d, dl_d = _ssm_bwd(u, d_ylin, starts[d], lam, bb, bbt, cb_t, direction=d, tb=ssm_tb,
                                          name=f"ssm_bwd_{d}")
        dus.append(du_d)
        g_bb.append(gb_d)
        g_cb.append(gc_d)
        g_lam.append(dl_d)
    g_ar, g_ai, g_dt, g_br, g_bi, g_cr, g_ci = _ssm_params_bwd(a_r, a_i, log_dt, b_r, b_i, g_bb, g_cb, g_lam)

    g_x, g_wt = _proj_bwd(xs, dq, dk, dv, d_za, dus[0], dus[1], d_ylin, d_zs, d_pre, row(ssm_d), cos128, sin128,
                          wt_full, min(256, seq))

    g_small = _pack_small_grads([g_br, g_bi, g_cr, g_ci], g_vec, g_ar, g_ai, g_dt, g_sink[:, 0], loss_blk[0, 0])
    return g_x, g_wt, g_w_out, g_w_glu, g_small
```

```python
import functools
import math

import numpy as np
import jax
import jax.numpy as jnp
from jax import lax
from jax.experimental import pallas as pl
from jax.experimental.pallas import tpu as pltpu

F32 = jnp.float32
BF16 = jnp.bfloat16
MESH = pl.DeviceIdType.MESH

D_MODEL = 1024
D_ATTN = 512
D_SSM = 512
HEAD_DIM = 64
N_Q_HEADS = 8
WINDOW = 128
ROPE_THETA = 10000.0
SSM_CH = 16
N_GROUPS = 32
SSM_STATE = 64
N_DIR = 2
STATE_W = N_GROUPS * SSM_STATE
N_SLAB = 4
SLAB_IN = 128
SLAB_ST = 512
NORM_EPS = 1e-5
NEG_INF = -1e30
ALPHA = 2.0 ** 0.25
D_IN_PROJ = 2304
N_CHIPS = 4

ADAM_LR = 0.001
ADAM_B1 = 0.9
ADAM_B2 = 0.999
ADAM_EPS = 1e-08
ADAM_WD = 0.01
ADAM_STEP = 10

SUBSEG = 8
SCAN_LANES = 512
SCAN_UNROLL = 4
VMEM_LIMIT = 48 * 1024 * 1024
ADAMW_BLOCK_BYTES = 3 * 512 * 1024

_PAIR_PERM = np.array([(64 * j + l) if l < 64 else (64 * (j + 4) + l - 64) for j in range(4) for l in range(128)])
_PAIR_INV = np.argsort(_PAIR_PERM)


def _cparams(sem=None):
    return pltpu.CompilerParams(dimension_semantics=sem, vmem_limit_bytes=VMEM_LIMIT)


def _dot(a, b):
    return jnp.dot(a, b, preferred_element_type=F32)


def _dot_nt(a, b):
    return lax.dot_general(a, b, (((1,), (1,)), ((), ())), preferred_element_type=F32)


def _dot_tn(a, b):
    return lax.dot_general(a, b, (((0,), (0,)), ((), ())), preferred_element_type=F32)


def _sigmoid(z):
    return 1.0 / (1.0 + jnp.exp(-z))


def _all_gather_chips(shards, out_dtype, name):
    n = len(shards)

    def body(*refs):
        in_refs, out_refs = refs[:n], refs[n:2 * n]
        send_sems, recv_sems = refs[2 * n:]
        x, y, c = lax.axis_index("x"), lax.axis_index("y"), lax.axis_index("c")
        sibling = (x, y, 1 - c)
        chips = [(1 - x, y), (x, 1 - y), (1 - x, 1 - y)]

        for a in range(n):
            out_refs[a][2 * x + y] = in_refs[a][...].astype(out_dtype)

        def half_of(a, px, py, half):
            rows = in_refs[a].shape[0] // 2
            return out_refs[a].at[2 * px + py, pl.ds(half * rows, rows), :]

        def copy(a, k, px, py, half, to):
            blk = half_of(a, px, py, half)
            return pltpu.make_async_remote_copy(src_ref=blk, dst_ref=blk, send_sem=send_sems.at[6 * a + k],
                                                recv_sem=recv_sems.at[6 * a + k], device_id=to, device_id_type=MESH)

        first = [copy(a, j, x, y, c, (*chips[j], c)) for a in range(n) for j in range(3)]
        for cp in first:
            cp.start()
        passed = []
        for a in range(n):
            for j in range(3):
                copy(a, j, *chips[j], c, (x, y, c)).wait_recv()
                fwd = copy(a, 3 + j, *chips[j], c, sibling)
                fwd.start()
                passed.append(fwd)
        for a in range(n):
            for j in range(3):
                copy(a, 3 + j, *chips[j], 1 - c, (x, y, c)).wait_recv()
        for cp in first + passed:
            cp.wait_send()

    vmem = pl.BlockSpec(memory_space=pltpu.VMEM)
    return pl.pallas_call(
        body, name=name,
        out_shape=[jax.ShapeDtypeStruct((N_CHIPS,) + s.shape, out_dtype) for s in shards],
        in_specs=[vmem] * n, out_specs=[vmem] * n,
        scratch_shapes=[pltpu.SemaphoreType.DMA((6 * n,)), pltpu.SemaphoreType.DMA((6 * n,))],
        compiler_params=pltpu.CompilerParams(vmem_limit_bytes=VMEM_LIMIT),
    )(*shards)


SEMS_PER_ARRAY = 11


def _reduce_all(pieces, narrow, name):
    n = len(pieces)
    halves = [p.shape[1] // 2 for p in pieces]
    wire = [BF16 if nar else F32 for nar in narrow]

    def body(*refs):
        p_refs, out_refs = refs[:n], refs[n:2 * n]
        a_refs, s_refs, b_refs = refs[2 * n:3 * n], refs[3 * n:4 * n], refs[4 * n:5 * n]
        send_sems, recv_sems = refs[5 * n:]
        x, y, c = lax.axis_index("x"), lax.axis_index("y"), lax.axis_index("c")
        me = 2 * x + y
        sibling = (x, y, 1 - c)
        chips = [(1 - x, y), (x, 1 - y), (1 - x, 1 - y)]
        slot = [2 * px + py for px, py in chips]

        def copy(a, k, src, dst, to):
            return pltpu.make_async_remote_copy(src_ref=src, dst_ref=dst, send_sem=send_sems.at[SEMS_PER_ARRAY * a + k],
                                                recv_sem=recv_sems.at[SEMS_PER_ARRAY * a + k],
                                                device_id=to, device_id_type=MESH)

        def rows(a, half):
            return pl.ds(pl.multiple_of(half * halves[a], 16), halves[a])

        started = []
        for a in range(n):
            cp = copy(a, 0, p_refs[a].at[:, rows(a, 1 - c), :], a_refs[a], sibling)
            cp.start()
            started.append(cp)
        for a in range(n):
            started[a].wait_recv()
            for k in range(N_CHIPS):
                acc = a_refs[a][k] + p_refs[a][k, rows(a, c), :]
                a_refs[a][k] = acc
                s_refs[a][k] = acc.astype(wire[a])
            b_refs[a][me] = s_refs[a][me]
            for j in range(3):
                cp = copy(a, 1 + j, s_refs[a].at[slot[j]], b_refs[a].at[me], (*chips[j], c))
                cp.start()
                started.append(cp)
        for a in range(n):
            for j in range(3):
                copy(a, 1 + j, s_refs[a].at[slot[j]], b_refs[a].at[slot[j]], (x, y, c)).wait_recv()
            terms = [jnp.where(me == k, a_refs[a][k], b_refs[a][k].astype(F32)) for k in range(N_CHIPS)]
            total = (terms[0] + terms[1]) + (terms[2] + terms[3])
            if a < n - 1:
                done = out_refs[a].at[rows(a, c), :]
                out_refs[a][rows(a, c), :] = total
            else:
                done = out_refs[a].at[me, rows(a, c), :]
                out_refs[a][me, rows(a, c), :] = total
            cp = copy(a, 4, done, done, sibling)
            cp.start()
            started.append(cp)
        last = n - 1
        piece = lambda k, half: out_refs[last].at[k, rows(last, half), :]
        for j in range(3):
            cp = copy(last, 5 + j, piece(me, c), piece(me, c), (*chips[j], c))
            cp.start()
            started.append(cp)
        for j in range(3):
            copy(last, 5 + j, piece(slot[j], c), piece(slot[j], c), (x, y, c)).wait_recv()
            cp = copy(last, 8 + j, piece(slot[j], c), piece(slot[j], c), sibling)
            cp.start()
            started.append(cp)
        for a in range(n - 1):
            copy(a, 4, out_refs[a].at[rows(a, 1 - c), :], out_refs[a].at[rows(a, 1 - c), :], (x, y, c)).wait_recv()
        copy(last, 4, piece(me, 1 - c), piece(me, 1 - c), (x, y, c)).wait_recv()
        for j in range(3):
            copy(last, 8 + j, piece(slot[j], 1 - c), piece(slot[j], 1 - c), (x, y, c)).wait_recv()
        for cp in started:
            cp.wait_send()

    vmem = pl.BlockSpec(memory_space=pltpu.VMEM)
    half_shape = lambda a: (N_CHIPS, halves[a], pieces[a].shape[2])
    return pl.pallas_call(
        body, name=name,
        out_shape=[jax.ShapeDtypeStruct(p.shape[1:] if a < n - 1 else p.shape, F32) for a, p in enumerate(pieces)],
        in_specs=[vmem] * n, out_specs=[vmem] * n,
        scratch_shapes=[pltpu.VMEM(half_shape(a), F32) for a in range(n)]
        + [pltpu.VMEM(half_shape(a), wire[a]) for a in range(n)]
        + [pltpu.VMEM(half_shape(a), wire[a]) for a in range(n)]
        + [pltpu.SemaphoreType.DMA((SEMS_PER_ARRAY * n,)), pltpu.SemaphoreType.DMA((SEMS_PER_ARRAY * n,))],
        compiler_params=pltpu.CompilerParams(vmem_limit_bytes=VMEM_LIMIT),
    )(*pieces)


def _ssm_param_values(ar, ai, logdt):
    dt = jnp.exp(logdt)
    mag = jnp.exp(dt * ar)
    cs, sn = jnp.cos(dt * ai), jnp.sin(dt * ai)
    lr, li = mag * cs, mag * sn
    den = ar * ar + ai * ai
    nr = (lr - 1.0) * ar + li * ai
    ni = li * ar - (lr - 1.0) * ai
    return dt, mag, lr, li, den, nr, ni


GROUPS_PER_SLAB = N_GROUPS // N_SLAB


def _slab_masks():
    def eq(shape, f_row, f_col):
        return (f_row(lax.broadcasted_iota(jnp.int32, shape, 0)) == f_col(lax.broadcasted_iota(jnp.int32, shape, 1))).astype(F32)
    spread = eq((SSM_STATE, SLAB_ST), lambda r: r, lambda c: c % SSM_STATE)
    spread_t = eq((SLAB_ST, SSM_STATE), lambda r: r % SSM_STATE, lambda c: c)
    keep = eq((SLAB_IN, SLAB_ST), lambda r: r // SSM_CH, lambda c: c // SSM_STATE)
    keep_t = eq((SLAB_ST, SLAB_IN), lambda r: r // SSM_STATE, lambda c: c // SSM_CH)
    repeat = eq((N_DG * SSM_CH, N_DG), lambda r: r // SSM_CH, lambda c: c)
    return spread, spread_t, keep, keep_t, repeat


def _split3(t):
    hi = t.astype(BF16)
    rest = t - hi.astype(F32)
    mid = rest.astype(BF16)
    return hi, mid, (rest - mid.astype(F32)).astype(BF16)


def _select(dot, ones01, t, ones_first):
    o = ones01.astype(BF16)
    parts = [dot(o, p) if ones_first else dot(p, o) for p in _split3(t)]
    return (parts[0] + parts[1]) + parts[2]


def _ssm_params_fwd(ar, ai, logdt, br, bi, cr, ci, n_square):
    def body(ar_ref, ai_ref, dt_ref, br_ref, bi_ref, cr_ref, ci_ref, lam_ref, bb_ref, bbt_ref, cb_ref, cbt_ref):
        _, _, lr, li, den, nr, ni = _ssm_param_values(ar_ref[...], ai_ref[...], dt_ref[...])
        lam_ref[0] = lr
        lam_ref[1] = li
        pr, pi = lr, li
        for _ in range(n_square):
            pr, pi = pr * pr - pi * pi, 2.0 * pr * pi
        lam_ref[2] = pr
        lam_ref[3] = pi
        spread, spread_t, keep, keep_t, repeat = _slab_masks()
        fr = _select(_dot, repeat, nr / den, True)
        fi = _select(_dot, repeat, ni / den, True)
        b_r, b_i = br_ref[...], bi_ref[...]
        bbar = (fr * b_r - fi * b_i, fr * b_i + fi * b_r)
        c_par = (cr_ref[...], ci_ref[...])
        spread, spread_t = spread.astype(BF16), spread_t.astype(BF16)
        for src, wide_ref, tall_ref in ((bbar, bb_ref, bbt_ref), (c_par, cbt_ref, cb_ref)):
            for q in range(2):
                for d in range(N_DIR):
                    for k in range(N_SLAB):
                        r0 = (d * N_GROUPS + k * GROUPS_PER_SLAB) * SSM_CH
                        blk = src[q][r0:r0 + SLAB_IN].astype(BF16)
                        wide_ref[q, d, k] = (_dot(blk, spread) * keep).astype(BF16)
                        tall_ref[q, d, k] = (_dot_nt(spread_t, blk) * keep_t).astype(BF16)

    wide = jax.ShapeDtypeStruct((2, N_DIR, N_SLAB, SLAB_IN, SLAB_ST), BF16)
    tall = jax.ShapeDtypeStruct((2, N_DIR, N_SLAB, SLAB_ST, SLAB_IN), BF16)
    return pl.pallas_call(body, name="ssm_params_fwd",
                          out_shape=[jax.ShapeDtypeStruct((4,) + ar.shape, F32), wide, tall, tall, wide],
                          compiler_params=pltpu.CompilerParams(vmem_limit_bytes=VMEM_LIMIT),
                          )(ar, ai, logdt, br, bi, cr, ci)


def _ssm_params_bwd(ar, ai, logdt, br, bi, g_slabs_b, g_slabs_c, g_lam):
    def body(ar_ref, ai_ref, dt_ref, br_ref, bi_ref, gb0_ref, gb1_ref, gc0_ref, gc1_ref, gl0_ref, gl1_ref,
             gar_ref, gai_ref, gdt_ref, gbr_ref, gbi_ref, gcr_ref, gci_ref, dbb, dlam):
        spread, spread_t, keep, keep_t, repeat = _slab_masks()
        for d, (gb_ref, gc_ref) in enumerate(((gb0_ref, gc0_ref), (gb1_ref, gc1_ref))):
            for q in range(2):
                for k in range(N_SLAB):
                    r0 = (d * N_GROUPS + k * GROUPS_PER_SLAB) * SSM_CH
                    dbb[q, r0:r0 + SLAB_IN, :] = _select(_dot, spread_t, gb_ref[q, k] * keep, False)
                    out_ref = gcr_ref if q == 0 else gci_ref
                    out_ref[r0:r0 + SLAB_IN, :] = _select(_dot_tn, spread_t, gc_ref[q, k] * keep_t, False)
        grp = (lax.broadcasted_iota(jnp.int32, (N_GROUPS, STATE_W), 0)
               == lax.broadcasted_iota(jnp.int32, (N_GROUPS, STATE_W), 1) // SSM_STATE).astype(F32)
        pick = (lax.broadcasted_iota(jnp.int32, (STATE_W, SSM_STATE), 0) % SSM_STATE
                == lax.broadcasted_iota(jnp.int32, (STATE_W, SSM_STATE), 1)).astype(F32)
        for d, gl_ref in enumerate((gl0_ref, gl1_ref)):
            for q in range(2):
                row = jnp.sum(gl_ref[q], axis=0, keepdims=True)
                dlam[q, d * N_GROUPS:(d + 1) * N_GROUPS, :] = _select(_dot, pick, grp * row, False)

        a_r, a_i = ar_ref[...], ai_ref[...]
        dt, mag, lr, li, den, nr, ni = _ssm_param_values(a_r, a_i, dt_ref[...])
        fr = _select(_dot, repeat, nr / den, True)
        fi = _select(_dot, repeat, ni / den, True)
        b_r, b_i = br_ref[...], bi_ref[...]
        g_r, g_i = dbb[0], dbb[1]
        gbr_ref[...] = fr * g_r + fi * g_i
        gbi_ref[...] = fr * g_i - fi * g_r
        d_fr = _select(_dot_tn, repeat, b_r * g_r + b_i * g_i, True)
        d_fi = _select(_dot_tn, repeat, b_r * g_i - b_i * g_r, True)
        d_nr, d_ni = d_fr / den, d_fi / den
        d_den = -(d_fr * nr + d_fi * ni) / (den * den)
        d_lr = dlam[0] + d_nr * a_r - d_ni * a_i
        d_li = dlam[1] + d_nr * a_i + d_ni * a_r
        d_ar = d_nr * (lr - 1.0) + d_ni * li + d_den * 2.0 * a_r
        d_ai = d_nr * li - d_ni * (lr - 1.0) + d_den * 2.0 * a_i
        d_mag = (d_lr * lr + d_li * li) / mag
        d_theta = d_li * lr - d_lr * li
        gar_ref[...] = d_ar + d_mag * mag * dt
        gai_ref[...] = d_ai + d_theta * dt
        d_dt = d_mag * mag * a_r + d_theta * a_i
        gdt_ref[...] = jnp.sum(d_dt, axis=1, keepdims=True) * dt

    small = jax.ShapeDtypeStruct(ar.shape, F32)
    big = jax.ShapeDtypeStruct(br.shape, F32)
    return pl.pallas_call(
        body, name="ssm_params_bwd",
        out_shape=[small, small, jax.ShapeDtypeStruct(logdt.shape, F32), big, big, big, big],
        scratch_shapes=[pltpu.VMEM((2,) + br.shape, F32), pltpu.VMEM((2,) + ar.shape, F32)],
        compiler_params=pltpu.CompilerParams(vmem_limit_bytes=VMEM_LIMIT),
    )(ar, ai, logdt, br, bi, *g_slabs_b, *g_slabs_c, *g_lam)


def _rope_tables(seq):
    half = HEAD_DIM // 2
    inv_freq = ROPE_THETA ** (-jnp.arange(half, dtype=F32) / half)
    ang = jnp.arange(seq, dtype=jnp.int32).astype(F32)[:, None] * inv_freq[None, :]
    cos, sin = jnp.cos(ang), jnp.sin(ang)
    cos128 = jnp.concatenate([cos, cos, cos, cos], axis=1)
    sin128 = jnp.concatenate([-sin, sin, -sin, sin], axis=1)
    return cos128, sin128


def _rotate_half_unsigned(t):
    lane = lax.broadcasted_iota(jnp.int32, t.shape, 1)
    return jnp.where((lane % HEAD_DIM) < HEAD_DIM // 2, pltpu.roll(t, 96, 1), pltpu.roll(t, 32, 1))


def _rope(t, cos, sin_signed):
    return t * cos + _rotate_half_unsigned(t) * sin_signed


def _pair_blocks(base):
    out = []
    for j in range(4):
        for g in range(2):
            nat = base + HEAD_DIM * (4 * g + j)
            par = base + 128 * j + HEAD_DIM * g
            out.append((slice(nat, nat + HEAD_DIM), slice(par, par + HEAD_DIM)))
    return out


W_Q, W_KV, W_ZA, W_U, W_ZS = 0, 512, 768, 1280, 1792


def _proj(x, wt, cos128, sin128, tb):
    seq = x.shape[0]

    def body(x_ref, wt_ref, cos_ref, sin_ref, q_ref, k_ref, v_ref, za_ref, u_ref, zs_ref, wp):
        @pl.when(pl.program_id(0) == 0)
        def _():
            for dst_base, src_base in ((0, W_Q), (512, W_ZA)):
                for nat, par in _pair_blocks(0):
                    wp[dst_base + par.start:dst_base + par.stop, :] = wt_ref[src_base + nat.start:src_base + nat.stop, :]

        xb = x_ref[...].astype(BF16)
        cos, sin = cos_ref[...], sin_ref[...]
        lo = lax.broadcasted_iota(jnp.int32, (tb, 128), 1) < HEAD_DIM
        q = _dot_nt(xb, wp[0:512, :])
        for j in range(4):
            qj = _rope(q[:, 128 * j:128 * (j + 1)], cos, sin)
            q_ref[j] = jnp.where(lo, qj, 0.0).astype(BF16)
            q_ref[4 + j] = jnp.where(lo, 0.0, qj).astype(BF16)
        kv = _dot_nt(xb, wt_ref[W_KV:W_ZA, :])
        k_ref[...] = _rope(kv[:, 0:128], cos, sin).astype(BF16)
        v_ref[...] = kv[:, 128:256].astype(BF16)
        za_ref[...] = _dot_nt(xb, wp[512:1024, :])
        u_val = _dot_nt(xb, wt_ref[W_U:W_ZS, :])
        for k in range(N_SLAB):
            u_ref[k] = u_val[:, k * SLAB_IN:(k + 1) * SLAB_IN]
        zs_ref[...] = _dot_nt(xb, wt_ref[W_ZS:D_IN_PROJ, :])

    row = lambda w: pl.BlockSpec((tb, w), lambda i: (i, 0))
    return pl.pallas_call(
        body, name="proj", grid=(seq // tb,),
        in_specs=[row(D_MODEL), pl.BlockSpec((D_IN_PROJ, D_MODEL), lambda i: (0, 0)), row(128), row(128)],
        out_specs=[pl.BlockSpec((8, tb, 128), lambda i: (0, i, 0)), row(128), row(128), row(512),
                   pl.BlockSpec((N_SLAB, tb, SLAB_IN), lambda i: (0, i, 0)), row(512)],
        out_shape=[jax.ShapeDtypeStruct((8, seq, 128), BF16), jax.ShapeDtypeStruct((seq, 128), BF16),
                   jax.ShapeDtypeStruct((seq, 128), BF16), jax.ShapeDtypeStruct((seq, 512), F32),
                   jax.ShapeDtypeStruct((N_SLAB, seq, SLAB_IN), F32), jax.ShapeDtypeStruct((seq, 512), F32)],
        scratch_shapes=[pltpu.VMEM((1024, D_MODEL), BF16)],
        compiler_params=_cparams(("arbitrary",)),
    )(x, wt, cos128, sin128)


ATT_TQ = 128
ATT_KEYS = 3 * ATT_TQ


def _attn_window(i, seq):
    start = jnp.clip(i * ATT_TQ - WINDOW, 0, seq - ATT_KEYS)
    return pl.multiple_of(start, ATT_TQ)


def _attn_bias():
    r = jnp.arange(ATT_TQ, dtype=jnp.int32)[None, :, None]
    c = jnp.arange(ATT_KEYS, dtype=jnp.int32)[None, None, :]
    off = (jnp.arange(3, dtype=jnp.int32) * ATT_TQ)[:, None, None]
    return jnp.where(jnp.abs(r + off - c) <= WINDOW, 0.0, NEG_INF).astype(F32)


def _attn_bias_spec(nblk):
    pick = lambda i: jnp.where(i == 0, 0, jnp.where(i == nblk - 1, 2, 1))
    return pl.BlockSpec((None, ATT_TQ, ATT_KEYS), lambda i: (pick(i), 0, 0))


def _attn_softmax(q_ref, k_ref, v_ref, sink_ref, bias_ref, start):
    kw = k_ref[pl.ds(start, ATT_KEYS), :]
    vw = v_ref[pl.ds(start, ATT_KEYS), :]
    qall = q_ref[...].reshape(N_Q_HEADS * ATT_TQ, 128)
    s = (_dot_nt(qall, kw) * (HEAD_DIM ** -0.5)).reshape(N_Q_HEADS, ATT_TQ, ATT_KEYS) + bias_ref[...][None]
    tiles = [s[:, :, 128 * t:128 * (t + 1)] for t in range(ATT_KEYS // 128)]
    m = jnp.max(jnp.maximum(jnp.maximum(tiles[0], tiles[1]), tiles[2]), axis=2, keepdims=True)
    sink = sink_ref[...]
    m_b = jnp.maximum(jnp.broadcast_to(m, (N_Q_HEADS, ATT_TQ, 128)), sink)
    p = jnp.concatenate([jnp.exp(t - m_b) for t in tiles], axis=2)
    p_sink = jnp.exp(sink - m_b)
    lo_k = lax.broadcasted_iota(jnp.int32, (ATT_KEYS, 128), 1) < HEAD_DIM
    v_f = vw.astype(F32)
    v_lo, v_hi = jnp.where(lo_k, v_f, 1.0).astype(BF16), jnp.where(lo_k, 1.0, v_f).astype(BF16)
    pb = p.astype(BF16).reshape(N_Q_HEADS * ATT_TQ, ATT_KEYS)
    half = 4 * ATT_TQ
    r = jnp.concatenate([_dot(pb[:half], v_lo), _dot(pb[half:], v_hi)], axis=0).reshape(N_Q_HEADS, ATT_TQ, 128)
    return kw, vw, qall, p, p_sink, r


def _attn_fwd(q_stack, k, v, sink128, bias):
    seq = k.shape[0]

    def body(q_ref, k_ref, v_ref, sink_ref, bias_ref, o_ref):
        start = _attn_window(pl.program_id(0), seq)
        _, _, _, _, p_sink, r = _attn_softmax(q_ref, k_ref, v_ref, sink_ref, bias_ref, start)
        out = r / (pltpu.roll(r, HEAD_DIM, 2) + p_sink)
        lo = lax.broadcasted_iota(jnp.int32, (ATT_TQ, 128), 1) < HEAD_DIM
        for j in range(4):
            o_ref[:, 128 * j:128 * (j + 1)] = jnp.where(lo, out[j], out[4 + j])

    full = lambda w: pl.BlockSpec((seq, w), lambda i: (0, 0))
    return pl.pallas_call(
        body, name="attn_fwd", grid=(seq // ATT_TQ,),
        in_specs=[pl.BlockSpec((8, ATT_TQ, 128), lambda i: (0, i, 0)), full(128), full(128),
                  pl.BlockSpec((N_Q_HEADS, 1, 128), lambda i: (0, 0, 0)), _attn_bias_spec(seq // ATT_TQ)],
        out_specs=pl.BlockSpec((ATT_TQ, 512), lambda i: (i, 0)),
        out_shape=jax.ShapeDtypeStruct((seq, 512), F32),
        compiler_params=_cparams(("arbitrary",)),
    )(q_stack, k, v, sink128, bias)


def _attn_bwd(q_stack, k, v, sink128, bias, d_o):
    seq = k.shape[0]

    def body(q_ref, k_ref, v_ref, sink_ref, bias_ref, do_ref, dq_ref, dk_ref, dv_ref, dsink_ref, sink_acc):
        i = pl.program_id(0)

        @pl.when(i == 0)
        def _():
            dk_ref[...] = jnp.zeros_like(dk_ref)
            dv_ref[...] = jnp.zeros_like(dv_ref)
            sink_acc[...] = jnp.zeros_like(sink_acc)

        start = _attn_window(i, seq)
        kw, vw, qall, p, p_sink, r = _attn_softmax(q_ref, k_ref, v_ref, sink_ref, bias_ref, start)
        lo = lax.broadcasted_iota(jnp.int32, (ATT_TQ, 128), 1) < HEAD_DIM
        lo3 = lo[None]
        grp0 = lax.broadcasted_iota(jnp.int32, (N_Q_HEADS, ATT_TQ, 128), 0) < 4
        val = grp0 == lo3
        swapped = pltpu.roll(r, HEAD_DIM, 2)
        inv = 1.0 / (jnp.where(val, swapped, r) + p_sink)
        d_o_blk = do_ref[...]
        do3 = jnp.where(val, jnp.concatenate([d_o_blk[None, :, 128 * j:128 * (j + 1)] for j in range(4)] * 2, axis=0), 0.0)
        t = (do3 * r).reshape(N_Q_HEADS * ATT_TQ, 128)
        t_hi = t.astype(BF16)
        t_lo = (t - t_hi.astype(F32)).astype(BF16)
        ones = jnp.ones((128, 128), BF16)
        delta = (_dot(t_hi, ones) + _dot(t_lo, ones)).reshape(N_Q_HEADS, ATT_TQ, 128) * inv
        sink_acc[...] += -(p_sink * inv) * delta
        do_all = do3.astype(BF16).reshape(N_Q_HEADS * ATT_TQ, 128)
        dp = _dot_nt(do_all, vw).reshape(N_Q_HEADS, ATT_TQ, ATT_KEYS)
        probs, ds = [], []
        for tl in range(ATT_KEYS // 128):
            cols = slice(128 * tl, 128 * (tl + 1))
            probs_t = p[:, :, cols] * inv
            probs.append(probs_t.astype(BF16))
            ds.append((probs_t * (dp[:, :, cols] - delta)).astype(BF16))
        probs_all = jnp.concatenate(probs, axis=2).reshape(N_Q_HEADS * ATT_TQ, ATT_KEYS)
        ds_all = jnp.concatenate(ds, axis=2).reshape(N_Q_HEADS * ATT_TQ, ATT_KEYS)
        scale = HEAD_DIM ** -0.5
        dq_all = (_dot(ds_all, kw) * scale).reshape(N_Q_HEADS, ATT_TQ, 128)
        for j in range(4):
            dq_ref[:, 128 * j:128 * (j + 1)] = jnp.where(lo, dq_all[j], dq_all[4 + j])
        dk_ref[pl.ds(start, ATT_KEYS), :] += _dot_tn(ds_all, qall) * scale
        dv_ref[pl.ds(start, ATT_KEYS), :] += _dot_tn(probs_all, do_all)

        @pl.when(i == pl.num_programs(0) - 1)
        def _():
            dsink_ref[...] = jnp.sum(sink_acc[...], axis=1)

    full = lambda w: pl.BlockSpec((seq, w), lambda i: (0, 0))
    return pl.pallas_call(
        body, name="attn_bwd", grid=(seq // ATT_TQ,),
        in_specs=[pl.BlockSpec((8, ATT_TQ, 128), lambda i: (0, i, 0)), full(128), full(128),
                  pl.BlockSpec((N_Q_HEADS, 1, 128), lambda i: (0, 0, 0)),
                  _attn_bias_spec(seq // ATT_TQ), pl.BlockSpec((ATT_TQ, 512), lambda i: (i, 0))],
        out_specs=[pl.BlockSpec((ATT_TQ, 512), lambda i: (i, 0)), full(128), full(128),
                   pl.BlockSpec((N_Q_HEADS, 128), lambda i: (0, 0))],
        out_shape=[jax.ShapeDtypeStruct((seq, 512), F32), jax.ShapeDtypeStruct((seq, 128), F32),
                   jax.ShapeDtypeStruct((seq, 128), F32), jax.ShapeDtypeStruct((N_Q_HEADS, 128), F32)],
        scratch_shapes=[pltpu.VMEM((N_Q_HEADS, ATT_TQ, 128), F32)],
        compiler_params=_cparams(("arbitrary",)),
    )(q_stack, k, v, sink128, bias, d_o)


def _permute_rows(dst_ref, src_ref, sub_len):
    for k in range(N_SLAB):
        for j in range(sub_len):
            dst_ref[k, 8 * j:8 * (j + 1), :] = src_ref.at[k][pl.ds(j, SUBSEG, stride=sub_len), :]


def _unpermute_rows(dst_ref, src_ref, sub_len):
    for k in range(N_SLAB):
        for s in range(SUBSEG):
            dst_ref[k, s * sub_len:(s + 1) * sub_len, :] = src_ref.at[k][pl.ds(s, sub_len, stride=SUBSEG), :]


def _scan_chunk(br_ref, bi_ref, lr_row, li_row, init, cols, *, sub_len, reverse, store):
    lr = jnp.broadcast_to(lr_row[:, cols], (SUBSEG, SCAN_LANES))
    li = jnp.broadcast_to(li_row[:, cols], (SUBSEG, SCAN_LANES))
    if init is None:
        sr = si = jnp.zeros((SUBSEG, SCAN_LANES), F32)
    else:
        sr, si = init
    for jj in range(sub_len):
        rows = slice(SUBSEG * ((sub_len - 1 - jj) if reverse else jj), SUBSEG * (((sub_len - 1 - jj) if reverse else jj) + 1))
        sr, si = lr * sr - li * si + br_ref[rows, cols], lr * si + li * sr + bi_ref[rows, cols]
        if store:
            br_ref[rows, cols] = sr
            bi_ref[rows, cols] = si
    return sr, si


def _resolve_chunk(z, carry_refs, start_refs, pr_row, pi_row, cols, *, reverse):
    cr, ci = carry_refs[0][0:1, cols], carry_refs[1][0:1, cols]
    pr, pi = pr_row[:, cols], pi_row[:, cols]
    for s in (range(SUBSEG - 1, -1, -1) if reverse else range(SUBSEG)):
        start_refs[0][s:s + 1, cols] = cr
        start_refs[1][s:s + 1, cols] = ci
        cr, ci = pr * cr - pi * ci + z[0][s:s + 1, :], pr * ci + pi * cr + z[1][s:s + 1, :]
    carry_refs[0][0:1, cols] = cr
    carry_refs[1][0:1, cols] = ci


def _param_specs(direction):
    row = lambda q: pl.BlockSpec((None, None, 1, STATE_W), lambda i: (q, direction, 0, 0))
    wide = lambda q: pl.BlockSpec((None, None, N_SLAB, SLAB_IN, SLAB_ST), lambda i: (q, direction, 0, 0, 0))
    tall = lambda q: pl.BlockSpec((None, None, N_SLAB, SLAB_ST, SLAB_IN), lambda i: (q, direction, 0, 0, 0))
    return [row(q) for q in range(4)], [wide(0), wide(1)], [tall(0), tall(1)]


def _ssm_fwd(u, lam, bb, cb, *, direction, tb, name):
    reverse = direction == 1
    seq = u.shape[1]
    nblk = seq // tb
    sub_len = tb // SUBSEG

    def body(u_ref, lr_ref, li_ref, pr_ref, pi_ref, bbr_ref, bbi_ref, cbr_ref, cbi_ref,
             y_ref, sr_ref, si_ref, xr, xi, up, yp, car, cai):
        @pl.when(pl.program_id(0) == 0)
        def _():
            car[...] = jnp.zeros_like(car)
            cai[...] = jnp.zeros_like(cai)

        _permute_rows(up, u_ref, sub_len)
        lr, li, pr, pi = lr_ref[...], li_ref[...], pr_ref[...], pi_ref[...]
        chunk = lambda k: slice(k * SLAB_ST, (k + 1) * SLAB_ST)

        def drive(k):
            ub = up[k].astype(BF16)
            xr[:, chunk(k)] = _dot(ub, bbr_ref[k])
            xi[:, chunk(k)] = _dot(ub, bbi_ref[k])

        def scan(k):
            z = _scan_chunk(xr, xi, lr, li, None, chunk(k), sub_len=sub_len, reverse=reverse, store=False)
            _resolve_chunk(z, (car, cai), (sr_ref, si_ref), pr, pi, chunk(k), reverse=reverse)
            _scan_chunk(xr, xi, lr, li, (sr_ref[:, chunk(k)], si_ref[:, chunk(k)]), chunk(k),
                        sub_len=sub_len, reverse=reverse, store=True)

        def read_out(k):
            yp[k] = _dot(xr[:, chunk(k)].astype(BF16), cbr_ref[k]) - _dot(xi[:, chunk(k)].astype(BF16), cbi_ref[k])

        drive(0)
        for k in range(N_SLAB):
            if k + 1 < N_SLAB:
                drive(k + 1)
            scan(k)
            if k > 0:
                read_out(k - 1)
        read_out(N_SLAB - 1)
        _unpermute_rows(y_ref, yp, sub_len)

    blk = (lambda i: nblk - 1 - i) if reverse else (lambda i: i)
    rows, wide, tall = _param_specs(direction)
    tok = pl.BlockSpec((N_SLAB, tb, SLAB_IN), lambda i: (0, blk(i), 0))
    start_spec = pl.BlockSpec((None, SUBSEG, STATE_W), lambda i: (blk(i), 0, 0))
    return pl.pallas_call(
        body, name=name, grid=(nblk,),
        in_specs=[tok] + rows + wide + tall,
        out_specs=[tok, start_spec, start_spec],
        out_shape=[jax.ShapeDtypeStruct((N_SLAB, seq, SLAB_IN), F32), jax.ShapeDtypeStruct((nblk, SUBSEG, STATE_W), F32),
                   jax.ShapeDtypeStruct((nblk, SUBSEG, STATE_W), F32)],
        scratch_shapes=[pltpu.VMEM((tb, STATE_W), F32), pltpu.VMEM((tb, STATE_W), F32),
                        pltpu.VMEM((N_SLAB, tb, SLAB_IN), F32), pltpu.VMEM((N_SLAB, tb, SLAB_IN), F32),
                        pltpu.VMEM((SUBSEG, STATE_W), F32), pltpu.VMEM((SUBSEG, STATE_W), F32)],
        compiler_params=_cparams(("arbitrary",)),
    )(u, lam, lam, lam, lam, bb, bb, cb, cb)


def _ssm_bwd(u, dy, starts, lam, bb, bbt, cb_t, *, direction, tb, name):
    reverse = direction == 1
    seq = u.shape[1]
    nblk = seq // tb
    sub_len = tb // SUBSEG

    def body(u_ref, dy_ref, sr_ref, si_ref, lr_ref, li_ref, pr_ref, pi_ref, bbr_ref, bbi_ref, btr_ref, bti_ref,
             ctr_ref, cti_ref, du_ref, gb_ref, gc_ref, dl_ref,
             xr, xi, gr, gi, up, dyp, dup, gsr, gsi, car, cai):
        gbr_ref, gbi_ref = gb_ref.at[0], gb_ref.at[1]
        gcr_ref, gci_ref = gc_ref.at[0], gc_ref.at[1]
        dlr_ref, dli_ref = dl_ref.at[0], dl_ref.at[1]

        @pl.when(pl.program_id(0) == 0)
        def _():
            for ref in (car, cai, gbr_ref, gbi_ref, gcr_ref, gci_ref, dlr_ref, dli_ref):
                ref[...] = jnp.zeros_like(ref)

        _permute_rows(up, u_ref, sub_len)
        _permute_rows(dyp, dy_ref, sub_len)
        lr, li, pr, pi = lr_ref[...], li_ref[...], pr_ref[...], pi_ref[...]
        nli, npi = -li, -pi
        chunk = lambda k: slice(k * SLAB_ST, (k + 1) * SLAB_ST)

        def drive(k):
            ub = up[k].astype(BF16)
            xr[:, chunk(k)] = _dot(ub, bbr_ref[k])
            xi[:, chunk(k)] = _dot(ub, bbi_ref[k])
            dyb = dyp[k].astype(BF16)
            gr[:, chunk(k)] = _dot(dyb, ctr_ref[k])
            gi[:, chunk(k)] = -_dot(dyb, cti_ref[k])

        def scan_x(k):
            _scan_chunk(xr, xi, lr, li, (sr_ref[:, chunk(k)], si_ref[:, chunk(k)]), chunk(k),
                        sub_len=sub_len, reverse=reverse, store=True)

        def grad_c(k):
            dyb = dyp[k].astype(BF16)
            gcr_ref[k] += _dot_tn(xr[:, chunk(k)].astype(BF16), dyb)
            gci_ref[k] -= _dot_tn(xi[:, chunk(k)].astype(BF16), dyb)

        def scan_g(k):
            z = _scan_chunk(gr, gi, lr, nli, None, chunk(k), sub_len=sub_len, reverse=not reverse, store=False)
            _resolve_chunk(z, (car, cai), (gsr, gsi), pr, npi, chunk(k), reverse=not reverse)
            _scan_chunk(gr, gi, lr, nli, (gsr[:, chunk(k)], gsi[:, chunk(k)]), chunk(k),
                        sub_len=sub_len, reverse=not reverse, store=True)

        def grad_b_du(k):
            ub = up[k].astype(BF16)
            grb, gib = gr[:, chunk(k)].astype(BF16), gi[:, chunk(k)].astype(BF16)
            gbr_ref[k] += _dot_tn(ub, grb)
            gbi_ref[k] += _dot_tn(ub, gib)
            dup[k] = _dot(grb, btr_ref[k]) + _dot(gib, bti_ref[k])

        def grad_lambda(k):
            cols = chunk(k)
            acc_r, acc_i = dlr_ref[:, cols], dli_ref[:, cols]
            for jj in range(sub_len):
                prev = jj + 1 if reverse else jj - 1
                if 0 <= prev < sub_len:
                    x_r, x_i = xr[SUBSEG * prev:SUBSEG * (prev + 1), cols], xi[SUBSEG * prev:SUBSEG * (prev + 1), cols]
                else:
                    x_r, x_i = sr_ref[:, cols], si_ref[:, cols]
                g_r, g_i = gr[SUBSEG * jj:SUBSEG * (jj + 1), cols], gi[SUBSEG * jj:SUBSEG * (jj + 1), cols]
                acc_r = acc_r + (g_r * x_r + g_i * x_i)
                acc_i = acc_i + (g_i * x_r - g_r * x_i)
            dlr_ref[:, cols] = acc_r
            dli_ref[:, cols] = acc_i

        drive(0)
        for k in range(N_SLAB):
            scan_x(k)
            if k + 1 < N_SLAB:
                drive(k + 1)
            grad_c(k)
            scan_g(k)
            grad_b_du(k)
            grad_lambda(k)
        _unpermute_rows(du_ref, dup, sub_len)

    blk = (lambda i: i) if reverse else (lambda i: nblk - 1 - i)
    rows, wide, tall = _param_specs(direction)
    tok = pl.BlockSpec((N_SLAB, tb, SLAB_IN), lambda i: (0, blk(i), 0))
    start_spec = pl.BlockSpec((None, SUBSEG, STATE_W), lambda i: (blk(i), 0, 0))
    gb_shape, gc_shape, dl_shape = (2, N_SLAB, SLAB_IN, SLAB_ST), (2, N_SLAB, SLAB_ST, SLAB_IN), (2, SUBSEG, STATE_W)
    whole = lambda shape: pl.BlockSpec(shape, lambda i: (0,) * len(shape))
    big = lambda: pltpu.VMEM((tb, STATE_W), F32)
    slabs = lambda: pltpu.VMEM((N_SLAB, tb, SLAB_IN), F32)
    tile = lambda: pltpu.VMEM((SUBSEG, STATE_W), F32)
    return pl.pallas_call(
        body, name=name, grid=(nblk,),
        in_specs=[tok, tok, start_spec, start_spec] + rows + wide + tall + wide,
        out_specs=[tok, whole(gb_shape), whole(gc_shape), whole(dl_shape)],
        out_shape=[jax.ShapeDtypeStruct((N_SLAB, seq, SLAB_IN), F32), jax.ShapeDtypeStruct(gb_shape, F32),
                   jax.ShapeDtypeStruct(gc_shape, F32), jax.ShapeDtypeStruct(dl_shape, F32)],
        scratch_shapes=[big(), big(), big(), big(), slabs(), slabs(), slabs(), tile(), tile(), tile(), tile()],
        compiler_params=_cparams(("arbitrary",)),
    )(u, dy, *starts, lam, lam, lam, lam, bb, bb, bbt, bbt, cb_t, cb_t)


GELU_C = math.sqrt(2.0 / math.pi)
GELU_K = 0.044715


def _mid(o, za, u, y_f, y_b, zs, x, target, ssm_d, w_glu, b_glu, g_attn, g_ssm, w_out, ln_g, ln_b, tb):
    seq = x.shape[0]

    def body(o_ref, za_ref, u_ref, yf_ref, yb_ref, zs_ref, x_ref, t_ref, d_ref, wg_ref, bg_ref, ga_ref, gs_ref,
             wo_ref, lg_ref, lb_ref,
             loss_ref, do_ref, dza_ref, dyl_ref, dzs_ref, dpre_ref, gwo_ref, gwg_ref, vec_ref, wop):
        @pl.when(pl.program_id(0) == 0)
        def _():
            for ref in (loss_ref, gwo_ref, gwg_ref, vec_ref):
                ref[...] = jnp.zeros_like(ref)
            for nat, par in _pair_blocks(0):
                wop[par, :] = wo_ref[nat, :]
            wop[D_ATTN:, :] = wo_ref[D_ATTN:, :]

        o, za = o_ref[...], za_ref[...]
        sig_a = _sigmoid(za)
        silu_a = za * sig_a
        ya = o * silu_a
        r_a = lax.rsqrt(jnp.mean(ya * ya, axis=1, keepdims=True) + NORM_EPS)
        n_a = ya * r_a
        g_a = ga_ref[...]
        unslab = lambda ref: jnp.concatenate([ref[k] for k in range(N_SLAB)], axis=1)
        u_blk, zs = unslab(u_ref), zs_ref[...]
        d_row = d_ref[...]
        ylin = d_row * u_blk + unslab(yf_ref) + unslab(yb_ref)
        inner = GELU_C * (ylin + GELU_K * ylin * ylin * ylin)
        th = jnp.tanh(inner)
        gl = 0.5 * ylin * (1.0 + th)
        glb = gl.astype(BF16)
        sg = _sigmoid(_dot(glb, wg_ref[...]) + bg_ref[...])
        y2 = gl * sg
        sig_s = _sigmoid(zs)
        silu_s = zs * sig_s
        ys = y2 * silu_s
        r_s = lax.rsqrt(jnp.mean(ys * ys, axis=1, keepdims=True) + NORM_EPS)
        n_s = ys * r_s
        g_s = gs_ref[...]
        mixed = jnp.concatenate([n_a * g_a, n_s * g_s], axis=1).astype(BF16)
        pre = ALPHA * x_ref[...] + _dot(mixed, wop[...])
        mu = jnp.mean(pre, axis=1, keepdims=True)
        cen = pre - mu
        rstd = lax.rsqrt(jnp.mean(cen * cen, axis=1, keepdims=True) + NORM_EPS)
        hhat = cen * rstd
        ln_g = lg_ref[...]
        err = hhat * ln_g + lb_ref[...] - t_ref[...]
        loss_ref[...] += 0.5 * jnp.sum(jnp.mean(err * err, axis=1, keepdims=True))

        dh = err * (1.0 / D_MODEL)
        vec_ref[0:1, :] += jnp.sum(dh * hhat, axis=0, keepdims=True)
        vec_ref[1:2, :] += jnp.sum(dh, axis=0, keepdims=True)
        dhh = dh * ln_g
        dpre = rstd * (dhh - jnp.mean(dhh, axis=1, keepdims=True) - hhat * jnp.mean(dhh * hhat, axis=1, keepdims=True))
        dpre_ref[...] = dpre
        dpb = dpre.astype(BF16)
        for j in range(4):
            g_pair = _dot_tn(mixed[:, 128 * j:128 * (j + 1)], dpb)
            for g in range(2):
                nat = HEAD_DIM * (4 * g + j)
                gwo_ref[nat:nat + HEAD_DIM, :] += g_pair[HEAD_DIM * g:HEAD_DIM * (g + 1), :]
        gwo_ref[D_ATTN:, :] += _dot_tn(mixed[:, D_ATTN:], dpb)
        dmix = _dot_nt(dpb, wop[...])
        dna = dmix[:, :D_ATTN]
        vec_ref[2:3, 0:D_ATTN] += jnp.sum(dna * n_a, axis=0, keepdims=True)
        dna = dna * g_a
        dya = r_a * (dna - n_a * jnp.mean(dna * n_a, axis=1, keepdims=True))
        do_ref[...] = dya * silu_a
        dza_ref[...] = dya * o * (sig_a * (1.0 + za * (1.0 - sig_a)))
        dns = dmix[:, D_ATTN:]
        vec_ref[2:3, D_ATTN:] += jnp.sum(dns * n_s, axis=0, keepdims=True)
        dns = dns * g_s
        dys = r_s * (dns - n_s * jnp.mean(dns * n_s, axis=1, keepdims=True))
        dzs_ref[...] = dys * y2 * (sig_s * (1.0 + zs * (1.0 - sig_s)))
        dy2 = dys * silu_s
        da = dy2 * gl * sg * (1.0 - sg)
        vec_ref[3:4, D_SSM:] += jnp.sum(da, axis=0, keepdims=True)
        dab = da.astype(BF16)
        gwg_ref[...] += _dot_tn(glb, dab)
        dgl = dy2 * sg + _dot_nt(dab, wg_ref[...])
        dylin = dgl * (0.5 * (1.0 + th) + 0.5 * ylin * (1.0 - th * th) * GELU_C * (1.0 + 3.0 * GELU_K * ylin * ylin))
        for k in range(N_SLAB):
            dyl_ref[k] = dylin[:, k * SLAB_IN:(k + 1) * SLAB_IN]
        vec_ref[3:4, 0:D_SSM] += jnp.sum(dylin * u_blk, axis=0, keepdims=True)

    tok = lambda w: pl.BlockSpec((tb, w), lambda i: (i, 0))
    slab = pl.BlockSpec((N_SLAB, tb, SLAB_IN), lambda i: (0, i, 0))
    const = lambda r, c: pl.BlockSpec((r, c), lambda i: (0, 0))
    tok_shape = jax.ShapeDtypeStruct((seq, 512), F32)
    return pl.pallas_call(
        body, name="mid", grid=(seq // tb,),
        in_specs=[tok(512), tok(512), slab, slab, slab, tok(512), tok(1024), tok(1024),
                  const(1, 512), const(512, 512), const(1, 512), const(1, 512), const(1, 512),
                  const(1024, 1024), const(1, 1024), const(1, 1024)],
        out_specs=[const(8, 128), tok(512), tok(512), slab, tok(512), tok(1024),
                   const(1024, 1024), const(512, 512), const(8, 1024)],
        out_shape=[jax.ShapeDtypeStruct((8, 128), F32), tok_shape, tok_shape,
                   jax.ShapeDtypeStruct((N_SLAB, seq, SLAB_IN), F32), tok_shape,
                   jax.ShapeDtypeStruct((seq, 1024), F32), jax.ShapeDtypeStruct((1024, 1024), F32),
                   jax.ShapeDtypeStruct((512, 512), F32), jax.ShapeDtypeStruct((8, 1024), F32)],
        scratch_shapes=[pltpu.VMEM((D_MODEL, D_MODEL), BF16)],
        compiler_params=_cparams(("arbitrary",)),
    )(o, za, u, y_f, y_b, zs, x, target, ssm_d, w_glu, b_glu, g_attn, g_ssm, w_out, ln_g, ln_b)


def _proj_bwd(x, dq, dk, dv, dza, du_f, du_b, dylin, dzs, dpre, ssm_d, cos128, sin128, wt, tb):
    seq = x.shape[0]

    def body(x_ref, dq_ref, dk_ref, dv_ref, dza_ref, duf_ref, dub_ref, dyl_ref, dzs_ref, dpre_ref, d_ref,
             cos_ref, sin_ref, wt_ref, gx_ref, gw_ref, wp):
        @pl.when(pl.program_id(0) == 0)
        def _():
            gw_ref[...] = jnp.zeros_like(gw_ref)
            for base in (W_Q, W_ZA):
                for nat, par in _pair_blocks(base):
                    wp[par, :] = wt_ref[nat, :]
            wp[W_KV:W_ZA, :] = wt_ref[W_KV:W_ZA, :]
            wp[W_U:, :] = wt_ref[W_U:, :]

        cos, sin = cos_ref[...], sin_ref[...]

        def unrope(t):
            return t * cos + _rotate_half_unsigned(t * sin)

        dq_rot = dq_ref[...]
        pieces = [unrope(dq_rot[:, 128 * j:128 * (j + 1)]) for j in range(4)]
        d_row = d_ref[...]
        pieces += [unrope(dk_ref[...]), dv_ref[...], dza_ref[...]]
        pieces += [duf_ref[k] + dub_ref[k] + d_row[:, k * SLAB_IN:(k + 1) * SLAB_IN] * dyl_ref[k] for k in range(N_SLAB)]
        pieces += [dzs_ref[...]]
        dproj = jnp.concatenate(pieces, axis=1).astype(BF16)
        gx_ref[...] = ALPHA * dpre_ref[...] + _dot(dproj, wp[...])
        xb = x_ref[...].astype(BF16)
        for base in (W_Q, W_ZA):
            for j in range(4):
                g_pair = _dot_tn(dproj[:, base + 128 * j:base + 128 * (j + 1)], xb)
                for g in range(2):
                    nat = base + HEAD_DIM * (4 * g + j)
                    gw_ref[nat:nat + HEAD_DIM, :] += g_pair[HEAD_DIM * g:HEAD_DIM * (g + 1), :]
        gw_ref[W_KV:W_ZA, :] += _dot_tn(dproj[:, W_KV:W_ZA], xb)
        gw_ref[W_U:, :] += _dot_tn(dproj[:, W_U:], xb)

    tok = lambda w: pl.BlockSpec((tb, w), lambda i: (i, 0))
    slab = pl.BlockSpec((N_SLAB, tb, SLAB_IN), lambda i: (0, i, 0))
    const = lambda r, c: pl.BlockSpec((r, c), lambda i: (0, 0))
    return pl.pallas_call(
        body, name="proj_bwd", grid=(seq // tb,),
        in_specs=[tok(1024), tok(512), tok(128), tok(128), tok(512), slab, slab, slab, tok(512), tok(1024),
                  const(1, 512), tok(128), tok(128), const(D_IN_PROJ, D_MODEL)],
        out_specs=[tok(1024), const(D_IN_PROJ, D_MODEL)],
        out_shape=[jax.ShapeDtypeStruct((seq, D_MODEL), F32), jax.ShapeDtypeStruct((D_IN_PROJ, D_MODEL), F32)],
        scratch_shapes=[pltpu.VMEM((D_IN_PROJ, D_MODEL), BF16)],
        compiler_params=_cparams(("arbitrary",)),
    )(x, dq, dk, dv, dza, du_f, du_b, dylin, dzs, dpre, ssm_d, cos128, sin128, wt)


def _adamw(w, g, m, v, name):
    rows, cols = w.shape
    tb = rows
    while tb * cols * 4 > ADAMW_BLOCK_BYTES and tb % 16 == 0:
        tb //= 2

    def body(w_ref, g_ref, m_ref, v_ref, d_ref, nm_ref, nv_ref):
        _adamw_update(w_ref, g_ref, m_ref, v_ref, d_ref, nm_ref, nv_ref)

    spec = pl.BlockSpec((tb, cols), lambda i: (i, 0))
    return pl.pallas_call(
        body, name=name, grid=(rows // tb,), in_specs=[spec] * 4, out_specs=[spec] * 3,
        out_shape=[jax.ShapeDtypeStruct((rows, cols), F32)] * 3,
        compiler_params=_cparams(("arbitrary",)),
    )(w, g, m, v)


def _adamw_update(w_ref, g_ref, m_ref, v_ref, d_ref, nm_ref, nv_ref):
    g_blk = g_ref[...]
    m_new = ADAM_B1 * m_ref[...] + (1.0 - ADAM_B1) * g_blk
    v_new = ADAM_B2 * v_ref[...] + (1.0 - ADAM_B2) * (g_blk * g_blk)
    m_hat = m_new / (1.0 - ADAM_B1 ** ADAM_STEP)
    v_hat = v_new / (1.0 - ADAM_B2 ** ADAM_STEP)
    d_ref[...] = -ADAM_LR * (m_hat / (jnp.sqrt(v_hat) + ADAM_EPS) + ADAM_WD * w_ref[...])
    nm_ref[...] = m_new
    nv_ref[...] = v_new


def _adamw_many(groups, name):
    n = len(groups)

    def body(*refs):
        for p in range(n):
            _adamw_update(*refs[4 * p:4 * p + 4], *refs[4 * n + 3 * p:4 * n + 3 * p + 3])

    return pl.pallas_call(
        body, name=name,
        out_shape=[jax.ShapeDtypeStruct(grp[0].shape, F32) for grp in groups for _ in range(3)],
    )(*[a for grp in groups for a in grp])


_WEIGHTS = ["w_in", "attn_sink", "ssm_a_re", "ssm_a_im", "ssm_log_dt", "ssm_b_re", "ssm_b_im", "ssm_c_re", "ssm_c_im",
            "ssm_d", "w_glu", "b_glu", "norm_attn_g", "norm_ssm_g", "w_out", "ln_g", "ln_b"]
N_DG = N_DIR * N_GROUPS
BIG_ROWS = N_DG * SSM_CH * SSM_STATE // 128
TINY_ROWS = 64


def _pack_small_grads(g_bc, g_vec, g_ar, g_ai, g_dt, g_sink, loss):
    big = jnp.stack([t.reshape(BIG_ROWS, 128) for t in g_bc])
    row = lambda t: jnp.pad(t.reshape(1, -1), ((0, 0), (0, 128 - t.size)))
    tiny = jnp.concatenate([g_vec.reshape(64, 128), g_ar.reshape(32, 128), g_ai.reshape(32, 128), row(g_dt), row(g_sink),
                            row(loss), jnp.zeros((N_CHIPS * TINY_ROWS - 131, 128), F32)], axis=0)
    return jnp.concatenate([big, tiny.reshape(N_CHIPS, TINY_ROWS, 128)], axis=1)


def _unpack_small_grads(packed):
    big = packed[:, :BIG_ROWS].reshape(N_CHIPS, 2 * BIG_ROWS, SSM_STATE)
    tiny = packed[:, BIG_ROWS:].reshape(N_CHIPS * TINY_ROWS, 128)
    g_vec = tiny[0:64].reshape(8, 1024)
    return tiny[130, 0], {
        "ssm_b_re": big[0], "ssm_b_im": big[1], "ssm_c_re": big[2], "ssm_c_im": big[3],
        "ln_g": g_vec[0:1], "ln_b": g_vec[1:2],
        "norm_attn_g": g_vec[2:3, :D_ATTN][:, _PAIR_INV], "norm_ssm_g": g_vec[2:3, D_ATTN:],
        "ssm_d": g_vec[3:4, :D_SSM], "b_glu": g_vec[3:4, D_SSM:],
        "ssm_a_re": tiny[64:96].reshape(N_DG, SSM_STATE), "ssm_a_im": tiny[96:128].reshape(N_DG, SSM_STATE),
        "ssm_log_dt": tiny[128:129, :N_DG].reshape(N_DIR, N_GROUPS), "attn_sink": tiny[129:130, :N_Q_HEADS],
    }


def _small_view(name, t):
    if name in ("ssm_b_re", "ssm_b_im"):
        return jnp.swapaxes(t[0], 2, 3).reshape(N_DG * SSM_CH, SSM_STATE)
    if name in ("ssm_c_re", "ssm_c_im"):
        return t.reshape(N_DG * SSM_CH, SSM_STATE)
    if name in ("ssm_a_re", "ssm_a_im"):
        return t.reshape(N_DG, SSM_STATE)
    if name == "ssm_log_dt":
        return t.reshape(N_DIR, N_GROUPS)
    return t.reshape(1, -1)


def _small_unview(name, t, shape):
    if name in ("ssm_b_re", "ssm_b_im"):
        return jnp.swapaxes(t.reshape(N_DIR, N_GROUPS, SSM_CH, SSM_STATE), 2, 3).reshape(shape)
    return t.reshape(shape)


def kernel(x, w_in, attn_sink, ssm_a_re, ssm_a_im, ssm_log_dt, ssm_b_re, ssm_b_im, ssm_c_re, ssm_c_im, ssm_d, w_glu, b_glu, norm_attn_g, norm_ssm_g, w_out, ln_g, ln_b, loss_target, m_w_in, m_attn_sink, m_ssm_a_re, m_ssm_a_im, m_ssm_log_dt, m_ssm_b_re, m_ssm_b_im, m_ssm_c_re, m_ssm_c_im, m_ssm_d, m_w_glu, m_b_glu, m_norm_attn_g, m_norm_ssm_g, m_w_out, m_ln_g, m_ln_b, v_w_in, v_attn_sink, v_ssm_a_re, v_ssm_a_im, v_ssm_log_dt, v_ssm_b_re, v_ssm_b_im, v_ssm_c_re, v_ssm_c_im, v_ssm_d, v_w_glu, v_b_glu, v_norm_attn_g, v_norm_ssm_g, v_w_out, v_ln_g, v_ln_b):
    args = dict(locals())
    weights = {n: args[n] for n in _WEIGHTS}
    mom_m = {n: args["m_" + n] for n in _WEIGHTS}
    mom_v = {n: args["v_" + n] for n in _WEIGHTS}
    xs = x[0]
    target = loss_target[0]

    wt_g, w_glu_g, w_out_g = _all_gather_chips([w_in[0].T, w_glu[0], w_out[0]], BF16, "gather_weights")
    wt_full = wt_g.reshape(D_IN_PROJ, D_MODEL)
    w_glu_full = w_glu_g.reshape(D_SSM, D_SSM)
    w_out_full = w_out_g.reshape(D_MODEL, D_MODEL)

    g_x, g_wt, g_w_out, g_w_glu, g_small = _local_step(
        xs, target, wt_full, w_glu_full, w_out_full, attn_sink, ssm_a_re, ssm_a_im, ssm_log_dt, ssm_b_re, ssm_b_im,
        ssm_c_re, ssm_c_im, ssm_d, b_glu, norm_attn_g, norm_ssm_g, ln_g, ln_b)

    r_wt, r_w_out, r_w_glu, g_small_all = _reduce_all(
        [g_wt.reshape(N_CHIPS, -1, D_MODEL), g_w_out.reshape(N_CHIPS, -1, D_MODEL), g_w_glu.reshape(N_CHIPS, -1, D_SSM),
         g_small], [True, True, True, False], "reduce_grads")
    loss, small_grads = _unpack_small_grads(g_small_all)

    grads, deltas, new_m, new_v = {}, {}, {}, {}
    d_w, m_w, v_w = _adamw(w_in[0].T, r_wt, m_w_in[0].T, v_w_in[0].T, "adamw_w_in")
    grads["w_in"], deltas["w_in"], new_m["w_in"], new_v["w_in"] = r_wt.T[None], d_w.T[None], m_w.T[None], v_w.T[None]
    for n, g in (("w_out", r_w_out), ("w_glu", r_w_glu)):
        d_w, m_w, v_w = _adamw(weights[n][0], g, mom_m[n][0], mom_v[n][0], "adamw_" + n)
        grads[n], deltas[n], new_m[n], new_v[n] = g[None], d_w[None], m_w[None], v_w[None]
    names = sorted(small_grads)
    updates = _adamw_many([(_small_view(n, weights[n]), small_grads[n], _small_view(n, mom_m[n]), _small_view(n, mom_v[n]))
                           for n in names], "adamw_small")
    for i, n in enumerate(names):
        shape = weights[n].shape
        grads[n] = _small_unview(n, small_grads[n], shape)
        deltas[n], new_m[n], new_v[n] = (_small_unview(n, t, shape) for t in updates[3 * i:3 * i + 3])

    return (loss, g_x[None], *[grads[n] for n in _WEIGHTS], *[deltas[n] for n in _WEIGHTS],
            *[new_m[n] for n in _WEIGHTS], *[new_v[n] for n in _WEIGHTS])


def _local_step(xs, target, wt_full, w_glu_full, w_out_full, attn_sink, ssm_a_re, ssm_a_im, ssm_log_dt, ssm_b_re,
                ssm_b_im, ssm_c_re, ssm_c_im, ssm_d, b_glu, norm_attn_g, norm_ssm_g, ln_g, ln_b):
    seq = xs.shape[0]

    a_r, a_i = _small_view("ssm_a_re", ssm_a_re), _small_view("ssm_a_im", ssm_a_im)
    log_dt = ssm_log_dt.reshape(N_DG, 1)
    b_r, b_i = _small_view("ssm_b_re", ssm_b_re), _small_view("ssm_b_im", ssm_b_im)
    c_r, c_i = _small_view("ssm_c_re", ssm_c_re), _small_view("ssm_c_im", ssm_c_im)
    ssm_tb = min(256, seq)
    sub_len = ssm_tb // SUBSEG
    lam, bb, bbt, cb, cb_t = _ssm_params_fwd(a_r, a_i, log_dt, b_r, b_i, c_r, c_i, int(math.log2(sub_len)))
    lam = lam.reshape(4, N_DIR, 1, STATE_W)

    cos128, sin128 = _rope_tables(seq)
    q_stack, k_rot, v_bf, z_attn, u, z_ssm = _proj(xs, wt_full, cos128, sin128, min(512, seq))
    sink128 = jnp.broadcast_to(attn_sink[0][:, None, None], (N_Q_HEADS, 1, 128))
    attn_bias = _attn_bias()
    o = _attn_fwd(q_stack, k_rot, v_bf, sink128, attn_bias)
    ys, starts = [], []
    for d in range(N_DIR):
        y_d, s_r, s_i = _ssm_fwd(u, lam, bb, cb, direction=d, tb=ssm_tb, name=f"ssm_fwd_{d}")
        ys.append(y_d)
        starts.append((s_r, s_i))

    row = lambda t: t.reshape(1, -1)
    g_attn_p = row(norm_attn_g)[:, _PAIR_PERM]
    loss_blk, d_o, d_za, d_ylin, d_zs, d_pre, g_w_out, g_w_glu, g_vec = _mid(
        o, z_attn, u, ys[0], ys[1], z_ssm, xs, target, row(ssm_d), w_glu_full, row(b_glu),
        g_attn_p, row(norm_ssm_g), w_out_full, row(ln_g), row(ln_b), min(256, seq))

    dq, dk, dv, g_sink = _attn_bwd(q_stack, k_rot, v_bf, sink128, attn_bias, d_o)
    dus, g_bb, g_cb, g_lam = [], [], [], []
    for d in range(N_DIR):
        du_d, gb_d, gc_d, dl_d = _ssm_bwd(u, d_ylin, starts[d], lam, bb, bbt, cb_t, direction=d, tb=ssm_tb,
                                          name=f"ssm_bwd_{d}")
        dus.append(du_d)
        g_bb.append(gb_d)
        g_cb.append(gc_d)
        g_lam.append(dl_d)
    g_ar, g_ai, g_dt, g_br, g_bi, g_cr, g_ci = _ssm_params_bwd(a_r, a_i, log_dt, b_r, b_i, g_bb, g_cb, g_lam)

    g_x, g_wt = _proj_bwd(xs, dq, dk, dv, d_za, dus[0], dus[1], d_ylin, d_zs, d_pre, row(ssm_d), cos128, sin128,
                          wt_full, min(256, seq))

    g_small = _pack_small_grads([g_br, g_bi, g_cr, g_ci], g_vec, g_ar, g_ai, g_dt, g_sink[:, 0], loss_blk[0, 0])
    return g_x, g_wt, g_w_out, g_w_glu, g_small
```

```python
import functools
import math

import numpy as np
import jax
import jax.numpy as jnp
from jax import lax
from jax.experimental import pallas as pl
from jax.experimental.pallas import tpu as pltpu

F32 = jnp.float32
BF16 = jnp.bfloat16
MESH = pl.DeviceIdType.MESH

D_MODEL = 1024
D_ATTN = 512
D_SSM = 512
HEAD_DIM = 64
N_Q_HEADS = 8
WINDOW = 128
ROPE_THETA = 10000.0
SSM_CH = 16
N_GROUPS = 32
SSM_STATE = 64
N_DIR = 2
STATE_W = N_GROUPS * SSM_STATE
N_SLAB = 4
SLAB_IN = 128
SLAB_ST = 512
NORM_EPS = 1e-5
NEG_INF = -1e30
ALPHA = 2.0 ** 0.25
D_IN_PROJ = 2304
N_CHIPS = 4

ADAM_LR = 0.001
ADAM_B1 = 0.9
ADAM_B2 = 0.999
ADAM_EPS = 1e-08
ADAM_WD = 0.01
ADAM_STEP = 10

SUBSEG = 8
SCAN_LANES = 512
SSM_BLOCK = 512
VMEM_LIMIT = 48 * 1024 * 1024
ADAMW_BLOCK_BYTES = 3 * 512 * 1024

def _to_pair_order(row):
    return jnp.transpose(row.reshape(2, 4, HEAD_DIM), (1, 0, 2)).reshape(1, D_ATTN)


def _from_pair_order(row):
    return jnp.transpose(row.reshape(4, 2, HEAD_DIM), (1, 0, 2)).reshape(1, D_ATTN)


def _cparams(sem=None):
    return pltpu.CompilerParams(dimension_semantics=sem, vmem_limit_bytes=VMEM_LIMIT)


def _dot(a, b):
    return jnp.dot(a, b, preferred_element_type=F32)


def _dot_nt(a, b):
    return lax.dot_general(a, b, (((1,), (1,)), ((), ())), preferred_element_type=F32)


def _dot_tn(a, b):
    return lax.dot_general(a, b, (((0,), (0,)), ((), ())), preferred_element_type=F32)


def _sigmoid(z):
    return 1.0 / (1.0 + jnp.exp(-z))


def _all_gather_chips(shards, out_dtype, name):
    n = len(shards)

    def body(*refs):
        in_refs, out_refs = refs[:n], refs[n:2 * n]
        send_sems, recv_sems = refs[2 * n:]
        x, y, c = lax.axis_index("x"), lax.axis_index("y"), lax.axis_index("c")
        sibling = (x, y, 1 - c)
        chips = [(1 - x, y), (x, 1 - y), (1 - x, 1 - y)]

        for a in range(n):
            out_refs[a][2 * x + y] = in_refs[a][...].astype(out_dtype)

        def half_of(a, px, py, half):
            rows = in_refs[a].shape[0] // 2
            return out_refs[a].at[2 * px + py, pl.ds(half * rows, rows), :]

        def copy(a, k, px, py, half, to):
            blk = half_of(a, px, py, half)
            return pltpu.make_async_remote_copy(src_ref=blk, dst_ref=blk, send_sem=send_sems.at[6 * a + k],
                                                recv_sem=recv_sems.at[6 * a + k], device_id=to, device_id_type=MESH)

        first = [copy(a, j, x, y, c, (*chips[j], c)) for a in range(n) for j in range(3)]
        for cp in first:
            cp.start()
        passed = []
        for a in range(n):
            for j in range(3):
                copy(a, j, *chips[j], c, (x, y, c)).wait_recv()
                fwd = copy(a, 3 + j, *chips[j], c, sibling)
                fwd.start()
                passed.append(fwd)
        for a in range(n):
            for j in range(3):
                copy(a, 3 + j, *chips[j], 1 - c, (x, y, c)).wait_recv()
        for cp in first + passed:
            cp.wait_send()

    vmem = pl.BlockSpec(memory_space=pltpu.VMEM)
    return pl.pallas_call(
        body, name=name,
        out_shape=[jax.ShapeDtypeStruct((N_CHIPS,) + s.shape, out_dtype) for s in shards],
        in_specs=[vmem] * n, out_specs=[vmem] * n,
        scratch_shapes=[pltpu.SemaphoreType.DMA((6 * n,)), pltpu.SemaphoreType.DMA((6 * n,))],
        compiler_params=pltpu.CompilerParams(vmem_limit_bytes=VMEM_LIMIT),
    )(*shards)


SEMS_PER_ARRAY = 11


def _reduce_all(pieces, narrow, name):
    n = len(pieces)
    halves = [p.shape[1] // 2 for p in pieces]
    wire = [BF16 if nar else F32 for nar in narrow]

    def body(*refs):
        p_refs, out_refs = refs[:n], refs[n:2 * n]
        a_refs, s_refs, b_refs = refs[2 * n:3 * n], refs[3 * n:4 * n], refs[4 * n:5 * n]
        send_sems, recv_sems = refs[5 * n:]
        x, y, c = lax.axis_index("x"), lax.axis_index("y"), lax.axis_index("c")
        me = 2 * x + y
        sibling = (x, y, 1 - c)
        chips = [(1 - x, y), (x, 1 - y), (1 - x, 1 - y)]
        slot = [2 * px + py for px, py in chips]

        def copy(a, k, src, dst, to):
            return pltpu.make_async_remote_copy(src_ref=src, dst_ref=dst, send_sem=send_sems.at[SEMS_PER_ARRAY * a + k],
                                                recv_sem=recv_sems.at[SEMS_PER_ARRAY * a + k],
                                                device_id=to, device_id_type=MESH)

        def rows(a, half):
            return pl.ds(pl.multiple_of(half * halves[a], 16), halves[a])

        started = []
        for a in range(n):
            cp = copy(a, 0, p_refs[a].at[:, rows(a, 1 - c), :], a_refs[a], sibling)
            cp.start()
            started.append(cp)
        for a in range(n):
            started[a].wait_recv()
            for k in range(N_CHIPS):
                acc = a_refs[a][k] + p_refs[a][k, rows(a, c), :]
                a_refs[a][k] = acc
                s_refs[a][k] = acc.astype(wire[a])
            b_refs[a][me] = s_refs[a][me]
            for j in range(3):
                cp = copy(a, 1 + j, s_refs[a].at[slot[j]], b_refs[a].at[me], (*chips[j], c))
                cp.start()
                started.append(cp)
        for a in range(n):
            for j in range(3):
                copy(a, 1 + j, s_refs[a].at[slot[j]], b_refs[a].at[slot[j]], (x, y, c)).wait_recv()
            terms = [jnp.where(me == k, a_refs[a][k], b_refs[a][k].astype(F32)) for k in range(N_CHIPS)]
            total = (terms[0] + terms[1]) + (terms[2] + terms[3])
            if a < n - 1:
                done = out_refs[a].at[rows(a, c), :]
                out_refs[a][rows(a, c), :] = total
            else:
                done = out_refs[a].at[me, rows(a, c), :]
                out_refs[a][me, rows(a, c), :] = total
            cp = copy(a, 4, done, done, sibling)
            cp.start()
            started.append(cp)
        last = n - 1
        piece = lambda k, half: out_refs[last].at[k, rows(last, half), :]
        for j in range(3):
            cp = copy(last, 5 + j, piece(me, c), piece(me, c), (*chips[j], c))
            cp.start()
            started.append(cp)
        for j in range(3):
            copy(last, 5 + j, piece(slot[j], c), piece(slot[j], c), (x, y, c)).wait_recv()
            cp = copy(last, 8 + j, piece(slot[j], c), piece(slot[j], c), sibling)
            cp.start()
            started.append(cp)
        for a in range(n - 1):
            copy(a, 4, out_refs[a].at[rows(a, 1 - c), :], out_refs[a].at[rows(a, 1 - c), :], (x, y, c)).wait_recv()
        copy(last, 4, piece(me, 1 - c), piece(me, 1 - c), (x, y, c)).wait_recv()
        for j in range(3):
            copy(last, 8 + j, piece(slot[j], 1 - c), piece(slot[j], 1 - c), (x, y, c)).wait_recv()
        for cp in started:
            cp.wait_send()

    vmem = pl.BlockSpec(memory_space=pltpu.VMEM)
    half_shape = lambda a: (N_CHIPS, halves[a], pieces[a].shape[2])
    return pl.pallas_call(
        body, name=name,
        out_shape=[jax.ShapeDtypeStruct(p.shape[1:] if a < n - 1 else p.shape, F32) for a, p in enumerate(pieces)],
        in_specs=[vmem] * n, out_specs=[vmem] * n,
        scratch_shapes=[pltpu.VMEM(half_shape(a), F32) for a in range(n)]
        + [pltpu.VMEM(half_shape(a), wire[a]) for a in range(n)]
        + [pltpu.VMEM(half_shape(a), wire[a]) for a in range(n)]
        + [pltpu.SemaphoreType.DMA((SEMS_PER_ARRAY * n,)), pltpu.SemaphoreType.DMA((SEMS_PER_ARRAY * n,))],
        compiler_params=pltpu.CompilerParams(vmem_limit_bytes=VMEM_LIMIT),
    )(*pieces)


def _ssm_param_values(ar, ai, logdt):
    dt = jnp.exp(logdt)
    mag = jnp.exp(dt * ar)
    cs, sn = jnp.cos(dt * ai), jnp.sin(dt * ai)
    lr, li = mag * cs, mag * sn
    den = ar * ar + ai * ai
    nr = (lr - 1.0) * ar + li * ai
    ni = li * ar - (lr - 1.0) * ai
    return dt, mag, lr, li, den, nr, ni


GROUPS_PER_SLAB = N_GROUPS // N_SLAB


def _slab_masks():
    def eq(shape, f_row, f_col):
        return (f_row(lax.broadcasted_iota(jnp.int32, shape, 0)) == f_col(lax.broadcasted_iota(jnp.int32, shape, 1))).astype(F32)
    spread = eq((SSM_STATE, SLAB_ST), lambda r: r, lambda c: c % SSM_STATE)
    spread_t = eq((SLAB_ST, SSM_STATE), lambda r: r % SSM_STATE, lambda c: c)
    keep = eq((SLAB_IN, SLAB_ST), lambda r: r // SSM_CH, lambda c: c // SSM_STATE)
    keep_t = eq((SLAB_ST, SLAB_IN), lambda r: r // SSM_STATE, lambda c: c // SSM_CH)
    repeat = eq((N_DG * SSM_CH, N_DG), lambda r: r // SSM_CH, lambda c: c)
    return spread, spread_t, keep, keep_t, repeat


def _split3(t):
    hi = t.astype(BF16)
    rest = t - hi.astype(F32)
    mid = rest.astype(BF16)
    return hi, mid, (rest - mid.astype(F32)).astype(BF16)


def _select(dot, ones01, t, ones_first):
    o = ones01.astype(BF16)
    parts = [dot(o, p) if ones_first else dot(p, o) for p in _split3(t)]
    return (parts[0] + parts[1]) + parts[2]


def _ssm_params_fwd(ar, ai, logdt, br, bi, cr, ci, n_square):
    def body(ar_ref, ai_ref, dt_ref, br_ref, bi_ref, cr_ref, ci_ref, lam_ref, bb_ref, bbt_ref, cb_ref, cbt_ref):
        _, _, lr, li, den, nr, ni = _ssm_param_values(ar_ref[...], ai_ref[...], dt_ref[...])
        lam_ref[0] = lr
        lam_ref[1] = li
        pr, pi = lr, li
        for _ in range(n_square):
            pr, pi = pr * pr - pi * pi, 2.0 * pr * pi
        lam_ref[2] = pr
        lam_ref[3] = pi
        spread, spread_t, keep, keep_t, repeat = _slab_masks()
        fr = _select(_dot, repeat, nr / den, True)
        fi = _select(_dot, repeat, ni / den, True)
        b_r, b_i = br_ref[...], bi_ref[...]
        bbar = (fr * b_r - fi * b_i, fr * b_i + fi * b_r)
        c_par = (cr_ref[...], ci_ref[...])
        spread, spread_t = spread.astype(BF16), spread_t.astype(BF16)
        for src, wide_ref, tall_ref in ((bbar, bb_ref, bbt_ref), (c_par, cbt_ref, cb_ref)):
            for q in range(2):
                for d in range(N_DIR):
                    for k in range(N_SLAB):
                        r0 = (d * N_GROUPS + k * GROUPS_PER_SLAB) * SSM_CH
                        blk = src[q][r0:r0 + SLAB_IN].astype(BF16)
                        wide_ref[q, d, k] = (_dot(blk, spread) * keep).astype(BF16)
                        tall_ref[q, d, k] = (_dot_nt(spread_t, blk) * keep_t).astype(BF16)

    wide = jax.ShapeDtypeStruct((2, N_DIR, N_SLAB, SLAB_IN, SLAB_ST), BF16)
    tall = jax.ShapeDtypeStruct((2, N_DIR, N_SLAB, SLAB_ST, SLAB_IN), BF16)
    return pl.pallas_call(body, name="ssm_params_fwd",
                          out_shape=[jax.ShapeDtypeStruct((4,) + ar.shape, F32), wide, tall, tall, wide],
                          compiler_params=pltpu.CompilerParams(vmem_limit_bytes=VMEM_LIMIT),
                          )(ar, ai, logdt, br, bi, cr, ci)


def _ssm_params_bwd(ar, ai, logdt, br, bi, g_slabs_b, g_slabs_c, g_lam):
    def body(ar_ref, ai_ref, dt_ref, br_ref, bi_ref, gb0_ref, gb1_ref, gc0_ref, gc1_ref, gl0_ref, gl1_ref,
             gar_ref, gai_ref, gdt_ref, gbr_ref, gbi_ref, gcr_ref, gci_ref, dbb, dlam):
        spread, spread_t, keep, keep_t, repeat = _slab_masks()
        for d, (gb_ref, gc_ref) in enumerate(((gb0_ref, gc0_ref), (gb1_ref, gc1_ref))):
            for q in range(2):
                for k in range(N_SLAB):
                    r0 = (d * N_GROUPS + k * GROUPS_PER_SLAB) * SSM_CH
                    dbb[q, r0:r0 + SLAB_IN, :] = _select(_dot, spread_t, gb_ref[q, k] * keep, False)
                    out_ref = gcr_ref if q == 0 else gci_ref
                    out_ref[r0:r0 + SLAB_IN, :] = _select(_dot_tn, spread_t, gc_ref[q, k] * keep_t, False)
        grp = (lax.broadcasted_iota(jnp.int32, (N_GROUPS, STATE_W), 0)
               == lax.broadcasted_iota(jnp.int32, (N_GROUPS, STATE_W), 1) // SSM_STATE).astype(F32)
        pick = (lax.broadcasted_iota(jnp.int32, (STATE_W, SSM_STATE), 0) % SSM_STATE
                == lax.broadcasted_iota(jnp.int32, (STATE_W, SSM_STATE), 1)).astype(F32)
        for d, gl_ref in enumerate((gl0_ref, gl1_ref)):
            for q in range(2):
                row = jnp.sum(gl_ref[q], axis=0, keepdims=True)
                dlam[q, d * N_GROUPS:(d + 1) * N_GROUPS, :] = _select(_dot, pick, grp * row, False)

        a_r, a_i = ar_ref[...], ai_ref[...]
        dt, mag, lr, li, den, nr, ni = _ssm_param_values(a_r, a_i, dt_ref[...])
        fr = _select(_dot, repeat, nr / den, True)
        fi = _select(_dot, repeat, ni / den, True)
        b_r, b_i = br_ref[...], bi_ref[...]
        g_r, g_i = dbb[0], dbb[1]
        gbr_ref[...] = fr * g_r + fi * g_i
        gbi_ref[...] = fr * g_i - fi * g_r
        d_fr = _select(_dot_tn, repeat, b_r * g_r + b_i * g_i, True)
        d_fi = _select(_dot_tn, repeat, b_r * g_i - b_i * g_r, True)
        d_nr, d_ni = d_fr / den, d_fi / den
        d_den = -(d_fr * nr + d_fi * ni) / (den * den)
        d_lr = dlam[0] + d_nr * a_r - d_ni * a_i
        d_li = dlam[1] + d_nr * a_i + d_ni * a_r
        d_ar = d_nr * (lr - 1.0) + d_ni * li + d_den * 2.0 * a_r
        d_ai = d_nr * li - d_ni * (lr - 1.0) + d_den * 2.0 * a_i
        d_mag = (d_lr * lr + d_li * li) / mag
        d_theta = d_li * lr - d_lr * li
        gar_ref[...] = d_ar + d_mag * mag * dt
        gai_ref[...] = d_ai + d_theta * dt
        d_dt = d_mag * mag * a_r + d_theta * a_i
        gdt_ref[...] = jnp.sum(d_dt, axis=1, keepdims=True) * dt

    small = jax.ShapeDtypeStruct(ar.shape, F32)
    big = jax.ShapeDtypeStruct(br.shape, F32)
    return pl.pallas_call(
        body, name="ssm_params_bwd",
        out_shape=[small, small, jax.ShapeDtypeStruct(logdt.shape, F32), big, big, big, big],
        scratch_shapes=[pltpu.VMEM((2,) + br.shape, F32), pltpu.VMEM((2,) + ar.shape, F32)],
        compiler_params=pltpu.CompilerParams(vmem_limit_bytes=VMEM_LIMIT),
    )(ar, ai, logdt, br, bi, *g_slabs_b, *g_slabs_c, *g_lam)


def _rope_tables(seq):
    half = HEAD_DIM // 2
    inv_freq = ROPE_THETA ** (-jnp.arange(half, dtype=F32) / half)
    ang = jnp.arange(seq, dtype=jnp.int32).astype(F32)[:, None] * inv_freq[None, :]
    cos, sin = jnp.cos(ang), jnp.sin(ang)
    cos128 = jnp.concatenate([cos, cos, cos, cos], axis=1)
    sin128 = jnp.concatenate([-sin, sin, -sin, sin], axis=1)
    return cos128, sin128


def _rotate_half_unsigned(t):
    lane = lax.broadcasted_iota(jnp.int32, t.shape, 1)
    return jnp.where((lane % HEAD_DIM) < HEAD_DIM // 2, pltpu.roll(t, 96, 1), pltpu.roll(t, 32, 1))


def _rope(t, cos, sin_signed):
    return t * cos + _rotate_half_unsigned(t) * sin_signed


def _pair_blocks(base):
    out = []
    for j in range(4):
        for g in range(2):
            nat = base + HEAD_DIM * (4 * g + j)
            par = base + 128 * j + HEAD_DIM * g
            out.append((slice(nat, nat + HEAD_DIM), slice(par, par + HEAD_DIM)))
    return out


W_Q, W_KV, W_ZA, W_U, W_ZS = 0, 512, 768, 1280, 1792


def _proj(x, wt, cos128, sin128, tb):
    seq = x.shape[0]

    def body(x_ref, wt_ref, cos_ref, sin_ref, q_ref, k_ref, v_ref, za_ref, u_ref, zs_ref, wp):
        @pl.when(pl.program_id(0) == 0)
        def _():
            for dst_base, src_base in ((0, W_Q), (512, W_ZA)):
                for nat, par in _pair_blocks(0):
                    wp[dst_base + par.start:dst_base + par.stop, :] = wt_ref[src_base + nat.start:src_base + nat.stop, :]

        xb = x_ref[...].astype(BF16)
        cos, sin = cos_ref[...], sin_ref[...]
        lo = lax.broadcasted_iota(jnp.int32, (tb, 128), 1) < HEAD_DIM
        q = _dot_nt(xb, wp[0:512, :])
        for j in range(4):
            qj = _rope(q[:, 128 * j:128 * (j + 1)], cos, sin)
            q_ref[j] = jnp.where(lo, qj, 0.0).astype(BF16)
            q_ref[4 + j] = jnp.where(lo, 0.0, qj).astype(BF16)
        kv = _dot_nt(xb, wt_ref[W_KV:W_ZA, :])
        k_ref[...] = _rope(kv[:, 0:128], cos, sin).astype(BF16)
        v_ref[...] = kv[:, 128:256].astype(BF16)
        za_ref[...] = _dot_nt(xb, wp[512:1024, :])
        u_val = _dot_nt(xb, wt_ref[W_U:W_ZS, :])
        for k in range(N_SLAB):
            u_ref[k] = u_val[:, k * SLAB_IN:(k + 1) * SLAB_IN]
        zs_ref[...] = _dot_nt(xb, wt_ref[W_ZS:D_IN_PROJ, :])

    row = lambda w: pl.BlockSpec((tb, w), lambda i: (i, 0))
    return pl.pallas_call(
        body, name="proj", grid=(seq // tb,),
        in_specs=[row(D_MODEL), pl.BlockSpec((D_IN_PROJ, D_MODEL), lambda i: (0, 0)), row(128), row(128)],
        out_specs=[pl.BlockSpec((8, tb, 128), lambda i: (0, i, 0)), row(128), row(128), row(512),
                   pl.BlockSpec((N_SLAB, tb, SLAB_IN), lambda i: (0, i, 0)), row(512)],
        out_shape=[jax.ShapeDtypeStruct((8, seq, 128), BF16), jax.ShapeDtypeStruct((seq, 128), BF16),
                   jax.ShapeDtypeStruct((seq, 128), BF16), jax.ShapeDtypeStruct((seq, 512), F32),
                   jax.ShapeDtypeStruct((N_SLAB, seq, SLAB_IN), F32), jax.ShapeDtypeStruct((seq, 512), F32)],
        scratch_shapes=[pltpu.VMEM((1024, D_MODEL), BF16)],
        compiler_params=_cparams(("arbitrary",)),
    )(x, wt, cos128, sin128)


ATT_TQ = 128
ATT_KEYS = 3 * ATT_TQ


def _attn_window(i, seq):
    start = jnp.clip(i * ATT_TQ - WINDOW, 0, seq - ATT_KEYS)
    return pl.multiple_of(start, ATT_TQ)


def _attn_bias():
    r = np.arange(ATT_TQ)[None, :, None]
    c = np.arange(ATT_KEYS)[None, None, :]
    off = (np.arange(3) * ATT_TQ)[:, None, None]
    return jnp.asarray(np.where(np.abs(r + off - c) <= WINDOW, 0.0, NEG_INF).astype(np.float32))


def _attn_bias_spec(nblk):
    pick = lambda i: jnp.where(i == 0, 0, jnp.where(i == nblk - 1, 2, 1))
    return pl.BlockSpec((None, ATT_TQ, ATT_KEYS), lambda i: (pick(i), 0, 0))


def _attn_softmax(q_ref, k_ref, v_ref, sink_ref, bias_ref, start):
    kw = k_ref[pl.ds(start, ATT_KEYS), :]
    vw = v_ref[pl.ds(start, ATT_KEYS), :]
    qall = q_ref[...].reshape(N_Q_HEADS * ATT_TQ, 128)
    s = (_dot_nt(qall, kw) * (HEAD_DIM ** -0.5)).reshape(N_Q_HEADS, ATT_TQ, ATT_KEYS) + bias_ref[...][None]
    tiles = [s[:, :, 128 * t:128 * (t + 1)] for t in range(ATT_KEYS // 128)]
    m = jnp.max(jnp.maximum(jnp.maximum(tiles[0], tiles[1]), tiles[2]), axis=2, keepdims=True)
    sink = sink_ref[...]
    m_b = jnp.maximum(jnp.broadcast_to(m, (N_Q_HEADS, ATT_TQ, 128)), sink)
    p = jnp.concatenate([jnp.exp(t - m_b) for t in tiles], axis=2)
    p_sink = jnp.exp(sink - m_b)
    lo_k = lax.broadcasted_iota(jnp.int32, (ATT_KEYS, 128), 1) < HEAD_DIM
    v_f = vw.astype(F32)
    v_lo, v_hi = jnp.where(lo_k, v_f, 1.0).astype(BF16), jnp.where(lo_k, 1.0, v_f).astype(BF16)
    pb = p.astype(BF16).reshape(N_Q_HEADS * ATT_TQ, ATT_KEYS)
    half = 4 * ATT_TQ
    r = jnp.concatenate([_dot(pb[:half], v_lo), _dot(pb[half:], v_hi)], axis=0).reshape(N_Q_HEADS, ATT_TQ, 128)
    return kw, vw, qall, p, p_sink, r


def _attn_fwd(q_stack, k, v, sink128, bias):
    seq = k.shape[0]

    def body(q_ref, k_ref, v_ref, sink_ref, bias_ref, o_ref):
        start = _attn_window(pl.program_id(0), seq)
        _, _, _, _, p_sink, r = _attn_softmax(q_ref, k_ref, v_ref, sink_ref, bias_ref, start)
        out = r / (pltpu.roll(r, HEAD_DIM, 2) + p_sink)
        lo = lax.broadcasted_iota(jnp.int32, (ATT_TQ, 128), 1) < HEAD_DIM
        for j in range(4):
            o_ref[:, 128 * j:128 * (j + 1)] = jnp.where(lo, out[j], out[4 + j])

    full = lambda w: pl.BlockSpec((seq, w), lambda i: (0, 0))
    return pl.pallas_call(
        body, name="attn_fwd", grid=(seq // ATT_TQ,),
        in_specs=[pl.BlockSpec((8, ATT_TQ, 128), lambda i: (0, i, 0)), full(128), full(128),
                  pl.BlockSpec((N_Q_HEADS, 1, 128), lambda i: (0, 0, 0)), _attn_bias_spec(seq // ATT_TQ)],
        out_specs=pl.BlockSpec((ATT_TQ, 512), lambda i: (i, 0)),
        out_shape=jax.ShapeDtypeStruct((seq, 512), F32),
        compiler_params=_cparams(("arbitrary",)),
    )(q_stack, k, v, sink128, bias)


def _attn_bwd(q_stack, k, v, sink128, bias, d_o):
    seq = k.shape[0]

    def body(q_ref, k_ref, v_ref, sink_ref, bias_ref, do_ref, dq_ref, dk_ref, dv_ref, dsink_ref, sink_acc):
        i = pl.program_id(0)

        @pl.when(i == 0)
        def _():
            dk_ref[...] = jnp.zeros_like(dk_ref)
            dv_ref[...] = jnp.zeros_like(dv_ref)
            sink_acc[...] = jnp.zeros_like(sink_acc)

        start = _attn_window(i, seq)
        kw, vw, qall, p, p_sink, r = _attn_softmax(q_ref, k_ref, v_ref, sink_ref, bias_ref, start)
        lo = lax.broadcasted_iota(jnp.int32, (ATT_TQ, 128), 1) < HEAD_DIM
        lo3 = lo[None]
        grp0 = lax.broadcasted_iota(jnp.int32, (N_Q_HEADS, ATT_TQ, 128), 0) < 4
        val = grp0 == lo3
        swapped = pltpu.roll(r, HEAD_DIM, 2)
        inv = 1.0 / (jnp.where(val, swapped, r) + p_sink)
        d_o_blk = do_ref[...]
        do3 = jnp.where(val, jnp.concatenate([d_o_blk[None, :, 128 * j:128 * (j + 1)] for j in range(4)] * 2, axis=0), 0.0)
        t = (do3 * r).reshape(N_Q_HEADS * ATT_TQ, 128)
        t_hi = t.astype(BF16)
        t_lo = (t - t_hi.astype(F32)).astype(BF16)
        ones = jnp.ones((128, 128), BF16)
        delta = (_dot(t_hi, ones) + _dot(t_lo, ones)).reshape(N_Q_HEADS, ATT_TQ, 128) * inv
        sink_acc[...] += -(p_sink * inv) * delta
        do_all = do3.astype(BF16).reshape(N_Q_HEADS * ATT_TQ, 128)
        dp = _dot_nt(do_all, vw).reshape(N_Q_HEADS, ATT_TQ, ATT_KEYS)
        probs, ds = [], []
        for tl in range(ATT_KEYS // 128):
            cols = slice(128 * tl, 128 * (tl + 1))
            probs_t = p[:, :, cols] * inv
            probs.append(probs_t.astype(BF16))
            ds.append((probs_t * (dp[:, :, cols] - delta)).astype(BF16))
        probs_all = jnp.concatenate(probs, axis=2).reshape(N_Q_HEADS * ATT_TQ, ATT_KEYS)
        ds_all = jnp.concatenate(ds, axis=2).reshape(N_Q_HEADS * ATT_TQ, ATT_KEYS)
        scale = HEAD_DIM ** -0.5
        dq_all = (_dot(ds_all, kw) * scale).reshape(N_Q_HEADS, ATT_TQ, 128)
        for j in range(4):
            dq_ref[:, 128 * j:128 * (j + 1)] = jnp.where(lo, dq_all[j], dq_all[4 + j])
        dk_ref[pl.ds(start, ATT_KEYS), :] += _dot_tn(ds_all, qall) * scale
        dv_ref[pl.ds(start, ATT_KEYS), :] += _dot_tn(probs_all, do_all)

        @pl.when(i == pl.num_programs(0) - 1)
        def _():
            dsink_ref[...] = jnp.sum(sink_acc[...], axis=1)

    full = lambda w: pl.BlockSpec((seq, w), lambda i: (0, 0))
    return pl.pallas_call(
        body, name="attn_bwd", grid=(seq // ATT_TQ,),
        in_specs=[pl.BlockSpec((8, ATT_TQ, 128), lambda i: (0, i, 0)), full(128), full(128),
                  pl.BlockSpec((N_Q_HEADS, 1, 128), lambda i: (0, 0, 0)),
                  _attn_bias_spec(seq // ATT_TQ), pl.BlockSpec((ATT_TQ, 512), lambda i: (i, 0))],
        out_specs=[pl.BlockSpec((ATT_TQ, 512), lambda i: (i, 0)), full(128), full(128),
                   pl.BlockSpec((N_Q_HEADS, 128), lambda i: (0, 0))],
        out_shape=[jax.ShapeDtypeStruct((seq, 512), F32), jax.ShapeDtypeStruct((seq, 128), F32),
                   jax.ShapeDtypeStruct((seq, 128), F32), jax.ShapeDtypeStruct((N_Q_HEADS, 128), F32)],
        scratch_shapes=[pltpu.VMEM((N_Q_HEADS, ATT_TQ, 128), F32)],
        compiler_params=_cparams(("arbitrary",)),
    )(q_stack, k, v, sink128, bias, d_o)


def _permute_rows(dst_ref, src_ref, sub_len):
    for k in range(N_SLAB):
        for j in range(sub_len):
            dst_ref[k, 8 * j:8 * (j + 1), :] = src_ref.at[k][pl.ds(j, SUBSEG, stride=sub_len), :]


def _unpermute_rows(dst_ref, src_ref, sub_len):
    for k in range(N_SLAB):
        for s in range(SUBSEG):
            dst_ref[k, s * sub_len:(s + 1) * sub_len, :] = src_ref.at[k][pl.ds(s, sub_len, stride=SUBSEG), :]


def _scan_chunk(br_ref, bi_ref, lr_row, li_row, init, cols, *, sub_len, reverse, store):
    lr = jnp.broadcast_to(lr_row[:, cols], (SUBSEG, SCAN_LANES))
    li = jnp.broadcast_to(li_row[:, cols], (SUBSEG, SCAN_LANES))
    if init is None:
        sr = si = jnp.zeros((SUBSEG, SCAN_LANES), F32)
    else:
        sr, si = init
    for jj in range(sub_len):
        rows = slice(SUBSEG * ((sub_len - 1 - jj) if reverse else jj), SUBSEG * (((sub_len - 1 - jj) if reverse else jj) + 1))
        sr, si = lr * sr - li * si + br_ref[rows, cols], lr * si + li * sr + bi_ref[rows, cols]
        if store:
            br_ref[rows, cols] = sr
            bi_ref[rows, cols] = si
    return sr, si


def _resolve_chunk(z, carry_refs, start_refs, pr_row, pi_row, cols, *, reverse):
    cr, ci = carry_refs[0][0:1, cols], carry_refs[1][0:1, cols]
    pr, pi = pr_row[:, cols], pi_row[:, cols]
    for s in (range(SUBSEG - 1, -1, -1) if reverse else range(SUBSEG)):
        start_refs[0][s:s + 1, cols] = cr
        start_refs[1][s:s + 1, cols] = ci
        cr, ci = pr * cr - pi * ci + z[0][s:s + 1, :], pr * ci + pi * cr + z[1][s:s + 1, :]
    carry_refs[0][0:1, cols] = cr
    carry_refs[1][0:1, cols] = ci


def _param_specs(direction):
    row = lambda q: pl.BlockSpec((None, None, 1, STATE_W), lambda i: (q, direction, 0, 0))
    wide = lambda q: pl.BlockSpec((None, None, N_SLAB, SLAB_IN, SLAB_ST), lambda i: (q, direction, 0, 0, 0))
    tall = lambda q: pl.BlockSpec((None, None, N_SLAB, SLAB_ST, SLAB_IN), lambda i: (q, direction, 0, 0, 0))
    return [row(q) for q in range(4)], [wide(0), wide(1)], [tall(0), tall(1)]


def _ssm_fwd(u, lam, bb, cb, *, direction, tb, name):
    reverse = direction == 1
    seq = u.shape[1]
    nblk = seq // tb
    sub_len = tb // SUBSEG

    def body(u_ref, lr_ref, li_ref, pr_ref, pi_ref, bbr_ref, bbi_ref, cbr_ref, cbi_ref,
             y_ref, sr_ref, si_ref, xr, xi, up, yp, car, cai):
        @pl.when(pl.program_id(0) == 0)
        def _():
            car[...] = jnp.zeros_like(car)
            cai[...] = jnp.zeros_like(cai)

        _permute_rows(up, u_ref, sub_len)
        lr, li, pr, pi = lr_ref[...], li_ref[...], pr_ref[...], pi_ref[...]
        chunk = lambda k: slice(k * SLAB_ST, (k + 1) * SLAB_ST)

        def drive(k):
            ub = up[k].astype(BF16)
            xr[:, chunk(k)] = _dot(ub, bbr_ref[k])
            xi[:, chunk(k)] = _dot(ub, bbi_ref[k])

        def scan(k):
            z = _scan_chunk(xr, xi, lr, li, None, chunk(k), sub_len=sub_len, reverse=reverse, store=False)
            _resolve_chunk(z, (car, cai), (sr_ref, si_ref), pr, pi, chunk(k), reverse=reverse)
            _scan_chunk(xr, xi, lr, li, (sr_ref[:, chunk(k)], si_ref[:, chunk(k)]), chunk(k),
                        sub_len=sub_len, reverse=reverse, store=True)

        def read_out(k):
            yp[k] = _dot(xr[:, chunk(k)].astype(BF16), cbr_ref[k]) - _dot(xi[:, chunk(k)].astype(BF16), cbi_ref[k])

        drive(0)
        for k in range(N_SLAB):
            if k + 1 < N_SLAB:
                drive(k + 1)
            scan(k)
            if k > 0:
                read_out(k - 1)
        read_out(N_SLAB - 1)
        _unpermute_rows(y_ref, yp, sub_len)

    blk = (lambda i: nblk - 1 - i) if reverse else (lambda i: i)
    rows, wide, tall = _param_specs(direction)
    tok = pl.BlockSpec((N_SLAB, tb, SLAB_IN), lambda i: (0, blk(i), 0))
    start_spec = pl.BlockSpec((None, SUBSEG, STATE_W), lambda i: (blk(i), 0, 0))
    return pl.pallas_call(
        body, name=name, grid=(nblk,),
        in_specs=[tok] + rows + wide + tall,
        out_specs=[tok, start_spec, start_spec],
        out_shape=[jax.ShapeDtypeStruct((N_SLAB, seq, SLAB_IN), F32), jax.ShapeDtypeStruct((nblk, SUBSEG, STATE_W), F32),
                   jax.ShapeDtypeStruct((nblk, SUBSEG, STATE_W), F32)],
        scratch_shapes=[pltpu.VMEM((tb, STATE_W), F32), pltpu.VMEM((tb, STATE_W), F32),
                        pltpu.VMEM((N_SLAB, tb, SLAB_IN), F32), pltpu.VMEM((N_SLAB, tb, SLAB_IN), F32),
                        pltpu.VMEM((SUBSEG, STATE_W), F32), pltpu.VMEM((SUBSEG, STATE_W), F32)],
        compiler_params=_cparams(("arbitrary",)),
    )(u, lam, lam, lam, lam, bb, bb, cb, cb)


def _ssm_bwd(u, dy, starts, lam, bb, bbt, cb_t, *, direction, tb, name):
    reverse = direction == 1
    seq = u.shape[1]
    nblk = seq // tb
    sub_len = tb // SUBSEG

    def body(u_ref, dy_ref, sr_ref, si_ref, lr_ref, li_ref, pr_ref, pi_ref, bbr_ref, bbi_ref, btr_ref, bti_ref,
             ctr_ref, cti_ref, du_ref, gb_ref, gc_ref, dl_ref,
             xr, xi, gr, gi, up, dyp, dup, gsr, gsi, car, cai):
        gbr_ref, gbi_ref = gb_ref.at[0], gb_ref.at[1]
        gcr_ref, gci_ref = gc_ref.at[0], gc_ref.at[1]
        dlr_ref, dli_ref = dl_ref.at[0], dl_ref.at[1]

        @pl.when(pl.program_id(0) == 0)
        def _():
            for ref in (car, cai, gbr_ref, gbi_ref, gcr_ref, gci_ref, dlr_ref, dli_ref):
                ref[...] = jnp.zeros_like(ref)

        _permute_rows(up, u_ref, sub_len)
        _permute_rows(dyp, dy_ref, sub_len)
        lr, li, pr, pi = lr_ref[...], li_ref[...], pr_ref[...], pi_ref[...]
        nli, npi = -li, -pi
        chunk = lambda k: slice(k * SLAB_ST, (k + 1) * SLAB_ST)

        def drive(k):
            ub = up[k].astype(BF16)
            xr[:, chunk(k)] = _dot(ub, bbr_ref[k])
            xi[:, chunk(k)] = _dot(ub, bbi_ref[k])
            dyb = dyp[k].astype(BF16)
            gr[:, chunk(k)] = _dot(dyb, ctr_ref[k])
            gi[:, chunk(k)] = -_dot(dyb, cti_ref[k])

        def scan_x(k):
            _scan_chunk(xr, xi, lr, li, (sr_ref[:, chunk(k)], si_ref[:, chunk(k)]), chunk(k),
                        sub_len=sub_len, reverse=reverse, store=True)

        def grad_c(k):
            dyb = dyp[k].astype(BF16)
            gcr_ref[k] += _dot_tn(xr[:, chunk(k)].astype(BF16), dyb)
            gci_ref[k] -= _dot_tn(xi[:, chunk(k)].astype(BF16), dyb)

        def scan_g(k):
            z = _scan_chunk(gr, gi, lr, nli, None, chunk(k), sub_len=sub_len, reverse=not reverse, store=False)
            _resolve_chunk(z, (car, cai), (gsr, gsi), pr, npi, chunk(k), reverse=not reverse)
            _scan_chunk(gr, gi, lr, nli, (gsr[:, chunk(k)], gsi[:, chunk(k)]), chunk(k),
                        sub_len=sub_len, reverse=not reverse, store=True)

        def grad_b_du(k):
            ub = up[k].astype(BF16)
            grb, gib = gr[:, chunk(k)].astype(BF16), gi[:, chunk(k)].astype(BF16)
            gbr_ref[k] += _dot_tn(ub, grb)
            gbi_ref[k] += _dot_tn(ub, gib)
            dup[k] = _dot(grb, btr_ref[k]) + _dot(gib, bti_ref[k])

        def grad_lambda(k):
            cols = chunk(k)
            acc_r, acc_i = dlr_ref[:, cols], dli_ref[:, cols]
            for jj in range(sub_len):
                prev = jj + 1 if reverse else jj - 1
                if 0 <= prev < sub_len:
                    x_r, x_i = xr[SUBSEG * prev:SUBSEG * (prev + 1), cols], xi[SUBSEG * prev:SUBSEG * (prev + 1), cols]
                else:
                    x_r, x_i = sr_ref[:, cols], si_ref[:, cols]
                g_r, g_i = gr[SUBSEG * jj:SUBSEG * (jj + 1), cols], gi[SUBSEG * jj:SUBSEG * (jj + 1), cols]
                acc_r = acc_r + (g_r * x_r + g_i * x_i)
                acc_i = acc_i + (g_i * x_r - g_r * x_i)
            dlr_ref[:, cols] = acc_r
            dli_ref[:, cols] = acc_i

        drive(0)
        for k in range(N_SLAB):
            scan_x(k)
            if k + 1 < N_SLAB:
                drive(k + 1)
            grad_c(k)
            scan_g(k)
            grad_b_du(k)
            grad_lambda(k)
        _unpermute_rows(du_ref, dup, sub_len)

    blk = (lambda i: i) if reverse else (lambda i: nblk - 1 - i)
    rows, wide, tall = _param_specs(direction)
    tok = pl.BlockSpec((N_SLAB, tb, SLAB_IN), lambda i: (0, blk(i), 0))
    start_spec = pl.BlockSpec((None, SUBSEG, STATE_W), lambda i: (blk(i), 0, 0))
    gb_shape, gc_shape, dl_shape = (2, N_SLAB, SLAB_IN, SLAB_ST), (2, N_SLAB, SLAB_ST, SLAB_IN), (2, SUBSEG, STATE_W)
    whole = lambda shape: pl.BlockSpec(shape, lambda i: (0,) * len(shape))
    big = lambda: pltpu.VMEM((tb, STATE_W), F32)
    slabs = lambda: pltpu.VMEM((N_SLAB, tb, SLAB_IN), F32)
    tile = lambda: pltpu.VMEM((SUBSEG, STATE_W), F32)
    return pl.pallas_call(
        body, name=name, grid=(nblk,),
        in_specs=[tok, tok, start_spec, start_spec] + rows + wide + tall + wide,
        out_specs=[tok, whole(gb_shape), whole(gc_shape), whole(dl_shape)],
        out_shape=[jax.ShapeDtypeStruct((N_SLAB, seq, SLAB_IN), F32), jax.ShapeDtypeStruct(gb_shape, F32),
                   jax.ShapeDtypeStruct(gc_shape, F32), jax.ShapeDtypeStruct(dl_shape, F32)],
        scratch_shapes=[big(), big(), big(), big(), slabs(), slabs(), slabs(), tile(), tile(), tile(), tile()],
        compiler_params=_cparams(("arbitrary",)),
    )(u, dy, *starts, lam, lam, lam, lam, bb, bb, bbt, bbt, cb_t, cb_t)


GELU_C = math.sqrt(2.0 / math.pi)
GELU_K = 0.044715


def _mid(o, za, u, y_f, y_b, zs, x, target, ssm_d, w_glu, b_glu, g_attn, g_ssm, w_out, ln_g, ln_b, tb):
    seq = x.shape[0]

    def body(o_ref, za_ref, u_ref, yf_ref, yb_ref, zs_ref, x_ref, t_ref, d_ref, wg_ref, bg_ref, ga_ref, gs_ref,
             wo_ref, lg_ref, lb_ref,
             loss_ref, do_ref, dza_ref, dyl_ref, dzs_ref, dpre_ref, gwo_ref, gwg_ref, vec_ref, wop):
        @pl.when(pl.program_id(0) == 0)
        def _():
            for ref in (loss_ref, gwo_ref, gwg_ref, vec_ref):
                ref[...] = jnp.zeros_like(ref)
            for nat, par in _pair_blocks(0):
                wop[par, :] = wo_ref[nat, :]
            wop[D_ATTN:, :] = wo_ref[D_ATTN:, :]

        o, za = o_ref[...], za_ref[...]
        sig_a = _sigmoid(za)
        silu_a = za * sig_a
        ya = o * silu_a
        r_a = lax.rsqrt(jnp.mean(ya * ya, axis=1, keepdims=True) + NORM_EPS)
        n_a = ya * r_a
        g_a = ga_ref[...]
        unslab = lambda ref: jnp.concatenate([ref[k] for k in range(N_SLAB)], axis=1)
        u_blk, zs = unslab(u_ref), zs_ref[...]
        d_row = d_ref[...]
        ylin = d_row * u_blk + unslab(yf_ref) + unslab(yb_ref)
        inner = GELU_C * (ylin + GELU_K * ylin * ylin * ylin)
        th = jnp.tanh(inner)
        gl = 0.5 * ylin * (1.0 + th)
        glb = gl.astype(BF16)
        sg = _sigmoid(_dot(glb, wg_ref[...]) + bg_ref[...])
        y2 = gl * sg
        sig_s = _sigmoid(zs)
        silu_s = zs * sig_s
        ys = y2 * silu_s
        r_s = lax.rsqrt(jnp.mean(ys * ys, axis=1, keepdims=True) + NORM_EPS)
        n_s = ys * r_s
        g_s = gs_ref[...]
        mixed = jnp.concatenate([n_a * g_a, n_s * g_s], axis=1).astype(BF16)
        pre = ALPHA * x_ref[...] + _dot(mixed, wop[...])
        mu = jnp.mean(pre, axis=1, keepdims=True)
        cen = pre - mu
        rstd = lax.rsqrt(jnp.mean(cen * cen, axis=1, keepdims=True) + NORM_EPS)
        hhat = cen * rstd
        ln_g = lg_ref[...]
        err = hhat * ln_g + lb_ref[...] - t_ref[...]
        loss_ref[...] += 0.5 * jnp.sum(jnp.mean(err * err, axis=1, keepdims=True))

        dh = err * (1.0 / D_MODEL)
        vec_ref[0:1, :] += jnp.sum(dh * hhat, axis=0, keepdims=True)
        vec_ref[1:2, :] += jnp.sum(dh, axis=0, keepdims=True)
        dhh = dh * ln_g
        dpre = rstd * (dhh - jnp.mean(dhh, axis=1, keepdims=True) - hhat * jnp.mean(dhh * hhat, axis=1, keepdims=True))
        dpre_ref[...] = dpre
        dpb = dpre.astype(BF16)
        for j in range(4):
            g_pair = _dot_tn(mixed[:, 128 * j:128 * (j + 1)], dpb)
            for g in range(2):
                nat = HEAD_DIM * (4 * g + j)
                gwo_ref[nat:nat + HEAD_DIM, :] += g_pair[HEAD_DIM * g:HEAD_DIM * (g + 1), :]
        gwo_ref[D_ATTN:, :] += _dot_tn(mixed[:, D_ATTN:], dpb)
        dmix = _dot_nt(dpb, wop[...])
        dna = dmix[:, :D_ATTN]
        vec_ref[2:3, 0:D_ATTN] += jnp.sum(dna * n_a, axis=0, keepdims=True)
        dna = dna * g_a
        dya = r_a * (dna - n_a * jnp.mean(dna * n_a, axis=1, keepdims=True))
        do_ref[...] = dya * silu_a
        dza_ref[...] = dya * o * (sig_a * (1.0 + za * (1.0 - sig_a)))
        dns = dmix[:, D_ATTN:]
        vec_ref[2:3, D_ATTN:] += jnp.sum(dns * n_s, axis=0, keepdims=True)
        dns = dns * g_s
        dys = r_s * (dns - n_s * jnp.mean(dns * n_s, axis=1, keepdims=True))
        dzs_ref[...] = dys * y2 * (sig_s * (1.0 + zs * (1.0 - sig_s)))
        dy2 = dys * silu_s
        da = dy2 * gl * sg * (1.0 - sg)
        vec_ref[3:4, D_SSM:] += jnp.sum(da, axis=0, keepdims=True)
        dab = da.astype(BF16)
        gwg_ref[...] += _dot_tn(glb, dab)
        dgl = dy2 * sg + _dot_nt(dab, wg_ref[...])
        dylin = dgl * (0.5 * (1.0 + th) + 0.5 * ylin * (1.0 - th * th) * GELU_C * (1.0 + 3.0 * GELU_K * ylin * ylin))
        for k in range(N_SLAB):
            dyl_ref[k] = dylin[:, k * SLAB_IN:(k + 1) * SLAB_IN]
        vec_ref[3:4, 0:D_SSM] += jnp.sum(dylin * u_blk, axis=0, keepdims=True)

    tok = lambda w: pl.BlockSpec((tb, w), lambda i: (i, 0))
    slab = pl.BlockSpec((N_SLAB, tb, SLAB_IN), lambda i: (0, i, 0))
    const = lambda r, c: pl.BlockSpec((r, c), lambda i: (0, 0))
    tok_shape = jax.ShapeDtypeStruct((seq, 512), F32)
    return pl.pallas_call(
        body, name="mid", grid=(seq // tb,),
        in_specs=[tok(512), tok(512), slab, slab, slab, tok(512), tok(1024), tok(1024),
                  const(1, 512), const(512, 512), const(1, 512), const(1, 512), const(1, 512),
                  const(1024, 1024), const(1, 1024), const(1, 1024)],
        out_specs=[const(8, 128), tok(512), tok(512), slab, tok(512), tok(1024),
                   const(1024, 1024), const(512, 512), const(8, 1024)],
        out_shape=[jax.ShapeDtypeStruct((8, 128), F32), tok_shape, tok_shape,
                   jax.ShapeDtypeStruct((N_SLAB, seq, SLAB_IN), F32), tok_shape,
                   jax.ShapeDtypeStruct((seq, 1024), F32), jax.ShapeDtypeStruct((1024, 1024), F32),
                   jax.ShapeDtypeStruct((512, 512), F32), jax.ShapeDtypeStruct((8, 1024), F32)],
        scratch_shapes=[pltpu.VMEM((D_MODEL, D_MODEL), BF16)],
        compiler_params=_cparams(("arbitrary",)),
    )(o, za, u, y_f, y_b, zs, x, target, ssm_d, w_glu, b_glu, g_attn, g_ssm, w_out, ln_g, ln_b)


def _proj_bwd(x, dq, dk, dv, dza, du_f, du_b, dylin, dzs, dpre, ssm_d, cos128, sin128, wt, tb):
    seq = x.shape[0]

    def body(x_ref, dq_ref, dk_ref, dv_ref, dza_ref, duf_ref, dub_ref, dyl_ref, dzs_ref, dpre_ref, d_ref,
             cos_ref, sin_ref, wt_ref, gx_ref, gw_ref, wp):
        @pl.when(pl.program_id(0) == 0)
        def _():
            gw_ref[...] = jnp.zeros_like(gw_ref)
            for base in (W_Q, W_ZA):
                for nat, par in _pair_blocks(base):
                    wp[par, :] = wt_ref[nat, :]
            wp[W_KV:W_ZA, :] = wt_ref[W_KV:W_ZA, :]
            wp[W_U:, :] = wt_ref[W_U:, :]

        cos, sin = cos_ref[...], sin_ref[...]

        def unrope(t):
            return t * cos + _rotate_half_unsigned(t * sin)

        dq_rot = dq_ref[...]
        pieces = [unrope(dq_rot[:, 128 * j:128 * (j + 1)]) for j in range(4)]
        d_row = d_ref[...]
        pieces += [unrope(dk_ref[...]), dv_ref[...], dza_ref[...]]
        pieces += [duf_ref[k] + dub_ref[k] + d_row[:, k * SLAB_IN:(k + 1) * SLAB_IN] * dyl_ref[k] for k in range(N_SLAB)]
        pieces += [dzs_ref[...]]
        dproj = jnp.concatenate(pieces, axis=1).astype(BF16)
        gx_ref[...] = ALPHA * dpre_ref[...] + _dot(dproj, wp[...])
        xb = x_ref[...].astype(BF16)
        for base in (W_Q, W_ZA):
            for j in range(4):
                g_pair = _dot_tn(dproj[:, base + 128 * j:base + 128 * (j + 1)], xb)
                for g in range(2):
                    nat = base + HEAD_DIM * (4 * g + j)
                    gw_ref[nat:nat + HEAD_DIM, :] += g_pair[HEAD_DIM * g:HEAD_DIM * (g + 1), :]
        gw_ref[W_KV:W_ZA, :] += _dot_tn(dproj[:, W_KV:W_ZA], xb)
        gw_ref[W_U:, :] += _dot_tn(dproj[:, W_U:], xb)

    tok = lambda w: pl.BlockSpec((tb, w), lambda i: (i, 0))
    slab = pl.BlockSpec((N_SLAB, tb, SLAB_IN), lambda i: (0, i, 0))
    const = lambda r, c: pl.BlockSpec((r, c), lambda i: (0, 0))
    return pl.pallas_call(
        body, name="proj_bwd", grid=(seq // tb,),
        in_specs=[tok(1024), tok(512), tok(128), tok(128), tok(512), slab, slab, slab, tok(512), tok(1024),
                  const(1, 512), tok(128), tok(128), const(D_IN_PROJ, D_MODEL)],
        out_specs=[tok(1024), const(D_IN_PROJ, D_MODEL)],
        out_shape=[jax.ShapeDtypeStruct((seq, D_MODEL), F32), jax.ShapeDtypeStruct((D_IN_PROJ, D_MODEL), F32)],
        scratch_shapes=[pltpu.VMEM((D_IN_PROJ, D_MODEL), BF16)],
        compiler_params=_cparams(("arbitrary",)),
    )(x, dq, dk, dv, dza, du_f, du_b, dylin, dzs, dpre, ssm_d, cos128, sin128, wt)


def _adamw(w, g, m, v, name):
    rows, cols = w.shape
    tb = rows
    while tb * cols * 4 > ADAMW_BLOCK_BYTES and tb % 16 == 0:
        tb //= 2

    def body(w_ref, g_ref, m_ref, v_ref, d_ref, nm_ref, nv_ref):
        _adamw_update(w_ref, g_ref, m_ref, v_ref, d_ref, nm_ref, nv_ref)

    spec = pl.BlockSpec((tb, cols), lambda i: (i, 0))
    return pl.pallas_call(
        body, name=name, grid=(rows // tb,), in_specs=[spec] * 4, out_specs=[spec] * 3,
        out_shape=[jax.ShapeDtypeStruct((rows, cols), F32)] * 3,
        compiler_params=_cparams(("arbitrary",)),
    )(w, g, m, v)


def _adamw_update(w_ref, g_ref, m_ref, v_ref, d_ref, nm_ref, nv_ref):
    g_blk = g_ref[...]
    m_new = ADAM_B1 * m_ref[...] + (1.0 - ADAM_B1) * g_blk
    v_new = ADAM_B2 * v_ref[...] + (1.0 - ADAM_B2) * (g_blk * g_blk)
    m_hat = m_new / (1.0 - ADAM_B1 ** ADAM_STEP)
    v_hat = v_new / (1.0 - ADAM_B2 ** ADAM_STEP)
    d_ref[...] = -ADAM_LR * (m_hat / (jnp.sqrt(v_hat) + ADAM_EPS) + ADAM_WD * w_ref[...])
    nm_ref[...] = m_new
    nv_ref[...] = v_new


def _adamw_many(groups, name):
    n = len(groups)

    def body(*refs):
        for p in range(n):
            _adamw_update(*refs[4 * p:4 * p + 4], *refs[4 * n + 3 * p:4 * n + 3 * p + 3])

    return pl.pallas_call(
        body, name=name,
        out_shape=[jax.ShapeDtypeStruct(grp[0].shape, F32) for grp in groups for _ in range(3)],
    )(*[a for grp in groups for a in grp])


_WEIGHTS = ["w_in", "attn_sink", "ssm_a_re", "ssm_a_im", "ssm_log_dt", "ssm_b_re", "ssm_b_im", "ssm_c_re", "ssm_c_im",
            "ssm_d", "w_glu", "b_glu", "norm_attn_g", "norm_ssm_g", "w_out", "ln_g", "ln_b"]
N_DG = N_DIR * N_GROUPS
BIG_ROWS = N_DG * SSM_CH * SSM_STATE // 128
TINY_ROWS = 64


def _pack_small_grads(g_bc, g_vec, g_ar, g_ai, g_dt, g_sink, loss):
    big = jnp.stack([t.reshape(BIG_ROWS, 128) for t in g_bc])
    row = lambda t: jnp.pad(t.reshape(1, -1), ((0, 0), (0, 128 - t.size)))
    tiny = jnp.concatenate([g_vec.reshape(64, 128), g_ar.reshape(32, 128), g_ai.reshape(32, 128), row(g_dt), row(g_sink),
                            row(loss), jnp.zeros((N_CHIPS * TINY_ROWS - 131, 128), F32)], axis=0)
    return jnp.concatenate([big, tiny.reshape(N_CHIPS, TINY_ROWS, 128)], axis=1)


def _unpack_small_grads(packed):
    big = packed[:, :BIG_ROWS].reshape(N_CHIPS, 2 * BIG_ROWS, SSM_STATE)
    tiny = packed[:, BIG_ROWS:].reshape(N_CHIPS * TINY_ROWS, 128)
    g_vec = tiny[0:64].reshape(8, 1024)
    return tiny[130, 0], {
        "ssm_b_re": big[0], "ssm_b_im": big[1], "ssm_c_re": big[2], "ssm_c_im": big[3],
        "ln_g": g_vec[0:1], "ln_b": g_vec[1:2],
        "norm_attn_g": _from_pair_order(g_vec[2:3, :D_ATTN]), "norm_ssm_g": g_vec[2:3, D_ATTN:],
        "ssm_d": g_vec[3:4, :D_SSM], "b_glu": g_vec[3:4, D_SSM:],
        "ssm_a_re": tiny[64:96].reshape(N_DG, SSM_STATE), "ssm_a_im": tiny[96:128].reshape(N_DG, SSM_STATE),
        "ssm_log_dt": tiny[128:129, :N_DG].reshape(N_DIR, N_GROUPS), "attn_sink": tiny[129:130, :N_Q_HEADS],
    }


def _small_view(name, t):
    if name in ("ssm_b_re", "ssm_b_im"):
        return jnp.swapaxes(t[0], 2, 3).reshape(N_DG * SSM_CH, SSM_STATE)
    if name in ("ssm_c_re", "ssm_c_im"):
        return t.reshape(N_DG * SSM_CH, SSM_STATE)
    if name in ("ssm_a_re", "ssm_a_im"):
        return t.reshape(N_DG, SSM_STATE)
    if name == "ssm_log_dt":
        return t.reshape(N_DIR, N_GROUPS)
    return t.reshape(1, -1)


def _small_unview(name, t, shape):
    if name in ("ssm_b_re", "ssm_b_im"):
        return jnp.swapaxes(t.reshape(N_DIR, N_GROUPS, SSM_CH, SSM_STATE), 2, 3).reshape(shape)
    return t.reshape(shape)


def kernel(x, w_in, attn_sink, ssm_a_re, ssm_a_im, ssm_log_dt, ssm_b_re, ssm_b_im, ssm_c_re, ssm_c_im, ssm_d, w_glu, b_glu, norm_attn_g, norm_ssm_g, w_out, ln_g, ln_b, loss_target, m_w_in, m_attn_sink, m_ssm_a_re, m_ssm_a_im, m_ssm_log_dt, m_ssm_b_re, m_ssm_b_im, m_ssm_c_re, m_ssm_c_im, m_ssm_d, m_w_glu, m_b_glu, m_norm_attn_g, m_norm_ssm_g, m_w_out, m_ln_g, m_ln_b, v_w_in, v_attn_sink, v_ssm_a_re, v_ssm_a_im, v_ssm_log_dt, v_ssm_b_re, v_ssm_b_im, v_ssm_c_re, v_ssm_c_im, v_ssm_d, v_w_glu, v_b_glu, v_norm_attn_g, v_norm_ssm_g, v_w_out, v_ln_g, v_ln_b):
    args = dict(locals())
    weights = {n: args[n] for n in _WEIGHTS}
    mom_m = {n: args["m_" + n] for n in _WEIGHTS}
    mom_v = {n: args["v_" + n] for n in _WEIGHTS}
    xs = x[0]
    target = loss_target[0]

    wt_g, w_glu_g, w_out_g = _all_gather_chips([w_in[0].T, w_glu[0], w_out[0]], BF16, "gather_weights")
    wt_full = wt_g.reshape(D_IN_PROJ, D_MODEL)
    w_glu_full = w_glu_g.reshape(D_SSM, D_SSM)
    w_out_full = w_out_g.reshape(D_MODEL, D_MODEL)

    g_x, g_wt, g_w_out, g_w_glu, g_small = _local_step(
        xs, target, wt_full, w_glu_full, w_out_full, attn_sink, ssm_a_re, ssm_a_im, ssm_log_dt, ssm_b_re, ssm_b_im,
        ssm_c_re, ssm_c_im, ssm_d, b_glu, norm_attn_g, norm_ssm_g, ln_g, ln_b)

    r_wt, r_w_out, r_w_glu, g_small_all = _reduce_all(
        [g_wt.reshape(N_CHIPS, -1, D_MODEL), g_w_out.reshape(N_CHIPS, -1, D_MODEL), g_w_glu.reshape(N_CHIPS, -1, D_SSM),
         g_small], [True, True, True, False], "reduce_grads")
    loss, small_grads = _unpack_small_grads(g_small_all)

    grads, deltas, new_m, new_v = {}, {}, {}, {}
    d_w, m_w, v_w = _adamw(w_in[0].T, r_wt, m_w_in[0].T, v_w_in[0].T, "adamw_w_in")
    grads["w_in"], deltas["w_in"], new_m["w_in"], new_v["w_in"] = r_wt.T[None], d_w.T[None], m_w.T[None], v_w.T[None]
    for n, g in (("w_out", r_w_out), ("w_glu", r_w_glu)):
        d_w, m_w, v_w = _adamw(weights[n][0], g, mom_m[n][0], mom_v[n][0], "adamw_" + n)
        grads[n], deltas[n], new_m[n], new_v[n] = g[None], d_w[None], m_w[None], v_w[None]
    names = sorted(small_grads)
    updates = _adamw_many([(_small_view(n, weights[n]), small_grads[n], _small_view(n, mom_m[n]), _small_view(n, mom_v[n]))
                           for n in names], "adamw_small")
    for i, n in enumerate(names):
        shape = weights[n].shape
        grads[n] = _small_unview(n, small_grads[n], shape)
        deltas[n], new_m[n], new_v[n] = (_small_unview(n, t, shape) for t in updates[3 * i:3 * i + 3])

    return (loss, g_x[None], *[grads[n] for n in _WEIGHTS], *[deltas[n] for n in _WEIGHTS],
            *[new_m[n] for n in _WEIGHTS], *[new_v[n] for n in _WEIGHTS])


def _local_step(xs, target, wt_full, w_glu_full, w_out_full, attn_sink, ssm_a_re, ssm_a_im, ssm_log_dt, ssm_b_re,
                ssm_b_im, ssm_c_re, ssm_c_im, ssm_d, b_glu, norm_attn_g, norm_ssm_g, ln_g, ln_b):
    seq = xs.shape[0]

    a_r, a_i = _small_view("ssm_a_re", ssm_a_re), _small_view("ssm_a_im", ssm_a_im)
    log_dt = ssm_log_dt.reshape(N_DG, 1)
    b_r, b_i = _small_view("ssm_b_re", ssm_b_re), _small_view("ssm_b_im", ssm_b_im)
    c_r, c_i = _small_view("ssm_c_re", ssm_c_re), _small_view("ssm_c_im", ssm_c_im)
    ssm_tb = min(SSM_BLOCK, seq)
    sub_len = ssm_tb // SUBSEG
    lam, bb, bbt, cb, cb_t = _ssm_params_fwd(a_r, a_i, log_dt, b_r, b_i, c_r, c_i, int(math.log2(sub_len)))
    lam = lam.reshape(4, N_DIR, 1, STATE_W)

    cos128, sin128 = _rope_tables(seq)
    q_stack, k_rot, v_bf, z_attn, u, z_ssm = _proj(xs, wt_full, cos128, sin128, min(512, seq))
    sink128 = jnp.broadcast_to(attn_sink[0][:, None, None], (N_Q_HEADS, 1, 128))
    attn_bias = _attn_bias()
    o = _attn_fwd(q_stack, k_rot, v_bf, sink128, attn_bias)
    ys, starts = [], []
    for d in range(N_DIR):
        y_d, s_r, s_i = _ssm_fwd(u, lam, bb, cb, direction=d, tb=ssm_tb, name=f"ssm_fwd_{d}")
        ys.append(y_d)
        starts.append((s_r, s_i))

    row = lambda t: t.reshape(1, -1)
    g_attn_p = _to_pair_order(norm_attn_g)
    loss_blk, d_o, d_za, d_ylin, d_zs, d_pre, g_w_out, g_w_glu, g_vec = _mid(
        o, z_attn, u, ys[0], ys[1], z_ssm, xs, target, row(ssm_d), w_glu_full, row(b_glu),
        g_attn_p, row(norm_ssm_g), w_out_full, row(ln_g), row(ln_b), min(256, seq))

    dq, dk, dv, g_sink = _attn_bwd(q_stack, k_rot, v_bf, sink128, attn_bias, d_o)
    dus, g_bb, g_cb, g_lam = [], [], [], []
    for d in range(N_DIR):
        du_d, gb_d, gc_d, dl_d = _ssm_bwd(u, d_ylin, starts[d], lam, bb, bbt, cb_t, direction=d, tb=ssm_tb,
                                          name=f"ssm_bwd_{d}")
        dus.append(du_d)
        g_bb.append(gb_d)
        g_cb.append(gc_d)
        g_lam.append(dl_d)
    g_ar, g_ai, g_dt, g_br, g_bi, g_cr, g_ci = _ssm_params_bwd(a_r, a_i, log_dt, b_r, b_i, g_bb, g_cb, g_lam)

    g_x, g_wt = _proj_bwd(xs, dq, dk, dv, d_za, dus[0], dus[1], d_ylin, d_zs, d_pre, row(ssm_d), cos128, sin128,
                          wt_full, min(256, seq))

    g_small = _pack_small_grads([g_br, g_bi, g_cr, g_ci], g_vec, g_ar, g_ai, g_dt, g_sink[:, 0], loss_blk[0, 0])
    return g_x, g_wt, g_w_out, g_w_glu, g_small
```

```python
import functools
import math

import numpy as np
import jax
import jax.numpy as jnp
from jax import lax
from jax.experimental import pallas as pl
from jax.experimental.pallas import tpu as pltpu

F32 = jnp.float32
BF16 = jnp.bfloat16
MESH = pl.DeviceIdType.MESH

D_MODEL = 1024
D_ATTN = 512
D_SSM = 512
HEAD_DIM = 64
N_Q_HEADS = 8
WINDOW = 128
ROPE_THETA = 10000.0
SSM_CH = 16
N_GROUPS = 32
SSM_STATE = 64
N_DIR = 2
STATE_W = N_GROUPS * SSM_STATE
N_SLAB = 4
SLAB_IN = 128
SLAB_ST = 512
NORM_EPS = 1e-5
NEG_INF = -1e30
ALPHA = 2.0 ** 0.25
D_IN_PROJ = 2304
N_CHIPS = 4

ADAM_LR = 0.001
ADAM_B1 = 0.9
ADAM_B2 = 0.999
ADAM_EPS = 1e-08
ADAM_WD = 0.01
ADAM_STEP = 10

SUBSEG = 8
SCAN_LANES = 512
SSM_BLOCK = 512
VMEM_LIMIT = 48 * 1024 * 1024
ADAMW_BLOCK_BYTES = 3 * 512 * 1024

def _to_pair_order(row):
    return jnp.transpose(row.reshape(2, 4, HEAD_DIM), (1, 0, 2)).reshape(1, D_ATTN)


def _from_pair_order(row):
    return jnp.transpose(row.reshape(4, 2, HEAD_DIM), (1, 0, 2)).reshape(1, D_ATTN)


def _cparams(sem=None):
    return pltpu.CompilerParams(dimension_semantics=sem, vmem_limit_bytes=VMEM_LIMIT)


def _dot(a, b):
    return jnp.dot(a, b, preferred_element_type=F32)


def _dot_nt(a, b):
    return lax.dot_general(a, b, (((1,), (1,)), ((), ())), preferred_element_type=F32)


def _dot_tn(a, b):
    return lax.dot_general(a, b, (((0,), (0,)), ((), ())), preferred_element_type=F32)


def _sigmoid(z):
    return 1.0 / (1.0 + jnp.exp(-z))


def _all_gather_chips(shards, out_dtype, name):
    n = len(shards)

    def body(*refs):
        in_refs, out_refs = refs[:n], refs[n:2 * n]
        send_sems, recv_sems = refs[2 * n:]
        x, y, c = lax.axis_index("x"), lax.axis_index("y"), lax.axis_index("c")
        sibling = (x, y, 1 - c)
        chips = [(1 - x, y), (x, 1 - y), (1 - x, 1 - y)]

        for a in range(n):
            out_refs[a][2 * x + y] = in_refs[a][...].astype(out_dtype)

        def half_of(a, px, py, half):
            rows = in_refs[a].shape[0] // 2
            return out_refs[a].at[2 * px + py, pl.ds(half * rows, rows), :]

        def copy(a, k, px, py, half, to):
            blk = half_of(a, px, py, half)
            return pltpu.make_async_remote_copy(src_ref=blk, dst_ref=blk, send_sem=send_sems.at[6 * a + k],
                                                recv_sem=recv_sems.at[6 * a + k], device_id=to, device_id_type=MESH)

        first = [copy(a, j, x, y, c, (*chips[j], c)) for a in range(n) for j in range(3)]
        for cp in first:
            cp.start()
        passed = []
        for a in range(n):
            for j in range(3):
                copy(a, j, *chips[j], c, (x, y, c)).wait_recv()
                fwd = copy(a, 3 + j, *chips[j], c, sibling)
                fwd.start()
                passed.append(fwd)
        for a in range(n):
            for j in range(3):
                copy(a, 3 + j, *chips[j], 1 - c, (x, y, c)).wait_recv()
        for cp in first + passed:
            cp.wait_send()

    vmem = pl.BlockSpec(memory_space=pltpu.VMEM)
    return pl.pallas_call(
        body, name=name,
        out_shape=[jax.ShapeDtypeStruct((N_CHIPS,) + s.shape, out_dtype) for s in shards],
        in_specs=[vmem] * n, out_specs=[vmem] * n,
        scratch_shapes=[pltpu.SemaphoreType.DMA((6 * n,)), pltpu.SemaphoreType.DMA((6 * n,))],
        compiler_params=pltpu.CompilerParams(vmem_limit_bytes=VMEM_LIMIT),
    )(*shards)


SEMS_PER_ARRAY = 11


def _reduce_all(pieces, narrow, name):
    n = len(pieces)
    halves = [p.shape[1] // 2 for p in pieces]
    wire = [BF16 if nar else F32 for nar in narrow]

    def body(*refs):
        p_refs, out_refs = refs[:n], refs[n:2 * n]
        a_refs, s_refs, b_refs = refs[2 * n:3 * n], refs[3 * n:4 * n], refs[4 * n:5 * n]
        send_sems, recv_sems = refs[5 * n:]
        x, y, c = lax.axis_index("x"), lax.axis_index("y"), lax.axis_index("c")
        me = 2 * x + y
        sibling = (x, y, 1 - c)
        chips = [(1 - x, y), (x, 1 - y), (1 - x, 1 - y)]
        slot = [2 * px + py for px, py in chips]

        def copy(a, k, src, dst, to):
            return pltpu.make_async_remote_copy(src_ref=src, dst_ref=dst, send_sem=send_sems.at[SEMS_PER_ARRAY * a + k],
                                                recv_sem=recv_sems.at[SEMS_PER_ARRAY * a + k],
                                                device_id=to, device_id_type=MESH)

        def rows(a, half):
            return pl.ds(pl.multiple_of(half * halves[a], 16), halves[a])

        started = []
        for a in range(n):
            cp = copy(a, 0, p_refs[a].at[:, rows(a, 1 - c), :], a_refs[a], sibling)
            cp.start()
            started.append(cp)
        for a in range(n):
            started[a].wait_recv()
            for k in range(N_CHIPS):
                acc = a_refs[a][k] + p_refs[a][k, rows(a, c), :]
                a_refs[a][k] = acc
                s_refs[a][k] = acc.astype(wire[a])
            b_refs[a][me] = s_refs[a][me]
            for j in range(3):
                cp = copy(a, 1 + j, s_refs[a].at[slot[j]], b_refs[a].at[me], (*chips[j], c))
                cp.start()
                started.append(cp)
        for a in range(n):
            for j in range(3):
                copy(a, 1 + j, s_refs[a].at[slot[j]], b_refs[a].at[slot[j]], (x, y, c)).wait_recv()
            terms = [jnp.where(me == k, a_refs[a][k], b_refs[a][k].astype(F32)) for k in range(N_CHIPS)]
            total = (terms[0] + terms[1]) + (terms[2] + terms[3])
            if a < n - 1:
                done = out_refs[a].at[rows(a, c), :]
                out_refs[a][rows(a, c), :] = total
            else:
                done = out_refs[a].at[me, rows(a, c), :]
                out_refs[a][me, rows(a, c), :] = total
            cp = copy(a, 4, done, done, sibling)
            cp.start()
            started.append(cp)
        last = n - 1
        piece = lambda k, half: out_refs[last].at[k, rows(last, half), :]
        for j in range(3):
            cp = copy(last, 5 + j, piece(me, c), piece(me, c), (*chips[j], c))
            cp.start()
            started.append(cp)
        for j in range(3):
            copy(last, 5 + j, piece(slot[j], c), piece(slot[j], c), (x, y, c)).wait_recv()
            cp = copy(last, 8 + j, piece(slot[j], c), piece(slot[j], c), sibling)
            cp.start()
            started.append(cp)
        for a in range(n - 1):
            copy(a, 4, out_refs[a].at[rows(a, 1 - c), :], out_refs[a].at[rows(a, 1 - c), :], (x, y, c)).wait_recv()
        copy(last, 4, piece(me, 1 - c), piece(me, 1 - c), (x, y, c)).wait_recv()
        for j in range(3):
            copy(last, 8 + j, piece(slot[j], 1 - c), piece(slot[j], 1 - c), (x, y, c)).wait_recv()
        for cp in started:
            cp.wait_send()

    vmem = pl.BlockSpec(memory_space=pltpu.VMEM)
    half_shape = lambda a: (N_CHIPS, halves[a], pieces[a].shape[2])
    return pl.pallas_call(
        body, name=name,
        out_shape=[jax.ShapeDtypeStruct(p.shape[1:] if a < n - 1 else p.shape, F32) for a, p in enumerate(pieces)],
        in_specs=[vmem] * n, out_specs=[vmem] * n,
        scratch_shapes=[pltpu.VMEM(half_shape(a), F32) for a in range(n)]
        + [pltpu.VMEM(half_shape(a), wire[a]) for a in range(n)]
        + [pltpu.VMEM(half_shape(a), wire[a]) for a in range(n)]
        + [pltpu.SemaphoreType.DMA((SEMS_PER_ARRAY * n,)), pltpu.SemaphoreType.DMA((SEMS_PER_ARRAY * n,))],
        compiler_params=pltpu.CompilerParams(vmem_limit_bytes=VMEM_LIMIT),
    )(*pieces)


def _ssm_param_values(ar, ai, logdt):
    dt = jnp.exp(logdt)
    mag = jnp.exp(dt * ar)
    cs, sn = jnp.cos(dt * ai), jnp.sin(dt * ai)
    lr, li = mag * cs, mag * sn
    den = ar * ar + ai * ai
    nr = (lr - 1.0) * ar + li * ai
    ni = li * ar - (lr - 1.0) * ai
    return dt, mag, lr, li, den, nr, ni


GROUPS_PER_SLAB = N_GROUPS // N_SLAB


def _slab_masks():
    def eq(shape, f_row, f_col):
        return (f_row(lax.broadcasted_iota(jnp.int32, shape, 0)) == f_col(lax.broadcasted_iota(jnp.int32, shape, 1))).astype(F32)
    spread = eq((SSM_STATE, SLAB_ST), lambda r: r, lambda c: c % SSM_STATE)
    spread_t = eq((SLAB_ST, SSM_STATE), lambda r: r % SSM_STATE, lambda c: c)
    keep = eq((SLAB_IN, SLAB_ST), lambda r: r // SSM_CH, lambda c: c // SSM_STATE)
    keep_t = eq((SLAB_ST, SLAB_IN), lambda r: r // SSM_STATE, lambda c: c // SSM_CH)
    repeat = eq((N_DG * SSM_CH, N_DG), lambda r: r // SSM_CH, lambda c: c)
    return spread, spread_t, keep, keep_t, repeat


def _split3(t):
    hi = t.astype(BF16)
    rest = t - hi.astype(F32)
    mid = rest.astype(BF16)
    return hi, mid, (rest - mid.astype(F32)).astype(BF16)


def _select(dot, ones01, t, ones_first):
    o = ones01.astype(BF16)
    parts = [dot(o, p) if ones_first else dot(p, o) for p in _split3(t)]
    return (parts[0] + parts[1]) + parts[2]


def _ssm_params_fwd(ar, ai, logdt, br, bi, cr, ci, n_square):
    def body(ar_ref, ai_ref, dt_ref, br_ref, bi_ref, cr_ref, ci_ref, lam_ref, bb_ref, bbt_ref, cb_ref, cbt_ref):
        _, _, lr, li, den, nr, ni = _ssm_param_values(ar_ref[...], ai_ref[...], dt_ref[...])
        lam_ref[0] = lr
        lam_ref[1] = li
        pr, pi = lr, li
        for _ in range(n_square):
            pr, pi = pr * pr - pi * pi, 2.0 * pr * pi
        lam_ref[2] = pr
        lam_ref[3] = pi
        spread, spread_t, keep, keep_t, repeat = _slab_masks()
        fr = _select(_dot, repeat, nr / den, True)
        fi = _select(_dot, repeat, ni / den, True)
        b_r, b_i = br_ref[...], bi_ref[...]
        bbar = (fr * b_r - fi * b_i, fr * b_i + fi * b_r)
        c_par = (cr_ref[...], ci_ref[...])
        spread, spread_t = spread.astype(BF16), spread_t.astype(BF16)
        for src, wide_ref, tall_ref in ((bbar, bb_ref, bbt_ref), (c_par, cbt_ref, cb_ref)):
            for q in range(2):
                for d in range(N_DIR):
                    for k in range(N_SLAB):
                        r0 = (d * N_GROUPS + k * GROUPS_PER_SLAB) * SSM_CH
                        blk = src[q][r0:r0 + SLAB_IN].astype(BF16)
                        wide_ref[q, d, k] = (_dot(blk, spread) * keep).astype(BF16)
                        tall_ref[q, d, k] = (_dot_nt(spread_t, blk) * keep_t).astype(BF16)

    wide = jax.ShapeDtypeStruct((2, N_DIR, N_SLAB, SLAB_IN, SLAB_ST), BF16)
    tall = jax.ShapeDtypeStruct((2, N_DIR, N_SLAB, SLAB_ST, SLAB_IN), BF16)
    return pl.pallas_call(body, name="ssm_params_fwd",
                          out_shape=[jax.ShapeDtypeStruct((4,) + ar.shape, F32), wide, tall, tall, wide],
                          compiler_params=pltpu.CompilerParams(vmem_limit_bytes=VMEM_LIMIT),
                          )(ar, ai, logdt, br, bi, cr, ci)


def _ssm_params_bwd(ar, ai, logdt, br, bi, g_slabs_b, g_slabs_c, g_lam):
    def body(ar_ref, ai_ref, dt_ref, br_ref, bi_ref, gb0_ref, gb1_ref, gc0_ref, gc1_ref, gl0_ref, gl1_ref,
             gar_ref, gai_ref, gdt_ref, gbr_ref, gbi_ref, gcr_ref, gci_ref, dbb, dlam):
        spread, spread_t, keep, keep_t, repeat = _slab_masks()
        for d, (gb_ref, gc_ref) in enumerate(((gb0_ref, gc0_ref), (gb1_ref, gc1_ref))):
            for q in range(2):
                for k in range(N_SLAB):
                    r0 = (d * N_GROUPS + k * GROUPS_PER_SLAB) * SSM_CH
                    dbb[q, r0:r0 + SLAB_IN, :] = _select(_dot, spread_t, gb_ref[q, k] * keep, False)
                    out_ref = gcr_ref if q == 0 else gci_ref
                    out_ref[r0:r0 + SLAB_IN, :] = _select(_dot_tn, spread_t, gc_ref[q, k] * keep_t, False)
        grp = (lax.broadcasted_iota(jnp.int32, (N_GROUPS, STATE_W), 0)
               == lax.broadcasted_iota(jnp.int32, (N_GROUPS, STATE_W), 1) // SSM_STATE).astype(F32)
        pick = (lax.broadcasted_iota(jnp.int32, (STATE_W, SSM_STATE), 0) % SSM_STATE
                == lax.broadcasted_iota(jnp.int32, (STATE_W, SSM_STATE), 1)).astype(F32)
        for d, gl_ref in enumerate((gl0_ref, gl1_ref)):
            for q in range(2):
                row = jnp.sum(gl_ref[q], axis=0, keepdims=True)
                dlam[q, d * N_GROUPS:(d + 1) * N_GROUPS, :] = _select(_dot, pick, grp * row, False)

        a_r, a_i = ar_ref[...], ai_ref[...]
        dt, mag, lr, li, den, nr, ni = _ssm_param_values(a_r, a_i, dt_ref[...])
        fr = _select(_dot, repeat, nr / den, True)
        fi = _select(_dot, repeat, ni / den, True)
        b_r, b_i = br_ref[...], bi_ref[...]
        g_r, g_i = dbb[0], dbb[1]
        gbr_ref[...] = fr * g_r + fi * g_i
        gbi_ref[...] = fr * g_i - fi * g_r
        d_fr = _select(_dot_tn, repeat, b_r * g_r + b_i * g_i, True)
        d_fi = _select(_dot_tn, repeat, b_r * g_i - b_i * g_r, True)
        d_nr, d_ni = d_fr / den, d_fi / den
        d_den = -(d_fr * nr + d_fi * ni) / (den * den)
        d_lr = dlam[0] + d_nr * a_r - d_ni * a_i
        d_li = dlam[1] + d_nr * a_i + d_ni * a_r
        d_ar = d_nr * (lr - 1.0) + d_ni * li + d_den * 2.0 * a_r
        d_ai = d_nr * li - d_ni * (lr - 1.0) + d_den * 2.0 * a_i
        d_mag = (d_lr * lr + d_li * li) / mag
        d_theta = d_li * lr - d_lr * li
        gar_ref[...] = d_ar + d_mag * mag * dt
        gai_ref[...] = d_ai + d_theta * dt
        d_dt = d_mag * mag * a_r + d_theta * a_i
        gdt_ref[...] = jnp.sum(d_dt, axis=1, keepdims=True) * dt

    small = jax.ShapeDtypeStruct(ar.shape, F32)
    big = jax.ShapeDtypeStruct(br.shape, F32)
    return pl.pallas_call(
        body, name="ssm_params_bwd",
        out_shape=[small, small, jax.ShapeDtypeStruct(logdt.shape, F32), big, big, big, big],
        scratch_shapes=[pltpu.VMEM((2,) + br.shape, F32), pltpu.VMEM((2,) + ar.shape, F32)],
        compiler_params=pltpu.CompilerParams(vmem_limit_bytes=VMEM_LIMIT),
    )(ar, ai, logdt, br, bi, *g_slabs_b, *g_slabs_c, *g_lam)


def _rope_tables(seq):
    half = HEAD_DIM // 2
    inv_freq = ROPE_THETA ** (-jnp.arange(half, dtype=F32) / half)
    ang = jnp.arange(seq, dtype=jnp.int32).astype(F32)[:, None] * inv_freq[None, :]
    cos, sin = jnp.cos(ang), jnp.sin(ang)
    cos128 = jnp.concatenate([cos, cos, cos, cos], axis=1)
    sin128 = jnp.concatenate([-sin, sin, -sin, sin], axis=1)
    return cos128, sin128


def _rotate_half_unsigned(t):
    lane = lax.broadcasted_iota(jnp.int32, t.shape, 1)
    return jnp.where((lane % HEAD_DIM) < HEAD_DIM // 2, pltpu.roll(t, 96, 1), pltpu.roll(t, 32, 1))


def _rope(t, cos, sin_signed):
    return t * cos + _rotate_half_unsigned(t) * sin_signed


def _pair_blocks(base):
    out = []
    for j in range(4):
        for g in range(2):
            nat = base + HEAD_DIM * (4 * g + j)
            par = base + 128 * j + HEAD_DIM * g
            out.append((slice(nat, nat + HEAD_DIM), slice(par, par + HEAD_DIM)))
    return out


W_Q, W_KV, W_ZA, W_U, W_ZS = 0, 512, 768, 1280, 1792


def _proj(x, wt, cos128, sin128, tb):
    seq = x.shape[0]

    def body(x_ref, wt_ref, cos_ref, sin_ref, q_ref, k_ref, v_ref, za_ref, u_ref, zs_ref, wp):
        @pl.when(pl.program_id(0) == 0)
        def _():
            for dst_base, src_base in ((0, W_Q), (512, W_ZA)):
                for nat, par in _pair_blocks(0):
                    wp[dst_base + par.start:dst_base + par.stop, :] = wt_ref[src_base + nat.start:src_base + nat.stop, :]

        xb = x_ref[...].astype(BF16)
        cos, sin = cos_ref[...], sin_ref[...]
        lo = lax.broadcasted_iota(jnp.int32, (tb, 128), 1) < HEAD_DIM
        q = _dot_nt(xb, wp[0:512, :])
        for j in range(4):
            qj = _rope(q[:, 128 * j:128 * (j + 1)], cos, sin)
            q_ref[j] = jnp.where(lo, qj, 0.0).astype(BF16)
            q_ref[4 + j] = jnp.where(lo, 0.0, qj).astype(BF16)
        kv = _dot_nt(xb, wt_ref[W_KV:W_ZA, :])
        k_ref[...] = _rope(kv[:, 0:128], cos, sin).astype(BF16)
        v_ref[...] = kv[:, 128:256].astype(BF16)
        za_ref[...] = _dot_nt(xb, wp[512:1024, :])
        u_val = _dot_nt(xb, wt_ref[W_U:W_ZS, :])
        for k in range(N_SLAB):
            u_ref[k] = u_val[:, k * SLAB_IN:(k + 1) * SLAB_IN]
        zs_ref[...] = _dot_nt(xb, wt_ref[W_ZS:D_IN_PROJ, :])

    row = lambda w: pl.BlockSpec((tb, w), lambda i: (i, 0))
    return pl.pallas_call(
        body, name="proj", grid=(seq // tb,),
        in_specs=[row(D_MODEL), pl.BlockSpec((D_IN_PROJ, D_MODEL), lambda i: (0, 0), pipeline_mode=pl.Buffered(1)),
                  row(128), row(128)],
        out_specs=[pl.BlockSpec((8, tb, 128), lambda i: (0, i, 0)), row(128), row(128), row(512),
                   pl.BlockSpec((N_SLAB, tb, SLAB_IN), lambda i: (0, i, 0)), row(512)],
        out_shape=[jax.ShapeDtypeStruct((8, seq, 128), BF16), jax.ShapeDtypeStruct((seq, 128), BF16),
                   jax.ShapeDtypeStruct((seq, 128), BF16), jax.ShapeDtypeStruct((seq, 512), F32),
                   jax.ShapeDtypeStruct((N_SLAB, seq, SLAB_IN), F32), jax.ShapeDtypeStruct((seq, 512), F32)],
        scratch_shapes=[pltpu.VMEM((1024, D_MODEL), BF16)],
        compiler_params=_cparams(("arbitrary",)),
    )(x, wt, cos128, sin128)


ATT_TQ = 128
ATT_KEYS = 3 * ATT_TQ


def _attn_window(i, seq):
    start = jnp.clip(i * ATT_TQ - WINDOW, 0, seq - ATT_KEYS)
    return pl.multiple_of(start, ATT_TQ)


def _attn_bias():
    r = np.arange(ATT_TQ)[None, :, None]
    c = np.arange(ATT_KEYS)[None, None, :]
    off = (np.arange(3) * ATT_TQ)[:, None, None]
    return jnp.asarray(np.where(np.abs(r + off - c) <= WINDOW, 0.0, NEG_INF).astype(np.float32))


def _attn_bias_spec(nblk):
    pick = lambda i: jnp.where(i == 0, 0, jnp.where(i == nblk - 1, 2, 1))
    return pl.BlockSpec((None, ATT_TQ, ATT_KEYS), lambda i: (pick(i), 0, 0))


def _attn_softmax(q_ref, k_ref, v_ref, sink_ref, bias_ref, start):
    kw = k_ref[pl.ds(start, ATT_KEYS), :]
    vw = v_ref[pl.ds(start, ATT_KEYS), :]
    qall = q_ref[...].reshape(N_Q_HEADS * ATT_TQ, 128)
    s = (_dot_nt(qall, kw) * (HEAD_DIM ** -0.5)).reshape(N_Q_HEADS, ATT_TQ, ATT_KEYS) + bias_ref[...][None]
    tiles = [s[:, :, 128 * t:128 * (t + 1)] for t in range(ATT_KEYS // 128)]
    m = jnp.max(jnp.maximum(jnp.maximum(tiles[0], tiles[1]), tiles[2]), axis=2, keepdims=True)
    sink = sink_ref[...]
    m_b = jnp.maximum(jnp.broadcast_to(m, (N_Q_HEADS, ATT_TQ, 128)), sink)
    p = jnp.concatenate([jnp.exp(t - m_b) for t in tiles], axis=2)
    p_sink = jnp.exp(sink - m_b)
    lo_k = lax.broadcasted_iota(jnp.int32, (ATT_KEYS, 128), 1) < HEAD_DIM
    v_f = vw.astype(F32)
    v_lo, v_hi = jnp.where(lo_k, v_f, 1.0).astype(BF16), jnp.where(lo_k, 1.0, v_f).astype(BF16)
    pb = p.astype(BF16).reshape(N_Q_HEADS * ATT_TQ, ATT_KEYS)
    half = 4 * ATT_TQ
    r = jnp.concatenate([_dot(pb[:half], v_lo), _dot(pb[half:], v_hi)], axis=0).reshape(N_Q_HEADS, ATT_TQ, 128)
    return kw, vw, qall, p, p_sink, r


def _attn_fwd(q_stack, k, v, sink128, bias):
    seq = k.shape[0]

    def body(q_ref, k_ref, v_ref, sink_ref, bias_ref, o_ref):
        start = _attn_window(pl.program_id(0), seq)
        _, _, _, _, p_sink, r = _attn_softmax(q_ref, k_ref, v_ref, sink_ref, bias_ref, start)
        out = r / (pltpu.roll(r, HEAD_DIM, 2) + p_sink)
        lo = lax.broadcasted_iota(jnp.int32, (ATT_TQ, 128), 1) < HEAD_DIM
        for j in range(4):
            o_ref[:, 128 * j:128 * (j + 1)] = jnp.where(lo, out[j], out[4 + j])

    full = lambda w: pl.BlockSpec((seq, w), lambda i: (0, 0))
    return pl.pallas_call(
        body, name="attn_fwd", grid=(seq // ATT_TQ,),
        in_specs=[pl.BlockSpec((8, ATT_TQ, 128), lambda i: (0, i, 0)), full(128), full(128),
                  pl.BlockSpec((N_Q_HEADS, 1, 128), lambda i: (0, 0, 0)), _attn_bias_spec(seq // ATT_TQ)],
        out_specs=pl.BlockSpec((ATT_TQ, 512), lambda i: (i, 0)),
        out_shape=jax.ShapeDtypeStruct((seq, 512), F32),
        compiler_params=_cparams(("arbitrary",)),
    )(q_stack, k, v, sink128, bias)


def _attn_bwd(q_stack, k, v, sink128, bias, d_o):
    seq = k.shape[0]

    def body(q_ref, k_ref, v_ref, sink_ref, bias_ref, do_ref, dq_ref, dk_ref, dv_ref, dsink_ref, sink_acc):
        i = pl.program_id(0)

        @pl.when(i == 0)
        def _():
            dk_ref[...] = jnp.zeros_like(dk_ref)
            dv_ref[...] = jnp.zeros_like(dv_ref)
            sink_acc[...] = jnp.zeros_like(sink_acc)

        start = _attn_window(i, seq)
        kw, vw, qall, p, p_sink, r = _attn_softmax(q_ref, k_ref, v_ref, sink_ref, bias_ref, start)
        lo = lax.broadcasted_iota(jnp.int32, (ATT_TQ, 128), 1) < HEAD_DIM
        lo3 = lo[None]
        grp0 = lax.broadcasted_iota(jnp.int32, (N_Q_HEADS, ATT_TQ, 128), 0) < 4
        val = grp0 == lo3
        swapped = pltpu.roll(r, HEAD_DIM, 2)
        inv = 1.0 / (jnp.where(val, swapped, r) + p_sink)
        d_o_blk = do_ref[...]
        do3 = jnp.where(val, jnp.concatenate([d_o_blk[None, :, 128 * j:128 * (j + 1)] for j in range(4)] * 2, axis=0), 0.0)
        t = (do3 * r).reshape(N_Q_HEADS * ATT_TQ, 128)
        t_hi = t.astype(BF16)
        t_lo = (t - t_hi.astype(F32)).astype(BF16)
        ones = jnp.ones((128, 128), BF16)
        delta = (_dot(t_hi, ones) + _dot(t_lo, ones)).reshape(N_Q_HEADS, ATT_TQ, 128) * inv
        sink_acc[...] += -(p_sink * inv) * delta
        do_all = do3.astype(BF16).reshape(N_Q_HEADS * ATT_TQ, 128)
        dp = _dot_nt(do_all, vw).reshape(N_Q_HEADS, ATT_TQ, ATT_KEYS)
        probs, ds = [], []
        for tl in range(ATT_KEYS // 128):
            cols = slice(128 * tl, 128 * (tl + 1))
            probs_t = p[:, :, cols] * inv
            probs.append(probs_t.astype(BF16))
            ds.append((probs_t * (dp[:, :, cols] - delta)).astype(BF16))
        probs_all = jnp.concatenate(probs, axis=2).reshape(N_Q_HEADS * ATT_TQ, ATT_KEYS)
        ds_all = jnp.concatenate(ds, axis=2).reshape(N_Q_HEADS * ATT_TQ, ATT_KEYS)
        scale = HEAD_DIM ** -0.5
        dq_all = (_dot(ds_all, kw) * scale).reshape(N_Q_HEADS, ATT_TQ, 128)
        for j in range(4):
            dq_ref[:, 128 * j:128 * (j + 1)] = jnp.where(lo, dq_all[j], dq_all[4 + j])
        dk_ref[pl.ds(start, ATT_KEYS), :] += _dot_tn(ds_all, qall) * scale
        dv_ref[pl.ds(start, ATT_KEYS), :] += _dot_tn(probs_all, do_all)

        @pl.when(i == pl.num_programs(0) - 1)
        def _():
            dsink_ref[...] = jnp.sum(sink_acc[...], axis=1)

    full = lambda w: pl.BlockSpec((seq, w), lambda i: (0, 0))
    return pl.pallas_call(
        body, name="attn_bwd", grid=(seq // ATT_TQ,),
        in_specs=[pl.BlockSpec((8, ATT_TQ, 128), lambda i: (0, i, 0)), full(128), full(128),
                  pl.BlockSpec((N_Q_HEADS, 1, 128), lambda i: (0, 0, 0)),
                  _attn_bias_spec(seq // ATT_TQ), pl.BlockSpec((ATT_TQ, 512), lambda i: (i, 0))],
        out_specs=[pl.BlockSpec((ATT_TQ, 512), lambda i: (i, 0)), full(128), full(128),
                   pl.BlockSpec((N_Q_HEADS, 128), lambda i: (0, 0))],
        out_shape=[jax.ShapeDtypeStruct((seq, 512), F32), jax.ShapeDtypeStruct((seq, 128), F32),
                   jax.ShapeDtypeStruct((seq, 128), F32), jax.ShapeDtypeStruct((N_Q_HEADS, 128), F32)],
        scratch_shapes=[pltpu.VMEM((N_Q_HEADS, ATT_TQ, 128), F32)],
        compiler_params=_cparams(("arbitrary",)),
    )(q_stack, k, v, sink128, bias, d_o)


def _permute_rows(dst_ref, src_ref, sub_len):
    for k in range(N_SLAB):
        for j in range(sub_len):
            dst_ref[k, 8 * j:8 * (j + 1), :] = src_ref.at[k][pl.ds(j, SUBSEG, stride=sub_len), :]


def _unpermute_rows(dst_ref, src_ref, sub_len):
    for k in range(N_SLAB):
        for s in range(SUBSEG):
            dst_ref[k, s * sub_len:(s + 1) * sub_len, :] = src_ref.at[k][pl.ds(s, sub_len, stride=SUBSEG), :]


def _scan_chunk(br_ref, bi_ref, lr_row, li_row, init, cols, *, sub_len, reverse, store):
    lr = jnp.broadcast_to(lr_row[:, cols], (SUBSEG, SCAN_LANES))
    li = jnp.broadcast_to(li_row[:, cols], (SUBSEG, SCAN_LANES))
    if init is None:
        sr = si = jnp.zeros((SUBSEG, SCAN_LANES), F32)
    else:
        sr, si = init
    for jj in range(sub_len):
        rows = slice(SUBSEG * ((sub_len - 1 - jj) if reverse else jj), SUBSEG * (((sub_len - 1 - jj) if reverse else jj) + 1))
        sr, si = lr * sr - li * si + br_ref[rows, cols], lr * si + li * sr + bi_ref[rows, cols]
        if store:
            br_ref[rows, cols] = sr
            bi_ref[rows, cols] = si
    return sr, si


def _resolve_chunk(z, carry_refs, start_refs, pr_row, pi_row, cols, *, reverse):
    cr, ci = carry_refs[0][0:1, cols], carry_refs[1][0:1, cols]
    pr, pi = pr_row[:, cols], pi_row[:, cols]
    for s in (range(SUBSEG - 1, -1, -1) if reverse else range(SUBSEG)):
        start_refs[0][s:s + 1, cols] = cr
        start_refs[1][s:s + 1, cols] = ci
        cr, ci = pr * cr - pi * ci + z[0][s:s + 1, :], pr * ci + pi * cr + z[1][s:s + 1, :]
    carry_refs[0][0:1, cols] = cr
    carry_refs[1][0:1, cols] = ci


def _param_specs(direction):
    row = lambda q: pl.BlockSpec((None, None, 1, STATE_W), lambda i: (q, direction, 0, 0))
    wide = lambda q: pl.BlockSpec((None, None, N_SLAB, SLAB_IN, SLAB_ST), lambda i: (q, direction, 0, 0, 0))
    tall = lambda q: pl.BlockSpec((None, None, N_SLAB, SLAB_ST, SLAB_IN), lambda i: (q, direction, 0, 0, 0))
    return [row(q) for q in range(4)], [wide(0), wide(1)], [tall(0), tall(1)]


def _ssm_fwd(u, lam, bb, cb, *, direction, tb, name):
    reverse = direction == 1
    seq = u.shape[1]
    nblk = seq // tb
    sub_len = tb // SUBSEG

    def body(u_ref, lr_ref, li_ref, pr_ref, pi_ref, bbr_ref, bbi_ref, cbr_ref, cbi_ref,
             y_ref, sr_ref, si_ref, xr, xi, up, yp, car, cai):
        @pl.when(pl.program_id(0) == 0)
        def _():
            car[...] = jnp.zeros_like(car)
            cai[...] = jnp.zeros_like(cai)

        _permute_rows(up, u_ref, sub_len)
        lr, li, pr, pi = lr_ref[...], li_ref[...], pr_ref[...], pi_ref[...]
        chunk = lambda k: slice(k * SLAB_ST, (k + 1) * SLAB_ST)

        def drive(k):
            ub = up[k].astype(BF16)
            xr[:, chunk(k)] = _dot(ub, bbr_ref[k])
            xi[:, chunk(k)] = _dot(ub, bbi_ref[k])

        def scan(k):
            z = _scan_chunk(xr, xi, lr, li, None, chunk(k), sub_len=sub_len, reverse=reverse, store=False)
            _resolve_chunk(z, (car, cai), (sr_ref, si_ref), pr, pi, chunk(k), reverse=reverse)
            _scan_chunk(xr, xi, lr, li, (sr_ref[:, chunk(k)], si_ref[:, chunk(k)]), chunk(k),
                        sub_len=sub_len, reverse=reverse, store=True)

        def read_out(k):
            yp[k] = _dot(xr[:, chunk(k)].astype(BF16), cbr_ref[k]) - _dot(xi[:, chunk(k)].astype(BF16), cbi_ref[k])

        drive(0)
        for k in range(N_SLAB):
            if k + 1 < N_SLAB:
                drive(k + 1)
            scan(k)
            if k > 0:
                read_out(k - 1)
        read_out(N_SLAB - 1)
        _unpermute_rows(y_ref, yp, sub_len)

    blk = (lambda i: nblk - 1 - i) if reverse else (lambda i: i)
    rows, wide, tall = _param_specs(direction)
    tok = pl.BlockSpec((N_SLAB, tb, SLAB_IN), lambda i: (0, blk(i), 0))
    start_spec = pl.BlockSpec((None, SUBSEG, STATE_W), lambda i: (blk(i), 0, 0))
    return pl.pallas_call(
        body, name=name, grid=(nblk,),
        in_specs=[tok] + rows + wide + tall,
        out_specs=[tok, start_spec, start_spec],
        out_shape=[jax.ShapeDtypeStruct((N_SLAB, seq, SLAB_IN), F32), jax.ShapeDtypeStruct((nblk, SUBSEG, STATE_W), F32),
                   jax.ShapeDtypeStruct((nblk, SUBSEG, STATE_W), F32)],
        scratch_shapes=[pltpu.VMEM((tb, STATE_W), F32), pltpu.VMEM((tb, STATE_W), F32),
                        pltpu.VMEM((N_SLAB, tb, SLAB_IN), F32), pltpu.VMEM((N_SLAB, tb, SLAB_IN), F32),
                        pltpu.VMEM((SUBSEG, STATE_W), F32), pltpu.VMEM((SUBSEG, STATE_W), F32)],
        compiler_params=_cparams(("arbitrary",)),
    )(u, lam, lam, lam, lam, bb, bb, cb, cb)


def _ssm_bwd(u, dy, starts, lam, bb, bbt, cb_t, *, direction, tb, name):
    reverse = direction == 1
    seq = u.shape[1]
    nblk = seq // tb
    sub_len = tb // SUBSEG

    def body(u_ref, dy_ref, sr_ref, si_ref, lr_ref, li_ref, pr_ref, pi_ref, bbr_ref, bbi_ref, btr_ref, bti_ref,
             ctr_ref, cti_ref, du_ref, gb_ref, gc_ref, dl_ref,
             xr, xi, gr, gi, up, dyp, dup, gsr, gsi, car, cai):
        gbr_ref, gbi_ref = gb_ref.at[0], gb_ref.at[1]
        gcr_ref, gci_ref = gc_ref.at[0], gc_ref.at[1]
        dlr_ref, dli_ref = dl_ref.at[0], dl_ref.at[1]

        @pl.when(pl.program_id(0) == 0)
        def _():
            for ref in (car, cai, gbr_ref, gbi_ref, gcr_ref, gci_ref, dlr_ref, dli_ref):
                ref[...] = jnp.zeros_like(ref)

        _permute_rows(up, u_ref, sub_len)
        _permute_rows(dyp, dy_ref, sub_len)
        lr, li, pr, pi = lr_ref[...], li_ref[...], pr_ref[...], pi_ref[...]
        nli, npi = -li, -pi
        chunk = lambda k: slice(k * SLAB_ST, (k + 1) * SLAB_ST)

        def drive(k):
            ub = up[k].astype(BF16)
            xr[:, chunk(k)] = _dot(ub, bbr_ref[k])
            xi[:, chunk(k)] = _dot(ub, bbi_ref[k])
            dyb = dyp[k].astype(BF16)
            gr[:, chunk(k)] = _dot(dyb, ctr_ref[k])
            gi[:, chunk(k)] = -_dot(dyb, cti_ref[k])

        def scan_x(k):
            _scan_chunk(xr, xi, lr, li, (sr_ref[:, chunk(k)], si_ref[:, chunk(k)]), chunk(k),
                        sub_len=sub_len, reverse=reverse, store=True)

        def grad_c(k):
            dyb = dyp[k].astype(BF16)
            gcr_ref[k] += _dot_tn(xr[:, chunk(k)].astype(BF16), dyb)
            gci_ref[k] -= _dot_tn(xi[:, chunk(k)].astype(BF16), dyb)

        def scan_g(k):
            z = _scan_chunk(gr, gi, lr, nli, None, chunk(k), sub_len=sub_len, reverse=not reverse, store=False)
            _resolve_chunk(z, (car, cai), (gsr, gsi), pr, npi, chunk(k), reverse=not reverse)
            _scan_chunk(gr, gi, lr, nli, (gsr[:, chunk(k)], gsi[:, chunk(k)]), chunk(k),
                        sub_len=sub_len, reverse=not reverse, store=True)

        def grad_b_du(k):
            ub = up[k].astype(BF16)
            grb, gib = gr[:, chunk(k)].astype(BF16), gi[:, chunk(k)].astype(BF16)
            gbr_ref[k] += _dot_tn(ub, grb)
            gbi_ref[k] += _dot_tn(ub, gib)
            dup[k] = _dot(grb, btr_ref[k]) + _dot(gib, bti_ref[k])

        def grad_lambda(k):
            cols = chunk(k)
            acc_r, acc_i = dlr_ref[:, cols], dli_ref[:, cols]
            for jj in range(sub_len):
                prev = jj + 1 if reverse else jj - 1
                if 0 <= prev < sub_len:
                    x_r, x_i = xr[SUBSEG * prev:SUBSEG * (prev + 1), cols], xi[SUBSEG * prev:SUBSEG * (prev + 1), cols]
                else:
                    x_r, x_i = sr_ref[:, cols], si_ref[:, cols]
                g_r, g_i = gr[SUBSEG * jj:SUBSEG * (jj + 1), cols], gi[SUBSEG * jj:SUBSEG * (jj + 1), cols]
                acc_r = acc_r + (g_r * x_r + g_i * x_i)
                acc_i = acc_i + (g_i * x_r - g_r * x_i)
            dlr_ref[:, cols] = acc_r
            dli_ref[:, cols] = acc_i

        drive(0)
        for k in range(N_SLAB):
            scan_x(k)
            if k + 1 < N_SLAB:
                drive(k + 1)
            grad_c(k)
            scan_g(k)
            grad_b_du(k)
            grad_lambda(k)
        _unpermute_rows(du_ref, dup, sub_len)

    blk = (lambda i: i) if reverse else (lambda i: nblk - 1 - i)
    rows, wide, tall = _param_specs(direction)
    tok = pl.BlockSpec((N_SLAB, tb, SLAB_IN), lambda i: (0, blk(i), 0))
    start_spec = pl.BlockSpec((None, SUBSEG, STATE_W), lambda i: (blk(i), 0, 0))
    gb_shape, gc_shape, dl_shape = (2, N_SLAB, SLAB_IN, SLAB_ST), (2, N_SLAB, SLAB_ST, SLAB_IN), (2, SUBSEG, STATE_W)
    whole = lambda shape: pl.BlockSpec(shape, lambda i: (0,) * len(shape))
    big = lambda: pltpu.VMEM((tb, STATE_W), F32)
    slabs = lambda: pltpu.VMEM((N_SLAB, tb, SLAB_IN), F32)
    tile = lambda: pltpu.VMEM((SUBSEG, STATE_W), F32)
    return pl.pallas_call(
        body, name=name, grid=(nblk,),
        in_specs=[tok, tok, start_spec, start_spec] + rows + wide + tall + wide,
        out_specs=[tok, whole(gb_shape), whole(gc_shape), whole(dl_shape)],
        out_shape=[jax.ShapeDtypeStruct((N_SLAB, seq, SLAB_IN), F32), jax.ShapeDtypeStruct(gb_shape, F32),
                   jax.ShapeDtypeStruct(gc_shape, F32), jax.ShapeDtypeStruct(dl_shape, F32)],
        scratch_shapes=[big(), big(), big(), big(), slabs(), slabs(), slabs(), tile(), tile(), tile(), tile()],
        compiler_params=_cparams(("arbitrary",)),
    )(u, dy, *starts, lam, lam, lam, lam, bb, bb, bbt, bbt, cb_t, cb_t)


GELU_C = math.sqrt(2.0 / math.pi)
GELU_K = 0.044715
MID_ROW_GROUPS = 1


def _mid(o, za, u, y_f, y_b, zs, x, target, ssm_d, w_glu, b_glu, g_attn, g_ssm, w_out, ln_g, ln_b, tb):
    seq = x.shape[0]

    def body(o_ref, za_ref, u_ref, yf_ref, yb_ref, zs_ref, x_ref, t_ref, d_ref, wg_ref, bg_ref, ga_ref, gs_ref,
             wo_ref, lg_ref, lb_ref,
             loss_ref, do_ref, dza_ref, dyl_ref, dzs_ref, dpre_ref, gwo_ref, gwg_ref, vec_ref, wop):
        @pl.when(pl.program_id(0) == 0)
        def _():
            for ref in (loss_ref, gwo_ref, gwg_ref, vec_ref):
                ref[...] = jnp.zeros_like(ref)
            for nat, par in _pair_blocks(0):
                wop[par, :] = wo_ref[nat, :]
            wop[D_ATTN:, :] = wo_ref[D_ATTN:, :]

        def rows_of(rs):
            o, za = o_ref[rs, :], za_ref[rs, :]
            sig_a = _sigmoid(za)
            silu_a = za * sig_a
            ya = o * silu_a
            r_a = lax.rsqrt(jnp.mean(ya * ya, axis=1, keepdims=True) + NORM_EPS)
            n_a = ya * r_a
            g_a = ga_ref[...]
            unslab = lambda ref: jnp.concatenate([ref[k, rs, :] for k in range(N_SLAB)], axis=1)
            u_blk, zs = unslab(u_ref), zs_ref[rs, :]
            d_row = d_ref[...]
            ylin = d_row * u_blk + unslab(yf_ref) + unslab(yb_ref)
            inner = GELU_C * (ylin + GELU_K * ylin * ylin * ylin)
            th = jnp.tanh(inner)
            gl = 0.5 * ylin * (1.0 + th)
            glb = gl.astype(BF16)
            gate = _dot(glb, wg_ref[...])
            yield
            sg = _sigmoid(gate + bg_ref[...])
            y2 = gl * sg
            sig_s = _sigmoid(zs)
            silu_s = zs * sig_s
            ys = y2 * silu_s
            r_s = lax.rsqrt(jnp.mean(ys * ys, axis=1, keepdims=True) + NORM_EPS)
            n_s = ys * r_s
            g_s = gs_ref[...]
            mixed = jnp.concatenate([n_a * g_a, n_s * g_s], axis=1).astype(BF16)
            out = _dot(mixed, wop[...])
            yield
            pre = ALPHA * x_ref[rs, :] + out
            mu = jnp.mean(pre, axis=1, keepdims=True)
            cen = pre - mu
            rstd = lax.rsqrt(jnp.mean(cen * cen, axis=1, keepdims=True) + NORM_EPS)
            hhat = cen * rstd
            ln_g = lg_ref[...]
            err = hhat * ln_g + lb_ref[...] - t_ref[rs, :]
            loss_ref[...] += 0.5 * jnp.sum(jnp.mean(err * err, axis=1, keepdims=True))

            dh = err * (1.0 / D_MODEL)
            vec_ref[0:1, :] += jnp.sum(dh * hhat, axis=0, keepdims=True)
            vec_ref[1:2, :] += jnp.sum(dh, axis=0, keepdims=True)
            dhh = dh * ln_g
            dpre = rstd * (dhh - jnp.mean(dhh, axis=1, keepdims=True)
                           - hhat * jnp.mean(dhh * hhat, axis=1, keepdims=True))
            dpre_ref[rs, :] = dpre
            dpb = dpre.astype(BF16)
            for j in range(4):
                g_pair = _dot_tn(mixed[:, 128 * j:128 * (j + 1)], dpb)
                for g in range(2):
                    nat = HEAD_DIM * (4 * g + j)
                    gwo_ref[nat:nat + HEAD_DIM, :] += g_pair[HEAD_DIM * g:HEAD_DIM * (g + 1), :]
            gwo_ref[D_ATTN:, :] += _dot_tn(mixed[:, D_ATTN:], dpb)
            dmix = _dot_nt(dpb, wop[...])
            yield
            dna = dmix[:, :D_ATTN]
            vec_ref[2:3, 0:D_ATTN] += jnp.sum(dna * n_a, axis=0, keepdims=True)
            dna = dna * g_a
            dya = r_a * (dna - n_a * jnp.mean(dna * n_a, axis=1, keepdims=True))
            do_ref[rs, :] = dya * silu_a
            dza_ref[rs, :] = dya * o * (sig_a * (1.0 + za * (1.0 - sig_a)))
            dns = dmix[:, D_ATTN:]
            vec_ref[2:3, D_ATTN:] += jnp.sum(dns * n_s, axis=0, keepdims=True)
            dns = dns * g_s
            dys = r_s * (dns - n_s * jnp.mean(dns * n_s, axis=1, keepdims=True))
            dzs_ref[rs, :] = dys * y2 * (sig_s * (1.0 + zs * (1.0 - sig_s)))
            dy2 = dys * silu_s
            da = dy2 * gl * sg * (1.0 - sg)
            vec_ref[3:4, D_SSM:] += jnp.sum(da, axis=0, keepdims=True)
            dab = da.astype(BF16)
            gwg_ref[...] += _dot_tn(glb, dab)
            dgl_mm = _dot_nt(dab, wg_ref[...])
            yield
            dgl = dy2 * sg + dgl_mm
            dylin = dgl * (0.5 * (1.0 + th)
                           + 0.5 * ylin * (1.0 - th * th) * GELU_C * (1.0 + 3.0 * GELU_K * ylin * ylin))
            for k in range(N_SLAB):
                dyl_ref[k, rs, :] = dylin[:, k * SLAB_IN:(k + 1) * SLAB_IN]
            vec_ref[3:4, 0:D_SSM] += jnp.sum(dylin * u_blk, axis=0, keepdims=True)
            yield

        groups = [rows_of(slice(r0, r0 + tb // MID_ROW_GROUPS)) for r0 in range(0, tb, tb // MID_ROW_GROUPS)]
        for _ in range(5):
            for gen in groups:
                next(gen)

    tok = lambda w: pl.BlockSpec((tb, w), lambda i: (i, 0))
    slab = pl.BlockSpec((N_SLAB, tb, SLAB_IN), lambda i: (0, i, 0))
    const = lambda r, c: pl.BlockSpec((r, c), lambda i: (0, 0))
    tok_shape = jax.ShapeDtypeStruct((seq, 512), F32)
    return pl.pallas_call(
        body, name="mid", grid=(seq // tb,),
        in_specs=[tok(512), tok(512), slab, slab, slab, tok(512), tok(1024), tok(1024),
                  const(1, 512), const(512, 512), const(1, 512), const(1, 512), const(1, 512),
                  const(1024, 1024), const(1, 1024), const(1, 1024)],
        out_specs=[const(8, 128), tok(512), tok(512), slab, tok(512), tok(1024),
                   const(1024, 1024), const(512, 512), const(8, 1024)],
        out_shape=[jax.ShapeDtypeStruct((8, 128), F32), tok_shape, tok_shape,
                   jax.ShapeDtypeStruct((N_SLAB, seq, SLAB_IN), F32), tok_shape,
                   jax.ShapeDtypeStruct((seq, 1024), F32), jax.ShapeDtypeStruct((1024, 1024), F32),
                   jax.ShapeDtypeStruct((512, 512), F32), jax.ShapeDtypeStruct((8, 1024), F32)],
        scratch_shapes=[pltpu.VMEM((D_MODEL, D_MODEL), BF16)],
        compiler_params=_cparams(("arbitrary",)),
    )(o, za, u, y_f, y_b, zs, x, target, ssm_d, w_glu, b_glu, g_attn, g_ssm, w_out, ln_g, ln_b)


def _proj_bwd(x, dq, dk, dv, dza, du_f, du_b, dylin, dzs, dpre, ssm_d, cos128, sin128, wt, tb):
    seq = x.shape[0]

    def body(x_ref, dq_ref, dk_ref, dv_ref, dza_ref, duf_ref, dub_ref, dyl_ref, dzs_ref, dpre_ref, d_ref,
             cos_ref, sin_ref, wt_ref, gx_ref, gw_ref, wp):
        @pl.when(pl.program_id(0) == 0)
        def _():
            gw_ref[...] = jnp.zeros_like(gw_ref)
            for base in (W_Q, W_ZA):
                for nat, par in _pair_blocks(base):
                    wp[par, :] = wt_ref[nat, :]
            wp[W_KV:W_ZA, :] = wt_ref[W_KV:W_ZA, :]
            wp[W_U:, :] = wt_ref[W_U:, :]

        cos, sin = cos_ref[...], sin_ref[...]

        def unrope(t):
            return t * cos + _rotate_half_unsigned(t * sin)

        dq_rot = dq_ref[...]
        pieces = [unrope(dq_rot[:, 128 * j:128 * (j + 1)]) for j in range(4)]
        d_row = d_ref[...]
        pieces += [unrope(dk_ref[...]), dv_ref[...], dza_ref[...]]
        pieces += [duf_ref[k] + dub_ref[k] + d_row[:, k * SLAB_IN:(k + 1) * SLAB_IN] * dyl_ref[k] for k in range(N_SLAB)]
        pieces += [dzs_ref[...]]
        dproj = jnp.concatenate(pieces, axis=1).astype(BF16)
        gx_ref[...] = ALPHA * dpre_ref[...] + _dot(dproj, wp[...])
        xb = x_ref[...].astype(BF16)
        for base in (W_Q, W_ZA):
            for j in range(4):
                g_pair = _dot_tn(dproj[:, base + 128 * j:base + 128 * (j + 1)], xb)
                for g in range(2):
                    nat = base + HEAD_DIM * (4 * g + j)
                    gw_ref[nat:nat + HEAD_DIM, :] += g_pair[HEAD_DIM * g:HEAD_DIM * (g + 1), :]
        gw_ref[W_KV:W_ZA, :] += _dot_tn(dproj[:, W_KV:W_ZA], xb)
        gw_ref[W_U:, :] += _dot_tn(dproj[:, W_U:], xb)

    tok = lambda w: pl.BlockSpec((tb, w), lambda i: (i, 0))
    slab = pl.BlockSpec((N_SLAB, tb, SLAB_IN), lambda i: (0, i, 0))
    const = lambda r, c: pl.BlockSpec((r, c), lambda i: (0, 0))
    whole = pl.BlockSpec((D_IN_PROJ, D_MODEL), lambda i: (0, 0), pipeline_mode=pl.Buffered(1))
    return pl.pallas_call(
        body, name="proj_bwd", grid=(seq // tb,),
        in_specs=[tok(1024), tok(512), tok(128), tok(128), tok(512), slab, slab, slab, tok(512), tok(1024),
                  const(1, 512), tok(128), tok(128), whole],
        out_specs=[tok(1024), whole],
        out_shape=[jax.ShapeDtypeStruct((seq, D_MODEL), F32), jax.ShapeDtypeStruct((D_IN_PROJ, D_MODEL), F32)],
        scratch_shapes=[pltpu.VMEM((D_IN_PROJ, D_MODEL), BF16)],
        compiler_params=_cparams(("arbitrary",)),
    )(x, dq, dk, dv, dza, du_f, du_b, dylin, dzs, dpre, ssm_d, cos128, sin128, wt)


def _adamw(w, g, m, v, name):
    rows, cols = w.shape
    tb = rows
    while tb * cols * 4 > ADAMW_BLOCK_BYTES and tb % 16 == 0:
        tb //= 2

    def body(w_ref, g_ref, m_ref, v_ref, d_ref, nm_ref, nv_ref):
        _adamw_update(w_ref, g_ref, m_ref, v_ref, d_ref, nm_ref, nv_ref)

    spec = pl.BlockSpec((tb, cols), lambda i: (i, 0))
    return pl.pallas_call(
        body, name=name, grid=(rows // tb,), in_specs=[spec] * 4, out_specs=[spec] * 3,
        out_shape=[jax.ShapeDtypeStruct((rows, cols), F32)] * 3,
        compiler_params=_cparams(("arbitrary",)),
    )(w, g, m, v)


def _adamw_update(w_ref, g_ref, m_ref, v_ref, d_ref, nm_ref, nv_ref):
    g_blk = g_ref[...]
    m_new = ADAM_B1 * m_ref[...] + (1.0 - ADAM_B1) * g_blk
    v_new = ADAM_B2 * v_ref[...] + (1.0 - ADAM_B2) * (g_blk * g_blk)
    m_hat = m_new / (1.0 - ADAM_B1 ** ADAM_STEP)
    v_hat = v_new / (1.0 - ADAM_B2 ** ADAM_STEP)
    d_ref[...] = -ADAM_LR * (m_hat / (jnp.sqrt(v_hat) + ADAM_EPS) + ADAM_WD * w_ref[...])
    nm_ref[...] = m_new
    nv_ref[...] = v_new


def _adamw_many(groups, name):
    n = len(groups)

    def body(*refs):
        for p in range(n):
            _adamw_update(*refs[4 * p:4 * p + 4], *refs[4 * n + 3 * p:4 * n + 3 * p + 3])

    return pl.pallas_call(
        body, name=name,
        out_shape=[jax.ShapeDtypeStruct(grp[0].shape, F32) for grp in groups for _ in range(3)],
    )(*[a for grp in groups for a in grp])


_WEIGHTS = ["w_in", "attn_sink", "ssm_a_re", "ssm_a_im", "ssm_log_dt", "ssm_b_re", "ssm_b_im", "ssm_c_re", "ssm_c_im",
            "ssm_d", "w_glu", "b_glu", "norm_attn_g", "norm_ssm_g", "w_out", "ln_g", "ln_b"]
N_DG = N_DIR * N_GROUPS
BIG_ROWS = N_DG * SSM_CH * SSM_STATE // 128
TINY_ROWS = 64


def _pack_small_grads(g_bc, g_vec, g_ar, g_ai, g_dt, g_sink, loss):
    big = jnp.stack([t.reshape(BIG_ROWS, 128) for t in g_bc])
    row = lambda t: jnp.pad(t.reshape(1, -1), ((0, 0), (0, 128 - t.size)))
    tiny = jnp.concatenate([g_vec.reshape(64, 128), g_ar.reshape(32, 128), g_ai.reshape(32, 128), row(g_dt), row(g_sink),
                            row(loss), jnp.zeros((N_CHIPS * TINY_ROWS - 131, 128), F32)], axis=0)
    return jnp.concatenate([big, tiny.reshape(N_CHIPS, TINY_ROWS, 128)], axis=1)


def _unpack_small_grads(packed):
    big = packed[:, :BIG_ROWS].reshape(N_CHIPS, 2 * BIG_ROWS, SSM_STATE)
    tiny = packed[:, BIG_ROWS:].reshape(N_CHIPS * TINY_ROWS, 128)
    g_vec = tiny[0:64].reshape(8, 1024)
    return tiny[130, 0], {
        "ssm_b_re": big[0], "ssm_b_im": big[1], "ssm_c_re": big[2], "ssm_c_im": big[3],
        "ln_g": g_vec[0:1], "ln_b": g_vec[1:2],
        "norm_attn_g": _from_pair_order(g_vec[2:3, :D_ATTN]), "norm_ssm_g": g_vec[2:3, D_ATTN:],
        "ssm_d": g_vec[3:4, :D_SSM], "b_glu": g_vec[3:4, D_SSM:],
        "ssm_a_re": tiny[64:96].reshape(N_DG, SSM_STATE), "ssm_a_im": tiny[96:128].reshape(N_DG, SSM_STATE),
        "ssm_log_dt": tiny[128:129, :N_DG].reshape(N_DIR, N_GROUPS), "attn_sink": tiny[129:130, :N_Q_HEADS],
    }


def _small_view(name, t):
    if name in ("ssm_b_re", "ssm_b_im"):
        return jnp.swapaxes(t[0], 2, 3).reshape(N_DG * SSM_CH, SSM_STATE)
    if name in ("ssm_c_re", "ssm_c_im"):
        return t.reshape(N_DG * SSM_CH, SSM_STATE)
    if name in ("ssm_a_re", "ssm_a_im"):
        return t.reshape(N_DG, SSM_STATE)
    if name == "ssm_log_dt":
        return t.reshape(N_DIR, N_GROUPS)
    return t.reshape(1, -1)


def _small_unview(name, t, shape):
    if name in ("ssm_b_re", "ssm_b_im"):
        return jnp.swapaxes(t.reshape(N_DIR, N_GROUPS, SSM_CH, SSM_STATE), 2, 3).reshape(shape)
    return t.reshape(shape)


def kernel(x, w_in, attn_sink, ssm_a_re, ssm_a_im, ssm_log_dt, ssm_b_re, ssm_b_im, ssm_c_re, ssm_c_im, ssm_d, w_glu, b_glu, norm_attn_g, norm_ssm_g, w_out, ln_g, ln_b, loss_target, m_w_in, m_attn_sink, m_ssm_a_re, m_ssm_a_im, m_ssm_log_dt, m_ssm_b_re, m_ssm_b_im, m_ssm_c_re, m_ssm_c_im, m_ssm_d, m_w_glu, m_b_glu, m_norm_attn_g, m_norm_ssm_g, m_w_out, m_ln_g, m_ln_b, v_w_in, v_attn_sink, v_ssm_a_re, v_ssm_a_im, v_ssm_log_dt, v_ssm_b_re, v_ssm_b_im, v_ssm_c_re, v_ssm_c_im, v_ssm_d, v_w_glu, v_b_glu, v_norm_attn_g, v_norm_ssm_g, v_w_out, v_ln_g, v_ln_b):
    args = dict(locals())
    weights = {n: args[n] for n in _WEIGHTS}
    mom_m = {n: args["m_" + n] for n in _WEIGHTS}
    mom_v = {n: args["v_" + n] for n in _WEIGHTS}
    xs = x[0]
    target = loss_target[0]

    wt_g, w_glu_g, w_out_g = _all_gather_chips([w_in[0].T, w_glu[0], w_out[0]], BF16, "gather_weights")
    wt_full = wt_g.reshape(D_IN_PROJ, D_MODEL)
    w_glu_full = w_glu_g.reshape(D_SSM, D_SSM)
    w_out_full = w_out_g.reshape(D_MODEL, D_MODEL)

    g_x, g_wt, g_w_out, g_w_glu, g_small = _local_step(
        xs, target, wt_full, w_glu_full, w_out_full, attn_sink, ssm_a_re, ssm_a_im, ssm_log_dt, ssm_b_re, ssm_b_im,
        ssm_c_re, ssm_c_im, ssm_d, b_glu, norm_attn_g, norm_ssm_g, ln_g, ln_b)

    r_w_glu, r_w_out, r_wt, g_small_all = _reduce_all(
        [g_w_glu.reshape(N_CHIPS, -1, D_SSM), g_w_out.reshape(N_CHIPS, -1, D_MODEL), g_wt.reshape(N_CHIPS, -1, D_MODEL),
         g_small], [True, True, True, False], "reduce_grads")
    loss, small_grads = _unpack_small_grads(g_small_all)

    grads, deltas, new_m, new_v = {}, {}, {}, {}
    d_w, m_w, v_w = _adamw(w_in[0].T, r_wt, m_w_in[0].T, v_w_in[0].T, "adamw_w_in")
    grads["w_in"], deltas["w_in"], new_m["w_in"], new_v["w_in"] = r_wt.T[None], d_w.T[None], m_w.T[None], v_w.T[None]
    for n, g in (("w_out", r_w_out), ("w_glu", r_w_glu)):
        d_w, m_w, v_w = _adamw(weights[n][0], g, mom_m[n][0], mom_v[n][0], "adamw_" + n)
        grads[n], deltas[n], new_m[n], new_v[n] = g[None], d_w[None], m_w[None], v_w[None]
    names = sorted(small_grads)
    updates = _adamw_many([(_small_view(n, weights[n]), small_grads[n], _small_view(n, mom_m[n]), _small_view(n, mom_v[n]))
                           for n in names], "adamw_small")
    for i, n in enumerate(names):
        shape = weights[n].shape
        grads[n] = _small_unview(n, small_grads[n], shape)
        deltas[n], new_m[n], new_v[n] = (_small_unview(n, t, shape) for t in updates[3 * i:3 * i + 3])

    return (loss, g_x[None], *[grads[n] for n in _WEIGHTS], *[deltas[n] for n in _WEIGHTS],
            *[new_m[n] for n in _WEIGHTS], *[new_v[n] for n in _WEIGHTS])


def _local_step(xs, target, wt_full, w_glu_full, w_out_full, attn_sink, ssm_a_re, ssm_a_im, ssm_log_dt, ssm_b_re,
                ssm_b_im, ssm_c_re, ssm_c_im, ssm_d, b_glu, norm_attn_g, norm_ssm_g, ln_g, ln_b):
    seq = xs.shape[0]

    a_r, a_i = _small_view("ssm_a_re", ssm_a_re), _small_view("ssm_a_im", ssm_a_im)
    log_dt = ssm_log_dt.reshape(N_DG, 1)
    b_r, b_i = _small_view("ssm_b_re", ssm_b_re), _small_view("ssm_b_im", ssm_b_im)
    c_r, c_i = _small_view("ssm_c_re", ssm_c_re), _small_view("ssm_c_im", ssm_c_im)
    ssm_tb = min(SSM_BLOCK, seq)
    sub_len = ssm_tb // SUBSEG
    lam, bb, bbt, cb, cb_t = _ssm_params_fwd(a_r, a_i, log_dt, b_r, b_i, c_r, c_i, int(math.log2(sub_len)))
    lam = lam.reshape(4, N_DIR, 1, STATE_W)

    cos128, sin128 = _rope_tables(seq)
    q_stack, k_rot, v_bf, z_attn, u, z_ssm = _proj(xs, wt_full, cos128, sin128, min(1024, seq))
    sink128 = jnp.broadcast_to(attn_sink[0][:, None, None], (N_Q_HEADS, 1, 128))
    attn_bias = _attn_bias()
    o = _attn_fwd(q_stack, k_rot, v_bf, sink128, attn_bias)
    ys, starts = [], []
    for d in range(N_DIR):
        y_d, s_r, s_i = _ssm_fwd(u, lam, bb, cb, direction=d, tb=ssm_tb, name=f"ssm_fwd_{d}")
        ys.append(y_d)
        starts.append((s_r, s_i))

    row = lambda t: t.reshape(1, -1)
    g_attn_p = _to_pair_order(norm_attn_g)
    loss_blk, d_o, d_za, d_ylin, d_zs, d_pre, g_w_out, g_w_glu, g_vec = _mid(
        o, z_attn, u, ys[0], ys[1], z_ssm, xs, target, row(ssm_d), w_glu_full, row(b_glu),
        g_attn_p, row(norm_ssm_g), w_out_full, row(ln_g), row(ln_b), min(256, seq))

    dq, dk, dv, g_sink = _attn_bwd(q_stack, k_rot, v_bf, sink128, attn_bias, d_o)
    dus, g_bb, g_cb, g_lam = [], [], [], []
    for d in range(N_DIR):
        du_d, gb_d, gc_d, dl_d = _ssm_bwd(u, d_ylin, starts[d], lam, bb, bbt, cb_t, direction=d, tb=ssm_tb,
                                          name=f"ssm_bwd_{d}")
        dus.append(du_d)
        g_bb.append(gb_d)
        g_cb.append(gc_d)
        g_lam.append(dl_d)
    g_ar, g_ai, g_dt, g_br, g_bi, g_cr, g_ci = _ssm_params_bwd(a_r, a_i, log_dt, b_r, b_i, g_bb, g_cb, g_lam)

    g_x, g_wt = _proj_bwd(xs, dq, dk, dv, d_za, dus[0], dus[1], d_ylin, d_zs, d_pre, row(ssm_d), cos128, sin128,
                          wt_full, min(512, seq))

    g_small = _pack_small_grads([g_br, g_bi, g_cr, g_ci], g_vec, g_ar, g_ai, g_dt, g_sink[:, 0], loss_blk[0, 0])
    return g_x, g_wt, g_w_out, g_w_glu, g_small
```

```python
import functools
import math

import numpy as np
import jax
import jax.numpy as jnp
from jax import lax
from jax.experimental import pallas as pl
from jax.experimental.pallas import tpu as pltpu

F32 = jnp.float32
BF16 = jnp.bfloat16
MESH = pl.DeviceIdType.MESH

D_MODEL = 1024
D_ATTN = 512
D_SSM = 512
HEAD_DIM = 64
N_Q_HEADS = 8
WINDOW = 128
ROPE_THETA = 10000.0
SSM_CH = 16
N_GROUPS = 32
SSM_STATE = 64
N_DIR = 2
STATE_W = N_GROUPS * SSM_STATE
N_SLAB = 4
SLAB_IN = 128
SLAB_ST = 512
NORM_EPS = 1e-5
NEG_INF = -1e30
ALPHA = 2.0 ** 0.25
D_IN_PROJ = 2304
N_CHIPS = 4

ADAM_LR = 0.001
ADAM_B1 = 0.9
ADAM_B2 = 0.999
ADAM_EPS = 1e-08
ADAM_WD = 0.01
ADAM_STEP = 10

SUBSEG = 8
SCAN_LANES = 512
SSM_BLOCK = 512
VMEM_LIMIT = 48 * 1024 * 1024
ADAMW_BLOCK_BYTES = 3 * 512 * 1024

def _to_pair_order(row):
    return jnp.transpose(row.reshape(2, 4, HEAD_DIM), (1, 0, 2)).reshape(1, D_ATTN)


def _from_pair_order(row):
    return jnp.transpose(row.reshape(4, 2, HEAD_DIM), (1, 0, 2)).reshape(1, D_ATTN)


def _cparams(sem=None):
    return pltpu.CompilerParams(dimension_semantics=sem, vmem_limit_bytes=VMEM_LIMIT)


def _dot(a, b):
    return jnp.dot(a, b, preferred_element_type=F32)


def _dot_nt(a, b):
    return lax.dot_general(a, b, (((1,), (1,)), ((), ())), preferred_element_type=F32)


def _dot_tn(a, b):
    return lax.dot_general(a, b, (((0,), (0,)), ((), ())), preferred_element_type=F32)


def _sigmoid(z):
    return 1.0 / (1.0 + jnp.exp(-z))


def _all_gather_chips(shards, out_dtype, name):
    n = len(shards)

    def body(*refs):
        start, relay, finish = _gather_phases(refs[:n], refs[n:2 * n], *refs[2 * n:], out_dtype)
        start()
        relay()
        finish()

    vmem = pl.BlockSpec(memory_space=pltpu.VMEM)
    return pl.pallas_call(
        body, name=name,
        out_shape=[jax.ShapeDtypeStruct((N_CHIPS,) + s.shape, out_dtype) for s in shards],
        in_specs=[vmem] * n, out_specs=[vmem] * n,
        scratch_shapes=_gather_sems(n),
        compiler_params=pltpu.CompilerParams(vmem_limit_bytes=VMEM_LIMIT),
    )(*shards)


def _gather_sems(n):
    return [pltpu.SemaphoreType.DMA((6 * n,)), pltpu.SemaphoreType.DMA((6 * n,))]


def _gather_phases(in_refs, out_refs, send_sems, recv_sems, out_dtype):
    n = len(in_refs)
    x, y, c = lax.axis_index("x"), lax.axis_index("y"), lax.axis_index("c")
    sibling = (x, y, 1 - c)
    chips = [(1 - x, y), (x, 1 - y), (1 - x, 1 - y)]

    def half_of(a, px, py, half):
        rows = in_refs[a].shape[0] // 2
        return out_refs[a].at[2 * px + py, pl.ds(half * rows, rows), :]

    def copy(a, k, px, py, half, to):
        blk = half_of(a, px, py, half)
        return pltpu.make_async_remote_copy(src_ref=blk, dst_ref=blk, send_sem=send_sems.at[6 * a + k],
                                            recv_sem=recv_sems.at[6 * a + k], device_id=to, device_id_type=MESH)

    first = [copy(a, j, x, y, c, (*chips[j], c)) for a in range(n) for j in range(3)]
    passed = [copy(a, 3 + j, *chips[j], c, sibling) for a in range(n) for j in range(3)]

    def start():
        for a in range(n):
            out_refs[a][2 * x + y] = in_refs[a][...].astype(out_dtype)
        for cp in first:
            cp.start()

    def relay():
        for a in range(n):
            for j in range(3):
                copy(a, j, *chips[j], c, (x, y, c)).wait_recv()
                passed[3 * a + j].start()

    def finish():
        for a in range(n):
            for j in range(3):
                copy(a, 3 + j, *chips[j], 1 - c, (x, y, c)).wait_recv()
        for cp in first + passed:
            cp.wait_send()

    return start, relay, finish


SEMS_PER_ARRAY = 11


def _reduce_all(pieces, narrow, name):
    n = len(pieces)
    shapes = [p.shape for p in pieces]

    def body(*refs):
        start, exchange, combine, finish = _reduce_phases(
            refs[:n], refs[n:2 * n], refs[2 * n:3 * n], refs[3 * n:4 * n], refs[4 * n:5 * n], *refs[5 * n:],
            narrow, gather_last=True)
        start()
        exchange()
        combine()
        finish()

    vmem = pl.BlockSpec(memory_space=pltpu.VMEM)
    return pl.pallas_call(
        body, name=name,
        out_shape=[jax.ShapeDtypeStruct(p.shape[1:] if a < n - 1 else p.shape, F32) for a, p in enumerate(pieces)],
        in_specs=[vmem] * n, out_specs=[vmem] * n,
        scratch_shapes=_reduce_scratch(shapes, narrow),
        compiler_params=pltpu.CompilerParams(vmem_limit_bytes=VMEM_LIMIT),
    )(*pieces)


def _reduce_scratch(shapes, narrow):
    half = [(N_CHIPS, s[1] // 2, s[2]) for s in shapes]
    wire = [BF16 if nar else F32 for nar in narrow]
    n = len(shapes)
    return ([pltpu.VMEM(half[a], F32) for a in range(n)] + [pltpu.VMEM(half[a], wire[a]) for a in range(n)]
            + [pltpu.VMEM(half[a], wire[a]) for a in range(n)]
            + [pltpu.SemaphoreType.DMA((SEMS_PER_ARRAY * n,)), pltpu.SemaphoreType.DMA((SEMS_PER_ARRAY * n,))])


def _reduce_phases(p_refs, out_refs, a_refs, s_refs, b_refs, send_sems, recv_sems, narrow, gather_last):
    n = len(p_refs)
    halves = [p.shape[1] // 2 for p in p_refs]
    wire = [BF16 if nar else F32 for nar in narrow]
    x, y, c = lax.axis_index("x"), lax.axis_index("y"), lax.axis_index("c")
    me = 2 * x + y
    sibling = (x, y, 1 - c)
    chips = [(1 - x, y), (x, 1 - y), (1 - x, 1 - y)]
    slot = [2 * px + py for px, py in chips]
    last = n - 1

    def copy(a, k, src, dst, to):
        return pltpu.make_async_remote_copy(src_ref=src, dst_ref=dst, send_sem=send_sems.at[SEMS_PER_ARRAY * a + k],
                                            recv_sem=recv_sems.at[SEMS_PER_ARRAY * a + k],
                                            device_id=to, device_id_type=MESH)

    def rows(a, half):
        return pl.ds(pl.multiple_of(half * halves[a], 16), halves[a])

    def finished(a, k, half):
        if gather_last and a == last:
            return out_refs[a].at[k, rows(a, half), :]
        return out_refs[a].at[rows(a, half), :]

    swaps = [copy(a, 0, p_refs[a].at[:, rows(a, 1 - c), :], a_refs[a], sibling) for a in range(n)]
    sends = [[copy(a, 1 + j, s_refs[a].at[slot[j]], b_refs[a].at[me], (*chips[j], c)) for j in range(3)] for a in range(n)]
    backs = [copy(a, 4, finished(a, me, c), finished(a, me, c), sibling) for a in range(n)]
    spread = [copy(last, 5 + j, finished(last, me, c), finished(last, me, c), (*chips[j], c)) for j in range(3)]
    relays = [copy(last, 8 + j, finished(last, slot[j], c), finished(last, slot[j], c), sibling) for j in range(3)]

    def start():
        for cp in swaps:
            cp.start()

    def exchange():
        for a in range(n):
            swaps[a].wait_recv()
            for k in range(N_CHIPS):
                acc = a_refs[a][k] + p_refs[a][k, rows(a, c), :]
                a_refs[a][k] = acc
                s_refs[a][k] = acc.astype(wire[a])
            b_refs[a][me] = s_refs[a][me]
            for cp in sends[a]:
                cp.start()

    def combine():
        for a in range(n):
            for j in range(3):
                copy(a, 1 + j, s_refs[a].at[slot[j]], b_refs[a].at[slot[j]], (x, y, c)).wait_recv()
            terms = [jnp.where(me == k, a_refs[a][k], b_refs[a][k].astype(F32)) for k in range(N_CHIPS)]
            total = (terms[0] + terms[1]) + (terms[2] + terms[3])
            if gather_last and a == last:
                out_refs[a][me, rows(a, c), :] = total
            else:
                out_refs[a][rows(a, c), :] = total
            backs[a].start()
        if gather_last:
            for cp in spread:
                cp.start()

    def finish():
        if gather_last:
            for j in range(3):
                copy(last, 5 + j, finished(last, slot[j], c), finished(last, slot[j], c), (x, y, c)).wait_recv()
                relays[j].start()
        for a in range(n):
            copy(a, 4, finished(a, me, 1 - c), finished(a, me, 1 - c), (x, y, c)).wait_recv()
        if gather_last:
            for j in range(3):
                copy(last, 8 + j, finished(last, slot[j], 1 - c), finished(last, slot[j], 1 - c), (x, y, c)).wait_recv()
        for cp in swaps + [cp for group in sends for cp in group] + backs + (spread + relays if gather_last else []):
            cp.wait_send()

    return start, exchange, combine, finish


def _ssm_param_values(ar, ai, logdt):
    dt = jnp.exp(logdt)
    mag = jnp.exp(dt * ar)
    cs, sn = jnp.cos(dt * ai), jnp.sin(dt * ai)
    lr, li = mag * cs, mag * sn
    den = ar * ar + ai * ai
    nr = (lr - 1.0) * ar + li * ai
    ni = li * ar - (lr - 1.0) * ai
    return dt, mag, lr, li, den, nr, ni


GROUPS_PER_SLAB = N_GROUPS // N_SLAB


def _slab_masks():
    def eq(shape, f_row, f_col):
        return (f_row(lax.broadcasted_iota(jnp.int32, shape, 0)) == f_col(lax.broadcasted_iota(jnp.int32, shape, 1))).astype(F32)
    spread = eq((SSM_STATE, SLAB_ST), lambda r: r, lambda c: c % SSM_STATE)
    spread_t = eq((SLAB_ST, SSM_STATE), lambda r: r % SSM_STATE, lambda c: c)
    keep = eq((SLAB_IN, SLAB_ST), lambda r: r // SSM_CH, lambda c: c // SSM_STATE)
    keep_t = eq((SLAB_ST, SLAB_IN), lambda r: r // SSM_STATE, lambda c: c // SSM_CH)
    repeat = eq((N_DG * SSM_CH, N_DG), lambda r: r // SSM_CH, lambda c: c)
    return spread, spread_t, keep, keep_t, repeat


def _split3(t):
    hi = t.astype(BF16)
    rest = t - hi.astype(F32)
    mid = rest.astype(BF16)
    return hi, mid, (rest - mid.astype(F32)).astype(BF16)


def _select(dot, ones01, t, ones_first):
    o = ones01.astype(BF16)
    parts = [dot(o, p) if ones_first else dot(p, o) for p in _split3(t)]
    return (parts[0] + parts[1]) + parts[2]


def _ssm_params_fwd(ar, ai, logdt, br, bi, cr, ci, n_square):
    def body(ar_ref, ai_ref, dt_ref, br_ref, bi_ref, cr_ref, ci_ref, lam_ref, bb_ref, bbt_ref, cb_ref, cbt_ref):
        _, _, lr, li, den, nr, ni = _ssm_param_values(ar_ref[...], ai_ref[...], dt_ref[...])
        lam_ref[0] = lr
        lam_ref[1] = li
        pr, pi = lr, li
        for _ in range(n_square):
            pr, pi = pr * pr - pi * pi, 2.0 * pr * pi
        lam_ref[2] = pr
        lam_ref[3] = pi
        spread, spread_t, keep, keep_t, repeat = _slab_masks()
        fr = _select(_dot, repeat, nr / den, True)
        fi = _select(_dot, repeat, ni / den, True)
        b_r, b_i = br_ref[...], bi_ref[...]
        bbar = (fr * b_r - fi * b_i, fr * b_i + fi * b_r)
        c_par = (cr_ref[...], ci_ref[...])
        spread, spread_t = spread.astype(BF16), spread_t.astype(BF16)
        for src, wide_ref, tall_ref in ((bbar, bb_ref, bbt_ref), (c_par, cbt_ref, cb_ref)):
            for q in range(2):
                for d in range(N_DIR):
                    for k in range(N_SLAB):
                        r0 = (d * N_GROUPS + k * GROUPS_PER_SLAB) * SSM_CH
                        blk = src[q][r0:r0 + SLAB_IN].astype(BF16)
                        wide_ref[q, d, k] = (_dot(blk, spread) * keep).astype(BF16)
                        tall_ref[q, d, k] = (_dot_nt(spread_t, blk) * keep_t).astype(BF16)

    wide = jax.ShapeDtypeStruct((2, N_DIR, N_SLAB, SLAB_IN, SLAB_ST), BF16)
    tall = jax.ShapeDtypeStruct((2, N_DIR, N_SLAB, SLAB_ST, SLAB_IN), BF16)
    return pl.pallas_call(body, name="ssm_params_fwd",
                          out_shape=[jax.ShapeDtypeStruct((4,) + ar.shape, F32), wide, tall, tall, wide],
                          compiler_params=pltpu.CompilerParams(vmem_limit_bytes=VMEM_LIMIT),
                          )(ar, ai, logdt, br, bi, cr, ci)


def _ssm_params_bwd(ar, ai, logdt, br, bi, g_slabs_b, g_slabs_c, g_lam):
    def body(ar_ref, ai_ref, dt_ref, br_ref, bi_ref, gb0_ref, gb1_ref, gc0_ref, gc1_ref, gl0_ref, gl1_ref,
             gar_ref, gai_ref, gdt_ref, gbr_ref, gbi_ref, gcr_ref, gci_ref, dbb, dlam):
        spread, spread_t, keep, keep_t, repeat = _slab_masks()
        for d, (gb_ref, gc_ref) in enumerate(((gb0_ref, gc0_ref), (gb1_ref, gc1_ref))):
            for q in range(2):
                for k in range(N_SLAB):
                    r0 = (d * N_GROUPS + k * GROUPS_PER_SLAB) * SSM_CH
                    dbb[q, r0:r0 + SLAB_IN, :] = _select(_dot, spread_t, gb_ref[q, k] * keep, False)
                    out_ref = gcr_ref if q == 0 else gci_ref
                    out_ref[r0:r0 + SLAB_IN, :] = _select(_dot_tn, spread_t, gc_ref[q, k] * keep_t, False)
        grp = (lax.broadcasted_iota(jnp.int32, (N_GROUPS, STATE_W), 0)
               == lax.broadcasted_iota(jnp.int32, (N_GROUPS, STATE_W), 1) // SSM_STATE).astype(F32)
        pick = (lax.broadcasted_iota(jnp.int32, (STATE_W, SSM_STATE), 0) % SSM_STATE
                == lax.broadcasted_iota(jnp.int32, (STATE_W, SSM_STATE), 1)).astype(F32)
        for d, gl_ref in enumerate((gl0_ref, gl1_ref)):
            for q in range(2):
                row = jnp.sum(gl_ref[q], axis=0, keepdims=True)
                dlam[q, d * N_GROUPS:(d + 1) * N_GROUPS, :] = _select(_dot, pick, grp * row, False)

        a_r, a_i = ar_ref[...], ai_ref[...]
        dt, mag, lr, li, den, nr, ni = _ssm_param_values(a_r, a_i, dt_ref[...])
        fr = _select(_dot, repeat, nr / den, True)
        fi = _select(_dot, repeat, ni / den, True)
        b_r, b_i = br_ref[...], bi_ref[...]
        g_r, g_i = dbb[0], dbb[1]
        gbr_ref[...] = fr * g_r + fi * g_i
        gbi_ref[...] = fr * g_i - fi * g_r
        d_fr = _select(_dot_tn, repeat, b_r * g_r + b_i * g_i, True)
        d_fi = _select(_dot_tn, repeat, b_r * g_i - b_i * g_r, True)
        d_nr, d_ni = d_fr / den, d_fi / den
        d_den = -(d_fr * nr + d_fi * ni) / (den * den)
        d_lr = dlam[0] + d_nr * a_r - d_ni * a_i
        d_li = dlam[1] + d_nr * a_i + d_ni * a_r
        d_ar = d_nr * (lr - 1.0) + d_ni * li + d_den * 2.0 * a_r
        d_ai = d_nr * li - d_ni * (lr - 1.0) + d_den * 2.0 * a_i
        d_mag = (d_lr * lr + d_li * li) / mag
        d_theta = d_li * lr - d_lr * li
        gar_ref[...] = d_ar + d_mag * mag * dt
        gai_ref[...] = d_ai + d_theta * dt
        d_dt = d_mag * mag * a_r + d_theta * a_i
        gdt_ref[...] = jnp.sum(d_dt, axis=1, keepdims=True) * dt

    small = jax.ShapeDtypeStruct(ar.shape, F32)
    big = jax.ShapeDtypeStruct(br.shape, F32)
    return pl.pallas_call(
        body, name="ssm_params_bwd",
        out_shape=[small, small, jax.ShapeDtypeStruct(logdt.shape, F32), big, big, big, big],
        scratch_shapes=[pltpu.VMEM((2,) + br.shape, F32), pltpu.VMEM((2,) + ar.shape, F32)],
        compiler_params=pltpu.CompilerParams(vmem_limit_bytes=VMEM_LIMIT),
    )(ar, ai, logdt, br, bi, *g_slabs_b, *g_slabs_c, *g_lam)


ROPE_GROUP = 128


def _rope_tables(seq):
    half = HEAD_DIM // 2
    inv_freq = jnp.tile(ROPE_THETA ** (-jnp.arange(half, dtype=F32) / half), 4)
    sign = jnp.tile(jnp.concatenate([-jnp.ones((half,), F32), jnp.ones((half,), F32)]), 2)

    def table(pos):
        ang = pos.astype(F32)[:, None] * inv_freq[None, :]
        return jnp.stack([jnp.cos(ang), jnp.sin(ang), sign * jnp.sin(ang)])

    return table(jnp.arange(seq // ROPE_GROUP) * ROPE_GROUP), table(jnp.arange(ROPE_GROUP))


def _rope_block(hi_ref, lo_ref, first_group, n_groups):
    cl, sl, sl_s = lo_ref[0], lo_ref[1], lo_ref[2]
    cos, sin = [], []
    for g in range(n_groups):
        ch, sh, sh_s = (hi_ref[q, pl.ds(first_group + g, 1), :] for q in range(3))
        cos.append(ch * cl - sh * sl)
        sin.append(sh_s * cl + ch * sl_s)
    return jnp.concatenate(cos, axis=0), jnp.concatenate(sin, axis=0)


def _rotate_half_unsigned(t):
    lane = lax.broadcasted_iota(jnp.int32, t.shape, 1)
    return jnp.where((lane % HEAD_DIM) < HEAD_DIM // 2, pltpu.roll(t, 96, 1), pltpu.roll(t, 32, 1))


def _rope(t, cos, sin_signed):
    return t * cos + _rotate_half_unsigned(t) * sin_signed


def _pair_blocks(base):
    out = []
    for j in range(4):
        for g in range(2):
            nat = base + HEAD_DIM * (4 * g + j)
            par = base + 128 * j + HEAD_DIM * g
            out.append((slice(nat, nat + HEAD_DIM), slice(par, par + HEAD_DIM)))
    return out


W_Q, W_KV, W_ZA, W_U, W_ZS = 0, 512, 768, 1280, 1792


def _proj(x, wt, rope_hi, rope_lo, shards, tb):
    seq = x.shape[0]
    steps = seq // tb
    n_sh = len(shards)

    def body(*refs):
        x_ref, wt_ref, hi_ref, lo_ref = refs[:4]
        shard_refs = refs[4:4 + n_sh]
        q_ref, k_ref, v_ref, za_ref, u_ref, zs_ref = refs[4 + n_sh:10 + n_sh]
        gathered_refs = refs[10 + n_sh:10 + 2 * n_sh]
        wp = refs[10 + 2 * n_sh]
        step = pl.program_id(0)
        if n_sh:
            landing_refs = refs[11 + 2 * n_sh:11 + 3 * n_sh]
            start, relay, finish = _gather_phases(shard_refs, landing_refs, *refs[11 + 3 * n_sh:], BF16)
            pl.when(step == 0)(start)
            pl.when(step == max(steps - 2, 0))(relay)

        @pl.when(step == 0)
        def _():
            for dst_base, src_base in ((0, W_Q), (512, W_ZA)):
                for nat, par in _pair_blocks(0):
                    wp[dst_base + par.start:dst_base + par.stop, :] = wt_ref[src_base + nat.start:src_base + nat.stop, :]

        xb = x_ref[...].astype(BF16)
        cos, sin = _rope_block(hi_ref, lo_ref, pl.program_id(0) * (tb // ROPE_GROUP), tb // ROPE_GROUP)
        lo = lax.broadcasted_iota(jnp.int32, (tb, 128), 1) < HEAD_DIM
        q = _dot_nt(xb, wp[0:512, :])
        for j in range(4):
            qj = _rope(q[:, 128 * j:128 * (j + 1)], cos, sin)
            q_ref[j] = jnp.where(lo, qj, 0.0).astype(BF16)
            q_ref[4 + j] = jnp.where(lo, 0.0, qj).astype(BF16)
        kv = _dot_nt(xb, wt_ref[W_KV:W_ZA, :])
        k_ref[...] = _rope(kv[:, 0:128], cos, sin).astype(BF16)
        v_ref[...] = kv[:, 128:256].astype(BF16)
        za_ref[...] = _dot_nt(xb, wp[512:1024, :])
        u_val = _dot_nt(xb, wt_ref[W_U:W_ZS, :])
        for k in range(N_SLAB):
            u_ref[k] = u_val[:, k * SLAB_IN:(k + 1) * SLAB_IN]
        zs_ref[...] = _dot_nt(xb, wt_ref[W_ZS:D_IN_PROJ, :])
        if n_sh:
            @pl.when(step == steps - 1)
            def _():
                finish()
                for a in range(n_sh):
                    gathered_refs[a][...] = landing_refs[a][...]

    row = lambda w: pl.BlockSpec((tb, w), lambda i: (i, 0))
    table = lambda t: pl.BlockSpec(t.shape, lambda i: (0, 0, 0))
    vmem = pl.BlockSpec(memory_space=pltpu.VMEM)
    return pl.pallas_call(
        body, name="proj", grid=(steps,),
        in_specs=[row(D_MODEL), pl.BlockSpec((D_IN_PROJ, D_MODEL), lambda i: (0, 0), pipeline_mode=pl.Buffered(1)),
                  table(rope_hi), table(rope_lo)] + [vmem] * n_sh,
        out_specs=[pl.BlockSpec((8, tb, 128), lambda i: (0, i, 0)), row(128), row(128), row(512),
                   pl.BlockSpec((N_SLAB, tb, SLAB_IN), lambda i: (0, i, 0)), row(512)] + [vmem] * n_sh,
        out_shape=[jax.ShapeDtypeStruct((8, seq, 128), BF16), jax.ShapeDtypeStruct((seq, 128), BF16),
                   jax.ShapeDtypeStruct((seq, 128), BF16), jax.ShapeDtypeStruct((seq, 512), F32),
                   jax.ShapeDtypeStruct((N_SLAB, seq, SLAB_IN), F32), jax.ShapeDtypeStruct((seq, 512), F32)]
        + [jax.ShapeDtypeStruct((N_CHIPS,) + s.shape, BF16) for s in shards],
        scratch_shapes=[pltpu.VMEM((1024, D_MODEL), BF16)] + [pltpu.VMEM((N_CHIPS,) + s.shape, BF16) for s in shards]
        + (_gather_sems(n_sh) if n_sh else []),
        compiler_params=_cparams(("arbitrary",)),
    )(x, wt, rope_hi, rope_lo, *shards)


ATT_TQ = 128
ATT_KEYS = 3 * ATT_TQ


def _attn_window(i, seq):
    start = jnp.clip(i * ATT_TQ - WINDOW, 0, seq - ATT_KEYS)
    return pl.multiple_of(start, ATT_TQ)


def _attn_bias():
    r = np.arange(ATT_TQ)[None, :, None]
    c = np.arange(ATT_KEYS)[None, None, :]
    off = (np.arange(3) * ATT_TQ)[:, None, None]
    return jnp.asarray(np.where(np.abs(r + off - c) <= WINDOW, 0.0, NEG_INF).astype(np.float32))


def _attn_bias_spec(nblk):
    pick = lambda i: jnp.where(i == 0, 0, jnp.where(i == nblk - 1, 2, 1))
    return pl.BlockSpec((None, ATT_TQ, ATT_KEYS), lambda i: (pick(i), 0, 0))


def _attn_softmax(q_ref, k_ref, v_ref, sink_ref, bias_ref, start):
    kw = k_ref[pl.ds(start, ATT_KEYS), :]
    vw = v_ref[pl.ds(start, ATT_KEYS), :]
    qall = q_ref[...].reshape(N_Q_HEADS * ATT_TQ, 128)
    s = (_dot_nt(qall, kw) * (HEAD_DIM ** -0.5)).reshape(N_Q_HEADS, ATT_TQ, ATT_KEYS) + bias_ref[...][None]
    tiles = [s[:, :, 128 * t:128 * (t + 1)] for t in range(ATT_KEYS // 128)]
    m = jnp.max(jnp.maximum(jnp.maximum(tiles[0], tiles[1]), tiles[2]), axis=2, keepdims=True)
    sink = sink_ref[...]
    m_b = jnp.maximum(jnp.broadcast_to(m, (N_Q_HEADS, ATT_TQ, 128)), sink)
    p = jnp.concatenate([jnp.exp(t - m_b) for t in tiles], axis=2)
    p_sink = jnp.exp(sink - m_b)
    lo_k = lax.broadcasted_iota(jnp.int32, (ATT_KEYS, 128), 1) < HEAD_DIM
    v_f = vw.astype(F32)
    v_lo, v_hi = jnp.where(lo_k, v_f, 1.0).astype(BF16), jnp.where(lo_k, 1.0, v_f).astype(BF16)
    pb = p.astype(BF16).reshape(N_Q_HEADS * ATT_TQ, ATT_KEYS)
    half = 4 * ATT_TQ
    r = jnp.concatenate([_dot(pb[:half], v_lo), _dot(pb[half:], v_hi)], axis=0).reshape(N_Q_HEADS, ATT_TQ, 128)
    return kw, vw, qall, p, p_sink, r


def _attn_fwd(q_stack, k, v, sink128, bias):
    seq = k.shape[0]

    def body(q_ref, k_ref, v_ref, sink_ref, bias_ref, o_ref):
        start = _attn_window(pl.program_id(0), seq)
        _, _, _, _, p_sink, r = _attn_softmax(q_ref, k_ref, v_ref, sink_ref, bias_ref, start)
        out = r / (pltpu.roll(r, HEAD_DIM, 2) + p_sink)
        lo = lax.broadcasted_iota(jnp.int32, (ATT_TQ, 128), 1) < HEAD_DIM
        for j in range(4):
            o_ref[:, 128 * j:128 * (j + 1)] = jnp.where(lo, out[j], out[4 + j])

    full = lambda w: pl.BlockSpec((seq, w), lambda i: (0, 0))
    return pl.pallas_call(
        body, name="attn_fwd", grid=(seq // ATT_TQ,),
        in_specs=[pl.BlockSpec((8, ATT_TQ, 128), lambda i: (0, i, 0)), full(128), full(128),
                  pl.BlockSpec((N_Q_HEADS, 1, 128), lambda i: (0, 0, 0)), _attn_bias_spec(seq // ATT_TQ)],
        out_specs=pl.BlockSpec((ATT_TQ, 512), lambda i: (i, 0)),
        out_shape=jax.ShapeDtypeStruct((seq, 512), F32),
        compiler_params=_cparams(("arbitrary",)),
    )(q_stack, k, v, sink128, bias)


def _attn_bwd(q_stack, k, v, sink128, bias, d_o, pieces):
    seq = k.shape[0]
    steps = seq // ATT_TQ
    n_p = len(pieces)

    def body(*refs):
        q_ref, k_ref, v_ref, sink_ref, bias_ref, do_ref = refs[:6]
        piece_refs = refs[6:6 + n_p]
        dq_ref, dk_ref, dv_ref, dsink_ref = refs[6 + n_p:10 + n_p]
        reduced_refs = refs[10 + n_p:10 + 2 * n_p]
        sink_acc = refs[10 + 2 * n_p]
        i = pl.program_id(0)
        if n_p:
            landing_refs = refs[11 + 2 * n_p:11 + 3 * n_p]
            scratch = refs[11 + 3 * n_p:]
            begin, exchange, combine, finish = _reduce_phases(
                piece_refs, landing_refs, scratch[:n_p], scratch[n_p:2 * n_p], scratch[2 * n_p:3 * n_p],
                *scratch[3 * n_p:], [True] * n_p, gather_last=False)
            pl.when(i == 0)(begin)
            pl.when(i == min(2, steps - 1))(exchange)
            pl.when(i == (3 * steps) // 4)(combine)

        @pl.when(i == 0)
        def _():
            dk_ref[...] = jnp.zeros_like(dk_ref)
            dv_ref[...] = jnp.zeros_like(dv_ref)
            sink_acc[...] = jnp.zeros_like(sink_acc)

        start = _attn_window(i, seq)
        kw, vw, qall, p, p_sink, r = _attn_softmax(q_ref, k_ref, v_ref, sink_ref, bias_ref, start)
        lo = lax.broadcasted_iota(jnp.int32, (ATT_TQ, 128), 1) < HEAD_DIM
        lo3 = lo[None]
        grp0 = lax.broadcasted_iota(jnp.int32, (N_Q_HEADS, ATT_TQ, 128), 0) < 4
        val = grp0 == lo3
        swapped = pltpu.roll(r, HEAD_DIM, 2)
        inv = 1.0 / (jnp.where(val, swapped, r) + p_sink)
        d_o_blk = do_ref[...]
        do3 = jnp.where(val, jnp.concatenate([d_o_blk[None, :, 128 * j:128 * (j + 1)] for j in range(4)] * 2, axis=0), 0.0)
        t = (do3 * r).reshape(N_Q_HEADS * ATT_TQ, 128)
        t_hi = t.astype(BF16)
        t_lo = (t - t_hi.astype(F32)).astype(BF16)
        ones = jnp.ones((128, 128), BF16)
        delta = (_dot(t_hi, ones) + _dot(t_lo, ones)).reshape(N_Q_HEADS, ATT_TQ, 128) * inv
        sink_acc[...] += -(p_sink * inv) * delta
        do_all = do3.astype(BF16).reshape(N_Q_HEADS * ATT_TQ, 128)
        dp = _dot_nt(do_all, vw).reshape(N_Q_HEADS, ATT_TQ, ATT_KEYS)
        probs, ds = [], []
        for tl in range(ATT_KEYS // 128):
            cols = slice(128 * tl, 128 * (tl + 1))
            probs_t = p[:, :, cols] * inv
            probs.append(probs_t.astype(BF16))
            ds.append((probs_t * (dp[:, :, cols] - delta)).astype(BF16))
        probs_all = jnp.concatenate(probs, axis=2).reshape(N_Q_HEADS * ATT_TQ, ATT_KEYS)
        ds_all = jnp.concatenate(ds, axis=2).reshape(N_Q_HEADS * ATT_TQ, ATT_KEYS)
        scale = HEAD_DIM ** -0.5
        dq_all = (_dot(ds_all, kw) * scale).reshape(N_Q_HEADS, ATT_TQ, 128)
        for j in range(4):
            dq_ref[:, 128 * j:128 * (j + 1)] = jnp.where(lo, dq_all[j], dq_all[4 + j])
        dk_ref[pl.ds(start, ATT_KEYS), :] += _dot_tn(ds_all, qall) * scale
        dv_ref[pl.ds(start, ATT_KEYS), :] += _dot_tn(probs_all, do_all)

        @pl.when(i == steps - 1)
        def _():
            dsink_ref[...] = jnp.sum(sink_acc[...], axis=1)

        if n_p:
            @pl.when(i == steps - 1)
            def _():
                finish()
                for a in range(n_p):
                    reduced_refs[a][...] = landing_refs[a][...]

    full = lambda w: pl.BlockSpec((seq, w), lambda i: (0, 0))
    vmem = pl.BlockSpec(memory_space=pltpu.VMEM)
    return pl.pallas_call(
        body, name="attn_bwd", grid=(steps,),
        in_specs=[pl.BlockSpec((8, ATT_TQ, 128), lambda i: (0, i, 0)), full(128), full(128),
                  pl.BlockSpec((N_Q_HEADS, 1, 128), lambda i: (0, 0, 0)),
                  _attn_bias_spec(steps), pl.BlockSpec((ATT_TQ, 512), lambda i: (i, 0))] + [vmem] * n_p,
        out_specs=[pl.BlockSpec((ATT_TQ, 512), lambda i: (i, 0)), full(128), full(128),
                   pl.BlockSpec((N_Q_HEADS, 128), lambda i: (0, 0))] + [vmem] * n_p,
        out_shape=[jax.ShapeDtypeStruct((seq, 512), F32), jax.ShapeDtypeStruct((seq, 128), F32),
                   jax.ShapeDtypeStruct((seq, 128), F32), jax.ShapeDtypeStruct((N_Q_HEADS, 128), F32)]
        + [jax.ShapeDtypeStruct(p.shape[1:], F32) for p in pieces],
        scratch_shapes=[pltpu.VMEM((N_Q_HEADS, ATT_TQ, 128), F32)] + [pltpu.VMEM(p.shape[1:], F32) for p in pieces]
        + (_reduce_scratch([p.shape for p in pieces], [True] * n_p) if n_p else []),
        compiler_params=_cparams(("arbitrary",)),
    )(q_stack, k, v, sink128, bias, d_o, *pieces)


def _permute_rows(dst_ref, src_ref, sub_len):
    for k in range(N_SLAB):
        for j in range(sub_len):
            dst_ref[k, 8 * j:8 * (j + 1), :] = src_ref.at[k][pl.ds(j, SUBSEG, stride=sub_len), :]


def _unpermute_rows(dst_ref, src_ref, sub_len):
    for k in range(N_SLAB):
        for s in range(SUBSEG):
            dst_ref[k, s * sub_len:(s + 1) * sub_len, :] = src_ref.at[k][pl.ds(s, sub_len, stride=SUBSEG), :]


def _scan_chunk(br_ref, bi_ref, lr_row, li_row, init, cols, *, sub_len, reverse, store):
    lr = jnp.broadcast_to(lr_row[:, cols], (SUBSEG, SCAN_LANES))
    li = jnp.broadcast_to(li_row[:, cols], (SUBSEG, SCAN_LANES))
    if init is None:
        sr = si = jnp.zeros((SUBSEG, SCAN_LANES), F32)
    else:
        sr, si = init
    for jj in range(sub_len):
        rows = slice(SUBSEG * ((sub_len - 1 - jj) if reverse else jj), SUBSEG * (((sub_len - 1 - jj) if reverse else jj) + 1))
        sr, si = lr * sr - li * si + br_ref[rows, cols], lr * si + li * sr + bi_ref[rows, cols]
        if store:
            br_ref[rows, cols] = sr
            bi_ref[rows, cols] = si
    return sr, si


def _resolve_chunk(z, carry_refs, start_refs, pr_row, pi_row, cols, *, reverse):
    cr, ci = carry_refs[0][0:1, cols], carry_refs[1][0:1, cols]
    pr, pi = pr_row[:, cols], pi_row[:, cols]
    for s in (range(SUBSEG - 1, -1, -1) if reverse else range(SUBSEG)):
        start_refs[0][s:s + 1, cols] = cr
        start_refs[1][s:s + 1, cols] = ci
        cr, ci = pr * cr - pi * ci + z[0][s:s + 1, :], pr * ci + pi * cr + z[1][s:s + 1, :]
    carry_refs[0][0:1, cols] = cr
    carry_refs[1][0:1, cols] = ci


def _param_specs(direction):
    row = lambda q: pl.BlockSpec((None, None, 1, STATE_W), lambda i: (q, direction, 0, 0))
    wide = lambda q: pl.BlockSpec((None, None, N_SLAB, SLAB_IN, SLAB_ST), lambda i: (q, direction, 0, 0, 0))
    tall = lambda q: pl.BlockSpec((None, None, N_SLAB, SLAB_ST, SLAB_IN), lambda i: (q, direction, 0, 0, 0))
    return [row(q) for q in range(4)], [wide(0), wide(1)], [tall(0), tall(1)]


def _ssm_fwd(u, lam, bb, cb, *, direction, tb, name):
    reverse = direction == 1
    seq = u.shape[1]
    nblk = seq // tb
    sub_len = tb // SUBSEG

    def body(u_ref, lr_ref, li_ref, pr_ref, pi_ref, bbr_ref, bbi_ref, cbr_ref, cbi_ref,
             y_ref, sr_ref, si_ref, xr, xi, up, yp, car, cai):
        @pl.when(pl.program_id(0) == 0)
        def _():
            car[...] = jnp.zeros_like(car)
            cai[...] = jnp.zeros_like(cai)

        _permute_rows(up, u_ref, sub_len)
        lr, li, pr, pi = lr_ref[...], li_ref[...], pr_ref[...], pi_ref[...]
        chunk = lambda k: slice(k * SLAB_ST, (k + 1) * SLAB_ST)

        def drive(k):
            ub = up[k].astype(BF16)
            xr[:, chunk(k)] = _dot(ub, bbr_ref[k])
            xi[:, chunk(k)] = _dot(ub, bbi_ref[k])

        def scan(k):
            z = _scan_chunk(xr, xi, lr, li, None, chunk(k), sub_len=sub_len, reverse=reverse, store=False)
            _resolve_chunk(z, (car, cai), (sr_ref, si_ref), pr, pi, chunk(k), reverse=reverse)
            _scan_chunk(xr, xi, lr, li, (sr_ref[:, chunk(k)], si_ref[:, chunk(k)]), chunk(k),
                        sub_len=sub_len, reverse=reverse, store=True)

        def read_out(k):
            yp[k] = _dot(xr[:, chunk(k)].astype(BF16), cbr_ref[k]) - _dot(xi[:, chunk(k)].astype(BF16), cbi_ref[k])

        drive(0)
        for k in range(N_SLAB):
            if k + 1 < N_SLAB:
                drive(k + 1)
            scan(k)
            if k > 0:
                read_out(k - 1)
        read_out(N_SLAB - 1)
        _unpermute_rows(y_ref, yp, sub_len)

    blk = (lambda i: nblk - 1 - i) if reverse else (lambda i: i)
    rows, wide, tall = _param_specs(direction)
    tok = pl.BlockSpec((N_SLAB, tb, SLAB_IN), lambda i: (0, blk(i), 0))
    start_spec = pl.BlockSpec((None, SUBSEG, STATE_W), lambda i: (blk(i), 0, 0))
    return pl.pallas_call(
        body, name=name, grid=(nblk,),
        in_specs=[tok] + rows + wide + tall,
        out_specs=[tok, start_spec, start_spec],
        out_shape=[jax.ShapeDtypeStruct((N_SLAB, seq, SLAB_IN), F32), jax.ShapeDtypeStruct((nblk, SUBSEG, STATE_W), F32),
                   jax.ShapeDtypeStruct((nblk, SUBSEG, STATE_W), F32)],
        scratch_shapes=[pltpu.VMEM((tb, STATE_W), F32), pltpu.VMEM((tb, STATE_W), F32),
                        pltpu.VMEM((N_SLAB, tb, SLAB_IN), F32), pltpu.VMEM((N_SLAB, tb, SLAB_IN), F32),
                        pltpu.VMEM((SUBSEG, STATE_W), F32), pltpu.VMEM((SUBSEG, STATE_W), F32)],
        compiler_params=_cparams(("arbitrary",)),
    )(u, lam, lam, lam, lam, bb, bb, cb, cb)


def _ssm_bwd(u, dy, starts, lam, bb, bbt, cb_t, *, direction, tb, name):
    reverse = direction == 1
    seq = u.shape[1]
    nblk = seq // tb
    sub_len = tb // SUBSEG

    def body(u_ref, dy_ref, sr_ref, si_ref, lr_ref, li_ref, pr_ref, pi_ref, bbr_ref, bbi_ref, btr_ref, bti_ref,
             ctr_ref, cti_ref, du_ref, gb_ref, gc_ref, dl_ref,
             xr, xi, gr, gi, up, dyp, dup, gsr, gsi, car, cai):
        gbr_ref, gbi_ref = gb_ref.at[0], gb_ref.at[1]
        gcr_ref, gci_ref = gc_ref.at[0], gc_ref.at[1]
        dlr_ref, dli_ref = dl_ref.at[0], dl_ref.at[1]

        @pl.when(pl.program_id(0) == 0)
        def _():
            for ref in (car, cai, gbr_ref, gbi_ref, gcr_ref, gci_ref, dlr_ref, dli_ref):
                ref[...] = jnp.zeros_like(ref)

        _permute_rows(up, u_ref, sub_len)
        _permute_rows(dyp, dy_ref, sub_len)
        lr, li, pr, pi = lr_ref[...], li_ref[...], pr_ref[...], pi_ref[...]
        nli, npi = -li, -pi
        chunk = lambda k: slice(k * SLAB_ST, (k + 1) * SLAB_ST)

        def drive(k):
            ub = up[k].astype(BF16)
            xr[:, chunk(k)] = _dot(ub, bbr_ref[k])
            xi[:, chunk(k)] = _dot(ub, bbi_ref[k])
            dyb = dyp[k].astype(BF16)
            gr[:, chunk(k)] = _dot(dyb, ctr_ref[k])
            gi[:, chunk(k)] = -_dot(dyb, cti_ref[k])

        def scan_x(k):
            _scan_chunk(xr, xi, lr, li, (sr_ref[:, chunk(k)], si_ref[:, chunk(k)]), chunk(k),
                        sub_len=sub_len, reverse=reverse, store=True)

        def grad_c(k):
            dyb = dyp[k].astype(BF16)
            gcr_ref[k] += _dot_tn(xr[:, chunk(k)].astype(BF16), dyb)
            gci_ref[k] -= _dot_tn(xi[:, chunk(k)].astype(BF16), dyb)

        def scan_g(k):
            z = _scan_chunk(gr, gi, lr, nli, None, chunk(k), sub_len=sub_len, reverse=not reverse, store=False)
            _resolve_chunk(z, (car, cai), (gsr, gsi), pr, npi, chunk(k), reverse=not reverse)
            _scan_chunk(gr, gi, lr, nli, (gsr[:, chunk(k)], gsi[:, chunk(k)]), chunk(k),
                        sub_len=sub_len, reverse=not reverse, store=True)

        def grad_b_du(k):
            ub = up[k].astype(BF16)
            grb, gib = gr[:, chunk(k)].astype(BF16), gi[:, chunk(k)].astype(BF16)
            gbr_ref[k] += _dot_tn(ub, grb)
            gbi_ref[k] += _dot_tn(ub, gib)
            dup[k] = _dot(grb, btr_ref[k]) + _dot(gib, bti_ref[k])

        def grad_lambda(k):
            cols = chunk(k)
            acc_r, acc_i = dlr_ref[:, cols], dli_ref[:, cols]
            for jj in range(sub_len):
                prev = jj + 1 if reverse else jj - 1
                if 0 <= prev < sub_len:
                    x_r, x_i = xr[SUBSEG * prev:SUBSEG * (prev + 1), cols], xi[SUBSEG * prev:SUBSEG * (prev + 1), cols]
                else:
                    x_r, x_i = sr_ref[:, cols], si_ref[:, cols]
                g_r, g_i = gr[SUBSEG * jj:SUBSEG * (jj + 1), cols], gi[SUBSEG * jj:SUBSEG * (jj + 1), cols]
                acc_r = acc_r + (g_r * x_r + g_i * x_i)
                acc_i = acc_i + (g_i * x_r - g_r * x_i)
            dlr_ref[:, cols] = acc_r
            dli_ref[:, cols] = acc_i

        drive(0)
        for k in range(N_SLAB):
            scan_x(k)
            if k + 1 < N_SLAB:
                drive(k + 1)
            grad_c(k)
            scan_g(k)
            grad_b_du(k)
            grad_lambda(k)
        _unpermute_rows(du_ref, dup, sub_len)

    blk = (lambda i: i) if reverse else (lambda i: nblk - 1 - i)
    rows, wide, tall = _param_specs(direction)
    tok = pl.BlockSpec((N_SLAB, tb, SLAB_IN), lambda i: (0, blk(i), 0))
    start_spec = pl.BlockSpec((None, SUBSEG, STATE_W), lambda i: (blk(i), 0, 0))
    gb_shape, gc_shape, dl_shape = (2, N_SLAB, SLAB_IN, SLAB_ST), (2, N_SLAB, SLAB_ST, SLAB_IN), (2, SUBSEG, STATE_W)
    whole = lambda shape: pl.BlockSpec(shape, lambda i: (0,) * len(shape))
    big = lambda: pltpu.VMEM((tb, STATE_W), F32)
    slabs = lambda: pltpu.VMEM((N_SLAB, tb, SLAB_IN), F32)
    tile = lambda: pltpu.VMEM((SUBSEG, STATE_W), F32)
    return pl.pallas_call(
        body, name=name, grid=(nblk,),
        in_specs=[tok, tok, start_spec, start_spec] + rows + wide + tall + wide,
        out_specs=[tok, whole(gb_shape), whole(gc_shape), whole(dl_shape)],
        out_shape=[jax.ShapeDtypeStruct((N_SLAB, seq, SLAB_IN), F32), jax.ShapeDtypeStruct(gb_shape, F32),
                   jax.ShapeDtypeStruct(gc_shape, F32), jax.ShapeDtypeStruct(dl_shape, F32)],
        scratch_shapes=[big(), big(), big(), big(), slabs(), slabs(), slabs(), tile(), tile(), tile(), tile()],
        compiler_params=_cparams(("arbitrary",)),
    )(u, dy, *starts, lam, lam, lam, lam, bb, bb, bbt, bbt, cb_t, cb_t)


GELU_C = math.sqrt(2.0 / math.pi)
GELU_K = 0.044715
MID_ROW_GROUPS = 1


def _mid(o, za, u, y_f, y_b, zs, x, target, ssm_d, w_glu, b_glu, g_attn, g_ssm, w_out, ln_g, ln_b, tb):
    seq = x.shape[0]

    def body(o_ref, za_ref, u_ref, yf_ref, yb_ref, zs_ref, x_ref, t_ref, d_ref, wg_ref, bg_ref, ga_ref, gs_ref,
             wo_ref, lg_ref, lb_ref,
             loss_ref, do_ref, dza_ref, dyl_ref, dzs_ref, dpre_ref, gwo_ref, gwg_ref, vec_ref, wop):
        @pl.when(pl.program_id(0) == 0)
        def _():
            for ref in (loss_ref, gwo_ref, gwg_ref, vec_ref):
                ref[...] = jnp.zeros_like(ref)
            for nat, par in _pair_blocks(0):
                wop[par, :] = wo_ref[nat, :]
            wop[D_ATTN:, :] = wo_ref[D_ATTN:, :]

        def rows_of(rs):
            o, za = o_ref[rs, :], za_ref[rs, :]
            sig_a = _sigmoid(za)
            silu_a = za * sig_a
            ya = o * silu_a
            r_a = lax.rsqrt(jnp.mean(ya * ya, axis=1, keepdims=True) + NORM_EPS)
            n_a = ya * r_a
            g_a = ga_ref[...]
            unslab = lambda ref: jnp.concatenate([ref[k, rs, :] for k in range(N_SLAB)], axis=1)
            u_blk, zs = unslab(u_ref), zs_ref[rs, :]
            d_row = d_ref[...]
            ylin = d_row * u_blk + unslab(yf_ref) + unslab(yb_ref)
            inner = GELU_C * (ylin + GELU_K * ylin * ylin * ylin)
            th = jnp.tanh(inner)
            gl = 0.5 * ylin * (1.0 + th)
            glb = gl.astype(BF16)
            gate = _dot(glb, wg_ref[...])
            yield
            sg = _sigmoid(gate + bg_ref[...])
            y2 = gl * sg
            sig_s = _sigmoid(zs)
            silu_s = zs * sig_s
            ys = y2 * silu_s
            r_s = lax.rsqrt(jnp.mean(ys * ys, axis=1, keepdims=True) + NORM_EPS)
            n_s = ys * r_s
            g_s = gs_ref[...]
            mixed = jnp.concatenate([n_a * g_a, n_s * g_s], axis=1).astype(BF16)
            out = _dot(mixed, wop[...])
            yield
            pre = ALPHA * x_ref[rs, :] + out
            mu = jnp.mean(pre, axis=1, keepdims=True)
            cen = pre - mu
            rstd = lax.rsqrt(jnp.mean(cen * cen, axis=1, keepdims=True) + NORM_EPS)
            hhat = cen * rstd
            ln_g = lg_ref[...]
            err = hhat * ln_g + lb_ref[...] - t_ref[rs, :]
            loss_ref[...] += 0.5 * jnp.sum(jnp.mean(err * err, axis=1, keepdims=True))

            dh = err * (1.0 / D_MODEL)
            vec_ref[0:1, :] += jnp.sum(dh * hhat, axis=0, keepdims=True)
            vec_ref[1:2, :] += jnp.sum(dh, axis=0, keepdims=True)
            dhh = dh * ln_g
            dpre = rstd * (dhh - jnp.mean(dhh, axis=1, keepdims=True)
                           - hhat * jnp.mean(dhh * hhat, axis=1, keepdims=True))
            dpre_ref[rs, :] = dpre
            dpb = dpre.astype(BF16)
            for j in range(4):
                g_pair = _dot_tn(mixed[:, 128 * j:128 * (j + 1)], dpb)
                for g in range(2):
                    nat = HEAD_DIM * (4 * g + j)
                    gwo_ref[nat:nat + HEAD_DIM, :] += g_pair[HEAD_DIM * g:HEAD_DIM * (g + 1), :]
            gwo_ref[D_ATTN:, :] += _dot_tn(mixed[:, D_ATTN:], dpb)
            dmix = _dot_nt(dpb, wop[...])
            yield
            dna = dmix[:, :D_ATTN]
            vec_ref[2:3, 0:D_ATTN] += jnp.sum(dna * n_a, axis=0, keepdims=True)
            dna = dna * g_a
            dya = r_a * (dna - n_a * jnp.mean(dna * n_a, axis=1, keepdims=True))
            do_ref[rs, :] = dya * silu_a
            dza_ref[rs, :] = dya * o * (sig_a * (1.0 + za * (1.0 - sig_a)))
            dns = dmix[:, D_ATTN:]
            vec_ref[2:3, D_ATTN:] += jnp.sum(dns * n_s, axis=0, keepdims=True)
            dns = dns * g_s
            dys = r_s * (dns - n_s * jnp.mean(dns * n_s, axis=1, keepdims=True))
            dzs_ref[rs, :] = dys * y2 * (sig_s * (1.0 + zs * (1.0 - sig_s)))
            dy2 = dys * silu_s
            da = dy2 * gl * sg * (1.0 - sg)
            vec_ref[3:4, D_SSM:] += jnp.sum(da, axis=0, keepdims=True)
            dab = da.astype(BF16)
            gwg_ref[...] += _dot_tn(glb, dab)
            dgl_mm = _dot_nt(dab, wg_ref[...])
            yield
            dgl = dy2 * sg + dgl_mm
            dylin = dgl * (0.5 * (1.0 + th)
                           + 0.5 * ylin * (1.0 - th * th) * GELU_C * (1.0 + 3.0 * GELU_K * ylin * ylin))
            for k in range(N_SLAB):
                dyl_ref[k, rs, :] = dylin[:, k * SLAB_IN:(k + 1) * SLAB_IN]
            vec_ref[3:4, 0:D_SSM] += jnp.sum(dylin * u_blk, axis=0, keepdims=True)
            yield

        groups = [rows_of(slice(r0, r0 + tb // MID_ROW_GROUPS)) for r0 in range(0, tb, tb // MID_ROW_GROUPS)]
        for _ in range(5):
            for gen in groups:
                next(gen)

    tok = lambda w: pl.BlockSpec((tb, w), lambda i: (i, 0))
    slab = pl.BlockSpec((N_SLAB, tb, SLAB_IN), lambda i: (0, i, 0))
    const = lambda r, c: pl.BlockSpec((r, c), lambda i: (0, 0))
    tok_shape = jax.ShapeDtypeStruct((seq, 512), F32)
    return pl.pallas_call(
        body, name="mid", grid=(seq // tb,),
        in_specs=[tok(512), tok(512), slab, slab, slab, tok(512), tok(1024), tok(1024),
                  const(1, 512), const(512, 512), const(1, 512), const(1, 512), const(1, 512),
                  const(1024, 1024), const(1, 1024), const(1, 1024)],
        out_specs=[const(8, 128), tok(512), tok(512), slab, tok(512), tok(1024),
                   const(1024, 1024), const(512, 512), const(8, 1024)],
        out_shape=[jax.ShapeDtypeStruct((8, 128), F32), tok_shape, tok_shape,
                   jax.ShapeDtypeStruct((N_SLAB, seq, SLAB_IN), F32), tok_shape,
                   jax.ShapeDtypeStruct((seq, 1024), F32), jax.ShapeDtypeStruct((1024, 1024), F32),
                   jax.ShapeDtypeStruct((512, 512), F32), jax.ShapeDtypeStruct((8, 1024), F32)],
        scratch_shapes=[pltpu.VMEM((D_MODEL, D_MODEL), BF16)],
        compiler_params=_cparams(("arbitrary",)),
    )(o, za, u, y_f, y_b, zs, x, target, ssm_d, w_glu, b_glu, g_attn, g_ssm, w_out, ln_g, ln_b)


def _proj_bwd(x, dq, dk, dv, dza, du_f, du_b, dylin, dzs, dpre, ssm_d, rope_hi, rope_lo, wt, tb):
    seq = x.shape[0]

    def body(x_ref, dq_ref, dk_ref, dv_ref, dza_ref, duf_ref, dub_ref, dyl_ref, dzs_ref, dpre_ref, d_ref,
             hi_ref, lo_ref, wt_ref, gx_ref, gw_ref, wp):
        @pl.when(pl.program_id(0) == 0)
        def _():
            gw_ref[...] = jnp.zeros_like(gw_ref)
            for base in (W_Q, W_ZA):
                for nat, par in _pair_blocks(base):
                    wp[par, :] = wt_ref[nat, :]
            wp[W_KV:W_ZA, :] = wt_ref[W_KV:W_ZA, :]
            wp[W_U:, :] = wt_ref[W_U:, :]

        cos, sin = _rope_block(hi_ref, lo_ref, pl.program_id(0) * (tb // ROPE_GROUP), tb // ROPE_GROUP)

        def unrope(t):
            return t * cos + _rotate_half_unsigned(t * sin)

        dq_rot = dq_ref[...]
        pieces = [unrope(dq_rot[:, 128 * j:128 * (j + 1)]) for j in range(4)]
        d_row = d_ref[...]
        pieces += [unrope(dk_ref[...]), dv_ref[...], dza_ref[...]]
        pieces += [duf_ref[k] + dub_ref[k] + d_row[:, k * SLAB_IN:(k + 1) * SLAB_IN] * dyl_ref[k] for k in range(N_SLAB)]
        pieces += [dzs_ref[...]]
        dproj = jnp.concatenate(pieces, axis=1).astype(BF16)
        gx_ref[...] = ALPHA * dpre_ref[...] + _dot(dproj, wp[...])
        xb = x_ref[...].astype(BF16)
        for base in (W_Q, W_ZA):
            for j in range(4):
                g_pair = _dot_tn(dproj[:, base + 128 * j:base + 128 * (j + 1)], xb)
                for g in range(2):
                    nat = base + HEAD_DIM * (4 * g + j)
                    gw_ref[nat:nat + HEAD_DIM, :] += g_pair[HEAD_DIM * g:HEAD_DIM * (g + 1), :]
        gw_ref[W_KV:W_ZA, :] += _dot_tn(dproj[:, W_KV:W_ZA], xb)
        gw_ref[W_U:, :] += _dot_tn(dproj[:, W_U:], xb)

    tok = lambda w: pl.BlockSpec((tb, w), lambda i: (i, 0))
    slab = pl.BlockSpec((N_SLAB, tb, SLAB_IN), lambda i: (0, i, 0))
    const = lambda r, c: pl.BlockSpec((r, c), lambda i: (0, 0))
    whole = pl.BlockSpec((D_IN_PROJ, D_MODEL), lambda i: (0, 0), pipeline_mode=pl.Buffered(1))
    table = lambda t: pl.BlockSpec(t.shape, lambda i: (0, 0, 0))
    return pl.pallas_call(
        body, name="proj_bwd", grid=(seq // tb,),
        in_specs=[tok(1024), tok(512), tok(128), tok(128), tok(512), slab, slab, slab, tok(512), tok(1024),
                  const(1, 512), table(rope_hi), table(rope_lo), whole],
        out_specs=[tok(1024), whole],
        out_shape=[jax.ShapeDtypeStruct((seq, D_MODEL), F32), jax.ShapeDtypeStruct((D_IN_PROJ, D_MODEL), F32)],
        scratch_shapes=[pltpu.VMEM((D_IN_PROJ, D_MODEL), BF16)],
        compiler_params=_cparams(("arbitrary",)),
    )(x, dq, dk, dv, dza, du_f, du_b, dylin, dzs, dpre, ssm_d, rope_hi, rope_lo, wt)


def _adamw(w, g, m, v, name):
    rows, cols = w.shape
    tb = rows
    while tb * cols * 4 > ADAMW_BLOCK_BYTES and tb % 16 == 0:
        tb //= 2

    def body(w_ref, g_ref, m_ref, v_ref, d_ref, nm_ref, nv_ref):
        _adamw_update(w_ref, g_ref, m_ref, v_ref, d_ref, nm_ref, nv_ref)

    spec = pl.BlockSpec((tb, cols), lambda i: (i, 0))
    return pl.pallas_call(
        body, name=name, grid=(rows // tb,), in_specs=[spec] * 4, out_specs=[spec] * 3,
        out_shape=[jax.ShapeDtypeStruct((rows, cols), F32)] * 3,
        compiler_params=_cparams(("arbitrary",)),
    )(w, g, m, v)


def _adamw_update(w_ref, g_ref, m_ref, v_ref, d_ref, nm_ref, nv_ref):
    g_blk = g_ref[...]
    m_new = ADAM_B1 * m_ref[...] + (1.0 - ADAM_B1) * g_blk
    v_new = ADAM_B2 * v_ref[...] + (1.0 - ADAM_B2) * (g_blk * g_blk)
    m_hat = m_new / (1.0 - ADAM_B1 ** ADAM_STEP)
    v_hat = v_new / (1.0 - ADAM_B2 ** ADAM_STEP)
    d_ref[...] = -ADAM_LR * (m_hat / (jnp.sqrt(v_hat) + ADAM_EPS) + ADAM_WD * w_ref[...])
    nm_ref[...] = m_new
    nv_ref[...] = v_new


def _adamw_many(groups, name):
    n = len(groups)

    def body(*refs):
        for p in range(n):
            _adamw_update(*refs[4 * p:4 * p + 4], *refs[4 * n + 3 * p:4 * n + 3 * p + 3])

    return pl.pallas_call(
        body, name=name,
        out_shape=[jax.ShapeDtypeStruct(grp[0].shape, F32) for grp in groups for _ in range(3)],
    )(*[a for grp in groups for a in grp])


_WEIGHTS = ["w_in", "attn_sink", "ssm_a_re", "ssm_a_im", "ssm_log_dt", "ssm_b_re", "ssm_b_im", "ssm_c_re", "ssm_c_im",
            "ssm_d", "w_glu", "b_glu", "norm_attn_g", "norm_ssm_g", "w_out", "ln_g", "ln_b"]
N_DG = N_DIR * N_GROUPS
BIG_ROWS = N_DG * SSM_CH * SSM_STATE // 128
TINY_ROWS = 64


def _pack_small_grads(g_bc, g_vec, g_ar, g_ai, g_dt, g_sink, loss):
    big = jnp.stack([t.reshape(BIG_ROWS, 128) for t in g_bc])
    row = lambda t: jnp.pad(t.reshape(1, -1), ((0, 0), (0, 128 - t.size)))
    tiny = jnp.concatenate([g_vec.reshape(64, 128), g_ar.reshape(32, 128), g_ai.reshape(32, 128), row(g_dt), row(g_sink),
                            row(loss), jnp.zeros((N_CHIPS * TINY_ROWS - 131, 128), F32)], axis=0)
    return jnp.concatenate([big, tiny.reshape(N_CHIPS, TINY_ROWS, 128)], axis=1)


def _unpack_small_grads(packed):
    big = packed[:, :BIG_ROWS].reshape(N_CHIPS, 2 * BIG_ROWS, SSM_STATE)
    tiny = packed[:, BIG_ROWS:].reshape(N_CHIPS * TINY_ROWS, 128)
    g_vec = tiny[0:64].reshape(8, 1024)
    return tiny[130, 0], {
        "ssm_b_re": big[0], "ssm_b_im": big[1], "ssm_c_re": big[2], "ssm_c_im": big[3],
        "ln_g": g_vec[0:1], "ln_b": g_vec[1:2],
        "norm_attn_g": _from_pair_order(g_vec[2:3, :D_ATTN]), "norm_ssm_g": g_vec[2:3, D_ATTN:],
        "ssm_d": g_vec[3:4, :D_SSM], "b_glu": g_vec[3:4, D_SSM:],
        "ssm_a_re": tiny[64:96].reshape(N_DG, SSM_STATE), "ssm_a_im": tiny[96:128].reshape(N_DG, SSM_STATE),
        "ssm_log_dt": tiny[128:129, :N_DG].reshape(N_DIR, N_GROUPS), "attn_sink": tiny[129:130, :N_Q_HEADS],
    }


def _small_view(name, t):
    if name in ("ssm_b_re", "ssm_b_im"):
        return jnp.swapaxes(t[0], 2, 3).reshape(N_DG * SSM_CH, SSM_STATE)
    if name in ("ssm_c_re", "ssm_c_im"):
        return t.reshape(N_DG * SSM_CH, SSM_STATE)
    if name in ("ssm_a_re", "ssm_a_im"):
        return t.reshape(N_DG, SSM_STATE)
    if name == "ssm_log_dt":
        return t.reshape(N_DIR, N_GROUPS)
    return t.reshape(1, -1)


def _small_unview(name, t, shape):
    if name in ("ssm_b_re", "ssm_b_im"):
        return jnp.swapaxes(t.reshape(N_DIR, N_GROUPS, SSM_CH, SSM_STATE), 2, 3).reshape(shape)
    return t.reshape(shape)


def kernel(x, w_in, attn_sink, ssm_a_re, ssm_a_im, ssm_log_dt, ssm_b_re, ssm_b_im, ssm_c_re, ssm_c_im, ssm_d, w_glu, b_glu, norm_attn_g, norm_ssm_g, w_out, ln_g, ln_b, loss_target, m_w_in, m_attn_sink, m_ssm_a_re, m_ssm_a_im, m_ssm_log_dt, m_ssm_b_re, m_ssm_b_im, m_ssm_c_re, m_ssm_c_im, m_ssm_d, m_w_glu, m_b_glu, m_norm_attn_g, m_norm_ssm_g, m_w_out, m_ln_g, m_ln_b, v_w_in, v_attn_sink, v_ssm_a_re, v_ssm_a_im, v_ssm_log_dt, v_ssm_b_re, v_ssm_b_im, v_ssm_c_re, v_ssm_c_im, v_ssm_d, v_w_glu, v_b_glu, v_norm_attn_g, v_norm_ssm_g, v_w_out, v_ln_g, v_ln_b):
    args = dict(locals())
    weights = {n: args[n] for n in _WEIGHTS}
    mom_m = {n: args["m_" + n] for n in _WEIGHTS}
    mom_v = {n: args["v_" + n] for n in _WEIGHTS}
    xs = x[0]
    target = loss_target[0]

    (wt_g,) = _all_gather_chips([w_in[0].T], BF16, "gather_weights")
    wt_full = wt_g.reshape(D_IN_PROJ, D_MODEL)

    g_x, g_wt, r_w_out, r_w_glu, g_small = _local_step(
        xs, target, wt_full, w_glu[0], w_out[0], attn_sink, ssm_a_re, ssm_a_im, ssm_log_dt, ssm_b_re, ssm_b_im,
        ssm_c_re, ssm_c_im, ssm_d, b_glu, norm_attn_g, norm_ssm_g, ln_g, ln_b, sharded=True)

    r_wt, g_small_all = _reduce_all([g_wt.reshape(N_CHIPS, -1, D_MODEL), g_small], [True, False], "reduce_grads")
    loss, small_grads = _unpack_small_grads(g_small_all)

    grads, deltas, new_m, new_v = {}, {}, {}, {}
    d_w, m_w, v_w = _adamw(w_in[0].T, r_wt, m_w_in[0].T, v_w_in[0].T, "adamw_w_in")
    grads["w_in"], deltas["w_in"], new_m["w_in"], new_v["w_in"] = r_wt.T[None], d_w.T[None], m_w.T[None], v_w.T[None]
    for n, g in (("w_out", r_w_out), ("w_glu", r_w_glu)):
        d_w, m_w, v_w = _adamw(weights[n][0], g, mom_m[n][0], mom_v[n][0], "adamw_" + n)
        grads[n], deltas[n], new_m[n], new_v[n] = g[None], d_w[None], m_w[None], v_w[None]
    names = sorted(small_grads)
    updates = _adamw_many([(_small_view(n, weights[n]), small_grads[n], _small_view(n, mom_m[n]), _small_view(n, mom_v[n]))
                           for n in names], "adamw_small")
    for i, n in enumerate(names):
        shape = weights[n].shape
        grads[n] = _small_unview(n, small_grads[n], shape)
        deltas[n], new_m[n], new_v[n] = (_small_unview(n, t, shape) for t in updates[3 * i:3 * i + 3])

    return (loss, g_x[None], *[grads[n] for n in _WEIGHTS], *[deltas[n] for n in _WEIGHTS],
            *[new_m[n] for n in _WEIGHTS], *[new_v[n] for n in _WEIGHTS])


def _local_step(xs, target, wt_full, w_glu_in, w_out_in, attn_sink, ssm_a_re, ssm_a_im, ssm_log_dt, ssm_b_re,
                ssm_b_im, ssm_c_re, ssm_c_im, ssm_d, b_glu, norm_attn_g, norm_ssm_g, ln_g, ln_b, sharded):
    seq = xs.shape[0]

    a_r, a_i = _small_view("ssm_a_re", ssm_a_re), _small_view("ssm_a_im", ssm_a_im)
    log_dt = ssm_log_dt.reshape(N_DG, 1)
    b_r, b_i = _small_view("ssm_b_re", ssm_b_re), _small_view("ssm_b_im", ssm_b_im)
    c_r, c_i = _small_view("ssm_c_re", ssm_c_re), _small_view("ssm_c_im", ssm_c_im)
    ssm_tb = min(SSM_BLOCK, seq)
    sub_len = ssm_tb // SUBSEG
    lam, bb, bbt, cb, cb_t = _ssm_params_fwd(a_r, a_i, log_dt, b_r, b_i, c_r, c_i, int(math.log2(sub_len)))
    lam = lam.reshape(4, N_DIR, 1, STATE_W)

    rope_hi, rope_lo = _rope_tables(seq)
    projected = _proj(xs, wt_full, rope_hi, rope_lo, [w_glu_in, w_out_in] if sharded else [], min(1024, seq))
    q_stack, k_rot, v_bf, z_attn, u, z_ssm = projected[:6]
    if sharded:
        w_glu_full, w_out_full = projected[6].reshape(D_SSM, D_SSM), projected[7].reshape(D_MODEL, D_MODEL)
    else:
        w_glu_full, w_out_full = w_glu_in, w_out_in
    sink128 = jnp.broadcast_to(attn_sink[0][:, None, None], (N_Q_HEADS, 1, 128))
    attn_bias = _attn_bias()
    o = _attn_fwd(q_stack, k_rot, v_bf, sink128, attn_bias)
    ys, starts = [], []
    for d in range(N_DIR):
        y_d, s_r, s_i = _ssm_fwd(u, lam, bb, cb, direction=d, tb=ssm_tb, name=f"ssm_fwd_{d}")
        ys.append(y_d)
        starts.append((s_r, s_i))

    row = lambda t: t.reshape(1, -1)
    g_attn_p = _to_pair_order(norm_attn_g)
    loss_blk, d_o, d_za, d_ylin, d_zs, d_pre, g_w_out, g_w_glu, g_vec = _mid(
        o, z_attn, u, ys[0], ys[1], z_ssm, xs, target, row(ssm_d), w_glu_full, row(b_glu),
        g_attn_p, row(norm_ssm_g), w_out_full, row(ln_g), row(ln_b), min(256, seq))

    pieces = [g_w_glu.reshape(N_CHIPS, -1, D_SSM), g_w_out.reshape(N_CHIPS, -1, D_MODEL)] if sharded else []
    attn_grads = _attn_bwd(q_stack, k_rot, v_bf, sink128, attn_bias, d_o, pieces)
    dq, dk, dv, g_sink = attn_grads[:4]
    if sharded:
        g_w_glu, g_w_out = attn_grads[4:]
    dus, g_bb, g_cb, g_lam = [], [], [], []
    for d in range(N_DIR):
        du_d, gb_d, gc_d, dl_d = _ssm_bwd(u, d_ylin, starts[d], lam, bb, bbt, cb_t, direction=d, tb=ssm_tb,
                                          name=f"ssm_bwd_{d}")
        dus.append(du_d)
        g_bb.append(gb_d)
        g_cb.append(gc_d)
        g_lam.append(dl_d)
    g_ar, g_ai, g_dt, g_br, g_bi, g_cr, g_ci = _ssm_params_bwd(a_r, a_i, log_dt, b_r, b_i, g_bb, g_cb, g_lam)

    g_x, g_wt = _proj_bwd(xs, dq, dk, dv, d_za, dus[0], dus[1], d_ylin, d_zs, d_pre, row(ssm_d), rope_hi, rope_lo,
                          wt_full, min(512, seq))

    g_small = _pack_small_grads([g_br, g_bi, g_cr, g_ci], g_vec, g_ar, g_ai, g_dt, g_sink[:, 0], loss_blk[0, 0])
    return g_x, g_wt, g_w_out, g_w_glu, g_small
```

```python
import functools
import math

import numpy as np
import jax
import jax.numpy as jnp
from jax import lax
from jax.experimental import pallas as pl
from jax.experimental.pallas import tpu as pltpu

F32 = jnp.float32
BF16 = jnp.bfloat16
MESH = pl.DeviceIdType.MESH

D_MODEL = 1024
D_ATTN = 512
D_SSM = 512
HEAD_DIM = 64
N_Q_HEADS = 8
WINDOW = 128
ROPE_THETA = 10000.0
SSM_CH = 16
N_GROUPS = 32
SSM_STATE = 64
N_DIR = 2
STATE_W = N_GROUPS * SSM_STATE
N_SLAB = 4
SLAB_IN = 128
SLAB_ST = 512
NORM_EPS = 1e-5
NEG_INF = -1e30
ALPHA = 2.0 ** 0.25
D_IN_PROJ = 2304
N_CHIPS = 4

ADAM_LR = 0.001
ADAM_B1 = 0.9
ADAM_B2 = 0.999
ADAM_EPS = 1e-08
ADAM_WD = 0.01
ADAM_STEP = 10

SUBSEG = 8
SCAN_LANES = 512
SSM_BLOCK = 512
VMEM_LIMIT = 48 * 1024 * 1024
ADAMW_BLOCK_BYTES = 3 * 512 * 1024
PROJ_BWD_X_VMEM = 56 * 1024 * 1024

def _to_pair_order(row):
    return jnp.transpose(row.reshape(2, 4, HEAD_DIM), (1, 0, 2)).reshape(1, D_ATTN)


def _from_pair_order(row):
    return jnp.transpose(row.reshape(4, 2, HEAD_DIM), (1, 0, 2)).reshape(1, D_ATTN)


def _cparams(sem=None):
    return pltpu.CompilerParams(dimension_semantics=sem, vmem_limit_bytes=VMEM_LIMIT)


def _dot(a, b):
    return jnp.dot(a, b, preferred_element_type=F32)


def _dot_nt(a, b):
    return lax.dot_general(a, b, (((1,), (1,)), ((), ())), preferred_element_type=F32)


def _dot_tn(a, b):
    return lax.dot_general(a, b, (((0,), (0,)), ((), ())), preferred_element_type=F32)


def _sigmoid(z):
    return 1.0 / (1.0 + jnp.exp(-z))


def _all_gather_chips(shards, out_dtype, name):
    n = len(shards)

    def body(*refs):
        start, relay, finish = _gather_phases(refs[:n], refs[n:2 * n], *refs[2 * n:], out_dtype)
        start()
        relay()
        finish()

    vmem = pl.BlockSpec(memory_space=pltpu.VMEM)
    return pl.pallas_call(
        body, name=name,
        out_shape=[jax.ShapeDtypeStruct((N_CHIPS,) + s.shape, out_dtype) for s in shards],
        in_specs=[vmem] * n, out_specs=[vmem] * n,
        scratch_shapes=_gather_sems(n),
        compiler_params=pltpu.CompilerParams(vmem_limit_bytes=VMEM_LIMIT),
    )(*shards)


def _gather_sems(n):
    return [pltpu.SemaphoreType.DMA((6 * n,)), pltpu.SemaphoreType.DMA((6 * n,))]


def _gather_phases(in_refs, out_refs, send_sems, recv_sems, out_dtype):
    n = len(in_refs)
    x, y, c = lax.axis_index("x"), lax.axis_index("y"), lax.axis_index("c")
    sibling = (x, y, 1 - c)
    chips = [(1 - x, y), (x, 1 - y), (1 - x, 1 - y)]

    def half_of(a, px, py, half):
        rows = in_refs[a].shape[0] // 2
        return out_refs[a].at[2 * px + py, pl.ds(half * rows, rows), :]

    def copy(a, k, px, py, half, to):
        blk = half_of(a, px, py, half)
        return pltpu.make_async_remote_copy(src_ref=blk, dst_ref=blk, send_sem=send_sems.at[6 * a + k],
                                            recv_sem=recv_sems.at[6 * a + k], device_id=to, device_id_type=MESH)

    first = [copy(a, j, x, y, c, (*chips[j], c)) for a in range(n) for j in range(3)]
    passed = [copy(a, 3 + j, *chips[j], c, sibling) for a in range(n) for j in range(3)]

    def start():
        for a in range(n):
            out_refs[a][2 * x + y] = in_refs[a][...].astype(out_dtype)
        for cp in first:
            cp.start()

    def relay():
        for a in range(n):
            for j in range(3):
                copy(a, j, *chips[j], c, (x, y, c)).wait_recv()
                passed[3 * a + j].start()

    def finish():
        for a in range(n):
            for j in range(3):
                copy(a, 3 + j, *chips[j], 1 - c, (x, y, c)).wait_recv()
        for cp in first + passed:
            cp.wait_send()

    return start, relay, finish


SEMS_PER_ARRAY = 11


def _reduce_all(pieces, narrow, name):
    n = len(pieces)
    shapes = [p.shape for p in pieces]

    def body(*refs):
        start, exchange, combine, finish = _reduce_phases(
            refs[:n], refs[n:2 * n], refs[2 * n:3 * n], refs[3 * n:4 * n], refs[4 * n:5 * n], *refs[5 * n:],
            narrow, gather_last=True)
        start()
        exchange()
        combine()
        finish()

    vmem = pl.BlockSpec(memory_space=pltpu.VMEM)
    return pl.pallas_call(
        body, name=name,
        out_shape=[jax.ShapeDtypeStruct(p.shape[1:] if a < n - 1 else p.shape, F32) for a, p in enumerate(pieces)],
        in_specs=[vmem] * n, out_specs=[vmem] * n,
        scratch_shapes=_reduce_scratch(shapes, narrow),
        compiler_params=pltpu.CompilerParams(vmem_limit_bytes=VMEM_LIMIT),
    )(*pieces)


def _reduce_scratch(shapes, narrow):
    half = [(N_CHIPS, s[1] // 2, s[2]) for s in shapes]
    wire = [BF16 if nar else F32 for nar in narrow]
    n = len(shapes)
    return ([pltpu.VMEM(half[a], F32) for a in range(n)] + [pltpu.VMEM(half[a], wire[a]) for a in range(n)]
            + [pltpu.VMEM(half[a], wire[a]) for a in range(n)]
            + [pltpu.SemaphoreType.DMA((SEMS_PER_ARRAY * n,)), pltpu.SemaphoreType.DMA((SEMS_PER_ARRAY * n,))])


def _reduce_phases(p_refs, out_refs, a_refs, s_refs, b_refs, send_sems, recv_sems, narrow, gather_last):
    n = len(p_refs)
    halves = [p.shape[1] // 2 for p in p_refs]
    wire = [BF16 if nar else F32 for nar in narrow]
    x, y, c = lax.axis_index("x"), lax.axis_index("y"), lax.axis_index("c")
    me = 2 * x + y
    sibling = (x, y, 1 - c)
    chips = [(1 - x, y), (x, 1 - y), (1 - x, 1 - y)]
    slot = [2 * px + py for px, py in chips]
    last = n - 1

    def copy(a, k, src, dst, to):
        return pltpu.make_async_remote_copy(src_ref=src, dst_ref=dst, send_sem=send_sems.at[SEMS_PER_ARRAY * a + k],
                                            recv_sem=recv_sems.at[SEMS_PER_ARRAY * a + k],
                                            device_id=to, device_id_type=MESH)

    def rows(a, half):
        return pl.ds(pl.multiple_of(half * halves[a], 16), halves[a])

    def finished(a, k, half):
        if gather_last and a == last:
            return out_refs[a].at[k, rows(a, half), :]
        return out_refs[a].at[rows(a, half), :]

    swaps = [copy(a, 0, p_refs[a].at[:, rows(a, 1 - c), :], a_refs[a], sibling) for a in range(n)]
    sends = [[copy(a, 1 + j, s_refs[a].at[slot[j]], b_refs[a].at[me], (*chips[j], c)) for j in range(3)] for a in range(n)]
    backs = [copy(a, 4, finished(a, me, c), finished(a, me, c), sibling) for a in range(n)]
    spread = [copy(last, 5 + j, finished(last, me, c), finished(last, me, c), (*chips[j], c)) for j in range(3)]
    relays = [copy(last, 8 + j, finished(last, slot[j], c), finished(last, slot[j], c), sibling) for j in range(3)]

    def start():
        for cp in swaps:
            cp.start()

    def exchange():
        for a in range(n):
            swaps[a].wait_recv()
            for k in range(N_CHIPS):
                acc = a_refs[a][k] + p_refs[a][k, rows(a, c), :]
                a_refs[a][k] = acc
                s_refs[a][k] = acc.astype(wire[a])
            b_refs[a][me] = s_refs[a][me]
            for cp in sends[a]:
                cp.start()

    def combine():
        for a in range(n):
            for j in range(3):
                copy(a, 1 + j, s_refs[a].at[slot[j]], b_refs[a].at[slot[j]], (x, y, c)).wait_recv()
            terms = [jnp.where(me == k, a_refs[a][k], b_refs[a][k].astype(F32)) for k in range(N_CHIPS)]
            total = (terms[0] + terms[1]) + (terms[2] + terms[3])
            if gather_last and a == last:
                out_refs[a][me, rows(a, c), :] = total
            else:
                out_refs[a][rows(a, c), :] = total
            backs[a].start()
        if gather_last:
            for cp in spread:
                cp.start()

    def finish():
        if gather_last:
            for j in range(3):
                copy(last, 5 + j, finished(last, slot[j], c), finished(last, slot[j], c), (x, y, c)).wait_recv()
                relays[j].start()
        for a in range(n):
            copy(a, 4, finished(a, me, 1 - c), finished(a, me, 1 - c), (x, y, c)).wait_recv()
        if gather_last:
            for j in range(3):
                copy(last, 8 + j, finished(last, slot[j], 1 - c), finished(last, slot[j], 1 - c), (x, y, c)).wait_recv()
        for cp in swaps + [cp for group in sends for cp in group] + backs + (spread + relays if gather_last else []):
            cp.wait_send()

    return start, exchange, combine, finish


def _ssm_param_values(ar, ai, logdt):
    dt = jnp.exp(logdt)
    mag = jnp.exp(dt * ar)
    cs, sn = jnp.cos(dt * ai), jnp.sin(dt * ai)
    lr, li = mag * cs, mag * sn
    den = ar * ar + ai * ai
    nr = (lr - 1.0) * ar + li * ai
    ni = li * ar - (lr - 1.0) * ai
    return dt, mag, lr, li, den, nr, ni


GROUPS_PER_SLAB = N_GROUPS // N_SLAB


def _slab_masks():
    def eq(shape, f_row, f_col):
        return (f_row(lax.broadcasted_iota(jnp.int32, shape, 0)) == f_col(lax.broadcasted_iota(jnp.int32, shape, 1))).astype(F32)
    spread = eq((SSM_STATE, SLAB_ST), lambda r: r, lambda c: c % SSM_STATE)
    spread_t = eq((SLAB_ST, SSM_STATE), lambda r: r % SSM_STATE, lambda c: c)
    keep = eq((SLAB_IN, SLAB_ST), lambda r: r // SSM_CH, lambda c: c // SSM_STATE)
    keep_t = eq((SLAB_ST, SLAB_IN), lambda r: r // SSM_STATE, lambda c: c // SSM_CH)
    repeat = eq((N_DG * SSM_CH, N_DG), lambda r: r // SSM_CH, lambda c: c)
    return spread, spread_t, keep, keep_t, repeat


def _split3(t):
    hi = t.astype(BF16)
    rest = t - hi.astype(F32)
    mid = rest.astype(BF16)
    return hi, mid, (rest - mid.astype(F32)).astype(BF16)


def _select(dot, ones01, t, ones_first):
    o = ones01.astype(BF16)
    parts = [dot(o, p) if ones_first else dot(p, o) for p in _split3(t)]
    return (parts[0] + parts[1]) + parts[2]


def _ssm_params_fwd(ar, ai, logdt, br, bi, cr, ci, n_square):
    def body(ar_ref, ai_ref, dt_ref, br_ref, bi_ref, cr_ref, ci_ref, lam_ref, bb_ref, bbt_ref, cb_ref, cbt_ref):
        _, _, lr, li, den, nr, ni = _ssm_param_values(ar_ref[...], ai_ref[...], dt_ref[...])
        lam_ref[0] = lr
        lam_ref[1] = li
        pr, pi = lr, li
        for _ in range(n_square):
            pr, pi = pr * pr - pi * pi, 2.0 * pr * pi
        lam_ref[2] = pr
        lam_ref[3] = pi
        spread, spread_t, keep, keep_t, repeat = _slab_masks()
        fr = _select(_dot, repeat, nr / den, True)
        fi = _select(_dot, repeat, ni / den, True)
        b_r, b_i = br_ref[...], bi_ref[...]
        bbar = (fr * b_r - fi * b_i, fr * b_i + fi * b_r)
        c_par = (cr_ref[...], ci_ref[...])
        spread, spread_t = spread.astype(BF16), spread_t.astype(BF16)
        for src, wide_ref, tall_ref in ((bbar, bb_ref, bbt_ref), (c_par, cbt_ref, cb_ref)):
            for q in range(2):
                for d in range(N_DIR):
                    for k in range(N_SLAB):
                        r0 = (d * N_GROUPS + k * GROUPS_PER_SLAB) * SSM_CH
                        blk = src[q][r0:r0 + SLAB_IN].astype(BF16)
                        wide_ref[q, d, k] = (_dot(blk, spread) * keep).astype(BF16)
                        tall_ref[q, d, k] = (_dot_nt(spread_t, blk) * keep_t).astype(BF16)

    wide = jax.ShapeDtypeStruct((2, N_DIR, N_SLAB, SLAB_IN, SLAB_ST), BF16)
    tall = jax.ShapeDtypeStruct((2, N_DIR, N_SLAB, SLAB_ST, SLAB_IN), BF16)
    return pl.pallas_call(body, name="ssm_params_fwd",
                          out_shape=[jax.ShapeDtypeStruct((4,) + ar.shape, F32), wide, tall, tall, wide],
                          compiler_params=pltpu.CompilerParams(vmem_limit_bytes=VMEM_LIMIT),
                          )(ar, ai, logdt, br, bi, cr, ci)


def _ssm_params_bwd(ar, ai, logdt, br, bi, g_slabs_b, g_slabs_c, g_lam):
    def body(ar_ref, ai_ref, dt_ref, br_ref, bi_ref, gb0_ref, gb1_ref, gc0_ref, gc1_ref, gl0_ref, gl1_ref,
             gar_ref, gai_ref, gdt_ref, gbr_ref, gbi_ref, gcr_ref, gci_ref, dbb, dlam):
        spread, spread_t, keep, keep_t, repeat = _slab_masks()
        for d, (gb_ref, gc_ref) in enumerate(((gb0_ref, gc0_ref), (gb1_ref, gc1_ref))):
            for q in range(2):
                for k in range(N_SLAB):
                    r0 = (d * N_GROUPS + k * GROUPS_PER_SLAB) * SSM_CH
                    dbb[q, r0:r0 + SLAB_IN, :] = _select(_dot, spread_t, gb_ref[q, k] * keep, False)
                    out_ref = gcr_ref if q == 0 else gci_ref
                    out_ref[r0:r0 + SLAB_IN, :] = _select(_dot_tn, spread_t, gc_ref[q, k] * keep_t, False)
        grp = (lax.broadcasted_iota(jnp.int32, (N_GROUPS, STATE_W), 0)
               == lax.broadcasted_iota(jnp.int32, (N_GROUPS, STATE_W), 1) // SSM_STATE).astype(F32)
        pick = (lax.broadcasted_iota(jnp.int32, (STATE_W, SSM_STATE), 0) % SSM_STATE
                == lax.broadcasted_iota(jnp.int32, (STATE_W, SSM_STATE), 1)).astype(F32)
        for d, gl_ref in enumerate((gl0_ref, gl1_ref)):
            for q in range(2):
                row = jnp.sum(gl_ref[q], axis=0, keepdims=True)
                dlam[q, d * N_GROUPS:(d + 1) * N_GROUPS, :] = _select(_dot, pick, grp * row, False)

        a_r, a_i = ar_ref[...], ai_ref[...]
        dt, mag, lr, li, den, nr, ni = _ssm_param_values(a_r, a_i, dt_ref[...])
        fr = _select(_dot, repeat, nr / den, True)
        fi = _select(_dot, repeat, ni / den, True)
        b_r, b_i = br_ref[...], bi_ref[...]
        g_r, g_i = dbb[0], dbb[1]
        gbr_ref[...] = fr * g_r + fi * g_i
        gbi_ref[...] = fr * g_i - fi * g_r
        d_fr = _select(_dot_tn, repeat, b_r * g_r + b_i * g_i, True)
        d_fi = _select(_dot_tn, repeat, b_r * g_i - b_i * g_r, True)
        d_nr, d_ni = d_fr / den, d_fi / den
        d_den = -(d_fr * nr + d_fi * ni) / (den * den)
        d_lr = dlam[0] + d_nr * a_r - d_ni * a_i
        d_li = dlam[1] + d_nr * a_i + d_ni * a_r
        d_ar = d_nr * (lr - 1.0) + d_ni * li + d_den * 2.0 * a_r
        d_ai = d_nr * li - d_ni * (lr - 1.0) + d_den * 2.0 * a_i
        d_mag = (d_lr * lr + d_li * li) / mag
        d_theta = d_li * lr - d_lr * li
        gar_ref[...] = d_ar + d_mag * mag * dt
        gai_ref[...] = d_ai + d_theta * dt
        d_dt = d_mag * mag * a_r + d_theta * a_i
        gdt_ref[...] = jnp.sum(d_dt, axis=1, keepdims=True) * dt

    small = jax.ShapeDtypeStruct(ar.shape, F32)
    big = jax.ShapeDtypeStruct(br.shape, F32)
    return pl.pallas_call(
        body, name="ssm_params_bwd",
        out_shape=[small, small, jax.ShapeDtypeStruct(logdt.shape, F32), big, big, big, big],
        scratch_shapes=[pltpu.VMEM((2,) + br.shape, F32), pltpu.VMEM((2,) + ar.shape, F32)],
        compiler_params=pltpu.CompilerParams(vmem_limit_bytes=VMEM_LIMIT),
    )(ar, ai, logdt, br, bi, *g_slabs_b, *g_slabs_c, *g_lam)


ROPE_GROUP = 128


def _rope_tables(seq):
    half = HEAD_DIM // 2
    inv_freq = jnp.tile(ROPE_THETA ** (-jnp.arange(half, dtype=F32) / half), 4)
    sign = jnp.tile(jnp.concatenate([-jnp.ones((half,), F32), jnp.ones((half,), F32)]), 2)

    def table(pos):
        ang = pos.astype(F32)[:, None] * inv_freq[None, :]
        return jnp.stack([jnp.cos(ang), jnp.sin(ang), sign * jnp.sin(ang)])

    return table(jnp.arange(seq // ROPE_GROUP) * ROPE_GROUP), table(jnp.arange(ROPE_GROUP))


def _rope_block(hi_ref, lo_ref, first_group, n_groups):
    cl, sl, sl_s = lo_ref[0], lo_ref[1], lo_ref[2]
    cos, sin = [], []
    for g in range(n_groups):
        ch, sh, sh_s = (hi_ref[q, pl.ds(first_group + g, 1), :] for q in range(3))
        cos.append(ch * cl - sh * sl)
        sin.append(sh_s * cl + ch * sl_s)
    return jnp.concatenate(cos, axis=0), jnp.concatenate(sin, axis=0)


def _rotate_half_unsigned(t):
    lane = lax.broadcasted_iota(jnp.int32, t.shape, 1)
    return jnp.where((lane % HEAD_DIM) < HEAD_DIM // 2, pltpu.roll(t, 96, 1), pltpu.roll(t, 32, 1))


def _rope(t, cos, sin_signed):
    return t * cos + _rotate_half_unsigned(t) * sin_signed


def _pair_blocks(base):
    out = []
    for j in range(4):
        for g in range(2):
            nat = base + HEAD_DIM * (4 * g + j)
            par = base + 128 * j + HEAD_DIM * g
            out.append((slice(nat, nat + HEAD_DIM), slice(par, par + HEAD_DIM)))
    return out


W_Q, W_KV, W_ZA, W_U, W_ZS = 0, 512, 768, 1280, 1792


def _proj(x, wt, rope_hi, rope_lo, shards, tb):
    seq = x.shape[0]
    steps = seq // tb
    n_sh = len(shards)

    def body(*refs):
        x_ref, wt_ref, hi_ref, lo_ref = refs[:4]
        shard_refs = refs[4:4 + n_sh]
        q_ref, k_ref, v_ref, za_ref, u_ref, zs_ref = refs[4 + n_sh:10 + n_sh]
        gathered_refs = refs[10 + n_sh:10 + 2 * n_sh]
        wp = refs[10 + 2 * n_sh]
        step = pl.program_id(0)
        if n_sh:
            landing_refs = refs[11 + 2 * n_sh:11 + 3 * n_sh]
            start, relay, finish = _gather_phases(shard_refs, landing_refs, *refs[11 + 3 * n_sh:], BF16)
            pl.when(step == 0)(start)
            pl.when(step == max(steps - 2, 0))(relay)

        @pl.when(step == 0)
        def _():
            for dst_base, src_base in ((0, W_Q), (512, W_ZA)):
                for nat, par in _pair_blocks(0):
                    wp[dst_base + par.start:dst_base + par.stop, :] = wt_ref[src_base + nat.start:src_base + nat.stop, :]

        xb = x_ref[...].astype(BF16)
        cos, sin = _rope_block(hi_ref, lo_ref, pl.program_id(0) * (tb // ROPE_GROUP), tb // ROPE_GROUP)
        lo = lax.broadcasted_iota(jnp.int32, (tb, 128), 1) < HEAD_DIM
        q = _dot_nt(xb, wp[0:512, :])
        for j in range(4):
            qj = _rope(q[:, 128 * j:128 * (j + 1)], cos, sin)
            q_ref[j] = jnp.where(lo, qj, 0.0).astype(BF16)
            q_ref[4 + j] = jnp.where(lo, 0.0, qj).astype(BF16)
        kv = _dot_nt(xb, wt_ref[W_KV:W_ZA, :])
        k_ref[...] = _rope(kv[:, 0:128], cos, sin).astype(BF16)
        v_ref[...] = kv[:, 128:256].astype(BF16)
        za_ref[...] = _dot_nt(xb, wp[512:1024, :])
        u_val = _dot_nt(xb, wt_ref[W_U:W_ZS, :])
        for k in range(N_SLAB):
            u_ref[k] = u_val[:, k * SLAB_IN:(k + 1) * SLAB_IN]
        zs_ref[...] = _dot_nt(xb, wt_ref[W_ZS:D_IN_PROJ, :])
        if n_sh:
            @pl.when(step == steps - 1)
            def _():
                finish()
                for a in range(n_sh):
                    gathered_refs[a][...] = landing_refs[a][...]

    row = lambda w: pl.BlockSpec((tb, w), lambda i: (i, 0))
    table = lambda t: pl.BlockSpec(t.shape, lambda i: (0, 0, 0))
    vmem = pl.BlockSpec(memory_space=pltpu.VMEM)
    return pl.pallas_call(
        body, name="proj", grid=(steps,),
        in_specs=[row(D_MODEL), pl.BlockSpec((D_IN_PROJ, D_MODEL), lambda i: (0, 0), pipeline_mode=pl.Buffered(1)),
                  table(rope_hi), table(rope_lo)] + [vmem] * n_sh,
        out_specs=[pl.BlockSpec((8, tb, 128), lambda i: (0, i, 0)), row(128), row(128), row(512),
                   pl.BlockSpec((N_SLAB, tb, SLAB_IN), lambda i: (0, i, 0)), row(512)] + [vmem] * n_sh,
        out_shape=[jax.ShapeDtypeStruct((8, seq, 128), BF16), jax.ShapeDtypeStruct((seq, 128), BF16),
                   jax.ShapeDtypeStruct((seq, 128), BF16), jax.ShapeDtypeStruct((seq, 512), F32),
                   jax.ShapeDtypeStruct((N_SLAB, seq, SLAB_IN), F32), jax.ShapeDtypeStruct((seq, 512), F32)]
        + [jax.ShapeDtypeStruct((N_CHIPS,) + s.shape, BF16) for s in shards],
        scratch_shapes=[pltpu.VMEM((1024, D_MODEL), BF16)] + [pltpu.VMEM((N_CHIPS,) + s.shape, BF16) for s in shards]
        + (_gather_sems(n_sh) if n_sh else []),
        compiler_params=_cparams(("arbitrary",)),
    )(x, wt, rope_hi, rope_lo, *shards)


ATT_TQ = 128
ATT_KEYS = 3 * ATT_TQ


def _attn_window(i, seq):
    start = jnp.clip(i * ATT_TQ - WINDOW, 0, seq - ATT_KEYS)
    return pl.multiple_of(start, ATT_TQ)


def _attn_bias():
    r = np.arange(ATT_TQ)[None, :, None]
    c = np.arange(ATT_KEYS)[None, None, :]
    off = (np.arange(3) * ATT_TQ)[:, None, None]
    return jnp.asarray(np.where(np.abs(r + off - c) <= WINDOW, 0.0, NEG_INF).astype(np.float32))


def _attn_bias_spec(nblk):
    pick = lambda i: jnp.where(i == 0, 0, jnp.where(i == nblk - 1, 2, 1))
    return pl.BlockSpec((None, ATT_TQ, ATT_KEYS), lambda i: (pick(i), 0, 0))


def _attn_softmax(q_ref, k_ref, v_ref, sink_ref, bias_ref, start):
    kw = k_ref[pl.ds(start, ATT_KEYS), :]
    vw = v_ref[pl.ds(start, ATT_KEYS), :]
    qall = q_ref[...].reshape(N_Q_HEADS * ATT_TQ, 128)
    s = (_dot_nt(qall, kw) * (HEAD_DIM ** -0.5)).reshape(N_Q_HEADS, ATT_TQ, ATT_KEYS) + bias_ref[...][None]
    tiles = [s[:, :, 128 * t:128 * (t + 1)] for t in range(ATT_KEYS // 128)]
    m = jnp.max(jnp.maximum(jnp.maximum(tiles[0], tiles[1]), tiles[2]), axis=2, keepdims=True)
    sink = sink_ref[...]
    m_b = jnp.maximum(jnp.broadcast_to(m, (N_Q_HEADS, ATT_TQ, 128)), sink)
    p = jnp.concatenate([jnp.exp(t - m_b) for t in tiles], axis=2)
    p_sink = jnp.exp(sink - m_b)
    lo_k = lax.broadcasted_iota(jnp.int32, (ATT_KEYS, 128), 1) < HEAD_DIM
    v_f = vw.astype(F32)
    v_lo, v_hi = jnp.where(lo_k, v_f, 1.0).astype(BF16), jnp.where(lo_k, 1.0, v_f).astype(BF16)
    pb = p.astype(BF16).reshape(N_Q_HEADS * ATT_TQ, ATT_KEYS)
    half = 4 * ATT_TQ
    r = jnp.concatenate([_dot(pb[:half], v_lo), _dot(pb[half:], v_hi)], axis=0).reshape(N_Q_HEADS, ATT_TQ, 128)
    return kw, vw, qall, p, p_sink, r


def _attn_fwd(q_stack, k, v, sink128, bias):
    seq = k.shape[0]

    def body(q_ref, k_ref, v_ref, sink_ref, bias_ref, o_ref):
        start = _attn_window(pl.program_id(0), seq)
        _, _, _, _, p_sink, r = _attn_softmax(q_ref, k_ref, v_ref, sink_ref, bias_ref, start)
        out = r / (pltpu.roll(r, HEAD_DIM, 2) + p_sink)
        lo = lax.broadcasted_iota(jnp.int32, (ATT_TQ, 128), 1) < HEAD_DIM
        for j in range(4):
            o_ref[:, 128 * j:128 * (j + 1)] = jnp.where(lo, out[j], out[4 + j])

    full = lambda w: pl.BlockSpec((seq, w), lambda i: (0, 0))
    return pl.pallas_call(
        body, name="attn_fwd", grid=(seq // ATT_TQ,),
        in_specs=[pl.BlockSpec((8, ATT_TQ, 128), lambda i: (0, i, 0)), full(128), full(128),
                  pl.BlockSpec((N_Q_HEADS, 1, 128), lambda i: (0, 0, 0)), _attn_bias_spec(seq // ATT_TQ)],
        out_specs=pl.BlockSpec((ATT_TQ, 512), lambda i: (i, 0)),
        out_shape=jax.ShapeDtypeStruct((seq, 512), F32),
        compiler_params=_cparams(("arbitrary",)),
    )(q_stack, k, v, sink128, bias)


def _attn_bwd(q_stack, k, v, sink128, bias, d_o, pieces):
    seq = k.shape[0]
    steps = seq // ATT_TQ
    n_p = len(pieces)

    def body(*refs):
        q_ref, k_ref, v_ref, sink_ref, bias_ref, do_ref = refs[:6]
        piece_refs = refs[6:6 + n_p]
        dq_ref, dk_ref, dv_ref, dsink_ref = refs[6 + n_p:10 + n_p]
        reduced_refs = refs[10 + n_p:10 + 2 * n_p]
        sink_acc = refs[10 + 2 * n_p]
        i = pl.program_id(0)
        if n_p:
            landing_refs = refs[11 + 2 * n_p:11 + 3 * n_p]
            scratch = refs[11 + 3 * n_p:]
            begin, exchange, combine, finish = _reduce_phases(
                piece_refs, landing_refs, scratch[:n_p], scratch[n_p:2 * n_p], scratch[2 * n_p:3 * n_p],
                *scratch[3 * n_p:], [True] * n_p, gather_last=False)
            pl.when(i == 0)(begin)
            pl.when(i == min(4, steps - 1))(exchange)
            pl.when(i == (3 * steps) // 4)(combine)

        @pl.when(i == 0)
        def _():
            dk_ref[...] = jnp.zeros_like(dk_ref)
            dv_ref[...] = jnp.zeros_like(dv_ref)
            sink_acc[...] = jnp.zeros_like(sink_acc)

        start = _attn_window(i, seq)
        kw, vw, qall, p, p_sink, r = _attn_softmax(q_ref, k_ref, v_ref, sink_ref, bias_ref, start)
        lo = lax.broadcasted_iota(jnp.int32, (ATT_TQ, 128), 1) < HEAD_DIM
        lo3 = lo[None]
        grp0 = lax.broadcasted_iota(jnp.int32, (N_Q_HEADS, ATT_TQ, 128), 0) < 4
        val = grp0 == lo3
        swapped = pltpu.roll(r, HEAD_DIM, 2)
        inv = 1.0 / (jnp.where(val, swapped, r) + p_sink)
        d_o_blk = do_ref[...]
        do3 = jnp.where(val, jnp.concatenate([d_o_blk[None, :, 128 * j:128 * (j + 1)] for j in range(4)] * 2, axis=0), 0.0)
        t = (do3 * r).reshape(N_Q_HEADS * ATT_TQ, 128)
        t_hi = t.astype(BF16)
        t_lo = (t - t_hi.astype(F32)).astype(BF16)
        ones = jnp.ones((128, 128), BF16)
        delta = (_dot(t_hi, ones) + _dot(t_lo, ones)).reshape(N_Q_HEADS, ATT_TQ, 128) * inv
        sink_acc[...] += -(p_sink * inv) * delta
        do_all = do3.astype(BF16).reshape(N_Q_HEADS * ATT_TQ, 128)
        dp = _dot_nt(do_all, vw).reshape(N_Q_HEADS, ATT_TQ, ATT_KEYS)
        probs, ds = [], []
        for tl in range(ATT_KEYS // 128):
            cols = slice(128 * tl, 128 * (tl + 1))
            probs_t = p[:, :, cols] * inv
            probs.append(probs_t.astype(BF16))
            ds.append((probs_t * (dp[:, :, cols] - delta)).astype(BF16))
        probs_all = jnp.concatenate(probs, axis=2).reshape(N_Q_HEADS * ATT_TQ, ATT_KEYS)
        ds_all = jnp.concatenate(ds, axis=2).reshape(N_Q_HEADS * ATT_TQ, ATT_KEYS)
        scale = HEAD_DIM ** -0.5
        dq_all = (_dot(ds_all, kw) * scale).reshape(N_Q_HEADS, ATT_TQ, 128)
        for j in range(4):
            dq_ref[:, 128 * j:128 * (j + 1)] = jnp.where(lo, dq_all[j], dq_all[4 + j])
        dk_ref[pl.ds(start, ATT_KEYS), :] += _dot_tn(ds_all, qall) * scale
        dv_ref[pl.ds(start, ATT_KEYS), :] += _dot_tn(probs_all, do_all)

        @pl.when(i == steps - 1)
        def _():
            dsink_ref[...] = jnp.sum(sink_acc[...], axis=1)

        if n_p:
            @pl.when(i == steps - 1)
            def _():
                finish()
                for a in range(n_p):
                    reduced_refs[a][...] = landing_refs[a][...]

    full = lambda w: pl.BlockSpec((seq, w), lambda i: (0, 0))
    vmem = pl.BlockSpec(memory_space=pltpu.VMEM)
    return pl.pallas_call(
        body, name="attn_bwd", grid=(steps,),
        in_specs=[pl.BlockSpec((8, ATT_TQ, 128), lambda i: (0, i, 0)), full(128), full(128),
                  pl.BlockSpec((N_Q_HEADS, 1, 128), lambda i: (0, 0, 0)),
                  _attn_bias_spec(steps), pl.BlockSpec((ATT_TQ, 512), lambda i: (i, 0))] + [vmem] * n_p,
        out_specs=[pl.BlockSpec((ATT_TQ, 512), lambda i: (i, 0)), full(128), full(128),
                   pl.BlockSpec((N_Q_HEADS, 128), lambda i: (0, 0))] + [vmem] * n_p,
        out_shape=[jax.ShapeDtypeStruct((seq, 512), F32), jax.ShapeDtypeStruct((seq, 128), F32),
                   jax.ShapeDtypeStruct((seq, 128), F32), jax.ShapeDtypeStruct((N_Q_HEADS, 128), F32)]
        + [jax.ShapeDtypeStruct(p.shape[1:], F32) for p in pieces],
        scratch_shapes=[pltpu.VMEM((N_Q_HEADS, ATT_TQ, 128), F32)] + [pltpu.VMEM(p.shape[1:], F32) for p in pieces]
        + (_reduce_scratch([p.shape for p in pieces], [True] * n_p) if n_p else []),
        compiler_params=_cparams(("arbitrary",)),
    )(q_stack, k, v, sink128, bias, d_o, *pieces)


def _permute_rows(dst_ref, src_ref, sub_len):
    for k in range(N_SLAB):
        for j in range(sub_len):
            dst_ref[k, 8 * j:8 * (j + 1), :] = src_ref.at[k][pl.ds(j, SUBSEG, stride=sub_len), :]


def _unpermute_rows(dst_ref, src_ref, sub_len):
    for k in range(N_SLAB):
        for s in range(SUBSEG):
            dst_ref[k, s * sub_len:(s + 1) * sub_len, :] = src_ref.at[k][pl.ds(s, sub_len, stride=SUBSEG), :]


def _scan_chunk(br_ref, bi_ref, lr_row, li_row, init, cols, *, sub_len, reverse, store):
    lr = jnp.broadcast_to(lr_row[:, cols], (SUBSEG, SCAN_LANES))
    li = jnp.broadcast_to(li_row[:, cols], (SUBSEG, SCAN_LANES))
    if init is None:
        sr = si = jnp.zeros((SUBSEG, SCAN_LANES), F32)
    else:
        sr, si = init
    for jj in range(sub_len):
        rows = slice(SUBSEG * ((sub_len - 1 - jj) if reverse else jj), SUBSEG * (((sub_len - 1 - jj) if reverse else jj) + 1))
        sr, si = lr * sr - li * si + br_ref[rows, cols], lr * si + li * sr + bi_ref[rows, cols]
        if store:
            br_ref[rows, cols] = sr
            bi_ref[rows, cols] = si
    return sr, si


def _resolve_chunk(z, carry_refs, start_refs, pr_row, pi_row, cols, *, reverse):
    cr, ci = carry_refs[0][0:1, cols], carry_refs[1][0:1, cols]
    pr, pi = pr_row[:, cols], pi_row[:, cols]
    for s in (range(SUBSEG - 1, -1, -1) if reverse else range(SUBSEG)):
        start_refs[0][s:s + 1, cols] = cr
        start_refs[1][s:s + 1, cols] = ci
        cr, ci = pr * cr - pi * ci + z[0][s:s + 1, :], pr * ci + pi * cr + z[1][s:s + 1, :]
    carry_refs[0][0:1, cols] = cr
    carry_refs[1][0:1, cols] = ci


def _param_specs(direction):
    row = lambda q: pl.BlockSpec((None, None, 1, STATE_W), lambda i: (q, direction, 0, 0))
    wide = lambda q: pl.BlockSpec((None, None, N_SLAB, SLAB_IN, SLAB_ST), lambda i: (q, direction, 0, 0, 0))
    tall = lambda q: pl.BlockSpec((None, None, N_SLAB, SLAB_ST, SLAB_IN), lambda i: (q, direction, 0, 0, 0))
    return [row(q) for q in range(4)], [wide(0), wide(1)], [tall(0), tall(1)]


def _ssm_fwd(u, lam, bb, cb, *, direction, tb, name):
    reverse = direction == 1
    seq = u.shape[1]
    nblk = seq // tb
    sub_len = tb // SUBSEG

    def body(u_ref, lr_ref, li_ref, pr_ref, pi_ref, bbr_ref, bbi_ref, cbr_ref, cbi_ref,
             y_ref, sr_ref, si_ref, xr, xi, up, yp, car, cai):
        @pl.when(pl.program_id(0) == 0)
        def _():
            car[...] = jnp.zeros_like(car)
            cai[...] = jnp.zeros_like(cai)

        _permute_rows(up, u_ref, sub_len)
        lr, li, pr, pi = lr_ref[...], li_ref[...], pr_ref[...], pi_ref[...]
        chunk = lambda k: slice(k * SLAB_ST, (k + 1) * SLAB_ST)

        def drive(k):
            ub = up[k].astype(BF16)
            xr[:, chunk(k)] = _dot(ub, bbr_ref[k])
            xi[:, chunk(k)] = _dot(ub, bbi_ref[k])

        def scan(k):
            z = _scan_chunk(xr, xi, lr, li, None, chunk(k), sub_len=sub_len, reverse=reverse, store=False)
            _resolve_chunk(z, (car, cai), (sr_ref, si_ref), pr, pi, chunk(k), reverse=reverse)
            _scan_chunk(xr, xi, lr, li, (sr_ref[:, chunk(k)], si_ref[:, chunk(k)]), chunk(k),
                        sub_len=sub_len, reverse=reverse, store=True)

        def read_out(k):
            yp[k] = _dot(xr[:, chunk(k)].astype(BF16), cbr_ref[k]) - _dot(xi[:, chunk(k)].astype(BF16), cbi_ref[k])

        drive(0)
        for k in range(N_SLAB):
            if k + 1 < N_SLAB:
                drive(k + 1)
            scan(k)
            if k > 0:
                read_out(k - 1)
        read_out(N_SLAB - 1)
        _unpermute_rows(y_ref, yp, sub_len)

    blk = (lambda i: nblk - 1 - i) if reverse else (lambda i: i)
    rows, wide, tall = _param_specs(direction)
    tok = pl.BlockSpec((N_SLAB, tb, SLAB_IN), lambda i: (0, blk(i), 0))
    start_spec = pl.BlockSpec((None, SUBSEG, STATE_W), lambda i: (blk(i), 0, 0))
    return pl.pallas_call(
        body, name=name, grid=(nblk,),
        in_specs=[tok] + rows + wide + tall,
        out_specs=[tok, start_spec, start_spec],
        out_shape=[jax.ShapeDtypeStruct((N_SLAB, seq, SLAB_IN), F32), jax.ShapeDtypeStruct((nblk, SUBSEG, STATE_W), F32),
                   jax.ShapeDtypeStruct((nblk, SUBSEG, STATE_W), F32)],
        scratch_shapes=[pltpu.VMEM((tb, STATE_W), F32), pltpu.VMEM((tb, STATE_W), F32),
                        pltpu.VMEM((N_SLAB, tb, SLAB_IN), F32), pltpu.VMEM((N_SLAB, tb, SLAB_IN), F32),
                        pltpu.VMEM((SUBSEG, STATE_W), F32), pltpu.VMEM((SUBSEG, STATE_W), F32)],
        compiler_params=_cparams(("arbitrary",)),
    )(u, lam, lam, lam, lam, bb, bb, cb, cb)


def _ssm_bwd(u, dy, starts, lam, bb, bbt, cb_t, *, direction, tb, name):
    reverse = direction == 1
    seq = u.shape[1]
    nblk = seq // tb
    sub_len = tb // SUBSEG

    def body(u_ref, dy_ref, sr_ref, si_ref, lr_ref, li_ref, pr_ref, pi_ref, bbr_ref, bbi_ref, btr_ref, bti_ref,
             ctr_ref, cti_ref, du_ref, gb_ref, gc_ref, dl_ref,
             xr, xi, gr, gi, up, dyp, dup, gsr, gsi, car, cai):
        gbr_ref, gbi_ref = gb_ref.at[0], gb_ref.at[1]
        gcr_ref, gci_ref = gc_ref.at[0], gc_ref.at[1]
        dlr_ref, dli_ref = dl_ref.at[0], dl_ref.at[1]

        @pl.when(pl.program_id(0) == 0)
        def _():
            for ref in (car, cai, gbr_ref, gbi_ref, gcr_ref, gci_ref, dlr_ref, dli_ref):
                ref[...] = jnp.zeros_like(ref)

        _permute_rows(up, u_ref, sub_len)
        _permute_rows(dyp, dy_ref, sub_len)
        lr, li, pr, pi = lr_ref[...], li_ref[...], pr_ref[...], pi_ref[...]
        nli, npi = -li, -pi
        chunk = lambda k: slice(k * SLAB_ST, (k + 1) * SLAB_ST)

        def drive(k):
            ub = up[k].astype(BF16)
            xr[:, chunk(k)] = _dot(ub, bbr_ref[k])
            xi[:, chunk(k)] = _dot(ub, bbi_ref[k])
            dyb = dyp[k].astype(BF16)
            gr[:, chunk(k)] = _dot(dyb, ctr_ref[k])
            gi[:, chunk(k)] = -_dot(dyb, cti_ref[k])

        def scan_x(k):
            _scan_chunk(xr, xi, lr, li, (sr_ref[:, chunk(k)], si_ref[:, chunk(k)]), chunk(k),
                        sub_len=sub_len, reverse=reverse, store=True)

        def grad_c(k):
            dyb = dyp[k].astype(BF16)
            gcr_ref[k] += _dot_tn(xr[:, chunk(k)].astype(BF16), dyb)
            gci_ref[k] -= _dot_tn(xi[:, chunk(k)].astype(BF16), dyb)

        def scan_g(k):
            z = _scan_chunk(gr, gi, lr, nli, None, chunk(k), sub_len=sub_len, reverse=not reverse, store=False)
            _resolve_chunk(z, (car, cai), (gsr, gsi), pr, npi, chunk(k), reverse=not reverse)
            _scan_chunk(gr, gi, lr, nli, (gsr[:, chunk(k)], gsi[:, chunk(k)]), chunk(k),
                        sub_len=sub_len, reverse=not reverse, store=True)

        def grad_b_du(k):
            ub = up[k].astype(BF16)
            grb, gib = gr[:, chunk(k)].astype(BF16), gi[:, chunk(k)].astype(BF16)
            gbr_ref[k] += _dot_tn(ub, grb)
            gbi_ref[k] += _dot_tn(ub, gib)
            dup[k] = _dot(grb, btr_ref[k]) + _dot(gib, bti_ref[k])

        def grad_lambda(k):
            cols = chunk(k)
            acc_r, acc_i = dlr_ref[:, cols], dli_ref[:, cols]
            for jj in range(sub_len):
                prev = jj + 1 if reverse else jj - 1
                if 0 <= prev < sub_len:
                    x_r, x_i = xr[SUBSEG * prev:SUBSEG * (prev + 1), cols], xi[SUBSEG * prev:SUBSEG * (prev + 1), cols]
                else:
                    x_r, x_i = sr_ref[:, cols], si_ref[:, cols]
                g_r, g_i = gr[SUBSEG * jj:SUBSEG * (jj + 1), cols], gi[SUBSEG * jj:SUBSEG * (jj + 1), cols]
                acc_r = acc_r + (g_r * x_r + g_i * x_i)
                acc_i = acc_i + (g_i * x_r - g_r * x_i)
            dlr_ref[:, cols] = acc_r
            dli_ref[:, cols] = acc_i

        drive(0)
        for k in range(N_SLAB):
            scan_x(k)
            if k + 1 < N_SLAB:
                drive(k + 1)
            grad_c(k)
            scan_g(k)
            grad_b_du(k)
            grad_lambda(k)
        _unpermute_rows(du_ref, dup, sub_len)

    blk = (lambda i: i) if reverse else (lambda i: nblk - 1 - i)
    rows, wide, tall = _param_specs(direction)
    tok = pl.BlockSpec((N_SLAB, tb, SLAB_IN), lambda i: (0, blk(i), 0))
    start_spec = pl.BlockSpec((None, SUBSEG, STATE_W), lambda i: (blk(i), 0, 0))
    gb_shape, gc_shape, dl_shape = (2, N_SLAB, SLAB_IN, SLAB_ST), (2, N_SLAB, SLAB_ST, SLAB_IN), (2, SUBSEG, STATE_W)
    whole = lambda shape: pl.BlockSpec(shape, lambda i: (0,) * len(shape))
    big = lambda: pltpu.VMEM((tb, STATE_W), F32)
    slabs = lambda: pltpu.VMEM((N_SLAB, tb, SLAB_IN), F32)
    tile = lambda: pltpu.VMEM((SUBSEG, STATE_W), F32)
    return pl.pallas_call(
        body, name=name, grid=(nblk,),
        in_specs=[tok, tok, start_spec, start_spec] + rows + wide + tall + wide,
        out_specs=[tok, whole(gb_shape), whole(gc_shape), whole(dl_shape)],
        out_shape=[jax.ShapeDtypeStruct((N_SLAB, seq, SLAB_IN), F32), jax.ShapeDtypeStruct(gb_shape, F32),
                   jax.ShapeDtypeStruct(gc_shape, F32), jax.ShapeDtypeStruct(dl_shape, F32)],
        scratch_shapes=[big(), big(), big(), big(), slabs(), slabs(), slabs(), tile(), tile(), tile(), tile()],
        compiler_params=_cparams(("arbitrary",)),
    )(u, dy, *starts, lam, lam, lam, lam, bb, bb, bbt, bbt, cb_t, cb_t)


GELU_C = math.sqrt(2.0 / math.pi)
GELU_K = 0.044715
MID_ROW_GROUPS = 1


def _mid(o, za, u, y_f, y_b, zs, x, target, ssm_d, w_glu, b_glu, g_attn, g_ssm, w_out, ln_g, ln_b, tb):
    seq = x.shape[0]

    def body(o_ref, za_ref, u_ref, yf_ref, yb_ref, zs_ref, x_ref, t_ref, d_ref, wg_ref, bg_ref, ga_ref, gs_ref,
             wo_ref, lg_ref, lb_ref,
             loss_ref, do_ref, dza_ref, dyl_ref, dzs_ref, dpre_ref, gwo_ref, gwg_ref, vec_ref, wop):
        @pl.when(pl.program_id(0) == 0)
        def _():
            for ref in (loss_ref, gwo_ref, gwg_ref, vec_ref):
                ref[...] = jnp.zeros_like(ref)
            for nat, par in _pair_blocks(0):
                wop[par, :] = wo_ref[nat, :]
            wop[D_ATTN:, :] = wo_ref[D_ATTN:, :]

        def rows_of(rs):
            o, za = o_ref[rs, :], za_ref[rs, :]
            sig_a = _sigmoid(za)
            silu_a = za * sig_a
            ya = o * silu_a
            r_a = lax.rsqrt(jnp.mean(ya * ya, axis=1, keepdims=True) + NORM_EPS)
            n_a = ya * r_a
            g_a = ga_ref[...]
            unslab = lambda ref: jnp.concatenate([ref[k, rs, :] for k in range(N_SLAB)], axis=1)
            u_blk, zs = unslab(u_ref), zs_ref[rs, :]
            d_row = d_ref[...]
            ylin = d_row * u_blk + unslab(yf_ref) + unslab(yb_ref)
            inner = GELU_C * (ylin + GELU_K * ylin * ylin * ylin)
            th = jnp.tanh(inner)
            gl = 0.5 * ylin * (1.0 + th)
            glb = gl.astype(BF16)
            gate = _dot(glb, wg_ref[...])
            yield
            sg = _sigmoid(gate + bg_ref[...])
            y2 = gl * sg
            sig_s = _sigmoid(zs)
            silu_s = zs * sig_s
            ys = y2 * silu_s
            r_s = lax.rsqrt(jnp.mean(ys * ys, axis=1, keepdims=True) + NORM_EPS)
            n_s = ys * r_s
            g_s = gs_ref[...]
            mixed = jnp.concatenate([n_a * g_a, n_s * g_s], axis=1).astype(BF16)
            out = _dot(mixed, wop[...])
            yield
            pre = ALPHA * x_ref[rs, :] + out
            mu = jnp.mean(pre, axis=1, keepdims=True)
            cen = pre - mu
            rstd = lax.rsqrt(jnp.mean(cen * cen, axis=1, keepdims=True) + NORM_EPS)
            hhat = cen * rstd
            ln_g = lg_ref[...]
            err = hhat * ln_g + lb_ref[...] - t_ref[rs, :]
            loss_ref[...] += 0.5 * jnp.sum(jnp.mean(err * err, axis=1, keepdims=True))

            dh = err * (1.0 / D_MODEL)
            vec_ref[0:1, :] += jnp.sum(dh * hhat, axis=0, keepdims=True)
            vec_ref[1:2, :] += jnp.sum(dh, axis=0, keepdims=True)
            dhh = dh * ln_g
            dpre = rstd * (dhh - jnp.mean(dhh, axis=1, keepdims=True)
                           - hhat * jnp.mean(dhh * hhat, axis=1, keepdims=True))
            dpre_ref[rs, :] = dpre
            dpb = dpre.astype(BF16)
            for j in range(4):
                g_pair = _dot_tn(mixed[:, 128 * j:128 * (j + 1)], dpb)
                for g in range(2):
                    nat = HEAD_DIM * (4 * g + j)
                    gwo_ref[nat:nat + HEAD_DIM, :] += g_pair[HEAD_DIM * g:HEAD_DIM * (g + 1), :]
            gwo_ref[D_ATTN:, :] += _dot_tn(mixed[:, D_ATTN:], dpb)
            dmix = _dot_nt(dpb, wop[...])
            yield
            dna = dmix[:, :D_ATTN]
            vec_ref[2:3, 0:D_ATTN] += jnp.sum(dna * n_a, axis=0, keepdims=True)
            dna = dna * g_a
            dya = r_a * (dna - n_a * jnp.mean(dna * n_a, axis=1, keepdims=True))
            do_ref[rs, :] = dya * silu_a
            dza_ref[rs, :] = dya * o * (sig_a * (1.0 + za * (1.0 - sig_a)))
            dns = dmix[:, D_ATTN:]
            vec_ref[2:3, D_ATTN:] += jnp.sum(dns * n_s, axis=0, keepdims=True)
            dns = dns * g_s
            dys = r_s * (dns - n_s * jnp.mean(dns * n_s, axis=1, keepdims=True))
            dzs_ref[rs, :] = dys * y2 * (sig_s * (1.0 + zs * (1.0 - sig_s)))
            dy2 = dys * silu_s
            da = dy2 * gl * sg * (1.0 - sg)
            vec_ref[3:4, D_SSM:] += jnp.sum(da, axis=0, keepdims=True)
            dab = da.astype(BF16)
            gwg_ref[...] += _dot_tn(glb, dab)
            dgl_mm = _dot_nt(dab, wg_ref[...])
            yield
            dgl = dy2 * sg + dgl_mm
            dylin = dgl * (0.5 * (1.0 + th)
                           + 0.5 * ylin * (1.0 - th * th) * GELU_C * (1.0 + 3.0 * GELU_K * ylin * ylin))
            for k in range(N_SLAB):
                dyl_ref[k, rs, :] = dylin[:, k * SLAB_IN:(k + 1) * SLAB_IN]
            vec_ref[3:4, 0:D_SSM] += jnp.sum(dylin * u_blk, axis=0, keepdims=True)
            yield

        groups = [rows_of(slice(r0, r0 + tb // MID_ROW_GROUPS)) for r0 in range(0, tb, tb // MID_ROW_GROUPS)]
        for _ in range(5):
            for gen in groups:
                next(gen)

    tok = lambda w: pl.BlockSpec((tb, w), lambda i: (i, 0))
    slab = pl.BlockSpec((N_SLAB, tb, SLAB_IN), lambda i: (0, i, 0))
    const = lambda r, c: pl.BlockSpec((r, c), lambda i: (0, 0))
    tok_shape = jax.ShapeDtypeStruct((seq, 512), F32)
    return pl.pallas_call(
        body, name="mid", grid=(seq // tb,),
        in_specs=[tok(512), tok(512), slab, slab, slab, tok(512), tok(1024), tok(1024),
                  const(1, 512), const(512, 512), const(1, 512), const(1, 512), const(1, 512),
                  const(1024, 1024), const(1, 1024), const(1, 1024)],
        out_specs=[const(8, 128), tok(512), tok(512), slab, tok(512), tok(1024),
                   const(1024, 1024), const(512, 512), const(8, 1024)],
        out_shape=[jax.ShapeDtypeStruct((8, 128), F32), tok_shape, tok_shape,
                   jax.ShapeDtypeStruct((N_SLAB, seq, SLAB_IN), F32), tok_shape,
                   jax.ShapeDtypeStruct((seq, 1024), F32), jax.ShapeDtypeStruct((1024, 1024), F32),
                   jax.ShapeDtypeStruct((512, 512), F32), jax.ShapeDtypeStruct((8, 1024), F32)],
        scratch_shapes=[pltpu.VMEM((D_MODEL, D_MODEL), BF16)],
        compiler_params=_cparams(("arbitrary",)),
    )(o, za, u, y_f, y_b, zs, x, target, ssm_d, w_glu, b_glu, g_attn, g_ssm, w_out, ln_g, ln_b)


def _ride_shapes(pieces, narrow, gather_last):
    outs = [p.shape if (gather_last and a == len(pieces) - 1) else p.shape[1:] for a, p in enumerate(pieces)]
    return outs, [pltpu.VMEM(s, F32) for s in outs] + _reduce_scratch([p.shape for p in pieces], narrow)


def _ride_phases(piece_refs, out_refs, scratch_refs, narrow, gather_last):
    n = len(piece_refs)
    landing, rest = scratch_refs[:n], scratch_refs[n:]
    begin, exchange, combine, finish = _reduce_phases(piece_refs, landing, rest[:n], rest[n:2 * n], rest[2 * n:3 * n],
                                                      *rest[3 * n:], narrow, gather_last)

    def end():
        finish()
        for a in range(n):
            out_refs[a][...] = landing[a][...]

    return begin, exchange, combine, end


def _dproj_block(dq_ref, dk_ref, dv_ref, dza_ref, duf_ref, dub_ref, dyl_ref, dzs_ref, d_ref, hi_ref, lo_ref, tb):
    cos, sin = _rope_block(hi_ref, lo_ref, pl.program_id(0) * (tb // ROPE_GROUP), tb // ROPE_GROUP)

    def unrope(t):
        return t * cos + _rotate_half_unsigned(t * sin)

    dq_rot = dq_ref[...]
    pieces = [unrope(dq_rot[:, 128 * j:128 * (j + 1)]) for j in range(4)]
    d_row = d_ref[...]
    pieces += [unrope(dk_ref[...]), dv_ref[...], dza_ref[...]]
    pieces += [duf_ref[k] + dub_ref[k] + d_row[:, k * SLAB_IN:(k + 1) * SLAB_IN] * dyl_ref[k] for k in range(N_SLAB)]
    pieces += [dzs_ref[...]]
    return jnp.concatenate(pieces, axis=1).astype(BF16)


def _dproj_specs(tb, rope_hi, rope_lo):
    tok = lambda w: pl.BlockSpec((tb, w), lambda i: (i, 0))
    slab = pl.BlockSpec((N_SLAB, tb, SLAB_IN), lambda i: (0, i, 0))
    table = lambda t: pl.BlockSpec(t.shape, lambda i: (0, 0, 0))
    return [tok(512), tok(128), tok(128), tok(512), slab, slab, slab, tok(512), pl.BlockSpec((1, 512), lambda i: (0, 0)),
            table(rope_hi), table(rope_lo)]


N_DPROJ = 11


def _proj_bwd_w(x, dproj_args, rope_hi, rope_lo, pieces, tb):
    seq = x.shape[0]
    steps = seq // tb
    n_p = len(pieces)
    narrow = [False] * n_p

    def body(*refs):
        x_ref, grads = refs[0], refs[1:1 + N_DPROJ]
        piece_refs = refs[1 + N_DPROJ:1 + N_DPROJ + n_p]
        gw_ref = refs[1 + N_DPROJ + n_p]
        out_refs = refs[2 + N_DPROJ + n_p:2 + N_DPROJ + 2 * n_p]
        step = pl.program_id(0)
        if n_p:
            begin, exchange, combine, end = _ride_phases(piece_refs, out_refs, refs[2 + N_DPROJ + 2 * n_p:], narrow, True)
            pl.when(step == 0)(begin)
            pl.when(step == min(1, steps - 1))(exchange)
            pl.when(step == steps // 2)(combine)

        @pl.when(step == 0)
        def _():
            gw_ref[...] = jnp.zeros_like(gw_ref)

        dproj = _dproj_block(*grads, tb)
        xb = x_ref[...].astype(BF16)
        for base in (W_Q, W_ZA):
            for j in range(4):
                g_pair = _dot_tn(dproj[:, base + 128 * j:base + 128 * (j + 1)], xb)
                for g in range(2):
                    nat = base + HEAD_DIM * (4 * g + j)
                    gw_ref[nat:nat + HEAD_DIM, :] += g_pair[HEAD_DIM * g:HEAD_DIM * (g + 1), :]
        gw_ref[W_KV:W_ZA, :] += _dot_tn(dproj[:, W_KV:W_ZA], xb)
        gw_ref[W_U:, :] += _dot_tn(dproj[:, W_U:], xb)
        if n_p:
            pl.when(step == steps - 1)(end)

    vmem = pl.BlockSpec(memory_space=pltpu.VMEM)
    whole = pl.BlockSpec((D_IN_PROJ, D_MODEL), lambda i: (0, 0), pipeline_mode=pl.Buffered(1))
    ride_outs, ride_scratch = _ride_shapes(pieces, narrow, True) if n_p else ([], [])
    return pl.pallas_call(
        body, name="proj_bwd_w", grid=(steps,),
        in_specs=[pl.BlockSpec((tb, D_MODEL), lambda i: (i, 0))] + _dproj_specs(tb, rope_hi, rope_lo) + [vmem] * n_p,
        out_specs=[whole] + [vmem] * n_p,
        out_shape=[jax.ShapeDtypeStruct((D_IN_PROJ, D_MODEL), F32)] + [jax.ShapeDtypeStruct(s, F32) for s in ride_outs],
        scratch_shapes=ride_scratch,
        compiler_params=_cparams(("arbitrary",)),
    )(x, *dproj_args, rope_hi, rope_lo, *pieces)


def _proj_bwd_x(dproj_args, rope_hi, rope_lo, dpre, wt, pieces, tb):
    seq = dpre.shape[0]
    steps = seq // tb
    n_p = len(pieces)
    narrow = [True] * n_p

    def body(*refs):
        grads = refs[:N_DPROJ]
        dpre_ref, wt_ref = refs[N_DPROJ:N_DPROJ + 2]
        piece_refs = refs[N_DPROJ + 2:N_DPROJ + 2 + n_p]
        gx_ref = refs[N_DPROJ + 2 + n_p]
        out_refs = refs[N_DPROJ + 3 + n_p:N_DPROJ + 3 + 2 * n_p]
        wp = refs[N_DPROJ + 3 + 2 * n_p]
        step = pl.program_id(0)
        if n_p:
            begin, exchange, combine, end = _ride_phases(piece_refs, out_refs, refs[N_DPROJ + 4 + 2 * n_p:], narrow, False)
            pl.when(step == 0)(begin)
            pl.when(step == min(steps // 3, steps - 1))(exchange)
            pl.when(step == steps - 1)(combine)

        @pl.when(step == 0)
        def _():
            for base in (W_Q, W_ZA):
                for nat, par in _pair_blocks(base):
                    wp[par, :] = wt_ref[nat, :]
            wp[W_KV:W_ZA, :] = wt_ref[W_KV:W_ZA, :]
            wp[W_U:, :] = wt_ref[W_U:, :]

        dproj = _dproj_block(*grads, tb)
        gx_ref[...] = ALPHA * dpre_ref[...] + _dot(dproj, wp[...])
        if n_p:
            pl.when(step == steps - 1)(end)

    vmem = pl.BlockSpec(memory_space=pltpu.VMEM)
    whole = pl.BlockSpec((D_IN_PROJ, D_MODEL), lambda i: (0, 0), pipeline_mode=pl.Buffered(1))
    ride_outs, ride_scratch = _ride_shapes(pieces, narrow, False) if n_p else ([], [])
    return pl.pallas_call(
        body, name="proj_bwd_x", grid=(steps,),
        in_specs=_dproj_specs(tb, rope_hi, rope_lo) + [pl.BlockSpec((tb, D_MODEL), lambda i: (i, 0)), whole] + [vmem] * n_p,
        out_specs=[pl.BlockSpec((tb, D_MODEL), lambda i: (i, 0))] + [vmem] * n_p,
        out_shape=[jax.ShapeDtypeStruct((seq, D_MODEL), F32)] + [jax.ShapeDtypeStruct(s, F32) for s in ride_outs],
        scratch_shapes=[pltpu.VMEM((D_IN_PROJ, D_MODEL), BF16)] + ride_scratch,
        compiler_params=pltpu.CompilerParams(dimension_semantics=("arbitrary",), vmem_limit_bytes=PROJ_BWD_X_VMEM),
    )(*dproj_args, rope_hi, rope_lo, dpre, wt, *pieces)


def _adamw(w, g, m, v, name):
    rows, cols = w.shape
    tb = rows
    while tb * cols * 4 > ADAMW_BLOCK_BYTES and tb % 16 == 0:
        tb //= 2

    def body(w_ref, g_ref, m_ref, v_ref, d_ref, nm_ref, nv_ref):
        _adamw_update(w_ref, g_ref, m_ref, v_ref, d_ref, nm_ref, nv_ref)

    spec = pl.BlockSpec((tb, cols), lambda i: (i, 0))
    return pl.pallas_call(
        body, name=name, grid=(rows // tb,), in_specs=[spec] * 4, out_specs=[spec] * 3,
        out_shape=[jax.ShapeDtypeStruct((rows, cols), F32)] * 3,
        compiler_params=_cparams(("arbitrary",)),
    )(w, g, m, v)


def _adamw_update(w_ref, g_ref, m_ref, v_ref, d_ref, nm_ref, nv_ref):
    g_blk = g_ref[...]
    m_new = ADAM_B1 * m_ref[...] + (1.0 - ADAM_B1) * g_blk
    v_new = ADAM_B2 * v_ref[...] + (1.0 - ADAM_B2) * (g_blk * g_blk)
    m_hat = m_new / (1.0 - ADAM_B1 ** ADAM_STEP)
    v_hat = v_new / (1.0 - ADAM_B2 ** ADAM_STEP)
    d_ref[...] = -ADAM_LR * (m_hat / (jnp.sqrt(v_hat) + ADAM_EPS) + ADAM_WD * w_ref[...])
    nm_ref[...] = m_new
    nv_ref[...] = v_new


def _adamw_many(groups, name):
    n = len(groups)

    def body(*refs):
        for p in range(n):
            _adamw_update(*refs[4 * p:4 * p + 4], *refs[4 * n + 3 * p:4 * n + 3 * p + 3])

    return pl.pallas_call(
        body, name=name,
        out_shape=[jax.ShapeDtypeStruct(grp[0].shape, F32) for grp in groups for _ in range(3)],
    )(*[a for grp in groups for a in grp])


_WEIGHTS = ["w_in", "attn_sink", "ssm_a_re", "ssm_a_im", "ssm_log_dt", "ssm_b_re", "ssm_b_im", "ssm_c_re", "ssm_c_im",
            "ssm_d", "w_glu", "b_glu", "norm_attn_g", "norm_ssm_g", "w_out", "ln_g", "ln_b"]
N_DG = N_DIR * N_GROUPS
BIG_ROWS = N_DG * SSM_CH * SSM_STATE // 128
TINY_ROWS = 64


def _pack_small_grads(g_bc, g_vec, g_ar, g_ai, g_dt, g_sink, loss):
    big = jnp.stack([t.reshape(BIG_ROWS, 128) for t in g_bc])
    row = lambda t: jnp.pad(t.reshape(1, -1), ((0, 0), (0, 128 - t.size)))
    tiny = jnp.concatenate([g_vec.reshape(64, 128), g_ar.reshape(32, 128), g_ai.reshape(32, 128), row(g_dt), row(g_sink),
                            row(loss), jnp.zeros((N_CHIPS * TINY_ROWS - 131, 128), F32)], axis=0)
    return jnp.concatenate([big, tiny.reshape(N_CHIPS, TINY_ROWS, 128)], axis=1)


def _unpack_small_grads(packed):
    big = packed[:, :BIG_ROWS].reshape(N_CHIPS, 2 * BIG_ROWS, SSM_STATE)
    tiny = packed[:, BIG_ROWS:].reshape(N_CHIPS * TINY_ROWS, 128)
    g_vec = tiny[0:64].reshape(8, 1024)
    return tiny[130, 0], {
        "ssm_b_re": big[0], "ssm_b_im": big[1], "ssm_c_re": big[2], "ssm_c_im": big[3],
        "ln_g": g_vec[0:1], "ln_b": g_vec[1:2],
        "norm_attn_g": _from_pair_order(g_vec[2:3, :D_ATTN]), "norm_ssm_g": g_vec[2:3, D_ATTN:],
        "ssm_d": g_vec[3:4, :D_SSM], "b_glu": g_vec[3:4, D_SSM:],
        "ssm_a_re": tiny[64:96].reshape(N_DG, SSM_STATE), "ssm_a_im": tiny[96:128].reshape(N_DG, SSM_STATE),
        "ssm_log_dt": tiny[128:129, :N_DG].reshape(N_DIR, N_GROUPS), "attn_sink": tiny[129:130, :N_Q_HEADS],
    }


def _small_view(name, t):
    if name in ("ssm_b_re", "ssm_b_im"):
        return jnp.swapaxes(t[0], 2, 3).reshape(N_DG * SSM_CH, SSM_STATE)
    if name in ("ssm_c_re", "ssm_c_im"):
        return t.reshape(N_DG * SSM_CH, SSM_STATE)
    if name in ("ssm_a_re", "ssm_a_im"):
        return t.reshape(N_DG, SSM_STATE)
    if name == "ssm_log_dt":
        return t.reshape(N_DIR, N_GROUPS)
    return t.reshape(1, -1)


def _small_unview(name, t, shape):
    if name in ("ssm_b_re", "ssm_b_im"):
        return jnp.swapaxes(t.reshape(N_DIR, N_GROUPS, SSM_CH, SSM_STATE), 2, 3).reshape(shape)
    return t.reshape(shape)


def kernel(x, w_in, attn_sink, ssm_a_re, ssm_a_im, ssm_log_dt, ssm_b_re, ssm_b_im, ssm_c_re, ssm_c_im, ssm_d, w_glu, b_glu, norm_attn_g, norm_ssm_g, w_out, ln_g, ln_b, loss_target, m_w_in, m_attn_sink, m_ssm_a_re, m_ssm_a_im, m_ssm_log_dt, m_ssm_b_re, m_ssm_b_im, m_ssm_c_re, m_ssm_c_im, m_ssm_d, m_w_glu, m_b_glu, m_norm_attn_g, m_norm_ssm_g, m_w_out, m_ln_g, m_ln_b, v_w_in, v_attn_sink, v_ssm_a_re, v_ssm_a_im, v_ssm_log_dt, v_ssm_b_re, v_ssm_b_im, v_ssm_c_re, v_ssm_c_im, v_ssm_d, v_w_glu, v_b_glu, v_norm_attn_g, v_norm_ssm_g, v_w_out, v_ln_g, v_ln_b):
    args = dict(locals())
    weights = {n: args[n] for n in _WEIGHTS}
    mom_m = {n: args["m_" + n] for n in _WEIGHTS}
    mom_v = {n: args["v_" + n] for n in _WEIGHTS}
    xs = x[0]
    target = loss_target[0]

    (wt_g,) = _all_gather_chips([w_in[0].T], BF16, "gather_weights")
    wt_full = wt_g.reshape(D_IN_PROJ, D_MODEL)

    g_x, r_wt, r_w_out, r_w_glu, g_small_all = _local_step(
        xs, target, wt_full, w_glu[0], w_out[0], attn_sink, ssm_a_re, ssm_a_im, ssm_log_dt, ssm_b_re, ssm_b_im,
        ssm_c_re, ssm_c_im, ssm_d, b_glu, norm_attn_g, norm_ssm_g, ln_g, ln_b, sharded=True)
    loss, small_grads = _unpack_small_grads(g_small_all)

    grads, deltas, new_m, new_v = {}, {}, {}, {}
    d_w, m_w, v_w = _adamw(w_in[0].T, r_wt, m_w_in[0].T, v_w_in[0].T, "adamw_w_in")
    grads["w_in"], deltas["w_in"], new_m["w_in"], new_v["w_in"] = r_wt.T[None], d_w.T[None], m_w.T[None], v_w.T[None]
    for n, g in (("w_out", r_w_out), ("w_glu", r_w_glu)):
        d_w, m_w, v_w = _adamw(weights[n][0], g, mom_m[n][0], mom_v[n][0], "adamw_" + n)
        grads[n], deltas[n], new_m[n], new_v[n] = g[None], d_w[None], m_w[None], v_w[None]
    names = sorted(small_grads)
    updates = _adamw_many([(_small_view(n, weights[n]), small_grads[n], _small_view(n, mom_m[n]), _small_view(n, mom_v[n]))
                           for n in names], "adamw_small")
    for i, n in enumerate(names):
        shape = weights[n].shape
        grads[n] = _small_unview(n, small_grads[n], shape)
        deltas[n], new_m[n], new_v[n] = (_small_unview(n, t, shape) for t in updates[3 * i:3 * i + 3])

    return (loss, g_x[None], *[grads[n] for n in _WEIGHTS], *[deltas[n] for n in _WEIGHTS],
            *[new_m[n] for n in _WEIGHTS], *[new_v[n] for n in _WEIGHTS])


def _local_step(xs, target, wt_full, w_glu_in, w_out_in, attn_sink, ssm_a_re, ssm_a_im, ssm_log_dt, ssm_b_re,
                ssm_b_im, ssm_c_re, ssm_c_im, ssm_d, b_glu, norm_attn_g, norm_ssm_g, ln_g, ln_b, sharded):
    seq = xs.shape[0]

    a_r, a_i = _small_view("ssm_a_re", ssm_a_re), _small_view("ssm_a_im", ssm_a_im)
    log_dt = ssm_log_dt.reshape(N_DG, 1)
    b_r, b_i = _small_view("ssm_b_re", ssm_b_re), _small_view("ssm_b_im", ssm_b_im)
    c_r, c_i = _small_view("ssm_c_re", ssm_c_re), _small_view("ssm_c_im", ssm_c_im)
    ssm_tb = min(SSM_BLOCK, seq)
    sub_len = ssm_tb // SUBSEG
    lam, bb, bbt, cb, cb_t = _ssm_params_fwd(a_r, a_i, log_dt, b_r, b_i, c_r, c_i, int(math.log2(sub_len)))
    lam = lam.reshape(4, N_DIR, 1, STATE_W)

    rope_hi, rope_lo = _rope_tables(seq)
    projected = _proj(xs, wt_full, rope_hi, rope_lo, [w_glu_in, w_out_in] if sharded else [], min(512, seq))
    q_stack, k_rot, v_bf, z_attn, u, z_ssm = projected[:6]
    if sharded:
        w_glu_full, w_out_full = projected[6].reshape(D_SSM, D_SSM), projected[7].reshape(D_MODEL, D_MODEL)
    else:
        w_glu_full, w_out_full = w_glu_in, w_out_in
    sink128 = jnp.broadcast_to(attn_sink[0][:, None, None], (N_Q_HEADS, 1, 128))
    attn_bias = _attn_bias()
    o = _attn_fwd(q_stack, k_rot, v_bf, sink128, attn_bias)
    ys, starts = [], []
    for d in range(N_DIR):
        y_d, s_r, s_i = _ssm_fwd(u, lam, bb, cb, direction=d, tb=ssm_tb, name=f"ssm_fwd_{d}")
        ys.append(y_d)
        starts.append((s_r, s_i))

    row = lambda t: t.reshape(1, -1)
    g_attn_p = _to_pair_order(norm_attn_g)
    loss_blk, d_o, d_za, d_ylin, d_zs, d_pre, g_w_out, g_w_glu, g_vec = _mid(
        o, z_attn, u, ys[0], ys[1], z_ssm, xs, target, row(ssm_d), w_glu_full, row(b_glu),
        g_attn_p, row(norm_ssm_g), w_out_full, row(ln_g), row(ln_b), min(256, seq))

    pieces = [g_w_glu.reshape(N_CHIPS, -1, D_SSM), g_w_out.reshape(N_CHIPS, -1, D_MODEL)] if sharded else []
    attn_grads = _attn_bwd(q_stack, k_rot, v_bf, sink128, attn_bias, d_o, pieces)
    dq, dk, dv, g_sink = attn_grads[:4]
    if sharded:
        g_w_glu, g_w_out = attn_grads[4:]
    dus, g_bb, g_cb, g_lam = [], [], [], []
    for d in range(N_DIR):
        du_d, gb_d, gc_d, dl_d = _ssm_bwd(u, d_ylin, starts[d], lam, bb, bbt, cb_t, direction=d, tb=ssm_tb,
                                          name=f"ssm_bwd_{d}")
        dus.append(du_d)
        g_bb.append(gb_d)
        g_cb.append(gc_d)
        g_lam.append(dl_d)
    g_ar, g_ai, g_dt, g_br, g_bi, g_cr, g_ci = _ssm_params_bwd(a_r, a_i, log_dt, b_r, b_i, g_bb, g_cb, g_lam)

    g_small = _pack_small_grads([g_br, g_bi, g_cr, g_ci], g_vec, g_ar, g_ai, g_dt, g_sink[:, 0], loss_blk[0, 0])
    dproj_args = (dq, dk, dv, d_za, dus[0], dus[1], d_ylin, d_zs, row(ssm_d))
    w_grads = _proj_bwd_w(xs, dproj_args, rope_hi, rope_lo, [g_small] if sharded else [], min(512, seq))
    g_wt = w_grads[0]
    if sharded:
        g_small = w_grads[1]
    x_grads = _proj_bwd_x(dproj_args, rope_hi, rope_lo, d_pre, wt_full,
                          [g_wt.reshape(N_CHIPS, -1, D_MODEL)] if sharded else [], min(256, seq))
    g_x = x_grads[0]
    if sharded:
        g_wt = x_grads[1]
    return g_x, g_wt, g_w_out, g_w_glu, g_small
```

```python
import functools
import math

import numpy as np
import jax
import jax.numpy as jnp
from jax import lax
from jax.experimental import pallas as pl
from jax.experimental.pallas import tpu as pltpu

F32 = jnp.float32
BF16 = jnp.bfloat16
MESH = pl.DeviceIdType.MESH

D_MODEL = 1024
D_ATTN = 512
D_SSM = 512
HEAD_DIM = 64
N_Q_HEADS = 8
WINDOW = 128
ROPE_THETA = 10000.0
SSM_CH = 16
N_GROUPS = 32
SSM_STATE = 64
N_DIR = 2
STATE_W = N_GROUPS * SSM_STATE
N_SLAB = 4
SLAB_IN = 128
SLAB_ST = 512
NORM_EPS = 1e-5
NEG_INF = -1e30
ALPHA = 2.0 ** 0.25
D_IN_PROJ = 2304
N_CHIPS = 4

ADAM_LR = 0.001
ADAM_B1 = 0.9
ADAM_B2 = 0.999
ADAM_EPS = 1e-08
ADAM_WD = 0.01
ADAM_STEP = 10

SUBSEG = 8
SCAN_LANES = 512
SSM_BLOCK = 512
VMEM_LIMIT = 48 * 1024 * 1024
ADAMW_BLOCK_BYTES = 3 * 512 * 1024
PROJ_BWD_X_VMEM = 56 * 1024 * 1024

def _to_pair_order(row):
    return jnp.transpose(row.reshape(2, 4, HEAD_DIM), (1, 0, 2)).reshape(1, D_ATTN)


def _from_pair_order(row):
    return jnp.transpose(row.reshape(4, 2, HEAD_DIM), (1, 0, 2)).reshape(1, D_ATTN)


def _cparams(sem=None):
    return pltpu.CompilerParams(dimension_semantics=sem, vmem_limit_bytes=VMEM_LIMIT)


def _dot(a, b):
    return jnp.dot(a, b, preferred_element_type=F32)


def _dot_nt(a, b):
    return lax.dot_general(a, b, (((1,), (1,)), ((), ())), preferred_element_type=F32)


def _dot_tn(a, b):
    return lax.dot_general(a, b, (((0,), (0,)), ((), ())), preferred_element_type=F32)


def _sigmoid(z):
    return 1.0 / (1.0 + jnp.exp(-z))


def _all_gather_chips(shards, out_dtype, name):
    n = len(shards)

    def body(*refs):
        start, relay, finish = _gather_phases(refs[:n], refs[n:2 * n], *refs[2 * n:], out_dtype)
        start()
        relay()
        finish()

    vmem = pl.BlockSpec(memory_space=pltpu.VMEM)
    return pl.pallas_call(
        body, name=name,
        out_shape=[jax.ShapeDtypeStruct((N_CHIPS,) + s.shape, out_dtype) for s in shards],
        in_specs=[vmem] * n, out_specs=[vmem] * n,
        scratch_shapes=_gather_sems(n),
        compiler_params=pltpu.CompilerParams(vmem_limit_bytes=VMEM_LIMIT),
    )(*shards)


def _gather_sems(n):
    return [pltpu.SemaphoreType.DMA((6 * n,)), pltpu.SemaphoreType.DMA((6 * n,))]


def _gather_phases(in_refs, out_refs, send_sems, recv_sems, out_dtype):
    n = len(in_refs)
    x, y, c = lax.axis_index("x"), lax.axis_index("y"), lax.axis_index("c")
    sibling = (x, y, 1 - c)
    chips = [(1 - x, y), (x, 1 - y), (1 - x, 1 - y)]

    def half_of(a, px, py, half):
        rows = in_refs[a].shape[0] // 2
        return out_refs[a].at[2 * px + py, pl.ds(half * rows, rows), :]

    def copy(a, k, px, py, half, to):
        blk = half_of(a, px, py, half)
        return pltpu.make_async_remote_copy(src_ref=blk, dst_ref=blk, send_sem=send_sems.at[6 * a + k],
                                            recv_sem=recv_sems.at[6 * a + k], device_id=to, device_id_type=MESH)

    first = [copy(a, j, x, y, c, (*chips[j], c)) for a in range(n) for j in range(3)]
    passed = [copy(a, 3 + j, *chips[j], c, sibling) for a in range(n) for j in range(3)]

    def start():
        for a in range(n):
            out_refs[a][2 * x + y] = in_refs[a][...].astype(out_dtype)
        for cp in first:
            cp.start()

    def relay():
        for a in range(n):
            for j in range(3):
                copy(a, j, *chips[j], c, (x, y, c)).wait_recv()
                passed[3 * a + j].start()

    def finish():
        for a in range(n):
            for j in range(3):
                copy(a, 3 + j, *chips[j], 1 - c, (x, y, c)).wait_recv()
        for cp in first + passed:
            cp.wait_send()

    return start, relay, finish


SEMS_PER_ARRAY = 11


def _reduce_scratch(shapes, narrow):
    half = [(N_CHIPS, s[1] // 2, s[2]) for s in shapes]
    wire = [BF16 if nar else F32 for nar in narrow]
    n = len(shapes)
    return ([pltpu.VMEM(half[a], F32) for a in range(n)] + [pltpu.VMEM(half[a], wire[a]) for a in range(n)]
            + [pltpu.VMEM(half[a], wire[a]) for a in range(n)]
            + [pltpu.SemaphoreType.DMA((SEMS_PER_ARRAY * n,)), pltpu.SemaphoreType.DMA((SEMS_PER_ARRAY * n,))])


def _reduce_phases(p_refs, out_refs, a_refs, s_refs, b_refs, send_sems, recv_sems, narrow, gather_last):
    n = len(p_refs)
    halves = [p.shape[1] // 2 for p in p_refs]
    wire = [BF16 if nar else F32 for nar in narrow]
    x, y, c = lax.axis_index("x"), lax.axis_index("y"), lax.axis_index("c")
    me = 2 * x + y
    sibling = (x, y, 1 - c)
    chips = [(1 - x, y), (x, 1 - y), (1 - x, 1 - y)]
    slot = [2 * px + py for px, py in chips]
    last = n - 1

    def copy(a, k, src, dst, to):
        return pltpu.make_async_remote_copy(src_ref=src, dst_ref=dst, send_sem=send_sems.at[SEMS_PER_ARRAY * a + k],
                                            recv_sem=recv_sems.at[SEMS_PER_ARRAY * a + k],
                                            device_id=to, device_id_type=MESH)

    def rows(a, half):
        return pl.ds(pl.multiple_of(half * halves[a], 16), halves[a])

    def finished(a, k, half):
        if gather_last and a == last:
            return out_refs[a].at[k, rows(a, half), :]
        return out_refs[a].at[rows(a, half), :]

    swaps = [copy(a, 0, p_refs[a].at[:, rows(a, 1 - c), :], a_refs[a], sibling) for a in range(n)]
    sends = [[copy(a, 1 + j, s_refs[a].at[slot[j]], b_refs[a].at[me], (*chips[j], c)) for j in range(3)] for a in range(n)]
    backs = [copy(a, 4, finished(a, me, c), finished(a, me, c), sibling) for a in range(n)]
    spread = [copy(last, 5 + j, finished(last, me, c), finished(last, me, c), (*chips[j], c)) for j in range(3)]
    relays = [copy(last, 8 + j, finished(last, slot[j], c), finished(last, slot[j], c), sibling) for j in range(3)]

    def start():
        for cp in swaps:
            cp.start()

    def exchange():
        for a in range(n):
            swaps[a].wait_recv()
            for k in range(N_CHIPS):
                acc = a_refs[a][k] + p_refs[a][k, rows(a, c), :]
                a_refs[a][k] = acc
                s_refs[a][k] = acc.astype(wire[a])
            b_refs[a][me] = s_refs[a][me]
            for cp in sends[a]:
                cp.start()

    def combine():
        for a in range(n):
            for j in range(3):
                copy(a, 1 + j, s_refs[a].at[slot[j]], b_refs[a].at[slot[j]], (x, y, c)).wait_recv()
            terms = [jnp.where(me == k, a_refs[a][k], b_refs[a][k].astype(F32)) for k in range(N_CHIPS)]
            total = (terms[0] + terms[1]) + (terms[2] + terms[3])
            if gather_last and a == last:
                out_refs[a][me, rows(a, c), :] = total
            else:
                out_refs[a][rows(a, c), :] = total
            backs[a].start()
        if gather_last:
            for cp in spread:
                cp.start()

    def finish():
        if gather_last:
            for j in range(3):
                copy(last, 5 + j, finished(last, slot[j], c), finished(last, slot[j], c), (x, y, c)).wait_recv()
                relays[j].start()
        for a in range(n):
            copy(a, 4, finished(a, me, 1 - c), finished(a, me, 1 - c), (x, y, c)).wait_recv()
        if gather_last:
            for j in range(3):
                copy(last, 8 + j, finished(last, slot[j], 1 - c), finished(last, slot[j], 1 - c), (x, y, c)).wait_recv()
        for cp in swaps + [cp for group in sends for cp in group] + backs + (spread + relays if gather_last else []):
            cp.wait_send()

    return start, exchange, combine, finish


def _ssm_param_values(ar, ai, logdt):
    dt = jnp.exp(logdt)
    mag = jnp.exp(dt * ar)
    cs, sn = jnp.cos(dt * ai), jnp.sin(dt * ai)
    lr, li = mag * cs, mag * sn
    den = ar * ar + ai * ai
    nr = (lr - 1.0) * ar + li * ai
    ni = li * ar - (lr - 1.0) * ai
    return dt, mag, lr, li, den, nr, ni


GROUPS_PER_SLAB = N_GROUPS // N_SLAB


def _slab_masks():
    def eq(shape, f_row, f_col):
        return (f_row(lax.broadcasted_iota(jnp.int32, shape, 0)) == f_col(lax.broadcasted_iota(jnp.int32, shape, 1))).astype(F32)
    spread = eq((SSM_STATE, SLAB_ST), lambda r: r, lambda c: c % SSM_STATE)
    spread_t = eq((SLAB_ST, SSM_STATE), lambda r: r % SSM_STATE, lambda c: c)
    keep = eq((SLAB_IN, SLAB_ST), lambda r: r // SSM_CH, lambda c: c // SSM_STATE)
    keep_t = eq((SLAB_ST, SLAB_IN), lambda r: r // SSM_STATE, lambda c: c // SSM_CH)
    repeat = eq((N_DG * SSM_CH, N_DG), lambda r: r // SSM_CH, lambda c: c)
    return spread, spread_t, keep, keep_t, repeat


def _split3(t):
    hi = t.astype(BF16)
    rest = t - hi.astype(F32)
    mid = rest.astype(BF16)
    return hi, mid, (rest - mid.astype(F32)).astype(BF16)


def _select(dot, ones01, t, ones_first):
    o = ones01.astype(BF16)
    parts = [dot(o, p) if ones_first else dot(p, o) for p in _split3(t)]
    return (parts[0] + parts[1]) + parts[2]


def _ssm_params_fwd(ar, ai, logdt, br, bi, cr, ci, n_square):
    def body(ar_ref, ai_ref, dt_ref, br_ref, bi_ref, cr_ref, ci_ref, lam_ref, bb_ref, bbt_ref, cb_ref, cbt_ref):
        _, _, lr, li, den, nr, ni = _ssm_param_values(ar_ref[...], ai_ref[...], dt_ref[...])
        lam_ref[0] = lr
        lam_ref[1] = li
        pr, pi = lr, li
        for _ in range(n_square):
            pr, pi = pr * pr - pi * pi, 2.0 * pr * pi
        lam_ref[2] = pr
        lam_ref[3] = pi
        spread, spread_t, keep, keep_t, repeat = _slab_masks()
        fr = _select(_dot, repeat, nr / den, True)
        fi = _select(_dot, repeat, ni / den, True)
        b_r, b_i = br_ref[...], bi_ref[...]
        bbar = (fr * b_r - fi * b_i, fr * b_i + fi * b_r)
        c_par = (cr_ref[...], ci_ref[...])
        spread, spread_t = spread.astype(BF16), spread_t.astype(BF16)
        for src, wide_ref, tall_ref in ((bbar, bb_ref, bbt_ref), (c_par, cbt_ref, cb_ref)):
            for q in range(2):
                for d in range(N_DIR):
                    for k in range(N_SLAB):
                        r0 = (d * N_GROUPS + k * GROUPS_PER_SLAB) * SSM_CH
                        blk = src[q][r0:r0 + SLAB_IN].astype(BF16)
                        wide_ref[q, d, k] = (_dot(blk, spread) * keep).astype(BF16)
                        tall_ref[q, d, k] = (_dot_nt(spread_t, blk) * keep_t).astype(BF16)

    wide = jax.ShapeDtypeStruct((2, N_DIR, N_SLAB, SLAB_IN, SLAB_ST), BF16)
    tall = jax.ShapeDtypeStruct((2, N_DIR, N_SLAB, SLAB_ST, SLAB_IN), BF16)
    return pl.pallas_call(body, name="ssm_params_fwd",
                          out_shape=[jax.ShapeDtypeStruct((4,) + ar.shape, F32), wide, tall, tall, wide],
                          compiler_params=pltpu.CompilerParams(vmem_limit_bytes=VMEM_LIMIT),
                          )(ar, ai, logdt, br, bi, cr, ci)


def _ssm_params_bwd(ar, ai, logdt, br, bi, g_slabs_b, g_slabs_c, g_lam):
    def body(ar_ref, ai_ref, dt_ref, br_ref, bi_ref, gb0_ref, gb1_ref, gc0_ref, gc1_ref, gl0_ref, gl1_ref,
             gar_ref, gai_ref, gdt_ref, gbr_ref, gbi_ref, gcr_ref, gci_ref, dbb, dlam):
        spread, spread_t, keep, keep_t, repeat = _slab_masks()
        for d, (gb_ref, gc_ref) in enumerate(((gb0_ref, gc0_ref), (gb1_ref, gc1_ref))):
            for q in range(2):
                for k in range(N_SLAB):
                    r0 = (d * N_GROUPS + k * GROUPS_PER_SLAB) * SSM_CH
                    dbb[q, r0:r0 + SLAB_IN, :] = _select(_dot, spread_t, gb_ref[q, k] * keep, False)
                    out_ref = gcr_ref if q == 0 else gci_ref
                    out_ref[r0:r0 + SLAB_IN, :] = _select(_dot_tn, spread_t, gc_ref[q, k] * keep_t, False)
        grp = (lax.broadcasted_iota(jnp.int32, (N_GROUPS, STATE_W), 0)
               == lax.broadcasted_iota(jnp.int32, (N_GROUPS, STATE_W), 1) // SSM_STATE).astype(F32)
        pick = (lax.broadcasted_iota(jnp.int32, (STATE_W, SSM_STATE), 0) % SSM_STATE
                == lax.broadcasted_iota(jnp.int32, (STATE_W, SSM_STATE), 1)).astype(F32)
        for d, gl_ref in enumerate((gl0_ref, gl1_ref)):
            for q in range(2):
                row = jnp.sum(gl_ref[q], axis=0, keepdims=True)
                dlam[q, d * N_GROUPS:(d + 1) * N_GROUPS, :] = _select(_dot, pick, grp * row, False)

        a_r, a_i = ar_ref[...], ai_ref[...]
        dt, mag, lr, li, den, nr, ni = _ssm_param_values(a_r, a_i, dt_ref[...])
        fr = _select(_dot, repeat, nr / den, True)
        fi = _select(_dot, repeat, ni / den, True)
        b_r, b_i = br_ref[...], bi_ref[...]
        g_r, g_i = dbb[0], dbb[1]
        gbr_ref[...] = fr * g_r + fi * g_i
        gbi_ref[...] = fr * g_i - fi * g_r
        d_fr = _select(_dot_tn, repeat, b_r * g_r + b_i * g_i, True)
        d_fi = _select(_dot_tn, repeat, b_r * g_i - b_i * g_r, True)
        d_nr, d_ni = d_fr / den, d_fi / den
        d_den = -(d_fr * nr + d_fi * ni) / (den * den)
        d_lr = dlam[0] + d_nr * a_r - d_ni * a_i
        d_li = dlam[1] + d_nr * a_i + d_ni * a_r
        d_ar = d_nr * (lr - 1.0) + d_ni * li + d_den * 2.0 * a_r
        d_ai = d_nr * li - d_ni * (lr - 1.0) + d_den * 2.0 * a_i
        d_mag = (d_lr * lr + d_li * li) / mag
        d_theta = d_li * lr - d_lr * li
        gar_ref[...] = d_ar + d_mag * mag * dt
        gai_ref[...] = d_ai + d_theta * dt
        d_dt = d_mag * mag * a_r + d_theta * a_i
        gdt_ref[...] = jnp.sum(d_dt, axis=1, keepdims=True) * dt

    small = jax.ShapeDtypeStruct(ar.shape, F32)
    big = jax.ShapeDtypeStruct(br.shape, F32)
    return pl.pallas_call(
        body, name="ssm_params_bwd",
        out_shape=[small, small, jax.ShapeDtypeStruct(logdt.shape, F32), big, big, big, big],
        scratch_shapes=[pltpu.VMEM((2,) + br.shape, F32), pltpu.VMEM((2,) + ar.shape, F32)],
        compiler_params=pltpu.CompilerParams(vmem_limit_bytes=VMEM_LIMIT),
    )(ar, ai, logdt, br, bi, *g_slabs_b, *g_slabs_c, *g_lam)


ROPE_GROUP = 128


def _rope_tables(seq):
    half = HEAD_DIM // 2
    inv_freq = jnp.tile(ROPE_THETA ** (-jnp.arange(half, dtype=F32) / half), 4)
    sign = jnp.tile(jnp.concatenate([-jnp.ones((half,), F32), jnp.ones((half,), F32)]), 2)

    def table(pos):
        ang = pos.astype(F32)[:, None] * inv_freq[None, :]
        return jnp.stack([jnp.cos(ang), jnp.sin(ang), sign * jnp.sin(ang)])

    return table(jnp.arange(seq // ROPE_GROUP) * ROPE_GROUP), table(jnp.arange(ROPE_GROUP))


def _rope_block(hi_ref, lo_ref, first_group, n_groups):
    cl, sl, sl_s = lo_ref[0], lo_ref[1], lo_ref[2]
    cos, sin = [], []
    for g in range(n_groups):
        ch, sh, sh_s = (hi_ref[q, pl.ds(first_group + g, 1), :] for q in range(3))
        cos.append(ch * cl - sh * sl)
        sin.append(sh_s * cl + ch * sl_s)
    return jnp.concatenate(cos, axis=0), jnp.concatenate(sin, axis=0)


def _rotate_half_unsigned(t):
    lane = lax.broadcasted_iota(jnp.int32, t.shape, 1)
    return jnp.where((lane % HEAD_DIM) < HEAD_DIM // 2, pltpu.roll(t, 96, 1), pltpu.roll(t, 32, 1))


def _rope(t, cos, sin_signed):
    return t * cos + _rotate_half_unsigned(t) * sin_signed


def _pair_blocks(base):
    out = []
    for j in range(4):
        for g in range(2):
            nat = base + HEAD_DIM * (4 * g + j)
            par = base + 128 * j + HEAD_DIM * g
            out.append((slice(nat, nat + HEAD_DIM), slice(par, par + HEAD_DIM)))
    return out


W_Q, W_KV, W_ZA, W_U, W_ZS = 0, 512, 768, 1280, 1792


def _proj(x, wt, rope_hi, rope_lo, shards, tb):
    seq = x.shape[0]
    steps = seq // tb
    n_sh = len(shards)

    def body(*refs):
        x_ref, wt_ref, hi_ref, lo_ref = refs[:4]
        shard_refs = refs[4:4 + n_sh]
        q_ref, k_ref, v_ref, za_ref, u_ref, zs_ref = refs[4 + n_sh:10 + n_sh]
        gathered_refs = refs[10 + n_sh:10 + 2 * n_sh]
        wp = refs[10 + 2 * n_sh]
        step = pl.program_id(0)
        if n_sh:
            landing_refs = refs[11 + 2 * n_sh:11 + 3 * n_sh]
            start, relay, finish = _gather_phases(shard_refs, landing_refs, *refs[11 + 3 * n_sh:], BF16)
            pl.when(step == 0)(start)
            pl.when(step == max(steps - 2, 0))(relay)

        @pl.when(step == 0)
        def _():
            for dst_base, src_base in ((0, W_Q), (512, W_ZA)):
                for nat, par in _pair_blocks(0):
                    wp[dst_base + par.start:dst_base + par.stop, :] = wt_ref[src_base + nat.start:src_base + nat.stop, :]

        xb = x_ref[...].astype(BF16)
        cos, sin = _rope_block(hi_ref, lo_ref, pl.program_id(0) * (tb // ROPE_GROUP), tb // ROPE_GROUP)
        lo = lax.broadcasted_iota(jnp.int32, (tb, 128), 1) < HEAD_DIM
        q = _dot_nt(xb, wp[0:512, :])
        for j in range(4):
            qj = _rope(q[:, 128 * j:128 * (j + 1)], cos, sin)
            q_ref[j] = jnp.where(lo, qj, 0.0).astype(BF16)
            q_ref[4 + j] = jnp.where(lo, 0.0, qj).astype(BF16)
        kv = _dot_nt(xb, wt_ref[W_KV:W_ZA, :])
        k_ref[...] = _rope(kv[:, 0:128], cos, sin).astype(BF16)
        v_ref[...] = kv[:, 128:256].astype(BF16)
        za_ref[...] = _dot_nt(xb, wp[512:1024, :])
        u_val = _dot_nt(xb, wt_ref[W_U:W_ZS, :])
        for k in range(N_SLAB):
            u_ref[k] = u_val[:, k * SLAB_IN:(k + 1) * SLAB_IN]
        zs_ref[...] = _dot_nt(xb, wt_ref[W_ZS:D_IN_PROJ, :])
        if n_sh:
            @pl.when(step == steps - 1)
            def _():
                finish()
                for a in range(n_sh):
                    gathered_refs[a][...] = landing_refs[a][...]

    row = lambda w: pl.BlockSpec((tb, w), lambda i: (i, 0))
    table = lambda t: pl.BlockSpec(t.shape, lambda i: (0, 0, 0))
    vmem = pl.BlockSpec(memory_space=pltpu.VMEM)
    return pl.pallas_call(
        body, name="proj", grid=(steps,),
        in_specs=[row(D_MODEL), pl.BlockSpec((D_IN_PROJ, D_MODEL), lambda i: (0, 0), pipeline_mode=pl.Buffered(1)),
                  table(rope_hi), table(rope_lo)] + [vmem] * n_sh,
        out_specs=[pl.BlockSpec((8, tb, 128), lambda i: (0, i, 0)), row(128), row(128), row(512),
                   pl.BlockSpec((N_SLAB, tb, SLAB_IN), lambda i: (0, i, 0)), row(512)] + [vmem] * n_sh,
        out_shape=[jax.ShapeDtypeStruct((8, seq, 128), BF16), jax.ShapeDtypeStruct((seq, 128), BF16),
                   jax.ShapeDtypeStruct((seq, 128), BF16), jax.ShapeDtypeStruct((seq, 512), F32),
                   jax.ShapeDtypeStruct((N_SLAB, seq, SLAB_IN), F32), jax.ShapeDtypeStruct((seq, 512), F32)]
        + [jax.ShapeDtypeStruct((N_CHIPS,) + s.shape, BF16) for s in shards],
        scratch_shapes=[pltpu.VMEM((1024, D_MODEL), BF16)] + [pltpu.VMEM((N_CHIPS,) + s.shape, BF16) for s in shards]
        + (_gather_sems(n_sh) if n_sh else []),
        compiler_params=_cparams(("arbitrary",)),
    )(x, wt, rope_hi, rope_lo, *shards)


ATT_TQ = 128
ATT_KEYS = ATT_TQ + 2 * WINDOW


def _attn_window(i, seq):
    start = jnp.clip(i * ATT_TQ - WINDOW, 0, seq - ATT_KEYS)
    return pl.multiple_of(start, WINDOW)


def _attn_bias():
    r = np.arange(ATT_TQ)[None, :, None]
    c = np.arange(ATT_KEYS)[None, None, :]
    off = np.array([0, WINDOW, ATT_KEYS - ATT_TQ])[:, None, None]
    return jnp.asarray(np.where(np.abs(r + off - c) <= WINDOW, 0.0, NEG_INF).astype(np.float32))


def _attn_bias_spec(nblk):
    pick = lambda i: jnp.where(i == 0, 0, jnp.where(i == nblk - 1, 2, 1))
    return pl.BlockSpec((None, ATT_TQ, ATT_KEYS), lambda i: (pick(i), 0, 0))


def _attn_softmax(q_ref, k_ref, v_ref, sink_ref, bias_ref, start):
    kw = k_ref[pl.ds(start, ATT_KEYS), :]
    vw = v_ref[pl.ds(start, ATT_KEYS), :]
    qall = q_ref[...].reshape(N_Q_HEADS * ATT_TQ, 128)
    s = (_dot_nt(qall, kw) * (HEAD_DIM ** -0.5)).reshape(N_Q_HEADS, ATT_TQ, ATT_KEYS) + bias_ref[...][None]
    tiles = [s[:, :, 128 * t:128 * (t + 1)] for t in range(ATT_KEYS // 128)]
    m = jnp.max(functools.reduce(jnp.maximum, tiles), axis=2, keepdims=True)
    sink = sink_ref[...]
    m_b = jnp.maximum(jnp.broadcast_to(m, (N_Q_HEADS, ATT_TQ, 128)), sink)
    p = jnp.concatenate([jnp.exp(t - m_b) for t in tiles], axis=2)
    p_sink = jnp.exp(sink - m_b)
    lo_k = lax.broadcasted_iota(jnp.int32, (ATT_KEYS, 128), 1) < HEAD_DIM
    v_f = vw.astype(F32)
    v_lo, v_hi = jnp.where(lo_k, v_f, 1.0).astype(BF16), jnp.where(lo_k, 1.0, v_f).astype(BF16)
    pb = p.astype(BF16).reshape(N_Q_HEADS * ATT_TQ, ATT_KEYS)
    half = 4 * ATT_TQ
    r = jnp.concatenate([_dot(pb[:half], v_lo), _dot(pb[half:], v_hi)], axis=0).reshape(N_Q_HEADS, ATT_TQ, 128)
    return kw, vw, qall, p, p_sink, r


def _attn_fwd(q_stack, k, v, sink128, bias):
    seq = k.shape[0]

    def body(q_ref, k_ref, v_ref, sink_ref, bias_ref, o_ref):
        start = _attn_window(pl.program_id(0), seq)
        _, _, _, _, p_sink, r = _attn_softmax(q_ref, k_ref, v_ref, sink_ref, bias_ref, start)
        out = r / (pltpu.roll(r, HEAD_DIM, 2) + p_sink)
        lo = lax.broadcasted_iota(jnp.int32, (ATT_TQ, 128), 1) < HEAD_DIM
        for j in range(4):
            o_ref[:, 128 * j:128 * (j + 1)] = jnp.where(lo, out[j], out[4 + j])

    full = lambda w: pl.BlockSpec((seq, w), lambda i: (0, 0))
    return pl.pallas_call(
        body, name="attn_fwd", grid=(seq // ATT_TQ,),
        in_specs=[pl.BlockSpec((8, ATT_TQ, 128), lambda i: (0, i, 0)), full(128), full(128),
                  pl.BlockSpec((N_Q_HEADS, 1, 128), lambda i: (0, 0, 0)), _attn_bias_spec(seq // ATT_TQ)],
        out_specs=pl.BlockSpec((ATT_TQ, 512), lambda i: (i, 0)),
        out_shape=jax.ShapeDtypeStruct((seq, 512), F32),
        compiler_params=_cparams(("arbitrary",)),
    )(q_stack, k, v, sink128, bias)


def _attn_bwd(q_stack, k, v, sink128, bias, d_o, pieces):
    seq = k.shape[0]
    steps = seq // ATT_TQ
    n_p = len(pieces)

    def body(*refs):
        q_ref, k_ref, v_ref, sink_ref, bias_ref, do_ref = refs[:6]
        piece_refs = refs[6:6 + n_p]
        dq_ref, dk_ref, dv_ref, dsink_ref = refs[6 + n_p:10 + n_p]
        reduced_refs = refs[10 + n_p:10 + 2 * n_p]
        sink_acc = refs[10 + 2 * n_p]
        i = pl.program_id(0)
        if n_p:
            landing_refs = refs[11 + 2 * n_p:11 + 3 * n_p]
            scratch = refs[11 + 3 * n_p:]
            begin, exchange, combine, finish = _reduce_phases(
                piece_refs, landing_refs, scratch[:n_p], scratch[n_p:2 * n_p], scratch[2 * n_p:3 * n_p],
                *scratch[3 * n_p:], [True] * n_p, gather_last=False)
            pl.when(i == 0)(begin)
            pl.when(i == min(4, steps - 1))(exchange)
            pl.when(i == (3 * steps) // 4)(combine)

        @pl.when(i == 0)
        def _():
            dk_ref[...] = jnp.zeros_like(dk_ref)
            dv_ref[...] = jnp.zeros_like(dv_ref)
            sink_acc[...] = jnp.zeros_like(sink_acc)

        start = _attn_window(i, seq)
        kw, vw, qall, p, p_sink, r = _attn_softmax(q_ref, k_ref, v_ref, sink_ref, bias_ref, start)
        lo = lax.broadcasted_iota(jnp.int32, (ATT_TQ, 128), 1) < HEAD_DIM
        lo3 = lo[None]
        grp0 = lax.broadcasted_iota(jnp.int32, (N_Q_HEADS, ATT_TQ, 128), 0) < 4
        val = grp0 == lo3
        swapped = pltpu.roll(r, HEAD_DIM, 2)
        inv = 1.0 / (jnp.where(val, swapped, r) + p_sink)
        d_o_blk = do_ref[...]
        do3 = jnp.where(val, jnp.concatenate([d_o_blk[None, :, 128 * j:128 * (j + 1)] for j in range(4)] * 2, axis=0), 0.0)
        t = (do3 * r).reshape(N_Q_HEADS * ATT_TQ, 128)
        t_hi = t.astype(BF16)
        t_lo = (t - t_hi.astype(F32)).astype(BF16)
        ones = jnp.ones((128, 128), BF16)
        delta = (_dot(t_hi, ones) + _dot(t_lo, ones)).reshape(N_Q_HEADS, ATT_TQ, 128) * inv
        sink_acc[...] += -(p_sink * inv) * delta
        do_all = do3.astype(BF16).reshape(N_Q_HEADS * ATT_TQ, 128)
        dp = _dot_nt(do_all, vw).reshape(N_Q_HEADS, ATT_TQ, ATT_KEYS)
        probs, ds = [], []
        for tl in range(ATT_KEYS // 128):
            cols = slice(128 * tl, 128 * (tl + 1))
            probs_t = p[:, :, cols] * inv
            probs.append(probs_t.astype(BF16))
            ds.append((probs_t * (dp[:, :, cols] - delta)).astype(BF16))
        probs_all = jnp.concatenate(probs, axis=2).reshape(N_Q_HEADS * ATT_TQ, ATT_KEYS)
        ds_all = jnp.concatenate(ds, axis=2).reshape(N_Q_HEADS * ATT_TQ, ATT_KEYS)
        scale = HEAD_DIM ** -0.5
        dq_all = (_dot(ds_all, kw) * scale).reshape(N_Q_HEADS, ATT_TQ, 128)
        for j in range(4):
            dq_ref[:, 128 * j:128 * (j + 1)] = jnp.where(lo, dq_all[j], dq_all[4 + j])
        dk_ref[pl.ds(start, ATT_KEYS), :] += _dot_tn(ds_all, qall) * scale
        dv_ref[pl.ds(start, ATT_KEYS), :] += _dot_tn(probs_all, do_all)

        @pl.when(i == steps - 1)
        def _():
            dsink_ref[...] = jnp.sum(sink_acc[...], axis=1)

        if n_p:
            @pl.when(i == steps - 1)
            def _():
                finish()
                for a in range(n_p):
                    reduced_refs[a][...] = landing_refs[a][...]

    full = lambda w: pl.BlockSpec((seq, w), lambda i: (0, 0))
    vmem = pl.BlockSpec(memory_space=pltpu.VMEM)
    return pl.pallas_call(
        body, name="attn_bwd", grid=(steps,),
        in_specs=[pl.BlockSpec((8, ATT_TQ, 128), lambda i: (0, i, 0)), full(128), full(128),
                  pl.BlockSpec((N_Q_HEADS, 1, 128), lambda i: (0, 0, 0)),
                  _attn_bias_spec(steps), pl.BlockSpec((ATT_TQ, 512), lambda i: (i, 0))] + [vmem] * n_p,
        out_specs=[pl.BlockSpec((ATT_TQ, 512), lambda i: (i, 0)), full(128), full(128),
                   pl.BlockSpec((N_Q_HEADS, 128), lambda i: (0, 0))] + [vmem] * n_p,
        out_shape=[jax.ShapeDtypeStruct((seq, 512), F32), jax.ShapeDtypeStruct((seq, 128), F32),
                   jax.ShapeDtypeStruct((seq, 128), F32), jax.ShapeDtypeStruct((N_Q_HEADS, 128), F32)]
        + [jax.ShapeDtypeStruct(p.shape[1:], F32) for p in pieces],
        scratch_shapes=[pltpu.VMEM((N_Q_HEADS, ATT_TQ, 128), F32)] + [pltpu.VMEM(p.shape[1:], F32) for p in pieces]
        + (_reduce_scratch([p.shape for p in pieces], [True] * n_p) if n_p else []),
        compiler_params=_cparams(("arbitrary",)),
    )(q_stack, k, v, sink128, bias, d_o, *pieces)


def _permute_rows(dst_ref, src_ref, sub_len):
    for k in range(N_SLAB):
        for j in range(sub_len):
            dst_ref[k, 8 * j:8 * (j + 1), :] = src_ref.at[k][pl.ds(j, SUBSEG, stride=sub_len), :]


def _unpermute_rows(dst_ref, src_ref, sub_len):
    for k in range(N_SLAB):
        for s in range(SUBSEG):
            dst_ref[k, s * sub_len:(s + 1) * sub_len, :] = src_ref.at[k][pl.ds(s, sub_len, stride=SUBSEG), :]


def _scan_chunk(br_ref, bi_ref, lr_row, li_row, init, cols, *, sub_len, reverse, store):
    lr = jnp.broadcast_to(lr_row[:, cols], (SUBSEG, SCAN_LANES))
    li = jnp.broadcast_to(li_row[:, cols], (SUBSEG, SCAN_LANES))
    if init is None:
        sr = si = jnp.zeros((SUBSEG, SCAN_LANES), F32)
    else:
        sr, si = init
    for jj in range(sub_len):
        rows = slice(SUBSEG * ((sub_len - 1 - jj) if reverse else jj), SUBSEG * (((sub_len - 1 - jj) if reverse else jj) + 1))
        sr, si = lr * sr - li * si + br_ref[rows, cols], lr * si + li * sr + bi_ref[rows, cols]
        if store:
            br_ref[rows, cols] = sr
            bi_ref[rows, cols] = si
    return sr, si


def _resolve_chunk(z, carry_refs, start_refs, pr_row, pi_row, cols, *, reverse):
    cr, ci = carry_refs[0][0:1, cols], carry_refs[1][0:1, cols]
    pr, pi = pr_row[:, cols], pi_row[:, cols]
    for s in (range(SUBSEG - 1, -1, -1) if reverse else range(SUBSEG)):
        start_refs[0][s:s + 1, cols] = cr
        start_refs[1][s:s + 1, cols] = ci
        cr, ci = pr * cr - pi * ci + z[0][s:s + 1, :], pr * ci + pi * cr + z[1][s:s + 1, :]
    carry_refs[0][0:1, cols] = cr
    carry_refs[1][0:1, cols] = ci


def _param_specs(direction):
    row = lambda q: pl.BlockSpec((None, None, 1, STATE_W), lambda i: (q, direction, 0, 0))
    wide = lambda q: pl.BlockSpec((None, None, N_SLAB, SLAB_IN, SLAB_ST), lambda i: (q, direction, 0, 0, 0))
    tall = lambda q: pl.BlockSpec((None, None, N_SLAB, SLAB_ST, SLAB_IN), lambda i: (q, direction, 0, 0, 0))
    return [row(q) for q in range(4)], [wide(0), wide(1)], [tall(0), tall(1)]


def _ssm_fwd(u, lam, bb, cb, *, direction, tb, name):
    reverse = direction == 1
    seq = u.shape[1]
    nblk = seq // tb
    sub_len = tb // SUBSEG

    def body(u_ref, lr_ref, li_ref, pr_ref, pi_ref, bbr_ref, bbi_ref, cbr_ref, cbi_ref,
             y_ref, sr_ref, si_ref, xr, xi, up, yp, car, cai):
        @pl.when(pl.program_id(0) == 0)
        def _():
            car[...] = jnp.zeros_like(car)
            cai[...] = jnp.zeros_like(cai)

        _permute_rows(up, u_ref, sub_len)
        lr, li, pr, pi = lr_ref[...], li_ref[...], pr_ref[...], pi_ref[...]
        chunk = lambda k: slice(k * SLAB_ST, (k + 1) * SLAB_ST)

        def drive(k):
            ub = up[k].astype(BF16)
            xr[:, chunk(k)] = _dot(ub, bbr_ref[k])
            xi[:, chunk(k)] = _dot(ub, bbi_ref[k])

        def scan(k):
            z = _scan_chunk(xr, xi, lr, li, None, chunk(k), sub_len=sub_len, reverse=reverse, store=False)
            _resolve_chunk(z, (car, cai), (sr_ref, si_ref), pr, pi, chunk(k), reverse=reverse)
            _scan_chunk(xr, xi, lr, li, (sr_ref[:, chunk(k)], si_ref[:, chunk(k)]), chunk(k),
                        sub_len=sub_len, reverse=reverse, store=True)

        def read_out(k):
            yp[k] = _dot(xr[:, chunk(k)].astype(BF16), cbr_ref[k]) - _dot(xi[:, chunk(k)].astype(BF16), cbi_ref[k])

        drive(0)
        for k in range(N_SLAB):
            if k + 1 < N_SLAB:
                drive(k + 1)
            scan(k)
            if k > 0:
                read_out(k - 1)
        read_out(N_SLAB - 1)
        _unpermute_rows(y_ref, yp, sub_len)

    blk = (lambda i: nblk - 1 - i) if reverse else (lambda i: i)
    rows, wide, tall = _param_specs(direction)
    tok = pl.BlockSpec((N_SLAB, tb, SLAB_IN), lambda i: (0, blk(i), 0))
    start_spec = pl.BlockSpec((None, SUBSEG, STATE_W), lambda i: (blk(i), 0, 0))
    return pl.pallas_call(
        body, name=name, grid=(nblk,),
        in_specs=[tok] + rows + wide + tall,
        out_specs=[tok, start_spec, start_spec],
        out_shape=[jax.ShapeDtypeStruct((N_SLAB, seq, SLAB_IN), F32), jax.ShapeDtypeStruct((nblk, SUBSEG, STATE_W), F32),
                   jax.ShapeDtypeStruct((nblk, SUBSEG, STATE_W), F32)],
        scratch_shapes=[pltpu.VMEM((tb, STATE_W), F32), pltpu.VMEM((tb, STATE_W), F32),
                        pltpu.VMEM((N_SLAB, tb, SLAB_IN), F32), pltpu.VMEM((N_SLAB, tb, SLAB_IN), F32),
                        pltpu.VMEM((SUBSEG, STATE_W), F32), pltpu.VMEM((SUBSEG, STATE_W), F32)],
        compiler_params=_cparams(("arbitrary",)),
    )(u, lam, lam, lam, lam, bb, bb, cb, cb)


def _ssm_bwd(u, dy, starts, lam, bb, bbt, cb_t, *, direction, tb, name):
    reverse = direction == 1
    seq = u.shape[1]
    nblk = seq // tb
    sub_len = tb // SUBSEG

    def body(u_ref, dy_ref, sr_ref, si_ref, lr_ref, li_ref, pr_ref, pi_ref, bbr_ref, bbi_ref, btr_ref, bti_ref,
             ctr_ref, cti_ref, du_ref, gb_ref, gc_ref, dl_ref,
             xr, xi, gr, gi, up, dyp, dup, gsr, gsi, car, cai):
        gbr_ref, gbi_ref = gb_ref.at[0], gb_ref.at[1]
        gcr_ref, gci_ref = gc_ref.at[0], gc_ref.at[1]
        dlr_ref, dli_ref = dl_ref.at[0], dl_ref.at[1]

        @pl.when(pl.program_id(0) == 0)
        def _():
            for ref in (car, cai, gbr_ref, gbi_ref, gcr_ref, gci_ref, dlr_ref, dli_ref):
                ref[...] = jnp.zeros_like(ref)

        _permute_rows(up, u_ref, sub_len)
        _permute_rows(dyp, dy_ref, sub_len)
        lr, li, pr, pi = lr_ref[...], li_ref[...], pr_ref[...], pi_ref[...]
        nli, npi = -li, -pi
        chunk = lambda k: slice(k * SLAB_ST, (k + 1) * SLAB_ST)

        def drive(k):
            ub = up[k].astype(BF16)
            xr[:, chunk(k)] = _dot(ub, bbr_ref[k])
            xi[:, chunk(k)] = _dot(ub, bbi_ref[k])
            dyb = dyp[k].astype(BF16)
            gr[:, chunk(k)] = _dot(dyb, ctr_ref[k])
            gi[:, chunk(k)] = -_dot(dyb, cti_ref[k])

        def scan_x(k):
            _scan_chunk(xr, xi, lr, li, (sr_ref[:, chunk(k)], si_ref[:, chunk(k)]), chunk(k),
                        sub_len=sub_len, reverse=reverse, store=True)

        def grad_c(k):
            dyb = dyp[k].astype(BF16)
            gcr_ref[k] += _dot_tn(xr[:, chunk(k)].astype(BF16), dyb)
            gci_ref[k] -= _dot_tn(xi[:, chunk(k)].astype(BF16), dyb)

        def scan_g(k):
            z = _scan_chunk(gr, gi, lr, nli, None, chunk(k), sub_len=sub_len, reverse=not reverse, store=False)
            _resolve_chunk(z, (car, cai), (gsr, gsi), pr, npi, chunk(k), reverse=not reverse)
            _scan_chunk(gr, gi, lr, nli, (gsr[:, chunk(k)], gsi[:, chunk(k)]), chunk(k),
                        sub_len=sub_len, reverse=not reverse, store=True)

        def grad_b_du(k):
            ub = up[k].astype(BF16)
            grb, gib = gr[:, chunk(k)].astype(BF16), gi[:, chunk(k)].astype(BF16)
            gbr_ref[k] += _dot_tn(ub, grb)
            gbi_ref[k] += _dot_tn(ub, gib)
            dup[k] = _dot(grb, btr_ref[k]) + _dot(gib, bti_ref[k])

        def grad_lambda(k):
            cols = chunk(k)
            acc_r, acc_i = dlr_ref[:, cols], dli_ref[:, cols]
            for jj in range(sub_len):
                prev = jj + 1 if reverse else jj - 1
                if 0 <= prev < sub_len:
                    x_r, x_i = xr[SUBSEG * prev:SUBSEG * (prev + 1), cols], xi[SUBSEG * prev:SUBSEG * (prev + 1), cols]
                else:
                    x_r, x_i = sr_ref[:, cols], si_ref[:, cols]
                g_r, g_i = gr[SUBSEG * jj:SUBSEG * (jj + 1), cols], gi[SUBSEG * jj:SUBSEG * (jj + 1), cols]
                acc_r = acc_r + (g_r * x_r + g_i * x_i)
                acc_i = acc_i + (g_i * x_r - g_r * x_i)
            dlr_ref[:, cols] = acc_r
            dli_ref[:, cols] = acc_i

        drive(0)
        for k in range(N_SLAB):
            if k + 1 < N_SLAB:
                drive(k + 1)
            scan_x(k)
            grad_c(k)
            scan_g(k)
            grad_b_du(k)
            grad_lambda(k)
        _unpermute_rows(du_ref, dup, sub_len)

    blk = (lambda i: i) if reverse else (lambda i: nblk - 1 - i)
    rows, wide, tall = _param_specs(direction)
    tok = pl.BlockSpec((N_SLAB, tb, SLAB_IN), lambda i: (0, blk(i), 0))
    start_spec = pl.BlockSpec((None, SUBSEG, STATE_W), lambda i: (blk(i), 0, 0))
    gb_shape, gc_shape, dl_shape = (2, N_SLAB, SLAB_IN, SLAB_ST), (2, N_SLAB, SLAB_ST, SLAB_IN), (2, SUBSEG, STATE_W)
    whole = lambda shape: pl.BlockSpec(shape, lambda i: (0,) * len(shape))
    big = lambda: pltpu.VMEM((tb, STATE_W), F32)
    slabs = lambda: pltpu.VMEM((N_SLAB, tb, SLAB_IN), F32)
    tile = lambda: pltpu.VMEM((SUBSEG, STATE_W), F32)
    return pl.pallas_call(
        body, name=name, grid=(nblk,),
        in_specs=[tok, tok, start_spec, start_spec] + rows + wide + tall + wide,
        out_specs=[tok, whole(gb_shape), whole(gc_shape), whole(dl_shape)],
        out_shape=[jax.ShapeDtypeStruct((N_SLAB, seq, SLAB_IN), F32), jax.ShapeDtypeStruct(gb_shape, F32),
                   jax.ShapeDtypeStruct(gc_shape, F32), jax.ShapeDtypeStruct(dl_shape, F32)],
        scratch_shapes=[big(), big(), big(), big(), slabs(), slabs(), slabs(), tile(), tile(), tile(), tile()],
        compiler_params=_cparams(("arbitrary",)),
    )(u, dy, *starts, lam, lam, lam, lam, bb, bb, bbt, bbt, cb_t, cb_t)


GELU_C = math.sqrt(2.0 / math.pi)
GELU_K = 0.044715
MID_ROW_GROUPS = 1


def _mid(o, za, u, y_f, y_b, zs, x, target, ssm_d, w_glu, b_glu, g_attn, g_ssm, w_out, ln_g, ln_b, tb):
    seq = x.shape[0]

    def body(o_ref, za_ref, u_ref, yf_ref, yb_ref, zs_ref, x_ref, t_ref, d_ref, wg_ref, bg_ref, ga_ref, gs_ref,
             wo_ref, lg_ref, lb_ref,
             loss_ref, do_ref, dza_ref, dyl_ref, dzs_ref, dpre_ref, gwo_ref, gwg_ref, vec_ref, wop):
        @pl.when(pl.program_id(0) == 0)
        def _():
            for ref in (loss_ref, gwo_ref, gwg_ref, vec_ref):
                ref[...] = jnp.zeros_like(ref)
            for nat, par in _pair_blocks(0):
                wop[par, :] = wo_ref[nat, :]
            wop[D_ATTN:, :] = wo_ref[D_ATTN:, :]

        def rows_of(rs):
            o, za = o_ref[rs, :], za_ref[rs, :]
            sig_a = _sigmoid(za)
            silu_a = za * sig_a
            ya = o * silu_a
            r_a = lax.rsqrt(jnp.mean(ya * ya, axis=1, keepdims=True) + NORM_EPS)
            n_a = ya * r_a
            g_a = ga_ref[...]
            unslab = lambda ref: jnp.concatenate([ref[k, rs, :] for k in range(N_SLAB)], axis=1)
            u_blk, zs = unslab(u_ref), zs_ref[rs, :]
            d_row = d_ref[...]
            ylin = d_row * u_blk + unslab(yf_ref) + unslab(yb_ref)
            inner = GELU_C * (ylin + GELU_K * ylin * ylin * ylin)
            th = jnp.tanh(inner)
            gl = 0.5 * ylin * (1.0 + th)
            glb = gl.astype(BF16)
            gate = _dot(glb, wg_ref[...])
            yield
            sg = _sigmoid(gate + bg_ref[...])
            y2 = gl * sg
            sig_s = _sigmoid(zs)
            silu_s = zs * sig_s
            ys = y2 * silu_s
            r_s = lax.rsqrt(jnp.mean(ys * ys, axis=1, keepdims=True) + NORM_EPS)
            n_s = ys * r_s
            g_s = gs_ref[...]
            mixed = jnp.concatenate([n_a * g_a, n_s * g_s], axis=1).astype(BF16)
            out = _dot(mixed, wop[...])
            yield
            pre = ALPHA * x_ref[rs, :] + out
            mu = jnp.mean(pre, axis=1, keepdims=True)
            cen = pre - mu
            rstd = lax.rsqrt(jnp.mean(cen * cen, axis=1, keepdims=True) + NORM_EPS)
            hhat = cen * rstd
            ln_g = lg_ref[...]
            err = hhat * ln_g + lb_ref[...] - t_ref[rs, :]
            loss_ref[...] += 0.5 * jnp.sum(jnp.mean(err * err, axis=1, keepdims=True))

            dh = err * (1.0 / D_MODEL)
            vec_ref[0:1, :] += jnp.sum(dh * hhat, axis=0, keepdims=True)
            vec_ref[1:2, :] += jnp.sum(dh, axis=0, keepdims=True)
            dhh = dh * ln_g
            dpre = rstd * (dhh - jnp.mean(dhh, axis=1, keepdims=True)
                           - hhat * jnp.mean(dhh * hhat, axis=1, keepdims=True))
            dpre_ref[rs, :] = dpre
            dpb = dpre.astype(BF16)
            for j in range(4):
                g_pair = _dot_tn(mixed[:, 128 * j:128 * (j + 1)], dpb)
                for g in range(2):
                    nat = HEAD_DIM * (4 * g + j)
                    gwo_ref[nat:nat + HEAD_DIM, :] += g_pair[HEAD_DIM * g:HEAD_DIM * (g + 1), :]
            gwo_ref[D_ATTN:, :] += _dot_tn(mixed[:, D_ATTN:], dpb)
            dmix = _dot_nt(dpb, wop[...])
            yield
            dna = dmix[:, :D_ATTN]
            vec_ref[2:3, 0:D_ATTN] += jnp.sum(dna * n_a, axis=0, keepdims=True)
            dna = dna * g_a
            dya = r_a * (dna - n_a * jnp.mean(dna * n_a, axis=1, keepdims=True))
            do_ref[rs, :] = dya * silu_a
            dza_ref[rs, :] = dya * o * (sig_a * (1.0 + za * (1.0 - sig_a)))
            dns = dmix[:, D_ATTN:]
            vec_ref[2:3, D_ATTN:] += jnp.sum(dns * n_s, axis=0, keepdims=True)
            dns = dns * g_s
            dys = r_s * (dns - n_s * jnp.mean(dns * n_s, axis=1, keepdims=True))
            dzs_ref[rs, :] = dys * y2 * (sig_s * (1.0 + zs * (1.0 - sig_s)))
            dy2 = dys * silu_s
            da = dy2 * gl * sg * (1.0 - sg)
            vec_ref[3:4, D_SSM:] += jnp.sum(da, axis=0, keepdims=True)
            dab = da.astype(BF16)
            gwg_ref[...] += _dot_tn(glb, dab)
            dgl_mm = _dot_nt(dab, wg_ref[...])
            yield
            dgl = dy2 * sg + dgl_mm
            dylin = dgl * (0.5 * (1.0 + th)
                           + 0.5 * ylin * (1.0 - th * th) * GELU_C * (1.0 + 3.0 * GELU_K * ylin * ylin))
            for k in range(N_SLAB):
                dyl_ref[k, rs, :] = dylin[:, k * SLAB_IN:(k + 1) * SLAB_IN]
            vec_ref[3:4, 0:D_SSM] += jnp.sum(dylin * u_blk, axis=0, keepdims=True)
            yield

        groups = [rows_of(slice(r0, r0 + tb // MID_ROW_GROUPS)) for r0 in range(0, tb, tb // MID_ROW_GROUPS)]
        for _ in range(5):
            for gen in groups:
                next(gen)

    tok = lambda w: pl.BlockSpec((tb, w), lambda i: (i, 0))
    slab = pl.BlockSpec((N_SLAB, tb, SLAB_IN), lambda i: (0, i, 0))
    const = lambda r, c: pl.BlockSpec((r, c), lambda i: (0, 0), pipeline_mode=pl.Buffered(1))
    tok_shape = jax.ShapeDtypeStruct((seq, 512), F32)
    return pl.pallas_call(
        body, name="mid", grid=(seq // tb,),
        in_specs=[tok(512), tok(512), slab, slab, slab, tok(512), tok(1024), tok(1024),
                  const(1, 512), const(512, 512), const(1, 512), const(1, 512), const(1, 512),
                  const(1024, 1024), const(1, 1024), const(1, 1024)],
        out_specs=[const(8, 128), tok(512), tok(512), slab, tok(512), tok(1024),
                   const(1024, 1024), const(512, 512), const(8, 1024)],
        out_shape=[jax.ShapeDtypeStruct((8, 128), F32), tok_shape, tok_shape,
                   jax.ShapeDtypeStruct((N_SLAB, seq, SLAB_IN), F32), tok_shape,
                   jax.ShapeDtypeStruct((seq, 1024), F32), jax.ShapeDtypeStruct((1024, 1024), F32),
                   jax.ShapeDtypeStruct((512, 512), F32), jax.ShapeDtypeStruct((8, 1024), F32)],
        scratch_shapes=[pltpu.VMEM((D_MODEL, D_MODEL), BF16)],
        compiler_params=_cparams(("arbitrary",)),
    )(o, za, u, y_f, y_b, zs, x, target, ssm_d, w_glu, b_glu, g_attn, g_ssm, w_out, ln_g, ln_b)


def _ride_shapes(pieces, narrow, gather_last):
    outs = [p.shape if (gather_last and a == len(pieces) - 1) else p.shape[1:] for a, p in enumerate(pieces)]
    return outs, [pltpu.VMEM(s, F32) for s in outs] + _reduce_scratch([p.shape for p in pieces], narrow)


def _ride_phases(piece_refs, out_refs, scratch_refs, narrow, gather_last):
    n = len(piece_refs)
    landing, rest = scratch_refs[:n], scratch_refs[n:]
    begin, exchange, combine, finish = _reduce_phases(piece_refs, landing, rest[:n], rest[n:2 * n], rest[2 * n:3 * n],
                                                      *rest[3 * n:], narrow, gather_last)

    def end():
        finish()
        for a in range(n):
            out_refs[a][...] = landing[a][...]

    return begin, exchange, combine, end


def _dproj_block(dq_ref, dk_ref, dv_ref, dza_ref, duf_ref, dub_ref, dyl_ref, dzs_ref, d_ref, hi_ref, lo_ref, tb):
    cos, sin = _rope_block(hi_ref, lo_ref, pl.program_id(0) * (tb // ROPE_GROUP), tb // ROPE_GROUP)

    def unrope(t):
        return t * cos + _rotate_half_unsigned(t * sin)

    dq_rot = dq_ref[...]
    pieces = [unrope(dq_rot[:, 128 * j:128 * (j + 1)]) for j in range(4)]
    d_row = d_ref[...]
    pieces += [unrope(dk_ref[...]), dv_ref[...], dza_ref[...]]
    pieces += [duf_ref[k] + dub_ref[k] + d_row[:, k * SLAB_IN:(k + 1) * SLAB_IN] * dyl_ref[k] for k in range(N_SLAB)]
    pieces += [dzs_ref[...]]
    return jnp.concatenate(pieces, axis=1).astype(BF16)


def _dproj_specs(tb, rope_hi, rope_lo):
    tok = lambda w: pl.BlockSpec((tb, w), lambda i: (i, 0))
    slab = pl.BlockSpec((N_SLAB, tb, SLAB_IN), lambda i: (0, i, 0))
    table = lambda t: pl.BlockSpec(t.shape, lambda i: (0, 0, 0))
    return [tok(512), tok(128), tok(128), tok(512), slab, slab, slab, tok(512), pl.BlockSpec((1, 512), lambda i: (0, 0)),
            table(rope_hi), table(rope_lo)]


N_DPROJ = 11


def _proj_bwd_w(x, dproj_args, rope_hi, rope_lo, pieces, tb):
    seq = x.shape[0]
    steps = seq // tb
    n_p = len(pieces)
    narrow = [False] * n_p

    def body(*refs):
        x_ref, grads = refs[0], refs[1:1 + N_DPROJ]
        piece_refs = refs[1 + N_DPROJ:1 + N_DPROJ + n_p]
        gw_ref = refs[1 + N_DPROJ + n_p]
        out_refs = refs[2 + N_DPROJ + n_p:2 + N_DPROJ + 2 * n_p]
        step = pl.program_id(0)
        if n_p:
            begin, exchange, combine, end = _ride_phases(piece_refs, out_refs, refs[2 + N_DPROJ + 2 * n_p:], narrow, True)
            pl.when(step == 0)(begin)
            pl.when(step == min(1, steps - 1))(exchange)
            pl.when(step == steps // 2)(combine)

        @pl.when(step == 0)
        def _():
            gw_ref[...] = jnp.zeros_like(gw_ref)

        dproj = _dproj_block(*grads, tb)
        xb = x_ref[...].astype(BF16)
        for base in (W_Q, W_ZA):
            for j in range(4):
                g_pair = _dot_tn(dproj[:, base + 128 * j:base + 128 * (j + 1)], xb)
                for g in range(2):
                    nat = base + HEAD_DIM * (4 * g + j)
                    gw_ref[nat:nat + HEAD_DIM, :] += g_pair[HEAD_DIM * g:HEAD_DIM * (g + 1), :]
        gw_ref[W_KV:W_ZA, :] += _dot_tn(dproj[:, W_KV:W_ZA], xb)
        gw_ref[W_U:, :] += _dot_tn(dproj[:, W_U:], xb)
        if n_p:
            pl.when(step == steps - 1)(end)

    vmem = pl.BlockSpec(memory_space=pltpu.VMEM)
    whole = pl.BlockSpec((D_IN_PROJ, D_MODEL), lambda i: (0, 0), pipeline_mode=pl.Buffered(1))
    ride_outs, ride_scratch = _ride_shapes(pieces, narrow, True) if n_p else ([], [])
    return pl.pallas_call(
        body, name="proj_bwd_w", grid=(steps,),
        in_specs=[pl.BlockSpec((tb, D_MODEL), lambda i: (i, 0))] + _dproj_specs(tb, rope_hi, rope_lo) + [vmem] * n_p,
        out_specs=[whole] + [vmem] * n_p,
        out_shape=[jax.ShapeDtypeStruct((D_IN_PROJ, D_MODEL), F32)] + [jax.ShapeDtypeStruct(s, F32) for s in ride_outs],
        scratch_shapes=ride_scratch,
        compiler_params=_cparams(("arbitrary",)),
    )(x, *dproj_args, rope_hi, rope_lo, *pieces)


def _proj_bwd_x(dproj_args, rope_hi, rope_lo, dpre, wt, pieces, tb):
    seq = dpre.shape[0]
    steps = seq // tb
    n_p = len(pieces)
    narrow = [True] * n_p

    def body(*refs):
        grads = refs[:N_DPROJ]
        dpre_ref, wt_ref = refs[N_DPROJ:N_DPROJ + 2]
        piece_refs = refs[N_DPROJ + 2:N_DPROJ + 2 + n_p]
        gx_ref = refs[N_DPROJ + 2 + n_p]
        out_refs = refs[N_DPROJ + 3 + n_p:N_DPROJ + 3 + 2 * n_p]
        wp = refs[N_DPROJ + 3 + 2 * n_p]
        step = pl.program_id(0)
        if n_p:
            begin, exchange, combine, end = _ride_phases(piece_refs, out_refs, refs[N_DPROJ + 4 + 2 * n_p:], narrow, False)
            pl.when(step == 0)(begin)
            pl.when(step == min(steps // 4, steps - 1))(exchange)
            pl.when(step == steps - 1)(combine)

        @pl.when(step == 0)
        def _():
            for base in (W_Q, W_ZA):
                for nat, par in _pair_blocks(base):
                    wp[par, :] = wt_ref[nat, :]
            wp[W_KV:W_ZA, :] = wt_ref[W_KV:W_ZA, :]
            wp[W_U:, :] = wt_ref[W_U:, :]

        dproj = _dproj_block(*grads, tb)
        gx_ref[...] = ALPHA * dpre_ref[...] + _dot(dproj, wp[...])
        if n_p:
            pl.when(step == steps - 1)(end)

    vmem = pl.BlockSpec(memory_space=pltpu.VMEM)
    whole = pl.BlockSpec((D_IN_PROJ, D_MODEL), lambda i: (0, 0), pipeline_mode=pl.Buffered(1))
    ride_outs, ride_scratch = _ride_shapes(pieces, narrow, False) if n_p else ([], [])
    return pl.pallas_call(
        body, name="proj_bwd_x", grid=(steps,),
        in_specs=_dproj_specs(tb, rope_hi, rope_lo) + [pl.BlockSpec((tb, D_MODEL), lambda i: (i, 0)), whole] + [vmem] * n_p,
        out_specs=[pl.BlockSpec((tb, D_MODEL), lambda i: (i, 0))] + [vmem] * n_p,
        out_shape=[jax.ShapeDtypeStruct((seq, D_MODEL), F32)] + [jax.ShapeDtypeStruct(s, F32) for s in ride_outs],
        scratch_shapes=[pltpu.VMEM((D_IN_PROJ, D_MODEL), BF16)] + ride_scratch,
        compiler_params=pltpu.CompilerParams(dimension_semantics=("arbitrary",), vmem_limit_bytes=PROJ_BWD_X_VMEM),
    )(*dproj_args, rope_hi, rope_lo, dpre, wt, *pieces)


def _adamw(w, g, m, v, name):
    rows, cols = w.shape
    tb = rows
    while tb * cols * 4 > ADAMW_BLOCK_BYTES and tb % 16 == 0:
        tb //= 2

    def body(w_ref, g_ref, m_ref, v_ref, d_ref, nm_ref, nv_ref):
        _adamw_update(w_ref, g_ref, m_ref, v_ref, d_ref, nm_ref, nv_ref)

    spec = pl.BlockSpec((tb, cols), lambda i: (i, 0))
    return pl.pallas_call(
        body, name=name, grid=(rows // tb,), in_specs=[spec] * 4, out_specs=[spec] * 3,
        out_shape=[jax.ShapeDtypeStruct((rows, cols), F32)] * 3,
        compiler_params=_cparams(("arbitrary",)),
    )(w, g, m, v)


def _adamw_update(w_ref, g_ref, m_ref, v_ref, d_ref, nm_ref, nv_ref):
    g_blk = g_ref[...]
    m_new = ADAM_B1 * m_ref[...] + (1.0 - ADAM_B1) * g_blk
    v_new = ADAM_B2 * v_ref[...] + (1.0 - ADAM_B2) * (g_blk * g_blk)
    m_hat = m_new / (1.0 - ADAM_B1 ** ADAM_STEP)
    v_hat = v_new / (1.0 - ADAM_B2 ** ADAM_STEP)
    d_ref[...] = -ADAM_LR * (m_hat / (jnp.sqrt(v_hat) + ADAM_EPS) + ADAM_WD * w_ref[...])
    nm_ref[...] = m_new
    nv_ref[...] = v_new


def _adamw_many(groups, name):
    n = len(groups)

    def body(*refs):
        for p in range(n):
            _adamw_update(*refs[4 * p:4 * p + 4], *refs[4 * n + 3 * p:4 * n + 3 * p + 3])

    return pl.pallas_call(
        body, name=name,
        out_shape=[jax.ShapeDtypeStruct(grp[0].shape, F32) for grp in groups for _ in range(3)],
    )(*[a for grp in groups for a in grp])


_WEIGHTS = ["w_in", "attn_sink", "ssm_a_re", "ssm_a_im", "ssm_log_dt", "ssm_b_re", "ssm_b_im", "ssm_c_re", "ssm_c_im",
            "ssm_d", "w_glu", "b_glu", "norm_attn_g", "norm_ssm_g", "w_out", "ln_g", "ln_b"]
N_DG = N_DIR * N_GROUPS
BIG_ROWS = N_DG * SSM_CH * SSM_STATE // 128
TINY_ROWS = 64


def _pack_small_grads(g_bc, g_vec, g_ar, g_ai, g_dt, g_sink, loss):
    big = jnp.stack([t.reshape(BIG_ROWS, 128) for t in g_bc])
    row = lambda t: jnp.pad(t.reshape(1, -1), ((0, 0), (0, 128 - t.size)))
    tiny = jnp.concatenate([g_vec.reshape(64, 128), g_ar.reshape(32, 128), g_ai.reshape(32, 128), row(g_dt), row(g_sink),
                            row(loss), jnp.zeros((N_CHIPS * TINY_ROWS - 131, 128), F32)], axis=0)
    return jnp.concatenate([big, tiny.reshape(N_CHIPS, TINY_ROWS, 128)], axis=1)


def _unpack_small_grads(packed):
    big = packed[:, :BIG_ROWS].reshape(N_CHIPS, 2 * BIG_ROWS, SSM_STATE)
    tiny = packed[:, BIG_ROWS:].reshape(N_CHIPS * TINY_ROWS, 128)
    g_vec = tiny[0:64].reshape(8, 1024)
    return tiny[130, 0], {
        "ssm_b_re": big[0], "ssm_b_im": big[1], "ssm_c_re": big[2], "ssm_c_im": big[3],
        "ln_g": g_vec[0:1], "ln_b": g_vec[1:2],
        "norm_attn_g": _from_pair_order(g_vec[2:3, :D_ATTN]), "norm_ssm_g": g_vec[2:3, D_ATTN:],
        "ssm_d": g_vec[3:4, :D_SSM], "b_glu": g_vec[3:4, D_SSM:],
        "ssm_a_re": tiny[64:96].reshape(N_DG, SSM_STATE), "ssm_a_im": tiny[96:128].reshape(N_DG, SSM_STATE),
        "ssm_log_dt": tiny[128:129, :N_DG].reshape(N_DIR, N_GROUPS), "attn_sink": tiny[129:130, :N_Q_HEADS],
    }


def _small_view(name, t):
    if name in ("ssm_b_re", "ssm_b_im"):
        return jnp.swapaxes(t[0], 2, 3).reshape(N_DG * SSM_CH, SSM_STATE)
    if name in ("ssm_c_re", "ssm_c_im"):
        return t.reshape(N_DG * SSM_CH, SSM_STATE)
    if name in ("ssm_a_re", "ssm_a_im"):
        return t.reshape(N_DG, SSM_STATE)
    if name == "ssm_log_dt":
        return t.reshape(N_DIR, N_GROUPS)
    return t.reshape(1, -1)


def _small_unview(name, t, shape):
    if name in ("ssm_b_re", "ssm_b_im"):
        return jnp.swapaxes(t.reshape(N_DIR, N_GROUPS, SSM_CH, SSM_STATE), 2, 3).reshape(shape)
    return t.reshape(shape)


def kernel(x, w_in, attn_sink, ssm_a_re, ssm_a_im, ssm_log_dt, ssm_b_re, ssm_b_im, ssm_c_re, ssm_c_im, ssm_d, w_glu, b_glu, norm_attn_g, norm_ssm_g, w_out, ln_g, ln_b, loss_target, m_w_in, m_attn_sink, m_ssm_a_re, m_ssm_a_im, m_ssm_log_dt, m_ssm_b_re, m_ssm_b_im, m_ssm_c_re, m_ssm_c_im, m_ssm_d, m_w_glu, m_b_glu, m_norm_attn_g, m_norm_ssm_g, m_w_out, m_ln_g, m_ln_b, v_w_in, v_attn_sink, v_ssm_a_re, v_ssm_a_im, v_ssm_log_dt, v_ssm_b_re, v_ssm_b_im, v_ssm_c_re, v_ssm_c_im, v_ssm_d, v_w_glu, v_b_glu, v_norm_attn_g, v_norm_ssm_g, v_w_out, v_ln_g, v_ln_b):
    args = dict(locals())
    weights = {n: args[n] for n in _WEIGHTS}
    mom_m = {n: args["m_" + n] for n in _WEIGHTS}
    mom_v = {n: args["v_" + n] for n in _WEIGHTS}
    xs = x[0]
    target = loss_target[0]

    (wt_g,) = _all_gather_chips([w_in[0].T], BF16, "gather_weights")
    wt_full = wt_g.reshape(D_IN_PROJ, D_MODEL)

    g_x, r_wt, r_w_out, r_w_glu, g_small_all = _local_step(
        xs, target, wt_full, w_glu[0], w_out[0], attn_sink, ssm_a_re, ssm_a_im, ssm_log_dt, ssm_b_re, ssm_b_im,
        ssm_c_re, ssm_c_im, ssm_d, b_glu, norm_attn_g, norm_ssm_g, ln_g, ln_b, sharded=True)
    loss, small_grads = _unpack_small_grads(g_small_all)

    grads, deltas, new_m, new_v = {}, {}, {}, {}
    d_w, m_w, v_w = _adamw(w_in[0].T, r_wt, m_w_in[0].T, v_w_in[0].T, "adamw_w_in")
    grads["w_in"], deltas["w_in"], new_m["w_in"], new_v["w_in"] = r_wt.T[None], d_w.T[None], m_w.T[None], v_w.T[None]
    for n, g in (("w_out", r_w_out), ("w_glu", r_w_glu)):
        d_w, m_w, v_w = _adamw(weights[n][0], g, mom_m[n][0], mom_v[n][0], "adamw_" + n)
        grads[n], deltas[n], new_m[n], new_v[n] = g[None], d_w[None], m_w[None], v_w[None]
    names = sorted(small_grads)
    updates = _adamw_many([(_small_view(n, weights[n]), small_grads[n], _small_view(n, mom_m[n]), _small_view(n, mom_v[n]))
                           for n in names], "adamw_small")
    for i, n in enumerate(names):
        shape = weights[n].shape
        grads[n] = _small_unview(n, small_grads[n], shape)
        deltas[n], new_m[n], new_v[n] = (_small_unview(n, t, shape) for t in updates[3 * i:3 * i + 3])

    return (loss, g_x[None], *[grads[n] for n in _WEIGHTS], *[deltas[n] for n in _WEIGHTS],
            *[new_m[n] for n in _WEIGHTS], *[new_v[n] for n in _WEIGHTS])


def _local_step(xs, target, wt_full, w_glu_in, w_out_in, attn_sink, ssm_a_re, ssm_a_im, ssm_log_dt, ssm_b_re,
                ssm_b_im, ssm_c_re, ssm_c_im, ssm_d, b_glu, norm_attn_g, norm_ssm_g, ln_g, ln_b, sharded):
    seq = xs.shape[0]

    a_r, a_i = _small_view("ssm_a_re", ssm_a_re), _small_view("ssm_a_im", ssm_a_im)
    log_dt = ssm_log_dt.reshape(N_DG, 1)
    b_r, b_i = _small_view("ssm_b_re", ssm_b_re), _small_view("ssm_b_im", ssm_b_im)
    c_r, c_i = _small_view("ssm_c_re", ssm_c_re), _small_view("ssm_c_im", ssm_c_im)
    ssm_tb = min(SSM_BLOCK, seq)
    sub_len = ssm_tb // SUBSEG
    lam, bb, bbt, cb, cb_t = _ssm_params_fwd(a_r, a_i, log_dt, b_r, b_i, c_r, c_i, int(math.log2(sub_len)))
    lam = lam.reshape(4, N_DIR, 1, STATE_W)

    rope_hi, rope_lo = _rope_tables(seq)
    projected = _proj(xs, wt_full, rope_hi, rope_lo, [w_glu_in, w_out_in] if sharded else [], min(512, seq))
    q_stack, k_rot, v_bf, z_attn, u, z_ssm = projected[:6]
    if sharded:
        w_glu_full, w_out_full = projected[6].reshape(D_SSM, D_SSM), projected[7].reshape(D_MODEL, D_MODEL)
    else:
        w_glu_full, w_out_full = w_glu_in, w_out_in
    sink128 = jnp.broadcast_to(attn_sink[0][:, None, None], (N_Q_HEADS, 1, 128))
    attn_bias = _attn_bias()
    o = _attn_fwd(q_stack, k_rot, v_bf, sink128, attn_bias)
    ys, starts = [], []
    for d in range(N_DIR):
        y_d, s_r, s_i = _ssm_fwd(u, lam, bb, cb, direction=d, tb=ssm_tb, name=f"ssm_fwd_{d}")
        ys.append(y_d)
        starts.append((s_r, s_i))

    row = lambda t: t.reshape(1, -1)
    g_attn_p = _to_pair_order(norm_attn_g)
    loss_blk, d_o, d_za, d_ylin, d_zs, d_pre, g_w_out, g_w_glu, g_vec = _mid(
        o, z_attn, u, ys[0], ys[1], z_ssm, xs, target, row(ssm_d), w_glu_full, row(b_glu),
        g_attn_p, row(norm_ssm_g), w_out_full, row(ln_g), row(ln_b), min(256, seq))

    pieces = [g_w_glu.reshape(N_CHIPS, -1, D_SSM), g_w_out.reshape(N_CHIPS, -1, D_MODEL)] if sharded else []
    attn_grads = _attn_bwd(q_stack, k_rot, v_bf, sink128, attn_bias, d_o, pieces)
    dq, dk, dv, g_sink = attn_grads[:4]
    if sharded:
        g_w_glu, g_w_out = attn_grads[4:]
    dus, g_bb, g_cb, g_lam = [], [], [], []
    for d in range(N_DIR):
        du_d, gb_d, gc_d, dl_d = _ssm_bwd(u, d_ylin, starts[d], lam, bb, bbt, cb_t, direction=d, tb=ssm_tb,
                                          name=f"ssm_bwd_{d}")
        dus.append(du_d)
        g_bb.append(gb_d)
        g_cb.append(gc_d)
        g_lam.append(dl_d)
    g_ar, g_ai, g_dt, g_br, g_bi, g_cr, g_ci = _ssm_params_bwd(a_r, a_i, log_dt, b_r, b_i, g_bb, g_cb, g_lam)

    g_small = _pack_small_grads([g_br, g_bi, g_cr, g_ci], g_vec, g_ar, g_ai, g_dt, g_sink[:, 0], loss_blk[0, 0])
    dproj_args = (dq, dk, dv, d_za, dus[0], dus[1], d_ylin, d_zs, row(ssm_d))
    w_grads = _proj_bwd_w(xs, dproj_args, rope_hi, rope_lo, [g_small] if sharded else [], min(512, seq))
    g_wt = w_grads[0]
    if sharded:
        g_small = w_grads[1]
    x_grads = _proj_bwd_x(dproj_args, rope_hi, rope_lo, d_pre, wt_full,
                          [g_wt.reshape(N_CHIPS, -1, D_MODEL)] if sharded else [], min(256, seq))
    g_x = x_grads[0]
    if sharded:
        g_wt = x_grads[1]
    return g_x, g_wt, g_w_out, g_w_glu, g_small
```

```python
import functools
import math

import numpy as np
import jax
import jax.numpy as jnp
from jax import lax
from jax.experimental import pallas as pl
from jax.experimental.pallas import tpu as pltpu

F32 = jnp.float32
BF16 = jnp.bfloat16
MESH = pl.DeviceIdType.MESH

D_MODEL = 1024
D_ATTN = 512
D_SSM = 512
HEAD_DIM = 64
N_Q_HEADS = 8
WINDOW = 128
ROPE_THETA = 10000.0
SSM_CH = 16
N_GROUPS = 32
SSM_STATE = 64
N_DIR = 2
STATE_W = N_GROUPS * SSM_STATE
N_SLAB = 4
SLAB_IN = 128
SLAB_ST = 512
NORM_EPS = 1e-5
NEG_INF = -1e30
ALPHA = 2.0 ** 0.25
D_IN_PROJ = 2304
N_CHIPS = 4

ADAM_LR = 0.001
ADAM_B1 = 0.9
ADAM_B2 = 0.999
ADAM_EPS = 1e-08
ADAM_WD = 0.01
ADAM_STEP = 10

SUBSEG = 8
SCAN_LANES = 512
SSM_BLOCK = 512
VMEM_LIMIT = 48 * 1024 * 1024
ADAMW_BLOCK_BYTES = 3 * 512 * 1024
PROJ_BWD_X_VMEM = 56 * 1024 * 1024

def _to_pair_order(row):
    return jnp.transpose(row.reshape(2, 4, HEAD_DIM), (1, 0, 2)).reshape(1, D_ATTN)


def _from_pair_order(row):
    return jnp.transpose(row.reshape(4, 2, HEAD_DIM), (1, 0, 2)).reshape(1, D_ATTN)


def _cparams(sem=None):
    return pltpu.CompilerParams(dimension_semantics=sem, vmem_limit_bytes=VMEM_LIMIT)


def _dot(a, b):
    return jnp.dot(a, b, preferred_element_type=F32)


def _dot_nt(a, b):
    return lax.dot_general(a, b, (((1,), (1,)), ((), ())), preferred_element_type=F32)


def _dot_tn(a, b):
    return lax.dot_general(a, b, (((0,), (0,)), ((), ())), preferred_element_type=F32)


def _sigmoid(z):
    return 1.0 / (1.0 + jnp.exp(-z))


def _all_gather_chips(shards, out_dtype, name):
    n = len(shards)

    def body(*refs):
        start, relay, finish = _gather_phases(refs[:n], refs[n:2 * n], *refs[2 * n:], out_dtype)
        start()
        relay()
        finish()

    vmem = pl.BlockSpec(memory_space=pltpu.VMEM)
    return pl.pallas_call(
        body, name=name,
        out_shape=[jax.ShapeDtypeStruct((N_CHIPS,) + s.shape, out_dtype) for s in shards],
        in_specs=[vmem] * n, out_specs=[vmem] * n,
        scratch_shapes=_gather_sems(n),
        compiler_params=pltpu.CompilerParams(vmem_limit_bytes=VMEM_LIMIT),
    )(*shards)


def _gather_sems(n):
    return [pltpu.SemaphoreType.DMA((6 * n,)), pltpu.SemaphoreType.DMA((6 * n,))]


def _gather_phases(in_refs, out_refs, send_sems, recv_sems, out_dtype):
    n = len(in_refs)
    x, y, c = lax.axis_index("x"), lax.axis_index("y"), lax.axis_index("c")
    sibling = (x, y, 1 - c)
    chips = [(1 - x, y), (x, 1 - y), (1 - x, 1 - y)]

    def half_of(a, px, py, half):
        rows = in_refs[a].shape[0] // 2
        return out_refs[a].at[2 * px + py, pl.ds(half * rows, rows), :]

    def copy(a, k, px, py, half, to):
        blk = half_of(a, px, py, half)
        return pltpu.make_async_remote_copy(src_ref=blk, dst_ref=blk, send_sem=send_sems.at[6 * a + k],
                                            recv_sem=recv_sems.at[6 * a + k], device_id=to, device_id_type=MESH)

    first = [copy(a, j, x, y, c, (*chips[j], c)) for a in range(n) for j in range(3)]
    passed = [copy(a, 3 + j, *chips[j], c, sibling) for a in range(n) for j in range(3)]

    def start():
        for a in range(n):
            out_refs[a][2 * x + y] = in_refs[a][...].astype(out_dtype)
        for cp in first:
            cp.start()

    def relay():
        for a in range(n):
            for j in range(3):
                copy(a, j, *chips[j], c, (x, y, c)).wait_recv()
                passed[3 * a + j].start()

    def finish():
        for a in range(n):
            for j in range(3):
                copy(a, 3 + j, *chips[j], 1 - c, (x, y, c)).wait_recv()
        for cp in first + passed:
            cp.wait_send()

    return start, relay, finish


SEMS_PER_ARRAY = 11


def _reduce_scratch(shapes, narrow):
    half = [(N_CHIPS, s[1] // 2, s[2]) for s in shapes]
    wire = [BF16 if nar else F32 for nar in narrow]
    n = len(shapes)
    return ([pltpu.VMEM(half[a], F32) for a in range(n)] + [pltpu.VMEM(half[a], wire[a]) for a in range(n)]
            + [pltpu.VMEM(half[a], wire[a]) for a in range(n)]
            + [pltpu.SemaphoreType.DMA((SEMS_PER_ARRAY * n,)), pltpu.SemaphoreType.DMA((SEMS_PER_ARRAY * n,))])


def _reduce_phases(p_refs, out_refs, a_refs, s_refs, b_refs, send_sems, recv_sems, narrow, gather_last):
    n = len(p_refs)
    halves = [p.shape[1] // 2 for p in p_refs]
    wire = [BF16 if nar else F32 for nar in narrow]
    x, y, c = lax.axis_index("x"), lax.axis_index("y"), lax.axis_index("c")
    me = 2 * x + y
    sibling = (x, y, 1 - c)
    chips = [(1 - x, y), (x, 1 - y), (1 - x, 1 - y)]
    slot = [2 * px + py for px, py in chips]
    last = n - 1

    def copy(a, k, src, dst, to):
        return pltpu.make_async_remote_copy(src_ref=src, dst_ref=dst, send_sem=send_sems.at[SEMS_PER_ARRAY * a + k],
                                            recv_sem=recv_sems.at[SEMS_PER_ARRAY * a + k],
                                            device_id=to, device_id_type=MESH)

    def rows(a, half):
        return pl.ds(pl.multiple_of(half * halves[a], 16), halves[a])

    def finished(a, k, half):
        if gather_last and a == last:
            return out_refs[a].at[k, rows(a, half), :]
        return out_refs[a].at[rows(a, half), :]

    swaps = [copy(a, 0, p_refs[a].at[:, rows(a, 1 - c), :], a_refs[a], sibling) for a in range(n)]
    sends = [[copy(a, 1 + j, s_refs[a].at[slot[j]], b_refs[a].at[me], (*chips[j], c)) for j in range(3)] for a in range(n)]
    backs = [copy(a, 4, finished(a, me, c), finished(a, me, c), sibling) for a in range(n)]
    spread = [copy(last, 5 + j, finished(last, me, c), finished(last, me, c), (*chips[j], c)) for j in range(3)]
    relays = [copy(last, 8 + j, finished(last, slot[j], c), finished(last, slot[j], c), sibling) for j in range(3)]

    def start():
        for cp in swaps:
            cp.start()

    def exchange():
        for a in range(n):
            swaps[a].wait_recv()
            for k in range(N_CHIPS):
                acc = a_refs[a][k] + p_refs[a][k, rows(a, c), :]
                a_refs[a][k] = acc
                s_refs[a][k] = acc.astype(wire[a])
            b_refs[a][me] = s_refs[a][me]
            for cp in sends[a]:
                cp.start()

    def combine():
        for a in range(n):
            for j in range(3):
                copy(a, 1 + j, s_refs[a].at[slot[j]], b_refs[a].at[slot[j]], (x, y, c)).wait_recv()
            terms = [jnp.where(me == k, a_refs[a][k], b_refs[a][k].astype(F32)) for k in range(N_CHIPS)]
            total = (terms[0] + terms[1]) + (terms[2] + terms[3])
            if gather_last and a == last:
                out_refs[a][me, rows(a, c), :] = total
            else:
                out_refs[a][rows(a, c), :] = total
            backs[a].start()
        if gather_last:
            for cp in spread:
                cp.start()

    def finish():
        if gather_last:
            for j in range(3):
                copy(last, 5 + j, finished(last, slot[j], c), finished(last, slot[j], c), (x, y, c)).wait_recv()
                relays[j].start()
        for a in range(n):
            copy(a, 4, finished(a, me, 1 - c), finished(a, me, 1 - c), (x, y, c)).wait_recv()
        if gather_last:
            for j in range(3):
                copy(last, 8 + j, finished(last, slot[j], 1 - c), finished(last, slot[j], 1 - c), (x, y, c)).wait_recv()
        for cp in swaps + [cp for group in sends for cp in group] + backs + (spread + relays if gather_last else []):
            cp.wait_send()

    return start, exchange, combine, finish


def _ssm_param_values(ar, ai, logdt):
    dt = jnp.exp(logdt)
    mag = jnp.exp(dt * ar)
    cs, sn = jnp.cos(dt * ai), jnp.sin(dt * ai)
    lr, li = mag * cs, mag * sn
    den = ar * ar + ai * ai
    nr = (lr - 1.0) * ar + li * ai
    ni = li * ar - (lr - 1.0) * ai
    return dt, mag, lr, li, den, nr, ni


GROUPS_PER_SLAB = N_GROUPS // N_SLAB


def _slab_masks():
    def eq(shape, f_row, f_col):
        return (f_row(lax.broadcasted_iota(jnp.int32, shape, 0)) == f_col(lax.broadcasted_iota(jnp.int32, shape, 1))).astype(F32)
    spread = eq((SSM_STATE, SLAB_ST), lambda r: r, lambda c: c % SSM_STATE)
    spread_t = eq((SLAB_ST, SSM_STATE), lambda r: r % SSM_STATE, lambda c: c)
    keep = eq((SLAB_IN, SLAB_ST), lambda r: r // SSM_CH, lambda c: c // SSM_STATE)
    keep_t = eq((SLAB_ST, SLAB_IN), lambda r: r // SSM_STATE, lambda c: c // SSM_CH)
    repeat = eq((N_DG * SSM_CH, N_DG), lambda r: r // SSM_CH, lambda c: c)
    return spread, spread_t, keep, keep_t, repeat


def _split3(t):
    hi = t.astype(BF16)
    rest = t - hi.astype(F32)
    mid = rest.astype(BF16)
    return hi, mid, (rest - mid.astype(F32)).astype(BF16)


def _select(dot, ones01, t, ones_first):
    o = ones01.astype(BF16)
    parts = [dot(o, p) if ones_first else dot(p, o) for p in _split3(t)]
    return (parts[0] + parts[1]) + parts[2]


def _ssm_params_fwd(ar, ai, logdt, br, bi, cr, ci, n_square):
    def body(ar_ref, ai_ref, dt_ref, br_ref, bi_ref, cr_ref, ci_ref, lam_ref, bb_ref, bbt_ref, cb_ref, cbt_ref):
        _, _, lr, li, den, nr, ni = _ssm_param_values(ar_ref[...], ai_ref[...], dt_ref[...])
        lam_ref[0] = lr
        lam_ref[1] = li
        pr, pi = lr, li
        for _ in range(n_square):
            pr, pi = pr * pr - pi * pi, 2.0 * pr * pi
        lam_ref[2] = pr
        lam_ref[3] = pi
        spread, spread_t, keep, keep_t, repeat = _slab_masks()
        fr = _select(_dot, repeat, nr / den, True)
        fi = _select(_dot, repeat, ni / den, True)
        b_r, b_i = br_ref[...], bi_ref[...]
        bbar = (fr * b_r - fi * b_i, fr * b_i + fi * b_r)
        c_par = (cr_ref[...], ci_ref[...])
        spread, spread_t = spread.astype(BF16), spread_t.astype(BF16)
        for src, wide_ref, tall_ref in ((bbar, bb_ref, bbt_ref), (c_par, cbt_ref, cb_ref)):
            for q in range(2):
                for d in range(N_DIR):
                    for k in range(N_SLAB):
                        r0 = (d * N_GROUPS + k * GROUPS_PER_SLAB) * SSM_CH
                        blk = src[q][r0:r0 + SLAB_IN].astype(BF16)
                        wide_ref[q, d, k] = (_dot(blk, spread) * keep).astype(BF16)
                        tall_ref[q, d, k] = (_dot_nt(spread_t, blk) * keep_t).astype(BF16)

    wide = jax.ShapeDtypeStruct((2, N_DIR, N_SLAB, SLAB_IN, SLAB_ST), BF16)
    tall = jax.ShapeDtypeStruct((2, N_DIR, N_SLAB, SLAB_ST, SLAB_IN), BF16)
    return pl.pallas_call(body, name="ssm_params_fwd",
                          out_shape=[jax.ShapeDtypeStruct((4,) + ar.shape, F32), wide, tall, tall, wide],
                          compiler_params=pltpu.CompilerParams(vmem_limit_bytes=VMEM_LIMIT),
                          )(ar, ai, logdt, br, bi, cr, ci)


def _ssm_params_bwd(ar, ai, logdt, br, bi, g_slabs_b, g_slabs_c, g_lam):
    def body(ar_ref, ai_ref, dt_ref, br_ref, bi_ref, gb0_ref, gb1_ref, gc0_ref, gc1_ref, gl0_ref, gl1_ref,
             gar_ref, gai_ref, gdt_ref, gbr_ref, gbi_ref, gcr_ref, gci_ref, dbb, dlam):
        spread, spread_t, keep, keep_t, repeat = _slab_masks()
        for d, (gb_ref, gc_ref) in enumerate(((gb0_ref, gc0_ref), (gb1_ref, gc1_ref))):
            for q in range(2):
                for k in range(N_SLAB):
                    r0 = (d * N_GROUPS + k * GROUPS_PER_SLAB) * SSM_CH
                    dbb[q, r0:r0 + SLAB_IN, :] = _select(_dot, spread_t, gb_ref[q, k] * keep, False)
                    out_ref = gcr_ref if q == 0 else gci_ref
                    out_ref[r0:r0 + SLAB_IN, :] = _select(_dot_tn, spread_t, gc_ref[q, k] * keep_t, False)
        grp = (lax.broadcasted_iota(jnp.int32, (N_GROUPS, STATE_W), 0)
               == lax.broadcasted_iota(jnp.int32, (N_GROUPS, STATE_W), 1) // SSM_STATE).astype(F32)
        pick = (lax.broadcasted_iota(jnp.int32, (STATE_W, SSM_STATE), 0) % SSM_STATE
                == lax.broadcasted_iota(jnp.int32, (STATE_W, SSM_STATE), 1)).astype(F32)
        for d, gl_ref in enumerate((gl0_ref, gl1_ref)):
            for q in range(2):
                row = jnp.sum(gl_ref[q], axis=0, keepdims=True)
                dlam[q, d * N_GROUPS:(d + 1) * N_GROUPS, :] = _select(_dot, pick, grp * row, False)

        a_r, a_i = ar_ref[...], ai_ref[...]
        dt, mag, lr, li, den, nr, ni = _ssm_param_values(a_r, a_i, dt_ref[...])
        fr = _select(_dot, repeat, nr / den, True)
        fi = _select(_dot, repeat, ni / den, True)
        b_r, b_i = br_ref[...], bi_ref[...]
        g_r, g_i = dbb[0], dbb[1]
        gbr_ref[...] = fr * g_r + fi * g_i
        gbi_ref[...] = fr * g_i - fi * g_r
        d_fr = _select(_dot_tn, repeat, b_r * g_r + b_i * g_i, True)
        d_fi = _select(_dot_tn, repeat, b_r * g_i - b_i * g_r, True)
        d_nr, d_ni = d_fr / den, d_fi / den
        d_den = -(d_fr * nr + d_fi * ni) / (den * den)
        d_lr = dlam[0] + d_nr * a_r - d_ni * a_i
        d_li = dlam[1] + d_nr * a_i + d_ni * a_r
        d_ar = d_nr * (lr - 1.0) + d_ni * li + d_den * 2.0 * a_r
        d_ai = d_nr * li - d_ni * (lr - 1.0) + d_den * 2.0 * a_i
        d_mag = (d_lr * lr + d_li * li) / mag
        d_theta = d_li * lr - d_lr * li
        gar_ref[...] = d_ar + d_mag * mag * dt
        gai_ref[...] = d_ai + d_theta * dt
        d_dt = d_mag * mag * a_r + d_theta * a_i
        gdt_ref[...] = jnp.sum(d_dt, axis=1, keepdims=True) * dt

    small = jax.ShapeDtypeStruct(ar.shape, F32)
    big = jax.ShapeDtypeStruct(br.shape, F32)
    return pl.pallas_call(
        body, name="ssm_params_bwd",
        out_shape=[small, small, jax.ShapeDtypeStruct(logdt.shape, F32), big, big, big, big],
        scratch_shapes=[pltpu.VMEM((2,) + br.shape, F32), pltpu.VMEM((2,) + ar.shape, F32)],
        compiler_params=pltpu.CompilerParams(vmem_limit_bytes=VMEM_LIMIT),
    )(ar, ai, logdt, br, bi, *g_slabs_b, *g_slabs_c, *g_lam)


ROPE_GROUP = 128


def _rope_tables(seq):
    half = HEAD_DIM // 2
    inv_freq = jnp.tile(ROPE_THETA ** (-jnp.arange(half, dtype=F32) / half), 4)
    sign = jnp.tile(jnp.concatenate([-jnp.ones((half,), F32), jnp.ones((half,), F32)]), 2)

    def table(pos):
        ang = pos.astype(F32)[:, None] * inv_freq[None, :]
        return jnp.stack([jnp.cos(ang), jnp.sin(ang), sign * jnp.sin(ang)])

    return table(jnp.arange(seq // ROPE_GROUP) * ROPE_GROUP), table(jnp.arange(ROPE_GROUP))


def _rope_block(hi_ref, lo_ref, first_group, n_groups):
    cl, sl, sl_s = lo_ref[0], lo_ref[1], lo_ref[2]
    cos, sin = [], []
    for g in range(n_groups):
        ch, sh, sh_s = (hi_ref[q, pl.ds(first_group + g, 1), :] for q in range(3))
        cos.append(ch * cl - sh * sl)
        sin.append(sh_s * cl + ch * sl_s)
    return jnp.concatenate(cos, axis=0), jnp.concatenate(sin, axis=0)


def _rotate_half_unsigned(t):
    lane = lax.broadcasted_iota(jnp.int32, t.shape, 1)
    return jnp.where((lane % HEAD_DIM) < HEAD_DIM // 2, pltpu.roll(t, 96, 1), pltpu.roll(t, 32, 1))


def _rope(t, cos, sin_signed):
    return t * cos + _rotate_half_unsigned(t) * sin_signed


def _pair_blocks(base):
    out = []
    for j in range(4):
        for g in range(2):
            nat = base + HEAD_DIM * (4 * g + j)
            par = base + 128 * j + HEAD_DIM * g
            out.append((slice(nat, nat + HEAD_DIM), slice(par, par + HEAD_DIM)))
    return out


W_Q, W_KV, W_ZA, W_U, W_ZS = 0, 512, 768, 1280, 1792


def _proj(x, wt, rope_hi, rope_lo, shards, tb):
    seq = x.shape[0]
    steps = seq // tb
    n_sh = len(shards)

    def body(*refs):
        x_ref, wt_ref, hi_ref, lo_ref = refs[:4]
        shard_refs = refs[4:4 + n_sh]
        q_ref, k_ref, v_ref, za_ref, u_ref, zs_ref = refs[4 + n_sh:10 + n_sh]
        gathered_refs = refs[10 + n_sh:10 + 2 * n_sh]
        wp = refs[10 + 2 * n_sh]
        step = pl.program_id(0)
        if n_sh:
            landing_refs = refs[11 + 2 * n_sh:11 + 3 * n_sh]
            start, relay, finish = _gather_phases(shard_refs, landing_refs, *refs[11 + 3 * n_sh:], BF16)
            pl.when(step == 0)(start)
            pl.when(step == max(steps - 2, 0))(relay)

        @pl.when(step == 0)
        def _():
            for dst_base, src_base in ((0, W_Q), (512, W_ZA)):
                for nat, par in _pair_blocks(0):
                    wp[dst_base + par.start:dst_base + par.stop, :] = wt_ref[src_base + nat.start:src_base + nat.stop, :]

        xb = x_ref[...].astype(BF16)
        cos, sin = _rope_block(hi_ref, lo_ref, pl.program_id(0) * (tb // ROPE_GROUP), tb // ROPE_GROUP)
        lo = lax.broadcasted_iota(jnp.int32, (tb, 128), 1) < HEAD_DIM
        q = _dot_nt(xb, wp[0:512, :])
        for j in range(4):
            qj = _rope(q[:, 128 * j:128 * (j + 1)], cos, sin)
            q_ref[j] = jnp.where(lo, qj, 0.0).astype(BF16)
            q_ref[4 + j] = jnp.where(lo, 0.0, qj).astype(BF16)
        kv = _dot_nt(xb, wt_ref[W_KV:W_ZA, :])
        k_ref[...] = _rope(kv[:, 0:128], cos, sin).astype(BF16)
        v_ref[...] = kv[:, 128:256].astype(BF16)
        za_ref[...] = _dot_nt(xb, wp[512:1024, :])
        u_val = _dot_nt(xb, wt_ref[W_U:W_ZS, :])
        for k in range(N_SLAB):
            u_ref[k] = u_val[:, k * SLAB_IN:(k + 1) * SLAB_IN]
        zs_ref[...] = _dot_nt(xb, wt_ref[W_ZS:D_IN_PROJ, :])
        if n_sh:
            @pl.when(step == steps - 1)
            def _():
                finish()
                for a in range(n_sh):
                    gathered_refs[a][...] = landing_refs[a][...]

    row = lambda w: pl.BlockSpec((tb, w), lambda i: (i, 0))
    table = lambda t: pl.BlockSpec(t.shape, lambda i: (0, 0, 0))
    vmem = pl.BlockSpec(memory_space=pltpu.VMEM)
    return pl.pallas_call(
        body, name="proj", grid=(steps,),
        in_specs=[row(D_MODEL), pl.BlockSpec((D_IN_PROJ, D_MODEL), lambda i: (0, 0), pipeline_mode=pl.Buffered(1)),
                  table(rope_hi), table(rope_lo)] + [vmem] * n_sh,
        out_specs=[pl.BlockSpec((8, tb, 128), lambda i: (0, i, 0)), row(128), row(128), row(512),
                   pl.BlockSpec((N_SLAB, tb, SLAB_IN), lambda i: (0, i, 0)), row(512)] + [vmem] * n_sh,
        out_shape=[jax.ShapeDtypeStruct((8, seq, 128), BF16), jax.ShapeDtypeStruct((seq, 128), BF16),
                   jax.ShapeDtypeStruct((seq, 128), BF16), jax.ShapeDtypeStruct((seq, 512), F32),
                   jax.ShapeDtypeStruct((N_SLAB, seq, SLAB_IN), F32), jax.ShapeDtypeStruct((seq, 512), F32)]
        + [jax.ShapeDtypeStruct((N_CHIPS,) + s.shape, BF16) for s in shards],
        scratch_shapes=[pltpu.VMEM((1024, D_MODEL), BF16)] + [pltpu.VMEM((N_CHIPS,) + s.shape, BF16) for s in shards]
        + (_gather_sems(n_sh) if n_sh else []),
        compiler_params=_cparams(("arbitrary",)),
    )(x, wt, rope_hi, rope_lo, *shards)


ATT_TQ = 128
ATT_KEYS = ATT_TQ + 2 * WINDOW


def _attn_window(i, seq):
    start = jnp.clip(i * ATT_TQ - WINDOW, 0, seq - ATT_KEYS)
    return pl.multiple_of(start, WINDOW)


def _attn_bias():
    r = np.arange(ATT_TQ)[None, :, None]
    c = np.arange(ATT_KEYS)[None, None, :]
    off = np.array([0, WINDOW, ATT_KEYS - ATT_TQ])[:, None, None]
    return jnp.asarray(np.where(np.abs(r + off - c) <= WINDOW, 0.0, NEG_INF).astype(np.float32))


def _attn_bias_spec(nblk):
    pick = lambda i: jnp.where(i == 0, 0, jnp.where(i == nblk - 1, 2, 1))
    return pl.BlockSpec((None, ATT_TQ, ATT_KEYS), lambda i: (pick(i), 0, 0))


def _attn_softmax(q_ref, k_ref, v_ref, sink_ref, bias_ref, start):
    kw = k_ref[pl.ds(start, ATT_KEYS), :]
    vw = v_ref[pl.ds(start, ATT_KEYS), :]
    qall = q_ref[...].reshape(N_Q_HEADS * ATT_TQ, 128)
    s = (_dot_nt(qall, kw) * (HEAD_DIM ** -0.5)).reshape(N_Q_HEADS, ATT_TQ, ATT_KEYS) + bias_ref[...][None]
    tiles = [s[:, :, 128 * t:128 * (t + 1)] for t in range(ATT_KEYS // 128)]
    m = jnp.max(functools.reduce(jnp.maximum, tiles), axis=2, keepdims=True)
    sink = sink_ref[...]
    m_b = jnp.maximum(jnp.broadcast_to(m, (N_Q_HEADS, ATT_TQ, 128)), sink)
    p = jnp.concatenate([jnp.exp(t - m_b) for t in tiles], axis=2)
    p_sink = jnp.exp(sink - m_b)
    lo_k = lax.broadcasted_iota(jnp.int32, (ATT_KEYS, 128), 1) < HEAD_DIM
    v_f = vw.astype(F32)
    v_lo, v_hi = jnp.where(lo_k, v_f, 1.0).astype(BF16), jnp.where(lo_k, 1.0, v_f).astype(BF16)
    pb = p.astype(BF16).reshape(N_Q_HEADS * ATT_TQ, ATT_KEYS)
    half = 4 * ATT_TQ
    r = jnp.concatenate([_dot(pb[:half], v_lo), _dot(pb[half:], v_hi)], axis=0).reshape(N_Q_HEADS, ATT_TQ, 128)
    return kw, vw, qall, p, p_sink, r


def _attn_fwd(q_stack, k, v, sink128, bias):
    seq = k.shape[0]

    def body(q_ref, k_ref, v_ref, sink_ref, bias_ref, o_ref):
        start = _attn_window(pl.program_id(0), seq)
        _, _, _, _, p_sink, r = _attn_softmax(q_ref, k_ref, v_ref, sink_ref, bias_ref, start)
        out = r / (pltpu.roll(r, HEAD_DIM, 2) + p_sink)
        lo = lax.broadcasted_iota(jnp.int32, (ATT_TQ, 128), 1) < HEAD_DIM
        for j in range(4):
            o_ref[:, 128 * j:128 * (j + 1)] = jnp.where(lo, out[j], out[4 + j])

    full = lambda w: pl.BlockSpec((seq, w), lambda i: (0, 0))
    return pl.pallas_call(
        body, name="attn_fwd", grid=(seq // ATT_TQ,),
        in_specs=[pl.BlockSpec((8, ATT_TQ, 128), lambda i: (0, i, 0)), full(128), full(128),
                  pl.BlockSpec((N_Q_HEADS, 1, 128), lambda i: (0, 0, 0)), _attn_bias_spec(seq // ATT_TQ)],
        out_specs=pl.BlockSpec((ATT_TQ, 512), lambda i: (i, 0)),
        out_shape=jax.ShapeDtypeStruct((seq, 512), F32),
        compiler_params=_cparams(("arbitrary",)),
    )(q_stack, k, v, sink128, bias)


def _attn_bwd(q_stack, k, v, sink128, bias, d_o, pieces):
    seq = k.shape[0]
    steps = seq // ATT_TQ
    n_p = len(pieces)

    def body(*refs):
        q_ref, k_ref, v_ref, sink_ref, bias_ref, do_ref = refs[:6]
        piece_refs = refs[6:6 + n_p]
        dq_ref, dk_ref, dv_ref, dsink_ref = refs[6 + n_p:10 + n_p]
        reduced_refs = refs[10 + n_p:10 + 2 * n_p]
        sink_acc = refs[10 + 2 * n_p]
        i = pl.program_id(0)
        if n_p:
            landing_refs = refs[11 + 2 * n_p:11 + 3 * n_p]
            scratch = refs[11 + 3 * n_p:]
            begin, exchange, combine, finish = _reduce_phases(
                piece_refs, landing_refs, scratch[:n_p], scratch[n_p:2 * n_p], scratch[2 * n_p:3 * n_p],
                *scratch[3 * n_p:], [True] * n_p, gather_last=False)
            pl.when(i == 0)(begin)
            pl.when(i == min(4, steps - 1))(exchange)
            pl.when(i == (3 * steps) // 4)(combine)

        @pl.when(i == 0)
        def _():
            dk_ref[...] = jnp.zeros_like(dk_ref)
            dv_ref[...] = jnp.zeros_like(dv_ref)
            sink_acc[...] = jnp.zeros_like(sink_acc)

        start = _attn_window(i, seq)
        kw, vw, qall, p, p_sink, r = _attn_softmax(q_ref, k_ref, v_ref, sink_ref, bias_ref, start)
        lo = lax.broadcasted_iota(jnp.int32, (ATT_TQ, 128), 1) < HEAD_DIM
        lo3 = lo[None]
        grp0 = lax.broadcasted_iota(jnp.int32, (N_Q_HEADS, ATT_TQ, 128), 0) < 4
        val = grp0 == lo3
        swapped = pltpu.roll(r, HEAD_DIM, 2)
        inv = 1.0 / (jnp.where(val, swapped, r) + p_sink)
        d_o_blk = do_ref[...]
        do3 = jnp.where(val, jnp.concatenate([d_o_blk[None, :, 128 * j:128 * (j + 1)] for j in range(4)] * 2, axis=0), 0.0)
        t = (do3 * r).reshape(N_Q_HEADS * ATT_TQ, 128)
        t_hi = t.astype(BF16)
        t_lo = (t - t_hi.astype(F32)).astype(BF16)
        ones = jnp.ones((128, 128), BF16)
        delta = (_dot(t_hi, ones) + _dot(t_lo, ones)).reshape(N_Q_HEADS, ATT_TQ, 128) * inv
        sink_acc[...] += -(p_sink * inv) * delta
        do_all = do3.astype(BF16).reshape(N_Q_HEADS * ATT_TQ, 128)
        dp = _dot_nt(do_all, vw).reshape(N_Q_HEADS, ATT_TQ, ATT_KEYS)
        probs, ds = [], []
        for tl in range(ATT_KEYS // 128):
            cols = slice(128 * tl, 128 * (tl + 1))
            probs_t = p[:, :, cols] * inv
            probs.append(probs_t.astype(BF16))
            ds.append((probs_t * (dp[:, :, cols] - delta)).astype(BF16))
        probs_all = jnp.concatenate(probs, axis=2).reshape(N_Q_HEADS * ATT_TQ, ATT_KEYS)
        ds_all = jnp.concatenate(ds, axis=2).reshape(N_Q_HEADS * ATT_TQ, ATT_KEYS)
        scale = HEAD_DIM ** -0.5
        dq_all = (_dot(ds_all, kw) * scale).reshape(N_Q_HEADS, ATT_TQ, 128)
        for j in range(4):
            dq_ref[:, 128 * j:128 * (j + 1)] = jnp.where(lo, dq_all[j], dq_all[4 + j])
        dk_ref[pl.ds(start, ATT_KEYS), :] += _dot_tn(ds_all, qall) * scale
        dv_ref[pl.ds(start, ATT_KEYS), :] += _dot_tn(probs_all, do_all)

        @pl.when(i == steps - 1)
        def _():
            dsink_ref[...] = jnp.sum(sink_acc[...], axis=1)

        if n_p:
            @pl.when(i == steps - 1)
            def _():
                finish()
                for a in range(n_p):
                    reduced_refs[a][...] = landing_refs[a][...]

    full = lambda w: pl.BlockSpec((seq, w), lambda i: (0, 0))
    vmem = pl.BlockSpec(memory_space=pltpu.VMEM)
    return pl.pallas_call(
        body, name="attn_bwd", grid=(steps,),
        in_specs=[pl.BlockSpec((8, ATT_TQ, 128), lambda i: (0, i, 0)), full(128), full(128),
                  pl.BlockSpec((N_Q_HEADS, 1, 128), lambda i: (0, 0, 0)),
                  _attn_bias_spec(steps), pl.BlockSpec((ATT_TQ, 512), lambda i: (i, 0))] + [vmem] * n_p,
        out_specs=[pl.BlockSpec((ATT_TQ, 512), lambda i: (i, 0)), full(128), full(128),
                   pl.BlockSpec((N_Q_HEADS, 128), lambda i: (0, 0))] + [vmem] * n_p,
        out_shape=[jax.ShapeDtypeStruct((seq, 512), F32), jax.ShapeDtypeStruct((seq, 128), F32),
                   jax.ShapeDtypeStruct((seq, 128), F32), jax.ShapeDtypeStruct((N_Q_HEADS, 128), F32)]
        + [jax.ShapeDtypeStruct(p.shape[1:], F32) for p in pieces],
        scratch_shapes=[pltpu.VMEM((N_Q_HEADS, ATT_TQ, 128), F32)] + [pltpu.VMEM(p.shape[1:], F32) for p in pieces]
        + (_reduce_scratch([p.shape for p in pieces], [True] * n_p) if n_p else []),
        compiler_params=_cparams(("arbitrary",)),
    )(q_stack, k, v, sink128, bias, d_o, *pieces)


def _permute_rows(dst_ref, src_ref, sub_len):
    for k in range(N_SLAB):
        for j in range(sub_len):
            dst_ref[k, 8 * j:8 * (j + 1), :] = src_ref.at[k][pl.ds(j, SUBSEG, stride=sub_len), :]


def _unpermute_rows(dst_ref, src_ref, sub_len):
    for k in range(N_SLAB):
        for s in range(SUBSEG):
            dst_ref[k, s * sub_len:(s + 1) * sub_len, :] = src_ref.at[k][pl.ds(s, sub_len, stride=SUBSEG), :]


def _scan_chunk(br_ref, bi_ref, lr_row, li_row, init, cols, *, sub_len, reverse, store):
    lr = jnp.broadcast_to(lr_row[:, cols], (SUBSEG, SCAN_LANES))
    li = jnp.broadcast_to(li_row[:, cols], (SUBSEG, SCAN_LANES))
    if init is None:
        sr = si = jnp.zeros((SUBSEG, SCAN_LANES), F32)
    else:
        sr, si = init
    for jj in range(sub_len):
        rows = slice(SUBSEG * ((sub_len - 1 - jj) if reverse else jj), SUBSEG * (((sub_len - 1 - jj) if reverse else jj) + 1))
        sr, si = lr * sr - li * si + br_ref[rows, cols], lr * si + li * sr + bi_ref[rows, cols]
        if store:
            br_ref[rows, cols] = sr
            bi_ref[rows, cols] = si
    return sr, si


def _resolve_chunk(z, carry_refs, start_refs, pr_row, pi_row, cols, *, reverse):
    cr, ci = carry_refs[0][0:1, cols], carry_refs[1][0:1, cols]
    pr, pi = pr_row[:, cols], pi_row[:, cols]
    for s in (range(SUBSEG - 1, -1, -1) if reverse else range(SUBSEG)):
        start_refs[0][s:s + 1, cols] = cr
        start_refs[1][s:s + 1, cols] = ci
        cr, ci = pr * cr - pi * ci + z[0][s:s + 1, :], pr * ci + pi * cr + z[1][s:s + 1, :]
    carry_refs[0][0:1, cols] = cr
    carry_refs[1][0:1, cols] = ci


def _param_specs(direction):
    row = lambda q: pl.BlockSpec((None, None, 1, STATE_W), lambda i: (q, direction, 0, 0))
    wide = lambda q: pl.BlockSpec((None, None, N_SLAB, SLAB_IN, SLAB_ST), lambda i: (q, direction, 0, 0, 0))
    tall = lambda q: pl.BlockSpec((None, None, N_SLAB, SLAB_ST, SLAB_IN), lambda i: (q, direction, 0, 0, 0))
    return [row(q) for q in range(4)], [wide(0), wide(1)], [tall(0), tall(1)]


def _ssm_fwd(u, lam, bb, cb, *, direction, tb, name):
    reverse = direction == 1
    seq = u.shape[1]
    nblk = seq // tb
    sub_len = tb // SUBSEG

    def body(u_ref, lr_ref, li_ref, pr_ref, pi_ref, bbr_ref, bbi_ref, cbr_ref, cbi_ref,
             y_ref, sr_ref, si_ref, xr, xi, up, yp, car, cai):
        @pl.when(pl.program_id(0) == 0)
        def _():
            car[...] = jnp.zeros_like(car)
            cai[...] = jnp.zeros_like(cai)

        _permute_rows(up, u_ref, sub_len)
        lr, li, pr, pi = lr_ref[...], li_ref[...], pr_ref[...], pi_ref[...]
        chunk = lambda k: slice(k * SLAB_ST, (k + 1) * SLAB_ST)

        def drive(k):
            ub = up[k].astype(BF16)
            xr[:, chunk(k)] = _dot(ub, bbr_ref[k])
            xi[:, chunk(k)] = _dot(ub, bbi_ref[k])

        def scan(k):
            z = _scan_chunk(xr, xi, lr, li, None, chunk(k), sub_len=sub_len, reverse=reverse, store=False)
            _resolve_chunk(z, (car, cai), (sr_ref, si_ref), pr, pi, chunk(k), reverse=reverse)
            _scan_chunk(xr, xi, lr, li, (sr_ref[:, chunk(k)], si_ref[:, chunk(k)]), chunk(k),
                        sub_len=sub_len, reverse=reverse, store=True)

        def read_out(k):
            yp[k] = _dot(xr[:, chunk(k)].astype(BF16), cbr_ref[k]) - _dot(xi[:, chunk(k)].astype(BF16), cbi_ref[k])

        drive(0)
        for k in range(N_SLAB):
            if k + 1 < N_SLAB:
                drive(k + 1)
            scan(k)
            if k > 0:
                read_out(k - 1)
        read_out(N_SLAB - 1)
        _unpermute_rows(y_ref, yp, sub_len)

    blk = (lambda i: nblk - 1 - i) if reverse else (lambda i: i)
    rows, wide, tall = _param_specs(direction)
    tok = pl.BlockSpec((N_SLAB, tb, SLAB_IN), lambda i: (0, blk(i), 0))
    start_spec = pl.BlockSpec((None, SUBSEG, STATE_W), lambda i: (blk(i), 0, 0))
    return pl.pallas_call(
        body, name=name, grid=(nblk,),
        in_specs=[tok] + rows + wide + tall,
        out_specs=[tok, start_spec, start_spec],
        out_shape=[jax.ShapeDtypeStruct((N_SLAB, seq, SLAB_IN), F32), jax.ShapeDtypeStruct((nblk, SUBSEG, STATE_W), F32),
                   jax.ShapeDtypeStruct((nblk, SUBSEG, STATE_W), F32)],
        scratch_shapes=[pltpu.VMEM((tb, STATE_W), F32), pltpu.VMEM((tb, STATE_W), F32),
                        pltpu.VMEM((N_SLAB, tb, SLAB_IN), F32), pltpu.VMEM((N_SLAB, tb, SLAB_IN), F32),
                        pltpu.VMEM((SUBSEG, STATE_W), F32), pltpu.VMEM((SUBSEG, STATE_W), F32)],
        compiler_params=_cparams(("arbitrary",)),
    )(u, lam, lam, lam, lam, bb, bb, cb, cb)


def _ssm_bwd(u, dy, starts, lam, bb, bbt, cb_t, *, direction, tb, name):
    reverse = direction == 1
    seq = u.shape[1]
    nblk = seq // tb
    sub_len = tb // SUBSEG

    def body(u_ref, dy_ref, sr_ref, si_ref, lr_ref, li_ref, pr_ref, pi_ref, bbr_ref, bbi_ref, btr_ref, bti_ref,
             ctr_ref, cti_ref, du_ref, gb_ref, gc_ref, dl_ref,
             xr, xi, gr, gi, up, dyp, dup, gsr, gsi, car, cai):
        gbr_ref, gbi_ref = gb_ref.at[0], gb_ref.at[1]
        gcr_ref, gci_ref = gc_ref.at[0], gc_ref.at[1]
        dlr_ref, dli_ref = dl_ref.at[0], dl_ref.at[1]

        @pl.when(pl.program_id(0) == 0)
        def _():
            for ref in (car, cai, gbr_ref, gbi_ref, gcr_ref, gci_ref, dlr_ref, dli_ref):
                ref[...] = jnp.zeros_like(ref)

        _permute_rows(up, u_ref, sub_len)
        _permute_rows(dyp, dy_ref, sub_len)
        lr, li, pr, pi = lr_ref[...], li_ref[...], pr_ref[...], pi_ref[...]
        nli, npi = -li, -pi
        chunk = lambda k: slice(k * SLAB_ST, (k + 1) * SLAB_ST)

        def drive(k):
            ub = up[k].astype(BF16)
            xr[:, chunk(k)] = _dot(ub, bbr_ref[k])
            xi[:, chunk(k)] = _dot(ub, bbi_ref[k])
            dyb = dyp[k].astype(BF16)
            gr[:, chunk(k)] = _dot(dyb, ctr_ref[k])
            gi[:, chunk(k)] = -_dot(dyb, cti_ref[k])

        def scan_x(k):
            _scan_chunk(xr, xi, lr, li, (sr_ref[:, chunk(k)], si_ref[:, chunk(k)]), chunk(k),
                        sub_len=sub_len, reverse=reverse, store=True)

        def grad_c(k):
            dyb = dyp[k].astype(BF16)
            gcr_ref[k] += _dot_tn(xr[:, chunk(k)].astype(BF16), dyb)
            gci_ref[k] -= _dot_tn(xi[:, chunk(k)].astype(BF16), dyb)

        def scan_g(k):
            z = _scan_chunk(gr, gi, lr, nli, None, chunk(k), sub_len=sub_len, reverse=not reverse, store=False)
            _resolve_chunk(z, (car, cai), (gsr, gsi), pr, npi, chunk(k), reverse=not reverse)
            _scan_chunk(gr, gi, lr, nli, (gsr[:, chunk(k)], gsi[:, chunk(k)]), chunk(k),
                        sub_len=sub_len, reverse=not reverse, store=True)

        def grad_b_du(k):
            ub = up[k].astype(BF16)
            grb, gib = gr[:, chunk(k)].astype(BF16), gi[:, chunk(k)].astype(BF16)
            gbr_ref[k] += _dot_tn(ub, grb)
            gbi_ref[k] += _dot_tn(ub, gib)
            dup[k] = _dot(grb, btr_ref[k]) + _dot(gib, bti_ref[k])

        def grad_lambda(k):
            cols = chunk(k)
            acc_r, acc_i = dlr_ref[:, cols], dli_ref[:, cols]
            for jj in range(sub_len):
                prev = jj + 1 if reverse else jj - 1
                if 0 <= prev < sub_len:
                    x_r, x_i = xr[SUBSEG * prev:SUBSEG * (prev + 1), cols], xi[SUBSEG * prev:SUBSEG * (prev + 1), cols]
                else:
                    x_r, x_i = sr_ref[:, cols], si_ref[:, cols]
                g_r, g_i = gr[SUBSEG * jj:SUBSEG * (jj + 1), cols], gi[SUBSEG * jj:SUBSEG * (jj + 1), cols]
                acc_r = acc_r + (g_r * x_r + g_i * x_i)
                acc_i = acc_i + (g_i * x_r - g_r * x_i)
            dlr_ref[:, cols] = acc_r
            dli_ref[:, cols] = acc_i

        drive(0)
        for k in range(N_SLAB):
            if k + 1 < N_SLAB:
                drive(k + 1)
            scan_x(k)
            grad_c(k)
            scan_g(k)
            grad_b_du(k)
            grad_lambda(k)
        _unpermute_rows(du_ref, dup, sub_len)

    blk = (lambda i: i) if reverse else (lambda i: nblk - 1 - i)
    rows, wide, tall = _param_specs(direction)
    tok = pl.BlockSpec((N_SLAB, tb, SLAB_IN), lambda i: (0, blk(i), 0))
    start_spec = pl.BlockSpec((None, SUBSEG, STATE_W), lambda i: (blk(i), 0, 0))
    gb_shape, gc_shape, dl_shape = (2, N_SLAB, SLAB_IN, SLAB_ST), (2, N_SLAB, SLAB_ST, SLAB_IN), (2, SUBSEG, STATE_W)
    whole = lambda shape: pl.BlockSpec(shape, lambda i: (0,) * len(shape))
    big = lambda: pltpu.VMEM((tb, STATE_W), F32)
    slabs = lambda: pltpu.VMEM((N_SLAB, tb, SLAB_IN), F32)
    tile = lambda: pltpu.VMEM((SUBSEG, STATE_W), F32)
    return pl.pallas_call(
        body, name=name, grid=(nblk,),
        in_specs=[tok, tok, start_spec, start_spec] + rows + wide + tall + wide,
        out_specs=[tok, whole(gb_shape), whole(gc_shape), whole(dl_shape)],
        out_shape=[jax.ShapeDtypeStruct((N_SLAB, seq, SLAB_IN), F32), jax.ShapeDtypeStruct(gb_shape, F32),
                   jax.ShapeDtypeStruct(gc_shape, F32), jax.ShapeDtypeStruct(dl_shape, F32)],
        scratch_shapes=[big(), big(), big(), big(), slabs(), slabs(), slabs(), tile(), tile(), tile(), tile()],
        compiler_params=_cparams(("arbitrary",)),
    )(u, dy, *starts, lam, lam, lam, lam, bb, bb, bbt, bbt, cb_t, cb_t)


GELU_C = math.sqrt(2.0 / math.pi)
GELU_K = 0.044715
MID_ROW_GROUPS = 1


def _mid(o, za, u, y_f, y_b, zs, x, target, ssm_d, w_glu, b_glu, g_attn, g_ssm, w_out, ln_g, ln_b, tb):
    seq = x.shape[0]

    def body(o_ref, za_ref, u_ref, yf_ref, yb_ref, zs_ref, x_ref, t_ref, d_ref, wg_ref, bg_ref, ga_ref, gs_ref,
             wo_ref, lg_ref, lb_ref,
             loss_ref, do_ref, dza_ref, dyl_ref, dzs_ref, dpre_ref, gwo_ref, gwg_ref, vec_ref, wop):
        @pl.when(pl.program_id(0) == 0)
        def _():
            for ref in (loss_ref, gwo_ref, gwg_ref, vec_ref):
                ref[...] = jnp.zeros_like(ref)
            for nat, par in _pair_blocks(0):
                wop[par, :] = wo_ref[nat, :]
            wop[D_ATTN:, :] = wo_ref[D_ATTN:, :]

        def rows_of(rs):
            o, za = o_ref[rs, :], za_ref[rs, :]
            sig_a = _sigmoid(za)
            silu_a = za * sig_a
            ya = o * silu_a
            r_a = lax.rsqrt(jnp.mean(ya * ya, axis=1, keepdims=True) + NORM_EPS)
            n_a = ya * r_a
            g_a = ga_ref[...]
            unslab = lambda ref: jnp.concatenate([ref[k, rs, :] for k in range(N_SLAB)], axis=1)
            u_blk, zs = unslab(u_ref), zs_ref[rs, :]
            d_row = d_ref[...]
            ylin = d_row * u_blk + unslab(yf_ref) + unslab(yb_ref)
            inner = GELU_C * (ylin + GELU_K * ylin * ylin * ylin)
            th = jnp.tanh(inner)
            gl = 0.5 * ylin * (1.0 + th)
            glb = gl.astype(BF16)
            gate = _dot(glb, wg_ref[...])
            yield
            sg = _sigmoid(gate + bg_ref[...])
            y2 = gl * sg
            sig_s = _sigmoid(zs)
            silu_s = zs * sig_s
            ys = y2 * silu_s
            r_s = lax.rsqrt(jnp.mean(ys * ys, axis=1, keepdims=True) + NORM_EPS)
            n_s = ys * r_s
            g_s = gs_ref[...]
            mixed = jnp.concatenate([n_a * g_a, n_s * g_s], axis=1).astype(BF16)
            out = _dot(mixed, wop[...])
            yield
            pre = ALPHA * x_ref[rs, :] + out
            mu = jnp.mean(pre, axis=1, keepdims=True)
            cen = pre - mu
            rstd = lax.rsqrt(jnp.mean(cen * cen, axis=1, keepdims=True) + NORM_EPS)
            hhat = cen * rstd
            ln_g = lg_ref[...]
            err = hhat * ln_g + lb_ref[...] - t_ref[rs, :]
            loss_ref[...] += 0.5 * jnp.sum(jnp.mean(err * err, axis=1, keepdims=True))

            dh = err * (1.0 / D_MODEL)
            vec_ref[0:1, :] += jnp.sum(dh * hhat, axis=0, keepdims=True)
            vec_ref[1:2, :] += jnp.sum(dh, axis=0, keepdims=True)
            dhh = dh * ln_g
            dpre = rstd * (dhh - jnp.mean(dhh, axis=1, keepdims=True)
                           - hhat * jnp.mean(dhh * hhat, axis=1, keepdims=True))
            dpre_ref[rs, :] = dpre
            dpb = dpre.astype(BF16)
            for j in range(4):
                g_pair = _dot_tn(mixed[:, 128 * j:128 * (j + 1)], dpb)
                for g in range(2):
                    nat = HEAD_DIM * (4 * g + j)
                    gwo_ref[nat:nat + HEAD_DIM, :] += g_pair[HEAD_DIM * g:HEAD_DIM * (g + 1), :]
            gwo_ref[D_ATTN:, :] += _dot_tn(mixed[:, D_ATTN:], dpb)
            dmix = _dot_nt(dpb, wop[...])
            yield
            dna = dmix[:, :D_ATTN]
            vec_ref[2:3, 0:D_ATTN] += jnp.sum(dna * n_a, axis=0, keepdims=True)
            dna = dna * g_a
            dya = r_a * (dna - n_a * jnp.mean(dna * n_a, axis=1, keepdims=True))
            do_ref[rs, :] = dya * silu_a
            dza_ref[rs, :] = dya * o * (sig_a * (1.0 + za * (1.0 - sig_a)))
            dns = dmix[:, D_ATTN:]
            vec_ref[2:3, D_ATTN:] += jnp.sum(dns * n_s, axis=0, keepdims=True)
            dns = dns * g_s
            dys = r_s * (dns - n_s * jnp.mean(dns * n_s, axis=1, keepdims=True))
            dzs_ref[rs, :] = dys * y2 * (sig_s * (1.0 + zs * (1.0 - sig_s)))
            dy2 = dys * silu_s
            da = dy2 * gl * sg * (1.0 - sg)
            vec_ref[3:4, D_SSM:] += jnp.sum(da, axis=0, keepdims=True)
            dab = da.astype(BF16)
            gwg_ref[...] += _dot_tn(glb, dab)
            dgl_mm = _dot_nt(dab, wg_ref[...])
            yield
            dgl = dy2 * sg + dgl_mm
            dylin = dgl * (0.5 * (1.0 + th)
                           + 0.5 * ylin * (1.0 - th * th) * GELU_C * (1.0 + 3.0 * GELU_K * ylin * ylin))
            for k in range(N_SLAB):
                dyl_ref[k, rs, :] = dylin[:, k * SLAB_IN:(k + 1) * SLAB_IN]
            vec_ref[3:4, 0:D_SSM] += jnp.sum(dylin * u_blk, axis=0, keepdims=True)
            yield

        groups = [rows_of(slice(r0, r0 + tb // MID_ROW_GROUPS)) for r0 in range(0, tb, tb // MID_ROW_GROUPS)]
        for _ in range(5):
            for gen in groups:
                next(gen)

    tok = lambda w: pl.BlockSpec((tb, w), lambda i: (i, 0))
    slab = pl.BlockSpec((N_SLAB, tb, SLAB_IN), lambda i: (0, i, 0))
    const = lambda r, c: pl.BlockSpec((r, c), lambda i: (0, 0), pipeline_mode=pl.Buffered(1))
    tok_shape = jax.ShapeDtypeStruct((seq, 512), F32)
    return pl.pallas_call(
        body, name="mid", grid=(seq // tb,),
        in_specs=[tok(512), tok(512), slab, slab, slab, tok(512), tok(1024), tok(1024),
                  const(1, 512), const(512, 512), const(1, 512), const(1, 512), const(1, 512),
                  const(1024, 1024), const(1, 1024), const(1, 1024)],
        out_specs=[const(8, 128), tok(512), tok(512), slab, tok(512), tok(1024),
                   const(1024, 1024), const(512, 512), const(8, 1024)],
        out_shape=[jax.ShapeDtypeStruct((8, 128), F32), tok_shape, tok_shape,
                   jax.ShapeDtypeStruct((N_SLAB, seq, SLAB_IN), F32), tok_shape,
                   jax.ShapeDtypeStruct((seq, 1024), F32), jax.ShapeDtypeStruct((1024, 1024), F32),
                   jax.ShapeDtypeStruct((512, 512), F32), jax.ShapeDtypeStruct((8, 1024), F32)],
        scratch_shapes=[pltpu.VMEM((D_MODEL, D_MODEL), BF16)],
        compiler_params=_cparams(("arbitrary",)),
    )(o, za, u, y_f, y_b, zs, x, target, ssm_d, w_glu, b_glu, g_attn, g_ssm, w_out, ln_g, ln_b)


def _ride_shapes(pieces, narrow, gather_last):
    outs = [p.shape if (gather_last and a == len(pieces) - 1) else p.shape[1:] for a, p in enumerate(pieces)]
    return outs, [pltpu.VMEM(s, F32) for s in outs] + _reduce_scratch([p.shape for p in pieces], narrow)


def _ride_phases(piece_refs, out_refs, scratch_refs, narrow, gather_last):
    n = len(piece_refs)
    landing, rest = scratch_refs[:n], scratch_refs[n:]
    begin, exchange, combine, finish = _reduce_phases(piece_refs, landing, rest[:n], rest[n:2 * n], rest[2 * n:3 * n],
                                                      *rest[3 * n:], narrow, gather_last)

    def end():
        finish()
        for a in range(n):
            out_refs[a][...] = landing[a][...]

    return begin, exchange, combine, end


def _dproj_block(dq_ref, dk_ref, dv_ref, dza_ref, duf_ref, dub_ref, dyl_ref, dzs_ref, d_ref, hi_ref, lo_ref, tb):
    cos, sin = _rope_block(hi_ref, lo_ref, pl.program_id(0) * (tb // ROPE_GROUP), tb // ROPE_GROUP)
    lo = lax.broadcasted_iota(jnp.int32, (tb, 128), 1) < HEAD_DIM

    def unrope(t):
        return t * cos + _rotate_half_unsigned(t * sin)

    def natural(pairs):
        swapped = [pltpu.roll(t, HEAD_DIM, 1) for t in pairs]
        return [jnp.where(lo, pairs[0], swapped[1]), jnp.where(lo, pairs[2], swapped[3]),
                jnp.where(lo, swapped[0], pairs[1]), jnp.where(lo, swapped[2], pairs[3])]

    dq_rot, dza = dq_ref[...], dza_ref[...]
    pieces = natural([unrope(dq_rot[:, 128 * j:128 * (j + 1)]) for j in range(4)])
    d_row = d_ref[...]
    pieces += [unrope(dk_ref[...]), dv_ref[...]] + natural([dza[:, 128 * j:128 * (j + 1)] for j in range(4)])
    pieces += [duf_ref[k] + dub_ref[k] + d_row[:, k * SLAB_IN:(k + 1) * SLAB_IN] * dyl_ref[k] for k in range(N_SLAB)]
    pieces += [dzs_ref[...]]
    return jnp.concatenate(pieces, axis=1).astype(BF16)


def _dproj_specs(tb, rope_hi, rope_lo):
    tok = lambda w: pl.BlockSpec((tb, w), lambda i: (i, 0))
    slab = pl.BlockSpec((N_SLAB, tb, SLAB_IN), lambda i: (0, i, 0))
    table = lambda t: pl.BlockSpec(t.shape, lambda i: (0, 0, 0))
    return [tok(512), tok(128), tok(128), tok(512), slab, slab, slab, tok(512), pl.BlockSpec((1, 512), lambda i: (0, 0)),
            table(rope_hi), table(rope_lo)]


N_DPROJ = 11
GW_ROWS = 768


def _proj_bwd_w(x, dproj_args, rope_hi, rope_lo, pieces, tb):
    seq = x.shape[0]
    steps = seq // tb
    n_p = len(pieces)
    narrow = [False] * n_p

    def body(*refs):
        x_ref, grads = refs[0], refs[1:1 + N_DPROJ]
        piece_refs = refs[1 + N_DPROJ:1 + N_DPROJ + n_p]
        gw_ref = refs[1 + N_DPROJ + n_p]
        out_refs = refs[2 + N_DPROJ + n_p:2 + N_DPROJ + 2 * n_p]
        step = pl.program_id(0)
        if n_p:
            begin, exchange, combine, end = _ride_phases(piece_refs, out_refs, refs[2 + N_DPROJ + 2 * n_p:], narrow, True)
            pl.when(step == 0)(begin)
            pl.when(step == min(1, steps - 1))(exchange)
            pl.when(step == steps // 2)(combine)

        @pl.when(step == 0)
        def _():
            gw_ref[...] = jnp.zeros_like(gw_ref)

        dproj = _dproj_block(*grads, tb)
        xb = x_ref[...].astype(BF16)
        for r0 in range(0, D_IN_PROJ, GW_ROWS):
            gw_ref[r0:r0 + GW_ROWS, :] += _dot_tn(dproj[:, r0:r0 + GW_ROWS], xb)
        if n_p:
            pl.when(step == steps - 1)(end)

    vmem = pl.BlockSpec(memory_space=pltpu.VMEM)
    whole = pl.BlockSpec((D_IN_PROJ, D_MODEL), lambda i: (0, 0), pipeline_mode=pl.Buffered(1))
    ride_outs, ride_scratch = _ride_shapes(pieces, narrow, True) if n_p else ([], [])
    return pl.pallas_call(
        body, name="proj_bwd_w", grid=(steps,),
        in_specs=[pl.BlockSpec((tb, D_MODEL), lambda i: (i, 0))] + _dproj_specs(tb, rope_hi, rope_lo) + [vmem] * n_p,
        out_specs=[whole] + [vmem] * n_p,
        out_shape=[jax.ShapeDtypeStruct((D_IN_PROJ, D_MODEL), F32)] + [jax.ShapeDtypeStruct(s, F32) for s in ride_outs],
        scratch_shapes=ride_scratch,
        compiler_params=_cparams(("arbitrary",)),
    )(x, *dproj_args, rope_hi, rope_lo, *pieces)


def _proj_bwd_x(dproj_args, rope_hi, rope_lo, dpre, wt, pieces, tb):
    seq = dpre.shape[0]
    steps = seq // tb
    n_p = len(pieces)
    narrow = [True] * n_p

    def body(*refs):
        grads = refs[:N_DPROJ]
        dpre_ref, wt_ref = refs[N_DPROJ:N_DPROJ + 2]
        piece_refs = refs[N_DPROJ + 2:N_DPROJ + 2 + n_p]
        gx_ref = refs[N_DPROJ + 2 + n_p]
        out_refs = refs[N_DPROJ + 3 + n_p:N_DPROJ + 3 + 2 * n_p]
        step = pl.program_id(0)
        if n_p:
            begin, exchange, combine, end = _ride_phases(piece_refs, out_refs, refs[N_DPROJ + 3 + 2 * n_p:], narrow, False)
            pl.when(step == 0)(begin)
            pl.when(step == min(steps // 4, steps - 1))(exchange)
            pl.when(step == steps - 1)(combine)

        dproj = _dproj_block(*grads, tb)
        gx_ref[...] = ALPHA * dpre_ref[...] + _dot(dproj, wt_ref[...])
        if n_p:
            pl.when(step == steps - 1)(end)

    vmem = pl.BlockSpec(memory_space=pltpu.VMEM)
    whole = pl.BlockSpec((D_IN_PROJ, D_MODEL), lambda i: (0, 0), pipeline_mode=pl.Buffered(1))
    ride_outs, ride_scratch = _ride_shapes(pieces, narrow, False) if n_p else ([], [])
    return pl.pallas_call(
        body, name="proj_bwd_x", grid=(steps,),
        in_specs=_dproj_specs(tb, rope_hi, rope_lo) + [pl.BlockSpec((tb, D_MODEL), lambda i: (i, 0)), whole] + [vmem] * n_p,
        out_specs=[pl.BlockSpec((tb, D_MODEL), lambda i: (i, 0))] + [vmem] * n_p,
        out_shape=[jax.ShapeDtypeStruct((seq, D_MODEL), F32)] + [jax.ShapeDtypeStruct(s, F32) for s in ride_outs],
        scratch_shapes=ride_scratch,
        compiler_params=pltpu.CompilerParams(dimension_semantics=("arbitrary",), vmem_limit_bytes=PROJ_BWD_X_VMEM),
    )(*dproj_args, rope_hi, rope_lo, dpre, wt, *pieces)


def _adamw(w, g, m, v, name):
    rows, cols = w.shape
    tb = rows
    while tb * cols * 4 > ADAMW_BLOCK_BYTES and tb % 16 == 0:
        tb //= 2

    def body(w_ref, g_ref, m_ref, v_ref, d_ref, nm_ref, nv_ref):
        _adamw_update(w_ref, g_ref, m_ref, v_ref, d_ref, nm_ref, nv_ref)

    spec = pl.BlockSpec((tb, cols), lambda i: (i, 0))
    return pl.pallas_call(
        body, name=name, grid=(rows // tb,), in_specs=[spec] * 4, out_specs=[spec] * 3,
        out_shape=[jax.ShapeDtypeStruct((rows, cols), F32)] * 3,
        compiler_params=_cparams(("arbitrary",)),
    )(w, g, m, v)


def _adamw_update(w_ref, g_ref, m_ref, v_ref, d_ref, nm_ref, nv_ref):
    g_blk = g_ref[...]
    m_new = ADAM_B1 * m_ref[...] + (1.0 - ADAM_B1) * g_blk
    v_new = ADAM_B2 * v_ref[...] + (1.0 - ADAM_B2) * (g_blk * g_blk)
    m_hat = m_new / (1.0 - ADAM_B1 ** ADAM_STEP)
    v_hat = v_new / (1.0 - ADAM_B2 ** ADAM_STEP)
    d_ref[...] = -ADAM_LR * (m_hat / (jnp.sqrt(v_hat) + ADAM_EPS) + ADAM_WD * w_ref[...])
    nm_ref[...] = m_new
    nv_ref[...] = v_new


def _adamw_many(groups, name):
    n = len(groups)

    def body(*refs):
        for p in range(n):
            _adamw_update(*refs[4 * p:4 * p + 4], *refs[4 * n + 3 * p:4 * n + 3 * p + 3])

    return pl.pallas_call(
        body, name=name,
        out_shape=[jax.ShapeDtypeStruct(grp[0].shape, F32) for grp in groups for _ in range(3)],
    )(*[a for grp in groups for a in grp])


_WEIGHTS = ["w_in", "attn_sink", "ssm_a_re", "ssm_a_im", "ssm_log_dt", "ssm_b_re", "ssm_b_im", "ssm_c_re", "ssm_c_im",
            "ssm_d", "w_glu", "b_glu", "norm_attn_g", "norm_ssm_g", "w_out", "ln_g", "ln_b"]
N_DG = N_DIR * N_GROUPS
BIG_ROWS = N_DG * SSM_CH * SSM_STATE // 128
TINY_ROWS = 64


def _pack_small_grads(g_bc, g_vec, g_ar, g_ai, g_dt, g_sink, loss):
    big = jnp.stack([t.reshape(BIG_ROWS, 128) for t in g_bc])
    row = lambda t: jnp.pad(t.reshape(1, -1), ((0, 0), (0, 128 - t.size)))
    tiny = jnp.concatenate([g_vec.reshape(64, 128), g_ar.reshape(32, 128), g_ai.reshape(32, 128), row(g_dt), row(g_sink),
                            row(loss), jnp.zeros((N_CHIPS * TINY_ROWS - 131, 128), F32)], axis=0)
    return jnp.concatenate([big, tiny.reshape(N_CHIPS, TINY_ROWS, 128)], axis=1)


def _unpack_small_grads(packed):
    big = packed[:, :BIG_ROWS].reshape(N_CHIPS, 2 * BIG_ROWS, SSM_STATE)
    tiny = packed[:, BIG_ROWS:].reshape(N_CHIPS * TINY_ROWS, 128)
    g_vec = tiny[0:64].reshape(8, 1024)
    return tiny[130, 0], {
        "ssm_b_re": big[0], "ssm_b_im": big[1], "ssm_c_re": big[2], "ssm_c_im": big[3],
        "ln_g": g_vec[0:1], "ln_b": g_vec[1:2],
        "norm_attn_g": _from_pair_order(g_vec[2:3, :D_ATTN]), "norm_ssm_g": g_vec[2:3, D_ATTN:],
        "ssm_d": g_vec[3:4, :D_SSM], "b_glu": g_vec[3:4, D_SSM:],
        "ssm_a_re": tiny[64:96].reshape(N_DG, SSM_STATE), "ssm_a_im": tiny[96:128].reshape(N_DG, SSM_STATE),
        "ssm_log_dt": tiny[128:129, :N_DG].reshape(N_DIR, N_GROUPS), "attn_sink": tiny[129:130, :N_Q_HEADS],
    }


def _small_view(name, t):
    if name in ("ssm_b_re", "ssm_b_im"):
        return jnp.swapaxes(t[0], 2, 3).reshape(N_DG * SSM_CH, SSM_STATE)
    if name in ("ssm_c_re", "ssm_c_im"):
        return t.reshape(N_DG * SSM_CH, SSM_STATE)
    if name in ("ssm_a_re", "ssm_a_im"):
        return t.reshape(N_DG, SSM_STATE)
    if name == "ssm_log_dt":
        return t.reshape(N_DIR, N_GROUPS)
    return t.reshape(1, -1)


def _small_unview(name, t, shape):
    if name in ("ssm_b_re", "ssm_b_im"):
        return jnp.swapaxes(t.reshape(N_DIR, N_GROUPS, SSM_CH, SSM_STATE), 2, 3).reshape(shape)
    return t.reshape(shape)


def kernel(x, w_in, attn_sink, ssm_a_re, ssm_a_im, ssm_log_dt, ssm_b_re, ssm_b_im, ssm_c_re, ssm_c_im, ssm_d, w_glu, b_glu, norm_attn_g, norm_ssm_g, w_out, ln_g, ln_b, loss_target, m_w_in, m_attn_sink, m_ssm_a_re, m_ssm_a_im, m_ssm_log_dt, m_ssm_b_re, m_ssm_b_im, m_ssm_c_re, m_ssm_c_im, m_ssm_d, m_w_glu, m_b_glu, m_norm_attn_g, m_norm_ssm_g, m_w_out, m_ln_g, m_ln_b, v_w_in, v_attn_sink, v_ssm_a_re, v_ssm_a_im, v_ssm_log_dt, v_ssm_b_re, v_ssm_b_im, v_ssm_c_re, v_ssm_c_im, v_ssm_d, v_w_glu, v_b_glu, v_norm_attn_g, v_norm_ssm_g, v_w_out, v_ln_g, v_ln_b):
    args = dict(locals())
    weights = {n: args[n] for n in _WEIGHTS}
    mom_m = {n: args["m_" + n] for n in _WEIGHTS}
    mom_v = {n: args["v_" + n] for n in _WEIGHTS}
    xs = x[0]
    target = loss_target[0]

    (wt_g,) = _all_gather_chips([w_in[0].T], BF16, "gather_weights")
    wt_full = wt_g.reshape(D_IN_PROJ, D_MODEL)

    g_x, r_wt, r_w_out, r_w_glu, g_small_all = _local_step(
        xs, target, wt_full, w_glu[0], w_out[0], attn_sink, ssm_a_re, ssm_a_im, ssm_log_dt, ssm_b_re, ssm_b_im,
        ssm_c_re, ssm_c_im, ssm_d, b_glu, norm_attn_g, norm_ssm_g, ln_g, ln_b, sharded=True)
    loss, small_grads = _unpack_small_grads(g_small_all)

    grads, deltas, new_m, new_v = {}, {}, {}, {}
    d_w, m_w, v_w = _adamw(w_in[0].T, r_wt, m_w_in[0].T, v_w_in[0].T, "adamw_w_in")
    grads["w_in"], deltas["w_in"], new_m["w_in"], new_v["w_in"] = r_wt.T[None], d_w.T[None], m_w.T[None], v_w.T[None]
    for n, g in (("w_out", r_w_out), ("w_glu", r_w_glu)):
        d_w, m_w, v_w = _adamw(weights[n][0], g, mom_m[n][0], mom_v[n][0], "adamw_" + n)
        grads[n], deltas[n], new_m[n], new_v[n] = g[None], d_w[None], m_w[None], v_w[None]
    names = sorted(small_grads)
    updates = _adamw_many([(_small_view(n, weights[n]), small_grads[n], _small_view(n, mom_m[n]), _small_view(n, mom_v[n]))
                           for n in names], "adamw_small")
    for i, n in enumerate(names):
        shape = weights[n].shape
        grads[n] = _small_unview(n, small_grads[n], shape)
        deltas[n], new_m[n], new_v[n] = (_small_unview(n, t, shape) for t in updates[3 * i:3 * i + 3])

    return (loss, g_x[None], *[grads[n] for n in _WEIGHTS], *[deltas[n] for n in _WEIGHTS],
            *[new_m[n] for n in _WEIGHTS], *[new_v[n] for n in _WEIGHTS])


def _local_step(xs, target, wt_full, w_glu_in, w_out_in, attn_sink, ssm_a_re, ssm_a_im, ssm_log_dt, ssm_b_re,
                ssm_b_im, ssm_c_re, ssm_c_im, ssm_d, b_glu, norm_attn_g, norm_ssm_g, ln_g, ln_b, sharded):
    seq = xs.shape[0]

    a_r, a_i = _small_view("ssm_a_re", ssm_a_re), _small_view("ssm_a_im", ssm_a_im)
    log_dt = ssm_log_dt.reshape(N_DG, 1)
    b_r, b_i = _small_view("ssm_b_re", ssm_b_re), _small_view("ssm_b_im", ssm_b_im)
    c_r, c_i = _small_view("ssm_c_re", ssm_c_re), _small_view("ssm_c_im", ssm_c_im)
    ssm_tb = min(SSM_BLOCK, seq)
    sub_len = ssm_tb // SUBSEG
    lam, bb, bbt, cb, cb_t = _ssm_params_fwd(a_r, a_i, log_dt, b_r, b_i, c_r, c_i, int(math.log2(sub_len)))
    lam = lam.reshape(4, N_DIR, 1, STATE_W)

    rope_hi, rope_lo = _rope_tables(seq)
    projected = _proj(xs, wt_full, rope_hi, rope_lo, [w_glu_in, w_out_in] if sharded else [], min(512, seq))
    q_stack, k_rot, v_bf, z_attn, u, z_ssm = projected[:6]
    if sharded:
        w_glu_full, w_out_full = projected[6].reshape(D_SSM, D_SSM), projected[7].reshape(D_MODEL, D_MODEL)
    else:
        w_glu_full, w_out_full = w_glu_in, w_out_in
    sink128 = jnp.broadcast_to(attn_sink[0][:, None, None], (N_Q_HEADS, 1, 128))
    attn_bias = _attn_bias()
    o = _attn_fwd(q_stack, k_rot, v_bf, sink128, attn_bias)
    ys, starts = [], []
    for d in range(N_DIR):
        y_d, s_r, s_i = _ssm_fwd(u, lam, bb, cb, direction=d, tb=ssm_tb, name=f"ssm_fwd_{d}")
        ys.append(y_d)
        starts.append((s_r, s_i))

    row = lambda t: t.reshape(1, -1)
    g_attn_p = _to_pair_order(norm_attn_g)
    loss_blk, d_o, d_za, d_ylin, d_zs, d_pre, g_w_out, g_w_glu, g_vec = _mid(
        o, z_attn, u, ys[0], ys[1], z_ssm, xs, target, row(ssm_d), w_glu_full, row(b_glu),
        g_attn_p, row(norm_ssm_g), w_out_full, row(ln_g), row(ln_b), min(256, seq))

    pieces = [g_w_glu.reshape(N_CHIPS, -1, D_SSM), g_w_out.reshape(N_CHIPS, -1, D_MODEL)] if sharded else []
    attn_grads = _attn_bwd(q_stack, k_rot, v_bf, sink128, attn_bias, d_o, pieces)
    dq, dk, dv, g_sink = attn_grads[:4]
    if sharded:
        g_w_glu, g_w_out = attn_grads[4:]
    dus, g_bb, g_cb, g_lam = [], [], [], []
    for d in range(N_DIR):
        du_d, gb_d, gc_d, dl_d = _ssm_bwd(u, d_ylin, starts[d], lam, bb, bbt, cb_t, direction=d, tb=ssm_tb,
                                          name=f"ssm_bwd_{d}")
        dus.append(du_d)
        g_bb.append(gb_d)
        g_cb.append(gc_d)
        g_lam.append(dl_d)
    g_ar, g_ai, g_dt, g_br, g_bi, g_cr, g_ci = _ssm_params_bwd(a_r, a_i, log_dt, b_r, b_i, g_bb, g_cb, g_lam)

    g_small = _pack_small_grads([g_br, g_bi, g_cr, g_ci], g_vec, g_ar, g_ai, g_dt, g_sink[:, 0], loss_blk[0, 0])
    dproj_args = (dq, dk, dv, d_za, dus[0], dus[1], d_ylin, d_zs, row(ssm_d))
    w_grads = _proj_bwd_w(xs, dproj_args, rope_hi, rope_lo, [g_small] if sharded else [], min(512, seq))
    g_wt = w_grads[0]
    if sharded:
        g_small = w_grads[1]
    x_grads = _proj_bwd_x(dproj_args, rope_hi, rope_lo, d_pre, wt_full,
                          [g_wt.reshape(N_CHIPS, -1, D_MODEL)] if sharded else [], min(512, seq))
    g_x = x_grads[0]
    if sharded:
        g_wt = x_grads[1]
    return g_x, g_wt, g_w_out, g_w_glu, g_small
```

```python
import functools
import math

import numpy as np
import jax
import jax.numpy as jnp
from jax import lax
from jax.experimental import pallas as pl
from jax.experimental.pallas import tpu as pltpu

F32 = jnp.float32
BF16 = jnp.bfloat16
MESH = pl.DeviceIdType.MESH

D_MODEL = 1024
D_ATTN = 512
D_SSM = 512
HEAD_DIM = 64
N_Q_HEADS = 8
WINDOW = 128
ROPE_THETA = 10000.0
SSM_CH = 16
N_GROUPS = 32
SSM_STATE = 64
N_DIR = 2
STATE_W = N_GROUPS * SSM_STATE
N_SLAB = 4
SLAB_IN = 128
SLAB_ST = 512
NORM_EPS = 1e-5
NEG_INF = -1e30
ALPHA = 2.0 ** 0.25
D_IN_PROJ = 2304
N_CHIPS = 4

ADAM_LR = 0.001
ADAM_B1 = 0.9
ADAM_B2 = 0.999
ADAM_EPS = 1e-08
ADAM_WD = 0.01
ADAM_STEP = 10

SUBSEG = 8
SCAN_LANES = 512
SSM_BLOCK = 512
VMEM_LIMIT = 48 * 1024 * 1024
ADAMW_BLOCK_BYTES = 3 * 512 * 1024
PROJ_BWD_X_VMEM = 56 * 1024 * 1024

def _to_pair_order(row):
    return jnp.transpose(row.reshape(2, 4, HEAD_DIM), (1, 0, 2)).reshape(1, D_ATTN)


def _from_pair_order(row):
    return jnp.transpose(row.reshape(4, 2, HEAD_DIM), (1, 0, 2)).reshape(1, D_ATTN)


def _cparams(sem=None):
    return pltpu.CompilerParams(dimension_semantics=sem, vmem_limit_bytes=VMEM_LIMIT)


def _dot(a, b):
    return jnp.dot(a, b, preferred_element_type=F32)


def _dot_nt(a, b):
    return lax.dot_general(a, b, (((1,), (1,)), ((), ())), preferred_element_type=F32)


def _dot_tn(a, b):
    return lax.dot_general(a, b, (((0,), (0,)), ((), ())), preferred_element_type=F32)


def _sigmoid(z):
    return 1.0 / (1.0 + jnp.exp(-z))


def _all_gather_chips(shards, out_dtype, name):
    n = len(shards)

    def body(*refs):
        start, relay, finish = _gather_phases(refs[:n], refs[n:2 * n], *refs[2 * n:], out_dtype)
        start()
        relay()
        finish()

    vmem = pl.BlockSpec(memory_space=pltpu.VMEM)
    return pl.pallas_call(
        body, name=name,
        out_shape=[jax.ShapeDtypeStruct((N_CHIPS,) + s.shape, out_dtype) for s in shards],
        in_specs=[vmem] * n, out_specs=[vmem] * n,
        scratch_shapes=_gather_sems(n),
        compiler_params=pltpu.CompilerParams(vmem_limit_bytes=VMEM_LIMIT),
    )(*shards)


def _gather_sems(n):
    return [pltpu.SemaphoreType.DMA((6 * n,)), pltpu.SemaphoreType.DMA((6 * n,))]


def _gather_phases(in_refs, out_refs, send_sems, recv_sems, out_dtype):
    n = len(in_refs)
    x, y, c = lax.axis_index("x"), lax.axis_index("y"), lax.axis_index("c")
    sibling = (x, y, 1 - c)
    chips = [(1 - x, y), (x, 1 - y), (1 - x, 1 - y)]

    def half_of(a, px, py, half):
        rows = in_refs[a].shape[0] // 2
        return out_refs[a].at[2 * px + py, pl.ds(half * rows, rows), :]

    def copy(a, k, px, py, half, to):
        blk = half_of(a, px, py, half)
        return pltpu.make_async_remote_copy(src_ref=blk, dst_ref=blk, send_sem=send_sems.at[6 * a + k],
                                            recv_sem=recv_sems.at[6 * a + k], device_id=to, device_id_type=MESH)

    first = [copy(a, j, x, y, c, (*chips[j], c)) for a in range(n) for j in range(3)]
    passed = [copy(a, 3 + j, *chips[j], c, sibling) for a in range(n) for j in range(3)]

    def start():
        for a in range(n):
            out_refs[a][2 * x + y] = in_refs[a][...].astype(out_dtype)
        for cp in first:
            cp.start()

    def relay():
        for a in range(n):
            for j in range(3):
                copy(a, j, *chips[j], c, (x, y, c)).wait_recv()
                passed[3 * a + j].start()

    def finish():
        for a in range(n):
            for j in range(3):
                copy(a, 3 + j, *chips[j], 1 - c, (x, y, c)).wait_recv()
        for cp in first + passed:
            cp.wait_send()

    return start, relay, finish


SEMS_PER_ARRAY = 14


def _reduce_scratch(shapes, narrow):
    half = [(N_CHIPS, s[1] // 2, s[2]) for s in shapes]
    wire = [BF16 if nar else F32 for nar in narrow]
    n = len(shapes)
    return ([pltpu.VMEM(half[a], F32) for a in range(n)] + [pltpu.VMEM(half[a], wire[a]) for a in range(n)]
            + [pltpu.VMEM(half[a], wire[a]) for a in range(n)]
            + [pltpu.SemaphoreType.DMA((SEMS_PER_ARRAY * n,)), pltpu.SemaphoreType.DMA((SEMS_PER_ARRAY * n,))])


def _reduce_phases(p_refs, out_refs, a_refs, s_refs, b_refs, send_sems, recv_sems, narrow, gather_last):
    n = len(p_refs)
    halves = [p.shape[1] // 2 for p in p_refs]
    wire = [BF16 if nar else F32 for nar in narrow]
    x, y, c = lax.axis_index("x"), lax.axis_index("y"), lax.axis_index("c")
    me = 2 * x + y
    sibling = (x, y, 1 - c)
    chips = [(1 - x, y), (x, 1 - y), (1 - x, 1 - y)]
    slot = [2 * px + py for px, py in chips]
    last = n - 1

    def copy(a, k, src, dst, to):
        return pltpu.make_async_remote_copy(src_ref=src, dst_ref=dst, send_sem=send_sems.at[SEMS_PER_ARRAY * a + k],
                                            recv_sem=recv_sems.at[SEMS_PER_ARRAY * a + k],
                                            device_id=to, device_id_type=MESH)

    def rows(a, half):
        return pl.ds(pl.multiple_of(half * halves[a], 16), halves[a])

    def finished(a, k, half):
        if gather_last and a == last:
            return out_refs[a].at[k, rows(a, half), :]
        return out_refs[a].at[rows(a, half), :]

    order = slot + [me]
    swaps = [[copy(a, q, p_refs[a].at[order[q], rows(a, 1 - c), :], a_refs[a].at[order[q]], sibling)
              for q in range(N_CHIPS)] for a in range(n)]
    sends = [[copy(a, 4 + j, s_refs[a].at[slot[j]], b_refs[a].at[me], (*chips[j], c)) for j in range(3)] for a in range(n)]
    backs = [copy(a, 7, finished(a, me, c), finished(a, me, c), sibling) for a in range(n)]
    spread = [copy(last, 8 + j, finished(last, me, c), finished(last, me, c), (*chips[j], c)) for j in range(3)]
    relays = [copy(last, 11 + j, finished(last, slot[j], c), finished(last, slot[j], c), sibling) for j in range(3)]

    def start():
        for group in swaps:
            for cp in group:
                cp.start()

    def exchange():
        for a in range(n):
            for q in range(N_CHIPS):
                swaps[a][q].wait_recv()
                acc = a_refs[a][order[q]] + p_refs[a][order[q], rows(a, c), :]
                a_refs[a][order[q]] = acc
                s_refs[a][order[q]] = acc.astype(wire[a])
                if q < 3:
                    sends[a][q].start()
            b_refs[a][me] = s_refs[a][me]

    def combine():
        for a in range(n):
            for j in range(3):
                copy(a, 4 + j, s_refs[a].at[slot[j]], b_refs[a].at[slot[j]], (x, y, c)).wait_recv()
            terms = [jnp.where(me == k, a_refs[a][k], b_refs[a][k].astype(F32)) for k in range(N_CHIPS)]
            total = (terms[0] + terms[1]) + (terms[2] + terms[3])
            if gather_last and a == last:
                out_refs[a][me, rows(a, c), :] = total
            else:
                out_refs[a][rows(a, c), :] = total
            backs[a].start()
        if gather_last:
            for cp in spread:
                cp.start()

    def finish():
        if gather_last:
            for j in range(3):
                copy(last, 8 + j, finished(last, slot[j], c), finished(last, slot[j], c), (x, y, c)).wait_recv()
                relays[j].start()
        for a in range(n):
            copy(a, 7, finished(a, me, 1 - c), finished(a, me, 1 - c), (x, y, c)).wait_recv()
        if gather_last:
            for j in range(3):
                copy(last, 11 + j, finished(last, slot[j], 1 - c), finished(last, slot[j], 1 - c), (x, y, c)).wait_recv()
        started = [cp for group in swaps + sends for cp in group] + backs + (spread + relays if gather_last else [])
        for cp in started:
            cp.wait_send()

    return start, exchange, combine, finish


def _ssm_param_values(ar, ai, logdt):
    dt = jnp.exp(logdt)
    mag = jnp.exp(dt * ar)
    cs, sn = jnp.cos(dt * ai), jnp.sin(dt * ai)
    lr, li = mag * cs, mag * sn
    den = ar * ar + ai * ai
    nr = (lr - 1.0) * ar + li * ai
    ni = li * ar - (lr - 1.0) * ai
    return dt, mag, lr, li, den, nr, ni


GROUPS_PER_SLAB = N_GROUPS // N_SLAB


def _slab_masks():
    def eq(shape, f_row, f_col):
        return (f_row(lax.broadcasted_iota(jnp.int32, shape, 0)) == f_col(lax.broadcasted_iota(jnp.int32, shape, 1))).astype(F32)
    spread = eq((SSM_STATE, SLAB_ST), lambda r: r, lambda c: c % SSM_STATE)
    spread_t = eq((SLAB_ST, SSM_STATE), lambda r: r % SSM_STATE, lambda c: c)
    keep = eq((SLAB_IN, SLAB_ST), lambda r: r // SSM_CH, lambda c: c // SSM_STATE)
    keep_t = eq((SLAB_ST, SLAB_IN), lambda r: r // SSM_STATE, lambda c: c // SSM_CH)
    repeat = eq((N_DG * SSM_CH, N_DG), lambda r: r // SSM_CH, lambda c: c)
    return spread, spread_t, keep, keep_t, repeat


def _split3(t):
    hi = t.astype(BF16)
    rest = t - hi.astype(F32)
    mid = rest.astype(BF16)
    return hi, mid, (rest - mid.astype(F32)).astype(BF16)


def _select(dot, ones01, t, ones_first):
    o = ones01.astype(BF16)
    parts = [dot(o, p) if ones_first else dot(p, o) for p in _split3(t)]
    return (parts[0] + parts[1]) + parts[2]


def _ssm_params_fwd(ar, ai, logdt, br, bi, cr, ci, n_square):
    def body(ar_ref, ai_ref, dt_ref, br_ref, bi_ref, cr_ref, ci_ref, lam_ref, bb_ref, bbt_ref, cb_ref, cbt_ref):
        _, _, lr, li, den, nr, ni = _ssm_param_values(ar_ref[...], ai_ref[...], dt_ref[...])
        lam_ref[0] = lr
        lam_ref[1] = li
        pr, pi = lr, li
        for _ in range(n_square):
            pr, pi = pr * pr - pi * pi, 2.0 * pr * pi
        lam_ref[2] = pr
        lam_ref[3] = pi
        spread, spread_t, keep, keep_t, repeat = _slab_masks()
        fr = _select(_dot, repeat, nr / den, True)
        fi = _select(_dot, repeat, ni / den, True)
        b_r, b_i = br_ref[...], bi_ref[...]
        bbar = (fr * b_r - fi * b_i, fr * b_i + fi * b_r)
        c_par = (cr_ref[...], ci_ref[...])
        spread, spread_t = spread.astype(BF16), spread_t.astype(BF16)
        for src, wide_ref, tall_ref in ((bbar, bb_ref, bbt_ref), (c_par, cbt_ref, cb_ref)):
            for q in range(2):
                for d in range(N_DIR):
                    for k in range(N_SLAB):
                        r0 = (d * N_GROUPS + k * GROUPS_PER_SLAB) * SSM_CH
                        blk = src[q][r0:r0 + SLAB_IN].astype(BF16)
                        wide_ref[q, d, k] = (_dot(blk, spread) * keep).astype(BF16)
                        tall_ref[q, d, k] = (_dot_nt(spread_t, blk) * keep_t).astype(BF16)

    wide = jax.ShapeDtypeStruct((2, N_DIR, N_SLAB, SLAB_IN, SLAB_ST), BF16)
    tall = jax.ShapeDtypeStruct((2, N_DIR, N_SLAB, SLAB_ST, SLAB_IN), BF16)
    return pl.pallas_call(body, name="ssm_params_fwd",
                          out_shape=[jax.ShapeDtypeStruct((4,) + ar.shape, F32), wide, tall, tall, wide],
                          compiler_params=pltpu.CompilerParams(vmem_limit_bytes=VMEM_LIMIT),
                          )(ar, ai, logdt, br, bi, cr, ci)


def _ssm_params_bwd(ar, ai, logdt, br, bi, g_slabs_b, g_slabs_c, g_lam):
    def body(ar_ref, ai_ref, dt_ref, br_ref, bi_ref, gb0_ref, gb1_ref, gc0_ref, gc1_ref, gl0_ref, gl1_ref,
             gar_ref, gai_ref, gdt_ref, gbr_ref, gbi_ref, gcr_ref, gci_ref, dbb, dlam):
        spread, spread_t, keep, keep_t, repeat = _slab_masks()
        for d, (gb_ref, gc_ref) in enumerate(((gb0_ref, gc0_ref), (gb1_ref, gc1_ref))):
            for q in range(2):
                for k in range(N_SLAB):
                    r0 = (d * N_GROUPS + k * GROUPS_PER_SLAB) * SSM_CH
                    dbb[q, r0:r0 + SLAB_IN, :] = _select(_dot, spread_t, gb_ref[q, k] * keep, False)
                    out_ref = gcr_ref if q == 0 else gci_ref
                    out_ref[r0:r0 + SLAB_IN, :] = _select(_dot_tn, spread_t, gc_ref[q, k] * keep_t, False)
        grp = (lax.broadcasted_iota(jnp.int32, (N_GROUPS, STATE_W), 0)
               == lax.broadcasted_iota(jnp.int32, (N_GROUPS, STATE_W), 1) // SSM_STATE).astype(F32)
        pick = (lax.broadcasted_iota(jnp.int32, (STATE_W, SSM_STATE), 0) % SSM_STATE
                == lax.broadcasted_iota(jnp.int32, (STATE_W, SSM_STATE), 1)).astype(F32)
        for d, gl_ref in enumerate((gl0_ref, gl1_ref)):
            for q in range(2):
                row = jnp.sum(gl_ref[q], axis=0, keepdims=True)
                dlam[q, d * N_GROUPS:(d + 1) * N_GROUPS, :] = _select(_dot, pick, grp * row, False)

        a_r, a_i = ar_ref[...], ai_ref[...]
        dt, mag, lr, li, den, nr, ni = _ssm_param_values(a_r, a_i, dt_ref[...])
        fr = _select(_dot, repeat, nr / den, True)
        fi = _select(_dot, repeat, ni / den, True)
        b_r, b_i = br_ref[...], bi_ref[...]
        g_r, g_i = dbb[0], dbb[1]
        gbr_ref[...] = fr * g_r + fi * g_i
        gbi_ref[...] = fr * g_i - fi * g_r
        d_fr = _select(_dot_tn, repeat, b_r * g_r + b_i * g_i, True)
        d_fi = _select(_dot_tn, repeat, b_r * g_i - b_i * g_r, True)
        d_nr, d_ni = d_fr / den, d_fi / den
        d_den = -(d_fr * nr + d_fi * ni) / (den * den)
        d_lr = dlam[0] + d_nr * a_r - d_ni * a_i
        d_li = dlam[1] + d_nr * a_i + d_ni * a_r
        d_ar = d_nr * (lr - 1.0) + d_ni * li + d_den * 2.0 * a_r
        d_ai = d_nr * li - d_ni * (lr - 1.0) + d_den * 2.0 * a_i
        d_mag = (d_lr * lr + d_li * li) / mag
        d_theta = d_li * lr - d_lr * li
        gar_ref[...] = d_ar + d_mag * mag * dt
        gai_ref[...] = d_ai + d_theta * dt
        d_dt = d_mag * mag * a_r + d_theta * a_i
        gdt_ref[...] = jnp.sum(d_dt, axis=1, keepdims=True) * dt

    small = jax.ShapeDtypeStruct(ar.shape, F32)
    big = jax.ShapeDtypeStruct(br.shape, F32)
    return pl.pallas_call(
        body, name="ssm_params_bwd",
        out_shape=[small, small, jax.ShapeDtypeStruct(logdt.shape, F32), big, big, big, big],
        scratch_shapes=[pltpu.VMEM((2,) + br.shape, F32), pltpu.VMEM((2,) + ar.shape, F32)],
        compiler_params=pltpu.CompilerParams(vmem_limit_bytes=VMEM_LIMIT),
    )(ar, ai, logdt, br, bi, *g_slabs_b, *g_slabs_c, *g_lam)


ROPE_GROUP = 128


def _rope_tables(seq):
    half = HEAD_DIM // 2
    inv_freq = jnp.tile(ROPE_THETA ** (-jnp.arange(half, dtype=F32) / half), 4)
    sign = jnp.tile(jnp.concatenate([-jnp.ones((half,), F32), jnp.ones((half,), F32)]), 2)

    def table(pos):
        ang = pos.astype(F32)[:, None] * inv_freq[None, :]
        return jnp.stack([jnp.cos(ang), jnp.sin(ang), sign * jnp.sin(ang)])

    return table(jnp.arange(seq // ROPE_GROUP) * ROPE_GROUP), table(jnp.arange(ROPE_GROUP))


def _rope_block(hi_ref, lo_ref, first_group, n_groups):
    cl, sl, sl_s = lo_ref[0], lo_ref[1], lo_ref[2]
    cos, sin = [], []
    for g in range(n_groups):
        ch, sh, sh_s = (hi_ref[q, pl.ds(first_group + g, 1), :] for q in range(3))
        cos.append(ch * cl - sh * sl)
        sin.append(sh_s * cl + ch * sl_s)
    return jnp.concatenate(cos, axis=0), jnp.concatenate(sin, axis=0)


def _rotate_half_unsigned(t):
    lane = lax.broadcasted_iota(jnp.int32, t.shape, 1)
    return jnp.where((lane % HEAD_DIM) < HEAD_DIM // 2, pltpu.roll(t, 96, 1), pltpu.roll(t, 32, 1))


def _rope(t, cos, sin_signed):
    return t * cos + _rotate_half_unsigned(t) * sin_signed


def _pair_blocks(base):
    out = []
    for j in range(4):
        for g in range(2):
            nat = base + HEAD_DIM * (4 * g + j)
            par = base + 128 * j + HEAD_DIM * g
            out.append((slice(nat, nat + HEAD_DIM), slice(par, par + HEAD_DIM)))
    return out


W_Q, W_KV, W_ZA, W_U, W_ZS = 0, 512, 768, 1280, 1792


def _proj(x, wt, rope_hi, rope_lo, shards, tb):
    seq = x.shape[0]
    steps = seq // tb
    n_sh = len(shards)

    def body(*refs):
        x_ref, wt_ref, hi_ref, lo_ref = refs[:4]
        shard_refs = refs[4:4 + n_sh]
        q_ref, k_ref, v_ref, za_ref, u_ref, zs_ref = refs[4 + n_sh:10 + n_sh]
        gathered_refs = refs[10 + n_sh:10 + 2 * n_sh]
        wp = refs[10 + 2 * n_sh]
        step = pl.program_id(0)
        if n_sh:
            landing_refs = refs[11 + 2 * n_sh:11 + 3 * n_sh]
            start, relay, finish = _gather_phases(shard_refs, landing_refs, *refs[11 + 3 * n_sh:], BF16)
            pl.when(step == 0)(start)
            pl.when(step == max(steps - 2, 0))(relay)

        @pl.when(step == 0)
        def _():
            for dst_base, src_base in ((0, W_Q), (512, W_ZA)):
                for nat, par in _pair_blocks(0):
                    wp[dst_base + par.start:dst_base + par.stop, :] = wt_ref[src_base + nat.start:src_base + nat.stop, :]

        xb = x_ref[...].astype(BF16)
        cos, sin = _rope_block(hi_ref, lo_ref, pl.program_id(0) * (tb // ROPE_GROUP), tb // ROPE_GROUP)
        lo = lax.broadcasted_iota(jnp.int32, (tb, 128), 1) < HEAD_DIM
        q = _dot_nt(xb, wp[0:512, :])
        for j in range(4):
            qj = _rope(q[:, 128 * j:128 * (j + 1)], cos, sin)
            q_ref[j] = jnp.where(lo, qj, 0.0).astype(BF16)
            q_ref[4 + j] = jnp.where(lo, 0.0, qj).astype(BF16)
        kv = _dot_nt(xb, wt_ref[W_KV:W_ZA, :])
        k_ref[...] = _rope(kv[:, 0:128], cos, sin).astype(BF16)
        v_ref[...] = kv[:, 128:256].astype(BF16)
        za_ref[...] = _dot_nt(xb, wp[512:1024, :])
        u_val = _dot_nt(xb, wt_ref[W_U:W_ZS, :])
        for k in range(N_SLAB):
            u_ref[k] = u_val[:, k * SLAB_IN:(k + 1) * SLAB_IN]
        zs_ref[...] = _dot_nt(xb, wt_ref[W_ZS:D_IN_PROJ, :])
        if n_sh:
            @pl.when(step == steps - 1)
            def _():
                finish()
                for a in range(n_sh):
                    gathered_refs[a][...] = landing_refs[a][...]

    row = lambda w: pl.BlockSpec((tb, w), lambda i: (i, 0))
    table = lambda t: pl.BlockSpec(t.shape, lambda i: (0, 0, 0))
    vmem = pl.BlockSpec(memory_space=pltpu.VMEM)
    return pl.pallas_call(
        body, name="proj", grid=(steps,),
        in_specs=[row(D_MODEL), pl.BlockSpec((D_IN_PROJ, D_MODEL), lambda i: (0, 0), pipeline_mode=pl.Buffered(1)),
                  table(rope_hi), table(rope_lo)] + [vmem] * n_sh,
        out_specs=[pl.BlockSpec((8, tb, 128), lambda i: (0, i, 0)), row(128), row(128), row(512),
                   pl.BlockSpec((N_SLAB, tb, SLAB_IN), lambda i: (0, i, 0)), row(512)] + [vmem] * n_sh,
        out_shape=[jax.ShapeDtypeStruct((8, seq, 128), BF16), jax.ShapeDtypeStruct((seq, 128), BF16),
                   jax.ShapeDtypeStruct((seq, 128), BF16), jax.ShapeDtypeStruct((seq, 512), F32),
                   jax.ShapeDtypeStruct((N_SLAB, seq, SLAB_IN), F32), jax.ShapeDtypeStruct((seq, 512), F32)]
        + [jax.ShapeDtypeStruct((N_CHIPS,) + s.shape, BF16) for s in shards],
        scratch_shapes=[pltpu.VMEM((1024, D_MODEL), BF16)] + [pltpu.VMEM((N_CHIPS,) + s.shape, BF16) for s in shards]
        + (_gather_sems(n_sh) if n_sh else []),
        compiler_params=_cparams(("arbitrary",)),
    )(x, wt, rope_hi, rope_lo, *shards)


ATT_TQ = 128
ATT_KEYS = ATT_TQ + 2 * WINDOW


def _attn_window(i, seq):
    start = jnp.clip(i * ATT_TQ - WINDOW, 0, seq - ATT_KEYS)
    return pl.multiple_of(start, WINDOW)


def _attn_bias():
    r = np.arange(ATT_TQ)[None, :, None]
    c = np.arange(ATT_KEYS)[None, None, :]
    off = np.array([0, WINDOW, ATT_KEYS - ATT_TQ])[:, None, None]
    return jnp.asarray(np.where(np.abs(r + off - c) <= WINDOW, 0.0, NEG_INF).astype(np.float32))


def _attn_bias_spec(nblk):
    pick = lambda i: jnp.where(i == 0, 0, jnp.where(i == nblk - 1, 2, 1))
    return pl.BlockSpec((None, ATT_TQ, ATT_KEYS), lambda i: (pick(i), 0, 0))


def _attn_softmax(q_ref, k_ref, v_ref, sink_ref, bias_ref, start):
    kw = k_ref[pl.ds(start, ATT_KEYS), :]
    vw = v_ref[pl.ds(start, ATT_KEYS), :]
    qall = q_ref[...].reshape(N_Q_HEADS * ATT_TQ, 128)
    s = (_dot_nt(qall, kw) * (HEAD_DIM ** -0.5)).reshape(N_Q_HEADS, ATT_TQ, ATT_KEYS) + bias_ref[...][None]
    tiles = [s[:, :, 128 * t:128 * (t + 1)] for t in range(ATT_KEYS // 128)]
    m = jnp.max(functools.reduce(jnp.maximum, tiles), axis=2, keepdims=True)
    sink = sink_ref[...]
    m_b = jnp.maximum(jnp.broadcast_to(m, (N_Q_HEADS, ATT_TQ, 128)), sink)
    p = jnp.concatenate([jnp.exp(t - m_b) for t in tiles], axis=2)
    p_sink = jnp.exp(sink - m_b)
    lo_k = lax.broadcasted_iota(jnp.int32, (ATT_KEYS, 128), 1) < HEAD_DIM
    v_f = vw.astype(F32)
    v_lo, v_hi = jnp.where(lo_k, v_f, 1.0).astype(BF16), jnp.where(lo_k, 1.0, v_f).astype(BF16)
    pb = p.astype(BF16).reshape(N_Q_HEADS * ATT_TQ, ATT_KEYS)
    half = 4 * ATT_TQ
    r = jnp.concatenate([_dot(pb[:half], v_lo), _dot(pb[half:], v_hi)], axis=0).reshape(N_Q_HEADS, ATT_TQ, 128)
    return kw, vw, qall, p, p_sink, r


def _attn_fwd(q_stack, k, v, sink128, bias):
    seq = k.shape[0]

    def body(q_ref, k_ref, v_ref, sink_ref, bias_ref, o_ref):
        start = _attn_window(pl.program_id(0), seq)
        _, _, _, _, p_sink, r = _attn_softmax(q_ref, k_ref, v_ref, sink_ref, bias_ref, start)
        out = r / (pltpu.roll(r, HEAD_DIM, 2) + p_sink)
        lo = lax.broadcasted_iota(jnp.int32, (ATT_TQ, 128), 1) < HEAD_DIM
        for j in range(4):
            o_ref[:, 128 * j:128 * (j + 1)] = jnp.where(lo, out[j], out[4 + j])

    full = lambda w: pl.BlockSpec((seq, w), lambda i: (0, 0))
    return pl.pallas_call(
        body, name="attn_fwd", grid=(seq // ATT_TQ,),
        in_specs=[pl.BlockSpec((8, ATT_TQ, 128), lambda i: (0, i, 0)), full(128), full(128),
                  pl.BlockSpec((N_Q_HEADS, 1, 128), lambda i: (0, 0, 0)), _attn_bias_spec(seq // ATT_TQ)],
        out_specs=pl.BlockSpec((ATT_TQ, 512), lambda i: (i, 0)),
        out_shape=jax.ShapeDtypeStruct((seq, 512), F32),
        compiler_params=_cparams(("arbitrary",)),
    )(q_stack, k, v, sink128, bias)


def _attn_bwd(q_stack, k, v, sink128, bias, d_o, pieces):
    seq = k.shape[0]
    steps = seq // ATT_TQ
    n_p = len(pieces)

    def body(*refs):
        q_ref, k_ref, v_ref, sink_ref, bias_ref, do_ref = refs[:6]
        piece_refs = refs[6:6 + n_p]
        dq_ref, dk_ref, dv_ref, dsink_ref = refs[6 + n_p:10 + n_p]
        reduced_refs = refs[10 + n_p:10 + 2 * n_p]
        sink_acc = refs[10 + 2 * n_p]
        i = pl.program_id(0)
        if n_p:
            landing_refs = refs[11 + 2 * n_p:11 + 3 * n_p]
            scratch = refs[11 + 3 * n_p:]
            begin, exchange, combine, finish = _reduce_phases(
                piece_refs, landing_refs, scratch[:n_p], scratch[n_p:2 * n_p], scratch[2 * n_p:3 * n_p],
                *scratch[3 * n_p:], [True] * n_p, gather_last=False)
            pl.when(i == 0)(begin)
            pl.when(i == min(4, steps - 1))(exchange)
            pl.when(i == (3 * steps) // 4)(combine)

        @pl.when(i == 0)
        def _():
            dk_ref[...] = jnp.zeros_like(dk_ref)
            dv_ref[...] = jnp.zeros_like(dv_ref)
            sink_acc[...] = jnp.zeros_like(sink_acc)

        start = _attn_window(i, seq)
        kw, vw, qall, p, p_sink, r = _attn_softmax(q_ref, k_ref, v_ref, sink_ref, bias_ref, start)
        lo = lax.broadcasted_iota(jnp.int32, (ATT_TQ, 128), 1) < HEAD_DIM
        lo3 = lo[None]
        grp0 = lax.broadcasted_iota(jnp.int32, (N_Q_HEADS, ATT_TQ, 128), 0) < 4
        val = grp0 == lo3
        swapped = pltpu.roll(r, HEAD_DIM, 2)
        inv = 1.0 / (jnp.where(val, swapped, r) + p_sink)
        d_o_blk = do_ref[...]
        do3 = jnp.where(val, jnp.concatenate([d_o_blk[None, :, 128 * j:128 * (j + 1)] for j in range(4)] * 2, axis=0), 0.0)
        t = (do3 * r).reshape(N_Q_HEADS * ATT_TQ, 128)
        t_hi = t.astype(BF16)
        t_lo = (t - t_hi.astype(F32)).astype(BF16)
        ones = jnp.ones((128, 128), BF16)
        delta = (_dot(t_hi, ones) + _dot(t_lo, ones)).reshape(N_Q_HEADS, ATT_TQ, 128) * inv
        sink_acc[...] += -(p_sink * inv) * delta
        do_all = do3.astype(BF16).reshape(N_Q_HEADS * ATT_TQ, 128)
        dp = _dot_nt(do_all, vw).reshape(N_Q_HEADS, ATT_TQ, ATT_KEYS)
        probs, ds = [], []
        for tl in range(ATT_KEYS // 128):
            cols = slice(128 * tl, 128 * (tl + 1))
            probs_t = p[:, :, cols] * inv
            probs.append(probs_t.astype(BF16))
            ds.append((probs_t * (dp[:, :, cols] - delta)).astype(BF16))
        probs_all = jnp.concatenate(probs, axis=2).reshape(N_Q_HEADS * ATT_TQ, ATT_KEYS)
        ds_all = jnp.concatenate(ds, axis=2).reshape(N_Q_HEADS * ATT_TQ, ATT_KEYS)
        scale = HEAD_DIM ** -0.5
        dq_all = (_dot(ds_all, kw) * scale).reshape(N_Q_HEADS, ATT_TQ, 128)
        for j in range(4):
            dq_ref[:, 128 * j:128 * (j + 1)] = jnp.where(lo, dq_all[j], dq_all[4 + j])
        dk_ref[pl.ds(start, ATT_KEYS), :] += _dot_tn(ds_all, qall) * scale
        dv_ref[pl.ds(start, ATT_KEYS), :] += _dot_tn(probs_all, do_all)

        @pl.when(i == steps - 1)
        def _():
            dsink_ref[...] = jnp.sum(sink_acc[...], axis=1)

        if n_p:
            @pl.when(i == steps - 1)
            def _():
                finish()
                for a in range(n_p):
                    reduced_refs[a][...] = landing_refs[a][...]

    full = lambda w: pl.BlockSpec((seq, w), lambda i: (0, 0))
    vmem = pl.BlockSpec(memory_space=pltpu.VMEM)
    return pl.pallas_call(
        body, name="attn_bwd", grid=(steps,),
        in_specs=[pl.BlockSpec((8, ATT_TQ, 128), lambda i: (0, i, 0)), full(128), full(128),
                  pl.BlockSpec((N_Q_HEADS, 1, 128), lambda i: (0, 0, 0)),
                  _attn_bias_spec(steps), pl.BlockSpec((ATT_TQ, 512), lambda i: (i, 0))] + [vmem] * n_p,
        out_specs=[pl.BlockSpec((ATT_TQ, 512), lambda i: (i, 0)), full(128), full(128),
                   pl.BlockSpec((N_Q_HEADS, 128), lambda i: (0, 0))] + [vmem] * n_p,
        out_shape=[jax.ShapeDtypeStruct((seq, 512), F32), jax.ShapeDtypeStruct((seq, 128), F32),
                   jax.ShapeDtypeStruct((seq, 128), F32), jax.ShapeDtypeStruct((N_Q_HEADS, 128), F32)]
        + [jax.ShapeDtypeStruct(p.shape[1:], F32) for p in pieces],
        scratch_shapes=[pltpu.VMEM((N_Q_HEADS, ATT_TQ, 128), F32)] + [pltpu.VMEM(p.shape[1:], F32) for p in pieces]
        + (_reduce_scratch([p.shape for p in pieces], [True] * n_p) if n_p else []),
        compiler_params=_cparams(("arbitrary",)),
    )(q_stack, k, v, sink128, bias, d_o, *pieces)


def _permute_rows(dst_ref, src_ref, sub_len):
    for k in range(N_SLAB):
        for j in range(sub_len):
            dst_ref[k, 8 * j:8 * (j + 1), :] = src_ref.at[k][pl.ds(j, SUBSEG, stride=sub_len), :]


def _unpermute_rows(dst_ref, src_ref, sub_len):
    for k in range(N_SLAB):
        for s in range(SUBSEG):
            dst_ref[k, s * sub_len:(s + 1) * sub_len, :] = src_ref.at[k][pl.ds(s, sub_len, stride=SUBSEG), :]


def _scan_chunk(br_ref, bi_ref, lr_row, li_row, init, cols, *, sub_len, reverse, store):
    lr = jnp.broadcast_to(lr_row[:, cols], (SUBSEG, SCAN_LANES))
    li = jnp.broadcast_to(li_row[:, cols], (SUBSEG, SCAN_LANES))
    if init is None:
        sr = si = jnp.zeros((SUBSEG, SCAN_LANES), F32)
    else:
        sr, si = init
    for jj in range(sub_len):
        rows = slice(SUBSEG * ((sub_len - 1 - jj) if reverse else jj), SUBSEG * (((sub_len - 1 - jj) if reverse else jj) + 1))
        sr, si = lr * sr - li * si + br_ref[rows, cols], lr * si + li * sr + bi_ref[rows, cols]
        if store:
            br_ref[rows, cols] = sr
            bi_ref[rows, cols] = si
    return sr, si


def _resolve_chunk(z, carry_refs, start_refs, pr_row, pi_row, cols, *, reverse):
    cr, ci = carry_refs[0][0:1, cols], carry_refs[1][0:1, cols]
    pr, pi = pr_row[:, cols], pi_row[:, cols]
    for s in (range(SUBSEG - 1, -1, -1) if reverse else range(SUBSEG)):
        start_refs[0][s:s + 1, cols] = cr
        start_refs[1][s:s + 1, cols] = ci
        cr, ci = pr * cr - pi * ci + z[0][s:s + 1, :], pr * ci + pi * cr + z[1][s:s + 1, :]
    carry_refs[0][0:1, cols] = cr
    carry_refs[1][0:1, cols] = ci


def _param_specs(direction):
    row = lambda q: pl.BlockSpec((None, None, 1, STATE_W), lambda i: (q, direction, 0, 0))
    wide = lambda q: pl.BlockSpec((None, None, N_SLAB, SLAB_IN, SLAB_ST), lambda i: (q, direction, 0, 0, 0))
    tall = lambda q: pl.BlockSpec((None, None, N_SLAB, SLAB_ST, SLAB_IN), lambda i: (q, direction, 0, 0, 0))
    return [row(q) for q in range(4)], [wide(0), wide(1)], [tall(0), tall(1)]


def _ssm_fwd(u, lam, bb, cb, *, direction, tb, name):
    reverse = direction == 1
    seq = u.shape[1]
    nblk = seq // tb
    sub_len = tb // SUBSEG

    def body(u_ref, lr_ref, li_ref, pr_ref, pi_ref, bbr_ref, bbi_ref, cbr_ref, cbi_ref,
             y_ref, sr_ref, si_ref, xr, xi, up, yp, car, cai):
        @pl.when(pl.program_id(0) == 0)
        def _():
            car[...] = jnp.zeros_like(car)
            cai[...] = jnp.zeros_like(cai)

        _permute_rows(up, u_ref, sub_len)
        lr, li, pr, pi = lr_ref[...], li_ref[...], pr_ref[...], pi_ref[...]
        chunk = lambda k: slice(k * SLAB_ST, (k + 1) * SLAB_ST)

        def drive(k):
            ub = up[k].astype(BF16)
            xr[:, chunk(k)] = _dot(ub, bbr_ref[k])
            xi[:, chunk(k)] = _dot(ub, bbi_ref[k])

        def scan(k):
            z = _scan_chunk(xr, xi, lr, li, None, chunk(k), sub_len=sub_len, reverse=reverse, store=False)
            _resolve_chunk(z, (car, cai), (sr_ref, si_ref), pr, pi, chunk(k), reverse=reverse)
            _scan_chunk(xr, xi, lr, li, (sr_ref[:, chunk(k)], si_ref[:, chunk(k)]), chunk(k),
                        sub_len=sub_len, reverse=reverse, store=True)

        def read_out(k):
            yp[k] = _dot(xr[:, chunk(k)].astype(BF16), cbr_ref[k]) - _dot(xi[:, chunk(k)].astype(BF16), cbi_ref[k])

        drive(0)
        for k in range(N_SLAB):
            if k + 1 < N_SLAB:
                drive(k + 1)
            scan(k)
            if k > 0:
                read_out(k - 1)
        read_out(N_SLAB - 1)
        _unpermute_rows(y_ref, yp, sub_len)

    blk = (lambda i: nblk - 1 - i) if reverse else (lambda i: i)
    rows, wide, tall = _param_specs(direction)
    tok = pl.BlockSpec((N_SLAB, tb, SLAB_IN), lambda i: (0, blk(i), 0))
    start_spec = pl.BlockSpec((None, SUBSEG, STATE_W), lambda i: (blk(i), 0, 0))
    return pl.pallas_call(
        body, name=name, grid=(nblk,),
        in_specs=[tok] + rows + wide + tall,
        out_specs=[tok, start_spec, start_spec],
        out_shape=[jax.ShapeDtypeStruct((N_SLAB, seq, SLAB_IN), F32), jax.ShapeDtypeStruct((nblk, SUBSEG, STATE_W), F32),
                   jax.ShapeDtypeStruct((nblk, SUBSEG, STATE_W), F32)],
        scratch_shapes=[pltpu.VMEM((tb, STATE_W), F32), pltpu.VMEM((tb, STATE_W), F32),
                        pltpu.VMEM((N_SLAB, tb, SLAB_IN), F32), pltpu.VMEM((N_SLAB, tb, SLAB_IN), F32),
                        pltpu.VMEM((SUBSEG, STATE_W), F32), pltpu.VMEM((SUBSEG, STATE_W), F32)],
        compiler_params=_cparams(("arbitrary",)),
    )(u, lam, lam, lam, lam, bb, bb, cb, cb)


def _ssm_bwd(u, dy, starts, lam, bb, bbt, cb_t, *, direction, tb, name):
    reverse = direction == 1
    seq = u.shape[1]
    nblk = seq // tb
    sub_len = tb // SUBSEG

    def body(u_ref, dy_ref, sr_ref, si_ref, lr_ref, li_ref, pr_ref, pi_ref, bbr_ref, bbi_ref, btr_ref, bti_ref,
             ctr_ref, cti_ref, du_ref, gb_ref, gc_ref, dl_ref,
             xr, xi, gr, gi, up, dyp, dup, gsr, gsi, car, cai):
        gbr_ref, gbi_ref = gb_ref.at[0], gb_ref.at[1]
        gcr_ref, gci_ref = gc_ref.at[0], gc_ref.at[1]
        dlr_ref, dli_ref = dl_ref.at[0], dl_ref.at[1]

        @pl.when(pl.program_id(0) == 0)
        def _():
            for ref in (car, cai, gbr_ref, gbi_ref, gcr_ref, gci_ref, dlr_ref, dli_ref):
                ref[...] = jnp.zeros_like(ref)

        _permute_rows(up, u_ref, sub_len)
        _permute_rows(dyp, dy_ref, sub_len)
        lr, li, pr, pi = lr_ref[...], li_ref[...], pr_ref[...], pi_ref[...]
        nli, npi = -li, -pi
        chunk = lambda k: slice(k * SLAB_ST, (k + 1) * SLAB_ST)

        def drive(k):
            ub = up[k].astype(BF16)
            xr[:, chunk(k)] = _dot(ub, bbr_ref[k])
            xi[:, chunk(k)] = _dot(ub, bbi_ref[k])
            dyb = dyp[k].astype(BF16)
            gr[:, chunk(k)] = _dot(dyb, ctr_ref[k])
            gi[:, chunk(k)] = -_dot(dyb, cti_ref[k])

        def scan_x(k):
            _scan_chunk(xr, xi, lr, li, (sr_ref[:, chunk(k)], si_ref[:, chunk(k)]), chunk(k),
                        sub_len=sub_len, reverse=reverse, store=True)

        def grad_c(k):
            dyb = dyp[k].astype(BF16)
            gcr_ref[k] += _dot_tn(xr[:, chunk(k)].astype(BF16), dyb)
            gci_ref[k] -= _dot_tn(xi[:, chunk(k)].astype(BF16), dyb)

        def scan_g(k):
            z = _scan_chunk(gr, gi, lr, nli, None, chunk(k), sub_len=sub_len, reverse=not reverse, store=False)
            _resolve_chunk(z, (car, cai), (gsr, gsi), pr, npi, chunk(k), reverse=not reverse)
            _scan_chunk(gr, gi, lr, nli, (gsr[:, chunk(k)], gsi[:, chunk(k)]), chunk(k),
                        sub_len=sub_len, reverse=not reverse, store=True)

        def grad_b_du(k):
            ub = up[k].astype(BF16)
            grb, gib = gr[:, chunk(k)].astype(BF16), gi[:, chunk(k)].astype(BF16)
            gbr_ref[k] += _dot_tn(ub, grb)
            gbi_ref[k] += _dot_tn(ub, gib)
            dup[k] = _dot(grb, btr_ref[k]) + _dot(gib, bti_ref[k])

        def grad_lambda(k):
            cols = chunk(k)
            acc_r, acc_i = dlr_ref[:, cols], dli_ref[:, cols]
            for jj in range(sub_len):
                prev = jj + 1 if reverse else jj - 1
                if 0 <= prev < sub_len:
                    x_r, x_i = xr[SUBSEG * prev:SUBSEG * (prev + 1), cols], xi[SUBSEG * prev:SUBSEG * (prev + 1), cols]
                else:
                    x_r, x_i = sr_ref[:, cols], si_ref[:, cols]
                g_r, g_i = gr[SUBSEG * jj:SUBSEG * (jj + 1), cols], gi[SUBSEG * jj:SUBSEG * (jj + 1), cols]
                acc_r = acc_r + (g_r * x_r + g_i * x_i)
                acc_i = acc_i + (g_i * x_r - g_r * x_i)
            dlr_ref[:, cols] = acc_r
            dli_ref[:, cols] = acc_i

        drive(0)
        for k in range(N_SLAB):
            if k + 1 < N_SLAB:
                drive(k + 1)
            scan_x(k)
            grad_c(k)
            scan_g(k)
            grad_b_du(k)
            grad_lambda(k)
        _unpermute_rows(du_ref, dup, sub_len)

    blk = (lambda i: i) if reverse else (lambda i: nblk - 1 - i)
    rows, wide, tall = _param_specs(direction)
    tok = pl.BlockSpec((N_SLAB, tb, SLAB_IN), lambda i: (0, blk(i), 0))
    start_spec = pl.BlockSpec((None, SUBSEG, STATE_W), lambda i: (blk(i), 0, 0))
    gb_shape, gc_shape, dl_shape = (2, N_SLAB, SLAB_IN, SLAB_ST), (2, N_SLAB, SLAB_ST, SLAB_IN), (2, SUBSEG, STATE_W)
    whole = lambda shape: pl.BlockSpec(shape, lambda i: (0,) * len(shape))
    big = lambda: pltpu.VMEM((tb, STATE_W), F32)
    slabs = lambda: pltpu.VMEM((N_SLAB, tb, SLAB_IN), F32)
    tile = lambda: pltpu.VMEM((SUBSEG, STATE_W), F32)
    return pl.pallas_call(
        body, name=name, grid=(nblk,),
        in_specs=[tok, tok, start_spec, start_spec] + rows + wide + tall + wide,
        out_specs=[tok, whole(gb_shape), whole(gc_shape), whole(dl_shape)],
        out_shape=[jax.ShapeDtypeStruct((N_SLAB, seq, SLAB_IN), F32), jax.ShapeDtypeStruct(gb_shape, F32),
                   jax.ShapeDtypeStruct(gc_shape, F32), jax.ShapeDtypeStruct(dl_shape, F32)],
        scratch_shapes=[big(), big(), big(), big(), slabs(), slabs(), slabs(), tile(), tile(), tile(), tile()],
        compiler_params=_cparams(("arbitrary",)),
    )(u, dy, *starts, lam, lam, lam, lam, bb, bb, bbt, bbt, cb_t, cb_t)


GELU_C = math.sqrt(2.0 / math.pi)
GELU_K = 0.044715
MID_ROW_GROUPS = 1


def _mid(o, za, u, y_f, y_b, zs, x, target, ssm_d, w_glu, b_glu, g_attn, g_ssm, w_out, ln_g, ln_b, tb):
    seq = x.shape[0]

    def body(o_ref, za_ref, u_ref, yf_ref, yb_ref, zs_ref, x_ref, t_ref, d_ref, wg_ref, bg_ref, ga_ref, gs_ref,
             wo_ref, lg_ref, lb_ref,
             loss_ref, do_ref, dza_ref, dyl_ref, dzs_ref, dpre_ref, gwo_ref, gwg_ref, vec_ref, wop):
        @pl.when(pl.program_id(0) == 0)
        def _():
            for ref in (loss_ref, gwo_ref, gwg_ref, vec_ref):
                ref[...] = jnp.zeros_like(ref)
            for nat, par in _pair_blocks(0):
                wop[par, :] = wo_ref[nat, :]
            wop[D_ATTN:, :] = wo_ref[D_ATTN:, :]

        def rows_of(rs):
            o, za = o_ref[rs, :], za_ref[rs, :]
            sig_a = _sigmoid(za)
            silu_a = za * sig_a
            ya = o * silu_a
            r_a = lax.rsqrt(jnp.mean(ya * ya, axis=1, keepdims=True) + NORM_EPS)
            n_a = ya * r_a
            g_a = ga_ref[...]
            unslab = lambda ref: jnp.concatenate([ref[k, rs, :] for k in range(N_SLAB)], axis=1)
            u_blk, zs = unslab(u_ref), zs_ref[rs, :]
            d_row = d_ref[...]
            ylin = d_row * u_blk + unslab(yf_ref) + unslab(yb_ref)
            inner = GELU_C * (ylin + GELU_K * ylin * ylin * ylin)
            th = jnp.tanh(inner)
            gl = 0.5 * ylin * (1.0 + th)
            glb = gl.astype(BF16)
            gate = _dot(glb, wg_ref[...])
            yield
            sg = _sigmoid(gate + bg_ref[...])
            y2 = gl * sg
            sig_s = _sigmoid(zs)
            silu_s = zs * sig_s
            ys = y2 * silu_s
            r_s = lax.rsqrt(jnp.mean(ys * ys, axis=1, keepdims=True) + NORM_EPS)
            n_s = ys * r_s
            g_s = gs_ref[...]
            mixed = jnp.concatenate([n_a * g_a, n_s * g_s], axis=1).astype(BF16)
            out = _dot(mixed, wop[...])
            yield
            pre = ALPHA * x_ref[rs, :] + out
            mu = jnp.mean(pre, axis=1, keepdims=True)
            cen = pre - mu
            rstd = lax.rsqrt(jnp.mean(cen * cen, axis=1, keepdims=True) + NORM_EPS)
            hhat = cen * rstd
            ln_g = lg_ref[...]
            err = hhat * ln_g + lb_ref[...] - t_ref[rs, :]
            loss_ref[...] += 0.5 * jnp.sum(jnp.mean(err * err, axis=1, keepdims=True))

            dh = err * (1.0 / D_MODEL)
            vec_ref[0:1, :] += jnp.sum(dh * hhat, axis=0, keepdims=True)
            vec_ref[1:2, :] += jnp.sum(dh, axis=0, keepdims=True)
            dhh = dh * ln_g
            dpre = rstd * (dhh - jnp.mean(dhh, axis=1, keepdims=True)
                           - hhat * jnp.mean(dhh * hhat, axis=1, keepdims=True))
            dpre_ref[rs, :] = dpre
            dpb = dpre.astype(BF16)
            for j in range(4):
                g_pair = _dot_tn(mixed[:, 128 * j:128 * (j + 1)], dpb)
                for g in range(2):
                    nat = HEAD_DIM * (4 * g + j)
                    gwo_ref[nat:nat + HEAD_DIM, :] += g_pair[HEAD_DIM * g:HEAD_DIM * (g + 1), :]
            gwo_ref[D_ATTN:, :] += _dot_tn(mixed[:, D_ATTN:], dpb)
            dmix = _dot_nt(dpb, wop[...])
            yield
            dna = dmix[:, :D_ATTN]
            vec_ref[2:3, 0:D_ATTN] += jnp.sum(dna * n_a, axis=0, keepdims=True)
            dna = dna * g_a
            dya = r_a * (dna - n_a * jnp.mean(dna * n_a, axis=1, keepdims=True))
            do_ref[rs, :] = dya * silu_a
            dza_ref[rs, :] = dya * o * (sig_a * (1.0 + za * (1.0 - sig_a)))
            dns = dmix[:, D_ATTN:]
            vec_ref[2:3, D_ATTN:] += jnp.sum(dns * n_s, axis=0, keepdims=True)
            dns = dns * g_s
            dys = r_s * (dns - n_s * jnp.mean(dns * n_s, axis=1, keepdims=True))
            dzs_ref[rs, :] = dys * y2 * (sig_s * (1.0 + zs * (1.0 - sig_s)))
            dy2 = dys * silu_s
            da = dy2 * gl * sg * (1.0 - sg)
            vec_ref[3:4, D_SSM:] += jnp.sum(da, axis=0, keepdims=True)
            dab = da.astype(BF16)
            gwg_ref[...] += _dot_tn(glb, dab)
            dgl_mm = _dot_nt(dab, wg_ref[...])
            yield
            dgl = dy2 * sg + dgl_mm
            dylin = dgl * (0.5 * (1.0 + th)
                           + 0.5 * ylin * (1.0 - th * th) * GELU_C * (1.0 + 3.0 * GELU_K * ylin * ylin))
            for k in range(N_SLAB):
                dyl_ref[k, rs, :] = dylin[:, k * SLAB_IN:(k + 1) * SLAB_IN]
            vec_ref[3:4, 0:D_SSM] += jnp.sum(dylin * u_blk, axis=0, keepdims=True)
            yield

        groups = [rows_of(slice(r0, r0 + tb // MID_ROW_GROUPS)) for r0 in range(0, tb, tb // MID_ROW_GROUPS)]
        for _ in range(5):
            for gen in groups:
                next(gen)

    tok = lambda w: pl.BlockSpec((tb, w), lambda i: (i, 0))
    slab = pl.BlockSpec((N_SLAB, tb, SLAB_IN), lambda i: (0, i, 0))
    const = lambda r, c: pl.BlockSpec((r, c), lambda i: (0, 0), pipeline_mode=pl.Buffered(1))
    tok_shape = jax.ShapeDtypeStruct((seq, 512), F32)
    return pl.pallas_call(
        body, name="mid", grid=(seq // tb,),
        in_specs=[tok(512), tok(512), slab, slab, slab, tok(512), tok(1024), tok(1024),
                  const(1, 512), const(512, 512), const(1, 512), const(1, 512), const(1, 512),
                  const(1024, 1024), const(1, 1024), const(1, 1024)],
        out_specs=[const(8, 128), tok(512), tok(512), slab, tok(512), tok(1024),
                   const(1024, 1024), const(512, 512), const(8, 1024)],
        out_shape=[jax.ShapeDtypeStruct((8, 128), F32), tok_shape, tok_shape,
                   jax.ShapeDtypeStruct((N_SLAB, seq, SLAB_IN), F32), tok_shape,
                   jax.ShapeDtypeStruct((seq, 1024), F32), jax.ShapeDtypeStruct((1024, 1024), F32),
                   jax.ShapeDtypeStruct((512, 512), F32), jax.ShapeDtypeStruct((8, 1024), F32)],
        scratch_shapes=[pltpu.VMEM((D_MODEL, D_MODEL), BF16)],
        compiler_params=_cparams(("arbitrary",)),
    )(o, za, u, y_f, y_b, zs, x, target, ssm_d, w_glu, b_glu, g_attn, g_ssm, w_out, ln_g, ln_b)


def _ride_shapes(pieces, narrow, gather_last):
    outs = [p.shape if (gather_last and a == len(pieces) - 1) else p.shape[1:] for a, p in enumerate(pieces)]
    return outs, [pltpu.VMEM(s, F32) for s in outs] + _reduce_scratch([p.shape for p in pieces], narrow)


def _ride_phases(piece_refs, out_refs, scratch_refs, narrow, gather_last):
    n = len(piece_refs)
    landing, rest = scratch_refs[:n], scratch_refs[n:]
    begin, exchange, combine, finish = _reduce_phases(piece_refs, landing, rest[:n], rest[n:2 * n], rest[2 * n:3 * n],
                                                      *rest[3 * n:], narrow, gather_last)

    def end():
        finish()
        for a in range(n):
            out_refs[a][...] = landing[a][...]

    return begin, exchange, combine, end


def _dproj_block(dq_ref, dk_ref, dv_ref, dza_ref, duf_ref, dub_ref, dyl_ref, dzs_ref, d_ref, hi_ref, lo_ref, tb):
    cos, sin = _rope_block(hi_ref, lo_ref, pl.program_id(0) * (tb // ROPE_GROUP), tb // ROPE_GROUP)
    lo = lax.broadcasted_iota(jnp.int32, (tb, 128), 1) < HEAD_DIM

    def unrope(t):
        return t * cos + _rotate_half_unsigned(t * sin)

    def natural(pairs):
        swapped = [pltpu.roll(t, HEAD_DIM, 1) for t in pairs]
        return [jnp.where(lo, pairs[0], swapped[1]), jnp.where(lo, pairs[2], swapped[3]),
                jnp.where(lo, swapped[0], pairs[1]), jnp.where(lo, swapped[2], pairs[3])]

    dq_rot, dza = dq_ref[...], dza_ref[...]
    pieces = natural([unrope(dq_rot[:, 128 * j:128 * (j + 1)]) for j in range(4)])
    d_row = d_ref[...]
    pieces += [unrope(dk_ref[...]), dv_ref[...]] + natural([dza[:, 128 * j:128 * (j + 1)] for j in range(4)])
    pieces += [duf_ref[k] + dub_ref[k] + d_row[:, k * SLAB_IN:(k + 1) * SLAB_IN] * dyl_ref[k] for k in range(N_SLAB)]
    pieces += [dzs_ref[...]]
    return jnp.concatenate(pieces, axis=1).astype(BF16)


def _dproj_specs(tb, rope_hi, rope_lo):
    tok = lambda w: pl.BlockSpec((tb, w), lambda i: (i, 0))
    slab = pl.BlockSpec((N_SLAB, tb, SLAB_IN), lambda i: (0, i, 0))
    table = lambda t: pl.BlockSpec(t.shape, lambda i: (0, 0, 0))
    return [tok(512), tok(128), tok(128), tok(512), slab, slab, slab, tok(512), pl.BlockSpec((1, 512), lambda i: (0, 0)),
            table(rope_hi), table(rope_lo)]


N_DPROJ = 11
GW_ROWS = 768


def _proj_bwd_w(x, dproj_args, rope_hi, rope_lo, pieces, tb):
    seq = x.shape[0]
    steps = seq // tb
    n_p = len(pieces)
    narrow = [False] * n_p

    def body(*refs):
        x_ref, grads = refs[0], refs[1:1 + N_DPROJ]
        piece_refs = refs[1 + N_DPROJ:1 + N_DPROJ + n_p]
        gw_ref = refs[1 + N_DPROJ + n_p]
        out_refs = refs[2 + N_DPROJ + n_p:2 + N_DPROJ + 2 * n_p]
        step = pl.program_id(0)
        if n_p:
            begin, exchange, combine, end = _ride_phases(piece_refs, out_refs, refs[2 + N_DPROJ + 2 * n_p:], narrow, True)
            pl.when(step == 0)(begin)
            pl.when(step == min(1, steps - 1))(exchange)
            pl.when(step == steps // 2)(combine)

        @pl.when(step == 0)
        def _():
            gw_ref[...] = jnp.zeros_like(gw_ref)

        dproj = _dproj_block(*grads, tb)
        xb = x_ref[...].astype(BF16)
        for r0 in range(0, D_IN_PROJ, GW_ROWS):
            gw_ref[r0:r0 + GW_ROWS, :] += _dot_tn(dproj[:, r0:r0 + GW_ROWS], xb)
        if n_p:
            pl.when(step == steps - 1)(end)

    vmem = pl.BlockSpec(memory_space=pltpu.VMEM)
    whole = pl.BlockSpec((D_IN_PROJ, D_MODEL), lambda i: (0, 0), pipeline_mode=pl.Buffered(1))
    ride_outs, ride_scratch = _ride_shapes(pieces, narrow, True) if n_p else ([], [])
    return pl.pallas_call(
        body, name="proj_bwd_w", grid=(steps,),
        in_specs=[pl.BlockSpec((tb, D_MODEL), lambda i: (i, 0))] + _dproj_specs(tb, rope_hi, rope_lo) + [vmem] * n_p,
        out_specs=[whole] + [vmem] * n_p,
        out_shape=[jax.ShapeDtypeStruct((D_IN_PROJ, D_MODEL), F32)] + [jax.ShapeDtypeStruct(s, F32) for s in ride_outs],
        scratch_shapes=ride_scratch,
        compiler_params=_cparams(("arbitrary",)),
    )(x, *dproj_args, rope_hi, rope_lo, *pieces)


def _proj_bwd_x(dproj_args, rope_hi, rope_lo, dpre, wt, pieces, tb):
    seq = dpre.shape[0]
    steps = seq // tb
    n_p = len(pieces)
    narrow = [True] * n_p

    def body(*refs):
        grads = refs[:N_DPROJ]
        dpre_ref, wt_ref = refs[N_DPROJ:N_DPROJ + 2]
        piece_refs = refs[N_DPROJ + 2:N_DPROJ + 2 + n_p]
        gx_ref = refs[N_DPROJ + 2 + n_p]
        out_refs = refs[N_DPROJ + 3 + n_p:N_DPROJ + 3 + 2 * n_p]
        step = pl.program_id(0)
        if n_p:
            begin, exchange, combine, end = _ride_phases(piece_refs, out_refs, refs[N_DPROJ + 3 + 2 * n_p:], narrow, False)
            pl.when(step == 0)(begin)
            pl.when(step == min(1, steps - 1))(exchange)
            pl.when(step == steps - 1)(combine)

        dproj = _dproj_block(*grads, tb)
        gx_ref[...] = ALPHA * dpre_ref[...] + _dot(dproj, wt_ref[...])
        if n_p:
            pl.when(step == steps - 1)(end)

    vmem = pl.BlockSpec(memory_space=pltpu.VMEM)
    whole = pl.BlockSpec((D_IN_PROJ, D_MODEL), lambda i: (0, 0), pipeline_mode=pl.Buffered(1))
    ride_outs, ride_scratch = _ride_shapes(pieces, narrow, False) if n_p else ([], [])
    return pl.pallas_call(
        body, name="proj_bwd_x", grid=(steps,),
        in_specs=_dproj_specs(tb, rope_hi, rope_lo) + [pl.BlockSpec((tb, D_MODEL), lambda i: (i, 0)), whole] + [vmem] * n_p,
        out_specs=[pl.BlockSpec((tb, D_MODEL), lambda i: (i, 0))] + [vmem] * n_p,
        out_shape=[jax.ShapeDtypeStruct((seq, D_MODEL), F32)] + [jax.ShapeDtypeStruct(s, F32) for s in ride_outs],
        scratch_shapes=ride_scratch,
        compiler_params=pltpu.CompilerParams(dimension_semantics=("arbitrary",), vmem_limit_bytes=PROJ_BWD_X_VMEM),
    )(*dproj_args, rope_hi, rope_lo, dpre, wt, *pieces)


def _adamw(w, g, m, v, name):
    rows, cols = w.shape
    tb = rows
    while tb * cols * 4 > ADAMW_BLOCK_BYTES and tb % 16 == 0:
        tb //= 2

    def body(w_ref, g_ref, m_ref, v_ref, d_ref, nm_ref, nv_ref):
        _adamw_update(w_ref, g_ref, m_ref, v_ref, d_ref, nm_ref, nv_ref)

    spec = pl.BlockSpec((tb, cols), lambda i: (i, 0))
    return pl.pallas_call(
        body, name=name, grid=(rows // tb,), in_specs=[spec] * 4, out_specs=[spec] * 3,
        out_shape=[jax.ShapeDtypeStruct((rows, cols), F32)] * 3,
        compiler_params=_cparams(("arbitrary",)),
    )(w, g, m, v)


def _adamw_update(w_ref, g_ref, m_ref, v_ref, d_ref, nm_ref, nv_ref):
    g_blk = g_ref[...]
    m_new = ADAM_B1 * m_ref[...] + (1.0 - ADAM_B1) * g_blk
    v_new = ADAM_B2 * v_ref[...] + (1.0 - ADAM_B2) * (g_blk * g_blk)
    m_hat = m_new / (1.0 - ADAM_B1 ** ADAM_STEP)
    v_hat = v_new / (1.0 - ADAM_B2 ** ADAM_STEP)
    d_ref[...] = -ADAM_LR * (m_hat / (jnp.sqrt(v_hat) + ADAM_EPS) + ADAM_WD * w_ref[...])
    nm_ref[...] = m_new
    nv_ref[...] = v_new


def _adamw_many(groups, name):
    n = len(groups)

    def body(*refs):
        for p in range(n):
            _adamw_update(*refs[4 * p:4 * p + 4], *refs[4 * n + 3 * p:4 * n + 3 * p + 3])

    return pl.pallas_call(
        body, name=name,
        out_shape=[jax.ShapeDtypeStruct(grp[0].shape, F32) for grp in groups for _ in range(3)],
    )(*[a for grp in groups for a in grp])


_WEIGHTS = ["w_in", "attn_sink", "ssm_a_re", "ssm_a_im", "ssm_log_dt", "ssm_b_re", "ssm_b_im", "ssm_c_re", "ssm_c_im",
            "ssm_d", "w_glu", "b_glu", "norm_attn_g", "norm_ssm_g", "w_out", "ln_g", "ln_b"]
N_DG = N_DIR * N_GROUPS
BIG_ROWS = N_DG * SSM_CH * SSM_STATE // 128
TINY_ROWS = 64


def _pack_small_grads(g_bc, g_vec, g_ar, g_ai, g_dt, g_sink, loss):
    big = jnp.stack([t.reshape(BIG_ROWS, 128) for t in g_bc])
    row = lambda t: jnp.pad(t.reshape(1, -1), ((0, 0), (0, 128 - t.size)))
    tiny = jnp.concatenate([g_vec.reshape(64, 128), g_ar.reshape(32, 128), g_ai.reshape(32, 128), row(g_dt), row(g_sink),
                            row(loss), jnp.zeros((N_CHIPS * TINY_ROWS - 131, 128), F32)], axis=0)
    return jnp.concatenate([big, tiny.reshape(N_CHIPS, TINY_ROWS, 128)], axis=1)


def _unpack_small_grads(packed):
    big = packed[:, :BIG_ROWS].reshape(N_CHIPS, 2 * BIG_ROWS, SSM_STATE)
    tiny = packed[:, BIG_ROWS:].reshape(N_CHIPS * TINY_ROWS, 128)
    g_vec = tiny[0:64].reshape(8, 1024)
    return tiny[130, 0], {
        "ssm_b_re": big[0], "ssm_b_im": big[1], "ssm_c_re": big[2], "ssm_c_im": big[3],
        "ln_g": g_vec[0:1], "ln_b": g_vec[1:2],
        "norm_attn_g": _from_pair_order(g_vec[2:3, :D_ATTN]), "norm_ssm_g": g_vec[2:3, D_ATTN:],
        "ssm_d": g_vec[3:4, :D_SSM], "b_glu": g_vec[3:4, D_SSM:],
        "ssm_a_re": tiny[64:96].reshape(N_DG, SSM_STATE), "ssm_a_im": tiny[96:128].reshape(N_DG, SSM_STATE),
        "ssm_log_dt": tiny[128:129, :N_DG].reshape(N_DIR, N_GROUPS), "attn_sink": tiny[129:130, :N_Q_HEADS],
    }


def _small_view(name, t):
    if name in ("ssm_b_re", "ssm_b_im"):
        return jnp.swapaxes(t[0], 2, 3).reshape(N_DG * SSM_CH, SSM_STATE)
    if name in ("ssm_c_re", "ssm_c_im"):
        return t.reshape(N_DG * SSM_CH, SSM_STATE)
    if name in ("ssm_a_re", "ssm_a_im"):
        return t.reshape(N_DG, SSM_STATE)
    if name == "ssm_log_dt":
        return t.reshape(N_DIR, N_GROUPS)
    return t.reshape(1, -1)


def _small_unview(name, t, shape):
    if name in ("ssm_b_re", "ssm_b_im"):
        return jnp.swapaxes(t.reshape(N_DIR, N_GROUPS, SSM_CH, SSM_STATE), 2, 3).reshape(shape)
    return t.reshape(shape)


def kernel(x, w_in, attn_sink, ssm_a_re, ssm_a_im, ssm_log_dt, ssm_b_re, ssm_b_im, ssm_c_re, ssm_c_im, ssm_d, w_glu, b_glu, norm_attn_g, norm_ssm_g, w_out, ln_g, ln_b, loss_target, m_w_in, m_attn_sink, m_ssm_a_re, m_ssm_a_im, m_ssm_log_dt, m_ssm_b_re, m_ssm_b_im, m_ssm_c_re, m_ssm_c_im, m_ssm_d, m_w_glu, m_b_glu, m_norm_attn_g, m_norm_ssm_g, m_w_out, m_ln_g, m_ln_b, v_w_in, v_attn_sink, v_ssm_a_re, v_ssm_a_im, v_ssm_log_dt, v_ssm_b_re, v_ssm_b_im, v_ssm_c_re, v_ssm_c_im, v_ssm_d, v_w_glu, v_b_glu, v_norm_attn_g, v_norm_ssm_g, v_w_out, v_ln_g, v_ln_b):
    args = dict(locals())
    weights = {n: args[n] for n in _WEIGHTS}
    mom_m = {n: args["m_" + n] for n in _WEIGHTS}
    mom_v = {n: args["v_" + n] for n in _WEIGHTS}
    xs = x[0]
    target = loss_target[0]

    (wt_g,) = _all_gather_chips([w_in[0].T], BF16, "gather_weights")
    wt_full = wt_g.reshape(D_IN_PROJ, D_MODEL)

    g_x, r_wt, r_w_out, r_w_glu, g_small_all = _local_step(
        xs, target, wt_full, w_glu[0], w_out[0], attn_sink, ssm_a_re, ssm_a_im, ssm_log_dt, ssm_b_re, ssm_b_im,
        ssm_c_re, ssm_c_im, ssm_d, b_glu, norm_attn_g, norm_ssm_g, ln_g, ln_b, sharded=True)
    loss, small_grads = _unpack_small_grads(g_small_all)

    grads, deltas, new_m, new_v = {}, {}, {}, {}
    d_w, m_w, v_w = _adamw(w_in[0].T, r_wt, m_w_in[0].T, v_w_in[0].T, "adamw_w_in")
    grads["w_in"], deltas["w_in"], new_m["w_in"], new_v["w_in"] = r_wt.T[None], d_w.T[None], m_w.T[None], v_w.T[None]
    for n, g in (("w_out", r_w_out), ("w_glu", r_w_glu)):
        d_w, m_w, v_w = _adamw(weights[n][0], g, mom_m[n][0], mom_v[n][0], "adamw_" + n)
        grads[n], deltas[n], new_m[n], new_v[n] = g[None], d_w[None], m_w[None], v_w[None]
    names = sorted(small_grads)
    updates = _adamw_many([(_small_view(n, weights[n]), small_grads[n], _small_view(n, mom_m[n]), _small_view(n, mom_v[n]))
                           for n in names], "adamw_small")
    for i, n in enumerate(names):
        shape = weights[n].shape
        grads[n] = _small_unview(n, small_grads[n], shape)
        deltas[n], new_m[n], new_v[n] = (_small_unview(n, t, shape) for t in updates[3 * i:3 * i + 3])

    return (loss, g_x[None], *[grads[n] for n in _WEIGHTS], *[deltas[n] for n in _WEIGHTS],
            *[new_m[n] for n in _WEIGHTS], *[new_v[n] for n in _WEIGHTS])


def _local_step(xs, target, wt_full, w_glu_in, w_out_in, attn_sink, ssm_a_re, ssm_a_im, ssm_log_dt, ssm_b_re,
                ssm_b_im, ssm_c_re, ssm_c_im, ssm_d, b_glu, norm_attn_g, norm_ssm_g, ln_g, ln_b, sharded):
    seq = xs.shape[0]

    a_r, a_i = _small_view("ssm_a_re", ssm_a_re), _small_view("ssm_a_im", ssm_a_im)
    log_dt = ssm_log_dt.reshape(N_DG, 1)
    b_r, b_i = _small_view("ssm_b_re", ssm_b_re), _small_view("ssm_b_im", ssm_b_im)
    c_r, c_i = _small_view("ssm_c_re", ssm_c_re), _small_view("ssm_c_im", ssm_c_im)
    ssm_tb = min(SSM_BLOCK, seq)
    sub_len = ssm_tb // SUBSEG
    lam, bb, bbt, cb, cb_t = _ssm_params_fwd(a_r, a_i, log_dt, b_r, b_i, c_r, c_i, int(math.log2(sub_len)))
    lam = lam.reshape(4, N_DIR, 1, STATE_W)

    rope_hi, rope_lo = _rope_tables(seq)
    projected = _proj(xs, wt_full, rope_hi, rope_lo, [w_glu_in, w_out_in] if sharded else [], min(512, seq))
    q_stack, k_rot, v_bf, z_attn, u, z_ssm = projected[:6]
    if sharded:
        w_glu_full, w_out_full = projected[6].reshape(D_SSM, D_SSM), projected[7].reshape(D_MODEL, D_MODEL)
    else:
        w_glu_full, w_out_full = w_glu_in, w_out_in
    sink128 = jnp.broadcast_to(attn_sink[0][:, None, None], (N_Q_HEADS, 1, 128))
    attn_bias = _attn_bias()
    o = _attn_fwd(q_stack, k_rot, v_bf, sink128, attn_bias)
    ys, starts = [], []
    for d in range(N_DIR):
        y_d, s_r, s_i = _ssm_fwd(u, lam, bb, cb, direction=d, tb=ssm_tb, name=f"ssm_fwd_{d}")
        ys.append(y_d)
        starts.append((s_r, s_i))

    row = lambda t: t.reshape(1, -1)
    g_attn_p = _to_pair_order(norm_attn_g)
    loss_blk, d_o, d_za, d_ylin, d_zs, d_pre, g_w_out, g_w_glu, g_vec = _mid(
        o, z_attn, u, ys[0], ys[1], z_ssm, xs, target, row(ssm_d), w_glu_full, row(b_glu),
        g_attn_p, row(norm_ssm_g), w_out_full, row(ln_g), row(ln_b), min(256, seq))

    pieces = [g_w_glu.reshape(N_CHIPS, -1, D_SSM), g_w_out.reshape(N_CHIPS, -1, D_MODEL)] if sharded else []
    attn_grads = _attn_bwd(q_stack, k_rot, v_bf, sink128, attn_bias, d_o, pieces)
    dq, dk, dv, g_sink = attn_grads[:4]
    if sharded:
        g_w_glu, g_w_out = attn_grads[4:]
    dus, g_bb, g_cb, g_lam = [], [], [], []
    for d in range(N_DIR):
        du_d, gb_d, gc_d, dl_d = _ssm_bwd(u, d_ylin, starts[d], lam, bb, bbt, cb_t, direction=d, tb=ssm_tb,
                                          name=f"ssm_bwd_{d}")
        dus.append(du_d)
        g_bb.append(gb_d)
        g_cb.append(gc_d)
        g_lam.append(dl_d)
    g_ar, g_ai, g_dt, g_br, g_bi, g_cr, g_ci = _ssm_params_bwd(a_r, a_i, log_dt, b_r, b_i, g_bb, g_cb, g_lam)

    g_small = _pack_small_grads([g_br, g_bi, g_cr, g_ci], g_vec, g_ar, g_ai, g_dt, g_sink[:, 0], loss_blk[0, 0])
    dproj_args = (dq, dk, dv, d_za, dus[0], dus[1], d_ylin, d_zs, row(ssm_d))
    w_grads = _proj_bwd_w(xs, dproj_args, rope_hi, rope_lo, [g_small] if sharded else [], min(512, seq))
    g_wt = w_grads[0]
    if sharded:
        g_small = w_grads[1]
    x_grads = _proj_bwd_x(dproj_args, rope_hi, rope_lo, d_pre, wt_full,
                          [g_wt.reshape(N_CHIPS, -1, D_MODEL)] if sharded else [], min(512, seq))
    g_x = x_grads[0]
    if sharded:
        g_wt = x_grads[1]
    return g_x, g_wt, g_w_out, g_w_glu, g_small
```

```python
import functools
import math

import numpy as np
import jax
import jax.numpy as jnp
from jax import lax
from jax.experimental import pallas as pl
from jax.experimental.pallas import tpu as pltpu

F32 = jnp.float32
BF16 = jnp.bfloat16
MESH = pl.DeviceIdType.MESH

D_MODEL = 1024
D_ATTN = 512
D_SSM = 512
HEAD_DIM = 64
N_Q_HEADS = 8
WINDOW = 128
ROPE_THETA = 10000.0
SSM_CH = 16
N_GROUPS = 32
SSM_STATE = 64
N_DIR = 2
STATE_W = N_GROUPS * SSM_STATE
N_SLAB = 4
SLAB_IN = 128
SLAB_ST = 512
NORM_EPS = 1e-5
NEG_INF = -1e30
ALPHA = 2.0 ** 0.25
D_IN_PROJ = 2304
N_CHIPS = 4

ADAM_LR = 0.001
ADAM_B1 = 0.9
ADAM_B2 = 0.999
ADAM_EPS = 1e-08
ADAM_WD = 0.01
ADAM_STEP = 10

SUBSEG = 8
SCAN_LANES = 512
SSM_BLOCK = 512
VMEM_LIMIT = 48 * 1024 * 1024
ADAMW_BLOCK_BYTES = 3 * 512 * 1024
PROJ_BWD_X_VMEM = 56 * 1024 * 1024

def _to_pair_order(row):
    return jnp.transpose(row.reshape(2, 4, HEAD_DIM), (1, 0, 2)).reshape(1, D_ATTN)


def _from_pair_order(row):
    return jnp.transpose(row.reshape(4, 2, HEAD_DIM), (1, 0, 2)).reshape(1, D_ATTN)


def _cparams(sem=None):
    return pltpu.CompilerParams(dimension_semantics=sem, vmem_limit_bytes=VMEM_LIMIT)


def _dot(a, b):
    return jnp.dot(a, b, preferred_element_type=F32)


def _dot_nt(a, b):
    return lax.dot_general(a, b, (((1,), (1,)), ((), ())), preferred_element_type=F32)


def _dot_tn(a, b):
    return lax.dot_general(a, b, (((0,), (0,)), ((), ())), preferred_element_type=F32)


def _sigmoid(z):
    return 1.0 / (1.0 + jnp.exp(-z))


def _all_gather_chips(shards, out_dtype, name):
    n = len(shards)

    def body(*refs):
        start, relay, finish = _gather_phases(refs[:n], refs[n:2 * n], *refs[2 * n:], out_dtype)
        start()
        relay()
        finish()

    vmem = pl.BlockSpec(memory_space=pltpu.VMEM)
    return pl.pallas_call(
        body, name=name,
        out_shape=[jax.ShapeDtypeStruct((N_CHIPS,) + s.shape, out_dtype) for s in shards],
        in_specs=[vmem] * n, out_specs=[vmem] * n,
        scratch_shapes=_gather_sems(n),
        compiler_params=pltpu.CompilerParams(vmem_limit_bytes=VMEM_LIMIT),
    )(*shards)


def _gather_sems(n):
    return [pltpu.SemaphoreType.DMA((6 * n,)), pltpu.SemaphoreType.DMA((6 * n,))]


def _gather_phases(in_refs, out_refs, send_sems, recv_sems, out_dtype):
    n = len(in_refs)
    x, y, c = lax.axis_index("x"), lax.axis_index("y"), lax.axis_index("c")
    sibling = (x, y, 1 - c)
    chips = [(1 - x, y), (x, 1 - y), (1 - x, 1 - y)]

    def half_of(a, px, py, half):
        rows = in_refs[a].shape[0] // 2
        return out_refs[a].at[2 * px + py, pl.ds(half * rows, rows), :]

    def copy(a, k, px, py, half, to):
        blk = half_of(a, px, py, half)
        return pltpu.make_async_remote_copy(src_ref=blk, dst_ref=blk, send_sem=send_sems.at[6 * a + k],
                                            recv_sem=recv_sems.at[6 * a + k], device_id=to, device_id_type=MESH)

    first = [copy(a, j, x, y, c, (*chips[j], c)) for a in range(n) for j in range(3)]
    passed = [copy(a, 3 + j, *chips[j], c, sibling) for a in range(n) for j in range(3)]

    def start():
        for a in range(n):
            out_refs[a][2 * x + y] = in_refs[a][...].astype(out_dtype)
        for cp in first:
            cp.start()

    def relay():
        for a in range(n):
            for j in range(3):
                copy(a, j, *chips[j], c, (x, y, c)).wait_recv()
                passed[3 * a + j].start()

    def finish():
        for a in range(n):
            for j in range(3):
                copy(a, 3 + j, *chips[j], 1 - c, (x, y, c)).wait_recv()
        for cp in first + passed:
            cp.wait_send()

    return start, relay, finish


SEMS_PER_ARRAY = 14


def _reduce_scratch(shapes, narrow):
    half = [(N_CHIPS, s[1] // 2, s[2]) for s in shapes]
    wire = [BF16 if nar else F32 for nar in narrow]
    n = len(shapes)
    return ([pltpu.VMEM(half[a], F32) for a in range(n)] + [pltpu.VMEM(half[a], wire[a]) for a in range(n)]
            + [pltpu.VMEM(half[a], wire[a]) for a in range(n)]
            + [pltpu.SemaphoreType.DMA((SEMS_PER_ARRAY * n,)), pltpu.SemaphoreType.DMA((SEMS_PER_ARRAY * n,))])


def _reduce_phases(p_refs, out_refs, a_refs, s_refs, b_refs, send_sems, recv_sems, narrow, gather_last):
    n = len(p_refs)
    halves = [p.shape[1] // 2 for p in p_refs]
    wire = [BF16 if nar else F32 for nar in narrow]
    x, y, c = lax.axis_index("x"), lax.axis_index("y"), lax.axis_index("c")
    me = 2 * x + y
    sibling = (x, y, 1 - c)
    chips = [(1 - x, y), (x, 1 - y), (1 - x, 1 - y)]
    slot = [2 * px + py for px, py in chips]
    last = n - 1

    def copy(a, k, src, dst, to):
        return pltpu.make_async_remote_copy(src_ref=src, dst_ref=dst, send_sem=send_sems.at[SEMS_PER_ARRAY * a + k],
                                            recv_sem=recv_sems.at[SEMS_PER_ARRAY * a + k],
                                            device_id=to, device_id_type=MESH)

    def rows(a, half):
        return pl.ds(pl.multiple_of(half * halves[a], 16), halves[a])

    def finished(a, k, half):
        if gather_last and a == last:
            return out_refs[a].at[k, rows(a, half), :]
        return out_refs[a].at[rows(a, half), :]

    order = slot + [me]
    swaps = [[copy(a, q, p_refs[a].at[order[q], rows(a, 1 - c), :], a_refs[a].at[order[q]], sibling)
              for q in range(N_CHIPS)] for a in range(n)]
    sends = [[copy(a, 4 + j, s_refs[a].at[slot[j]], b_refs[a].at[me], (*chips[j], c)) for j in range(3)] for a in range(n)]
    backs = [copy(a, 7, finished(a, me, c), finished(a, me, c), sibling) for a in range(n)]
    spread = [copy(last, 8 + j, finished(last, me, c), finished(last, me, c), (*chips[j], c)) for j in range(3)]
    relays = [copy(last, 11 + j, finished(last, slot[j], c), finished(last, slot[j], c), sibling) for j in range(3)]

    def start():
        for group in swaps:
            for cp in group:
                cp.start()

    def exchange():
        for a in range(n):
            for q in range(N_CHIPS):
                swaps[a][q].wait_recv()
                acc = a_refs[a][order[q]] + p_refs[a][order[q], rows(a, c), :]
                a_refs[a][order[q]] = acc
                s_refs[a][order[q]] = acc.astype(wire[a])
                if q < 3:
                    sends[a][q].start()
            b_refs[a][me] = s_refs[a][me]

    def combine():
        for a in range(n):
            for j in range(3):
                copy(a, 4 + j, s_refs[a].at[slot[j]], b_refs[a].at[slot[j]], (x, y, c)).wait_recv()
            terms = [jnp.where(me == k, a_refs[a][k], b_refs[a][k].astype(F32)) for k in range(N_CHIPS)]
            total = (terms[0] + terms[1]) + (terms[2] + terms[3])
            if gather_last and a == last:
                out_refs[a][me, rows(a, c), :] = total
            else:
                out_refs[a][rows(a, c), :] = total
            backs[a].start()
        if gather_last:
            for cp in spread:
                cp.start()

    def finish():
        if gather_last:
            for j in range(3):
                copy(last, 8 + j, finished(last, slot[j], c), finished(last, slot[j], c), (x, y, c)).wait_recv()
                relays[j].start()
        for a in range(n):
            copy(a, 7, finished(a, me, 1 - c), finished(a, me, 1 - c), (x, y, c)).wait_recv()
        if gather_last:
            for j in range(3):
                copy(last, 11 + j, finished(last, slot[j], 1 - c), finished(last, slot[j], 1 - c), (x, y, c)).wait_recv()
        started = [cp for group in swaps + sends for cp in group] + backs + (spread + relays if gather_last else [])
        for cp in started:
            cp.wait_send()

    return start, exchange, combine, finish


def _ssm_param_values(ar, ai, logdt):
    dt = jnp.exp(logdt)
    mag = jnp.exp(dt * ar)
    cs, sn = jnp.cos(dt * ai), jnp.sin(dt * ai)
    lr, li = mag * cs, mag * sn
    den = ar * ar + ai * ai
    nr = (lr - 1.0) * ar + li * ai
    ni = li * ar - (lr - 1.0) * ai
    return dt, mag, lr, li, den, nr, ni


GROUPS_PER_SLAB = N_GROUPS // N_SLAB


def _slab_masks():
    def eq(shape, f_row, f_col):
        return (f_row(lax.broadcasted_iota(jnp.int32, shape, 0)) == f_col(lax.broadcasted_iota(jnp.int32, shape, 1))).astype(F32)
    spread = eq((SSM_STATE, SLAB_ST), lambda r: r, lambda c: c % SSM_STATE)
    spread_t = eq((SLAB_ST, SSM_STATE), lambda r: r % SSM_STATE, lambda c: c)
    keep = eq((SLAB_IN, SLAB_ST), lambda r: r // SSM_CH, lambda c: c // SSM_STATE)
    keep_t = eq((SLAB_ST, SLAB_IN), lambda r: r // SSM_STATE, lambda c: c // SSM_CH)
    repeat = eq((N_DG * SSM_CH, N_DG), lambda r: r // SSM_CH, lambda c: c)
    return spread, spread_t, keep, keep_t, repeat


def _rows(ref):
    return ref[...].reshape(-1, SSM_STATE)


def _split3(t):
    hi = t.astype(BF16)
    rest = t - hi.astype(F32)
    mid = rest.astype(BF16)
    return hi, mid, (rest - mid.astype(F32)).astype(BF16)


def _select(dot, ones01, t, ones_first):
    o = ones01.astype(BF16)
    parts = [dot(o, p) if ones_first else dot(p, o) for p in _split3(t)]
    return (parts[0] + parts[1]) + parts[2]


def _ssm_params_fwd(ar, ai, logdt, br, bi, cr, ci, n_square):
    def body(ar_ref, ai_ref, dt_ref, br_ref, bi_ref, cr_ref, ci_ref, lam_ref, bb_ref, bbt_ref, cb_ref, cbt_ref):
        _, _, lr, li, den, nr, ni = _ssm_param_values(_rows(ar_ref), _rows(ai_ref), dt_ref[...])
        lam_ref[0] = lr
        lam_ref[1] = li
        pr, pi = lr, li
        for _ in range(n_square):
            pr, pi = pr * pr - pi * pi, 2.0 * pr * pi
        lam_ref[2] = pr
        lam_ref[3] = pi
        spread, spread_t, keep, keep_t, repeat = _slab_masks()
        fr = _select(_dot, repeat, nr / den, True)
        fi = _select(_dot, repeat, ni / den, True)
        b_r, b_i = _rows(br_ref), _rows(bi_ref)
        bbar = (fr * b_r - fi * b_i, fr * b_i + fi * b_r)
        c_par = (_rows(cr_ref), _rows(ci_ref))
        spread, spread_t = spread.astype(BF16), spread_t.astype(BF16)
        for src, wide_ref, tall_ref in ((bbar, bb_ref, bbt_ref), (c_par, cbt_ref, cb_ref)):
            for q in range(2):
                for d in range(N_DIR):
                    for k in range(N_SLAB):
                        r0 = (d * N_GROUPS + k * GROUPS_PER_SLAB) * SSM_CH
                        blk = src[q][r0:r0 + SLAB_IN].astype(BF16)
                        wide_ref[q, d, k] = (_dot(blk, spread) * keep).astype(BF16)
                        tall_ref[q, d, k] = (_dot_nt(spread_t, blk) * keep_t).astype(BF16)

    wide = jax.ShapeDtypeStruct((2, N_DIR, N_SLAB, SLAB_IN, SLAB_ST), BF16)
    tall = jax.ShapeDtypeStruct((2, N_DIR, N_SLAB, SLAB_ST, SLAB_IN), BF16)
    return pl.pallas_call(body, name="ssm_params_fwd",
                          out_shape=[jax.ShapeDtypeStruct((4, N_DG, SSM_STATE), F32), wide, tall, tall, wide],
                          compiler_params=pltpu.CompilerParams(vmem_limit_bytes=VMEM_LIMIT),
                          )(ar, ai, logdt, br, bi, cr, ci)


def _ssm_params_bwd(ar, ai, logdt, br, bi, g_slabs_b, g_slabs_c, g_lam):
    def body(ar_ref, ai_ref, dt_ref, br_ref, bi_ref, gb0_ref, gb1_ref, gc0_ref, gc1_ref, gl0_ref, gl1_ref,
             gar_ref, gai_ref, gdt_ref, gbr_ref, gbi_ref, gcr_ref, gci_ref, dbb, dlam):
        spread, spread_t, keep, keep_t, repeat = _slab_masks()
        for d, (gb_ref, gc_ref) in enumerate(((gb0_ref, gc0_ref), (gb1_ref, gc1_ref))):
            for q in range(2):
                for k in range(N_SLAB):
                    r0 = (d * N_GROUPS + k * GROUPS_PER_SLAB) * SSM_CH
                    dbb[q, r0:r0 + SLAB_IN, :] = _select(_dot, spread_t, gb_ref[q, k] * keep, False)
                    out_ref = gcr_ref if q == 0 else gci_ref
                    out_ref[r0:r0 + SLAB_IN, :] = _select(_dot_tn, spread_t, gc_ref[q, k] * keep_t, False)
        grp = (lax.broadcasted_iota(jnp.int32, (N_GROUPS, STATE_W), 0)
               == lax.broadcasted_iota(jnp.int32, (N_GROUPS, STATE_W), 1) // SSM_STATE).astype(F32)
        pick = (lax.broadcasted_iota(jnp.int32, (STATE_W, SSM_STATE), 0) % SSM_STATE
                == lax.broadcasted_iota(jnp.int32, (STATE_W, SSM_STATE), 1)).astype(F32)
        for d, gl_ref in enumerate((gl0_ref, gl1_ref)):
            for q in range(2):
                row = jnp.sum(gl_ref[q], axis=0, keepdims=True)
                dlam[q, d * N_GROUPS:(d + 1) * N_GROUPS, :] = _select(_dot, pick, grp * row, False)

        a_r, a_i = _rows(ar_ref), _rows(ai_ref)
        dt, mag, lr, li, den, nr, ni = _ssm_param_values(a_r, a_i, dt_ref[...])
        fr = _select(_dot, repeat, nr / den, True)
        fi = _select(_dot, repeat, ni / den, True)
        b_r, b_i = _rows(br_ref), _rows(bi_ref)
        g_r, g_i = dbb[0], dbb[1]
        gbr_ref[...] = fr * g_r + fi * g_i
        gbi_ref[...] = fr * g_i - fi * g_r
        d_fr = _select(_dot_tn, repeat, b_r * g_r + b_i * g_i, True)
        d_fi = _select(_dot_tn, repeat, b_r * g_i - b_i * g_r, True)
        d_nr, d_ni = d_fr / den, d_fi / den
        d_den = -(d_fr * nr + d_fi * ni) / (den * den)
        d_lr = dlam[0] + d_nr * a_r - d_ni * a_i
        d_li = dlam[1] + d_nr * a_i + d_ni * a_r
        d_ar = d_nr * (lr - 1.0) + d_ni * li + d_den * 2.0 * a_r
        d_ai = d_nr * li - d_ni * (lr - 1.0) + d_den * 2.0 * a_i
        d_mag = (d_lr * lr + d_li * li) / mag
        d_theta = d_li * lr - d_lr * li
        gar_ref[...] = d_ar + d_mag * mag * dt
        gai_ref[...] = d_ai + d_theta * dt
        d_dt = d_mag * mag * a_r + d_theta * a_i
        gdt_ref[...] = jnp.sum(d_dt, axis=1, keepdims=True) * dt

    small = jax.ShapeDtypeStruct((N_DG, SSM_STATE), F32)
    big = jax.ShapeDtypeStruct((N_DG * SSM_CH, SSM_STATE), F32)
    return pl.pallas_call(
        body, name="ssm_params_bwd",
        out_shape=[small, small, jax.ShapeDtypeStruct(logdt.shape, F32), big, big, big, big],
        scratch_shapes=[pltpu.VMEM((2,) + big.shape, F32), pltpu.VMEM((2,) + small.shape, F32)],
        compiler_params=pltpu.CompilerParams(vmem_limit_bytes=VMEM_LIMIT),
    )(ar, ai, logdt, br, bi, *g_slabs_b, *g_slabs_c, *g_lam)


ROPE_GROUP = 128


def _rope_tables(seq):
    half = HEAD_DIM // 2
    inv_freq = jnp.tile(ROPE_THETA ** (-jnp.arange(half, dtype=F32) / half), 4)
    sign = jnp.tile(jnp.concatenate([-jnp.ones((half,), F32), jnp.ones((half,), F32)]), 2)

    def table(pos):
        ang = pos.astype(F32)[:, None] * inv_freq[None, :]
        return jnp.stack([jnp.cos(ang), jnp.sin(ang), sign * jnp.sin(ang)])

    return table(jnp.arange(seq // ROPE_GROUP) * ROPE_GROUP), table(jnp.arange(ROPE_GROUP))


def _rope_block(hi_ref, lo_ref, first_group, n_groups):
    cl, sl, sl_s = lo_ref[0], lo_ref[1], lo_ref[2]
    cos, sin = [], []
    for g in range(n_groups):
        ch, sh, sh_s = (hi_ref[q, pl.ds(first_group + g, 1), :] for q in range(3))
        cos.append(ch * cl - sh * sl)
        sin.append(sh_s * cl + ch * sl_s)
    return jnp.concatenate(cos, axis=0), jnp.concatenate(sin, axis=0)


def _rotate_half_unsigned(t):
    lane = lax.broadcasted_iota(jnp.int32, t.shape, 1)
    return jnp.where((lane % HEAD_DIM) < HEAD_DIM // 2, pltpu.roll(t, 96, 1), pltpu.roll(t, 32, 1))


def _rope(t, cos, sin_signed):
    return t * cos + _rotate_half_unsigned(t) * sin_signed


def _pair_blocks(base):
    out = []
    for j in range(4):
        for g in range(2):
            nat = base + HEAD_DIM * (4 * g + j)
            par = base + 128 * j + HEAD_DIM * g
            out.append((slice(nat, nat + HEAD_DIM), slice(par, par + HEAD_DIM)))
    return out


W_Q, W_KV, W_ZA, W_U, W_ZS = 0, 512, 768, 1280, 1792


def _proj(x, wt, rope_hi, rope_lo, shards, tb):
    seq = x.shape[0]
    steps = seq // tb
    n_sh = len(shards)

    def body(*refs):
        x_ref, wt_ref, hi_ref, lo_ref = refs[:4]
        shard_refs = refs[4:4 + n_sh]
        q_ref, k_ref, v_ref, za_ref, u_ref, zs_ref = refs[4 + n_sh:10 + n_sh]
        gathered_refs = refs[10 + n_sh:10 + 2 * n_sh]
        wp = refs[10 + 2 * n_sh]
        step = pl.program_id(0)
        if n_sh:
            landing_refs = refs[11 + 2 * n_sh:11 + 3 * n_sh]
            start, relay, finish = _gather_phases(shard_refs, landing_refs, *refs[11 + 3 * n_sh:], BF16)
            pl.when(step == 0)(start)
            pl.when(step == max(steps - 2, 0))(relay)

        @pl.when(step == 0)
        def _():
            for dst_base, src_base in ((0, W_Q), (512, W_ZA)):
                for nat, par in _pair_blocks(0):
                    wp[dst_base + par.start:dst_base + par.stop, :] = wt_ref[src_base + nat.start:src_base + nat.stop, :]

        xb = x_ref[...].astype(BF16)
        cos, sin = _rope_block(hi_ref, lo_ref, pl.program_id(0) * (tb // ROPE_GROUP), tb // ROPE_GROUP)
        lo = lax.broadcasted_iota(jnp.int32, (tb, 128), 1) < HEAD_DIM
        q = _dot_nt(xb, wp[0:512, :])
        for j in range(4):
            qj = _rope(q[:, 128 * j:128 * (j + 1)], cos, sin)
            q_ref[j] = jnp.where(lo, qj, 0.0).astype(BF16)
            q_ref[4 + j] = jnp.where(lo, 0.0, qj).astype(BF16)
        kv = _dot_nt(xb, wt_ref[W_KV:W_ZA, :])
        k_ref[...] = _rope(kv[:, 0:128], cos, sin).astype(BF16)
        v_ref[...] = kv[:, 128:256].astype(BF16)
        za_ref[...] = _dot_nt(xb, wp[512:1024, :])
        u_val = _dot_nt(xb, wt_ref[W_U:W_ZS, :])
        for k in range(N_SLAB):
            u_ref[k] = u_val[:, k * SLAB_IN:(k + 1) * SLAB_IN]
        zs_ref[...] = _dot_nt(xb, wt_ref[W_ZS:D_IN_PROJ, :])
        if n_sh:
            @pl.when(step == steps - 1)
            def _():
                finish()
                for a in range(n_sh):
                    gathered_refs[a][...] = landing_refs[a][...]

    row = lambda w: pl.BlockSpec((tb, w), lambda i: (i, 0))
    table = lambda t: pl.BlockSpec(t.shape, lambda i: (0, 0, 0))
    vmem = pl.BlockSpec(memory_space=pltpu.VMEM)
    return pl.pallas_call(
        body, name="proj", grid=(steps,),
        in_specs=[row(D_MODEL), pl.BlockSpec((D_IN_PROJ, D_MODEL), lambda i: (0, 0), pipeline_mode=pl.Buffered(1)),
                  table(rope_hi), table(rope_lo)] + [vmem] * n_sh,
        out_specs=[pl.BlockSpec((8, tb, 128), lambda i: (0, i, 0)), row(128), row(128), row(512),
                   pl.BlockSpec((N_SLAB, tb, SLAB_IN), lambda i: (0, i, 0)), row(512)] + [vmem] * n_sh,
        out_shape=[jax.ShapeDtypeStruct((8, seq, 128), BF16), jax.ShapeDtypeStruct((seq, 128), BF16),
                   jax.ShapeDtypeStruct((seq, 128), BF16), jax.ShapeDtypeStruct((seq, 512), F32),
                   jax.ShapeDtypeStruct((N_SLAB, seq, SLAB_IN), F32), jax.ShapeDtypeStruct((seq, 512), F32)]
        + [jax.ShapeDtypeStruct((N_CHIPS,) + s.shape, BF16) for s in shards],
        scratch_shapes=[pltpu.VMEM((1024, D_MODEL), BF16)] + [pltpu.VMEM((N_CHIPS,) + s.shape, BF16) for s in shards]
        + (_gather_sems(n_sh) if n_sh else []),
        compiler_params=_cparams(("arbitrary",)),
    )(x, wt, rope_hi, rope_lo, *shards)


ATT_TQ = 128
ATT_KEYS = ATT_TQ + 2 * WINDOW


def _attn_window(i, seq):
    start = jnp.clip(i * ATT_TQ - WINDOW, 0, seq - ATT_KEYS)
    return pl.multiple_of(start, WINDOW)


def _attn_bias():
    r = np.arange(ATT_TQ)[None, :, None]
    c = np.arange(ATT_KEYS)[None, None, :]
    off = np.array([0, WINDOW, ATT_KEYS - ATT_TQ])[:, None, None]
    return jnp.asarray(np.where(np.abs(r + off - c) <= WINDOW, 0.0, NEG_INF).astype(np.float32))


def _attn_bias_spec(nblk):
    pick = lambda i: jnp.where(i == 0, 0, jnp.where(i == nblk - 1, 2, 1))
    return pl.BlockSpec((None, ATT_TQ, ATT_KEYS), lambda i: (pick(i), 0, 0))


def _attn_softmax(q_ref, k_ref, v_ref, sink_ref, bias_ref, start):
    kw = k_ref[pl.ds(start, ATT_KEYS), :]
    vw = v_ref[pl.ds(start, ATT_KEYS), :]
    qall = q_ref[...].reshape(N_Q_HEADS * ATT_TQ, 128)
    s = (_dot_nt(qall, kw) * (HEAD_DIM ** -0.5)).reshape(N_Q_HEADS, ATT_TQ, ATT_KEYS) + bias_ref[...][None]
    tiles = [s[:, :, 128 * t:128 * (t + 1)] for t in range(ATT_KEYS // 128)]
    m = jnp.max(functools.reduce(jnp.maximum, tiles), axis=2, keepdims=True)
    sink = sink_ref[...]
    m_b = jnp.maximum(jnp.broadcast_to(m, (N_Q_HEADS, ATT_TQ, 128)), sink)
    p = jnp.concatenate([jnp.exp(t - m_b) for t in tiles], axis=2)
    p_sink = jnp.exp(sink - m_b)
    lo_k = lax.broadcasted_iota(jnp.int32, (ATT_KEYS, 128), 1) < HEAD_DIM
    v_f = vw.astype(F32)
    v_lo, v_hi = jnp.where(lo_k, v_f, 1.0).astype(BF16), jnp.where(lo_k, 1.0, v_f).astype(BF16)
    pb = p.astype(BF16).reshape(N_Q_HEADS * ATT_TQ, ATT_KEYS)
    half = 4 * ATT_TQ
    r = jnp.concatenate([_dot(pb[:half], v_lo), _dot(pb[half:], v_hi)], axis=0).reshape(N_Q_HEADS, ATT_TQ, 128)
    return kw, vw, qall, p, p_sink, r


def _attn_fwd(q_stack, k, v, sink128, bias):
    seq = k.shape[0]

    def body(q_ref, k_ref, v_ref, sink_ref, bias_ref, o_ref):
        start = _attn_window(pl.program_id(0), seq)
        _, _, _, _, p_sink, r = _attn_softmax(q_ref, k_ref, v_ref, sink_ref, bias_ref, start)
        out = r / (pltpu.roll(r, HEAD_DIM, 2) + p_sink)
        lo = lax.broadcasted_iota(jnp.int32, (ATT_TQ, 128), 1) < HEAD_DIM
        for j in range(4):
            o_ref[:, 128 * j:128 * (j + 1)] = jnp.where(lo, out[j], out[4 + j])

    full = lambda w: pl.BlockSpec((seq, w), lambda i: (0, 0))
    return pl.pallas_call(
        body, name="attn_fwd", grid=(seq // ATT_TQ,),
        in_specs=[pl.BlockSpec((8, ATT_TQ, 128), lambda i: (0, i, 0)), full(128), full(128),
                  pl.BlockSpec((N_Q_HEADS, 1, 128), lambda i: (0, 0, 0)), _attn_bias_spec(seq // ATT_TQ)],
        out_specs=pl.BlockSpec((ATT_TQ, 512), lambda i: (i, 0)),
        out_shape=jax.ShapeDtypeStruct((seq, 512), F32),
        compiler_params=_cparams(("arbitrary",)),
    )(q_stack, k, v, sink128, bias)


def _attn_bwd(q_stack, k, v, sink128, bias, d_o, pieces):
    seq = k.shape[0]
    steps = seq // ATT_TQ
    n_p = len(pieces)

    def body(*refs):
        q_ref, k_ref, v_ref, sink_ref, bias_ref, do_ref = refs[:6]
        piece_refs = refs[6:6 + n_p]
        dq_ref, dk_ref, dv_ref, dsink_ref = refs[6 + n_p:10 + n_p]
        reduced_refs = refs[10 + n_p:10 + 2 * n_p]
        sink_acc = refs[10 + 2 * n_p]
        i = pl.program_id(0)
        if n_p:
            landing_refs = refs[11 + 2 * n_p:11 + 3 * n_p]
            scratch = refs[11 + 3 * n_p:]
            begin, exchange, combine, finish = _reduce_phases(
                piece_refs, landing_refs, scratch[:n_p], scratch[n_p:2 * n_p], scratch[2 * n_p:3 * n_p],
                *scratch[3 * n_p:], [True] * n_p, gather_last=False)
            pl.when(i == 0)(begin)
            pl.when(i == min(4, steps - 1))(exchange)
            pl.when(i == (3 * steps) // 4)(combine)

        @pl.when(i == 0)
        def _():
            dk_ref[...] = jnp.zeros_like(dk_ref)
            dv_ref[...] = jnp.zeros_like(dv_ref)
            sink_acc[...] = jnp.zeros_like(sink_acc)

        start = _attn_window(i, seq)
        kw, vw, qall, p, p_sink, r = _attn_softmax(q_ref, k_ref, v_ref, sink_ref, bias_ref, start)
        lo = lax.broadcasted_iota(jnp.int32, (ATT_TQ, 128), 1) < HEAD_DIM
        lo3 = lo[None]
        grp0 = lax.broadcasted_iota(jnp.int32, (N_Q_HEADS, ATT_TQ, 128), 0) < 4
        val = grp0 == lo3
        swapped = pltpu.roll(r, HEAD_DIM, 2)
        inv = 1.0 / (jnp.where(val, swapped, r) + p_sink)
        d_o_blk = do_ref[...]
        do3 = jnp.where(val, jnp.concatenate([d_o_blk[None, :, 128 * j:128 * (j + 1)] for j in range(4)] * 2, axis=0), 0.0)
        t = (do3 * r).reshape(N_Q_HEADS * ATT_TQ, 128)
        t_hi = t.astype(BF16)
        t_lo = (t - t_hi.astype(F32)).astype(BF16)
        ones = jnp.ones((128, 128), BF16)
        delta = (_dot(t_hi, ones) + _dot(t_lo, ones)).reshape(N_Q_HEADS, ATT_TQ, 128) * inv
        sink_acc[...] += -(p_sink * inv) * delta
        do_all = do3.astype(BF16).reshape(N_Q_HEADS * ATT_TQ, 128)
        dp = _dot_nt(do_all, vw).reshape(N_Q_HEADS, ATT_TQ, ATT_KEYS)
        probs, ds = [], []
        for tl in range(ATT_KEYS // 128):
            cols = slice(128 * tl, 128 * (tl + 1))
            probs_t = p[:, :, cols] * inv
            probs.append(probs_t.astype(BF16))
            ds.append((probs_t * (dp[:, :, cols] - delta)).astype(BF16))
        probs_all = jnp.concatenate(probs, axis=2).reshape(N_Q_HEADS * ATT_TQ, ATT_KEYS)
        ds_all = jnp.concatenate(ds, axis=2).reshape(N_Q_HEADS * ATT_TQ, ATT_KEYS)
        scale = HEAD_DIM ** -0.5
        dq_all = (_dot(ds_all, kw) * scale).reshape(N_Q_HEADS, ATT_TQ, 128)
        for j in range(4):
            dq_ref[:, 128 * j:128 * (j + 1)] = jnp.where(lo, dq_all[j], dq_all[4 + j])
        dk_ref[pl.ds(start, ATT_KEYS), :] += _dot_tn(ds_all, qall) * scale
        dv_ref[pl.ds(start, ATT_KEYS), :] += _dot_tn(probs_all, do_all)

        @pl.when(i == steps - 1)
        def _():
            dsink_ref[...] = jnp.sum(sink_acc[...], axis=1)

        if n_p:
            @pl.when(i == steps - 1)
            def _():
                finish()
                for a in range(n_p):
                    reduced_refs[a][...] = landing_refs[a][...]

    full = lambda w: pl.BlockSpec((seq, w), lambda i: (0, 0))
    vmem = pl.BlockSpec(memory_space=pltpu.VMEM)
    return pl.pallas_call(
        body, name="attn_bwd", grid=(steps,),
        in_specs=[pl.BlockSpec((8, ATT_TQ, 128), lambda i: (0, i, 0)), full(128), full(128),
                  pl.BlockSpec((N_Q_HEADS, 1, 128), lambda i: (0, 0, 0)),
                  _attn_bias_spec(steps), pl.BlockSpec((ATT_TQ, 512), lambda i: (i, 0))] + [vmem] * n_p,
        out_specs=[pl.BlockSpec((ATT_TQ, 512), lambda i: (i, 0)), full(128), full(128),
                   pl.BlockSpec((N_Q_HEADS, 128), lambda i: (0, 0))] + [vmem] * n_p,
        out_shape=[jax.ShapeDtypeStruct((seq, 512), F32), jax.ShapeDtypeStruct((seq, 128), F32),
                   jax.ShapeDtypeStruct((seq, 128), F32), jax.ShapeDtypeStruct((N_Q_HEADS, 128), F32)]
        + [jax.ShapeDtypeStruct(p.shape[1:], F32) for p in pieces],
        scratch_shapes=[pltpu.VMEM((N_Q_HEADS, ATT_TQ, 128), F32)] + [pltpu.VMEM(p.shape[1:], F32) for p in pieces]
        + (_reduce_scratch([p.shape for p in pieces], [True] * n_p) if n_p else []),
        compiler_params=_cparams(("arbitrary",)),
    )(q_stack, k, v, sink128, bias, d_o, *pieces)


def _permute_rows(dst_ref, src_ref, sub_len):
    for k in range(N_SLAB):
        for j in range(sub_len):
            dst_ref[k, 8 * j:8 * (j + 1), :] = src_ref.at[k][pl.ds(j, SUBSEG, stride=sub_len), :]


def _unpermute_rows(dst_ref, src_ref, sub_len):
    for k in range(N_SLAB):
        for s in range(SUBSEG):
            dst_ref[k, s * sub_len:(s + 1) * sub_len, :] = src_ref.at[k][pl.ds(s, sub_len, stride=SUBSEG), :]


def _scan_chunk(br_ref, bi_ref, lr_row, li_row, init, cols, *, sub_len, reverse, store):
    lr = jnp.broadcast_to(lr_row[:, cols], (SUBSEG, SCAN_LANES))
    li = jnp.broadcast_to(li_row[:, cols], (SUBSEG, SCAN_LANES))
    if init is None:
        sr = si = jnp.zeros((SUBSEG, SCAN_LANES), F32)
    else:
        sr, si = init
    for jj in range(sub_len):
        rows = slice(SUBSEG * ((sub_len - 1 - jj) if reverse else jj), SUBSEG * (((sub_len - 1 - jj) if reverse else jj) + 1))
        sr, si = lr * sr - li * si + br_ref[rows, cols], lr * si + li * sr + bi_ref[rows, cols]
        if store:
            br_ref[rows, cols] = sr
            bi_ref[rows, cols] = si
    return sr, si


def _resolve_chunk(z, carry_refs, start_refs, pr_row, pi_row, cols, *, reverse):
    cr, ci = carry_refs[0][0:1, cols], carry_refs[1][0:1, cols]
    pr, pi = pr_row[:, cols], pi_row[:, cols]
    for s in (range(SUBSEG - 1, -1, -1) if reverse else range(SUBSEG)):
        start_refs[0][s:s + 1, cols] = cr
        start_refs[1][s:s + 1, cols] = ci
        cr, ci = pr * cr - pi * ci + z[0][s:s + 1, :], pr * ci + pi * cr + z[1][s:s + 1, :]
    carry_refs[0][0:1, cols] = cr
    carry_refs[1][0:1, cols] = ci


def _param_specs(direction):
    row = lambda q: pl.BlockSpec((None, None, 1, STATE_W), lambda i: (q, direction, 0, 0))
    wide = lambda q: pl.BlockSpec((None, None, N_SLAB, SLAB_IN, SLAB_ST), lambda i: (q, direction, 0, 0, 0))
    tall = lambda q: pl.BlockSpec((None, None, N_SLAB, SLAB_ST, SLAB_IN), lambda i: (q, direction, 0, 0, 0))
    return [row(q) for q in range(4)], [wide(0), wide(1)], [tall(0), tall(1)]


def _ssm_fwd(u, lam, bb, cb, *, direction, tb, name):
    reverse = direction == 1
    seq = u.shape[1]
    nblk = seq // tb
    sub_len = tb // SUBSEG

    def body(u_ref, lr_ref, li_ref, pr_ref, pi_ref, bbr_ref, bbi_ref, cbr_ref, cbi_ref,
             y_ref, sr_ref, si_ref, xr, xi, up, yp, car, cai):
        @pl.when(pl.program_id(0) == 0)
        def _():
            car[...] = jnp.zeros_like(car)
            cai[...] = jnp.zeros_like(cai)

        _permute_rows(up, u_ref, sub_len)
        lr, li, pr, pi = lr_ref[...], li_ref[...], pr_ref[...], pi_ref[...]
        chunk = lambda k: slice(k * SLAB_ST, (k + 1) * SLAB_ST)

        def drive(k):
            ub = up[k].astype(BF16)
            xr[:, chunk(k)] = _dot(ub, bbr_ref[k])
            xi[:, chunk(k)] = _dot(ub, bbi_ref[k])

        def scan(k):
            z = _scan_chunk(xr, xi, lr, li, None, chunk(k), sub_len=sub_len, reverse=reverse, store=False)
            _resolve_chunk(z, (car, cai), (sr_ref, si_ref), pr, pi, chunk(k), reverse=reverse)
            _scan_chunk(xr, xi, lr, li, (sr_ref[:, chunk(k)], si_ref[:, chunk(k)]), chunk(k),
                        sub_len=sub_len, reverse=reverse, store=True)

        def read_out(k):
            yp[k] = _dot(xr[:, chunk(k)].astype(BF16), cbr_ref[k]) - _dot(xi[:, chunk(k)].astype(BF16), cbi_ref[k])

        drive(0)
        for k in range(N_SLAB):
            if k + 1 < N_SLAB:
                drive(k + 1)
            scan(k)
            if k > 0:
                read_out(k - 1)
        read_out(N_SLAB - 1)
        _unpermute_rows(y_ref, yp, sub_len)

    blk = (lambda i: nblk - 1 - i) if reverse else (lambda i: i)
    rows, wide, tall = _param_specs(direction)
    tok = pl.BlockSpec((N_SLAB, tb, SLAB_IN), lambda i: (0, blk(i), 0))
    start_spec = pl.BlockSpec((None, SUBSEG, STATE_W), lambda i: (blk(i), 0, 0))
    return pl.pallas_call(
        body, name=name, grid=(nblk,),
        in_specs=[tok] + rows + wide + tall,
        out_specs=[tok, start_spec, start_spec],
        out_shape=[jax.ShapeDtypeStruct((N_SLAB, seq, SLAB_IN), F32), jax.ShapeDtypeStruct((nblk, SUBSEG, STATE_W), F32),
                   jax.ShapeDtypeStruct((nblk, SUBSEG, STATE_W), F32)],
        scratch_shapes=[pltpu.VMEM((tb, STATE_W), F32), pltpu.VMEM((tb, STATE_W), F32),
                        pltpu.VMEM((N_SLAB, tb, SLAB_IN), F32), pltpu.VMEM((N_SLAB, tb, SLAB_IN), F32),
                        pltpu.VMEM((SUBSEG, STATE_W), F32), pltpu.VMEM((SUBSEG, STATE_W), F32)],
        compiler_params=_cparams(("arbitrary",)),
    )(u, lam, lam, lam, lam, bb, bb, cb, cb)


def _ssm_bwd(u, dy, starts, lam, bb, bbt, cb_t, *, direction, tb, name):
    reverse = direction == 1
    seq = u.shape[1]
    nblk = seq // tb
    sub_len = tb // SUBSEG

    def body(u_ref, dy_ref, sr_ref, si_ref, lr_ref, li_ref, pr_ref, pi_ref, bbr_ref, bbi_ref, btr_ref, bti_ref,
             ctr_ref, cti_ref, du_ref, gb_ref, gc_ref, dl_ref,
             xr, xi, gr, gi, up, dyp, dup, gsr, gsi, car, cai):
        gbr_ref, gbi_ref = gb_ref.at[0], gb_ref.at[1]
        gcr_ref, gci_ref = gc_ref.at[0], gc_ref.at[1]
        dlr_ref, dli_ref = dl_ref.at[0], dl_ref.at[1]

        @pl.when(pl.program_id(0) == 0)
        def _():
            for ref in (car, cai, gbr_ref, gbi_ref, gcr_ref, gci_ref, dlr_ref, dli_ref):
                ref[...] = jnp.zeros_like(ref)

        _permute_rows(up, u_ref, sub_len)
        _permute_rows(dyp, dy_ref, sub_len)
        lr, li, pr, pi = lr_ref[...], li_ref[...], pr_ref[...], pi_ref[...]
        nli, npi = -li, -pi
        chunk = lambda k: slice(k * SLAB_ST, (k + 1) * SLAB_ST)

        def drive(k):
            ub = up[k].astype(BF16)
            xr[:, chunk(k)] = _dot(ub, bbr_ref[k])
            xi[:, chunk(k)] = _dot(ub, bbi_ref[k])
            dyb = dyp[k].astype(BF16)
            gr[:, chunk(k)] = _dot(dyb, ctr_ref[k])
            gi[:, chunk(k)] = -_dot(dyb, cti_ref[k])

        def scan_x(k):
            _scan_chunk(xr, xi, lr, li, (sr_ref[:, chunk(k)], si_ref[:, chunk(k)]), chunk(k),
                        sub_len=sub_len, reverse=reverse, store=True)

        def grad_c(k):
            dyb = dyp[k].astype(BF16)
            gcr_ref[k] += _dot_tn(xr[:, chunk(k)].astype(BF16), dyb)
            gci_ref[k] -= _dot_tn(xi[:, chunk(k)].astype(BF16), dyb)

        def scan_g(k):
            z = _scan_chunk(gr, gi, lr, nli, None, chunk(k), sub_len=sub_len, reverse=not reverse, store=False)
            _resolve_chunk(z, (car, cai), (gsr, gsi), pr, npi, chunk(k), reverse=not reverse)
            _scan_chunk(gr, gi, lr, nli, (gsr[:, chunk(k)], gsi[:, chunk(k)]), chunk(k),
                        sub_len=sub_len, reverse=not reverse, store=True)

        def grad_b_du(k):
            ub = up[k].astype(BF16)
            grb, gib = gr[:, chunk(k)].astype(BF16), gi[:, chunk(k)].astype(BF16)
            gbr_ref[k] += _dot_tn(ub, grb)
            gbi_ref[k] += _dot_tn(ub, gib)
            dup[k] = _dot(grb, btr_ref[k]) + _dot(gib, bti_ref[k])

        def grad_lambda(k):
            cols = chunk(k)
            acc_r, acc_i = dlr_ref[:, cols], dli_ref[:, cols]
            for jj in range(sub_len):
                prev = jj + 1 if reverse else jj - 1
                if 0 <= prev < sub_len:
                    x_r, x_i = xr[SUBSEG * prev:SUBSEG * (prev + 1), cols], xi[SUBSEG * prev:SUBSEG * (prev + 1), cols]
                else:
                    x_r, x_i = sr_ref[:, cols], si_ref[:, cols]
                g_r, g_i = gr[SUBSEG * jj:SUBSEG * (jj + 1), cols], gi[SUBSEG * jj:SUBSEG * (jj + 1), cols]
                acc_r = acc_r + (g_r * x_r + g_i * x_i)
                acc_i = acc_i + (g_i * x_r - g_r * x_i)
            dlr_ref[:, cols] = acc_r
            dli_ref[:, cols] = acc_i

        drive(0)
        for k in range(N_SLAB):
            if k + 1 < N_SLAB:
                drive(k + 1)
            scan_x(k)
            grad_c(k)
            scan_g(k)
            grad_b_du(k)
            grad_lambda(k)
        _unpermute_rows(du_ref, dup, sub_len)

    blk = (lambda i: i) if reverse else (lambda i: nblk - 1 - i)
    rows, wide, tall = _param_specs(direction)
    tok = pl.BlockSpec((N_SLAB, tb, SLAB_IN), lambda i: (0, blk(i), 0))
    start_spec = pl.BlockSpec((None, SUBSEG, STATE_W), lambda i: (blk(i), 0, 0))
    gb_shape, gc_shape, dl_shape = (2, N_SLAB, SLAB_IN, SLAB_ST), (2, N_SLAB, SLAB_ST, SLAB_IN), (2, SUBSEG, STATE_W)
    whole = lambda shape: pl.BlockSpec(shape, lambda i: (0,) * len(shape))
    big = lambda: pltpu.VMEM((tb, STATE_W), F32)
    slabs = lambda: pltpu.VMEM((N_SLAB, tb, SLAB_IN), F32)
    tile = lambda: pltpu.VMEM((SUBSEG, STATE_W), F32)
    return pl.pallas_call(
        body, name=name, grid=(nblk,),
        in_specs=[tok, tok, start_spec, start_spec] + rows + wide + tall + wide,
        out_specs=[tok, whole(gb_shape), whole(gc_shape), whole(dl_shape)],
        out_shape=[jax.ShapeDtypeStruct((N_SLAB, seq, SLAB_IN), F32), jax.ShapeDtypeStruct(gb_shape, F32),
                   jax.ShapeDtypeStruct(gc_shape, F32), jax.ShapeDtypeStruct(dl_shape, F32)],
        scratch_shapes=[big(), big(), big(), big(), slabs(), slabs(), slabs(), tile(), tile(), tile(), tile()],
        compiler_params=_cparams(("arbitrary",)),
    )(u, dy, *starts, lam, lam, lam, lam, bb, bb, bbt, bbt, cb_t, cb_t)


GELU_C = math.sqrt(2.0 / math.pi)
GELU_K = 0.044715


def _mid(o, za, u, y_f, y_b, zs, x, target, ssm_d, w_glu, b_glu, g_attn, g_ssm, w_out, ln_g, ln_b, tb):
    seq = x.shape[0]

    def body(o_ref, za_ref, u_ref, yf_ref, yb_ref, zs_ref, x_ref, t_ref, d_ref, wg_ref, bg_ref, ga_ref, gs_ref,
             wo_ref, lg_ref, lb_ref,
             loss_ref, do_ref, dza_ref, dyl_ref, dzs_ref, dpre_ref, gwo_ref, gwg_ref, vec_ref, wop):
        @pl.when(pl.program_id(0) == 0)
        def _():
            for ref in (loss_ref, gwo_ref, gwg_ref, vec_ref):
                ref[...] = jnp.zeros_like(ref)
            for nat, par in _pair_blocks(0):
                wop[par, :] = wo_ref[nat, :]
            wop[D_ATTN:, :] = wo_ref[D_ATTN:, :]

        def rows_of(rs):
            o, za = o_ref[rs, :], za_ref[rs, :]
            sig_a = _sigmoid(za)
            silu_a = za * sig_a
            ya = o * silu_a
            r_a = lax.rsqrt(jnp.mean(ya * ya, axis=1, keepdims=True) + NORM_EPS)
            n_a = ya * r_a
            g_a = ga_ref[...]
            unslab = lambda ref: jnp.concatenate([ref[k, rs, :] for k in range(N_SLAB)], axis=1)
            u_blk, zs = unslab(u_ref), zs_ref[rs, :]
            d_row = d_ref[...]
            ylin = d_row * u_blk + unslab(yf_ref) + unslab(yb_ref)
            inner = GELU_C * (ylin + GELU_K * ylin * ylin * ylin)
            th = jnp.tanh(inner)
            gl = 0.5 * ylin * (1.0 + th)
            glb = gl.astype(BF16)
            gate = _dot(glb, wg_ref[...])
            sg = _sigmoid(gate + bg_ref[...])
            y2 = gl * sg
            sig_s = _sigmoid(zs)
            silu_s = zs * sig_s
            ys = y2 * silu_s
            r_s = lax.rsqrt(jnp.mean(ys * ys, axis=1, keepdims=True) + NORM_EPS)
            n_s = ys * r_s
            g_s = gs_ref[...]
            mixed = jnp.concatenate([n_a * g_a, n_s * g_s], axis=1).astype(BF16)
            out = _dot(mixed, wop[...])
            pre = ALPHA * x_ref[rs, :] + out
            mu = jnp.mean(pre, axis=1, keepdims=True)
            cen = pre - mu
            rstd = lax.rsqrt(jnp.mean(cen * cen, axis=1, keepdims=True) + NORM_EPS)
            hhat = cen * rstd
            ln_g = lg_ref[...]
            err = hhat * ln_g + lb_ref[...] - t_ref[rs, :]
            loss_ref[...] += 0.5 * jnp.sum(jnp.mean(err * err, axis=1, keepdims=True))

            dh = err * (1.0 / D_MODEL)
            vec_ref[0:1, :] += jnp.sum(dh * hhat, axis=0, keepdims=True)
            vec_ref[1:2, :] += jnp.sum(dh, axis=0, keepdims=True)
            dhh = dh * ln_g
            dpre = rstd * (dhh - jnp.mean(dhh, axis=1, keepdims=True)
                           - hhat * jnp.mean(dhh * hhat, axis=1, keepdims=True))
            dpre_ref[rs, :] = dpre
            dpb = dpre.astype(BF16)
            for j in range(4):
                g_pair = _dot_tn(mixed[:, 128 * j:128 * (j + 1)], dpb)
                for g in range(2):
                    nat = HEAD_DIM * (4 * g + j)
                    gwo_ref[nat:nat + HEAD_DIM, :] += g_pair[HEAD_DIM * g:HEAD_DIM * (g + 1), :]
            gwo_ref[D_ATTN:, :] += _dot_tn(mixed[:, D_ATTN:], dpb)
            dmix = _dot_nt(dpb, wop[...])
            dna = dmix[:, :D_ATTN]
            vec_ref[2:3, 0:D_ATTN] += jnp.sum(dna * n_a, axis=0, keepdims=True)
            dna = dna * g_a
            dya = r_a * (dna - n_a * jnp.mean(dna * n_a, axis=1, keepdims=True))
            do_ref[rs, :] = dya * silu_a
            dza_ref[rs, :] = dya * o * (sig_a * (1.0 + za * (1.0 - sig_a)))
            dns = dmix[:, D_ATTN:]
            vec_ref[2:3, D_ATTN:] += jnp.sum(dns * n_s, axis=0, keepdims=True)
            dns = dns * g_s
            dys = r_s * (dns - n_s * jnp.mean(dns * n_s, axis=1, keepdims=True))
            dzs_ref[rs, :] = dys * y2 * (sig_s * (1.0 + zs * (1.0 - sig_s)))
            dy2 = dys * silu_s
            da = dy2 * gl * sg * (1.0 - sg)
            vec_ref[3:4, D_SSM:] += jnp.sum(da, axis=0, keepdims=True)
            dab = da.astype(BF16)
            gwg_ref[...] += _dot_tn(glb, dab)
            dgl_mm = _dot_nt(dab, wg_ref[...])
            dgl = dy2 * sg + dgl_mm
            dylin = dgl * (0.5 * (1.0 + th)
                           + 0.5 * ylin * (1.0 - th * th) * GELU_C * (1.0 + 3.0 * GELU_K * ylin * ylin))
            for k in range(N_SLAB):
                dyl_ref[k, rs, :] = dylin[:, k * SLAB_IN:(k + 1) * SLAB_IN]
            vec_ref[3:4, 0:D_SSM] += jnp.sum(dylin * u_blk, axis=0, keepdims=True)

        rows_of(slice(0, tb))

    tok = lambda w: pl.BlockSpec((tb, w), lambda i: (i, 0))
    slab = pl.BlockSpec((N_SLAB, tb, SLAB_IN), lambda i: (0, i, 0))
    const = lambda r, c: pl.BlockSpec((r, c), lambda i: (0, 0), pipeline_mode=pl.Buffered(1))
    tok_shape = jax.ShapeDtypeStruct((seq, 512), F32)
    return pl.pallas_call(
        body, name="mid", grid=(seq // tb,),
        in_specs=[tok(512), tok(512), slab, slab, slab, tok(512), tok(1024), tok(1024),
                  const(1, 512), const(512, 512), const(1, 512), const(1, 512), const(1, 512),
                  const(1024, 1024), const(1, 1024), const(1, 1024)],
        out_specs=[const(8, 128), tok(512), tok(512), slab, tok(512), tok(1024),
                   const(1024, 1024), const(512, 512), const(8, 1024)],
        out_shape=[jax.ShapeDtypeStruct((8, 128), F32), tok_shape, tok_shape,
                   jax.ShapeDtypeStruct((N_SLAB, seq, SLAB_IN), F32), tok_shape,
                   jax.ShapeDtypeStruct((seq, 1024), F32), jax.ShapeDtypeStruct((1024, 1024), F32),
                   jax.ShapeDtypeStruct((512, 512), F32), jax.ShapeDtypeStruct((8, 1024), F32)],
        scratch_shapes=[pltpu.VMEM((D_MODEL, D_MODEL), BF16)],
        compiler_params=_cparams(("arbitrary",)),
    )(o, za, u, y_f, y_b, zs, x, target, ssm_d, w_glu, b_glu, g_attn, g_ssm, w_out, ln_g, ln_b)


def _ride_shapes(pieces, narrow, gather_last):
    outs = [p.shape if (gather_last and a == len(pieces) - 1) else p.shape[1:] for a, p in enumerate(pieces)]
    return outs, [pltpu.VMEM(s, F32) for s in outs] + _reduce_scratch([p.shape for p in pieces], narrow)


def _ride_phases(piece_refs, out_refs, scratch_refs, narrow, gather_last):
    n = len(piece_refs)
    landing, rest = scratch_refs[:n], scratch_refs[n:]
    begin, exchange, combine, finish = _reduce_phases(piece_refs, landing, rest[:n], rest[n:2 * n], rest[2 * n:3 * n],
                                                      *rest[3 * n:], narrow, gather_last)

    def end():
        finish()
        for a in range(n):
            out_refs[a][...] = landing[a][...]

    return begin, exchange, combine, end


def _dproj_block(dq_ref, dk_ref, dv_ref, dza_ref, duf_ref, dub_ref, dyl_ref, dzs_ref, d_ref, hi_ref, lo_ref, tb):
    cos, sin = _rope_block(hi_ref, lo_ref, pl.program_id(0) * (tb // ROPE_GROUP), tb // ROPE_GROUP)
    lo = lax.broadcasted_iota(jnp.int32, (tb, 128), 1) < HEAD_DIM

    def unrope(t):
        return t * cos + _rotate_half_unsigned(t * sin)

    def natural(pairs):
        swapped = [pltpu.roll(t, HEAD_DIM, 1) for t in pairs]
        return [jnp.where(lo, pairs[0], swapped[1]), jnp.where(lo, pairs[2], swapped[3]),
                jnp.where(lo, swapped[0], pairs[1]), jnp.where(lo, swapped[2], pairs[3])]

    dq_rot, dza = dq_ref[...], dza_ref[...]
    pieces = natural([unrope(dq_rot[:, 128 * j:128 * (j + 1)]) for j in range(4)])
    d_row = d_ref[...]
    pieces += [unrope(dk_ref[...]), dv_ref[...]] + natural([dza[:, 128 * j:128 * (j + 1)] for j in range(4)])
    pieces += [duf_ref[k] + dub_ref[k] + d_row[:, k * SLAB_IN:(k + 1) * SLAB_IN] * dyl_ref[k] for k in range(N_SLAB)]
    pieces += [dzs_ref[...]]
    return jnp.concatenate(pieces, axis=1).astype(BF16)


def _dproj_specs(tb, rope_hi, rope_lo):
    tok = lambda w: pl.BlockSpec((tb, w), lambda i: (i, 0))
    slab = pl.BlockSpec((N_SLAB, tb, SLAB_IN), lambda i: (0, i, 0))
    table = lambda t: pl.BlockSpec(t.shape, lambda i: (0, 0, 0))
    return [tok(512), tok(128), tok(128), tok(512), slab, slab, slab, tok(512), pl.BlockSpec((1, 512), lambda i: (0, 0)),
            table(rope_hi), table(rope_lo)]


N_DPROJ = 11
GW_ROWS = 768


def _proj_bwd_w(x, dproj_args, rope_hi, rope_lo, pieces, tb):
    seq = x.shape[0]
    steps = seq // tb
    n_p = len(pieces)
    narrow = [False] * n_p

    def body(*refs):
        x_ref, grads = refs[0], refs[1:1 + N_DPROJ]
        piece_refs = refs[1 + N_DPROJ:1 + N_DPROJ + n_p]
        gw_ref = refs[1 + N_DPROJ + n_p]
        out_refs = refs[2 + N_DPROJ + n_p:2 + N_DPROJ + 2 * n_p]
        step = pl.program_id(0)
        if n_p:
            begin, exchange, combine, end = _ride_phases(piece_refs, out_refs, refs[2 + N_DPROJ + 2 * n_p:], narrow, True)
            pl.when(step == 0)(begin)
            pl.when(step == min(1, steps - 1))(exchange)
            pl.when(step == steps // 2)(combine)

        @pl.when(step == 0)
        def _():
            gw_ref[...] = jnp.zeros_like(gw_ref)

        dproj = _dproj_block(*grads, tb)
        xb = x_ref[...].astype(BF16)
        for r0 in range(0, D_IN_PROJ, GW_ROWS):
            gw_ref[r0:r0 + GW_ROWS, :] += _dot_tn(dproj[:, r0:r0 + GW_ROWS], xb)
        if n_p:
            pl.when(step == steps - 1)(end)

    vmem = pl.BlockSpec(memory_space=pltpu.VMEM)
    whole = pl.BlockSpec((D_IN_PROJ, D_MODEL), lambda i: (0, 0), pipeline_mode=pl.Buffered(1))
    ride_outs, ride_scratch = _ride_shapes(pieces, narrow, True) if n_p else ([], [])
    return pl.pallas_call(
        body, name="proj_bwd_w", grid=(steps,),
        in_specs=[pl.BlockSpec((tb, D_MODEL), lambda i: (i, 0))] + _dproj_specs(tb, rope_hi, rope_lo) + [vmem] * n_p,
        out_specs=[whole] + [vmem] * n_p,
        out_shape=[jax.ShapeDtypeStruct((D_IN_PROJ, D_MODEL), F32)] + [jax.ShapeDtypeStruct(s, F32) for s in ride_outs],
        scratch_shapes=ride_scratch,
        compiler_params=_cparams(("arbitrary",)),
    )(x, *dproj_args, rope_hi, rope_lo, *pieces)


def _proj_bwd_x(dproj_args, rope_hi, rope_lo, dpre, wt, pieces, tb):
    seq = dpre.shape[0]
    steps = seq // tb
    n_p = len(pieces)
    narrow = [True] * n_p

    def body(*refs):
        grads = refs[:N_DPROJ]
        dpre_ref, wt_ref = refs[N_DPROJ:N_DPROJ + 2]
        piece_refs = refs[N_DPROJ + 2:N_DPROJ + 2 + n_p]
        gx_ref = refs[N_DPROJ + 2 + n_p]
        out_refs = refs[N_DPROJ + 3 + n_p:N_DPROJ + 3 + 2 * n_p]
        step = pl.program_id(0)
        if n_p:
            begin, exchange, combine, end = _ride_phases(piece_refs, out_refs, refs[N_DPROJ + 3 + 2 * n_p:], narrow, False)
            pl.when(step == 0)(begin)
            pl.when(step == min(1, steps - 1))(exchange)
            pl.when(step == steps - 1)(combine)

        dproj = _dproj_block(*grads, tb)
        gx_ref[...] = ALPHA * dpre_ref[...] + _dot(dproj, wt_ref[...])
        if n_p:
            pl.when(step == steps - 1)(end)

    vmem = pl.BlockSpec(memory_space=pltpu.VMEM)
    whole = pl.BlockSpec((D_IN_PROJ, D_MODEL), lambda i: (0, 0), pipeline_mode=pl.Buffered(1))
    ride_outs, ride_scratch = _ride_shapes(pieces, narrow, False) if n_p else ([], [])
    return pl.pallas_call(
        body, name="proj_bwd_x", grid=(steps,),
        in_specs=_dproj_specs(tb, rope_hi, rope_lo) + [pl.BlockSpec((tb, D_MODEL), lambda i: (i, 0)), whole] + [vmem] * n_p,
        out_specs=[pl.BlockSpec((tb, D_MODEL), lambda i: (i, 0))] + [vmem] * n_p,
        out_shape=[jax.ShapeDtypeStruct((seq, D_MODEL), F32)] + [jax.ShapeDtypeStruct(s, F32) for s in ride_outs],
        scratch_shapes=ride_scratch,
        compiler_params=pltpu.CompilerParams(dimension_semantics=("arbitrary",), vmem_limit_bytes=PROJ_BWD_X_VMEM),
    )(*dproj_args, rope_hi, rope_lo, dpre, wt, *pieces)


def _adamw(w, g, m, v, name):
    rows, cols = w.shape
    tb = rows
    while tb * cols * 4 > ADAMW_BLOCK_BYTES and tb % 16 == 0:
        tb //= 2

    def body(w_ref, g_ref, m_ref, v_ref, d_ref, nm_ref, nv_ref):
        _adamw_update(w_ref, g_ref, m_ref, v_ref, d_ref, nm_ref, nv_ref)

    spec = pl.BlockSpec((tb, cols), lambda i: (i, 0))
    return pl.pallas_call(
        body, name=name, grid=(rows // tb,), in_specs=[spec] * 4, out_specs=[spec] * 3,
        out_shape=[jax.ShapeDtypeStruct((rows, cols), F32)] * 3,
        compiler_params=_cparams(("arbitrary",)),
    )(w, g, m, v)


def _adamw_update(w_ref, g_ref, m_ref, v_ref, d_ref, nm_ref, nv_ref):
    g_blk = g_ref[...]
    m_new = ADAM_B1 * m_ref[...] + (1.0 - ADAM_B1) * g_blk
    v_new = ADAM_B2 * v_ref[...] + (1.0 - ADAM_B2) * (g_blk * g_blk)
    m_hat = m_new / (1.0 - ADAM_B1 ** ADAM_STEP)
    v_hat = v_new / (1.0 - ADAM_B2 ** ADAM_STEP)
    d_ref[...] = -ADAM_LR * (m_hat / (jnp.sqrt(v_hat) + ADAM_EPS) + ADAM_WD * w_ref[...])
    nm_ref[...] = m_new
    nv_ref[...] = v_new


def _adamw_many(groups, name):
    n = len(groups)

    def body(*refs):
        for p in range(n):
            w_ref, g_ref, m_ref, v_ref = refs[4 * p:4 * p + 4]
            gn_ref, d_ref, nm_ref, nv_ref = refs[4 * n + 4 * p:4 * n + 4 * p + 4]
            gn_ref[...] = g_ref[...].reshape(w_ref.shape)
            _adamw_update(w_ref, gn_ref, m_ref, v_ref, d_ref, nm_ref, nv_ref)

    return pl.pallas_call(
        body, name=name,
        out_shape=[jax.ShapeDtypeStruct(grp[0].shape, F32) for grp in groups for _ in range(4)],
    )(*[a for grp in groups for a in grp])


_WEIGHTS = ["w_in", "attn_sink", "ssm_a_re", "ssm_a_im", "ssm_log_dt", "ssm_b_re", "ssm_b_im", "ssm_c_re", "ssm_c_im",
            "ssm_d", "w_glu", "b_glu", "norm_attn_g", "norm_ssm_g", "w_out", "ln_g", "ln_b"]
N_DG = N_DIR * N_GROUPS
BIG_ROWS = N_DG * SSM_CH * SSM_STATE // 128
TINY_ROWS = 64


def _pack_small_grads(g_bc, g_vec, g_ar, g_ai, g_dt, g_sink, loss):
    big = jnp.stack([t.reshape(BIG_ROWS, 128) for t in g_bc])
    row = lambda t: jnp.pad(t.reshape(1, -1), ((0, 0), (0, 128 - t.size)))
    tiny = jnp.concatenate([g_vec.reshape(64, 128), g_ar.reshape(32, 128), g_ai.reshape(32, 128), row(g_dt), row(g_sink),
                            row(loss), jnp.zeros((N_CHIPS * TINY_ROWS - 131, 128), F32)], axis=0)
    return jnp.concatenate([big, tiny.reshape(N_CHIPS, TINY_ROWS, 128)], axis=1)


def _unpack_small_grads(packed):
    big = packed[:, :BIG_ROWS].reshape(N_CHIPS, 2 * BIG_ROWS, SSM_STATE)
    tiny = packed[:, BIG_ROWS:].reshape(N_CHIPS * TINY_ROWS, 128)
    g_vec = tiny[0:64].reshape(8, 1024)
    return tiny[130, 0], {
        "ssm_b_re": big[0], "ssm_b_im": big[1], "ssm_c_re": big[2], "ssm_c_im": big[3],
        "ln_g": g_vec[0:1], "ln_b": g_vec[1:2],
        "norm_attn_g": _from_pair_order(g_vec[2:3, :D_ATTN]), "norm_ssm_g": g_vec[2:3, D_ATTN:],
        "ssm_d": g_vec[3:4, :D_SSM], "b_glu": g_vec[3:4, D_SSM:],
        "ssm_a_re": tiny[64:96].reshape(N_DG, SSM_STATE), "ssm_a_im": tiny[96:128].reshape(N_DG, SSM_STATE),
        "ssm_log_dt": tiny[128:129, :N_DG].reshape(N_DIR, N_GROUPS), "attn_sink": tiny[129:130, :N_Q_HEADS],
    }


def _small_unview(name, t, shape):
    if name in ("ssm_b_re", "ssm_b_im"):
        return jnp.swapaxes(t.reshape(N_DIR, N_GROUPS, SSM_CH, SSM_STATE), 2, 3).reshape(shape)
    return t.reshape(shape)


def _channel_major(name, t):
    return jnp.swapaxes(t, 3, 4) if name in ("ssm_b_re", "ssm_b_im") else t


def kernel(x, w_in, attn_sink, ssm_a_re, ssm_a_im, ssm_log_dt, ssm_b_re, ssm_b_im, ssm_c_re, ssm_c_im, ssm_d, w_glu, b_glu, norm_attn_g, norm_ssm_g, w_out, ln_g, ln_b, loss_target, m_w_in, m_attn_sink, m_ssm_a_re, m_ssm_a_im, m_ssm_log_dt, m_ssm_b_re, m_ssm_b_im, m_ssm_c_re, m_ssm_c_im, m_ssm_d, m_w_glu, m_b_glu, m_norm_attn_g, m_norm_ssm_g, m_w_out, m_ln_g, m_ln_b, v_w_in, v_attn_sink, v_ssm_a_re, v_ssm_a_im, v_ssm_log_dt, v_ssm_b_re, v_ssm_b_im, v_ssm_c_re, v_ssm_c_im, v_ssm_d, v_w_glu, v_b_glu, v_norm_attn_g, v_norm_ssm_g, v_w_out, v_ln_g, v_ln_b):
    args = dict(locals())
    weights = {n: args[n] for n in _WEIGHTS}
    mom_m = {n: args["m_" + n] for n in _WEIGHTS}
    mom_v = {n: args["v_" + n] for n in _WEIGHTS}
    xs = x[0]
    target = loss_target[0]

    (wt_g,) = _all_gather_chips([w_in[0].T], BF16, "gather_weights")
    wt_full = wt_g.reshape(D_IN_PROJ, D_MODEL)

    g_x, r_wt, r_w_out, r_w_glu, g_small_all = _local_step(
        xs, target, wt_full, w_glu[0], w_out[0], attn_sink, ssm_a_re, ssm_a_im, ssm_log_dt, ssm_b_re, ssm_b_im,
        ssm_c_re, ssm_c_im, ssm_d, b_glu, norm_attn_g, norm_ssm_g, ln_g, ln_b, sharded=True)
    loss, small_grads = _unpack_small_grads(g_small_all)

    grads, deltas, new_m, new_v = {}, {}, {}, {}
    d_w, m_w, v_w = _adamw(w_in[0].T, r_wt, m_w_in[0].T, v_w_in[0].T, "adamw_w_in")
    grads["w_in"], deltas["w_in"], new_m["w_in"], new_v["w_in"] = r_wt.T[None], d_w.T[None], m_w.T[None], v_w.T[None]
    for n, g in (("w_out", r_w_out), ("w_glu", r_w_glu)):
        d_w, m_w, v_w = _adamw(weights[n][0], g, mom_m[n][0], mom_v[n][0], "adamw_" + n)
        grads[n], deltas[n], new_m[n], new_v[n] = g[None], d_w[None], m_w[None], v_w[None]
    names = sorted(small_grads)
    updates = _adamw_many([(_channel_major(n, weights[n]), small_grads[n], _channel_major(n, mom_m[n]),
                            _channel_major(n, mom_v[n])) for n in names], "adamw_small")
    for i, n in enumerate(names):
        grads[n], deltas[n], new_m[n], new_v[n] = (_channel_major(n, t) for t in updates[4 * i:4 * i + 4])

    return (loss, g_x[None], *[grads[n] for n in _WEIGHTS], *[deltas[n] for n in _WEIGHTS],
            *[new_m[n] for n in _WEIGHTS], *[new_v[n] for n in _WEIGHTS])


def _local_step(xs, target, wt_full, w_glu_in, w_out_in, attn_sink, ssm_a_re, ssm_a_im, ssm_log_dt, ssm_b_re,
                ssm_b_im, ssm_c_re, ssm_c_im, ssm_d, b_glu, norm_attn_g, norm_ssm_g, ln_g, ln_b, sharded):
    seq = xs.shape[0]

    a_r, a_i = ssm_a_re, ssm_a_im
    log_dt = ssm_log_dt.reshape(N_DG, 1)
    b_r, b_i = _channel_major("ssm_b_re", ssm_b_re), _channel_major("ssm_b_im", ssm_b_im)
    c_r, c_i = ssm_c_re, ssm_c_im
    ssm_tb = min(SSM_BLOCK, seq)
    sub_len = ssm_tb // SUBSEG
    lam, bb, bbt, cb, cb_t = _ssm_params_fwd(a_r, a_i, log_dt, b_r, b_i, c_r, c_i, int(math.log2(sub_len)))
    lam = lam.reshape(4, N_DIR, 1, STATE_W)

    rope_hi, rope_lo = _rope_tables(seq)
    projected = _proj(xs, wt_full, rope_hi, rope_lo, [w_glu_in, w_out_in] if sharded else [], min(512, seq))
    q_stack, k_rot, v_bf, z_attn, u, z_ssm = projected[:6]
    if sharded:
        w_glu_full, w_out_full = projected[6].reshape(D_SSM, D_SSM), projected[7].reshape(D_MODEL, D_MODEL)
    else:
        w_glu_full, w_out_full = w_glu_in, w_out_in
    sink128 = jnp.broadcast_to(attn_sink[0][:, None, None], (N_Q_HEADS, 1, 128))
    attn_bias = _attn_bias()
    o = _attn_fwd(q_stack, k_rot, v_bf, sink128, attn_bias)
    ys, starts = [], []
    for d in range(N_DIR):
        y_d, s_r, s_i = _ssm_fwd(u, lam, bb, cb, direction=d, tb=ssm_tb, name=f"ssm_fwd_{d}")
        ys.append(y_d)
        starts.append((s_r, s_i))

    row = lambda t: t.reshape(1, -1)
    g_attn_p = _to_pair_order(norm_attn_g)
    loss_blk, d_o, d_za, d_ylin, d_zs, d_pre, g_w_out, g_w_glu, g_vec = _mid(
        o, z_attn, u, ys[0], ys[1], z_ssm, xs, target, row(ssm_d), w_glu_full, row(b_glu),
        g_attn_p, row(norm_ssm_g), w_out_full, row(ln_g), row(ln_b), min(256, seq))

    pieces = [g_w_glu.reshape(N_CHIPS, -1, D_SSM), g_w_out.reshape(N_CHIPS, -1, D_MODEL)] if sharded else []
    attn_grads = _attn_bwd(q_stack, k_rot, v_bf, sink128, attn_bias, d_o, pieces)
    dq, dk, dv, g_sink = attn_grads[:4]
    if sharded:
        g_w_glu, g_w_out = attn_grads[4:]
    dus, g_bb, g_cb, g_lam = [], [], [], []
    for d in range(N_DIR):
        du_d, gb_d, gc_d, dl_d = _ssm_bwd(u, d_ylin, starts[d], lam, bb, bbt, cb_t, direction=d, tb=ssm_tb,
                                          name=f"ssm_bwd_{d}")
        dus.append(du_d)
        g_bb.append(gb_d)
        g_cb.append(gc_d)
        g_lam.append(dl_d)
    g_ar, g_ai, g_dt, g_br, g_bi, g_cr, g_ci = _ssm_params_bwd(a_r, a_i, log_dt, b_r, b_i, g_bb, g_cb, g_lam)

    g_small = _pack_small_grads([g_br, g_bi, g_cr, g_ci], g_vec, g_ar, g_ai, g_dt, g_sink[:, 0], loss_blk[0, 0])
    dproj_args = (dq, dk, dv, d_za, dus[0], dus[1], d_ylin, d_zs, row(ssm_d))
    w_grads = _proj_bwd_w(xs, dproj_args, rope_hi, rope_lo, [g_small] if sharded else [], min(512, seq))
    g_wt = w_grads[0]
    if sharded:
        g_small = w_grads[1]
    x_grads = _proj_bwd_x(dproj_args, rope_hi, rope_lo, d_pre, wt_full,
                          [g_wt.reshape(N_CHIPS, -1, D_MODEL)] if sharded else [], min(512, seq))
    g_x = x_grads[0]
    if sharded:
        g_wt = x_grads[1]
    return g_x, g_wt, g_w_out, g_w_glu, g_small
```

```python
import functools
import math

import numpy as np
import jax
import jax.numpy as jnp
from jax import lax
from jax.experimental import pallas as pl
from jax.experimental.pallas import tpu as pltpu

F32 = jnp.float32
BF16 = jnp.bfloat16
MESH = pl.DeviceIdType.MESH

D_MODEL = 1024
D_ATTN = 512
D_SSM = 512
HEAD_DIM = 64
N_Q_HEADS = 8
WINDOW = 128
ROPE_THETA = 10000.0
SSM_CH = 16
N_GROUPS = 32
SSM_STATE = 64
N_DIR = 2
STATE_W = N_GROUPS * SSM_STATE
N_SLAB = 4
SLAB_IN = 128
SLAB_ST = 512
NORM_EPS = 1e-5
NEG_INF = -1e30
ALPHA = 2.0 ** 0.25
D_IN_PROJ = 2304
N_CHIPS = 4

ADAM_LR = 0.001
ADAM_B1 = 0.9
ADAM_B2 = 0.999
ADAM_EPS = 1e-08
ADAM_WD = 0.01
ADAM_STEP = 10

SUBSEG = 8
SCAN_LANES = 512
SSM_BLOCK = 512
VMEM_LIMIT = 48 * 1024 * 1024
ADAMW_BLOCK_BYTES = 3 * 512 * 1024
PROJ_BWD_X_VMEM = 56 * 1024 * 1024

def _to_pair_order(row):
    return jnp.transpose(row.reshape(2, 4, HEAD_DIM), (1, 0, 2)).reshape(1, D_ATTN)


def _from_pair_order(row):
    return jnp.transpose(row.reshape(4, 2, HEAD_DIM), (1, 0, 2)).reshape(1, D_ATTN)


def _cparams(sem=None):
    return pltpu.CompilerParams(dimension_semantics=sem, vmem_limit_bytes=VMEM_LIMIT)


def _dot(a, b):
    return jnp.dot(a, b, preferred_element_type=F32)


def _dot_nt(a, b):
    return lax.dot_general(a, b, (((1,), (1,)), ((), ())), preferred_element_type=F32)


def _dot_tn(a, b):
    return lax.dot_general(a, b, (((0,), (0,)), ((), ())), preferred_element_type=F32)


def _sigmoid(z):
    return 0.5 * jnp.tanh(0.5 * z) + 0.5


def _all_gather_chips(shards, out_dtype, name):
    n = len(shards)

    def body(*refs):
        start, relay, finish = _gather_phases(refs[:n], refs[n:2 * n], *refs[2 * n:], out_dtype)
        start()
        relay()
        finish()

    vmem = pl.BlockSpec(memory_space=pltpu.VMEM)
    return pl.pallas_call(
        body, name=name,
        out_shape=[jax.ShapeDtypeStruct((N_CHIPS,) + s.shape, out_dtype) for s in shards],
        in_specs=[vmem] * n, out_specs=[vmem] * n,
        scratch_shapes=_gather_sems(n),
        compiler_params=pltpu.CompilerParams(vmem_limit_bytes=VMEM_LIMIT),
    )(*shards)


def _gather_sems(n):
    return [pltpu.SemaphoreType.DMA((6 * n,)), pltpu.SemaphoreType.DMA((6 * n,))]


def _gather_phases(in_refs, out_refs, send_sems, recv_sems, out_dtype):
    n = len(in_refs)
    x, y, c = lax.axis_index("x"), lax.axis_index("y"), lax.axis_index("c")
    sibling = (x, y, 1 - c)
    chips = [(1 - x, y), (x, 1 - y), (1 - x, 1 - y)]

    def half_of(a, px, py, half):
        rows = in_refs[a].shape[0] // 2
        return out_refs[a].at[2 * px + py, pl.ds(half * rows, rows), :]

    def copy(a, k, px, py, half, to):
        blk = half_of(a, px, py, half)
        return pltpu.make_async_remote_copy(src_ref=blk, dst_ref=blk, send_sem=send_sems.at[6 * a + k],
                                            recv_sem=recv_sems.at[6 * a + k], device_id=to, device_id_type=MESH)

    first = [copy(a, j, x, y, c, (*chips[j], c)) for a in range(n) for j in range(3)]
    passed = [copy(a, 3 + j, *chips[j], c, sibling) for a in range(n) for j in range(3)]

    def start():
        for a in range(n):
            out_refs[a][2 * x + y] = in_refs[a][...].astype(out_dtype)
        for cp in first:
            cp.start()

    def relay():
        for a in range(n):
            for j in range(3):
                copy(a, j, *chips[j], c, (x, y, c)).wait_recv()
                passed[3 * a + j].start()

    def finish():
        for a in range(n):
            for j in range(3):
                copy(a, 3 + j, *chips[j], 1 - c, (x, y, c)).wait_recv()
        for cp in first + passed:
            cp.wait_send()

    return start, relay, finish


SEMS_PER_ARRAY = 14


def _reduce_scratch(shapes, narrow):
    half = [(N_CHIPS, s[1] // 2, s[2]) for s in shapes]
    wire = [BF16 if nar else F32 for nar in narrow]
    n = len(shapes)
    return ([pltpu.VMEM(half[a], F32) for a in range(n)] + [pltpu.VMEM(half[a], wire[a]) for a in range(n)]
            + [pltpu.VMEM(half[a], wire[a]) for a in range(n)]
            + [pltpu.SemaphoreType.DMA((SEMS_PER_ARRAY * n,)), pltpu.SemaphoreType.DMA((SEMS_PER_ARRAY * n,))])


def _reduce_phases(p_refs, out_refs, a_refs, s_refs, b_refs, send_sems, recv_sems, narrow, gather_last):
    n = len(p_refs)
    halves = [p.shape[1] // 2 for p in p_refs]
    wire = [BF16 if nar else F32 for nar in narrow]
    x, y, c = lax.axis_index("x"), lax.axis_index("y"), lax.axis_index("c")
    me = 2 * x + y
    sibling = (x, y, 1 - c)
    chips = [(1 - x, y), (x, 1 - y), (1 - x, 1 - y)]
    slot = [2 * px + py for px, py in chips]
    last = n - 1

    def copy(a, k, src, dst, to):
        return pltpu.make_async_remote_copy(src_ref=src, dst_ref=dst, send_sem=send_sems.at[SEMS_PER_ARRAY * a + k],
                                            recv_sem=recv_sems.at[SEMS_PER_ARRAY * a + k],
                                            device_id=to, device_id_type=MESH)

    def rows(a, half):
        return pl.ds(pl.multiple_of(half * halves[a], 16), halves[a])

    def finished(a, k, half):
        if gather_last and a == last:
            return out_refs[a].at[k, rows(a, half), :]
        return out_refs[a].at[rows(a, half), :]

    order = slot + [me]
    swaps = [[copy(a, q, p_refs[a].at[order[q], rows(a, 1 - c), :], a_refs[a].at[order[q]], sibling)
              for q in range(N_CHIPS)] for a in range(n)]
    sends = [[copy(a, 4 + j, s_refs[a].at[slot[j]], b_refs[a].at[me], (*chips[j], c)) for j in range(3)] for a in range(n)]
    backs = [copy(a, 7, finished(a, me, c), finished(a, me, c), sibling) for a in range(n)]
    spread = [copy(last, 8 + j, finished(last, me, c), finished(last, me, c), (*chips[j], c)) for j in range(3)]
    relays = [copy(last, 11 + j, finished(last, slot[j], c), finished(last, slot[j], c), sibling) for j in range(3)]

    def start():
        for group in swaps:
            for cp in group:
                cp.start()

    def exchange():
        for a in range(n):
            for q in range(N_CHIPS):
                swaps[a][q].wait_recv()
                acc = a_refs[a][order[q]] + p_refs[a][order[q], rows(a, c), :]
                a_refs[a][order[q]] = acc
                s_refs[a][order[q]] = acc.astype(wire[a])
                if q < 3:
                    sends[a][q].start()
            b_refs[a][me] = s_refs[a][me]

    def combine():
        for a in range(n):
            for j in range(3):
                copy(a, 4 + j, s_refs[a].at[slot[j]], b_refs[a].at[slot[j]], (x, y, c)).wait_recv()
            terms = [jnp.where(me == k, a_refs[a][k], b_refs[a][k].astype(F32)) for k in range(N_CHIPS)]
            total = (terms[0] + terms[1]) + (terms[2] + terms[3])
            if gather_last and a == last:
                out_refs[a][me, rows(a, c), :] = total
            else:
                out_refs[a][rows(a, c), :] = total
            backs[a].start()
        if gather_last:
            for cp in spread:
                cp.start()

    def finish():
        if gather_last:
            for j in range(3):
                copy(last, 8 + j, finished(last, slot[j], c), finished(last, slot[j], c), (x, y, c)).wait_recv()
                relays[j].start()
        for a in range(n):
            copy(a, 7, finished(a, me, 1 - c), finished(a, me, 1 - c), (x, y, c)).wait_recv()
        if gather_last:
            for j in range(3):
                copy(last, 11 + j, finished(last, slot[j], 1 - c), finished(last, slot[j], 1 - c), (x, y, c)).wait_recv()
        started = [cp for group in swaps + sends for cp in group] + backs + (spread + relays if gather_last else [])
        for cp in started:
            cp.wait_send()

    return start, exchange, combine, finish


def _ssm_param_values(ar, ai, logdt):
    dt = jnp.exp(logdt)
    mag = jnp.exp(dt * ar)
    cs, sn = jnp.cos(dt * ai), jnp.sin(dt * ai)
    lr, li = mag * cs, mag * sn
    den = ar * ar + ai * ai
    nr = (lr - 1.0) * ar + li * ai
    ni = li * ar - (lr - 1.0) * ai
    return dt, mag, lr, li, den, nr, ni


GROUPS_PER_SLAB = N_GROUPS // N_SLAB


def _slab_masks():
    def eq(shape, f_row, f_col):
        return (f_row(lax.broadcasted_iota(jnp.int32, shape, 0)) == f_col(lax.broadcasted_iota(jnp.int32, shape, 1))).astype(F32)
    spread = eq((SSM_STATE, SLAB_ST), lambda r: r, lambda c: c % SSM_STATE)
    spread_t = eq((SLAB_ST, SSM_STATE), lambda r: r % SSM_STATE, lambda c: c)
    keep = eq((SLAB_IN, SLAB_ST), lambda r: r // SSM_CH, lambda c: c // SSM_STATE)
    keep_t = eq((SLAB_ST, SLAB_IN), lambda r: r // SSM_STATE, lambda c: c // SSM_CH)
    repeat = eq((N_DG * SSM_CH, N_DG), lambda r: r // SSM_CH, lambda c: c)
    return spread, spread_t, keep, keep_t, repeat


def _rows(ref):
    return ref[...].reshape(-1, SSM_STATE)


def _split3(t):
    hi = t.astype(BF16)
    rest = t - hi.astype(F32)
    mid = rest.astype(BF16)
    return hi, mid, (rest - mid.astype(F32)).astype(BF16)


def _select(dot, ones01, t, ones_first):
    o = ones01.astype(BF16)
    parts = [dot(o, p) if ones_first else dot(p, o) for p in _split3(t)]
    return (parts[0] + parts[1]) + parts[2]


def _ssm_params_fwd(ar, ai, logdt, br, bi, cr, ci, n_square):
    def body(ar_ref, ai_ref, dt_ref, br_ref, bi_ref, cr_ref, ci_ref, lam_ref, bb_ref, bbt_ref, cb_ref, cbt_ref):
        _, _, lr, li, den, nr, ni = _ssm_param_values(_rows(ar_ref), _rows(ai_ref), dt_ref[...])
        lam_ref[0] = lr
        lam_ref[1] = li
        pr, pi = lr, li
        for _ in range(n_square):
            pr, pi = pr * pr - pi * pi, 2.0 * pr * pi
        lam_ref[2] = pr
        lam_ref[3] = pi
        spread, spread_t, keep, keep_t, repeat = _slab_masks()
        fr = _select(_dot, repeat, nr / den, True)
        fi = _select(_dot, repeat, ni / den, True)
        b_r, b_i = _rows(br_ref), _rows(bi_ref)
        bbar = (fr * b_r - fi * b_i, fr * b_i + fi * b_r)
        c_par = (_rows(cr_ref), _rows(ci_ref))
        spread, spread_t = spread.astype(BF16), spread_t.astype(BF16)
        for src, wide_ref, tall_ref in ((bbar, bb_ref, bbt_ref), (c_par, cbt_ref, cb_ref)):
            for q in range(2):
                for d in range(N_DIR):
                    for k in range(N_SLAB):
                        r0 = (d * N_GROUPS + k * GROUPS_PER_SLAB) * SSM_CH
                        blk = src[q][r0:r0 + SLAB_IN].astype(BF16)
                        wide_ref[q, d, k] = (_dot(blk, spread) * keep).astype(BF16)
                        tall_ref[q, d, k] = (_dot_nt(spread_t, blk) * keep_t).astype(BF16)

    wide = jax.ShapeDtypeStruct((2, N_DIR, N_SLAB, SLAB_IN, SLAB_ST), BF16)
    tall = jax.ShapeDtypeStruct((2, N_DIR, N_SLAB, SLAB_ST, SLAB_IN), BF16)
    return pl.pallas_call(body, name="ssm_params_fwd",
                          out_shape=[jax.ShapeDtypeStruct((4, N_DG, SSM_STATE), F32), wide, tall, tall, wide],
                          compiler_params=pltpu.CompilerParams(vmem_limit_bytes=VMEM_LIMIT),
                          )(ar, ai, logdt, br, bi, cr, ci)


def _ssm_params_bwd(ar, ai, logdt, br, bi, g_slabs_b, g_slabs_c, g_lam):
    def body(ar_ref, ai_ref, dt_ref, br_ref, bi_ref, gb0_ref, gb1_ref, gc0_ref, gc1_ref, gl0_ref, gl1_ref,
             gar_ref, gai_ref, gdt_ref, gbr_ref, gbi_ref, gcr_ref, gci_ref, dbb, dlam):
        spread, spread_t, keep, keep_t, repeat = _slab_masks()
        for d, (gb_ref, gc_ref) in enumerate(((gb0_ref, gc0_ref), (gb1_ref, gc1_ref))):
            for q in range(2):
                for k in range(N_SLAB):
                    r0 = (d * N_GROUPS + k * GROUPS_PER_SLAB) * SSM_CH
                    dbb[q, r0:r0 + SLAB_IN, :] = _select(_dot, spread_t, gb_ref[q, k] * keep, False)
                    out_ref = gcr_ref if q == 0 else gci_ref
                    out_ref[r0:r0 + SLAB_IN, :] = _select(_dot_tn, spread_t, gc_ref[q, k] * keep_t, False)
        grp = (lax.broadcasted_iota(jnp.int32, (N_GROUPS, STATE_W), 0)
               == lax.broadcasted_iota(jnp.int32, (N_GROUPS, STATE_W), 1) // SSM_STATE).astype(F32)
        pick = (lax.broadcasted_iota(jnp.int32, (STATE_W, SSM_STATE), 0) % SSM_STATE
                == lax.broadcasted_iota(jnp.int32, (STATE_W, SSM_STATE), 1)).astype(F32)
        for d, gl_ref in enumerate((gl0_ref, gl1_ref)):
            for q in range(2):
                row = jnp.sum(gl_ref[q], axis=0, keepdims=True)
                dlam[q, d * N_GROUPS:(d + 1) * N_GROUPS, :] = _select(_dot, pick, grp * row, False)

        a_r, a_i = _rows(ar_ref), _rows(ai_ref)
        dt, mag, lr, li, den, nr, ni = _ssm_param_values(a_r, a_i, dt_ref[...])
        fr = _select(_dot, repeat, nr / den, True)
        fi = _select(_dot, repeat, ni / den, True)
        b_r, b_i = _rows(br_ref), _rows(bi_ref)
        g_r, g_i = dbb[0], dbb[1]
        gbr_ref[...] = fr * g_r + fi * g_i
        gbi_ref[...] = fr * g_i - fi * g_r
        d_fr = _select(_dot_tn, repeat, b_r * g_r + b_i * g_i, True)
        d_fi = _select(_dot_tn, repeat, b_r * g_i - b_i * g_r, True)
        d_nr, d_ni = d_fr / den, d_fi / den
        d_den = -(d_fr * nr + d_fi * ni) / (den * den)
        d_lr = dlam[0] + d_nr * a_r - d_ni * a_i
        d_li = dlam[1] + d_nr * a_i + d_ni * a_r
        d_ar = d_nr * (lr - 1.0) + d_ni * li + d_den * 2.0 * a_r
        d_ai = d_nr * li - d_ni * (lr - 1.0) + d_den * 2.0 * a_i
        d_mag = (d_lr * lr + d_li * li) / mag
        d_theta = d_li * lr - d_lr * li
        gar_ref[...] = d_ar + d_mag * mag * dt
        gai_ref[...] = d_ai + d_theta * dt
        d_dt = d_mag * mag * a_r + d_theta * a_i
        gdt_ref[...] = jnp.sum(d_dt, axis=1, keepdims=True) * dt

    small = jax.ShapeDtypeStruct((N_DG, SSM_STATE), F32)
    big = jax.ShapeDtypeStruct((N_DG * SSM_CH, SSM_STATE), F32)
    return pl.pallas_call(
        body, name="ssm_params_bwd",
        out_shape=[small, small, jax.ShapeDtypeStruct(logdt.shape, F32), big, big, big, big],
        scratch_shapes=[pltpu.VMEM((2,) + big.shape, F32), pltpu.VMEM((2,) + small.shape, F32)],
        compiler_params=pltpu.CompilerParams(vmem_limit_bytes=VMEM_LIMIT),
    )(ar, ai, logdt, br, bi, *g_slabs_b, *g_slabs_c, *g_lam)


ROPE_GROUP = 128


def _rope_tables(seq):
    half = HEAD_DIM // 2
    inv_freq = jnp.tile(ROPE_THETA ** (-jnp.arange(half, dtype=F32) / half), 4)
    sign = jnp.tile(jnp.concatenate([-jnp.ones((half,), F32), jnp.ones((half,), F32)]), 2)

    def table(pos):
        ang = pos.astype(F32)[:, None] * inv_freq[None, :]
        return jnp.stack([jnp.cos(ang), jnp.sin(ang), sign * jnp.sin(ang)])

    return table(jnp.arange(seq // ROPE_GROUP) * ROPE_GROUP), table(jnp.arange(ROPE_GROUP))


def _rope_block(hi_ref, lo_ref, first_group, n_groups):
    cl, sl, sl_s = lo_ref[0], lo_ref[1], lo_ref[2]
    cos, sin = [], []
    for g in range(n_groups):
        ch, sh, sh_s = (hi_ref[q, pl.ds(first_group + g, 1), :] for q in range(3))
        cos.append(ch * cl - sh * sl)
        sin.append(sh_s * cl + ch * sl_s)
    return jnp.concatenate(cos, axis=0), jnp.concatenate(sin, axis=0)


def _rotate_half_unsigned(t):
    lane = lax.broadcasted_iota(jnp.int32, t.shape, 1)
    return jnp.where((lane % HEAD_DIM) < HEAD_DIM // 2, pltpu.roll(t, 96, 1), pltpu.roll(t, 32, 1))


def _rope(t, cos, sin_signed):
    return t * cos + _rotate_half_unsigned(t) * sin_signed


def _pair_blocks(base):
    out = []
    for j in range(4):
        for g in range(2):
            nat = base + HEAD_DIM * (4 * g + j)
            par = base + 128 * j + HEAD_DIM * g
            out.append((slice(nat, nat + HEAD_DIM), slice(par, par + HEAD_DIM)))
    return out


W_Q, W_KV, W_ZA, W_U, W_ZS = 0, 512, 768, 1280, 1792


def _proj(x, wt, rope_hi, rope_lo, shards, tb):
    seq = x.shape[0]
    steps = seq // tb
    n_sh = len(shards)

    def body(*refs):
        x_ref, wt_ref, hi_ref, lo_ref = refs[:4]
        shard_refs = refs[4:4 + n_sh]
        q_ref, k_ref, v_ref, za_ref, u_ref, zs_ref = refs[4 + n_sh:10 + n_sh]
        gathered_refs = refs[10 + n_sh:10 + 2 * n_sh]
        wp = refs[10 + 2 * n_sh]
        step = pl.program_id(0)
        if n_sh:
            landing_refs = refs[11 + 2 * n_sh:11 + 3 * n_sh]
            start, relay, finish = _gather_phases(shard_refs, landing_refs, *refs[11 + 3 * n_sh:], BF16)
            pl.when(step == 0)(start)
            pl.when(step == max(steps - 2, 0))(relay)

        @pl.when(step == 0)
        def _():
            for dst_base, src_base in ((0, W_Q), (512, W_ZA)):
                for nat, par in _pair_blocks(0):
                    wp[dst_base + par.start:dst_base + par.stop, :] = wt_ref[src_base + nat.start:src_base + nat.stop, :]

        xb = x_ref[...].astype(BF16)
        cos, sin = _rope_block(hi_ref, lo_ref, pl.program_id(0) * (tb // ROPE_GROUP), tb // ROPE_GROUP)
        lo = lax.broadcasted_iota(jnp.int32, (tb, 128), 1) < HEAD_DIM
        q = _dot_nt(xb, wp[0:512, :])
        for j in range(4):
            qj = _rope(q[:, 128 * j:128 * (j + 1)], cos, sin)
            q_ref[j] = jnp.where(lo, qj, 0.0).astype(BF16)
            q_ref[4 + j] = jnp.where(lo, 0.0, qj).astype(BF16)
        kv = _dot_nt(xb, wt_ref[W_KV:W_ZA, :])
        k_ref[...] = _rope(kv[:, 0:128], cos, sin).astype(BF16)
        v_ref[...] = kv[:, 128:256].astype(BF16)
        za_ref[...] = _dot_nt(xb, wp[512:1024, :])
        u_val = _dot_nt(xb, wt_ref[W_U:W_ZS, :])
        for k in range(N_SLAB):
            u_ref[k] = u_val[:, k * SLAB_IN:(k + 1) * SLAB_IN]
        zs_ref[...] = _dot_nt(xb, wt_ref[W_ZS:D_IN_PROJ, :])
        if n_sh:
            @pl.when(step == steps - 1)
            def _():
                finish()
                for a in range(n_sh):
                    gathered_refs[a][...] = landing_refs[a][...]

    row = lambda w: pl.BlockSpec((tb, w), lambda i: (i, 0))
    table = lambda t: pl.BlockSpec(t.shape, lambda i: (0, 0, 0))
    vmem = pl.BlockSpec(memory_space=pltpu.VMEM)
    return pl.pallas_call(
        body, name="proj", grid=(steps,),
        in_specs=[row(D_MODEL), pl.BlockSpec((D_IN_PROJ, D_MODEL), lambda i: (0, 0), pipeline_mode=pl.Buffered(1)),
                  table(rope_hi), table(rope_lo)] + [vmem] * n_sh,
        out_specs=[pl.BlockSpec((8, tb, 128), lambda i: (0, i, 0)), row(128), row(128), row(512),
                   pl.BlockSpec((N_SLAB, tb, SLAB_IN), lambda i: (0, i, 0)), row(512)] + [vmem] * n_sh,
        out_shape=[jax.ShapeDtypeStruct((8, seq, 128), BF16), jax.ShapeDtypeStruct((seq, 128), BF16),
                   jax.ShapeDtypeStruct((seq, 128), BF16), jax.ShapeDtypeStruct((seq, 512), F32),
                   jax.ShapeDtypeStruct((N_SLAB, seq, SLAB_IN), F32), jax.ShapeDtypeStruct((seq, 512), F32)]
        + [jax.ShapeDtypeStruct((N_CHIPS,) + s.shape, BF16) for s in shards],
        scratch_shapes=[pltpu.VMEM((1024, D_MODEL), BF16)] + [pltpu.VMEM((N_CHIPS,) + s.shape, BF16) for s in shards]
        + (_gather_sems(n_sh) if n_sh else []),
        compiler_params=_cparams(("arbitrary",)),
    )(x, wt, rope_hi, rope_lo, *shards)


ATT_TQ = 128
ATT_KEYS = ATT_TQ + 2 * WINDOW


def _attn_window(i, seq):
    start = jnp.clip(i * ATT_TQ - WINDOW, 0, seq - ATT_KEYS)
    return pl.multiple_of(start, WINDOW)


def _attn_bias():
    r = np.arange(ATT_TQ)[None, :, None]
    c = np.arange(ATT_KEYS)[None, None, :]
    off = np.array([0, WINDOW, ATT_KEYS - ATT_TQ])[:, None, None]
    return jnp.asarray(np.where(np.abs(r + off - c) <= WINDOW, 0.0, NEG_INF).astype(np.float32))


def _attn_bias_spec(nblk):
    pick = lambda i: jnp.where(i == 0, 0, jnp.where(i == nblk - 1, 2, 1))
    return pl.BlockSpec((None, ATT_TQ, ATT_KEYS), lambda i: (pick(i), 0, 0))


def _attn_softmax(q_ref, k_ref, v_ref, sink_ref, bias_ref, start):
    kw = k_ref[pl.ds(start, ATT_KEYS), :]
    vw = v_ref[pl.ds(start, ATT_KEYS), :]
    qall = q_ref[...].reshape(N_Q_HEADS * ATT_TQ, 128)
    s = (_dot_nt(qall, kw) * (HEAD_DIM ** -0.5)).reshape(N_Q_HEADS, ATT_TQ, ATT_KEYS) + bias_ref[...][None]
    tiles = [s[:, :, 128 * t:128 * (t + 1)] for t in range(ATT_KEYS // 128)]
    m = jnp.max(functools.reduce(jnp.maximum, tiles), axis=2, keepdims=True)
    sink = sink_ref[...]
    m_b = jnp.maximum(jnp.broadcast_to(m, (N_Q_HEADS, ATT_TQ, 128)), sink)
    p = jnp.concatenate([jnp.exp(t - m_b) for t in tiles], axis=2)
    p_sink = jnp.exp(sink - m_b)
    lo_k = lax.broadcasted_iota(jnp.int32, (ATT_KEYS, 128), 1) < HEAD_DIM
    v_f = vw.astype(F32)
    v_lo, v_hi = jnp.where(lo_k, v_f, 1.0).astype(BF16), jnp.where(lo_k, 1.0, v_f).astype(BF16)
    pb = p.astype(BF16).reshape(N_Q_HEADS * ATT_TQ, ATT_KEYS)
    half = 4 * ATT_TQ
    r = jnp.concatenate([_dot(pb[:half], v_lo), _dot(pb[half:], v_hi)], axis=0).reshape(N_Q_HEADS, ATT_TQ, 128)
    return kw, vw, qall, p, p_sink, r


def _attn_fwd(q_stack, k, v, sink128, bias):
    seq = k.shape[0]

    def body(q_ref, k_ref, v_ref, sink_ref, bias_ref, o_ref):
        start = _attn_window(pl.program_id(0), seq)
        _, _, _, _, p_sink, r = _attn_softmax(q_ref, k_ref, v_ref, sink_ref, bias_ref, start)
        out = r / (pltpu.roll(r, HEAD_DIM, 2) + p_sink)
        lo = lax.broadcasted_iota(jnp.int32, (ATT_TQ, 128), 1) < HEAD_DIM
        for j in range(4):
            o_ref[:, 128 * j:128 * (j + 1)] = jnp.where(lo, out[j], out[4 + j])

    full = lambda w: pl.BlockSpec((seq, w), lambda i: (0, 0))
    return pl.pallas_call(
        body, name="attn_fwd", grid=(seq // ATT_TQ,),
        in_specs=[pl.BlockSpec((8, ATT_TQ, 128), lambda i: (0, i, 0)), full(128), full(128),
                  pl.BlockSpec((N_Q_HEADS, 1, 128), lambda i: (0, 0, 0)), _attn_bias_spec(seq // ATT_TQ)],
        out_specs=pl.BlockSpec((ATT_TQ, 512), lambda i: (i, 0)),
        out_shape=jax.ShapeDtypeStruct((seq, 512), F32),
        compiler_params=_cparams(("arbitrary",)),
    )(q_stack, k, v, sink128, bias)


def _attn_bwd(q_stack, k, v, sink128, bias, d_o, pieces):
    seq = k.shape[0]
    steps = seq // ATT_TQ
    n_p = len(pieces)

    def body(*refs):
        q_ref, k_ref, v_ref, sink_ref, bias_ref, do_ref = refs[:6]
        piece_refs = refs[6:6 + n_p]
        dq_ref, dk_ref, dv_ref, dsink_ref = refs[6 + n_p:10 + n_p]
        reduced_refs = refs[10 + n_p:10 + 2 * n_p]
        sink_acc = refs[10 + 2 * n_p]
        i = pl.program_id(0)
        if n_p:
            landing_refs = refs[11 + 2 * n_p:11 + 3 * n_p]
            scratch = refs[11 + 3 * n_p:]
            begin, exchange, combine, finish = _reduce_phases(
                piece_refs, landing_refs, scratch[:n_p], scratch[n_p:2 * n_p], scratch[2 * n_p:3 * n_p],
                *scratch[3 * n_p:], [True] * n_p, gather_last=False)
            pl.when(i == 0)(begin)
            pl.when(i == min(4, steps - 1))(exchange)
            pl.when(i == (3 * steps) // 4)(combine)

        @pl.when(i == 0)
        def _():
            dk_ref[...] = jnp.zeros_like(dk_ref)
            dv_ref[...] = jnp.zeros_like(dv_ref)
            sink_acc[...] = jnp.zeros_like(sink_acc)

        start = _attn_window(i, seq)
        kw, vw, qall, p, p_sink, r = _attn_softmax(q_ref, k_ref, v_ref, sink_ref, bias_ref, start)
        lo = lax.broadcasted_iota(jnp.int32, (ATT_TQ, 128), 1) < HEAD_DIM
        lo3 = lo[None]
        grp0 = lax.broadcasted_iota(jnp.int32, (N_Q_HEADS, ATT_TQ, 128), 0) < 4
        val = grp0 == lo3
        swapped = pltpu.roll(r, HEAD_DIM, 2)
        inv = 1.0 / (jnp.where(val, swapped, r) + p_sink)
        d_o_blk = do_ref[...]
        do3 = jnp.where(val, jnp.concatenate([d_o_blk[None, :, 128 * j:128 * (j + 1)] for j in range(4)] * 2, axis=0), 0.0)
        t = (do3 * r).reshape(N_Q_HEADS * ATT_TQ, 128)
        t_hi = t.astype(BF16)
        t_lo = (t - t_hi.astype(F32)).astype(BF16)
        ones = jnp.ones((128, 128), BF16)
        delta = (_dot(t_hi, ones) + _dot(t_lo, ones)).reshape(N_Q_HEADS, ATT_TQ, 128) * inv
        sink_acc[...] += -(p_sink * inv) * delta
        do_all = do3.astype(BF16).reshape(N_Q_HEADS * ATT_TQ, 128)
        dp = _dot_nt(do_all, vw).reshape(N_Q_HEADS, ATT_TQ, ATT_KEYS)
        probs, ds = [], []
        for tl in range(ATT_KEYS // 128):
            cols = slice(128 * tl, 128 * (tl + 1))
            probs_t = p[:, :, cols] * inv
            probs.append(probs_t.astype(BF16))
            ds.append((probs_t * (dp[:, :, cols] - delta)).astype(BF16))
        probs_all = jnp.concatenate(probs, axis=2).reshape(N_Q_HEADS * ATT_TQ, ATT_KEYS)
        ds_all = jnp.concatenate(ds, axis=2).reshape(N_Q_HEADS * ATT_TQ, ATT_KEYS)
        scale = HEAD_DIM ** -0.5
        dq_all = (_dot(ds_all, kw) * scale).reshape(N_Q_HEADS, ATT_TQ, 128)
        for j in range(4):
            dq_ref[:, 128 * j:128 * (j + 1)] = jnp.where(lo, dq_all[j], dq_all[4 + j])
        dk_ref[pl.ds(start, ATT_KEYS), :] += _dot_tn(ds_all, qall) * scale
        dv_ref[pl.ds(start, ATT_KEYS), :] += _dot_tn(probs_all, do_all)

        @pl.when(i == steps - 1)
        def _():
            dsink_ref[...] = jnp.sum(sink_acc[...], axis=1)

        if n_p:
            @pl.when(i == steps - 1)
            def _():
                finish()
                for a in range(n_p):
                    reduced_refs[a][...] = landing_refs[a][...]

    full = lambda w: pl.BlockSpec((seq, w), lambda i: (0, 0))
    vmem = pl.BlockSpec(memory_space=pltpu.VMEM)
    return pl.pallas_call(
        body, name="attn_bwd", grid=(steps,),
        in_specs=[pl.BlockSpec((8, ATT_TQ, 128), lambda i: (0, i, 0)), full(128), full(128),
                  pl.BlockSpec((N_Q_HEADS, 1, 128), lambda i: (0, 0, 0)),
                  _attn_bias_spec(steps), pl.BlockSpec((ATT_TQ, 512), lambda i: (i, 0))] + [vmem] * n_p,
        out_specs=[pl.BlockSpec((ATT_TQ, 512), lambda i: (i, 0)), full(128), full(128),
                   pl.BlockSpec((N_Q_HEADS, 128), lambda i: (0, 0))] + [vmem] * n_p,
        out_shape=[jax.ShapeDtypeStruct((seq, 512), F32), jax.ShapeDtypeStruct((seq, 128), F32),
                   jax.ShapeDtypeStruct((seq, 128), F32), jax.ShapeDtypeStruct((N_Q_HEADS, 128), F32)]
        + [jax.ShapeDtypeStruct(p.shape[1:], F32) for p in pieces],
        scratch_shapes=[pltpu.VMEM((N_Q_HEADS, ATT_TQ, 128), F32)] + [pltpu.VMEM(p.shape[1:], F32) for p in pieces]
        + (_reduce_scratch([p.shape for p in pieces], [True] * n_p) if n_p else []),
        compiler_params=_cparams(("arbitrary",)),
    )(q_stack, k, v, sink128, bias, d_o, *pieces)


def _permute_rows(dst_ref, src_ref, sub_len):
    for k in range(N_SLAB):
        for j in range(sub_len):
            dst_ref[k, 8 * j:8 * (j + 1), :] = src_ref.at[k][pl.ds(j, SUBSEG, stride=sub_len), :]


def _unpermute_rows(dst_ref, src_ref, sub_len):
    for k in range(N_SLAB):
        for s in range(SUBSEG):
            dst_ref[k, s * sub_len:(s + 1) * sub_len, :] = src_ref.at[k][pl.ds(s, sub_len, stride=SUBSEG), :]


def _scan_chunk(br_ref, bi_ref, lr_row, li_row, init, cols, *, sub_len, reverse, store):
    lr = jnp.broadcast_to(lr_row[:, cols], (SUBSEG, SCAN_LANES))
    li = jnp.broadcast_to(li_row[:, cols], (SUBSEG, SCAN_LANES))
    if init is None:
        sr = si = jnp.zeros((SUBSEG, SCAN_LANES), F32)
    else:
        sr, si = init
    for jj in range(sub_len):
        rows = slice(SUBSEG * ((sub_len - 1 - jj) if reverse else jj), SUBSEG * (((sub_len - 1 - jj) if reverse else jj) + 1))
        sr, si = lr * sr - li * si + br_ref[rows, cols], lr * si + li * sr + bi_ref[rows, cols]
        if store:
            br_ref[rows, cols] = sr
            bi_ref[rows, cols] = si
    return sr, si


def _resolve_chunk(z, carry_refs, start_refs, pr_row, pi_row, cols, *, reverse):
    cr, ci = carry_refs[0][0:1, cols], carry_refs[1][0:1, cols]
    pr, pi = pr_row[:, cols], pi_row[:, cols]
    for s in (range(SUBSEG - 1, -1, -1) if reverse else range(SUBSEG)):
        start_refs[0][s:s + 1, cols] = cr
        start_refs[1][s:s + 1, cols] = ci
        cr, ci = pr * cr - pi * ci + z[0][s:s + 1, :], pr * ci + pi * cr + z[1][s:s + 1, :]
    carry_refs[0][0:1, cols] = cr
    carry_refs[1][0:1, cols] = ci


def _param_specs(direction):
    row = lambda q: pl.BlockSpec((None, None, 1, STATE_W), lambda i: (q, direction, 0, 0))
    wide = lambda q: pl.BlockSpec((None, None, N_SLAB, SLAB_IN, SLAB_ST), lambda i: (q, direction, 0, 0, 0))
    tall = lambda q: pl.BlockSpec((None, None, N_SLAB, SLAB_ST, SLAB_IN), lambda i: (q, direction, 0, 0, 0))
    return [row(q) for q in range(4)], [wide(0), wide(1)], [tall(0), tall(1)]


def _ssm_fwd(u, lam, bb, cb, *, direction, tb, name):
    reverse = direction == 1
    seq = u.shape[1]
    nblk = seq // tb
    sub_len = tb // SUBSEG

    def body(u_ref, lr_ref, li_ref, pr_ref, pi_ref, bbr_ref, bbi_ref, cbr_ref, cbi_ref,
             y_ref, sr_ref, si_ref, xr, xi, up, yp, car, cai):
        @pl.when(pl.program_id(0) == 0)
        def _():
            car[...] = jnp.zeros_like(car)
            cai[...] = jnp.zeros_like(cai)

        _permute_rows(up, u_ref, sub_len)
        lr, li, pr, pi = lr_ref[...], li_ref[...], pr_ref[...], pi_ref[...]
        chunk = lambda k: slice(k * SLAB_ST, (k + 1) * SLAB_ST)

        def drive(k):
            ub = up[k].astype(BF16)
            xr[:, chunk(k)] = _dot(ub, bbr_ref[k])
            xi[:, chunk(k)] = _dot(ub, bbi_ref[k])

        def scan(k):
            z = _scan_chunk(xr, xi, lr, li, None, chunk(k), sub_len=sub_len, reverse=reverse, store=False)
            _resolve_chunk(z, (car, cai), (sr_ref, si_ref), pr, pi, chunk(k), reverse=reverse)
            _scan_chunk(xr, xi, lr, li, (sr_ref[:, chunk(k)], si_ref[:, chunk(k)]), chunk(k),
                        sub_len=sub_len, reverse=reverse, store=True)

        def read_out(k):
            yp[k] = _dot(xr[:, chunk(k)].astype(BF16), cbr_ref[k]) - _dot(xi[:, chunk(k)].astype(BF16), cbi_ref[k])

        drive(0)
        for k in range(N_SLAB):
            if k + 1 < N_SLAB:
                drive(k + 1)
            scan(k)
            if k > 0:
                read_out(k - 1)
        read_out(N_SLAB - 1)
        _unpermute_rows(y_ref, yp, sub_len)

    blk = (lambda i: nblk - 1 - i) if reverse else (lambda i: i)
    rows, wide, tall = _param_specs(direction)
    tok = pl.BlockSpec((N_SLAB, tb, SLAB_IN), lambda i: (0, blk(i), 0))
    start_spec = pl.BlockSpec((None, SUBSEG, STATE_W), lambda i: (blk(i), 0, 0))
    return pl.pallas_call(
        body, name=name, grid=(nblk,),
        in_specs=[tok] + rows + wide + tall,
        out_specs=[tok, start_spec, start_spec],
        out_shape=[jax.ShapeDtypeStruct((N_SLAB, seq, SLAB_IN), F32), jax.ShapeDtypeStruct((nblk, SUBSEG, STATE_W), F32),
                   jax.ShapeDtypeStruct((nblk, SUBSEG, STATE_W), F32)],
        scratch_shapes=[pltpu.VMEM((tb, STATE_W), F32), pltpu.VMEM((tb, STATE_W), F32),
                        pltpu.VMEM((N_SLAB, tb, SLAB_IN), F32), pltpu.VMEM((N_SLAB, tb, SLAB_IN), F32),
                        pltpu.VMEM((SUBSEG, STATE_W), F32), pltpu.VMEM((SUBSEG, STATE_W), F32)],
        compiler_params=_cparams(("arbitrary",)),
    )(u, lam, lam, lam, lam, bb, bb, cb, cb)


def _ssm_bwd(u, dy, starts, lam, bb, bbt, cb_t, *, direction, tb, name):
    reverse = direction == 1
    seq = u.shape[1]
    nblk = seq // tb
    sub_len = tb // SUBSEG

    def body(u_ref, dy_ref, sr_ref, si_ref, lr_ref, li_ref, pr_ref, pi_ref, bbr_ref, bbi_ref, btr_ref, bti_ref,
             ctr_ref, cti_ref, du_ref, gb_ref, gc_ref, dl_ref,
             xr, xi, gr, gi, up, dyp, dup, gsr, gsi, car, cai):
        gbr_ref, gbi_ref = gb_ref.at[0], gb_ref.at[1]
        gcr_ref, gci_ref = gc_ref.at[0], gc_ref.at[1]
        dlr_ref, dli_ref = dl_ref.at[0], dl_ref.at[1]

        @pl.when(pl.program_id(0) == 0)
        def _():
            for ref in (car, cai, gbr_ref, gbi_ref, gcr_ref, gci_ref, dlr_ref, dli_ref):
                ref[...] = jnp.zeros_like(ref)

        _permute_rows(up, u_ref, sub_len)
        _permute_rows(dyp, dy_ref, sub_len)
        lr, li, pr, pi = lr_ref[...], li_ref[...], pr_ref[...], pi_ref[...]
        nli, npi = -li, -pi
        chunk = lambda k: slice(k * SLAB_ST, (k + 1) * SLAB_ST)

        def drive(k):
            ub = up[k].astype(BF16)
            xr[:, chunk(k)] = _dot(ub, bbr_ref[k])
            xi[:, chunk(k)] = _dot(ub, bbi_ref[k])
            dyb = dyp[k].astype(BF16)
            gr[:, chunk(k)] = _dot(dyb, ctr_ref[k])
            gi[:, chunk(k)] = -_dot(dyb, cti_ref[k])

        def scan_x(k):
            _scan_chunk(xr, xi, lr, li, (sr_ref[:, chunk(k)], si_ref[:, chunk(k)]), chunk(k),
                        sub_len=sub_len, reverse=reverse, store=True)

        def grad_c(k):
            dyb = dyp[k].astype(BF16)
            gcr_ref[k] += _dot_tn(xr[:, chunk(k)].astype(BF16), dyb)
            gci_ref[k] -= _dot_tn(xi[:, chunk(k)].astype(BF16), dyb)

        def scan_g(k):
            z = _scan_chunk(gr, gi, lr, nli, None, chunk(k), sub_len=sub_len, reverse=not reverse, store=False)
            _resolve_chunk(z, (car, cai), (gsr, gsi), pr, npi, chunk(k), reverse=not reverse)
            _scan_chunk(gr, gi, lr, nli, (gsr[:, chunk(k)], gsi[:, chunk(k)]), chunk(k),
                        sub_len=sub_len, reverse=not reverse, store=True)

        def grad_b_du(k):
            ub = up[k].astype(BF16)
            grb, gib = gr[:, chunk(k)].astype(BF16), gi[:, chunk(k)].astype(BF16)
            gbr_ref[k] += _dot_tn(ub, grb)
            gbi_ref[k] += _dot_tn(ub, gib)
            dup[k] = _dot(grb, btr_ref[k]) + _dot(gib, bti_ref[k])

        def grad_lambda(k):
            cols = chunk(k)
            acc_r, acc_i = dlr_ref[:, cols], dli_ref[:, cols]
            for jj in range(sub_len):
                prev = jj + 1 if reverse else jj - 1
                if 0 <= prev < sub_len:
                    x_r, x_i = xr[SUBSEG * prev:SUBSEG * (prev + 1), cols], xi[SUBSEG * prev:SUBSEG * (prev + 1), cols]
                else:
                    x_r, x_i = sr_ref[:, cols], si_ref[:, cols]
                g_r, g_i = gr[SUBSEG * jj:SUBSEG * (jj + 1), cols], gi[SUBSEG * jj:SUBSEG * (jj + 1), cols]
                acc_r = acc_r + (g_r * x_r + g_i * x_i)
                acc_i = acc_i + (g_i * x_r - g_r * x_i)
            dlr_ref[:, cols] = acc_r
            dli_ref[:, cols] = acc_i

        drive(0)
        for k in range(N_SLAB):
            if k + 1 < N_SLAB:
                drive(k + 1)
            scan_x(k)
            grad_c(k)
            scan_g(k)
            grad_b_du(k)
            grad_lambda(k)
        _unpermute_rows(du_ref, dup, sub_len)

    blk = (lambda i: i) if reverse else (lambda i: nblk - 1 - i)
    rows, wide, tall = _param_specs(direction)
    tok = pl.BlockSpec((N_SLAB, tb, SLAB_IN), lambda i: (0, blk(i), 0))
    start_spec = pl.BlockSpec((None, SUBSEG, STATE_W), lambda i: (blk(i), 0, 0))
    gb_shape, gc_shape, dl_shape = (2, N_SLAB, SLAB_IN, SLAB_ST), (2, N_SLAB, SLAB_ST, SLAB_IN), (2, SUBSEG, STATE_W)
    whole = lambda shape: pl.BlockSpec(shape, lambda i: (0,) * len(shape))
    big = lambda: pltpu.VMEM((tb, STATE_W), F32)
    slabs = lambda: pltpu.VMEM((N_SLAB, tb, SLAB_IN), F32)
    tile = lambda: pltpu.VMEM((SUBSEG, STATE_W), F32)
    return pl.pallas_call(
        body, name=name, grid=(nblk,),
        in_specs=[tok, tok, start_spec, start_spec] + rows + wide + tall + wide,
        out_specs=[tok, whole(gb_shape), whole(gc_shape), whole(dl_shape)],
        out_shape=[jax.ShapeDtypeStruct((N_SLAB, seq, SLAB_IN), F32), jax.ShapeDtypeStruct(gb_shape, F32),
                   jax.ShapeDtypeStruct(gc_shape, F32), jax.ShapeDtypeStruct(dl_shape, F32)],
        scratch_shapes=[big(), big(), big(), big(), slabs(), slabs(), slabs(), tile(), tile(), tile(), tile()],
        compiler_params=_cparams(("arbitrary",)),
    )(u, dy, *starts, lam, lam, lam, lam, bb, bb, bbt, bbt, cb_t, cb_t)


GELU_C = math.sqrt(2.0 / math.pi)
GELU_K = 0.044715


def _mid(o, za, u, y_f, y_b, zs, x, target, ssm_d, w_glu, b_glu, g_attn, g_ssm, w_out, ln_g, ln_b, tb):
    seq = x.shape[0]

    def body(o_ref, za_ref, u_ref, yf_ref, yb_ref, zs_ref, x_ref, t_ref, d_ref, wg_ref, bg_ref, ga_ref, gs_ref,
             wo_ref, lg_ref, lb_ref,
             loss_ref, do_ref, dza_ref, dyl_ref, dzs_ref, dpre_ref, gwo_ref, gwg_ref, vec_ref, wop):
        @pl.when(pl.program_id(0) == 0)
        def _():
            for ref in (loss_ref, gwo_ref, gwg_ref, vec_ref):
                ref[...] = jnp.zeros_like(ref)
            for nat, par in _pair_blocks(0):
                wop[par, :] = wo_ref[nat, :]
            wop[D_ATTN:, :] = wo_ref[D_ATTN:, :]

        def rows_of(rs):
            o, za = o_ref[rs, :], za_ref[rs, :]
            sig_a = _sigmoid(za)
            silu_a = za * sig_a
            ya = o * silu_a
            r_a = lax.rsqrt(jnp.mean(ya * ya, axis=1, keepdims=True) + NORM_EPS)
            n_a = ya * r_a
            g_a = ga_ref[...]
            unslab = lambda ref: jnp.concatenate([ref[k, rs, :] for k in range(N_SLAB)], axis=1)
            u_blk, zs = unslab(u_ref), zs_ref[rs, :]
            d_row = d_ref[...]
            ylin = d_row * u_blk + unslab(yf_ref) + unslab(yb_ref)
            inner = GELU_C * (ylin + GELU_K * ylin * ylin * ylin)
            th = jnp.tanh(inner)
            gl = 0.5 * ylin * (1.0 + th)
            glb = gl.astype(BF16)
            gate = _dot(glb, wg_ref[...])
            sg = _sigmoid(gate + bg_ref[...])
            y2 = gl * sg
            sig_s = _sigmoid(zs)
            silu_s = zs * sig_s
            ys = y2 * silu_s
            r_s = lax.rsqrt(jnp.mean(ys * ys, axis=1, keepdims=True) + NORM_EPS)
            n_s = ys * r_s
            g_s = gs_ref[...]
            mixed = jnp.concatenate([n_a * g_a, n_s * g_s], axis=1).astype(BF16)
            out = _dot(mixed, wop[...])
            pre = ALPHA * x_ref[rs, :] + out
            mu = jnp.mean(pre, axis=1, keepdims=True)
            cen = pre - mu
            rstd = lax.rsqrt(jnp.mean(cen * cen, axis=1, keepdims=True) + NORM_EPS)
            hhat = cen * rstd
            ln_g = lg_ref[...]
            err = hhat * ln_g + lb_ref[...] - t_ref[rs, :]
            loss_ref[...] += 0.5 * jnp.sum(jnp.mean(err * err, axis=1, keepdims=True))

            dh = err * (1.0 / D_MODEL)
            vec_ref[0:1, :] += jnp.sum(dh * hhat, axis=0, keepdims=True)
            vec_ref[1:2, :] += jnp.sum(dh, axis=0, keepdims=True)
            dhh = dh * ln_g
            dpre = rstd * (dhh - jnp.mean(dhh, axis=1, keepdims=True)
                           - hhat * jnp.mean(dhh * hhat, axis=1, keepdims=True))
            dpre_ref[rs, :] = dpre
            dpb = dpre.astype(BF16)
            for j in range(4):
                g_pair = _dot_tn(mixed[:, 128 * j:128 * (j + 1)], dpb)
                for g in range(2):
                    nat = HEAD_DIM * (4 * g + j)
                    gwo_ref[nat:nat + HEAD_DIM, :] += g_pair[HEAD_DIM * g:HEAD_DIM * (g + 1), :]
            gwo_ref[D_ATTN:, :] += _dot_tn(mixed[:, D_ATTN:], dpb)
            dmix = _dot_nt(dpb, wop[...])
            dna = dmix[:, :D_ATTN]
            vec_ref[2:3, 0:D_ATTN] += jnp.sum(dna * n_a, axis=0, keepdims=True)
            dna = dna * g_a
            dya = r_a * (dna - n_a * jnp.mean(dna * n_a, axis=1, keepdims=True))
            do_ref[rs, :] = dya * silu_a
            dza_ref[rs, :] = dya * o * (sig_a * (1.0 + za * (1.0 - sig_a)))
            dns = dmix[:, D_ATTN:]
            vec_ref[2:3, D_ATTN:] += jnp.sum(dns * n_s, axis=0, keepdims=True)
            dns = dns * g_s
            dys = r_s * (dns - n_s * jnp.mean(dns * n_s, axis=1, keepdims=True))
            dzs_ref[rs, :] = dys * y2 * (sig_s * (1.0 + zs * (1.0 - sig_s)))
            dy2 = dys * silu_s
            da = dy2 * gl * sg * (1.0 - sg)
            vec_ref[3:4, D_SSM:] += jnp.sum(da, axis=0, keepdims=True)
            dab = da.astype(BF16)
            gwg_ref[...] += _dot_tn(glb, dab)
            dgl_mm = _dot_nt(dab, wg_ref[...])
            dgl = dy2 * sg + dgl_mm
            dylin = dgl * (0.5 * (1.0 + th)
                           + 0.5 * ylin * (1.0 - th * th) * GELU_C * (1.0 + 3.0 * GELU_K * ylin * ylin))
            for k in range(N_SLAB):
                dyl_ref[k, rs, :] = dylin[:, k * SLAB_IN:(k + 1) * SLAB_IN]
            vec_ref[3:4, 0:D_SSM] += jnp.sum(dylin * u_blk, axis=0, keepdims=True)

        rows_of(slice(0, tb))

    tok = lambda w: pl.BlockSpec((tb, w), lambda i: (i, 0))
    slab = pl.BlockSpec((N_SLAB, tb, SLAB_IN), lambda i: (0, i, 0))
    const = lambda r, c: pl.BlockSpec((r, c), lambda i: (0, 0), pipeline_mode=pl.Buffered(1))
    tok_shape = jax.ShapeDtypeStruct((seq, 512), F32)
    return pl.pallas_call(
        body, name="mid", grid=(seq // tb,),
        in_specs=[tok(512), tok(512), slab, slab, slab, tok(512), tok(1024), tok(1024),
                  const(1, 512), const(512, 512), const(1, 512), const(1, 512), const(1, 512),
                  const(1024, 1024), const(1, 1024), const(1, 1024)],
        out_specs=[const(8, 128), tok(512), tok(512), slab, tok(512), tok(1024),
                   const(1024, 1024), const(512, 512), const(8, 1024)],
        out_shape=[jax.ShapeDtypeStruct((8, 128), F32), tok_shape, tok_shape,
                   jax.ShapeDtypeStruct((N_SLAB, seq, SLAB_IN), F32), tok_shape,
                   jax.ShapeDtypeStruct((seq, 1024), F32), jax.ShapeDtypeStruct((1024, 1024), F32),
                   jax.ShapeDtypeStruct((512, 512), F32), jax.ShapeDtypeStruct((8, 1024), F32)],
        scratch_shapes=[pltpu.VMEM((D_MODEL, D_MODEL), BF16)],
        compiler_params=_cparams(("arbitrary",)),
    )(o, za, u, y_f, y_b, zs, x, target, ssm_d, w_glu, b_glu, g_attn, g_ssm, w_out, ln_g, ln_b)


def _ride_shapes(pieces, narrow, gather_last):
    outs = [p.shape if (gather_last and a == len(pieces) - 1) else p.shape[1:] for a, p in enumerate(pieces)]
    return outs, [pltpu.VMEM(s, F32) for s in outs] + _reduce_scratch([p.shape for p in pieces], narrow)


def _ride_phases(piece_refs, out_refs, scratch_refs, narrow, gather_last):
    n = len(piece_refs)
    landing, rest = scratch_refs[:n], scratch_refs[n:]
    begin, exchange, combine, finish = _reduce_phases(piece_refs, landing, rest[:n], rest[n:2 * n], rest[2 * n:3 * n],
                                                      *rest[3 * n:], narrow, gather_last)

    def end():
        finish()
        for a in range(n):
            out_refs[a][...] = landing[a][...]

    return begin, exchange, combine, end


def _dproj_block(dq_ref, dk_ref, dv_ref, dza_ref, duf_ref, dub_ref, dyl_ref, dzs_ref, d_ref, hi_ref, lo_ref, tb):
    cos, sin = _rope_block(hi_ref, lo_ref, pl.program_id(0) * (tb // ROPE_GROUP), tb // ROPE_GROUP)
    lo = lax.broadcasted_iota(jnp.int32, (tb, 128), 1) < HEAD_DIM

    def unrope(t):
        return t * cos + _rotate_half_unsigned(t * sin)

    def natural(pairs):
        swapped = [pltpu.roll(t, HEAD_DIM, 1) for t in pairs]
        return [jnp.where(lo, pairs[0], swapped[1]), jnp.where(lo, pairs[2], swapped[3]),
                jnp.where(lo, swapped[0], pairs[1]), jnp.where(lo, swapped[2], pairs[3])]

    dq_rot, dza = dq_ref[...], dza_ref[...]
    pieces = natural([unrope(dq_rot[:, 128 * j:128 * (j + 1)]) for j in range(4)])
    d_row = d_ref[...]
    pieces += [unrope(dk_ref[...]), dv_ref[...]] + natural([dza[:, 128 * j:128 * (j + 1)] for j in range(4)])
    pieces += [duf_ref[k] + dub_ref[k] + d_row[:, k * SLAB_IN:(k + 1) * SLAB_IN] * dyl_ref[k] for k in range(N_SLAB)]
    pieces += [dzs_ref[...]]
    return jnp.concatenate(pieces, axis=1).astype(BF16)


def _dproj_specs(tb, rope_hi, rope_lo):
    tok = lambda w: pl.BlockSpec((tb, w), lambda i: (i, 0))
    slab = pl.BlockSpec((N_SLAB, tb, SLAB_IN), lambda i: (0, i, 0))
    table = lambda t: pl.BlockSpec(t.shape, lambda i: (0, 0, 0))
    return [tok(512), tok(128), tok(128), tok(512), slab, slab, slab, tok(512), pl.BlockSpec((1, 512), lambda i: (0, 0)),
            table(rope_hi), table(rope_lo)]


N_DPROJ = 11
GW_ROWS = 768


def _proj_bwd_w(x, dproj_args, rope_hi, rope_lo, pieces, tb):
    seq = x.shape[0]
    steps = seq // tb
    n_p = len(pieces)
    narrow = [False] * n_p

    def body(*refs):
        x_ref, grads = refs[0], refs[1:1 + N_DPROJ]
        piece_refs = refs[1 + N_DPROJ:1 + N_DPROJ + n_p]
        gw_ref = refs[1 + N_DPROJ + n_p]
        out_refs = refs[2 + N_DPROJ + n_p:2 + N_DPROJ + 2 * n_p]
        step = pl.program_id(0)
        if n_p:
            begin, exchange, combine, end = _ride_phases(piece_refs, out_refs, refs[2 + N_DPROJ + 2 * n_p:], narrow, True)
            pl.when(step == 0)(begin)
            pl.when(step == min(1, steps - 1))(exchange)
            pl.when(step == steps // 2)(combine)

        @pl.when(step == 0)
        def _():
            gw_ref[...] = jnp.zeros_like(gw_ref)

        dproj = _dproj_block(*grads, tb)
        xb = x_ref[...].astype(BF16)
        for r0 in range(0, D_IN_PROJ, GW_ROWS):
            gw_ref[r0:r0 + GW_ROWS, :] += _dot_tn(dproj[:, r0:r0 + GW_ROWS], xb)
        if n_p:
            pl.when(step == steps - 1)(end)

    vmem = pl.BlockSpec(memory_space=pltpu.VMEM)
    whole = pl.BlockSpec((D_IN_PROJ, D_MODEL), lambda i: (0, 0), pipeline_mode=pl.Buffered(1))
    ride_outs, ride_scratch = _ride_shapes(pieces, narrow, True) if n_p else ([], [])
    return pl.pallas_call(
        body, name="proj_bwd_w", grid=(steps,),
        in_specs=[pl.BlockSpec((tb, D_MODEL), lambda i: (i, 0))] + _dproj_specs(tb, rope_hi, rope_lo) + [vmem] * n_p,
        out_specs=[whole] + [vmem] * n_p,
        out_shape=[jax.ShapeDtypeStruct((D_IN_PROJ, D_MODEL), F32)] + [jax.ShapeDtypeStruct(s, F32) for s in ride_outs],
        scratch_shapes=ride_scratch,
        compiler_params=_cparams(("arbitrary",)),
    )(x, *dproj_args, rope_hi, rope_lo, *pieces)


def _proj_bwd_x(dproj_args, rope_hi, rope_lo, dpre, wt, pieces, tb):
    seq = dpre.shape[0]
    steps = seq // tb
    n_p = len(pieces)
    narrow = [True] * n_p

    def body(*refs):
        grads = refs[:N_DPROJ]
        dpre_ref, wt_ref = refs[N_DPROJ:N_DPROJ + 2]
        piece_refs = refs[N_DPROJ + 2:N_DPROJ + 2 + n_p]
        gx_ref = refs[N_DPROJ + 2 + n_p]
        out_refs = refs[N_DPROJ + 3 + n_p:N_DPROJ + 3 + 2 * n_p]
        step = pl.program_id(0)
        if n_p:
            begin, exchange, combine, end = _ride_phases(piece_refs, out_refs, refs[N_DPROJ + 3 + 2 * n_p:], narrow, False)
            pl.when(step == 0)(begin)
            pl.when(step == min(1, steps - 1))(exchange)
            pl.when(step == steps - 1)(combine)

        dproj = _dproj_block(*grads, tb)
        gx_ref[...] = ALPHA * dpre_ref[...] + _dot(dproj, wt_ref[...])
        if n_p:
            pl.when(step == steps - 1)(end)

    vmem = pl.BlockSpec(memory_space=pltpu.VMEM)
    whole = pl.BlockSpec((D_IN_PROJ, D_MODEL), lambda i: (0, 0), pipeline_mode=pl.Buffered(1))
    ride_outs, ride_scratch = _ride_shapes(pieces, narrow, False) if n_p else ([], [])
    return pl.pallas_call(
        body, name="proj_bwd_x", grid=(steps,),
        in_specs=_dproj_specs(tb, rope_hi, rope_lo) + [pl.BlockSpec((tb, D_MODEL), lambda i: (i, 0)), whole] + [vmem] * n_p,
        out_specs=[pl.BlockSpec((tb, D_MODEL), lambda i: (i, 0))] + [vmem] * n_p,
        out_shape=[jax.ShapeDtypeStruct((seq, D_MODEL), F32)] + [jax.ShapeDtypeStruct(s, F32) for s in ride_outs],
        scratch_shapes=ride_scratch,
        compiler_params=pltpu.CompilerParams(dimension_semantics=("arbitrary",), vmem_limit_bytes=PROJ_BWD_X_VMEM),
    )(*dproj_args, rope_hi, rope_lo, dpre, wt, *pieces)


def _adamw(w, g, m, v, name):
    rows, cols = w.shape
    tb = rows
    while tb * cols * 4 > ADAMW_BLOCK_BYTES and tb % 16 == 0:
        tb //= 2

    def body(w_ref, g_ref, m_ref, v_ref, d_ref, nm_ref, nv_ref):
        _adamw_update(w_ref, g_ref, m_ref, v_ref, d_ref, nm_ref, nv_ref)

    spec = pl.BlockSpec((tb, cols), lambda i: (i, 0))
    return pl.pallas_call(
        body, name=name, grid=(rows // tb,), in_specs=[spec] * 4, out_specs=[spec] * 3,
        out_shape=[jax.ShapeDtypeStruct((rows, cols), F32)] * 3,
        compiler_params=_cparams(("arbitrary",)),
    )(w, g, m, v)


def _adamw_update(w_ref, g_ref, m_ref, v_ref, d_ref, nm_ref, nv_ref):
    g_blk = g_ref[...]
    m_new = ADAM_B1 * m_ref[...] + (1.0 - ADAM_B1) * g_blk
    v_new = ADAM_B2 * v_ref[...] + (1.0 - ADAM_B2) * (g_blk * g_blk)
    m_hat = m_new / (1.0 - ADAM_B1 ** ADAM_STEP)
    v_hat = v_new / (1.0 - ADAM_B2 ** ADAM_STEP)
    d_ref[...] = -ADAM_LR * (m_hat / (jnp.sqrt(v_hat) + ADAM_EPS) + ADAM_WD * w_ref[...])
    nm_ref[...] = m_new
    nv_ref[...] = v_new


def _adamw_many(groups, name):
    n = len(groups)

    def body(*refs):
        for p in range(n):
            w_ref, g_ref, m_ref, v_ref = refs[4 * p:4 * p + 4]
            gn_ref, d_ref, nm_ref, nv_ref = refs[4 * n + 4 * p:4 * n + 4 * p + 4]
            gn_ref[...] = g_ref[...].reshape(w_ref.shape)
            _adamw_update(w_ref, gn_ref, m_ref, v_ref, d_ref, nm_ref, nv_ref)

    return pl.pallas_call(
        body, name=name,
        out_shape=[jax.ShapeDtypeStruct(grp[0].shape, F32) for grp in groups for _ in range(4)],
    )(*[a for grp in groups for a in grp])


_WEIGHTS = ["w_in", "attn_sink", "ssm_a_re", "ssm_a_im", "ssm_log_dt", "ssm_b_re", "ssm_b_im", "ssm_c_re", "ssm_c_im",
            "ssm_d", "w_glu", "b_glu", "norm_attn_g", "norm_ssm_g", "w_out", "ln_g", "ln_b"]
N_DG = N_DIR * N_GROUPS
BIG_ROWS = N_DG * SSM_CH * SSM_STATE // 128
TINY_ROWS = 64


def _pack_small_grads(g_bc, g_vec, g_ar, g_ai, g_dt, g_sink, loss):
    big = jnp.stack([t.reshape(BIG_ROWS, 128) for t in g_bc])
    row = lambda t: jnp.pad(t.reshape(1, -1), ((0, 0), (0, 128 - t.size)))
    tiny = jnp.concatenate([g_vec.reshape(64, 128), g_ar.reshape(32, 128), g_ai.reshape(32, 128), row(g_dt), row(g_sink),
                            row(loss), jnp.zeros((N_CHIPS * TINY_ROWS - 131, 128), F32)], axis=0)
    return jnp.concatenate([big, tiny.reshape(N_CHIPS, TINY_ROWS, 128)], axis=1)


def _unpack_small_grads(packed):
    big = packed[:, :BIG_ROWS].reshape(N_CHIPS, 2 * BIG_ROWS, SSM_STATE)
    tiny = packed[:, BIG_ROWS:].reshape(N_CHIPS * TINY_ROWS, 128)
    g_vec = tiny[0:64].reshape(8, 1024)
    return tiny[130, 0], {
        "ssm_b_re": big[0], "ssm_b_im": big[1], "ssm_c_re": big[2], "ssm_c_im": big[3],
        "ln_g": g_vec[0:1], "ln_b": g_vec[1:2],
        "norm_attn_g": _from_pair_order(g_vec[2:3, :D_ATTN]), "norm_ssm_g": g_vec[2:3, D_ATTN:],
        "ssm_d": g_vec[3:4, :D_SSM], "b_glu": g_vec[3:4, D_SSM:],
        "ssm_a_re": tiny[64:96].reshape(N_DG, SSM_STATE), "ssm_a_im": tiny[96:128].reshape(N_DG, SSM_STATE),
        "ssm_log_dt": tiny[128:129, :N_DG].reshape(N_DIR, N_GROUPS), "attn_sink": tiny[129:130, :N_Q_HEADS],
    }


def _small_unview(name, t, shape):
    if name in ("ssm_b_re", "ssm_b_im"):
        return jnp.swapaxes(t.reshape(N_DIR, N_GROUPS, SSM_CH, SSM_STATE), 2, 3).reshape(shape)
    return t.reshape(shape)


def _channel_major(name, t):
    return jnp.swapaxes(t, 3, 4) if name in ("ssm_b_re", "ssm_b_im") else t


def kernel(x, w_in, attn_sink, ssm_a_re, ssm_a_im, ssm_log_dt, ssm_b_re, ssm_b_im, ssm_c_re, ssm_c_im, ssm_d, w_glu, b_glu, norm_attn_g, norm_ssm_g, w_out, ln_g, ln_b, loss_target, m_w_in, m_attn_sink, m_ssm_a_re, m_ssm_a_im, m_ssm_log_dt, m_ssm_b_re, m_ssm_b_im, m_ssm_c_re, m_ssm_c_im, m_ssm_d, m_w_glu, m_b_glu, m_norm_attn_g, m_norm_ssm_g, m_w_out, m_ln_g, m_ln_b, v_w_in, v_attn_sink, v_ssm_a_re, v_ssm_a_im, v_ssm_log_dt, v_ssm_b_re, v_ssm_b_im, v_ssm_c_re, v_ssm_c_im, v_ssm_d, v_w_glu, v_b_glu, v_norm_attn_g, v_norm_ssm_g, v_w_out, v_ln_g, v_ln_b):
    args = dict(locals())
    weights = {n: args[n] for n in _WEIGHTS}
    mom_m = {n: args["m_" + n] for n in _WEIGHTS}
    mom_v = {n: args["v_" + n] for n in _WEIGHTS}
    xs = x[0]
    target = loss_target[0]

    (wt_g,) = _all_gather_chips([w_in[0].T], BF16, "gather_weights")
    wt_full = wt_g.reshape(D_IN_PROJ, D_MODEL)

    g_x, r_wt, r_w_out, r_w_glu, g_small_all = _local_step(
        xs, target, wt_full, w_glu[0], w_out[0], attn_sink, ssm_a_re, ssm_a_im, ssm_log_dt, ssm_b_re, ssm_b_im,
        ssm_c_re, ssm_c_im, ssm_d, b_glu, norm_attn_g, norm_ssm_g, ln_g, ln_b, sharded=True)
    loss, small_grads = _unpack_small_grads(g_small_all)

    grads, deltas, new_m, new_v = {}, {}, {}, {}
    d_w, m_w, v_w = _adamw(w_in[0].T, r_wt, m_w_in[0].T, v_w_in[0].T, "adamw_w_in")
    grads["w_in"], deltas["w_in"], new_m["w_in"], new_v["w_in"] = r_wt.T[None], d_w.T[None], m_w.T[None], v_w.T[None]
    for n, g in (("w_out", r_w_out), ("w_glu", r_w_glu)):
        d_w, m_w, v_w = _adamw(weights[n][0], g, mom_m[n][0], mom_v[n][0], "adamw_" + n)
        grads[n], deltas[n], new_m[n], new_v[n] = g[None], d_w[None], m_w[None], v_w[None]
    names = sorted(small_grads)
    updates = _adamw_many([(_channel_major(n, weights[n]), small_grads[n], _channel_major(n, mom_m[n]),
                            _channel_major(n, mom_v[n])) for n in names], "adamw_small")
    for i, n in enumerate(names):
        grads[n], deltas[n], new_m[n], new_v[n] = (_channel_major(n, t) for t in updates[4 * i:4 * i + 4])

    return (loss, g_x[None], *[grads[n] for n in _WEIGHTS], *[deltas[n] for n in _WEIGHTS],
            *[new_m[n] for n in _WEIGHTS], *[new_v[n] for n in _WEIGHTS])


def _local_step(xs, target, wt_full, w_glu_in, w_out_in, attn_sink, ssm_a_re, ssm_a_im, ssm_log_dt, ssm_b_re,
                ssm_b_im, ssm_c_re, ssm_c_im, ssm_d, b_glu, norm_attn_g, norm_ssm_g, ln_g, ln_b, sharded):
    seq = xs.shape[0]

    a_r, a_i = ssm_a_re, ssm_a_im
    log_dt = ssm_log_dt.reshape(N_DG, 1)
    b_r, b_i = _channel_major("ssm_b_re", ssm_b_re), _channel_major("ssm_b_im", ssm_b_im)
    c_r, c_i = ssm_c_re, ssm_c_im
    ssm_tb = min(SSM_BLOCK, seq)
    sub_len = ssm_tb // SUBSEG
    lam, bb, bbt, cb, cb_t = _ssm_params_fwd(a_r, a_i, log_dt, b_r, b_i, c_r, c_i, int(math.log2(sub_len)))
    lam = lam.reshape(4, N_DIR, 1, STATE_W)

    rope_hi, rope_lo = _rope_tables(seq)
    projected = _proj(xs, wt_full, rope_hi, rope_lo, [w_glu_in, w_out_in] if sharded else [], min(512, seq))
    q_stack, k_rot, v_bf, z_attn, u, z_ssm = projected[:6]
    if sharded:
        w_glu_full, w_out_full = projected[6].reshape(D_SSM, D_SSM), projected[7].reshape(D_MODEL, D_MODEL)
    else:
        w_glu_full, w_out_full = w_glu_in, w_out_in
    sink128 = jnp.broadcast_to(attn_sink[0][:, None, None], (N_Q_HEADS, 1, 128))
    attn_bias = _attn_bias()
    o = _attn_fwd(q_stack, k_rot, v_bf, sink128, attn_bias)
    ys, starts = [], []
    for d in range(N_DIR):
        y_d, s_r, s_i = _ssm_fwd(u, lam, bb, cb, direction=d, tb=ssm_tb, name=f"ssm_fwd_{d}")
        ys.append(y_d)
        starts.append((s_r, s_i))

    row = lambda t: t.reshape(1, -1)
    g_attn_p = _to_pair_order(norm_attn_g)
    loss_blk, d_o, d_za, d_ylin, d_zs, d_pre, g_w_out, g_w_glu, g_vec = _mid(
        o, z_attn, u, ys[0], ys[1], z_ssm, xs, target, row(ssm_d), w_glu_full, row(b_glu),
        g_attn_p, row(norm_ssm_g), w_out_full, row(ln_g), row(ln_b), min(256, seq))

    pieces = [g_w_glu.reshape(N_CHIPS, -1, D_SSM), g_w_out.reshape(N_CHIPS, -1, D_MODEL)] if sharded else []
    attn_grads = _attn_bwd(q_stack, k_rot, v_bf, sink128, attn_bias, d_o, pieces)
    dq, dk, dv, g_sink = attn_grads[:4]
    if sharded:
        g_w_glu, g_w_out = attn_grads[4:]
    dus, g_bb, g_cb, g_lam = [], [], [], []
    for d in range(N_DIR):
        du_d, gb_d, gc_d, dl_d = _ssm_bwd(u, d_ylin, starts[d], lam, bb, bbt, cb_t, direction=d, tb=ssm_tb,
                                          name=f"ssm_bwd_{d}")
        dus.append(du_d)
        g_bb.append(gb_d)
        g_cb.append(gc_d)
        g_lam.append(dl_d)
    g_ar, g_ai, g_dt, g_br, g_bi, g_cr, g_ci = _ssm_params_bwd(a_r, a_i, log_dt, b_r, b_i, g_bb, g_cb, g_lam)

    g_small = _pack_small_grads([g_br, g_bi, g_cr, g_ci], g_vec, g_ar, g_ai, g_dt, g_sink[:, 0], loss_blk[0, 0])
    dproj_args = (dq, dk, dv, d_za, dus[0], dus[1], d_ylin, d_zs, row(ssm_d))
    w_grads = _proj_bwd_w(xs, dproj_args, rope_hi, rope_lo, [g_small] if sharded else [], min(512, seq))
    g_wt = w_grads[0]
    if sharded:
        g_small = w_grads[1]
    x_grads = _proj_bwd_x(dproj_args, rope_hi, rope_lo, d_pre, wt_full,
                          [g_wt.reshape(N_CHIPS, -1, D_MODEL)] if sharded else [], min(512, seq))
    g_x = x_grads[0]
    if sharded:
        g_wt = x_grads[1]
    return g_x, g_wt, g_w_out, g_w_glu, g_small
```

```python
import functools
import math

import numpy as np
import jax
import jax.numpy as jnp
from jax import lax
from jax.experimental import pallas as pl
from jax.experimental.pallas import tpu as pltpu

F32 = jnp.float32
BF16 = jnp.bfloat16
MESH = pl.DeviceIdType.MESH

D_MODEL = 1024
D_ATTN = 512
D_SSM = 512
HEAD_DIM = 64
N_Q_HEADS = 8
WINDOW = 128
ROPE_THETA = 10000.0
SSM_CH = 16
N_GROUPS = 32
SSM_STATE = 64
N_DIR = 2
STATE_W = N_GROUPS * SSM_STATE
N_SLAB = 4
SLAB_IN = 128
SLAB_ST = 512
NORM_EPS = 1e-5
NEG_INF = -1e30
ALPHA = 2.0 ** 0.25
D_IN_PROJ = 2304
N_CHIPS = 4

ADAM_LR = 0.001
ADAM_B1 = 0.9
ADAM_B2 = 0.999
ADAM_EPS = 1e-08
ADAM_WD = 0.01
ADAM_STEP = 10

SUBSEG = 8
SCAN_LANES = 512
SSM_BLOCK = 512
VMEM_LIMIT = 48 * 1024 * 1024
ADAMW_BLOCK_BYTES = 3 * 512 * 1024
PROJ_BWD_X_VMEM = 56 * 1024 * 1024

def _to_pair_order(row):
    return jnp.transpose(row.reshape(2, 4, HEAD_DIM), (1, 0, 2)).reshape(1, D_ATTN)


def _from_pair_order(row):
    return jnp.transpose(row.reshape(4, 2, HEAD_DIM), (1, 0, 2)).reshape(1, D_ATTN)


def _cparams(sem=None):
    return pltpu.CompilerParams(dimension_semantics=sem, vmem_limit_bytes=VMEM_LIMIT)


def _dot(a, b):
    return jnp.dot(a, b, preferred_element_type=F32)


def _dot_nt(a, b):
    return lax.dot_general(a, b, (((1,), (1,)), ((), ())), preferred_element_type=F32)


def _dot_tn(a, b):
    return lax.dot_general(a, b, (((0,), (0,)), ((), ())), preferred_element_type=F32)


def _sigmoid(z):
    return 0.5 * jnp.tanh(0.5 * z) + 0.5


def _all_gather_chips(shards, out_dtype, name):
    n = len(shards)

    def body(*refs):
        start, relay, finish = _gather_phases(refs[:n], refs[n:2 * n], *refs[2 * n:], out_dtype)
        start()
        relay()
        finish()

    vmem = pl.BlockSpec(memory_space=pltpu.VMEM)
    return pl.pallas_call(
        body, name=name,
        out_shape=[jax.ShapeDtypeStruct((N_CHIPS,) + s.shape, out_dtype) for s in shards],
        in_specs=[vmem] * n, out_specs=[vmem] * n,
        scratch_shapes=_gather_sems(n),
        compiler_params=pltpu.CompilerParams(vmem_limit_bytes=VMEM_LIMIT),
    )(*shards)


def _gather_sems(n):
    return [pltpu.SemaphoreType.DMA((6 * n,)), pltpu.SemaphoreType.DMA((6 * n,))]


def _gather_phases(in_refs, out_refs, send_sems, recv_sems, out_dtype):
    n = len(in_refs)
    x, y, c = lax.axis_index("x"), lax.axis_index("y"), lax.axis_index("c")
    sibling = (x, y, 1 - c)
    chips = [(1 - x, y), (x, 1 - y), (1 - x, 1 - y)]

    def half_of(a, px, py, half):
        rows = in_refs[a].shape[0] // 2
        return out_refs[a].at[2 * px + py, pl.ds(half * rows, rows), :]

    def copy(a, k, px, py, half, to):
        blk = half_of(a, px, py, half)
        return pltpu.make_async_remote_copy(src_ref=blk, dst_ref=blk, send_sem=send_sems.at[6 * a + k],
                                            recv_sem=recv_sems.at[6 * a + k], device_id=to, device_id_type=MESH)

    first = [copy(a, j, x, y, c, (*chips[j], c)) for a in range(n) for j in range(3)]
    passed = [copy(a, 3 + j, *chips[j], c, sibling) for a in range(n) for j in range(3)]

    def start():
        for a in range(n):
            out_refs[a][2 * x + y] = in_refs[a][...].astype(out_dtype)
        for cp in first:
            cp.start()

    def relay():
        for a in range(n):
            for j in range(3):
                copy(a, j, *chips[j], c, (x, y, c)).wait_recv()
                passed[3 * a + j].start()

    def finish():
        for a in range(n):
            for j in range(3):
                copy(a, 3 + j, *chips[j], 1 - c, (x, y, c)).wait_recv()
        for cp in first + passed:
            cp.wait_send()

    return start, relay, finish


SEMS_PER_ARRAY = 14


def _reduce_scratch(shapes, narrow):
    half = [(N_CHIPS, s[1] // 2, s[2]) for s in shapes]
    wire = [BF16 if nar else F32 for nar in narrow]
    n = len(shapes)
    return ([pltpu.VMEM(half[a], F32) for a in range(n)] + [pltpu.VMEM(half[a], wire[a]) for a in range(n)]
            + [pltpu.VMEM(half[a], wire[a]) for a in range(n)]
            + [pltpu.SemaphoreType.DMA((SEMS_PER_ARRAY * n,)), pltpu.SemaphoreType.DMA((SEMS_PER_ARRAY * n,))])


def _reduce_phases(p_refs, out_refs, a_refs, s_refs, b_refs, send_sems, recv_sems, narrow, gather_last):
    n = len(p_refs)
    halves = [p.shape[1] // 2 for p in p_refs]
    wire = [BF16 if nar else F32 for nar in narrow]
    x, y, c = lax.axis_index("x"), lax.axis_index("y"), lax.axis_index("c")
    me = 2 * x + y
    sibling = (x, y, 1 - c)
    chips = [(1 - x, y), (x, 1 - y), (1 - x, 1 - y)]
    slot = [2 * px + py for px, py in chips]
    last = n - 1

    def copy(a, k, src, dst, to):
        return pltpu.make_async_remote_copy(src_ref=src, dst_ref=dst, send_sem=send_sems.at[SEMS_PER_ARRAY * a + k],
                                            recv_sem=recv_sems.at[SEMS_PER_ARRAY * a + k],
                                            device_id=to, device_id_type=MESH)

    def rows(a, half):
        return pl.ds(pl.multiple_of(half * halves[a], 16), halves[a])

    def finished(a, k, half):
        if gather_last and a == last:
            return out_refs[a].at[k, rows(a, half), :]
        return out_refs[a].at[rows(a, half), :]

    order = slot + [me]
    swaps = [[copy(a, q, p_refs[a].at[order[q], rows(a, 1 - c), :], a_refs[a].at[order[q]], sibling)
              for q in range(N_CHIPS)] for a in range(n)]
    sends = [[copy(a, 4 + j, s_refs[a].at[slot[j]], b_refs[a].at[me], (*chips[j], c)) for j in range(3)] for a in range(n)]
    backs = [copy(a, 7, finished(a, me, c), finished(a, me, c), sibling) for a in range(n)]
    spread = [copy(last, 8 + j, finished(last, me, c), finished(last, me, c), (*chips[j], c)) for j in range(3)]
    relays = [copy(last, 11 + j, finished(last, slot[j], c), finished(last, slot[j], c), sibling) for j in range(3)]

    def start():
        for group in swaps:
            for cp in group:
                cp.start()

    def exchange():
        for a in range(n):
            for q in range(N_CHIPS):
                swaps[a][q].wait_recv()
                acc = a_refs[a][order[q]] + p_refs[a][order[q], rows(a, c), :]
                a_refs[a][order[q]] = acc
                s_refs[a][order[q]] = acc.astype(wire[a])
                if q < 3:
                    sends[a][q].start()
            b_refs[a][me] = s_refs[a][me]

    def combine():
        for a in range(n):
            for j in range(3):
                copy(a, 4 + j, s_refs[a].at[slot[j]], b_refs[a].at[slot[j]], (x, y, c)).wait_recv()
            terms = [jnp.where(me == k, a_refs[a][k], b_refs[a][k].astype(F32)) for k in range(N_CHIPS)]
            total = (terms[0] + terms[1]) + (terms[2] + terms[3])
            if gather_last and a == last:
                out_refs[a][me, rows(a, c), :] = total
            else:
                out_refs[a][rows(a, c), :] = total
            backs[a].start()
        if gather_last:
            for cp in spread:
                cp.start()

    def finish():
        if gather_last:
            for j in range(3):
                copy(last, 8 + j, finished(last, slot[j], c), finished(last, slot[j], c), (x, y, c)).wait_recv()
                relays[j].start()
        for a in range(n):
            copy(a, 7, finished(a, me, 1 - c), finished(a, me, 1 - c), (x, y, c)).wait_recv()
        if gather_last:
            for j in range(3):
                copy(last, 11 + j, finished(last, slot[j], 1 - c), finished(last, slot[j], 1 - c), (x, y, c)).wait_recv()
        started = [cp for group in swaps + sends for cp in group] + backs + (spread + relays if gather_last else [])
        for cp in started:
            cp.wait_send()

    return start, exchange, combine, finish


def _ssm_param_values(ar, ai, logdt):
    dt = jnp.exp(logdt)
    mag = jnp.exp(dt * ar)
    cs, sn = jnp.cos(dt * ai), jnp.sin(dt * ai)
    lr, li = mag * cs, mag * sn
    den = ar * ar + ai * ai
    nr = (lr - 1.0) * ar + li * ai
    ni = li * ar - (lr - 1.0) * ai
    return dt, mag, lr, li, den, nr, ni


GROUPS_PER_SLAB = N_GROUPS // N_SLAB


def _slab_masks():
    def eq(shape, f_row, f_col):
        return (f_row(lax.broadcasted_iota(jnp.int32, shape, 0)) == f_col(lax.broadcasted_iota(jnp.int32, shape, 1))).astype(F32)
    spread = eq((SSM_STATE, SLAB_ST), lambda r: r, lambda c: c % SSM_STATE)
    spread_t = eq((SLAB_ST, SSM_STATE), lambda r: r % SSM_STATE, lambda c: c)
    keep = eq((SLAB_IN, SLAB_ST), lambda r: r // SSM_CH, lambda c: c // SSM_STATE)
    keep_t = eq((SLAB_ST, SLAB_IN), lambda r: r // SSM_STATE, lambda c: c // SSM_CH)
    repeat = eq((N_DG * SSM_CH, N_DG), lambda r: r // SSM_CH, lambda c: c)
    return spread, spread_t, keep, keep_t, repeat


def _rows(ref):
    return ref[...].reshape(-1, SSM_STATE)


def _split3(t):
    hi = t.astype(BF16)
    rest = t - hi.astype(F32)
    mid = rest.astype(BF16)
    return hi, mid, (rest - mid.astype(F32)).astype(BF16)


def _select(dot, ones01, t, ones_first):
    o = ones01.astype(BF16)
    parts = [dot(o, p) if ones_first else dot(p, o) for p in _split3(t)]
    return (parts[0] + parts[1]) + parts[2]


def _ssm_params_fwd(ar, ai, logdt, br, bi, cr, ci, n_square):
    def body(ar_ref, ai_ref, dt_ref, br_ref, bi_ref, cr_ref, ci_ref, lam_ref, bb_ref, bbt_ref, cb_ref, cbt_ref):
        _, _, lr, li, den, nr, ni = _ssm_param_values(_rows(ar_ref), _rows(ai_ref), dt_ref[...])
        lam_ref[0] = lr
        lam_ref[1] = li
        pr, pi = lr, li
        for _ in range(n_square):
            pr, pi = pr * pr - pi * pi, 2.0 * pr * pi
        lam_ref[2] = pr
        lam_ref[3] = pi
        spread, spread_t, keep, keep_t, repeat = _slab_masks()
        fr = _select(_dot, repeat, nr / den, True)
        fi = _select(_dot, repeat, ni / den, True)
        b_r, b_i = _rows(br_ref), _rows(bi_ref)
        bbar = (fr * b_r - fi * b_i, fr * b_i + fi * b_r)
        c_par = (_rows(cr_ref), _rows(ci_ref))
        spread, spread_t = spread.astype(BF16), spread_t.astype(BF16)
        for src, wide_ref, tall_ref in ((bbar, bb_ref, bbt_ref), (c_par, cbt_ref, cb_ref)):
            for q in range(2):
                for d in range(N_DIR):
                    for k in range(N_SLAB):
                        r0 = (d * N_GROUPS + k * GROUPS_PER_SLAB) * SSM_CH
                        blk = src[q][r0:r0 + SLAB_IN].astype(BF16)
                        wide_ref[q, d, k] = (_dot(blk, spread) * keep).astype(BF16)
                        tall_ref[q, d, k] = (_dot_nt(spread_t, blk) * keep_t).astype(BF16)

    wide = jax.ShapeDtypeStruct((2, N_DIR, N_SLAB, SLAB_IN, SLAB_ST), BF16)
    tall = jax.ShapeDtypeStruct((2, N_DIR, N_SLAB, SLAB_ST, SLAB_IN), BF16)
    return pl.pallas_call(body, name="ssm_params_fwd",
                          out_shape=[jax.ShapeDtypeStruct((4, N_DG, SSM_STATE), F32), wide, tall, tall, wide],
                          compiler_params=pltpu.CompilerParams(vmem_limit_bytes=VMEM_LIMIT),
                          )(ar, ai, logdt, br, bi, cr, ci)


def _ssm_params_bwd(ar, ai, logdt, br, bi, g_slabs_b, g_slabs_c, g_lam):
    def body(ar_ref, ai_ref, dt_ref, br_ref, bi_ref, gb0_ref, gb1_ref, gc0_ref, gc1_ref, gl0_ref, gl1_ref,
             gar_ref, gai_ref, gdt_ref, gbr_ref, gbi_ref, gcr_ref, gci_ref, dbb, dlam):
        spread, spread_t, keep, _, repeat = _slab_masks()
        for d, (gb_ref, gc_ref) in enumerate(((gb0_ref, gc0_ref), (gb1_ref, gc1_ref))):
            for q in range(2):
                for k in range(N_SLAB):
                    r0 = (d * N_GROUPS + k * GROUPS_PER_SLAB) * SSM_CH
                    dbb[q, r0:r0 + SLAB_IN, :] = _select(_dot, spread_t, gb_ref[q, k] * keep, False)
                    out_ref = gcr_ref if q == 0 else gci_ref
                    out_ref[r0:r0 + SLAB_IN, :] = _select(_dot, spread_t, gc_ref[q, k] * keep, False)
        grp = (lax.broadcasted_iota(jnp.int32, (N_GROUPS, STATE_W), 0)
               == lax.broadcasted_iota(jnp.int32, (N_GROUPS, STATE_W), 1) // SSM_STATE).astype(F32)
        pick = (lax.broadcasted_iota(jnp.int32, (STATE_W, SSM_STATE), 0) % SSM_STATE
                == lax.broadcasted_iota(jnp.int32, (STATE_W, SSM_STATE), 1)).astype(F32)
        for d, gl_ref in enumerate((gl0_ref, gl1_ref)):
            for q in range(2):
                row = jnp.sum(gl_ref[q], axis=0, keepdims=True)
                dlam[q, d * N_GROUPS:(d + 1) * N_GROUPS, :] = _select(_dot, pick, grp * row, False)

        a_r, a_i = _rows(ar_ref), _rows(ai_ref)
        dt, mag, lr, li, den, nr, ni = _ssm_param_values(a_r, a_i, dt_ref[...])
        fr = _select(_dot, repeat, nr / den, True)
        fi = _select(_dot, repeat, ni / den, True)
        b_r, b_i = _rows(br_ref), _rows(bi_ref)
        g_r, g_i = dbb[0], dbb[1]
        gbr_ref[...] = fr * g_r + fi * g_i
        gbi_ref[...] = fr * g_i - fi * g_r
        d_fr = _select(_dot_tn, repeat, b_r * g_r + b_i * g_i, True)
        d_fi = _select(_dot_tn, repeat, b_r * g_i - b_i * g_r, True)
        d_nr, d_ni = d_fr / den, d_fi / den
        d_den = -(d_fr * nr + d_fi * ni) / (den * den)
        d_lr = dlam[0] + d_nr * a_r - d_ni * a_i
        d_li = dlam[1] + d_nr * a_i + d_ni * a_r
        d_ar = d_nr * (lr - 1.0) + d_ni * li + d_den * 2.0 * a_r
        d_ai = d_nr * li - d_ni * (lr - 1.0) + d_den * 2.0 * a_i
        d_mag = (d_lr * lr + d_li * li) / mag
        d_theta = d_li * lr - d_lr * li
        gar_ref[...] = d_ar + d_mag * mag * dt
        gai_ref[...] = d_ai + d_theta * dt
        d_dt = d_mag * mag * a_r + d_theta * a_i
        gdt_ref[...] = jnp.sum(d_dt, axis=1, keepdims=True) * dt

    small = jax.ShapeDtypeStruct((N_DG, SSM_STATE), F32)
    big = jax.ShapeDtypeStruct((N_DG * SSM_CH, SSM_STATE), F32)
    return pl.pallas_call(
        body, name="ssm_params_bwd",
        out_shape=[small, small, jax.ShapeDtypeStruct(logdt.shape, F32), big, big, big, big],
        scratch_shapes=[pltpu.VMEM((2,) + big.shape, F32), pltpu.VMEM((2,) + small.shape, F32)],
        compiler_params=pltpu.CompilerParams(vmem_limit_bytes=VMEM_LIMIT),
    )(ar, ai, logdt, br, bi, *g_slabs_b, *g_slabs_c, *g_lam)


ROPE_GROUP = 128


def _rope_tables(seq):
    half = HEAD_DIM // 2
    inv_freq = jnp.tile(ROPE_THETA ** (-jnp.arange(half, dtype=F32) / half), 4)
    sign = jnp.tile(jnp.concatenate([-jnp.ones((half,), F32), jnp.ones((half,), F32)]), 2)

    def table(pos):
        ang = pos.astype(F32)[:, None] * inv_freq[None, :]
        return jnp.stack([jnp.cos(ang), jnp.sin(ang), sign * jnp.sin(ang)])

    return table(jnp.arange(seq // ROPE_GROUP) * ROPE_GROUP), table(jnp.arange(ROPE_GROUP))


def _rope_block(hi_ref, lo_ref, first_group, n_groups):
    cl, sl, sl_s = lo_ref[0], lo_ref[1], lo_ref[2]
    cos, sin = [], []
    for g in range(n_groups):
        ch, sh, sh_s = (hi_ref[q, pl.ds(first_group + g, 1), :] for q in range(3))
        cos.append(ch * cl - sh * sl)
        sin.append(sh_s * cl + ch * sl_s)
    return jnp.concatenate(cos, axis=0), jnp.concatenate(sin, axis=0)


def _rotate_half_unsigned(t):
    lane = lax.broadcasted_iota(jnp.int32, t.shape, 1)
    return jnp.where((lane % HEAD_DIM) < HEAD_DIM // 2, pltpu.roll(t, 96, 1), pltpu.roll(t, 32, 1))


def _rope(t, cos, sin_signed):
    return t * cos + _rotate_half_unsigned(t) * sin_signed


def _pair_blocks(base):
    out = []
    for j in range(4):
        for g in range(2):
            nat = base + HEAD_DIM * (4 * g + j)
            par = base + 128 * j + HEAD_DIM * g
            out.append((slice(nat, nat + HEAD_DIM), slice(par, par + HEAD_DIM)))
    return out


W_Q, W_KV, W_ZA, W_U, W_ZS = 0, 512, 768, 1280, 1792


def _proj(x, wt, rope_hi, rope_lo, shards, tb):
    seq = x.shape[0]
    steps = seq // tb
    n_sh = len(shards)

    def body(*refs):
        x_ref, wt_ref, hi_ref, lo_ref = refs[:4]
        shard_refs = refs[4:4 + n_sh]
        q_ref, k_ref, v_ref, za_ref, u_ref, zs_ref = refs[4 + n_sh:10 + n_sh]
        gathered_refs = refs[10 + n_sh:10 + 2 * n_sh]
        wp = refs[10 + 2 * n_sh]
        step = pl.program_id(0)
        if n_sh:
            landing_refs = refs[11 + 2 * n_sh:11 + 3 * n_sh]
            start, relay, finish = _gather_phases(shard_refs, landing_refs, *refs[11 + 3 * n_sh:], BF16)
            pl.when(step == 0)(start)
            pl.when(step == max(steps - 2, 0))(relay)

        @pl.when(step == 0)
        def _():
            for dst_base, src_base in ((0, W_Q), (512, W_ZA)):
                for nat, par in _pair_blocks(0):
                    wp[dst_base + par.start:dst_base + par.stop, :] = wt_ref[src_base + nat.start:src_base + nat.stop, :]

        xb = x_ref[...].astype(BF16)
        cos, sin = _rope_block(hi_ref, lo_ref, pl.program_id(0) * (tb // ROPE_GROUP), tb // ROPE_GROUP)
        lo = lax.broadcasted_iota(jnp.int32, (tb, 128), 1) < HEAD_DIM
        q = _dot_nt(xb, wp[0:512, :])
        for j in range(4):
            qj = _rope(q[:, 128 * j:128 * (j + 1)], cos, sin)
            q_ref[j] = jnp.where(lo, qj, 0.0).astype(BF16)
            q_ref[4 + j] = jnp.where(lo, 0.0, qj).astype(BF16)
        kv = _dot_nt(xb, wt_ref[W_KV:W_ZA, :])
        k_ref[...] = _rope(kv[:, 0:128], cos, sin).astype(BF16)
        v_ref[...] = kv[:, 128:256].astype(BF16)
        za_ref[...] = _dot_nt(xb, wp[512:1024, :])
        u_val = _dot_nt(xb, wt_ref[W_U:W_ZS, :])
        for k in range(N_SLAB):
            u_ref[k] = u_val[:, k * SLAB_IN:(k + 1) * SLAB_IN]
        zs_ref[...] = _dot_nt(xb, wt_ref[W_ZS:D_IN_PROJ, :])
        if n_sh:
            @pl.when(step == steps - 1)
            def _():
                finish()
                for a in range(n_sh):
                    gathered_refs[a][...] = landing_refs[a][...]

    row = lambda w: pl.BlockSpec((tb, w), lambda i: (i, 0))
    table = lambda t: pl.BlockSpec(t.shape, lambda i: (0, 0, 0))
    vmem = pl.BlockSpec(memory_space=pltpu.VMEM)
    return pl.pallas_call(
        body, name="proj", grid=(steps,),
        in_specs=[row(D_MODEL), pl.BlockSpec((D_IN_PROJ, D_MODEL), lambda i: (0, 0), pipeline_mode=pl.Buffered(1)),
                  table(rope_hi), table(rope_lo)] + [vmem] * n_sh,
        out_specs=[pl.BlockSpec((8, tb, 128), lambda i: (0, i, 0)), row(128), row(128), row(512),
                   pl.BlockSpec((N_SLAB, tb, SLAB_IN), lambda i: (0, i, 0)), row(512)] + [vmem] * n_sh,
        out_shape=[jax.ShapeDtypeStruct((8, seq, 128), BF16), jax.ShapeDtypeStruct((seq, 128), BF16),
                   jax.ShapeDtypeStruct((seq, 128), BF16), jax.ShapeDtypeStruct((seq, 512), F32),
                   jax.ShapeDtypeStruct((N_SLAB, seq, SLAB_IN), F32), jax.ShapeDtypeStruct((seq, 512), F32)]
        + [jax.ShapeDtypeStruct((N_CHIPS,) + s.shape, BF16) for s in shards],
        scratch_shapes=[pltpu.VMEM((1024, D_MODEL), BF16)] + [pltpu.VMEM((N_CHIPS,) + s.shape, BF16) for s in shards]
        + (_gather_sems(n_sh) if n_sh else []),
        compiler_params=_cparams(("arbitrary",)),
    )(x, wt, rope_hi, rope_lo, *shards)


ATT_TQ = 128
ATT_KEYS = ATT_TQ + 2 * WINDOW


def _attn_window(i, seq):
    start = jnp.clip(i * ATT_TQ - WINDOW, 0, seq - ATT_KEYS)
    return pl.multiple_of(start, WINDOW)


def _attn_bias():
    r = np.arange(ATT_TQ)[None, :, None]
    c = np.arange(ATT_KEYS)[None, None, :]
    off = np.array([0, WINDOW, ATT_KEYS - ATT_TQ])[:, None, None]
    return jnp.asarray(np.where(np.abs(r + off - c) <= WINDOW, 0.0, NEG_INF).astype(np.float32))


def _attn_bias_spec(nblk):
    pick = lambda i: jnp.where(i == 0, 0, jnp.where(i == nblk - 1, 2, 1))
    return pl.BlockSpec((None, ATT_TQ, ATT_KEYS), lambda i: (pick(i), 0, 0))


def _attn_softmax(q_ref, k_ref, v_ref, sink_ref, bias_ref, start):
    kw = k_ref[pl.ds(start, ATT_KEYS), :]
    vw = v_ref[pl.ds(start, ATT_KEYS), :]
    qall = q_ref[...].reshape(N_Q_HEADS * ATT_TQ, 128)
    s = (_dot_nt(qall, kw) * (HEAD_DIM ** -0.5)).reshape(N_Q_HEADS, ATT_TQ, ATT_KEYS) + bias_ref[...][None]
    tiles = [s[:, :, 128 * t:128 * (t + 1)] for t in range(ATT_KEYS // 128)]
    m = jnp.max(functools.reduce(jnp.maximum, tiles), axis=2, keepdims=True)
    sink = sink_ref[...]
    m_b = jnp.maximum(jnp.broadcast_to(m, (N_Q_HEADS, ATT_TQ, 128)), sink)
    p = jnp.concatenate([jnp.exp(t - m_b) for t in tiles], axis=2)
    p_sink = jnp.exp(sink - m_b)
    lo_k = lax.broadcasted_iota(jnp.int32, (ATT_KEYS, 128), 1) < HEAD_DIM
    v_f = vw.astype(F32)
    v_lo, v_hi = jnp.where(lo_k, v_f, 1.0).astype(BF16), jnp.where(lo_k, 1.0, v_f).astype(BF16)
    pb = p.astype(BF16).reshape(N_Q_HEADS * ATT_TQ, ATT_KEYS)
    half = 4 * ATT_TQ
    r = jnp.concatenate([_dot(pb[:half], v_lo), _dot(pb[half:], v_hi)], axis=0).reshape(N_Q_HEADS, ATT_TQ, 128)
    return kw, vw, qall, p, p_sink, r


def _attn_fwd(q_stack, k, v, sink128, bias):
    seq = k.shape[0]

    def body(q_ref, k_ref, v_ref, sink_ref, bias_ref, o_ref):
        start = _attn_window(pl.program_id(0), seq)
        _, _, _, _, p_sink, r = _attn_softmax(q_ref, k_ref, v_ref, sink_ref, bias_ref, start)
        out = r / (pltpu.roll(r, HEAD_DIM, 2) + p_sink)
        lo = lax.broadcasted_iota(jnp.int32, (ATT_TQ, 128), 1) < HEAD_DIM
        for j in range(4):
            o_ref[:, 128 * j:128 * (j + 1)] = jnp.where(lo, out[j], out[4 + j])

    full = lambda w: pl.BlockSpec((seq, w), lambda i: (0, 0))
    return pl.pallas_call(
        body, name="attn_fwd", grid=(seq // ATT_TQ,),
        in_specs=[pl.BlockSpec((8, ATT_TQ, 128), lambda i: (0, i, 0)), full(128), full(128),
                  pl.BlockSpec((N_Q_HEADS, 1, 128), lambda i: (0, 0, 0)), _attn_bias_spec(seq // ATT_TQ)],
        out_specs=pl.BlockSpec((ATT_TQ, 512), lambda i: (i, 0)),
        out_shape=jax.ShapeDtypeStruct((seq, 512), F32),
        compiler_params=_cparams(("arbitrary",)),
    )(q_stack, k, v, sink128, bias)


def _attn_bwd(q_stack, k, v, sink128, bias, d_o, pieces):
    seq = k.shape[0]
    steps = seq // ATT_TQ
    n_p = len(pieces)

    def body(*refs):
        q_ref, k_ref, v_ref, sink_ref, bias_ref, do_ref = refs[:6]
        piece_refs = refs[6:6 + n_p]
        dq_ref, dk_ref, dv_ref, dsink_ref = refs[6 + n_p:10 + n_p]
        reduced_refs = refs[10 + n_p:10 + 2 * n_p]
        sink_acc = refs[10 + 2 * n_p]
        i = pl.program_id(0)
        if n_p:
            landing_refs = refs[11 + 2 * n_p:11 + 3 * n_p]
            scratch = refs[11 + 3 * n_p:]
            begin, exchange, combine, finish = _reduce_phases(
                piece_refs, landing_refs, scratch[:n_p], scratch[n_p:2 * n_p], scratch[2 * n_p:3 * n_p],
                *scratch[3 * n_p:], [True] * n_p, gather_last=False)
            pl.when(i == 0)(begin)
            pl.when(i == min(4, steps - 1))(exchange)
            pl.when(i == (3 * steps) // 4)(combine)

        @pl.when(i == 0)
        def _():
            dk_ref[...] = jnp.zeros_like(dk_ref)
            dv_ref[...] = jnp.zeros_like(dv_ref)
            sink_acc[...] = jnp.zeros_like(sink_acc)

        start = _attn_window(i, seq)
        kw, vw, qall, p, p_sink, r = _attn_softmax(q_ref, k_ref, v_ref, sink_ref, bias_ref, start)
        lo = lax.broadcasted_iota(jnp.int32, (ATT_TQ, 128), 1) < HEAD_DIM
        lo3 = lo[None]
        grp0 = lax.broadcasted_iota(jnp.int32, (N_Q_HEADS, ATT_TQ, 128), 0) < 4
        val = grp0 == lo3
        swapped = pltpu.roll(r, HEAD_DIM, 2)
        inv = 1.0 / (jnp.where(val, swapped, r) + p_sink)
        d_o_blk = do_ref[...]
        do3 = jnp.where(val, jnp.concatenate([d_o_blk[None, :, 128 * j:128 * (j + 1)] for j in range(4)] * 2, axis=0), 0.0)
        t = (do3 * r).reshape(N_Q_HEADS * ATT_TQ, 128)
        t_hi = t.astype(BF16)
        t_lo = (t - t_hi.astype(F32)).astype(BF16)
        ones = jnp.ones((128, 128), BF16)
        delta = (_dot(t_hi, ones) + _dot(t_lo, ones)).reshape(N_Q_HEADS, ATT_TQ, 128) * inv
        sink_acc[...] += -(p_sink * inv) * delta
        do_all = do3.astype(BF16).reshape(N_Q_HEADS * ATT_TQ, 128)
        dp = _dot_nt(do_all, vw).reshape(N_Q_HEADS, ATT_TQ, ATT_KEYS)
        probs, ds = [], []
        for tl in range(ATT_KEYS // 128):
            cols = slice(128 * tl, 128 * (tl + 1))
            probs_t = p[:, :, cols] * inv
            probs.append(probs_t.astype(BF16))
            ds.append((probs_t * (dp[:, :, cols] - delta)).astype(BF16))
        probs_all = jnp.concatenate(probs, axis=2).reshape(N_Q_HEADS * ATT_TQ, ATT_KEYS)
        ds_all = jnp.concatenate(ds, axis=2).reshape(N_Q_HEADS * ATT_TQ, ATT_KEYS)
        scale = HEAD_DIM ** -0.5
        dq_all = (_dot(ds_all, kw) * scale).reshape(N_Q_HEADS, ATT_TQ, 128)
        for j in range(4):
            dq_ref[:, 128 * j:128 * (j + 1)] = jnp.where(lo, dq_all[j], dq_all[4 + j])
        dk_ref[pl.ds(start, ATT_KEYS), :] += _dot_tn(ds_all, qall) * scale
        dv_ref[pl.ds(start, ATT_KEYS), :] += _dot_tn(probs_all, do_all)

        @pl.when(i == steps - 1)
        def _():
            dsink_ref[...] = jnp.sum(sink_acc[...], axis=1)

        if n_p:
            @pl.when(i == steps - 1)
            def _():
                finish()
                for a in range(n_p):
                    reduced_refs[a][...] = landing_refs[a][...]

    full = lambda w: pl.BlockSpec((seq, w), lambda i: (0, 0))
    vmem = pl.BlockSpec(memory_space=pltpu.VMEM)
    return pl.pallas_call(
        body, name="attn_bwd", grid=(steps,),
        in_specs=[pl.BlockSpec((8, ATT_TQ, 128), lambda i: (0, i, 0)), full(128), full(128),
                  pl.BlockSpec((N_Q_HEADS, 1, 128), lambda i: (0, 0, 0)),
                  _attn_bias_spec(steps), pl.BlockSpec((ATT_TQ, 512), lambda i: (i, 0))] + [vmem] * n_p,
        out_specs=[pl.BlockSpec((ATT_TQ, 512), lambda i: (i, 0)), full(128), full(128),
                   pl.BlockSpec((N_Q_HEADS, 128), lambda i: (0, 0))] + [vmem] * n_p,
        out_shape=[jax.ShapeDtypeStruct((seq, 512), F32), jax.ShapeDtypeStruct((seq, 128), F32),
                   jax.ShapeDtypeStruct((seq, 128), F32), jax.ShapeDtypeStruct((N_Q_HEADS, 128), F32)]
        + [jax.ShapeDtypeStruct(p.shape[1:], F32) for p in pieces],
        scratch_shapes=[pltpu.VMEM((N_Q_HEADS, ATT_TQ, 128), F32)] + [pltpu.VMEM(p.shape[1:], F32) for p in pieces]
        + (_reduce_scratch([p.shape for p in pieces], [True] * n_p) if n_p else []),
        compiler_params=_cparams(("arbitrary",)),
    )(q_stack, k, v, sink128, bias, d_o, *pieces)


def _permute_rows(dst_ref, src_ref, sub_len):
    for k in range(N_SLAB):
        for j in range(sub_len):
            dst_ref[k, 8 * j:8 * (j + 1), :] = src_ref.at[k][pl.ds(j, SUBSEG, stride=sub_len), :]


def _unpermute_rows(dst_ref, src_ref, sub_len):
    for k in range(N_SLAB):
        for s in range(SUBSEG):
            dst_ref[k, s * sub_len:(s + 1) * sub_len, :] = src_ref.at[k][pl.ds(s, sub_len, stride=SUBSEG), :]


def _scan_chunk(br_ref, bi_ref, lr_row, li_row, init, cols, *, sub_len, reverse, store):
    lr = jnp.broadcast_to(lr_row[:, cols], (SUBSEG, SCAN_LANES))
    li = jnp.broadcast_to(li_row[:, cols], (SUBSEG, SCAN_LANES))
    if init is None:
        sr = si = jnp.zeros((SUBSEG, SCAN_LANES), F32)
    else:
        sr, si = init
    for jj in range(sub_len):
        rows = slice(SUBSEG * ((sub_len - 1 - jj) if reverse else jj), SUBSEG * (((sub_len - 1 - jj) if reverse else jj) + 1))
        sr, si = lr * sr - li * si + br_ref[rows, cols], lr * si + li * sr + bi_ref[rows, cols]
        if store:
            br_ref[rows, cols] = sr
            bi_ref[rows, cols] = si
    return sr, si


def _resolve_chunk(z, carry_refs, start_refs, pr_row, pi_row, cols, *, reverse):
    cr, ci = carry_refs[0][0:1, cols], carry_refs[1][0:1, cols]
    pr, pi = pr_row[:, cols], pi_row[:, cols]
    for s in (range(SUBSEG - 1, -1, -1) if reverse else range(SUBSEG)):
        start_refs[0][s:s + 1, cols] = cr
        start_refs[1][s:s + 1, cols] = ci
        cr, ci = pr * cr - pi * ci + z[0][s:s + 1, :], pr * ci + pi * cr + z[1][s:s + 1, :]
    carry_refs[0][0:1, cols] = cr
    carry_refs[1][0:1, cols] = ci


def _param_specs(direction):
    row = lambda q: pl.BlockSpec((None, None, 1, STATE_W), lambda i: (q, direction, 0, 0))
    wide = lambda q: pl.BlockSpec((None, None, N_SLAB, SLAB_IN, SLAB_ST), lambda i: (q, direction, 0, 0, 0))
    tall = lambda q: pl.BlockSpec((None, None, N_SLAB, SLAB_ST, SLAB_IN), lambda i: (q, direction, 0, 0, 0))
    return [row(q) for q in range(4)], [wide(0), wide(1)], [tall(0), tall(1)]


def _ssm_fwd(u, lam, bb, cb, *, direction, tb, name):
    reverse = direction == 1
    seq = u.shape[1]
    nblk = seq // tb
    sub_len = tb // SUBSEG

    def body(u_ref, lr_ref, li_ref, pr_ref, pi_ref, bbr_ref, bbi_ref, cbr_ref, cbi_ref,
             y_ref, sr_ref, si_ref, xr, xi, up, yp, car, cai):
        @pl.when(pl.program_id(0) == 0)
        def _():
            car[...] = jnp.zeros_like(car)
            cai[...] = jnp.zeros_like(cai)

        _permute_rows(up, u_ref, sub_len)
        lr, li, pr, pi = lr_ref[...], li_ref[...], pr_ref[...], pi_ref[...]
        chunk = lambda k: slice(k * SLAB_ST, (k + 1) * SLAB_ST)

        def drive(k):
            ub = up[k].astype(BF16)
            xr[:, chunk(k)] = _dot(ub, bbr_ref[k])
            xi[:, chunk(k)] = _dot(ub, bbi_ref[k])

        def scan(k):
            z = _scan_chunk(xr, xi, lr, li, None, chunk(k), sub_len=sub_len, reverse=reverse, store=False)
            _resolve_chunk(z, (car, cai), (sr_ref, si_ref), pr, pi, chunk(k), reverse=reverse)
            _scan_chunk(xr, xi, lr, li, (sr_ref[:, chunk(k)], si_ref[:, chunk(k)]), chunk(k),
                        sub_len=sub_len, reverse=reverse, store=True)

        def read_out(k):
            yp[k] = _dot(xr[:, chunk(k)].astype(BF16), cbr_ref[k]) - _dot(xi[:, chunk(k)].astype(BF16), cbi_ref[k])

        drive(0)
        for k in range(N_SLAB):
            if k + 1 < N_SLAB:
                drive(k + 1)
            scan(k)
            if k > 0:
                read_out(k - 1)
        read_out(N_SLAB - 1)
        _unpermute_rows(y_ref, yp, sub_len)

    blk = (lambda i: nblk - 1 - i) if reverse else (lambda i: i)
    rows, wide, tall = _param_specs(direction)
    tok = pl.BlockSpec((N_SLAB, tb, SLAB_IN), lambda i: (0, blk(i), 0))
    start_spec = pl.BlockSpec((None, SUBSEG, STATE_W), lambda i: (blk(i), 0, 0))
    return pl.pallas_call(
        body, name=name, grid=(nblk,),
        in_specs=[tok] + rows + wide + tall,
        out_specs=[tok, start_spec, start_spec],
        out_shape=[jax.ShapeDtypeStruct((N_SLAB, seq, SLAB_IN), F32), jax.ShapeDtypeStruct((nblk, SUBSEG, STATE_W), F32),
                   jax.ShapeDtypeStruct((nblk, SUBSEG, STATE_W), F32)],
        scratch_shapes=[pltpu.VMEM((tb, STATE_W), F32), pltpu.VMEM((tb, STATE_W), F32),
                        pltpu.VMEM((N_SLAB, tb, SLAB_IN), F32), pltpu.VMEM((N_SLAB, tb, SLAB_IN), F32),
                        pltpu.VMEM((SUBSEG, STATE_W), F32), pltpu.VMEM((SUBSEG, STATE_W), F32)],
        compiler_params=_cparams(("arbitrary",)),
    )(u, lam, lam, lam, lam, bb, bb, cb, cb)


def _ssm_bwd(u, dy, starts, lam, bb, bbt, cb_t, *, direction, tb, name):
    reverse = direction == 1
    seq = u.shape[1]
    nblk = seq // tb
    sub_len = tb // SUBSEG

    def body(u_ref, dy_ref, sr_ref, si_ref, lr_ref, li_ref, pr_ref, pi_ref, bbr_ref, bbi_ref, btr_ref, bti_ref,
             ctr_ref, cti_ref, du_ref, gb_ref, gc_ref, dl_ref,
             xr, xi, gr, gi, up, dyp, dup, gsr, gsi, car, cai):
        gbr_ref, gbi_ref = gb_ref.at[0], gb_ref.at[1]
        gcr_ref, gci_ref = gc_ref.at[0], gc_ref.at[1]
        dlr_ref, dli_ref = dl_ref.at[0], dl_ref.at[1]

        @pl.when(pl.program_id(0) == 0)
        def _():
            for ref in (car, cai, gbr_ref, gbi_ref, gcr_ref, gci_ref, dlr_ref, dli_ref):
                ref[...] = jnp.zeros_like(ref)

        _permute_rows(up, u_ref, sub_len)
        _permute_rows(dyp, dy_ref, sub_len)
        lr, li, pr, pi = lr_ref[...], li_ref[...], pr_ref[...], pi_ref[...]
        nli, npi = -li, -pi
        chunk = lambda k: slice(k * SLAB_ST, (k + 1) * SLAB_ST)

        def drive(k):
            ub = up[k].astype(BF16)
            xr[:, chunk(k)] = _dot(ub, bbr_ref[k])
            xi[:, chunk(k)] = _dot(ub, bbi_ref[k])
            dyb = dyp[k].astype(BF16)
            gr[:, chunk(k)] = _dot(dyb, ctr_ref[k])
            gi[:, chunk(k)] = -_dot(dyb, cti_ref[k])

        def scan_x(k):
            _scan_chunk(xr, xi, lr, li, (sr_ref[:, chunk(k)], si_ref[:, chunk(k)]), chunk(k),
                        sub_len=sub_len, reverse=reverse, store=True)

        def grad_c(k):
            dyb = dyp[k].astype(BF16)
            gcr_ref[k] += _dot_tn(dyb, xr[:, chunk(k)].astype(BF16))
            gci_ref[k] -= _dot_tn(dyb, xi[:, chunk(k)].astype(BF16))

        def scan_g(k):
            z = _scan_chunk(gr, gi, lr, nli, None, chunk(k), sub_len=sub_len, reverse=not reverse, store=False)
            _resolve_chunk(z, (car, cai), (gsr, gsi), pr, npi, chunk(k), reverse=not reverse)
            _scan_chunk(gr, gi, lr, nli, (gsr[:, chunk(k)], gsi[:, chunk(k)]), chunk(k),
                        sub_len=sub_len, reverse=not reverse, store=True)

        def grad_b_du(k):
            ub = up[k].astype(BF16)
            grb, gib = gr[:, chunk(k)].astype(BF16), gi[:, chunk(k)].astype(BF16)
            gbr_ref[k] += _dot_tn(ub, grb)
            gbi_ref[k] += _dot_tn(ub, gib)
            dup[k] = _dot(grb, btr_ref[k]) + _dot(gib, bti_ref[k])

        def grad_lambda(k):
            cols = chunk(k)
            acc_r, acc_i = dlr_ref[:, cols], dli_ref[:, cols]
            for jj in range(sub_len):
                prev = jj + 1 if reverse else jj - 1
                if 0 <= prev < sub_len:
                    x_r, x_i = xr[SUBSEG * prev:SUBSEG * (prev + 1), cols], xi[SUBSEG * prev:SUBSEG * (prev + 1), cols]
                else:
                    x_r, x_i = sr_ref[:, cols], si_ref[:, cols]
                g_r, g_i = gr[SUBSEG * jj:SUBSEG * (jj + 1), cols], gi[SUBSEG * jj:SUBSEG * (jj + 1), cols]
                acc_r = acc_r + (g_r * x_r + g_i * x_i)
                acc_i = acc_i + (g_i * x_r - g_r * x_i)
            dlr_ref[:, cols] = acc_r
            dli_ref[:, cols] = acc_i

        drive(0)
        for k in range(N_SLAB):
            if k + 1 < N_SLAB:
                drive(k + 1)
            scan_x(k)
            grad_c(k)
            scan_g(k)
            grad_b_du(k)
            grad_lambda(k)
        _unpermute_rows(du_ref, dup, sub_len)

    blk = (lambda i: i) if reverse else (lambda i: nblk - 1 - i)
    rows, wide, tall = _param_specs(direction)
    tok = pl.BlockSpec((N_SLAB, tb, SLAB_IN), lambda i: (0, blk(i), 0))
    start_spec = pl.BlockSpec((None, SUBSEG, STATE_W), lambda i: (blk(i), 0, 0))
    gb_shape, dl_shape = (2, N_SLAB, SLAB_IN, SLAB_ST), (2, SUBSEG, STATE_W)
    whole = lambda shape: pl.BlockSpec(shape, lambda i: (0,) * len(shape))
    big = lambda: pltpu.VMEM((tb, STATE_W), F32)
    slabs = lambda: pltpu.VMEM((N_SLAB, tb, SLAB_IN), F32)
    tile = lambda: pltpu.VMEM((SUBSEG, STATE_W), F32)
    return pl.pallas_call(
        body, name=name, grid=(nblk,),
        in_specs=[tok, tok, start_spec, start_spec] + rows + wide + tall + wide,
        out_specs=[tok, whole(gb_shape), whole(gb_shape), whole(dl_shape)],
        out_shape=[jax.ShapeDtypeStruct((N_SLAB, seq, SLAB_IN), F32), jax.ShapeDtypeStruct(gb_shape, F32),
                   jax.ShapeDtypeStruct(gb_shape, F32), jax.ShapeDtypeStruct(dl_shape, F32)],
        scratch_shapes=[big(), big(), big(), big(), slabs(), slabs(), slabs(), tile(), tile(), tile(), tile()],
        compiler_params=_cparams(("arbitrary",)),
    )(u, dy, *starts, lam, lam, lam, lam, bb, bb, bbt, bbt, cb_t, cb_t)


GELU_C = math.sqrt(2.0 / math.pi)
GELU_K = 0.044715


def _mid(o, za, u, y_f, y_b, zs, x, target, ssm_d, w_glu, b_glu, g_attn, g_ssm, w_out, ln_g, ln_b, tb):
    seq = x.shape[0]

    def body(o_ref, za_ref, u_ref, yf_ref, yb_ref, zs_ref, x_ref, t_ref, d_ref, wg_ref, bg_ref, ga_ref, gs_ref,
             wo_ref, lg_ref, lb_ref,
             loss_ref, do_ref, dza_ref, dyl_ref, dzs_ref, dpre_ref, gwo_ref, gwg_ref, vec_ref, wop):
        @pl.when(pl.program_id(0) == 0)
        def _():
            for ref in (loss_ref, gwo_ref, gwg_ref, vec_ref):
                ref[...] = jnp.zeros_like(ref)
            for nat, par in _pair_blocks(0):
                wop[par, :] = wo_ref[nat, :]
            wop[D_ATTN:, :] = wo_ref[D_ATTN:, :]

        def rows_of(rs):
            o, za = o_ref[rs, :], za_ref[rs, :]
            sig_a = _sigmoid(za)
            silu_a = za * sig_a
            ya = o * silu_a
            r_a = lax.rsqrt(jnp.mean(ya * ya, axis=1, keepdims=True) + NORM_EPS)
            n_a = ya * r_a
            g_a = ga_ref[...]
            unslab = lambda ref: jnp.concatenate([ref[k, rs, :] for k in range(N_SLAB)], axis=1)
            u_blk, zs = unslab(u_ref), zs_ref[rs, :]
            d_row = d_ref[...]
            ylin = d_row * u_blk + unslab(yf_ref) + unslab(yb_ref)
            inner = GELU_C * (ylin + GELU_K * ylin * ylin * ylin)
            th = jnp.tanh(inner)
            gl = 0.5 * ylin * (1.0 + th)
            glb = gl.astype(BF16)
            gate = _dot(glb, wg_ref[...])
            sg = _sigmoid(gate + bg_ref[...])
            y2 = gl * sg
            sig_s = _sigmoid(zs)
            silu_s = zs * sig_s
            ys = y2 * silu_s
            r_s = lax.rsqrt(jnp.mean(ys * ys, axis=1, keepdims=True) + NORM_EPS)
            n_s = ys * r_s
            g_s = gs_ref[...]
            mixed = jnp.concatenate([n_a * g_a, n_s * g_s], axis=1).astype(BF16)
            out = _dot(mixed, wop[...])
            pre = ALPHA * x_ref[rs, :] + out
            mu = jnp.mean(pre, axis=1, keepdims=True)
            cen = pre - mu
            rstd = lax.rsqrt(jnp.mean(cen * cen, axis=1, keepdims=True) + NORM_EPS)
            hhat = cen * rstd
            ln_g = lg_ref[...]
            err = hhat * ln_g + lb_ref[...] - t_ref[rs, :]
            loss_ref[...] += 0.5 * jnp.sum(jnp.mean(err * err, axis=1, keepdims=True))

            dh = err * (1.0 / D_MODEL)
            vec_ref[0:1, :] += jnp.sum(dh * hhat, axis=0, keepdims=True)
            vec_ref[1:2, :] += jnp.sum(dh, axis=0, keepdims=True)
            dhh = dh * ln_g
            dpre = rstd * (dhh - jnp.mean(dhh, axis=1, keepdims=True)
                           - hhat * jnp.mean(dhh * hhat, axis=1, keepdims=True))
            dpre_ref[rs, :] = dpre
            dpb = dpre.astype(BF16)
            for j in range(4):
                g_pair = _dot_tn(mixed[:, 128 * j:128 * (j + 1)], dpb)
                for g in range(2):
                    nat = HEAD_DIM * (4 * g + j)
                    gwo_ref[nat:nat + HEAD_DIM, :] += g_pair[HEAD_DIM * g:HEAD_DIM * (g + 1), :]
            gwo_ref[D_ATTN:, :] += _dot_tn(mixed[:, D_ATTN:], dpb)
            dmix = _dot_nt(dpb, wop[...])
            dna = dmix[:, :D_ATTN]
            vec_ref[2:3, 0:D_ATTN] += jnp.sum(dna * n_a, axis=0, keepdims=True)
            dna = dna * g_a
            dya = r_a * (dna - n_a * jnp.mean(dna * n_a, axis=1, keepdims=True))
            do_ref[rs, :] = dya * silu_a
            dza_ref[rs, :] = dya * o * (sig_a * (1.0 + za * (1.0 - sig_a)))
            dns = dmix[:, D_ATTN:]
            vec_ref[2:3, D_ATTN:] += jnp.sum(dns * n_s, axis=0, keepdims=True)
            dns = dns * g_s
            dys = r_s * (dns - n_s * jnp.mean(dns * n_s, axis=1, keepdims=True))
            dzs_ref[rs, :] = dys * y2 * (sig_s * (1.0 + zs * (1.0 - sig_s)))
            dy2 = dys * silu_s
            da = dy2 * gl * sg * (1.0 - sg)
            vec_ref[3:4, D_SSM:] += jnp.sum(da, axis=0, keepdims=True)
            dab = da.astype(BF16)
            gwg_ref[...] += _dot_tn(glb, dab)
            dgl_mm = _dot_nt(dab, wg_ref[...])
            dgl = dy2 * sg + dgl_mm
            dylin = dgl * (0.5 * (1.0 + th)
                           + 0.5 * ylin * (1.0 - th * th) * GELU_C * (1.0 + 3.0 * GELU_K * ylin * ylin))
            for k in range(N_SLAB):
                dyl_ref[k, rs, :] = dylin[:, k * SLAB_IN:(k + 1) * SLAB_IN]
            vec_ref[3:4, 0:D_SSM] += jnp.sum(dylin * u_blk, axis=0, keepdims=True)

        rows_of(slice(0, tb))

    tok = lambda w: pl.BlockSpec((tb, w), lambda i: (i, 0))
    slab = pl.BlockSpec((N_SLAB, tb, SLAB_IN), lambda i: (0, i, 0))
    const = lambda r, c: pl.BlockSpec((r, c), lambda i: (0, 0), pipeline_mode=pl.Buffered(1))
    tok_shape = jax.ShapeDtypeStruct((seq, 512), F32)
    return pl.pallas_call(
        body, name="mid", grid=(seq // tb,),
        in_specs=[tok(512), tok(512), slab, slab, slab, tok(512), tok(1024), tok(1024),
                  const(1, 512), const(512, 512), const(1, 512), const(1, 512), const(1, 512),
                  const(1024, 1024), const(1, 1024), const(1, 1024)],
        out_specs=[const(8, 128), tok(512), tok(512), slab, tok(512), tok(1024),
                   const(1024, 1024), const(512, 512), const(8, 1024)],
        out_shape=[jax.ShapeDtypeStruct((8, 128), F32), tok_shape, tok_shape,
                   jax.ShapeDtypeStruct((N_SLAB, seq, SLAB_IN), F32), tok_shape,
                   jax.ShapeDtypeStruct((seq, 1024), F32), jax.ShapeDtypeStruct((1024, 1024), F32),
                   jax.ShapeDtypeStruct((512, 512), F32), jax.ShapeDtypeStruct((8, 1024), F32)],
        scratch_shapes=[pltpu.VMEM((D_MODEL, D_MODEL), BF16)],
        compiler_params=_cparams(("arbitrary",)),
    )(o, za, u, y_f, y_b, zs, x, target, ssm_d, w_glu, b_glu, g_attn, g_ssm, w_out, ln_g, ln_b)


def _ride_shapes(pieces, narrow, gather_last):
    outs = [p.shape if (gather_last and a == len(pieces) - 1) else p.shape[1:] for a, p in enumerate(pieces)]
    return outs, [pltpu.VMEM(s, F32) for s in outs] + _reduce_scratch([p.shape for p in pieces], narrow)


def _ride_phases(piece_refs, out_refs, scratch_refs, narrow, gather_last):
    n = len(piece_refs)
    landing, rest = scratch_refs[:n], scratch_refs[n:]
    begin, exchange, combine, finish = _reduce_phases(piece_refs, landing, rest[:n], rest[n:2 * n], rest[2 * n:3 * n],
                                                      *rest[3 * n:], narrow, gather_last)

    def end():
        finish()
        for a in range(n):
            out_refs[a][...] = landing[a][...]

    return begin, exchange, combine, end


def _dproj_block(dq_ref, dk_ref, dv_ref, dza_ref, duf_ref, dub_ref, dyl_ref, dzs_ref, d_ref, hi_ref, lo_ref, tb):
    cos, sin = _rope_block(hi_ref, lo_ref, pl.program_id(0) * (tb // ROPE_GROUP), tb // ROPE_GROUP)
    lo = lax.broadcasted_iota(jnp.int32, (tb, 128), 1) < HEAD_DIM

    def unrope(t):
        return t * cos + _rotate_half_unsigned(t * sin)

    def natural(pairs):
        swapped = [pltpu.roll(t, HEAD_DIM, 1) for t in pairs]
        return [jnp.where(lo, pairs[0], swapped[1]), jnp.where(lo, pairs[2], swapped[3]),
                jnp.where(lo, swapped[0], pairs[1]), jnp.where(lo, swapped[2], pairs[3])]

    dq_rot, dza = dq_ref[...], dza_ref[...]
    pieces = natural([unrope(dq_rot[:, 128 * j:128 * (j + 1)]) for j in range(4)])
    d_row = d_ref[...]
    pieces += [unrope(dk_ref[...]), dv_ref[...]] + natural([dza[:, 128 * j:128 * (j + 1)] for j in range(4)])
    pieces += [duf_ref[k] + dub_ref[k] + d_row[:, k * SLAB_IN:(k + 1) * SLAB_IN] * dyl_ref[k] for k in range(N_SLAB)]
    pieces += [dzs_ref[...]]
    return jnp.concatenate(pieces, axis=1).astype(BF16)


def _dproj_specs(tb, rope_hi, rope_lo):
    tok = lambda w: pl.BlockSpec((tb, w), lambda i: (i, 0))
    slab = pl.BlockSpec((N_SLAB, tb, SLAB_IN), lambda i: (0, i, 0))
    table = lambda t: pl.BlockSpec(t.shape, lambda i: (0, 0, 0))
    return [tok(512), tok(128), tok(128), tok(512), slab, slab, slab, tok(512), pl.BlockSpec((1, 512), lambda i: (0, 0)),
            table(rope_hi), table(rope_lo)]


N_DPROJ = 11
GW_ROWS = 768


def _proj_bwd_w(x, dproj_args, rope_hi, rope_lo, pieces, tb):
    seq = x.shape[0]
    steps = seq // tb
    n_p = len(pieces)
    narrow = [False] * n_p

    def body(*refs):
        x_ref, grads = refs[0], refs[1:1 + N_DPROJ]
        piece_refs = refs[1 + N_DPROJ:1 + N_DPROJ + n_p]
        gw_ref = refs[1 + N_DPROJ + n_p]
        out_refs = refs[2 + N_DPROJ + n_p:2 + N_DPROJ + 2 * n_p]
        step = pl.program_id(0)
        if n_p:
            begin, exchange, combine, end = _ride_phases(piece_refs, out_refs, refs[2 + N_DPROJ + 2 * n_p:], narrow, True)
            pl.when(step == 0)(begin)
            pl.when(step == min(1, steps - 1))(exchange)
            pl.when(step == steps // 2)(combine)

        @pl.when(step == 0)
        def _():
            gw_ref[...] = jnp.zeros_like(gw_ref)

        dproj = _dproj_block(*grads, tb)
        xb = x_ref[...].astype(BF16)
        for r0 in range(0, D_IN_PROJ, GW_ROWS):
            gw_ref[r0:r0 + GW_ROWS, :] += _dot_tn(dproj[:, r0:r0 + GW_ROWS], xb)
        if n_p:
            pl.when(step == steps - 1)(end)

    vmem = pl.BlockSpec(memory_space=pltpu.VMEM)
    whole = pl.BlockSpec((D_IN_PROJ, D_MODEL), lambda i: (0, 0), pipeline_mode=pl.Buffered(1))
    ride_outs, ride_scratch = _ride_shapes(pieces, narrow, True) if n_p else ([], [])
    return pl.pallas_call(
        body, name="proj_bwd_w", grid=(steps,),
        in_specs=[pl.BlockSpec((tb, D_MODEL), lambda i: (i, 0))] + _dproj_specs(tb, rope_hi, rope_lo) + [vmem] * n_p,
        out_specs=[whole] + [vmem] * n_p,
        out_shape=[jax.ShapeDtypeStruct((D_IN_PROJ, D_MODEL), F32)] + [jax.ShapeDtypeStruct(s, F32) for s in ride_outs],
        scratch_shapes=ride_scratch,
        compiler_params=_cparams(("arbitrary",)),
    )(x, *dproj_args, rope_hi, rope_lo, *pieces)


def _proj_bwd_x(dproj_args, rope_hi, rope_lo, dpre, wt, pieces, tb):
    seq = dpre.shape[0]
    steps = seq // tb
    n_p = len(pieces)
    narrow = [True] * n_p

    def body(*refs):
        grads = refs[:N_DPROJ]
        dpre_ref, wt_ref = refs[N_DPROJ:N_DPROJ + 2]
        piece_refs = refs[N_DPROJ + 2:N_DPROJ + 2 + n_p]
        gx_ref = refs[N_DPROJ + 2 + n_p]
        out_refs = refs[N_DPROJ + 3 + n_p:N_DPROJ + 3 + 2 * n_p]
        step = pl.program_id(0)
        if n_p:
            begin, exchange, combine, end = _ride_phases(piece_refs, out_refs, refs[N_DPROJ + 3 + 2 * n_p:], narrow, False)
            pl.when(step == 0)(begin)
            pl.when(step == min(1, steps - 1))(exchange)
            pl.when(step == steps - 1)(combine)

        dproj = _dproj_block(*grads, tb)
        gx_ref[...] = ALPHA * dpre_ref[...] + _dot(dproj, wt_ref[...])
        if n_p:
            pl.when(step == steps - 1)(end)

    vmem = pl.BlockSpec(memory_space=pltpu.VMEM)
    whole = pl.BlockSpec((D_IN_PROJ, D_MODEL), lambda i: (0, 0), pipeline_mode=pl.Buffered(1))
    ride_outs, ride_scratch = _ride_shapes(pieces, narrow, False) if n_p else ([], [])
    return pl.pallas_call(
        body, name="proj_bwd_x", grid=(steps,),
        in_specs=_dproj_specs(tb, rope_hi, rope_lo) + [pl.BlockSpec((tb, D_MODEL), lambda i: (i, 0)), whole] + [vmem] * n_p,
        out_specs=[pl.BlockSpec((tb, D_MODEL), lambda i: (i, 0))] + [vmem] * n_p,
        out_shape=[jax.ShapeDtypeStruct((seq, D_MODEL), F32)] + [jax.ShapeDtypeStruct(s, F32) for s in ride_outs],
        scratch_shapes=ride_scratch,
        compiler_params=pltpu.CompilerParams(dimension_semantics=("arbitrary",), vmem_limit_bytes=PROJ_BWD_X_VMEM),
    )(*dproj_args, rope_hi, rope_lo, dpre, wt, *pieces)


def _adamw(w, g, m, v, name):
    rows, cols = w.shape
    tb = rows
    while tb * cols * 4 > ADAMW_BLOCK_BYTES and tb % 16 == 0:
        tb //= 2

    def body(w_ref, g_ref, m_ref, v_ref, d_ref, nm_ref, nv_ref):
        _adamw_update(w_ref, g_ref, m_ref, v_ref, d_ref, nm_ref, nv_ref)

    spec = pl.BlockSpec((tb, cols), lambda i: (i, 0))
    return pl.pallas_call(
        body, name=name, grid=(rows // tb,), in_specs=[spec] * 4, out_specs=[spec] * 3,
        out_shape=[jax.ShapeDtypeStruct((rows, cols), F32)] * 3,
        compiler_params=_cparams(("arbitrary",)),
    )(w, g, m, v)


def _adamw_update(w_ref, g_ref, m_ref, v_ref, d_ref, nm_ref, nv_ref):
    g_blk = g_ref[...]
    m_new = ADAM_B1 * m_ref[...] + (1.0 - ADAM_B1) * g_blk
    v_new = ADAM_B2 * v_ref[...] + (1.0 - ADAM_B2) * (g_blk * g_blk)
    m_hat = m_new / (1.0 - ADAM_B1 ** ADAM_STEP)
    v_hat = v_new / (1.0 - ADAM_B2 ** ADAM_STEP)
    d_ref[...] = -ADAM_LR * (m_hat / (jnp.sqrt(v_hat) + ADAM_EPS) + ADAM_WD * w_ref[...])
    nm_ref[...] = m_new
    nv_ref[...] = v_new


def _adamw_many(groups, name):
    n = len(groups)

    def body(*refs):
        for p in range(n):
            w_ref, g_ref, m_ref, v_ref = refs[4 * p:4 * p + 4]
            gn_ref, d_ref, nm_ref, nv_ref = refs[4 * n + 4 * p:4 * n + 4 * p + 4]
            gn_ref[...] = g_ref[...].reshape(w_ref.shape)
            _adamw_update(w_ref, gn_ref, m_ref, v_ref, d_ref, nm_ref, nv_ref)

    return pl.pallas_call(
        body, name=name,
        out_shape=[jax.ShapeDtypeStruct(grp[0].shape, F32) for grp in groups for _ in range(4)],
    )(*[a for grp in groups for a in grp])


_WEIGHTS = ["w_in", "attn_sink", "ssm_a_re", "ssm_a_im", "ssm_log_dt", "ssm_b_re", "ssm_b_im", "ssm_c_re", "ssm_c_im",
            "ssm_d", "w_glu", "b_glu", "norm_attn_g", "norm_ssm_g", "w_out", "ln_g", "ln_b"]
N_DG = N_DIR * N_GROUPS
BIG_ROWS = N_DG * SSM_CH * SSM_STATE // 128
TINY_ROWS = 64


def _pack_small_grads(g_bc, g_vec, g_ar, g_ai, g_dt, g_sink, loss):
    big = jnp.stack([t.reshape(BIG_ROWS, 128) for t in g_bc])
    row = lambda t: jnp.pad(t.reshape(1, -1), ((0, 0), (0, 128 - t.size)))
    tiny = jnp.concatenate([g_vec.reshape(64, 128), g_ar.reshape(32, 128), g_ai.reshape(32, 128), row(g_dt), row(g_sink),
                            row(loss), jnp.zeros((N_CHIPS * TINY_ROWS - 131, 128), F32)], axis=0)
    return jnp.concatenate([big, tiny.reshape(N_CHIPS, TINY_ROWS, 128)], axis=1)


def _unpack_small_grads(packed):
    big = packed[:, :BIG_ROWS].reshape(N_CHIPS, 2 * BIG_ROWS, SSM_STATE)
    tiny = packed[:, BIG_ROWS:].reshape(N_CHIPS * TINY_ROWS, 128)
    g_vec = tiny[0:64].reshape(8, 1024)
    return tiny[130, 0], {
        "ssm_b_re": big[0], "ssm_b_im": big[1], "ssm_c_re": big[2], "ssm_c_im": big[3],
        "ln_g": g_vec[0:1], "ln_b": g_vec[1:2],
        "norm_attn_g": _from_pair_order(g_vec[2:3, :D_ATTN]), "norm_ssm_g": g_vec[2:3, D_ATTN:],
        "ssm_d": g_vec[3:4, :D_SSM], "b_glu": g_vec[3:4, D_SSM:],
        "ssm_a_re": tiny[64:96].reshape(N_DG, SSM_STATE), "ssm_a_im": tiny[96:128].reshape(N_DG, SSM_STATE),
        "ssm_log_dt": tiny[128:129, :N_DG].reshape(N_DIR, N_GROUPS), "attn_sink": tiny[129:130, :N_Q_HEADS],
    }


def _small_unview(name, t, shape):
    if name in ("ssm_b_re", "ssm_b_im"):
        return jnp.swapaxes(t.reshape(N_DIR, N_GROUPS, SSM_CH, SSM_STATE), 2, 3).reshape(shape)
    return t.reshape(shape)


def _channel_major(name, t):
    return jnp.swapaxes(t, 3, 4) if name in ("ssm_b_re", "ssm_b_im") else t


def kernel(x, w_in, attn_sink, ssm_a_re, ssm_a_im, ssm_log_dt, ssm_b_re, ssm_b_im, ssm_c_re, ssm_c_im, ssm_d, w_glu, b_glu, norm_attn_g, norm_ssm_g, w_out, ln_g, ln_b, loss_target, m_w_in, m_attn_sink, m_ssm_a_re, m_ssm_a_im, m_ssm_log_dt, m_ssm_b_re, m_ssm_b_im, m_ssm_c_re, m_ssm_c_im, m_ssm_d, m_w_glu, m_b_glu, m_norm_attn_g, m_norm_ssm_g, m_w_out, m_ln_g, m_ln_b, v_w_in, v_attn_sink, v_ssm_a_re, v_ssm_a_im, v_ssm_log_dt, v_ssm_b_re, v_ssm_b_im, v_ssm_c_re, v_ssm_c_im, v_ssm_d, v_w_glu, v_b_glu, v_norm_attn_g, v_norm_ssm_g, v_w_out, v_ln_g, v_ln_b):
    args = dict(locals())
    weights = {n: args[n] for n in _WEIGHTS}
    mom_m = {n: args["m_" + n] for n in _WEIGHTS}
    mom_v = {n: args["v_" + n] for n in _WEIGHTS}
    xs = x[0]
    target = loss_target[0]

    (wt_g,) = _all_gather_chips([w_in[0].T], BF16, "gather_weights")
    wt_full = wt_g.reshape(D_IN_PROJ, D_MODEL)

    g_x, r_wt, r_w_out, r_w_glu, g_small_all = _local_step(
        xs, target, wt_full, w_glu[0], w_out[0], attn_sink, ssm_a_re, ssm_a_im, ssm_log_dt, ssm_b_re, ssm_b_im,
        ssm_c_re, ssm_c_im, ssm_d, b_glu, norm_attn_g, norm_ssm_g, ln_g, ln_b, sharded=True)
    loss, small_grads = _unpack_small_grads(g_small_all)

    grads, deltas, new_m, new_v = {}, {}, {}, {}
    d_w, m_w, v_w = _adamw(w_in[0].T, r_wt, m_w_in[0].T, v_w_in[0].T, "adamw_w_in")
    grads["w_in"], deltas["w_in"], new_m["w_in"], new_v["w_in"] = r_wt.T[None], d_w.T[None], m_w.T[None], v_w.T[None]
    for n, g in (("w_out", r_w_out), ("w_glu", r_w_glu)):
        d_w, m_w, v_w = _adamw(weights[n][0], g, mom_m[n][0], mom_v[n][0], "adamw_" + n)
        grads[n], deltas[n], new_m[n], new_v[n] = g[None], d_w[None], m_w[None], v_w[None]
    names = sorted(small_grads)
    updates = _adamw_many([(_channel_major(n, weights[n]), small_grads[n], _channel_major(n, mom_m[n]),
                            _channel_major(n, mom_v[n])) for n in names], "adamw_small")
    for i, n in enumerate(names):
        grads[n], deltas[n], new_m[n], new_v[n] = (_channel_major(n, t) for t in updates[4 * i:4 * i + 4])

    return (loss, g_x[None], *[grads[n] for n in _WEIGHTS], *[deltas[n] for n in _WEIGHTS],
            *[new_m[n] for n in _WEIGHTS], *[new_v[n] for n in _WEIGHTS])


def _local_step(xs, target, wt_full, w_glu_in, w_out_in, attn_sink, ssm_a_re, ssm_a_im, ssm_log_dt, ssm_b_re,
                ssm_b_im, ssm_c_re, ssm_c_im, ssm_d, b_glu, norm_attn_g, norm_ssm_g, ln_g, ln_b, sharded):
    seq = xs.shape[0]

    a_r, a_i = ssm_a_re, ssm_a_im
    log_dt = ssm_log_dt.reshape(N_DG, 1)
    b_r, b_i = _channel_major("ssm_b_re", ssm_b_re), _channel_major("ssm_b_im", ssm_b_im)
    c_r, c_i = ssm_c_re, ssm_c_im
    ssm_tb = min(SSM_BLOCK, seq)
    sub_len = ssm_tb // SUBSEG
    lam, bb, bbt, cb, cb_t = _ssm_params_fwd(a_r, a_i, log_dt, b_r, b_i, c_r, c_i, int(math.log2(sub_len)))
    lam = lam.reshape(4, N_DIR, 1, STATE_W)

    rope_hi, rope_lo = _rope_tables(seq)
    projected = _proj(xs, wt_full, rope_hi, rope_lo, [w_glu_in, w_out_in] if sharded else [], min(512, seq))
    q_stack, k_rot, v_bf, z_attn, u, z_ssm = projected[:6]
    if sharded:
        w_glu_full, w_out_full = projected[6].reshape(D_SSM, D_SSM), projected[7].reshape(D_MODEL, D_MODEL)
    else:
        w_glu_full, w_out_full = w_glu_in, w_out_in
    sink128 = jnp.broadcast_to(attn_sink[0][:, None, None], (N_Q_HEADS, 1, 128))
    attn_bias = _attn_bias()
    o = _attn_fwd(q_stack, k_rot, v_bf, sink128, attn_bias)
    ys, starts = [], []
    for d in range(N_DIR):
        y_d, s_r, s_i = _ssm_fwd(u, lam, bb, cb, direction=d, tb=ssm_tb, name=f"ssm_fwd_{d}")
        ys.append(y_d)
        starts.append((s_r, s_i))

    row = lambda t: t.reshape(1, -1)
    g_attn_p = _to_pair_order(norm_attn_g)
    loss_blk, d_o, d_za, d_ylin, d_zs, d_pre, g_w_out, g_w_glu, g_vec = _mid(
        o, z_attn, u, ys[0], ys[1], z_ssm, xs, target, row(ssm_d), w_glu_full, row(b_glu),
        g_attn_p, row(norm_ssm_g), w_out_full, row(ln_g), row(ln_b), min(256, seq))

    pieces = [g_w_glu.reshape(N_CHIPS, -1, D_SSM), g_w_out.reshape(N_CHIPS, -1, D_MODEL)] if sharded else []
    attn_grads = _attn_bwd(q_stack, k_rot, v_bf, sink128, attn_bias, d_o, pieces)
    dq, dk, dv, g_sink = attn_grads[:4]
    if sharded:
        g_w_glu, g_w_out = attn_grads[4:]
    dus, g_bb, g_cb, g_lam = [], [], [], []
    for d in range(N_DIR):
        du_d, gb_d, gc_d, dl_d = _ssm_bwd(u, d_ylin, starts[d], lam, bb, bbt, cb_t, direction=d, tb=ssm_tb,
                                          name=f"ssm_bwd_{d}")
        dus.append(du_d)
        g_bb.append(gb_d)
        g_cb.append(gc_d)
        g_lam.append(dl_d)
    g_ar, g_ai, g_dt, g_br, g_bi, g_cr, g_ci = _ssm_params_bwd(a_r, a_i, log_dt, b_r, b_i, g_bb, g_cb, g_lam)

    g_small = _pack_small_grads([g_br, g_bi, g_cr, g_ci], g_vec, g_ar, g_ai, g_dt, g_sink[:, 0], loss_blk[0, 0])
    dproj_args = (dq, dk, dv, d_za, dus[0], dus[1], d_ylin, d_zs, row(ssm_d))
    w_grads = _proj_bwd_w(xs, dproj_args, rope_hi, rope_lo, [g_small] if sharded else [], min(512, seq))
    g_wt = w_grads[0]
    if sharded:
        g_small = w_grads[1]
    x_grads = _proj_bwd_x(dproj_args, rope_hi, rope_lo, d_pre, wt_full,
                          [g_wt.reshape(N_CHIPS, -1, D_MODEL)] if sharded else [], min(512, seq))
    g_x = x_grads[0]
    if sharded:
        g_wt = x_grads[1]
    return g_x, g_wt, g_w_out, g_w_glu, g_small
```

```python
import functools
import math

import numpy as np
import jax
import jax.numpy as jnp
from jax import lax
from jax.experimental import pallas as pl
from jax.experimental.pallas import tpu as pltpu

F32 = jnp.float32
BF16 = jnp.bfloat16
MESH = pl.DeviceIdType.MESH

D_MODEL = 1024
D_ATTN = 512
D_SSM = 512
HEAD_DIM = 64
N_Q_HEADS = 8
WINDOW = 128
ROPE_THETA = 10000.0
SSM_CH = 16
N_GROUPS = 32
SSM_STATE = 64
N_DIR = 2
STATE_W = N_GROUPS * SSM_STATE
N_SLAB = 4
SLAB_IN = 128
SLAB_ST = 512
NORM_EPS = 1e-5
NEG_INF = -1e30
ALPHA = 2.0 ** 0.25
D_IN_PROJ = 2304
N_CHIPS = 4

ADAM_LR = 0.001
ADAM_B1 = 0.9
ADAM_B2 = 0.999
ADAM_EPS = 1e-08
ADAM_WD = 0.01
ADAM_STEP = 10

SUBSEG = 8
SCAN_LANES = 512
SSM_BLOCK = 512
VMEM_LIMIT = 48 * 1024 * 1024
ADAMW_BLOCK_BYTES = 3 * 512 * 1024
PROJ_BWD_X_VMEM = 56 * 1024 * 1024
MID_VMEM = 60 * 1024 * 1024
MID_BLOCK = 512

def _to_pair_order(row):
    return jnp.transpose(row.reshape(2, 4, HEAD_DIM), (1, 0, 2)).reshape(1, D_ATTN)


def _from_pair_order(row):
    return jnp.transpose(row.reshape(4, 2, HEAD_DIM), (1, 0, 2)).reshape(1, D_ATTN)


def _cparams(sem=None):
    return pltpu.CompilerParams(dimension_semantics=sem, vmem_limit_bytes=VMEM_LIMIT)


def _dot(a, b):
    return jnp.dot(a, b, preferred_element_type=F32)


def _dot_nt(a, b):
    return lax.dot_general(a, b, (((1,), (1,)), ((), ())), preferred_element_type=F32)


def _dot_tn(a, b):
    return lax.dot_general(a, b, (((0,), (0,)), ((), ())), preferred_element_type=F32)


def _sigmoid(z):
    return 0.5 * jnp.tanh(0.5 * z) + 0.5


def _all_gather_chips(shards, out_dtype, name):
    n = len(shards)

    def body(*refs):
        start, relay, finish = _gather_phases(refs[:n], refs[n:2 * n], *refs[2 * n:], out_dtype)
        start()
        relay()
        finish()

    vmem = pl.BlockSpec(memory_space=pltpu.VMEM)
    return pl.pallas_call(
        body, name=name,
        out_shape=[jax.ShapeDtypeStruct((N_CHIPS,) + s.shape, out_dtype) for s in shards],
        in_specs=[vmem] * n, out_specs=[vmem] * n,
        scratch_shapes=_gather_sems(n),
        compiler_params=pltpu.CompilerParams(vmem_limit_bytes=VMEM_LIMIT),
    )(*shards)


def _gather_sems(n):
    return [pltpu.SemaphoreType.DMA((6 * n,)), pltpu.SemaphoreType.DMA((6 * n,))]


def _gather_phases(in_refs, out_refs, send_sems, recv_sems, out_dtype):
    n = len(in_refs)
    x, y, c = lax.axis_index("x"), lax.axis_index("y"), lax.axis_index("c")
    sibling = (x, y, 1 - c)
    chips = [(1 - x, y), (x, 1 - y), (1 - x, 1 - y)]

    def half_of(a, px, py, half):
        rows = in_refs[a].shape[0] // 2
        return out_refs[a].at[2 * px + py, pl.ds(half * rows, rows), :]

    def copy(a, k, px, py, half, to):
        blk = half_of(a, px, py, half)
        return pltpu.make_async_remote_copy(src_ref=blk, dst_ref=blk, send_sem=send_sems.at[6 * a + k],
                                            recv_sem=recv_sems.at[6 * a + k], device_id=to, device_id_type=MESH)

    first = [copy(a, j, x, y, c, (*chips[j], c)) for a in range(n) for j in range(3)]
    passed = [copy(a, 3 + j, *chips[j], c, sibling) for a in range(n) for j in range(3)]

    def start():
        for a in range(n):
            out_refs[a][2 * x + y] = in_refs[a][...].astype(out_dtype)
        for cp in first:
            cp.start()

    def relay():
        for a in range(n):
            for j in range(3):
                copy(a, j, *chips[j], c, (x, y, c)).wait_recv()
                passed[3 * a + j].start()

    def finish():
        for a in range(n):
            for j in range(3):
                copy(a, 3 + j, *chips[j], 1 - c, (x, y, c)).wait_recv()
        for cp in first + passed:
            cp.wait_send()

    return start, relay, finish


SEMS_PER_ARRAY = 14


def _reduce_scratch(shapes, narrow):
    half = [(N_CHIPS, s[1] // 2, s[2]) for s in shapes]
    wire = [BF16 if nar else F32 for nar in narrow]
    n = len(shapes)
    return ([pltpu.VMEM(half[a], F32) for a in range(n)] + [pltpu.VMEM(half[a], wire[a]) for a in range(n)]
            + [pltpu.VMEM(half[a], wire[a]) for a in range(n)]
            + [pltpu.SemaphoreType.DMA((SEMS_PER_ARRAY * n,)), pltpu.SemaphoreType.DMA((SEMS_PER_ARRAY * n,))])


def _reduce_phases(p_refs, out_refs, a_refs, s_refs, b_refs, send_sems, recv_sems, narrow, gather_last):
    n = len(p_refs)
    halves = [p.shape[1] // 2 for p in p_refs]
    wire = [BF16 if nar else F32 for nar in narrow]
    x, y, c = lax.axis_index("x"), lax.axis_index("y"), lax.axis_index("c")
    me = 2 * x + y
    sibling = (x, y, 1 - c)
    chips = [(1 - x, y), (x, 1 - y), (1 - x, 1 - y)]
    slot = [2 * px + py for px, py in chips]
    last = n - 1

    def copy(a, k, src, dst, to):
        return pltpu.make_async_remote_copy(src_ref=src, dst_ref=dst, send_sem=send_sems.at[SEMS_PER_ARRAY * a + k],
                                            recv_sem=recv_sems.at[SEMS_PER_ARRAY * a + k],
                                            device_id=to, device_id_type=MESH)

    def rows(a, half):
        return pl.ds(pl.multiple_of(half * halves[a], 16), halves[a])

    def finished(a, k, half):
        if gather_last and a == last:
            return out_refs[a].at[k, rows(a, half), :]
        return out_refs[a].at[rows(a, half), :]

    order = slot + [me]
    swaps = [[copy(a, q, p_refs[a].at[order[q], rows(a, 1 - c), :], a_refs[a].at[order[q]], sibling)
              for q in range(N_CHIPS)] for a in range(n)]
    sends = [[copy(a, 4 + j, s_refs[a].at[slot[j]], b_refs[a].at[me], (*chips[j], c)) for j in range(3)] for a in range(n)]
    backs = [copy(a, 7, finished(a, me, c), finished(a, me, c), sibling) for a in range(n)]
    spread = [copy(last, 8 + j, finished(last, me, c), finished(last, me, c), (*chips[j], c)) for j in range(3)]
    relays = [copy(last, 11 + j, finished(last, slot[j], c), finished(last, slot[j], c), sibling) for j in range(3)]

    def start():
        for group in swaps:
            for cp in group:
                cp.start()

    def exchange():
        for a in range(n):
            for q in range(N_CHIPS):
                swaps[a][q].wait_recv()
                acc = a_refs[a][order[q]] + p_refs[a][order[q], rows(a, c), :]
                a_refs[a][order[q]] = acc
                s_refs[a][order[q]] = acc.astype(wire[a])
                if q < 3:
                    sends[a][q].start()
            b_refs[a][me] = s_refs[a][me]

    def combine():
        for a in range(n):
            for j in range(3):
                copy(a, 4 + j, s_refs[a].at[slot[j]], b_refs[a].at[slot[j]], (x, y, c)).wait_recv()
            terms = [jnp.where(me == k, a_refs[a][k], b_refs[a][k].astype(F32)) for k in range(N_CHIPS)]
            total = (terms[0] + terms[1]) + (terms[2] + terms[3])
            if gather_last and a == last:
                out_refs[a][me, rows(a, c), :] = total
            else:
                out_refs[a][rows(a, c), :] = total
            backs[a].start()
        if gather_last:
            for cp in spread:
                cp.start()

    def finish():
        if gather_last:
            for j in range(3):
                copy(last, 8 + j, finished(last, slot[j], c), finished(last, slot[j], c), (x, y, c)).wait_recv()
                relays[j].start()
        for a in range(n):
            copy(a, 7, finished(a, me, 1 - c), finished(a, me, 1 - c), (x, y, c)).wait_recv()
        if gather_last:
            for j in range(3):
                copy(last, 11 + j, finished(last, slot[j], 1 - c), finished(last, slot[j], 1 - c), (x, y, c)).wait_recv()
        started = [cp for group in swaps + sends for cp in group] + backs + (spread + relays if gather_last else [])
        for cp in started:
            cp.wait_send()

    return start, exchange, combine, finish


def _ssm_param_values(ar, ai, logdt):
    dt = jnp.exp(logdt)
    mag = jnp.exp(dt * ar)
    cs, sn = jnp.cos(dt * ai), jnp.sin(dt * ai)
    lr, li = mag * cs, mag * sn
    den = ar * ar + ai * ai
    nr = (lr - 1.0) * ar + li * ai
    ni = li * ar - (lr - 1.0) * ai
    return dt, mag, lr, li, den, nr, ni


GROUPS_PER_SLAB = N_GROUPS // N_SLAB


def _slab_masks():
    def eq(shape, f_row, f_col):
        return (f_row(lax.broadcasted_iota(jnp.int32, shape, 0)) == f_col(lax.broadcasted_iota(jnp.int32, shape, 1))).astype(F32)
    spread = eq((SSM_STATE, SLAB_ST), lambda r: r, lambda c: c % SSM_STATE)
    spread_t = eq((SLAB_ST, SSM_STATE), lambda r: r % SSM_STATE, lambda c: c)
    keep = eq((SLAB_IN, SLAB_ST), lambda r: r // SSM_CH, lambda c: c // SSM_STATE)
    keep_t = eq((SLAB_ST, SLAB_IN), lambda r: r // SSM_STATE, lambda c: c // SSM_CH)
    repeat = eq((N_DG * SSM_CH, N_DG), lambda r: r // SSM_CH, lambda c: c)
    return spread, spread_t, keep, keep_t, repeat


def _rows(ref):
    return ref[...].reshape(-1, SSM_STATE)


def _split3(t):
    hi = t.astype(BF16)
    rest = t - hi.astype(F32)
    mid = rest.astype(BF16)
    return hi, mid, (rest - mid.astype(F32)).astype(BF16)


def _select(dot, ones01, t, ones_first):
    o = ones01.astype(BF16)
    parts = [dot(o, p) if ones_first else dot(p, o) for p in _split3(t)]
    return (parts[0] + parts[1]) + parts[2]


def _ssm_params_fwd(ar, ai, logdt, br, bi, cr, ci, n_square):
    def body(ar_ref, ai_ref, dt_ref, br_ref, bi_ref, cr_ref, ci_ref, lam_ref, bb_ref, bbt_ref, cb_ref, cbt_ref):
        _, _, lr, li, den, nr, ni = _ssm_param_values(_rows(ar_ref), _rows(ai_ref), dt_ref[...])
        lam_ref[0] = lr
        lam_ref[1] = li
        pr, pi = lr, li
        for _ in range(n_square):
            pr, pi = pr * pr - pi * pi, 2.0 * pr * pi
        lam_ref[2] = pr
        lam_ref[3] = pi
        spread, spread_t, keep, keep_t, repeat = _slab_masks()
        fr = _select(_dot, repeat, nr / den, True)
        fi = _select(_dot, repeat, ni / den, True)
        b_r, b_i = _rows(br_ref), _rows(bi_ref)
        bbar = (fr * b_r - fi * b_i, fr * b_i + fi * b_r)
        c_par = (_rows(cr_ref), _rows(ci_ref))
        spread, spread_t = spread.astype(BF16), spread_t.astype(BF16)
        for src, wide_ref, tall_ref in ((bbar, bb_ref, bbt_ref), (c_par, cbt_ref, cb_ref)):
            for q in range(2):
                for d in range(N_DIR):
                    for k in range(N_SLAB):
                        r0 = (d * N_GROUPS + k * GROUPS_PER_SLAB) * SSM_CH
                        blk = src[q][r0:r0 + SLAB_IN].astype(BF16)
                        wide_ref[q, d, k] = (_dot(blk, spread) * keep).astype(BF16)
                        tall_ref[q, d, k] = (_dot_nt(spread_t, blk) * keep_t).astype(BF16)

    wide = jax.ShapeDtypeStruct((2, N_DIR, N_SLAB, SLAB_IN, SLAB_ST), BF16)
    tall = jax.ShapeDtypeStruct((2, N_DIR, N_SLAB, SLAB_ST, SLAB_IN), BF16)
    return pl.pallas_call(body, name="ssm_params_fwd",
                          out_shape=[jax.ShapeDtypeStruct((4, N_DG, SSM_STATE), F32), wide, tall, tall, wide],
                          compiler_params=pltpu.CompilerParams(vmem_limit_bytes=VMEM_LIMIT),
                          )(ar, ai, logdt, br, bi, cr, ci)


def _ssm_params_bwd(ar, ai, logdt, br, bi, g_slabs_b, g_slabs_c, g_lam):
    def body(ar_ref, ai_ref, dt_ref, br_ref, bi_ref, gb0_ref, gb1_ref, gc0_ref, gc1_ref, gl0_ref, gl1_ref,
             gar_ref, gai_ref, gdt_ref, gbr_ref, gbi_ref, gcr_ref, gci_ref, dbb, dlam):
        spread, spread_t, keep, _, repeat = _slab_masks()
        for d, (gb_ref, gc_ref) in enumerate(((gb0_ref, gc0_ref), (gb1_ref, gc1_ref))):
            for q in range(2):
                for k in range(N_SLAB):
                    r0 = (d * N_GROUPS + k * GROUPS_PER_SLAB) * SSM_CH
                    dbb[q, r0:r0 + SLAB_IN, :] = _select(_dot, spread_t, gb_ref[q, k] * keep, False)
                    out_ref = gcr_ref if q == 0 else gci_ref
                    out_ref[r0:r0 + SLAB_IN, :] = _select(_dot, spread_t, gc_ref[q, k] * keep, False)
        grp = (lax.broadcasted_iota(jnp.int32, (N_GROUPS, STATE_W), 0)
               == lax.broadcasted_iota(jnp.int32, (N_GROUPS, STATE_W), 1) // SSM_STATE).astype(F32)
        pick = (lax.broadcasted_iota(jnp.int32, (STATE_W, SSM_STATE), 0) % SSM_STATE
                == lax.broadcasted_iota(jnp.int32, (STATE_W, SSM_STATE), 1)).astype(F32)
        for d, gl_ref in enumerate((gl0_ref, gl1_ref)):
            for q in range(2):
                row = jnp.sum(gl_ref[q], axis=0, keepdims=True)
                dlam[q, d * N_GROUPS:(d + 1) * N_GROUPS, :] = _select(_dot, pick, grp * row, False)

        a_r, a_i = _rows(ar_ref), _rows(ai_ref)
        dt, mag, lr, li, den, nr, ni = _ssm_param_values(a_r, a_i, dt_ref[...])
        fr = _select(_dot, repeat, nr / den, True)
        fi = _select(_dot, repeat, ni / den, True)
        b_r, b_i = _rows(br_ref), _rows(bi_ref)
        g_r, g_i = dbb[0], dbb[1]
        gbr_ref[...] = fr * g_r + fi * g_i
        gbi_ref[...] = fr * g_i - fi * g_r
        d_fr = _select(_dot_tn, repeat, b_r * g_r + b_i * g_i, True)
        d_fi = _select(_dot_tn, repeat, b_r * g_i - b_i * g_r, True)
        d_nr, d_ni = d_fr / den, d_fi / den
        d_den = -(d_fr * nr + d_fi * ni) / (den * den)
        d_lr = dlam[0] + d_nr * a_r - d_ni * a_i
        d_li = dlam[1] + d_nr * a_i + d_ni * a_r
        d_ar = d_nr * (lr - 1.0) + d_ni * li + d_den * 2.0 * a_r
        d_ai = d_nr * li - d_ni * (lr - 1.0) + d_den * 2.0 * a_i
        d_mag = (d_lr * lr + d_li * li) / mag
        d_theta = d_li * lr - d_lr * li
        gar_ref[...] = d_ar + d_mag * mag * dt
        gai_ref[...] = d_ai + d_theta * dt
        d_dt = d_mag * mag * a_r + d_theta * a_i
        gdt_ref[...] = jnp.sum(d_dt, axis=1, keepdims=True) * dt

    small = jax.ShapeDtypeStruct((N_DG, SSM_STATE), F32)
    big = jax.ShapeDtypeStruct((N_DG * SSM_CH, SSM_STATE), F32)
    return pl.pallas_call(
        body, name="ssm_params_bwd",
        out_shape=[small, small, jax.ShapeDtypeStruct(logdt.shape, F32), big, big, big, big],
        scratch_shapes=[pltpu.VMEM((2,) + big.shape, F32), pltpu.VMEM((2,) + small.shape, F32)],
        compiler_params=pltpu.CompilerParams(vmem_limit_bytes=VMEM_LIMIT),
    )(ar, ai, logdt, br, bi, *g_slabs_b, *g_slabs_c, *g_lam)


ROPE_GROUP = 128


def _rope_tables(seq):
    half = HEAD_DIM // 2
    inv_freq = jnp.tile(ROPE_THETA ** (-jnp.arange(half, dtype=F32) / half), 4)
    sign = jnp.tile(jnp.concatenate([-jnp.ones((half,), F32), jnp.ones((half,), F32)]), 2)

    def table(pos):
        ang = pos.astype(F32)[:, None] * inv_freq[None, :]
        return jnp.stack([jnp.cos(ang), jnp.sin(ang), sign * jnp.sin(ang)])

    return table(jnp.arange(seq // ROPE_GROUP) * ROPE_GROUP), table(jnp.arange(ROPE_GROUP))


def _rope_block(hi_ref, lo_ref, first_group, n_groups):
    cl, sl, sl_s = lo_ref[0], lo_ref[1], lo_ref[2]
    cos, sin = [], []
    for g in range(n_groups):
        ch, sh, sh_s = (hi_ref[q, pl.ds(first_group + g, 1), :] for q in range(3))
        cos.append(ch * cl - sh * sl)
        sin.append(sh_s * cl + ch * sl_s)
    return jnp.concatenate(cos, axis=0), jnp.concatenate(sin, axis=0)


def _rotate_half_unsigned(t):
    lane = lax.broadcasted_iota(jnp.int32, t.shape, 1)
    return jnp.where((lane % HEAD_DIM) < HEAD_DIM // 2, pltpu.roll(t, 96, 1), pltpu.roll(t, 32, 1))


def _rope(t, cos, sin_signed):
    return t * cos + _rotate_half_unsigned(t) * sin_signed


def _pair_blocks(base):
    out = []
    for j in range(4):
        for g in range(2):
            nat = base + HEAD_DIM * (4 * g + j)
            par = base + 128 * j + HEAD_DIM * g
            out.append((slice(nat, nat + HEAD_DIM), slice(par, par + HEAD_DIM)))
    return out


W_Q, W_KV, W_ZA, W_U, W_ZS = 0, 512, 768, 1280, 1792


def _proj(x, wt, rope_hi, rope_lo, shards, tb):
    seq = x.shape[0]
    steps = seq // tb
    n_sh = len(shards)

    def body(*refs):
        x_ref, wt_ref, hi_ref, lo_ref = refs[:4]
        shard_refs = refs[4:4 + n_sh]
        q_ref, k_ref, v_ref, za_ref, u_ref, zs_ref = refs[4 + n_sh:10 + n_sh]
        gathered_refs = refs[10 + n_sh:10 + 2 * n_sh]
        wp = refs[10 + 2 * n_sh]
        step = pl.program_id(0)
        if n_sh:
            landing_refs = refs[11 + 2 * n_sh:11 + 3 * n_sh]
            start, relay, finish = _gather_phases(shard_refs, landing_refs, *refs[11 + 3 * n_sh:], BF16)
            pl.when(step == 0)(start)
            pl.when(step == max(steps - 2, 0))(relay)

        @pl.when(step == 0)
        def _():
            for dst_base, src_base in ((0, W_Q), (512, W_ZA)):
                for nat, par in _pair_blocks(0):
                    wp[dst_base + par.start:dst_base + par.stop, :] = wt_ref[src_base + nat.start:src_base + nat.stop, :]

        xb = x_ref[...].astype(BF16)
        cos, sin = _rope_block(hi_ref, lo_ref, pl.program_id(0) * (tb // ROPE_GROUP), tb // ROPE_GROUP)
        lo = lax.broadcasted_iota(jnp.int32, (tb, 128), 1) < HEAD_DIM
        q = _dot_nt(xb, wp[0:512, :])
        for j in range(4):
            qj = _rope(q[:, 128 * j:128 * (j + 1)], cos, sin)
            q_ref[j] = jnp.where(lo, qj, 0.0).astype(BF16)
            q_ref[4 + j] = jnp.where(lo, 0.0, qj).astype(BF16)
        kv = _dot_nt(xb, wt_ref[W_KV:W_ZA, :])
        k_ref[...] = _rope(kv[:, 0:128], cos, sin).astype(BF16)
        v_ref[...] = kv[:, 128:256].astype(BF16)
        za_ref[...] = _dot_nt(xb, wp[512:1024, :])
        u_val = _dot_nt(xb, wt_ref[W_U:W_ZS, :])
        for k in range(N_SLAB):
            u_ref[k] = u_val[:, k * SLAB_IN:(k + 1) * SLAB_IN]
        zs_ref[...] = _dot_nt(xb, wt_ref[W_ZS:D_IN_PROJ, :])
        if n_sh:
            @pl.when(step == steps - 1)
            def _():
                finish()
                for a in range(n_sh):
                    gathered_refs[a][...] = landing_refs[a][...]

    row = lambda w: pl.BlockSpec((tb, w), lambda i: (i, 0))
    table = lambda t: pl.BlockSpec(t.shape, lambda i: (0, 0, 0))
    vmem = pl.BlockSpec(memory_space=pltpu.VMEM)
    return pl.pallas_call(
        body, name="proj", grid=(steps,),
        in_specs=[row(D_MODEL), pl.BlockSpec((D_IN_PROJ, D_MODEL), lambda i: (0, 0), pipeline_mode=pl.Buffered(1)),
                  table(rope_hi), table(rope_lo)] + [vmem] * n_sh,
        out_specs=[pl.BlockSpec((8, tb, 128), lambda i: (0, i, 0)), row(128), row(128), row(512),
                   pl.BlockSpec((N_SLAB, tb, SLAB_IN), lambda i: (0, i, 0)), row(512)] + [vmem] * n_sh,
        out_shape=[jax.ShapeDtypeStruct((8, seq, 128), BF16), jax.ShapeDtypeStruct((seq, 128), BF16),
                   jax.ShapeDtypeStruct((seq, 128), BF16), jax.ShapeDtypeStruct((seq, 512), F32),
                   jax.ShapeDtypeStruct((N_SLAB, seq, SLAB_IN), F32), jax.ShapeDtypeStruct((seq, 512), F32)]
        + [jax.ShapeDtypeStruct((N_CHIPS,) + s.shape, BF16) for s in shards],
        scratch_shapes=[pltpu.VMEM((1024, D_MODEL), BF16)] + [pltpu.VMEM((N_CHIPS,) + s.shape, BF16) for s in shards]
        + (_gather_sems(n_sh) if n_sh else []),
        compiler_params=_cparams(("arbitrary",)),
    )(x, wt, rope_hi, rope_lo, *shards)


ATT_TQ = 128
ATT_KEYS = ATT_TQ + 2 * WINDOW


def _attn_window(i, seq):
    start = jnp.clip(i * ATT_TQ - WINDOW, 0, seq - ATT_KEYS)
    return pl.multiple_of(start, WINDOW)


def _attn_bias():
    r = np.arange(ATT_TQ)[None, :, None]
    c = np.arange(ATT_KEYS)[None, None, :]
    off = np.array([0, WINDOW, ATT_KEYS - ATT_TQ])[:, None, None]
    return jnp.asarray(np.where(np.abs(r + off - c) <= WINDOW, 0.0, NEG_INF).astype(np.float32))


def _attn_bias_spec(nblk):
    pick = lambda i: jnp.where(i == 0, 0, jnp.where(i == nblk - 1, 2, 1))
    return pl.BlockSpec((None, ATT_TQ, ATT_KEYS), lambda i: (pick(i), 0, 0))


def _attn_softmax(q_ref, k_ref, v_ref, sink_ref, bias_ref, start):
    kw = k_ref[pl.ds(start, ATT_KEYS), :]
    vw = v_ref[pl.ds(start, ATT_KEYS), :]
    qall = q_ref[...].reshape(N_Q_HEADS * ATT_TQ, 128)
    s = (_dot_nt(qall, kw) * (HEAD_DIM ** -0.5)).reshape(N_Q_HEADS, ATT_TQ, ATT_KEYS) + bias_ref[...][None]
    tiles = [s[:, :, 128 * t:128 * (t + 1)] for t in range(ATT_KEYS // 128)]
    m = jnp.max(functools.reduce(jnp.maximum, tiles), axis=2, keepdims=True)
    sink = sink_ref[...]
    m_b = jnp.maximum(jnp.broadcast_to(m, (N_Q_HEADS, ATT_TQ, 128)), sink)
    p = jnp.concatenate([jnp.exp(t - m_b) for t in tiles], axis=2)
    p_sink = jnp.exp(sink - m_b)
    lo_k = lax.broadcasted_iota(jnp.int32, (ATT_KEYS, 128), 1) < HEAD_DIM
    v_f = vw.astype(F32)
    v_lo, v_hi = jnp.where(lo_k, v_f, 1.0).astype(BF16), jnp.where(lo_k, 1.0, v_f).astype(BF16)
    pb = p.astype(BF16).reshape(N_Q_HEADS * ATT_TQ, ATT_KEYS)
    half = 4 * ATT_TQ
    r = jnp.concatenate([_dot(pb[:half], v_lo), _dot(pb[half:], v_hi)], axis=0).reshape(N_Q_HEADS, ATT_TQ, 128)
    return kw, vw, qall, p, p_sink, r


def _attn_fwd(q_stack, k, v, sink128, bias):
    seq = k.shape[0]

    def body(q_ref, k_ref, v_ref, sink_ref, bias_ref, o_ref):
        start = _attn_window(pl.program_id(0), seq)
        _, _, _, _, p_sink, r = _attn_softmax(q_ref, k_ref, v_ref, sink_ref, bias_ref, start)
        out = r / (pltpu.roll(r, HEAD_DIM, 2) + p_sink)
        lo = lax.broadcasted_iota(jnp.int32, (ATT_TQ, 128), 1) < HEAD_DIM
        for j in range(4):
            o_ref[:, 128 * j:128 * (j + 1)] = jnp.where(lo, out[j], out[4 + j])

    full = lambda w: pl.BlockSpec((seq, w), lambda i: (0, 0))
    return pl.pallas_call(
        body, name="attn_fwd", grid=(seq // ATT_TQ,),
        in_specs=[pl.BlockSpec((8, ATT_TQ, 128), lambda i: (0, i, 0)), full(128), full(128),
                  pl.BlockSpec((N_Q_HEADS, 1, 128), lambda i: (0, 0, 0)), _attn_bias_spec(seq // ATT_TQ)],
        out_specs=pl.BlockSpec((ATT_TQ, 512), lambda i: (i, 0)),
        out_shape=jax.ShapeDtypeStruct((seq, 512), F32),
        compiler_params=_cparams(("arbitrary",)),
    )(q_stack, k, v, sink128, bias)


def _attn_bwd(q_stack, k, v, sink128, bias, d_o, pieces):
    seq = k.shape[0]
    steps = seq // ATT_TQ
    n_p = len(pieces)

    def body(*refs):
        q_ref, k_ref, v_ref, sink_ref, bias_ref, do_ref = refs[:6]
        piece_refs = refs[6:6 + n_p]
        dq_ref, dk_ref, dv_ref, dsink_ref = refs[6 + n_p:10 + n_p]
        reduced_refs = refs[10 + n_p:10 + 2 * n_p]
        sink_acc = refs[10 + 2 * n_p]
        i = pl.program_id(0)
        if n_p:
            landing_refs = refs[11 + 2 * n_p:11 + 3 * n_p]
            scratch = refs[11 + 3 * n_p:]
            begin, exchange, combine, finish = _reduce_phases(
                piece_refs, landing_refs, scratch[:n_p], scratch[n_p:2 * n_p], scratch[2 * n_p:3 * n_p],
                *scratch[3 * n_p:], [True] * n_p, gather_last=False)
            pl.when(i == 0)(begin)
            pl.when(i == min(4, steps - 1))(exchange)
            pl.when(i == (3 * steps) // 4)(combine)

        @pl.when(i == 0)
        def _():
            dk_ref[...] = jnp.zeros_like(dk_ref)
            dv_ref[...] = jnp.zeros_like(dv_ref)
            sink_acc[...] = jnp.zeros_like(sink_acc)

        start = _attn_window(i, seq)
        kw, vw, qall, p, p_sink, r = _attn_softmax(q_ref, k_ref, v_ref, sink_ref, bias_ref, start)
        lo = lax.broadcasted_iota(jnp.int32, (ATT_TQ, 128), 1) < HEAD_DIM
        lo3 = lo[None]
        grp0 = lax.broadcasted_iota(jnp.int32, (N_Q_HEADS, ATT_TQ, 128), 0) < 4
        val = grp0 == lo3
        swapped = pltpu.roll(r, HEAD_DIM, 2)
        inv = 1.0 / (jnp.where(val, swapped, r) + p_sink)
        d_o_blk = do_ref[...]
        do3 = jnp.where(val, jnp.concatenate([d_o_blk[None, :, 128 * j:128 * (j + 1)] for j in range(4)] * 2, axis=0), 0.0)
        t = (do3 * r).reshape(N_Q_HEADS * ATT_TQ, 128)
        t_hi = t.astype(BF16)
        t_lo = (t - t_hi.astype(F32)).astype(BF16)
        ones = jnp.ones((128, 128), BF16)
        delta = (_dot(t_hi, ones) + _dot(t_lo, ones)).reshape(N_Q_HEADS, ATT_TQ, 128) * inv
        sink_acc[...] += -(p_sink * inv) * delta
        do_all = do3.astype(BF16).reshape(N_Q_HEADS * ATT_TQ, 128)
        dp = _dot_nt(do_all, vw).reshape(N_Q_HEADS, ATT_TQ, ATT_KEYS)
        probs, ds = [], []
        for tl in range(ATT_KEYS // 128):
            cols = slice(128 * tl, 128 * (tl + 1))
            probs_t = p[:, :, cols] * inv
            probs.append(probs_t.astype(BF16))
            ds.append((probs_t * (dp[:, :, cols] - delta)).astype(BF16))
        probs_all = jnp.concatenate(probs, axis=2).reshape(N_Q_HEADS * ATT_TQ, ATT_KEYS)
        ds_all = jnp.concatenate(ds, axis=2).reshape(N_Q_HEADS * ATT_TQ, ATT_KEYS)
        scale = HEAD_DIM ** -0.5
        dq_all = (_dot(ds_all, kw) * scale).reshape(N_Q_HEADS, ATT_TQ, 128)
        for j in range(4):
            dq_ref[:, 128 * j:128 * (j + 1)] = jnp.where(lo, dq_all[j], dq_all[4 + j])
        dk_ref[pl.ds(start, ATT_KEYS), :] += _dot_tn(ds_all, qall) * scale
        dv_ref[pl.ds(start, ATT_KEYS), :] += _dot_tn(probs_all, do_all)

        @pl.when(i == steps - 1)
        def _():
            dsink_ref[...] = jnp.sum(sink_acc[...], axis=1)

        if n_p:
            @pl.when(i == steps - 1)
            def _():
                finish()
                for a in range(n_p):
                    reduced_refs[a][...] = landing_refs[a][...]

    full = lambda w: pl.BlockSpec((seq, w), lambda i: (0, 0))
    vmem = pl.BlockSpec(memory_space=pltpu.VMEM)
    return pl.pallas_call(
        body, name="attn_bwd", grid=(steps,),
        in_specs=[pl.BlockSpec((8, ATT_TQ, 128), lambda i: (0, i, 0)), full(128), full(128),
                  pl.BlockSpec((N_Q_HEADS, 1, 128), lambda i: (0, 0, 0)),
                  _attn_bias_spec(steps), pl.BlockSpec((ATT_TQ, 512), lambda i: (i, 0))] + [vmem] * n_p,
        out_specs=[pl.BlockSpec((ATT_TQ, 512), lambda i: (i, 0)), full(128), full(128),
                   pl.BlockSpec((N_Q_HEADS, 128), lambda i: (0, 0))] + [vmem] * n_p,
        out_shape=[jax.ShapeDtypeStruct((seq, 512), F32), jax.ShapeDtypeStruct((seq, 128), F32),
                   jax.ShapeDtypeStruct((seq, 128), F32), jax.ShapeDtypeStruct((N_Q_HEADS, 128), F32)]
        + [jax.ShapeDtypeStruct(p.shape[1:], F32) for p in pieces],
        scratch_shapes=[pltpu.VMEM((N_Q_HEADS, ATT_TQ, 128), F32)] + [pltpu.VMEM(p.shape[1:], F32) for p in pieces]
        + (_reduce_scratch([p.shape for p in pieces], [True] * n_p) if n_p else []),
        compiler_params=_cparams(("arbitrary",)),
    )(q_stack, k, v, sink128, bias, d_o, *pieces)


def _permute_rows(dst_ref, src_ref, sub_len):
    for k in range(N_SLAB):
        for j in range(sub_len):
            dst_ref[k, 8 * j:8 * (j + 1), :] = src_ref.at[k][pl.ds(j, SUBSEG, stride=sub_len), :]


def _unpermute_rows(dst_ref, src_ref, sub_len):
    for k in range(N_SLAB):
        for s in range(SUBSEG):
            dst_ref[k, s * sub_len:(s + 1) * sub_len, :] = src_ref.at[k][pl.ds(s, sub_len, stride=SUBSEG), :]


def _scan_chunk(br_ref, bi_ref, lr_row, li_row, init, cols, *, sub_len, reverse, store):
    lr = jnp.broadcast_to(lr_row[:, cols], (SUBSEG, SCAN_LANES))
    li = jnp.broadcast_to(li_row[:, cols], (SUBSEG, SCAN_LANES))
    if init is None:
        sr = si = jnp.zeros((SUBSEG, SCAN_LANES), F32)
    else:
        sr, si = init
    for jj in range(sub_len):
        rows = slice(SUBSEG * ((sub_len - 1 - jj) if reverse else jj), SUBSEG * (((sub_len - 1 - jj) if reverse else jj) + 1))
        sr, si = lr * sr - li * si + br_ref[rows, cols], lr * si + li * sr + bi_ref[rows, cols]
        if store:
            br_ref[rows, cols] = sr
            bi_ref[rows, cols] = si
    return sr, si


def _resolve_chunk(z, carry_refs, start_refs, pr_row, pi_row, cols, *, reverse):
    cr, ci = carry_refs[0][0:1, cols], carry_refs[1][0:1, cols]
    pr, pi = pr_row[:, cols], pi_row[:, cols]
    for s in (range(SUBSEG - 1, -1, -1) if reverse else range(SUBSEG)):
        start_refs[0][s:s + 1, cols] = cr
        start_refs[1][s:s + 1, cols] = ci
        cr, ci = pr * cr - pi * ci + z[0][s:s + 1, :], pr * ci + pi * cr + z[1][s:s + 1, :]
    carry_refs[0][0:1, cols] = cr
    carry_refs[1][0:1, cols] = ci


def _param_specs(direction):
    row = lambda q: pl.BlockSpec((None, None, 1, STATE_W), lambda i: (q, direction, 0, 0))
    wide = lambda q: pl.BlockSpec((None, None, N_SLAB, SLAB_IN, SLAB_ST), lambda i: (q, direction, 0, 0, 0))
    tall = lambda q: pl.BlockSpec((None, None, N_SLAB, SLAB_ST, SLAB_IN), lambda i: (q, direction, 0, 0, 0))
    return [row(q) for q in range(4)], [wide(0), wide(1)], [tall(0), tall(1)]


def _ssm_fwd(u, lam, bb, cb, *, direction, tb, name):
    reverse = direction == 1
    seq = u.shape[1]
    nblk = seq // tb
    sub_len = tb // SUBSEG

    def body(u_ref, lr_ref, li_ref, pr_ref, pi_ref, bbr_ref, bbi_ref, cbr_ref, cbi_ref,
             y_ref, sr_ref, si_ref, xr, xi, up, yp, car, cai):
        @pl.when(pl.program_id(0) == 0)
        def _():
            car[...] = jnp.zeros_like(car)
            cai[...] = jnp.zeros_like(cai)

        _permute_rows(up, u_ref, sub_len)
        lr, li, pr, pi = lr_ref[...], li_ref[...], pr_ref[...], pi_ref[...]
        chunk = lambda k: slice(k * SLAB_ST, (k + 1) * SLAB_ST)

        def drive(k):
            ub = up[k].astype(BF16)
            xr[:, chunk(k)] = _dot(ub, bbr_ref[k])
            xi[:, chunk(k)] = _dot(ub, bbi_ref[k])

        def scan(k):
            z = _scan_chunk(xr, xi, lr, li, None, chunk(k), sub_len=sub_len, reverse=reverse, store=False)
            _resolve_chunk(z, (car, cai), (sr_ref, si_ref), pr, pi, chunk(k), reverse=reverse)
            _scan_chunk(xr, xi, lr, li, (sr_ref[:, chunk(k)], si_ref[:, chunk(k)]), chunk(k),
                        sub_len=sub_len, reverse=reverse, store=True)

        def read_out(k):
            yp[k] = _dot(xr[:, chunk(k)].astype(BF16), cbr_ref[k]) - _dot(xi[:, chunk(k)].astype(BF16), cbi_ref[k])

        drive(0)
        for k in range(N_SLAB):
            if k + 1 < N_SLAB:
                drive(k + 1)
            scan(k)
            if k > 0:
                read_out(k - 1)
        read_out(N_SLAB - 1)
        _unpermute_rows(y_ref, yp, sub_len)

    blk = (lambda i: nblk - 1 - i) if reverse else (lambda i: i)
    rows, wide, tall = _param_specs(direction)
    tok = pl.BlockSpec((N_SLAB, tb, SLAB_IN), lambda i: (0, blk(i), 0))
    start_spec = pl.BlockSpec((None, SUBSEG, STATE_W), lambda i: (blk(i), 0, 0))
    return pl.pallas_call(
        body, name=name, grid=(nblk,),
        in_specs=[tok] + rows + wide + tall,
        out_specs=[tok, start_spec, start_spec],
        out_shape=[jax.ShapeDtypeStruct((N_SLAB, seq, SLAB_IN), F32), jax.ShapeDtypeStruct((nblk, SUBSEG, STATE_W), F32),
                   jax.ShapeDtypeStruct((nblk, SUBSEG, STATE_W), F32)],
        scratch_shapes=[pltpu.VMEM((tb, STATE_W), F32), pltpu.VMEM((tb, STATE_W), F32),
                        pltpu.VMEM((N_SLAB, tb, SLAB_IN), F32), pltpu.VMEM((N_SLAB, tb, SLAB_IN), F32),
                        pltpu.VMEM((SUBSEG, STATE_W), F32), pltpu.VMEM((SUBSEG, STATE_W), F32)],
        compiler_params=_cparams(("arbitrary",)),
    )(u, lam, lam, lam, lam, bb, bb, cb, cb)


def _ssm_bwd(u, dy, starts, lam, bb, bbt, cb_t, *, direction, tb, name):
    reverse = direction == 1
    seq = u.shape[1]
    nblk = seq // tb
    sub_len = tb // SUBSEG

    def body(u_ref, dy_ref, sr_ref, si_ref, lr_ref, li_ref, pr_ref, pi_ref, bbr_ref, bbi_ref, btr_ref, bti_ref,
             ctr_ref, cti_ref, du_ref, gb_ref, gc_ref, dl_ref,
             xr, xi, gr, gi, up, dyp, dup, gsr, gsi, car, cai):
        gbr_ref, gbi_ref = gb_ref.at[0], gb_ref.at[1]
        gcr_ref, gci_ref = gc_ref.at[0], gc_ref.at[1]
        dlr_ref, dli_ref = dl_ref.at[0], dl_ref.at[1]

        @pl.when(pl.program_id(0) == 0)
        def _():
            for ref in (car, cai, gbr_ref, gbi_ref, gcr_ref, gci_ref, dlr_ref, dli_ref):
                ref[...] = jnp.zeros_like(ref)

        _permute_rows(up, u_ref, sub_len)
        _permute_rows(dyp, dy_ref, sub_len)
        lr, li, pr, pi = lr_ref[...], li_ref[...], pr_ref[...], pi_ref[...]
        nli, npi = -li, -pi
        chunk = lambda k: slice(k * SLAB_ST, (k + 1) * SLAB_ST)

        def drive(k):
            ub = up[k].astype(BF16)
            xr[:, chunk(k)] = _dot(ub, bbr_ref[k])
            xi[:, chunk(k)] = _dot(ub, bbi_ref[k])
            dyb = dyp[k].astype(BF16)
            gr[:, chunk(k)] = _dot(dyb, ctr_ref[k])
            gi[:, chunk(k)] = -_dot(dyb, cti_ref[k])

        def scan_x(k):
            _scan_chunk(xr, xi, lr, li, (sr_ref[:, chunk(k)], si_ref[:, chunk(k)]), chunk(k),
                        sub_len=sub_len, reverse=reverse, store=True)

        def grad_c(k):
            dyb = dyp[k].astype(BF16)
            gcr_ref[k] += _dot_tn(dyb, xr[:, chunk(k)].astype(BF16))
            gci_ref[k] -= _dot_tn(dyb, xi[:, chunk(k)].astype(BF16))

        def scan_g(k):
            z = _scan_chunk(gr, gi, lr, nli, None, chunk(k), sub_len=sub_len, reverse=not reverse, store=False)
            _resolve_chunk(z, (car, cai), (gsr, gsi), pr, npi, chunk(k), reverse=not reverse)
            _scan_chunk(gr, gi, lr, nli, (gsr[:, chunk(k)], gsi[:, chunk(k)]), chunk(k),
                        sub_len=sub_len, reverse=not reverse, store=True)

        def grad_b_du(k):
            ub = up[k].astype(BF16)
            grb, gib = gr[:, chunk(k)].astype(BF16), gi[:, chunk(k)].astype(BF16)
            gbr_ref[k] += _dot_tn(ub, grb)
            gbi_ref[k] += _dot_tn(ub, gib)
            dup[k] = _dot(grb, btr_ref[k]) + _dot(gib, bti_ref[k])

        def grad_lambda(k):
            cols = chunk(k)
            acc_r, acc_i = dlr_ref[:, cols], dli_ref[:, cols]
            for jj in range(sub_len):
                prev = jj + 1 if reverse else jj - 1
                if 0 <= prev < sub_len:
                    x_r, x_i = xr[SUBSEG * prev:SUBSEG * (prev + 1), cols], xi[SUBSEG * prev:SUBSEG * (prev + 1), cols]
                else:
                    x_r, x_i = sr_ref[:, cols], si_ref[:, cols]
                g_r, g_i = gr[SUBSEG * jj:SUBSEG * (jj + 1), cols], gi[SUBSEG * jj:SUBSEG * (jj + 1), cols]
                acc_r = acc_r + (g_r * x_r + g_i * x_i)
                acc_i = acc_i + (g_i * x_r - g_r * x_i)
            dlr_ref[:, cols] = acc_r
            dli_ref[:, cols] = acc_i

        drive(0)
        for k in range(N_SLAB):
            if k + 1 < N_SLAB:
                drive(k + 1)
            scan_x(k)
            grad_c(k)
            scan_g(k)
            grad_b_du(k)
            grad_lambda(k)
        _unpermute_rows(du_ref, dup, sub_len)

    blk = (lambda i: i) if reverse else (lambda i: nblk - 1 - i)
    rows, wide, tall = _param_specs(direction)
    tok = pl.BlockSpec((N_SLAB, tb, SLAB_IN), lambda i: (0, blk(i), 0))
    start_spec = pl.BlockSpec((None, SUBSEG, STATE_W), lambda i: (blk(i), 0, 0))
    gb_shape, dl_shape = (2, N_SLAB, SLAB_IN, SLAB_ST), (2, SUBSEG, STATE_W)
    whole = lambda shape: pl.BlockSpec(shape, lambda i: (0,) * len(shape))
    big = lambda: pltpu.VMEM((tb, STATE_W), F32)
    slabs = lambda: pltpu.VMEM((N_SLAB, tb, SLAB_IN), F32)
    tile = lambda: pltpu.VMEM((SUBSEG, STATE_W), F32)
    return pl.pallas_call(
        body, name=name, grid=(nblk,),
        in_specs=[tok, tok, start_spec, start_spec] + rows + wide + tall + wide,
        out_specs=[tok, whole(gb_shape), whole(gb_shape), whole(dl_shape)],
        out_shape=[jax.ShapeDtypeStruct((N_SLAB, seq, SLAB_IN), F32), jax.ShapeDtypeStruct(gb_shape, F32),
                   jax.ShapeDtypeStruct(gb_shape, F32), jax.ShapeDtypeStruct(dl_shape, F32)],
        scratch_shapes=[big(), big(), big(), big(), slabs(), slabs(), slabs(), tile(), tile(), tile(), tile()],
        compiler_params=_cparams(("arbitrary",)),
    )(u, dy, *starts, lam, lam, lam, lam, bb, bb, bbt, bbt, cb_t, cb_t)


GELU_C = math.sqrt(2.0 / math.pi)
GELU_K = 0.044715


def _mid(o, za, u, y_f, y_b, zs, x, target, ssm_d, w_glu, b_glu, g_attn, g_ssm, w_out, ln_g, ln_b, tb):
    seq = x.shape[0]

    def body(o_ref, za_ref, u_ref, yf_ref, yb_ref, zs_ref, x_ref, t_ref, d_ref, wg_ref, bg_ref, ga_ref, gs_ref,
             wo_ref, lg_ref, lb_ref,
             loss_ref, do_ref, dza_ref, dyl_ref, dzs_ref, dpre_ref, gwo_ref, gwg_ref, vec_ref, wop):
        @pl.when(pl.program_id(0) == 0)
        def _():
            for ref in (loss_ref, gwo_ref, gwg_ref, vec_ref):
                ref[...] = jnp.zeros_like(ref)
            for nat, par in _pair_blocks(0):
                wop[par, :] = wo_ref[nat, :]
            wop[D_ATTN:, :] = wo_ref[D_ATTN:, :]

        def rows_of(rs):
            o, za = o_ref[rs, :], za_ref[rs, :]
            sig_a = _sigmoid(za)
            silu_a = za * sig_a
            ya = o * silu_a
            r_a = lax.rsqrt(jnp.mean(ya * ya, axis=1, keepdims=True) + NORM_EPS)
            n_a = ya * r_a
            g_a = ga_ref[...]
            unslab = lambda ref: jnp.concatenate([ref[k, rs, :] for k in range(N_SLAB)], axis=1)
            u_blk, zs = unslab(u_ref), zs_ref[rs, :]
            d_row = d_ref[...]
            ylin = d_row * u_blk + unslab(yf_ref) + unslab(yb_ref)
            inner = GELU_C * (ylin + GELU_K * ylin * ylin * ylin)
            th = jnp.tanh(inner)
            gl = 0.5 * ylin * (1.0 + th)
            glb = gl.astype(BF16)
            gate = _dot(glb, wg_ref[...])
            sg = _sigmoid(gate + bg_ref[...])
            y2 = gl * sg
            sig_s = _sigmoid(zs)
            silu_s = zs * sig_s
            ys = y2 * silu_s
            r_s = lax.rsqrt(jnp.mean(ys * ys, axis=1, keepdims=True) + NORM_EPS)
            n_s = ys * r_s
            g_s = gs_ref[...]
            mixed = jnp.concatenate([n_a * g_a, n_s * g_s], axis=1).astype(BF16)
            out = _dot(mixed, wop[...])
            pre = ALPHA * x_ref[rs, :] + out
            mu = jnp.mean(pre, axis=1, keepdims=True)
            cen = pre - mu
            rstd = lax.rsqrt(jnp.mean(cen * cen, axis=1, keepdims=True) + NORM_EPS)
            hhat = cen * rstd
            ln_g = lg_ref[...]
            err = hhat * ln_g + lb_ref[...] - t_ref[rs, :]
            loss_ref[...] += 0.5 * jnp.sum(jnp.mean(err * err, axis=1, keepdims=True))

            dh = err * (1.0 / D_MODEL)
            vec_ref[0:1, :] += jnp.sum(dh * hhat, axis=0, keepdims=True)
            vec_ref[1:2, :] += jnp.sum(dh, axis=0, keepdims=True)
            dhh = dh * ln_g
            dpre = rstd * (dhh - jnp.mean(dhh, axis=1, keepdims=True)
                           - hhat * jnp.mean(dhh * hhat, axis=1, keepdims=True))
            dpre_ref[rs, :] = dpre
            dpb = dpre.astype(BF16)
            for j in range(4):
                g_pair = _dot_tn(mixed[:, 128 * j:128 * (j + 1)], dpb)
                for g in range(2):
                    nat = HEAD_DIM * (4 * g + j)
                    gwo_ref[nat:nat + HEAD_DIM, :] += g_pair[HEAD_DIM * g:HEAD_DIM * (g + 1), :]
            gwo_ref[D_ATTN:, :] += _dot_tn(mixed[:, D_ATTN:], dpb)
            dmix = _dot_nt(dpb, wop[...])
            dna = dmix[:, :D_ATTN]
            vec_ref[2:3, 0:D_ATTN] += jnp.sum(dna * n_a, axis=0, keepdims=True)
            dna = dna * g_a
            dya = r_a * (dna - n_a * jnp.mean(dna * n_a, axis=1, keepdims=True))
            do_ref[rs, :] = dya * silu_a
            dza_ref[rs, :] = dya * o * (sig_a * (1.0 + za * (1.0 - sig_a)))
            dns = dmix[:, D_ATTN:]
            vec_ref[2:3, D_ATTN:] += jnp.sum(dns * n_s, axis=0, keepdims=True)
            dns = dns * g_s
            dys = r_s * (dns - n_s * jnp.mean(dns * n_s, axis=1, keepdims=True))
            dzs_ref[rs, :] = dys * y2 * (sig_s * (1.0 + zs * (1.0 - sig_s)))
            dy2 = dys * silu_s
            da = dy2 * gl * sg * (1.0 - sg)
            vec_ref[3:4, D_SSM:] += jnp.sum(da, axis=0, keepdims=True)
            dab = da.astype(BF16)
            gwg_ref[...] += _dot_tn(glb, dab)
            dgl_mm = _dot_nt(dab, wg_ref[...])
            dgl = dy2 * sg + dgl_mm
            dylin = dgl * (0.5 * (1.0 + th)
                           + 0.5 * ylin * (1.0 - th * th) * GELU_C * (1.0 + 3.0 * GELU_K * ylin * ylin))
            for k in range(N_SLAB):
                dyl_ref[k, rs, :] = dylin[:, k * SLAB_IN:(k + 1) * SLAB_IN]
            vec_ref[3:4, 0:D_SSM] += jnp.sum(dylin * u_blk, axis=0, keepdims=True)

        rows_of(slice(0, tb))

    tok = lambda w: pl.BlockSpec((tb, w), lambda i: (i, 0))
    slab = pl.BlockSpec((N_SLAB, tb, SLAB_IN), lambda i: (0, i, 0))
    const = lambda r, c: pl.BlockSpec((r, c), lambda i: (0, 0), pipeline_mode=pl.Buffered(1))
    tok_shape = jax.ShapeDtypeStruct((seq, 512), F32)
    return pl.pallas_call(
        body, name="mid", grid=(seq // tb,),
        in_specs=[tok(512), tok(512), slab, slab, slab, tok(512), tok(1024), tok(1024),
                  const(1, 512), const(512, 512), const(1, 512), const(1, 512), const(1, 512),
                  const(1024, 1024), const(1, 1024), const(1, 1024)],
        out_specs=[const(8, 128), tok(512), tok(512), slab, tok(512), tok(1024),
                   const(1024, 1024), const(512, 512), const(8, 1024)],
        out_shape=[jax.ShapeDtypeStruct((8, 128), F32), tok_shape, tok_shape,
                   jax.ShapeDtypeStruct((N_SLAB, seq, SLAB_IN), F32), tok_shape,
                   jax.ShapeDtypeStruct((seq, 1024), F32), jax.ShapeDtypeStruct((1024, 1024), F32),
                   jax.ShapeDtypeStruct((512, 512), F32), jax.ShapeDtypeStruct((8, 1024), F32)],
        scratch_shapes=[pltpu.VMEM((D_MODEL, D_MODEL), BF16)],
        compiler_params=pltpu.CompilerParams(dimension_semantics=("arbitrary",), vmem_limit_bytes=MID_VMEM),
    )(o, za, u, y_f, y_b, zs, x, target, ssm_d, w_glu, b_glu, g_attn, g_ssm, w_out, ln_g, ln_b)


def _ride_shapes(pieces, narrow, gather_last):
    outs = [p.shape if (gather_last and a == len(pieces) - 1) else p.shape[1:] for a, p in enumerate(pieces)]
    return outs, [pltpu.VMEM(s, F32) for s in outs] + _reduce_scratch([p.shape for p in pieces], narrow)


def _ride_phases(piece_refs, out_refs, scratch_refs, narrow, gather_last):
    n = len(piece_refs)
    landing, rest = scratch_refs[:n], scratch_refs[n:]
    begin, exchange, combine, finish = _reduce_phases(piece_refs, landing, rest[:n], rest[n:2 * n], rest[2 * n:3 * n],
                                                      *rest[3 * n:], narrow, gather_last)

    def end():
        finish()
        for a in range(n):
            out_refs[a][...] = landing[a][...]

    return begin, exchange, combine, end


def _dproj_block(dq_ref, dk_ref, dv_ref, dza_ref, duf_ref, dub_ref, dyl_ref, dzs_ref, d_ref, hi_ref, lo_ref, tb):
    cos, sin = _rope_block(hi_ref, lo_ref, pl.program_id(0) * (tb // ROPE_GROUP), tb // ROPE_GROUP)
    lo = lax.broadcasted_iota(jnp.int32, (tb, 128), 1) < HEAD_DIM

    def unrope(t):
        return t * cos + _rotate_half_unsigned(t * sin)

    def natural(pairs):
        swapped = [pltpu.roll(t, HEAD_DIM, 1) for t in pairs]
        return [jnp.where(lo, pairs[0], swapped[1]), jnp.where(lo, pairs[2], swapped[3]),
                jnp.where(lo, swapped[0], pairs[1]), jnp.where(lo, swapped[2], pairs[3])]

    dq_rot, dza = dq_ref[...], dza_ref[...]
    pieces = natural([unrope(dq_rot[:, 128 * j:128 * (j + 1)]) for j in range(4)])
    d_row = d_ref[...]
    pieces += [unrope(dk_ref[...]), dv_ref[...]] + natural([dza[:, 128 * j:128 * (j + 1)] for j in range(4)])
    pieces += [duf_ref[k] + dub_ref[k] + d_row[:, k * SLAB_IN:(k + 1) * SLAB_IN] * dyl_ref[k] for k in range(N_SLAB)]
    pieces += [dzs_ref[...]]
    return jnp.concatenate(pieces, axis=1).astype(BF16)


def _dproj_specs(tb, rope_hi, rope_lo):
    tok = lambda w: pl.BlockSpec((tb, w), lambda i: (i, 0))
    slab = pl.BlockSpec((N_SLAB, tb, SLAB_IN), lambda i: (0, i, 0))
    table = lambda t: pl.BlockSpec(t.shape, lambda i: (0, 0, 0))
    return [tok(512), tok(128), tok(128), tok(512), slab, slab, slab, tok(512), pl.BlockSpec((1, 512), lambda i: (0, 0)),
            table(rope_hi), table(rope_lo)]


N_DPROJ = 11
GW_ROWS = 768


def _proj_bwd_w(x, dproj_args, rope_hi, rope_lo, pieces, tb):
    seq = x.shape[0]
    steps = seq // tb
    n_p = len(pieces)
    narrow = [False] * n_p

    def body(*refs):
        x_ref, grads = refs[0], refs[1:1 + N_DPROJ]
        piece_refs = refs[1 + N_DPROJ:1 + N_DPROJ + n_p]
        gw_ref = refs[1 + N_DPROJ + n_p]
        out_refs = refs[2 + N_DPROJ + n_p:2 + N_DPROJ + 2 * n_p]
        step = pl.program_id(0)
        if n_p:
            begin, exchange, combine, end = _ride_phases(piece_refs, out_refs, refs[2 + N_DPROJ + 2 * n_p:], narrow, True)
            pl.when(step == 0)(begin)
            pl.when(step == min(1, steps - 1))(exchange)
            pl.when(step == steps // 2)(combine)

        @pl.when(step == 0)
        def _():
            gw_ref[...] = jnp.zeros_like(gw_ref)

        dproj = _dproj_block(*grads, tb)
        xb = x_ref[...].astype(BF16)
        for r0 in range(0, D_IN_PROJ, GW_ROWS):
            gw_ref[r0:r0 + GW_ROWS, :] += _dot_tn(dproj[:, r0:r0 + GW_ROWS], xb)
        if n_p:
            pl.when(step == steps - 1)(end)

    vmem = pl.BlockSpec(memory_space=pltpu.VMEM)
    whole = pl.BlockSpec((D_IN_PROJ, D_MODEL), lambda i: (0, 0), pipeline_mode=pl.Buffered(1))
    ride_outs, ride_scratch = _ride_shapes(pieces, narrow, True) if n_p else ([], [])
    return pl.pallas_call(
        body, name="proj_bwd_w", grid=(steps,),
        in_specs=[pl.BlockSpec((tb, D_MODEL), lambda i: (i, 0))] + _dproj_specs(tb, rope_hi, rope_lo) + [vmem] * n_p,
        out_specs=[whole] + [vmem] * n_p,
        out_shape=[jax.ShapeDtypeStruct((D_IN_PROJ, D_MODEL), F32)] + [jax.ShapeDtypeStruct(s, F32) for s in ride_outs],
        scratch_shapes=ride_scratch,
        compiler_params=_cparams(("arbitrary",)),
    )(x, *dproj_args, rope_hi, rope_lo, *pieces)


def _proj_bwd_x(dproj_args, rope_hi, rope_lo, dpre, wt, pieces, tb):
    seq = dpre.shape[0]
    steps = seq // tb
    n_p = len(pieces)
    narrow = [True] * n_p

    def body(*refs):
        grads = refs[:N_DPROJ]
        dpre_ref, wt_ref = refs[N_DPROJ:N_DPROJ + 2]
        piece_refs = refs[N_DPROJ + 2:N_DPROJ + 2 + n_p]
        gx_ref = refs[N_DPROJ + 2 + n_p]
        out_refs = refs[N_DPROJ + 3 + n_p:N_DPROJ + 3 + 2 * n_p]
        step = pl.program_id(0)
        if n_p:
            begin, exchange, combine, end = _ride_phases(piece_refs, out_refs, refs[N_DPROJ + 3 + 2 * n_p:], narrow, False)
            pl.when(step == 0)(begin)
            pl.when(step == min(1, steps - 1))(exchange)
            pl.when(step == steps - 1)(combine)

        dproj = _dproj_block(*grads, tb)
        gx_ref[...] = ALPHA * dpre_ref[...] + _dot(dproj, wt_ref[...])
        if n_p:
            pl.when(step == steps - 1)(end)

    vmem = pl.BlockSpec(memory_space=pltpu.VMEM)
    whole = pl.BlockSpec((D_IN_PROJ, D_MODEL), lambda i: (0, 0), pipeline_mode=pl.Buffered(1))
    ride_outs, ride_scratch = _ride_shapes(pieces, narrow, False) if n_p else ([], [])
    return pl.pallas_call(
        body, name="proj_bwd_x", grid=(steps,),
        in_specs=_dproj_specs(tb, rope_hi, rope_lo) + [pl.BlockSpec((tb, D_MODEL), lambda i: (i, 0)), whole] + [vmem] * n_p,
        out_specs=[pl.BlockSpec((tb, D_MODEL), lambda i: (i, 0))] + [vmem] * n_p,
        out_shape=[jax.ShapeDtypeStruct((seq, D_MODEL), F32)] + [jax.ShapeDtypeStruct(s, F32) for s in ride_outs],
        scratch_shapes=ride_scratch,
        compiler_params=pltpu.CompilerParams(dimension_semantics=("arbitrary",), vmem_limit_bytes=PROJ_BWD_X_VMEM),
    )(*dproj_args, rope_hi, rope_lo, dpre, wt, *pieces)


def _adamw(w, g, m, v, name):
    rows, cols = w.shape
    tb = rows
    while tb * cols * 4 > ADAMW_BLOCK_BYTES and tb % 16 == 0:
        tb //= 2

    def body(w_ref, g_ref, m_ref, v_ref, d_ref, nm_ref, nv_ref):
        _adamw_update(w_ref, g_ref, m_ref, v_ref, d_ref, nm_ref, nv_ref)

    spec = pl.BlockSpec((tb, cols), lambda i: (i, 0))
    return pl.pallas_call(
        body, name=name, grid=(rows // tb,), in_specs=[spec] * 4, out_specs=[spec] * 3,
        out_shape=[jax.ShapeDtypeStruct((rows, cols), F32)] * 3,
        compiler_params=_cparams(("arbitrary",)),
    )(w, g, m, v)


def _adamw_update(w_ref, g_ref, m_ref, v_ref, d_ref, nm_ref, nv_ref):
    g_blk = g_ref[...]
    m_new = ADAM_B1 * m_ref[...] + (1.0 - ADAM_B1) * g_blk
    v_new = ADAM_B2 * v_ref[...] + (1.0 - ADAM_B2) * (g_blk * g_blk)
    m_hat = m_new / (1.0 - ADAM_B1 ** ADAM_STEP)
    v_hat = v_new / (1.0 - ADAM_B2 ** ADAM_STEP)
    d_ref[...] = -ADAM_LR * (m_hat / (jnp.sqrt(v_hat) + ADAM_EPS) + ADAM_WD * w_ref[...])
    nm_ref[...] = m_new
    nv_ref[...] = v_new


def _adamw_many(groups, name):
    n = len(groups)

    def body(*refs):
        for p in range(n):
            w_ref, g_ref, m_ref, v_ref = refs[4 * p:4 * p + 4]
            gn_ref, d_ref, nm_ref, nv_ref = refs[4 * n + 4 * p:4 * n + 4 * p + 4]
            gn_ref[...] = g_ref[...].reshape(w_ref.shape)
            _adamw_update(w_ref, gn_ref, m_ref, v_ref, d_ref, nm_ref, nv_ref)

    return pl.pallas_call(
        body, name=name,
        out_shape=[jax.ShapeDtypeStruct(grp[0].shape, F32) for grp in groups for _ in range(4)],
    )(*[a for grp in groups for a in grp])


_WEIGHTS = ["w_in", "attn_sink", "ssm_a_re", "ssm_a_im", "ssm_log_dt", "ssm_b_re", "ssm_b_im", "ssm_c_re", "ssm_c_im",
            "ssm_d", "w_glu", "b_glu", "norm_attn_g", "norm_ssm_g", "w_out", "ln_g", "ln_b"]
N_DG = N_DIR * N_GROUPS
BIG_ROWS = N_DG * SSM_CH * SSM_STATE // 128
TINY_ROWS = 64


def _pack_small_grads(g_bc, g_vec, g_ar, g_ai, g_dt, g_sink, loss):
    big = jnp.stack([t.reshape(BIG_ROWS, 128) for t in g_bc])
    row = lambda t: jnp.pad(t.reshape(1, -1), ((0, 0), (0, 128 - t.size)))
    tiny = jnp.concatenate([g_vec.reshape(64, 128), g_ar.reshape(32, 128), g_ai.reshape(32, 128), row(g_dt), row(g_sink),
                            row(loss), jnp.zeros((N_CHIPS * TINY_ROWS - 131, 128), F32)], axis=0)
    return jnp.concatenate([big, tiny.reshape(N_CHIPS, TINY_ROWS, 128)], axis=1)


def _unpack_small_grads(packed):
    big = packed[:, :BIG_ROWS].reshape(N_CHIPS, 2 * BIG_ROWS, SSM_STATE)
    tiny = packed[:, BIG_ROWS:].reshape(N_CHIPS * TINY_ROWS, 128)
    g_vec = tiny[0:64].reshape(8, 1024)
    return tiny[130, 0], {
        "ssm_b_re": big[0], "ssm_b_im": big[1], "ssm_c_re": big[2], "ssm_c_im": big[3],
        "ln_g": g_vec[0:1], "ln_b": g_vec[1:2],
        "norm_attn_g": _from_pair_order(g_vec[2:3, :D_ATTN]), "norm_ssm_g": g_vec[2:3, D_ATTN:],
        "ssm_d": g_vec[3:4, :D_SSM], "b_glu": g_vec[3:4, D_SSM:],
        "ssm_a_re": tiny[64:96].reshape(N_DG, SSM_STATE), "ssm_a_im": tiny[96:128].reshape(N_DG, SSM_STATE),
        "ssm_log_dt": tiny[128:129, :N_DG].reshape(N_DIR, N_GROUPS), "attn_sink": tiny[129:130, :N_Q_HEADS],
    }


def _small_unview(name, t, shape):
    if name in ("ssm_b_re", "ssm_b_im"):
        return jnp.swapaxes(t.reshape(N_DIR, N_GROUPS, SSM_CH, SSM_STATE), 2, 3).reshape(shape)
    return t.reshape(shape)


def _channel_major(name, t):
    return jnp.swapaxes(t, 3, 4) if name in ("ssm_b_re", "ssm_b_im") else t


def kernel(x, w_in, attn_sink, ssm_a_re, ssm_a_im, ssm_log_dt, ssm_b_re, ssm_b_im, ssm_c_re, ssm_c_im, ssm_d, w_glu, b_glu, norm_attn_g, norm_ssm_g, w_out, ln_g, ln_b, loss_target, m_w_in, m_attn_sink, m_ssm_a_re, m_ssm_a_im, m_ssm_log_dt, m_ssm_b_re, m_ssm_b_im, m_ssm_c_re, m_ssm_c_im, m_ssm_d, m_w_glu, m_b_glu, m_norm_attn_g, m_norm_ssm_g, m_w_out, m_ln_g, m_ln_b, v_w_in, v_attn_sink, v_ssm_a_re, v_ssm_a_im, v_ssm_log_dt, v_ssm_b_re, v_ssm_b_im, v_ssm_c_re, v_ssm_c_im, v_ssm_d, v_w_glu, v_b_glu, v_norm_attn_g, v_norm_ssm_g, v_w_out, v_ln_g, v_ln_b):
    args = dict(locals())
    weights = {n: args[n] for n in _WEIGHTS}
    mom_m = {n: args["m_" + n] for n in _WEIGHTS}
    mom_v = {n: args["v_" + n] for n in _WEIGHTS}
    xs = x[0]
    target = loss_target[0]

    (wt_g,) = _all_gather_chips([w_in[0].T], BF16, "gather_weights")
    wt_full = wt_g.reshape(D_IN_PROJ, D_MODEL)

    g_x, r_wt, r_w_out, r_w_glu, g_small_all = _local_step(
        xs, target, wt_full, w_glu[0], w_out[0], attn_sink, ssm_a_re, ssm_a_im, ssm_log_dt, ssm_b_re, ssm_b_im,
        ssm_c_re, ssm_c_im, ssm_d, b_glu, norm_attn_g, norm_ssm_g, ln_g, ln_b, sharded=True)
    loss, small_grads = _unpack_small_grads(g_small_all)

    grads, deltas, new_m, new_v = {}, {}, {}, {}
    d_w, m_w, v_w = _adamw(w_in[0].T, r_wt, m_w_in[0].T, v_w_in[0].T, "adamw_w_in")
    grads["w_in"], deltas["w_in"], new_m["w_in"], new_v["w_in"] = r_wt.T[None], d_w.T[None], m_w.T[None], v_w.T[None]
    for n, g in (("w_out", r_w_out), ("w_glu", r_w_glu)):
        d_w, m_w, v_w = _adamw(weights[n][0], g, mom_m[n][0], mom_v[n][0], "adamw_" + n)
        grads[n], deltas[n], new_m[n], new_v[n] = g[None], d_w[None], m_w[None], v_w[None]
    names = sorted(small_grads)
    updates = _adamw_many([(_channel_major(n, weights[n]), small_grads[n], _channel_major(n, mom_m[n]),
                            _channel_major(n, mom_v[n])) for n in names], "adamw_small")
    for i, n in enumerate(names):
        grads[n], deltas[n], new_m[n], new_v[n] = (_channel_major(n, t) for t in updates[4 * i:4 * i + 4])

    return (loss, g_x[None], *[grads[n] for n in _WEIGHTS], *[deltas[n] for n in _WEIGHTS],
            *[new_m[n] for n in _WEIGHTS], *[new_v[n] for n in _WEIGHTS])


def _local_step(xs, target, wt_full, w_glu_in, w_out_in, attn_sink, ssm_a_re, ssm_a_im, ssm_log_dt, ssm_b_re,
                ssm_b_im, ssm_c_re, ssm_c_im, ssm_d, b_glu, norm_attn_g, norm_ssm_g, ln_g, ln_b, sharded):
    seq = xs.shape[0]

    a_r, a_i = ssm_a_re, ssm_a_im
    log_dt = ssm_log_dt.reshape(N_DG, 1)
    b_r, b_i = _channel_major("ssm_b_re", ssm_b_re), _channel_major("ssm_b_im", ssm_b_im)
    c_r, c_i = ssm_c_re, ssm_c_im
    ssm_tb = min(SSM_BLOCK, seq)
    sub_len = ssm_tb // SUBSEG
    lam, bb, bbt, cb, cb_t = _ssm_params_fwd(a_r, a_i, log_dt, b_r, b_i, c_r, c_i, int(math.log2(sub_len)))
    lam = lam.reshape(4, N_DIR, 1, STATE_W)

    rope_hi, rope_lo = _rope_tables(seq)
    projected = _proj(xs, wt_full, rope_hi, rope_lo, [w_glu_in, w_out_in] if sharded else [], min(512, seq))
    q_stack, k_rot, v_bf, z_attn, u, z_ssm = projected[:6]
    if sharded:
        w_glu_full, w_out_full = projected[6].reshape(D_SSM, D_SSM), projected[7].reshape(D_MODEL, D_MODEL)
    else:
        w_glu_full, w_out_full = w_glu_in, w_out_in
    sink128 = jnp.broadcast_to(attn_sink[0][:, None, None], (N_Q_HEADS, 1, 128))
    attn_bias = _attn_bias()
    o = _attn_fwd(q_stack, k_rot, v_bf, sink128, attn_bias)
    ys, starts = [], []
    for d in range(N_DIR):
        y_d, s_r, s_i = _ssm_fwd(u, lam, bb, cb, direction=d, tb=ssm_tb, name=f"ssm_fwd_{d}")
        ys.append(y_d)
        starts.append((s_r, s_i))

    row = lambda t: t.reshape(1, -1)
    g_attn_p = _to_pair_order(norm_attn_g)
    loss_blk, d_o, d_za, d_ylin, d_zs, d_pre, g_w_out, g_w_glu, g_vec = _mid(
        o, z_attn, u, ys[0], ys[1], z_ssm, xs, target, row(ssm_d), w_glu_full, row(b_glu),
        g_attn_p, row(norm_ssm_g), w_out_full, row(ln_g), row(ln_b), min(MID_BLOCK, seq))

    pieces = [g_w_glu.reshape(N_CHIPS, -1, D_SSM), g_w_out.reshape(N_CHIPS, -1, D_MODEL)] if sharded else []
    attn_grads = _attn_bwd(q_stack, k_rot, v_bf, sink128, attn_bias, d_o, pieces)
    dq, dk, dv, g_sink = attn_grads[:4]
    if sharded:
        g_w_glu, g_w_out = attn_grads[4:]
    dus, g_bb, g_cb, g_lam = [], [], [], []
    for d in range(N_DIR):
        du_d, gb_d, gc_d, dl_d = _ssm_bwd(u, d_ylin, starts[d], lam, bb, bbt, cb_t, direction=d, tb=ssm_tb,
                                          name=f"ssm_bwd_{d}")
        dus.append(du_d)
        g_bb.append(gb_d)
        g_cb.append(gc_d)
        g_lam.append(dl_d)
    g_ar, g_ai, g_dt, g_br, g_bi, g_cr, g_ci = _ssm_params_bwd(a_r, a_i, log_dt, b_r, b_i, g_bb, g_cb, g_lam)

    g_small = _pack_small_grads([g_br, g_bi, g_cr, g_ci], g_vec, g_ar, g_ai, g_dt, g_sink[:, 0], loss_blk[0, 0])
    dproj_args = (dq, dk, dv, d_za, dus[0], dus[1], d_ylin, d_zs, row(ssm_d))
    w_grads = _proj_bwd_w(xs, dproj_args, rope_hi, rope_lo, [g_small] if sharded else [], min(512, seq))
    g_wt = w_grads[0]
    if sharded:
        g_small = w_grads[1]
    x_grads = _proj_bwd_x(dproj_args, rope_hi, rope_lo, d_pre, wt_full,
                          [g_wt.reshape(N_CHIPS, -1, D_MODEL)] if sharded else [], min(512, seq))
    g_x = x_grads[0]
    if sharded:
        g_wt = x_grads[1]
    return g_x, g_wt, g_w_out, g_w_glu, g_small
```

```python
import functools
import math

import numpy as np
import jax
import jax.numpy as jnp
from jax import lax
from jax.experimental import pallas as pl
from jax.experimental.pallas import tpu as pltpu

F32 = jnp.float32
BF16 = jnp.bfloat16
MESH = pl.DeviceIdType.MESH

D_MODEL = 1024
D_ATTN = 512
D_SSM = 512
HEAD_DIM = 64
N_Q_HEADS = 8
WINDOW = 128
ROPE_THETA = 10000.0
SSM_CH = 16
N_GROUPS = 32
SSM_STATE = 64
N_DIR = 2
STATE_W = N_GROUPS * SSM_STATE
N_SLAB = 4
SLAB_IN = 128
SLAB_ST = 512
NORM_EPS = 1e-5
NEG_INF = -1e30
ALPHA = 2.0 ** 0.25
D_IN_PROJ = 2304
N_CHIPS = 4

ADAM_LR = 0.001
ADAM_B1 = 0.9
ADAM_B2 = 0.999
ADAM_EPS = 1e-08
ADAM_WD = 0.01
ADAM_STEP = 10

SUBSEG = 8
SCAN_LANES = 512
SSM_BLOCK = 512
VMEM_LIMIT = 60 * 1024 * 1024
ADAMW_BLOCK_BYTES = 3 * 512 * 1024
MID_BLOCK = 512

def _to_pair_order(row):
    return jnp.transpose(row.reshape(2, 4, HEAD_DIM), (1, 0, 2)).reshape(1, D_ATTN)


def _from_pair_order(row):
    return jnp.transpose(row.reshape(4, 2, HEAD_DIM), (1, 0, 2)).reshape(1, D_ATTN)


def _cparams(sem=None):
    return pltpu.CompilerParams(dimension_semantics=sem, vmem_limit_bytes=VMEM_LIMIT)


def _dot(a, b):
    return jnp.dot(a, b, preferred_element_type=F32)


def _dot_nt(a, b):
    return lax.dot_general(a, b, (((1,), (1,)), ((), ())), preferred_element_type=F32)


def _dot_tn(a, b):
    return lax.dot_general(a, b, (((0,), (0,)), ((), ())), preferred_element_type=F32)


def _sigmoid(z):
    return 0.5 * jnp.tanh(0.5 * z) + 0.5


def _all_gather_chips(shards, out_dtype, name):
    n = len(shards)

    def body(*refs):
        start, relay, finish = _gather_phases(refs[:n], refs[n:2 * n], *refs[2 * n:], out_dtype)
        start()
        relay()
        finish()

    vmem = pl.BlockSpec(memory_space=pltpu.VMEM)
    return pl.pallas_call(
        body, name=name,
        out_shape=[jax.ShapeDtypeStruct((N_CHIPS,) + s.shape, out_dtype) for s in shards],
        in_specs=[vmem] * n, out_specs=[vmem] * n,
        scratch_shapes=_gather_sems(n),
        compiler_params=pltpu.CompilerParams(vmem_limit_bytes=VMEM_LIMIT),
    )(*shards)


def _gather_sems(n):
    return [pltpu.SemaphoreType.DMA((6 * n,)), pltpu.SemaphoreType.DMA((6 * n,))]


def _gather_phases(in_refs, out_refs, send_sems, recv_sems, out_dtype):
    n = len(in_refs)
    x, y, c = lax.axis_index("x"), lax.axis_index("y"), lax.axis_index("c")
    sibling = (x, y, 1 - c)
    chips = [(1 - x, y), (x, 1 - y), (1 - x, 1 - y)]

    def half_of(a, px, py, half):
        rows = in_refs[a].shape[0] // 2
        return out_refs[a].at[2 * px + py, pl.ds(half * rows, rows), :]

    def copy(a, k, px, py, half, to):
        blk = half_of(a, px, py, half)
        return pltpu.make_async_remote_copy(src_ref=blk, dst_ref=blk, send_sem=send_sems.at[6 * a + k],
                                            recv_sem=recv_sems.at[6 * a + k], device_id=to, device_id_type=MESH)

    first = [copy(a, j, x, y, c, (*chips[j], c)) for a in range(n) for j in range(3)]
    passed = [copy(a, 3 + j, *chips[j], c, sibling) for a in range(n) for j in range(3)]

    def start():
        for a in range(n):
            out_refs[a][2 * x + y] = in_refs[a][...].astype(out_dtype)
        for cp in first:
            cp.start()

    def relay():
        for a in range(n):
            for j in range(3):
                copy(a, j, *chips[j], c, (x, y, c)).wait_recv()
                passed[3 * a + j].start()

    def finish():
        for a in range(n):
            for j in range(3):
                copy(a, 3 + j, *chips[j], 1 - c, (x, y, c)).wait_recv()
        for cp in first + passed:
            cp.wait_send()

    return start, relay, finish


SEMS_PER_ARRAY = 14


def _reduce_scratch(shapes, narrow):
    half = [(N_CHIPS, s[1] // 2, s[2]) for s in shapes]
    wire = [BF16 if nar else F32 for nar in narrow]
    n = len(shapes)
    return ([pltpu.VMEM(half[a], F32) for a in range(n)] + [pltpu.VMEM(half[a], wire[a]) for a in range(n)]
            + [pltpu.VMEM(half[a], wire[a]) for a in range(n)]
            + [pltpu.SemaphoreType.DMA((SEMS_PER_ARRAY * n,)), pltpu.SemaphoreType.DMA((SEMS_PER_ARRAY * n,))])


def _reduce_phases(p_refs, out_refs, a_refs, s_refs, b_refs, send_sems, recv_sems, narrow, gather_last):
    n = len(p_refs)
    halves = [p.shape[1] // 2 for p in p_refs]
    wire = [BF16 if nar else F32 for nar in narrow]
    x, y, c = lax.axis_index("x"), lax.axis_index("y"), lax.axis_index("c")
    me = 2 * x + y
    sibling = (x, y, 1 - c)
    chips = [(1 - x, y), (x, 1 - y), (1 - x, 1 - y)]
    slot = [2 * px + py for px, py in chips]
    last = n - 1

    def copy(a, k, src, dst, to):
        return pltpu.make_async_remote_copy(src_ref=src, dst_ref=dst, send_sem=send_sems.at[SEMS_PER_ARRAY * a + k],
                                            recv_sem=recv_sems.at[SEMS_PER_ARRAY * a + k],
                                            device_id=to, device_id_type=MESH)

    def rows(a, half):
        return pl.ds(pl.multiple_of(half * halves[a], 16), halves[a])

    def finished(a, k, half):
        if gather_last and a == last:
            return out_refs[a].at[k, rows(a, half), :]
        return out_refs[a].at[rows(a, half), :]

    order = slot + [me]
    swaps = [[copy(a, q, p_refs[a].at[order[q], rows(a, 1 - c), :], a_refs[a].at[order[q]], sibling)
              for q in range(N_CHIPS)] for a in range(n)]
    sends = [[copy(a, 4 + j, s_refs[a].at[slot[j]], b_refs[a].at[me], (*chips[j], c)) for j in range(3)] for a in range(n)]
    backs = [copy(a, 7, finished(a, me, c), finished(a, me, c), sibling) for a in range(n)]
    spread = [copy(last, 8 + j, finished(last, me, c), finished(last, me, c), (*chips[j], c)) for j in range(3)]
    relays = [copy(last, 11 + j, finished(last, slot[j], c), finished(last, slot[j], c), sibling) for j in range(3)]

    def start():
        for group in swaps:
            for cp in group:
                cp.start()

    def exchange():
        for a in range(n):
            for q in range(N_CHIPS):
                swaps[a][q].wait_recv()
                acc = a_refs[a][order[q]] + p_refs[a][order[q], rows(a, c), :]
                a_refs[a][order[q]] = acc
                s_refs[a][order[q]] = acc.astype(wire[a])
                if q < 3:
                    sends[a][q].start()
            b_refs[a][me] = s_refs[a][me]

    def combine():
        for a in range(n):
            for j in range(3):
                copy(a, 4 + j, s_refs[a].at[slot[j]], b_refs[a].at[slot[j]], (x, y, c)).wait_recv()
            terms = [jnp.where(me == k, a_refs[a][k], b_refs[a][k].astype(F32)) for k in range(N_CHIPS)]
            total = (terms[0] + terms[1]) + (terms[2] + terms[3])
            if gather_last and a == last:
                out_refs[a][me, rows(a, c), :] = total
            else:
                out_refs[a][rows(a, c), :] = total
            backs[a].start()
        if gather_last:
            for cp in spread:
                cp.start()

    def finish():
        if gather_last:
            for j in range(3):
                copy(last, 8 + j, finished(last, slot[j], c), finished(last, slot[j], c), (x, y, c)).wait_recv()
                relays[j].start()
        for a in range(n):
            copy(a, 7, finished(a, me, 1 - c), finished(a, me, 1 - c), (x, y, c)).wait_recv()
        if gather_last:
            for j in range(3):
                copy(last, 11 + j, finished(last, slot[j], 1 - c), finished(last, slot[j], 1 - c), (x, y, c)).wait_recv()
        started = [cp for group in swaps + sends for cp in group] + backs + (spread + relays if gather_last else [])
        for cp in started:
            cp.wait_send()

    return start, exchange, combine, finish


def _ssm_param_values(ar, ai, logdt):
    dt = jnp.exp(logdt)
    mag = jnp.exp(dt * ar)
    cs, sn = jnp.cos(dt * ai), jnp.sin(dt * ai)
    lr, li = mag * cs, mag * sn
    den = ar * ar + ai * ai
    nr = (lr - 1.0) * ar + li * ai
    ni = li * ar - (lr - 1.0) * ai
    return dt, mag, lr, li, den, nr, ni


GROUPS_PER_SLAB = N_GROUPS // N_SLAB


def _slab_masks():
    def eq(shape, f_row, f_col):
        return (f_row(lax.broadcasted_iota(jnp.int32, shape, 0)) == f_col(lax.broadcasted_iota(jnp.int32, shape, 1))).astype(F32)
    spread = eq((SSM_STATE, SLAB_ST), lambda r: r, lambda c: c % SSM_STATE)
    spread_t = eq((SLAB_ST, SSM_STATE), lambda r: r % SSM_STATE, lambda c: c)
    keep = eq((SLAB_IN, SLAB_ST), lambda r: r // SSM_CH, lambda c: c // SSM_STATE)
    keep_t = eq((SLAB_ST, SLAB_IN), lambda r: r // SSM_STATE, lambda c: c // SSM_CH)
    repeat = eq((N_DG * SSM_CH, N_DG), lambda r: r // SSM_CH, lambda c: c)
    return spread, spread_t, keep, keep_t, repeat


def _rows(ref):
    return ref[...].reshape(-1, SSM_STATE)


def _split3(t):
    hi = t.astype(BF16)
    rest = t - hi.astype(F32)
    mid = rest.astype(BF16)
    return hi, mid, (rest - mid.astype(F32)).astype(BF16)


def _select(dot, ones01, t, ones_first):
    o = ones01.astype(BF16)
    parts = [dot(o, p) if ones_first else dot(p, o) for p in _split3(t)]
    return (parts[0] + parts[1]) + parts[2]


def _ssm_params_fwd(ar, ai, logdt, br, bi, cr, ci, n_square):
    def body(ar_ref, ai_ref, dt_ref, br_ref, bi_ref, cr_ref, ci_ref, lam_ref, bb_ref, bbt_ref, cb_ref, cbt_ref):
        _, _, lr, li, den, nr, ni = _ssm_param_values(_rows(ar_ref), _rows(ai_ref), dt_ref[...])
        lam_ref[0] = lr
        lam_ref[1] = li
        pr, pi = lr, li
        for _ in range(n_square):
            pr, pi = pr * pr - pi * pi, 2.0 * pr * pi
        lam_ref[2] = pr
        lam_ref[3] = pi
        spread, spread_t, keep, keep_t, repeat = _slab_masks()
        fr = _select(_dot, repeat, nr / den, True)
        fi = _select(_dot, repeat, ni / den, True)
        b_r, b_i = _rows(br_ref), _rows(bi_ref)
        bbar = (fr * b_r - fi * b_i, fr * b_i + fi * b_r)
        c_par = (_rows(cr_ref), _rows(ci_ref))
        spread, spread_t = spread.astype(BF16), spread_t.astype(BF16)
        for src, wide_ref, tall_ref in ((bbar, bb_ref, bbt_ref), (c_par, cbt_ref, cb_ref)):
            for q in range(2):
                for d in range(N_DIR):
                    for k in range(N_SLAB):
                        r0 = (d * N_GROUPS + k * GROUPS_PER_SLAB) * SSM_CH
                        blk = src[q][r0:r0 + SLAB_IN].astype(BF16)
                        wide_ref[q, d, k] = (_dot(blk, spread) * keep).astype(BF16)
                        tall_ref[q, d, k] = (_dot_nt(spread_t, blk) * keep_t).astype(BF16)

    wide = jax.ShapeDtypeStruct((2, N_DIR, N_SLAB, SLAB_IN, SLAB_ST), BF16)
    tall = jax.ShapeDtypeStruct((2, N_DIR, N_SLAB, SLAB_ST, SLAB_IN), BF16)
    return pl.pallas_call(body, name="ssm_params_fwd",
                          out_shape=[jax.ShapeDtypeStruct((4, N_DG, SSM_STATE), F32), wide, tall, tall, wide],
                          compiler_params=pltpu.CompilerParams(vmem_limit_bytes=VMEM_LIMIT),
                          )(ar, ai, logdt, br, bi, cr, ci)


def _ssm_params_bwd(ar, ai, logdt, br, bi, g_slabs_b, g_slabs_c, g_lam):
    def body(ar_ref, ai_ref, dt_ref, br_ref, bi_ref, gb0_ref, gb1_ref, gc0_ref, gc1_ref, gl0_ref, gl1_ref,
             gar_ref, gai_ref, gdt_ref, gbr_ref, gbi_ref, gcr_ref, gci_ref, dbb, dlam):
        spread, spread_t, keep, _, repeat = _slab_masks()
        for d, (gb_ref, gc_ref) in enumerate(((gb0_ref, gc0_ref), (gb1_ref, gc1_ref))):
            for q in range(2):
                for k in range(N_SLAB):
                    r0 = (d * N_GROUPS + k * GROUPS_PER_SLAB) * SSM_CH
                    dbb[q, r0:r0 + SLAB_IN, :] = _select(_dot, spread_t, gb_ref[q, k] * keep, False)
                    out_ref = gcr_ref if q == 0 else gci_ref
                    out_ref[r0:r0 + SLAB_IN, :] = _select(_dot, spread_t, gc_ref[q, k] * keep, False)
        grp = (lax.broadcasted_iota(jnp.int32, (N_GROUPS, STATE_W), 0)
               == lax.broadcasted_iota(jnp.int32, (N_GROUPS, STATE_W), 1) // SSM_STATE).astype(F32)
        pick = (lax.broadcasted_iota(jnp.int32, (STATE_W, SSM_STATE), 0) % SSM_STATE
                == lax.broadcasted_iota(jnp.int32, (STATE_W, SSM_STATE), 1)).astype(F32)
        for d, gl_ref in enumerate((gl0_ref, gl1_ref)):
            for q in range(2):
                row = jnp.sum(gl_ref[q], axis=0, keepdims=True)
                dlam[q, d * N_GROUPS:(d + 1) * N_GROUPS, :] = _select(_dot, pick, grp * row, False)

        a_r, a_i = _rows(ar_ref), _rows(ai_ref)
        dt, mag, lr, li, den, nr, ni = _ssm_param_values(a_r, a_i, dt_ref[...])
        fr = _select(_dot, repeat, nr / den, True)
        fi = _select(_dot, repeat, ni / den, True)
        b_r, b_i = _rows(br_ref), _rows(bi_ref)
        g_r, g_i = dbb[0], dbb[1]
        gbr_ref[...] = fr * g_r + fi * g_i
        gbi_ref[...] = fr * g_i - fi * g_r
        d_fr = _select(_dot_tn, repeat, b_r * g_r + b_i * g_i, True)
        d_fi = _select(_dot_tn, repeat, b_r * g_i - b_i * g_r, True)
        d_nr, d_ni = d_fr / den, d_fi / den
        d_den = -(d_fr * nr + d_fi * ni) / (den * den)
        d_lr = dlam[0] + d_nr * a_r - d_ni * a_i
        d_li = dlam[1] + d_nr * a_i + d_ni * a_r
        d_ar = d_nr * (lr - 1.0) + d_ni * li + d_den * 2.0 * a_r
        d_ai = d_nr * li - d_ni * (lr - 1.0) + d_den * 2.0 * a_i
        d_mag = (d_lr * lr + d_li * li) / mag
        d_theta = d_li * lr - d_lr * li
        gar_ref[...] = d_ar + d_mag * mag * dt
        gai_ref[...] = d_ai + d_theta * dt
        d_dt = d_mag * mag * a_r + d_theta * a_i
        gdt_ref[...] = jnp.sum(d_dt, axis=1, keepdims=True) * dt

    small = jax.ShapeDtypeStruct((N_DG, SSM_STATE), F32)
    big = jax.ShapeDtypeStruct((N_DG * SSM_CH, SSM_STATE), F32)
    return pl.pallas_call(
        body, name="ssm_params_bwd",
        out_shape=[small, small, jax.ShapeDtypeStruct(logdt.shape, F32), big, big, big, big],
        scratch_shapes=[pltpu.VMEM((2,) + big.shape, F32), pltpu.VMEM((2,) + small.shape, F32)],
        compiler_params=pltpu.CompilerParams(vmem_limit_bytes=VMEM_LIMIT),
    )(ar, ai, logdt, br, bi, *g_slabs_b, *g_slabs_c, *g_lam)


ROPE_GROUP = 128


def _rope_tables(seq):
    half = HEAD_DIM // 2
    inv_freq = jnp.tile(ROPE_THETA ** (-jnp.arange(half, dtype=F32) / half), 4)
    sign = jnp.tile(jnp.concatenate([-jnp.ones((half,), F32), jnp.ones((half,), F32)]), 2)

    def table(pos):
        ang = pos.astype(F32)[:, None] * inv_freq[None, :]
        return jnp.stack([jnp.cos(ang), jnp.sin(ang), sign * jnp.sin(ang)])

    return table(jnp.arange(seq // ROPE_GROUP) * ROPE_GROUP), table(jnp.arange(ROPE_GROUP))


def _rope_block(hi_ref, lo_ref, first_group, n_groups):
    cl, sl, sl_s = lo_ref[0], lo_ref[1], lo_ref[2]
    cos, sin = [], []
    for g in range(n_groups):
        ch, sh, sh_s = (hi_ref[q, pl.ds(first_group + g, 1), :] for q in range(3))
        cos.append(ch * cl - sh * sl)
        sin.append(sh_s * cl + ch * sl_s)
    return jnp.concatenate(cos, axis=0), jnp.concatenate(sin, axis=0)


def _rotate_half_unsigned(t):
    lane = lax.broadcasted_iota(jnp.int32, t.shape, 1)
    return jnp.where((lane % HEAD_DIM) < HEAD_DIM // 2, pltpu.roll(t, 96, 1), pltpu.roll(t, 32, 1))


def _rope(t, cos, sin_signed):
    return t * cos + _rotate_half_unsigned(t) * sin_signed


def _pair_blocks(base):
    out = []
    for j in range(4):
        for g in range(2):
            nat = base + HEAD_DIM * (4 * g + j)
            par = base + 128 * j + HEAD_DIM * g
            out.append((slice(nat, nat + HEAD_DIM), slice(par, par + HEAD_DIM)))
    return out


W_Q, W_KV, W_ZA, W_U, W_ZS = 0, 512, 768, 1280, 1792


def _proj(x, wt, rope_hi, rope_lo, shards, tb):
    seq = x.shape[0]
    steps = seq // tb
    n_sh = len(shards)

    def body(*refs):
        x_ref, wt_ref, hi_ref, lo_ref = refs[:4]
        shard_refs = refs[4:4 + n_sh]
        q_ref, k_ref, v_ref, za_ref, u_ref, zs_ref = refs[4 + n_sh:10 + n_sh]
        gathered_refs = refs[10 + n_sh:10 + 2 * n_sh]
        wp = refs[10 + 2 * n_sh]
        step = pl.program_id(0)
        if n_sh:
            landing_refs = refs[11 + 2 * n_sh:11 + 3 * n_sh]
            start, relay, finish = _gather_phases(shard_refs, landing_refs, *refs[11 + 3 * n_sh:], BF16)
            pl.when(step == 0)(start)
            pl.when(step == max(steps - 2, 0))(relay)

        @pl.when(step == 0)
        def _():
            for dst_base, src_base in ((0, W_Q), (512, W_ZA)):
                for nat, par in _pair_blocks(0):
                    wp[dst_base + par.start:dst_base + par.stop, :] = wt_ref[src_base + nat.start:src_base + nat.stop, :]

        xb = x_ref[...].astype(BF16)
        cos, sin = _rope_block(hi_ref, lo_ref, pl.program_id(0) * (tb // ROPE_GROUP), tb // ROPE_GROUP)
        lo = lax.broadcasted_iota(jnp.int32, (tb, 128), 1) < HEAD_DIM
        q = _dot_nt(xb, wp[0:512, :])
        for j in range(4):
            qj = _rope(q[:, 128 * j:128 * (j + 1)], cos, sin)
            q_ref[j] = jnp.where(lo, qj, 0.0).astype(BF16)
            q_ref[4 + j] = jnp.where(lo, 0.0, qj).astype(BF16)
        kv = _dot_nt(xb, wt_ref[W_KV:W_ZA, :])
        k_ref[...] = _rope(kv[:, 0:128], cos, sin).astype(BF16)
        v_ref[...] = kv[:, 128:256].astype(BF16)
        za_ref[...] = _dot_nt(xb, wp[512:1024, :])
        u_val = _dot_nt(xb, wt_ref[W_U:W_ZS, :])
        for k in range(N_SLAB):
            u_ref[k] = u_val[:, k * SLAB_IN:(k + 1) * SLAB_IN]
        zs_ref[...] = _dot_nt(xb, wt_ref[W_ZS:D_IN_PROJ, :])
        if n_sh:
            @pl.when(step == steps - 1)
            def _():
                finish()
                for a in range(n_sh):
                    gathered_refs[a][...] = landing_refs[a][...]

    row = lambda w: pl.BlockSpec((tb, w), lambda i: (i, 0))
    table = lambda t: pl.BlockSpec(t.shape, lambda i: (0, 0, 0))
    vmem = pl.BlockSpec(memory_space=pltpu.VMEM)
    return pl.pallas_call(
        body, name="proj", grid=(steps,),
        in_specs=[row(D_MODEL), pl.BlockSpec((D_IN_PROJ, D_MODEL), lambda i: (0, 0), pipeline_mode=pl.Buffered(1)),
                  table(rope_hi), table(rope_lo)] + [vmem] * n_sh,
        out_specs=[pl.BlockSpec((8, tb, 128), lambda i: (0, i, 0)), row(128), row(128), row(512),
                   pl.BlockSpec((N_SLAB, tb, SLAB_IN), lambda i: (0, i, 0)), row(512)] + [vmem] * n_sh,
        out_shape=[jax.ShapeDtypeStruct((8, seq, 128), BF16), jax.ShapeDtypeStruct((seq, 128), BF16),
                   jax.ShapeDtypeStruct((seq, 128), BF16), jax.ShapeDtypeStruct((seq, 512), F32),
                   jax.ShapeDtypeStruct((N_SLAB, seq, SLAB_IN), F32), jax.ShapeDtypeStruct((seq, 512), F32)]
        + [jax.ShapeDtypeStruct((N_CHIPS,) + s.shape, BF16) for s in shards],
        scratch_shapes=[pltpu.VMEM((1024, D_MODEL), BF16)] + [pltpu.VMEM((N_CHIPS,) + s.shape, BF16) for s in shards]
        + (_gather_sems(n_sh) if n_sh else []),
        compiler_params=_cparams(("arbitrary",)),
    )(x, wt, rope_hi, rope_lo, *shards)


ATT_TQ = 128
ATT_KEYS = ATT_TQ + 2 * WINDOW


def _attn_window(i, seq):
    start = jnp.clip(i * ATT_TQ - WINDOW, 0, seq - ATT_KEYS)
    return pl.multiple_of(start, WINDOW)


def _attn_bias():
    r = np.arange(ATT_TQ)[None, :, None]
    c = np.arange(ATT_KEYS)[None, None, :]
    off = np.array([0, WINDOW, ATT_KEYS - ATT_TQ])[:, None, None]
    return jnp.asarray(np.where(np.abs(r + off - c) <= WINDOW, 0.0, NEG_INF).astype(np.float32))


def _attn_bias_spec(nblk):
    pick = lambda i: jnp.where(i == 0, 0, jnp.where(i == nblk - 1, 2, 1))
    return pl.BlockSpec((None, ATT_TQ, ATT_KEYS), lambda i: (pick(i), 0, 0))


def _attn_softmax(q_ref, k_ref, v_ref, sink_ref, bias_ref, start):
    kw = k_ref[pl.ds(start, ATT_KEYS), :]
    vw = v_ref[pl.ds(start, ATT_KEYS), :]
    qall = q_ref[...].reshape(N_Q_HEADS * ATT_TQ, 128)
    s = (_dot_nt(qall, kw) * (HEAD_DIM ** -0.5)).reshape(N_Q_HEADS, ATT_TQ, ATT_KEYS) + bias_ref[...][None]
    tiles = [s[:, :, 128 * t:128 * (t + 1)] for t in range(ATT_KEYS // 128)]
    m = jnp.max(functools.reduce(jnp.maximum, tiles), axis=2, keepdims=True)
    sink = sink_ref[...]
    m_b = jnp.maximum(jnp.broadcast_to(m, (N_Q_HEADS, ATT_TQ, 128)), sink)
    p = jnp.concatenate([jnp.exp(t - m_b) for t in tiles], axis=2)
    p_sink = jnp.exp(sink - m_b)
    lo_k = lax.broadcasted_iota(jnp.int32, (ATT_KEYS, 128), 1) < HEAD_DIM
    v_f = vw.astype(F32)
    v_lo, v_hi = jnp.where(lo_k, v_f, 1.0).astype(BF16), jnp.where(lo_k, 1.0, v_f).astype(BF16)
    pb = p.astype(BF16).reshape(N_Q_HEADS * ATT_TQ, ATT_KEYS)
    half = 4 * ATT_TQ
    r = jnp.concatenate([_dot(pb[:half], v_lo), _dot(pb[half:], v_hi)], axis=0).reshape(N_Q_HEADS, ATT_TQ, 128)
    return kw, vw, qall, p, p_sink, r


def _attn_fwd(q_stack, k, v, sink128, bias):
    seq = k.shape[0]

    def body(q_ref, k_ref, v_ref, sink_ref, bias_ref, o_ref):
        start = _attn_window(pl.program_id(0), seq)
        _, _, _, _, p_sink, r = _attn_softmax(q_ref, k_ref, v_ref, sink_ref, bias_ref, start)
        out = r / (pltpu.roll(r, HEAD_DIM, 2) + p_sink)
        lo = lax.broadcasted_iota(jnp.int32, (ATT_TQ, 128), 1) < HEAD_DIM
        for j in range(4):
            o_ref[:, 128 * j:128 * (j + 1)] = jnp.where(lo, out[j], out[4 + j])

    full = lambda w: pl.BlockSpec((seq, w), lambda i: (0, 0))
    return pl.pallas_call(
        body, name="attn_fwd", grid=(seq // ATT_TQ,),
        in_specs=[pl.BlockSpec((8, ATT_TQ, 128), lambda i: (0, i, 0)), full(128), full(128),
                  pl.BlockSpec((N_Q_HEADS, 1, 128), lambda i: (0, 0, 0)), _attn_bias_spec(seq // ATT_TQ)],
        out_specs=pl.BlockSpec((ATT_TQ, 512), lambda i: (i, 0)),
        out_shape=jax.ShapeDtypeStruct((seq, 512), F32),
        compiler_params=_cparams(("arbitrary",)),
    )(q_stack, k, v, sink128, bias)


def _attn_bwd(q_stack, k, v, sink128, bias, d_o, pieces):
    seq = k.shape[0]
    steps = seq // ATT_TQ
    n_p = len(pieces)

    def body(*refs):
        q_ref, k_ref, v_ref, sink_ref, bias_ref, do_ref = refs[:6]
        piece_refs = refs[6:6 + n_p]
        dq_ref, dk_ref, dv_ref, dsink_ref = refs[6 + n_p:10 + n_p]
        reduced_refs = refs[10 + n_p:10 + 2 * n_p]
        sink_acc = refs[10 + 2 * n_p]
        i = pl.program_id(0)
        if n_p:
            landing_refs = refs[11 + 2 * n_p:11 + 3 * n_p]
            scratch = refs[11 + 3 * n_p:]
            begin, exchange, combine, finish = _reduce_phases(
                piece_refs, landing_refs, scratch[:n_p], scratch[n_p:2 * n_p], scratch[2 * n_p:3 * n_p],
                *scratch[3 * n_p:], [True] * n_p, gather_last=False)
            pl.when(i == 0)(begin)
            pl.when(i == min(4, steps - 1))(exchange)
            pl.when(i == (3 * steps) // 4)(combine)

        @pl.when(i == 0)
        def _():
            dk_ref[...] = jnp.zeros_like(dk_ref)
            dv_ref[...] = jnp.zeros_like(dv_ref)
            sink_acc[...] = jnp.zeros_like(sink_acc)

        start = _attn_window(i, seq)
        kw, vw, qall, p, p_sink, r = _attn_softmax(q_ref, k_ref, v_ref, sink_ref, bias_ref, start)
        lo = lax.broadcasted_iota(jnp.int32, (ATT_TQ, 128), 1) < HEAD_DIM
        lo3 = lo[None]
        grp0 = lax.broadcasted_iota(jnp.int32, (N_Q_HEADS, ATT_TQ, 128), 0) < 4
        val = grp0 == lo3
        swapped = pltpu.roll(r, HEAD_DIM, 2)
        inv = 1.0 / (jnp.where(val, swapped, r) + p_sink)
        d_o_blk = do_ref[...]
        do3 = jnp.where(val, jnp.concatenate([d_o_blk[None, :, 128 * j:128 * (j + 1)] for j in range(4)] * 2, axis=0), 0.0)
        t = (do3 * r).reshape(N_Q_HEADS * ATT_TQ, 128)
        t_hi = t.astype(BF16)
        t_lo = (t - t_hi.astype(F32)).astype(BF16)
        ones = jnp.ones((128, 128), BF16)
        delta = (_dot(t_hi, ones) + _dot(t_lo, ones)).reshape(N_Q_HEADS, ATT_TQ, 128) * inv
        sink_acc[...] += -(p_sink * inv) * delta
        do_all = do3.astype(BF16).reshape(N_Q_HEADS * ATT_TQ, 128)
        dp = _dot_nt(do_all, vw).reshape(N_Q_HEADS, ATT_TQ, ATT_KEYS)
        probs, ds = [], []
        for tl in range(ATT_KEYS // 128):
            cols = slice(128 * tl, 128 * (tl + 1))
            probs_t = p[:, :, cols] * inv
            probs.append(probs_t.astype(BF16))
            ds.append((probs_t * (dp[:, :, cols] - delta)).astype(BF16))
        probs_all = jnp.concatenate(probs, axis=2).reshape(N_Q_HEADS * ATT_TQ, ATT_KEYS)
        ds_all = jnp.concatenate(ds, axis=2).reshape(N_Q_HEADS * ATT_TQ, ATT_KEYS)
        scale = HEAD_DIM ** -0.5
        dq_all = (_dot(ds_all, kw) * scale).reshape(N_Q_HEADS, ATT_TQ, 128)
        for j in range(4):
            dq_ref[:, 128 * j:128 * (j + 1)] = jnp.where(lo, dq_all[j], dq_all[4 + j])
        dk_ref[pl.ds(start, ATT_KEYS), :] += _dot_tn(ds_all, qall) * scale
        dv_ref[pl.ds(start, ATT_KEYS), :] += _dot_tn(probs_all, do_all)

        @pl.when(i == steps - 1)
        def _():
            dsink_ref[...] = jnp.sum(sink_acc[...], axis=1)

        if n_p:
            @pl.when(i == steps - 1)
            def _():
                finish()
                for a in range(n_p):
                    reduced_refs[a][...] = landing_refs[a][...]

    full = lambda w: pl.BlockSpec((seq, w), lambda i: (0, 0))
    vmem = pl.BlockSpec(memory_space=pltpu.VMEM)
    return pl.pallas_call(
        body, name="attn_bwd", grid=(steps,),
        in_specs=[pl.BlockSpec((8, ATT_TQ, 128), lambda i: (0, i, 0)), full(128), full(128),
                  pl.BlockSpec((N_Q_HEADS, 1, 128), lambda i: (0, 0, 0)),
                  _attn_bias_spec(steps), pl.BlockSpec((ATT_TQ, 512), lambda i: (i, 0))] + [vmem] * n_p,
        out_specs=[pl.BlockSpec((ATT_TQ, 512), lambda i: (i, 0)), full(128), full(128),
                   pl.BlockSpec((N_Q_HEADS, 128), lambda i: (0, 0))] + [vmem] * n_p,
        out_shape=[jax.ShapeDtypeStruct((seq, 512), F32), jax.ShapeDtypeStruct((seq, 128), F32),
                   jax.ShapeDtypeStruct((seq, 128), F32), jax.ShapeDtypeStruct((N_Q_HEADS, 128), F32)]
        + [jax.ShapeDtypeStruct(p.shape[1:], F32) for p in pieces],
        scratch_shapes=[pltpu.VMEM((N_Q_HEADS, ATT_TQ, 128), F32)] + [pltpu.VMEM(p.shape[1:], F32) for p in pieces]
        + (_reduce_scratch([p.shape for p in pieces], [True] * n_p) if n_p else []),
        compiler_params=_cparams(("arbitrary",)),
    )(q_stack, k, v, sink128, bias, d_o, *pieces)


def _permute_rows(dst_ref, src_ref, sub_len):
    for k in range(N_SLAB):
        for j in range(sub_len):
            dst_ref[k, 8 * j:8 * (j + 1), :] = src_ref.at[k][pl.ds(j, SUBSEG, stride=sub_len), :]


def _unpermute_rows(dst_ref, src_ref, sub_len):
    for k in range(N_SLAB):
        for s in range(SUBSEG):
            dst_ref[k, s * sub_len:(s + 1) * sub_len, :] = src_ref.at[k][pl.ds(s, sub_len, stride=SUBSEG), :]


def _scan_chunk(br_ref, bi_ref, lr_row, li_row, init, cols, *, sub_len, reverse, store):
    lr = jnp.broadcast_to(lr_row[:, cols], (SUBSEG, SCAN_LANES))
    li = jnp.broadcast_to(li_row[:, cols], (SUBSEG, SCAN_LANES))
    if init is None:
        sr = si = jnp.zeros((SUBSEG, SCAN_LANES), F32)
    else:
        sr, si = init
    for jj in range(sub_len):
        rows = slice(SUBSEG * ((sub_len - 1 - jj) if reverse else jj), SUBSEG * (((sub_len - 1 - jj) if reverse else jj) + 1))
        sr, si = lr * sr - li * si + br_ref[rows, cols], lr * si + li * sr + bi_ref[rows, cols]
        if store:
            br_ref[rows, cols] = sr
            bi_ref[rows, cols] = si
    return sr, si


def _resolve_chunk(z, carry_refs, start_refs, pr_row, pi_row, cols, *, reverse):
    cr, ci = carry_refs[0][0:1, cols], carry_refs[1][0:1, cols]
    pr, pi = pr_row[:, cols], pi_row[:, cols]
    for s in (range(SUBSEG - 1, -1, -1) if reverse else range(SUBSEG)):
        start_refs[0][s:s + 1, cols] = cr
        start_refs[1][s:s + 1, cols] = ci
        cr, ci = pr * cr - pi * ci + z[0][s:s + 1, :], pr * ci + pi * cr + z[1][s:s + 1, :]
    carry_refs[0][0:1, cols] = cr
    carry_refs[1][0:1, cols] = ci


def _param_specs(direction):
    row = lambda q: pl.BlockSpec((None, None, 1, STATE_W), lambda i: (q, direction, 0, 0))
    wide = lambda q: pl.BlockSpec((None, None, N_SLAB, SLAB_IN, SLAB_ST), lambda i: (q, direction, 0, 0, 0))
    tall = lambda q: pl.BlockSpec((None, None, N_SLAB, SLAB_ST, SLAB_IN), lambda i: (q, direction, 0, 0, 0))
    return [row(q) for q in range(4)], [wide(0), wide(1)], [tall(0), tall(1)]


def _ssm_fwd(u, lam, bb, cb, *, direction, tb, name):
    reverse = direction == 1
    seq = u.shape[1]
    nblk = seq // tb
    sub_len = tb // SUBSEG

    def body(u_ref, lr_ref, li_ref, pr_ref, pi_ref, bbr_ref, bbi_ref, cbr_ref, cbi_ref,
             y_ref, sr_ref, si_ref, xr, xi, up, yp, car, cai):
        @pl.when(pl.program_id(0) == 0)
        def _():
            car[...] = jnp.zeros_like(car)
            cai[...] = jnp.zeros_like(cai)

        _permute_rows(up, u_ref, sub_len)
        lr, li, pr, pi = lr_ref[...], li_ref[...], pr_ref[...], pi_ref[...]
        chunk = lambda k: slice(k * SLAB_ST, (k + 1) * SLAB_ST)

        def drive(k):
            ub = up[k].astype(BF16)
            xr[:, chunk(k)] = _dot(ub, bbr_ref[k])
            xi[:, chunk(k)] = _dot(ub, bbi_ref[k])

        def scan(k):
            z = _scan_chunk(xr, xi, lr, li, None, chunk(k), sub_len=sub_len, reverse=reverse, store=False)
            _resolve_chunk(z, (car, cai), (sr_ref, si_ref), pr, pi, chunk(k), reverse=reverse)
            _scan_chunk(xr, xi, lr, li, (sr_ref[:, chunk(k)], si_ref[:, chunk(k)]), chunk(k),
                        sub_len=sub_len, reverse=reverse, store=True)

        def read_out(k):
            yp[k] = _dot(xr[:, chunk(k)].astype(BF16), cbr_ref[k]) - _dot(xi[:, chunk(k)].astype(BF16), cbi_ref[k])

        drive(0)
        for k in range(N_SLAB):
            if k + 1 < N_SLAB:
                drive(k + 1)
            scan(k)
            if k > 0:
                read_out(k - 1)
        read_out(N_SLAB - 1)
        _unpermute_rows(y_ref, yp, sub_len)

    blk = (lambda i: nblk - 1 - i) if reverse else (lambda i: i)
    rows, wide, tall = _param_specs(direction)
    tok = pl.BlockSpec((N_SLAB, tb, SLAB_IN), lambda i: (0, blk(i), 0))
    start_spec = pl.BlockSpec((None, SUBSEG, STATE_W), lambda i: (blk(i), 0, 0))
    return pl.pallas_call(
        body, name=name, grid=(nblk,),
        in_specs=[tok] + rows + wide + tall,
        out_specs=[tok, start_spec, start_spec],
        out_shape=[jax.ShapeDtypeStruct((N_SLAB, seq, SLAB_IN), F32), jax.ShapeDtypeStruct((nblk, SUBSEG, STATE_W), F32),
                   jax.ShapeDtypeStruct((nblk, SUBSEG, STATE_W), F32)],
        scratch_shapes=[pltpu.VMEM((tb, STATE_W), F32), pltpu.VMEM((tb, STATE_W), F32),
                        pltpu.VMEM((N_SLAB, tb, SLAB_IN), F32), pltpu.VMEM((N_SLAB, tb, SLAB_IN), F32),
                        pltpu.VMEM((SUBSEG, STATE_W), F32), pltpu.VMEM((SUBSEG, STATE_W), F32)],
        compiler_params=_cparams(("arbitrary",)),
    )(u, lam, lam, lam, lam, bb, bb, cb, cb)


def _ssm_bwd(u, dy, starts, lam, bb, bbt, cb_t, *, direction, tb, name):
    reverse = direction == 1
    seq = u.shape[1]
    nblk = seq // tb
    sub_len = tb // SUBSEG

    def body(u_ref, dy_ref, sr_ref, si_ref, lr_ref, li_ref, pr_ref, pi_ref, bbr_ref, bbi_ref, btr_ref, bti_ref,
             ctr_ref, cti_ref, du_ref, gb_ref, gc_ref, dl_ref,
             xr, xi, gr, gi, up, dyp, dup, gsr, gsi, car, cai):
        gbr_ref, gbi_ref = gb_ref.at[0], gb_ref.at[1]
        gcr_ref, gci_ref = gc_ref.at[0], gc_ref.at[1]
        dlr_ref, dli_ref = dl_ref.at[0], dl_ref.at[1]

        @pl.when(pl.program_id(0) == 0)
        def _():
            for ref in (car, cai, gbr_ref, gbi_ref, gcr_ref, gci_ref, dlr_ref, dli_ref):
                ref[...] = jnp.zeros_like(ref)

        _permute_rows(up, u_ref, sub_len)
        _permute_rows(dyp, dy_ref, sub_len)
        lr, li, pr, pi = lr_ref[...], li_ref[...], pr_ref[...], pi_ref[...]
        nli, npi = -li, -pi
        chunk = lambda k: slice(k * SLAB_ST, (k + 1) * SLAB_ST)

        def drive(k):
            ub = up[k].astype(BF16)
            xr[:, chunk(k)] = _dot(ub, bbr_ref[k])
            xi[:, chunk(k)] = _dot(ub, bbi_ref[k])
            dyb = dyp[k].astype(BF16)
            gr[:, chunk(k)] = _dot(dyb, ctr_ref[k])
            gi[:, chunk(k)] = -_dot(dyb, cti_ref[k])

        def scan_x(k):
            _scan_chunk(xr, xi, lr, li, (sr_ref[:, chunk(k)], si_ref[:, chunk(k)]), chunk(k),
                        sub_len=sub_len, reverse=reverse, store=True)

        def grad_c(k):
            dyb = dyp[k].astype(BF16)
            gcr_ref[k] += _dot_tn(dyb, xr[:, chunk(k)].astype(BF16))
            gci_ref[k] -= _dot_tn(dyb, xi[:, chunk(k)].astype(BF16))

        def scan_g(k):
            z = _scan_chunk(gr, gi, lr, nli, None, chunk(k), sub_len=sub_len, reverse=not reverse, store=False)
            _resolve_chunk(z, (car, cai), (gsr, gsi), pr, npi, chunk(k), reverse=not reverse)
            _scan_chunk(gr, gi, lr, nli, (gsr[:, chunk(k)], gsi[:, chunk(k)]), chunk(k),
                        sub_len=sub_len, reverse=not reverse, store=True)

        def grad_b_du(k):
            ub = up[k].astype(BF16)
            grb, gib = gr[:, chunk(k)].astype(BF16), gi[:, chunk(k)].astype(BF16)
            gbr_ref[k] += _dot_tn(ub, grb)
            gbi_ref[k] += _dot_tn(ub, gib)
            dup[k] = _dot(grb, btr_ref[k]) + _dot(gib, bti_ref[k])

        def grad_lambda(k):
            cols = chunk(k)
            acc_r, acc_i = dlr_ref[:, cols], dli_ref[:, cols]
            for jj in range(sub_len):
                prev = jj + 1 if reverse else jj - 1
                if 0 <= prev < sub_len:
                    x_r, x_i = xr[SUBSEG * prev:SUBSEG * (prev + 1), cols], xi[SUBSEG * prev:SUBSEG * (prev + 1), cols]
                else:
                    x_r, x_i = sr_ref[:, cols], si_ref[:, cols]
                g_r, g_i = gr[SUBSEG * jj:SUBSEG * (jj + 1), cols], gi[SUBSEG * jj:SUBSEG * (jj + 1), cols]
                acc_r = acc_r + (g_r * x_r + g_i * x_i)
                acc_i = acc_i + (g_i * x_r - g_r * x_i)
            dlr_ref[:, cols] = acc_r
            dli_ref[:, cols] = acc_i

        drive(0)
        for k in range(N_SLAB):
            if k + 1 < N_SLAB:
                drive(k + 1)
            scan_x(k)
            grad_c(k)
            scan_g(k)
            grad_b_du(k)
            grad_lambda(k)
        _unpermute_rows(du_ref, dup, sub_len)

    blk = (lambda i: i) if reverse else (lambda i: nblk - 1 - i)
    rows, wide, tall = _param_specs(direction)
    tok = pl.BlockSpec((N_SLAB, tb, SLAB_IN), lambda i: (0, blk(i), 0))
    start_spec = pl.BlockSpec((None, SUBSEG, STATE_W), lambda i: (blk(i), 0, 0))
    gb_shape, dl_shape = (2, N_SLAB, SLAB_IN, SLAB_ST), (2, SUBSEG, STATE_W)
    whole = lambda shape: pl.BlockSpec(shape, lambda i: (0,) * len(shape))
    big = lambda: pltpu.VMEM((tb, STATE_W), F32)
    slabs = lambda: pltpu.VMEM((N_SLAB, tb, SLAB_IN), F32)
    tile = lambda: pltpu.VMEM((SUBSEG, STATE_W), F32)
    return pl.pallas_call(
        body, name=name, grid=(nblk,),
        in_specs=[tok, tok, start_spec, start_spec] + rows + wide + tall + wide,
        out_specs=[tok, whole(gb_shape), whole(gb_shape), whole(dl_shape)],
        out_shape=[jax.ShapeDtypeStruct((N_SLAB, seq, SLAB_IN), F32), jax.ShapeDtypeStruct(gb_shape, F32),
                   jax.ShapeDtypeStruct(gb_shape, F32), jax.ShapeDtypeStruct(dl_shape, F32)],
        scratch_shapes=[big(), big(), big(), big(), slabs(), slabs(), slabs(), tile(), tile(), tile(), tile()],
        compiler_params=_cparams(("arbitrary",)),
    )(u, dy, *starts, lam, lam, lam, lam, bb, bb, bbt, bbt, cb_t, cb_t)


GELU_C = math.sqrt(2.0 / math.pi)
GELU_K = 0.044715


def _mid(o, za, u, y_f, y_b, zs, x, target, ssm_d, w_glu, b_glu, g_attn, g_ssm, w_out, ln_g, ln_b, tb):
    seq = x.shape[0]

    def body(o_ref, za_ref, u_ref, yf_ref, yb_ref, zs_ref, x_ref, t_ref, d_ref, wg_ref, bg_ref, ga_ref, gs_ref,
             wo_ref, lg_ref, lb_ref,
             loss_ref, do_ref, dza_ref, dyl_ref, dzs_ref, dpre_ref, gwo_ref, gwg_ref, vec_ref, wop):
        @pl.when(pl.program_id(0) == 0)
        def _():
            for ref in (loss_ref, gwo_ref, gwg_ref, vec_ref):
                ref[...] = jnp.zeros_like(ref)
            for nat, par in _pair_blocks(0):
                wop[par, :] = wo_ref[nat, :]
            wop[D_ATTN:, :] = wo_ref[D_ATTN:, :]

        def rows_of(rs):
            o, za = o_ref[rs, :], za_ref[rs, :]
            sig_a = _sigmoid(za)
            silu_a = za * sig_a
            ya = o * silu_a
            r_a = lax.rsqrt(jnp.mean(ya * ya, axis=1, keepdims=True) + NORM_EPS)
            n_a = ya * r_a
            g_a = ga_ref[...]
            unslab = lambda ref: jnp.concatenate([ref[k, rs, :] for k in range(N_SLAB)], axis=1)
            u_blk, zs = unslab(u_ref), zs_ref[rs, :]
            d_row = d_ref[...]
            ylin = d_row * u_blk + unslab(yf_ref) + unslab(yb_ref)
            inner = GELU_C * (ylin + GELU_K * ylin * ylin * ylin)
            th = jnp.tanh(inner)
            gl = 0.5 * ylin * (1.0 + th)
            glb = gl.astype(BF16)
            gate = _dot(glb, wg_ref[...])
            sg = _sigmoid(gate + bg_ref[...])
            y2 = gl * sg
            sig_s = _sigmoid(zs)
            silu_s = zs * sig_s
            ys = y2 * silu_s
            r_s = lax.rsqrt(jnp.mean(ys * ys, axis=1, keepdims=True) + NORM_EPS)
            n_s = ys * r_s
            g_s = gs_ref[...]
            mixed = jnp.concatenate([n_a * g_a, n_s * g_s], axis=1).astype(BF16)
            out = _dot(mixed, wop[...])
            pre = ALPHA * x_ref[rs, :] + out
            mu = jnp.mean(pre, axis=1, keepdims=True)
            cen = pre - mu
            rstd = lax.rsqrt(jnp.mean(cen * cen, axis=1, keepdims=True) + NORM_EPS)
            hhat = cen * rstd
            ln_g = lg_ref[...]
            err = hhat * ln_g + lb_ref[...] - t_ref[rs, :]
            loss_ref[...] += 0.5 * jnp.sum(jnp.mean(err * err, axis=1, keepdims=True))

            dh = err * (1.0 / D_MODEL)
            vec_ref[0:1, :] += jnp.sum(dh * hhat, axis=0, keepdims=True)
            vec_ref[1:2, :] += jnp.sum(dh, axis=0, keepdims=True)
            dhh = dh * ln_g
            dpre = rstd * (dhh - jnp.mean(dhh, axis=1, keepdims=True)
                           - hhat * jnp.mean(dhh * hhat, axis=1, keepdims=True))
            dpre_ref[rs, :] = dpre
            dpb = dpre.astype(BF16)
            for j in range(4):
                g_pair = _dot_tn(mixed[:, 128 * j:128 * (j + 1)], dpb)
                for g in range(2):
                    nat = HEAD_DIM * (4 * g + j)
                    gwo_ref[nat:nat + HEAD_DIM, :] += g_pair[HEAD_DIM * g:HEAD_DIM * (g + 1), :]
            gwo_ref[D_ATTN:, :] += _dot_tn(mixed[:, D_ATTN:], dpb)
            dmix = _dot_nt(dpb, wop[...])
            dna = dmix[:, :D_ATTN]
            vec_ref[2:3, 0:D_ATTN] += jnp.sum(dna * n_a, axis=0, keepdims=True)
            dna = dna * g_a
            dya = r_a * (dna - n_a * jnp.mean(dna * n_a, axis=1, keepdims=True))
            do_ref[rs, :] = dya * silu_a
            dza_ref[rs, :] = dya * o * (sig_a * (1.0 + za * (1.0 - sig_a)))
            dns = dmix[:, D_ATTN:]
            vec_ref[2:3, D_ATTN:] += jnp.sum(dns * n_s, axis=0, keepdims=True)
            dns = dns * g_s
            dys = r_s * (dns - n_s * jnp.mean(dns * n_s, axis=1, keepdims=True))
            dzs_ref[rs, :] = dys * y2 * (sig_s * (1.0 + zs * (1.0 - sig_s)))
            dy2 = dys * silu_s
            da = dy2 * gl * sg * (1.0 - sg)
            vec_ref[3:4, D_SSM:] += jnp.sum(da, axis=0, keepdims=True)
            dab = da.astype(BF16)
            gwg_ref[...] += _dot_tn(glb, dab)
            dgl_mm = _dot_nt(dab, wg_ref[...])
            dgl = dy2 * sg + dgl_mm
            dylin = dgl * (0.5 * (1.0 + th)
                           + 0.5 * ylin * (1.0 - th * th) * GELU_C * (1.0 + 3.0 * GELU_K * ylin * ylin))
            for k in range(N_SLAB):
                dyl_ref[k, rs, :] = dylin[:, k * SLAB_IN:(k + 1) * SLAB_IN]
            vec_ref[3:4, 0:D_SSM] += jnp.sum(dylin * u_blk, axis=0, keepdims=True)

        rows_of(slice(0, tb))

    tok = lambda w: pl.BlockSpec((tb, w), lambda i: (i, 0))
    slab = pl.BlockSpec((N_SLAB, tb, SLAB_IN), lambda i: (0, i, 0))
    const = lambda r, c: pl.BlockSpec((r, c), lambda i: (0, 0), pipeline_mode=pl.Buffered(1))
    tok_shape = jax.ShapeDtypeStruct((seq, 512), F32)
    return pl.pallas_call(
        body, name="mid", grid=(seq // tb,),
        in_specs=[tok(512), tok(512), slab, slab, slab, tok(512), tok(1024), tok(1024),
                  const(1, 512), const(512, 512), const(1, 512), const(1, 512), const(1, 512),
                  const(1024, 1024), const(1, 1024), const(1, 1024)],
        out_specs=[const(8, 128), tok(512), tok(512), slab, tok(512), tok(1024),
                   const(1024, 1024), const(512, 512), const(8, 1024)],
        out_shape=[jax.ShapeDtypeStruct((8, 128), F32), tok_shape, tok_shape,
                   jax.ShapeDtypeStruct((N_SLAB, seq, SLAB_IN), F32), tok_shape,
                   jax.ShapeDtypeStruct((seq, 1024), F32), jax.ShapeDtypeStruct((1024, 1024), F32),
                   jax.ShapeDtypeStruct((512, 512), F32), jax.ShapeDtypeStruct((8, 1024), F32)],
        scratch_shapes=[pltpu.VMEM((D_MODEL, D_MODEL), BF16)],
        compiler_params=_cparams(("arbitrary",)),
    )(o, za, u, y_f, y_b, zs, x, target, ssm_d, w_glu, b_glu, g_attn, g_ssm, w_out, ln_g, ln_b)


def _ride_shapes(pieces, narrow, gather_last):
    outs = [p.shape if (gather_last and a == len(pieces) - 1) else p.shape[1:] for a, p in enumerate(pieces)]
    return outs, [pltpu.VMEM(s, F32) for s in outs] + _reduce_scratch([p.shape for p in pieces], narrow)


def _ride_phases(piece_refs, out_refs, scratch_refs, narrow, gather_last):
    n = len(piece_refs)
    landing, rest = scratch_refs[:n], scratch_refs[n:]
    begin, exchange, combine, finish = _reduce_phases(piece_refs, landing, rest[:n], rest[n:2 * n], rest[2 * n:3 * n],
                                                      *rest[3 * n:], narrow, gather_last)

    def end():
        finish()
        for a in range(n):
            out_refs[a][...] = landing[a][...]

    return begin, exchange, combine, end


def _dproj_block(dq_ref, dk_ref, dv_ref, dza_ref, duf_ref, dub_ref, dyl_ref, dzs_ref, d_ref, hi_ref, lo_ref, tb):
    cos, sin = _rope_block(hi_ref, lo_ref, pl.program_id(0) * (tb // ROPE_GROUP), tb // ROPE_GROUP)
    lo = lax.broadcasted_iota(jnp.int32, (tb, 128), 1) < HEAD_DIM

    def unrope(t):
        return t * cos + _rotate_half_unsigned(t * sin)

    def natural(pairs):
        swapped = [pltpu.roll(t, HEAD_DIM, 1) for t in pairs]
        return [jnp.where(lo, pairs[0], swapped[1]), jnp.where(lo, pairs[2], swapped[3]),
                jnp.where(lo, swapped[0], pairs[1]), jnp.where(lo, swapped[2], pairs[3])]

    dq_rot, dza = dq_ref[...], dza_ref[...]
    pieces = natural([unrope(dq_rot[:, 128 * j:128 * (j + 1)]) for j in range(4)])
    d_row = d_ref[...]
    pieces += [unrope(dk_ref[...]), dv_ref[...]] + natural([dza[:, 128 * j:128 * (j + 1)] for j in range(4)])
    pieces += [duf_ref[k] + dub_ref[k] + d_row[:, k * SLAB_IN:(k + 1) * SLAB_IN] * dyl_ref[k] for k in range(N_SLAB)]
    pieces += [dzs_ref[...]]
    return jnp.concatenate(pieces, axis=1).astype(BF16)


def _dproj_specs(tb, rope_hi, rope_lo):
    tok = lambda w: pl.BlockSpec((tb, w), lambda i: (i, 0))
    slab = pl.BlockSpec((N_SLAB, tb, SLAB_IN), lambda i: (0, i, 0))
    table = lambda t: pl.BlockSpec(t.shape, lambda i: (0, 0, 0))
    return [tok(512), tok(128), tok(128), tok(512), slab, slab, slab, tok(512), pl.BlockSpec((1, 512), lambda i: (0, 0)),
            table(rope_hi), table(rope_lo)]


N_DPROJ = 11
GW_ROWS = 768


def _proj_bwd_w(x, dproj_args, rope_hi, rope_lo, pieces, tb):
    seq = x.shape[0]
    steps = seq // tb
    n_p = len(pieces)
    narrow = [False] * n_p

    def body(*refs):
        x_ref, grads = refs[0], refs[1:1 + N_DPROJ]
        piece_refs = refs[1 + N_DPROJ:1 + N_DPROJ + n_p]
        gw_ref = refs[1 + N_DPROJ + n_p]
        out_refs = refs[2 + N_DPROJ + n_p:2 + N_DPROJ + 2 * n_p]
        step = pl.program_id(0)
        if n_p:
            begin, exchange, combine, end = _ride_phases(piece_refs, out_refs, refs[2 + N_DPROJ + 2 * n_p:], narrow, True)
            pl.when(step == 0)(begin)
            pl.when(step == min(1, steps - 1))(exchange)
            pl.when(step == steps // 2)(combine)

        @pl.when(step == 0)
        def _():
            gw_ref[...] = jnp.zeros_like(gw_ref)

        dproj = _dproj_block(*grads, tb)
        xb = x_ref[...].astype(BF16)
        for r0 in range(0, D_IN_PROJ, GW_ROWS):
            gw_ref[r0:r0 + GW_ROWS, :] += _dot_tn(dproj[:, r0:r0 + GW_ROWS], xb)
        if n_p:
            pl.when(step == steps - 1)(end)

    vmem = pl.BlockSpec(memory_space=pltpu.VMEM)
    whole = pl.BlockSpec((D_IN_PROJ, D_MODEL), lambda i: (0, 0), pipeline_mode=pl.Buffered(1))
    ride_outs, ride_scratch = _ride_shapes(pieces, narrow, True) if n_p else ([], [])
    return pl.pallas_call(
        body, name="proj_bwd_w", grid=(steps,),
        in_specs=[pl.BlockSpec((tb, D_MODEL), lambda i: (i, 0))] + _dproj_specs(tb, rope_hi, rope_lo) + [vmem] * n_p,
        out_specs=[whole] + [vmem] * n_p,
        out_shape=[jax.ShapeDtypeStruct((D_IN_PROJ, D_MODEL), F32)] + [jax.ShapeDtypeStruct(s, F32) for s in ride_outs],
        scratch_shapes=ride_scratch,
        compiler_params=_cparams(("arbitrary",)),
    )(x, *dproj_args, rope_hi, rope_lo, *pieces)


def _proj_bwd_x(dproj_args, rope_hi, rope_lo, dpre, wt, pieces, tb):
    seq = dpre.shape[0]
    steps = seq // tb
    n_p = len(pieces)
    narrow = [True] * n_p

    def body(*refs):
        grads = refs[:N_DPROJ]
        dpre_ref, wt_ref = refs[N_DPROJ:N_DPROJ + 2]
        piece_refs = refs[N_DPROJ + 2:N_DPROJ + 2 + n_p]
        gx_ref = refs[N_DPROJ + 2 + n_p]
        out_refs = refs[N_DPROJ + 3 + n_p:N_DPROJ + 3 + 2 * n_p]
        step = pl.program_id(0)
        if n_p:
            begin, exchange, combine, end = _ride_phases(piece_refs, out_refs, refs[N_DPROJ + 3 + 2 * n_p:], narrow, False)
            pl.when(step == 0)(begin)
            pl.when(step == min(1, steps - 1))(exchange)
            pl.when(step == steps - 1)(combine)

        dproj = _dproj_block(*grads, tb)
        gx_ref[...] = ALPHA * dpre_ref[...] + _dot(dproj, wt_ref[...])
        if n_p:
            pl.when(step == steps - 1)(end)

    vmem = pl.BlockSpec(memory_space=pltpu.VMEM)
    whole = pl.BlockSpec((D_IN_PROJ, D_MODEL), lambda i: (0, 0), pipeline_mode=pl.Buffered(1))
    ride_outs, ride_scratch = _ride_shapes(pieces, narrow, False) if n_p else ([], [])
    return pl.pallas_call(
        body, name="proj_bwd_x", grid=(steps,),
        in_specs=_dproj_specs(tb, rope_hi, rope_lo) + [pl.BlockSpec((tb, D_MODEL), lambda i: (i, 0)), whole] + [vmem] * n_p,
        out_specs=[pl.BlockSpec((tb, D_MODEL), lambda i: (i, 0))] + [vmem] * n_p,
        out_shape=[jax.ShapeDtypeStruct((seq, D_MODEL), F32)] + [jax.ShapeDtypeStruct(s, F32) for s in ride_outs],
        scratch_shapes=ride_scratch,
        compiler_params=_cparams(("arbitrary",)),
    )(*dproj_args, rope_hi, rope_lo, dpre, wt, *pieces)


def _adamw(w, g, m, v, name):
    rows, cols = w.shape
    tb = rows
    while tb * cols * 4 > ADAMW_BLOCK_BYTES and tb % 16 == 0:
        tb //= 2

    def body(w_ref, g_ref, m_ref, v_ref, d_ref, nm_ref, nv_ref):
        _adamw_update(w_ref, g_ref, m_ref, v_ref, d_ref, nm_ref, nv_ref)

    spec = pl.BlockSpec((tb, cols), lambda i: (i, 0))
    return pl.pallas_call(
        body, name=name, grid=(rows // tb,), in_specs=[spec] * 4, out_specs=[spec] * 3,
        out_shape=[jax.ShapeDtypeStruct((rows, cols), F32)] * 3,
        compiler_params=_cparams(("arbitrary",)),
    )(w, g, m, v)


def _adamw_update(w_ref, g_ref, m_ref, v_ref, d_ref, nm_ref, nv_ref):
    g_blk = g_ref[...]
    m_new = ADAM_B1 * m_ref[...] + (1.0 - ADAM_B1) * g_blk
    v_new = ADAM_B2 * v_ref[...] + (1.0 - ADAM_B2) * (g_blk * g_blk)
    m_hat = m_new / (1.0 - ADAM_B1 ** ADAM_STEP)
    v_hat = v_new / (1.0 - ADAM_B2 ** ADAM_STEP)
    d_ref[...] = -ADAM_LR * (m_hat / (jnp.sqrt(v_hat) + ADAM_EPS) + ADAM_WD * w_ref[...])
    nm_ref[...] = m_new
    nv_ref[...] = v_new


def _adamw_many(groups, name):
    n = len(groups)

    def body(*refs):
        for p in range(n):
            w_ref, g_ref, m_ref, v_ref = refs[4 * p:4 * p + 4]
            gn_ref, d_ref, nm_ref, nv_ref = refs[4 * n + 4 * p:4 * n + 4 * p + 4]
            gn_ref[...] = g_ref[...].reshape(w_ref.shape)
            _adamw_update(w_ref, gn_ref, m_ref, v_ref, d_ref, nm_ref, nv_ref)

    return pl.pallas_call(
        body, name=name,
        out_shape=[jax.ShapeDtypeStruct(grp[0].shape, F32) for grp in groups for _ in range(4)],
    )(*[a for grp in groups for a in grp])


_WEIGHTS = ["w_in", "attn_sink", "ssm_a_re", "ssm_a_im", "ssm_log_dt", "ssm_b_re", "ssm_b_im", "ssm_c_re", "ssm_c_im",
            "ssm_d", "w_glu", "b_glu", "norm_attn_g", "norm_ssm_g", "w_out", "ln_g", "ln_b"]
N_DG = N_DIR * N_GROUPS
BIG_ROWS = N_DG * SSM_CH * SSM_STATE // 128
TINY_ROWS = 64


def _pack_small_grads(g_bc, g_vec, g_ar, g_ai, g_dt, g_sink, loss):
    big = jnp.stack([t.reshape(BIG_ROWS, 128) for t in g_bc])
    row = lambda t: jnp.pad(t.reshape(1, -1), ((0, 0), (0, 128 - t.size)))
    tiny = jnp.concatenate([g_vec.reshape(64, 128), g_ar.reshape(32, 128), g_ai.reshape(32, 128), row(g_dt), row(g_sink),
                            row(loss), jnp.zeros((N_CHIPS * TINY_ROWS - 131, 128), F32)], axis=0)
    return jnp.concatenate([big, tiny.reshape(N_CHIPS, TINY_ROWS, 128)], axis=1)


def _unpack_small_grads(packed):
    big = packed[:, :BIG_ROWS].reshape(N_CHIPS, 2 * BIG_ROWS, SSM_STATE)
    tiny = packed[:, BIG_ROWS:].reshape(N_CHIPS * TINY_ROWS, 128)
    g_vec = tiny[0:64].reshape(8, 1024)
    return tiny[130, 0], {
        "ssm_b_re": big[0], "ssm_b_im": big[1], "ssm_c_re": big[2], "ssm_c_im": big[3],
        "ln_g": g_vec[0:1], "ln_b": g_vec[1:2],
        "norm_attn_g": _from_pair_order(g_vec[2:3, :D_ATTN]), "norm_ssm_g": g_vec[2:3, D_ATTN:],
        "ssm_d": g_vec[3:4, :D_SSM], "b_glu": g_vec[3:4, D_SSM:],
        "ssm_a_re": tiny[64:96].reshape(N_DG, SSM_STATE), "ssm_a_im": tiny[96:128].reshape(N_DG, SSM_STATE),
        "ssm_log_dt": tiny[128:129, :N_DG].reshape(N_DIR, N_GROUPS), "attn_sink": tiny[129:130, :N_Q_HEADS],
    }


def _small_unview(name, t, shape):
    if name in ("ssm_b_re", "ssm_b_im"):
        return jnp.swapaxes(t.reshape(N_DIR, N_GROUPS, SSM_CH, SSM_STATE), 2, 3).reshape(shape)
    return t.reshape(shape)


def _channel_major(name, t):
    return jnp.swapaxes(t, 3, 4) if name in ("ssm_b_re", "ssm_b_im") else t


def kernel(x, w_in, attn_sink, ssm_a_re, ssm_a_im, ssm_log_dt, ssm_b_re, ssm_b_im, ssm_c_re, ssm_c_im, ssm_d, w_glu, b_glu, norm_attn_g, norm_ssm_g, w_out, ln_g, ln_b, loss_target, m_w_in, m_attn_sink, m_ssm_a_re, m_ssm_a_im, m_ssm_log_dt, m_ssm_b_re, m_ssm_b_im, m_ssm_c_re, m_ssm_c_im, m_ssm_d, m_w_glu, m_b_glu, m_norm_attn_g, m_norm_ssm_g, m_w_out, m_ln_g, m_ln_b, v_w_in, v_attn_sink, v_ssm_a_re, v_ssm_a_im, v_ssm_log_dt, v_ssm_b_re, v_ssm_b_im, v_ssm_c_re, v_ssm_c_im, v_ssm_d, v_w_glu, v_b_glu, v_norm_attn_g, v_norm_ssm_g, v_w_out, v_ln_g, v_ln_b):
    args = dict(locals())
    weights = {n: args[n] for n in _WEIGHTS}
    mom_m = {n: args["m_" + n] for n in _WEIGHTS}
    mom_v = {n: args["v_" + n] for n in _WEIGHTS}
    xs = x[0]
    target = loss_target[0]

    (wt_g,) = _all_gather_chips([w_in[0].T], BF16, "gather_weights")
    wt_full = wt_g.reshape(D_IN_PROJ, D_MODEL)

    g_x, r_wt, r_w_out, r_w_glu, g_small_all = _local_step(
        xs, target, wt_full, w_glu[0], w_out[0], attn_sink, ssm_a_re, ssm_a_im, ssm_log_dt, ssm_b_re, ssm_b_im,
        ssm_c_re, ssm_c_im, ssm_d, b_glu, norm_attn_g, norm_ssm_g, ln_g, ln_b, sharded=True)
    loss, small_grads = _unpack_small_grads(g_small_all)

    grads, deltas, new_m, new_v = {}, {}, {}, {}
    d_w, m_w, v_w = _adamw(w_in[0].T, r_wt, m_w_in[0].T, v_w_in[0].T, "adamw_w_in")
    grads["w_in"], deltas["w_in"], new_m["w_in"], new_v["w_in"] = r_wt.T[None], d_w.T[None], m_w.T[None], v_w.T[None]
    for n, g in (("w_out", r_w_out), ("w_glu", r_w_glu)):
        d_w, m_w, v_w = _adamw(weights[n][0], g, mom_m[n][0], mom_v[n][0], "adamw_" + n)
        grads[n], deltas[n], new_m[n], new_v[n] = g[None], d_w[None], m_w[None], v_w[None]
    names = sorted(small_grads)
    updates = _adamw_many([(_channel_major(n, weights[n]), small_grads[n], _channel_major(n, mom_m[n]),
                            _channel_major(n, mom_v[n])) for n in names], "adamw_small")
    for i, n in enumerate(names):
        grads[n], deltas[n], new_m[n], new_v[n] = (_channel_major(n, t) for t in updates[4 * i:4 * i + 4])

    return (loss, g_x[None], *[grads[n] for n in _WEIGHTS], *[deltas[n] for n in _WEIGHTS],
            *[new_m[n] for n in _WEIGHTS], *[new_v[n] for n in _WEIGHTS])


def _local_step(xs, target, wt_full, w_glu_in, w_out_in, attn_sink, ssm_a_re, ssm_a_im, ssm_log_dt, ssm_b_re,
                ssm_b_im, ssm_c_re, ssm_c_im, ssm_d, b_glu, norm_attn_g, norm_ssm_g, ln_g, ln_b, sharded):
    seq = xs.shape[0]

    a_r, a_i = ssm_a_re, ssm_a_im
    log_dt = ssm_log_dt.reshape(N_DG, 1)
    b_r, b_i = _channel_major("ssm_b_re", ssm_b_re), _channel_major("ssm_b_im", ssm_b_im)
    c_r, c_i = ssm_c_re, ssm_c_im
    ssm_tb = min(SSM_BLOCK, seq)
    sub_len = ssm_tb // SUBSEG
    lam, bb, bbt, cb, cb_t = _ssm_params_fwd(a_r, a_i, log_dt, b_r, b_i, c_r, c_i, int(math.log2(sub_len)))
    lam = lam.reshape(4, N_DIR, 1, STATE_W)

    rope_hi, rope_lo = _rope_tables(seq)
    projected = _proj(xs, wt_full, rope_hi, rope_lo, [w_glu_in, w_out_in] if sharded else [], min(512, seq))
    q_stack, k_rot, v_bf, z_attn, u, z_ssm = projected[:6]
    if sharded:
        w_glu_full, w_out_full = projected[6].reshape(D_SSM, D_SSM), projected[7].reshape(D_MODEL, D_MODEL)
    else:
        w_glu_full, w_out_full = w_glu_in, w_out_in
    sink128 = jnp.broadcast_to(attn_sink[0][:, None, None], (N_Q_HEADS, 1, 128))
    attn_bias = _attn_bias()
    o = _attn_fwd(q_stack, k_rot, v_bf, sink128, attn_bias)
    ys, starts = [], []
    for d in range(N_DIR):
        y_d, s_r, s_i = _ssm_fwd(u, lam, bb, cb, direction=d, tb=ssm_tb, name=f"ssm_fwd_{d}")
        ys.append(y_d)
        starts.append((s_r, s_i))

    row = lambda t: t.reshape(1, -1)
    g_attn_p = _to_pair_order(norm_attn_g)
    loss_blk, d_o, d_za, d_ylin, d_zs, d_pre, g_w_out, g_w_glu, g_vec = _mid(
        o, z_attn, u, ys[0], ys[1], z_ssm, xs, target, row(ssm_d), w_glu_full, row(b_glu),
        g_attn_p, row(norm_ssm_g), w_out_full, row(ln_g), row(ln_b), min(MID_BLOCK, seq))

    pieces = [g_w_glu.reshape(N_CHIPS, -1, D_SSM), g_w_out.reshape(N_CHIPS, -1, D_MODEL)] if sharded else []
    attn_grads = _attn_bwd(q_stack, k_rot, v_bf, sink128, attn_bias, d_o, pieces)
    dq, dk, dv, g_sink = attn_grads[:4]
    if sharded:
        g_w_glu, g_w_out = attn_grads[4:]
    dus, g_bb, g_cb, g_lam = [], [], [], []
    for d in range(N_DIR):
        du_d, gb_d, gc_d, dl_d = _ssm_bwd(u, d_ylin, starts[d], lam, bb, bbt, cb_t, direction=d, tb=ssm_tb,
                                          name=f"ssm_bwd_{d}")
        dus.append(du_d)
        g_bb.append(gb_d)
        g_cb.append(gc_d)
        g_lam.append(dl_d)
    g_ar, g_ai, g_dt, g_br, g_bi, g_cr, g_ci = _ssm_params_bwd(a_r, a_i, log_dt, b_r, b_i, g_bb, g_cb, g_lam)

    g_small = _pack_small_grads([g_br, g_bi, g_cr, g_ci], g_vec, g_ar, g_ai, g_dt, g_sink[:, 0], loss_blk[0, 0])
    dproj_args = (dq, dk, dv, d_za, dus[0], dus[1], d_ylin, d_zs, row(ssm_d))
    w_grads = _proj_bwd_w(xs, dproj_args, rope_hi, rope_lo, [g_small] if sharded else [], min(512, seq))
    g_wt = w_grads[0]
    if sharded:
        g_small = w_grads[1]
    x_grads = _proj_bwd_x(dproj_args, rope_hi, rope_lo, d_pre, wt_full,
                          [g_wt.reshape(N_CHIPS, -1, D_MODEL)] if sharded else [], min(512, seq))
    g_x = x_grads[0]
    if sharded:
        g_wt = x_grads[1]
    return g_x, g_wt, g_w_out, g_w_glu, g_small
```

```python
import functools
import math

import numpy as np
import jax
import jax.numpy as jnp
from jax import lax
from jax.experimental import pallas as pl
from jax.experimental.pallas import tpu as pltpu

F32 = jnp.float32
BF16 = jnp.bfloat16
MESH = pl.DeviceIdType.MESH

D_MODEL = 1024
D_ATTN = 512
D_SSM = 512
HEAD_DIM = 64
N_Q_HEADS = 8
WINDOW = 128
ROPE_THETA = 10000.0
SSM_CH = 16
N_GROUPS = 32
SSM_STATE = 64
N_DIR = 2
STATE_W = N_GROUPS * SSM_STATE
N_SLAB = 4
SLAB_IN = 128
SLAB_ST = 512
NORM_EPS = 1e-5
NEG_INF = -1e30
ALPHA = 2.0 ** 0.25
D_IN_PROJ = 2304
N_CHIPS = 4

ADAM_LR = 0.001
ADAM_B1 = 0.9
ADAM_B2 = 0.999
ADAM_EPS = 1e-08
ADAM_WD = 0.01
ADAM_STEP = 10

SUBSEG = 8
SCAN_LANES = 512
SSM_BLOCK = 1024
VMEM_LIMIT = 48 * 1024 * 1024
ADAMW_BLOCK_BYTES = 3 * 512 * 1024
PROJ_BWD_X_VMEM = 56 * 1024 * 1024
MID_VMEM = 60 * 1024 * 1024
MID_BLOCK = 512

def _to_pair_order(row):
    return jnp.transpose(row.reshape(2, 4, HEAD_DIM), (1, 0, 2)).reshape(1, D_ATTN)


def _from_pair_order(row):
    return jnp.transpose(row.reshape(4, 2, HEAD_DIM), (1, 0, 2)).reshape(1, D_ATTN)


def _cparams(sem=None):
    return pltpu.CompilerParams(dimension_semantics=sem, vmem_limit_bytes=VMEM_LIMIT)


def _dot(a, b):
    return jnp.dot(a, b, preferred_element_type=F32)


def _dot_nt(a, b):
    return lax.dot_general(a, b, (((1,), (1,)), ((), ())), preferred_element_type=F32)


def _dot_tn(a, b):
    return lax.dot_general(a, b, (((0,), (0,)), ((), ())), preferred_element_type=F32)


def _sigmoid(z):
    return 0.5 * jnp.tanh(0.5 * z) + 0.5


def _all_gather_chips(shards, out_dtype, name):
    n = len(shards)

    def body(*refs):
        start, relay, finish = _gather_phases(refs[:n], refs[n:2 * n], *refs[2 * n:], out_dtype)
        start()
        relay()
        finish()

    vmem = pl.BlockSpec(memory_space=pltpu.VMEM)
    return pl.pallas_call(
        body, name=name,
        out_shape=[jax.ShapeDtypeStruct((N_CHIPS,) + s.shape, out_dtype) for s in shards],
        in_specs=[vmem] * n, out_specs=[vmem] * n,
        scratch_shapes=_gather_sems(n),
        compiler_params=pltpu.CompilerParams(vmem_limit_bytes=VMEM_LIMIT),
    )(*shards)


def _gather_sems(n):
    return [pltpu.SemaphoreType.DMA((6 * n,)), pltpu.SemaphoreType.DMA((6 * n,))]


def _gather_phases(in_refs, out_refs, send_sems, recv_sems, out_dtype):
    n = len(in_refs)
    x, y, c = lax.axis_index("x"), lax.axis_index("y"), lax.axis_index("c")
    sibling = (x, y, 1 - c)
    chips = [(1 - x, y), (x, 1 - y), (1 - x, 1 - y)]

    def half_of(a, px, py, half):
        rows = in_refs[a].shape[0] // 2
        return out_refs[a].at[2 * px + py, pl.ds(half * rows, rows), :]

    def copy(a, k, px, py, half, to):
        blk = half_of(a, px, py, half)
        return pltpu.make_async_remote_copy(src_ref=blk, dst_ref=blk, send_sem=send_sems.at[6 * a + k],
                                            recv_sem=recv_sems.at[6 * a + k], device_id=to, device_id_type=MESH)

    first = [copy(a, j, x, y, c, (*chips[j], c)) for a in range(n) for j in range(3)]
    passed = [copy(a, 3 + j, *chips[j], c, sibling) for a in range(n) for j in range(3)]

    def start():
        for a in range(n):
            out_refs[a][2 * x + y] = in_refs[a][...].astype(out_dtype)
        for cp in first:
            cp.start()

    def relay():
        for a in range(n):
            for j in range(3):
                copy(a, j, *chips[j], c, (x, y, c)).wait_recv()
                passed[3 * a + j].start()

    def finish():
        for a in range(n):
            for j in range(3):
                copy(a, 3 + j, *chips[j], 1 - c, (x, y, c)).wait_recv()
        for cp in first + passed:
            cp.wait_send()

    return start, relay, finish


SEMS_PER_ARRAY = 14


def _reduce_scratch(shapes, narrow):
    half = [(N_CHIPS, s[1] // 2, s[2]) for s in shapes]
    wire = [BF16 if nar else F32 for nar in narrow]
    n = len(shapes)
    return ([pltpu.VMEM(half[a], F32) for a in range(n)] + [pltpu.VMEM(half[a], wire[a]) for a in range(n)]
            + [pltpu.VMEM(half[a], wire[a]) for a in range(n)]
            + [pltpu.SemaphoreType.DMA((SEMS_PER_ARRAY * n,)), pltpu.SemaphoreType.DMA((SEMS_PER_ARRAY * n,))])


def _reduce_phases(p_refs, out_refs, a_refs, s_refs, b_refs, send_sems, recv_sems, narrow, gather_last):
    n = len(p_refs)
    halves = [p.shape[1] // 2 for p in p_refs]
    wire = [BF16 if nar else F32 for nar in narrow]
    x, y, c = lax.axis_index("x"), lax.axis_index("y"), lax.axis_index("c")
    me = 2 * x + y
    sibling = (x, y, 1 - c)
    chips = [(1 - x, y), (x, 1 - y), (1 - x, 1 - y)]
    slot = [2 * px + py for px, py in chips]
    last = n - 1

    def copy(a, k, src, dst, to):
        return pltpu.make_async_remote_copy(src_ref=src, dst_ref=dst, send_sem=send_sems.at[SEMS_PER_ARRAY * a + k],
                                            recv_sem=recv_sems.at[SEMS_PER_ARRAY * a + k],
                                            device_id=to, device_id_type=MESH)

    def rows(a, half):
        return pl.ds(pl.multiple_of(half * halves[a], 16), halves[a])

    def finished(a, k, half):
        if gather_last and a == last:
            return out_refs[a].at[k, rows(a, half), :]
        return out_refs[a].at[rows(a, half), :]

    order = slot + [me]
    swaps = [[copy(a, q, p_refs[a].at[order[q], rows(a, 1 - c), :], a_refs[a].at[order[q]], sibling)
              for q in range(N_CHIPS)] for a in range(n)]
    sends = [[copy(a, 4 + j, s_refs[a].at[slot[j]], b_refs[a].at[me], (*chips[j], c)) for j in range(3)] for a in range(n)]
    backs = [copy(a, 7, finished(a, me, c), finished(a, me, c), sibling) for a in range(n)]
    spread = [copy(last, 8 + j, finished(last, me, c), finished(last, me, c), (*chips[j], c)) for j in range(3)]
    relays = [copy(last, 11 + j, finished(last, slot[j], c), finished(last, slot[j], c), sibling) for j in range(3)]

    def start():
        for group in swaps:
            for cp in group:
                cp.start()

    def exchange():
        for a in range(n):
            for q in range(N_CHIPS):
                swaps[a][q].wait_recv()
                acc = a_refs[a][order[q]] + p_refs[a][order[q], rows(a, c), :]
                a_refs[a][order[q]] = acc
                s_refs[a][order[q]] = acc.astype(wire[a])
                if q < 3:
                    sends[a][q].start()
            b_refs[a][me] = s_refs[a][me]

    def combine():
        for a in range(n):
            for j in range(3):
                copy(a, 4 + j, s_refs[a].at[slot[j]], b_refs[a].at[slot[j]], (x, y, c)).wait_recv()
            terms = [jnp.where(me == k, a_refs[a][k], b_refs[a][k].astype(F32)) for k in range(N_CHIPS)]
            total = (terms[0] + terms[1]) + (terms[2] + terms[3])
            if gather_last and a == last:
                out_refs[a][me, rows(a, c), :] = total
            else:
                out_refs[a][rows(a, c), :] = total
            backs[a].start()
        if gather_last:
            for cp in spread:
                cp.start()

    def finish():
        if gather_last:
            for j in range(3):
                copy(last, 8 + j, finished(last, slot[j], c), finished(last, slot[j], c), (x, y, c)).wait_recv()
                relays[j].start()
        for a in range(n):
            copy(a, 7, finished(a, me, 1 - c), finished(a, me, 1 - c), (x, y, c)).wait_recv()
        if gather_last:
            for j in range(3):
                copy(last, 11 + j, finished(last, slot[j], 1 - c), finished(last, slot[j], 1 - c), (x, y, c)).wait_recv()
        started = [cp for group in swaps + sends for cp in group] + backs + (spread + relays if gather_last else [])
        for cp in started:
            cp.wait_send()

    return start, exchange, combine, finish


def _ssm_param_values(ar, ai, logdt):
    dt = jnp.exp(logdt)
    mag = jnp.exp(dt * ar)
    cs, sn = jnp.cos(dt * ai), jnp.sin(dt * ai)
    lr, li = mag * cs, mag * sn
    den = ar * ar + ai * ai
    nr = (lr - 1.0) * ar + li * ai
    ni = li * ar - (lr - 1.0) * ai
    return dt, mag, lr, li, den, nr, ni


GROUPS_PER_SLAB = N_GROUPS // N_SLAB


def _slab_masks():
    def eq(shape, f_row, f_col):
        return (f_row(lax.broadcasted_iota(jnp.int32, shape, 0)) == f_col(lax.broadcasted_iota(jnp.int32, shape, 1))).astype(F32)
    spread = eq((SSM_STATE, SLAB_ST), lambda r: r, lambda c: c % SSM_STATE)
    spread_t = eq((SLAB_ST, SSM_STATE), lambda r: r % SSM_STATE, lambda c: c)
    keep = eq((SLAB_IN, SLAB_ST), lambda r: r // SSM_CH, lambda c: c // SSM_STATE)
    keep_t = eq((SLAB_ST, SLAB_IN), lambda r: r // SSM_STATE, lambda c: c // SSM_CH)
    repeat = eq((N_DG * SSM_CH, N_DG), lambda r: r // SSM_CH, lambda c: c)
    return spread, spread_t, keep, keep_t, repeat


def _rows(ref):
    return ref[...].reshape(-1, SSM_STATE)


def _split3(t):
    hi = t.astype(BF16)
    rest = t - hi.astype(F32)
    mid = rest.astype(BF16)
    return hi, mid, (rest - mid.astype(F32)).astype(BF16)


def _select(dot, ones01, t, ones_first):
    o = ones01.astype(BF16)
    parts = [dot(o, p) if ones_first else dot(p, o) for p in _split3(t)]
    return (parts[0] + parts[1]) + parts[2]


def _ssm_params_fwd(ar, ai, logdt, br, bi, cr, ci, n_square):
    def body(ar_ref, ai_ref, dt_ref, br_ref, bi_ref, cr_ref, ci_ref, lam_ref, bb_ref, bbt_ref, cb_ref, cbt_ref):
        _, _, lr, li, den, nr, ni = _ssm_param_values(_rows(ar_ref), _rows(ai_ref), dt_ref[...])
        lam_ref[0] = lr
        lam_ref[1] = li
        pr, pi = lr, li
        for _ in range(n_square):
            pr, pi = pr * pr - pi * pi, 2.0 * pr * pi
        lam_ref[2] = pr
        lam_ref[3] = pi
        spread, spread_t, keep, keep_t, repeat = _slab_masks()
        fr = _select(_dot, repeat, nr / den, True)
        fi = _select(_dot, repeat, ni / den, True)
        b_r, b_i = _rows(br_ref), _rows(bi_ref)
        bbar = (fr * b_r - fi * b_i, fr * b_i + fi * b_r)
        c_par = (_rows(cr_ref), _rows(ci_ref))
        spread, spread_t = spread.astype(BF16), spread_t.astype(BF16)
        for src, wide_ref, tall_ref in ((bbar, bb_ref, bbt_ref), (c_par, cbt_ref, cb_ref)):
            for q in range(2):
                for d in range(N_DIR):
                    for k in range(N_SLAB):
                        r0 = (d * N_GROUPS + k * GROUPS_PER_SLAB) * SSM_CH
                        blk = src[q][r0:r0 + SLAB_IN].astype(BF16)
                        wide_ref[q, d, k] = (_dot(blk, spread) * keep).astype(BF16)
                        tall_ref[q, d, k] = (_dot_nt(spread_t, blk) * keep_t).astype(BF16)

    wide = jax.ShapeDtypeStruct((2, N_DIR, N_SLAB, SLAB_IN, SLAB_ST), BF16)
    tall = jax.ShapeDtypeStruct((2, N_DIR, N_SLAB, SLAB_ST, SLAB_IN), BF16)
    return pl.pallas_call(body, name="ssm_params_fwd",
                          out_shape=[jax.ShapeDtypeStruct((4, N_DG, SSM_STATE), F32), wide, tall, tall, wide],
                          compiler_params=pltpu.CompilerParams(vmem_limit_bytes=VMEM_LIMIT),
                          )(ar, ai, logdt, br, bi, cr, ci)


def _ssm_params_bwd(ar, ai, logdt, br, bi, g_slabs_b, g_slabs_c, g_lam):
    def body(ar_ref, ai_ref, dt_ref, br_ref, bi_ref, gb0_ref, gb1_ref, gc0_ref, gc1_ref, gl0_ref, gl1_ref,
             gar_ref, gai_ref, gdt_ref, gbr_ref, gbi_ref, gcr_ref, gci_ref, dbb, dlam):
        spread, spread_t, keep, _, repeat = _slab_masks()
        for d, (gb_ref, gc_ref) in enumerate(((gb0_ref, gc0_ref), (gb1_ref, gc1_ref))):
            for q in range(2):
                for k in range(N_SLAB):
                    r0 = (d * N_GROUPS + k * GROUPS_PER_SLAB) * SSM_CH
                    dbb[q, r0:r0 + SLAB_IN, :] = _select(_dot, spread_t, gb_ref[q, k] * keep, False)
                    out_ref = gcr_ref if q == 0 else gci_ref
                    out_ref[r0:r0 + SLAB_IN, :] = _select(_dot, spread_t, gc_ref[q, k] * keep, False)
        grp = (lax.broadcasted_iota(jnp.int32, (N_GROUPS, STATE_W), 0)
               == lax.broadcasted_iota(jnp.int32, (N_GROUPS, STATE_W), 1) // SSM_STATE).astype(F32)
        pick = (lax.broadcasted_iota(jnp.int32, (STATE_W, SSM_STATE), 0) % SSM_STATE
                == lax.broadcasted_iota(jnp.int32, (STATE_W, SSM_STATE), 1)).astype(F32)
        for d, gl_ref in enumerate((gl0_ref, gl1_ref)):
            for q in range(2):
                row = jnp.sum(gl_ref[q], axis=0, keepdims=True)
                dlam[q, d * N_GROUPS:(d + 1) * N_GROUPS, :] = _select(_dot, pick, grp * row, False)

        a_r, a_i = _rows(ar_ref), _rows(ai_ref)
        dt, mag, lr, li, den, nr, ni = _ssm_param_values(a_r, a_i, dt_ref[...])
        fr = _select(_dot, repeat, nr / den, True)
        fi = _select(_dot, repeat, ni / den, True)
        b_r, b_i = _rows(br_ref), _rows(bi_ref)
        g_r, g_i = dbb[0], dbb[1]
        gbr_ref[...] = fr * g_r + fi * g_i
        gbi_ref[...] = fr * g_i - fi * g_r
        d_fr = _select(_dot_tn, repeat, b_r * g_r + b_i * g_i, True)
        d_fi = _select(_dot_tn, repeat, b_r * g_i - b_i * g_r, True)
        d_nr, d_ni = d_fr / den, d_fi / den
        d_den = -(d_fr * nr + d_fi * ni) / (den * den)
        d_lr = dlam[0] + d_nr * a_r - d_ni * a_i
        d_li = dlam[1] + d_nr * a_i + d_ni * a_r
        d_ar = d_nr * (lr - 1.0) + d_ni * li + d_den * 2.0 * a_r
        d_ai = d_nr * li - d_ni * (lr - 1.0) + d_den * 2.0 * a_i
        d_mag = (d_lr * lr + d_li * li) / mag
        d_theta = d_li * lr - d_lr * li
        gar_ref[...] = d_ar + d_mag * mag * dt
        gai_ref[...] = d_ai + d_theta * dt
        d_dt = d_mag * mag * a_r + d_theta * a_i
        gdt_ref[...] = jnp.sum(d_dt, axis=1, keepdims=True) * dt

    small = jax.ShapeDtypeStruct((N_DG, SSM_STATE), F32)
    big = jax.ShapeDtypeStruct((N_DG * SSM_CH, SSM_STATE), F32)
    return pl.pallas_call(
        body, name="ssm_params_bwd",
        out_shape=[small, small, jax.ShapeDtypeStruct(logdt.shape, F32), big, big, big, big],
        scratch_shapes=[pltpu.VMEM((2,) + big.shape, F32), pltpu.VMEM((2,) + small.shape, F32)],
        compiler_params=pltpu.CompilerParams(vmem_limit_bytes=VMEM_LIMIT),
    )(ar, ai, logdt, br, bi, *g_slabs_b, *g_slabs_c, *g_lam)


ROPE_GROUP = 128


def _rope_tables(seq):
    half = HEAD_DIM // 2
    inv_freq = jnp.tile(ROPE_THETA ** (-jnp.arange(half, dtype=F32) / half), 4)
    sign = jnp.tile(jnp.concatenate([-jnp.ones((half,), F32), jnp.ones((half,), F32)]), 2)

    def table(pos):
        ang = pos.astype(F32)[:, None] * inv_freq[None, :]
        return jnp.stack([jnp.cos(ang), jnp.sin(ang), sign * jnp.sin(ang)])

    return table(jnp.arange(seq // ROPE_GROUP) * ROPE_GROUP), table(jnp.arange(ROPE_GROUP))


def _rope_block(hi_ref, lo_ref, first_group, n_groups):
    cl, sl, sl_s = lo_ref[0], lo_ref[1], lo_ref[2]
    cos, sin = [], []
    for g in range(n_groups):
        ch, sh, sh_s = (hi_ref[q, pl.ds(first_group + g, 1), :] for q in range(3))
        cos.append(ch * cl - sh * sl)
        sin.append(sh_s * cl + ch * sl_s)
    return jnp.concatenate(cos, axis=0), jnp.concatenate(sin, axis=0)


def _rotate_half_unsigned(t):
    lane = lax.broadcasted_iota(jnp.int32, t.shape, 1)
    return jnp.where((lane % HEAD_DIM) < HEAD_DIM // 2, pltpu.roll(t, 96, 1), pltpu.roll(t, 32, 1))


def _rope(t, cos, sin_signed):
    return t * cos + _rotate_half_unsigned(t) * sin_signed


def _pair_blocks(base):
    out = []
    for j in range(4):
        for g in range(2):
            nat = base + HEAD_DIM * (4 * g + j)
            par = base + 128 * j + HEAD_DIM * g
            out.append((slice(nat, nat + HEAD_DIM), slice(par, par + HEAD_DIM)))
    return out


W_Q, W_KV, W_ZA, W_U, W_ZS = 0, 512, 768, 1280, 1792


def _proj(x, wt, rope_hi, rope_lo, shards, tb):
    seq = x.shape[0]
    steps = seq // tb
    n_sh = len(shards)

    def body(*refs):
        x_ref, wt_ref, hi_ref, lo_ref = refs[:4]
        shard_refs = refs[4:4 + n_sh]
        q_ref, k_ref, v_ref, za_ref, u_ref, zs_ref = refs[4 + n_sh:10 + n_sh]
        gathered_refs = refs[10 + n_sh:10 + 2 * n_sh]
        wp = refs[10 + 2 * n_sh]
        step = pl.program_id(0)
        if n_sh:
            landing_refs = refs[11 + 2 * n_sh:11 + 3 * n_sh]
            start, relay, finish = _gather_phases(shard_refs, landing_refs, *refs[11 + 3 * n_sh:], BF16)
            pl.when(step == 0)(start)
            pl.when(step == max(steps - 2, 0))(relay)

        @pl.when(step == 0)
        def _():
            for dst_base, src_base in ((0, W_Q), (512, W_ZA)):
                for nat, par in _pair_blocks(0):
                    wp[dst_base + par.start:dst_base + par.stop, :] = wt_ref[src_base + nat.start:src_base + nat.stop, :]

        xb = x_ref[...].astype(BF16)
        cos, sin = _rope_block(hi_ref, lo_ref, pl.program_id(0) * (tb // ROPE_GROUP), tb // ROPE_GROUP)
        lo = lax.broadcasted_iota(jnp.int32, (tb, 128), 1) < HEAD_DIM
        q = _dot_nt(xb, wp[0:512, :])
        for j in range(4):
            qj = _rope(q[:, 128 * j:128 * (j + 1)], cos, sin)
            q_ref[j] = jnp.where(lo, qj, 0.0).astype(BF16)
            q_ref[4 + j] = jnp.where(lo, 0.0, qj).astype(BF16)
        kv = _dot_nt(xb, wt_ref[W_KV:W_ZA, :])
        k_ref[...] = _rope(kv[:, 0:128], cos, sin).astype(BF16)
        v_ref[...] = kv[:, 128:256].astype(BF16)
        za_ref[...] = _dot_nt(xb, wp[512:1024, :])
        u_val = _dot_nt(xb, wt_ref[W_U:W_ZS, :])
        for k in range(N_SLAB):
            u_ref[k] = u_val[:, k * SLAB_IN:(k + 1) * SLAB_IN]
        zs_ref[...] = _dot_nt(xb, wt_ref[W_ZS:D_IN_PROJ, :])
        if n_sh:
            @pl.when(step == steps - 1)
            def _():
                finish()
                for a in range(n_sh):
                    gathered_refs[a][...] = landing_refs[a][...]

    row = lambda w: pl.BlockSpec((tb, w), lambda i: (i, 0))
    table = lambda t: pl.BlockSpec(t.shape, lambda i: (0, 0, 0))
    vmem = pl.BlockSpec(memory_space=pltpu.VMEM)
    return pl.pallas_call(
        body, name="proj", grid=(steps,),
        in_specs=[row(D_MODEL), pl.BlockSpec((D_IN_PROJ, D_MODEL), lambda i: (0, 0), pipeline_mode=pl.Buffered(1)),
                  table(rope_hi), table(rope_lo)] + [vmem] * n_sh,
        out_specs=[pl.BlockSpec((8, tb, 128), lambda i: (0, i, 0)), row(128), row(128), row(512),
                   pl.BlockSpec((N_SLAB, tb, SLAB_IN), lambda i: (0, i, 0)), row(512)] + [vmem] * n_sh,
        out_shape=[jax.ShapeDtypeStruct((8, seq, 128), BF16), jax.ShapeDtypeStruct((seq, 128), BF16),
                   jax.ShapeDtypeStruct((seq, 128), BF16), jax.ShapeDtypeStruct((seq, 512), F32),
                   jax.ShapeDtypeStruct((N_SLAB, seq, SLAB_IN), F32), jax.ShapeDtypeStruct((seq, 512), F32)]
        + [jax.ShapeDtypeStruct((N_CHIPS,) + s.shape, BF16) for s in shards],
        scratch_shapes=[pltpu.VMEM((1024, D_MODEL), BF16)] + [pltpu.VMEM((N_CHIPS,) + s.shape, BF16) for s in shards]
        + (_gather_sems(n_sh) if n_sh else []),
        compiler_params=_cparams(("arbitrary",)),
    )(x, wt, rope_hi, rope_lo, *shards)


ATT_TQ = 128
ATT_KEYS = ATT_TQ + 2 * WINDOW


def _attn_window(i, seq):
    start = jnp.clip(i * ATT_TQ - WINDOW, 0, seq - ATT_KEYS)
    return pl.multiple_of(start, WINDOW)


def _attn_bias():
    r = np.arange(ATT_TQ)[None, :, None]
    c = np.arange(ATT_KEYS)[None, None, :]
    off = np.array([0, WINDOW, ATT_KEYS - ATT_TQ])[:, None, None]
    return jnp.asarray(np.where(np.abs(r + off - c) <= WINDOW, 0.0, NEG_INF).astype(np.float32))


def _attn_bias_spec(nblk):
    pick = lambda i: jnp.where(i == 0, 0, jnp.where(i == nblk - 1, 2, 1))
    return pl.BlockSpec((None, ATT_TQ, ATT_KEYS), lambda i: (pick(i), 0, 0))


def _attn_softmax(q_ref, k_ref, v_ref, sink_ref, bias_ref, start):
    kw = k_ref[pl.ds(start, ATT_KEYS), :]
    vw = v_ref[pl.ds(start, ATT_KEYS), :]
    qall = q_ref[...].reshape(N_Q_HEADS * ATT_TQ, 128)
    s = (_dot_nt(qall, kw) * (HEAD_DIM ** -0.5)).reshape(N_Q_HEADS, ATT_TQ, ATT_KEYS) + bias_ref[...][None]
    tiles = [s[:, :, 128 * t:128 * (t + 1)] for t in range(ATT_KEYS // 128)]
    m = jnp.max(functools.reduce(jnp.maximum, tiles), axis=2, keepdims=True)
    sink = sink_ref[...]
    m_b = jnp.maximum(jnp.broadcast_to(m, (N_Q_HEADS, ATT_TQ, 128)), sink)
    p = jnp.concatenate([jnp.exp(t - m_b) for t in tiles], axis=2)
    p_sink = jnp.exp(sink - m_b)
    lo_k = lax.broadcasted_iota(jnp.int32, (ATT_KEYS, 128), 1) < HEAD_DIM
    v_f = vw.astype(F32)
    v_lo, v_hi = jnp.where(lo_k, v_f, 1.0).astype(BF16), jnp.where(lo_k, 1.0, v_f).astype(BF16)
    pb = p.astype(BF16).reshape(N_Q_HEADS * ATT_TQ, ATT_KEYS)
    half = 4 * ATT_TQ
    r = jnp.concatenate([_dot(pb[:half], v_lo), _dot(pb[half:], v_hi)], axis=0).reshape(N_Q_HEADS, ATT_TQ, 128)
    return kw, vw, qall, p, p_sink, r


def _attn_fwd(q_stack, k, v, sink128, bias):
    seq = k.shape[0]

    def body(q_ref, k_ref, v_ref, sink_ref, bias_ref, o_ref):
        start = _attn_window(pl.program_id(0), seq)
        _, _, _, _, p_sink, r = _attn_softmax(q_ref, k_ref, v_ref, sink_ref, bias_ref, start)
        out = r / (pltpu.roll(r, HEAD_DIM, 2) + p_sink)
        lo = lax.broadcasted_iota(jnp.int32, (ATT_TQ, 128), 1) < HEAD_DIM
        for j in range(4):
            o_ref[:, 128 * j:128 * (j + 1)] = jnp.where(lo, out[j], out[4 + j])

    full = lambda w: pl.BlockSpec((seq, w), lambda i: (0, 0))
    return pl.pallas_call(
        body, name="attn_fwd", grid=(seq // ATT_TQ,),
        in_specs=[pl.BlockSpec((8, ATT_TQ, 128), lambda i: (0, i, 0)), full(128), full(128),
                  pl.BlockSpec((N_Q_HEADS, 1, 128), lambda i: (0, 0, 0)), _attn_bias_spec(seq // ATT_TQ)],
        out_specs=pl.BlockSpec((ATT_TQ, 512), lambda i: (i, 0)),
        out_shape=jax.ShapeDtypeStruct((seq, 512), F32),
        compiler_params=_cparams(("arbitrary",)),
    )(q_stack, k, v, sink128, bias)


def _attn_bwd(q_stack, k, v, sink128, bias, d_o, pieces):
    seq = k.shape[0]
    steps = seq // ATT_TQ
    n_p = len(pieces)

    def body(*refs):
        q_ref, k_ref, v_ref, sink_ref, bias_ref, do_ref = refs[:6]
        piece_refs = refs[6:6 + n_p]
        dq_ref, dk_ref, dv_ref, dsink_ref = refs[6 + n_p:10 + n_p]
        reduced_refs = refs[10 + n_p:10 + 2 * n_p]
        sink_acc = refs[10 + 2 * n_p]
        i = pl.program_id(0)
        if n_p:
            landing_refs = refs[11 + 2 * n_p:11 + 3 * n_p]
            scratch = refs[11 + 3 * n_p:]
            begin, exchange, combine, finish = _reduce_phases(
                piece_refs, landing_refs, scratch[:n_p], scratch[n_p:2 * n_p], scratch[2 * n_p:3 * n_p],
                *scratch[3 * n_p:], [True] * n_p, gather_last=False)
            pl.when(i == 0)(begin)
            pl.when(i == min(4, steps - 1))(exchange)
            pl.when(i == (3 * steps) // 4)(combine)

        @pl.when(i == 0)
        def _():
            dk_ref[...] = jnp.zeros_like(dk_ref)
            dv_ref[...] = jnp.zeros_like(dv_ref)
            sink_acc[...] = jnp.zeros_like(sink_acc)

        start = _attn_window(i, seq)
        kw, vw, qall, p, p_sink, r = _attn_softmax(q_ref, k_ref, v_ref, sink_ref, bias_ref, start)
        lo = lax.broadcasted_iota(jnp.int32, (ATT_TQ, 128), 1) < HEAD_DIM
        lo3 = lo[None]
        grp0 = lax.broadcasted_iota(jnp.int32, (N_Q_HEADS, ATT_TQ, 128), 0) < 4
        val = grp0 == lo3
        swapped = pltpu.roll(r, HEAD_DIM, 2)
        inv = 1.0 / (jnp.where(val, swapped, r) + p_sink)
        d_o_blk = do_ref[...]
        do3 = jnp.where(val, jnp.concatenate([d_o_blk[None, :, 128 * j:128 * (j + 1)] for j in range(4)] * 2, axis=0), 0.0)
        t = (do3 * r).reshape(N_Q_HEADS * ATT_TQ, 128)
        t_hi = t.astype(BF16)
        t_lo = (t - t_hi.astype(F32)).astype(BF16)
        ones = jnp.ones((128, 128), BF16)
        delta = (_dot(t_hi, ones) + _dot(t_lo, ones)).reshape(N_Q_HEADS, ATT_TQ, 128) * inv
        sink_acc[...] += -(p_sink * inv) * delta
        do_all = do3.astype(BF16).reshape(N_Q_HEADS * ATT_TQ, 128)
        dp = _dot_nt(do_all, vw).reshape(N_Q_HEADS, ATT_TQ, ATT_KEYS)
        probs, ds = [], []
        for tl in range(ATT_KEYS // 128):
            cols = slice(128 * tl, 128 * (tl + 1))
            probs_t = p[:, :, cols] * inv
            probs.append(probs_t.astype(BF16))
            ds.append((probs_t * (dp[:, :, cols] - delta)).astype(BF16))
        probs_all = jnp.concatenate(probs, axis=2).reshape(N_Q_HEADS * ATT_TQ, ATT_KEYS)
        ds_all = jnp.concatenate(ds, axis=2).reshape(N_Q_HEADS * ATT_TQ, ATT_KEYS)
        scale = HEAD_DIM ** -0.5
        dq_all = (_dot(ds_all, kw) * scale).reshape(N_Q_HEADS, ATT_TQ, 128)
        for j in range(4):
            dq_ref[:, 128 * j:128 * (j + 1)] = jnp.where(lo, dq_all[j], dq_all[4 + j])
        dk_ref[pl.ds(start, ATT_KEYS), :] += _dot_tn(ds_all, qall) * scale
        dv_ref[pl.ds(start, ATT_KEYS), :] += _dot_tn(probs_all, do_all)

        @pl.when(i == steps - 1)
        def _():
            dsink_ref[...] = jnp.sum(sink_acc[...], axis=1)

        if n_p:
            @pl.when(i == steps - 1)
            def _():
                finish()
                for a in range(n_p):
                    reduced_refs[a][...] = landing_refs[a][...]

    full = lambda w: pl.BlockSpec((seq, w), lambda i: (0, 0))
    vmem = pl.BlockSpec(memory_space=pltpu.VMEM)
    return pl.pallas_call(
        body, name="attn_bwd", grid=(steps,),
        in_specs=[pl.BlockSpec((8, ATT_TQ, 128), lambda i: (0, i, 0)), full(128), full(128),
                  pl.BlockSpec((N_Q_HEADS, 1, 128), lambda i: (0, 0, 0)),
                  _attn_bias_spec(steps), pl.BlockSpec((ATT_TQ, 512), lambda i: (i, 0))] + [vmem] * n_p,
        out_specs=[pl.BlockSpec((ATT_TQ, 512), lambda i: (i, 0)), full(128), full(128),
                   pl.BlockSpec((N_Q_HEADS, 128), lambda i: (0, 0))] + [vmem] * n_p,
        out_shape=[jax.ShapeDtypeStruct((seq, 512), F32), jax.ShapeDtypeStruct((seq, 128), F32),
                   jax.ShapeDtypeStruct((seq, 128), F32), jax.ShapeDtypeStruct((N_Q_HEADS, 128), F32)]
        + [jax.ShapeDtypeStruct(p.shape[1:], F32) for p in pieces],
        scratch_shapes=[pltpu.VMEM((N_Q_HEADS, ATT_TQ, 128), F32)] + [pltpu.VMEM(p.shape[1:], F32) for p in pieces]
        + (_reduce_scratch([p.shape for p in pieces], [True] * n_p) if n_p else []),
        compiler_params=_cparams(("arbitrary",)),
    )(q_stack, k, v, sink128, bias, d_o, *pieces)


def _permute_rows(dst_ref, src_ref, sub_len):
    for k in range(N_SLAB):
        for j in range(sub_len):
            dst_ref[k, 8 * j:8 * (j + 1), :] = src_ref.at[k][pl.ds(j, SUBSEG, stride=sub_len), :]


def _unpermute_rows(dst_ref, src_ref, sub_len):
    for k in range(N_SLAB):
        for s in range(SUBSEG):
            dst_ref[k, s * sub_len:(s + 1) * sub_len, :] = src_ref.at[k][pl.ds(s, sub_len, stride=SUBSEG), :]


def _scan_chunk(br_ref, bi_ref, lr_row, li_row, init, cols, *, sub_len, reverse, store):
    lr = jnp.broadcast_to(lr_row[:, cols], (SUBSEG, SCAN_LANES))
    li = jnp.broadcast_to(li_row[:, cols], (SUBSEG, SCAN_LANES))
    if init is None:
        sr = si = jnp.zeros((SUBSEG, SCAN_LANES), F32)
    else:
        sr, si = init
    for jj in range(sub_len):
        rows = slice(SUBSEG * ((sub_len - 1 - jj) if reverse else jj), SUBSEG * (((sub_len - 1 - jj) if reverse else jj) + 1))
        sr, si = lr * sr - li * si + br_ref[rows, cols], lr * si + li * sr + bi_ref[rows, cols]
        if store:
            br_ref[rows, cols] = sr
            bi_ref[rows, cols] = si
    return sr, si


def _resolve_chunk(z, carry_refs, start_refs, pr_row, pi_row, cols, *, reverse):
    cr, ci = carry_refs[0][0:1, cols], carry_refs[1][0:1, cols]
    pr, pi = pr_row[:, cols], pi_row[:, cols]
    for s in (range(SUBSEG - 1, -1, -1) if reverse else range(SUBSEG)):
        start_refs[0][s:s + 1, cols] = cr
        start_refs[1][s:s + 1, cols] = ci
        cr, ci = pr * cr - pi * ci + z[0][s:s + 1, :], pr * ci + pi * cr + z[1][s:s + 1, :]
    carry_refs[0][0:1, cols] = cr
    carry_refs[1][0:1, cols] = ci


def _param_specs(direction):
    row = lambda q: pl.BlockSpec((None, None, 1, STATE_W), lambda i: (q, direction, 0, 0))
    wide = lambda q: pl.BlockSpec((None, None, N_SLAB, SLAB_IN, SLAB_ST), lambda i: (q, direction, 0, 0, 0))
    tall = lambda q: pl.BlockSpec((None, None, N_SLAB, SLAB_ST, SLAB_IN), lambda i: (q, direction, 0, 0, 0))
    return [row(q) for q in range(4)], [wide(0), wide(1)], [tall(0), tall(1)]


def _ssm_fwd(u, lam, bb, cb, *, direction, tb, name):
    reverse = direction == 1
    seq = u.shape[1]
    nblk = seq // tb
    sub_len = tb // SUBSEG

    def body(u_ref, lr_ref, li_ref, pr_ref, pi_ref, bbr_ref, bbi_ref, cbr_ref, cbi_ref,
             y_ref, sr_ref, si_ref, xr, xi, up, yp, car, cai):
        @pl.when(pl.program_id(0) == 0)
        def _():
            car[...] = jnp.zeros_like(car)
            cai[...] = jnp.zeros_like(cai)

        _permute_rows(up, u_ref, sub_len)
        lr, li, pr, pi = lr_ref[...], li_ref[...], pr_ref[...], pi_ref[...]
        chunk = lambda k: slice(k * SLAB_ST, (k + 1) * SLAB_ST)

        def drive(k):
            ub = up[k].astype(BF16)
            xr[:, chunk(k)] = _dot(ub, bbr_ref[k])
            xi[:, chunk(k)] = _dot(ub, bbi_ref[k])

        def scan(k):
            z = _scan_chunk(xr, xi, lr, li, None, chunk(k), sub_len=sub_len, reverse=reverse, store=False)
            _resolve_chunk(z, (car, cai), (sr_ref, si_ref), pr, pi, chunk(k), reverse=reverse)
            _scan_chunk(xr, xi, lr, li, (sr_ref[:, chunk(k)], si_ref[:, chunk(k)]), chunk(k),
                        sub_len=sub_len, reverse=reverse, store=True)

        def read_out(k):
            yp[k] = _dot(xr[:, chunk(k)].astype(BF16), cbr_ref[k]) - _dot(xi[:, chunk(k)].astype(BF16), cbi_ref[k])

        drive(0)
        for k in range(N_SLAB):
            if k + 1 < N_SLAB:
                drive(k + 1)
            scan(k)
            if k > 0:
                read_out(k - 1)
        read_out(N_SLAB - 1)
        _unpermute_rows(y_ref, yp, sub_len)

    blk = (lambda i: nblk - 1 - i) if reverse else (lambda i: i)
    rows, wide, tall = _param_specs(direction)
    tok = pl.BlockSpec((N_SLAB, tb, SLAB_IN), lambda i: (0, blk(i), 0))
    start_spec = pl.BlockSpec((None, SUBSEG, STATE_W), lambda i: (blk(i), 0, 0))
    return pl.pallas_call(
        body, name=name, grid=(nblk,),
        in_specs=[tok] + rows + wide + tall,
        out_specs=[tok, start_spec, start_spec],
        out_shape=[jax.ShapeDtypeStruct((N_SLAB, seq, SLAB_IN), F32), jax.ShapeDtypeStruct((nblk, SUBSEG, STATE_W), F32),
                   jax.ShapeDtypeStruct((nblk, SUBSEG, STATE_W), F32)],
        scratch_shapes=[pltpu.VMEM((tb, STATE_W), F32), pltpu.VMEM((tb, STATE_W), F32),
                        pltpu.VMEM((N_SLAB, tb, SLAB_IN), F32), pltpu.VMEM((N_SLAB, tb, SLAB_IN), F32),
                        pltpu.VMEM((SUBSEG, STATE_W), F32), pltpu.VMEM((SUBSEG, STATE_W), F32)],
        compiler_params=_cparams(("arbitrary",)),
    )(u, lam, lam, lam, lam, bb, bb, cb, cb)


def _ssm_bwd(u, dy, starts, lam, bb, bbt, cb_t, *, direction, tb, name):
    reverse = direction == 1
    seq = u.shape[1]
    nblk = seq // tb
    sub_len = tb // SUBSEG

    def body(u_ref, dy_ref, sr_ref, si_ref, lr_ref, li_ref, pr_ref, pi_ref, bbr_ref, bbi_ref, btr_ref, bti_ref,
             ctr_ref, cti_ref, du_ref, gb_ref, gc_ref, dl_ref,
             xr, xi, gr, gi, up, dyp, dup, gsr, gsi, car, cai):
        gbr_ref, gbi_ref = gb_ref.at[0], gb_ref.at[1]
        gcr_ref, gci_ref = gc_ref.at[0], gc_ref.at[1]
        dlr_ref, dli_ref = dl_ref.at[0], dl_ref.at[1]

        @pl.when(pl.program_id(0) == 0)
        def _():
            for ref in (car, cai, gbr_ref, gbi_ref, gcr_ref, gci_ref, dlr_ref, dli_ref):
                ref[...] = jnp.zeros_like(ref)

        _permute_rows(up, u_ref, sub_len)
        _permute_rows(dyp, dy_ref, sub_len)
        lr, li, pr, pi = lr_ref[...], li_ref[...], pr_ref[...], pi_ref[...]
        nli, npi = -li, -pi
        chunk = lambda k: slice(k * SLAB_ST, (k + 1) * SLAB_ST)

        def drive(k):
            ub = up[k].astype(BF16)
            xr[:, chunk(k)] = _dot(ub, bbr_ref[k])
            xi[:, chunk(k)] = _dot(ub, bbi_ref[k])
            dyb = dyp[k].astype(BF16)
            gr[:, chunk(k)] = _dot(dyb, ctr_ref[k])
            gi[:, chunk(k)] = -_dot(dyb, cti_ref[k])

        def scan_x(k):
            _scan_chunk(xr, xi, lr, li, (sr_ref[:, chunk(k)], si_ref[:, chunk(k)]), chunk(k),
                        sub_len=sub_len, reverse=reverse, store=True)

        def grad_c(k):
            dyb = dyp[k].astype(BF16)
            gcr_ref[k] += _dot_tn(dyb, xr[:, chunk(k)].astype(BF16))
            gci_ref[k] -= _dot_tn(dyb, xi[:, chunk(k)].astype(BF16))

        def scan_g(k):
            z = _scan_chunk(gr, gi, lr, nli, None, chunk(k), sub_len=sub_len, reverse=not reverse, store=False)
            _resolve_chunk(z, (car, cai), (gsr, gsi), pr, npi, chunk(k), reverse=not reverse)
            _scan_chunk(gr, gi, lr, nli, (gsr[:, chunk(k)], gsi[:, chunk(k)]), chunk(k),
                        sub_len=sub_len, reverse=not reverse, store=True)

        def grad_b_du(k):
            ub = up[k].astype(BF16)
            grb, gib = gr[:, chunk(k)].astype(BF16), gi[:, chunk(k)].astype(BF16)
            gbr_ref[k] += _dot_tn(ub, grb)
            gbi_ref[k] += _dot_tn(ub, gib)
            dup[k] = _dot(grb, btr_ref[k]) + _dot(gib, bti_ref[k])

        def grad_lambda(k):
            cols = chunk(k)
            acc_r, acc_i = dlr_ref[:, cols], dli_ref[:, cols]
            for jj in range(sub_len):
                prev = jj + 1 if reverse else jj - 1
                if 0 <= prev < sub_len:
                    x_r, x_i = xr[SUBSEG * prev:SUBSEG * (prev + 1), cols], xi[SUBSEG * prev:SUBSEG * (prev + 1), cols]
                else:
                    x_r, x_i = sr_ref[:, cols], si_ref[:, cols]
                g_r, g_i = gr[SUBSEG * jj:SUBSEG * (jj + 1), cols], gi[SUBSEG * jj:SUBSEG * (jj + 1), cols]
                acc_r = acc_r + (g_r * x_r + g_i * x_i)
                acc_i = acc_i + (g_i * x_r - g_r * x_i)
            dlr_ref[:, cols] = acc_r
            dli_ref[:, cols] = acc_i

        drive(0)
        for k in range(N_SLAB):
            if k + 1 < N_SLAB:
                drive(k + 1)
            scan_x(k)
            grad_c(k)
            scan_g(k)
            grad_b_du(k)
            grad_lambda(k)
        _unpermute_rows(du_ref, dup, sub_len)

    blk = (lambda i: i) if reverse else (lambda i: nblk - 1 - i)
    rows, wide, tall = _param_specs(direction)
    tok = pl.BlockSpec((N_SLAB, tb, SLAB_IN), lambda i: (0, blk(i), 0))
    start_spec = pl.BlockSpec((None, SUBSEG, STATE_W), lambda i: (blk(i), 0, 0))
    gb_shape, dl_shape = (2, N_SLAB, SLAB_IN, SLAB_ST), (2, SUBSEG, STATE_W)
    whole = lambda shape: pl.BlockSpec(shape, lambda i: (0,) * len(shape))
    big = lambda: pltpu.VMEM((tb, STATE_W), F32)
    slabs = lambda: pltpu.VMEM((N_SLAB, tb, SLAB_IN), F32)
    tile = lambda: pltpu.VMEM((SUBSEG, STATE_W), F32)
    return pl.pallas_call(
        body, name=name, grid=(nblk,),
        in_specs=[tok, tok, start_spec, start_spec] + rows + wide + tall + wide,
        out_specs=[tok, whole(gb_shape), whole(gb_shape), whole(dl_shape)],
        out_shape=[jax.ShapeDtypeStruct((N_SLAB, seq, SLAB_IN), F32), jax.ShapeDtypeStruct(gb_shape, F32),
                   jax.ShapeDtypeStruct(gb_shape, F32), jax.ShapeDtypeStruct(dl_shape, F32)],
        scratch_shapes=[big(), big(), big(), big(), slabs(), slabs(), slabs(), tile(), tile(), tile(), tile()],
        compiler_params=_cparams(("arbitrary",)),
    )(u, dy, *starts, lam, lam, lam, lam, bb, bb, bbt, bbt, cb_t, cb_t)


GELU_C = math.sqrt(2.0 / math.pi)
GELU_K = 0.044715


def _mid(o, za, u, y_f, y_b, zs, x, target, ssm_d, w_glu, b_glu, g_attn, g_ssm, w_out, ln_g, ln_b, tb):
    seq = x.shape[0]

    def body(o_ref, za_ref, u_ref, yf_ref, yb_ref, zs_ref, x_ref, t_ref, d_ref, wg_ref, bg_ref, ga_ref, gs_ref,
             wo_ref, lg_ref, lb_ref,
             loss_ref, do_ref, dza_ref, dyl_ref, dzs_ref, dpre_ref, gwo_ref, gwg_ref, vec_ref, wop):
        @pl.when(pl.program_id(0) == 0)
        def _():
            for ref in (loss_ref, gwo_ref, gwg_ref, vec_ref):
                ref[...] = jnp.zeros_like(ref)
            for nat, par in _pair_blocks(0):
                wop[par, :] = wo_ref[nat, :]
            wop[D_ATTN:, :] = wo_ref[D_ATTN:, :]

        def rows_of(rs):
            o, za = o_ref[rs, :], za_ref[rs, :]
            sig_a = _sigmoid(za)
            silu_a = za * sig_a
            ya = o * silu_a
            r_a = lax.rsqrt(jnp.mean(ya * ya, axis=1, keepdims=True) + NORM_EPS)
            n_a = ya * r_a
            g_a = ga_ref[...]
            unslab = lambda ref: jnp.concatenate([ref[k, rs, :] for k in range(N_SLAB)], axis=1)
            u_blk, zs = unslab(u_ref), zs_ref[rs, :]
            d_row = d_ref[...]
            ylin = d_row * u_blk + unslab(yf_ref) + unslab(yb_ref)
            inner = GELU_C * (ylin + GELU_K * ylin * ylin * ylin)
            th = jnp.tanh(inner)
            gl = 0.5 * ylin * (1.0 + th)
            glb = gl.astype(BF16)
            gate = _dot(glb, wg_ref[...])
            sg = _sigmoid(gate + bg_ref[...])
            y2 = gl * sg
            sig_s = _sigmoid(zs)
            silu_s = zs * sig_s
            ys = y2 * silu_s
            r_s = lax.rsqrt(jnp.mean(ys * ys, axis=1, keepdims=True) + NORM_EPS)
            n_s = ys * r_s
            g_s = gs_ref[...]
            mixed = jnp.concatenate([n_a * g_a, n_s * g_s], axis=1).astype(BF16)
            out = _dot(mixed, wop[...])
            pre = ALPHA * x_ref[rs, :] + out
            mu = jnp.mean(pre, axis=1, keepdims=True)
            cen = pre - mu
            rstd = lax.rsqrt(jnp.mean(cen * cen, axis=1, keepdims=True) + NORM_EPS)
            hhat = cen * rstd
            ln_g = lg_ref[...]
            err = hhat * ln_g + lb_ref[...] - t_ref[rs, :]
            loss_ref[...] += 0.5 * jnp.sum(jnp.mean(err * err, axis=1, keepdims=True))

            dh = err * (1.0 / D_MODEL)
            vec_ref[0:1, :] += jnp.sum(dh * hhat, axis=0, keepdims=True)
            vec_ref[1:2, :] += jnp.sum(dh, axis=0, keepdims=True)
            dhh = dh * ln_g
            dpre = rstd * (dhh - jnp.mean(dhh, axis=1, keepdims=True)
                           - hhat * jnp.mean(dhh * hhat, axis=1, keepdims=True))
            dpre_ref[rs, :] = dpre
            dpb = dpre.astype(BF16)
            for j in range(4):
                g_pair = _dot_tn(mixed[:, 128 * j:128 * (j + 1)], dpb)
                for g in range(2):
                    nat = HEAD_DIM * (4 * g + j)
                    gwo_ref[nat:nat + HEAD_DIM, :] += g_pair[HEAD_DIM * g:HEAD_DIM * (g + 1), :]
            gwo_ref[D_ATTN:, :] += _dot_tn(mixed[:, D_ATTN:], dpb)
            dmix = _dot_nt(dpb, wop[...])
            dna = dmix[:, :D_ATTN]
            vec_ref[2:3, 0:D_ATTN] += jnp.sum(dna * n_a, axis=0, keepdims=True)
            dna = dna * g_a
            dya = r_a * (dna - n_a * jnp.mean(dna * n_a, axis=1, keepdims=True))
            do_ref[rs, :] = dya * silu_a
            dza_ref[rs, :] = dya * o * (sig_a * (1.0 + za * (1.0 - sig_a)))
            dns = dmix[:, D_ATTN:]
            vec_ref[2:3, D_ATTN:] += jnp.sum(dns * n_s, axis=0, keepdims=True)
            dns = dns * g_s
            dys = r_s * (dns - n_s * jnp.mean(dns * n_s, axis=1, keepdims=True))
            dzs_ref[rs, :] = dys * y2 * (sig_s * (1.0 + zs * (1.0 - sig_s)))
            dy2 = dys * silu_s
            da = dy2 * gl * sg * (1.0 - sg)
            vec_ref[3:4, D_SSM:] += jnp.sum(da, axis=0, keepdims=True)
            dab = da.astype(BF16)
            gwg_ref[...] += _dot_tn(glb, dab)
            dgl_mm = _dot_nt(dab, wg_ref[...])
            dgl = dy2 * sg + dgl_mm
            dylin = dgl * (0.5 * (1.0 + th)
                           + 0.5 * ylin * (1.0 - th * th) * GELU_C * (1.0 + 3.0 * GELU_K * ylin * ylin))
            for k in range(N_SLAB):
                dyl_ref[k, rs, :] = dylin[:, k * SLAB_IN:(k + 1) * SLAB_IN]
            vec_ref[3:4, 0:D_SSM] += jnp.sum(dylin * u_blk, axis=0, keepdims=True)

        rows_of(slice(0, tb))

    tok = lambda w: pl.BlockSpec((tb, w), lambda i: (i, 0))
    slab = pl.BlockSpec((N_SLAB, tb, SLAB_IN), lambda i: (0, i, 0))
    const = lambda r, c: pl.BlockSpec((r, c), lambda i: (0, 0), pipeline_mode=pl.Buffered(1))
    tok_shape = jax.ShapeDtypeStruct((seq, 512), F32)
    return pl.pallas_call(
        body, name="mid", grid=(seq // tb,),
        in_specs=[tok(512), tok(512), slab, slab, slab, tok(512), tok(1024), tok(1024),
                  const(1, 512), const(512, 512), const(1, 512), const(1, 512), const(1, 512),
                  const(1024, 1024), const(1, 1024), const(1, 1024)],
        out_specs=[const(8, 128), tok(512), tok(512), slab, tok(512), tok(1024),
                   const(1024, 1024), const(512, 512), const(8, 1024)],
        out_shape=[jax.ShapeDtypeStruct((8, 128), F32), tok_shape, tok_shape,
                   jax.ShapeDtypeStruct((N_SLAB, seq, SLAB_IN), F32), tok_shape,
                   jax.ShapeDtypeStruct((seq, 1024), F32), jax.ShapeDtypeStruct((1024, 1024), F32),
                   jax.ShapeDtypeStruct((512, 512), F32), jax.ShapeDtypeStruct((8, 1024), F32)],
        scratch_shapes=[pltpu.VMEM((D_MODEL, D_MODEL), BF16)],
        compiler_params=pltpu.CompilerParams(dimension_semantics=("arbitrary",), vmem_limit_bytes=MID_VMEM),
    )(o, za, u, y_f, y_b, zs, x, target, ssm_d, w_glu, b_glu, g_attn, g_ssm, w_out, ln_g, ln_b)


def _ride_shapes(pieces, narrow, gather_last):
    outs = [p.shape if (gather_last and a == len(pieces) - 1) else p.shape[1:] for a, p in enumerate(pieces)]
    return outs, [pltpu.VMEM(s, F32) for s in outs] + _reduce_scratch([p.shape for p in pieces], narrow)


def _ride_phases(piece_refs, out_refs, scratch_refs, narrow, gather_last):
    n = len(piece_refs)
    landing, rest = scratch_refs[:n], scratch_refs[n:]
    begin, exchange, combine, finish = _reduce_phases(piece_refs, landing, rest[:n], rest[n:2 * n], rest[2 * n:3 * n],
                                                      *rest[3 * n:], narrow, gather_last)

    def end():
        finish()
        for a in range(n):
            out_refs[a][...] = landing[a][...]

    return begin, exchange, combine, end


def _dproj_block(dq_ref, dk_ref, dv_ref, dza_ref, duf_ref, dub_ref, dyl_ref, dzs_ref, d_ref, hi_ref, lo_ref, tb):
    cos, sin = _rope_block(hi_ref, lo_ref, pl.program_id(0) * (tb // ROPE_GROUP), tb // ROPE_GROUP)
    lo = lax.broadcasted_iota(jnp.int32, (tb, 128), 1) < HEAD_DIM

    def unrope(t):
        return t * cos + _rotate_half_unsigned(t * sin)

    def natural(pairs):
        swapped = [pltpu.roll(t, HEAD_DIM, 1) for t in pairs]
        return [jnp.where(lo, pairs[0], swapped[1]), jnp.where(lo, pairs[2], swapped[3]),
                jnp.where(lo, swapped[0], pairs[1]), jnp.where(lo, swapped[2], pairs[3])]

    dq_rot, dza = dq_ref[...], dza_ref[...]
    pieces = natural([unrope(dq_rot[:, 128 * j:128 * (j + 1)]) for j in range(4)])
    d_row = d_ref[...]
    pieces += [unrope(dk_ref[...]), dv_ref[...]] + natural([dza[:, 128 * j:128 * (j + 1)] for j in range(4)])
    pieces += [duf_ref[k] + dub_ref[k] + d_row[:, k * SLAB_IN:(k + 1) * SLAB_IN] * dyl_ref[k] for k in range(N_SLAB)]
    pieces += [dzs_ref[...]]
    return jnp.concatenate(pieces, axis=1).astype(BF16)


def _dproj_specs(tb, rope_hi, rope_lo):
    tok = lambda w: pl.BlockSpec((tb, w), lambda i: (i, 0))
    slab = pl.BlockSpec((N_SLAB, tb, SLAB_IN), lambda i: (0, i, 0))
    table = lambda t: pl.BlockSpec(t.shape, lambda i: (0, 0, 0))
    return [tok(512), tok(128), tok(128), tok(512), slab, slab, slab, tok(512), pl.BlockSpec((1, 512), lambda i: (0, 0)),
            table(rope_hi), table(rope_lo)]


N_DPROJ = 11
GW_ROWS = 768


def _proj_bwd_w(x, dproj_args, rope_hi, rope_lo, pieces, tb):
    seq = x.shape[0]
    steps = seq // tb
    n_p = len(pieces)
    narrow = [False] * n_p

    def body(*refs):
        x_ref, grads = refs[0], refs[1:1 + N_DPROJ]
        piece_refs = refs[1 + N_DPROJ:1 + N_DPROJ + n_p]
        gw_ref = refs[1 + N_DPROJ + n_p]
        out_refs = refs[2 + N_DPROJ + n_p:2 + N_DPROJ + 2 * n_p]
        step = pl.program_id(0)
        if n_p:
            begin, exchange, combine, end = _ride_phases(piece_refs, out_refs, refs[2 + N_DPROJ + 2 * n_p:], narrow, True)
            pl.when(step == 0)(begin)
            pl.when(step == min(1, steps - 1))(exchange)
            pl.when(step == steps // 2)(combine)

        @pl.when(step == 0)
        def _():
            gw_ref[...] = jnp.zeros_like(gw_ref)

        dproj = _dproj_block(*grads, tb)
        xb = x_ref[...].astype(BF16)
        for r0 in range(0, D_IN_PROJ, GW_ROWS):
            gw_ref[r0:r0 + GW_ROWS, :] += _dot_tn(dproj[:, r0:r0 + GW_ROWS], xb)
        if n_p:
            pl.when(step == steps - 1)(end)

    vmem = pl.BlockSpec(memory_space=pltpu.VMEM)
    whole = pl.BlockSpec((D_IN_PROJ, D_MODEL), lambda i: (0, 0), pipeline_mode=pl.Buffered(1))
    ride_outs, ride_scratch = _ride_shapes(pieces, narrow, True) if n_p else ([], [])
    return pl.pallas_call(
        body, name="proj_bwd_w", grid=(steps,),
        in_specs=[pl.BlockSpec((tb, D_MODEL), lambda i: (i, 0))] + _dproj_specs(tb, rope_hi, rope_lo) + [vmem] * n_p,
        out_specs=[whole] + [vmem] * n_p,
        out_shape=[jax.ShapeDtypeStruct((D_IN_PROJ, D_MODEL), F32)] + [jax.ShapeDtypeStruct(s, F32) for s in ride_outs],
        scratch_shapes=ride_scratch,
        compiler_params=_cparams(("arbitrary",)),
    )(x, *dproj_args, rope_hi, rope_lo, *pieces)


def _proj_bwd_x(dproj_args, rope_hi, rope_lo, dpre, wt, pieces, tb):
    seq = dpre.shape[0]
    steps = seq // tb
    n_p = len(pieces)
    narrow = [True] * n_p

    def body(*refs):
        grads = refs[:N_DPROJ]
        dpre_ref, wt_ref = refs[N_DPROJ:N_DPROJ + 2]
        piece_refs = refs[N_DPROJ + 2:N_DPROJ + 2 + n_p]
        gx_ref = refs[N_DPROJ + 2 + n_p]
        out_refs = refs[N_DPROJ + 3 + n_p:N_DPROJ + 3 + 2 * n_p]
        step = pl.program_id(0)
        if n_p:
            begin, exchange, combine, end = _ride_phases(piece_refs, out_refs, refs[N_DPROJ + 3 + 2 * n_p:], narrow, False)
            pl.when(step == 0)(begin)
            pl.when(step == min(1, steps - 1))(exchange)
            pl.when(step == steps - 1)(combine)

        dproj = _dproj_block(*grads, tb)
        gx_ref[...] = ALPHA * dpre_ref[...] + _dot(dproj, wt_ref[...])
        if n_p:
            pl.when(step == steps - 1)(end)

    vmem = pl.BlockSpec(memory_space=pltpu.VMEM)
    whole = pl.BlockSpec((D_IN_PROJ, D_MODEL), lambda i: (0, 0), pipeline_mode=pl.Buffered(1))
    ride_outs, ride_scratch = _ride_shapes(pieces, narrow, False) if n_p else ([], [])
    return pl.pallas_call(
        body, name="proj_bwd_x", grid=(steps,),
        in_specs=_dproj_specs(tb, rope_hi, rope_lo) + [pl.BlockSpec((tb, D_MODEL), lambda i: (i, 0)), whole] + [vmem] * n_p,
        out_specs=[pl.BlockSpec((tb, D_MODEL), lambda i: (i, 0))] + [vmem] * n_p,
        out_shape=[jax.ShapeDtypeStruct((seq, D_MODEL), F32)] + [jax.ShapeDtypeStruct(s, F32) for s in ride_outs],
        scratch_shapes=ride_scratch,
        compiler_params=pltpu.CompilerParams(dimension_semantics=("arbitrary",), vmem_limit_bytes=PROJ_BWD_X_VMEM),
    )(*dproj_args, rope_hi, rope_lo, dpre, wt, *pieces)


def _adamw(w, g, m, v, name):
    rows, cols = w.shape
    tb = rows
    while tb * cols * 4 > ADAMW_BLOCK_BYTES and tb % 16 == 0:
        tb //= 2

    def body(w_ref, g_ref, m_ref, v_ref, d_ref, nm_ref, nv_ref):
        _adamw_update(w_ref, g_ref, m_ref, v_ref, d_ref, nm_ref, nv_ref)

    spec = pl.BlockSpec((tb, cols), lambda i: (i, 0))
    return pl.pallas_call(
        body, name=name, grid=(rows // tb,), in_specs=[spec] * 4, out_specs=[spec] * 3,
        out_shape=[jax.ShapeDtypeStruct((rows, cols), F32)] * 3,
        compiler_params=_cparams(("arbitrary",)),
    )(w, g, m, v)


def _adamw_update(w_ref, g_ref, m_ref, v_ref, d_ref, nm_ref, nv_ref):
    g_blk = g_ref[...]
    m_new = ADAM_B1 * m_ref[...] + (1.0 - ADAM_B1) * g_blk
    v_new = ADAM_B2 * v_ref[...] + (1.0 - ADAM_B2) * (g_blk * g_blk)
    m_hat = m_new / (1.0 - ADAM_B1 ** ADAM_STEP)
    v_hat = v_new / (1.0 - ADAM_B2 ** ADAM_STEP)
    d_ref[...] = -ADAM_LR * (m_hat / (jnp.sqrt(v_hat) + ADAM_EPS) + ADAM_WD * w_ref[...])
    nm_ref[...] = m_new
    nv_ref[...] = v_new


def _adamw_many(groups, name):
    n = len(groups)

    def body(*refs):
        for p in range(n):
            w_ref, g_ref, m_ref, v_ref = refs[4 * p:4 * p + 4]
            gn_ref, d_ref, nm_ref, nv_ref = refs[4 * n + 4 * p:4 * n + 4 * p + 4]
            gn_ref[...] = g_ref[...].reshape(w_ref.shape)
            _adamw_update(w_ref, gn_ref, m_ref, v_ref, d_ref, nm_ref, nv_ref)

    return pl.pallas_call(
        body, name=name,
        out_shape=[jax.ShapeDtypeStruct(grp[0].shape, F32) for grp in groups for _ in range(4)],
    )(*[a for grp in groups for a in grp])


_WEIGHTS = ["w_in", "attn_sink", "ssm_a_re", "ssm_a_im", "ssm_log_dt", "ssm_b_re", "ssm_b_im", "ssm_c_re", "ssm_c_im",
            "ssm_d", "w_glu", "b_glu", "norm_attn_g", "norm_ssm_g", "w_out", "ln_g", "ln_b"]
N_DG = N_DIR * N_GROUPS
BIG_ROWS = N_DG * SSM_CH * SSM_STATE // 128
TINY_ROWS = 64


def _pack_small_grads(g_bc, g_vec, g_ar, g_ai, g_dt, g_sink, loss):
    big = jnp.stack([t.reshape(BIG_ROWS, 128) for t in g_bc])
    row = lambda t: jnp.pad(t.reshape(1, -1), ((0, 0), (0, 128 - t.size)))
    tiny = jnp.concatenate([g_vec.reshape(64, 128), g_ar.reshape(32, 128), g_ai.reshape(32, 128), row(g_dt), row(g_sink),
                            row(loss), jnp.zeros((N_CHIPS * TINY_ROWS - 131, 128), F32)], axis=0)
    return jnp.concatenate([big, tiny.reshape(N_CHIPS, TINY_ROWS, 128)], axis=1)


def _unpack_small_grads(packed):
    big = packed[:, :BIG_ROWS].reshape(N_CHIPS, 2 * BIG_ROWS, SSM_STATE)
    tiny = packed[:, BIG_ROWS:].reshape(N_CHIPS * TINY_ROWS, 128)
    g_vec = tiny[0:64].reshape(8, 1024)
    return tiny[130, 0], {
        "ssm_b_re": big[0], "ssm_b_im": big[1], "ssm_c_re": big[2], "ssm_c_im": big[3],
        "ln_g": g_vec[0:1], "ln_b": g_vec[1:2],
        "norm_attn_g": _from_pair_order(g_vec[2:3, :D_ATTN]), "norm_ssm_g": g_vec[2:3, D_ATTN:],
        "ssm_d": g_vec[3:4, :D_SSM], "b_glu": g_vec[3:4, D_SSM:],
        "ssm_a_re": tiny[64:96].reshape(N_DG, SSM_STATE), "ssm_a_im": tiny[96:128].reshape(N_DG, SSM_STATE),
        "ssm_log_dt": tiny[128:129, :N_DG].reshape(N_DIR, N_GROUPS), "attn_sink": tiny[129:130, :N_Q_HEADS],
    }


def _small_unview(name, t, shape):
    if name in ("ssm_b_re", "ssm_b_im"):
        return jnp.swapaxes(t.reshape(N_DIR, N_GROUPS, SSM_CH, SSM_STATE), 2, 3).reshape(shape)
    return t.reshape(shape)


def _channel_major(name, t):
    return jnp.swapaxes(t, 3, 4) if name in ("ssm_b_re", "ssm_b_im") else t


def kernel(x, w_in, attn_sink, ssm_a_re, ssm_a_im, ssm_log_dt, ssm_b_re, ssm_b_im, ssm_c_re, ssm_c_im, ssm_d, w_glu, b_glu, norm_attn_g, norm_ssm_g, w_out, ln_g, ln_b, loss_target, m_w_in, m_attn_sink, m_ssm_a_re, m_ssm_a_im, m_ssm_log_dt, m_ssm_b_re, m_ssm_b_im, m_ssm_c_re, m_ssm_c_im, m_ssm_d, m_w_glu, m_b_glu, m_norm_attn_g, m_norm_ssm_g, m_w_out, m_ln_g, m_ln_b, v_w_in, v_attn_sink, v_ssm_a_re, v_ssm_a_im, v_ssm_log_dt, v_ssm_b_re, v_ssm_b_im, v_ssm_c_re, v_ssm_c_im, v_ssm_d, v_w_glu, v_b_glu, v_norm_attn_g, v_norm_ssm_g, v_w_out, v_ln_g, v_ln_b):
    args = dict(locals())
    weights = {n: args[n] for n in _WEIGHTS}
    mom_m = {n: args["m_" + n] for n in _WEIGHTS}
    mom_v = {n: args["v_" + n] for n in _WEIGHTS}
    xs = x[0]
    target = loss_target[0]

    (wt_g,) = _all_gather_chips([w_in[0].T], BF16, "gather_weights")
    wt_full = wt_g.reshape(D_IN_PROJ, D_MODEL)

    g_x, r_wt, r_w_out, r_w_glu, g_small_all = _local_step(
        xs, target, wt_full, w_glu[0], w_out[0], attn_sink, ssm_a_re, ssm_a_im, ssm_log_dt, ssm_b_re, ssm_b_im,
        ssm_c_re, ssm_c_im, ssm_d, b_glu, norm_attn_g, norm_ssm_g, ln_g, ln_b, sharded=True)
    loss, small_grads = _unpack_small_grads(g_small_all)

    grads, deltas, new_m, new_v = {}, {}, {}, {}
    d_w, m_w, v_w = _adamw(w_in[0].T, r_wt, m_w_in[0].T, v_w_in[0].T, "adamw_w_in")
    grads["w_in"], deltas["w_in"], new_m["w_in"], new_v["w_in"] = r_wt.T[None], d_w.T[None], m_w.T[None], v_w.T[None]
    for n, g in (("w_out", r_w_out), ("w_glu", r_w_glu)):
        d_w, m_w, v_w = _adamw(weights[n][0], g, mom_m[n][0], mom_v[n][0], "adamw_" + n)
        grads[n], deltas[n], new_m[n], new_v[n] = g[None], d_w[None], m_w[None], v_w[None]
    names = sorted(small_grads)
    updates = _adamw_many([(_channel_major(n, weights[n]), small_grads[n], _channel_major(n, mom_m[n]),
                            _channel_major(n, mom_v[n])) for n in names], "adamw_small")
    for i, n in enumerate(names):
        grads[n], deltas[n], new_m[n], new_v[n] = (_channel_major(n, t) for t in updates[4 * i:4 * i + 4])

    return (loss, g_x[None], *[grads[n] for n in _WEIGHTS], *[deltas[n] for n in _WEIGHTS],
            *[new_m[n] for n in _WEIGHTS], *[new_v[n] for n in _WEIGHTS])


def _local_step(xs, target, wt_full, w_glu_in, w_out_in, attn_sink, ssm_a_re, ssm_a_im, ssm_log_dt, ssm_b_re,
                ssm_b_im, ssm_c_re, ssm_c_im, ssm_d, b_glu, norm_attn_g, norm_ssm_g, ln_g, ln_b, sharded):
    seq = xs.shape[0]

    a_r, a_i = ssm_a_re, ssm_a_im
    log_dt = ssm_log_dt.reshape(N_DG, 1)
    b_r, b_i = _channel_major("ssm_b_re", ssm_b_re), _channel_major("ssm_b_im", ssm_b_im)
    c_r, c_i = ssm_c_re, ssm_c_im
    ssm_tb = min(SSM_BLOCK, seq)
    sub_len = ssm_tb // SUBSEG
    lam, bb, bbt, cb, cb_t = _ssm_params_fwd(a_r, a_i, log_dt, b_r, b_i, c_r, c_i, int(math.log2(sub_len)))
    lam = lam.reshape(4, N_DIR, 1, STATE_W)

    rope_hi, rope_lo = _rope_tables(seq)
    projected = _proj(xs, wt_full, rope_hi, rope_lo, [w_glu_in, w_out_in] if sharded else [], min(512, seq))
    q_stack, k_rot, v_bf, z_attn, u, z_ssm = projected[:6]
    if sharded:
        w_glu_full, w_out_full = projected[6].reshape(D_SSM, D_SSM), projected[7].reshape(D_MODEL, D_MODEL)
    else:
        w_glu_full, w_out_full = w_glu_in, w_out_in
    sink128 = jnp.broadcast_to(attn_sink[0][:, None, None], (N_Q_HEADS, 1, 128))
    attn_bias = _attn_bias()
    o = _attn_fwd(q_stack, k_rot, v_bf, sink128, attn_bias)
    ys, starts = [], []
    for d in range(N_DIR):
        y_d, s_r, s_i = _ssm_fwd(u, lam, bb, cb, direction=d, tb=ssm_tb, name=f"ssm_fwd_{d}")
        ys.append(y_d)
        starts.append((s_r, s_i))

    row = lambda t: t.reshape(1, -1)
    g_attn_p = _to_pair_order(norm_attn_g)
    loss_blk, d_o, d_za, d_ylin, d_zs, d_pre, g_w_out, g_w_glu, g_vec = _mid(
        o, z_attn, u, ys[0], ys[1], z_ssm, xs, target, row(ssm_d), w_glu_full, row(b_glu),
        g_attn_p, row(norm_ssm_g), w_out_full, row(ln_g), row(ln_b), min(MID_BLOCK, seq))

    pieces = [g_w_glu.reshape(N_CHIPS, -1, D_SSM), g_w_out.reshape(N_CHIPS, -1, D_MODEL)] if sharded else []
    attn_grads = _attn_bwd(q_stack, k_rot, v_bf, sink128, attn_bias, d_o, pieces)
    dq, dk, dv, g_sink = attn_grads[:4]
    if sharded:
        g_w_glu, g_w_out = attn_grads[4:]
    dus, g_bb, g_cb, g_lam = [], [], [], []
    for d in range(N_DIR):
        du_d, gb_d, gc_d, dl_d = _ssm_bwd(u, d_ylin, starts[d], lam, bb, bbt, cb_t, direction=d, tb=ssm_tb,
                                          name=f"ssm_bwd_{d}")
        dus.append(du_d)
        g_bb.append(gb_d)
        g_cb.append(gc_d)
        g_lam.append(dl_d)
    g_ar, g_ai, g_dt, g_br, g_bi, g_cr, g_ci = _ssm_params_bwd(a_r, a_i, log_dt, b_r, b_i, g_bb, g_cb, g_lam)

    g_small = _pack_small_grads([g_br, g_bi, g_cr, g_ci], g_vec, g_ar, g_ai, g_dt, g_sink[:, 0], loss_blk[0, 0])
    dproj_args = (dq, dk, dv, d_za, dus[0], dus[1], d_ylin, d_zs, row(ssm_d))
    w_grads = _proj_bwd_w(xs, dproj_args, rope_hi, rope_lo, [g_small] if sharded else [], min(512, seq))
    g_wt = w_grads[0]
    if sharded:
        g_small = w_grads[1]
    x_grads = _proj_bwd_x(dproj_args, rope_hi, rope_lo, d_pre, wt_full,
                          [g_wt.reshape(N_CHIPS, -1, D_MODEL)] if sharded else [], min(512, seq))
    g_x = x_grads[0]
    if sharded:
        g_wt = x_grads[1]
    return g_x, g_wt, g_w_out, g_w_glu, g_small
```

```python
import functools
import math

import numpy as np
import jax
import jax.numpy as jnp
from jax import lax
from jax.experimental import pallas as pl
from jax.experimental.pallas import tpu as pltpu

F32 = jnp.float32
BF16 = jnp.bfloat16
MESH = pl.DeviceIdType.MESH

D_MODEL = 1024
D_ATTN = 512
D_SSM = 512
HEAD_DIM = 64
N_Q_HEADS = 8
WINDOW = 128
ROPE_THETA = 10000.0
SSM_CH = 16
N_GROUPS = 32
SSM_STATE = 64
N_DIR = 2
STATE_W = N_GROUPS * SSM_STATE
N_SLAB = 4
SLAB_IN = 128
SLAB_ST = 512
NORM_EPS = 1e-5
NEG_INF = -1e30
ALPHA = 2.0 ** 0.25
D_IN_PROJ = 2304
N_CHIPS = 4

ADAM_LR = 0.001
ADAM_B1 = 0.9
ADAM_B2 = 0.999
ADAM_EPS = 1e-08
ADAM_WD = 0.01
ADAM_STEP = 10

SUBSEG = 8
SCAN_LANES = 512
SSM_BLOCK = 512
VMEM_LIMIT = 48 * 1024 * 1024
ADAMW_BLOCK_BYTES = 3 * 512 * 1024
PROJ_BWD_X_VMEM = 56 * 1024 * 1024
MID_VMEM = 60 * 1024 * 1024
MID_BLOCK = 512

def _to_pair_order(row):
    return jnp.transpose(row.reshape(2, 4, HEAD_DIM), (1, 0, 2)).reshape(1, D_ATTN)


def _from_pair_order(row):
    return jnp.transpose(row.reshape(4, 2, HEAD_DIM), (1, 0, 2)).reshape(1, D_ATTN)


def _cparams(sem=None):
    return pltpu.CompilerParams(dimension_semantics=sem, vmem_limit_bytes=VMEM_LIMIT)


def _dot(a, b):
    return jnp.dot(a, b, preferred_element_type=F32)


def _dot_nt(a, b):
    return lax.dot_general(a, b, (((1,), (1,)), ((), ())), preferred_element_type=F32)


def _dot_tn(a, b):
    return lax.dot_general(a, b, (((0,), (0,)), ((), ())), preferred_element_type=F32)


def _sigmoid(z):
    return 0.5 * jnp.tanh(0.5 * z) + 0.5


def _gather_sems(n):
    return [pltpu.SemaphoreType.DMA((6 * n,)), pltpu.SemaphoreType.DMA((6 * n,))]


def _gather_phases(in_refs, out_refs, send_sems, recv_sems, out_dtype):
    n = len(in_refs)
    x, y, c = lax.axis_index("x"), lax.axis_index("y"), lax.axis_index("c")
    sibling = (x, y, 1 - c)
    chips = [(1 - x, y), (x, 1 - y), (1 - x, 1 - y)]

    def half_of(a, px, py, half):
        rows = in_refs[a].shape[0] // 2
        return out_refs[a].at[2 * px + py, pl.ds(half * rows, rows), :]

    def copy(a, k, px, py, half, to):
        blk = half_of(a, px, py, half)
        return pltpu.make_async_remote_copy(src_ref=blk, dst_ref=blk, send_sem=send_sems.at[6 * a + k],
                                            recv_sem=recv_sems.at[6 * a + k], device_id=to, device_id_type=MESH)

    first = [copy(a, j, x, y, c, (*chips[j], c)) for a in range(n) for j in range(3)]
    passed = [copy(a, 3 + j, *chips[j], c, sibling) for a in range(n) for j in range(3)]

    def start():
        for a in range(n):
            out_refs[a][2 * x + y] = in_refs[a][...].astype(out_dtype)
        for cp in first:
            cp.start()

    def relay():
        for a in range(n):
            for j in range(3):
                copy(a, j, *chips[j], c, (x, y, c)).wait_recv()
                passed[3 * a + j].start()

    def finish():
        for a in range(n):
            for j in range(3):
                copy(a, 3 + j, *chips[j], 1 - c, (x, y, c)).wait_recv()
        for cp in first + passed:
            cp.wait_send()

    return start, relay, finish


SEMS_PER_ARRAY = 14


def _reduce_scratch(shapes, narrow):
    half = [(N_CHIPS, s[1] // 2, s[2]) for s in shapes]
    wire = [BF16 if nar else F32 for nar in narrow]
    n = len(shapes)
    return ([pltpu.VMEM(half[a], F32) for a in range(n)] + [pltpu.VMEM(half[a], wire[a]) for a in range(n)]
            + [pltpu.VMEM(half[a], wire[a]) for a in range(n)]
            + [pltpu.SemaphoreType.DMA((SEMS_PER_ARRAY * n,)), pltpu.SemaphoreType.DMA((SEMS_PER_ARRAY * n,))])


def _reduce_phases(p_refs, out_refs, a_refs, s_refs, b_refs, send_sems, recv_sems, narrow, gather_last):
    n = len(p_refs)
    halves = [p.shape[1] // 2 for p in p_refs]
    wire = [BF16 if nar else F32 for nar in narrow]
    x, y, c = lax.axis_index("x"), lax.axis_index("y"), lax.axis_index("c")
    me = 2 * x + y
    sibling = (x, y, 1 - c)
    chips = [(1 - x, y), (x, 1 - y), (1 - x, 1 - y)]
    slot = [2 * px + py for px, py in chips]
    last = n - 1

    def copy(a, k, src, dst, to):
        return pltpu.make_async_remote_copy(src_ref=src, dst_ref=dst, send_sem=send_sems.at[SEMS_PER_ARRAY * a + k],
                                            recv_sem=recv_sems.at[SEMS_PER_ARRAY * a + k],
                                            device_id=to, device_id_type=MESH)

    def rows(a, half):
        return pl.ds(pl.multiple_of(half * halves[a], 16), halves[a])

    def finished(a, k, half):
        if gather_last and a == last:
            return out_refs[a].at[k, rows(a, half), :]
        return out_refs[a].at[rows(a, half), :]

    order = slot + [me]
    swaps = [[copy(a, q, p_refs[a].at[order[q], rows(a, 1 - c), :], a_refs[a].at[order[q]], sibling)
              for q in range(N_CHIPS)] for a in range(n)]
    sends = [[copy(a, 4 + j, s_refs[a].at[slot[j]], b_refs[a].at[me], (*chips[j], c)) for j in range(3)] for a in range(n)]
    backs = [copy(a, 7, finished(a, me, c), finished(a, me, c), sibling) for a in range(n)]
    spread = [copy(last, 8 + j, finished(last, me, c), finished(last, me, c), (*chips[j], c)) for j in range(3)]
    relays = [copy(last, 11 + j, finished(last, slot[j], c), finished(last, slot[j], c), sibling) for j in range(3)]

    def start():
        for group in swaps:
            for cp in group:
                cp.start()

    def exchange():
        for a in range(n):
            for q in range(N_CHIPS):
                swaps[a][q].wait_recv()
                acc = a_refs[a][order[q]] + p_refs[a][order[q], rows(a, c), :]
                a_refs[a][order[q]] = acc
                s_refs[a][order[q]] = acc.astype(wire[a])
                if q < 3:
                    sends[a][q].start()
            b_refs[a][me] = s_refs[a][me]

    def combine():
        for a in range(n):
            for j in range(3):
                copy(a, 4 + j, s_refs[a].at[slot[j]], b_refs[a].at[slot[j]], (x, y, c)).wait_recv()
            terms = [jnp.where(me == k, a_refs[a][k], b_refs[a][k].astype(F32)) for k in range(N_CHIPS)]
            total = (terms[0] + terms[1]) + (terms[2] + terms[3])
            if gather_last and a == last:
                out_refs[a][me, rows(a, c), :] = total
            else:
                out_refs[a][rows(a, c), :] = total
            backs[a].start()
        if gather_last:
            for cp in spread:
                cp.start()

    def finish():
        if gather_last:
            for j in range(3):
                copy(last, 8 + j, finished(last, slot[j], c), finished(last, slot[j], c), (x, y, c)).wait_recv()
                relays[j].start()
        for a in range(n):
            copy(a, 7, finished(a, me, 1 - c), finished(a, me, 1 - c), (x, y, c)).wait_recv()
        if gather_last:
            for j in range(3):
                copy(last, 11 + j, finished(last, slot[j], 1 - c), finished(last, slot[j], 1 - c), (x, y, c)).wait_recv()
        started = [cp for group in swaps + sends for cp in group] + backs + (spread + relays if gather_last else [])
        for cp in started:
            cp.wait_send()

    return start, exchange, combine, finish


def _ssm_param_values(ar, ai, logdt):
    dt = jnp.exp(logdt)
    mag = jnp.exp(dt * ar)
    cs, sn = jnp.cos(dt * ai), jnp.sin(dt * ai)
    lr, li = mag * cs, mag * sn
    den = ar * ar + ai * ai
    nr = (lr - 1.0) * ar + li * ai
    ni = li * ar - (lr - 1.0) * ai
    return dt, mag, lr, li, den, nr, ni


GROUPS_PER_SLAB = N_GROUPS // N_SLAB


def _slab_masks():
    def eq(shape, f_row, f_col):
        return (f_row(lax.broadcasted_iota(jnp.int32, shape, 0)) == f_col(lax.broadcasted_iota(jnp.int32, shape, 1))).astype(F32)
    spread = eq((SSM_STATE, SLAB_ST), lambda r: r, lambda c: c % SSM_STATE)
    spread_t = eq((SLAB_ST, SSM_STATE), lambda r: r % SSM_STATE, lambda c: c)
    keep = eq((SLAB_IN, SLAB_ST), lambda r: r // SSM_CH, lambda c: c // SSM_STATE)
    keep_t = eq((SLAB_ST, SLAB_IN), lambda r: r // SSM_STATE, lambda c: c // SSM_CH)
    repeat = eq((N_DG * SSM_CH, N_DG), lambda r: r // SSM_CH, lambda c: c)
    return spread, spread_t, keep, keep_t, repeat


def _rows(ref):
    return ref[...].reshape(-1, SSM_STATE)


def _split3(t):
    hi = t.astype(BF16)
    rest = t - hi.astype(F32)
    mid = rest.astype(BF16)
    return hi, mid, (rest - mid.astype(F32)).astype(BF16)


def _select(dot, ones01, t, ones_first):
    o = ones01.astype(BF16)
    parts = [dot(o, p) if ones_first else dot(p, o) for p in _split3(t)]
    return (parts[0] + parts[1]) + parts[2]


def _ssm_params_fwd(ar, ai, logdt, br, bi, cr, ci, n_square, shards):
    n = len(shards)

    def body(*refs):
        ar_ref, ai_ref, dt_ref, br_ref, bi_ref, cr_ref, ci_ref = refs[:7]
        lam_ref, bb_ref, bbt_ref, cb_ref, cbt_ref = refs[7 + n:12 + n]
        if n:
            start, relay, finish = _gather_phases(refs[7:7 + n], refs[12 + n:12 + 2 * n], *refs[12 + 2 * n:], BF16)
            start()
        _, _, lr, li, den, nr, ni = _ssm_param_values(_rows(ar_ref), _rows(ai_ref), dt_ref[...])
        lam_ref[0] = lr
        lam_ref[1] = li
        pr, pi = lr, li
        for _ in range(n_square):
            pr, pi = pr * pr - pi * pi, 2.0 * pr * pi
        lam_ref[2] = pr
        lam_ref[3] = pi
        spread, spread_t, keep, keep_t, repeat = _slab_masks()
        fr = _select(_dot, repeat, nr / den, True)
        fi = _select(_dot, repeat, ni / den, True)
        b_r, b_i = _rows(br_ref), _rows(bi_ref)
        bbar = (fr * b_r - fi * b_i, fr * b_i + fi * b_r)
        c_par = (_rows(cr_ref), _rows(ci_ref))
        spread, spread_t = spread.astype(BF16), spread_t.astype(BF16)
        for src, wide_ref, tall_ref in ((bbar, bb_ref, bbt_ref), (c_par, cbt_ref, cb_ref)):
            for q in range(2):
                for d in range(N_DIR):
                    for k in range(N_SLAB):
                        r0 = (d * N_GROUPS + k * GROUPS_PER_SLAB) * SSM_CH
                        blk = src[q][r0:r0 + SLAB_IN].astype(BF16)
                        wide_ref[q, d, k] = (_dot(blk, spread) * keep).astype(BF16)
                        tall_ref[q, d, k] = (_dot_nt(spread_t, blk) * keep_t).astype(BF16)
        if n:
            relay()
            finish()

    wide = jax.ShapeDtypeStruct((2, N_DIR, N_SLAB, SLAB_IN, SLAB_ST), BF16)
    tall = jax.ShapeDtypeStruct((2, N_DIR, N_SLAB, SLAB_ST, SLAB_IN), BF16)
    vmem = pl.BlockSpec(memory_space=pltpu.VMEM)
    return pl.pallas_call(body, name="ssm_params_fwd",
                          out_shape=[jax.ShapeDtypeStruct((4, N_DG, SSM_STATE), F32), wide, tall, tall, wide]
                          + [jax.ShapeDtypeStruct((N_CHIPS,) + s.shape, BF16) for s in shards],
                          in_specs=[vmem] * (7 + n), out_specs=[vmem] * (5 + n),
                          scratch_shapes=_gather_sems(n) if n else [],
                          compiler_params=pltpu.CompilerParams(vmem_limit_bytes=VMEM_LIMIT),
                          )(ar, ai, logdt, br, bi, cr, ci, *shards)


def _ssm_params_bwd(ar, ai, logdt, br, bi, g_slabs_b, g_slabs_c, g_lam):
    def body(ar_ref, ai_ref, dt_ref, br_ref, bi_ref, gb0_ref, gb1_ref, gc0_ref, gc1_ref, gl0_ref, gl1_ref,
             gar_ref, gai_ref, gdt_ref, gbr_ref, gbi_ref, gcr_ref, gci_ref, dbb, dlam):
        spread, spread_t, keep, _, repeat = _slab_masks()
        for d, (gb_ref, gc_ref) in enumerate(((gb0_ref, gc0_ref), (gb1_ref, gc1_ref))):
            for q in range(2):
                for k in range(N_SLAB):
                    r0 = (d * N_GROUPS + k * GROUPS_PER_SLAB) * SSM_CH
                    dbb[q, r0:r0 + SLAB_IN, :] = _select(_dot, spread_t, gb_ref[q, k] * keep, False)
                    out_ref = gcr_ref if q == 0 else gci_ref
                    out_ref[r0:r0 + SLAB_IN, :] = _select(_dot, spread_t, gc_ref[q, k] * keep, False)
        grp = (lax.broadcasted_iota(jnp.int32, (N_GROUPS, STATE_W), 0)
               == lax.broadcasted_iota(jnp.int32, (N_GROUPS, STATE_W), 1) // SSM_STATE).astype(F32)
        pick = (lax.broadcasted_iota(jnp.int32, (STATE_W, SSM_STATE), 0) % SSM_STATE
                == lax.broadcasted_iota(jnp.int32, (STATE_W, SSM_STATE), 1)).astype(F32)
        for d, gl_ref in enumerate((gl0_ref, gl1_ref)):
            for q in range(2):
                row = jnp.sum(gl_ref[q], axis=0, keepdims=True)
                dlam[q, d * N_GROUPS:(d + 1) * N_GROUPS, :] = _select(_dot, pick, grp * row, False)

        a_r, a_i = _rows(ar_ref), _rows(ai_ref)
        dt, mag, lr, li, den, nr, ni = _ssm_param_values(a_r, a_i, dt_ref[...])
        fr = _select(_dot, repeat, nr / den, True)
        fi = _select(_dot, repeat, ni / den, True)
        b_r, b_i = _rows(br_ref), _rows(bi_ref)
        g_r, g_i = dbb[0], dbb[1]
        gbr_ref[...] = fr * g_r + fi * g_i
        gbi_ref[...] = fr * g_i - fi * g_r
        d_fr = _select(_dot_tn, repeat, b_r * g_r + b_i * g_i, True)
        d_fi = _select(_dot_tn, repeat, b_r * g_i - b_i * g_r, True)
        d_nr, d_ni = d_fr / den, d_fi / den
        d_den = -(d_fr * nr + d_fi * ni) / (den * den)
        d_lr = dlam[0] + d_nr * a_r - d_ni * a_i
        d_li = dlam[1] + d_nr * a_i + d_ni * a_r
        d_ar = d_nr * (lr - 1.0) + d_ni * li + d_den * 2.0 * a_r
        d_ai = d_nr * li - d_ni * (lr - 1.0) + d_den * 2.0 * a_i
        d_mag = (d_lr * lr + d_li * li) / mag
        d_theta = d_li * lr - d_lr * li
        gar_ref[...] = d_ar + d_mag * mag * dt
        gai_ref[...] = d_ai + d_theta * dt
        d_dt = d_mag * mag * a_r + d_theta * a_i
        gdt_ref[...] = jnp.sum(d_dt, axis=1, keepdims=True) * dt

    small = jax.ShapeDtypeStruct((N_DG, SSM_STATE), F32)
    big = jax.ShapeDtypeStruct((N_DG * SSM_CH, SSM_STATE), F32)
    return pl.pallas_call(
        body, name="ssm_params_bwd",
        out_shape=[small, small, jax.ShapeDtypeStruct(logdt.shape, F32), big, big, big, big],
        scratch_shapes=[pltpu.VMEM((2,) + big.shape, F32), pltpu.VMEM((2,) + small.shape, F32)],
        compiler_params=pltpu.CompilerParams(vmem_limit_bytes=VMEM_LIMIT),
    )(ar, ai, logdt, br, bi, *g_slabs_b, *g_slabs_c, *g_lam)


ROPE_GROUP = 128


def _rope_tables(seq):
    half = HEAD_DIM // 2
    inv_freq = jnp.tile(ROPE_THETA ** (-jnp.arange(half, dtype=F32) / half), 4)
    sign = jnp.tile(jnp.concatenate([-jnp.ones((half,), F32), jnp.ones((half,), F32)]), 2)

    def table(pos):
        ang = pos.astype(F32)[:, None] * inv_freq[None, :]
        return jnp.stack([jnp.cos(ang), jnp.sin(ang), sign * jnp.sin(ang)])

    return table(jnp.arange(seq // ROPE_GROUP) * ROPE_GROUP), table(jnp.arange(ROPE_GROUP))


def _rope_block(hi_ref, lo_ref, first_group, n_groups):
    cl, sl, sl_s = lo_ref[0], lo_ref[1], lo_ref[2]
    cos, sin = [], []
    for g in range(n_groups):
        ch, sh, sh_s = (hi_ref[q, pl.ds(first_group + g, 1), :] for q in range(3))
        cos.append(ch * cl - sh * sl)
        sin.append(sh_s * cl + ch * sl_s)
    return jnp.concatenate(cos, axis=0), jnp.concatenate(sin, axis=0)


def _rotate_half_unsigned(t):
    lane = lax.broadcasted_iota(jnp.int32, t.shape, 1)
    return jnp.where((lane % HEAD_DIM) < HEAD_DIM // 2, pltpu.roll(t, 96, 1), pltpu.roll(t, 32, 1))


def _rope(t, cos, sin_signed):
    return t * cos + _rotate_half_unsigned(t) * sin_signed


def _pair_blocks(base):
    out = []
    for j in range(4):
        for g in range(2):
            nat = base + HEAD_DIM * (4 * g + j)
            par = base + 128 * j + HEAD_DIM * g
            out.append((slice(nat, nat + HEAD_DIM), slice(par, par + HEAD_DIM)))
    return out


W_Q, W_KV, W_ZA, W_U, W_ZS = 0, 512, 768, 1280, 1792


def _proj(x, wt, rope_hi, rope_lo, shards, tb):
    seq = x.shape[0]
    steps = seq // tb
    n_sh = len(shards)

    def body(*refs):
        x_ref, wt_ref, hi_ref, lo_ref = refs[:4]
        shard_refs = refs[4:4 + n_sh]
        q_ref, k_ref, v_ref, za_ref, u_ref, zs_ref = refs[4 + n_sh:10 + n_sh]
        gathered_refs = refs[10 + n_sh:10 + 2 * n_sh]
        wp = refs[10 + 2 * n_sh]
        step = pl.program_id(0)
        if n_sh:
            landing_refs = refs[11 + 2 * n_sh:11 + 3 * n_sh]
            start, relay, finish = _gather_phases(shard_refs, landing_refs, *refs[11 + 3 * n_sh:], BF16)
            pl.when(step == 0)(start)
            pl.when(step == max(steps - 2, 0))(relay)

        @pl.when(step == 0)
        def _():
            for dst_base, src_base in ((0, W_Q), (512, W_ZA)):
                for nat, par in _pair_blocks(0):
                    wp[dst_base + par.start:dst_base + par.stop, :] = wt_ref[src_base + nat.start:src_base + nat.stop, :]

        xb = x_ref[...].astype(BF16)
        cos, sin = _rope_block(hi_ref, lo_ref, pl.program_id(0) * (tb // ROPE_GROUP), tb // ROPE_GROUP)
        lo = lax.broadcasted_iota(jnp.int32, (tb, 128), 1) < HEAD_DIM
        q = _dot_nt(xb, wp[0:512, :])
        for j in range(4):
            qj = _rope(q[:, 128 * j:128 * (j + 1)], cos, sin)
            q_ref[j] = jnp.where(lo, qj, 0.0).astype(BF16)
            q_ref[4 + j] = jnp.where(lo, 0.0, qj).astype(BF16)
        kv = _dot_nt(xb, wt_ref[W_KV:W_ZA, :])
        k_ref[...] = _rope(kv[:, 0:128], cos, sin).astype(BF16)
        v_ref[...] = kv[:, 128:256].astype(BF16)
        za_ref[...] = _dot_nt(xb, wp[512:1024, :])
        u_val = _dot_nt(xb, wt_ref[W_U:W_ZS, :])
        for k in range(N_SLAB):
            u_ref[k] = u_val[:, k * SLAB_IN:(k + 1) * SLAB_IN]
        zs_ref[...] = _dot_nt(xb, wt_ref[W_ZS:D_IN_PROJ, :])
        if n_sh:
            @pl.when(step == steps - 1)
            def _():
                finish()
                for a in range(n_sh):
                    gathered_refs[a][...] = landing_refs[a][...]

    row = lambda w: pl.BlockSpec((tb, w), lambda i: (i, 0))
    table = lambda t: pl.BlockSpec(t.shape, lambda i: (0, 0, 0))
    vmem = pl.BlockSpec(memory_space=pltpu.VMEM)
    return pl.pallas_call(
        body, name="proj", grid=(steps,),
        in_specs=[row(D_MODEL), pl.BlockSpec((D_IN_PROJ, D_MODEL), lambda i: (0, 0), pipeline_mode=pl.Buffered(1)),
                  table(rope_hi), table(rope_lo)] + [vmem] * n_sh,
        out_specs=[pl.BlockSpec((8, tb, 128), lambda i: (0, i, 0)), row(128), row(128), row(512),
                   pl.BlockSpec((N_SLAB, tb, SLAB_IN), lambda i: (0, i, 0)), row(512)] + [vmem] * n_sh,
        out_shape=[jax.ShapeDtypeStruct((8, seq, 128), BF16), jax.ShapeDtypeStruct((seq, 128), BF16),
                   jax.ShapeDtypeStruct((seq, 128), BF16), jax.ShapeDtypeStruct((seq, 512), F32),
                   jax.ShapeDtypeStruct((N_SLAB, seq, SLAB_IN), F32), jax.ShapeDtypeStruct((seq, 512), F32)]
        + [jax.ShapeDtypeStruct((N_CHIPS,) + s.shape, BF16) for s in shards],
        scratch_shapes=[pltpu.VMEM((1024, D_MODEL), BF16)] + [pltpu.VMEM((N_CHIPS,) + s.shape, BF16) for s in shards]
        + (_gather_sems(n_sh) if n_sh else []),
        compiler_params=_cparams(("arbitrary",)),
    )(x, wt, rope_hi, rope_lo, *shards)


ATT_TQ = 128
ATT_KEYS = ATT_TQ + 2 * WINDOW


def _attn_window(i, seq):
    start = jnp.clip(i * ATT_TQ - WINDOW, 0, seq - ATT_KEYS)
    return pl.multiple_of(start, WINDOW)


def _attn_bias():
    r = np.arange(ATT_TQ)[None, :, None]
    c = np.arange(ATT_KEYS)[None, None, :]
    off = np.array([0, WINDOW, ATT_KEYS - ATT_TQ])[:, None, None]
    return jnp.asarray(np.where(np.abs(r + off - c) <= WINDOW, 0.0, NEG_INF).astype(np.float32))


def _attn_bias_spec(nblk):
    pick = lambda i: jnp.where(i == 0, 0, jnp.where(i == nblk - 1, 2, 1))
    return pl.BlockSpec((None, ATT_TQ, ATT_KEYS), lambda i: (pick(i), 0, 0))


def _attn_softmax(q_ref, k_ref, v_ref, sink_ref, bias_ref, start):
    kw = k_ref[pl.ds(start, ATT_KEYS), :]
    vw = v_ref[pl.ds(start, ATT_KEYS), :]
    qall = q_ref[...].reshape(N_Q_HEADS * ATT_TQ, 128)
    s = (_dot_nt(qall, kw) * (HEAD_DIM ** -0.5)).reshape(N_Q_HEADS, ATT_TQ, ATT_KEYS) + bias_ref[...][None]
    tiles = [s[:, :, 128 * t:128 * (t + 1)] for t in range(ATT_KEYS // 128)]
    m = jnp.max(functools.reduce(jnp.maximum, tiles), axis=2, keepdims=True)
    sink = sink_ref[...]
    m_b = jnp.maximum(jnp.broadcast_to(m, (N_Q_HEADS, ATT_TQ, 128)), sink)
    p = jnp.concatenate([jnp.exp(t - m_b) for t in tiles], axis=2)
    p_sink = jnp.exp(sink - m_b)
    lo_k = lax.broadcasted_iota(jnp.int32, (ATT_KEYS, 128), 1) < HEAD_DIM
    v_f = vw.astype(F32)
    v_lo, v_hi = jnp.where(lo_k, v_f, 1.0).astype(BF16), jnp.where(lo_k, 1.0, v_f).astype(BF16)
    pb = p.astype(BF16).reshape(N_Q_HEADS * ATT_TQ, ATT_KEYS)
    half = 4 * ATT_TQ
    r = jnp.concatenate([_dot(pb[:half], v_lo), _dot(pb[half:], v_hi)], axis=0).reshape(N_Q_HEADS, ATT_TQ, 128)
    return kw, vw, qall, p, p_sink, r


def _attn_fwd(q_stack, k, v, sink128, bias):
    seq = k.shape[0]

    def body(q_ref, k_ref, v_ref, sink_ref, bias_ref, o_ref):
        start = _attn_window(pl.program_id(0), seq)
        _, _, _, _, p_sink, r = _attn_softmax(q_ref, k_ref, v_ref, sink_ref, bias_ref, start)
        out = r / (pltpu.roll(r, HEAD_DIM, 2) + p_sink)
        lo = lax.broadcasted_iota(jnp.int32, (ATT_TQ, 128), 1) < HEAD_DIM
        for j in range(4):
            o_ref[:, 128 * j:128 * (j + 1)] = jnp.where(lo, out[j], out[4 + j])

    full = lambda w: pl.BlockSpec((seq, w), lambda i: (0, 0))
    return pl.pallas_call(
        body, name="attn_fwd", grid=(seq // ATT_TQ,),
        in_specs=[pl.BlockSpec((8, ATT_TQ, 128), lambda i: (0, i, 0)), full(128), full(128),
                  pl.BlockSpec((N_Q_HEADS, 1, 128), lambda i: (0, 0, 0)), _attn_bias_spec(seq // ATT_TQ)],
        out_specs=pl.BlockSpec((ATT_TQ, 512), lambda i: (i, 0)),
        out_shape=jax.ShapeDtypeStruct((seq, 512), F32),
        compiler_params=_cparams(("arbitrary",)),
    )(q_stack, k, v, sink128, bias)


def _attn_bwd(q_stack, k, v, sink128, bias, d_o, pieces):
    seq = k.shape[0]
    steps = seq // ATT_TQ
    n_p = len(pieces)

    def body(*refs):
        q_ref, k_ref, v_ref, sink_ref, bias_ref, do_ref = refs[:6]
        piece_refs = refs[6:6 + n_p]
        dq_ref, dk_ref, dv_ref, dsink_ref = refs[6 + n_p:10 + n_p]
        reduced_refs = refs[10 + n_p:10 + 2 * n_p]
        sink_acc = refs[10 + 2 * n_p]
        i = pl.program_id(0)
        if n_p:
            landing_refs = refs[11 + 2 * n_p:11 + 3 * n_p]
            scratch = refs[11 + 3 * n_p:]
            begin, exchange, combine, finish = _reduce_phases(
                piece_refs, landing_refs, scratch[:n_p], scratch[n_p:2 * n_p], scratch[2 * n_p:3 * n_p],
                *scratch[3 * n_p:], [True] * n_p, gather_last=False)
            pl.when(i == 0)(begin)
            pl.when(i == min(4, steps - 1))(exchange)
            pl.when(i == (3 * steps) // 4)(combine)

        @pl.when(i == 0)
        def _():
            dk_ref[...] = jnp.zeros_like(dk_ref)
            dv_ref[...] = jnp.zeros_like(dv_ref)
            sink_acc[...] = jnp.zeros_like(sink_acc)

        start = _attn_window(i, seq)
        kw, vw, qall, p, p_sink, r = _attn_softmax(q_ref, k_ref, v_ref, sink_ref, bias_ref, start)
        lo = lax.broadcasted_iota(jnp.int32, (ATT_TQ, 128), 1) < HEAD_DIM
        lo3 = lo[None]
        grp0 = lax.broadcasted_iota(jnp.int32, (N_Q_HEADS, ATT_TQ, 128), 0) < 4
        val = grp0 == lo3
        swapped = pltpu.roll(r, HEAD_DIM, 2)
        inv = 1.0 / (jnp.where(val, swapped, r) + p_sink)
        d_o_blk = do_ref[...]
        do3 = jnp.where(val, jnp.concatenate([d_o_blk[None, :, 128 * j:128 * (j + 1)] for j in range(4)] * 2, axis=0), 0.0)
        t = (do3 * r).reshape(N_Q_HEADS * ATT_TQ, 128)
        t_hi = t.astype(BF16)
        t_lo = (t - t_hi.astype(F32)).astype(BF16)
        ones = jnp.ones((128, 128), BF16)
        delta = (_dot(t_hi, ones) + _dot(t_lo, ones)).reshape(N_Q_HEADS, ATT_TQ, 128) * inv
        sink_acc[...] += -(p_sink * inv) * delta
        do_all = do3.astype(BF16).reshape(N_Q_HEADS * ATT_TQ, 128)
        dp = _dot_nt(do_all, vw).reshape(N_Q_HEADS, ATT_TQ, ATT_KEYS)
        probs, ds = [], []
        for tl in range(ATT_KEYS // 128):
            cols = slice(128 * tl, 128 * (tl + 1))
            probs_t = p[:, :, cols] * inv
            probs.append(probs_t.astype(BF16))
            ds.append((probs_t * (dp[:, :, cols] - delta)).astype(BF16))
        probs_all = jnp.concatenate(probs, axis=2).reshape(N_Q_HEADS * ATT_TQ, ATT_KEYS)
        ds_all = jnp.concatenate(ds, axis=2).reshape(N_Q_HEADS * ATT_TQ, ATT_KEYS)
        scale = HEAD_DIM ** -0.5
        dq_all = (_dot(ds_all, kw) * scale).reshape(N_Q_HEADS, ATT_TQ, 128)
        for j in range(4):
            dq_ref[:, 128 * j:128 * (j + 1)] = jnp.where(lo, dq_all[j], dq_all[4 + j])
        dk_ref[pl.ds(start, ATT_KEYS), :] += _dot_tn(ds_all, qall) * scale
        dv_ref[pl.ds(start, ATT_KEYS), :] += _dot_tn(probs_all, do_all)

        @pl.when(i == steps - 1)
        def _():
            dsink_ref[...] = jnp.sum(sink_acc[...], axis=1)

        if n_p:
            @pl.when(i == steps - 1)
            def _():
                finish()
                for a in range(n_p):
                    reduced_refs[a][...] = landing_refs[a][...]

    full = lambda w: pl.BlockSpec((seq, w), lambda i: (0, 0))
    vmem = pl.BlockSpec(memory_space=pltpu.VMEM)
    return pl.pallas_call(
        body, name="attn_bwd", grid=(steps,),
        in_specs=[pl.BlockSpec((8, ATT_TQ, 128), lambda i: (0, i, 0)), full(128), full(128),
                  pl.BlockSpec((N_Q_HEADS, 1, 128), lambda i: (0, 0, 0)),
                  _attn_bias_spec(steps), pl.BlockSpec((ATT_TQ, 512), lambda i: (i, 0))] + [vmem] * n_p,
        out_specs=[pl.BlockSpec((ATT_TQ, 512), lambda i: (i, 0)), full(128), full(128),
                   pl.BlockSpec((N_Q_HEADS, 128), lambda i: (0, 0))] + [vmem] * n_p,
        out_shape=[jax.ShapeDtypeStruct((seq, 512), F32), jax.ShapeDtypeStruct((seq, 128), F32),
                   jax.ShapeDtypeStruct((seq, 128), F32), jax.ShapeDtypeStruct((N_Q_HEADS, 128), F32)]
        + [jax.ShapeDtypeStruct(p.shape[1:], F32) for p in pieces],
        scratch_shapes=[pltpu.VMEM((N_Q_HEADS, ATT_TQ, 128), F32)] + [pltpu.VMEM(p.shape[1:], F32) for p in pieces]
        + (_reduce_scratch([p.shape for p in pieces], [True] * n_p) if n_p else []),
        compiler_params=_cparams(("arbitrary",)),
    )(q_stack, k, v, sink128, bias, d_o, *pieces)


def _permute_rows(dst_ref, src_ref, sub_len):
    for k in range(N_SLAB):
        for j in range(sub_len):
            dst_ref[k, 8 * j:8 * (j + 1), :] = src_ref.at[k][pl.ds(j, SUBSEG, stride=sub_len), :]


def _unpermute_rows(dst_ref, src_ref, sub_len):
    for k in range(N_SLAB):
        for s in range(SUBSEG):
            dst_ref[k, s * sub_len:(s + 1) * sub_len, :] = src_ref.at[k][pl.ds(s, sub_len, stride=SUBSEG), :]


def _scan_chunk(br_ref, bi_ref, lr_row, li_row, init, cols, *, sub_len, reverse, store):
    lr = jnp.broadcast_to(lr_row[:, cols], (SUBSEG, SCAN_LANES))
    li = jnp.broadcast_to(li_row[:, cols], (SUBSEG, SCAN_LANES))
    if init is None:
        sr = si = jnp.zeros((SUBSEG, SCAN_LANES), F32)
    else:
        sr, si = init
    for jj in range(sub_len):
        rows = slice(SUBSEG * ((sub_len - 1 - jj) if reverse else jj), SUBSEG * (((sub_len - 1 - jj) if reverse else jj) + 1))
        sr, si = lr * sr - li * si + br_ref[rows, cols], lr * si + li * sr + bi_ref[rows, cols]
        if store:
            br_ref[rows, cols] = sr
            bi_ref[rows, cols] = si
    return sr, si


def _resolve_chunk(z, carry_refs, start_refs, pr_row, pi_row, cols, *, reverse):
    cr, ci = carry_refs[0][0:1, cols], carry_refs[1][0:1, cols]
    pr, pi = pr_row[:, cols], pi_row[:, cols]
    for s in (range(SUBSEG - 1, -1, -1) if reverse else range(SUBSEG)):
        start_refs[0][s:s + 1, cols] = cr
        start_refs[1][s:s + 1, cols] = ci
        cr, ci = pr * cr - pi * ci + z[0][s:s + 1, :], pr * ci + pi * cr + z[1][s:s + 1, :]
    carry_refs[0][0:1, cols] = cr
    carry_refs[1][0:1, cols] = ci


def _param_specs(direction):
    row = lambda q: pl.BlockSpec((None, None, 1, STATE_W), lambda i: (q, direction, 0, 0))
    wide = lambda q: pl.BlockSpec((None, None, N_SLAB, SLAB_IN, SLAB_ST), lambda i: (q, direction, 0, 0, 0))
    tall = lambda q: pl.BlockSpec((None, None, N_SLAB, SLAB_ST, SLAB_IN), lambda i: (q, direction, 0, 0, 0))
    return [row(q) for q in range(4)], [wide(0), wide(1)], [tall(0), tall(1)]


def _ssm_fwd(u, lam, bb, cb, *, direction, tb, name):
    reverse = direction == 1
    seq = u.shape[1]
    nblk = seq // tb
    sub_len = tb // SUBSEG

    def body(u_ref, lr_ref, li_ref, pr_ref, pi_ref, bbr_ref, bbi_ref, cbr_ref, cbi_ref,
             y_ref, sr_ref, si_ref, xr, xi, up, yp, car, cai):
        @pl.when(pl.program_id(0) == 0)
        def _():
            car[...] = jnp.zeros_like(car)
            cai[...] = jnp.zeros_like(cai)

        _permute_rows(up, u_ref, sub_len)
        lr, li, pr, pi = lr_ref[...], li_ref[...], pr_ref[...], pi_ref[...]
        chunk = lambda k: slice(k * SLAB_ST, (k + 1) * SLAB_ST)

        def drive(k):
            ub = up[k].astype(BF16)
            xr[:, chunk(k)] = _dot(ub, bbr_ref[k])
            xi[:, chunk(k)] = _dot(ub, bbi_ref[k])

        def scan(k):
            z = _scan_chunk(xr, xi, lr, li, None, chunk(k), sub_len=sub_len, reverse=reverse, store=False)
            _resolve_chunk(z, (car, cai), (sr_ref, si_ref), pr, pi, chunk(k), reverse=reverse)
            _scan_chunk(xr, xi, lr, li, (sr_ref[:, chunk(k)], si_ref[:, chunk(k)]), chunk(k),
                        sub_len=sub_len, reverse=reverse, store=True)

        def read_out(k):
            yp[k] = _dot(xr[:, chunk(k)].astype(BF16), cbr_ref[k]) - _dot(xi[:, chunk(k)].astype(BF16), cbi_ref[k])

        drive(0)
        for k in range(N_SLAB):
            if k + 1 < N_SLAB:
                drive(k + 1)
            scan(k)
            if k > 0:
                read_out(k - 1)
        read_out(N_SLAB - 1)
        _unpermute_rows(y_ref, yp, sub_len)

    blk = (lambda i: nblk - 1 - i) if reverse else (lambda i: i)
    rows, wide, tall = _param_specs(direction)
    tok = pl.BlockSpec((N_SLAB, tb, SLAB_IN), lambda i: (0, blk(i), 0))
    start_spec = pl.BlockSpec((None, SUBSEG, STATE_W), lambda i: (blk(i), 0, 0))
    return pl.pallas_call(
        body, name=name, grid=(nblk,),
        in_specs=[tok] + rows + wide + tall,
        out_specs=[tok, start_spec, start_spec],
        out_shape=[jax.ShapeDtypeStruct((N_SLAB, seq, SLAB_IN), F32), jax.ShapeDtypeStruct((nblk, SUBSEG, STATE_W), F32),
                   jax.ShapeDtypeStruct((nblk, SUBSEG, STATE_W), F32)],
        scratch_shapes=[pltpu.VMEM((tb, STATE_W), F32), pltpu.VMEM((tb, STATE_W), F32),
                        pltpu.VMEM((N_SLAB, tb, SLAB_IN), F32), pltpu.VMEM((N_SLAB, tb, SLAB_IN), F32),
                        pltpu.VMEM((SUBSEG, STATE_W), F32), pltpu.VMEM((SUBSEG, STATE_W), F32)],
        compiler_params=_cparams(("arbitrary",)),
    )(u, lam, lam, lam, lam, bb, bb, cb, cb)


def _ssm_bwd(u, dy, starts, lam, bb, bbt, cb_t, *, direction, tb, name):
    reverse = direction == 1
    seq = u.shape[1]
    nblk = seq // tb
    sub_len = tb // SUBSEG

    def body(u_ref, dy_ref, sr_ref, si_ref, lr_ref, li_ref, pr_ref, pi_ref, bbr_ref, bbi_ref, btr_ref, bti_ref,
             ctr_ref, cti_ref, du_ref, gb_ref, gc_ref, dl_ref,
             xr, xi, gr, gi, up, dyp, dup, gsr, gsi, car, cai):
        gbr_ref, gbi_ref = gb_ref.at[0], gb_ref.at[1]
        gcr_ref, gci_ref = gc_ref.at[0], gc_ref.at[1]
        dlr_ref, dli_ref = dl_ref.at[0], dl_ref.at[1]

        @pl.when(pl.program_id(0) == 0)
        def _():
            for ref in (car, cai, gbr_ref, gbi_ref, gcr_ref, gci_ref, dlr_ref, dli_ref):
                ref[...] = jnp.zeros_like(ref)

        _permute_rows(up, u_ref, sub_len)
        _permute_rows(dyp, dy_ref, sub_len)
        lr, li, pr, pi = lr_ref[...], li_ref[...], pr_ref[...], pi_ref[...]
        nli, npi = -li, -pi
        chunk = lambda k: slice(k * SLAB_ST, (k + 1) * SLAB_ST)

        def drive(k):
            ub = up[k].astype(BF16)
            xr[:, chunk(k)] = _dot(ub, bbr_ref[k])
            xi[:, chunk(k)] = _dot(ub, bbi_ref[k])
            dyb = dyp[k].astype(BF16)
            gr[:, chunk(k)] = _dot(dyb, ctr_ref[k])
            gi[:, chunk(k)] = -_dot(dyb, cti_ref[k])

        def scan_x(k):
            _scan_chunk(xr, xi, lr, li, (sr_ref[:, chunk(k)], si_ref[:, chunk(k)]), chunk(k),
                        sub_len=sub_len, reverse=reverse, store=True)

        def grad_c(k):
            dyb = dyp[k].astype(BF16)
            gcr_ref[k] += _dot_tn(dyb, xr[:, chunk(k)].astype(BF16))
            gci_ref[k] -= _dot_tn(dyb, xi[:, chunk(k)].astype(BF16))

        def scan_g(k):
            z = _scan_chunk(gr, gi, lr, nli, None, chunk(k), sub_len=sub_len, reverse=not reverse, store=False)
            _resolve_chunk(z, (car, cai), (gsr, gsi), pr, npi, chunk(k), reverse=not reverse)
            _scan_chunk(gr, gi, lr, nli, (gsr[:, chunk(k)], gsi[:, chunk(k)]), chunk(k),
                        sub_len=sub_len, reverse=not reverse, store=True)

        def grad_b_du(k):
            ub = up[k].astype(BF16)
            grb, gib = gr[:, chunk(k)].astype(BF16), gi[:, chunk(k)].astype(BF16)
            gbr_ref[k] += _dot_tn(ub, grb)
            gbi_ref[k] += _dot_tn(ub, gib)
            dup[k] = _dot(grb, btr_ref[k]) + _dot(gib, bti_ref[k])

        def grad_lambda(k):
            cols = chunk(k)
            acc_r, acc_i = dlr_ref[:, cols], dli_ref[:, cols]
            for jj in range(sub_len):
                prev = jj + 1 if reverse else jj - 1
                if 0 <= prev < sub_len:
                    x_r, x_i = xr[SUBSEG * prev:SUBSEG * (prev + 1), cols], xi[SUBSEG * prev:SUBSEG * (prev + 1), cols]
                else:
                    x_r, x_i = sr_ref[:, cols], si_ref[:, cols]
                g_r, g_i = gr[SUBSEG * jj:SUBSEG * (jj + 1), cols], gi[SUBSEG * jj:SUBSEG * (jj + 1), cols]
                acc_r = acc_r + (g_r * x_r + g_i * x_i)
                acc_i = acc_i + (g_i * x_r - g_r * x_i)
            dlr_ref[:, cols] = acc_r
            dli_ref[:, cols] = acc_i

        drive(0)
        for k in range(N_SLAB):
            if k + 1 < N_SLAB:
                drive(k + 1)
            scan_x(k)
            grad_c(k)
            scan_g(k)
            grad_b_du(k)
            grad_lambda(k)
        _unpermute_rows(du_ref, dup, sub_len)

    blk = (lambda i: i) if reverse else (lambda i: nblk - 1 - i)
    rows, wide, tall = _param_specs(direction)
    tok = pl.BlockSpec((N_SLAB, tb, SLAB_IN), lambda i: (0, blk(i), 0))
    start_spec = pl.BlockSpec((None, SUBSEG, STATE_W), lambda i: (blk(i), 0, 0))
    gb_shape, dl_shape = (2, N_SLAB, SLAB_IN, SLAB_ST), (2, SUBSEG, STATE_W)
    whole = lambda shape: pl.BlockSpec(shape, lambda i: (0,) * len(shape))
    big = lambda: pltpu.VMEM((tb, STATE_W), F32)
    slabs = lambda: pltpu.VMEM((N_SLAB, tb, SLAB_IN), F32)
    tile = lambda: pltpu.VMEM((SUBSEG, STATE_W), F32)
    return pl.pallas_call(
        body, name=name, grid=(nblk,),
        in_specs=[tok, tok, start_spec, start_spec] + rows + wide + tall + wide,
        out_specs=[tok, whole(gb_shape), whole(gb_shape), whole(dl_shape)],
        out_shape=[jax.ShapeDtypeStruct((N_SLAB, seq, SLAB_IN), F32), jax.ShapeDtypeStruct(gb_shape, F32),
                   jax.ShapeDtypeStruct(gb_shape, F32), jax.ShapeDtypeStruct(dl_shape, F32)],
        scratch_shapes=[big(), big(), big(), big(), slabs(), slabs(), slabs(), tile(), tile(), tile(), tile()],
        compiler_params=_cparams(("arbitrary",)),
    )(u, dy, *starts, lam, lam, lam, lam, bb, bb, bbt, bbt, cb_t, cb_t)


GELU_C = math.sqrt(2.0 / math.pi)
GELU_K = 0.044715


def _mid(o, za, u, y_f, y_b, zs, x, target, ssm_d, w_glu, b_glu, g_attn, g_ssm, w_out, ln_g, ln_b, tb):
    seq = x.shape[0]

    def body(o_ref, za_ref, u_ref, yf_ref, yb_ref, zs_ref, x_ref, t_ref, d_ref, wg_ref, bg_ref, ga_ref, gs_ref,
             wo_ref, lg_ref, lb_ref,
             loss_ref, do_ref, dza_ref, dyl_ref, dzs_ref, dpre_ref, gwo_ref, gwg_ref, vec_ref, wop):
        @pl.when(pl.program_id(0) == 0)
        def _():
            for ref in (loss_ref, gwo_ref, gwg_ref, vec_ref):
                ref[...] = jnp.zeros_like(ref)
            for nat, par in _pair_blocks(0):
                wop[par, :] = wo_ref[nat, :]
            wop[D_ATTN:, :] = wo_ref[D_ATTN:, :]

        def rows_of(rs):
            o, za = o_ref[rs, :], za_ref[rs, :]
            sig_a = _sigmoid(za)
            silu_a = za * sig_a
            ya = o * silu_a
            r_a = lax.rsqrt(jnp.mean(ya * ya, axis=1, keepdims=True) + NORM_EPS)
            n_a = ya * r_a
            g_a = ga_ref[...]
            unslab = lambda ref: jnp.concatenate([ref[k, rs, :] for k in range(N_SLAB)], axis=1)
            u_blk, zs = unslab(u_ref), zs_ref[rs, :]
            d_row = d_ref[...]
            ylin = d_row * u_blk + unslab(yf_ref) + unslab(yb_ref)
            inner = GELU_C * (ylin + GELU_K * ylin * ylin * ylin)
            th = jnp.tanh(inner)
            gl = 0.5 * ylin * (1.0 + th)
            glb = gl.astype(BF16)
            gate = _dot(glb, wg_ref[...])
            sg = _sigmoid(gate + bg_ref[...])
            y2 = gl * sg
            sig_s = _sigmoid(zs)
            silu_s = zs * sig_s
            ys = y2 * silu_s
            r_s = lax.rsqrt(jnp.mean(ys * ys, axis=1, keepdims=True) + NORM_EPS)
            n_s = ys * r_s
            g_s = gs_ref[...]
            mixed = jnp.concatenate([n_a * g_a, n_s * g_s], axis=1).astype(BF16)
            out = _dot(mixed, wop[...])
            pre = ALPHA * x_ref[rs, :] + out
            mu = jnp.mean(pre, axis=1, keepdims=True)
            cen = pre - mu
            rstd = lax.rsqrt(jnp.mean(cen * cen, axis=1, keepdims=True) + NORM_EPS)
            hhat = cen * rstd
            ln_g = lg_ref[...]
            err = hhat * ln_g + lb_ref[...] - t_ref[rs, :]
            loss_ref[...] += 0.5 * jnp.sum(jnp.mean(err * err, axis=1, keepdims=True))

            dh = err * (1.0 / D_MODEL)
            vec_ref[0:1, :] += jnp.sum(dh * hhat, axis=0, keepdims=True)
            vec_ref[1:2, :] += jnp.sum(dh, axis=0, keepdims=True)
            dhh = dh * ln_g
            dpre = rstd * (dhh - jnp.mean(dhh, axis=1, keepdims=True)
                           - hhat * jnp.mean(dhh * hhat, axis=1, keepdims=True))
            dpre_ref[rs, :] = dpre
            dpb = dpre.astype(BF16)
            for j in range(4):
                g_pair = _dot_tn(mixed[:, 128 * j:128 * (j + 1)], dpb)
                for g in range(2):
                    nat = HEAD_DIM * (4 * g + j)
                    gwo_ref[nat:nat + HEAD_DIM, :] += g_pair[HEAD_DIM * g:HEAD_DIM * (g + 1), :]
            gwo_ref[D_ATTN:, :] += _dot_tn(mixed[:, D_ATTN:], dpb)
            dmix = _dot_nt(dpb, wop[...])
            dna = dmix[:, :D_ATTN]
            vec_ref[2:3, 0:D_ATTN] += jnp.sum(dna * n_a, axis=0, keepdims=True)
            dna = dna * g_a
            dya = r_a * (dna - n_a * jnp.mean(dna * n_a, axis=1, keepdims=True))
            do_ref[rs, :] = dya * silu_a
            dza_ref[rs, :] = dya * o * (sig_a * (1.0 + za * (1.0 - sig_a)))
            dns = dmix[:, D_ATTN:]
            vec_ref[2:3, D_ATTN:] += jnp.sum(dns * n_s, axis=0, keepdims=True)
            dns = dns * g_s
            dys = r_s * (dns - n_s * jnp.mean(dns * n_s, axis=1, keepdims=True))
            dzs_ref[rs, :] = dys * y2 * (sig_s * (1.0 + zs * (1.0 - sig_s)))
            dy2 = dys * silu_s
            da = dy2 * gl * sg * (1.0 - sg)
            vec_ref[3:4, D_SSM:] += jnp.sum(da, axis=0, keepdims=True)
            dab = da.astype(BF16)
            gwg_ref[...] += _dot_tn(glb, dab)
            dgl_mm = _dot_nt(dab, wg_ref[...])
            dgl = dy2 * sg + dgl_mm
            dylin = dgl * (0.5 * (1.0 + th)
                           + 0.5 * ylin * (1.0 - th * th) * GELU_C * (1.0 + 3.0 * GELU_K * ylin * ylin))
            for k in range(N_SLAB):
                dyl_ref[k, rs, :] = dylin[:, k * SLAB_IN:(k + 1) * SLAB_IN]
            vec_ref[3:4, 0:D_SSM] += jnp.sum(dylin * u_blk, axis=0, keepdims=True)

        rows_of(slice(0, tb))

    tok = lambda w: pl.BlockSpec((tb, w), lambda i: (i, 0))
    slab = pl.BlockSpec((N_SLAB, tb, SLAB_IN), lambda i: (0, i, 0))
    const = lambda r, c: pl.BlockSpec((r, c), lambda i: (0, 0), pipeline_mode=pl.Buffered(1))
    tok_shape = jax.ShapeDtypeStruct((seq, 512), F32)
    return pl.pallas_call(
        body, name="mid", grid=(seq // tb,),
        in_specs=[tok(512), tok(512), slab, slab, slab, tok(512), tok(1024), tok(1024),
                  const(1, 512), const(512, 512), const(1, 512), const(1, 512), const(1, 512),
                  const(1024, 1024), const(1, 1024), const(1, 1024)],
        out_specs=[const(8, 128), tok(512), tok(512), slab, tok(512), tok(1024),
                   const(1024, 1024), const(512, 512), const(8, 1024)],
        out_shape=[jax.ShapeDtypeStruct((8, 128), F32), tok_shape, tok_shape,
                   jax.ShapeDtypeStruct((N_SLAB, seq, SLAB_IN), F32), tok_shape,
                   jax.ShapeDtypeStruct((seq, 1024), F32), jax.ShapeDtypeStruct((1024, 1024), F32),
                   jax.ShapeDtypeStruct((512, 512), F32), jax.ShapeDtypeStruct((8, 1024), F32)],
        scratch_shapes=[pltpu.VMEM((D_MODEL, D_MODEL), BF16)],
        compiler_params=pltpu.CompilerParams(dimension_semantics=("arbitrary",), vmem_limit_bytes=MID_VMEM),
    )(o, za, u, y_f, y_b, zs, x, target, ssm_d, w_glu, b_glu, g_attn, g_ssm, w_out, ln_g, ln_b)


def _ride_shapes(pieces, narrow, gather_last):
    outs = [p.shape if (gather_last and a == len(pieces) - 1) else p.shape[1:] for a, p in enumerate(pieces)]
    return outs, [pltpu.VMEM(s, F32) for s in outs] + _reduce_scratch([p.shape for p in pieces], narrow)


def _ride_phases(piece_refs, out_refs, scratch_refs, narrow, gather_last):
    n = len(piece_refs)
    landing, rest = scratch_refs[:n], scratch_refs[n:]
    begin, exchange, combine, finish = _reduce_phases(piece_refs, landing, rest[:n], rest[n:2 * n], rest[2 * n:3 * n],
                                                      *rest[3 * n:], narrow, gather_last)

    def end():
        finish()
        for a in range(n):
            out_refs[a][...] = landing[a][...]

    return begin, exchange, combine, end


def _dproj_block(dq_ref, dk_ref, dv_ref, dza_ref, duf_ref, dub_ref, dyl_ref, dzs_ref, d_ref, hi_ref, lo_ref, tb):
    cos, sin = _rope_block(hi_ref, lo_ref, pl.program_id(0) * (tb // ROPE_GROUP), tb // ROPE_GROUP)
    lo = lax.broadcasted_iota(jnp.int32, (tb, 128), 1) < HEAD_DIM

    def unrope(t):
        return t * cos + _rotate_half_unsigned(t * sin)

    def natural(pairs):
        swapped = [pltpu.roll(t, HEAD_DIM, 1) for t in pairs]
        return [jnp.where(lo, pairs[0], swapped[1]), jnp.where(lo, pairs[2], swapped[3]),
                jnp.where(lo, swapped[0], pairs[1]), jnp.where(lo, swapped[2], pairs[3])]

    dq_rot, dza = dq_ref[...], dza_ref[...]
    pieces = natural([unrope(dq_rot[:, 128 * j:128 * (j + 1)]) for j in range(4)])
    d_row = d_ref[...]
    pieces += [unrope(dk_ref[...]), dv_ref[...]] + natural([dza[:, 128 * j:128 * (j + 1)] for j in range(4)])
    pieces += [duf_ref[k] + dub_ref[k] + d_row[:, k * SLAB_IN:(k + 1) * SLAB_IN] * dyl_ref[k] for k in range(N_SLAB)]
    pieces += [dzs_ref[...]]
    return jnp.concatenate(pieces, axis=1).astype(BF16)


def _dproj_specs(tb, rope_hi, rope_lo):
    tok = lambda w: pl.BlockSpec((tb, w), lambda i: (i, 0))
    slab = pl.BlockSpec((N_SLAB, tb, SLAB_IN), lambda i: (0, i, 0))
    table = lambda t: pl.BlockSpec(t.shape, lambda i: (0, 0, 0))
    return [tok(512), tok(128), tok(128), tok(512), slab, slab, slab, tok(512), pl.BlockSpec((1, 512), lambda i: (0, 0)),
            table(rope_hi), table(rope_lo)]


N_DPROJ = 11
GW_ROWS = 768


def _proj_bwd_w(x, dproj_args, rope_hi, rope_lo, pieces, tb):
    seq = x.shape[0]
    steps = seq // tb
    n_p = len(pieces)
    narrow = [False] * n_p

    def body(*refs):
        x_ref, grads = refs[0], refs[1:1 + N_DPROJ]
        piece_refs = refs[1 + N_DPROJ:1 + N_DPROJ + n_p]
        gw_ref = refs[1 + N_DPROJ + n_p]
        out_refs = refs[2 + N_DPROJ + n_p:2 + N_DPROJ + 2 * n_p]
        step = pl.program_id(0)
        if n_p:
            begin, exchange, combine, end = _ride_phases(piece_refs, out_refs, refs[2 + N_DPROJ + 2 * n_p:], narrow, True)
            pl.when(step == 0)(begin)
            pl.when(step == min(1, steps - 1))(exchange)
            pl.when(step == steps // 2)(combine)

        @pl.when(step == 0)
        def _():
            gw_ref[...] = jnp.zeros_like(gw_ref)

        dproj = _dproj_block(*grads, tb)
        xb = x_ref[...].astype(BF16)
        for r0 in range(0, D_IN_PROJ, GW_ROWS):
            gw_ref[r0:r0 + GW_ROWS, :] += _dot_tn(dproj[:, r0:r0 + GW_ROWS], xb)
        if n_p:
            pl.when(step == steps - 1)(end)

    vmem = pl.BlockSpec(memory_space=pltpu.VMEM)
    whole = pl.BlockSpec((D_IN_PROJ, D_MODEL), lambda i: (0, 0), pipeline_mode=pl.Buffered(1))
    ride_outs, ride_scratch = _ride_shapes(pieces, narrow, True) if n_p else ([], [])
    return pl.pallas_call(
        body, name="proj_bwd_w", grid=(steps,),
        in_specs=[pl.BlockSpec((tb, D_MODEL), lambda i: (i, 0))] + _dproj_specs(tb, rope_hi, rope_lo) + [vmem] * n_p,
        out_specs=[whole] + [vmem] * n_p,
        out_shape=[jax.ShapeDtypeStruct((D_IN_PROJ, D_MODEL), F32)] + [jax.ShapeDtypeStruct(s, F32) for s in ride_outs],
        scratch_shapes=ride_scratch,
        compiler_params=_cparams(("arbitrary",)),
    )(x, *dproj_args, rope_hi, rope_lo, *pieces)


def _proj_bwd_x(dproj_args, rope_hi, rope_lo, dpre, wt, pieces, tb):
    seq = dpre.shape[0]
    steps = seq // tb
    n_p = len(pieces)
    narrow = [True] * n_p

    def body(*refs):
        grads = refs[:N_DPROJ]
        dpre_ref, wt_ref = refs[N_DPROJ:N_DPROJ + 2]
        piece_refs = refs[N_DPROJ + 2:N_DPROJ + 2 + n_p]
        gx_ref = refs[N_DPROJ + 2 + n_p]
        out_refs = refs[N_DPROJ + 3 + n_p:N_DPROJ + 3 + 2 * n_p]
        step = pl.program_id(0)
        if n_p:
            begin, exchange, combine, end = _ride_phases(piece_refs, out_refs, refs[N_DPROJ + 3 + 2 * n_p:], narrow, False)
            pl.when(step == 0)(begin)
            pl.when(step == min(1, steps - 1))(exchange)
            pl.when(step == steps - 1)(combine)

        dproj = _dproj_block(*grads, tb)
        gx_ref[...] = ALPHA * dpre_ref[...] + _dot(dproj, wt_ref[...])
        if n_p:
            pl.when(step == steps - 1)(end)

    vmem = pl.BlockSpec(memory_space=pltpu.VMEM)
    whole = pl.BlockSpec((D_IN_PROJ, D_MODEL), lambda i: (0, 0), pipeline_mode=pl.Buffered(1))
    ride_outs, ride_scratch = _ride_shapes(pieces, narrow, False) if n_p else ([], [])
    return pl.pallas_call(
        body, name="proj_bwd_x", grid=(steps,),
        in_specs=_dproj_specs(tb, rope_hi, rope_lo) + [pl.BlockSpec((tb, D_MODEL), lambda i: (i, 0)), whole] + [vmem] * n_p,
        out_specs=[pl.BlockSpec((tb, D_MODEL), lambda i: (i, 0))] + [vmem] * n_p,
        out_shape=[jax.ShapeDtypeStruct((seq, D_MODEL), F32)] + [jax.ShapeDtypeStruct(s, F32) for s in ride_outs],
        scratch_shapes=ride_scratch,
        compiler_params=pltpu.CompilerParams(dimension_semantics=("arbitrary",), vmem_limit_bytes=PROJ_BWD_X_VMEM),
    )(*dproj_args, rope_hi, rope_lo, dpre, wt, *pieces)


def _adamw(w, g, m, v, name):
    rows, cols = w.shape
    tb = rows
    while tb * cols * 4 > ADAMW_BLOCK_BYTES and tb % 16 == 0:
        tb //= 2

    def body(w_ref, g_ref, m_ref, v_ref, d_ref, nm_ref, nv_ref):
        _adamw_update(w_ref, g_ref, m_ref, v_ref, d_ref, nm_ref, nv_ref)

    spec = pl.BlockSpec((tb, cols), lambda i: (i, 0))
    return pl.pallas_call(
        body, name=name, grid=(rows // tb,), in_specs=[spec] * 4, out_specs=[spec] * 3,
        out_shape=[jax.ShapeDtypeStruct((rows, cols), F32)] * 3,
        compiler_params=_cparams(("arbitrary",)),
    )(w, g, m, v)


def _adamw_update(w_ref, g_ref, m_ref, v_ref, d_ref, nm_ref, nv_ref):
    g_blk = g_ref[...]
    m_new = ADAM_B1 * m_ref[...] + (1.0 - ADAM_B1) * g_blk
    v_new = ADAM_B2 * v_ref[...] + (1.0 - ADAM_B2) * (g_blk * g_blk)
    m_hat = m_new / (1.0 - ADAM_B1 ** ADAM_STEP)
    v_hat = v_new / (1.0 - ADAM_B2 ** ADAM_STEP)
    d_ref[...] = -ADAM_LR * (m_hat / (jnp.sqrt(v_hat) + ADAM_EPS) + ADAM_WD * w_ref[...])
    nm_ref[...] = m_new
    nv_ref[...] = v_new


def _adamw_many(groups, name):
    n = len(groups)

    def body(*refs):
        for p in range(n):
            w_ref, g_ref, m_ref, v_ref = refs[4 * p:4 * p + 4]
            gn_ref, d_ref, nm_ref, nv_ref = refs[4 * n + 4 * p:4 * n + 4 * p + 4]
            gn_ref[...] = g_ref[...].reshape(w_ref.shape)
            _adamw_update(w_ref, gn_ref, m_ref, v_ref, d_ref, nm_ref, nv_ref)

    return pl.pallas_call(
        body, name=name,
        out_shape=[jax.ShapeDtypeStruct(grp[0].shape, F32) for grp in groups for _ in range(4)],
    )(*[a for grp in groups for a in grp])


_WEIGHTS = ["w_in", "attn_sink", "ssm_a_re", "ssm_a_im", "ssm_log_dt", "ssm_b_re", "ssm_b_im", "ssm_c_re", "ssm_c_im",
            "ssm_d", "w_glu", "b_glu", "norm_attn_g", "norm_ssm_g", "w_out", "ln_g", "ln_b"]
N_DG = N_DIR * N_GROUPS
BIG_ROWS = N_DG * SSM_CH * SSM_STATE // 128
TINY_ROWS = 64


def _pack_small_grads(g_bc, g_vec, g_ar, g_ai, g_dt, g_sink, loss):
    big = jnp.stack([t.reshape(BIG_ROWS, 128) for t in g_bc])
    row = lambda t: jnp.pad(t.reshape(1, -1), ((0, 0), (0, 128 - t.size)))
    tiny = jnp.concatenate([g_vec.reshape(64, 128), g_ar.reshape(32, 128), g_ai.reshape(32, 128), row(g_dt), row(g_sink),
                            row(loss), jnp.zeros((N_CHIPS * TINY_ROWS - 131, 128), F32)], axis=0)
    return jnp.concatenate([big, tiny.reshape(N_CHIPS, TINY_ROWS, 128)], axis=1)


def _unpack_small_grads(packed):
    big = packed[:, :BIG_ROWS].reshape(N_CHIPS, 2 * BIG_ROWS, SSM_STATE)
    tiny = packed[:, BIG_ROWS:].reshape(N_CHIPS * TINY_ROWS, 128)
    g_vec = tiny[0:64].reshape(8, 1024)
    return tiny[130, 0], {
        "ssm_b_re": big[0], "ssm_b_im": big[1], "ssm_c_re": big[2], "ssm_c_im": big[3],
        "ln_g": g_vec[0:1], "ln_b": g_vec[1:2],
        "norm_attn_g": _from_pair_order(g_vec[2:3, :D_ATTN]), "norm_ssm_g": g_vec[2:3, D_ATTN:],
        "ssm_d": g_vec[3:4, :D_SSM], "b_glu": g_vec[3:4, D_SSM:],
        "ssm_a_re": tiny[64:96].reshape(N_DG, SSM_STATE), "ssm_a_im": tiny[96:128].reshape(N_DG, SSM_STATE),
        "ssm_log_dt": tiny[128:129, :N_DG].reshape(N_DIR, N_GROUPS), "attn_sink": tiny[129:130, :N_Q_HEADS],
    }


def _small_unview(name, t, shape):
    if name in ("ssm_b_re", "ssm_b_im"):
        return jnp.swapaxes(t.reshape(N_DIR, N_GROUPS, SSM_CH, SSM_STATE), 2, 3).reshape(shape)
    return t.reshape(shape)


def _channel_major(name, t):
    return jnp.swapaxes(t, 3, 4) if name in ("ssm_b_re", "ssm_b_im") else t


def kernel(x, w_in, attn_sink, ssm_a_re, ssm_a_im, ssm_log_dt, ssm_b_re, ssm_b_im, ssm_c_re, ssm_c_im, ssm_d, w_glu, b_glu, norm_attn_g, norm_ssm_g, w_out, ln_g, ln_b, loss_target, m_w_in, m_attn_sink, m_ssm_a_re, m_ssm_a_im, m_ssm_log_dt, m_ssm_b_re, m_ssm_b_im, m_ssm_c_re, m_ssm_c_im, m_ssm_d, m_w_glu, m_b_glu, m_norm_attn_g, m_norm_ssm_g, m_w_out, m_ln_g, m_ln_b, v_w_in, v_attn_sink, v_ssm_a_re, v_ssm_a_im, v_ssm_log_dt, v_ssm_b_re, v_ssm_b_im, v_ssm_c_re, v_ssm_c_im, v_ssm_d, v_w_glu, v_b_glu, v_norm_attn_g, v_norm_ssm_g, v_w_out, v_ln_g, v_ln_b):
    args = dict(locals())
    weights = {n: args[n] for n in _WEIGHTS}
    mom_m = {n: args["m_" + n] for n in _WEIGHTS}
    mom_v = {n: args["v_" + n] for n in _WEIGHTS}
    xs = x[0]
    target = loss_target[0]

    g_x, r_wt, r_w_out, r_w_glu, g_small_all = _local_step(
        xs, target, w_in[0].T, w_glu[0], w_out[0], attn_sink, ssm_a_re, ssm_a_im, ssm_log_dt, ssm_b_re, ssm_b_im,
        ssm_c_re, ssm_c_im, ssm_d, b_glu, norm_attn_g, norm_ssm_g, ln_g, ln_b, sharded=True)
    loss, small_grads = _unpack_small_grads(g_small_all)

    grads, deltas, new_m, new_v = {}, {}, {}, {}
    d_w, m_w, v_w = _adamw(w_in[0].T, r_wt, m_w_in[0].T, v_w_in[0].T, "adamw_w_in")
    grads["w_in"], deltas["w_in"], new_m["w_in"], new_v["w_in"] = r_wt.T[None], d_w.T[None], m_w.T[None], v_w.T[None]
    for n, g in (("w_out", r_w_out), ("w_glu", r_w_glu)):
        d_w, m_w, v_w = _adamw(weights[n][0], g, mom_m[n][0], mom_v[n][0], "adamw_" + n)
        grads[n], deltas[n], new_m[n], new_v[n] = g[None], d_w[None], m_w[None], v_w[None]
    names = sorted(small_grads)
    updates = _adamw_many([(_channel_major(n, weights[n]), small_grads[n], _channel_major(n, mom_m[n]),
                            _channel_major(n, mom_v[n])) for n in names], "adamw_small")
    for i, n in enumerate(names):
        grads[n], deltas[n], new_m[n], new_v[n] = (_channel_major(n, t) for t in updates[4 * i:4 * i + 4])

    return (loss, g_x[None], *[grads[n] for n in _WEIGHTS], *[deltas[n] for n in _WEIGHTS],
            *[new_m[n] for n in _WEIGHTS], *[new_v[n] for n in _WEIGHTS])


def _local_step(xs, target, wt_in, w_glu_in, w_out_in, attn_sink, ssm_a_re, ssm_a_im, ssm_log_dt, ssm_b_re,
                ssm_b_im, ssm_c_re, ssm_c_im, ssm_d, b_glu, norm_attn_g, norm_ssm_g, ln_g, ln_b, sharded):
    seq = xs.shape[0]

    a_r, a_i = ssm_a_re, ssm_a_im
    log_dt = ssm_log_dt.reshape(N_DG, 1)
    b_r, b_i = _channel_major("ssm_b_re", ssm_b_re), _channel_major("ssm_b_im", ssm_b_im)
    c_r, c_i = ssm_c_re, ssm_c_im
    ssm_tb = min(SSM_BLOCK, seq)
    sub_len = ssm_tb // SUBSEG
    lam, bb, bbt, cb, cb_t, *gathered = _ssm_params_fwd(a_r, a_i, log_dt, b_r, b_i, c_r, c_i, int(math.log2(sub_len)),
                                                        [wt_in] if sharded else [])
    wt_full = gathered[0].reshape(D_IN_PROJ, D_MODEL) if sharded else wt_in
    lam = lam.reshape(4, N_DIR, 1, STATE_W)

    rope_hi, rope_lo = _rope_tables(seq)
    projected = _proj(xs, wt_full, rope_hi, rope_lo, [w_glu_in, w_out_in] if sharded else [], min(512, seq))
    q_stack, k_rot, v_bf, z_attn, u, z_ssm = projected[:6]
    if sharded:
        w_glu_full, w_out_full = projected[6].reshape(D_SSM, D_SSM), projected[7].reshape(D_MODEL, D_MODEL)
    else:
        w_glu_full, w_out_full = w_glu_in, w_out_in
    sink128 = jnp.broadcast_to(attn_sink[0][:, None, None], (N_Q_HEADS, 1, 128))
    attn_bias = _attn_bias()
    o = _attn_fwd(q_stack, k_rot, v_bf, sink128, attn_bias)
    ys, starts = [], []
    for d in range(N_DIR):
        y_d, s_r, s_i = _ssm_fwd(u, lam, bb, cb, direction=d, tb=ssm_tb, name=f"ssm_fwd_{d}")
        ys.append(y_d)
        starts.append((s_r, s_i))

    row = lambda t: t.reshape(1, -1)
    g_attn_p = _to_pair_order(norm_attn_g)
    loss_blk, d_o, d_za, d_ylin, d_zs, d_pre, g_w_out, g_w_glu, g_vec = _mid(
        o, z_attn, u, ys[0], ys[1], z_ssm, xs, target, row(ssm_d), w_glu_full, row(b_glu),
        g_attn_p, row(norm_ssm_g), w_out_full, row(ln_g), row(ln_b), min(MID_BLOCK, seq))

    pieces = [g_w_glu.reshape(N_CHIPS, -1, D_SSM), g_w_out.reshape(N_CHIPS, -1, D_MODEL)] if sharded else []
    attn_grads = _attn_bwd(q_stack, k_rot, v_bf, sink128, attn_bias, d_o, pieces)
    dq, dk, dv, g_sink = attn_grads[:4]
    if sharded:
        g_w_glu, g_w_out = attn_grads[4:]
    dus, g_bb, g_cb, g_lam = [], [], [], []
    for d in range(N_DIR):
        du_d, gb_d, gc_d, dl_d = _ssm_bwd(u, d_ylin, starts[d], lam, bb, bbt, cb_t, direction=d, tb=ssm_tb,
                                          name=f"ssm_bwd_{d}")
        dus.append(du_d)
        g_bb.append(gb_d)
        g_cb.append(gc_d)
        g_lam.append(dl_d)
    g_ar, g_ai, g_dt, g_br, g_bi, g_cr, g_ci = _ssm_params_bwd(a_r, a_i, log_dt, b_r, b_i, g_bb, g_cb, g_lam)

    g_small = _pack_small_grads([g_br, g_bi, g_cr, g_ci], g_vec, g_ar, g_ai, g_dt, g_sink[:, 0], loss_blk[0, 0])
    dproj_args = (dq, dk, dv, d_za, dus[0], dus[1], d_ylin, d_zs, row(ssm_d))
    w_grads = _proj_bwd_w(xs, dproj_args, rope_hi, rope_lo, [g_small] if sharded else [], min(512, seq))
    g_wt = w_grads[0]
    if sharded:
        g_small = w_grads[1]
    x_grads = _proj_bwd_x(dproj_args, rope_hi, rope_lo, d_pre, wt_full,
                          [g_wt.reshape(N_CHIPS, -1, D_MODEL)] if sharded else [], min(512, seq))
    g_x = x_grads[0]
    if sharded:
        g_wt = x_grads[1]
    return g_x, g_wt, g_w_out, g_w_glu, g_small
```

```python
import functools
import math

import numpy as np
import jax
import jax.numpy as jnp
from jax import lax
from jax.experimental import pallas as pl
from jax.experimental.pallas import tpu as pltpu

F32 = jnp.float32
BF16 = jnp.bfloat16
MESH = pl.DeviceIdType.MESH

D_MODEL = 1024
D_ATTN = 512
D_SSM = 512
HEAD_DIM = 64
N_Q_HEADS = 8
WINDOW = 128
ROPE_THETA = 10000.0
SSM_CH = 16
N_GROUPS = 32
SSM_STATE = 64
N_DIR = 2
STATE_W = N_GROUPS * SSM_STATE
N_SLAB = 4
SLAB_IN = 128
SLAB_ST = 512
NORM_EPS = 1e-5
NEG_INF = -1e30
ALPHA = 2.0 ** 0.25
D_IN_PROJ = 2304
N_CHIPS = 4

ADAM_LR = 0.001
ADAM_B1 = 0.9
ADAM_B2 = 0.999
ADAM_EPS = 1e-08
ADAM_WD = 0.01
ADAM_STEP = 10

SUBSEG = 8
SCAN_LANES = 512
SSM_BLOCK = 512
VMEM_LIMIT = 48 * 1024 * 1024
ADAMW_BLOCK_BYTES = 3 * 512 * 1024
PROJ_BWD_X_VMEM = 56 * 1024 * 1024
MID_VMEM = 60 * 1024 * 1024
MID_BLOCK = 512

def _to_pair_order(row):
    return jnp.transpose(row.reshape(2, 4, HEAD_DIM), (1, 0, 2)).reshape(1, D_ATTN)


def _from_pair_order(row):
    return jnp.transpose(row.reshape(4, 2, HEAD_DIM), (1, 0, 2)).reshape(1, D_ATTN)


def _cparams(sem=None):
    return pltpu.CompilerParams(dimension_semantics=sem, vmem_limit_bytes=VMEM_LIMIT)


def _dot(a, b):
    return jnp.dot(a, b, preferred_element_type=F32)


def _dot_nt(a, b):
    return lax.dot_general(a, b, (((1,), (1,)), ((), ())), preferred_element_type=F32)


def _dot_tn(a, b):
    return lax.dot_general(a, b, (((0,), (0,)), ((), ())), preferred_element_type=F32)


def _sigmoid(z):
    return 0.5 * jnp.tanh(0.5 * z) + 0.5


def _all_gather_chips(shards, out_dtype, name):
    n = len(shards)

    def body(*refs):
        start, relay, finish = _gather_phases(refs[:n], refs[n:2 * n], *refs[2 * n:], out_dtype)
        start()
        relay()
        finish()

    vmem = pl.BlockSpec(memory_space=pltpu.VMEM)
    return pl.pallas_call(
        body, name=name,
        out_shape=[jax.ShapeDtypeStruct((N_CHIPS,) + s.shape, out_dtype) for s in shards],
        in_specs=[vmem] * n, out_specs=[vmem] * n,
        scratch_shapes=_gather_sems(n),
        compiler_params=pltpu.CompilerParams(vmem_limit_bytes=VMEM_LIMIT),
    )(*shards)


def _gather_sems(n):
    return [pltpu.SemaphoreType.DMA((6 * n,)), pltpu.SemaphoreType.DMA((6 * n,))]


def _gather_phases(in_refs, out_refs, send_sems, recv_sems, out_dtype):
    n = len(in_refs)
    x, y, c = lax.axis_index("x"), lax.axis_index("y"), lax.axis_index("c")
    sibling = (x, y, 1 - c)
    chips = [(1 - x, y), (x, 1 - y), (1 - x, 1 - y)]

    def half_of(a, px, py, half):
        rows = in_refs[a].shape[0] // 2
        return out_refs[a].at[2 * px + py, pl.ds(half * rows, rows), :]

    def copy(a, k, px, py, half, to):
        blk = half_of(a, px, py, half)
        return pltpu.make_async_remote_copy(src_ref=blk, dst_ref=blk, send_sem=send_sems.at[6 * a + k],
                                            recv_sem=recv_sems.at[6 * a + k], device_id=to, device_id_type=MESH)

    first = [copy(a, j, x, y, c, (*chips[j], c)) for a in range(n) for j in range(3)]
    passed = [copy(a, 3 + j, *chips[j], c, sibling) for a in range(n) for j in range(3)]

    def start():
        for a in range(n):
            out_refs[a][2 * x + y] = in_refs[a][...].astype(out_dtype)
        for cp in first:
            cp.start()

    def relay():
        for a in range(n):
            for j in range(3):
                copy(a, j, *chips[j], c, (x, y, c)).wait_recv()
                passed[3 * a + j].start()

    def finish():
        for a in range(n):
            for j in range(3):
                copy(a, 3 + j, *chips[j], 1 - c, (x, y, c)).wait_recv()
        for cp in first + passed:
            cp.wait_send()

    return start, relay, finish


SEMS_PER_ARRAY = 14


def _reduce_scratch(shapes, narrow):
    half = [(N_CHIPS, s[1] // 2, s[2]) for s in shapes]
    wire = [BF16 if nar else F32 for nar in narrow]
    n = len(shapes)
    return ([pltpu.VMEM(half[a], F32) for a in range(n)] + [pltpu.VMEM(half[a], wire[a]) for a in range(n)]
            + [pltpu.VMEM(half[a], wire[a]) for a in range(n)]
            + [pltpu.SemaphoreType.DMA((SEMS_PER_ARRAY * n,)), pltpu.SemaphoreType.DMA((SEMS_PER_ARRAY * n,))])


def _reduce_phases(p_refs, out_refs, a_refs, s_refs, b_refs, send_sems, recv_sems, narrow, gather_last):
    n = len(p_refs)
    halves = [p.shape[1] // 2 for p in p_refs]
    wire = [BF16 if nar else F32 for nar in narrow]
    x, y, c = lax.axis_index("x"), lax.axis_index("y"), lax.axis_index("c")
    me = 2 * x + y
    sibling = (x, y, 1 - c)
    chips = [(1 - x, y), (x, 1 - y), (1 - x, 1 - y)]
    slot = [2 * px + py for px, py in chips]
    last = n - 1

    def copy(a, k, src, dst, to):
        return pltpu.make_async_remote_copy(src_ref=src, dst_ref=dst, send_sem=send_sems.at[SEMS_PER_ARRAY * a + k],
                                            recv_sem=recv_sems.at[SEMS_PER_ARRAY * a + k],
                                            device_id=to, device_id_type=MESH)

    def rows(a, half):
        return pl.ds(pl.multiple_of(half * halves[a], 16), halves[a])

    def finished(a, k, half):
        if gather_last and a == last:
            return out_refs[a].at[k, rows(a, half), :]
        return out_refs[a].at[rows(a, half), :]

    order = slot + [me]
    swaps = [[copy(a, q, p_refs[a].at[order[q], rows(a, 1 - c), :], a_refs[a].at[order[q]], sibling)
              for q in range(N_CHIPS)] for a in range(n)]
    sends = [[copy(a, 4 + j, s_refs[a].at[slot[j]], b_refs[a].at[me], (*chips[j], c)) for j in range(3)] for a in range(n)]
    backs = [copy(a, 7, finished(a, me, c), finished(a, me, c), sibling) for a in range(n)]
    spread = [copy(last, 8 + j, finished(last, me, c), finished(last, me, c), (*chips[j], c)) for j in range(3)]
    relays = [copy(last, 11 + j, finished(last, slot[j], c), finished(last, slot[j], c), sibling) for j in range(3)]

    def start():
        for group in swaps:
            for cp in group:
                cp.start()

    def exchange():
        for a in range(n):
            for q in range(N_CHIPS):
                swaps[a][q].wait_recv()
                acc = a_refs[a][order[q]] + p_refs[a][order[q], rows(a, c), :]
                a_refs[a][order[q]] = acc
                s_refs[a][order[q]] = acc.astype(wire[a])
                if q < 3:
                    sends[a][q].start()
            b_refs[a][me] = s_refs[a][me]

    def combine():
        for a in range(n):
            for j in range(3):
                copy(a, 4 + j, s_refs[a].at[slot[j]], b_refs[a].at[slot[j]], (x, y, c)).wait_recv()
            terms = [jnp.where(me == k, a_refs[a][k], b_refs[a][k].astype(F32)) for k in range(N_CHIPS)]
            total = (terms[0] + terms[1]) + (terms[2] + terms[3])
            if gather_last and a == last:
                out_refs[a][me, rows(a, c), :] = total
            else:
                out_refs[a][rows(a, c), :] = total
            backs[a].start()
        if gather_last:
            for cp in spread:
                cp.start()

    def finish():
        if gather_last:
            for j in range(3):
                copy(last, 8 + j, finished(last, slot[j], c), finished(last, slot[j], c), (x, y, c)).wait_recv()
                relays[j].start()
        for a in range(n):
            copy(a, 7, finished(a, me, 1 - c), finished(a, me, 1 - c), (x, y, c)).wait_recv()
        if gather_last:
            for j in range(3):
                copy(last, 11 + j, finished(last, slot[j], 1 - c), finished(last, slot[j], 1 - c), (x, y, c)).wait_recv()
        started = [cp for group in swaps + sends for cp in group] + backs + (spread + relays if gather_last else [])
        for cp in started:
            cp.wait_send()

    return start, exchange, combine, finish


def _ssm_param_values(ar, ai, logdt):
    dt = jnp.exp(logdt)
    mag = jnp.exp(dt * ar)
    cs, sn = jnp.cos(dt * ai), jnp.sin(dt * ai)
    lr, li = mag * cs, mag * sn
    den = ar * ar + ai * ai
    nr = (lr - 1.0) * ar + li * ai
    ni = li * ar - (lr - 1.0) * ai
    return dt, mag, lr, li, den, nr, ni


GROUPS_PER_SLAB = N_GROUPS // N_SLAB


def _slab_masks():
    def eq(shape, f_row, f_col):
        return (f_row(lax.broadcasted_iota(jnp.int32, shape, 0)) == f_col(lax.broadcasted_iota(jnp.int32, shape, 1))).astype(F32)
    spread = eq((SSM_STATE, SLAB_ST), lambda r: r, lambda c: c % SSM_STATE)
    spread_t = eq((SLAB_ST, SSM_STATE), lambda r: r % SSM_STATE, lambda c: c)
    keep = eq((SLAB_IN, SLAB_ST), lambda r: r // SSM_CH, lambda c: c // SSM_STATE)
    keep_t = eq((SLAB_ST, SLAB_IN), lambda r: r // SSM_STATE, lambda c: c // SSM_CH)
    repeat = eq((N_DG * SSM_CH, N_DG), lambda r: r // SSM_CH, lambda c: c)
    return spread, spread_t, keep, keep_t, repeat


def _rows(ref):
    return ref[...].reshape(-1, SSM_STATE)


def _split3(t):
    hi = t.astype(BF16)
    rest = t - hi.astype(F32)
    mid = rest.astype(BF16)
    return hi, mid, (rest - mid.astype(F32)).astype(BF16)


def _select(dot, ones01, t, ones_first):
    o = ones01.astype(BF16)
    parts = [dot(o, p) if ones_first else dot(p, o) for p in _split3(t)]
    return (parts[0] + parts[1]) + parts[2]


def _ssm_params_fwd(ar, ai, logdt, br, bi, cr, ci, n_square):
    def body(ar_ref, ai_ref, dt_ref, br_ref, bi_ref, cr_ref, ci_ref, lam_ref, bb_ref, bbt_ref, cb_ref, cbt_ref):
        _, _, lr, li, den, nr, ni = _ssm_param_values(_rows(ar_ref), _rows(ai_ref), dt_ref[...])
        lam_ref[0] = lr
        lam_ref[1] = li
        pr, pi = lr, li
        for _ in range(n_square):
            pr, pi = pr * pr - pi * pi, 2.0 * pr * pi
        lam_ref[2] = pr
        lam_ref[3] = pi
        spread, spread_t, keep, keep_t, repeat = _slab_masks()
        fr = _select(_dot, repeat, nr / den, True)
        fi = _select(_dot, repeat, ni / den, True)
        b_r, b_i = _rows(br_ref), _rows(bi_ref)
        bbar = (fr * b_r - fi * b_i, fr * b_i + fi * b_r)
        c_par = (_rows(cr_ref), _rows(ci_ref))
        spread, spread_t = spread.astype(BF16), spread_t.astype(BF16)
        for src, wide_ref, tall_ref in ((bbar, bb_ref, bbt_ref), (c_par, cbt_ref, cb_ref)):
            for q in range(2):
                for d in range(N_DIR):
                    for k in range(N_SLAB):
                        r0 = (d * N_GROUPS + k * GROUPS_PER_SLAB) * SSM_CH
                        blk = src[q][r0:r0 + SLAB_IN].astype(BF16)
                        wide_ref[q, d, k] = (_dot(blk, spread) * keep).astype(BF16)
                        tall_ref[q, d, k] = (_dot_nt(spread_t, blk) * keep_t).astype(BF16)

    wide = jax.ShapeDtypeStruct((2, N_DIR, N_SLAB, SLAB_IN, SLAB_ST), BF16)
    tall = jax.ShapeDtypeStruct((2, N_DIR, N_SLAB, SLAB_ST, SLAB_IN), BF16)
    return pl.pallas_call(body, name="ssm_params_fwd",
                          out_shape=[jax.ShapeDtypeStruct((4, N_DG, SSM_STATE), F32), wide, tall, tall, wide],
                          compiler_params=pltpu.CompilerParams(vmem_limit_bytes=VMEM_LIMIT),
                          )(ar, ai, logdt, br, bi, cr, ci)


def _ssm_params_bwd(ar, ai, logdt, br, bi, g_slabs_b, g_slabs_c, g_lam):
    def body(ar_ref, ai_ref, dt_ref, br_ref, bi_ref, gb0_ref, gb1_ref, gc0_ref, gc1_ref, gl0_ref, gl1_ref,
             gar_ref, gai_ref, gdt_ref, gbr_ref, gbi_ref, gcr_ref, gci_ref, dbb, dlam):
        spread, spread_t, keep, _, repeat = _slab_masks()
        for d, (gb_ref, gc_ref) in enumerate(((gb0_ref, gc0_ref), (gb1_ref, gc1_ref))):
            for q in range(2):
                for k in range(N_SLAB):
                    r0 = (d * N_GROUPS + k * GROUPS_PER_SLAB) * SSM_CH
                    dbb[q, r0:r0 + SLAB_IN, :] = _select(_dot, spread_t, gb_ref[q, k] * keep, False)
                    out_ref = gcr_ref if q == 0 else gci_ref
                    out_ref[r0:r0 + SLAB_IN, :] = _select(_dot, spread_t, gc_ref[q, k] * keep, False)
        grp = (lax.broadcasted_iota(jnp.int32, (N_GROUPS, STATE_W), 0)
               == lax.broadcasted_iota(jnp.int32, (N_GROUPS, STATE_W), 1) // SSM_STATE).astype(F32)
        pick = (lax.broadcasted_iota(jnp.int32, (STATE_W, SSM_STATE), 0) % SSM_STATE
                == lax.broadcasted_iota(jnp.int32, (STATE_W, SSM_STATE), 1)).astype(F32)
        for d, gl_ref in enumerate((gl0_ref, gl1_ref)):
            for q in range(2):
                row = jnp.sum(gl_ref[q], axis=0, keepdims=True)
                dlam[q, d * N_GROUPS:(d + 1) * N_GROUPS, :] = _select(_dot, pick, grp * row, False)

        a_r, a_i = _rows(ar_ref), _rows(ai_ref)
        dt, mag, lr, li, den, nr, ni = _ssm_param_values(a_r, a_i, dt_ref[...])
        fr = _select(_dot, repeat, nr / den, True)
        fi = _select(_dot, repeat, ni / den, True)
        b_r, b_i = _rows(br_ref), _rows(bi_ref)
        g_r, g_i = dbb[0], dbb[1]
        gbr_ref[...] = fr * g_r + fi * g_i
        gbi_ref[...] = fr * g_i - fi * g_r
        d_fr = _select(_dot_tn, repeat, b_r * g_r + b_i * g_i, True)
        d_fi = _select(_dot_tn, repeat, b_r * g_i - b_i * g_r, True)
        d_nr, d_ni = d_fr / den, d_fi / den
        d_den = -(d_fr * nr + d_fi * ni) / (den * den)
        d_lr = dlam[0] + d_nr * a_r - d_ni * a_i
        d_li = dlam[1] + d_nr * a_i + d_ni * a_r
        d_ar = d_nr * (lr - 1.0) + d_ni * li + d_den * 2.0 * a_r
        d_ai = d_nr * li - d_ni * (lr - 1.0) + d_den * 2.0 * a_i
        d_mag = (d_lr * lr + d_li * li) / mag
        d_theta = d_li * lr - d_lr * li
        gar_ref[...] = d_ar + d_mag * mag * dt
        gai_ref[...] = d_ai + d_theta * dt
        d_dt = d_mag * mag * a_r + d_theta * a_i
        gdt_ref[...] = jnp.sum(d_dt, axis=1, keepdims=True) * dt

    small = jax.ShapeDtypeStruct((N_DG, SSM_STATE), F32)
    big = jax.ShapeDtypeStruct((N_DG * SSM_CH, SSM_STATE), F32)
    return pl.pallas_call(
        body, name="ssm_params_bwd",
        out_shape=[small, small, jax.ShapeDtypeStruct(logdt.shape, F32), big, big, big, big],
        scratch_shapes=[pltpu.VMEM((2,) + big.shape, F32), pltpu.VMEM((2,) + small.shape, F32)],
        compiler_params=pltpu.CompilerParams(vmem_limit_bytes=VMEM_LIMIT),
    )(ar, ai, logdt, br, bi, *g_slabs_b, *g_slabs_c, *g_lam)


ROPE_GROUP = 128


def _rope_tables(seq):
    half = HEAD_DIM // 2
    inv_freq = jnp.tile(ROPE_THETA ** (-jnp.arange(half, dtype=F32) / half), 4)
    sign = jnp.tile(jnp.concatenate([-jnp.ones((half,), F32), jnp.ones((half,), F32)]), 2)

    def table(pos):
        ang = pos.astype(F32)[:, None] * inv_freq[None, :]
        return jnp.stack([jnp.cos(ang), jnp.sin(ang), sign * jnp.sin(ang)])

    return table(jnp.arange(seq // ROPE_GROUP) * ROPE_GROUP), table(jnp.arange(ROPE_GROUP))


def _rope_block(hi_ref, lo_ref, first_group, n_groups):
    cl, sl, sl_s = lo_ref[0], lo_ref[1], lo_ref[2]
    cos, sin = [], []
    for g in range(n_groups):
        ch, sh, sh_s = (hi_ref[q, pl.ds(first_group + g, 1), :] for q in range(3))
        cos.append(ch * cl - sh * sl)
        sin.append(sh_s * cl + ch * sl_s)
    return jnp.concatenate(cos, axis=0), jnp.concatenate(sin, axis=0)


def _rotate_half_unsigned(t):
    lane = lax.broadcasted_iota(jnp.int32, t.shape, 1)
    return jnp.where((lane % HEAD_DIM) < HEAD_DIM // 2, pltpu.roll(t, 96, 1), pltpu.roll(t, 32, 1))


def _rope(t, cos, sin_signed):
    return t * cos + _rotate_half_unsigned(t) * sin_signed


def _pair_blocks(base):
    out = []
    for j in range(4):
        for g in range(2):
            nat = base + HEAD_DIM * (4 * g + j)
            par = base + 128 * j + HEAD_DIM * g
            out.append((slice(nat, nat + HEAD_DIM), slice(par, par + HEAD_DIM)))
    return out


W_Q, W_KV, W_ZA, W_U, W_ZS = 0, 512, 768, 1280, 1792


def _proj(x, wt, rope_hi, rope_lo, shards, tb):
    seq = x.shape[0]
    steps = seq // tb
    n_sh = len(shards)

    def body(*refs):
        x_ref, wt_ref, hi_ref, lo_ref = refs[:4]
        shard_refs = refs[4:4 + n_sh]
        q_ref, k_ref, v_ref, za_ref, u_ref, zs_ref = refs[4 + n_sh:10 + n_sh]
        gathered_refs = refs[10 + n_sh:10 + 2 * n_sh]
        wp = refs[10 + 2 * n_sh]
        step = pl.program_id(0)
        if n_sh:
            landing_refs = refs[11 + 2 * n_sh:11 + 3 * n_sh]
            start, relay, finish = _gather_phases(shard_refs, landing_refs, *refs[11 + 3 * n_sh:], BF16)
            pl.when(step == 0)(start)
            pl.when(step == max(steps - 2, 0))(relay)

        @pl.when(step == 0)
        def _():
            for dst_base, src_base in ((0, W_Q), (512, W_ZA)):
                for nat, par in _pair_blocks(0):
                    wp[dst_base + par.start:dst_base + par.stop, :] = wt_ref[src_base + nat.start:src_base + nat.stop, :]

        xb = x_ref[...].astype(BF16)
        cos, sin = _rope_block(hi_ref, lo_ref, pl.program_id(0) * (tb // ROPE_GROUP), tb // ROPE_GROUP)
        lo = lax.broadcasted_iota(jnp.int32, (tb, 128), 1) < HEAD_DIM
        q = _dot_nt(xb, wp[0:512, :])
        for j in range(4):
            qj = _rope(q[:, 128 * j:128 * (j + 1)], cos, sin)
            q_ref[j] = jnp.where(lo, qj, 0.0).astype(BF16)
            q_ref[4 + j] = jnp.where(lo, 0.0, qj).astype(BF16)
        kv = _dot_nt(xb, wt_ref[W_KV:W_ZA, :])
        k_ref[...] = _rope(kv[:, 0:128], cos, sin).astype(BF16)
        v_ref[...] = kv[:, 128:256].astype(BF16)
        za_ref[...] = _dot_nt(xb, wp[512:1024, :])
        u_val = _dot_nt(xb, wt_ref[W_U:W_ZS, :])
        for k in range(N_SLAB):
            u_ref[k] = u_val[:, k * SLAB_IN:(k + 1) * SLAB_IN]
        zs_ref[...] = _dot_nt(xb, wt_ref[W_ZS:D_IN_PROJ, :])
        if n_sh:
            @pl.when(step == steps - 1)
            def _():
                finish()
                for a in range(n_sh):
                    gathered_refs[a][...] = landing_refs[a][...]

    row = lambda w: pl.BlockSpec((tb, w), lambda i: (i, 0))
    table = lambda t: pl.BlockSpec(t.shape, lambda i: (0, 0, 0))
    vmem = pl.BlockSpec(memory_space=pltpu.VMEM)
    return pl.pallas_call(
        body, name="proj", grid=(steps,),
        in_specs=[row(D_MODEL), pl.BlockSpec((D_IN_PROJ, D_MODEL), lambda i: (0, 0), pipeline_mode=pl.Buffered(1)),
                  table(rope_hi), table(rope_lo)] + [vmem] * n_sh,
        out_specs=[pl.BlockSpec((8, tb, 128), lambda i: (0, i, 0)), row(128), row(128), row(512),
                   pl.BlockSpec((N_SLAB, tb, SLAB_IN), lambda i: (0, i, 0)), row(512)] + [vmem] * n_sh,
        out_shape=[jax.ShapeDtypeStruct((8, seq, 128), BF16), jax.ShapeDtypeStruct((seq, 128), BF16),
                   jax.ShapeDtypeStruct((seq, 128), BF16), jax.ShapeDtypeStruct((seq, 512), F32),
                   jax.ShapeDtypeStruct((N_SLAB, seq, SLAB_IN), F32), jax.ShapeDtypeStruct((seq, 512), F32)]
        + [jax.ShapeDtypeStruct((N_CHIPS,) + s.shape, BF16) for s in shards],
        scratch_shapes=[pltpu.VMEM((1024, D_MODEL), BF16)] + [pltpu.VMEM((N_CHIPS,) + s.shape, BF16) for s in shards]
        + (_gather_sems(n_sh) if n_sh else []),
        compiler_params=_cparams(("arbitrary",)),
    )(x, wt, rope_hi, rope_lo, *shards)


ATT_TQ = 128
ATT_KEYS = ATT_TQ + 2 * WINDOW


def _attn_window(i, seq):
    start = jnp.clip(i * ATT_TQ - WINDOW, 0, seq - ATT_KEYS)
    return pl.multiple_of(start, WINDOW)


def _attn_bias():
    r = np.arange(ATT_TQ)[None, :, None]
    c = np.arange(ATT_KEYS)[None, None, :]
    off = np.array([0, WINDOW, ATT_KEYS - ATT_TQ])[:, None, None]
    return jnp.asarray(np.where(np.abs(r + off - c) <= WINDOW, 0.0, NEG_INF).astype(np.float32))


def _attn_bias_spec(nblk):
    pick = lambda i: jnp.where(i == 0, 0, jnp.where(i == nblk - 1, 2, 1))
    return pl.BlockSpec((None, ATT_TQ, ATT_KEYS), lambda i: (pick(i), 0, 0))


def _attn_softmax(q_ref, k_ref, v_ref, sink_ref, bias_ref, start):
    kw = k_ref[pl.ds(start, ATT_KEYS), :]
    vw = v_ref[pl.ds(start, ATT_KEYS), :]
    qall = q_ref[...].reshape(N_Q_HEADS * ATT_TQ, 128)
    s = (_dot_nt(qall, kw) * (HEAD_DIM ** -0.5)).reshape(N_Q_HEADS, ATT_TQ, ATT_KEYS) + bias_ref[...][None]
    tiles = [s[:, :, 128 * t:128 * (t + 1)] for t in range(ATT_KEYS // 128)]
    m = jnp.max(functools.reduce(jnp.maximum, tiles), axis=2, keepdims=True)
    sink = sink_ref[...]
    m_b = jnp.maximum(jnp.broadcast_to(m, (N_Q_HEADS, ATT_TQ, 128)), sink)
    p = jnp.concatenate([jnp.exp(t - m_b) for t in tiles], axis=2)
    p_sink = jnp.exp(sink - m_b)
    lo_k = lax.broadcasted_iota(jnp.int32, (ATT_KEYS, 128), 1) < HEAD_DIM
    v_f = vw.astype(F32)
    v_lo, v_hi = jnp.where(lo_k, v_f, 1.0).astype(BF16), jnp.where(lo_k, 1.0, v_f).astype(BF16)
    pb = p.astype(BF16).reshape(N_Q_HEADS * ATT_TQ, ATT_KEYS)
    half = 4 * ATT_TQ
    r = jnp.concatenate([_dot(pb[:half], v_lo), _dot(pb[half:], v_hi)], axis=0).reshape(N_Q_HEADS, ATT_TQ, 128)
    return kw, vw, qall, p, p_sink, r


def _attn_fwd(q_stack, k, v, sink128, bias):
    seq = k.shape[0]

    def body(q_ref, k_ref, v_ref, sink_ref, bias_ref, o_ref):
        start = _attn_window(pl.program_id(0), seq)
        _, _, _, _, p_sink, r = _attn_softmax(q_ref, k_ref, v_ref, sink_ref, bias_ref, start)
        out = r / (pltpu.roll(r, HEAD_DIM, 2) + p_sink)
        lo = lax.broadcasted_iota(jnp.int32, (ATT_TQ, 128), 1) < HEAD_DIM
        for j in range(4):
            o_ref[:, 128 * j:128 * (j + 1)] = jnp.where(lo, out[j], out[4 + j])

    full = lambda w: pl.BlockSpec((seq, w), lambda i: (0, 0))
    return pl.pallas_call(
        body, name="attn_fwd", grid=(seq // ATT_TQ,),
        in_specs=[pl.BlockSpec((8, ATT_TQ, 128), lambda i: (0, i, 0)), full(128), full(128),
                  pl.BlockSpec((N_Q_HEADS, 1, 128), lambda i: (0, 0, 0)), _attn_bias_spec(seq // ATT_TQ)],
        out_specs=pl.BlockSpec((ATT_TQ, 512), lambda i: (i, 0)),
        out_shape=jax.ShapeDtypeStruct((seq, 512), F32),
        compiler_params=_cparams(("arbitrary",)),
    )(q_stack, k, v, sink128, bias)


def _attn_bwd(q_stack, k, v, sink128, bias, d_o, pieces):
    seq = k.shape[0]
    steps = seq // ATT_TQ
    n_p = len(pieces)

    def body(*refs):
        q_ref, k_ref, v_ref, sink_ref, bias_ref, do_ref = refs[:6]
        piece_refs = refs[6:6 + n_p]
        dq_ref, dk_ref, dv_ref, dsink_ref = refs[6 + n_p:10 + n_p]
        reduced_refs = refs[10 + n_p:10 + 2 * n_p]
        sink_acc = refs[10 + 2 * n_p]
        i = pl.program_id(0)
        if n_p:
            landing_refs = refs[11 + 2 * n_p:11 + 3 * n_p]
            scratch = refs[11 + 3 * n_p:]
            begin, exchange, combine, finish = _reduce_phases(
                piece_refs, landing_refs, scratch[:n_p], scratch[n_p:2 * n_p], scratch[2 * n_p:3 * n_p],
                *scratch[3 * n_p:], [True] * n_p, gather_last=False)
            pl.when(i == 0)(begin)
            pl.when(i == min(4, steps - 1))(exchange)
            pl.when(i == (3 * steps) // 4)(combine)

        @pl.when(i == 0)
        def _():
            dk_ref[...] = jnp.zeros_like(dk_ref)
            dv_ref[...] = jnp.zeros_like(dv_ref)
            sink_acc[...] = jnp.zeros_like(sink_acc)

        start = _attn_window(i, seq)
        kw, vw, qall, p, p_sink, r = _attn_softmax(q_ref, k_ref, v_ref, sink_ref, bias_ref, start)
        lo = lax.broadcasted_iota(jnp.int32, (ATT_TQ, 128), 1) < HEAD_DIM
        lo3 = lo[None]
        grp0 = lax.broadcasted_iota(jnp.int32, (N_Q_HEADS, ATT_TQ, 128), 0) < 4
        val = grp0 == lo3
        swapped = pltpu.roll(r, HEAD_DIM, 2)
        inv = 1.0 / (jnp.where(val, swapped, r) + p_sink)
        d_o_blk = do_ref[...]
        do3 = jnp.where(val, jnp.concatenate([d_o_blk[None, :, 128 * j:128 * (j + 1)] for j in range(4)] * 2, axis=0), 0.0)
        t = (do3 * r).reshape(N_Q_HEADS * ATT_TQ, 128)
        t_hi = t.astype(BF16)
        t_lo = (t - t_hi.astype(F32)).astype(BF16)
        ones = jnp.ones((128, 128), BF16)
        delta = (_dot(t_hi, ones) + _dot(t_lo, ones)).reshape(N_Q_HEADS, ATT_TQ, 128) * inv
        sink_acc[...] += -(p_sink * inv) * delta
        do_all = do3.astype(BF16).reshape(N_Q_HEADS * ATT_TQ, 128)
        dp = _dot_nt(do_all, vw).reshape(N_Q_HEADS, ATT_TQ, ATT_KEYS)
        probs, ds = [], []
        for tl in range(ATT_KEYS // 128):
            cols = slice(128 * tl, 128 * (tl + 1))
            probs_t = p[:, :, cols] * inv
            probs.append(probs_t.astype(BF16))
            ds.append((probs_t * (dp[:, :, cols] - delta)).astype(BF16))
        probs_all = jnp.concatenate(probs, axis=2).reshape(N_Q_HEADS * ATT_TQ, ATT_KEYS)
        ds_all = jnp.concatenate(ds, axis=2).reshape(N_Q_HEADS * ATT_TQ, ATT_KEYS)
        scale = HEAD_DIM ** -0.5
        dq_all = (_dot(ds_all, kw) * scale).reshape(N_Q_HEADS, ATT_TQ, 128)
        for j in range(4):
            dq_ref[:, 128 * j:128 * (j + 1)] = jnp.where(lo, dq_all[j], dq_all[4 + j])
        dk_ref[pl.ds(start, ATT_KEYS), :] += (_dot_tn(qall, ds_all) * scale).T
        dv_ref[pl.ds(start, ATT_KEYS), :] += _dot_tn(do_all, probs_all).T

        @pl.when(i == steps - 1)
        def _():
            dsink_ref[...] = jnp.sum(sink_acc[...], axis=1)

        if n_p:
            @pl.when(i == steps - 1)
            def _():
                finish()
                for a in range(n_p):
                    reduced_refs[a][...] = landing_refs[a][...]

    full = lambda w: pl.BlockSpec((seq, w), lambda i: (0, 0))
    vmem = pl.BlockSpec(memory_space=pltpu.VMEM)
    return pl.pallas_call(
        body, name="attn_bwd", grid=(steps,),
        in_specs=[pl.BlockSpec((8, ATT_TQ, 128), lambda i: (0, i, 0)), full(128), full(128),
                  pl.BlockSpec((N_Q_HEADS, 1, 128), lambda i: (0, 0, 0)),
                  _attn_bias_spec(steps), pl.BlockSpec((ATT_TQ, 512), lambda i: (i, 0))] + [vmem] * n_p,
        out_specs=[pl.BlockSpec((ATT_TQ, 512), lambda i: (i, 0)), full(128), full(128),
                   pl.BlockSpec((N_Q_HEADS, 128), lambda i: (0, 0))] + [vmem] * n_p,
        out_shape=[jax.ShapeDtypeStruct((seq, 512), F32), jax.ShapeDtypeStruct((seq, 128), F32),
                   jax.ShapeDtypeStruct((seq, 128), F32), jax.ShapeDtypeStruct((N_Q_HEADS, 128), F32)]
        + [jax.ShapeDtypeStruct(p.shape[1:], F32) for p in pieces],
        scratch_shapes=[pltpu.VMEM((N_Q_HEADS, ATT_TQ, 128), F32)] + [pltpu.VMEM(p.shape[1:], F32) for p in pieces]
        + (_reduce_scratch([p.shape for p in pieces], [True] * n_p) if n_p else []),
        compiler_params=_cparams(("arbitrary",)),
    )(q_stack, k, v, sink128, bias, d_o, *pieces)


def _permute_rows(dst_ref, src_ref, sub_len):
    for k in range(N_SLAB):
        for j in range(sub_len):
            dst_ref[k, 8 * j:8 * (j + 1), :] = src_ref.at[k][pl.ds(j, SUBSEG, stride=sub_len), :]


def _unpermute_rows(dst_ref, src_ref, sub_len):
    for k in range(N_SLAB):
        for s in range(SUBSEG):
            dst_ref[k, s * sub_len:(s + 1) * sub_len, :] = src_ref.at[k][pl.ds(s, sub_len, stride=SUBSEG), :]


def _scan_chunk(br_ref, bi_ref, lr_row, li_row, init, cols, *, sub_len, reverse, store):
    lr = jnp.broadcast_to(lr_row[:, cols], (SUBSEG, SCAN_LANES))
    li = jnp.broadcast_to(li_row[:, cols], (SUBSEG, SCAN_LANES))
    if init is None:
        sr = si = jnp.zeros((SUBSEG, SCAN_LANES), F32)
    else:
        sr, si = init
    for jj in range(sub_len):
        rows = slice(SUBSEG * ((sub_len - 1 - jj) if reverse else jj), SUBSEG * (((sub_len - 1 - jj) if reverse else jj) + 1))
        sr, si = lr * sr - li * si + br_ref[rows, cols], lr * si + li * sr + bi_ref[rows, cols]
        if store:
            br_ref[rows, cols] = sr
            bi_ref[rows, cols] = si
    return sr, si


def _resolve_chunk(z, carry_refs, start_refs, pr_row, pi_row, cols, *, reverse):
    cr, ci = carry_refs[0][0:1, cols], carry_refs[1][0:1, cols]
    pr, pi = pr_row[:, cols], pi_row[:, cols]
    for s in (range(SUBSEG - 1, -1, -1) if reverse else range(SUBSEG)):
        start_refs[0][s:s + 1, cols] = cr
        start_refs[1][s:s + 1, cols] = ci
        cr, ci = pr * cr - pi * ci + z[0][s:s + 1, :], pr * ci + pi * cr + z[1][s:s + 1, :]
    carry_refs[0][0:1, cols] = cr
    carry_refs[1][0:1, cols] = ci


def _param_specs(direction):
    row = lambda q: pl.BlockSpec((None, None, 1, STATE_W), lambda i: (q, direction, 0, 0))
    wide = lambda q: pl.BlockSpec((None, None, N_SLAB, SLAB_IN, SLAB_ST), lambda i: (q, direction, 0, 0, 0))
    tall = lambda q: pl.BlockSpec((None, None, N_SLAB, SLAB_ST, SLAB_IN), lambda i: (q, direction, 0, 0, 0))
    return [row(q) for q in range(4)], [wide(0), wide(1)], [tall(0), tall(1)]


def _ssm_fwd(u, lam, bb, cb, *, direction, tb, name):
    reverse = direction == 1
    seq = u.shape[1]
    nblk = seq // tb
    sub_len = tb // SUBSEG

    def body(u_ref, lr_ref, li_ref, pr_ref, pi_ref, bbr_ref, bbi_ref, cbr_ref, cbi_ref,
             y_ref, sr_ref, si_ref, xr, xi, up, yp, car, cai):
        @pl.when(pl.program_id(0) == 0)
        def _():
            car[...] = jnp.zeros_like(car)
            cai[...] = jnp.zeros_like(cai)

        _permute_rows(up, u_ref, sub_len)
        lr, li, pr, pi = lr_ref[...], li_ref[...], pr_ref[...], pi_ref[...]
        chunk = lambda k: slice(k * SLAB_ST, (k + 1) * SLAB_ST)

        def drive(k):
            ub = up[k].astype(BF16)
            xr[:, chunk(k)] = _dot(ub, bbr_ref[k])
            xi[:, chunk(k)] = _dot(ub, bbi_ref[k])

        def scan(k):
            z = _scan_chunk(xr, xi, lr, li, None, chunk(k), sub_len=sub_len, reverse=reverse, store=False)
            _resolve_chunk(z, (car, cai), (sr_ref, si_ref), pr, pi, chunk(k), reverse=reverse)
            _scan_chunk(xr, xi, lr, li, (sr_ref[:, chunk(k)], si_ref[:, chunk(k)]), chunk(k),
                        sub_len=sub_len, reverse=reverse, store=True)

        def read_out(k):
            yp[k] = _dot(xr[:, chunk(k)].astype(BF16), cbr_ref[k]) - _dot(xi[:, chunk(k)].astype(BF16), cbi_ref[k])

        drive(0)
        for k in range(N_SLAB):
            if k + 1 < N_SLAB:
                drive(k + 1)
            scan(k)
            if k > 0:
                read_out(k - 1)
        read_out(N_SLAB - 1)
        _unpermute_rows(y_ref, yp, sub_len)

    blk = (lambda i: nblk - 1 - i) if reverse else (lambda i: i)
    rows, wide, tall = _param_specs(direction)
    tok = pl.BlockSpec((N_SLAB, tb, SLAB_IN), lambda i: (0, blk(i), 0))
    start_spec = pl.BlockSpec((None, SUBSEG, STATE_W), lambda i: (blk(i), 0, 0))
    return pl.pallas_call(
        body, name=name, grid=(nblk,),
        in_specs=[tok] + rows + wide + tall,
        out_specs=[tok, start_spec, start_spec],
        out_shape=[jax.ShapeDtypeStruct((N_SLAB, seq, SLAB_IN), F32), jax.ShapeDtypeStruct((nblk, SUBSEG, STATE_W), F32),
                   jax.ShapeDtypeStruct((nblk, SUBSEG, STATE_W), F32)],
        scratch_shapes=[pltpu.VMEM((tb, STATE_W), F32), pltpu.VMEM((tb, STATE_W), F32),
                        pltpu.VMEM((N_SLAB, tb, SLAB_IN), F32), pltpu.VMEM((N_SLAB, tb, SLAB_IN), F32),
                        pltpu.VMEM((SUBSEG, STATE_W), F32), pltpu.VMEM((SUBSEG, STATE_W), F32)],
        compiler_params=_cparams(("arbitrary",)),
    )(u, lam, lam, lam, lam, bb, bb, cb, cb)


def _ssm_bwd(u, dy, starts, lam, bb, bbt, cb_t, *, direction, tb, name):
    reverse = direction == 1
    seq = u.shape[1]
    nblk = seq // tb
    sub_len = tb // SUBSEG

    def body(u_ref, dy_ref, sr_ref, si_ref, lr_ref, li_ref, pr_ref, pi_ref, bbr_ref, bbi_ref, btr_ref, bti_ref,
             ctr_ref, cti_ref, du_ref, gb_ref, gc_ref, dl_ref,
             xr, xi, gr, gi, up, dyp, dup, gsr, gsi, car, cai):
        gbr_ref, gbi_ref = gb_ref.at[0], gb_ref.at[1]
        gcr_ref, gci_ref = gc_ref.at[0], gc_ref.at[1]
        dlr_ref, dli_ref = dl_ref.at[0], dl_ref.at[1]

        @pl.when(pl.program_id(0) == 0)
        def _():
            for ref in (car, cai, gbr_ref, gbi_ref, gcr_ref, gci_ref, dlr_ref, dli_ref):
                ref[...] = jnp.zeros_like(ref)

        _permute_rows(up, u_ref, sub_len)
        _permute_rows(dyp, dy_ref, sub_len)
        lr, li, pr, pi = lr_ref[...], li_ref[...], pr_ref[...], pi_ref[...]
        nli, npi = -li, -pi
        chunk = lambda k: slice(k * SLAB_ST, (k + 1) * SLAB_ST)

        def drive(k):
            ub = up[k].astype(BF16)
            xr[:, chunk(k)] = _dot(ub, bbr_ref[k])
            xi[:, chunk(k)] = _dot(ub, bbi_ref[k])
            dyb = dyp[k].astype(BF16)
            gr[:, chunk(k)] = _dot(dyb, ctr_ref[k])
            gi[:, chunk(k)] = -_dot(dyb, cti_ref[k])

        def scan_x(k):
            _scan_chunk(xr, xi, lr, li, (sr_ref[:, chunk(k)], si_ref[:, chunk(k)]), chunk(k),
                        sub_len=sub_len, reverse=reverse, store=True)

        def grad_c(k):
            dyb = dyp[k].astype(BF16)
            gcr_ref[k] += _dot_tn(dyb, xr[:, chunk(k)].astype(BF16))
            gci_ref[k] -= _dot_tn(dyb, xi[:, chunk(k)].astype(BF16))

        def scan_g(k):
            z = _scan_chunk(gr, gi, lr, nli, None, chunk(k), sub_len=sub_len, reverse=not reverse, store=False)
            _resolve_chunk(z, (car, cai), (gsr, gsi), pr, npi, chunk(k), reverse=not reverse)
            _scan_chunk(gr, gi, lr, nli, (gsr[:, chunk(k)], gsi[:, chunk(k)]), chunk(k),
                        sub_len=sub_len, reverse=not reverse, store=True)

        def grad_b_du(k):
            ub = up[k].astype(BF16)
            grb, gib = gr[:, chunk(k)].astype(BF16), gi[:, chunk(k)].astype(BF16)
            gbr_ref[k] += _dot_tn(ub, grb)
            gbi_ref[k] += _dot_tn(ub, gib)
            dup[k] = _dot(grb, btr_ref[k]) + _dot(gib, bti_ref[k])

        def grad_lambda(k):
            cols = chunk(k)
            acc_r, acc_i = dlr_ref[:, cols], dli_ref[:, cols]
            for jj in range(sub_len):
                prev = jj + 1 if reverse else jj - 1
                if 0 <= prev < sub_len:
                    x_r, x_i = xr[SUBSEG * prev:SUBSEG * (prev + 1), cols], xi[SUBSEG * prev:SUBSEG * (prev + 1), cols]
                else:
                    x_r, x_i = sr_ref[:, cols], si_ref[:, cols]
                g_r, g_i = gr[SUBSEG * jj:SUBSEG * (jj + 1), cols], gi[SUBSEG * jj:SUBSEG * (jj + 1), cols]
                acc_r = acc_r + (g_r * x_r + g_i * x_i)
                acc_i = acc_i + (g_i * x_r - g_r * x_i)
            dlr_ref[:, cols] = acc_r
            dli_ref[:, cols] = acc_i

        drive(0)
        for k in range(N_SLAB):
            if k + 1 < N_SLAB:
                drive(k + 1)
            scan_x(k)
            grad_c(k)
            scan_g(k)
            grad_b_du(k)
            grad_lambda(k)
        _unpermute_rows(du_ref, dup, sub_len)

    blk = (lambda i: i) if reverse else (lambda i: nblk - 1 - i)
    rows, wide, tall = _param_specs(direction)
    tok = pl.BlockSpec((N_SLAB, tb, SLAB_IN), lambda i: (0, blk(i), 0))
    start_spec = pl.BlockSpec((None, SUBSEG, STATE_W), lambda i: (blk(i), 0, 0))
    gb_shape, dl_shape = (2, N_SLAB, SLAB_IN, SLAB_ST), (2, SUBSEG, STATE_W)
    whole = lambda shape: pl.BlockSpec(shape, lambda i: (0,) * len(shape))
    big = lambda: pltpu.VMEM((tb, STATE_W), F32)
    slabs = lambda: pltpu.VMEM((N_SLAB, tb, SLAB_IN), F32)
    tile = lambda: pltpu.VMEM((SUBSEG, STATE_W), F32)
    return pl.pallas_call(
        body, name=name, grid=(nblk,),
        in_specs=[tok, tok, start_spec, start_spec] + rows + wide + tall + wide,
        out_specs=[tok, whole(gb_shape), whole(gb_shape), whole(dl_shape)],
        out_shape=[jax.ShapeDtypeStruct((N_SLAB, seq, SLAB_IN), F32), jax.ShapeDtypeStruct(gb_shape, F32),
                   jax.ShapeDtypeStruct(gb_shape, F32), jax.ShapeDtypeStruct(dl_shape, F32)],
        scratch_shapes=[big(), big(), big(), big(), slabs(), slabs(), slabs(), tile(), tile(), tile(), tile()],
        compiler_params=_cparams(("arbitrary",)),
    )(u, dy, *starts, lam, lam, lam, lam, bb, bb, bbt, bbt, cb_t, cb_t)


GELU_C = math.sqrt(2.0 / math.pi)
GELU_K = 0.044715


def _mid(o, za, u, y_f, y_b, zs, x, target, ssm_d, w_glu, b_glu, g_attn, g_ssm, w_out, ln_g, ln_b, tb):
    seq = x.shape[0]

    def body(o_ref, za_ref, u_ref, yf_ref, yb_ref, zs_ref, x_ref, t_ref, d_ref, wg_ref, bg_ref, ga_ref, gs_ref,
             wo_ref, lg_ref, lb_ref,
             loss_ref, do_ref, dza_ref, dyl_ref, dzs_ref, dpre_ref, gwo_ref, gwg_ref, vec_ref, wop):
        @pl.when(pl.program_id(0) == 0)
        def _():
            for ref in (loss_ref, gwo_ref, gwg_ref, vec_ref):
                ref[...] = jnp.zeros_like(ref)
            for nat, par in _pair_blocks(0):
                wop[par, :] = wo_ref[nat, :]
            wop[D_ATTN:, :] = wo_ref[D_ATTN:, :]

        def rows_of(rs):
            o, za = o_ref[rs, :], za_ref[rs, :]
            sig_a = _sigmoid(za)
            silu_a = za * sig_a
            ya = o * silu_a
            r_a = lax.rsqrt(jnp.mean(ya * ya, axis=1, keepdims=True) + NORM_EPS)
            n_a = ya * r_a
            g_a = ga_ref[...]
            unslab = lambda ref: jnp.concatenate([ref[k, rs, :] for k in range(N_SLAB)], axis=1)
            u_blk, zs = unslab(u_ref), zs_ref[rs, :]
            d_row = d_ref[...]
            ylin = d_row * u_blk + unslab(yf_ref) + unslab(yb_ref)
            inner = GELU_C * (ylin + GELU_K * ylin * ylin * ylin)
            th = jnp.tanh(inner)
            gl = 0.5 * ylin * (1.0 + th)
            glb = gl.astype(BF16)
            gate = _dot(glb, wg_ref[...])
            sg = _sigmoid(gate + bg_ref[...])
            y2 = gl * sg
            sig_s = _sigmoid(zs)
            silu_s = zs * sig_s
            ys = y2 * silu_s
            r_s = lax.rsqrt(jnp.mean(ys * ys, axis=1, keepdims=True) + NORM_EPS)
            n_s = ys * r_s
            g_s = gs_ref[...]
            mixed = jnp.concatenate([n_a * g_a, n_s * g_s], axis=1).astype(BF16)
            out = _dot(mixed, wop[...])
            pre = ALPHA * x_ref[rs, :] + out
            mu = jnp.mean(pre, axis=1, keepdims=True)
            cen = pre - mu
            rstd = lax.rsqrt(jnp.mean(cen * cen, axis=1, keepdims=True) + NORM_EPS)
            hhat = cen * rstd
            ln_g = lg_ref[...]
            err = hhat * ln_g + lb_ref[...] - t_ref[rs, :]
            loss_ref[...] += 0.5 * jnp.sum(jnp.mean(err * err, axis=1, keepdims=True))

            dh = err * (1.0 / D_MODEL)
            vec_ref[0:1, :] += jnp.sum(dh * hhat, axis=0, keepdims=True)
            vec_ref[1:2, :] += jnp.sum(dh, axis=0, keepdims=True)
            dhh = dh * ln_g
            dpre = rstd * (dhh - jnp.mean(dhh, axis=1, keepdims=True)
                           - hhat * jnp.mean(dhh * hhat, axis=1, keepdims=True))
            dpre_ref[rs, :] = dpre
            dpb = dpre.astype(BF16)
            for j in range(4):
                g_pair = _dot_tn(mixed[:, 128 * j:128 * (j + 1)], dpb)
                for g in range(2):
                    nat = HEAD_DIM * (4 * g + j)
                    gwo_ref[nat:nat + HEAD_DIM, :] += g_pair[HEAD_DIM * g:HEAD_DIM * (g + 1), :]
            gwo_ref[D_ATTN:, :] += _dot_tn(mixed[:, D_ATTN:], dpb)
            dmix = _dot_nt(dpb, wop[...])
            dna = dmix[:, :D_ATTN]
            vec_ref[2:3, 0:D_ATTN] += jnp.sum(dna * n_a, axis=0, keepdims=True)
            dna = dna * g_a
            dya = r_a * (dna - n_a * jnp.mean(dna * n_a, axis=1, keepdims=True))
            do_ref[rs, :] = dya * silu_a
            dza_ref[rs, :] = dya * o * (sig_a * (1.0 + za * (1.0 - sig_a)))
            dns = dmix[:, D_ATTN:]
            vec_ref[2:3, D_ATTN:] += jnp.sum(dns * n_s, axis=0, keepdims=True)
            dns = dns * g_s
            dys = r_s * (dns - n_s * jnp.mean(dns * n_s, axis=1, keepdims=True))
            dzs_ref[rs, :] = dys * y2 * (sig_s * (1.0 + zs * (1.0 - sig_s)))
            dy2 = dys * silu_s
            da = dy2 * gl * sg * (1.0 - sg)
            vec_ref[3:4, D_SSM:] += jnp.sum(da, axis=0, keepdims=True)
            dab = da.astype(BF16)
            gwg_ref[...] += _dot_tn(glb, dab)
            dgl_mm = _dot_nt(dab, wg_ref[...])
            dgl = dy2 * sg + dgl_mm
            dylin = dgl * (0.5 * (1.0 + th)
                           + 0.5 * ylin * (1.0 - th * th) * GELU_C * (1.0 + 3.0 * GELU_K * ylin * ylin))
            for k in range(N_SLAB):
                dyl_ref[k, rs, :] = dylin[:, k * SLAB_IN:(k + 1) * SLAB_IN]
            vec_ref[3:4, 0:D_SSM] += jnp.sum(dylin * u_blk, axis=0, keepdims=True)

        rows_of(slice(0, tb))

    tok = lambda w: pl.BlockSpec((tb, w), lambda i: (i, 0))
    slab = pl.BlockSpec((N_SLAB, tb, SLAB_IN), lambda i: (0, i, 0))
    const = lambda r, c: pl.BlockSpec((r, c), lambda i: (0, 0), pipeline_mode=pl.Buffered(1))
    tok_shape = jax.ShapeDtypeStruct((seq, 512), F32)
    return pl.pallas_call(
        body, name="mid", grid=(seq // tb,),
        in_specs=[tok(512), tok(512), slab, slab, slab, tok(512), tok(1024), tok(1024),
                  const(1, 512), const(512, 512), const(1, 512), const(1, 512), const(1, 512),
                  const(1024, 1024), const(1, 1024), const(1, 1024)],
        out_specs=[const(8, 128), tok(512), tok(512), slab, tok(512), tok(1024),
                   const(1024, 1024), const(512, 512), const(8, 1024)],
        out_shape=[jax.ShapeDtypeStruct((8, 128), F32), tok_shape, tok_shape,
                   jax.ShapeDtypeStruct((N_SLAB, seq, SLAB_IN), F32), tok_shape,
                   jax.ShapeDtypeStruct((seq, 1024), F32), jax.ShapeDtypeStruct((1024, 1024), F32),
                   jax.ShapeDtypeStruct((512, 512), F32), jax.ShapeDtypeStruct((8, 1024), F32)],
        scratch_shapes=[pltpu.VMEM((D_MODEL, D_MODEL), BF16)],
        compiler_params=pltpu.CompilerParams(dimension_semantics=("arbitrary",), vmem_limit_bytes=MID_VMEM),
    )(o, za, u, y_f, y_b, zs, x, target, ssm_d, w_glu, b_glu, g_attn, g_ssm, w_out, ln_g, ln_b)


def _ride_shapes(pieces, narrow, gather_last):
    outs = [p.shape if (gather_last and a == len(pieces) - 1) else p.shape[1:] for a, p in enumerate(pieces)]
    return outs, [pltpu.VMEM(s, F32) for s in outs] + _reduce_scratch([p.shape for p in pieces], narrow)


def _ride_phases(piece_refs, out_refs, scratch_refs, narrow, gather_last):
    n = len(piece_refs)
    landing, rest = scratch_refs[:n], scratch_refs[n:]
    begin, exchange, combine, finish = _reduce_phases(piece_refs, landing, rest[:n], rest[n:2 * n], rest[2 * n:3 * n],
                                                      *rest[3 * n:], narrow, gather_last)

    def end():
        finish()
        for a in range(n):
            out_refs[a][...] = landing[a][...]

    return begin, exchange, combine, end


def _dproj_block(dq_ref, dk_ref, dv_ref, dza_ref, duf_ref, dub_ref, dyl_ref, dzs_ref, d_ref, hi_ref, lo_ref, tb):
    cos, sin = _rope_block(hi_ref, lo_ref, pl.program_id(0) * (tb // ROPE_GROUP), tb // ROPE_GROUP)
    lo = lax.broadcasted_iota(jnp.int32, (tb, 128), 1) < HEAD_DIM

    def unrope(t):
        return t * cos + _rotate_half_unsigned(t * sin)

    def natural(pairs):
        swapped = [pltpu.roll(t, HEAD_DIM, 1) for t in pairs]
        return [jnp.where(lo, pairs[0], swapped[1]), jnp.where(lo, pairs[2], swapped[3]),
                jnp.where(lo, swapped[0], pairs[1]), jnp.where(lo, swapped[2], pairs[3])]

    dq_rot, dza = dq_ref[...], dza_ref[...]
    pieces = natural([unrope(dq_rot[:, 128 * j:128 * (j + 1)]) for j in range(4)])
    d_row = d_ref[...]
    pieces += [unrope(dk_ref[...]), dv_ref[...]] + natural([dza[:, 128 * j:128 * (j + 1)] for j in range(4)])
    pieces += [duf_ref[k] + dub_ref[k] + d_row[:, k * SLAB_IN:(k + 1) * SLAB_IN] * dyl_ref[k] for k in range(N_SLAB)]
    pieces += [dzs_ref[...]]
    return jnp.concatenate(pieces, axis=1).astype(BF16)


def _dproj_specs(tb, rope_hi, rope_lo):
    tok = lambda w: pl.BlockSpec((tb, w), lambda i: (i, 0))
    slab = pl.BlockSpec((N_SLAB, tb, SLAB_IN), lambda i: (0, i, 0))
    table = lambda t: pl.BlockSpec(t.shape, lambda i: (0, 0, 0))
    return [tok(512), tok(128), tok(128), tok(512), slab, slab, slab, tok(512), pl.BlockSpec((1, 512), lambda i: (0, 0)),
            table(rope_hi), table(rope_lo)]


N_DPROJ = 11
GW_ROWS = 768


def _proj_bwd_w(x, dproj_args, rope_hi, rope_lo, pieces, tb):
    seq = x.shape[0]
    steps = seq // tb
    n_p = len(pieces)
    narrow = [False] * n_p

    def body(*refs):
        x_ref, grads = refs[0], refs[1:1 + N_DPROJ]
        piece_refs = refs[1 + N_DPROJ:1 + N_DPROJ + n_p]
        gw_ref = refs[1 + N_DPROJ + n_p]
        out_refs = refs[2 + N_DPROJ + n_p:2 + N_DPROJ + 2 * n_p]
        step = pl.program_id(0)
        if n_p:
            begin, exchange, combine, end = _ride_phases(piece_refs, out_refs, refs[2 + N_DPROJ + 2 * n_p:], narrow, True)
            pl.when(step == 0)(begin)
            pl.when(step == min(1, steps - 1))(exchange)
            pl.when(step == steps // 2)(combine)

        @pl.when(step == 0)
        def _():
            gw_ref[...] = jnp.zeros_like(gw_ref)

        dproj = _dproj_block(*grads, tb)
        xb = x_ref[...].astype(BF16)
        for r0 in range(0, D_IN_PROJ, GW_ROWS):
            gw_ref[r0:r0 + GW_ROWS, :] += _dot_tn(dproj[:, r0:r0 + GW_ROWS], xb)
        if n_p:
            pl.when(step == steps - 1)(end)

    vmem = pl.BlockSpec(memory_space=pltpu.VMEM)
    whole = pl.BlockSpec((D_IN_PROJ, D_MODEL), lambda i: (0, 0), pipeline_mode=pl.Buffered(1))
    ride_outs, ride_scratch = _ride_shapes(pieces, narrow, True) if n_p else ([], [])
    return pl.pallas_call(
        body, name="proj_bwd_w", grid=(steps,),
        in_specs=[pl.BlockSpec((tb, D_MODEL), lambda i: (i, 0))] + _dproj_specs(tb, rope_hi, rope_lo) + [vmem] * n_p,
        out_specs=[whole] + [vmem] * n_p,
        out_shape=[jax.ShapeDtypeStruct((D_IN_PROJ, D_MODEL), F32)] + [jax.ShapeDtypeStruct(s, F32) for s in ride_outs],
        scratch_shapes=ride_scratch,
        compiler_params=_cparams(("arbitrary",)),
    )(x, *dproj_args, rope_hi, rope_lo, *pieces)


def _proj_bwd_x(dproj_args, rope_hi, rope_lo, dpre, wt, pieces, tb):
    seq = dpre.shape[0]
    steps = seq // tb
    n_p = len(pieces)
    narrow = [True] * n_p

    def body(*refs):
        grads = refs[:N_DPROJ]
        dpre_ref, wt_ref = refs[N_DPROJ:N_DPROJ + 2]
        piece_refs = refs[N_DPROJ + 2:N_DPROJ + 2 + n_p]
        gx_ref = refs[N_DPROJ + 2 + n_p]
        out_refs = refs[N_DPROJ + 3 + n_p:N_DPROJ + 3 + 2 * n_p]
        step = pl.program_id(0)
        if n_p:
            begin, exchange, combine, end = _ride_phases(piece_refs, out_refs, refs[N_DPROJ + 3 + 2 * n_p:], narrow, False)
            pl.when(step == 0)(begin)
            pl.when(step == min(1, steps - 1))(exchange)
            pl.when(step == steps - 1)(combine)

        dproj = _dproj_block(*grads, tb)
        gx_ref[...] = ALPHA * dpre_ref[...] + _dot(dproj, wt_ref[...])
        if n_p:
            pl.when(step == steps - 1)(end)

    vmem = pl.BlockSpec(memory_space=pltpu.VMEM)
    whole = pl.BlockSpec((D_IN_PROJ, D_MODEL), lambda i: (0, 0), pipeline_mode=pl.Buffered(1))
    ride_outs, ride_scratch = _ride_shapes(pieces, narrow, False) if n_p else ([], [])
    return pl.pallas_call(
        body, name="proj_bwd_x", grid=(steps,),
        in_specs=_dproj_specs(tb, rope_hi, rope_lo) + [pl.BlockSpec((tb, D_MODEL), lambda i: (i, 0)), whole] + [vmem] * n_p,
        out_specs=[pl.BlockSpec((tb, D_MODEL), lambda i: (i, 0))] + [vmem] * n_p,
        out_shape=[jax.ShapeDtypeStruct((seq, D_MODEL), F32)] + [jax.ShapeDtypeStruct(s, F32) for s in ride_outs],
        scratch_shapes=ride_scratch,
        compiler_params=pltpu.CompilerParams(dimension_semantics=("arbitrary",), vmem_limit_bytes=PROJ_BWD_X_VMEM),
    )(*dproj_args, rope_hi, rope_lo, dpre, wt, *pieces)


def _adamw(w, g, m, v, name):
    rows, cols = w.shape
    tb = rows
    while tb * cols * 4 > ADAMW_BLOCK_BYTES and tb % 16 == 0:
        tb //= 2

    def body(w_ref, g_ref, m_ref, v_ref, d_ref, nm_ref, nv_ref):
        _adamw_update(w_ref, g_ref, m_ref, v_ref, d_ref, nm_ref, nv_ref)

    spec = pl.BlockSpec((tb, cols), lambda i: (i, 0))
    return pl.pallas_call(
        body, name=name, grid=(rows // tb,), in_specs=[spec] * 4, out_specs=[spec] * 3,
        out_shape=[jax.ShapeDtypeStruct((rows, cols), F32)] * 3,
        compiler_params=_cparams(("arbitrary",)),
    )(w, g, m, v)


def _adamw_update(w_ref, g_ref, m_ref, v_ref, d_ref, nm_ref, nv_ref):
    g_blk = g_ref[...]
    m_new = ADAM_B1 * m_ref[...] + (1.0 - ADAM_B1) * g_blk
    v_new = ADAM_B2 * v_ref[...] + (1.0 - ADAM_B2) * (g_blk * g_blk)
    m_hat = m_new / (1.0 - ADAM_B1 ** ADAM_STEP)
    v_hat = v_new / (1.0 - ADAM_B2 ** ADAM_STEP)
    d_ref[...] = -ADAM_LR * (m_hat / (jnp.sqrt(v_hat) + ADAM_EPS) + ADAM_WD * w_ref[...])
    nm_ref[...] = m_new
    nv_ref[...] = v_new


def _adamw_many(groups, name):
    n = len(groups)

    def body(*refs):
        for p in range(n):
            w_ref, g_ref, m_ref, v_ref = refs[4 * p:4 * p + 4]
            gn_ref, d_ref, nm_ref, nv_ref = refs[4 * n + 4 * p:4 * n + 4 * p + 4]
            gn_ref[...] = g_ref[...].reshape(w_ref.shape)
            _adamw_update(w_ref, gn_ref, m_ref, v_ref, d_ref, nm_ref, nv_ref)

    return pl.pallas_call(
        body, name=name,
        out_shape=[jax.ShapeDtypeStruct(grp[0].shape, F32) for grp in groups for _ in range(4)],
    )(*[a for grp in groups for a in grp])


_WEIGHTS = ["w_in", "attn_sink", "ssm_a_re", "ssm_a_im", "ssm_log_dt", "ssm_b_re", "ssm_b_im", "ssm_c_re", "ssm_c_im",
            "ssm_d", "w_glu", "b_glu", "norm_attn_g", "norm_ssm_g", "w_out", "ln_g", "ln_b"]
N_DG = N_DIR * N_GROUPS
BIG_ROWS = N_DG * SSM_CH * SSM_STATE // 128
TINY_ROWS = 64


def _pack_small_grads(g_bc, g_vec, g_ar, g_ai, g_dt, g_sink, loss):
    big = jnp.stack([t.reshape(BIG_ROWS, 128) for t in g_bc])
    row = lambda t: jnp.pad(t.reshape(1, -1), ((0, 0), (0, 128 - t.size)))
    tiny = jnp.concatenate([g_vec.reshape(64, 128), g_ar.reshape(32, 128), g_ai.reshape(32, 128), row(g_dt), row(g_sink),
                            row(loss), jnp.zeros((N_CHIPS * TINY_ROWS - 131, 128), F32)], axis=0)
    return jnp.concatenate([big, tiny.reshape(N_CHIPS, TINY_ROWS, 128)], axis=1)


def _unpack_small_grads(packed):
    big = packed[:, :BIG_ROWS].reshape(N_CHIPS, 2 * BIG_ROWS, SSM_STATE)
    tiny = packed[:, BIG_ROWS:].reshape(N_CHIPS * TINY_ROWS, 128)
    g_vec = tiny[0:64].reshape(8, 1024)
    return tiny[130, 0], {
        "ssm_b_re": big[0], "ssm_b_im": big[1], "ssm_c_re": big[2], "ssm_c_im": big[3],
        "ln_g": g_vec[0:1], "ln_b": g_vec[1:2],
        "norm_attn_g": _from_pair_order(g_vec[2:3, :D_ATTN]), "norm_ssm_g": g_vec[2:3, D_ATTN:],
        "ssm_d": g_vec[3:4, :D_SSM], "b_glu": g_vec[3:4, D_SSM:],
        "ssm_a_re": tiny[64:96].reshape(N_DG, SSM_STATE), "ssm_a_im": tiny[96:128].reshape(N_DG, SSM_STATE),
        "ssm_log_dt": tiny[128:129, :N_DG].reshape(N_DIR, N_GROUPS), "attn_sink": tiny[129:130, :N_Q_HEADS],
    }


def _small_unview(name, t, shape):
    if name in ("ssm_b_re", "ssm_b_im"):
        return jnp.swapaxes(t.reshape(N_DIR, N_GROUPS, SSM_CH, SSM_STATE), 2, 3).reshape(shape)
    return t.reshape(shape)


def _channel_major(name, t):
    return jnp.swapaxes(t, 3, 4) if name in ("ssm_b_re", "ssm_b_im") else t


def kernel(x, w_in, attn_sink, ssm_a_re, ssm_a_im, ssm_log_dt, ssm_b_re, ssm_b_im, ssm_c_re, ssm_c_im, ssm_d, w_glu, b_glu, norm_attn_g, norm_ssm_g, w_out, ln_g, ln_b, loss_target, m_w_in, m_attn_sink, m_ssm_a_re, m_ssm_a_im, m_ssm_log_dt, m_ssm_b_re, m_ssm_b_im, m_ssm_c_re, m_ssm_c_im, m_ssm_d, m_w_glu, m_b_glu, m_norm_attn_g, m_norm_ssm_g, m_w_out, m_ln_g, m_ln_b, v_w_in, v_attn_sink, v_ssm_a_re, v_ssm_a_im, v_ssm_log_dt, v_ssm_b_re, v_ssm_b_im, v_ssm_c_re, v_ssm_c_im, v_ssm_d, v_w_glu, v_b_glu, v_norm_attn_g, v_norm_ssm_g, v_w_out, v_ln_g, v_ln_b):
    args = dict(locals())
    weights = {n: args[n] for n in _WEIGHTS}
    mom_m = {n: args["m_" + n] for n in _WEIGHTS}
    mom_v = {n: args["v_" + n] for n in _WEIGHTS}
    xs = x[0]
    target = loss_target[0]

    (wt_g,) = _all_gather_chips([w_in[0].T], BF16, "gather_weights")
    wt_full = wt_g.reshape(D_IN_PROJ, D_MODEL)

    g_x, r_wt, r_w_out, r_w_glu, g_small_all = _local_step(
        xs, target, wt_full, w_glu[0], w_out[0], attn_sink, ssm_a_re, ssm_a_im, ssm_log_dt, ssm_b_re, ssm_b_im,
        ssm_c_re, ssm_c_im, ssm_d, b_glu, norm_attn_g, norm_ssm_g, ln_g, ln_b, sharded=True)
    loss, small_grads = _unpack_small_grads(g_small_all)

    grads, deltas, new_m, new_v = {}, {}, {}, {}
    d_w, m_w, v_w = _adamw(w_in[0].T, r_wt, m_w_in[0].T, v_w_in[0].T, "adamw_w_in")
    grads["w_in"], deltas["w_in"], new_m["w_in"], new_v["w_in"] = r_wt.T[None], d_w.T[None], m_w.T[None], v_w.T[None]
    for n, g in (("w_out", r_w_out), ("w_glu", r_w_glu)):
        d_w, m_w, v_w = _adamw(weights[n][0], g, mom_m[n][0], mom_v[n][0], "adamw_" + n)
        grads[n], deltas[n], new_m[n], new_v[n] = g[None], d_w[None], m_w[None], v_w[None]
    names = sorted(small_grads)
    updates = _adamw_many([(_channel_major(n, weights[n]), small_grads[n], _channel_major(n, mom_m[n]),
                            _channel_major(n, mom_v[n])) for n in names], "adamw_small")
    for i, n in enumerate(names):
        grads[n], deltas[n], new_m[n], new_v[n] = (_channel_major(n, t) for t in updates[4 * i:4 * i + 4])

    return (loss, g_x[None], *[grads[n] for n in _WEIGHTS], *[deltas[n] for n in _WEIGHTS],
            *[new_m[n] for n in _WEIGHTS], *[new_v[n] for n in _WEIGHTS])


def _local_step(xs, target, wt_full, w_glu_in, w_out_in, attn_sink, ssm_a_re, ssm_a_im, ssm_log_dt, ssm_b_re,
                ssm_b_im, ssm_c_re, ssm_c_im, ssm_d, b_glu, norm_attn_g, norm_ssm_g, ln_g, ln_b, sharded):
    seq = xs.shape[0]

    a_r, a_i = ssm_a_re, ssm_a_im
    log_dt = ssm_log_dt.reshape(N_DG, 1)
    b_r, b_i = _channel_major("ssm_b_re", ssm_b_re), _channel_major("ssm_b_im", ssm_b_im)
    c_r, c_i = ssm_c_re, ssm_c_im
    ssm_tb = min(SSM_BLOCK, seq)
    sub_len = ssm_tb // SUBSEG
    lam, bb, bbt, cb, cb_t = _ssm_params_fwd(a_r, a_i, log_dt, b_r, b_i, c_r, c_i, int(math.log2(sub_len)))
    lam = lam.reshape(4, N_DIR, 1, STATE_W)

    rope_hi, rope_lo = _rope_tables(seq)
    projected = _proj(xs, wt_full, rope_hi, rope_lo, [w_glu_in, w_out_in] if sharded else [], min(512, seq))
    q_stack, k_rot, v_bf, z_attn, u, z_ssm = projected[:6]
    if sharded:
        w_glu_full, w_out_full = projected[6].reshape(D_SSM, D_SSM), projected[7].reshape(D_MODEL, D_MODEL)
    else:
        w_glu_full, w_out_full = w_glu_in, w_out_in
    sink128 = jnp.broadcast_to(attn_sink[0][:, None, None], (N_Q_HEADS, 1, 128))
    attn_bias = _attn_bias()
    o = _attn_fwd(q_stack, k_rot, v_bf, sink128, attn_bias)
    ys, starts = [], []
    for d in range(N_DIR):
        y_d, s_r, s_i = _ssm_fwd(u, lam, bb, cb, direction=d, tb=ssm_tb, name=f"ssm_fwd_{d}")
        ys.append(y_d)
        starts.append((s_r, s_i))

    row = lambda t: t.reshape(1, -1)
    g_attn_p = _to_pair_order(norm_attn_g)
    loss_blk, d_o, d_za, d_ylin, d_zs, d_pre, g_w_out, g_w_glu, g_vec = _mid(
        o, z_attn, u, ys[0], ys[1], z_ssm, xs, target, row(ssm_d), w_glu_full, row(b_glu),
        g_attn_p, row(norm_ssm_g), w_out_full, row(ln_g), row(ln_b), min(MID_BLOCK, seq))

    pieces = [g_w_glu.reshape(N_CHIPS, -1, D_SSM), g_w_out.reshape(N_CHIPS, -1, D_MODEL)] if sharded else []
    attn_grads = _attn_bwd(q_stack, k_rot, v_bf, sink128, attn_bias, d_o, pieces)
    dq, dk, dv, g_sink = attn_grads[:4]
    if sharded:
        g_w_glu, g_w_out = attn_grads[4:]
    dus, g_bb, g_cb, g_lam = [], [], [], []
    for d in range(N_DIR):
        du_d, gb_d, gc_d, dl_d = _ssm_bwd(u, d_ylin, starts[d], lam, bb, bbt, cb_t, direction=d, tb=ssm_tb,
                                          name=f"ssm_bwd_{d}")
        dus.append(du_d)
        g_bb.append(gb_d)
        g_cb.append(gc_d)
        g_lam.append(dl_d)
    g_ar, g_ai, g_dt, g_br, g_bi, g_cr, g_ci = _ssm_params_bwd(a_r, a_i, log_dt, b_r, b_i, g_bb, g_cb, g_lam)

    g_small = _pack_small_grads([g_br, g_bi, g_cr, g_ci], g_vec, g_ar, g_ai, g_dt, g_sink[:, 0], loss_blk[0, 0])
    dproj_args = (dq, dk, dv, d_za, dus[0], dus[1], d_ylin, d_zs, row(ssm_d))
    w_grads = _proj_bwd_w(xs, dproj_args, rope_hi, rope_lo, [g_small] if sharded else [], min(512, seq))
    g_wt = w_grads[0]
    if sharded:
        g_small = w_grads[1]
    x_grads = _proj_bwd_x(dproj_args, rope_hi, rope_lo, d_pre, wt_full,
                          [g_wt.reshape(N_CHIPS, -1, D_MODEL)] if sharded else [], min(512, seq))
    g_x = x_grads[0]
    if sharded:
        g_wt = x_grads[1]
    return g_x, g_wt, g_w_out, g_w_glu, g_small
```

```python
import functools
import math

import numpy as np
import jax
import jax.numpy as jnp
from jax import lax
from jax.experimental import pallas as pl
from jax.experimental.pallas import tpu as pltpu

F32 = jnp.float32
BF16 = jnp.bfloat16
MESH = pl.DeviceIdType.MESH

D_MODEL = 1024
D_ATTN = 512
D_SSM = 512
HEAD_DIM = 64
N_Q_HEADS = 8
WINDOW = 128
ROPE_THETA = 10000.0
SSM_CH = 16
N_GROUPS = 32
SSM_STATE = 64
N_DIR = 2
STATE_W = N_GROUPS * SSM_STATE
N_SLAB = 4
SLAB_IN = 128
SLAB_ST = 512
NORM_EPS = 1e-5
NEG_INF = -1e30
ALPHA = 2.0 ** 0.25
D_IN_PROJ = 2304
N_CHIPS = 4

ADAM_LR = 0.001
ADAM_B1 = 0.9
ADAM_B2 = 0.999
ADAM_EPS = 1e-08
ADAM_WD = 0.01
ADAM_STEP = 10

SUBSEG = 8
SCAN_LANES = 512
SSM_BLOCK = 512
VMEM_LIMIT = 48 * 1024 * 1024
ADAMW_BLOCK_BYTES = 3 * 512 * 1024
PROJ_BWD_X_VMEM = 56 * 1024 * 1024
MID_VMEM = 60 * 1024 * 1024
MID_BLOCK = 512

def _to_pair_order(row):
    return jnp.transpose(row.reshape(2, 4, HEAD_DIM), (1, 0, 2)).reshape(1, D_ATTN)


def _from_pair_order(row):
    return jnp.transpose(row.reshape(4, 2, HEAD_DIM), (1, 0, 2)).reshape(1, D_ATTN)


def _cparams(sem=None):
    return pltpu.CompilerParams(dimension_semantics=sem, vmem_limit_bytes=VMEM_LIMIT)


def _dot(a, b):
    return jnp.dot(a, b, preferred_element_type=F32)


def _dot_nt(a, b):
    return lax.dot_general(a, b, (((1,), (1,)), ((), ())), preferred_element_type=F32)


def _dot_tn(a, b):
    return lax.dot_general(a, b, (((0,), (0,)), ((), ())), preferred_element_type=F32)


def _sigmoid(z):
    return 0.5 * jnp.tanh(0.5 * z) + 0.5


def _all_gather_chips(shards, out_dtype, name):
    n = len(shards)

    def body(*refs):
        start, relay, finish = _gather_phases(refs[:n], refs[n:2 * n], *refs[2 * n:], out_dtype)
        start()
        relay()
        finish()

    vmem = pl.BlockSpec(memory_space=pltpu.VMEM)
    return pl.pallas_call(
        body, name=name,
        out_shape=[jax.ShapeDtypeStruct((N_CHIPS,) + s.shape, out_dtype) for s in shards],
        in_specs=[vmem] * n, out_specs=[vmem] * n,
        scratch_shapes=_gather_sems(n),
        compiler_params=pltpu.CompilerParams(vmem_limit_bytes=VMEM_LIMIT),
    )(*shards)


def _gather_sems(n):
    return [pltpu.SemaphoreType.DMA((6 * n,)), pltpu.SemaphoreType.DMA((6 * n,))]


def _gather_phases(in_refs, out_refs, send_sems, recv_sems, out_dtype):
    n = len(in_refs)
    x, y, c = lax.axis_index("x"), lax.axis_index("y"), lax.axis_index("c")
    sibling = (x, y, 1 - c)
    chips = [(1 - x, y), (x, 1 - y), (1 - x, 1 - y)]

    def half_of(a, px, py, half):
        rows = in_refs[a].shape[0] // 2
        return out_refs[a].at[2 * px + py, pl.ds(half * rows, rows), :]

    def copy(a, k, px, py, half, to):
        blk = half_of(a, px, py, half)
        return pltpu.make_async_remote_copy(src_ref=blk, dst_ref=blk, send_sem=send_sems.at[6 * a + k],
                                            recv_sem=recv_sems.at[6 * a + k], device_id=to, device_id_type=MESH)

    first = [copy(a, j, x, y, c, (*chips[j], c)) for a in range(n) for j in range(3)]
    passed = [copy(a, 3 + j, *chips[j], c, sibling) for a in range(n) for j in range(3)]

    def start():
        for a in range(n):
            out_refs[a][2 * x + y] = in_refs[a][...].astype(out_dtype)
        for cp in first:
            cp.start()

    def relay():
        for a in range(n):
            for j in range(3):
                copy(a, j, *chips[j], c, (x, y, c)).wait_recv()
                passed[3 * a + j].start()

    def finish():
        for a in range(n):
            for j in range(3):
                copy(a, 3 + j, *chips[j], 1 - c, (x, y, c)).wait_recv()
        for cp in first + passed:
            cp.wait_send()

    return start, relay, finish


SEMS_PER_ARRAY = 14


def _reduce_scratch(shapes, narrow):
    half = [(N_CHIPS, s[1] // 2, s[2]) for s in shapes]
    wire = [BF16 if nar else F32 for nar in narrow]
    n = len(shapes)
    return ([pltpu.VMEM(half[a], F32) for a in range(n)] + [pltpu.VMEM(half[a], wire[a]) for a in range(n)]
            + [pltpu.VMEM(half[a], wire[a]) for a in range(n)]
            + [pltpu.SemaphoreType.DMA((SEMS_PER_ARRAY * n,)), pltpu.SemaphoreType.DMA((SEMS_PER_ARRAY * n,))])


def _reduce_phases(p_refs, out_refs, a_refs, s_refs, b_refs, send_sems, recv_sems, narrow, gather_last):
    n = len(p_refs)
    halves = [p.shape[1] // 2 for p in p_refs]
    wire = [BF16 if nar else F32 for nar in narrow]
    x, y, c = lax.axis_index("x"), lax.axis_index("y"), lax.axis_index("c")
    me = 2 * x + y
    sibling = (x, y, 1 - c)
    chips = [(1 - x, y), (x, 1 - y), (1 - x, 1 - y)]
    slot = [2 * px + py for px, py in chips]
    last = n - 1

    def copy(a, k, src, dst, to):
        return pltpu.make_async_remote_copy(src_ref=src, dst_ref=dst, send_sem=send_sems.at[SEMS_PER_ARRAY * a + k],
                                            recv_sem=recv_sems.at[SEMS_PER_ARRAY * a + k],
                                            device_id=to, device_id_type=MESH)

    def rows(a, half):
        return pl.ds(pl.multiple_of(half * halves[a], 16), halves[a])

    def finished(a, k, half):
        if gather_last and a == last:
            return out_refs[a].at[k, rows(a, half), :]
        return out_refs[a].at[rows(a, half), :]

    order = slot + [me]
    swaps = [[copy(a, q, p_refs[a].at[order[q], rows(a, 1 - c), :], a_refs[a].at[order[q]], sibling)
              for q in range(N_CHIPS)] for a in range(n)]
    sends = [[copy(a, 4 + j, s_refs[a].at[slot[j]], b_refs[a].at[me], (*chips[j], c)) for j in range(3)] for a in range(n)]
    backs = [copy(a, 7, finished(a, me, c), finished(a, me, c), sibling) for a in range(n)]
    spread = [copy(last, 8 + j, finished(last, me, c), finished(last, me, c), (*chips[j], c)) for j in range(3)]
    relays = [copy(last, 11 + j, finished(last, slot[j], c), finished(last, slot[j], c), sibling) for j in range(3)]

    def start():
        for group in swaps:
            for cp in group:
                cp.start()

    def exchange():
        for a in range(n):
            for q in range(N_CHIPS):
                swaps[a][q].wait_recv()
                acc = a_refs[a][order[q]] + p_refs[a][order[q], rows(a, c), :]
                a_refs[a][order[q]] = acc
                s_refs[a][order[q]] = acc.astype(wire[a])
                if q < 3:
                    sends[a][q].start()
            b_refs[a][me] = s_refs[a][me]

    def combine():
        for a in range(n):
            for j in range(3):
                copy(a, 4 + j, s_refs[a].at[slot[j]], b_refs[a].at[slot[j]], (x, y, c)).wait_recv()
            terms = [jnp.where(me == k, a_refs[a][k], b_refs[a][k].astype(F32)) for k in range(N_CHIPS)]
            total = (terms[0] + terms[1]) + (terms[2] + terms[3])
            if gather_last and a == last:
                out_refs[a][me, rows(a, c), :] = total
            else:
                out_refs[a][rows(a, c), :] = total
            backs[a].start()
        if gather_last:
            for cp in spread:
                cp.start()

    def finish():
        if gather_last:
            for j in range(3):
                copy(last, 8 + j, finished(last, slot[j], c), finished(last, slot[j], c), (x, y, c)).wait_recv()
                relays[j].start()
        for a in range(n):
            copy(a, 7, finished(a, me, 1 - c), finished(a, me, 1 - c), (x, y, c)).wait_recv()
        if gather_last:
            for j in range(3):
                copy(last, 11 + j, finished(last, slot[j], 1 - c), finished(last, slot[j], 1 - c), (x, y, c)).wait_recv()
        started = [cp for group in swaps + sends for cp in group] + backs + (spread + relays if gather_last else [])
        for cp in started:
            cp.wait_send()

    return start, exchange, combine, finish


def _ssm_param_values(ar, ai, logdt):
    dt = jnp.exp(logdt)
    mag = jnp.exp(dt * ar)
    cs, sn = jnp.cos(dt * ai), jnp.sin(dt * ai)
    lr, li = mag * cs, mag * sn
    den = ar * ar + ai * ai
    nr = (lr - 1.0) * ar + li * ai
    ni = li * ar - (lr - 1.0) * ai
    return dt, mag, lr, li, den, nr, ni


GROUPS_PER_SLAB = N_GROUPS // N_SLAB


def _slab_masks():
    def eq(shape, f_row, f_col):
        return (f_row(lax.broadcasted_iota(jnp.int32, shape, 0)) == f_col(lax.broadcasted_iota(jnp.int32, shape, 1))).astype(F32)
    spread = eq((SSM_STATE, SLAB_ST), lambda r: r, lambda c: c % SSM_STATE)
    spread_t = eq((SLAB_ST, SSM_STATE), lambda r: r % SSM_STATE, lambda c: c)
    keep = eq((SLAB_IN, SLAB_ST), lambda r: r // SSM_CH, lambda c: c // SSM_STATE)
    keep_t = eq((SLAB_ST, SLAB_IN), lambda r: r // SSM_STATE, lambda c: c // SSM_CH)
    repeat = eq((N_DG * SSM_CH, N_DG), lambda r: r // SSM_CH, lambda c: c)
    return spread, spread_t, keep, keep_t, repeat


def _rows(ref):
    return ref[...].reshape(-1, SSM_STATE)


def _split3(t):
    hi = t.astype(BF16)
    rest = t - hi.astype(F32)
    mid = rest.astype(BF16)
    return hi, mid, (rest - mid.astype(F32)).astype(BF16)


def _select(dot, ones01, t, ones_first):
    o = ones01.astype(BF16)
    parts = [dot(o, p) if ones_first else dot(p, o) for p in _split3(t)]
    return (parts[0] + parts[1]) + parts[2]


def _ssm_params_fwd(ar, ai, logdt, br, bi, cr, ci, n_square):
    def body(ar_ref, ai_ref, dt_ref, br_ref, bi_ref, cr_ref, ci_ref, lam_ref, bb_ref, bbt_ref, cb_ref, cbt_ref):
        _, _, lr, li, den, nr, ni = _ssm_param_values(_rows(ar_ref), _rows(ai_ref), dt_ref[...])
        lam_ref[0] = lr
        lam_ref[1] = li
        pr, pi = lr, li
        for _ in range(n_square):
            pr, pi = pr * pr - pi * pi, 2.0 * pr * pi
        lam_ref[2] = pr
        lam_ref[3] = pi
        spread, spread_t, keep, keep_t, repeat = _slab_masks()
        fr = _select(_dot, repeat, nr / den, True)
        fi = _select(_dot, repeat, ni / den, True)
        b_r, b_i = _rows(br_ref), _rows(bi_ref)
        bbar = (fr * b_r - fi * b_i, fr * b_i + fi * b_r)
        c_par = (_rows(cr_ref), _rows(ci_ref))
        spread, spread_t = spread.astype(BF16), spread_t.astype(BF16)
        for src, wide_ref, tall_ref in ((bbar, bb_ref, bbt_ref), (c_par, cbt_ref, cb_ref)):
            for q in range(2):
                for d in range(N_DIR):
                    for k in range(N_SLAB):
                        r0 = (d * N_GROUPS + k * GROUPS_PER_SLAB) * SSM_CH
                        blk = src[q][r0:r0 + SLAB_IN].astype(BF16)
                        wide_ref[q, d, k] = (_dot(blk, spread) * keep).astype(BF16)
                        tall_ref[q, d, k] = (_dot_nt(spread_t, blk) * keep_t).astype(BF16)

    wide = jax.ShapeDtypeStruct((2, N_DIR, N_SLAB, SLAB_IN, SLAB_ST), BF16)
    tall = jax.ShapeDtypeStruct((2, N_DIR, N_SLAB, SLAB_ST, SLAB_IN), BF16)
    return pl.pallas_call(body, name="ssm_params_fwd",
                          out_shape=[jax.ShapeDtypeStruct((4, N_DG, SSM_STATE), F32), wide, tall, tall, wide],
                          compiler_params=pltpu.CompilerParams(vmem_limit_bytes=VMEM_LIMIT),
                          )(ar, ai, logdt, br, bi, cr, ci)


def _ssm_params_bwd(ar, ai, logdt, br, bi, g_slabs_b, g_slabs_c, g_lam):
    def body(ar_ref, ai_ref, dt_ref, br_ref, bi_ref, gb0_ref, gb1_ref, gc0_ref, gc1_ref, gl0_ref, gl1_ref,
             gar_ref, gai_ref, gdt_ref, gbr_ref, gbi_ref, gcr_ref, gci_ref, dbb, dlam):
        spread, spread_t, keep, _, repeat = _slab_masks()
        for d, (gb_ref, gc_ref) in enumerate(((gb0_ref, gc0_ref), (gb1_ref, gc1_ref))):
            for q in range(2):
                for k in range(N_SLAB):
                    r0 = (d * N_GROUPS + k * GROUPS_PER_SLAB) * SSM_CH
                    dbb[q, r0:r0 + SLAB_IN, :] = _select(_dot, spread_t, gb_ref[q, k] * keep, False)
                    out_ref = gcr_ref if q == 0 else gci_ref
                    out_ref[r0:r0 + SLAB_IN, :] = _select(_dot, spread_t, gc_ref[q, k] * keep, False)
        grp = (lax.broadcasted_iota(jnp.int32, (N_GROUPS, STATE_W), 0)
               == lax.broadcasted_iota(jnp.int32, (N_GROUPS, STATE_W), 1) // SSM_STATE).astype(F32)
        pick = (lax.broadcasted_iota(jnp.int32, (STATE_W, SSM_STATE), 0) % SSM_STATE
                == lax.broadcasted_iota(jnp.int32, (STATE_W, SSM_STATE), 1)).astype(F32)
        for d, gl_ref in enumerate((gl0_ref, gl1_ref)):
            for q in range(2):
                row = jnp.sum(gl_ref[q], axis=0, keepdims=True)
                dlam[q, d * N_GROUPS:(d + 1) * N_GROUPS, :] = _select(_dot, pick, grp * row, False)

        a_r, a_i = _rows(ar_ref), _rows(ai_ref)
        dt, mag, lr, li, den, nr, ni = _ssm_param_values(a_r, a_i, dt_ref[...])
        fr = _select(_dot, repeat, nr / den, True)
        fi = _select(_dot, repeat, ni / den, True)
        b_r, b_i = _rows(br_ref), _rows(bi_ref)
        g_r, g_i = dbb[0], dbb[1]
        gbr_ref[...] = fr * g_r + fi * g_i
        gbi_ref[...] = fr * g_i - fi * g_r
        d_fr = _select(_dot_tn, repeat, b_r * g_r + b_i * g_i, True)
        d_fi = _select(_dot_tn, repeat, b_r * g_i - b_i * g_r, True)
        d_nr, d_ni = d_fr / den, d_fi / den
        d_den = -(d_fr * nr + d_fi * ni) / (den * den)
        d_lr = dlam[0] + d_nr * a_r - d_ni * a_i
        d_li = dlam[1] + d_nr * a_i + d_ni * a_r
        d_ar = d_nr * (lr - 1.0) + d_ni * li + d_den * 2.0 * a_r
        d_ai = d_nr * li - d_ni * (lr - 1.0) + d_den * 2.0 * a_i
        d_mag = (d_lr * lr + d_li * li) / mag
        d_theta = d_li * lr - d_lr * li
        gar_ref[...] = d_ar + d_mag * mag * dt
        gai_ref[...] = d_ai + d_theta * dt
        d_dt = d_mag * mag * a_r + d_theta * a_i
        gdt_ref[...] = jnp.sum(d_dt, axis=1, keepdims=True) * dt

    small = jax.ShapeDtypeStruct((N_DG, SSM_STATE), F32)
    big = jax.ShapeDtypeStruct((N_DG * SSM_CH, SSM_STATE), F32)
    return pl.pallas_call(
        body, name="ssm_params_bwd",
        out_shape=[small, small, jax.ShapeDtypeStruct(logdt.shape, F32), big, big, big, big],
        scratch_shapes=[pltpu.VMEM((2,) + big.shape, F32), pltpu.VMEM((2,) + small.shape, F32)],
        compiler_params=pltpu.CompilerParams(vmem_limit_bytes=VMEM_LIMIT),
    )(ar, ai, logdt, br, bi, *g_slabs_b, *g_slabs_c, *g_lam)


ROPE_GROUP = 128


def _rope_tables(seq):
    half = HEAD_DIM // 2
    inv_freq = jnp.tile(ROPE_THETA ** (-jnp.arange(half, dtype=F32) / half), 4)
    sign = jnp.tile(jnp.concatenate([-jnp.ones((half,), F32), jnp.ones((half,), F32)]), 2)

    def table(pos):
        ang = pos.astype(F32)[:, None] * inv_freq[None, :]
        return jnp.stack([jnp.cos(ang), jnp.sin(ang), sign * jnp.sin(ang)])

    return table(jnp.arange(seq // ROPE_GROUP) * ROPE_GROUP), table(jnp.arange(ROPE_GROUP))


def _rope_block(hi_ref, lo_ref, first_group, n_groups):
    cl, sl, sl_s = lo_ref[0], lo_ref[1], lo_ref[2]
    cos, sin = [], []
    for g in range(n_groups):
        ch, sh, sh_s = (hi_ref[q, pl.ds(first_group + g, 1), :] for q in range(3))
        cos.append(ch * cl - sh * sl)
        sin.append(sh_s * cl + ch * sl_s)
    return jnp.concatenate(cos, axis=0), jnp.concatenate(sin, axis=0)


def _rotate_half_unsigned(t):
    lane = lax.broadcasted_iota(jnp.int32, t.shape, 1)
    return jnp.where((lane % HEAD_DIM) < HEAD_DIM // 2, pltpu.roll(t, 96, 1), pltpu.roll(t, 32, 1))


def _rope(t, cos, sin_signed):
    return t * cos + _rotate_half_unsigned(t) * sin_signed


def _pair_blocks(base):
    out = []
    for j in range(4):
        for g in range(2):
            nat = base + HEAD_DIM * (4 * g + j)
            par = base + 128 * j + HEAD_DIM * g
            out.append((slice(nat, nat + HEAD_DIM), slice(par, par + HEAD_DIM)))
    return out


W_Q, W_KV, W_ZA, W_U, W_ZS = 0, 512, 768, 1280, 1792


def _proj(x, wt, rope_hi, rope_lo, shards, tb):
    seq = x.shape[0]
    steps = seq // tb
    n_sh = len(shards)

    def body(*refs):
        x_ref, wt_ref, hi_ref, lo_ref = refs[:4]
        shard_refs = refs[4:4 + n_sh]
        q_ref, k_ref, v_ref, za_ref, u_ref, zs_ref = refs[4 + n_sh:10 + n_sh]
        gathered_refs = refs[10 + n_sh:10 + 2 * n_sh]
        wp = refs[10 + 2 * n_sh]
        step = pl.program_id(0)
        if n_sh:
            landing_refs = refs[11 + 2 * n_sh:11 + 3 * n_sh]
            start, relay, finish = _gather_phases(shard_refs, landing_refs, *refs[11 + 3 * n_sh:], BF16)
            pl.when(step == 0)(start)
            pl.when(step == max(steps - 2, 0))(relay)

        @pl.when(step == 0)
        def _():
            for dst_base, src_base in ((0, W_Q), (512, W_ZA)):
                for nat, par in _pair_blocks(0):
                    wp[dst_base + par.start:dst_base + par.stop, :] = wt_ref[src_base + nat.start:src_base + nat.stop, :]

        xb = x_ref[...].astype(BF16)
        cos, sin = _rope_block(hi_ref, lo_ref, pl.program_id(0) * (tb // ROPE_GROUP), tb // ROPE_GROUP)
        lo = lax.broadcasted_iota(jnp.int32, (tb, 128), 1) < HEAD_DIM
        q = _dot_nt(xb, wp[0:512, :])
        for j in range(4):
            qj = _rope(q[:, 128 * j:128 * (j + 1)], cos, sin)
            q_ref[j] = jnp.where(lo, qj, 0.0).astype(BF16)
            q_ref[4 + j] = jnp.where(lo, 0.0, qj).astype(BF16)
        kv = _dot_nt(xb, wt_ref[W_KV:W_ZA, :])
        k_ref[...] = _rope(kv[:, 0:128], cos, sin).astype(BF16)
        v_ref[...] = kv[:, 128:256].astype(BF16)
        za_ref[...] = _dot_nt(xb, wp[512:1024, :])
        u_val = _dot_nt(xb, wt_ref[W_U:W_ZS, :])
        for k in range(N_SLAB):
            u_ref[k] = u_val[:, k * SLAB_IN:(k + 1) * SLAB_IN]
        zs_ref[...] = _dot_nt(xb, wt_ref[W_ZS:D_IN_PROJ, :])
        if n_sh:
            @pl.when(step == steps - 1)
            def _():
                finish()
                for a in range(n_sh):
                    gathered_refs[a][...] = landing_refs[a][...]

    row = lambda w: pl.BlockSpec((tb, w), lambda i: (i, 0))
    table = lambda t: pl.BlockSpec(t.shape, lambda i: (0, 0, 0))
    vmem = pl.BlockSpec(memory_space=pltpu.VMEM)
    return pl.pallas_call(
        body, name="proj", grid=(steps,),
        in_specs=[row(D_MODEL), pl.BlockSpec((D_IN_PROJ, D_MODEL), lambda i: (0, 0), pipeline_mode=pl.Buffered(1)),
                  table(rope_hi), table(rope_lo)] + [vmem] * n_sh,
        out_specs=[pl.BlockSpec((8, tb, 128), lambda i: (0, i, 0)), row(128), row(128), row(512),
                   pl.BlockSpec((N_SLAB, tb, SLAB_IN), lambda i: (0, i, 0)), row(512)] + [vmem] * n_sh,
        out_shape=[jax.ShapeDtypeStruct((8, seq, 128), BF16), jax.ShapeDtypeStruct((seq, 128), BF16),
                   jax.ShapeDtypeStruct((seq, 128), BF16), jax.ShapeDtypeStruct((seq, 512), F32),
                   jax.ShapeDtypeStruct((N_SLAB, seq, SLAB_IN), F32), jax.ShapeDtypeStruct((seq, 512), F32)]
        + [jax.ShapeDtypeStruct((N_CHIPS,) + s.shape, BF16) for s in shards],
        scratch_shapes=[pltpu.VMEM((1024, D_MODEL), BF16)] + [pltpu.VMEM((N_CHIPS,) + s.shape, BF16) for s in shards]
        + (_gather_sems(n_sh) if n_sh else []),
        compiler_params=_cparams(("arbitrary",)),
    )(x, wt, rope_hi, rope_lo, *shards)


ATT_TQ = 128
ATT_KEYS = ATT_TQ + 2 * WINDOW


def _attn_window(i, seq):
    start = jnp.clip(i * ATT_TQ - WINDOW, 0, seq - ATT_KEYS)
    return pl.multiple_of(start, WINDOW)


def _attn_bias():
    r = np.arange(ATT_TQ)[None, :, None]
    c = np.arange(ATT_KEYS)[None, None, :]
    off = np.array([0, WINDOW, ATT_KEYS - ATT_TQ])[:, None, None]
    return jnp.asarray(np.where(np.abs(r + off - c) <= WINDOW, 0.0, NEG_INF).astype(np.float32))


def _attn_bias_spec(nblk):
    pick = lambda i: jnp.where(i == 0, 0, jnp.where(i == nblk - 1, 2, 1))
    return pl.BlockSpec((None, ATT_TQ, ATT_KEYS), lambda i: (pick(i), 0, 0))


def _attn_softmax(q_ref, k_ref, v_ref, sink_ref, bias_ref, start, group=None):
    kw = k_ref[pl.ds(start, ATT_KEYS), :]
    vw = v_ref[pl.ds(start, ATT_KEYS), :]
    n_h = N_Q_HEADS if group is None else N_Q_HEADS // 2
    heads = slice(None) if group is None else slice(group * n_h, (group + 1) * n_h)
    qall = q_ref[heads].reshape(n_h * ATT_TQ, 128)
    s = (_dot_nt(qall, kw) * (HEAD_DIM ** -0.5)).reshape(n_h, ATT_TQ, ATT_KEYS) + bias_ref[...][None]
    tiles = [s[:, :, 128 * t:128 * (t + 1)] for t in range(ATT_KEYS // 128)]
    m = jnp.max(functools.reduce(jnp.maximum, tiles), axis=2, keepdims=True)
    sink = sink_ref[heads]
    m_b = jnp.maximum(jnp.broadcast_to(m, (n_h, ATT_TQ, 128)), sink)
    p = jnp.concatenate([jnp.exp(t - m_b) for t in tiles], axis=2)
    p_sink = jnp.exp(sink - m_b)
    lo_k = lax.broadcasted_iota(jnp.int32, (ATT_KEYS, 128), 1) < HEAD_DIM
    v_f = vw.astype(F32)
    v_lo, v_hi = jnp.where(lo_k, v_f, 1.0).astype(BF16), jnp.where(lo_k, 1.0, v_f).astype(BF16)
    pb = p.astype(BF16).reshape(n_h * ATT_TQ, ATT_KEYS)
    half = 4 * ATT_TQ
    if group is None:
        r = jnp.concatenate([_dot(pb[:half], v_lo), _dot(pb[half:], v_hi)], axis=0)
    else:
        r = _dot(pb, v_lo if group == 0 else v_hi)
    return kw, vw, qall, p, p_sink, r.reshape(n_h, ATT_TQ, 128)


def _attn_fwd(q_stack, k, v, sink128, bias):
    seq = k.shape[0]

    def body(q_ref, k_ref, v_ref, sink_ref, bias_ref, o_ref):
        start = _attn_window(pl.program_id(0), seq)
        _, _, _, _, p_sink, r = _attn_softmax(q_ref, k_ref, v_ref, sink_ref, bias_ref, start)
        out = r / (pltpu.roll(r, HEAD_DIM, 2) + p_sink)
        lo = lax.broadcasted_iota(jnp.int32, (ATT_TQ, 128), 1) < HEAD_DIM
        for j in range(4):
            o_ref[:, 128 * j:128 * (j + 1)] = jnp.where(lo, out[j], out[4 + j])

    full = lambda w: pl.BlockSpec((seq, w), lambda i: (0, 0))
    return pl.pallas_call(
        body, name="attn_fwd", grid=(seq // ATT_TQ,),
        in_specs=[pl.BlockSpec((8, ATT_TQ, 128), lambda i: (0, i, 0)), full(128), full(128),
                  pl.BlockSpec((N_Q_HEADS, 1, 128), lambda i: (0, 0, 0)), _attn_bias_spec(seq // ATT_TQ)],
        out_specs=pl.BlockSpec((ATT_TQ, 512), lambda i: (i, 0)),
        out_shape=jax.ShapeDtypeStruct((seq, 512), F32),
        compiler_params=_cparams(("arbitrary",)),
    )(q_stack, k, v, sink128, bias)


def _attn_bwd(q_stack, k, v, sink128, bias, d_o, pieces):
    seq = k.shape[0]
    steps = seq // ATT_TQ
    n_p = len(pieces)

    def body(*refs):
        q_ref, k_ref, v_ref, sink_ref, bias_ref, do_ref = refs[:6]
        piece_refs = refs[6:6 + n_p]
        dq_ref, dk_ref, dv_ref, dsink_ref = refs[6 + n_p:10 + n_p]
        reduced_refs = refs[10 + n_p:10 + 2 * n_p]
        sink_acc = refs[10 + 2 * n_p]
        i = pl.program_id(0)
        if n_p:
            landing_refs = refs[11 + 2 * n_p:11 + 3 * n_p]
            scratch = refs[11 + 3 * n_p:]
            begin, exchange, combine, finish = _reduce_phases(
                piece_refs, landing_refs, scratch[:n_p], scratch[n_p:2 * n_p], scratch[2 * n_p:3 * n_p],
                *scratch[3 * n_p:], [True] * n_p, gather_last=False)
            pl.when(i == 0)(begin)
            pl.when(i == min(4, steps - 1))(exchange)
            pl.when(i == (3 * steps) // 4)(combine)

        @pl.when(i == 0)
        def _():
            dk_ref[...] = jnp.zeros_like(dk_ref)
            dv_ref[...] = jnp.zeros_like(dv_ref)
            sink_acc[...] = jnp.zeros_like(sink_acc)

        start = _attn_window(i, seq)
        lo = lax.broadcasted_iota(jnp.int32, (ATT_TQ, 128), 1) < HEAD_DIM
        d_o_blk = do_ref[...]
        scale = HEAD_DIM ** -0.5
        n_h = N_Q_HEADS // 2

        def kv_group(g):
            kw, vw, qall, p, p_sink, r = _attn_softmax(q_ref, k_ref, v_ref, sink_ref, bias_ref, start, g)
            val = (lo if g == 0 else jnp.logical_not(lo))[None]
            swapped = pltpu.roll(r, HEAD_DIM, 2)
            inv = 1.0 / (jnp.where(val, swapped, r) + p_sink)
            do3 = jnp.where(val, jnp.concatenate([d_o_blk[None, :, 128 * j:128 * (j + 1)] for j in range(4)], axis=0), 0.0)
            t = (do3 * r).reshape(n_h * ATT_TQ, 128)
            t_hi = t.astype(BF16)
            t_lo = (t - t_hi.astype(F32)).astype(BF16)
            ones = jnp.ones((128, 128), BF16)
            delta = (_dot(t_hi, ones) + _dot(t_lo, ones)).reshape(n_h, ATT_TQ, 128) * inv
            sink_acc[g * n_h:(g + 1) * n_h] += -(p_sink * inv) * delta
            do_all = do3.astype(BF16).reshape(n_h * ATT_TQ, 128)
            dp = _dot_nt(do_all, vw).reshape(n_h, ATT_TQ, ATT_KEYS)
            probs, ds = [], []
            for tl in range(ATT_KEYS // 128):
                cols = slice(128 * tl, 128 * (tl + 1))
                probs_t = p[:, :, cols] * inv
                probs.append(probs_t.astype(BF16))
                ds.append((probs_t * (dp[:, :, cols] - delta)).astype(BF16))
            probs_all = jnp.concatenate(probs, axis=2).reshape(n_h * ATT_TQ, ATT_KEYS)
            ds_all = jnp.concatenate(ds, axis=2).reshape(n_h * ATT_TQ, ATT_KEYS)
            dq = (_dot(ds_all, kw) * scale).reshape(n_h, ATT_TQ, 128)
            return dq, _dot_tn(qall, ds_all), _dot_tn(do_all, probs_all)

        dq0, dk0, dv0 = kv_group(0)
        dq1, dk1, dv1 = kv_group(1)
        for j in range(4):
            dq_ref[:, 128 * j:128 * (j + 1)] = jnp.where(lo, dq0[j], dq1[j])
        dk_ref[pl.ds(start, ATT_KEYS), :] += ((dk0 + dk1) * scale).T
        dv_ref[pl.ds(start, ATT_KEYS), :] += (dv0 + dv1).T

        @pl.when(i == steps - 1)
        def _():
            dsink_ref[...] = jnp.sum(sink_acc[...], axis=1)

        if n_p:
            @pl.when(i == steps - 1)
            def _():
                finish()
                for a in range(n_p):
                    reduced_refs[a][...] = landing_refs[a][...]

    full = lambda w: pl.BlockSpec((seq, w), lambda i: (0, 0))
    vmem = pl.BlockSpec(memory_space=pltpu.VMEM)
    return pl.pallas_call(
        body, name="attn_bwd", grid=(steps,),
        in_specs=[pl.BlockSpec((8, ATT_TQ, 128), lambda i: (0, i, 0)), full(128), full(128),
                  pl.BlockSpec((N_Q_HEADS, 1, 128), lambda i: (0, 0, 0)),
                  _attn_bias_spec(steps), pl.BlockSpec((ATT_TQ, 512), lambda i: (i, 0))] + [vmem] * n_p,
        out_specs=[pl.BlockSpec((ATT_TQ, 512), lambda i: (i, 0)), full(128), full(128),
                   pl.BlockSpec((N_Q_HEADS, 128), lambda i: (0, 0))] + [vmem] * n_p,
        out_shape=[jax.ShapeDtypeStruct((seq, 512), F32), jax.ShapeDtypeStruct((seq, 128), F32),
                   jax.ShapeDtypeStruct((seq, 128), F32), jax.ShapeDtypeStruct((N_Q_HEADS, 128), F32)]
        + [jax.ShapeDtypeStruct(p.shape[1:], F32) for p in pieces],
        scratch_shapes=[pltpu.VMEM((N_Q_HEADS, ATT_TQ, 128), F32)] + [pltpu.VMEM(p.shape[1:], F32) for p in pieces]
        + (_reduce_scratch([p.shape for p in pieces], [True] * n_p) if n_p else []),
        compiler_params=_cparams(("arbitrary",)),
    )(q_stack, k, v, sink128, bias, d_o, *pieces)


def _permute_rows(dst_ref, src_ref, sub_len):
    for k in range(N_SLAB):
        for j in range(sub_len):
            dst_ref[k, 8 * j:8 * (j + 1), :] = src_ref.at[k][pl.ds(j, SUBSEG, stride=sub_len), :]


def _unpermute_rows(dst_ref, src_ref, sub_len):
    for k in range(N_SLAB):
        for s in range(SUBSEG):
            dst_ref[k, s * sub_len:(s + 1) * sub_len, :] = src_ref.at[k][pl.ds(s, sub_len, stride=SUBSEG), :]


def _scan_chunk(br_ref, bi_ref, lr_row, li_row, init, cols, *, sub_len, reverse, store):
    lr = jnp.broadcast_to(lr_row[:, cols], (SUBSEG, SCAN_LANES))
    li = jnp.broadcast_to(li_row[:, cols], (SUBSEG, SCAN_LANES))
    if init is None:
        sr = si = jnp.zeros((SUBSEG, SCAN_LANES), F32)
    else:
        sr, si = init
    for jj in range(sub_len):
        rows = slice(SUBSEG * ((sub_len - 1 - jj) if reverse else jj), SUBSEG * (((sub_len - 1 - jj) if reverse else jj) + 1))
        sr, si = lr * sr - li * si + br_ref[rows, cols], lr * si + li * sr + bi_ref[rows, cols]
        if store:
            br_ref[rows, cols] = sr
            bi_ref[rows, cols] = si
    return sr, si


def _resolve_chunk(z, carry_refs, start_refs, pr_row, pi_row, cols, *, reverse):
    cr, ci = carry_refs[0][0:1, cols], carry_refs[1][0:1, cols]
    pr, pi = pr_row[:, cols], pi_row[:, cols]
    for s in (range(SUBSEG - 1, -1, -1) if reverse else range(SUBSEG)):
        start_refs[0][s:s + 1, cols] = cr
        start_refs[1][s:s + 1, cols] = ci
        cr, ci = pr * cr - pi * ci + z[0][s:s + 1, :], pr * ci + pi * cr + z[1][s:s + 1, :]
    carry_refs[0][0:1, cols] = cr
    carry_refs[1][0:1, cols] = ci


def _param_specs(direction):
    row = lambda q: pl.BlockSpec((None, None, 1, STATE_W), lambda i: (q, direction, 0, 0))
    wide = lambda q: pl.BlockSpec((None, None, N_SLAB, SLAB_IN, SLAB_ST), lambda i: (q, direction, 0, 0, 0))
    tall = lambda q: pl.BlockSpec((None, None, N_SLAB, SLAB_ST, SLAB_IN), lambda i: (q, direction, 0, 0, 0))
    return [row(q) for q in range(4)], [wide(0), wide(1)], [tall(0), tall(1)]


def _ssm_fwd(u, lam, bb, cb, *, direction, tb, name):
    reverse = direction == 1
    seq = u.shape[1]
    nblk = seq // tb
    sub_len = tb // SUBSEG

    def body(u_ref, lr_ref, li_ref, pr_ref, pi_ref, bbr_ref, bbi_ref, cbr_ref, cbi_ref,
             y_ref, sr_ref, si_ref, xr, xi, up, yp, car, cai):
        @pl.when(pl.program_id(0) == 0)
        def _():
            car[...] = jnp.zeros_like(car)
            cai[...] = jnp.zeros_like(cai)

        _permute_rows(up, u_ref, sub_len)
        lr, li, pr, pi = lr_ref[...], li_ref[...], pr_ref[...], pi_ref[...]
        chunk = lambda k: slice(k * SLAB_ST, (k + 1) * SLAB_ST)

        def drive(k):
            ub = up[k].astype(BF16)
            xr[:, chunk(k)] = _dot(ub, bbr_ref[k])
            xi[:, chunk(k)] = _dot(ub, bbi_ref[k])

        def scan(k):
            z = _scan_chunk(xr, xi, lr, li, None, chunk(k), sub_len=sub_len, reverse=reverse, store=False)
            _resolve_chunk(z, (car, cai), (sr_ref, si_ref), pr, pi, chunk(k), reverse=reverse)
            _scan_chunk(xr, xi, lr, li, (sr_ref[:, chunk(k)], si_ref[:, chunk(k)]), chunk(k),
                        sub_len=sub_len, reverse=reverse, store=True)

        def read_out(k):
            yp[k] = _dot(xr[:, chunk(k)].astype(BF16), cbr_ref[k]) - _dot(xi[:, chunk(k)].astype(BF16), cbi_ref[k])

        drive(0)
        for k in range(N_SLAB):
            if k + 1 < N_SLAB:
                drive(k + 1)
            scan(k)
            if k > 0:
                read_out(k - 1)
        read_out(N_SLAB - 1)
        _unpermute_rows(y_ref, yp, sub_len)

    blk = (lambda i: nblk - 1 - i) if reverse else (lambda i: i)
    rows, wide, tall = _param_specs(direction)
    tok = pl.BlockSpec((N_SLAB, tb, SLAB_IN), lambda i: (0, blk(i), 0))
    start_spec = pl.BlockSpec((None, SUBSEG, STATE_W), lambda i: (blk(i), 0, 0))
    return pl.pallas_call(
        body, name=name, grid=(nblk,),
        in_specs=[tok] + rows + wide + tall,
        out_specs=[tok, start_spec, start_spec],
        out_shape=[jax.ShapeDtypeStruct((N_SLAB, seq, SLAB_IN), F32), jax.ShapeDtypeStruct((nblk, SUBSEG, STATE_W), F32),
                   jax.ShapeDtypeStruct((nblk, SUBSEG, STATE_W), F32)],
        scratch_shapes=[pltpu.VMEM((tb, STATE_W), F32), pltpu.VMEM((tb, STATE_W), F32),
                        pltpu.VMEM((N_SLAB, tb, SLAB_IN), F32), pltpu.VMEM((N_SLAB, tb, SLAB_IN), F32),
                        pltpu.VMEM((SUBSEG, STATE_W), F32), pltpu.VMEM((SUBSEG, STATE_W), F32)],
        compiler_params=_cparams(("arbitrary",)),
    )(u, lam, lam, lam, lam, bb, bb, cb, cb)


def _ssm_bwd(u, dy, starts, lam, bb, bbt, cb_t, *, direction, tb, name):
    reverse = direction == 1
    seq = u.shape[1]
    nblk = seq // tb
    sub_len = tb // SUBSEG

    def body(u_ref, dy_ref, sr_ref, si_ref, lr_ref, li_ref, pr_ref, pi_ref, bbr_ref, bbi_ref, btr_ref, bti_ref,
             ctr_ref, cti_ref, du_ref, gb_ref, gc_ref, dl_ref,
             xr, xi, gr, gi, up, dyp, dup, gsr, gsi, car, cai):
        gbr_ref, gbi_ref = gb_ref.at[0], gb_ref.at[1]
        gcr_ref, gci_ref = gc_ref.at[0], gc_ref.at[1]
        dlr_ref, dli_ref = dl_ref.at[0], dl_ref.at[1]

        @pl.when(pl.program_id(0) == 0)
        def _():
            for ref in (car, cai, gbr_ref, gbi_ref, gcr_ref, gci_ref, dlr_ref, dli_ref):
                ref[...] = jnp.zeros_like(ref)

        _permute_rows(up, u_ref, sub_len)
        _permute_rows(dyp, dy_ref, sub_len)
        lr, li, pr, pi = lr_ref[...], li_ref[...], pr_ref[...], pi_ref[...]
        nli, npi = -li, -pi
        chunk = lambda k: slice(k * SLAB_ST, (k + 1) * SLAB_ST)

        def drive(k):
            ub = up[k].astype(BF16)
            xr[:, chunk(k)] = _dot(ub, bbr_ref[k])
            xi[:, chunk(k)] = _dot(ub, bbi_ref[k])
            dyb = dyp[k].astype(BF16)
            gr[:, chunk(k)] = _dot(dyb, ctr_ref[k])
            gi[:, chunk(k)] = -_dot(dyb, cti_ref[k])

        def scan_x(k):
            _scan_chunk(xr, xi, lr, li, (sr_ref[:, chunk(k)], si_ref[:, chunk(k)]), chunk(k),
                        sub_len=sub_len, reverse=reverse, store=True)

        def grad_c(k):
            dyb = dyp[k].astype(BF16)
            gcr_ref[k] += _dot_tn(dyb, xr[:, chunk(k)].astype(BF16))
            gci_ref[k] -= _dot_tn(dyb, xi[:, chunk(k)].astype(BF16))

        def scan_g(k):
            z = _scan_chunk(gr, gi, lr, nli, None, chunk(k), sub_len=sub_len, reverse=not reverse, store=False)
            _resolve_chunk(z, (car, cai), (gsr, gsi), pr, npi, chunk(k), reverse=not reverse)
            _scan_chunk(gr, gi, lr, nli, (gsr[:, chunk(k)], gsi[:, chunk(k)]), chunk(k),
                        sub_len=sub_len, reverse=not reverse, store=True)

        def grad_b_du(k):
            ub = up[k].astype(BF16)
            grb, gib = gr[:, chunk(k)].astype(BF16), gi[:, chunk(k)].astype(BF16)
            gbr_ref[k] += _dot_tn(ub, grb)
            gbi_ref[k] += _dot_tn(ub, gib)
            dup[k] = _dot(grb, btr_ref[k]) + _dot(gib, bti_ref[k])

        def grad_lambda(k):
            cols = chunk(k)
            acc_r, acc_i = dlr_ref[:, cols], dli_ref[:, cols]
            for jj in range(sub_len):
                prev = jj + 1 if reverse else jj - 1
                if 0 <= prev < sub_len:
                    x_r, x_i = xr[SUBSEG * prev:SUBSEG * (prev + 1), cols], xi[SUBSEG * prev:SUBSEG * (prev + 1), cols]
                else:
                    x_r, x_i = sr_ref[:, cols], si_ref[:, cols]
                g_r, g_i = gr[SUBSEG * jj:SUBSEG * (jj + 1), cols], gi[SUBSEG * jj:SUBSEG * (jj + 1), cols]
                acc_r = acc_r + (g_r * x_r + g_i * x_i)
                acc_i = acc_i + (g_i * x_r - g_r * x_i)
            dlr_ref[:, cols] = acc_r
            dli_ref[:, cols] = acc_i

        drive(0)
        for k in range(N_SLAB):
            if k + 1 < N_SLAB:
                drive(k + 1)
            scan_x(k)
            grad_c(k)
            scan_g(k)
            grad_b_du(k)
            grad_lambda(k)
        _unpermute_rows(du_ref, dup, sub_len)

    blk = (lambda i: i) if reverse else (lambda i: nblk - 1 - i)
    rows, wide, tall = _param_specs(direction)
    tok = pl.BlockSpec((N_SLAB, tb, SLAB_IN), lambda i: (0, blk(i), 0))
    start_spec = pl.BlockSpec((None, SUBSEG, STATE_W), lambda i: (blk(i), 0, 0))
    gb_shape, dl_shape = (2, N_SLAB, SLAB_IN, SLAB_ST), (2, SUBSEG, STATE_W)
    whole = lambda shape: pl.BlockSpec(shape, lambda i: (0,) * len(shape))
    big = lambda: pltpu.VMEM((tb, STATE_W), F32)
    slabs = lambda: pltpu.VMEM((N_SLAB, tb, SLAB_IN), F32)
    tile = lambda: pltpu.VMEM((SUBSEG, STATE_W), F32)
    return pl.pallas_call(
        body, name=name, grid=(nblk,),
        in_specs=[tok, tok, start_spec, start_spec] + rows + wide + tall + wide,
        out_specs=[tok, whole(gb_shape), whole(gb_shape), whole(dl_shape)],
        out_shape=[jax.ShapeDtypeStruct((N_SLAB, seq, SLAB_IN), F32), jax.ShapeDtypeStruct(gb_shape, F32),
                   jax.ShapeDtypeStruct(gb_shape, F32), jax.ShapeDtypeStruct(dl_shape, F32)],
        scratch_shapes=[big(), big(), big(), big(), slabs(), slabs(), slabs(), tile(), tile(), tile(), tile()],
        compiler_params=_cparams(("arbitrary",)),
    )(u, dy, *starts, lam, lam, lam, lam, bb, bb, bbt, bbt, cb_t, cb_t)


GELU_C = math.sqrt(2.0 / math.pi)
GELU_K = 0.044715


def _mid(o, za, u, y_f, y_b, zs, x, target, ssm_d, w_glu, b_glu, g_attn, g_ssm, w_out, ln_g, ln_b, tb):
    seq = x.shape[0]

    def body(o_ref, za_ref, u_ref, yf_ref, yb_ref, zs_ref, x_ref, t_ref, d_ref, wg_ref, bg_ref, ga_ref, gs_ref,
             wo_ref, lg_ref, lb_ref,
             loss_ref, do_ref, dza_ref, dyl_ref, dzs_ref, dpre_ref, gwo_ref, gwg_ref, vec_ref, wop):
        @pl.when(pl.program_id(0) == 0)
        def _():
            for ref in (loss_ref, gwo_ref, gwg_ref, vec_ref):
                ref[...] = jnp.zeros_like(ref)
            for nat, par in _pair_blocks(0):
                wop[par, :] = wo_ref[nat, :]
            wop[D_ATTN:, :] = wo_ref[D_ATTN:, :]

        def rows_of(rs):
            o, za = o_ref[rs, :], za_ref[rs, :]
            sig_a = _sigmoid(za)
            silu_a = za * sig_a
            ya = o * silu_a
            r_a = lax.rsqrt(jnp.mean(ya * ya, axis=1, keepdims=True) + NORM_EPS)
            n_a = ya * r_a
            g_a = ga_ref[...]
            unslab = lambda ref: jnp.concatenate([ref[k, rs, :] for k in range(N_SLAB)], axis=1)
            u_blk, zs = unslab(u_ref), zs_ref[rs, :]
            d_row = d_ref[...]
            ylin = d_row * u_blk + unslab(yf_ref) + unslab(yb_ref)
            inner = GELU_C * (ylin + GELU_K * ylin * ylin * ylin)
            th = jnp.tanh(inner)
            gl = 0.5 * ylin * (1.0 + th)
            glb = gl.astype(BF16)
            gate = _dot(glb, wg_ref[...])
            sg = _sigmoid(gate + bg_ref[...])
            y2 = gl * sg
            sig_s = _sigmoid(zs)
            silu_s = zs * sig_s
            ys = y2 * silu_s
            r_s = lax.rsqrt(jnp.mean(ys * ys, axis=1, keepdims=True) + NORM_EPS)
            n_s = ys * r_s
            g_s = gs_ref[...]
            mixed = jnp.concatenate([n_a * g_a, n_s * g_s], axis=1).astype(BF16)
            out = _dot(mixed, wop[...])
            pre = ALPHA * x_ref[rs, :] + out
            mu = jnp.mean(pre, axis=1, keepdims=True)
            cen = pre - mu
            rstd = lax.rsqrt(jnp.mean(cen * cen, axis=1, keepdims=True) + NORM_EPS)
            hhat = cen * rstd
            ln_g = lg_ref[...]
            err = hhat * ln_g + lb_ref[...] - t_ref[rs, :]
            loss_ref[...] += 0.5 * jnp.sum(jnp.mean(err * err, axis=1, keepdims=True))

            dh = err * (1.0 / D_MODEL)
            vec_ref[0:1, :] += jnp.sum(dh * hhat, axis=0, keepdims=True)
            vec_ref[1:2, :] += jnp.sum(dh, axis=0, keepdims=True)
            dhh = dh * ln_g
            dpre = rstd * (dhh - jnp.mean(dhh, axis=1, keepdims=True)
                           - hhat * jnp.mean(dhh * hhat, axis=1, keepdims=True))
            dpre_ref[rs, :] = dpre
            dpb = dpre.astype(BF16)
            for j in range(4):
                g_pair = _dot_tn(mixed[:, 128 * j:128 * (j + 1)], dpb)
                for g in range(2):
                    nat = HEAD_DIM * (4 * g + j)
                    gwo_ref[nat:nat + HEAD_DIM, :] += g_pair[HEAD_DIM * g:HEAD_DIM * (g + 1), :]
            gwo_ref[D_ATTN:, :] += _dot_tn(mixed[:, D_ATTN:], dpb)
            dmix = _dot_nt(dpb, wop[...])
            dna = dmix[:, :D_ATTN]
            vec_ref[2:3, 0:D_ATTN] += jnp.sum(dna * n_a, axis=0, keepdims=True)
            dna = dna * g_a
            dya = r_a * (dna - n_a * jnp.mean(dna * n_a, axis=1, keepdims=True))
            do_ref[rs, :] = dya * silu_a
            dza_ref[rs, :] = dya * o * (sig_a * (1.0 + za * (1.0 - sig_a)))
            dns = dmix[:, D_ATTN:]
            vec_ref[2:3, D_ATTN:] += jnp.sum(dns * n_s, axis=0, keepdims=True)
            dns = dns * g_s
            dys = r_s * (dns - n_s * jnp.mean(dns * n_s, axis=1, keepdims=True))
            dzs_ref[rs, :] = dys * y2 * (sig_s * (1.0 + zs * (1.0 - sig_s)))
            dy2 = dys * silu_s
            da = dy2 * gl * sg * (1.0 - sg)
            vec_ref[3:4, D_SSM:] += jnp.sum(da, axis=0, keepdims=True)
            dab = da.astype(BF16)
            gwg_ref[...] += _dot_tn(glb, dab)
            dgl_mm = _dot_nt(dab, wg_ref[...])
            dgl = dy2 * sg + dgl_mm
            dylin = dgl * (0.5 * (1.0 + th)
                           + 0.5 * ylin * (1.0 - th * th) * GELU_C * (1.0 + 3.0 * GELU_K * ylin * ylin))
            for k in range(N_SLAB):
                dyl_ref[k, rs, :] = dylin[:, k * SLAB_IN:(k + 1) * SLAB_IN]
            vec_ref[3:4, 0:D_SSM] += jnp.sum(dylin * u_blk, axis=0, keepdims=True)

        rows_of(slice(0, tb))

    tok = lambda w: pl.BlockSpec((tb, w), lambda i: (i, 0))
    slab = pl.BlockSpec((N_SLAB, tb, SLAB_IN), lambda i: (0, i, 0))
    const = lambda r, c: pl.BlockSpec((r, c), lambda i: (0, 0), pipeline_mode=pl.Buffered(1))
    tok_shape = jax.ShapeDtypeStruct((seq, 512), F32)
    return pl.pallas_call(
        body, name="mid", grid=(seq // tb,),
        in_specs=[tok(512), tok(512), slab, slab, slab, tok(512), tok(1024), tok(1024),
                  const(1, 512), const(512, 512), const(1, 512), const(1, 512), const(1, 512),
                  const(1024, 1024), const(1, 1024), const(1, 1024)],
        out_specs=[const(8, 128), tok(512), tok(512), slab, tok(512), tok(1024),
                   const(1024, 1024), const(512, 512), const(8, 1024)],
        out_shape=[jax.ShapeDtypeStruct((8, 128), F32), tok_shape, tok_shape,
                   jax.ShapeDtypeStruct((N_SLAB, seq, SLAB_IN), F32), tok_shape,
                   jax.ShapeDtypeStruct((seq, 1024), F32), jax.ShapeDtypeStruct((1024, 1024), F32),
                   jax.ShapeDtypeStruct((512, 512), F32), jax.ShapeDtypeStruct((8, 1024), F32)],
        scratch_shapes=[pltpu.VMEM((D_MODEL, D_MODEL), BF16)],
        compiler_params=pltpu.CompilerParams(dimension_semantics=("arbitrary",), vmem_limit_bytes=MID_VMEM),
    )(o, za, u, y_f, y_b, zs, x, target, ssm_d, w_glu, b_glu, g_attn, g_ssm, w_out, ln_g, ln_b)


def _ride_shapes(pieces, narrow, gather_last):
    outs = [p.shape if (gather_last and a == len(pieces) - 1) else p.shape[1:] for a, p in enumerate(pieces)]
    return outs, [pltpu.VMEM(s, F32) for s in outs] + _reduce_scratch([p.shape for p in pieces], narrow)


def _ride_phases(piece_refs, out_refs, scratch_refs, narrow, gather_last):
    n = len(piece_refs)
    landing, rest = scratch_refs[:n], scratch_refs[n:]
    begin, exchange, combine, finish = _reduce_phases(piece_refs, landing, rest[:n], rest[n:2 * n], rest[2 * n:3 * n],
                                                      *rest[3 * n:], narrow, gather_last)

    def end():
        finish()
        for a in range(n):
            out_refs[a][...] = landing[a][...]

    return begin, exchange, combine, end


def _dproj_block(dq_ref, dk_ref, dv_ref, dza_ref, duf_ref, dub_ref, dyl_ref, dzs_ref, d_ref, hi_ref, lo_ref, tb):
    cos, sin = _rope_block(hi_ref, lo_ref, pl.program_id(0) * (tb // ROPE_GROUP), tb // ROPE_GROUP)
    lo = lax.broadcasted_iota(jnp.int32, (tb, 128), 1) < HEAD_DIM

    def unrope(t):
        return t * cos + _rotate_half_unsigned(t * sin)

    def natural(pairs):
        swapped = [pltpu.roll(t, HEAD_DIM, 1) for t in pairs]
        return [jnp.where(lo, pairs[0], swapped[1]), jnp.where(lo, pairs[2], swapped[3]),
                jnp.where(lo, swapped[0], pairs[1]), jnp.where(lo, swapped[2], pairs[3])]

    dq_rot, dza = dq_ref[...], dza_ref[...]
    pieces = natural([unrope(dq_rot[:, 128 * j:128 * (j + 1)]) for j in range(4)])
    d_row = d_ref[...]
    pieces += [unrope(dk_ref[...]), dv_ref[...]] + natural([dza[:, 128 * j:128 * (j + 1)] for j in range(4)])
    pieces += [duf_ref[k] + dub_ref[k] + d_row[:, k * SLAB_IN:(k + 1) * SLAB_IN] * dyl_ref[k] for k in range(N_SLAB)]
    pieces += [dzs_ref[...]]
    return jnp.concatenate(pieces, axis=1).astype(BF16)


def _dproj_specs(tb, rope_hi, rope_lo):
    tok = lambda w: pl.BlockSpec((tb, w), lambda i: (i, 0))
    slab = pl.BlockSpec((N_SLAB, tb, SLAB_IN), lambda i: (0, i, 0))
    table = lambda t: pl.BlockSpec(t.shape, lambda i: (0, 0, 0))
    return [tok(512), tok(128), tok(128), tok(512), slab, slab, slab, tok(512), pl.BlockSpec((1, 512), lambda i: (0, 0)),
            table(rope_hi), table(rope_lo)]


N_DPROJ = 11
GW_ROWS = 768


def _proj_bwd_w(x, dproj_args, rope_hi, rope_lo, pieces, tb):
    seq = x.shape[0]
    steps = seq // tb
    n_p = len(pieces)
    narrow = [False] * n_p

    def body(*refs):
        x_ref, grads = refs[0], refs[1:1 + N_DPROJ]
        piece_refs = refs[1 + N_DPROJ:1 + N_DPROJ + n_p]
        gw_ref = refs[1 + N_DPROJ + n_p]
        out_refs = refs[2 + N_DPROJ + n_p:2 + N_DPROJ + 2 * n_p]
        step = pl.program_id(0)
        if n_p:
            begin, exchange, combine, end = _ride_phases(piece_refs, out_refs, refs[2 + N_DPROJ + 2 * n_p:], narrow, True)
            pl.when(step == 0)(begin)
            pl.when(step == min(1, steps - 1))(exchange)
            pl.when(step == steps // 2)(combine)

        @pl.when(step == 0)
        def _():
            gw_ref[...] = jnp.zeros_like(gw_ref)

        dproj = _dproj_block(*grads, tb)
        xb = x_ref[...].astype(BF16)
        for r0 in range(0, D_IN_PROJ, GW_ROWS):
            gw_ref[r0:r0 + GW_ROWS, :] += _dot_tn(dproj[:, r0:r0 + GW_ROWS], xb)
        if n_p:
            pl.when(step == steps - 1)(end)

    vmem = pl.BlockSpec(memory_space=pltpu.VMEM)
    whole = pl.BlockSpec((D_IN_PROJ, D_MODEL), lambda i: (0, 0), pipeline_mode=pl.Buffered(1))
    ride_outs, ride_scratch = _ride_shapes(pieces, narrow, True) if n_p else ([], [])
    return pl.pallas_call(
        body, name="proj_bwd_w", grid=(steps,),
        in_specs=[pl.BlockSpec((tb, D_MODEL), lambda i: (i, 0))] + _dproj_specs(tb, rope_hi, rope_lo) + [vmem] * n_p,
        out_specs=[whole] + [vmem] * n_p,
        out_shape=[jax.ShapeDtypeStruct((D_IN_PROJ, D_MODEL), F32)] + [jax.ShapeDtypeStruct(s, F32) for s in ride_outs],
        scratch_shapes=ride_scratch,
        compiler_params=_cparams(("arbitrary",)),
    )(x, *dproj_args, rope_hi, rope_lo, *pieces)


def _proj_bwd_x(dproj_args, rope_hi, rope_lo, dpre, wt, pieces, tb):
    seq = dpre.shape[0]
    steps = seq // tb
    n_p = len(pieces)
    narrow = [True] * n_p

    def body(*refs):
        grads = refs[:N_DPROJ]
        dpre_ref, wt_ref = refs[N_DPROJ:N_DPROJ + 2]
        piece_refs = refs[N_DPROJ + 2:N_DPROJ + 2 + n_p]
        gx_ref = refs[N_DPROJ + 2 + n_p]
        out_refs = refs[N_DPROJ + 3 + n_p:N_DPROJ + 3 + 2 * n_p]
        step = pl.program_id(0)
        if n_p:
            begin, exchange, combine, end = _ride_phases(piece_refs, out_refs, refs[N_DPROJ + 3 + 2 * n_p:], narrow, False)
            pl.when(step == 0)(begin)
            pl.when(step == min(1, steps - 1))(exchange)
            pl.when(step == steps - 1)(combine)

        dproj = _dproj_block(*grads, tb)
        gx_ref[...] = ALPHA * dpre_ref[...] + _dot(dproj, wt_ref[...])
        if n_p:
            pl.when(step == steps - 1)(end)

    vmem = pl.BlockSpec(memory_space=pltpu.VMEM)
    whole = pl.BlockSpec((D_IN_PROJ, D_MODEL), lambda i: (0, 0), pipeline_mode=pl.Buffered(1))
    ride_outs, ride_scratch = _ride_shapes(pieces, narrow, False) if n_p else ([], [])
    return pl.pallas_call(
        body, name="proj_bwd_x", grid=(steps,),
        in_specs=_dproj_specs(tb, rope_hi, rope_lo) + [pl.BlockSpec((tb, D_MODEL), lambda i: (i, 0)), whole] + [vmem] * n_p,
        out_specs=[pl.BlockSpec((tb, D_MODEL), lambda i: (i, 0))] + [vmem] * n_p,
        out_shape=[jax.ShapeDtypeStruct((seq, D_MODEL), F32)] + [jax.ShapeDtypeStruct(s, F32) for s in ride_outs],
        scratch_shapes=ride_scratch,
        compiler_params=pltpu.CompilerParams(dimension_semantics=("arbitrary",), vmem_limit_bytes=PROJ_BWD_X_VMEM),
    )(*dproj_args, rope_hi, rope_lo, dpre, wt, *pieces)


def _adamw(w, g, m, v, name):
    rows, cols = w.shape
    tb = rows
    while tb * cols * 4 > ADAMW_BLOCK_BYTES and tb % 16 == 0:
        tb //= 2

    def body(w_ref, g_ref, m_ref, v_ref, d_ref, nm_ref, nv_ref):
        _adamw_update(w_ref, g_ref, m_ref, v_ref, d_ref, nm_ref, nv_ref)

    spec = pl.BlockSpec((tb, cols), lambda i: (i, 0))
    return pl.pallas_call(
        body, name=name, grid=(rows // tb,), in_specs=[spec] * 4, out_specs=[spec] * 3,
        out_shape=[jax.ShapeDtypeStruct((rows, cols), F32)] * 3,
        compiler_params=_cparams(("arbitrary",)),
    )(w, g, m, v)


def _adamw_update(w_ref, g_ref, m_ref, v_ref, d_ref, nm_ref, nv_ref):
    g_blk = g_ref[...]
    m_new = ADAM_B1 * m_ref[...] + (1.0 - ADAM_B1) * g_blk
    v_new = ADAM_B2 * v_ref[...] + (1.0 - ADAM_B2) * (g_blk * g_blk)
    m_hat = m_new / (1.0 - ADAM_B1 ** ADAM_STEP)
    v_hat = v_new / (1.0 - ADAM_B2 ** ADAM_STEP)
    d_ref[...] = -ADAM_LR * (m_hat / (jnp.sqrt(v_hat) + ADAM_EPS) + ADAM_WD * w_ref[...])
    nm_ref[...] = m_new
    nv_ref[...] = v_new


def _adamw_many(groups, name):
    n = len(groups)

    def body(*refs):
        for p in range(n):
            w_ref, g_ref, m_ref, v_ref = refs[4 * p:4 * p + 4]
            gn_ref, d_ref, nm_ref, nv_ref = refs[4 * n + 4 * p:4 * n + 4 * p + 4]
            gn_ref[...] = g_ref[...].reshape(w_ref.shape)
            _adamw_update(w_ref, gn_ref, m_ref, v_ref, d_ref, nm_ref, nv_ref)

    return pl.pallas_call(
        body, name=name,
        out_shape=[jax.ShapeDtypeStruct(grp[0].shape, F32) for grp in groups for _ in range(4)],
    )(*[a for grp in groups for a in grp])


_WEIGHTS = ["w_in", "attn_sink", "ssm_a_re", "ssm_a_im", "ssm_log_dt", "ssm_b_re", "ssm_b_im", "ssm_c_re", "ssm_c_im",
            "ssm_d", "w_glu", "b_glu", "norm_attn_g", "norm_ssm_g", "w_out", "ln_g", "ln_b"]
N_DG = N_DIR * N_GROUPS
BIG_ROWS = N_DG * SSM_CH * SSM_STATE // 128
TINY_ROWS = 64


def _pack_small_grads(g_bc, g_vec, g_ar, g_ai, g_dt, g_sink, loss):
    big = jnp.stack([t.reshape(BIG_ROWS, 128) for t in g_bc])
    row = lambda t: jnp.pad(t.reshape(1, -1), ((0, 0), (0, 128 - t.size)))
    tiny = jnp.concatenate([g_vec.reshape(64, 128), g_ar.reshape(32, 128), g_ai.reshape(32, 128), row(g_dt), row(g_sink),
                            row(loss), jnp.zeros((N_CHIPS * TINY_ROWS - 131, 128), F32)], axis=0)
    return jnp.concatenate([big, tiny.reshape(N_CHIPS, TINY_ROWS, 128)], axis=1)


def _unpack_small_grads(packed):
    big = packed[:, :BIG_ROWS].reshape(N_CHIPS, 2 * BIG_ROWS, SSM_STATE)
    tiny = packed[:, BIG_ROWS:].reshape(N_CHIPS * TINY_ROWS, 128)
    g_vec = tiny[0:64].reshape(8, 1024)
    return tiny[130, 0], {
        "ssm_b_re": big[0], "ssm_b_im": big[1], "ssm_c_re": big[2], "ssm_c_im": big[3],
        "ln_g": g_vec[0:1], "ln_b": g_vec[1:2],
        "norm_attn_g": _from_pair_order(g_vec[2:3, :D_ATTN]), "norm_ssm_g": g_vec[2:3, D_ATTN:],
        "ssm_d": g_vec[3:4, :D_SSM], "b_glu": g_vec[3:4, D_SSM:],
        "ssm_a_re": tiny[64:96].reshape(N_DG, SSM_STATE), "ssm_a_im": tiny[96:128].reshape(N_DG, SSM_STATE),
        "ssm_log_dt": tiny[128:129, :N_DG].reshape(N_DIR, N_GROUPS), "attn_sink": tiny[129:130, :N_Q_HEADS],
    }


def _small_unview(name, t, shape):
    if name in ("ssm_b_re", "ssm_b_im"):
        return jnp.swapaxes(t.reshape(N_DIR, N_GROUPS, SSM_CH, SSM_STATE), 2, 3).reshape(shape)
    return t.reshape(shape)


def _channel_major(name, t):
    return jnp.swapaxes(t, 3, 4) if name in ("ssm_b_re", "ssm_b_im") else t


def kernel(x, w_in, attn_sink, ssm_a_re, ssm_a_im, ssm_log_dt, ssm_b_re, ssm_b_im, ssm_c_re, ssm_c_im, ssm_d, w_glu, b_glu, norm_attn_g, norm_ssm_g, w_out, ln_g, ln_b, loss_target, m_w_in, m_attn_sink, m_ssm_a_re, m_ssm_a_im, m_ssm_log_dt, m_ssm_b_re, m_ssm_b_im, m_ssm_c_re, m_ssm_c_im, m_ssm_d, m_w_glu, m_b_glu, m_norm_attn_g, m_norm_ssm_g, m_w_out, m_ln_g, m_ln_b, v_w_in, v_attn_sink, v_ssm_a_re, v_ssm_a_im, v_ssm_log_dt, v_ssm_b_re, v_ssm_b_im, v_ssm_c_re, v_ssm_c_im, v_ssm_d, v_w_glu, v_b_glu, v_norm_attn_g, v_norm_ssm_g, v_w_out, v_ln_g, v_ln_b):
    args = dict(locals())
    weights = {n: args[n] for n in _WEIGHTS}
    mom_m = {n: args["m_" + n] for n in _WEIGHTS}
    mom_v = {n: args["v_" + n] for n in _WEIGHTS}
    xs = x[0]
    target = loss_target[0]

    (wt_g,) = _all_gather_chips([w_in[0].T], BF16, "gather_weights")
    wt_full = wt_g.reshape(D_IN_PROJ, D_MODEL)

    g_x, r_wt, r_w_out, r_w_glu, g_small_all = _local_step(
        xs, target, wt_full, w_glu[0], w_out[0], attn_sink, ssm_a_re, ssm_a_im, ssm_log_dt, ssm_b_re, ssm_b_im,
        ssm_c_re, ssm_c_im, ssm_d, b_glu, norm_attn_g, norm_ssm_g, ln_g, ln_b, sharded=True)
    loss, small_grads = _unpack_small_grads(g_small_all)

    grads, deltas, new_m, new_v = {}, {}, {}, {}
    d_w, m_w, v_w = _adamw(w_in[0].T, r_wt, m_w_in[0].T, v_w_in[0].T, "adamw_w_in")
    grads["w_in"], deltas["w_in"], new_m["w_in"], new_v["w_in"] = r_wt.T[None], d_w.T[None], m_w.T[None], v_w.T[None]
    for n, g in (("w_out", r_w_out), ("w_glu", r_w_glu)):
        d_w, m_w, v_w = _adamw(weights[n][0], g, mom_m[n][0], mom_v[n][0], "adamw_" + n)
        grads[n], deltas[n], new_m[n], new_v[n] = g[None], d_w[None], m_w[None], v_w[None]
    names = sorted(small_grads)
    updates = _adamw_many([(_channel_major(n, weights[n]), small_grads[n], _channel_major(n, mom_m[n]),
                            _channel_major(n, mom_v[n])) for n in names], "adamw_small")
    for i, n in enumerate(names):
        grads[n], deltas[n], new_m[n], new_v[n] = (_channel_major(n, t) for t in updates[4 * i:4 * i + 4])

    return (loss, g_x[None], *[grads[n] for n in _WEIGHTS], *[deltas[n] for n in _WEIGHTS],
            *[new_m[n] for n in _WEIGHTS], *[new_v[n] for n in _WEIGHTS])


def _local_step(xs, target, wt_full, w_glu_in, w_out_in, attn_sink, ssm_a_re, ssm_a_im, ssm_log_dt, ssm_b_re,
                ssm_b_im, ssm_c_re, ssm_c_im, ssm_d, b_glu, norm_attn_g, norm_ssm_g, ln_g, ln_b, sharded):
    seq = xs.shape[0]

    a_r, a_i = ssm_a_re, ssm_a_im
    log_dt = ssm_log_dt.reshape(N_DG, 1)
    b_r, b_i = _channel_major("ssm_b_re", ssm_b_re), _channel_major("ssm_b_im", ssm_b_im)
    c_r, c_i = ssm_c_re, ssm_c_im
    ssm_tb = min(SSM_BLOCK, seq)
    sub_len = ssm_tb // SUBSEG
    lam, bb, bbt, cb, cb_t = _ssm_params_fwd(a_r, a_i, log_dt, b_r, b_i, c_r, c_i, int(math.log2(sub_len)))
    lam = lam.reshape(4, N_DIR, 1, STATE_W)

    rope_hi, rope_lo = _rope_tables(seq)
    projected = _proj(xs, wt_full, rope_hi, rope_lo, [w_glu_in, w_out_in] if sharded else [], min(512, seq))
    q_stack, k_rot, v_bf, z_attn, u, z_ssm = projected[:6]
    if sharded:
        w_glu_full, w_out_full = projected[6].reshape(D_SSM, D_SSM), projected[7].reshape(D_MODEL, D_MODEL)
    else:
        w_glu_full, w_out_full = w_glu_in, w_out_in
    sink128 = jnp.broadcast_to(attn_sink[0][:, None, None], (N_Q_HEADS, 1, 128))
    attn_bias = _attn_bias()
    o = _attn_fwd(q_stack, k_rot, v_bf, sink128, attn_bias)
    ys, starts = [], []
    for d in range(N_DIR):
        y_d, s_r, s_i = _ssm_fwd(u, lam, bb, cb, direction=d, tb=ssm_tb, name=f"ssm_fwd_{d}")
        ys.append(y_d)
        starts.append((s_r, s_i))

    row = lambda t: t.reshape(1, -1)
    g_attn_p = _to_pair_order(norm_attn_g)
    loss_blk, d_o, d_za, d_ylin, d_zs, d_pre, g_w_out, g_w_glu, g_vec = _mid(
        o, z_attn, u, ys[0], ys[1], z_ssm, xs, target, row(ssm_d), w_glu_full, row(b_glu),
        g_attn_p, row(norm_ssm_g), w_out_full, row(ln_g), row(ln_b), min(MID_BLOCK, seq))

    pieces = [g_w_glu.reshape(N_CHIPS, -1, D_SSM), g_w_out.reshape(N_CHIPS, -1, D_MODEL)] if sharded else []
    attn_grads = _attn_bwd(q_stack, k_rot, v_bf, sink128, attn_bias, d_o, pieces)
    dq, dk, dv, g_sink = attn_grads[:4]
    if sharded:
        g_w_glu, g_w_out = attn_grads[4:]
    dus, g_bb, g_cb, g_lam = [], [], [], []
    for d in range(N_DIR):
        du_d, gb_d, gc_d, dl_d = _ssm_bwd(u, d_ylin, starts[d], lam, bb, bbt, cb_t, direction=d, tb=ssm_tb,
                                          name=f"ssm_bwd_{d}")
        dus.append(du_d)
        g_bb.append(gb_d)
        g_cb.append(gc_d)
        g_lam.append(dl_d)
    g_ar, g_ai, g_dt, g_br, g_bi, g_cr, g_ci = _ssm_params_bwd(a_r, a_i, log_dt, b_r, b_i, g_bb, g_cb, g_lam)

    g_small = _pack_small_grads([g_br, g_bi, g_cr, g_ci], g_vec, g_ar, g_ai, g_dt, g_sink[:, 0], loss_blk[0, 0])
    dproj_args = (dq, dk, dv, d_za, dus[0], dus[1], d_ylin, d_zs, row(ssm_d))
    w_grads = _proj_bwd_w(xs, dproj_args, rope_hi, rope_lo, [g_small] if sharded else [], min(512, seq))
    g_wt = w_grads[0]
    if sharded:
        g_small = w_grads[1]
    x_grads = _proj_bwd_x(dproj_args, rope_hi, rope_lo, d_pre, wt_full,
                          [g_wt.reshape(N_CHIPS, -1, D_MODEL)] if sharded else [], min(512, seq))
    g_x = x_grads[0]
    if sharded:
        g_wt = x_grads[1]
    return g_x, g_wt, g_w_out, g_w_glu, g_small
```
